```python
import math
import jax, jax.numpy as jnp
from jax import lax
import numpy as np

D_MODEL = 1024
BATCH = 8
SEQ = 4096
DEPTH = 1

N_MEM = 256
RMS_EPS = 1e-6
NEG_INF = -1e30

HG_EXPAND = 128
HG_HEADS = D_MODEL // HG_EXPAND
HG_DK = HG_EXPAND
HG_DV = D_MODEL // HG_HEADS
HG_WIDTH = HG_HEADS * HG_DV
HG_CHUNK = 64
HG_SCALE = HG_DK ** -0.5

DA_CONFIGS = ((128, 1), (512, 4), (2048, 16))
DA_HEADS_PER_GROUP = 4
DA_HEADS = DA_HEADS_PER_GROUP * len(DA_CONFIGS)
DA_HEAD_DIM = D_MODEL // 8
DA_QKV_WIDTH = DA_HEADS * DA_HEAD_DIM
DA_WIDTH = DA_HEADS_PER_GROUP * DA_HEAD_DIM
DA_SCALE = DA_HEAD_DIM ** -0.5

MEM_HEADS = 4
MEM_HEAD_DIM = D_MODEL // 8
MEM_WIDTH = MEM_HEADS * MEM_HEAD_DIM
MEM_SCALE = MEM_HEAD_DIM ** -0.5

D_FF = ((8 * D_MODEL // 3 + 255) // 256) * 256

IN_SPLITS = (HG_WIDTH,) * 5 + (DA_QKV_WIDTH,) * 3 + (MEM_WIDTH,) + (D_MODEL,) * 3
IN_COLS = sum(IN_SPLITS)
IN_SPLIT_POINTS = tuple(int(p) for p in np.cumsum(IN_SPLITS)[:-1])

kernel_name = "hybrid_hgrn2_dilated_memory_block"


def rmsnorm(x, gain):
    xf = x.astype(jnp.float32)
    y = xf * lax.rsqrt(jnp.mean(xf * xf, axis=-1, keepdims=True) + RMS_EPS)
    return (y * gain.astype(jnp.float32)).astype(x.dtype)


def alibi_slopes(n):
    return (2.0 ** (-8.0 * np.arange(1, n + 1) / n)).astype(np.float32)


def gla_chunkwise(q, k, v, log_f):
    B, H, L, dk = q.shape
    dv = v.shape[-1]
    C = HG_CHUNK
    n = L // C
    q, k, log_f = [t.reshape(B, H, n, C, dk) for t in (q, k, log_f)]
    v = v.reshape(B, H, n, C, dv)
    b = jnp.cumsum(log_f.astype(jnp.float32), axis=3)
    b_last = b[:, :, :, -1:, :]
    q_in = q * jnp.exp(b)
    k_in = k * jnp.exp(-b)
    k_st = k * jnp.exp(b_last - b)
    causal = jnp.tril(jnp.ones((C, C), dtype=bool))
    a = jnp.where(causal, jnp.einsum('bhnti,bhnsi->bhnts', q_in, k_in), 0.0)
    o_intra = jnp.einsum('bhnts,bhnsj->bhntj', a, v)
    chunk_state = jnp.einsum('bhnsi,bhnsj->bhnij', k_st, v)
    decay = jnp.exp(b_last[:, :, :, 0, :])

    def step(S, inp):
        d, s = inp
        return d[..., None] * S + s, S

    S0 = jnp.zeros((B, H, dk, dv), jnp.float32)
    _, S_in = lax.scan(step, S0, (jnp.moveaxis(decay, 2, 0), jnp.moveaxis(chunk_state, 2, 0)))
    S_in = jnp.moveaxis(S_in, 0, 2)
    o_inter = jnp.einsum('bhnti,bhnij->bhntj', q_in, S_in)
    return (o_intra + o_inter).reshape(B, H, L, dv)


def hgrn2_mixer(q, f_fw, f_bw, inp, gate, lb_fw, lb_bw, norm_gain):
    B, L, _ = q.shape

    def to_heads(t):
        return t.reshape(B, L, HG_HEADS, -1).transpose(0, 2, 1, 3)

    qh = to_heads(jax.nn.silu(q)) * HG_SCALE
    vh = to_heads(inp)

    def direction(f_logit, lb, flip):
        forget = lb + (1.0 - lb) * jax.nn.sigmoid(f_logit.astype(jnp.float32))
        kh, logf = to_heads(1.0 - forget), to_heads(jnp.log(forget))
        if flip:
            rev = lambda t: jnp.flip(t, axis=2)
            return rev(gla_chunkwise(rev(qh), rev(kh), rev(vh), rev(logf)))
        return gla_chunkwise(qh, kh, vh, logf)

    o = direction(f_fw, lb_fw, False) + direction(f_bw, lb_bw, True)
    o = o.transpose(0, 2, 1, 3)
    o = rmsnorm(o, norm_gain) * jax.nn.silu(gate.reshape(B, L, HG_HEADS, HG_DV).astype(jnp.float32))
    return o.reshape(B, L, HG_WIDTH).astype(q.dtype)


def dilated_group(q, k, v, dilation, radius, slopes):
    B, L, Hg, dh = q.shape
    d, P = dilation, radius
    Ld = L // d
    nb = -(-Ld // P)
    Lp = nb * P

    def residues(t):
        return t.reshape(B, Ld, d, Hg, dh).transpose(0, 3, 2, 1, 4)

    qr, kr, vr = residues(q), residues(k), residues(v)
    qb = jnp.pad(qr, ((0, 0),) * 3 + ((0, Lp - Ld), (0, 0))).reshape(B, Hg, d, nb, P, dh)
    kv_pad = ((0, 0),) * 3 + ((P, Lp - Ld + P), (0, 0))

    def key_blocks(t):
        tb = jnp.pad(t, kv_pad).reshape(B, Hg, d, nb + 2, P, dh)
        return jnp.concatenate([tb[:, :, :, :-2], tb[:, :, :, 1:-1], tb[:, :, :, 2:]], axis=4)

    kb, vb = key_blocks(kr), key_blocks(vr)
    qi = jnp.arange(P)[:, None]
    kj = jnp.arange(3 * P)[None, :]
    rel = kj - P - qi
    s_pos = jnp.arange(nb)[:, None, None] * P + kj[None] - P
    valid = (jnp.abs(rel) <= radius)[None] & (s_pos >= 0) & (s_pos < Ld)
    dist = (d * jnp.abs(rel)).astype(jnp.float32)
    slopes = jnp.asarray(slopes, jnp.float32)

    scores = jnp.einsum('bhrnid,bhrnjd->bhrnij', qb, kb).astype(jnp.float32) * DA_SCALE
    scores = scores - slopes[:, None, None, None, None] * dist
    scores = jnp.where(valid, scores, NEG_INF)
    lse = jax.nn.logsumexp(scores, axis=-1)
    p = jnp.exp(scores - lse[..., None])
    o = jnp.einsum('bhrnij,bhrnjd->bhrnid', p.astype(vb.dtype), vb)
    o = o.reshape(B, Hg, d, Lp, dh)[:, :, :, :Ld]
    o = o.transpose(0, 3, 2, 1, 4).reshape(B, L, Hg, dh)
    lse = lse.reshape(B, Hg, d, Lp)[:, :, :, :Ld].transpose(0, 3, 2, 1).reshape(B, L, Hg)
    return o, lse


def dilated_mixer(q, k, v, q_gain, k_gain):
    B, L, _ = q.shape
    q = rmsnorm(q.reshape(B, L, DA_HEADS, DA_HEAD_DIM), q_gain)
    k = rmsnorm(k.reshape(B, L, DA_HEADS, DA_HEAD_DIM), k_gain)
    v = v.reshape(B, L, DA_HEADS, DA_HEAD_DIM)
    slopes = alibi_slopes(DA_HEADS)
    outs, lses = [], []
    for g, (window, dilation) in enumerate(DA_CONFIGS):
        lo, hi = g * DA_HEADS_PER_GROUP, (g + 1) * DA_HEADS_PER_GROUP
        o, lse = dilated_group(q[:, :, lo:hi], k[:, :, lo:hi], v[:, :, lo:hi],
                               dilation, window // (2 * dilation), slopes[lo:hi])
        outs.append(o)
        lses.append(lse)
    w = jax.nn.softmax(jnp.stack(lses, axis=0), axis=0)
    o = jnp.sum(w[..., None] * jnp.stack(outs, axis=0).astype(jnp.float32), axis=0)
    return o.reshape(B, L, DA_WIDTH).astype(q.dtype)


def memory_mixer(q, mem_n, w_kv, q_gain, k_gain):
    B, L, _ = q.shape
    M = mem_n.shape[1]
    k, v = jnp.split(mem_n @ w_kv, 2, axis=-1)
    qh = rmsnorm(q.reshape(B, L, MEM_HEADS, MEM_HEAD_DIM), q_gain)
    kh = rmsnorm(k.reshape(B, M, MEM_HEADS, MEM_HEAD_DIM), k_gain)
    vh = v.reshape(B, M, MEM_HEADS, MEM_HEAD_DIM)
    s = jnp.einsum('blhd,bmhd->bhlm', qh, kh).astype(jnp.float32) * MEM_SCALE
    p = jax.nn.softmax(s, axis=-1)
    o = jnp.einsum('bhlm,bmhd->blhd', p.astype(vh.dtype), vh)
    return o.reshape(B, L, MEM_WIDTH)


def _fwd_setup_inputs(seed: int = 0) -> dict:
    key = jax.random.key(seed)
    ks = jax.random.split(key, 22)

    def w(k, shape, fan_in):
        return jax.random.normal(k, shape, jnp.float32) * fan_in ** -0.5

    def gain(k, shape):
        return 1.0 + 0.02 * jax.random.normal(k, shape, jnp.float32)

    return {
        "x": jax.random.normal(ks[0], (BATCH, SEQ, D_MODEL), jnp.float32),
        "mem": jax.random.normal(ks[1], (BATCH, N_MEM, D_MODEL), jnp.float32),
        "norm_mix_gain": gain(ks[2], (DEPTH, D_MODEL)),
        "norm_mem_gain": gain(ks[3], (DEPTH, D_MODEL)),
        "w_in": w(ks[4], (DEPTH, D_MODEL, IN_COLS), D_MODEL),
        "lb_logits_fw": 0.1 * jax.random.normal(ks[5], (DEPTH + 1, HG_WIDTH), jnp.float32),
        "lb_logits_bw": 0.1 * jax.random.normal(ks[6], (DEPTH + 1, HG_WIDTH), jnp.float32),
        "hg_norm_gain": gain(ks[7], (DEPTH, HG_DV)),
        "da_q_gain": gain(ks[8], (DEPTH, DA_HEAD_DIM)),
        "da_k_gain": gain(ks[9], (DEPTH, DA_HEAD_DIM)),
        "w_mem_kv": w(ks[10], (DEPTH, D_MODEL, 2 * MEM_WIDTH), D_MODEL),
        "mem_q_gain": gain(ks[11], (DEPTH, MEM_HEAD_DIM)),
        "mem_k_gain": gain(ks[12], (DEPTH, MEM_HEAD_DIM)),
        "w_proj_hg": w(ks[13], (DEPTH, HG_WIDTH, D_MODEL), HG_WIDTH),
        "w_proj_da": w(ks[14], (DEPTH, DA_WIDTH, D_MODEL), DA_WIDTH),
        "w_proj_mem": w(ks[15], (DEPTH, MEM_WIDTH, D_MODEL), MEM_WIDTH),
        "w_out": w(ks[16], (DEPTH, D_MODEL, D_MODEL), D_MODEL),
        "norm_ffn_gain": gain(ks[17], (DEPTH, D_MODEL)),
        "w_ffn_in": w(ks[18], (DEPTH, D_MODEL, 2 * D_FF), D_MODEL),
        "w_ffn_out": w(ks[19], (DEPTH, D_FF, D_MODEL), D_FF),
    }


def _fwd_reference(x, mem, norm_mix_gain, norm_mem_gain, w_in, lb_logits_fw, lb_logits_bw,
              hg_norm_gain, da_q_gain, da_k_gain, w_mem_kv, mem_q_gain, mem_k_gain,
              w_proj_hg, w_proj_da, w_proj_mem, w_out, norm_ffn_gain, w_ffn_in, w_ffn_out):
    lb_fw_table = jnp.cumsum(jax.nn.softmax(lb_logits_fw.astype(jnp.float32), axis=0), axis=0)
    lb_bw_table = jnp.cumsum(jax.nn.softmax(lb_logits_bw.astype(jnp.float32), axis=0), axis=0)
    for l in range(DEPTH):
        h = rmsnorm(x, norm_mix_gain[l])
        proj = h @ w_in[l]
        (hg_q, hg_f_fw, hg_f_bw, hg_i, hg_g, da_q, da_k, da_v, mem_q,
         gate_hg, gate_da, gate_mem) = jnp.split(proj, IN_SPLIT_POINTS, axis=-1)

        o_hg = hgrn2_mixer(hg_q, hg_f_fw, hg_f_bw, hg_i, hg_g,
                           lb_fw_table[l], lb_bw_table[l], hg_norm_gain[l])
        o_da = dilated_mixer(da_q, da_k, da_v, da_q_gain[l], da_k_gain[l])
        mem_n = rmsnorm(mem, norm_mem_gain[l])
        o_mem = memory_mixer(mem_q, mem_n, w_mem_kv[l], mem_q_gain[l], mem_k_gain[l])

        merged = (jax.nn.sigmoid(gate_hg) * (o_hg @ w_proj_hg[l])
                  + jax.nn.sigmoid(gate_da) * (o_da @ w_proj_da[l])
                  + jax.nn.sigmoid(gate_mem) * (o_mem @ w_proj_mem[l]))
        x = x + merged @ w_out[l]

        h = rmsnorm(x, norm_ffn_gain[l])
        a, b = jnp.split(h @ w_ffn_in[l], 2, axis=-1)
        x = x + (jax.nn.silu(a) * b) @ w_ffn_out[l]
    return x


import jax as _jax
import jax.numpy as _jnp

TWIN_FORMAT = 'train_step'
FWD_PARAMS = ['x', 'mem', 'norm_mix_gain', 'norm_mem_gain', 'w_in', 'lb_logits_fw', 'lb_logits_bw', 'hg_norm_gain', 'da_q_gain', 'da_k_gain', 'w_mem_kv', 'mem_q_gain', 'mem_k_gain', 'w_proj_hg', 'w_proj_da', 'w_proj_mem', 'w_out', 'norm_ffn_gain', 'w_ffn_in', 'w_ffn_out']
TWIN_WEIGHTS = ['norm_mix_gain', 'norm_mem_gain', 'w_in', 'lb_logits_fw', 'lb_logits_bw', 'hg_norm_gain', 'da_q_gain', 'da_k_gain', 'w_mem_kv', 'mem_q_gain', 'mem_k_gain', 'w_proj_hg', 'w_proj_da', 'w_proj_mem', 'w_out', 'norm_ffn_gain', 'w_ffn_in', 'w_ffn_out']
TWIN_DIFF_INPUT = 'x'
TWIN_INPUTS = ['x', 'mem', 'norm_mix_gain', 'norm_mem_gain', 'w_in', 'lb_logits_fw', 'lb_logits_bw', 'hg_norm_gain', 'da_q_gain', 'da_k_gain', 'w_mem_kv', 'mem_q_gain', 'mem_k_gain', 'w_proj_hg', 'w_proj_da', 'w_proj_mem', 'w_out', 'norm_ffn_gain', 'w_ffn_in', 'w_ffn_out', 'loss_target', 'm_norm_mix_gain', 'm_norm_mem_gain', 'm_w_in', 'm_lb_logits_fw', 'm_lb_logits_bw', 'm_hg_norm_gain', 'm_da_q_gain', 'm_da_k_gain', 'm_w_mem_kv', 'm_mem_q_gain', 'm_mem_k_gain', 'm_w_proj_hg', 'm_w_proj_da', 'm_w_proj_mem', 'm_w_out', 'm_norm_ffn_gain', 'm_w_ffn_in', 'm_w_ffn_out', 'v_norm_mix_gain', 'v_norm_mem_gain', 'v_w_in', 'v_lb_logits_fw', 'v_lb_logits_bw', 'v_hg_norm_gain', 'v_da_q_gain', 'v_da_k_gain', 'v_w_mem_kv', 'v_mem_q_gain', 'v_mem_k_gain', 'v_w_proj_hg', 'v_w_proj_da', 'v_w_proj_mem', 'v_w_out', 'v_norm_ffn_gain', 'v_w_ffn_in', 'v_w_ffn_out']
TWIN_OUTPUTS = ['loss', 'grad_x', 'grad_norm_mix_gain', 'grad_norm_mem_gain', 'grad_w_in', 'grad_lb_logits_fw', 'grad_lb_logits_bw', 'grad_hg_norm_gain', 'grad_da_q_gain', 'grad_da_k_gain', 'grad_w_mem_kv', 'grad_mem_q_gain', 'grad_mem_k_gain', 'grad_w_proj_hg', 'grad_w_proj_da', 'grad_w_proj_mem', 'grad_w_out', 'grad_norm_ffn_gain', 'grad_w_ffn_in', 'grad_w_ffn_out', 'delta_norm_mix_gain', 'delta_norm_mem_gain', 'delta_w_in', 'delta_lb_logits_fw', 'delta_lb_logits_bw', 'delta_hg_norm_gain', 'delta_da_q_gain', 'delta_da_k_gain', 'delta_w_mem_kv', 'delta_mem_q_gain', 'delta_mem_k_gain', 'delta_w_proj_hg', 'delta_w_proj_da', 'delta_w_proj_mem', 'delta_w_out', 'delta_norm_ffn_gain', 'delta_w_ffn_in', 'delta_w_ffn_out', 'new_m_norm_mix_gain', 'new_m_norm_mem_gain', 'new_m_w_in', 'new_m_lb_logits_fw', 'new_m_lb_logits_bw', 'new_m_hg_norm_gain', 'new_m_da_q_gain', 'new_m_da_k_gain', 'new_m_w_mem_kv', 'new_m_mem_q_gain', 'new_m_mem_k_gain', 'new_m_w_proj_hg', 'new_m_w_proj_da', 'new_m_w_proj_mem', 'new_m_w_out', 'new_m_norm_ffn_gain', 'new_m_w_ffn_in', 'new_m_w_ffn_out', 'new_v_norm_mix_gain', 'new_v_norm_mem_gain', 'new_v_w_in', 'new_v_lb_logits_fw', 'new_v_lb_logits_bw', 'new_v_hg_norm_gain', 'new_v_da_q_gain', 'new_v_da_k_gain', 'new_v_w_mem_kv', 'new_v_mem_q_gain', 'new_v_mem_k_gain', 'new_v_w_proj_hg', 'new_v_w_proj_da', 'new_v_w_proj_mem', 'new_v_w_out', 'new_v_norm_ffn_gain', 'new_v_w_ffn_in', 'new_v_w_ffn_out']
TWIN_LEAF_KINDS = {'loss': 'loss', 'grad_x': 'grad_x', 'grad_norm_mix_gain': 'grad_w', 'grad_norm_mem_gain': 'grad_w', 'grad_w_in': 'grad_w', 'grad_lb_logits_fw': 'grad_w', 'grad_lb_logits_bw': 'grad_w', 'grad_hg_norm_gain': 'grad_w', 'grad_da_q_gain': 'grad_w', 'grad_da_k_gain': 'grad_w', 'grad_w_mem_kv': 'grad_w', 'grad_mem_q_gain': 'grad_w', 'grad_mem_k_gain': 'grad_w', 'grad_w_proj_hg': 'grad_w', 'grad_w_proj_da': 'grad_w', 'grad_w_proj_mem': 'grad_w', 'grad_w_out': 'grad_w', 'grad_norm_ffn_gain': 'grad_w', 'grad_w_ffn_in': 'grad_w', 'grad_w_ffn_out': 'grad_w', 'delta_norm_mix_gain': 'delta_w', 'delta_norm_mem_gain': 'delta_w', 'delta_w_in': 'delta_w', 'delta_lb_logits_fw': 'delta_w', 'delta_lb_logits_bw': 'delta_w', 'delta_hg_norm_gain': 'delta_w', 'delta_da_q_gain': 'delta_w', 'delta_da_k_gain': 'delta_w', 'delta_w_mem_kv': 'delta_w', 'delta_mem_q_gain': 'delta_w', 'delta_mem_k_gain': 'delta_w', 'delta_w_proj_hg': 'delta_w', 'delta_w_proj_da': 'delta_w', 'delta_w_proj_mem': 'delta_w', 'delta_w_out': 'delta_w', 'delta_norm_ffn_gain': 'delta_w', 'delta_w_ffn_in': 'delta_w', 'delta_w_ffn_out': 'delta_w', 'new_m_norm_mix_gain': 'new_m', 'new_m_norm_mem_gain': 'new_m', 'new_m_w_in': 'new_m', 'new_m_lb_logits_fw': 'new_m', 'new_m_lb_logits_bw': 'new_m', 'new_m_hg_norm_gain': 'new_m', 'new_m_da_q_gain': 'new_m', 'new_m_da_k_gain': 'new_m', 'new_m_w_mem_kv': 'new_m', 'new_m_mem_q_gain': 'new_m', 'new_m_mem_k_gain': 'new_m', 'new_m_w_proj_hg': 'new_m', 'new_m_w_proj_da': 'new_m', 'new_m_w_proj_mem': 'new_m', 'new_m_w_out': 'new_m', 'new_m_norm_ffn_gain': 'new_m', 'new_m_w_ffn_in': 'new_m', 'new_m_w_ffn_out': 'new_m', 'new_v_norm_mix_gain': 'new_v', 'new_v_norm_mem_gain': 'new_v', 'new_v_w_in': 'new_v', 'new_v_lb_logits_fw': 'new_v', 'new_v_lb_logits_bw': 'new_v', 'new_v_hg_norm_gain': 'new_v', 'new_v_da_q_gain': 'new_v', 'new_v_da_k_gain': 'new_v', 'new_v_w_mem_kv': 'new_v', 'new_v_mem_q_gain': 'new_v', 'new_v_mem_k_gain': 'new_v', 'new_v_w_proj_hg': 'new_v', 'new_v_w_proj_da': 'new_v', 'new_v_w_proj_mem': 'new_v', 'new_v_w_out': 'new_v', 'new_v_norm_ffn_gain': 'new_v', 'new_v_w_ffn_in': 'new_v', 'new_v_w_ffn_out': 'new_v'}


def _forward(args):
    return _fwd_reference(*[args[k] for k in FWD_PARAMS])


def _output_shape():
    out = _jax.eval_shape(lambda: _forward(_fwd_setup_inputs(0)))
    return out.shape, out.dtype

N_MICROBATCH = 1
ADAM_LR = 0.001
ADAM_B1 = 0.9
ADAM_B2 = 0.999
ADAM_EPS = 1e-08
ADAM_WD = 0.01
ADAM_STEP = 10
PER_EXAMPLE_BATCH_AXIS = {'x': 0, 'mem': 0, 'loss_target': 0}
SHARED_INPUTS = []
_WEIGHT_DTYPES = {'norm_mix_gain': _jnp.float32, 'norm_mem_gain': _jnp.float32, 'w_in': _jnp.float32, 'lb_logits_fw': _jnp.float32, 'lb_logits_bw': _jnp.float32, 'hg_norm_gain': _jnp.float32, 'da_q_gain': _jnp.float32, 'da_k_gain': _jnp.float32, 'w_mem_kv': _jnp.float32, 'mem_q_gain': _jnp.float32, 'mem_k_gain': _jnp.float32, 'w_proj_hg': _jnp.float32, 'w_proj_da': _jnp.float32, 'w_proj_mem': _jnp.float32, 'w_out': _jnp.float32, 'norm_ffn_gain': _jnp.float32, 'w_ffn_in': _jnp.float32, 'w_ffn_out': _jnp.float32}
MOMENT_SCALE = {'norm_mix_gain': 4.333701e+00, 'norm_mem_gain': 9.932111e-02, 'w_in': 6.346454e-02, 'lb_logits_fw': 7.398245e-03, 'lb_logits_bw': 7.395939e-03, 'hg_norm_gain': 2.613163e+01, 'da_q_gain': 1.680047e+00, 'da_k_gain': 1.683404e+00, 'w_mem_kv': 5.912382e-02, 'mem_q_gain': 7.105685e-01, 'mem_k_gain': 7.028634e-01, 'w_proj_hg': 1.567786e-01, 'w_proj_da': 5.220735e-02, 'w_proj_mem': 5.410421e-02, 'w_out': 1.716495e-01, 'norm_ffn_gain': 2.466854e+01, 'w_ffn_in': 1.552791e-01, 'w_ffn_out': 2.429169e-01}


def _to_microbatches(a, axis):
    t = _jnp.moveaxis(a, axis, 0)
    t = t.reshape((N_MICROBATCH, t.shape[0] // N_MICROBATCH) + t.shape[1:])
    return _jnp.moveaxis(t, 1, axis + 1)


def setup_inputs(seed: int = 0) -> dict:
    inp = _fwd_setup_inputs(seed)
    key = _jax.random.fold_in(_jax.random.key(seed), 7919)
    shape, _ = _output_shape()
    out = dict(inp)
    out["loss_target"] = _jax.random.normal(_jax.random.fold_in(key, 0), shape, _jnp.float32)
    for i, name in enumerate(TWIN_WEIGHTS):
        w = inp[name].astype(_jnp.float32)
        if MOMENT_SCALE is None:
            s = _jnp.sqrt(_jnp.mean(_jnp.square(w)) + 1e-30)
        else:
            s = MOMENT_SCALE[name]
        km, kv = _jax.random.split(_jax.random.fold_in(key, i + 1))
        out[name] = w
        out["m_" + name] = s * _jax.random.normal(km, w.shape, _jnp.float32)
        out["v_" + name] = (s * s) * _jax.random.uniform(kv, w.shape, _jnp.float32, 0.5, 1.5)
    if N_MICROBATCH > 1:
        for name, axis in PER_EXAMPLE_BATCH_AXIS.items():
            out[name] = _to_microbatches(out[name], axis)
    return {'x': out['x'], 'mem': out['mem'], 'norm_mix_gain': out['norm_mix_gain'], 'norm_mem_gain': out['norm_mem_gain'], 'w_in': out['w_in'], 'lb_logits_fw': out['lb_logits_fw'], 'lb_logits_bw': out['lb_logits_bw'], 'hg_norm_gain': out['hg_norm_gain'], 'da_q_gain': out['da_q_gain'], 'da_k_gain': out['da_k_gain'], 'w_mem_kv': out['w_mem_kv'], 'mem_q_gain': out['mem_q_gain'], 'mem_k_gain': out['mem_k_gain'], 'w_proj_hg': out['w_proj_hg'], 'w_proj_da': out['w_proj_da'], 'w_proj_mem': out['w_proj_mem'], 'w_out': out['w_out'], 'norm_ffn_gain': out['norm_ffn_gain'], 'w_ffn_in': out['w_ffn_in'], 'w_ffn_out': out['w_ffn_out'], 'loss_target': out['loss_target'], 'm_norm_mix_gain': out['m_norm_mix_gain'], 'm_norm_mem_gain': out['m_norm_mem_gain'], 'm_w_in': out['m_w_in'], 'm_lb_logits_fw': out['m_lb_logits_fw'], 'm_lb_logits_bw': out['m_lb_logits_bw'], 'm_hg_norm_gain': out['m_hg_norm_gain'], 'm_da_q_gain': out['m_da_q_gain'], 'm_da_k_gain': out['m_da_k_gain'], 'm_w_mem_kv': out['m_w_mem_kv'], 'm_mem_q_gain': out['m_mem_q_gain'], 'm_mem_k_gain': out['m_mem_k_gain'], 'm_w_proj_hg': out['m_w_proj_hg'], 'm_w_proj_da': out['m_w_proj_da'], 'm_w_proj_mem': out['m_w_proj_mem'], 'm_w_out': out['m_w_out'], 'm_norm_ffn_gain': out['m_norm_ffn_gain'], 'm_w_ffn_in': out['m_w_ffn_in'], 'm_w_ffn_out': out['m_w_ffn_out'], 'v_norm_mix_gain': out['v_norm_mix_gain'], 'v_norm_mem_gain': out['v_norm_mem_gain'], 'v_w_in': out['v_w_in'], 'v_lb_logits_fw': out['v_lb_logits_fw'], 'v_lb_logits_bw': out['v_lb_logits_bw'], 'v_hg_norm_gain': out['v_hg_norm_gain'], 'v_da_q_gain': out['v_da_q_gain'], 'v_da_k_gain': out['v_da_k_gain'], 'v_w_mem_kv': out['v_w_mem_kv'], 'v_mem_q_gain': out['v_mem_q_gain'], 'v_mem_k_gain': out['v_mem_k_gain'], 'v_w_proj_hg': out['v_w_proj_hg'], 'v_w_proj_da': out['v_w_proj_da'], 'v_w_proj_mem': out['v_w_proj_mem'], 'v_w_out': out['v_w_out'], 'v_norm_ffn_gain': out['v_norm_ffn_gain'], 'v_w_ffn_in': out['v_w_ffn_in'], 'v_w_ffn_out': out['v_w_ffn_out']}


def _loss(weights, diff, rest, loss_target):
    with _jax.named_scope("forward"):
        args = {**rest, TWIN_DIFF_INPUT: diff, **{k: w.astype(_WEIGHT_DTYPES[k]) for k, w in weights.items()}}
        y = _forward(args)
    with _jax.named_scope("loss_head"):
        err = _jnp.square(y.astype(_jnp.float32) - loss_target)
        return 0.5 * _jnp.sum(_jnp.mean(err, axis=-1)) if err.ndim else 0.5 * err


def _adamw(w, g, m, v):
    m = ADAM_B1 * m + (1.0 - ADAM_B1) * g
    v = ADAM_B2 * v + (1.0 - ADAM_B2) * _jnp.square(g)
    m_hat = m / (1.0 - ADAM_B1 ** ADAM_STEP)
    v_hat = v / (1.0 - ADAM_B2 ** ADAM_STEP)
    delta = -ADAM_LR * (m_hat / (_jnp.sqrt(v_hat) + ADAM_EPS) + ADAM_WD * w)
    return delta, m, v


def reference(x, mem, norm_mix_gain, norm_mem_gain, w_in, lb_logits_fw, lb_logits_bw, hg_norm_gain, da_q_gain, da_k_gain, w_mem_kv, mem_q_gain, mem_k_gain, w_proj_hg, w_proj_da, w_proj_mem, w_out, norm_ffn_gain, w_ffn_in, w_ffn_out, loss_target, m_norm_mix_gain, m_norm_mem_gain, m_w_in, m_lb_logits_fw, m_lb_logits_bw, m_hg_norm_gain, m_da_q_gain, m_da_k_gain, m_w_mem_kv, m_mem_q_gain, m_mem_k_gain, m_w_proj_hg, m_w_proj_da, m_w_proj_mem, m_w_out, m_norm_ffn_gain, m_w_ffn_in, m_w_ffn_out, v_norm_mix_gain, v_norm_mem_gain, v_w_in, v_lb_logits_fw, v_lb_logits_bw, v_hg_norm_gain, v_da_q_gain, v_da_k_gain, v_w_mem_kv, v_mem_q_gain, v_mem_k_gain, v_w_proj_hg, v_w_proj_da, v_w_proj_mem, v_w_out, v_norm_ffn_gain, v_w_ffn_in, v_w_ffn_out):
    given = dict(x=x, mem=mem, norm_mix_gain=norm_mix_gain, norm_mem_gain=norm_mem_gain, w_in=w_in, lb_logits_fw=lb_logits_fw, lb_logits_bw=lb_logits_bw, hg_norm_gain=hg_norm_gain, da_q_gain=da_q_gain, da_k_gain=da_k_gain, w_mem_kv=w_mem_kv, mem_q_gain=mem_q_gain, mem_k_gain=mem_k_gain, w_proj_hg=w_proj_hg, w_proj_da=w_proj_da, w_proj_mem=w_proj_mem, w_out=w_out, norm_ffn_gain=norm_ffn_gain, w_ffn_in=w_ffn_in, w_ffn_out=w_ffn_out, loss_target=loss_target, m_norm_mix_gain=m_norm_mix_gain, m_norm_mem_gain=m_norm_mem_gain, m_w_in=m_w_in, m_lb_logits_fw=m_lb_logits_fw, m_lb_logits_bw=m_lb_logits_bw, m_hg_norm_gain=m_hg_norm_gain, m_da_q_gain=m_da_q_gain, m_da_k_gain=m_da_k_gain, m_w_mem_kv=m_w_mem_kv, m_mem_q_gain=m_mem_q_gain, m_mem_k_gain=m_mem_k_gain, m_w_proj_hg=m_w_proj_hg, m_w_proj_da=m_w_proj_da, m_w_proj_mem=m_w_proj_mem, m_w_out=m_w_out, m_norm_ffn_gain=m_norm_ffn_gain, m_w_ffn_in=m_w_ffn_in, m_w_ffn_out=m_w_ffn_out, v_norm_mix_gain=v_norm_mix_gain, v_norm_mem_gain=v_norm_mem_gain, v_w_in=v_w_in, v_lb_logits_fw=v_lb_logits_fw, v_lb_logits_bw=v_lb_logits_bw, v_hg_norm_gain=v_hg_norm_gain, v_da_q_gain=v_da_q_gain, v_da_k_gain=v_da_k_gain, v_w_mem_kv=v_w_mem_kv, v_mem_q_gain=v_mem_q_gain, v_mem_k_gain=v_mem_k_gain, v_w_proj_hg=v_w_proj_hg, v_w_proj_da=v_w_proj_da, v_w_proj_mem=v_w_proj_mem, v_w_out=v_w_out, v_norm_ffn_gain=v_norm_ffn_gain, v_w_ffn_in=v_w_ffn_in, v_w_ffn_out=v_w_ffn_out)
    weights = {n: given[n] for n in TWIN_WEIGHTS}
    shared = {n: given[n] for n in SHARED_INPUTS}
    per_example = {n: given[n] for n in ['x', 'mem']}
    grad_fn = _jax.value_and_grad(_loss, argnums=(0, 1))

    def one_microbatch(ex, loss_target):
        ex = dict(ex)
        diff = ex.pop(TWIN_DIFF_INPUT)
        return grad_fn(weights, diff, {**shared, **ex}, loss_target)

    if N_MICROBATCH == 1:
        loss, (grad_w, grad_x) = one_microbatch(per_example, given["loss_target"])
    else:
        def body(carry, xs):
            loss_sum, grad_sum = carry
            l_k, (gw_k, gx_k) = one_microbatch(xs[0], xs[1])
            with _jax.named_scope("update"):
                return (loss_sum + l_k, _jax.tree.map(_jnp.add, grad_sum, gw_k)), gx_k

        init = (_jnp.zeros((), _jnp.float32), _jax.tree.map(_jnp.zeros_like, weights))
        (loss, grad_w), grad_x = _jax.lax.scan(body, init, (per_example, given["loss_target"]))
    with _jax.named_scope("update"):
        delta_w, new_m, new_v = {}, {}, {}
        for n in TWIN_WEIGHTS:
            delta_w[n], new_m[n], new_v[n] = _adamw(weights[n], grad_w[n], given["m_" + n], given["v_" + n])
    return (loss, grad_x, *[grad_w[n] for n in TWIN_WEIGHTS], *[delta_w[n] for n in TWIN_WEIGHTS],
            *[new_m[n] for n in TWIN_WEIGHTS], *[new_v[n] for n in TWIN_WEIGHTS])
```

```python
import numpy as np
import jax
import jax.numpy as jnp
from jax import lax
from jax.experimental import pallas as pl
from jax.experimental.pallas import tpu as pltpu

F32 = jnp.float32
BF16 = jnp.bfloat16
MESH = pl.DeviceIdType.MESH

SEQ = 4096
D_MODEL = 1024
N_DEV = 8
N_MEM = 256
RMS_EPS = 1e-6
NEG_INF = -1e30
LANE = 128
HEAD_DIM = 128
HG_HEADS = 8
HG_CHUNK = 64
HG_SCALE = HEAD_DIM ** -0.5
DA_DILATIONS = (1, 4, 16)
DA_RADIUS = 64
DA_HEADS_PER_GROUP = 4
DA_HEADS = 12
DA_WIDTH = 512
DA_SCALE = HEAD_DIM ** -0.5
DA_QB = 128
DA_WIN = 256
MEM_HEADS = 4
MEM_WIDTH = 512
MEM_SCALE = HEAD_DIM ** -0.5
D_FF = 2816
IN_COLS = 13312
IN_SHARD = IN_COLS // N_DEV
CB_HG_Q, CB_F, CB_HG_I, CB_HG_G = 0, 8, 24, 32
CB_DA_Q, CB_DA_K, CB_DA_V, CB_MEM_Q = 40, 52, 64, 76
N_CB = IN_COLS // LANE
ADAM_LR, ADAM_B1, ADAM_B2, ADAM_EPS, ADAM_WD, ADAM_STEP = 0.001, 0.9, 0.999, 1e-08, 0.01, 10
VMEM_BYTES_V7X = 64 * 1024 * 1024
SMALL_ROWS = 64

_NN = (((1,), (0,)), ((), ()))
_NT = (((1,), (1,)), ((), ()))
_TN = (((0,), (0,)), ((), ()))


def _dot(a, b, dims):
    return lax.dot_general(a.astype(BF16), b.astype(BF16), dims, preferred_element_type=F32)


def _dot_exact(a, b, dims):
    return lax.dot_general(a, b, dims, precision=lax.Precision.HIGHEST, preferred_element_type=F32)


def _sigmoid(x):
    return 1.0 / (1.0 + jnp.exp(-x))


def _params(semantics, est_bytes):
    limit = int(min(VMEM_BYTES_V7X - (6 << 20), max(32 << 20, est_bytes * 3 // 2)))
    return pltpu.CompilerParams(dimension_semantics=semantics, vmem_limit_bytes=limit)


def _nbytes(shape, dtype):
    return int(np.prod(shape)) * jnp.dtype(dtype).itemsize


def _alibi_slopes(n):
    return (2.0 ** (-8.0 * np.arange(1, n + 1) / n)).astype(np.float32)


def _matmul(a, b, mode, out_dtype, tm, tn, tk, name, b_stacked=False, out_stacked=False, n_outer=False):
    if mode == "tn":
        kdim, m = a.shape
    else:
        m, kdim = a.shape
    if b_stacked:
        if mode == "nn":
            n = b.shape[0] * b.shape[2]
            assert tn == b.shape[2] and tk == kdim == b.shape[1]
        else:
            assert mode == "nt" and tk == b.shape[2] and b.shape[0] * tk == kdim
            n = b.shape[1]
    else:
        n = b.shape[0] if mode == "nt" else b.shape[1]
    assert m % tm == 0 and n % tn == 0 and kdim % tk == 0
    gm, gn, gk = m // tm, n // tn, kdim // tk

    def ijk(f):
        if n_outer:
            return lambda j, i, k: f(i, j, k)
        return f

    if mode == "tn":
        a_spec = pl.BlockSpec((tk, tm), ijk(lambda i, j, k: (k, i)))
    else:
        a_spec = pl.BlockSpec((tm, tk), ijk(lambda i, j, k: (i, k)))
    if b_stacked and mode == "nn":
        b_spec = pl.BlockSpec((None, tk, tn), ijk(lambda i, j, k: (j, 0, 0)))
    elif b_stacked:
        b_spec = pl.BlockSpec((None, tn, tk), ijk(lambda i, j, k: (k, j, 0)))
    elif mode == "nt":
        b_spec = pl.BlockSpec((tn, tk), ijk(lambda i, j, k: (j, k)))
    else:
        b_spec = pl.BlockSpec((tk, tn), ijk(lambda i, j, k: (k, j)))
    if out_stacked:
        assert tm == m
        out_shape = jax.ShapeDtypeStruct((gn, m, tn), out_dtype)
        o_spec = pl.BlockSpec((None, tm, tn), ijk(lambda i, j, k: (j, i, 0)))
    else:
        out_shape = jax.ShapeDtypeStruct((m, n), out_dtype)
        o_spec = pl.BlockSpec((tm, tn), ijk(lambda i, j, k: (i, j)))
    dims = {"nn": _NN, "nt": _NT, "tn": _TN}[mode]

    def body(a_ref, b_ref, o_ref, *scratch):
        part = _dot(a_ref[...], b_ref[...], dims)
        if gk == 1:
            o_ref[...] = part.astype(out_dtype)
            return
        acc_ref, = scratch
        k = pl.program_id(2)

        @pl.when(k == 0)
        def _():
            acc_ref[...] = part

        @pl.when(k > 0)
        def _():
            acc_ref[...] += part

        @pl.when(k == gk - 1)
        def _():
            o_ref[...] = acc_ref[...].astype(out_dtype)

    a_tile = _nbytes((tm, tk), a.dtype)
    b_tile = _nbytes((tk, tn), b.dtype)
    o_tile = _nbytes((tm, tn), out_dtype)
    est = 2 * (a_tile + b_tile + o_tile) + 3 * tm * tn * 4 + (a_tile + b_tile)
    grid = (gn, gm, gk) if n_outer else (gm, gn, gk)
    return pl.pallas_call(
        body, name=name, grid=grid, in_specs=[a_spec, b_spec], out_specs=o_spec, out_shape=out_shape,
        scratch_shapes=[] if gk == 1 else [pltpu.VMEM((tm, tn), F32)],
        compiler_params=_params(("parallel", "parallel", "arbitrary"), est),
    )(a, b)


def _row_spec(tr, width, col_block=0):
    return pl.BlockSpec((tr, width), lambda i: (i, col_block))


def _bcast_spec(width):
    return pl.BlockSpec((1, width), lambda i: (0, 0))


def _rmsnorm_fwd(x, gain, name, tr):
    rows, width = x.shape

    def body(x_ref, g_ref, o_ref):
        xv = x_ref[...]
        r = lax.rsqrt(jnp.mean(xv * xv, axis=-1, keepdims=True) + RMS_EPS)
        o_ref[...] = (xv * r * g_ref[...]).astype(BF16)

    return pl.pallas_call(
        body, name=name, grid=(rows // tr,), in_specs=[_row_spec(tr, width), _bcast_spec(width)],
        out_specs=_row_spec(tr, width), out_shape=jax.ShapeDtypeStruct((rows, width), BF16),
        compiler_params=_params(("parallel",), 6 * tr * width * 4),
    )(x, gain)


def _residual_rmsnorm_fwd(x, u, gain, name, tr):
    rows, width = x.shape

    def body(x_ref, u_ref, g_ref, x1_ref, h_ref):
        xv = x_ref[...] + u_ref[...]
        x1_ref[...] = xv
        r = lax.rsqrt(jnp.mean(xv * xv, axis=-1, keepdims=True) + RMS_EPS)
        h_ref[...] = (xv * r * g_ref[...]).astype(BF16)

    return pl.pallas_call(
        body, name=name, grid=(rows // tr,),
        in_specs=[_row_spec(tr, width), _row_spec(tr, width), _bcast_spec(width)],
        out_specs=[_row_spec(tr, width), _row_spec(tr, width)],
        out_shape=[jax.ShapeDtypeStruct((rows, width), F32), jax.ShapeDtypeStruct((rows, width), BF16)],
        compiler_params=_params(("parallel",), 10 * tr * width * 4),
    )(x, u, gain)


def _rmsnorm_bwd(x, dh, dres, gain, name, tr):
    rows, width = x.shape

    def body(x_ref, dh_ref, dres_ref, g_ref, dx_ref, dxb_ref, dg_ref):
        xv = x_ref[...]
        r = lax.rsqrt(jnp.mean(xv * xv, axis=-1, keepdims=True) + RMS_EPS)
        xhat = xv * r
        dhv = dh_ref[...]
        dyg = dhv * g_ref[...]
        dx = dres_ref[...] + r * (dyg - xhat * jnp.mean(dyg * xhat, axis=-1, keepdims=True))
        dx_ref[...] = dx
        dxb_ref[...] = dx.astype(BF16)
        part = jnp.sum(dhv * xhat, axis=0, keepdims=True)

        @pl.when(pl.program_id(0) == 0)
        def _():
            dg_ref[...] = part

        @pl.when(pl.program_id(0) > 0)
        def _():
            dg_ref[...] += part

    return pl.pallas_call(
        body, name=name, grid=(rows // tr,),
        in_specs=[_row_spec(tr, width), _row_spec(tr, width), _row_spec(tr, width), _bcast_spec(width)],
        out_specs=[_row_spec(tr, width), _row_spec(tr, width), _bcast_spec(width)],
        out_shape=[jax.ShapeDtypeStruct((rows, width), F32), jax.ShapeDtypeStruct((rows, width), BF16),
                   jax.ShapeDtypeStruct((1, width), F32)],
        compiler_params=_params(("arbitrary",), 14 * tr * width * 4),
    )(x, dh, dres, gain)


def _gain_grad(x, dh, name):
    rows, width = x.shape

    def body(x_ref, dh_ref, dg_ref):
        xv = x_ref[...]
        r = lax.rsqrt(jnp.mean(xv * xv, axis=-1, keepdims=True) + RMS_EPS)
        dg_ref[...] = jnp.sum(dh_ref[...] * xv * r, axis=0, keepdims=True)

    return pl.pallas_call(
        body, name=name, grid=(1,), in_specs=[_row_spec(rows, width), _row_spec(rows, width)],
        out_specs=_bcast_spec(width), out_shape=jax.ShapeDtypeStruct((1, width), F32),
        compiler_params=_params(("arbitrary",), 6 * rows * width * 4),
    )(x, dh)


def _lb_table(logits, name):
    slots, width = logits.shape

    def body(l_ref, o_ref):
        lv = l_ref[...]
        mx = jnp.max(lv, axis=0, keepdims=True)
        e = jnp.exp(lv - mx)
        o_ref[...] = e[0:1, :] / jnp.sum(e, axis=0, keepdims=True)

    return pl.pallas_call(
        body, name=name, grid=(1,), in_specs=[pl.BlockSpec((slots, width), lambda i: (0, 0))],
        out_specs=_bcast_spec(width), out_shape=jax.ShapeDtypeStruct((1, width), F32),
    )(logits)


def _gate_merge_fwd(proj, t_hg, t_da, t_mem, tr):
    w = D_MODEL

    def body(ghg_ref, gda_ref, gmem_ref, thg_ref, tda_ref, tmem_ref, o_ref):
        acc = _sigmoid(ghg_ref[...]) * thg_ref[...]
        acc += _sigmoid(gda_ref[...]) * tda_ref[...]
        acc += _sigmoid(gmem_ref[...]) * tmem_ref[...]
        o_ref[...] = acc.astype(BF16)

    return pl.pallas_call(
        body, name="gate_merge_fwd", grid=(SEQ // tr,),
        in_specs=[_row_spec(tr, w, 10), _row_spec(tr, w, 11), _row_spec(tr, w, 12),
                  _row_spec(tr, w), _row_spec(tr, w), _row_spec(tr, w)],
        out_specs=_row_spec(tr, w), out_shape=jax.ShapeDtypeStruct((SEQ, w), BF16),
        compiler_params=_params(("parallel",), 16 * tr * w * 4),
    )(proj, proj, proj, t_hg, t_da, t_mem)


def _gate_merge_bwd(proj, t_hg, t_da, t_mem, dmerged, tr):
    w = D_MODEL

    def body(ghg_ref, gda_ref, gmem_ref, thg_ref, tda_ref, tmem_ref, dm_ref, dthg_ref, dtda_ref, dtmem_ref, dg_ref):
        dm = dm_ref[...]
        for b, (g_ref, t_ref, dt_ref) in enumerate(((ghg_ref, thg_ref, dthg_ref), (gda_ref, tda_ref, dtda_ref),
                                                    (gmem_ref, tmem_ref, dtmem_ref))):
            s = _sigmoid(g_ref[...])
            dt_ref[...] = (s * dm).astype(BF16)
            dg_ref[:, b * w:(b + 1) * w] = (dm * t_ref[...] * s * (1.0 - s)).astype(BF16)

    return pl.pallas_call(
        body, name="gate_merge_bwd", grid=(SEQ // tr,),
        in_specs=[_row_spec(tr, w, 10), _row_spec(tr, w, 11), _row_spec(tr, w, 12),
                  _row_spec(tr, w), _row_spec(tr, w), _row_spec(tr, w), _row_spec(tr, w)],
        out_specs=[_row_spec(tr, w), _row_spec(tr, w), _row_spec(tr, w), _row_spec(tr, 3 * w)],
        out_shape=[jax.ShapeDtypeStruct((SEQ, w), BF16)] * 3 + [jax.ShapeDtypeStruct((SEQ, 3 * w), BF16)],
        compiler_params=_params(("parallel",), 24 * tr * w * 4),
    )(proj, proj, proj, t_hg, t_da, t_mem, dmerged)


def _swiglu_fwd(ab, tr):
    def body(ab_ref, o_ref):
        a = ab_ref[:, :D_FF]
        b = ab_ref[:, D_FF:]
        o_ref[...] = (a * _sigmoid(a) * b).astype(BF16)

    return pl.pallas_call(
        body, name="swiglu_fwd", grid=(SEQ // tr,), in_specs=[_row_spec(tr, 2 * D_FF)],
        out_specs=_row_spec(tr, D_FF), out_shape=jax.ShapeDtypeStruct((SEQ, D_FF), BF16),
        compiler_params=_params(("parallel",), 8 * tr * 2 * D_FF * 4),
    )(ab)


def _swiglu_bwd(ab, dact, tr):
    def body(ab_ref, d_ref, o_ref):
        a = ab_ref[:, :D_FF]
        b = ab_ref[:, D_FF:]
        d = d_ref[...]
        s = _sigmoid(a)
        silu = a * s
        o_ref[:, :D_FF] = (d * b * (s + silu * (1.0 - s))).astype(BF16)
        o_ref[:, D_FF:] = (d * silu).astype(BF16)

    return pl.pallas_call(
        body, name="swiglu_bwd", grid=(SEQ // tr,), in_specs=[_row_spec(tr, 2 * D_FF), _row_spec(tr, D_FF)],
        out_specs=_row_spec(tr, 2 * D_FF), out_shape=jax.ShapeDtypeStruct((SEQ, 2 * D_FF), BF16),
        compiler_params=_params(("parallel",), 10 * tr * 2 * D_FF * 4),
    )(ab, dact)


def _loss_head(x1, z, target, tr):
    w = D_MODEL

    def body(x_ref, z_ref, t_ref, dy_ref, dyb_ref, loss_ref, acc_ref):
        err = x_ref[...] + z_ref[...] - t_ref[...]
        dy = err * (1.0 / w)
        dy_ref[...] = dy
        dyb_ref[...] = dy.astype(BF16)
        part = jnp.sum(err * err, axis=0, keepdims=True)

        @pl.when(pl.program_id(0) == 0)
        def _():
            acc_ref[...] = part

        @pl.when(pl.program_id(0) > 0)
        def _():
            acc_ref[...] += part

        @pl.when(pl.program_id(0) == SEQ // tr - 1)
        def _():
            total = jnp.sum(acc_ref[...], axis=1, keepdims=True) * (0.5 / w)
            loss_ref[...] = jnp.broadcast_to(total, (1, LANE))

    return pl.pallas_call(
        body, name="loss_head", grid=(SEQ // tr,),
        in_specs=[_row_spec(tr, w), _row_spec(tr, w), _row_spec(tr, w)],
        out_specs=[_row_spec(tr, w), _row_spec(tr, w), _bcast_spec(LANE)],
        out_shape=[jax.ShapeDtypeStruct((SEQ, w), F32), jax.ShapeDtypeStruct((SEQ, w), BF16),
                   jax.ShapeDtypeStruct((1, LANE), F32)],
        scratch_shapes=[pltpu.VMEM((1, w), F32)],
        compiler_params=_params(("arbitrary",), 12 * tr * w * 4),
    )(x1, z, target)


def _gla_chunk(qc, fc, lbv, mask, maskf):
    sq = _sigmoid(qc)
    q = qc * sq * HG_SCALE
    sf = _sigmoid(fc)
    forget = lbv + (1.0 - lbv) * sf
    k = 1.0 - forget
    logf = jnp.log(forget)
    b = _dot_exact(maskf, logf, _NN)
    bt = jnp.sum(logf, axis=0, keepdims=True)
    eb = jnp.exp(b)
    qt = q * eb
    kt = k * jnp.exp(-b)
    kh = k * jnp.exp(bt - b)
    a = jnp.where(mask, _dot(qt, kt, _NT), 0.0)
    return dict(sq=sq, sf=sf, forget=forget, k=k, b=b, bt=bt, eb=eb, qt=qt, kt=kt, kh=kh, a=a)


def _gla_mask(direction):
    row = lax.broadcasted_iota(jnp.int32, (HG_CHUNK, HG_CHUNK), 0)
    col = lax.broadcasted_iota(jnp.int32, (HG_CHUNK, HG_CHUNK), 1)
    mask = jnp.where(direction == 0, row - col, col - row) >= 0
    return mask, mask.astype(F32)


def _head_spec(col_block0):
    return pl.BlockSpec((SEQ, HEAD_DIM), lambda h, d: (0, col_block0 + h))


def _gla_fwd(proj, lb, gain):
    nck = SEQ // HG_CHUNK

    def body(q_ref, f_ref, v_ref, g_ref, lb_ref, gain_ref, ohg_ref, opre_ref, st_ref):
        d = pl.program_id(1)
        mask, maskf = _gla_mask(d)
        lbv = lb_ref[...]

        def step(i, st):
            c = jnp.where(d == 0, i, nck - 1 - i)
            rows = pl.ds(pl.multiple_of(c * HG_CHUNK, HG_CHUNK), HG_CHUNK)
            ck = _gla_chunk(q_ref[rows, :], f_ref[rows, :], lbv, mask, maskf)
            v = v_ref[rows, :]
            o = _dot(ck["a"], v, _NN) + _dot(ck["qt"], st, _NT)
            st_ref[c] = st

            @pl.when(d == 0)
            def _():
                opre_ref[rows, :] = o

            @pl.when(d == 1)
            def _():
                opre_ref[rows, :] += o

            return st * jnp.exp(ck["bt"]) + _dot(v, ck["kh"], _TN)

        lax.fori_loop(0, nck, step, jnp.zeros((HEAD_DIM, HEAD_DIM), F32))

        @pl.when(d == 1)
        def _():
            o = opre_ref[...]
            r = lax.rsqrt(jnp.mean(o * o, axis=-1, keepdims=True) + RMS_EPS)
            g = g_ref[...]
            ohg_ref[...] = (o * r * gain_ref[...] * (g * _sigmoid(g))).astype(BF16)

    blk = SEQ * HEAD_DIM * 4
    return pl.pallas_call(
        body, name="gla_fwd", grid=(HG_HEADS, 2),
        in_specs=[_head_spec(CB_HG_Q),
                  pl.BlockSpec((SEQ, HEAD_DIM), lambda h, d: (0, CB_F + 8 * d + h)),
                  _head_spec(CB_HG_I), _head_spec(CB_HG_G),
                  pl.BlockSpec((None, None, 1, HEAD_DIM), lambda h, d: (d, h, 0, 0)),
                  pl.BlockSpec((1, HEAD_DIM), lambda h, d: (0, 0))],
        out_specs=[_head_spec(0), _head_spec(0),
                   pl.BlockSpec((None, None, nck, HEAD_DIM, HEAD_DIM), lambda h, d: (h, d, 0, 0, 0))],
        out_shape=[jax.ShapeDtypeStruct((SEQ, D_MODEL), BF16), jax.ShapeDtypeStruct((SEQ, D_MODEL), F32),
                   jax.ShapeDtypeStruct((HG_HEADS, 2, nck, HEAD_DIM, HEAD_DIM), F32)],
        compiler_params=_params(("parallel", "arbitrary"), 8 * blk + 3 * blk + 2 * blk),
    )(proj, proj, proj, proj, lb, gain)


def _gla_bwd(proj, lb, gain, o_pre, states, do_hg):
    nck = SEQ // HG_CHUNK

    def body(q_ref, f_ref, v_ref, g_ref, lb_ref, gain_ref, opre_ref, dohg_ref, st_ref,
             dq_ref, df_ref, dv_ref, dg_ref, dlb_ref, dgain_ref, do_scr, dq_acc, dv_acc):
        d = pl.program_id(1)
        mask, maskf = _gla_mask(d)
        lbv = lb_ref[...]

        @pl.when(d == 0)
        def _():
            o = opre_ref[...]
            r = lax.rsqrt(jnp.mean(o * o, axis=-1, keepdims=True) + RMS_EPS)
            ohat = o * r
            g = g_ref[...]
            sg = _sigmoid(g)
            silu = g * sg
            dout = dohg_ref[...]
            gainv = gain_ref[...]
            dg_ref[...] = (dout * ohat * gainv * (sg + silu * (1.0 - sg))).astype(BF16)
            dy = dout * silu
            dgain_ref[...] = jnp.sum(dy * ohat, axis=0, keepdims=True)
            dn = dy * gainv
            do_scr[...] = r * (dn - ohat * jnp.mean(dn * ohat, axis=-1, keepdims=True))

        def step(i, carry):
            dst, dlb = carry
            c = jnp.where(d == 0, nck - 1 - i, i)
            rows = pl.ds(pl.multiple_of(c * HG_CHUNK, HG_CHUNK), HG_CHUNK)
            qc = q_ref[rows, :]
            ck = _gla_chunk(qc, f_ref[rows, :], lbv, mask, maskf)
            v = v_ref[rows, :]
            st_in = st_ref[c]
            doc = do_scr[rows, :]
            qt, kt, kh, a = ck["qt"], ck["kt"], ck["kh"], ck["a"]
            ebt = jnp.exp(ck["bt"])
            dv = _dot(a, doc, _TN) + _dot(kh, dst, _NT)
            da = jnp.where(mask, _dot(doc, v, _NT), 0.0)
            dqt = _dot(da, kt, _NN) + _dot(doc, st_in, _NN)
            dkt = _dot(da, qt, _TN)
            dkh = _dot(v, dst, _NN)
            dbt = ebt * jnp.sum(dst * st_in, axis=0, keepdims=True) + jnp.sum(dkh * kh, axis=0, keepdims=True)
            db = dqt * qt - dkt * kt - dkh * kh
            dq = dqt * ck["eb"]
            dk = dkt * jnp.exp(-ck["b"]) + dkh * jnp.exp(ck["bt"] - ck["b"])
            dlogf = _dot_exact(maskf, db, _TN) + dbt
            dforget = dlogf / ck["forget"] - dk
            sf = ck["sf"]
            df_ref[rows, :] = (dforget * (1.0 - lbv) * sf * (1.0 - sf)).astype(BF16)
            sq = ck["sq"]
            dqc = dq * HG_SCALE * (sq + qc * sq * (1.0 - sq))

            @pl.when(d == 0)
            def _():
                dq_acc[rows, :] = dqc
                dv_acc[rows, :] = dv

            @pl.when(d == 1)
            def _():
                dq_acc[rows, :] += dqc
                dv_acc[rows, :] += dv

            dlb = dlb + jnp.sum(dforget * (1.0 - sf), axis=0, keepdims=True)
            return dst * ebt + _dot(doc, qt, _TN), dlb

        zero_state = jnp.zeros((HEAD_DIM, HEAD_DIM), F32)
        _, dlb = lax.fori_loop(0, nck, step, (zero_state, jnp.zeros((1, HEAD_DIM), F32)))
        dlb_ref[...] = dlb

        @pl.when(d == 1)
        def _():
            dq_ref[...] = dq_acc[...].astype(BF16)
            dv_ref[...] = dv_acc[...].astype(BF16)

    blk = SEQ * HEAD_DIM * 4
    bshape = jax.ShapeDtypeStruct((SEQ, D_MODEL), BF16)
    return pl.pallas_call(
        body, name="gla_bwd", grid=(HG_HEADS, 2),
        in_specs=[_head_spec(CB_HG_Q),
                  pl.BlockSpec((SEQ, HEAD_DIM), lambda h, d: (0, CB_F + 8 * d + h)),
                  _head_spec(CB_HG_I), _head_spec(CB_HG_G),
                  pl.BlockSpec((None, None, 1, HEAD_DIM), lambda h, d: (d, h, 0, 0)),
                  pl.BlockSpec((1, HEAD_DIM), lambda h, d: (0, 0)),
                  _head_spec(0), _head_spec(0),
                  pl.BlockSpec((None, None, nck, HEAD_DIM, HEAD_DIM), lambda h, d: (h, d, 0, 0, 0))],
        out_specs=[_head_spec(0),
                   pl.BlockSpec((None, SEQ, HEAD_DIM), lambda h, d: (d, 0, h)),
                   _head_spec(0), _head_spec(0),
                   pl.BlockSpec((None, None, 1, HEAD_DIM), lambda h, d: (d, h, 0, 0)),
                   pl.BlockSpec((None, 1, HEAD_DIM), lambda h, d: (h, 0, 0))],
        out_shape=[bshape, jax.ShapeDtypeStruct((2, SEQ, D_MODEL), BF16), bshape, bshape,
                   jax.ShapeDtypeStruct((2, HG_HEADS, 1, HEAD_DIM), F32),
                   jax.ShapeDtypeStruct((HG_HEADS, 1, HEAD_DIM), F32)],
        scratch_shapes=[pltpu.VMEM((SEQ, HEAD_DIM), F32)] * 3,
        compiler_params=_params(("parallel", "arbitrary"), 12 * blk + 4 * blk + 4 * blk + 3 * blk),
    )(proj, proj, proj, proj, lb, gain, o_pre, do_hg, states)


def _da_rmsnorm(x, gain):
    r = lax.rsqrt(jnp.mean(x * x, axis=-1, keepdims=True) + RMS_EPS)
    return x * r, r, x * r * gain


def _da_scores(qn_scr, kn_scr, slope, i, ld):
    w0 = jnp.clip(i * DA_QB - DA_RADIUS, 0, ld - DA_WIN)
    w0 = pl.multiple_of(w0, DA_RADIUS)
    qrows = pl.ds(pl.multiple_of(i * DA_QB, DA_QB), DA_QB)
    win = pl.ds(w0, DA_WIN)
    qb = qn_scr[qrows, :]
    kw = kn_scr[win, :]
    s = _dot(qb, kw, _NT) * DA_SCALE
    qpos = i * DA_QB + lax.broadcasted_iota(jnp.int32, (DA_QB, DA_WIN), 0)
    kpos = w0 + lax.broadcasted_iota(jnp.int32, (DA_QB, DA_WIN), 1)
    arel = jnp.abs(kpos - qpos)
    s = s - slope * arel.astype(F32)
    s = jnp.where(arel <= DA_RADIUS, s, NEG_INF)
    return s, qb, kw, qrows, win


def _da_slopes(group):
    d = DA_DILATIONS[group]
    sl = _alibi_slopes(DA_HEADS)[4 * group:4 * group + 4] * d
    return jnp.asarray(np.broadcast_to(sl[:, None, None], (4, 1, LANE)).copy())


def _da_fwd(proj, gq, gk, group):
    d = DA_DILATIONS[group]
    ld = SEQ // d
    nqb = ld // DA_QB
    pv = proj.reshape(ld, d * IN_COLS)

    def col(base):
        return lambda h, r: (0, r * N_CB + base + 4 * group + h)

    def body(q_ref, k_ref, v_ref, gq_ref, gk_ref, sl_ref, o_ref, lse_ref, qn_scr, kn_scr):
        qn_scr[...] = _da_rmsnorm(q_ref[...], gq_ref[...])[2].astype(BF16)
        kn_scr[...] = _da_rmsnorm(k_ref[...], gk_ref[...])[2].astype(BF16)
        slope = sl_ref[:, 0:1]

        def step(i, carry):
            s, _, _, qrows, win = _da_scores(qn_scr, kn_scr, slope, i, ld)
            m = jnp.max(s, axis=-1, keepdims=True)
            p = jnp.exp(s - m)
            l = jnp.sum(p, axis=-1, keepdims=True)
            o_ref[qrows, :] = _dot(p, v_ref[win, :], _NN) / l
            lse_ref[qrows, :] = jnp.broadcast_to(m + jnp.log(l), (DA_QB, HEAD_DIM))
            return carry

        lax.fori_loop(0, nqb, step, 0)

    seq_spec = lambda base: pl.BlockSpec((ld, HEAD_DIM), col(base))
    out_spec = pl.BlockSpec((ld, HEAD_DIM), lambda h, r: (0, r * 4 + h))
    gain_spec = pl.BlockSpec((1, HEAD_DIM), lambda h, r: (0, 0))
    o, lse = pl.pallas_call(
        body, name=f"da_fwd_g{group}", grid=(DA_HEADS_PER_GROUP, d),
        in_specs=[seq_spec(CB_DA_Q), seq_spec(CB_DA_K), seq_spec(CB_DA_V), gain_spec, gain_spec,
                  pl.BlockSpec((None, 1, LANE), lambda h, r: (h, 0, 0))],
        out_specs=[out_spec, out_spec],
        out_shape=[jax.ShapeDtypeStruct((ld, d * DA_WIDTH), F32)] * 2,
        scratch_shapes=[pltpu.VMEM((ld, HEAD_DIM), BF16)] * 2,
        compiler_params=_params(("parallel", "parallel"), 12 * ld * HEAD_DIM * 4),
    )(pv, pv, pv, gq, gk, _da_slopes(group))
    return o.reshape(SEQ, DA_WIDTH), lse.reshape(SEQ, DA_WIDTH)


def _da_merge(os, lses, tr):
    w = DA_WIDTH

    def body(o0, o1, o2, l0, l1, l2, ob_ref, of_ref, lse_ref):
        la, lb_, lc = l0[...], l1[...], l2[...]
        m = jnp.maximum(jnp.maximum(la, lb_), lc)
        ea, eb, ec = jnp.exp(la - m), jnp.exp(lb_ - m), jnp.exp(lc - m)
        tot = ea + eb + ec
        o = (ea * o0[...] + eb * o1[...] + ec * o2[...]) / tot
        of_ref[...] = o
        ob_ref[...] = o.astype(BF16)
        lse_ref[...] = m + jnp.log(tot)

    return pl.pallas_call(
        body, name="da_merge", grid=(SEQ // tr,), in_specs=[_row_spec(tr, w)] * 6,
        out_specs=[_row_spec(tr, w)] * 3,
        out_shape=[jax.ShapeDtypeStruct((SEQ, w), BF16), jax.ShapeDtypeStruct((SEQ, w), F32),
                   jax.ShapeDtypeStruct((SEQ, w), F32)],
        compiler_params=_params(("parallel",), 24 * tr * w * 4),
    )(*os, *lses)


def _da_rowdot(do, o, tr):
    w = DA_WIDTH

    def body(do_ref, o_ref, out_ref):
        prod = do_ref[...] * o_ref[...]
        for h in range(w // HEAD_DIM):
            sl = slice(h * HEAD_DIM, (h + 1) * HEAD_DIM)
            out_ref[:, sl] = jnp.broadcast_to(jnp.sum(prod[:, sl], axis=-1, keepdims=True), (tr, HEAD_DIM))

    return pl.pallas_call(
        body, name="da_rowdot", grid=(SEQ // tr,), in_specs=[_row_spec(tr, w)] * 2,
        out_specs=_row_spec(tr, w), out_shape=jax.ShapeDtypeStruct((SEQ, w), F32),
        compiler_params=_params(("parallel",), 10 * tr * w * 4),
    )(do, o)


def _da_bwd(proj, gq, gk, do, lse, dd, group):
    d = DA_DILATIONS[group]
    ld = SEQ // d
    nqb = ld // DA_QB
    pv = proj.reshape(ld, d * IN_COLS)
    view = lambda t: t.reshape(ld, d * DA_WIDTH)

    def col(base):
        return lambda h, r: (0, r * N_CB + base + 4 * group + h)

    def body(q_ref, k_ref, v_ref, gq_ref, gk_ref, sl_ref, do_ref, lse_ref, dd_ref,
             dq_ref, dk_ref, dv_ref, dgq_ref, dgk_ref, qn_scr, kn_scr, dqn_scr, dkn_scr, dv_scr):
        gqv, gkv = gq_ref[...], gk_ref[...]
        qn_scr[...] = _da_rmsnorm(q_ref[...], gqv)[2].astype(BF16)
        kn_scr[...] = _da_rmsnorm(k_ref[...], gkv)[2].astype(BF16)
        dkn_scr[...] = jnp.zeros_like(dkn_scr)
        dv_scr[...] = jnp.zeros_like(dv_scr)
        slope = sl_ref[:, 0:1]

        def step(i, carry):
            s, qb, kw, qrows, win = _da_scores(qn_scr, kn_scr, slope, i, ld)
            p = jnp.exp(s - lse_ref[qrows, :][:, 0:1])
            dob = do_ref[qrows, :]
            dv_scr[win, :] += _dot(p, dob, _TN)
            dp = _dot(dob, v_ref[win, :], _NT)
            ds = p * (dp - dd_ref[qrows, :][:, 0:1]) * DA_SCALE
            dqn_scr[qrows, :] = _dot(ds, kw, _NN)
            dkn_scr[win, :] += _dot(ds, qb, _TN)
            return carry

        lax.fori_loop(0, nqb, step, 0)

        first = jnp.logical_and(pl.program_id(0) == 0, pl.program_id(1) == 0)
        for x_ref, gv, dn_scr, dx_ref, dgain_ref in ((q_ref, gqv, dqn_scr, dq_ref, dgq_ref),
                                                     (k_ref, gkv, dkn_scr, dk_ref, dgk_ref)):
            hat, r, _ = _da_rmsnorm(x_ref[...], gv)
            dn = dn_scr[...]
            dyg = dn * gv
            dx_ref[...] = (r * (dyg - hat * jnp.mean(dyg * hat, axis=-1, keepdims=True))).astype(BF16)
            part = jnp.sum(dn * hat, axis=0, keepdims=True)

            @pl.when(first)
            def _():
                dgain_ref[...] = part

            @pl.when(jnp.logical_not(first))
            def _():
                dgain_ref[...] += part

        dv_ref[...] = dv_scr[...].astype(BF16)

    seq_spec = lambda base: pl.BlockSpec((ld, HEAD_DIM), col(base))
    out_spec = pl.BlockSpec((ld, HEAD_DIM), lambda h, r: (0, r * 4 + h))
    gain_spec = pl.BlockSpec((1, HEAD_DIM), lambda h, r: (0, 0))
    oshape = jax.ShapeDtypeStruct((ld, d * DA_WIDTH), BF16)
    gshape = jax.ShapeDtypeStruct((1, HEAD_DIM), F32)
    dq, dk, dv, dgq, dgk = pl.pallas_call(
        body, name=f"da_bwd_g{group}", grid=(DA_HEADS_PER_GROUP, d),
        in_specs=[seq_spec(CB_DA_Q), seq_spec(CB_DA_K), seq_spec(CB_DA_V), gain_spec, gain_spec,
                  pl.BlockSpec((None, 1, LANE), lambda h, r: (h, 0, 0)), out_spec, out_spec, out_spec],
        out_specs=[out_spec, out_spec, out_spec, gain_spec, gain_spec],
        out_shape=[oshape, oshape, oshape, gshape, gshape],
        scratch_shapes=[pltpu.VMEM((ld, HEAD_DIM), BF16)] * 2 + [pltpu.VMEM((ld, HEAD_DIM), F32)] * 3,
        compiler_params=_params(("arbitrary", "arbitrary"), 24 * ld * HEAD_DIM * 4),
    )(pv, pv, pv, gq, gk, _da_slopes(group), view(do), view(lse), view(dd))
    return dq.reshape(SEQ, DA_WIDTH), dk.reshape(SEQ, DA_WIDTH), dv.reshape(SEQ, DA_WIDTH), dgq, dgk


def _mem_softmax(q, k, gq, gk):
    qhat, rq, qn = _da_rmsnorm(q, gq)
    khat, rk, kn = _da_rmsnorm(k, gk)
    s = _dot(qn, kn, _NT) * MEM_SCALE
    m = jnp.max(s, axis=-1, keepdims=True)
    e = jnp.exp(s - m)
    p = e / jnp.sum(e, axis=-1, keepdims=True)
    return p, (qhat, rq, qn), (khat, rk, kn)


def _mem_fwd(proj, kv, gq, gk, tq):
    def body(q_ref, k_ref, v_ref, gq_ref, gk_ref, o_ref):
        p, _, _ = _mem_softmax(q_ref[...], k_ref[...], gq_ref[...], gk_ref[...])
        o_ref[...] = _dot(p, v_ref[...], _NN).astype(BF16)

    gain_spec = pl.BlockSpec((1, HEAD_DIM), lambda h, i: (0, 0))
    return pl.pallas_call(
        body, name="mem_fwd", grid=(MEM_HEADS, SEQ // tq),
        in_specs=[pl.BlockSpec((tq, HEAD_DIM), lambda h, i: (i, CB_MEM_Q + h)),
                  pl.BlockSpec((N_MEM, HEAD_DIM), lambda h, i: (0, h)),
                  pl.BlockSpec((N_MEM, HEAD_DIM), lambda h, i: (0, MEM_HEADS + h)), gain_spec, gain_spec],
        out_specs=pl.BlockSpec((tq, HEAD_DIM), lambda h, i: (i, h)),
        out_shape=jax.ShapeDtypeStruct((SEQ, MEM_WIDTH), BF16),
        compiler_params=_params(("parallel", "parallel"), 16 * tq * N_MEM * 4),
    )(proj, kv, kv, gq, gk)


def _mem_bwd(proj, kv, gq, gk, do, tq):
    nq = SEQ // tq

    def body(q_ref, k_ref, v_ref, gq_ref, gk_ref, do_ref, dq_ref, dk_ref, dv_ref, dgq_ref, dgk_ref, dkn_scr):
        h, i = pl.program_id(0), pl.program_id(1)
        gqv, gkv = gq_ref[...], gk_ref[...]
        p, (qhat, rq, qn), (khat, rk, kn) = _mem_softmax(q_ref[...], k_ref[...], gqv, gkv)
        dob = do_ref[...]
        dvp = _dot(p, dob, _TN)
        dp = _dot(dob, v_ref[...], _NT)
        ds = p * (dp - jnp.sum(p * dp, axis=-1, keepdims=True)) * MEM_SCALE
        dqn = _dot(ds, kn, _NN)
        dknp = _dot(ds, qn, _TN)
        dyg = dqn * gqv
        dq_ref[...] = (rq * (dyg - qhat * jnp.mean(dyg * qhat, axis=-1, keepdims=True))).astype(BF16)
        dgq_part = jnp.sum(dqn * qhat, axis=0, keepdims=True)
        first = jnp.logical_and(h == 0, i == 0)

        @pl.when(first)
        def _():
            dgq_ref[...] = dgq_part

        @pl.when(jnp.logical_not(first))
        def _():
            dgq_ref[...] += dgq_part

        @pl.when(i == 0)
        def _():
            dv_ref[...] = dvp
            dkn_scr[...] = dknp

        @pl.when(i > 0)
        def _():
            dv_ref[...] += dvp
            dkn_scr[...] += dknp

        @pl.when(i == nq - 1)
        def _():
            dkn = dkn_scr[...]
            dkg = dkn * gkv
            dk_ref[...] = rk * (dkg - khat * jnp.mean(dkg * khat, axis=-1, keepdims=True))
            dgk_part = jnp.sum(dkn * khat, axis=0, keepdims=True)

            @pl.when(h == 0)
            def _():
                dgk_ref[...] = dgk_part

            @pl.when(h > 0)
            def _():
                dgk_ref[...] += dgk_part

    gain_spec = pl.BlockSpec((1, HEAD_DIM), lambda h, i: (0, 0))
    kvout = pl.BlockSpec((N_MEM, HEAD_DIM), lambda h, i: (0, h))
    return pl.pallas_call(
        body, name="mem_bwd", grid=(MEM_HEADS, nq),
        in_specs=[pl.BlockSpec((tq, HEAD_DIM), lambda h, i: (i, CB_MEM_Q + h)),
                  pl.BlockSpec((N_MEM, HEAD_DIM), lambda h, i: (0, h)),
                  pl.BlockSpec((N_MEM, HEAD_DIM), lambda h, i: (0, MEM_HEADS + h)), gain_spec, gain_spec,
                  pl.BlockSpec((tq, HEAD_DIM), lambda h, i: (i, h))],
        out_specs=[pl.BlockSpec((tq, HEAD_DIM), lambda h, i: (i, h)), kvout, kvout, gain_spec, gain_spec],
        out_shape=[jax.ShapeDtypeStruct((SEQ, MEM_WIDTH), BF16), jax.ShapeDtypeStruct((N_MEM, MEM_WIDTH), F32),
                   jax.ShapeDtypeStruct((N_MEM, MEM_WIDTH), F32), jax.ShapeDtypeStruct((1, HEAD_DIM), F32),
                   jax.ShapeDtypeStruct((1, HEAD_DIM), F32)],
        scratch_shapes=[pltpu.VMEM((N_MEM, HEAD_DIM), F32)],
        compiler_params=_params(("arbitrary", "arbitrary"), 24 * tq * N_MEM * 4),
    )(proj, kv, kv, gq, gk, do)


def _mesh_position():
    return lax.axis_index("x"), lax.axis_index("y"), lax.axis_index("c")


def _any_spec():
    return pl.BlockSpec(memory_space=pl.ANY)


def _all_gather(shards):
    n = len(shards)

    def body(*refs):
        ins, outs = refs[:n], refs[n:2 * n]
        send_sems, recv_sems, local_sems = refs[2 * n:]
        x, y, c = _mesh_position()
        me, sibling = (x, y, c), (x, y, 1 - c)
        chips = [(1 - x, y), (x, 1 - y), (1 - x, 1 - y)]

        def copy(w, k, block, to, src=None):
            px, py, pc = block
            rows = outs[w].at[4 * px + 2 * py + pc]
            return pltpu.make_async_remote_copy(
                src_ref=rows if src is None else src, dst_ref=rows,
                send_sem=send_sems.at[7 * w + k], recv_sem=recv_sems.at[7 * w + k],
                device_id=to, device_id_type=MESH)

        started = []
        for w in range(n):
            mine = pltpu.make_async_copy(ins[w], outs[w].at[4 * x + 2 * y + c], local_sems.at[w])
            mine.start()
            started.append(mine)
        sends = []
        for w in range(n):
            first = [copy(w, 0, me, sibling, src=ins[w])]
            first += [copy(w, 1 + j, me, (*chip, c), src=ins[w]) for j, chip in enumerate(chips)]
            for cp in first:
                cp.start()
            sends += first
        for w in range(n):
            for j, chip in enumerate(chips):
                copy(w, 1 + j, (*chip, c), me).wait_recv()
                passed = copy(w, 4 + j, (*chip, c), sibling)
                passed.start()
                sends.append(passed)
        for w in range(n):
            copy(w, 0, sibling, me).wait_recv()
            for j, chip in enumerate(chips):
                copy(w, 4 + j, (*chip, 1 - c), me).wait_recv()
        for cp in sends:
            cp.wait_send()
        for mine in started:
            mine.wait()

    return pl.pallas_call(
        body, name="weights_all_gather",
        in_specs=[_any_spec()] * n, out_specs=[_any_spec()] * n,
        out_shape=[jax.ShapeDtypeStruct((N_DEV,) + s.shape, s.dtype) for s in shards],
        scratch_shapes=[pltpu.SemaphoreType.DMA((7 * n,)), pltpu.SemaphoreType.DMA((7 * n,)),
                        pltpu.SemaphoreType.DMA((n,))],
    )(*shards)


def _chip_of(j, x, y):
    return (1 - x if j & 1 else x, 1 - y if j & 2 else y)


def _sibling_exchange(grads):
    n = len(grads)

    def body(*refs):
        ins, outs = refs[:n], refs[n:2 * n]
        send_sems, recv_sems = refs[2 * n:]
        x, y, c = _mesh_position()
        copies = []
        for w in range(n):
            for j in range(4):
                px, py = _chip_of(j, x, y)
                cp = pltpu.make_async_remote_copy(
                    src_ref=ins[w].at[4 * px + 2 * py + (1 - c)], dst_ref=outs[w].at[j],
                    send_sem=send_sems.at[4 * w + j], recv_sem=recv_sems.at[4 * w + j],
                    device_id=(x, y, 1 - c), device_id_type=MESH)
                cp.start()
                copies.append(cp)
        for cp in copies:
            cp.wait()

    return pl.pallas_call(
        body, name="grads_sibling_exchange",
        in_specs=[_any_spec()] * n, out_specs=[_any_spec()] * n,
        out_shape=[jax.ShapeDtypeStruct((4,) + g.shape[1:], g.dtype) for g in grads],
        scratch_shapes=[pltpu.SemaphoreType.DMA((4 * n,)), pltpu.SemaphoreType.DMA((4 * n,))],
    )(*grads)


def _chip_partials(grad, recv, name, tr):
    _, rows, width = grad.shape

    def body(g_ref, r_ref, own_ref, other_ref):
        x, y, c = _mesh_position()
        for j in range(4):
            px, py = _chip_of(j, x, y)
            total = g_ref[4 * px + 2 * py + c].astype(F32) + r_ref[j].astype(F32)
            if j == 0:
                own_ref[...] = total
            else:
                other_ref[j - 1] = total.astype(BF16)

    return pl.pallas_call(
        body, name=name, grid=(rows // tr,),
        in_specs=[pl.BlockSpec((N_DEV, tr, width), lambda i: (0, i, 0)),
                  pl.BlockSpec((4, tr, width), lambda i: (0, i, 0))],
        out_specs=[pl.BlockSpec((tr, width), lambda i: (i, 0)), pl.BlockSpec((3, tr, width), lambda i: (0, i, 0))],
        out_shape=[jax.ShapeDtypeStruct((rows, width), F32), jax.ShapeDtypeStruct((3, rows, width), BF16)],
        compiler_params=_params(("parallel",), 2 * 20 * tr * width * 2 + 8 * tr * width * 4),
    )(grad, recv)


def _chip_exchange(partials):
    n = len(partials)

    def body(*refs):
        ins, outs = refs[:n], refs[n:2 * n]
        send_sems, recv_sems = refs[2 * n:]
        x, y, c = _mesh_position()
        copies = []
        for w in range(n):
            for j in range(1, 4):
                px, py = _chip_of(j, x, y)
                cp = pltpu.make_async_remote_copy(
                    src_ref=ins[w].at[j - 1], dst_ref=outs[w].at[j - 1],
                    send_sem=send_sems.at[3 * w + j - 1], recv_sem=recv_sems.at[3 * w + j - 1],
                    device_id=(px, py, c), device_id_type=MESH)
                cp.start()
                copies.append(cp)
        for cp in copies:
            cp.wait()

    return pl.pallas_call(
        body, name="grads_chip_exchange",
        in_specs=[_any_spec()] * n, out_specs=[_any_spec()] * n,
        out_shape=[jax.ShapeDtypeStruct(p.shape, p.dtype) for p in partials],
        scratch_shapes=[pltpu.SemaphoreType.DMA((3 * n,)), pltpu.SemaphoreType.DMA((3 * n,))],
    )(*partials)


def _adamw_math(w, g, m, v):
    m = ADAM_B1 * m + (1.0 - ADAM_B1) * g
    v = ADAM_B2 * v + (1.0 - ADAM_B2) * (g * g)
    m_hat = m / (1.0 - ADAM_B1 ** ADAM_STEP)
    v_hat = v / (1.0 - ADAM_B2 ** ADAM_STEP)
    delta = -ADAM_LR * (m_hat / (jnp.sqrt(v_hat) + ADAM_EPS) + ADAM_WD * w)
    return delta, m, v


def _adamw_shard(own, recv, w, m, v, name, tr):
    rows, width = own.shape

    def body(own_ref, r_ref, w_ref, m_ref, v_ref, g_ref, d_ref, nm_ref, nv_ref):
        g = own_ref[...]
        for j in range(3):
            g = g + r_ref[j].astype(F32)
        g_ref[...] = g
        d_ref[...], nm_ref[...], nv_ref[...] = _adamw_math(w_ref[...], g, m_ref[...], v_ref[...])

    spec = pl.BlockSpec((tr, width), lambda i: (i, 0))
    shape = jax.ShapeDtypeStruct((rows, width), F32)
    return pl.pallas_call(
        body, name=name, grid=(rows // tr,),
        in_specs=[spec, pl.BlockSpec((3, tr, width), lambda i: (0, i, 0)), spec, spec, spec],
        out_specs=[spec] * 4, out_shape=[shape] * 4,
        compiler_params=_params(("parallel",), 22 * tr * width * 4),
    )(own, recv, w, m, v)


def _small_all_reduce_adamw(gpart, lbpack, wpack, mpack, vpack):
    def body(gp_ref, lb_ref, w_ref, m_ref, v_ref, g_ref, d_ref, nm_ref, nv_ref, gath_ref, send_sems, recv_sems):
        x, y, c = _mesh_position()
        me = 4 * x + 2 * y + c
        gath_ref[me] = gp_ref[...]
        copies = []
        for j in range(1, N_DEV):
            peer = (x ^ (j >> 2), y ^ ((j >> 1) & 1), c ^ (j & 1))
            cp = pltpu.make_async_remote_copy(
                src_ref=gp_ref, dst_ref=gath_ref.at[me], send_sem=send_sems.at[j - 1], recv_sem=recv_sems.at[j - 1],
                device_id=peer, device_id_type=MESH)
            cp.start()
            copies.append(cp)
        for cp in copies:
            cp.wait()
        tot = gath_ref[0]
        for s in range(1, N_DEV):
            tot = tot + gath_ref[s]
        lb = lb_ref[...]
        dl = tot[16:32, :] * lb * (1.0 - lb)
        g = jnp.concatenate([tot[0:16, :], dl[0:8, :], -dl[0:8, :], dl[8:16, :], -dl[8:16, :],
                             tot[32:40, :], tot[40:48, :]], axis=0)
        g_ref[...] = g
        d_ref[...], nm_ref[...], nv_ref[...] = _adamw_math(w_ref[...], g, m_ref[...], v_ref[...])

    vm = pl.BlockSpec(memory_space=pltpu.VMEM)
    shape = jax.ShapeDtypeStruct((SMALL_ROWS, LANE), F32)
    return pl.pallas_call(
        body, name="small_all_reduce_adamw", in_specs=[vm] * 5, out_specs=[vm] * 4, out_shape=[shape] * 4,
        scratch_shapes=[pltpu.VMEM((N_DEV, SMALL_ROWS, LANE), F32),
                        pltpu.SemaphoreType.DMA((N_DEV - 1,)), pltpu.SemaphoreType.DMA((N_DEV - 1,))],
    )(gpart, lbpack, wpack, mpack, vpack)


_SMALL_NAMES = ("norm_mix_gain", "norm_mem_gain", "lb_logits_fw", "lb_logits_bw", "norm_ffn_gain",
                "hg_norm_gain", "da_q_gain", "da_k_gain", "mem_q_gain", "mem_k_gain")
_SMALL_ROW0 = {"norm_mix_gain": 0, "norm_mem_gain": 8, "lb_logits_fw": 16, "lb_logits_bw": 32, "norm_ffn_gain": 48,
               "hg_norm_gain": 56, "da_q_gain": 57, "da_k_gain": 58, "mem_q_gain": 59, "mem_k_gain": 60}
_LOSS_ROW = 61


def _pack_rows(parts, total_rows):
    rows = [p.reshape(-1, LANE) for p in parts]
    used = sum(r.shape[0] for r in rows)
    rows.append(jnp.zeros((total_rows - used, LANE), F32))
    return jnp.concatenate(rows, axis=0)


def _unpack_small(pack, like):
    out = {}
    for name in _SMALL_NAMES:
        n = like[name].size // LANE
        r0 = _SMALL_ROW0[name]
        out[name] = pack[r0:r0 + n].reshape(like[name].shape)
    return out


def kernel(x, mem, norm_mix_gain, norm_mem_gain, w_in, lb_logits_fw, lb_logits_bw, hg_norm_gain, da_q_gain, da_k_gain, w_mem_kv, mem_q_gain, mem_k_gain, w_proj_hg, w_proj_da, w_proj_mem, w_out, norm_ffn_gain, w_ffn_in, w_ffn_out, loss_target, m_norm_mix_gain, m_norm_mem_gain, m_w_in, m_lb_logits_fw, m_lb_logits_bw, m_hg_norm_gain, m_da_q_gain, m_da_k_gain, m_w_mem_kv, m_mem_q_gain, m_mem_k_gain, m_w_proj_hg, m_w_proj_da, m_w_proj_mem, m_w_out, m_norm_ffn_gain, m_w_ffn_in, m_w_ffn_out, v_norm_mix_gain, v_norm_mem_gain, v_w_in, v_lb_logits_fw, v_lb_logits_bw, v_hg_norm_gain, v_da_q_gain, v_da_k_gain, v_w_mem_kv, v_mem_q_gain, v_mem_k_gain, v_w_proj_hg, v_w_proj_da, v_w_proj_mem, v_w_out, v_norm_ffn_gain, v_w_ffn_in, v_w_ffn_out):
    small_w = dict(norm_mix_gain=norm_mix_gain, norm_mem_gain=norm_mem_gain, lb_logits_fw=lb_logits_fw,
                   lb_logits_bw=lb_logits_bw, norm_ffn_gain=norm_ffn_gain, hg_norm_gain=hg_norm_gain,
                   da_q_gain=da_q_gain, da_k_gain=da_k_gain, mem_q_gain=mem_q_gain, mem_k_gain=mem_k_gain)
    small_m = dict(norm_mix_gain=m_norm_mix_gain, norm_mem_gain=m_norm_mem_gain, lb_logits_fw=m_lb_logits_fw,
                   lb_logits_bw=m_lb_logits_bw, norm_ffn_gain=m_norm_ffn_gain, hg_norm_gain=m_hg_norm_gain,
                   da_q_gain=m_da_q_gain, da_k_gain=m_da_k_gain, mem_q_gain=m_mem_q_gain, mem_k_gain=m_mem_k_gain)
    small_v = dict(norm_mix_gain=v_norm_mix_gain, norm_mem_gain=v_norm_mem_gain, lb_logits_fw=v_lb_logits_fw,
                   lb_logits_bw=v_lb_logits_bw, norm_ffn_gain=v_norm_ffn_gain, hg_norm_gain=v_hg_norm_gain,
                   da_q_gain=v_da_q_gain, da_k_gain=v_da_k_gain, mem_q_gain=v_mem_q_gain, mem_k_gain=v_mem_k_gain)
    big_names = ("w_in", "w_mem_kv", "w_proj_hg", "w_proj_da", "w_proj_mem", "w_out", "w_ffn_in", "w_ffn_out")
    big_w = dict(w_in=w_in[0], w_mem_kv=w_mem_kv[0], w_proj_hg=w_proj_hg[0], w_proj_da=w_proj_da[0],
                 w_proj_mem=w_proj_mem[0], w_out=w_out[0], w_ffn_in=w_ffn_in[0], w_ffn_out=w_ffn_out[0])
    big_m = dict(w_in=m_w_in[0], w_mem_kv=m_w_mem_kv[0], w_proj_hg=m_w_proj_hg[0], w_proj_da=m_w_proj_da[0],
                 w_proj_mem=m_w_proj_mem[0], w_out=m_w_out[0], w_ffn_in=m_w_ffn_in[0], w_ffn_out=m_w_ffn_out[0])
    big_v = dict(w_in=v_w_in[0], w_mem_kv=v_w_mem_kv[0], w_proj_hg=v_w_proj_hg[0], w_proj_da=v_w_proj_da[0],
                 w_proj_mem=v_w_proj_mem[0], w_out=v_w_out[0], w_ffn_in=v_w_ffn_in[0], w_ffn_out=v_w_ffn_out[0])

    gathered = _all_gather([big_w[n].astype(BF16) for n in _BIG_NAMES])
    local = _local_step(x[0], mem[0], loss_target[0], small_w, dict(zip(_BIG_NAMES, gathered)))
    stacked, grad_x = local["stacked"], local["grad_x"]

    sib = _sibling_exchange([stacked[n] for n in _BIG_NAMES])
    row_tile = dict(w_in=128, w_mem_kv=128, w_proj_hg=128, w_proj_da=512, w_proj_mem=512, w_out=128,
                    w_ffn_in=128, w_ffn_out=352)
    partials = [_chip_partials(stacked[n], r, "chip_partials_" + n, row_tile[n]) for n, r in zip(_BIG_NAMES, sib)]
    received = _chip_exchange([p[1] for p in partials])
    big_out = {}
    for n, p, r in zip(_BIG_NAMES, partials, received):
        big_out[n] = _adamw_shard(p[0], r, big_w[n], big_m[n], big_v[n], "adamw_" + n, row_tile[n])

    wpack = _pack_rows([small_w[n] for n in _SMALL_NAMES], SMALL_ROWS)
    mpack = _pack_rows([small_m[n] for n in _SMALL_NAMES], SMALL_ROWS)
    vpack = _pack_rows([small_v[n] for n in _SMALL_NAMES], SMALL_ROWS)
    gs, ds, ms, vs = _small_all_reduce_adamw(local["gpart"], local["lbpack"], wpack, mpack, vpack)
    loss = gs[_LOSS_ROW, 0]
    small_out = [_unpack_small(t, small_w) for t in (gs, ds, ms, vs)]

    order = ("norm_mix_gain", "norm_mem_gain", "w_in", "lb_logits_fw", "lb_logits_bw", "hg_norm_gain", "da_q_gain",
             "da_k_gain", "w_mem_kv", "mem_q_gain", "mem_k_gain", "w_proj_hg", "w_proj_da", "w_proj_mem", "w_out",
             "norm_ffn_gain", "w_ffn_in", "w_ffn_out")
    outs = [loss, grad_x[None]]
    for kind in range(4):
        for n in order:
            outs.append(big_out[n][kind][None] if n in big_out else small_out[kind][n])
    return tuple(outs)


_BIG_NAMES = ("w_in", "w_mem_kv", "w_proj_hg", "w_proj_da", "w_proj_mem", "w_out", "w_ffn_in", "w_ffn_out")


def _local_step(xs, mems, target, sw, wg):
    norm_mix_gain, norm_mem_gain, norm_ffn_gain = sw["norm_mix_gain"], sw["norm_mem_gain"], sw["norm_ffn_gain"]
    lb_logits_fw, lb_logits_bw, hg_norm_gain = sw["lb_logits_fw"], sw["lb_logits_bw"], sw["hg_norm_gain"]
    da_q_gain, da_k_gain, mem_q_gain, mem_k_gain = sw["da_q_gain"], sw["da_k_gain"], sw["mem_q_gain"], sw["mem_k_gain"]
    win_st = wg["w_in"]
    wkv = wg["w_mem_kv"].reshape(D_MODEL, 2 * MEM_WIDTH)
    wphg = wg["w_proj_hg"].reshape(D_MODEL, D_MODEL)
    wpda = jnp.transpose(wg["w_proj_da"], (1, 0, 2)).reshape(DA_WIDTH, D_MODEL)
    wpmem = jnp.transpose(wg["w_proj_mem"], (1, 0, 2)).reshape(MEM_WIDTH, D_MODEL)
    wout = wg["w_out"].reshape(D_MODEL, D_MODEL)
    wfin = jnp.transpose(wg["w_ffn_in"], (1, 0, 2)).reshape(D_MODEL, 2 * D_FF)
    wfout = wg["w_ffn_out"].reshape(D_FF, D_MODEL)

    lb_fw = _lb_table(lb_logits_fw, "lb_table_fw")
    lb_bw = _lb_table(lb_logits_bw, "lb_table_bw")
    lb = jnp.concatenate([lb_fw, lb_bw], axis=0).reshape(2, HG_HEADS, 1, HEAD_DIM)
    h = _rmsnorm_fwd(xs, norm_mix_gain, "norm_mix_fwd", 512)
    proj = _matmul(h, win_st, "nn", F32, 512, IN_SHARD, D_MODEL, "proj_fwd", b_stacked=True, n_outer=True)
    o_hg, o_pre, states = _gla_fwd(proj, lb, hg_norm_gain)
    da = [_da_fwd(proj, da_q_gain, da_k_gain, g) for g in range(3)]
    o_da, o_da32, lse_da = _da_merge([t[0] for t in da], [t[1] for t in da], 512)
    mem_n = _rmsnorm_fwd(mems, norm_mem_gain, "norm_mem_fwd", N_MEM)
    kv = _matmul(mem_n, wkv, "nn", F32, N_MEM, 1024, D_MODEL, "mem_kv_fwd")
    o_mem = _mem_fwd(proj, kv, mem_q_gain, mem_k_gain, 512)
    t_hg = _matmul(o_hg, wphg, "nn", F32, 512, 1024, D_MODEL, "proj_hg_fwd")
    t_da = _matmul(o_da, wpda, "nn", F32, 512, 1024, DA_WIDTH, "proj_da_fwd")
    t_mem = _matmul(o_mem, wpmem, "nn", F32, 512, 1024, MEM_WIDTH, "proj_mem_fwd")
    merged = _gate_merge_fwd(proj, t_hg, t_da, t_mem, 256)
    u = _matmul(merged, wout, "nn", F32, 512, 1024, D_MODEL, "out_fwd")
    x1, h2 = _residual_rmsnorm_fwd(xs, u, norm_ffn_gain, "norm_ffn_fwd", 512)
    ab = _matmul(h2, wfin, "nn", F32, 512, 1408, D_MODEL, "ffn_in_fwd")
    act = _swiglu_fwd(ab, 256)
    z = _matmul(act, wfout, "nn", F32, 512, 1024, D_FF, "ffn_out_fwd")
    dy, dyb, loss_part = _loss_head(x1, z, target, 512)

    dact = _matmul(dyb, wfout, "nt", F32, 512, 1408, D_MODEL, "ffn_out_bwd_act")
    g_wfout = _matmul(act, dyb, "tn", BF16, 1408, 1024, 512, "ffn_out_bwd_w")
    dab = _swiglu_bwd(ab, dact, 256)
    dh2 = _matmul(dab, wfin, "nt", F32, 512, 1024, 1408, "ffn_in_bwd_act")
    g_wfin = _matmul(h2, dab, "tn", BF16, 1024, 1408, 512, "ffn_in_bwd_w")
    dx1, dx1b, g_norm_ffn = _rmsnorm_bwd(x1, dh2, dy, norm_ffn_gain, "norm_ffn_bwd", 256)
    dmerged = _matmul(dx1b, wout, "nt", F32, 512, 1024, D_MODEL, "out_bwd_act")
    g_wout = _matmul(merged, dx1b, "tn", BF16, 1024, 1024, 512, "out_bwd_w")
    dt_hg, dt_da, dt_mem, dgates = _gate_merge_bwd(proj, t_hg, t_da, t_mem, dmerged, 256)
    do_hg = _matmul(dt_hg, wphg, "nt", F32, 512, 1024, D_MODEL, "proj_hg_bwd_act")
    g_wphg = _matmul(o_hg, dt_hg, "tn", BF16, 1024, 1024, 512, "proj_hg_bwd_w")
    do_da = _matmul(dt_da, wpda, "nt", F32, 512, DA_WIDTH, D_MODEL, "proj_da_bwd_act")
    g_wpda = _matmul(o_da, dt_da, "tn", BF16, DA_WIDTH, LANE, 512, "proj_da_bwd_w", out_stacked=True)
    do_mem = _matmul(dt_mem, wpmem, "nt", F32, 512, MEM_WIDTH, D_MODEL, "proj_mem_bwd_act")
    g_wpmem = _matmul(o_mem, dt_mem, "tn", BF16, MEM_WIDTH, LANE, 512, "proj_mem_bwd_w", out_stacked=True)

    dmemq, dk_mem, dv_mem, g_mem_q, g_mem_k = _mem_bwd(proj, kv, mem_q_gain, mem_k_gain, do_mem, 512)
    dkv = jnp.concatenate([dk_mem, dv_mem], axis=1).astype(BF16)
    g_wkv = _matmul(mem_n, dkv, "tn", BF16, 1024, 1024, N_MEM, "mem_kv_bwd_w")
    dmem_n = _matmul(dkv, wkv, "nt", F32, N_MEM, 1024, 1024, "mem_kv_bwd_act")
    g_norm_mem = _gain_grad(mems, dmem_n, "norm_mem_bwd")

    dd_da = _da_rowdot(do_da, o_da32, 512)
    da_b = [_da_bwd(proj, da_q_gain, da_k_gain, do_da, lse_da, dd_da, g) for g in range(3)]
    g_da_q = da_b[0][3] + da_b[1][3] + da_b[2][3]
    g_da_k = da_b[0][4] + da_b[1][4] + da_b[2][4]

    dq_hg, df_hg, dv_hg, dg_hg, dlb, g_hg_norm = _gla_bwd(proj, lb, hg_norm_gain, o_pre, states, do_hg)
    g_hg_norm = jnp.sum(g_hg_norm, axis=0)

    dproj = jnp.concatenate(
        [dq_hg, df_hg[0], df_hg[1], dv_hg, dg_hg]
        + [t[0] for t in da_b] + [t[1] for t in da_b] + [t[2] for t in da_b] + [dmemq, dgates], axis=1)
    dh = _matmul(dproj, win_st, "nt", F32, 512, 1024, IN_SHARD, "proj_bwd_act", b_stacked=True)
    g_win = _matmul(h, dproj, "tn", BF16, 1024, IN_SHARD, 512, "proj_bwd_w", out_stacked=True)
    grad_x, _, g_norm_mix = _rmsnorm_bwd(xs, dh, dx1, norm_mix_gain, "norm_mix_bwd", 256)

    stacked = dict(
        w_in=g_win,
        w_mem_kv=g_wkv.reshape(N_DEV, D_MODEL // N_DEV, 2 * MEM_WIDTH),
        w_proj_hg=g_wphg.reshape(N_DEV, D_MODEL // N_DEV, D_MODEL),
        w_proj_da=g_wpda, w_proj_mem=g_wpmem,
        w_out=g_wout.reshape(N_DEV, D_MODEL // N_DEV, D_MODEL),
        w_ffn_in=jnp.transpose(g_wfin.reshape(D_MODEL, N_DEV, 2 * D_FF // N_DEV), (1, 0, 2)),
        w_ffn_out=g_wfout.reshape(N_DEV, D_FF // N_DEV, D_MODEL))
    gpart = _pack_rows([g_norm_mix, g_norm_mem, dlb[0], dlb[1], g_norm_ffn, g_hg_norm, g_da_q, g_da_k,
                        g_mem_q, g_mem_k, loss_part], SMALL_ROWS)
    lbpack = jnp.concatenate([lb_fw.reshape(8, LANE), lb_bw.reshape(8, LANE)], axis=0)
    return dict(grad_x=grad_x, stacked=stacked, gpart=gpart, lbpack=lbpack)
```

```python
import numpy as np
import jax
import jax.numpy as jnp
from jax import lax
from jax.experimental import pallas as pl
from jax.experimental.pallas import tpu as pltpu

F32 = jnp.float32
BF16 = jnp.bfloat16
MESH = pl.DeviceIdType.MESH

SEQ = 4096
D_MODEL = 1024
N_DEV = 8
N_MEM = 256
RMS_EPS = 1e-6
NEG_INF = -1e30
LANE = 128
HEAD_DIM = 128
HG_HEADS = 8
HG_CHUNK = 64
HG_SCALE = HEAD_DIM ** -0.5
DA_DILATIONS = (1, 4, 16)
DA_RADIUS = 64
DA_HEADS_PER_GROUP = 4
DA_HEADS = 12
DA_WIDTH = 512
DA_SCALE = HEAD_DIM ** -0.5
DA_QB = 128
DA_WIN = 256
MEM_HEADS = 4
MEM_WIDTH = 512
MEM_SCALE = HEAD_DIM ** -0.5
D_FF = 2816
IN_COLS = 13312
IN_SHARD = IN_COLS // N_DEV
CB_HG_Q, CB_F, CB_HG_I, CB_HG_G = 0, 8, 24, 32
CB_DA_Q, CB_DA_K, CB_DA_V, CB_MEM_Q = 40, 52, 64, 76
N_CB = IN_COLS // LANE
ADAM_LR, ADAM_B1, ADAM_B2, ADAM_EPS, ADAM_WD, ADAM_STEP = 0.001, 0.9, 0.999, 1e-08, 0.01, 10
VMEM_BYTES_V7X = 64 * 1024 * 1024
SMALL_ROWS = 104
SMALL_GRAD_ROWS = 88

_NN = (((1,), (0,)), ((), ()))
_NT = (((1,), (1,)), ((), ()))
_TN = (((0,), (0,)), ((), ()))


def _dot(a, b, dims):
    return lax.dot_general(a.astype(BF16), b.astype(BF16), dims, preferred_element_type=F32)


def _dot_exact(a, b, dims):
    return lax.dot_general(a, b, dims, precision=lax.Precision.HIGHEST, preferred_element_type=F32)


def _sigmoid(x):
    return 0.5 * jnp.tanh(0.5 * x) + 0.5


def _params(semantics, est_bytes):
    limit = int(min(VMEM_BYTES_V7X - (6 << 20), max(32 << 20, est_bytes * 3 // 2)))
    return pltpu.CompilerParams(dimension_semantics=semantics, vmem_limit_bytes=limit)


def _nbytes(shape, dtype):
    return int(np.prod(shape)) * jnp.dtype(dtype).itemsize


def _alibi_slopes(n):
    return (2.0 ** (-8.0 * np.arange(1, n + 1) / n)).astype(np.float32)


def _matmul(a, b, mode, out_dtype, tm, tn, tk, name, b_stacked=False, out_stacked=False, n_outer=False):
    if mode == "tn":
        kdim, m = a.shape
    else:
        m, kdim = a.shape
    if b_stacked:
        if mode == "nn":
            n = b.shape[0] * b.shape[2]
            assert tn == b.shape[2] and tk == kdim == b.shape[1]
        else:
            assert mode == "nt" and tk == b.shape[2] and b.shape[0] * tk == kdim
            n = b.shape[1]
    else:
        n = b.shape[0] if mode == "nt" else b.shape[1]
    assert m % tm == 0 and n % tn == 0 and kdim % tk == 0
    gm, gn, gk = m // tm, n // tn, kdim // tk

    def ijk(f):
        if n_outer:
            return lambda j, i, k: f(i, j, k)
        return f

    if mode == "tn":
        a_spec = pl.BlockSpec((tk, tm), ijk(lambda i, j, k: (k, i)))
    else:
        a_spec = pl.BlockSpec((tm, tk), ijk(lambda i, j, k: (i, k)))
    if b_stacked and mode == "nn":
        b_spec = pl.BlockSpec((None, tk, tn), ijk(lambda i, j, k: (j, 0, 0)))
    elif b_stacked:
        b_spec = pl.BlockSpec((None, tn, tk), ijk(lambda i, j, k: (k, j, 0)))
    elif mode == "nt":
        b_spec = pl.BlockSpec((tn, tk), ijk(lambda i, j, k: (j, k)))
    else:
        b_spec = pl.BlockSpec((tk, tn), ijk(lambda i, j, k: (k, j)))
    if out_stacked:
        assert tm == m
        out_shape = jax.ShapeDtypeStruct((gn, m, tn), out_dtype)
        o_spec = pl.BlockSpec((None, tm, tn), ijk(lambda i, j, k: (j, i, 0)))
    else:
        out_shape = jax.ShapeDtypeStruct((m, n), out_dtype)
        o_spec = pl.BlockSpec((tm, tn), ijk(lambda i, j, k: (i, j)))
    dims = {"nn": _NN, "nt": _NT, "tn": _TN}[mode]

    def body(a_ref, b_ref, o_ref, *scratch):
        part = _dot(a_ref[...], b_ref[...], dims)
        if gk == 1:
            o_ref[...] = part.astype(out_dtype)
            return
        acc_ref, = scratch
        k = pl.program_id(2)

        @pl.when(k == 0)
        def _():
            acc_ref[...] = part

        @pl.when(k > 0)
        def _():
            acc_ref[...] += part

        @pl.when(k == gk - 1)
        def _():
            o_ref[...] = acc_ref[...].astype(out_dtype)

    a_tile = _nbytes((tm, tk), a.dtype)
    b_tile = _nbytes((tk, tn), b.dtype)
    o_tile = _nbytes((tm, tn), out_dtype)
    est = 2 * (a_tile + b_tile + o_tile) + 3 * tm * tn * 4 + (a_tile + b_tile)
    grid = (gn, gm, gk) if n_outer else (gm, gn, gk)
    return pl.pallas_call(
        body, name=name, grid=grid, in_specs=[a_spec, b_spec], out_specs=o_spec, out_shape=out_shape,
        scratch_shapes=[] if gk == 1 else [pltpu.VMEM((tm, tn), F32)],
        compiler_params=_params(("parallel", "parallel", "arbitrary"), est),
    )(a, b)


def _row_spec(tr, width, col_block=0):
    return pl.BlockSpec((tr, width), lambda i: (i, col_block))


def _bcast_spec(width):
    return pl.BlockSpec((1, width), lambda i: (0, 0))


def _rmsnorm_fwd(x, gain, name, tr):
    rows, width = x.shape

    def body(x_ref, g_ref, o_ref):
        xv = x_ref[...]
        r = lax.rsqrt(jnp.mean(xv * xv, axis=-1, keepdims=True) + RMS_EPS)
        o_ref[...] = (xv * r * g_ref[...]).astype(BF16)

    return pl.pallas_call(
        body, name=name, grid=(rows // tr,), in_specs=[_row_spec(tr, width), _bcast_spec(width)],
        out_specs=_row_spec(tr, width), out_shape=jax.ShapeDtypeStruct((rows, width), BF16),
        compiler_params=_params(("parallel",), 6 * tr * width * 4),
    )(x, gain)


def _residual_rmsnorm_fwd(x, u, gain, name, tr):
    rows, width = x.shape

    def body(x_ref, u_ref, g_ref, x1_ref, h_ref):
        xv = x_ref[...] + u_ref[...]
        x1_ref[...] = xv
        r = lax.rsqrt(jnp.mean(xv * xv, axis=-1, keepdims=True) + RMS_EPS)
        h_ref[...] = (xv * r * g_ref[...]).astype(BF16)

    return pl.pallas_call(
        body, name=name, grid=(rows // tr,),
        in_specs=[_row_spec(tr, width), _row_spec(tr, width), _bcast_spec(width)],
        out_specs=[_row_spec(tr, width), _row_spec(tr, width)],
        out_shape=[jax.ShapeDtypeStruct((rows, width), F32), jax.ShapeDtypeStruct((rows, width), BF16)],
        compiler_params=_params(("parallel",), 10 * tr * width * 4),
    )(x, u, gain)


def _rmsnorm_bwd(x, dh, dres, gain, name, tr):
    rows, width = x.shape

    def body(x_ref, dh_ref, dres_ref, g_ref, dx_ref, dxb_ref, dg_ref):
        xv = x_ref[...]
        r = lax.rsqrt(jnp.mean(xv * xv, axis=-1, keepdims=True) + RMS_EPS)
        xhat = xv * r
        dhv = dh_ref[...]
        dyg = dhv * g_ref[...]
        dx = dres_ref[...] + r * (dyg - xhat * jnp.mean(dyg * xhat, axis=-1, keepdims=True))
        dx_ref[...] = dx
        dxb_ref[...] = dx.astype(BF16)
        part = jnp.sum(dhv * xhat, axis=0, keepdims=True)

        @pl.when(pl.program_id(0) == 0)
        def _():
            dg_ref[...] = part

        @pl.when(pl.program_id(0) > 0)
        def _():
            dg_ref[...] += part

    return pl.pallas_call(
        body, name=name, grid=(rows // tr,),
        in_specs=[_row_spec(tr, width), _row_spec(tr, width), _row_spec(tr, width), _bcast_spec(width)],
        out_specs=[_row_spec(tr, width), _row_spec(tr, width), _bcast_spec(width)],
        out_shape=[jax.ShapeDtypeStruct((rows, width), F32), jax.ShapeDtypeStruct((rows, width), BF16),
                   jax.ShapeDtypeStruct((1, width), F32)],
        compiler_params=_params(("arbitrary",), 14 * tr * width * 4),
    )(x, dh, dres, gain)


def _gain_grad(x, dh, name):
    rows, width = x.shape

    def body(x_ref, dh_ref, dg_ref):
        xv = x_ref[...]
        r = lax.rsqrt(jnp.mean(xv * xv, axis=-1, keepdims=True) + RMS_EPS)
        dg_ref[...] = jnp.sum(dh_ref[...] * xv * r, axis=0, keepdims=True)

    return pl.pallas_call(
        body, name=name, grid=(1,), in_specs=[_row_spec(rows, width), _row_spec(rows, width)],
        out_specs=_bcast_spec(width), out_shape=jax.ShapeDtypeStruct((1, width), F32),
        compiler_params=_params(("arbitrary",), 6 * rows * width * 4),
    )(x, dh)


def _lb_table(logits, name):
    slots, width = logits.shape

    def body(l_ref, o_ref):
        lv = l_ref[...]
        mx = jnp.max(lv, axis=0, keepdims=True)
        e = jnp.exp(lv - mx)
        o_ref[...] = e[0:1, :] / jnp.sum(e, axis=0, keepdims=True)

    return pl.pallas_call(
        body, name=name, grid=(1,), in_specs=[pl.BlockSpec((slots, width), lambda i: (0, 0))],
        out_specs=_bcast_spec(width), out_shape=jax.ShapeDtypeStruct((1, width), F32),
    )(logits)


def _gate_merge_fwd(proj, t_hg, t_da, t_mem, tr):
    w = D_MODEL

    def body(ghg_ref, gda_ref, gmem_ref, thg_ref, tda_ref, tmem_ref, o_ref):
        acc = _sigmoid(ghg_ref[...]) * thg_ref[...]
        acc += _sigmoid(gda_ref[...]) * tda_ref[...]
        acc += _sigmoid(gmem_ref[...]) * tmem_ref[...]
        o_ref[...] = acc.astype(BF16)

    return pl.pallas_call(
        body, name="gate_merge_fwd", grid=(SEQ // tr,),
        in_specs=[_row_spec(tr, w, 10), _row_spec(tr, w, 11), _row_spec(tr, w, 12),
                  _row_spec(tr, w), _row_spec(tr, w), _row_spec(tr, w)],
        out_specs=_row_spec(tr, w), out_shape=jax.ShapeDtypeStruct((SEQ, w), BF16),
        compiler_params=_params(("parallel",), 16 * tr * w * 4),
    )(proj, proj, proj, t_hg, t_da, t_mem)


def _gate_merge_bwd(proj, t_hg, t_da, t_mem, dmerged, tr):
    w = D_MODEL

    def body(ghg_ref, gda_ref, gmem_ref, thg_ref, tda_ref, tmem_ref, dm_ref, dthg_ref, dtda_ref, dtmem_ref, dg_ref):
        dm = dm_ref[...]
        for b, (g_ref, t_ref, dt_ref) in enumerate(((ghg_ref, thg_ref, dthg_ref), (gda_ref, tda_ref, dtda_ref),
                                                    (gmem_ref, tmem_ref, dtmem_ref))):
            s = _sigmoid(g_ref[...])
            dt_ref[...] = (s * dm).astype(BF16)
            dg_ref[:, b * w:(b + 1) * w] = (dm * t_ref[...] * s * (1.0 - s)).astype(BF16)

    return pl.pallas_call(
        body, name="gate_merge_bwd", grid=(SEQ // tr,),
        in_specs=[_row_spec(tr, w, 10), _row_spec(tr, w, 11), _row_spec(tr, w, 12),
                  _row_spec(tr, w), _row_spec(tr, w), _row_spec(tr, w), _row_spec(tr, w)],
        out_specs=[_row_spec(tr, w), _row_spec(tr, w), _row_spec(tr, w), _row_spec(tr, 3 * w)],
        out_shape=[jax.ShapeDtypeStruct((SEQ, w), BF16)] * 3 + [jax.ShapeDtypeStruct((SEQ, 3 * w), BF16)],
        compiler_params=_params(("parallel",), 24 * tr * w * 4),
    )(proj, proj, proj, t_hg, t_da, t_mem, dmerged)


def _swiglu_fwd(ab, tr):
    def body(ab_ref, o_ref):
        a = ab_ref[:, :D_FF]
        b = ab_ref[:, D_FF:]
        o_ref[...] = (a * _sigmoid(a) * b).astype(BF16)

    return pl.pallas_call(
        body, name="swiglu_fwd", grid=(SEQ // tr,), in_specs=[_row_spec(tr, 2 * D_FF)],
        out_specs=_row_spec(tr, D_FF), out_shape=jax.ShapeDtypeStruct((SEQ, D_FF), BF16),
        compiler_params=_params(("parallel",), 8 * tr * 2 * D_FF * 4),
    )(ab)


def _swiglu_bwd(ab, dact, tr):
    def body(ab_ref, d_ref, o_ref):
        a = ab_ref[:, :D_FF]
        b = ab_ref[:, D_FF:]
        d = d_ref[...]
        s = _sigmoid(a)
        silu = a * s
        o_ref[:, :D_FF] = (d * b * (s + silu * (1.0 - s))).astype(BF16)
        o_ref[:, D_FF:] = (d * silu).astype(BF16)

    return pl.pallas_call(
        body, name="swiglu_bwd", grid=(SEQ // tr,), in_specs=[_row_spec(tr, 2 * D_FF), _row_spec(tr, D_FF)],
        out_specs=_row_spec(tr, 2 * D_FF), out_shape=jax.ShapeDtypeStruct((SEQ, 2 * D_FF), BF16),
        compiler_params=_params(("parallel",), 10 * tr * 2 * D_FF * 4),
    )(ab, dact)


def _loss_head(x1, z, target, tr):
    w = D_MODEL

    def body(x_ref, z_ref, t_ref, dy_ref, dyb_ref, loss_ref, acc_ref):
        err = x_ref[...] + z_ref[...] - t_ref[...]
        dy = err * (1.0 / w)
        dy_ref[...] = dy
        dyb_ref[...] = dy.astype(BF16)
        part = jnp.sum(err * err, axis=0, keepdims=True)

        @pl.when(pl.program_id(0) == 0)
        def _():
            acc_ref[...] = part

        @pl.when(pl.program_id(0) > 0)
        def _():
            acc_ref[...] += part

        @pl.when(pl.program_id(0) == SEQ // tr - 1)
        def _():
            total = jnp.sum(acc_ref[...], axis=1, keepdims=True) * (0.5 / w)
            loss_ref[...] = jnp.broadcast_to(total, (1, LANE))

    return pl.pallas_call(
        body, name="loss_head", grid=(SEQ // tr,),
        in_specs=[_row_spec(tr, w), _row_spec(tr, w), _row_spec(tr, w)],
        out_specs=[_row_spec(tr, w), _row_spec(tr, w), _bcast_spec(LANE)],
        out_shape=[jax.ShapeDtypeStruct((SEQ, w), F32), jax.ShapeDtypeStruct((SEQ, w), BF16),
                   jax.ShapeDtypeStruct((1, LANE), F32)],
        scratch_shapes=[pltpu.VMEM((1, w), F32)],
        compiler_params=_params(("arbitrary",), 12 * tr * w * 4),
    )(x1, z, target)


GLA_ROWS = 256
GLA_CPB = GLA_ROWS // HG_CHUNK
GLA_NBLK = SEQ // GLA_ROWS
GLA_NCK = SEQ // HG_CHUNK


def _dot_split(m, xv, dims):
    hi = xv.astype(BF16)
    r1 = xv - hi.astype(F32)
    mid = r1.astype(BF16)
    lo = (r1 - mid.astype(F32)).astype(BF16)
    dot = lambda t: lax.dot_general(m, t, dims, preferred_element_type=F32)
    return (dot(lo) + dot(mid)) + dot(hi)


def _gla_block(qc, fc, lbv, masks):
    mask, maskb, direction = masks
    sq = _sigmoid(qc)
    q = qc * sq * HG_SCALE
    sf = _sigmoid(fc)
    forget = lbv + (1.0 - lbv) * sf
    k = 1.0 - forget
    logf = jnp.log(forget)
    b = _dot_split(maskb, logf, _NN)
    ends = []
    for j in range(GLA_CPB):
        lo, hi = j * HG_CHUNK, (j + 1) * HG_CHUNK
        end = jnp.where(direction == 0, b[hi - 1:hi, :], b[lo:lo + 1, :])
        ends.append(jnp.broadcast_to(end, (HG_CHUNK, HEAD_DIM)))
    bt = jnp.concatenate(ends, axis=0)
    eb = jnp.exp(b)
    qt = q * eb
    kt = k * jnp.exp(-b)
    kh = k * jnp.exp(bt - b)
    a = jnp.where(mask, _dot(qt, kt, _NT), 0.0)
    return dict(sq=sq, sf=sf, forget=forget, k=k, b=b, bt=bt, eb=eb, qt=qt, kt=kt, kh=kh, a=a)


def _gla_masks(direction):
    row = lax.broadcasted_iota(jnp.int32, (GLA_ROWS, GLA_ROWS), 0)
    col = lax.broadcasted_iota(jnp.int32, (GLA_ROWS, GLA_ROWS), 1)
    same = (row // HG_CHUNK) == (col // HG_CHUNK)
    mask = jnp.logical_and(same, jnp.where(direction == 0, row - col, col - row) >= 0)
    return mask, jnp.where(mask, 1.0, 0.0).astype(BF16), direction


def _gla_chunk_rows(j):
    return slice(j * HG_CHUNK, (j + 1) * HG_CHUNK)


def _head_spec(col_block0):
    return pl.BlockSpec((SEQ, HEAD_DIM), lambda h, d: (0, col_block0 + h))


def _gla_fwd(proj, lb, gain):
    nck = GLA_NCK

    def body(q_ref, f_ref, v_ref, g_ref, lb_ref, gain_ref, ohg_ref, opre_ref, st_ref, qt_scr, cs_scr, dec_scr):
        d = pl.program_id(1)
        masks = _gla_masks(d)
        lbv = lb_ref[...]

        def intra(s, carry):
            rows = pl.ds(pl.multiple_of(s * GLA_ROWS, GLA_ROWS), GLA_ROWS)
            ck = _gla_block(q_ref[rows, :], f_ref[rows, :], lbv, masks)
            v = v_ref[rows, :]
            o = _dot(ck["a"], v, _NN)

            @pl.when(d == 0)
            def _():
                opre_ref[rows, :] = o

            @pl.when(d == 1)
            def _():
                opre_ref[rows, :] += o

            qt_scr[rows, :] = ck["qt"].astype(BF16)
            for j in range(GLA_CPB):
                c = s * GLA_CPB + j
                cr = _gla_chunk_rows(j)
                cs_scr[c] = _dot(v[cr, :], ck["kh"][cr, :], _TN)
                dec_scr[c] = jnp.exp(ck["bt"][j * HG_CHUNK:j * HG_CHUNK + 8, :])
            return carry

        lax.fori_loop(0, GLA_NBLK, intra, 0)

        def scan(i, st):
            c = jnp.where(d == 0, i, nck - 1 - i)
            st_ref[c] = st
            return st * dec_scr[c][0:1, :] + cs_scr[c]

        lax.fori_loop(0, nck, scan, jnp.zeros((HEAD_DIM, HEAD_DIM), F32), unroll=4)

        def inter(c, carry):
            rows = pl.ds(pl.multiple_of(c * HG_CHUNK, HG_CHUNK), HG_CHUNK)
            opre_ref[rows, :] += _dot(qt_scr[rows, :], st_ref[c], _NT)
            return carry

        lax.fori_loop(0, nck, inter, 0, unroll=4)

        @pl.when(d == 1)
        def _():
            o = opre_ref[...]
            r = lax.rsqrt(jnp.mean(o * o, axis=-1, keepdims=True) + RMS_EPS)
            g = g_ref[...]
            ohg_ref[...] = (o * r * gain_ref[...] * (g * _sigmoid(g))).astype(BF16)

    blk = SEQ * HEAD_DIM * 4
    return pl.pallas_call(
        body, name="gla_fwd", grid=(HG_HEADS, 2),
        in_specs=[_head_spec(CB_HG_Q),
                  pl.BlockSpec((SEQ, HEAD_DIM), lambda h, d: (0, CB_F + 8 * d + h)),
                  _head_spec(CB_HG_I), _head_spec(CB_HG_G),
                  pl.BlockSpec((None, None, 1, HEAD_DIM), lambda h, d: (d, h, 0, 0)),
                  pl.BlockSpec((1, HEAD_DIM), lambda h, d: (0, 0))],
        out_specs=[_head_spec(0), _head_spec(0),
                   pl.BlockSpec((None, None, nck, HEAD_DIM, HEAD_DIM), lambda h, d: (h, d, 0, 0, 0))],
        out_shape=[jax.ShapeDtypeStruct((SEQ, D_MODEL), BF16), jax.ShapeDtypeStruct((SEQ, D_MODEL), F32),
                   jax.ShapeDtypeStruct((HG_HEADS, 2, nck, HEAD_DIM, HEAD_DIM), F32)],
        scratch_shapes=[pltpu.VMEM((SEQ, HEAD_DIM), BF16), pltpu.VMEM((nck, HEAD_DIM, HEAD_DIM), F32),
                        pltpu.VMEM((nck, 8, HEAD_DIM), F32)],
        compiler_params=_params(("parallel", "arbitrary"), 8 * blk + 3 * blk + 2 * blk + 2 * blk + blk // 2),
    )(proj, proj, proj, proj, lb, gain)


def _gla_bwd(proj, lb, gain, o_pre, states, do_hg):
    nck = GLA_NCK
    do, dg, dgain = _gla_bwd_norm(proj, gain, o_pre, do_hg, 256)

    def body(q_ref, f_ref, v_ref, lb_ref, do_ref, st_ref, dq_ref, df_ref, dv_ref, dlb_ref,
             dq_acc, dv_acc, dst_scr, cs_scr, dec_scr):
        d = pl.program_id(1)
        masks = _gla_masks(d)
        mask, maskb, _ = masks
        lbv = lb_ref[...]

        def intra(s, carry):
            rows = pl.ds(pl.multiple_of(s * GLA_ROWS, GLA_ROWS), GLA_ROWS)
            ck = _gla_block(q_ref[rows, :], f_ref[rows, :], lbv, masks)
            doc = do_ref[rows, :]
            for j in range(GLA_CPB):
                c = s * GLA_CPB + j
                cr = _gla_chunk_rows(j)
                cs_scr[c] = _dot(doc[cr, :], ck["qt"][cr, :], _TN)
                dec_scr[c] = jnp.exp(ck["bt"][j * HG_CHUNK:j * HG_CHUNK + 8, :])
            return carry

        lax.fori_loop(0, GLA_NBLK, intra, 0)

        def scan(i, dst):
            c = jnp.where(d == 0, nck - 1 - i, i)
            dst_scr[c] = dst
            return dst * dec_scr[c][0:1, :] + cs_scr[c]

        lax.fori_loop(0, nck, scan, jnp.zeros((HEAD_DIM, HEAD_DIM), F32), unroll=4)

        def grads(s, dlb):
            rows = pl.ds(pl.multiple_of(s * GLA_ROWS, GLA_ROWS), GLA_ROWS)
            qc = q_ref[rows, :]
            ck = _gla_block(qc, f_ref[rows, :], lbv, masks)
            v = v_ref[rows, :]
            doc = do_ref[rows, :]
            qt, kt, kh, a = ck["qt"], ck["kt"], ck["kh"], ck["a"]
            da = jnp.where(mask, _dot(doc, v, _NT), 0.0)
            dqt_i = _dot(da, kt, _NN)
            dkt = _dot(da, qt, _TN)
            dv_i = _dot(a, doc, _TN)
            dqt_p, dv_p, dkh_p, dbt_p = [], [], [], []
            for j in range(GLA_CPB):
                c = s * GLA_CPB + j
                cr = _gla_chunk_rows(j)
                st_in = st_ref[c]
                dst = dst_scr[c]
                dqt_p.append(dqt_i[cr, :] + _dot(doc[cr, :], st_in, _NN))
                dv_p.append(dv_i[cr, :] + _dot(kh[cr, :], dst, _NT))
                dkh_j = _dot(v[cr, :], dst, _NN)
                dkh_p.append(dkh_j)
                dbt_j = (dec_scr[c][0:1, :] * jnp.sum(dst * st_in, axis=0, keepdims=True)
                         + jnp.sum(dkh_j * kh[cr, :], axis=0, keepdims=True))
                dbt_p.append(jnp.broadcast_to(dbt_j, (HG_CHUNK, HEAD_DIM)))
            dqt = jnp.concatenate(dqt_p, axis=0)
            dv = jnp.concatenate(dv_p, axis=0)
            dkh = jnp.concatenate(dkh_p, axis=0)
            dbt = jnp.concatenate(dbt_p, axis=0)
            db = dqt * qt - dkt * kt - dkh * kh
            dq = dqt * ck["eb"]
            dk = dkt * jnp.exp(-ck["b"]) + dkh * jnp.exp(ck["bt"] - ck["b"])
            dlogf = _dot_split(maskb, db, _TN) + dbt
            dforget = dlogf / ck["forget"] - dk
            sf = ck["sf"]
            df_ref[rows, :] = (dforget * (1.0 - lbv) * sf * (1.0 - sf)).astype(BF16)
            sq = ck["sq"]
            dqc = dq * HG_SCALE * (sq + qc * sq * (1.0 - sq))

            @pl.when(d == 0)
            def _():
                dq_acc[rows, :] = dqc
                dv_acc[rows, :] = dv

            @pl.when(d == 1)
            def _():
                dq_acc[rows, :] += dqc
                dv_acc[rows, :] += dv

            return dlb + jnp.sum(dforget * (1.0 - sf), axis=0, keepdims=True)

        dlb_ref[...] = lax.fori_loop(0, GLA_NBLK, grads, jnp.zeros((1, HEAD_DIM), F32))

        @pl.when(d == 1)
        def _():
            dq_ref[...] = dq_acc[...].astype(BF16)
            dv_ref[...] = dv_acc[...].astype(BF16)

    blk = SEQ * HEAD_DIM * 4
    bshape = jax.ShapeDtypeStruct((SEQ, D_MODEL), BF16)
    state_bytes = nck * HEAD_DIM * HEAD_DIM * 4
    dq, df, dv, dlb = pl.pallas_call(
        body, name="gla_bwd", grid=(HG_HEADS, 2),
        in_specs=[_head_spec(CB_HG_Q),
                  pl.BlockSpec((SEQ, HEAD_DIM), lambda h, d: (0, CB_F + 8 * d + h)),
                  _head_spec(CB_HG_I),
                  pl.BlockSpec((None, None, 1, HEAD_DIM), lambda h, d: (d, h, 0, 0)),
                  _head_spec(0),
                  pl.BlockSpec((None, None, nck, HEAD_DIM, HEAD_DIM), lambda h, d: (h, d, 0, 0, 0))],
        out_specs=[_head_spec(0),
                   pl.BlockSpec((None, SEQ, HEAD_DIM), lambda h, d: (d, 0, h)),
                   _head_spec(0),
                   pl.BlockSpec((None, None, 1, HEAD_DIM), lambda h, d: (d, h, 0, 0))],
        out_shape=[bshape, jax.ShapeDtypeStruct((2, SEQ, D_MODEL), BF16), bshape,
                   jax.ShapeDtypeStruct((2, HG_HEADS, 1, HEAD_DIM), F32)],
        scratch_shapes=[pltpu.VMEM((SEQ, HEAD_DIM), F32)] * 2
        + [pltpu.VMEM((nck, HEAD_DIM, HEAD_DIM), F32)] * 2 + [pltpu.VMEM((nck, 8, HEAD_DIM), F32)],
        compiler_params=_params(("parallel", "arbitrary"), 8 * blk + 4 * state_bytes + 3 * blk + 2 * blk),
    )(proj, proj, proj, lb, do, states)
    return dq, df, dv, dg, dlb, dgain


def _gla_bwd_norm(proj, gain, o_pre, do_hg, tr):
    w = D_MODEL

    def body(g_ref, gain_ref, opre_ref, dohg_ref, do_ref, dg_ref, dgain_ref):
        gainv = gain_ref[...]
        part = jnp.zeros((1, HEAD_DIM), F32)
        for h in range(HG_HEADS):
            hs = slice(h * HEAD_DIM, (h + 1) * HEAD_DIM)
            o = opre_ref[:, hs]
            r = lax.rsqrt(jnp.mean(o * o, axis=-1, keepdims=True) + RMS_EPS)
            ohat = o * r
            g = g_ref[:, hs]
            sg = _sigmoid(g)
            silu = g * sg
            dout = dohg_ref[:, hs]
            dg_ref[:, hs] = (dout * ohat * gainv * (sg + silu * (1.0 - sg))).astype(BF16)
            dy = dout * silu
            part = part + jnp.sum(dy * ohat, axis=0, keepdims=True)
            dn = dy * gainv
            do_ref[:, hs] = r * (dn - ohat * jnp.mean(dn * ohat, axis=-1, keepdims=True))

        @pl.when(pl.program_id(0) == 0)
        def _():
            dgain_ref[...] = part

        @pl.when(pl.program_id(0) > 0)
        def _():
            dgain_ref[...] += part

    return pl.pallas_call(
        body, name="gla_bwd_norm", grid=(SEQ // tr,),
        in_specs=[_row_spec(tr, w, CB_HG_G * LANE // w), _bcast_spec(HEAD_DIM), _row_spec(tr, w), _row_spec(tr, w)],
        out_specs=[_row_spec(tr, w), _row_spec(tr, w), _bcast_spec(HEAD_DIM)],
        out_shape=[jax.ShapeDtypeStruct((SEQ, w), F32), jax.ShapeDtypeStruct((SEQ, w), BF16),
                   jax.ShapeDtypeStruct((1, HEAD_DIM), F32)],
        compiler_params=_params(("arbitrary",), 16 * tr * w * 4),
    )(proj, gain, o_pre, do_hg)


def _da_residue_rows(r, n0, size, d):
    if d == 1:
        return pl.ds(pl.multiple_of(n0, 8), size)
    return pl.ds(r + n0 * d, size, stride=d)


def _da_rmsnorm(x, gain):
    r = lax.rsqrt(jnp.mean(x * x, axis=-1, keepdims=True) + RMS_EPS)
    return x * r, r, x * r * gain


def _da_scores(qn_scr, kn_scr, slope, i, ld):
    w0 = jnp.clip(i * DA_QB - DA_RADIUS, 0, ld - DA_WIN)
    w0 = pl.multiple_of(w0, DA_RADIUS)
    qrows = pl.ds(pl.multiple_of(i * DA_QB, DA_QB), DA_QB)
    win = pl.ds(w0, DA_WIN)
    qb = qn_scr[qrows, :]
    kw = kn_scr[win, :]
    s = _dot(qb, kw, _NT) * DA_SCALE
    qpos = i * DA_QB + lax.broadcasted_iota(jnp.int32, (DA_QB, DA_WIN), 0)
    kpos = w0 + lax.broadcasted_iota(jnp.int32, (DA_QB, DA_WIN), 1)
    arel = jnp.abs(kpos - qpos)
    s = s - slope * arel.astype(F32)
    s = jnp.where(arel <= DA_RADIUS, s, NEG_INF)
    return s, qb, kw, qrows, win


def _da_slopes(group):
    d = DA_DILATIONS[group]
    sl = _alibi_slopes(DA_HEADS)[4 * group:4 * group + 4] * d
    return jnp.asarray(np.broadcast_to(sl[:, None, None], (4, 1, LANE)).copy())


def _da_fwd(proj, gq, gk, group):
    d = DA_DILATIONS[group]
    ld = SEQ // d
    nqb = ld // DA_QB

    def body(q_ref, k_ref, v_ref, gq_ref, gk_ref, sl_ref, o_ref, lse_ref, qn_scr, kn_scr, v_scr):
        slope = sl_ref[:, 0:1]

        def residue(r, carry):
            sel = _da_residue_rows(r, 0, ld, d)
            qn_scr[...] = _da_rmsnorm(q_ref[sel, :], gq_ref[...])[2].astype(BF16)
            kn_scr[...] = _da_rmsnorm(k_ref[sel, :], gk_ref[...])[2].astype(BF16)
            v_scr[...] = v_ref[sel, :].astype(BF16)

            def step(i, c2):
                s, _, _, _, win = _da_scores(qn_scr, kn_scr, slope, i, ld)
                m = jnp.max(s, axis=-1, keepdims=True)
                p = jnp.exp(s - m)
                l = jnp.sum(p, axis=-1, keepdims=True)
                out = _da_residue_rows(r, i * DA_QB, DA_QB, d)
                o_ref[out, :] = _dot(p, v_scr[win, :], _NN) / l
                lse_ref[out, :] = jnp.broadcast_to(m + jnp.log(l), (DA_QB, HEAD_DIM))
                return c2

            return lax.fori_loop(0, nqb, step, carry)

        lax.fori_loop(0, d, residue, 0)

    seq_spec = lambda base: pl.BlockSpec((SEQ, HEAD_DIM), lambda h: (0, base + 4 * group + h))
    out_spec = pl.BlockSpec((SEQ, HEAD_DIM), lambda h: (0, h))
    gain_spec = pl.BlockSpec((1, HEAD_DIM), lambda h: (0, 0))
    blk = SEQ * HEAD_DIM * 4
    return pl.pallas_call(
        body, name=f"da_fwd_g{group}", grid=(DA_HEADS_PER_GROUP,),
        in_specs=[seq_spec(CB_DA_Q), seq_spec(CB_DA_K), seq_spec(CB_DA_V), gain_spec, gain_spec,
                  pl.BlockSpec((None, 1, LANE), lambda h: (h, 0, 0))],
        out_specs=[out_spec, out_spec],
        out_shape=[jax.ShapeDtypeStruct((SEQ, DA_WIDTH), F32)] * 2,
        scratch_shapes=[pltpu.VMEM((ld, HEAD_DIM), BF16)] * 3,
        compiler_params=_params(("parallel",), 10 * blk + 2 * blk),
    )(proj, proj, proj, gq, gk, _da_slopes(group))


def _da_merge(os, lses, tr):
    w = DA_WIDTH

    def body(o0, o1, o2, l0, l1, l2, ob_ref, of_ref, lse_ref):
        la, lb_, lc = l0[...], l1[...], l2[...]
        m = jnp.maximum(jnp.maximum(la, lb_), lc)
        ea, eb, ec = jnp.exp(la - m), jnp.exp(lb_ - m), jnp.exp(lc - m)
        tot = ea + eb + ec
        o = (ea * o0[...] + eb * o1[...] + ec * o2[...]) / tot
        of_ref[...] = o
        ob_ref[...] = o.astype(BF16)
        lse_ref[...] = m + jnp.log(tot)

    return pl.pallas_call(
        body, name="da_merge", grid=(SEQ // tr,), in_specs=[_row_spec(tr, w)] * 6,
        out_specs=[_row_spec(tr, w)] * 3,
        out_shape=[jax.ShapeDtypeStruct((SEQ, w), BF16), jax.ShapeDtypeStruct((SEQ, w), F32),
                   jax.ShapeDtypeStruct((SEQ, w), F32)],
        compiler_params=_params(("parallel",), 24 * tr * w * 4),
    )(*os, *lses)


def _da_rowdot(do, o, tr):
    w = DA_WIDTH

    def body(do_ref, o_ref, out_ref):
        prod = do_ref[...] * o_ref[...]
        for h in range(w // HEAD_DIM):
            sl = slice(h * HEAD_DIM, (h + 1) * HEAD_DIM)
            out_ref[:, sl] = jnp.broadcast_to(jnp.sum(prod[:, sl], axis=-1, keepdims=True), (tr, HEAD_DIM))

    return pl.pallas_call(
        body, name="da_rowdot", grid=(SEQ // tr,), in_specs=[_row_spec(tr, w)] * 2,
        out_specs=_row_spec(tr, w), out_shape=jax.ShapeDtypeStruct((SEQ, w), F32),
        compiler_params=_params(("parallel",), 10 * tr * w * 4),
    )(do, o)


def _da_bwd(proj, gq, gk, do, lse, dd, group):
    d = DA_DILATIONS[group]
    ld = SEQ // d
    nqb = ld // DA_QB

    def body(q_ref, k_ref, v_ref, gq_ref, gk_ref, sl_ref, do_ref, lse_ref, dd_ref,
             dq_ref, dk_ref, dv_ref, dgq_ref, dgk_ref,
             qn_scr, kn_scr, v_scr, dqn_scr, dkn_scr, dvr_scr, dq_scr, dk_scr, dv_scr):
        gqv, gkv = gq_ref[...], gk_ref[...]
        slope = sl_ref[:, 0:1]

        def residue(r, carry):
            sel = _da_residue_rows(r, 0, ld, d)
            qn_scr[...] = _da_rmsnorm(q_ref[sel, :], gqv)[2].astype(BF16)
            kn_scr[...] = _da_rmsnorm(k_ref[sel, :], gkv)[2].astype(BF16)
            v_scr[...] = v_ref[sel, :].astype(BF16)
            dkn_scr[...] = jnp.zeros_like(dkn_scr)
            dvr_scr[...] = jnp.zeros_like(dvr_scr)

            def step(i, c2):
                s, qb, kw, qrows, win = _da_scores(qn_scr, kn_scr, slope, i, ld)
                src = _da_residue_rows(r, i * DA_QB, DA_QB, d)
                p = jnp.exp(s - lse_ref[src, :][:, 0:1])
                dob = do_ref[src, :]
                dvr_scr[win, :] += _dot(p, dob, _TN)
                dp = _dot(dob, v_scr[win, :], _NT)
                ds = p * (dp - dd_ref[src, :][:, 0:1]) * DA_SCALE
                dqn_scr[qrows, :] = _dot(ds, kw, _NN)
                dkn_scr[win, :] += _dot(ds, qb, _TN)
                return c2

            lax.fori_loop(0, nqb, step, 0)

            parts = []
            for x_ref, gv, dn_scr, dx_scr in ((q_ref, gqv, dqn_scr, dq_scr), (k_ref, gkv, dkn_scr, dk_scr)):
                hat, rstd, _ = _da_rmsnorm(x_ref[sel, :], gv)
                dn = dn_scr[...]
                dyg = dn * gv
                dx_scr[sel, :] = rstd * (dyg - hat * jnp.mean(dyg * hat, axis=-1, keepdims=True))
                parts.append(jnp.sum(dn * hat, axis=0, keepdims=True))
            dv_scr[sel, :] = dvr_scr[...]
            return carry[0] + parts[0], carry[1] + parts[1]

        zero = jnp.zeros((1, HEAD_DIM), F32)
        pq, pk = lax.fori_loop(0, d, residue, (zero, zero))

        @pl.when(pl.program_id(0) == 0)
        def _():
            dgq_ref[...] = pq
            dgk_ref[...] = pk

        @pl.when(pl.program_id(0) > 0)
        def _():
            dgq_ref[...] += pq
            dgk_ref[...] += pk

        dq_ref[...] = dq_scr[...].astype(BF16)
        dk_ref[...] = dk_scr[...].astype(BF16)
        dv_ref[...] = dv_scr[...].astype(BF16)

    seq_spec = lambda base: pl.BlockSpec((SEQ, HEAD_DIM), lambda h: (0, base + 4 * group + h))
    out_spec = pl.BlockSpec((SEQ, HEAD_DIM), lambda h: (0, h))
    gain_spec = pl.BlockSpec((1, HEAD_DIM), lambda h: (0, 0))
    oshape = jax.ShapeDtypeStruct((SEQ, DA_WIDTH), BF16)
    gshape = jax.ShapeDtypeStruct((1, HEAD_DIM), F32)
    blk = SEQ * HEAD_DIM * 4
    return pl.pallas_call(
        body, name=f"da_bwd_g{group}", grid=(DA_HEADS_PER_GROUP,),
        in_specs=[seq_spec(CB_DA_Q), seq_spec(CB_DA_K), seq_spec(CB_DA_V), gain_spec, gain_spec,
                  pl.BlockSpec((None, 1, LANE), lambda h: (h, 0, 0)), out_spec, out_spec, out_spec],
        out_specs=[out_spec, out_spec, out_spec, gain_spec, gain_spec],
        out_shape=[oshape, oshape, oshape, gshape, gshape],
        scratch_shapes=[pltpu.VMEM((ld, HEAD_DIM), BF16)] * 3 + [pltpu.VMEM((ld, HEAD_DIM), F32)] * 3
        + [pltpu.VMEM((SEQ, HEAD_DIM), F32)] * 3,
        compiler_params=_params(("arbitrary",), 12 * blk + 3 * blk + 3 * blk + 3 * blk),
    )(proj, proj, proj, gq, gk, _da_slopes(group), do, lse, dd)


def _mem_softmax(q, k, gq, gk):
    qhat, rq, qn = _da_rmsnorm(q, gq)
    khat, rk, kn = _da_rmsnorm(k, gk)
    s = _dot(qn, kn, _NT) * MEM_SCALE
    m = jnp.max(s, axis=-1, keepdims=True)
    e = jnp.exp(s - m)
    p = e / jnp.sum(e, axis=-1, keepdims=True)
    return p, (qhat, rq, qn), (khat, rk, kn)


def _mem_fwd(proj, kv, gq, gk, tq):
    def body(q_ref, k_ref, v_ref, gq_ref, gk_ref, o_ref):
        p, _, _ = _mem_softmax(q_ref[...], k_ref[...], gq_ref[...], gk_ref[...])
        o_ref[...] = _dot(p, v_ref[...], _NN).astype(BF16)

    gain_spec = pl.BlockSpec((1, HEAD_DIM), lambda h, i: (0, 0))
    return pl.pallas_call(
        body, name="mem_fwd", grid=(MEM_HEADS, SEQ // tq),
        in_specs=[pl.BlockSpec((tq, HEAD_DIM), lambda h, i: (i, CB_MEM_Q + h)),
                  pl.BlockSpec((N_MEM, HEAD_DIM), lambda h, i: (0, h)),
                  pl.BlockSpec((N_MEM, HEAD_DIM), lambda h, i: (0, MEM_HEADS + h)), gain_spec, gain_spec],
        out_specs=pl.BlockSpec((tq, HEAD_DIM), lambda h, i: (i, h)),
        out_shape=jax.ShapeDtypeStruct((SEQ, MEM_WIDTH), BF16),
        compiler_params=_params(("parallel", "parallel"), 16 * tq * N_MEM * 4),
    )(proj, kv, kv, gq, gk)


def _mem_bwd(proj, kv, gq, gk, do, tq):
    nq = SEQ // tq

    def body(q_ref, k_ref, v_ref, gq_ref, gk_ref, do_ref, dq_ref, dk_ref, dv_ref, dgq_ref, dgk_ref, dkn_scr):
        h, i = pl.program_id(0), pl.program_id(1)
        gqv, gkv = gq_ref[...], gk_ref[...]
        p, (qhat, rq, qn), (khat, rk, kn) = _mem_softmax(q_ref[...], k_ref[...], gqv, gkv)
        dob = do_ref[...]
        dvp = _dot(p, dob, _TN)
        dp = _dot(dob, v_ref[...], _NT)
        ds = p * (dp - jnp.sum(p * dp, axis=-1, keepdims=True)) * MEM_SCALE
        dqn = _dot(ds, kn, _NN)
        dknp = _dot(ds, qn, _TN)
        dyg = dqn * gqv
        dq_ref[...] = (rq * (dyg - qhat * jnp.mean(dyg * qhat, axis=-1, keepdims=True))).astype(BF16)
        dgq_part = jnp.sum(dqn * qhat, axis=0, keepdims=True)
        first = jnp.logical_and(h == 0, i == 0)

        @pl.when(first)
        def _():
            dgq_ref[...] = dgq_part

        @pl.when(jnp.logical_not(first))
        def _():
            dgq_ref[...] += dgq_part

        @pl.when(i == 0)
        def _():
            dv_ref[...] = dvp
            dkn_scr[...] = dknp

        @pl.when(i > 0)
        def _():
            dv_ref[...] += dvp
            dkn_scr[...] += dknp

        @pl.when(i == nq - 1)
        def _():
            dkn = dkn_scr[...]
            dkg = dkn * gkv
            dk_ref[...] = rk * (dkg - khat * jnp.mean(dkg * khat, axis=-1, keepdims=True))
            dgk_part = jnp.sum(dkn * khat, axis=0, keepdims=True)

            @pl.when(h == 0)
            def _():
                dgk_ref[...] = dgk_part

            @pl.when(h > 0)
            def _():
                dgk_ref[...] += dgk_part

    gain_spec = pl.BlockSpec((1, HEAD_DIM), lambda h, i: (0, 0))
    kvout = pl.BlockSpec((N_MEM, HEAD_DIM), lambda h, i: (0, h))
    return pl.pallas_call(
        body, name="mem_bwd", grid=(MEM_HEADS, nq),
        in_specs=[pl.BlockSpec((tq, HEAD_DIM), lambda h, i: (i, CB_MEM_Q + h)),
                  pl.BlockSpec((N_MEM, HEAD_DIM), lambda h, i: (0, h)),
                  pl.BlockSpec((N_MEM, HEAD_DIM), lambda h, i: (0, MEM_HEADS + h)), gain_spec, gain_spec,
                  pl.BlockSpec((tq, HEAD_DIM), lambda h, i: (i, h))],
        out_specs=[pl.BlockSpec((tq, HEAD_DIM), lambda h, i: (i, h)), kvout, kvout, gain_spec, gain_spec],
        out_shape=[jax.ShapeDtypeStruct((SEQ, MEM_WIDTH), BF16), jax.ShapeDtypeStruct((N_MEM, MEM_WIDTH), F32),
                   jax.ShapeDtypeStruct((N_MEM, MEM_WIDTH), F32), jax.ShapeDtypeStruct((1, HEAD_DIM), F32),
                   jax.ShapeDtypeStruct((1, HEAD_DIM), F32)],
        scratch_shapes=[pltpu.VMEM((N_MEM, HEAD_DIM), F32)],
        compiler_params=_params(("arbitrary", "arbitrary"), 24 * tq * N_MEM * 4),
    )(proj, kv, kv, gq, gk, do)


def _mesh_position():
    return lax.axis_index("x"), lax.axis_index("y"), lax.axis_index("c")


def _any_spec():
    return pl.BlockSpec(memory_space=pl.ANY)


def _all_gather(shards):
    n = len(shards)

    def body(*refs):
        ins, outs = refs[:n], refs[n:2 * n]
        send_sems, recv_sems, local_sems = refs[2 * n:]
        x, y, c = _mesh_position()
        me, sibling = (x, y, c), (x, y, 1 - c)
        chips = [(1 - x, y), (x, 1 - y), (1 - x, 1 - y)]

        def copy(w, k, block, to, src=None):
            px, py, pc = block
            rows = outs[w].at[4 * px + 2 * py + pc]
            return pltpu.make_async_remote_copy(
                src_ref=rows if src is None else src, dst_ref=rows,
                send_sem=send_sems.at[7 * w + k], recv_sem=recv_sems.at[7 * w + k],
                device_id=to, device_id_type=MESH)

        started = []
        for w in range(n):
            mine = pltpu.make_async_copy(ins[w], outs[w].at[4 * x + 2 * y + c], local_sems.at[w])
            mine.start()
            started.append(mine)
        sends = []
        for w in range(n):
            first = [copy(w, 0, me, sibling, src=ins[w])]
            first += [copy(w, 1 + j, me, (*chip, c), src=ins[w]) for j, chip in enumerate(chips)]
            for cp in first:
                cp.start()
            sends += first
        for w in range(n):
            for j, chip in enumerate(chips):
                copy(w, 1 + j, (*chip, c), me).wait_recv()
                passed = copy(w, 4 + j, (*chip, c), sibling)
                passed.start()
                sends.append(passed)
        for w in range(n):
            copy(w, 0, sibling, me).wait_recv()
            for j, chip in enumerate(chips):
                copy(w, 4 + j, (*chip, 1 - c), me).wait_recv()
        for cp in sends:
            cp.wait_send()
        for mine in started:
            mine.wait()

    return pl.pallas_call(
        body, name="weights_all_gather",
        in_specs=[_any_spec()] * n, out_specs=[_any_spec()] * n,
        out_shape=[jax.ShapeDtypeStruct((N_DEV,) + s.shape, s.dtype) for s in shards],
        scratch_shapes=[pltpu.SemaphoreType.DMA((7 * n,)), pltpu.SemaphoreType.DMA((7 * n,)),
                        pltpu.SemaphoreType.DMA((n,))],
    )(*shards)


def _chip_of(j, x, y):
    return (1 - x if j & 1 else x, 1 - y if j & 2 else y)


def _sibling_exchange(grads):
    n = len(grads)

    def body(*refs):
        ins, outs = refs[:n], refs[n:2 * n]
        send_sems, recv_sems = refs[2 * n:]
        x, y, c = _mesh_position()
        copies = []
        for w in range(n):
            for j in range(4):
                px, py = _chip_of(j, x, y)
                cp = pltpu.make_async_remote_copy(
                    src_ref=ins[w].at[4 * px + 2 * py + (1 - c)], dst_ref=outs[w].at[j],
                    send_sem=send_sems.at[4 * w + j], recv_sem=recv_sems.at[4 * w + j],
                    device_id=(x, y, 1 - c), device_id_type=MESH)
                cp.start()
                copies.append(cp)
        for cp in copies:
            cp.wait()

    return pl.pallas_call(
        body, name="grads_sibling_exchange",
        in_specs=[_any_spec()] * n, out_specs=[_any_spec()] * n,
        out_shape=[jax.ShapeDtypeStruct((4,) + g.shape[1:], g.dtype) for g in grads],
        scratch_shapes=[pltpu.SemaphoreType.DMA((4 * n,)), pltpu.SemaphoreType.DMA((4 * n,))],
    )(*grads)


def _chip_partials(grad, recv, name, tr):
    _, rows, width = grad.shape

    def body(g_ref, r_ref, own_ref, other_ref):
        x, y, c = _mesh_position()
        for j in range(4):
            px, py = _chip_of(j, x, y)
            total = g_ref[4 * px + 2 * py + c].astype(F32) + r_ref[j].astype(F32)
            if j == 0:
                own_ref[...] = total
            else:
                other_ref[j - 1] = total.astype(BF16)

    return pl.pallas_call(
        body, name=name, grid=(rows // tr,),
        in_specs=[pl.BlockSpec((N_DEV, tr, width), lambda i: (0, i, 0)),
                  pl.BlockSpec((4, tr, width), lambda i: (0, i, 0))],
        out_specs=[pl.BlockSpec((tr, width), lambda i: (i, 0)), pl.BlockSpec((3, tr, width), lambda i: (0, i, 0))],
        out_shape=[jax.ShapeDtypeStruct((rows, width), F32), jax.ShapeDtypeStruct((3, rows, width), BF16)],
        compiler_params=_params(("parallel",), 2 * 20 * tr * width * 2 + 8 * tr * width * 4),
    )(grad, recv)


def _chip_exchange(partials):
    n = len(partials)

    def body(*refs):
        ins, outs = refs[:n], refs[n:2 * n]
        send_sems, recv_sems = refs[2 * n:]
        x, y, c = _mesh_position()
        copies = []
        for w in range(n):
            for j in range(1, 4):
                px, py = _chip_of(j, x, y)
                cp = pltpu.make_async_remote_copy(
                    src_ref=ins[w].at[j - 1], dst_ref=outs[w].at[j - 1],
                    send_sem=send_sems.at[3 * w + j - 1], recv_sem=recv_sems.at[3 * w + j - 1],
                    device_id=(px, py, c), device_id_type=MESH)
                cp.start()
                copies.append(cp)
        for cp in copies:
            cp.wait()

    return pl.pallas_call(
        body, name="grads_chip_exchange",
        in_specs=[_any_spec()] * n, out_specs=[_any_spec()] * n,
        out_shape=[jax.ShapeDtypeStruct(p.shape, p.dtype) for p in partials],
        scratch_shapes=[pltpu.SemaphoreType.DMA((3 * n,)), pltpu.SemaphoreType.DMA((3 * n,))],
    )(*partials)


def _adamw_math(w, g, m, v):
    m = ADAM_B1 * m + (1.0 - ADAM_B1) * g
    v = ADAM_B2 * v + (1.0 - ADAM_B2) * (g * g)
    m_hat = m / (1.0 - ADAM_B1 ** ADAM_STEP)
    v_hat = v / (1.0 - ADAM_B2 ** ADAM_STEP)
    delta = -ADAM_LR * (m_hat / (jnp.sqrt(v_hat) + ADAM_EPS) + ADAM_WD * w)
    return delta, m, v


def _adamw_shard(own, recv, w, m, v, name, tr):
    rows, width = own.shape

    def body(own_ref, r_ref, w_ref, m_ref, v_ref, g_ref, d_ref, nm_ref, nv_ref):
        g = own_ref[...]
        for j in range(3):
            g = g + r_ref[j].astype(F32)
        g_ref[...] = g
        d_ref[...], nm_ref[...], nv_ref[...] = _adamw_math(w_ref[...], g, m_ref[...], v_ref[...])

    spec = pl.BlockSpec((tr, width), lambda i: (i, 0))
    shape = jax.ShapeDtypeStruct((rows, width), F32)
    return pl.pallas_call(
        body, name=name, grid=(rows // tr,),
        in_specs=[spec, pl.BlockSpec((3, tr, width), lambda i: (0, i, 0)), spec, spec, spec],
        out_specs=[spec] * 4, out_shape=[shape] * 4,
        compiler_params=_params(("parallel",), 22 * tr * width * 4),
    )(own, recv, w, m, v)


def _small_all_reduce_adamw(gpart, lbpack, wpack, mpack, vpack):
    def body(gp_ref, lb_ref, w_ref, m_ref, v_ref, g_ref, d_ref, nm_ref, nv_ref, gath_ref, send_sems, recv_sems):
        x, y, c = _mesh_position()
        me = 4 * x + 2 * y + c
        gath_ref[me] = gp_ref[...]
        copies = []
        for j in range(1, N_DEV):
            peer = (x ^ (j >> 2), y ^ ((j >> 1) & 1), c ^ (j & 1))
            cp = pltpu.make_async_remote_copy(
                src_ref=gp_ref, dst_ref=gath_ref.at[me], send_sem=send_sems.at[j - 1], recv_sem=recv_sems.at[j - 1],
                device_id=peer, device_id_type=MESH)
            cp.start()
            copies.append(cp)
        for cp in copies:
            cp.wait()
        tot = gath_ref[0]
        for s in range(1, N_DEV):
            tot = tot + gath_ref[s]
        lb = lb_ref[...]
        dl = tot[16:32, :] * lb * (1.0 - lb)
        g = jnp.concatenate([tot[0:16, :], dl[0:8, :], -dl[0:8, :], dl[8:16, :], -dl[8:16, :],
                             tot[32:SMALL_GRAD_ROWS, :]], axis=0)
        g_ref[...] = g
        d_ref[...], nm_ref[...], nv_ref[...] = _adamw_math(w_ref[...], g, m_ref[...], v_ref[...])

    vm = pl.BlockSpec(memory_space=pltpu.VMEM)
    shape = jax.ShapeDtypeStruct((SMALL_ROWS, LANE), F32)
    return pl.pallas_call(
        body, name="small_all_reduce_adamw", in_specs=[vm] * 5, out_specs=[vm] * 4, out_shape=[shape] * 4,
        scratch_shapes=[pltpu.VMEM((N_DEV, SMALL_GRAD_ROWS, LANE), F32),
                        pltpu.SemaphoreType.DMA((N_DEV - 1,)), pltpu.SemaphoreType.DMA((N_DEV - 1,))],
    )(gpart, lbpack, wpack, mpack, vpack)


_SMALL_NAMES = ("norm_mix_gain", "norm_mem_gain", "lb_logits_fw", "lb_logits_bw", "norm_ffn_gain",
                "hg_norm_gain", "da_q_gain", "da_k_gain", "mem_q_gain", "mem_k_gain")
_SMALL_ROW0 = {"norm_mix_gain": 0, "norm_mem_gain": 8, "lb_logits_fw": 16, "lb_logits_bw": 32, "norm_ffn_gain": 48,
               "hg_norm_gain": 56, "da_q_gain": 64, "da_k_gain": 72, "mem_q_gain": 80, "mem_k_gain": 88}
_LOSS_ROW = 96


def _pack_rows(parts, total_rows):
    rows = []
    for p in parts:
        r = p.reshape(-1, LANE)
        rows.append(jnp.pad(r, ((0, -r.shape[0] % 8), (0, 0))))
    used = sum(r.shape[0] for r in rows)
    if total_rows > used:
        rows.append(jnp.zeros((total_rows - used, LANE), F32))
    return jnp.concatenate(rows, axis=0)


def _unpack_small(pack, like):
    out = {}
    for name in _SMALL_NAMES:
        n = like[name].size // LANE
        r0 = _SMALL_ROW0[name]
        out[name] = pack[r0:r0 + n].reshape(like[name].shape)
    return out


def kernel(x, mem, norm_mix_gain, norm_mem_gain, w_in, lb_logits_fw, lb_logits_bw, hg_norm_gain, da_q_gain, da_k_gain, w_mem_kv, mem_q_gain, mem_k_gain, w_proj_hg, w_proj_da, w_proj_mem, w_out, norm_ffn_gain, w_ffn_in, w_ffn_out, loss_target, m_norm_mix_gain, m_norm_mem_gain, m_w_in, m_lb_logits_fw, m_lb_logits_bw, m_hg_norm_gain, m_da_q_gain, m_da_k_gain, m_w_mem_kv, m_mem_q_gain, m_mem_k_gain, m_w_proj_hg, m_w_proj_da, m_w_proj_mem, m_w_out, m_norm_ffn_gain, m_w_ffn_in, m_w_ffn_out, v_norm_mix_gain, v_norm_mem_gain, v_w_in, v_lb_logits_fw, v_lb_logits_bw, v_hg_norm_gain, v_da_q_gain, v_da_k_gain, v_w_mem_kv, v_mem_q_gain, v_mem_k_gain, v_w_proj_hg, v_w_proj_da, v_w_proj_mem, v_w_out, v_norm_ffn_gain, v_w_ffn_in, v_w_ffn_out):
    small_w = dict(norm_mix_gain=norm_mix_gain, norm_mem_gain=norm_mem_gain, lb_logits_fw=lb_logits_fw,
                   lb_logits_bw=lb_logits_bw, norm_ffn_gain=norm_ffn_gain, hg_norm_gain=hg_norm_gain,
                   da_q_gain=da_q_gain, da_k_gain=da_k_gain, mem_q_gain=mem_q_gain, mem_k_gain=mem_k_gain)
    small_m = dict(norm_mix_gain=m_norm_mix_gain, norm_mem_gain=m_norm_mem_gain, lb_logits_fw=m_lb_logits_fw,
                   lb_logits_bw=m_lb_logits_bw, norm_ffn_gain=m_norm_ffn_gain, hg_norm_gain=m_hg_norm_gain,
                   da_q_gain=m_da_q_gain, da_k_gain=m_da_k_gain, mem_q_gain=m_mem_q_gain, mem_k_gain=m_mem_k_gain)
    small_v = dict(norm_mix_gain=v_norm_mix_gain, norm_mem_gain=v_norm_mem_gain, lb_logits_fw=v_lb_logits_fw,
                   lb_logits_bw=v_lb_logits_bw, norm_ffn_gain=v_norm_ffn_gain, hg_norm_gain=v_hg_norm_gain,
                   da_q_gain=v_da_q_gain, da_k_gain=v_da_k_gain, mem_q_gain=v_mem_q_gain, mem_k_gain=v_mem_k_gain)
    big_names = ("w_in", "w_mem_kv", "w_proj_hg", "w_proj_da", "w_proj_mem", "w_out", "w_ffn_in", "w_ffn_out")
    big_w = dict(w_in=w_in[0], w_mem_kv=w_mem_kv[0], w_proj_hg=w_proj_hg[0], w_proj_da=w_proj_da[0],
                 w_proj_mem=w_proj_mem[0], w_out=w_out[0], w_ffn_in=w_ffn_in[0], w_ffn_out=w_ffn_out[0])
    big_m = dict(w_in=m_w_in[0], w_mem_kv=m_w_mem_kv[0], w_proj_hg=m_w_proj_hg[0], w_proj_da=m_w_proj_da[0],
                 w_proj_mem=m_w_proj_mem[0], w_out=m_w_out[0], w_ffn_in=m_w_ffn_in[0], w_ffn_out=m_w_ffn_out[0])
    big_v = dict(w_in=v_w_in[0], w_mem_kv=v_w_mem_kv[0], w_proj_hg=v_w_proj_hg[0], w_proj_da=v_w_proj_da[0],
                 w_proj_mem=v_w_proj_mem[0], w_out=v_w_out[0], w_ffn_in=v_w_ffn_in[0], w_ffn_out=v_w_ffn_out[0])

    gathered = _all_gather([big_w[n].astype(BF16) for n in _BIG_NAMES])
    local = _local_step(x[0], mem[0], loss_target[0], small_w, dict(zip(_BIG_NAMES, gathered)))
    stacked, grad_x = local["stacked"], local["grad_x"]

    sib = _sibling_exchange([stacked[n] for n in _BIG_NAMES])
    row_tile = dict(w_in=128, w_mem_kv=128, w_proj_hg=128, w_proj_da=512, w_proj_mem=512, w_out=128,
                    w_ffn_in=128, w_ffn_out=352)
    partials = [_chip_partials(stacked[n], r, "chip_partials_" + n, row_tile[n]) for n, r in zip(_BIG_NAMES, sib)]
    received = _chip_exchange([p[1] for p in partials])
    big_out = {}
    for n, p, r in zip(_BIG_NAMES, partials, received):
        big_out[n] = _adamw_shard(p[0], r, big_w[n], big_m[n], big_v[n], "adamw_" + n, row_tile[n])

    wpack = _pack_rows([small_w[n] for n in _SMALL_NAMES], SMALL_ROWS)
    mpack = _pack_rows([small_m[n] for n in _SMALL_NAMES], SMALL_ROWS)
    vpack = _pack_rows([small_v[n] for n in _SMALL_NAMES], SMALL_ROWS)
    gs, ds, ms, vs = _small_all_reduce_adamw(local["gpart"], local["lbpack"], wpack, mpack, vpack)
    loss = gs[_LOSS_ROW, 0]
    small_out = [_unpack_small(t, small_w) for t in (gs, ds, ms, vs)]

    order = ("norm_mix_gain", "norm_mem_gain", "w_in", "lb_logits_fw", "lb_logits_bw", "hg_norm_gain", "da_q_gain",
             "da_k_gain", "w_mem_kv", "mem_q_gain", "mem_k_gain", "w_proj_hg", "w_proj_da", "w_proj_mem", "w_out",
             "norm_ffn_gain", "w_ffn_in", "w_ffn_out")
    outs = [loss, grad_x[None]]
    for kind in range(4):
        for n in order:
            outs.append(big_out[n][kind][None] if n in big_out else small_out[kind][n])
    return tuple(outs)


_BIG_NAMES = ("w_in", "w_mem_kv", "w_proj_hg", "w_proj_da", "w_proj_mem", "w_out", "w_ffn_in", "w_ffn_out")


def _local_step(xs, mems, target, sw, wg):
    norm_mix_gain, norm_mem_gain, norm_ffn_gain = sw["norm_mix_gain"], sw["norm_mem_gain"], sw["norm_ffn_gain"]
    lb_logits_fw, lb_logits_bw, hg_norm_gain = sw["lb_logits_fw"], sw["lb_logits_bw"], sw["hg_norm_gain"]
    da_q_gain, da_k_gain, mem_q_gain, mem_k_gain = sw["da_q_gain"], sw["da_k_gain"], sw["mem_q_gain"], sw["mem_k_gain"]
    win_st = wg["w_in"]
    wkv = wg["w_mem_kv"].reshape(D_MODEL, 2 * MEM_WIDTH)
    wphg = wg["w_proj_hg"].reshape(D_MODEL, D_MODEL)
    wpda = jnp.transpose(wg["w_proj_da"], (1, 0, 2)).reshape(DA_WIDTH, D_MODEL)
    wpmem = jnp.transpose(wg["w_proj_mem"], (1, 0, 2)).reshape(MEM_WIDTH, D_MODEL)
    wout = wg["w_out"].reshape(D_MODEL, D_MODEL)
    wfin = jnp.transpose(wg["w_ffn_in"], (1, 0, 2)).reshape(D_MODEL, 2 * D_FF)
    wfout = wg["w_ffn_out"].reshape(D_FF, D_MODEL)

    lb_fw = _lb_table(lb_logits_fw, "lb_table_fw")
    lb_bw = _lb_table(lb_logits_bw, "lb_table_bw")
    lb = jnp.concatenate([lb_fw, lb_bw], axis=0).reshape(2, HG_HEADS, 1, HEAD_DIM)
    h = _rmsnorm_fwd(xs, norm_mix_gain, "norm_mix_fwd", 512)
    proj = _matmul(h, win_st, "nn", F32, 512, IN_SHARD, D_MODEL, "proj_fwd", b_stacked=True, n_outer=True)
    o_hg, o_pre, states = _gla_fwd(proj, lb, hg_norm_gain)
    da = [_da_fwd(proj, da_q_gain, da_k_gain, g) for g in range(3)]
    o_da, o_da32, lse_da = _da_merge([t[0] for t in da], [t[1] for t in da], 512)
    mem_n = _rmsnorm_fwd(mems, norm_mem_gain, "norm_mem_fwd", N_MEM)
    kv = _matmul(mem_n, wkv, "nn", F32, N_MEM, 1024, D_MODEL, "mem_kv_fwd")
    o_mem = _mem_fwd(proj, kv, mem_q_gain, mem_k_gain, 512)
    t_hg = _matmul(o_hg, wphg, "nn", F32, 512, 1024, D_MODEL, "proj_hg_fwd")
    t_da = _matmul(o_da, wpda, "nn", F32, 512, 1024, DA_WIDTH, "proj_da_fwd")
    t_mem = _matmul(o_mem, wpmem, "nn", F32, 512, 1024, MEM_WIDTH, "proj_mem_fwd")
    merged = _gate_merge_fwd(proj, t_hg, t_da, t_mem, 256)
    u = _matmul(merged, wout, "nn", F32, 512, 1024, D_MODEL, "out_fwd")
    x1, h2 = _residual_rmsnorm_fwd(xs, u, norm_ffn_gain, "norm_ffn_fwd", 512)
    ab = _matmul(h2, wfin, "nn", F32, 512, 1408, D_MODEL, "ffn_in_fwd")
    act = _swiglu_fwd(ab, 256)
    z = _matmul(act, wfout, "nn", F32, 512, 1024, D_FF, "ffn_out_fwd")
    dy, dyb, loss_part = _loss_head(x1, z, target, 512)

    dact = _matmul(dyb, wfout, "nt", F32, 512, 1408, D_MODEL, "ffn_out_bwd_act")
    g_wfout = _matmul(act, dyb, "tn", BF16, 1408, 1024, 512, "ffn_out_bwd_w")
    dab = _swiglu_bwd(ab, dact, 256)
    dh2 = _matmul(dab, wfin, "nt", F32, 512, 1024, 1408, "ffn_in_bwd_act")
    g_wfin = _matmul(h2, dab, "tn", BF16, 1024, 1408, 512, "ffn_in_bwd_w")
    dx1, dx1b, g_norm_ffn = _rmsnorm_bwd(x1, dh2, dy, norm_ffn_gain, "norm_ffn_bwd", 256)
    dmerged = _matmul(dx1b, wout, "nt", F32, 512, 1024, D_MODEL, "out_bwd_act")
    g_wout = _matmul(merged, dx1b, "tn", BF16, 1024, 1024, 512, "out_bwd_w")
    dt_hg, dt_da, dt_mem, dgates = _gate_merge_bwd(proj, t_hg, t_da, t_mem, dmerged, 256)
    do_hg = _matmul(dt_hg, wphg, "nt", F32, 512, 1024, D_MODEL, "proj_hg_bwd_act")
    g_wphg = _matmul(o_hg, dt_hg, "tn", BF16, 1024, 1024, 512, "proj_hg_bwd_w")
    do_da = _matmul(dt_da, wpda, "nt", F32, 512, DA_WIDTH, D_MODEL, "proj_da_bwd_act")
    g_wpda = _matmul(o_da, dt_da, "tn", BF16, DA_WIDTH, LANE, 512, "proj_da_bwd_w", out_stacked=True)
    do_mem = _matmul(dt_mem, wpmem, "nt", F32, 512, MEM_WIDTH, D_MODEL, "proj_mem_bwd_act")
    g_wpmem = _matmul(o_mem, dt_mem, "tn", BF16, MEM_WIDTH, LANE, 512, "proj_mem_bwd_w", out_stacked=True)

    dmemq, dk_mem, dv_mem, g_mem_q, g_mem_k = _mem_bwd(proj, kv, mem_q_gain, mem_k_gain, do_mem, 512)
    dkv = jnp.concatenate([dk_mem, dv_mem], axis=1).astype(BF16)
    g_wkv = _matmul(mem_n, dkv, "tn", BF16, 1024, 1024, N_MEM, "mem_kv_bwd_w")
    dmem_n = _matmul(dkv, wkv, "nt", F32, N_MEM, 1024, 1024, "mem_kv_bwd_act")
    g_norm_mem = _gain_grad(mems, dmem_n, "norm_mem_bwd")

    dd_da = _da_rowdot(do_da, o_da32, 512)
    da_b = [_da_bwd(proj, da_q_gain, da_k_gain, do_da, lse_da, dd_da, g) for g in range(3)]
    g_da_q = da_b[0][3] + da_b[1][3] + da_b[2][3]
    g_da_k = da_b[0][4] + da_b[1][4] + da_b[2][4]

    dq_hg, df_hg, dv_hg, dg_hg, dlb, g_hg_norm = _gla_bwd(proj, lb, hg_norm_gain, o_pre, states, do_hg)

    dproj = jnp.concatenate(
        [dq_hg, df_hg[0], df_hg[1], dv_hg, dg_hg]
        + [t[0] for t in da_b] + [t[1] for t in da_b] + [t[2] for t in da_b] + [dmemq, dgates], axis=1)
    dh = _matmul(dproj, win_st, "nt", F32, 512, 1024, IN_SHARD, "proj_bwd_act", b_stacked=True)
    g_win = _matmul(h, dproj, "tn", BF16, 1024, IN_SHARD, 512, "proj_bwd_w", out_stacked=True)
    grad_x, _, g_norm_mix = _rmsnorm_bwd(xs, dh, dx1, norm_mix_gain, "norm_mix_bwd", 256)

    stacked = dict(
        w_in=g_win,
        w_mem_kv=g_wkv.reshape(N_DEV, D_MODEL // N_DEV, 2 * MEM_WIDTH),
        w_proj_hg=g_wphg.reshape(N_DEV, D_MODEL // N_DEV, D_MODEL),
        w_proj_da=g_wpda, w_proj_mem=g_wpmem,
        w_out=g_wout.reshape(N_DEV, D_MODEL // N_DEV, D_MODEL),
        w_ffn_in=jnp.transpose(g_wfin.reshape(D_MODEL, N_DEV, 2 * D_FF // N_DEV), (1, 0, 2)),
        w_ffn_out=g_wfout.reshape(N_DEV, D_FF // N_DEV, D_MODEL))
    gpart = _pack_rows([g_norm_mix, g_norm_mem, dlb[0], dlb[1], g_norm_ffn, g_hg_norm, g_da_q, g_da_k,
                        g_mem_q, g_mem_k, loss_part], SMALL_GRAD_ROWS)
    lbpack = jnp.concatenate([lb_fw.reshape(8, LANE), lb_bw.reshape(8, LANE)], axis=0)
    return dict(grad_x=grad_x, stacked=stacked, gpart=gpart, lbpack=lbpack)
```

```python
import numpy as np
import jax
import jax.numpy as jnp
from jax import lax
from jax.experimental import pallas as pl
from jax.experimental.pallas import tpu as pltpu

F32 = jnp.float32
BF16 = jnp.bfloat16
MESH = pl.DeviceIdType.MESH

SEQ = 4096
D_MODEL = 1024
N_DEV = 8
N_MEM = 256
RMS_EPS = 1e-6
NEG_INF = -1e30
LANE = 128
HEAD_DIM = 128
HG_HEADS = 8
HG_CHUNK = 64
HG_SCALE = HEAD_DIM ** -0.5
DA_DILATIONS = (1, 4, 16)
DA_RADIUS = 64
DA_HEADS_PER_GROUP = 4
DA_HEADS = 12
DA_WIDTH = 512
DA_SCALE = HEAD_DIM ** -0.5
DA_QB = 128
DA_WIN = 256
MEM_HEADS = 4
MEM_WIDTH = 512
MEM_SCALE = HEAD_DIM ** -0.5
D_FF = 2816
IN_COLS = 13312
IN_SHARD = IN_COLS // N_DEV
CB_HG_Q, CB_F, CB_HG_I, CB_HG_G = 0, 8, 24, 32
CB_DA_Q, CB_DA_K, CB_DA_V, CB_MEM_Q = 40, 52, 64, 76
N_CB = IN_COLS // LANE
ADAM_LR, ADAM_B1, ADAM_B2, ADAM_EPS, ADAM_WD, ADAM_STEP = 0.001, 0.9, 0.999, 1e-08, 0.01, 10
VMEM_BYTES_V7X = 64 * 1024 * 1024
SMALL_ROWS = 104
SMALL_GRAD_ROWS = 88

_NN = (((1,), (0,)), ((), ()))
_NT = (((1,), (1,)), ((), ()))
_TN = (((0,), (0,)), ((), ()))


def _dot(a, b, dims):
    return lax.dot_general(a.astype(BF16), b.astype(BF16), dims, preferred_element_type=F32)


def _dot_exact(a, b, dims):
    return lax.dot_general(a, b, dims, precision=lax.Precision.HIGHEST, preferred_element_type=F32)


def _sigmoid(x):
    return 0.5 * jnp.tanh(0.5 * x) + 0.5


def _params(semantics, est_bytes):
    limit = int(min(VMEM_BYTES_V7X - (6 << 20), max(32 << 20, est_bytes * 3 // 2)))
    return pltpu.CompilerParams(dimension_semantics=semantics, vmem_limit_bytes=limit)


def _nbytes(shape, dtype):
    return int(np.prod(shape)) * jnp.dtype(dtype).itemsize


def _alibi_slopes(n):
    return (2.0 ** (-8.0 * np.arange(1, n + 1) / n)).astype(np.float32)


def _matmul(a, b, mode, out_dtype, tm, tn, tk, name, b_stacked=False, out_stacked=False, n_outer=False, after=None):
    if mode == "tn":
        kdim, m = a.shape
    else:
        m, kdim = a.shape
    if b_stacked:
        if mode == "nn":
            n = b.shape[0] * b.shape[2]
            assert tn == b.shape[2] and tk == kdim == b.shape[1]
        else:
            assert mode == "nt" and tk == b.shape[2] and b.shape[0] * tk == kdim
            n = b.shape[1]
    else:
        n = b.shape[0] if mode == "nt" else b.shape[1]
    assert m % tm == 0 and n % tn == 0 and kdim % tk == 0
    gm, gn, gk = m // tm, n // tn, kdim // tk

    def ijk(f):
        if n_outer:
            return lambda j, i, k: f(i, j, k)
        return f

    if mode == "tn":
        a_spec = pl.BlockSpec((tk, tm), ijk(lambda i, j, k: (k, i)))
    else:
        a_spec = pl.BlockSpec((tm, tk), ijk(lambda i, j, k: (i, k)))
    if b_stacked and mode == "nn":
        b_spec = pl.BlockSpec((None, tk, tn), ijk(lambda i, j, k: (j, 0, 0)))
    elif b_stacked:
        b_spec = pl.BlockSpec((None, tn, tk), ijk(lambda i, j, k: (k, j, 0)))
    elif mode == "nt":
        b_spec = pl.BlockSpec((tn, tk), ijk(lambda i, j, k: (j, k)))
    else:
        b_spec = pl.BlockSpec((tk, tn), ijk(lambda i, j, k: (k, j)))
    if out_stacked:
        assert tm == m
        out_shape = jax.ShapeDtypeStruct((gn, m, tn), out_dtype)
        o_spec = pl.BlockSpec((None, tm, tn), ijk(lambda i, j, k: (j, i, 0)))
    else:
        out_shape = jax.ShapeDtypeStruct((m, n), out_dtype)
        o_spec = pl.BlockSpec((tm, tn), ijk(lambda i, j, k: (i, j)))
    dims = {"nn": _NN, "nt": _NT, "tn": _TN}[mode]

    def body(*refs):
        a_ref, b_ref = refs[0], refs[1]
        o_ref = refs[2 if after is None else 3]
        part = _dot(a_ref[...], b_ref[...], dims)
        if gk == 1:
            o_ref[...] = part.astype(out_dtype)
            return
        acc_ref = refs[-1]
        k = pl.program_id(2)

        @pl.when(k == 0)
        def _():
            acc_ref[...] = part

        @pl.when(k > 0)
        def _():
            acc_ref[...] += part

        @pl.when(k == gk - 1)
        def _():
            o_ref[...] = acc_ref[...].astype(out_dtype)

    a_tile = _nbytes((tm, tk), a.dtype)
    b_tile = _nbytes((tk, tn), b.dtype)
    o_tile = _nbytes((tm, tn), out_dtype)
    est = 2 * (a_tile + b_tile + o_tile) + 3 * tm * tn * 4 + (a_tile + b_tile)
    grid = (gn, gm, gk) if n_outer else (gm, gn, gk)
    operands, in_specs = [a, b], [a_spec, b_spec]
    if after is not None:
        operands.append(after)
        in_specs.append(pl.BlockSpec(memory_space=pl.ANY))
    return pl.pallas_call(
        body, name=name, grid=grid, in_specs=in_specs, out_specs=o_spec, out_shape=out_shape,
        scratch_shapes=[] if gk == 1 else [pltpu.VMEM((tm, tn), F32)],
        compiler_params=_params(("parallel", "parallel", "arbitrary"), est),
    )(*operands)


def _row_spec(tr, width, col_block=0):
    return pl.BlockSpec((tr, width), lambda i: (i, col_block))


def _bcast_spec(width):
    return pl.BlockSpec((1, width), lambda i: (0, 0))


def _rmsnorm_fwd(x, gain, name, tr):
    rows, width = x.shape

    def body(x_ref, g_ref, o_ref):
        xv = x_ref[...]
        r = lax.rsqrt(jnp.mean(xv * xv, axis=-1, keepdims=True) + RMS_EPS)
        o_ref[...] = (xv * r * g_ref[...]).astype(BF16)

    return pl.pallas_call(
        body, name=name, grid=(rows // tr,), in_specs=[_row_spec(tr, width), _bcast_spec(width)],
        out_specs=_row_spec(tr, width), out_shape=jax.ShapeDtypeStruct((rows, width), BF16),
        compiler_params=_params(("parallel",), 6 * tr * width * 4),
    )(x, gain)


def _residual_rmsnorm_fwd(x, u, gain, name, tr):
    rows, width = x.shape

    def body(x_ref, u_ref, g_ref, x1_ref, h_ref):
        xv = x_ref[...] + u_ref[...]
        x1_ref[...] = xv
        r = lax.rsqrt(jnp.mean(xv * xv, axis=-1, keepdims=True) + RMS_EPS)
        h_ref[...] = (xv * r * g_ref[...]).astype(BF16)

    return pl.pallas_call(
        body, name=name, grid=(rows // tr,),
        in_specs=[_row_spec(tr, width), _row_spec(tr, width), _bcast_spec(width)],
        out_specs=[_row_spec(tr, width), _row_spec(tr, width)],
        out_shape=[jax.ShapeDtypeStruct((rows, width), F32), jax.ShapeDtypeStruct((rows, width), BF16)],
        compiler_params=_params(("parallel",), 10 * tr * width * 4),
    )(x, u, gain)


def _rmsnorm_bwd(x, dh, dres, gain, name, tr):
    rows, width = x.shape

    def body(x_ref, dh_ref, dres_ref, g_ref, dx_ref, dxb_ref, dg_ref):
        xv = x_ref[...]
        r = lax.rsqrt(jnp.mean(xv * xv, axis=-1, keepdims=True) + RMS_EPS)
        xhat = xv * r
        dhv = dh_ref[...]
        dyg = dhv * g_ref[...]
        dx = dres_ref[...] + r * (dyg - xhat * jnp.mean(dyg * xhat, axis=-1, keepdims=True))
        dx_ref[...] = dx
        dxb_ref[...] = dx.astype(BF16)
        part = jnp.sum(dhv * xhat, axis=0, keepdims=True)

        @pl.when(pl.program_id(0) == 0)
        def _():
            dg_ref[...] = part

        @pl.when(pl.program_id(0) > 0)
        def _():
            dg_ref[...] += part

    return pl.pallas_call(
        body, name=name, grid=(rows // tr,),
        in_specs=[_row_spec(tr, width), _row_spec(tr, width), _row_spec(tr, width), _bcast_spec(width)],
        out_specs=[_row_spec(tr, width), _row_spec(tr, width), _bcast_spec(width)],
        out_shape=[jax.ShapeDtypeStruct((rows, width), F32), jax.ShapeDtypeStruct((rows, width), BF16),
                   jax.ShapeDtypeStruct((1, width), F32)],
        compiler_params=_params(("arbitrary",), 14 * tr * width * 4),
    )(x, dh, dres, gain)


def _gain_grad(x, dh, name):
    rows, width = x.shape

    def body(x_ref, dh_ref, dg_ref):
        xv = x_ref[...]
        r = lax.rsqrt(jnp.mean(xv * xv, axis=-1, keepdims=True) + RMS_EPS)
        dg_ref[...] = jnp.sum(dh_ref[...] * xv * r, axis=0, keepdims=True)

    return pl.pallas_call(
        body, name=name, grid=(1,), in_specs=[_row_spec(rows, width), _row_spec(rows, width)],
        out_specs=_bcast_spec(width), out_shape=jax.ShapeDtypeStruct((1, width), F32),
        compiler_params=_params(("arbitrary",), 6 * rows * width * 4),
    )(x, dh)


def _lb_table(logits, name):
    slots, width = logits.shape

    def body(l_ref, o_ref):
        lv = l_ref[...]
        mx = jnp.max(lv, axis=0, keepdims=True)
        e = jnp.exp(lv - mx)
        o_ref[...] = e[0:1, :] / jnp.sum(e, axis=0, keepdims=True)

    return pl.pallas_call(
        body, name=name, grid=(1,), in_specs=[pl.BlockSpec((slots, width), lambda i: (0, 0))],
        out_specs=_bcast_spec(width), out_shape=jax.ShapeDtypeStruct((1, width), F32),
    )(logits)


def _gate_merge_fwd(proj, t_hg, t_da, t_mem, tr):
    w = D_MODEL

    def body(ghg_ref, gda_ref, gmem_ref, thg_ref, tda_ref, tmem_ref, o_ref):
        acc = _sigmoid(ghg_ref[...]) * thg_ref[...]
        acc += _sigmoid(gda_ref[...]) * tda_ref[...]
        acc += _sigmoid(gmem_ref[...]) * tmem_ref[...]
        o_ref[...] = acc.astype(BF16)

    return pl.pallas_call(
        body, name="gate_merge_fwd", grid=(SEQ // tr,),
        in_specs=[_row_spec(tr, w, 10), _row_spec(tr, w, 11), _row_spec(tr, w, 12),
                  _row_spec(tr, w), _row_spec(tr, w), _row_spec(tr, w)],
        out_specs=_row_spec(tr, w), out_shape=jax.ShapeDtypeStruct((SEQ, w), BF16),
        compiler_params=_params(("parallel",), 16 * tr * w * 4),
    )(proj, proj, proj, t_hg, t_da, t_mem)


def _gate_merge_bwd(proj, t_hg, t_da, t_mem, dmerged, tr):
    w = D_MODEL

    def body(ghg_ref, gda_ref, gmem_ref, thg_ref, tda_ref, tmem_ref, dm_ref, dthg_ref, dtda_ref, dtmem_ref, dg_ref):
        dm = dm_ref[...]
        for b, (g_ref, t_ref, dt_ref) in enumerate(((ghg_ref, thg_ref, dthg_ref), (gda_ref, tda_ref, dtda_ref),
                                                    (gmem_ref, tmem_ref, dtmem_ref))):
            s = _sigmoid(g_ref[...])
            dt_ref[...] = (s * dm).astype(BF16)
            dg_ref[:, b * w:(b + 1) * w] = (dm * t_ref[...] * s * (1.0 - s)).astype(BF16)

    return pl.pallas_call(
        body, name="gate_merge_bwd", grid=(SEQ // tr,),
        in_specs=[_row_spec(tr, w, 10), _row_spec(tr, w, 11), _row_spec(tr, w, 12),
                  _row_spec(tr, w), _row_spec(tr, w), _row_spec(tr, w), _row_spec(tr, w)],
        out_specs=[_row_spec(tr, w), _row_spec(tr, w), _row_spec(tr, w), _row_spec(tr, 3 * w)],
        out_shape=[jax.ShapeDtypeStruct((SEQ, w), BF16)] * 3 + [jax.ShapeDtypeStruct((SEQ, 3 * w), BF16)],
        compiler_params=_params(("parallel",), 24 * tr * w * 4),
    )(proj, proj, proj, t_hg, t_da, t_mem, dmerged)


def _swiglu_fwd(ab, tr):
    def body(ab_ref, o_ref):
        a = ab_ref[:, :D_FF]
        b = ab_ref[:, D_FF:]
        o_ref[...] = (a * _sigmoid(a) * b).astype(BF16)

    return pl.pallas_call(
        body, name="swiglu_fwd", grid=(SEQ // tr,), in_specs=[_row_spec(tr, 2 * D_FF)],
        out_specs=_row_spec(tr, D_FF), out_shape=jax.ShapeDtypeStruct((SEQ, D_FF), BF16),
        compiler_params=_params(("parallel",), 8 * tr * 2 * D_FF * 4),
    )(ab)


def _swiglu_bwd(ab, dact, tr):
    def body(ab_ref, d_ref, o_ref):
        a = ab_ref[:, :D_FF]
        b = ab_ref[:, D_FF:]
        d = d_ref[...]
        s = _sigmoid(a)
        silu = a * s
        o_ref[:, :D_FF] = (d * b * (s + silu * (1.0 - s))).astype(BF16)
        o_ref[:, D_FF:] = (d * silu).astype(BF16)

    return pl.pallas_call(
        body, name="swiglu_bwd", grid=(SEQ // tr,), in_specs=[_row_spec(tr, 2 * D_FF), _row_spec(tr, D_FF)],
        out_specs=_row_spec(tr, 2 * D_FF), out_shape=jax.ShapeDtypeStruct((SEQ, 2 * D_FF), BF16),
        compiler_params=_params(("parallel",), 10 * tr * 2 * D_FF * 4),
    )(ab, dact)


def _loss_head(x1, z, target, tr):
    w = D_MODEL

    def body(x_ref, z_ref, t_ref, dy_ref, dyb_ref, loss_ref, acc_ref):
        err = x_ref[...] + z_ref[...] - t_ref[...]
        dy = err * (1.0 / w)
        dy_ref[...] = dy
        dyb_ref[...] = dy.astype(BF16)
        part = jnp.sum(err * err, axis=0, keepdims=True)

        @pl.when(pl.program_id(0) == 0)
        def _():
            acc_ref[...] = part

        @pl.when(pl.program_id(0) > 0)
        def _():
            acc_ref[...] += part

        @pl.when(pl.program_id(0) == SEQ // tr - 1)
        def _():
            total = jnp.sum(acc_ref[...], axis=1, keepdims=True) * (0.5 / w)
            loss_ref[...] = jnp.broadcast_to(total, (1, LANE))

    return pl.pallas_call(
        body, name="loss_head", grid=(SEQ // tr,),
        in_specs=[_row_spec(tr, w), _row_spec(tr, w), _row_spec(tr, w)],
        out_specs=[_row_spec(tr, w), _row_spec(tr, w), _bcast_spec(LANE)],
        out_shape=[jax.ShapeDtypeStruct((SEQ, w), F32), jax.ShapeDtypeStruct((SEQ, w), BF16),
                   jax.ShapeDtypeStruct((1, LANE), F32)],
        scratch_shapes=[pltpu.VMEM((1, w), F32)],
        compiler_params=_params(("arbitrary",), 12 * tr * w * 4),
    )(x1, z, target)


GLA_ROWS = 256
GLA_CPB = GLA_ROWS // HG_CHUNK
GLA_NBLK = SEQ // GLA_ROWS
GLA_NCK = SEQ // HG_CHUNK


def _dot_split(m, xv, dims):
    hi = xv.astype(BF16)
    r1 = xv - hi.astype(F32)
    mid = r1.astype(BF16)
    lo = (r1 - mid.astype(F32)).astype(BF16)
    dot = lambda t: lax.dot_general(m, t, dims, preferred_element_type=F32)
    return (dot(lo) + dot(mid)) + dot(hi)


def _gla_block(qc, fc, lbv, masks):
    mask, maskb, direction = masks
    sq = _sigmoid(qc)
    q = qc * sq * HG_SCALE
    sf = _sigmoid(fc)
    forget = lbv + (1.0 - lbv) * sf
    k = 1.0 - forget
    logf = jnp.log(forget)
    b = _dot_split(maskb, logf, _NN)
    ends = []
    for j in range(GLA_CPB):
        lo, hi = j * HG_CHUNK, (j + 1) * HG_CHUNK
        end = jnp.where(direction == 0, b[hi - 1:hi, :], b[lo:lo + 1, :])
        ends.append(jnp.broadcast_to(end, (HG_CHUNK, HEAD_DIM)))
    bt = jnp.concatenate(ends, axis=0)
    eb = jnp.exp(b)
    qt = q * eb
    kt = k * jnp.exp(-b)
    kh = k * jnp.exp(bt - b)
    a = jnp.where(mask, _dot(qt, kt, _NT), 0.0)
    return dict(sq=sq, sf=sf, forget=forget, k=k, b=b, bt=bt, eb=eb, qt=qt, kt=kt, kh=kh, a=a)


def _gla_masks(direction):
    row = lax.broadcasted_iota(jnp.int32, (GLA_ROWS, GLA_ROWS), 0)
    col = lax.broadcasted_iota(jnp.int32, (GLA_ROWS, GLA_ROWS), 1)
    same = (row // HG_CHUNK) == (col // HG_CHUNK)
    mask = jnp.logical_and(same, jnp.where(direction == 0, row - col, col - row) >= 0)
    return mask, jnp.where(mask, 1.0, 0.0).astype(BF16), direction


def _gla_chunk_rows(j):
    return slice(j * HG_CHUNK, (j + 1) * HG_CHUNK)


def _head_spec(col_block0):
    return pl.BlockSpec((SEQ, HEAD_DIM), lambda h, d: (0, col_block0 + h))


def _gla_fwd(proj, lb, gain):
    nck = GLA_NCK

    def body(q_ref, f_ref, v_ref, g_ref, lb_ref, gain_ref, ohg_ref, opre_ref, st_ref, qt_scr, cs_scr, dec_scr):
        d = pl.program_id(1)
        masks = _gla_masks(d)
        lbv = lb_ref[...]

        def intra(s, carry):
            rows = pl.ds(pl.multiple_of(s * GLA_ROWS, GLA_ROWS), GLA_ROWS)
            ck = _gla_block(q_ref[rows, :], f_ref[rows, :], lbv, masks)
            v = v_ref[rows, :]
            o = _dot(ck["a"], v, _NN)

            @pl.when(d == 0)
            def _():
                opre_ref[rows, :] = o

            @pl.when(d == 1)
            def _():
                opre_ref[rows, :] += o

            qt_scr[rows, :] = ck["qt"].astype(BF16)
            for j in range(GLA_CPB):
                c = s * GLA_CPB + j
                cr = _gla_chunk_rows(j)
                cs_scr[c] = _dot(v[cr, :], ck["kh"][cr, :], _TN)
                dec_scr[c] = jnp.exp(ck["bt"][j * HG_CHUNK:j * HG_CHUNK + 8, :])
            return carry

        lax.fori_loop(0, GLA_NBLK, intra, 0)

        def scan(i, st):
            c = jnp.where(d == 0, i, nck - 1 - i)
            st_ref[c] = st
            return st * dec_scr[c][0:1, :] + cs_scr[c]

        lax.fori_loop(0, nck, scan, jnp.zeros((HEAD_DIM, HEAD_DIM), F32), unroll=4)

        def inter(c, carry):
            rows = pl.ds(pl.multiple_of(c * HG_CHUNK, HG_CHUNK), HG_CHUNK)
            opre_ref[rows, :] += _dot(qt_scr[rows, :], st_ref[c], _NT)
            return carry

        lax.fori_loop(0, nck, inter, 0, unroll=4)

        @pl.when(d == 1)
        def _():
            o = opre_ref[...]
            r = lax.rsqrt(jnp.mean(o * o, axis=-1, keepdims=True) + RMS_EPS)
            g = g_ref[...]
            ohg_ref[...] = (o * r * gain_ref[...] * (g * _sigmoid(g))).astype(BF16)

    blk = SEQ * HEAD_DIM * 4
    return pl.pallas_call(
        body, name="gla_fwd", grid=(HG_HEADS, 2),
        in_specs=[_head_spec(CB_HG_Q),
                  pl.BlockSpec((SEQ, HEAD_DIM), lambda h, d: (0, CB_F + 8 * d + h)),
                  _head_spec(CB_HG_I), _head_spec(CB_HG_G),
                  pl.BlockSpec((None, None, 1, HEAD_DIM), lambda h, d: (d, h, 0, 0)),
                  pl.BlockSpec((1, HEAD_DIM), lambda h, d: (0, 0))],
        out_specs=[_head_spec(0), _head_spec(0),
                   pl.BlockSpec((None, None, nck, HEAD_DIM, HEAD_DIM), lambda h, d: (h, d, 0, 0, 0))],
        out_shape=[jax.ShapeDtypeStruct((SEQ, D_MODEL), BF16), jax.ShapeDtypeStruct((SEQ, D_MODEL), F32),
                   jax.ShapeDtypeStruct((HG_HEADS, 2, nck, HEAD_DIM, HEAD_DIM), F32)],
        scratch_shapes=[pltpu.VMEM((SEQ, HEAD_DIM), BF16), pltpu.VMEM((nck, HEAD_DIM, HEAD_DIM), F32),
                        pltpu.VMEM((nck, 8, HEAD_DIM), F32)],
        compiler_params=_params(("parallel", "arbitrary"), 8 * blk + 3 * blk + 2 * blk + 2 * blk + blk // 2),
    )(proj, proj, proj, proj, lb, gain)


def _gla_bwd(proj, lb, gain, o_pre, states, do_hg):
    nck = GLA_NCK
    do, dg, dgain = _gla_bwd_norm(proj, gain, o_pre, do_hg, 256)

    def body(q_ref, f_ref, v_ref, lb_ref, do_ref, st_ref, dq_ref, df_ref, dv_ref, dlb_ref,
             dq_acc, dv_acc, dst_scr, cs_scr, dec_scr):
        d = pl.program_id(1)
        masks = _gla_masks(d)
        mask, maskb, _ = masks
        lbv = lb_ref[...]

        def intra(s, carry):
            rows = pl.ds(pl.multiple_of(s * GLA_ROWS, GLA_ROWS), GLA_ROWS)
            ck = _gla_block(q_ref[rows, :], f_ref[rows, :], lbv, masks)
            doc = do_ref[rows, :]
            for j in range(GLA_CPB):
                c = s * GLA_CPB + j
                cr = _gla_chunk_rows(j)
                cs_scr[c] = _dot(doc[cr, :], ck["qt"][cr, :], _TN)
                dec_scr[c] = jnp.exp(ck["bt"][j * HG_CHUNK:j * HG_CHUNK + 8, :])
            return carry

        lax.fori_loop(0, GLA_NBLK, intra, 0)

        def scan(i, dst):
            c = jnp.where(d == 0, nck - 1 - i, i)
            dst_scr[c] = dst
            return dst * dec_scr[c][0:1, :] + cs_scr[c]

        lax.fori_loop(0, nck, scan, jnp.zeros((HEAD_DIM, HEAD_DIM), F32), unroll=4)

        def grads(s, dlb):
            rows = pl.ds(pl.multiple_of(s * GLA_ROWS, GLA_ROWS), GLA_ROWS)
            qc = q_ref[rows, :]
            ck = _gla_block(qc, f_ref[rows, :], lbv, masks)
            v = v_ref[rows, :]
            doc = do_ref[rows, :]
            qt, kt, kh, a = ck["qt"], ck["kt"], ck["kh"], ck["a"]
            da = jnp.where(mask, _dot(doc, v, _NT), 0.0)
            dqt_i = _dot(da, kt, _NN)
            dkt = _dot(da, qt, _TN)
            dv_i = _dot(a, doc, _TN)
            dqt_p, dv_p, dkh_p, dbt_p = [], [], [], []
            for j in range(GLA_CPB):
                c = s * GLA_CPB + j
                cr = _gla_chunk_rows(j)
                st_in = st_ref[c]
                dst = dst_scr[c]
                dqt_p.append(dqt_i[cr, :] + _dot(doc[cr, :], st_in, _NN))
                dv_p.append(dv_i[cr, :] + _dot(kh[cr, :], dst, _NT))
                dkh_j = _dot(v[cr, :], dst, _NN)
                dkh_p.append(dkh_j)
                dbt_j = (dec_scr[c][0:1, :] * jnp.sum(dst * st_in, axis=0, keepdims=True)
                         + jnp.sum(dkh_j * kh[cr, :], axis=0, keepdims=True))
                dbt_p.append(jnp.broadcast_to(dbt_j, (HG_CHUNK, HEAD_DIM)))
            dqt = jnp.concatenate(dqt_p, axis=0)
            dv = jnp.concatenate(dv_p, axis=0)
            dkh = jnp.concatenate(dkh_p, axis=0)
            dbt = jnp.concatenate(dbt_p, axis=0)
            db = dqt * qt - dkt * kt - dkh * kh
            dq = dqt * ck["eb"]
            dk = dkt * jnp.exp(-ck["b"]) + dkh * jnp.exp(ck["bt"] - ck["b"])
            dlogf = _dot_split(maskb, db, _TN) + dbt
            dforget = dlogf / ck["forget"] - dk
            sf = ck["sf"]
            df_ref[rows, :] = (dforget * (1.0 - lbv) * sf * (1.0 - sf)).astype(BF16)
            sq = ck["sq"]
            dqc = dq * HG_SCALE * (sq + qc * sq * (1.0 - sq))

            @pl.when(d == 0)
            def _():
                dq_acc[rows, :] = dqc
                dv_acc[rows, :] = dv

            @pl.when(d == 1)
            def _():
                dq_acc[rows, :] += dqc
                dv_acc[rows, :] += dv

            return dlb + jnp.sum(dforget * (1.0 - sf), axis=0, keepdims=True)

        dlb_ref[...] = lax.fori_loop(0, GLA_NBLK, grads, jnp.zeros((1, HEAD_DIM), F32))

        @pl.when(d == 1)
        def _():
            dq_ref[...] = dq_acc[...].astype(BF16)
            dv_ref[...] = dv_acc[...].astype(BF16)

    blk = SEQ * HEAD_DIM * 4
    bshape = jax.ShapeDtypeStruct((SEQ, D_MODEL), BF16)
    state_bytes = nck * HEAD_DIM * HEAD_DIM * 4
    dq, df, dv, dlb = pl.pallas_call(
        body, name="gla_bwd", grid=(HG_HEADS, 2),
        in_specs=[_head_spec(CB_HG_Q),
                  pl.BlockSpec((SEQ, HEAD_DIM), lambda h, d: (0, CB_F + 8 * d + h)),
                  _head_spec(CB_HG_I),
                  pl.BlockSpec((None, None, 1, HEAD_DIM), lambda h, d: (d, h, 0, 0)),
                  _head_spec(0),
                  pl.BlockSpec((None, None, nck, HEAD_DIM, HEAD_DIM), lambda h, d: (h, d, 0, 0, 0))],
        out_specs=[_head_spec(0),
                   pl.BlockSpec((None, SEQ, HEAD_DIM), lambda h, d: (d, 0, h)),
                   _head_spec(0),
                   pl.BlockSpec((None, None, 1, HEAD_DIM), lambda h, d: (d, h, 0, 0))],
        out_shape=[bshape, jax.ShapeDtypeStruct((2, SEQ, D_MODEL), BF16), bshape,
                   jax.ShapeDtypeStruct((2, HG_HEADS, 1, HEAD_DIM), F32)],
        scratch_shapes=[pltpu.VMEM((SEQ, HEAD_DIM), F32)] * 2
        + [pltpu.VMEM((nck, HEAD_DIM, HEAD_DIM), F32)] * 2 + [pltpu.VMEM((nck, 8, HEAD_DIM), F32)],
        compiler_params=_params(("parallel", "arbitrary"), 8 * blk + 4 * state_bytes + 3 * blk + 2 * blk),
    )(proj, proj, proj, lb, do, states)
    return dq, df, dv, dg, dlb, dgain


def _gla_bwd_norm(proj, gain, o_pre, do_hg, tr):
    w = D_MODEL

    def body(g_ref, gain_ref, opre_ref, dohg_ref, do_ref, dg_ref, dgain_ref):
        gainv = gain_ref[...]
        part = jnp.zeros((1, HEAD_DIM), F32)
        for h in range(HG_HEADS):
            hs = slice(h * HEAD_DIM, (h + 1) * HEAD_DIM)
            o = opre_ref[:, hs]
            r = lax.rsqrt(jnp.mean(o * o, axis=-1, keepdims=True) + RMS_EPS)
            ohat = o * r
            g = g_ref[:, hs]
            sg = _sigmoid(g)
            silu = g * sg
            dout = dohg_ref[:, hs]
            dg_ref[:, hs] = (dout * ohat * gainv * (sg + silu * (1.0 - sg))).astype(BF16)
            dy = dout * silu
            part = part + jnp.sum(dy * ohat, axis=0, keepdims=True)
            dn = dy * gainv
            do_ref[:, hs] = r * (dn - ohat * jnp.mean(dn * ohat, axis=-1, keepdims=True))

        @pl.when(pl.program_id(0) == 0)
        def _():
            dgain_ref[...] = part

        @pl.when(pl.program_id(0) > 0)
        def _():
            dgain_ref[...] += part

    return pl.pallas_call(
        body, name="gla_bwd_norm", grid=(SEQ // tr,),
        in_specs=[_row_spec(tr, w, CB_HG_G * LANE // w), _bcast_spec(HEAD_DIM), _row_spec(tr, w), _row_spec(tr, w)],
        out_specs=[_row_spec(tr, w), _row_spec(tr, w), _bcast_spec(HEAD_DIM)],
        out_shape=[jax.ShapeDtypeStruct((SEQ, w), F32), jax.ShapeDtypeStruct((SEQ, w), BF16),
                   jax.ShapeDtypeStruct((1, HEAD_DIM), F32)],
        compiler_params=_params(("arbitrary",), 16 * tr * w * 4),
    )(proj, gain, o_pre, do_hg)


def _da_residue_rows(r, n0, size, d):
    if d == 1:
        return pl.ds(pl.multiple_of(n0, 8), size)
    return pl.ds(r + n0 * d, size, stride=d)


def _da_rmsnorm(x, gain):
    r = lax.rsqrt(jnp.mean(x * x, axis=-1, keepdims=True) + RMS_EPS)
    return x * r, r, x * r * gain


def _da_scores(qn_scr, kn_scr, slope, i, ld):
    w0 = jnp.clip(i * DA_QB - DA_RADIUS, 0, ld - DA_WIN)
    w0 = pl.multiple_of(w0, DA_RADIUS)
    qrows = pl.ds(pl.multiple_of(i * DA_QB, DA_QB), DA_QB)
    win = pl.ds(w0, DA_WIN)
    qb = qn_scr[qrows, :]
    kw = kn_scr[win, :]
    s = _dot(qb, kw, _NT) * DA_SCALE
    qpos = i * DA_QB + lax.broadcasted_iota(jnp.int32, (DA_QB, DA_WIN), 0)
    kpos = w0 + lax.broadcasted_iota(jnp.int32, (DA_QB, DA_WIN), 1)
    arel = jnp.abs(kpos - qpos)
    s = s - slope * arel.astype(F32)
    s = jnp.where(arel <= DA_RADIUS, s, NEG_INF)
    return s, qb, kw, qrows, win


def _da_slopes(group):
    d = DA_DILATIONS[group]
    sl = _alibi_slopes(DA_HEADS)[4 * group:4 * group + 4] * d
    return jnp.asarray(np.broadcast_to(sl[:, None, None], (4, 1, LANE)).copy())


def _da_fwd(proj, gq, gk, group):
    d = DA_DILATIONS[group]
    ld = SEQ // d
    nqb = ld // DA_QB

    def body(q_ref, k_ref, v_ref, gq_ref, gk_ref, sl_ref, o_ref, lse_ref, qn_scr, kn_scr, v_scr):
        slope = sl_ref[:, 0:1]

        def residue(r, carry):
            sel = _da_residue_rows(r, 0, ld, d)
            qn_scr[...] = _da_rmsnorm(q_ref[sel, :], gq_ref[...])[2].astype(BF16)
            kn_scr[...] = _da_rmsnorm(k_ref[sel, :], gk_ref[...])[2].astype(BF16)
            v_scr[...] = v_ref[sel, :].astype(BF16)

            def step(i, c2):
                s, _, _, _, win = _da_scores(qn_scr, kn_scr, slope, i, ld)
                m = jnp.max(s, axis=-1, keepdims=True)
                p = jnp.exp(s - m)
                l = jnp.sum(p, axis=-1, keepdims=True)
                out = _da_residue_rows(r, i * DA_QB, DA_QB, d)
                o_ref[out, :] = _dot(p, v_scr[win, :], _NN) / l
                lse_ref[out, :] = jnp.broadcast_to(m + jnp.log(l), (DA_QB, HEAD_DIM))
                return c2

            return lax.fori_loop(0, nqb, step, carry)

        lax.fori_loop(0, d, residue, 0)

    seq_spec = lambda base: pl.BlockSpec((SEQ, HEAD_DIM), lambda h: (0, base + 4 * group + h))
    out_spec = pl.BlockSpec((SEQ, HEAD_DIM), lambda h: (0, h))
    gain_spec = pl.BlockSpec((1, HEAD_DIM), lambda h: (0, 0))
    blk = SEQ * HEAD_DIM * 4
    return pl.pallas_call(
        body, name=f"da_fwd_g{group}", grid=(DA_HEADS_PER_GROUP,),
        in_specs=[seq_spec(CB_DA_Q), seq_spec(CB_DA_K), seq_spec(CB_DA_V), gain_spec, gain_spec,
                  pl.BlockSpec((None, 1, LANE), lambda h: (h, 0, 0))],
        out_specs=[out_spec, out_spec],
        out_shape=[jax.ShapeDtypeStruct((SEQ, DA_WIDTH), F32)] * 2,
        scratch_shapes=[pltpu.VMEM((ld, HEAD_DIM), BF16)] * 3,
        compiler_params=_params(("parallel",), 10 * blk + 2 * blk),
    )(proj, proj, proj, gq, gk, _da_slopes(group))


def _da_merge(os, lses, tr):
    w = DA_WIDTH

    def body(o0, o1, o2, l0, l1, l2, ob_ref, of_ref, lse_ref):
        la, lb_, lc = l0[...], l1[...], l2[...]
        m = jnp.maximum(jnp.maximum(la, lb_), lc)
        ea, eb, ec = jnp.exp(la - m), jnp.exp(lb_ - m), jnp.exp(lc - m)
        tot = ea + eb + ec
        o = (ea * o0[...] + eb * o1[...] + ec * o2[...]) / tot
        of_ref[...] = o
        ob_ref[...] = o.astype(BF16)
        lse_ref[...] = m + jnp.log(tot)

    return pl.pallas_call(
        body, name="da_merge", grid=(SEQ // tr,), in_specs=[_row_spec(tr, w)] * 6,
        out_specs=[_row_spec(tr, w)] * 3,
        out_shape=[jax.ShapeDtypeStruct((SEQ, w), BF16), jax.ShapeDtypeStruct((SEQ, w), F32),
                   jax.ShapeDtypeStruct((SEQ, w), F32)],
        compiler_params=_params(("parallel",), 24 * tr * w * 4),
    )(*os, *lses)


def _da_rowdot(do, o, tr):
    w = DA_WIDTH

    def body(do_ref, o_ref, out_ref):
        prod = do_ref[...] * o_ref[...]
        for h in range(w // HEAD_DIM):
            sl = slice(h * HEAD_DIM, (h + 1) * HEAD_DIM)
            out_ref[:, sl] = jnp.broadcast_to(jnp.sum(prod[:, sl], axis=-1, keepdims=True), (tr, HEAD_DIM))

    return pl.pallas_call(
        body, name="da_rowdot", grid=(SEQ // tr,), in_specs=[_row_spec(tr, w)] * 2,
        out_specs=_row_spec(tr, w), out_shape=jax.ShapeDtypeStruct((SEQ, w), F32),
        compiler_params=_params(("parallel",), 10 * tr * w * 4),
    )(do, o)


def _da_bwd(proj, gq, gk, do, lse, dd, group):
    d = DA_DILATIONS[group]
    ld = SEQ // d
    nqb = ld // DA_QB

    def body(q_ref, k_ref, v_ref, gq_ref, gk_ref, sl_ref, do_ref, lse_ref, dd_ref,
             dq_ref, dk_ref, dv_ref, dgq_ref, dgk_ref,
             qn_scr, kn_scr, v_scr, dqn_scr, dkn_scr, dvr_scr, dq_scr, dk_scr, dv_scr):
        gqv, gkv = gq_ref[...], gk_ref[...]
        slope = sl_ref[:, 0:1]

        def residue(r, carry):
            sel = _da_residue_rows(r, 0, ld, d)
            qn_scr[...] = _da_rmsnorm(q_ref[sel, :], gqv)[2].astype(BF16)
            kn_scr[...] = _da_rmsnorm(k_ref[sel, :], gkv)[2].astype(BF16)
            v_scr[...] = v_ref[sel, :].astype(BF16)
            dkn_scr[...] = jnp.zeros_like(dkn_scr)
            dvr_scr[...] = jnp.zeros_like(dvr_scr)

            def step(i, c2):
                s, qb, kw, qrows, win = _da_scores(qn_scr, kn_scr, slope, i, ld)
                src = _da_residue_rows(r, i * DA_QB, DA_QB, d)
                p = jnp.exp(s - lse_ref[src, :][:, 0:1])
                dob = do_ref[src, :]
                dvr_scr[win, :] += _dot(p, dob, _TN)
                dp = _dot(dob, v_scr[win, :], _NT)
                ds = p * (dp - dd_ref[src, :][:, 0:1]) * DA_SCALE
                dqn_scr[qrows, :] = _dot(ds, kw, _NN)
                dkn_scr[win, :] += _dot(ds, qb, _TN)
                return c2

            lax.fori_loop(0, nqb, step, 0)

            parts = []
            for x_ref, gv, dn_scr, dx_scr in ((q_ref, gqv, dqn_scr, dq_scr), (k_ref, gkv, dkn_scr, dk_scr)):
                hat, rstd, _ = _da_rmsnorm(x_ref[sel, :], gv)
                dn = dn_scr[...]
                dyg = dn * gv
                dx_scr[sel, :] = rstd * (dyg - hat * jnp.mean(dyg * hat, axis=-1, keepdims=True))
                parts.append(jnp.sum(dn * hat, axis=0, keepdims=True))
            dv_scr[sel, :] = dvr_scr[...]
            return carry[0] + parts[0], carry[1] + parts[1]

        zero = jnp.zeros((1, HEAD_DIM), F32)
        pq, pk = lax.fori_loop(0, d, residue, (zero, zero))

        @pl.when(pl.program_id(0) == 0)
        def _():
            dgq_ref[...] = pq
            dgk_ref[...] = pk

        @pl.when(pl.program_id(0) > 0)
        def _():
            dgq_ref[...] += pq
            dgk_ref[...] += pk

        dq_ref[...] = dq_scr[...].astype(BF16)
        dk_ref[...] = dk_scr[...].astype(BF16)
        dv_ref[...] = dv_scr[...].astype(BF16)

    seq_spec = lambda base: pl.BlockSpec((SEQ, HEAD_DIM), lambda h: (0, base + 4 * group + h))
    out_spec = pl.BlockSpec((SEQ, HEAD_DIM), lambda h: (0, h))
    gain_spec = pl.BlockSpec((1, HEAD_DIM), lambda h: (0, 0))
    oshape = jax.ShapeDtypeStruct((SEQ, DA_WIDTH), BF16)
    gshape = jax.ShapeDtypeStruct((1, HEAD_DIM), F32)
    blk = SEQ * HEAD_DIM * 4
    return pl.pallas_call(
        body, name=f"da_bwd_g{group}", grid=(DA_HEADS_PER_GROUP,),
        in_specs=[seq_spec(CB_DA_Q), seq_spec(CB_DA_K), seq_spec(CB_DA_V), gain_spec, gain_spec,
                  pl.BlockSpec((None, 1, LANE), lambda h: (h, 0, 0)), out_spec, out_spec, out_spec],
        out_specs=[out_spec, out_spec, out_spec, gain_spec, gain_spec],
        out_shape=[oshape, oshape, oshape, gshape, gshape],
        scratch_shapes=[pltpu.VMEM((ld, HEAD_DIM), BF16)] * 3 + [pltpu.VMEM((ld, HEAD_DIM), F32)] * 3
        + [pltpu.VMEM((SEQ, HEAD_DIM), F32)] * 3,
        compiler_params=_params(("arbitrary",), 12 * blk + 3 * blk + 3 * blk + 3 * blk),
    )(proj, proj, proj, gq, gk, _da_slopes(group), do, lse, dd)


def _mem_softmax(q, k, gq, gk):
    qhat, rq, qn = _da_rmsnorm(q, gq)
    khat, rk, kn = _da_rmsnorm(k, gk)
    s = _dot(qn, kn, _NT) * MEM_SCALE
    m = jnp.max(s, axis=-1, keepdims=True)
    e = jnp.exp(s - m)
    p = e / jnp.sum(e, axis=-1, keepdims=True)
    return p, (qhat, rq, qn), (khat, rk, kn)


def _mem_fwd(proj, kv, gq, gk, tq):
    def body(q_ref, k_ref, v_ref, gq_ref, gk_ref, o_ref):
        p, _, _ = _mem_softmax(q_ref[...], k_ref[...], gq_ref[...], gk_ref[...])
        o_ref[...] = _dot(p, v_ref[...], _NN).astype(BF16)

    gain_spec = pl.BlockSpec((1, HEAD_DIM), lambda h, i: (0, 0))
    return pl.pallas_call(
        body, name="mem_fwd", grid=(MEM_HEADS, SEQ // tq),
        in_specs=[pl.BlockSpec((tq, HEAD_DIM), lambda h, i: (i, CB_MEM_Q + h)),
                  pl.BlockSpec((N_MEM, HEAD_DIM), lambda h, i: (0, h)),
                  pl.BlockSpec((N_MEM, HEAD_DIM), lambda h, i: (0, MEM_HEADS + h)), gain_spec, gain_spec],
        out_specs=pl.BlockSpec((tq, HEAD_DIM), lambda h, i: (i, h)),
        out_shape=jax.ShapeDtypeStruct((SEQ, MEM_WIDTH), BF16),
        compiler_params=_params(("parallel", "parallel"), 16 * tq * N_MEM * 4),
    )(proj, kv, kv, gq, gk)


def _mem_bwd(proj, kv, gq, gk, do, tq):
    nq = SEQ // tq

    def body(q_ref, k_ref, v_ref, gq_ref, gk_ref, do_ref, dq_ref, dk_ref, dv_ref, dgq_ref, dgk_ref, dkn_scr):
        h, i = pl.program_id(0), pl.program_id(1)
        gqv, gkv = gq_ref[...], gk_ref[...]
        p, (qhat, rq, qn), (khat, rk, kn) = _mem_softmax(q_ref[...], k_ref[...], gqv, gkv)
        dob = do_ref[...]
        dvp = _dot(p, dob, _TN)
        dp = _dot(dob, v_ref[...], _NT)
        ds = p * (dp - jnp.sum(p * dp, axis=-1, keepdims=True)) * MEM_SCALE
        dqn = _dot(ds, kn, _NN)
        dknp = _dot(ds, qn, _TN)
        dyg = dqn * gqv
        dq_ref[...] = (rq * (dyg - qhat * jnp.mean(dyg * qhat, axis=-1, keepdims=True))).astype(BF16)
        dgq_part = jnp.sum(dqn * qhat, axis=0, keepdims=True)
        first = jnp.logical_and(h == 0, i == 0)

        @pl.when(first)
        def _():
            dgq_ref[...] = dgq_part

        @pl.when(jnp.logical_not(first))
        def _():
            dgq_ref[...] += dgq_part

        @pl.when(i == 0)
        def _():
            dv_ref[...] = dvp
            dkn_scr[...] = dknp

        @pl.when(i > 0)
        def _():
            dv_ref[...] += dvp
            dkn_scr[...] += dknp

        @pl.when(i == nq - 1)
        def _():
            dkn = dkn_scr[...]
            dkg = dkn * gkv
            dk_ref[...] = rk * (dkg - khat * jnp.mean(dkg * khat, axis=-1, keepdims=True))
            dgk_part = jnp.sum(dkn * khat, axis=0, keepdims=True)

            @pl.when(h == 0)
            def _():
                dgk_ref[...] = dgk_part

            @pl.when(h > 0)
            def _():
                dgk_ref[...] += dgk_part

    gain_spec = pl.BlockSpec((1, HEAD_DIM), lambda h, i: (0, 0))
    kvout = pl.BlockSpec((N_MEM, HEAD_DIM), lambda h, i: (0, h))
    return pl.pallas_call(
        body, name="mem_bwd", grid=(MEM_HEADS, nq),
        in_specs=[pl.BlockSpec((tq, HEAD_DIM), lambda h, i: (i, CB_MEM_Q + h)),
                  pl.BlockSpec((N_MEM, HEAD_DIM), lambda h, i: (0, h)),
                  pl.BlockSpec((N_MEM, HEAD_DIM), lambda h, i: (0, MEM_HEADS + h)), gain_spec, gain_spec,
                  pl.BlockSpec((tq, HEAD_DIM), lambda h, i: (i, h))],
        out_specs=[pl.BlockSpec((tq, HEAD_DIM), lambda h, i: (i, h)), kvout, kvout, gain_spec, gain_spec],
        out_shape=[jax.ShapeDtypeStruct((SEQ, MEM_WIDTH), BF16), jax.ShapeDtypeStruct((N_MEM, MEM_WIDTH), F32),
                   jax.ShapeDtypeStruct((N_MEM, MEM_WIDTH), F32), jax.ShapeDtypeStruct((1, HEAD_DIM), F32),
                   jax.ShapeDtypeStruct((1, HEAD_DIM), F32)],
        scratch_shapes=[pltpu.VMEM((N_MEM, HEAD_DIM), F32)],
        compiler_params=_params(("arbitrary", "arbitrary"), 24 * tq * N_MEM * 4),
    )(proj, kv, kv, gq, gk, do)


def _mesh_position():
    return lax.axis_index("x"), lax.axis_index("y"), lax.axis_index("c")


def _any_spec():
    return pl.BlockSpec(memory_space=pl.ANY)


def _all_gather(shards):
    n = len(shards)

    def body(*refs):
        ins, outs = refs[:n], refs[n:2 * n]
        send_sems, recv_sems, local_sems = refs[2 * n:]
        x, y, c = _mesh_position()
        me, sibling = (x, y, c), (x, y, 1 - c)
        chips = [(1 - x, y), (x, 1 - y), (1 - x, 1 - y)]

        def copy(w, k, block, to, src=None):
            px, py, pc = block
            rows = outs[w].at[4 * px + 2 * py + pc]
            return pltpu.make_async_remote_copy(
                src_ref=rows if src is None else src, dst_ref=rows,
                send_sem=send_sems.at[7 * w + k], recv_sem=recv_sems.at[7 * w + k],
                device_id=to, device_id_type=MESH)

        started = []
        for w in range(n):
            mine = pltpu.make_async_copy(ins[w], outs[w].at[4 * x + 2 * y + c], local_sems.at[w])
            mine.start()
            started.append(mine)
        sends = []
        for w in range(n):
            first = [copy(w, 0, me, sibling, src=ins[w])]
            first += [copy(w, 1 + j, me, (*chip, c), src=ins[w]) for j, chip in enumerate(chips)]
            for cp in first:
                cp.start()
            sends += first
        for w in range(n):
            for j, chip in enumerate(chips):
                copy(w, 1 + j, (*chip, c), me).wait_recv()
                passed = copy(w, 4 + j, (*chip, c), sibling)
                passed.start()
                sends.append(passed)
        for w in range(n):
            copy(w, 0, sibling, me).wait_recv()
            for j, chip in enumerate(chips):
                copy(w, 4 + j, (*chip, 1 - c), me).wait_recv()
        for cp in sends:
            cp.wait_send()
        for mine in started:
            mine.wait()

    return pl.pallas_call(
        body, name="weights_all_gather",
        in_specs=[_any_spec()] * n, out_specs=[_any_spec()] * n,
        out_shape=[jax.ShapeDtypeStruct((N_DEV,) + s.shape, s.dtype) for s in shards],
        scratch_shapes=[pltpu.SemaphoreType.DMA((7 * n,)), pltpu.SemaphoreType.DMA((7 * n,)),
                        pltpu.SemaphoreType.DMA((n,))],
    )(*shards)


def _chip_of(j, x, y):
    return (1 - x if j & 1 else x, 1 - y if j & 2 else y)


_HBM_SPEC = pl.BlockSpec(memory_space=pltpu.HBM)
_SEM_SPEC = pl.BlockSpec(memory_space=pltpu.SEMAPHORE)
_DATAFLOW_EFFECT = pltpu.SideEffectType.DATAFLOW_SIDE_EFFECTING
TOKEN_SHAPE = (8, D_MODEL)


def _copies_start(name, arrays, n_copies, plan):
    n = len(arrays)

    def body(*refs):
        send_sems, recv_sems, token = refs[n], refs[n + 1], refs[2 * n + 2]
        copies = plan(refs[:n])
        assert len(copies) == n_copies
        for k, (src, dst, dev) in enumerate(copies):
            pltpu.make_async_remote_copy(src_ref=src, dst_ref=dst, send_sem=send_sems.at[k], recv_sem=recv_sems.at[k],
                                         device_id=dev, device_id_type=MESH).start()
        token[...] = jnp.zeros_like(token)

    outs = pl.pallas_call(
        body, name=name,
        out_shape=(pltpu.SemaphoreType.DMA((n_copies,)), pltpu.SemaphoreType.DMA((n_copies,)),
                   *[pltpu.HBM(a.shape, a.dtype) for a in arrays], jax.ShapeDtypeStruct(TOKEN_SHAPE, F32)),
        in_specs=[_HBM_SPEC] * n,
        out_specs=(_SEM_SPEC, _SEM_SPEC, *[_HBM_SPEC] * n, pl.BlockSpec(memory_space=pltpu.VMEM)),
        input_output_aliases={i: i + 2 for i in range(n)},
        compiler_params=pltpu.CompilerParams(has_side_effects=_DATAFLOW_EFFECT),
    )(*[pltpu.with_memory_space_constraint(a, pltpu.HBM) for a in arrays])
    return outs[0], outs[1], list(outs[2:2 + n]), outs[2 + n]


def _copies_wait(name, send_sems, recv_sems, arrays, n_copies, plan, after):
    n = len(arrays)

    def body(*refs):
        send_ref, recv_ref = refs[n], refs[n + 1]
        copies = plan(refs[:n])
        assert len(copies) == n_copies
        for k, (src, dst, dev) in enumerate(copies):
            cp = pltpu.make_async_remote_copy(src_ref=src, dst_ref=dst, send_sem=send_ref.at[k], recv_sem=recv_ref.at[k],
                                              device_id=dev, device_id_type=MESH)
            cp.wait_send()
            cp.wait_recv()

    outs = pl.pallas_call(
        body, name=name, out_shape=tuple(pltpu.HBM(a.shape, a.dtype) for a in arrays),
        in_specs=[_HBM_SPEC] * n + [_SEM_SPEC, _SEM_SPEC, pl.BlockSpec(memory_space=pl.ANY)],
        out_specs=tuple([_HBM_SPEC] * n), input_output_aliases={i: i for i in range(n)},
        compiler_params=pltpu.CompilerParams(has_side_effects=_DATAFLOW_EFFECT),
    )(*arrays, send_sems, recv_sems, after)
    return list(outs)


def _after(small, token):
    return small if token is None else small + token[0:1, :small.shape[-1]]


def _gather_plan_out(n):
    def plan(refs):
        x, y, c = _mesh_position()
        me = 4 * x + 2 * y + c
        copies = []
        for w in range(n):
            land = refs[n + w].at[me]
            copies.append((refs[w], land, (x, y, 1 - c)))
            for j in range(1, 4):
                copies.append((refs[w], land, (*_chip_of(j, x, y), c)))
        return copies
    return plan


def _gather_plan_pass(n):
    def plan(refs):
        x, y, c = _mesh_position()
        copies = []
        for w in range(n):
            for j in range(1, 4):
                px, py = _chip_of(j, x, y)
                rows = refs[w].at[4 * px + 2 * py + c]
                copies.append((rows, rows, (x, y, 1 - c)))
        return copies
    return plan


def _place_own(lands, shards):
    n = len(lands)

    def body(*refs):
        x, y, c = _mesh_position()
        sems = refs[3 * n]
        copies = [pltpu.make_async_copy(refs[n + w], refs[2 * n + w].at[4 * x + 2 * y + c], sems.at[w]) for w in range(n)]
        for cp in copies:
            cp.start()
        for cp in copies:
            cp.wait()

    return pl.pallas_call(
        body, name="weights_place_own", in_specs=[_any_spec()] * (2 * n), out_specs=[_any_spec()] * n,
        out_shape=[jax.ShapeDtypeStruct(a.shape, a.dtype) for a in lands],
        input_output_aliases={i: i for i in range(n)},
        scratch_shapes=[pltpu.SemaphoreType.DMA((n,))],
    )(*lands, *shards)


def _reduce_plan_sibling(n):
    def plan(refs):
        x, y, c = _mesh_position()
        copies = []
        for w in range(n):
            for j in range(4):
                px, py = _chip_of(j, x, y)
                copies.append((refs[w].at[4 * px + 2 * py + (1 - c)], refs[n + w].at[j], (x, y, 1 - c)))
        return copies
    return plan


def _reduce_plan_chips(n):
    def plan(refs):
        x, y, c = _mesh_position()
        copies = []
        for w in range(n):
            for j in range(1, 4):
                copies.append((refs[w].at[j - 1], refs[n + w].at[j - 1], (*_chip_of(j, x, y), c)))
        return copies
    return plan


def _chip_partials(grad, recv, name, tr):
    _, rows, width = grad.shape

    def body(g_ref, r_ref, own_ref, other_ref):
        x, y, c = _mesh_position()
        for j in range(4):
            px, py = _chip_of(j, x, y)
            total = g_ref[4 * px + 2 * py + c].astype(F32) + r_ref[j].astype(F32)
            if j == 0:
                own_ref[...] = total
            else:
                other_ref[j - 1] = total.astype(BF16)

    return pl.pallas_call(
        body, name=name, grid=(rows // tr,),
        in_specs=[pl.BlockSpec((N_DEV, tr, width), lambda i: (0, i, 0)),
                  pl.BlockSpec((4, tr, width), lambda i: (0, i, 0))],
        out_specs=[pl.BlockSpec((tr, width), lambda i: (i, 0)), pl.BlockSpec((3, tr, width), lambda i: (0, i, 0))],
        out_shape=[jax.ShapeDtypeStruct((rows, width), F32), jax.ShapeDtypeStruct((3, rows, width), BF16)],
        compiler_params=_params(("parallel",), 2 * 20 * tr * width * 2 + 8 * tr * width * 4),
    )(grad, recv)


def _adamw_math(w, g, m, v):
    m = ADAM_B1 * m + (1.0 - ADAM_B1) * g
    v = ADAM_B2 * v + (1.0 - ADAM_B2) * (g * g)
    m_hat = m / (1.0 - ADAM_B1 ** ADAM_STEP)
    v_hat = v / (1.0 - ADAM_B2 ** ADAM_STEP)
    delta = -ADAM_LR * (m_hat / (jnp.sqrt(v_hat) + ADAM_EPS) + ADAM_WD * w)
    return delta, m, v


def _adamw_shard(own, recv, w, m, v, name, tr):
    rows, width = own.shape

    def body(own_ref, r_ref, w_ref, m_ref, v_ref, g_ref, d_ref, nm_ref, nv_ref):
        g = own_ref[...]
        for j in range(3):
            g = g + r_ref[j].astype(F32)
        g_ref[...] = g
        d_ref[...], nm_ref[...], nv_ref[...] = _adamw_math(w_ref[...], g, m_ref[...], v_ref[...])

    spec = pl.BlockSpec((tr, width), lambda i: (i, 0))
    shape = jax.ShapeDtypeStruct((rows, width), F32)
    return pl.pallas_call(
        body, name=name, grid=(rows // tr,),
        in_specs=[spec, pl.BlockSpec((3, tr, width), lambda i: (0, i, 0)), spec, spec, spec],
        out_specs=[spec] * 4, out_shape=[shape] * 4,
        compiler_params=_params(("parallel",), 22 * tr * width * 4),
    )(own, recv, w, m, v)


def _small_all_reduce_adamw(gpart, lbpack, wpack, mpack, vpack):
    def body(gp_ref, lb_ref, w_ref, m_ref, v_ref, g_ref, d_ref, nm_ref, nv_ref, gath_ref, send_sems, recv_sems):
        x, y, c = _mesh_position()
        me = 4 * x + 2 * y + c
        gath_ref[me] = gp_ref[...]
        copies = []
        for j in range(1, N_DEV):
            peer = (x ^ (j >> 2), y ^ ((j >> 1) & 1), c ^ (j & 1))
            cp = pltpu.make_async_remote_copy(
                src_ref=gp_ref, dst_ref=gath_ref.at[me], send_sem=send_sems.at[j - 1], recv_sem=recv_sems.at[j - 1],
                device_id=peer, device_id_type=MESH)
            cp.start()
            copies.append(cp)
        for cp in copies:
            cp.wait()
        tot = gath_ref[0]
        for s in range(1, N_DEV):
            tot = tot + gath_ref[s]
        lb = lb_ref[...]
        dl = tot[16:32, :] * lb * (1.0 - lb)
        g = jnp.concatenate([tot[0:16, :], dl[0:8, :], -dl[0:8, :], dl[8:16, :], -dl[8:16, :],
                             tot[32:SMALL_GRAD_ROWS, :]], axis=0)
        g_ref[...] = g
        d_ref[...], nm_ref[...], nv_ref[...] = _adamw_math(w_ref[...], g, m_ref[...], v_ref[...])

    vm = pl.BlockSpec(memory_space=pltpu.VMEM)
    shape = jax.ShapeDtypeStruct((SMALL_ROWS, LANE), F32)
    return pl.pallas_call(
        body, name="small_all_reduce_adamw", in_specs=[vm] * 5, out_specs=[vm] * 4, out_shape=[shape] * 4,
        scratch_shapes=[pltpu.VMEM((N_DEV, SMALL_GRAD_ROWS, LANE), F32),
                        pltpu.SemaphoreType.DMA((N_DEV - 1,)), pltpu.SemaphoreType.DMA((N_DEV - 1,))],
    )(gpart, lbpack, wpack, mpack, vpack)


_SMALL_NAMES = ("norm_mix_gain", "norm_mem_gain", "lb_logits_fw", "lb_logits_bw", "norm_ffn_gain",
                "hg_norm_gain", "da_q_gain", "da_k_gain", "mem_q_gain", "mem_k_gain")
_SMALL_ROW0 = {"norm_mix_gain": 0, "norm_mem_gain": 8, "lb_logits_fw": 16, "lb_logits_bw": 32, "norm_ffn_gain": 48,
               "hg_norm_gain": 56, "da_q_gain": 64, "da_k_gain": 72, "mem_q_gain": 80, "mem_k_gain": 88}
_LOSS_ROW = 96


def _pack_rows(parts, total_rows):
    rows = []
    for p in parts:
        r = p.reshape(-1, LANE)
        rows.append(jnp.pad(r, ((0, -r.shape[0] % 8), (0, 0))))
    used = sum(r.shape[0] for r in rows)
    if total_rows > used:
        rows.append(jnp.zeros((total_rows - used, LANE), F32))
    return jnp.concatenate(rows, axis=0)


def _unpack_small(pack, like):
    out = {}
    for name in _SMALL_NAMES:
        n = like[name].size // LANE
        r0 = _SMALL_ROW0[name]
        out[name] = pack[r0:r0 + n].reshape(like[name].shape)
    return out


def kernel(x, mem, norm_mix_gain, norm_mem_gain, w_in, lb_logits_fw, lb_logits_bw, hg_norm_gain, da_q_gain, da_k_gain, w_mem_kv, mem_q_gain, mem_k_gain, w_proj_hg, w_proj_da, w_proj_mem, w_out, norm_ffn_gain, w_ffn_in, w_ffn_out, loss_target, m_norm_mix_gain, m_norm_mem_gain, m_w_in, m_lb_logits_fw, m_lb_logits_bw, m_hg_norm_gain, m_da_q_gain, m_da_k_gain, m_w_mem_kv, m_mem_q_gain, m_mem_k_gain, m_w_proj_hg, m_w_proj_da, m_w_proj_mem, m_w_out, m_norm_ffn_gain, m_w_ffn_in, m_w_ffn_out, v_norm_mix_gain, v_norm_mem_gain, v_w_in, v_lb_logits_fw, v_lb_logits_bw, v_hg_norm_gain, v_da_q_gain, v_da_k_gain, v_w_mem_kv, v_mem_q_gain, v_mem_k_gain, v_w_proj_hg, v_w_proj_da, v_w_proj_mem, v_w_out, v_norm_ffn_gain, v_w_ffn_in, v_w_ffn_out):
    small_w = dict(norm_mix_gain=norm_mix_gain, norm_mem_gain=norm_mem_gain, lb_logits_fw=lb_logits_fw,
                   lb_logits_bw=lb_logits_bw, norm_ffn_gain=norm_ffn_gain, hg_norm_gain=hg_norm_gain,
                   da_q_gain=da_q_gain, da_k_gain=da_k_gain, mem_q_gain=mem_q_gain, mem_k_gain=mem_k_gain)
    small_m = dict(norm_mix_gain=m_norm_mix_gain, norm_mem_gain=m_norm_mem_gain, lb_logits_fw=m_lb_logits_fw,
                   lb_logits_bw=m_lb_logits_bw, norm_ffn_gain=m_norm_ffn_gain, hg_norm_gain=m_hg_norm_gain,
                   da_q_gain=m_da_q_gain, da_k_gain=m_da_k_gain, mem_q_gain=m_mem_q_gain, mem_k_gain=m_mem_k_gain)
    small_v = dict(norm_mix_gain=v_norm_mix_gain, norm_mem_gain=v_norm_mem_gain, lb_logits_fw=v_lb_logits_fw,
                   lb_logits_bw=v_lb_logits_bw, norm_ffn_gain=v_norm_ffn_gain, hg_norm_gain=v_hg_norm_gain,
                   da_q_gain=v_da_q_gain, da_k_gain=v_da_k_gain, mem_q_gain=v_mem_q_gain, mem_k_gain=v_mem_k_gain)
    big_names = ("w_in", "w_mem_kv", "w_proj_hg", "w_proj_da", "w_proj_mem", "w_out", "w_ffn_in", "w_ffn_out")
    big_w = dict(w_in=w_in[0], w_mem_kv=w_mem_kv[0], w_proj_hg=w_proj_hg[0], w_proj_da=w_proj_da[0],
                 w_proj_mem=w_proj_mem[0], w_out=w_out[0], w_ffn_in=w_ffn_in[0], w_ffn_out=w_ffn_out[0])
    big_m = dict(w_in=m_w_in[0], w_mem_kv=m_w_mem_kv[0], w_proj_hg=m_w_proj_hg[0], w_proj_da=m_w_proj_da[0],
                 w_proj_mem=m_w_proj_mem[0], w_out=m_w_out[0], w_ffn_in=m_w_ffn_in[0], w_ffn_out=m_w_ffn_out[0])
    big_v = dict(w_in=v_w_in[0], w_mem_kv=v_w_mem_kv[0], w_proj_hg=v_w_proj_hg[0], w_proj_da=v_w_proj_da[0],
                 w_proj_mem=v_w_proj_mem[0], w_out=v_w_out[0], w_ffn_in=v_w_ffn_in[0], w_ffn_out=v_w_ffn_out[0])

    row_tile = dict(w_in=128, w_mem_kv=128, w_proj_hg=128, w_proj_da=512, w_proj_mem=512, w_out=128,
                    w_ffn_in=128, w_ffn_out=352)
    rest = _BIG_NAMES[1:]
    shards = [big_w[n].astype(BF16) for n in rest]
    state = {}
    big_out = {}

    def reduce_start(group, stacked):
        names = tuple(stacked)
        arrays = [stacked[n] for n in names] + [lax.empty((4,) + stacked[n].shape[1:], BF16) for n in names]
        plan = _reduce_plan_sibling(len(names))
        send, recv, thru, token = _copies_start(f"grads_{group}_sibling_start", arrays, 4 * len(names), plan)
        state[group] = dict(names=names, plan=plan, send=send, recv=recv, arrays=thru)
        return token

    def reduce_middle(group, after):
        st = state[group]
        names, k = st["names"], len(st["names"])
        thru = _copies_wait(f"grads_{group}_sibling_wait", st["send"], st["recv"], st["arrays"], 4 * k, st["plan"], after)
        partials = [_chip_partials(thru[i], thru[k + i], f"chip_partials_{n}", row_tile[n]) for i, n in enumerate(names)]
        arrays = [p[1] for p in partials] + [lax.empty(p[1].shape, BF16) for p in partials]
        plan = _reduce_plan_chips(k)
        send, recv, thru2, token = _copies_start(f"grads_{group}_chips_start", arrays, 3 * k, plan)
        state[group] = dict(names=names, plan=plan, send=send, recv=recv, arrays=thru2, own=[p[0] for p in partials])
        return token

    def reduce_finish(group, after):
        st = state.pop(group)
        names, k = st["names"], len(st["names"])
        thru = _copies_wait(f"grads_{group}_chips_wait", st["send"], st["recv"], st["arrays"], 3 * k, st["plan"], after)
        for i, n in enumerate(names):
            big_out[n] = _adamw_shard(st["own"][i], thru[k + i], big_w[n], big_m[n], big_v[n], "adamw_" + n, row_tile[n])

    win_st = _all_gather([big_w["w_in"].astype(BF16)])[0]
    step = _local_step_stages(x[0], mem[0], loss_target[0], small_w, win_st)
    event, payload = next(step)
    local = None
    while True:
        reply = None
        if event == "begin":
            nr = len(rest)
            arrays = shards + [lax.empty((N_DEV,) + s.shape, BF16) for s in shards]
            plan = _gather_plan_out(nr)
            send, recv, thru, reply = _copies_start("weights_rest_out_start", arrays, 4 * nr, plan)
            state["gather"] = dict(plan=plan, send=send, recv=recv, arrays=thru)
        elif event == "after_gla_fwd":
            st = state["gather"]
            nr = len(rest)
            thru = _copies_wait("weights_rest_out_wait", st["send"], st["recv"], st["arrays"], 4 * nr, st["plan"], payload)
            plan = _gather_plan_pass(nr)
            send, recv, lands, reply = _copies_start("weights_rest_pass_start", thru[nr:], 3 * nr, plan)
            state["gather"] = dict(plan=plan, send=send, recv=recv, arrays=lands)
        elif event == "need_weights":
            st = state.pop("gather")
            nr = len(rest)
            lands = _copies_wait("weights_rest_pass_wait", st["send"], st["recv"], st["arrays"], 3 * nr, st["plan"], payload)
            reply = dict(zip(rest, _place_own(lands, shards)))
        elif event == "grads_ffn":
            reply = reduce_start("ffn", payload)
        elif event == "after_gate_merge_bwd":
            reply = reduce_middle("ffn", payload)
        elif event == "grads_mix":
            reply = reduce_start("mix", payload)
        elif event == "after_da_bwd_g0":
            reply = reduce_middle("mix", payload)
            reduce_finish("ffn", payload)
        elif event == "after_gla_bwd":
            reduce_finish("mix", payload)
        elif event == "grads_in":
            reply = reduce_start("in", payload)
        elif event == "after_proj_bwd_act":
            reply = reduce_middle("in", payload)
        elif event == "end":
            local = payload
            reduce_finish("in", local["grad_x"])
            break
        event, payload = step.send(reply)
    grad_x = local["grad_x"]

    wpack = _pack_rows([small_w[n] for n in _SMALL_NAMES], SMALL_ROWS)
    mpack = _pack_rows([small_m[n] for n in _SMALL_NAMES], SMALL_ROWS)
    vpack = _pack_rows([small_v[n] for n in _SMALL_NAMES], SMALL_ROWS)
    gs, ds, ms, vs = _small_all_reduce_adamw(local["gpart"], local["lbpack"], wpack, mpack, vpack)
    loss = gs[_LOSS_ROW, 0]
    small_out = [_unpack_small(t, small_w) for t in (gs, ds, ms, vs)]

    order = ("norm_mix_gain", "norm_mem_gain", "w_in", "lb_logits_fw", "lb_logits_bw", "hg_norm_gain", "da_q_gain",
             "da_k_gain", "w_mem_kv", "mem_q_gain", "mem_k_gain", "w_proj_hg", "w_proj_da", "w_proj_mem", "w_out",
             "norm_ffn_gain", "w_ffn_in", "w_ffn_out")
    outs = [loss, grad_x[None]]
    for kind in range(4):
        for n in order:
            outs.append(big_out[n][kind][None] if n in big_out else small_out[kind][n])
    return tuple(outs)


_BIG_NAMES = ("w_in", "w_mem_kv", "w_proj_hg", "w_proj_da", "w_proj_mem", "w_out", "w_ffn_in", "w_ffn_out")


def _local_step(xs, mems, target, sw, wg):
    step = _local_step_stages(xs, mems, target, sw, wg["w_in"])
    stacked = {}
    event, payload = next(step)
    while event != "end":
        if event.startswith("grads_"):
            stacked.update(payload)
        event, payload = step.send(wg if event == "need_weights" else None)
    return dict(payload, stacked=stacked)


def _local_step_stages(xs, mems, target, sw, win_st):
    norm_mix_gain, norm_mem_gain, norm_ffn_gain = sw["norm_mix_gain"], sw["norm_mem_gain"], sw["norm_ffn_gain"]
    lb_logits_fw, lb_logits_bw, hg_norm_gain = sw["lb_logits_fw"], sw["lb_logits_bw"], sw["hg_norm_gain"]
    da_q_gain, da_k_gain, mem_q_gain, mem_k_gain = sw["da_q_gain"], sw["da_k_gain"], sw["mem_q_gain"], sw["mem_k_gain"]

    token = yield "begin", None
    lb_fw = _lb_table(lb_logits_fw, "lb_table_fw")
    lb_bw = _lb_table(lb_logits_bw, "lb_table_bw")
    lb = jnp.concatenate([lb_fw, lb_bw], axis=0).reshape(2, HG_HEADS, 1, HEAD_DIM)
    h = _rmsnorm_fwd(xs, _after(norm_mix_gain, token), "norm_mix_fwd", 512)
    proj = _matmul(h, win_st, "nn", F32, 512, IN_SHARD, D_MODEL, "proj_fwd", b_stacked=True, n_outer=True)
    o_hg, o_pre, states = _gla_fwd(proj, lb, hg_norm_gain)
    token = yield "after_gla_fwd", o_hg
    da = [_da_fwd(proj, _after(da_q_gain, token), da_k_gain, g) for g in range(3)]
    o_da, o_da32, lse_da = _da_merge([t[0] for t in da], [t[1] for t in da], 512)
    wg = yield "need_weights", o_da
    wkv = wg["w_mem_kv"].reshape(D_MODEL, 2 * MEM_WIDTH)
    wphg = wg["w_proj_hg"].reshape(D_MODEL, D_MODEL)
    wpda = jnp.transpose(wg["w_proj_da"], (1, 0, 2)).reshape(DA_WIDTH, D_MODEL)
    wpmem = jnp.transpose(wg["w_proj_mem"], (1, 0, 2)).reshape(MEM_WIDTH, D_MODEL)
    wout = wg["w_out"].reshape(D_MODEL, D_MODEL)
    wfin = jnp.transpose(wg["w_ffn_in"], (1, 0, 2)).reshape(D_MODEL, 2 * D_FF)
    wfout = wg["w_ffn_out"].reshape(D_FF, D_MODEL)
    mem_n = _rmsnorm_fwd(mems, norm_mem_gain, "norm_mem_fwd", N_MEM)
    kv = _matmul(mem_n, wkv, "nn", F32, N_MEM, 1024, D_MODEL, "mem_kv_fwd")
    o_mem = _mem_fwd(proj, kv, mem_q_gain, mem_k_gain, 512)
    t_hg = _matmul(o_hg, wphg, "nn", F32, 512, 1024, D_MODEL, "proj_hg_fwd")
    t_da = _matmul(o_da, wpda, "nn", F32, 512, 1024, DA_WIDTH, "proj_da_fwd")
    t_mem = _matmul(o_mem, wpmem, "nn", F32, 512, 1024, MEM_WIDTH, "proj_mem_fwd")
    merged = _gate_merge_fwd(proj, t_hg, t_da, t_mem, 256)
    u = _matmul(merged, wout, "nn", F32, 512, 1024, D_MODEL, "out_fwd")
    x1, h2 = _residual_rmsnorm_fwd(xs, u, norm_ffn_gain, "norm_ffn_fwd", 512)
    ab = _matmul(h2, wfin, "nn", F32, 512, 1408, D_MODEL, "ffn_in_fwd")
    act = _swiglu_fwd(ab, 256)
    z = _matmul(act, wfout, "nn", F32, 512, 1024, D_FF, "ffn_out_fwd")
    dy, dyb, loss_part = _loss_head(x1, z, target, 512)

    dact = _matmul(dyb, wfout, "nt", F32, 512, 1408, D_MODEL, "ffn_out_bwd_act")
    g_wfout = _matmul(act, dyb, "tn", BF16, 1408, 1024, 512, "ffn_out_bwd_w")
    dab = _swiglu_bwd(ab, dact, 256)
    dh2 = _matmul(dab, wfin, "nt", F32, 512, 1024, 1408, "ffn_in_bwd_act")
    g_wfin = _matmul(h2, dab, "tn", BF16, 1024, 1408, 512, "ffn_in_bwd_w")
    token = yield "grads_ffn", dict(
        w_ffn_in=jnp.transpose(g_wfin.reshape(D_MODEL, N_DEV, 2 * D_FF // N_DEV), (1, 0, 2)),
        w_ffn_out=g_wfout.reshape(N_DEV, D_FF // N_DEV, D_MODEL))
    dx1, dx1b, g_norm_ffn = _rmsnorm_bwd(x1, dh2, dy, _after(norm_ffn_gain, token), "norm_ffn_bwd", 256)
    dmerged = _matmul(dx1b, wout, "nt", F32, 512, 1024, D_MODEL, "out_bwd_act")
    g_wout = _matmul(merged, dx1b, "tn", BF16, 1024, 1024, 512, "out_bwd_w")
    dt_hg, dt_da, dt_mem, dgates = _gate_merge_bwd(proj, t_hg, t_da, t_mem, dmerged, 256)
    token = yield "after_gate_merge_bwd", dt_hg
    do_hg = _matmul(dt_hg, wphg, "nt", F32, 512, 1024, D_MODEL, "proj_hg_bwd_act", after=token)
    g_wphg = _matmul(o_hg, dt_hg, "tn", BF16, 1024, 1024, 512, "proj_hg_bwd_w")
    do_da = _matmul(dt_da, wpda, "nt", F32, 512, DA_WIDTH, D_MODEL, "proj_da_bwd_act")
    g_wpda = _matmul(o_da, dt_da, "tn", BF16, DA_WIDTH, LANE, 512, "proj_da_bwd_w", out_stacked=True)
    do_mem = _matmul(dt_mem, wpmem, "nt", F32, 512, MEM_WIDTH, D_MODEL, "proj_mem_bwd_act")
    g_wpmem = _matmul(o_mem, dt_mem, "tn", BF16, MEM_WIDTH, LANE, 512, "proj_mem_bwd_w", out_stacked=True)

    dmemq, dk_mem, dv_mem, g_mem_q, g_mem_k = _mem_bwd(proj, kv, mem_q_gain, mem_k_gain, do_mem, 512)
    dkv = jnp.concatenate([dk_mem, dv_mem], axis=1).astype(BF16)
    g_wkv = _matmul(mem_n, dkv, "tn", BF16, 1024, 1024, N_MEM, "mem_kv_bwd_w")
    dmem_n = _matmul(dkv, wkv, "nt", F32, N_MEM, 1024, 1024, "mem_kv_bwd_act")
    g_norm_mem = _gain_grad(mems, dmem_n, "norm_mem_bwd")
    token = yield "grads_mix", dict(
        w_mem_kv=g_wkv.reshape(N_DEV, D_MODEL // N_DEV, 2 * MEM_WIDTH),
        w_proj_hg=g_wphg.reshape(N_DEV, D_MODEL // N_DEV, D_MODEL),
        w_proj_da=g_wpda, w_proj_mem=g_wpmem,
        w_out=g_wout.reshape(N_DEV, D_MODEL // N_DEV, D_MODEL))

    dd_da = _da_rowdot(do_da, o_da32, 512)
    da_b = [_da_bwd(proj, _after(da_q_gain, token), da_k_gain, do_da, lse_da, dd_da, 0)]
    token = yield "after_da_bwd_g0", da_b[0][0]
    da_b += [_da_bwd(proj, _after(da_q_gain, token), da_k_gain, do_da, lse_da, dd_da, g) for g in (1, 2)]
    g_da_q = da_b[0][3] + da_b[1][3] + da_b[2][3]
    g_da_k = da_b[0][4] + da_b[1][4] + da_b[2][4]

    dq_hg, df_hg, dv_hg, dg_hg, dlb, g_hg_norm = _gla_bwd(proj, lb, hg_norm_gain, o_pre, states, do_hg)
    yield "after_gla_bwd", dq_hg

    dproj = jnp.concatenate(
        [dq_hg, df_hg[0], df_hg[1], dv_hg, dg_hg]
        + [t[0] for t in da_b] + [t[1] for t in da_b] + [t[2] for t in da_b] + [dmemq, dgates], axis=1)
    g_win = _matmul(h, dproj, "tn", BF16, 1024, IN_SHARD, 512, "proj_bwd_w", out_stacked=True)
    token = yield "grads_in", dict(w_in=g_win)
    dh = _matmul(dproj, win_st, "nt", F32, 512, 1024, IN_SHARD, "proj_bwd_act", b_stacked=True, after=token)
    token = yield "after_proj_bwd_act", dh
    grad_x, _, g_norm_mix = _rmsnorm_bwd(xs, dh, dx1, _after(norm_mix_gain, token), "norm_mix_bwd", 256)

    gpart = _pack_rows([g_norm_mix, g_norm_mem, dlb[0], dlb[1], g_norm_ffn, g_hg_norm, g_da_q, g_da_k,
                        g_mem_q, g_mem_k, loss_part], SMALL_GRAD_ROWS)
    lbpack = jnp.concatenate([lb_fw.reshape(8, LANE), lb_bw.reshape(8, LANE)], axis=0)
    yield "end", dict(grad_x=grad_x, gpart=gpart, lbpack=lbpack)
```

```python
import numpy as np
import jax
import jax.numpy as jnp
from jax import lax
from jax.experimental import pallas as pl
from jax.experimental.pallas import tpu as pltpu

F32 = jnp.float32
BF16 = jnp.bfloat16
MESH = pl.DeviceIdType.MESH

SEQ = 4096
D_MODEL = 1024
N_DEV = 8
N_MEM = 256
RMS_EPS = 1e-6
NEG_INF = -1e30
LANE = 128
HEAD_DIM = 128
HG_HEADS = 8
HG_CHUNK = 64
HG_SCALE = HEAD_DIM ** -0.5
DA_DILATIONS = (1, 4, 16)
DA_RADIUS = 64
DA_HEADS_PER_GROUP = 4
DA_HEADS = 12
DA_WIDTH = 512
DA_SCALE = HEAD_DIM ** -0.5
DA_QB = 128
DA_WIN = 256
MEM_HEADS = 4
MEM_WIDTH = 512
MEM_SCALE = HEAD_DIM ** -0.5
D_FF = 2816
IN_COLS = 13312
IN_SHARD = IN_COLS // N_DEV
CB_HG_Q, CB_F, CB_HG_I, CB_HG_G = 0, 8, 24, 32
CB_DA_Q, CB_DA_K, CB_DA_V, CB_MEM_Q = 40, 52, 64, 76
N_CB = IN_COLS // LANE
ADAM_LR, ADAM_B1, ADAM_B2, ADAM_EPS, ADAM_WD, ADAM_STEP = 0.001, 0.9, 0.999, 1e-08, 0.01, 10
VMEM_BYTES_V7X = 64 * 1024 * 1024
SMALL_ROWS = 104
SMALL_GRAD_ROWS = 88

_NN = (((1,), (0,)), ((), ()))
_NT = (((1,), (1,)), ((), ()))
_TN = (((0,), (0,)), ((), ()))


def _dot(a, b, dims):
    return lax.dot_general(a.astype(BF16), b.astype(BF16), dims, preferred_element_type=F32)


def _dot_exact(a, b, dims):
    return lax.dot_general(a, b, dims, precision=lax.Precision.HIGHEST, preferred_element_type=F32)


def _sigmoid(x):
    return 0.5 * jnp.tanh(0.5 * x) + 0.5


def _params(semantics, est_bytes):
    limit = int(min(VMEM_BYTES_V7X - (6 << 20), max(32 << 20, est_bytes * 3 // 2)))
    return pltpu.CompilerParams(dimension_semantics=semantics, vmem_limit_bytes=limit)


def _nbytes(shape, dtype):
    return int(np.prod(shape)) * jnp.dtype(dtype).itemsize


def _alibi_slopes(n):
    return (2.0 ** (-8.0 * np.arange(1, n + 1) / n)).astype(np.float32)


def _matmul(a, b, mode, out_dtype, tm, tn, tk, name, b_stacked=False, out_stacked=False, n_outer=False, after=None):
    if mode == "tn":
        kdim, m = a.shape
    else:
        m, kdim = a.shape
    if b_stacked:
        if mode == "nn":
            n = b.shape[0] * b.shape[2]
            assert tn == b.shape[2] and tk == kdim == b.shape[1]
        else:
            assert mode == "nt" and tk == b.shape[2] and b.shape[0] * tk == kdim
            n = b.shape[1]
    else:
        n = b.shape[0] if mode == "nt" else b.shape[1]
    assert m % tm == 0 and n % tn == 0 and kdim % tk == 0
    gm, gn, gk = m // tm, n // tn, kdim // tk

    def ijk(f):
        if n_outer:
            return lambda j, i, k: f(i, j, k)
        return f

    if mode == "tn":
        a_spec = pl.BlockSpec((tk, tm), ijk(lambda i, j, k: (k, i)))
    else:
        a_spec = pl.BlockSpec((tm, tk), ijk(lambda i, j, k: (i, k)))
    if b_stacked and mode == "nn":
        b_spec = pl.BlockSpec((None, tk, tn), ijk(lambda i, j, k: (j, 0, 0)))
    elif b_stacked:
        b_spec = pl.BlockSpec((None, tn, tk), ijk(lambda i, j, k: (k, j, 0)))
    elif mode == "nt":
        b_spec = pl.BlockSpec((tn, tk), ijk(lambda i, j, k: (j, k)))
    else:
        b_spec = pl.BlockSpec((tk, tn), ijk(lambda i, j, k: (k, j)))
    if out_stacked:
        assert tm == m
        out_shape = jax.ShapeDtypeStruct((gn, m, tn), out_dtype)
        o_spec = pl.BlockSpec((None, tm, tn), ijk(lambda i, j, k: (j, i, 0)))
    else:
        out_shape = jax.ShapeDtypeStruct((m, n), out_dtype)
        o_spec = pl.BlockSpec((tm, tn), ijk(lambda i, j, k: (i, j)))
    dims = {"nn": _NN, "nt": _NT, "tn": _TN}[mode]

    def body(*refs):
        a_ref, b_ref = refs[0], refs[1]
        o_ref = refs[2 if after is None else 3]
        part = _dot(a_ref[...], b_ref[...], dims)
        if gk == 1:
            o_ref[...] = part.astype(out_dtype)
            return
        acc_ref = refs[-1]
        k = pl.program_id(2)

        @pl.when(k == 0)
        def _():
            acc_ref[...] = part

        @pl.when(k > 0)
        def _():
            acc_ref[...] += part

        @pl.when(k == gk - 1)
        def _():
            o_ref[...] = acc_ref[...].astype(out_dtype)

    a_tile = _nbytes((tm, tk), a.dtype)
    b_tile = _nbytes((tk, tn), b.dtype)
    o_tile = _nbytes((tm, tn), out_dtype)
    est = 2 * (a_tile + b_tile + o_tile) + 3 * tm * tn * 4 + (a_tile + b_tile)
    grid = (gn, gm, gk) if n_outer else (gm, gn, gk)
    operands, in_specs = [a, b], [a_spec, b_spec]
    if after is not None:
        operands.append(after)
        in_specs.append(pl.BlockSpec(memory_space=pl.ANY))
    return pl.pallas_call(
        body, name=name, grid=grid, in_specs=in_specs, out_specs=o_spec, out_shape=out_shape,
        scratch_shapes=[] if gk == 1 else [pltpu.VMEM((tm, tn), F32)],
        compiler_params=_params(("parallel", "parallel", "arbitrary"), est),
    )(*operands)


def _row_spec(tr, width, col_block=0):
    return pl.BlockSpec((tr, width), lambda i: (i, col_block))


def _bcast_spec(width):
    return pl.BlockSpec((1, width), lambda i: (0, 0))


def _rmsnorm_fwd(x, gain, name, tr):
    rows, width = x.shape

    def body(x_ref, g_ref, o_ref):
        xv = x_ref[...]
        r = lax.rsqrt(jnp.mean(xv * xv, axis=-1, keepdims=True) + RMS_EPS)
        o_ref[...] = (xv * r * g_ref[...]).astype(BF16)

    return pl.pallas_call(
        body, name=name, grid=(rows // tr,), in_specs=[_row_spec(tr, width), _bcast_spec(width)],
        out_specs=_row_spec(tr, width), out_shape=jax.ShapeDtypeStruct((rows, width), BF16),
        compiler_params=_params(("parallel",), 6 * tr * width * 4),
    )(x, gain)


def _residual_rmsnorm_fwd(x, u, gain, name, tr):
    rows, width = x.shape

    def body(x_ref, u_ref, g_ref, x1_ref, h_ref):
        xv = x_ref[...] + u_ref[...]
        x1_ref[...] = xv
        r = lax.rsqrt(jnp.mean(xv * xv, axis=-1, keepdims=True) + RMS_EPS)
        h_ref[...] = (xv * r * g_ref[...]).astype(BF16)

    return pl.pallas_call(
        body, name=name, grid=(rows // tr,),
        in_specs=[_row_spec(tr, width), _row_spec(tr, width), _bcast_spec(width)],
        out_specs=[_row_spec(tr, width), _row_spec(tr, width)],
        out_shape=[jax.ShapeDtypeStruct((rows, width), F32), jax.ShapeDtypeStruct((rows, width), BF16)],
        compiler_params=_params(("parallel",), 10 * tr * width * 4),
    )(x, u, gain)


def _rmsnorm_bwd(x, dh, dres, gain, name, tr):
    rows, width = x.shape

    def body(x_ref, dh_ref, dres_ref, g_ref, dx_ref, dxb_ref, dg_ref):
        xv = x_ref[...]
        r = lax.rsqrt(jnp.mean(xv * xv, axis=-1, keepdims=True) + RMS_EPS)
        xhat = xv * r
        dhv = dh_ref[...]
        dyg = dhv * g_ref[...]
        dx = dres_ref[...] + r * (dyg - xhat * jnp.mean(dyg * xhat, axis=-1, keepdims=True))
        dx_ref[...] = dx
        dxb_ref[...] = dx.astype(BF16)
        part = jnp.sum(dhv * xhat, axis=0, keepdims=True)

        @pl.when(pl.program_id(0) == 0)
        def _():
            dg_ref[...] = part

        @pl.when(pl.program_id(0) > 0)
        def _():
            dg_ref[...] += part

    return pl.pallas_call(
        body, name=name, grid=(rows // tr,),
        in_specs=[_row_spec(tr, width), _row_spec(tr, width), _row_spec(tr, width), _bcast_spec(width)],
        out_specs=[_row_spec(tr, width), _row_spec(tr, width), _bcast_spec(width)],
        out_shape=[jax.ShapeDtypeStruct((rows, width), F32), jax.ShapeDtypeStruct((rows, width), BF16),
                   jax.ShapeDtypeStruct((1, width), F32)],
        compiler_params=_params(("arbitrary",), 14 * tr * width * 4),
    )(x, dh, dres, gain)


def _gain_grad(x, dh, name):
    rows, width = x.shape

    def body(x_ref, dh_ref, dg_ref):
        xv = x_ref[...]
        r = lax.rsqrt(jnp.mean(xv * xv, axis=-1, keepdims=True) + RMS_EPS)
        dg_ref[...] = jnp.sum(dh_ref[...] * xv * r, axis=0, keepdims=True)

    return pl.pallas_call(
        body, name=name, grid=(1,), in_specs=[_row_spec(rows, width), _row_spec(rows, width)],
        out_specs=_bcast_spec(width), out_shape=jax.ShapeDtypeStruct((1, width), F32),
        compiler_params=_params(("arbitrary",), 6 * rows * width * 4),
    )(x, dh)


def _lb_table(logits, name):
    slots, width = logits.shape

    def body(l_ref, o_ref):
        lv = l_ref[...]
        mx = jnp.max(lv, axis=0, keepdims=True)
        e = jnp.exp(lv - mx)
        o_ref[...] = e[0:1, :] / jnp.sum(e, axis=0, keepdims=True)

    return pl.pallas_call(
        body, name=name, grid=(1,), in_specs=[pl.BlockSpec((slots, width), lambda i: (0, 0))],
        out_specs=_bcast_spec(width), out_shape=jax.ShapeDtypeStruct((1, width), F32),
    )(logits)


def _gate_merge_fwd(proj, t_hg, t_da, t_mem, tr):
    w = D_MODEL

    def body(ghg_ref, gda_ref, gmem_ref, thg_ref, tda_ref, tmem_ref, o_ref):
        acc = _sigmoid(ghg_ref[...]) * thg_ref[...]
        acc += _sigmoid(gda_ref[...]) * tda_ref[...]
        acc += _sigmoid(gmem_ref[...]) * tmem_ref[...]
        o_ref[...] = acc.astype(BF16)

    return pl.pallas_call(
        body, name="gate_merge_fwd", grid=(SEQ // tr,),
        in_specs=[_row_spec(tr, w, 10), _row_spec(tr, w, 11), _row_spec(tr, w, 12),
                  _row_spec(tr, w), _row_spec(tr, w), _row_spec(tr, w)],
        out_specs=_row_spec(tr, w), out_shape=jax.ShapeDtypeStruct((SEQ, w), BF16),
        compiler_params=_params(("parallel",), 16 * tr * w * 4),
    )(proj, proj, proj, t_hg, t_da, t_mem)


def _gate_merge_bwd(proj, t_hg, t_da, t_mem, dmerged, tr):
    w = D_MODEL

    def body(ghg_ref, gda_ref, gmem_ref, thg_ref, tda_ref, tmem_ref, dm_ref, dthg_ref, dtda_ref, dtmem_ref, dg_ref):
        dm = dm_ref[...]
        for b, (g_ref, t_ref, dt_ref) in enumerate(((ghg_ref, thg_ref, dthg_ref), (gda_ref, tda_ref, dtda_ref),
                                                    (gmem_ref, tmem_ref, dtmem_ref))):
            s = _sigmoid(g_ref[...])
            dt_ref[...] = (s * dm).astype(BF16)
            dg_ref[:, b * w:(b + 1) * w] = (dm * t_ref[...] * s * (1.0 - s)).astype(BF16)

    return pl.pallas_call(
        body, name="gate_merge_bwd", grid=(SEQ // tr,),
        in_specs=[_row_spec(tr, w, 10), _row_spec(tr, w, 11), _row_spec(tr, w, 12),
                  _row_spec(tr, w), _row_spec(tr, w), _row_spec(tr, w), _row_spec(tr, w)],
        out_specs=[_row_spec(tr, w), _row_spec(tr, w), _row_spec(tr, w), _row_spec(tr, 3 * w)],
        out_shape=[jax.ShapeDtypeStruct((SEQ, w), BF16)] * 3 + [jax.ShapeDtypeStruct((SEQ, 3 * w), BF16)],
        compiler_params=_params(("parallel",), 24 * tr * w * 4),
    )(proj, proj, proj, t_hg, t_da, t_mem, dmerged)


def _swiglu_fwd(ab, tr):
    def body(ab_ref, o_ref):
        a = ab_ref[:, :D_FF]
        b = ab_ref[:, D_FF:]
        o_ref[...] = (a * _sigmoid(a) * b).astype(BF16)

    return pl.pallas_call(
        body, name="swiglu_fwd", grid=(SEQ // tr,), in_specs=[_row_spec(tr, 2 * D_FF)],
        out_specs=_row_spec(tr, D_FF), out_shape=jax.ShapeDtypeStruct((SEQ, D_FF), BF16),
        compiler_params=_params(("parallel",), 8 * tr * 2 * D_FF * 4),
    )(ab)


def _swiglu_bwd(ab, dact, tr):
    def body(ab_ref, d_ref, o_ref):
        a = ab_ref[:, :D_FF]
        b = ab_ref[:, D_FF:]
        d = d_ref[...]
        s = _sigmoid(a)
        silu = a * s
        o_ref[:, :D_FF] = (d * b * (s + silu * (1.0 - s))).astype(BF16)
        o_ref[:, D_FF:] = (d * silu).astype(BF16)

    return pl.pallas_call(
        body, name="swiglu_bwd", grid=(SEQ // tr,), in_specs=[_row_spec(tr, 2 * D_FF), _row_spec(tr, D_FF)],
        out_specs=_row_spec(tr, 2 * D_FF), out_shape=jax.ShapeDtypeStruct((SEQ, 2 * D_FF), BF16),
        compiler_params=_params(("parallel",), 10 * tr * 2 * D_FF * 4),
    )(ab, dact)


def _loss_head(x1, z, target, tr):
    w = D_MODEL

    def body(x_ref, z_ref, t_ref, dy_ref, dyb_ref, loss_ref, acc_ref):
        err = x_ref[...] + z_ref[...] - t_ref[...]
        dy = err * (1.0 / w)
        dy_ref[...] = dy
        dyb_ref[...] = dy.astype(BF16)
        part = jnp.sum(err * err, axis=0, keepdims=True)

        @pl.when(pl.program_id(0) == 0)
        def _():
            acc_ref[...] = part

        @pl.when(pl.program_id(0) > 0)
        def _():
            acc_ref[...] += part

        @pl.when(pl.program_id(0) == SEQ // tr - 1)
        def _():
            total = jnp.sum(acc_ref[...], axis=1, keepdims=True) * (0.5 / w)
            loss_ref[...] = jnp.broadcast_to(total, (1, LANE))

    return pl.pallas_call(
        body, name="loss_head", grid=(SEQ // tr,),
        in_specs=[_row_spec(tr, w), _row_spec(tr, w), _row_spec(tr, w)],
        out_specs=[_row_spec(tr, w), _row_spec(tr, w), _bcast_spec(LANE)],
        out_shape=[jax.ShapeDtypeStruct((SEQ, w), F32), jax.ShapeDtypeStruct((SEQ, w), BF16),
                   jax.ShapeDtypeStruct((1, LANE), F32)],
        scratch_shapes=[pltpu.VMEM((1, w), F32)],
        compiler_params=_params(("arbitrary",), 12 * tr * w * 4),
    )(x1, z, target)


GLA_ROWS = 256
GLA_CPB = GLA_ROWS // HG_CHUNK
GLA_NBLK = SEQ // GLA_ROWS
GLA_NCK = SEQ // HG_CHUNK


def _dot_split(m, xv, dims):
    hi = xv.astype(BF16)
    r1 = xv - hi.astype(F32)
    mid = r1.astype(BF16)
    lo = (r1 - mid.astype(F32)).astype(BF16)
    dot = lambda t: lax.dot_general(m, t, dims, preferred_element_type=F32)
    return (dot(lo) + dot(mid)) + dot(hi)


def _gla_block(qc, fc, lbv, masks):
    mask, maskb, direction = masks
    sq = _sigmoid(qc)
    q = qc * sq * HG_SCALE
    sf = _sigmoid(fc)
    forget = lbv + (1.0 - lbv) * sf
    k = 1.0 - forget
    logf = jnp.log(forget)
    b = _dot_split(maskb, logf, _NN)
    ends = []
    for j in range(GLA_CPB):
        lo, hi = j * HG_CHUNK, (j + 1) * HG_CHUNK
        end = jnp.where(direction == 0, b[hi - 1:hi, :], b[lo:lo + 1, :])
        ends.append(jnp.broadcast_to(end, (HG_CHUNK, HEAD_DIM)))
    bt = jnp.concatenate(ends, axis=0)
    eb = jnp.exp(b)
    qt = q * eb
    kt = k * jnp.exp(-b)
    kh = k * jnp.exp(bt - b)
    a = jnp.where(mask, _dot(qt, kt, _NT), 0.0)
    return dict(sq=sq, sf=sf, forget=forget, k=k, b=b, bt=bt, eb=eb, qt=qt, kt=kt, kh=kh, a=a)


def _gla_masks(direction):
    row = lax.broadcasted_iota(jnp.int32, (GLA_ROWS, GLA_ROWS), 0)
    col = lax.broadcasted_iota(jnp.int32, (GLA_ROWS, GLA_ROWS), 1)
    same = (row // HG_CHUNK) == (col // HG_CHUNK)
    mask = jnp.logical_and(same, jnp.where(direction == 0, row - col, col - row) >= 0)
    return mask, jnp.where(mask, 1.0, 0.0).astype(BF16), direction


def _gla_chunk_rows(j):
    return slice(j * HG_CHUNK, (j + 1) * HG_CHUNK)


def _head_spec(col_block0):
    return pl.BlockSpec((SEQ, HEAD_DIM), lambda h, d: (0, col_block0 + h))


def _gla_fwd(proj, lb, gain):
    nck = GLA_NCK

    def body(q_ref, f_ref, v_ref, g_ref, lb_ref, gain_ref, ohg_ref, opre_ref, st_ref, qt_scr, cs_scr, dec_scr):
        d = pl.program_id(1)
        masks = _gla_masks(d)
        lbv = lb_ref[...]

        def intra(s, carry):
            rows = pl.ds(pl.multiple_of(s * GLA_ROWS, GLA_ROWS), GLA_ROWS)
            ck = _gla_block(q_ref[rows, :], f_ref[rows, :], lbv, masks)
            v = v_ref[rows, :]
            o = _dot(ck["a"], v, _NN)

            @pl.when(d == 0)
            def _():
                opre_ref[rows, :] = o

            @pl.when(d == 1)
            def _():
                opre_ref[rows, :] += o

            qt_scr[rows, :] = ck["qt"].astype(BF16)
            for j in range(GLA_CPB):
                c = s * GLA_CPB + j
                cr = _gla_chunk_rows(j)
                cs_scr[c] = _dot(v[cr, :], ck["kh"][cr, :], _TN)
                dec_scr[c] = jnp.exp(ck["bt"][j * HG_CHUNK:j * HG_CHUNK + 8, :])
            return carry

        lax.fori_loop(0, GLA_NBLK, intra, 0)

        def scan(i, st):
            c = jnp.where(d == 0, i, nck - 1 - i)
            st_ref[c] = st
            return st * dec_scr[c][0:1, :] + cs_scr[c]

        lax.fori_loop(0, nck, scan, jnp.zeros((HEAD_DIM, HEAD_DIM), F32), unroll=4)

        def inter(c, carry):
            rows = pl.ds(pl.multiple_of(c * HG_CHUNK, HG_CHUNK), HG_CHUNK)
            opre_ref[rows, :] += _dot(qt_scr[rows, :], st_ref[c], _NT)
            return carry

        lax.fori_loop(0, nck, inter, 0, unroll=4)

        @pl.when(d == 1)
        def _():
            o = opre_ref[...]
            r = lax.rsqrt(jnp.mean(o * o, axis=-1, keepdims=True) + RMS_EPS)
            g = g_ref[...]
            ohg_ref[...] = (o * r * gain_ref[...] * (g * _sigmoid(g))).astype(BF16)

    blk = SEQ * HEAD_DIM * 4
    return pl.pallas_call(
        body, name="gla_fwd", grid=(HG_HEADS, 2),
        in_specs=[_head_spec(CB_HG_Q),
                  pl.BlockSpec((SEQ, HEAD_DIM), lambda h, d: (0, CB_F + 8 * d + h)),
                  _head_spec(CB_HG_I), _head_spec(CB_HG_G),
                  pl.BlockSpec((None, None, 1, HEAD_DIM), lambda h, d: (d, h, 0, 0)),
                  pl.BlockSpec((1, HEAD_DIM), lambda h, d: (0, 0))],
        out_specs=[_head_spec(0), _head_spec(0),
                   pl.BlockSpec((None, None, nck, HEAD_DIM, HEAD_DIM), lambda h, d: (h, d, 0, 0, 0))],
        out_shape=[jax.ShapeDtypeStruct((SEQ, D_MODEL), BF16), jax.ShapeDtypeStruct((SEQ, D_MODEL), F32),
                   jax.ShapeDtypeStruct((HG_HEADS, 2, nck, HEAD_DIM, HEAD_DIM), F32)],
        scratch_shapes=[pltpu.VMEM((SEQ, HEAD_DIM), BF16), pltpu.VMEM((nck, HEAD_DIM, HEAD_DIM), F32),
                        pltpu.VMEM((nck, 8, HEAD_DIM), F32)],
        compiler_params=_params(("parallel", "arbitrary"), 8 * blk + 3 * blk + 2 * blk + 2 * blk + blk // 2),
    )(proj, proj, proj, proj, lb, gain)


def _gla_bwd(proj, lb, gain, o_pre, states, do_hg):
    nck = GLA_NCK
    do, dg, dgain = _gla_bwd_norm(proj, gain, o_pre, do_hg, 256)

    def body(q_ref, f_ref, v_ref, lb_ref, do_ref, st_ref, dq_ref, df_ref, dv_ref, dlb_ref,
             dq_acc, dv_acc, dst_scr, cs_scr, dec_scr):
        d = pl.program_id(1)
        masks = _gla_masks(d)
        mask, maskb, _ = masks
        lbv = lb_ref[...]

        def intra(s, carry):
            rows = pl.ds(pl.multiple_of(s * GLA_ROWS, GLA_ROWS), GLA_ROWS)
            ck = _gla_block(q_ref[rows, :], f_ref[rows, :], lbv, masks)
            doc = do_ref[rows, :]
            for j in range(GLA_CPB):
                c = s * GLA_CPB + j
                cr = _gla_chunk_rows(j)
                cs_scr[c] = _dot(doc[cr, :], ck["qt"][cr, :], _TN)
                dec_scr[c] = jnp.exp(ck["bt"][j * HG_CHUNK:j * HG_CHUNK + 8, :])
            return carry

        lax.fori_loop(0, GLA_NBLK, intra, 0)

        def scan(i, dst):
            c = jnp.where(d == 0, nck - 1 - i, i)
            dst_scr[c] = dst
            return dst * dec_scr[c][0:1, :] + cs_scr[c]

        lax.fori_loop(0, nck, scan, jnp.zeros((HEAD_DIM, HEAD_DIM), F32), unroll=4)

        def grads(s, dlb):
            rows = pl.ds(pl.multiple_of(s * GLA_ROWS, GLA_ROWS), GLA_ROWS)
            qc = q_ref[rows, :]
            ck = _gla_block(qc, f_ref[rows, :], lbv, masks)
            v = v_ref[rows, :]
            doc = do_ref[rows, :]
            qt, kt, kh, a = ck["qt"], ck["kt"], ck["kh"], ck["a"]
            da = jnp.where(mask, _dot(doc, v, _NT), 0.0)
            dqt_i = _dot(da, kt, _NN)
            dkt = _dot(da, qt, _TN)
            dv_i = _dot(a, doc, _TN)
            dqt_p, dv_p, dkh_p, dbt_p = [], [], [], []
            for j in range(GLA_CPB):
                c = s * GLA_CPB + j
                cr = _gla_chunk_rows(j)
                st_in = st_ref[c]
                dst = dst_scr[c]
                dqt_p.append(dqt_i[cr, :] + _dot(doc[cr, :], st_in, _NN))
                dv_p.append(dv_i[cr, :] + _dot(kh[cr, :], dst, _NT))
                dkh_j = _dot(v[cr, :], dst, _NN)
                dkh_p.append(dkh_j)
                dbt_j = (dec_scr[c][0:1, :] * jnp.sum(dst * st_in, axis=0, keepdims=True)
                         + jnp.sum(dkh_j * kh[cr, :], axis=0, keepdims=True))
                dbt_p.append(jnp.broadcast_to(dbt_j, (HG_CHUNK, HEAD_DIM)))
            dqt = jnp.concatenate(dqt_p, axis=0)
            dv = jnp.concatenate(dv_p, axis=0)
            dkh = jnp.concatenate(dkh_p, axis=0)
            dbt = jnp.concatenate(dbt_p, axis=0)
            db = dqt * qt - dkt * kt - dkh * kh
            dq = dqt * ck["eb"]
            dk = dkt * jnp.exp(-ck["b"]) + dkh * jnp.exp(ck["bt"] - ck["b"])
            dlogf = _dot_split(maskb, db, _TN) + dbt
            dforget = dlogf / ck["forget"] - dk
            sf = ck["sf"]
            df_ref[rows, :] = (dforget * (1.0 - lbv) * sf * (1.0 - sf)).astype(BF16)
            sq = ck["sq"]
            dqc = dq * HG_SCALE * (sq + qc * sq * (1.0 - sq))

            @pl.when(d == 0)
            def _():
                dq_acc[rows, :] = dqc
                dv_acc[rows, :] = dv

            @pl.when(d == 1)
            def _():
                dq_acc[rows, :] += dqc
                dv_acc[rows, :] += dv

            return dlb + jnp.sum(dforget * (1.0 - sf), axis=0, keepdims=True)

        dlb_ref[...] = lax.fori_loop(0, GLA_NBLK, grads, jnp.zeros((1, HEAD_DIM), F32))

        @pl.when(d == 1)
        def _():
            dq_ref[...] = dq_acc[...].astype(BF16)
            dv_ref[...] = dv_acc[...].astype(BF16)

    blk = SEQ * HEAD_DIM * 4
    bshape = jax.ShapeDtypeStruct((SEQ, D_MODEL), BF16)
    state_bytes = nck * HEAD_DIM * HEAD_DIM * 4
    dq, df, dv, dlb = pl.pallas_call(
        body, name="gla_bwd", grid=(HG_HEADS, 2),
        in_specs=[_head_spec(CB_HG_Q),
                  pl.BlockSpec((SEQ, HEAD_DIM), lambda h, d: (0, CB_F + 8 * d + h)),
                  _head_spec(CB_HG_I),
                  pl.BlockSpec((None, None, 1, HEAD_DIM), lambda h, d: (d, h, 0, 0)),
                  _head_spec(0),
                  pl.BlockSpec((None, None, nck, HEAD_DIM, HEAD_DIM), lambda h, d: (h, d, 0, 0, 0))],
        out_specs=[_head_spec(0),
                   pl.BlockSpec((None, SEQ, HEAD_DIM), lambda h, d: (d, 0, h)),
                   _head_spec(0),
                   pl.BlockSpec((None, None, 1, HEAD_DIM), lambda h, d: (d, h, 0, 0))],
        out_shape=[bshape, jax.ShapeDtypeStruct((2, SEQ, D_MODEL), BF16), bshape,
                   jax.ShapeDtypeStruct((2, HG_HEADS, 1, HEAD_DIM), F32)],
        scratch_shapes=[pltpu.VMEM((SEQ, HEAD_DIM), F32)] * 2
        + [pltpu.VMEM((nck, HEAD_DIM, HEAD_DIM), F32)] * 2 + [pltpu.VMEM((nck, 8, HEAD_DIM), F32)],
        compiler_params=_params(("parallel", "arbitrary"), 8 * blk + 4 * state_bytes + 3 * blk + 2 * blk),
    )(proj, proj, proj, lb, do, states)
    return dq, df, dv, dg, dlb, dgain


def _gla_bwd_norm(proj, gain, o_pre, do_hg, tr):
    w = D_MODEL

    def body(g_ref, gain_ref, opre_ref, dohg_ref, do_ref, dg_ref, dgain_ref):
        gainv = gain_ref[...]
        part = jnp.zeros((1, HEAD_DIM), F32)
        for h in range(HG_HEADS):
            hs = slice(h * HEAD_DIM, (h + 1) * HEAD_DIM)
            o = opre_ref[:, hs]
            r = lax.rsqrt(jnp.mean(o * o, axis=-1, keepdims=True) + RMS_EPS)
            ohat = o * r
            g = g_ref[:, hs]
            sg = _sigmoid(g)
            silu = g * sg
            dout = dohg_ref[:, hs]
            dg_ref[:, hs] = (dout * ohat * gainv * (sg + silu * (1.0 - sg))).astype(BF16)
            dy = dout * silu
            part = part + jnp.sum(dy * ohat, axis=0, keepdims=True)
            dn = dy * gainv
            do_ref[:, hs] = r * (dn - ohat * jnp.mean(dn * ohat, axis=-1, keepdims=True))

        @pl.when(pl.program_id(0) == 0)
        def _():
            dgain_ref[...] = part

        @pl.when(pl.program_id(0) > 0)
        def _():
            dgain_ref[...] += part

    return pl.pallas_call(
        body, name="gla_bwd_norm", grid=(SEQ // tr,),
        in_specs=[_row_spec(tr, w, CB_HG_G * LANE // w), _bcast_spec(HEAD_DIM), _row_spec(tr, w), _row_spec(tr, w)],
        out_specs=[_row_spec(tr, w), _row_spec(tr, w), _bcast_spec(HEAD_DIM)],
        out_shape=[jax.ShapeDtypeStruct((SEQ, w), F32), jax.ShapeDtypeStruct((SEQ, w), BF16),
                   jax.ShapeDtypeStruct((1, HEAD_DIM), F32)],
        compiler_params=_params(("arbitrary",), 16 * tr * w * 4),
    )(proj, gain, o_pre, do_hg)


def _da_residue_rows(r, n0, size, d):
    if d == 1:
        return pl.ds(pl.multiple_of(n0, 8), size)
    return pl.ds(r + n0 * d, size, stride=d)


def _da_rmsnorm(x, gain):
    r = lax.rsqrt(jnp.mean(x * x, axis=-1, keepdims=True) + RMS_EPS)
    return x * r, r, x * r * gain


def _da_scores(qn_scr, kn_scr, slope, i, ld):
    w0 = jnp.clip(i * DA_QB - DA_RADIUS, 0, ld - DA_WIN)
    w0 = pl.multiple_of(w0, DA_RADIUS)
    qrows = pl.ds(pl.multiple_of(i * DA_QB, DA_QB), DA_QB)
    win = pl.ds(w0, DA_WIN)
    qb = qn_scr[qrows, :]
    kw = kn_scr[win, :]
    s = _dot(qb, kw, _NT) * DA_SCALE
    qpos = i * DA_QB + lax.broadcasted_iota(jnp.int32, (DA_QB, DA_WIN), 0)
    kpos = w0 + lax.broadcasted_iota(jnp.int32, (DA_QB, DA_WIN), 1)
    arel = jnp.abs(kpos - qpos)
    s = s - slope * arel.astype(F32)
    s = jnp.where(arel <= DA_RADIUS, s, NEG_INF)
    return s, qb, kw, qrows, win


def _da_slopes(group):
    d = DA_DILATIONS[group]
    sl = _alibi_slopes(DA_HEADS)[4 * group:4 * group + 4] * d
    return jnp.asarray(np.broadcast_to(sl[:, None, None], (4, 1, LANE)).copy())


def _da_fwd(proj, gq, gk, group):
    d = DA_DILATIONS[group]
    ld = SEQ // d
    nqb = ld // DA_QB

    def body(q_ref, k_ref, v_ref, gq_ref, gk_ref, sl_ref, o_ref, lse_ref, qn_scr, kn_scr, v_scr):
        slope = sl_ref[:, 0:1]

        def residue(r, carry):
            sel = _da_residue_rows(r, 0, ld, d)
            qn_scr[...] = _da_rmsnorm(q_ref[sel, :], gq_ref[...])[2].astype(BF16)
            kn_scr[...] = _da_rmsnorm(k_ref[sel, :], gk_ref[...])[2].astype(BF16)
            v_scr[...] = v_ref[sel, :].astype(BF16)

            def step(i, c2):
                s, _, _, _, win = _da_scores(qn_scr, kn_scr, slope, i, ld)
                m = jnp.max(s, axis=-1, keepdims=True)
                p = jnp.exp(s - m)
                l = jnp.sum(p, axis=-1, keepdims=True)
                out = _da_residue_rows(r, i * DA_QB, DA_QB, d)
                o_ref[out, :] = _dot(p, v_scr[win, :], _NN) / l
                lse_ref[out, :] = jnp.broadcast_to(m + jnp.log(l), (DA_QB, HEAD_DIM))
                return c2

            return lax.fori_loop(0, nqb, step, carry)

        lax.fori_loop(0, d, residue, 0)

    seq_spec = lambda base: pl.BlockSpec((SEQ, HEAD_DIM), lambda h: (0, base + 4 * group + h))
    out_spec = pl.BlockSpec((SEQ, HEAD_DIM), lambda h: (0, h))
    gain_spec = pl.BlockSpec((1, HEAD_DIM), lambda h: (0, 0))
    blk = SEQ * HEAD_DIM * 4
    return pl.pallas_call(
        body, name=f"da_fwd_g{group}", grid=(DA_HEADS_PER_GROUP,),
        in_specs=[seq_spec(CB_DA_Q), seq_spec(CB_DA_K), seq_spec(CB_DA_V), gain_spec, gain_spec,
                  pl.BlockSpec((None, 1, LANE), lambda h: (h, 0, 0))],
        out_specs=[out_spec, out_spec],
        out_shape=[jax.ShapeDtypeStruct((SEQ, DA_WIDTH), F32)] * 2,
        scratch_shapes=[pltpu.VMEM((ld, HEAD_DIM), BF16)] * 3,
        compiler_params=_params(("parallel",), 10 * blk + 2 * blk),
    )(proj, proj, proj, gq, gk, _da_slopes(group))


def _da_merge(os, lses, tr):
    w = DA_WIDTH

    def body(o0, o1, o2, l0, l1, l2, ob_ref, of_ref, lse_ref):
        la, lb_, lc = l0[...], l1[...], l2[...]
        m = jnp.maximum(jnp.maximum(la, lb_), lc)
        ea, eb, ec = jnp.exp(la - m), jnp.exp(lb_ - m), jnp.exp(lc - m)
        tot = ea + eb + ec
        o = (ea * o0[...] + eb * o1[...] + ec * o2[...]) / tot
        of_ref[...] = o
        ob_ref[...] = o.astype(BF16)
        lse_ref[...] = m + jnp.log(tot)

    return pl.pallas_call(
        body, name="da_merge", grid=(SEQ // tr,), in_specs=[_row_spec(tr, w)] * 6,
        out_specs=[_row_spec(tr, w)] * 3,
        out_shape=[jax.ShapeDtypeStruct((SEQ, w), BF16), jax.ShapeDtypeStruct((SEQ, w), F32),
                   jax.ShapeDtypeStruct((SEQ, w), F32)],
        compiler_params=_params(("parallel",), 24 * tr * w * 4),
    )(*os, *lses)


def _da_rowdot(do, o, tr):
    w = DA_WIDTH

    def body(do_ref, o_ref, out_ref):
        prod = do_ref[...] * o_ref[...]
        for h in range(w // HEAD_DIM):
            sl = slice(h * HEAD_DIM, (h + 1) * HEAD_DIM)
            out_ref[:, sl] = jnp.broadcast_to(jnp.sum(prod[:, sl], axis=-1, keepdims=True), (tr, HEAD_DIM))

    return pl.pallas_call(
        body, name="da_rowdot", grid=(SEQ // tr,), in_specs=[_row_spec(tr, w)] * 2,
        out_specs=_row_spec(tr, w), out_shape=jax.ShapeDtypeStruct((SEQ, w), F32),
        compiler_params=_params(("parallel",), 10 * tr * w * 4),
    )(do, o)


def _da_bwd(proj, gq, gk, do, lse, dd, group):
    d = DA_DILATIONS[group]
    ld = SEQ // d
    nqb = ld // DA_QB

    def body(q_ref, k_ref, v_ref, gq_ref, gk_ref, sl_ref, do_ref, lse_ref, dd_ref,
             dq_ref, dk_ref, dv_ref, dgq_ref, dgk_ref,
             qn_scr, kn_scr, v_scr, dqn_scr, dkn_scr, dvr_scr, dq_scr, dk_scr, dv_scr):
        gqv, gkv = gq_ref[...], gk_ref[...]
        slope = sl_ref[:, 0:1]

        def residue(r, carry):
            sel = _da_residue_rows(r, 0, ld, d)
            qn_scr[...] = _da_rmsnorm(q_ref[sel, :], gqv)[2].astype(BF16)
            kn_scr[...] = _da_rmsnorm(k_ref[sel, :], gkv)[2].astype(BF16)
            v_scr[...] = v_ref[sel, :].astype(BF16)
            dkn_scr[...] = jnp.zeros_like(dkn_scr)
            dvr_scr[...] = jnp.zeros_like(dvr_scr)

            def step(i, c2):
                s, qb, kw, qrows, win = _da_scores(qn_scr, kn_scr, slope, i, ld)
                src = _da_residue_rows(r, i * DA_QB, DA_QB, d)
                p = jnp.exp(s - lse_ref[src, :][:, 0:1])
                dob = do_ref[src, :]
                dvr_scr[win, :] += _dot(p, dob, _TN)
                dp = _dot(dob, v_scr[win, :], _NT)
                ds = p * (dp - dd_ref[src, :][:, 0:1]) * DA_SCALE
                dqn_scr[qrows, :] = _dot(ds, kw, _NN)
                dkn_scr[win, :] += _dot(ds, qb, _TN)
                return c2

            lax.fori_loop(0, nqb, step, 0)

            parts = []
            for x_ref, gv, dn_scr, dx_scr in ((q_ref, gqv, dqn_scr, dq_scr), (k_ref, gkv, dkn_scr, dk_scr)):
                hat, rstd, _ = _da_rmsnorm(x_ref[sel, :], gv)
                dn = dn_scr[...]
                dyg = dn * gv
                dx_scr[sel, :] = rstd * (dyg - hat * jnp.mean(dyg * hat, axis=-1, keepdims=True))
                parts.append(jnp.sum(dn * hat, axis=0, keepdims=True))
            dv_scr[sel, :] = dvr_scr[...]
            return carry[0] + parts[0], carry[1] + parts[1]

        zero = jnp.zeros((1, HEAD_DIM), F32)
        pq, pk = lax.fori_loop(0, d, residue, (zero, zero))

        @pl.when(pl.program_id(0) == 0)
        def _():
            dgq_ref[...] = pq
            dgk_ref[...] = pk

        @pl.when(pl.program_id(0) > 0)
        def _():
            dgq_ref[...] += pq
            dgk_ref[...] += pk

        dq_ref[...] = dq_scr[...].astype(BF16)
        dk_ref[...] = dk_scr[...].astype(BF16)
        dv_ref[...] = dv_scr[...].astype(BF16)

    seq_spec = lambda base: pl.BlockSpec((SEQ, HEAD_DIM), lambda h: (0, base + 4 * group + h))
    out_spec = pl.BlockSpec((SEQ, HEAD_DIM), lambda h: (0, h))
    gain_spec = pl.BlockSpec((1, HEAD_DIM), lambda h: (0, 0))
    oshape = jax.ShapeDtypeStruct((SEQ, DA_WIDTH), BF16)
    gshape = jax.ShapeDtypeStruct((1, HEAD_DIM), F32)
    blk = SEQ * HEAD_DIM * 4
    return pl.pallas_call(
        body, name=f"da_bwd_g{group}", grid=(DA_HEADS_PER_GROUP,),
        in_specs=[seq_spec(CB_DA_Q), seq_spec(CB_DA_K), seq_spec(CB_DA_V), gain_spec, gain_spec,
                  pl.BlockSpec((None, 1, LANE), lambda h: (h, 0, 0)), out_spec, out_spec, out_spec],
        out_specs=[out_spec, out_spec, out_spec, gain_spec, gain_spec],
        out_shape=[oshape, oshape, oshape, gshape, gshape],
        scratch_shapes=[pltpu.VMEM((ld, HEAD_DIM), BF16)] * 3 + [pltpu.VMEM((ld, HEAD_DIM), F32)] * 3
        + [pltpu.VMEM((SEQ, HEAD_DIM), F32)] * 3,
        compiler_params=_params(("arbitrary",), 12 * blk + 3 * blk + 3 * blk + 3 * blk),
    )(proj, proj, proj, gq, gk, _da_slopes(group), do, lse, dd)


def _mem_softmax(q, k, gq, gk):
    qhat, rq, qn = _da_rmsnorm(q, gq)
    khat, rk, kn = _da_rmsnorm(k, gk)
    s = _dot(qn, kn, _NT) * MEM_SCALE
    m = jnp.max(s, axis=-1, keepdims=True)
    e = jnp.exp(s - m)
    p = e / jnp.sum(e, axis=-1, keepdims=True)
    return p, (qhat, rq, qn), (khat, rk, kn)


def _mem_fwd(proj, kv, gq, gk, tq):
    def body(q_ref, k_ref, v_ref, gq_ref, gk_ref, o_ref):
        p, _, _ = _mem_softmax(q_ref[...], k_ref[...], gq_ref[...], gk_ref[...])
        o_ref[...] = _dot(p, v_ref[...], _NN).astype(BF16)

    gain_spec = pl.BlockSpec((1, HEAD_DIM), lambda h, i: (0, 0))
    return pl.pallas_call(
        body, name="mem_fwd", grid=(MEM_HEADS, SEQ // tq),
        in_specs=[pl.BlockSpec((tq, HEAD_DIM), lambda h, i: (i, CB_MEM_Q + h)),
                  pl.BlockSpec((N_MEM, HEAD_DIM), lambda h, i: (0, h)),
                  pl.BlockSpec((N_MEM, HEAD_DIM), lambda h, i: (0, MEM_HEADS + h)), gain_spec, gain_spec],
        out_specs=pl.BlockSpec((tq, HEAD_DIM), lambda h, i: (i, h)),
        out_shape=jax.ShapeDtypeStruct((SEQ, MEM_WIDTH), BF16),
        compiler_params=_params(("parallel", "parallel"), 16 * tq * N_MEM * 4),
    )(proj, kv, kv, gq, gk)


def _mem_bwd(proj, kv, gq, gk, do, tq):
    nq = SEQ // tq

    def body(q_ref, k_ref, v_ref, gq_ref, gk_ref, do_ref, dq_ref, dk_ref, dv_ref, dgq_ref, dgk_ref, dkn_scr):
        h, i = pl.program_id(0), pl.program_id(1)
        gqv, gkv = gq_ref[...], gk_ref[...]
        p, (qhat, rq, qn), (khat, rk, kn) = _mem_softmax(q_ref[...], k_ref[...], gqv, gkv)
        dob = do_ref[...]
        dvp = _dot(p, dob, _TN)
        dp = _dot(dob, v_ref[...], _NT)
        ds = p * (dp - jnp.sum(p * dp, axis=-1, keepdims=True)) * MEM_SCALE
        dqn = _dot(ds, kn, _NN)
        dknp = _dot(ds, qn, _TN)
        dyg = dqn * gqv
        dq_ref[...] = (rq * (dyg - qhat * jnp.mean(dyg * qhat, axis=-1, keepdims=True))).astype(BF16)
        dgq_part = jnp.sum(dqn * qhat, axis=0, keepdims=True)
        first = jnp.logical_and(h == 0, i == 0)

        @pl.when(first)
        def _():
            dgq_ref[...] = dgq_part

        @pl.when(jnp.logical_not(first))
        def _():
            dgq_ref[...] += dgq_part

        @pl.when(i == 0)
        def _():
            dv_ref[...] = dvp
            dkn_scr[...] = dknp

        @pl.when(i > 0)
        def _():
            dv_ref[...] += dvp
            dkn_scr[...] += dknp

        @pl.when(i == nq - 1)
        def _():
            dkn = dkn_scr[...]
            dkg = dkn * gkv
            dk_ref[...] = rk * (dkg - khat * jnp.mean(dkg * khat, axis=-1, keepdims=True))
            dgk_part = jnp.sum(dkn * khat, axis=0, keepdims=True)

            @pl.when(h == 0)
            def _():
                dgk_ref[...] = dgk_part

            @pl.when(h > 0)
            def _():
                dgk_ref[...] += dgk_part

    gain_spec = pl.BlockSpec((1, HEAD_DIM), lambda h, i: (0, 0))
    kvout = pl.BlockSpec((N_MEM, HEAD_DIM), lambda h, i: (0, h))
    return pl.pallas_call(
        body, name="mem_bwd", grid=(MEM_HEADS, nq),
        in_specs=[pl.BlockSpec((tq, HEAD_DIM), lambda h, i: (i, CB_MEM_Q + h)),
                  pl.BlockSpec((N_MEM, HEAD_DIM), lambda h, i: (0, h)),
                  pl.BlockSpec((N_MEM, HEAD_DIM), lambda h, i: (0, MEM_HEADS + h)), gain_spec, gain_spec,
                  pl.BlockSpec((tq, HEAD_DIM), lambda h, i: (i, h))],
        out_specs=[pl.BlockSpec((tq, HEAD_DIM), lambda h, i: (i, h)), kvout, kvout, gain_spec, gain_spec],
        out_shape=[jax.ShapeDtypeStruct((SEQ, MEM_WIDTH), BF16), jax.ShapeDtypeStruct((N_MEM, MEM_WIDTH), F32),
                   jax.ShapeDtypeStruct((N_MEM, MEM_WIDTH), F32), jax.ShapeDtypeStruct((1, HEAD_DIM), F32),
                   jax.ShapeDtypeStruct((1, HEAD_DIM), F32)],
        scratch_shapes=[pltpu.VMEM((N_MEM, HEAD_DIM), F32)],
        compiler_params=_params(("arbitrary", "arbitrary"), 24 * tq * N_MEM * 4),
    )(proj, kv, kv, gq, gk, do)


def _mesh_position():
    return lax.axis_index("x"), lax.axis_index("y"), lax.axis_index("c")


def _any_spec():
    return pl.BlockSpec(memory_space=pl.ANY)


def _all_gather(shards):
    n = len(shards)

    def body(*refs):
        ins, outs = refs[:n], refs[n:2 * n]
        send_sems, recv_sems, local_sems = refs[2 * n:]
        x, y, c = _mesh_position()
        me, sibling = (x, y, c), (x, y, 1 - c)
        chips = [(1 - x, y), (x, 1 - y), (1 - x, 1 - y)]

        def copy(w, k, block, to, src=None):
            px, py, pc = block
            rows = outs[w].at[4 * px + 2 * py + pc]
            return pltpu.make_async_remote_copy(
                src_ref=rows if src is None else src, dst_ref=rows,
                send_sem=send_sems.at[7 * w + k], recv_sem=recv_sems.at[7 * w + k],
                device_id=to, device_id_type=MESH)

        started = []
        for w in range(n):
            mine = pltpu.make_async_copy(ins[w], outs[w].at[4 * x + 2 * y + c], local_sems.at[w])
            mine.start()
            started.append(mine)
        sends = []
        for w in range(n):
            first = [copy(w, 0, me, sibling, src=ins[w])]
            first += [copy(w, 1 + j, me, (*chip, c), src=ins[w]) for j, chip in enumerate(chips)]
            for cp in first:
                cp.start()
            sends += first
        for w in range(n):
            for j, chip in enumerate(chips):
                copy(w, 1 + j, (*chip, c), me).wait_recv()
                passed = copy(w, 4 + j, (*chip, c), sibling)
                passed.start()
                sends.append(passed)
        for w in range(n):
            copy(w, 0, sibling, me).wait_recv()
            for j, chip in enumerate(chips):
                copy(w, 4 + j, (*chip, 1 - c), me).wait_recv()
        for cp in sends:
            cp.wait_send()
        for mine in started:
            mine.wait()

    return pl.pallas_call(
        body, name="weights_all_gather",
        in_specs=[_any_spec()] * n, out_specs=[_any_spec()] * n,
        out_shape=[jax.ShapeDtypeStruct((N_DEV,) + s.shape, s.dtype) for s in shards],
        scratch_shapes=[pltpu.SemaphoreType.DMA((7 * n,)), pltpu.SemaphoreType.DMA((7 * n,)),
                        pltpu.SemaphoreType.DMA((n,))],
    )(*shards)


def _chip_of(j, x, y):
    return (1 - x if j & 1 else x, 1 - y if j & 2 else y)


_HBM_SPEC = pl.BlockSpec(memory_space=pltpu.HBM)
_SEM_SPEC = pl.BlockSpec(memory_space=pltpu.SEMAPHORE)
_DATAFLOW_EFFECT = pltpu.SideEffectType.DATAFLOW_SIDE_EFFECTING
TOKEN_SHAPE = (8, D_MODEL)


def _copies_start(name, arrays, n_copies, plan):
    n = len(arrays)

    def body(*refs):
        send_sems, recv_sems, token = refs[n], refs[n + 1], refs[2 * n + 2]
        copies = plan(refs[:n])
        assert len(copies) == n_copies
        for k, (src, dst, dev) in enumerate(copies):
            pltpu.make_async_remote_copy(src_ref=src, dst_ref=dst, send_sem=send_sems.at[k], recv_sem=recv_sems.at[k],
                                         device_id=dev, device_id_type=MESH).start()
        token[...] = jnp.zeros_like(token)

    outs = pl.pallas_call(
        body, name=name,
        out_shape=(pltpu.SemaphoreType.DMA((n_copies,)), pltpu.SemaphoreType.DMA((n_copies,)),
                   *[pltpu.HBM(a.shape, a.dtype) for a in arrays], jax.ShapeDtypeStruct(TOKEN_SHAPE, F32)),
        in_specs=[_HBM_SPEC] * n,
        out_specs=(_SEM_SPEC, _SEM_SPEC, *[_HBM_SPEC] * n, pl.BlockSpec(memory_space=pltpu.VMEM)),
        input_output_aliases={i: i + 2 for i in range(n)},
        compiler_params=pltpu.CompilerParams(has_side_effects=_DATAFLOW_EFFECT),
    )(*[pltpu.with_memory_space_constraint(a, pltpu.HBM) for a in arrays])
    return outs[0], outs[1], list(outs[2:2 + n]), outs[2 + n]


def _copies_wait(name, send_sems, recv_sems, arrays, n_copies, plan, after):
    n = len(arrays)
    after = list(after) if isinstance(after, (list, tuple)) else [after]

    def body(*refs):
        send_ref, recv_ref = refs[n], refs[n + 1]
        copies = plan(refs[:n])
        assert len(copies) == n_copies
        for k, (src, dst, dev) in enumerate(copies):
            cp = pltpu.make_async_remote_copy(src_ref=src, dst_ref=dst, send_sem=send_ref.at[k], recv_sem=recv_ref.at[k],
                                              device_id=dev, device_id_type=MESH)
            cp.wait_send()
            cp.wait_recv()

    outs = pl.pallas_call(
        body, name=name, out_shape=tuple(pltpu.HBM(a.shape, a.dtype) for a in arrays),
        in_specs=[_HBM_SPEC] * n + [_SEM_SPEC, _SEM_SPEC] + [pl.BlockSpec(memory_space=pl.ANY)] * len(after),
        out_specs=tuple([_HBM_SPEC] * n), input_output_aliases={i: i for i in range(n)},
        compiler_params=pltpu.CompilerParams(has_side_effects=_DATAFLOW_EFFECT),
    )(*arrays, send_sems, recv_sems, *after)
    return list(outs)


def _after(small, token):
    return small if token is None else small + token[0:1, :small.shape[-1]]


def _gather_plan_out(n):
    def plan(refs):
        x, y, c = _mesh_position()
        me = 4 * x + 2 * y + c
        copies = []
        for w in range(n):
            land = refs[n + w].at[me]
            copies.append((refs[w], land, (x, y, 1 - c)))
            for j in range(1, 4):
                copies.append((refs[w], land, (*_chip_of(j, x, y), c)))
        return copies
    return plan


def _gather_plan_pass(n):
    def plan(refs):
        x, y, c = _mesh_position()
        copies = []
        for w in range(n):
            for j in range(1, 4):
                px, py = _chip_of(j, x, y)
                rows = refs[w].at[4 * px + 2 * py + c]
                copies.append((rows, rows, (x, y, 1 - c)))
        return copies
    return plan


def _reduce_plan_sibling(n):
    def plan(refs):
        x, y, c = _mesh_position()
        copies = []
        for w in range(n):
            for j in range(4):
                px, py = _chip_of(j, x, y)
                copies.append((refs[w].at[4 * px + 2 * py + (1 - c)], refs[n + w].at[j], (x, y, 1 - c)))
        return copies
    return plan


def _reduce_plan_chips(n):
    def plan(refs):
        x, y, c = _mesh_position()
        copies = []
        for w in range(n):
            for j in range(1, 4):
                copies.append((refs[w].at[j - 1], refs[n + w].at[j - 1], (*_chip_of(j, x, y), c)))
        return copies
    return plan


def _chip_partials(grad, recv, name, tr):
    _, rows, width = grad.shape

    def body(g_ref, r_ref, own_ref, other_ref):
        x, y, c = _mesh_position()
        for j in range(4):
            px, py = _chip_of(j, x, y)
            total = g_ref[4 * px + 2 * py + c].astype(F32) + r_ref[j].astype(F32)
            if j == 0:
                own_ref[...] = total
            else:
                other_ref[j - 1] = total.astype(BF16)

    return pl.pallas_call(
        body, name=name, grid=(rows // tr,),
        in_specs=[pl.BlockSpec((N_DEV, tr, width), lambda i: (0, i, 0)),
                  pl.BlockSpec((4, tr, width), lambda i: (0, i, 0))],
        out_specs=[pl.BlockSpec((tr, width), lambda i: (i, 0)), pl.BlockSpec((3, tr, width), lambda i: (0, i, 0))],
        out_shape=[jax.ShapeDtypeStruct((rows, width), F32), jax.ShapeDtypeStruct((3, rows, width), BF16)],
        compiler_params=_params(("parallel",), 2 * 20 * tr * width * 2 + 8 * tr * width * 4),
    )(grad, recv)


def _adamw_math(w, g, m, v):
    m = ADAM_B1 * m + (1.0 - ADAM_B1) * g
    v = ADAM_B2 * v + (1.0 - ADAM_B2) * (g * g)
    m_hat = m / (1.0 - ADAM_B1 ** ADAM_STEP)
    v_hat = v / (1.0 - ADAM_B2 ** ADAM_STEP)
    delta = -ADAM_LR * (m_hat / (jnp.sqrt(v_hat) + ADAM_EPS) + ADAM_WD * w)
    return delta, m, v


def _adamw_shard(own, recv, w, m, v, name, tr):
    rows, width = own.shape

    def body(own_ref, r_ref, w_ref, m_ref, v_ref, g_ref, d_ref, nm_ref, nv_ref):
        g = own_ref[...]
        for j in range(3):
            g = g + r_ref[j].astype(F32)
        g_ref[...] = g
        d_ref[...], nm_ref[...], nv_ref[...] = _adamw_math(w_ref[...], g, m_ref[...], v_ref[...])

    spec = pl.BlockSpec((tr, width), lambda i: (i, 0))
    shape = jax.ShapeDtypeStruct((rows, width), F32)
    return pl.pallas_call(
        body, name=name, grid=(rows // tr,),
        in_specs=[spec, pl.BlockSpec((3, tr, width), lambda i: (0, i, 0)), spec, spec, spec],
        out_specs=[spec] * 4, out_shape=[shape] * 4,
        compiler_params=_params(("parallel",), 22 * tr * width * 4),
    )(own, recv, w, m, v)


def _small_all_reduce_adamw(gpart, lbpack, wpack, mpack, vpack, after):
    def body(gp_ref, lb_ref, w_ref, m_ref, v_ref, after_ref, g_ref, d_ref, nm_ref, nv_ref, gath_ref, send_sems, recv_sems):
        del after_ref
        x, y, c = _mesh_position()
        me = 4 * x + 2 * y + c
        gath_ref[me] = gp_ref[...]
        copies = []
        for j in range(1, N_DEV):
            peer = (x ^ (j >> 2), y ^ ((j >> 1) & 1), c ^ (j & 1))
            cp = pltpu.make_async_remote_copy(
                src_ref=gp_ref, dst_ref=gath_ref.at[me], send_sem=send_sems.at[j - 1], recv_sem=recv_sems.at[j - 1],
                device_id=peer, device_id_type=MESH)
            cp.start()
            copies.append(cp)
        for cp in copies:
            cp.wait()
        tot = gath_ref[0]
        for s in range(1, N_DEV):
            tot = tot + gath_ref[s]
        lb = lb_ref[...]
        dl = tot[16:32, :] * lb * (1.0 - lb)
        g = jnp.concatenate([tot[0:16, :], dl[0:8, :], -dl[0:8, :], dl[8:16, :], -dl[8:16, :],
                             tot[32:SMALL_GRAD_ROWS, :]], axis=0)
        g_ref[...] = g
        d_ref[...], nm_ref[...], nv_ref[...] = _adamw_math(w_ref[...], g, m_ref[...], v_ref[...])

    vm = pl.BlockSpec(memory_space=pltpu.VMEM)
    shape = jax.ShapeDtypeStruct((SMALL_ROWS, LANE), F32)
    return pl.pallas_call(
        body, name="small_all_reduce_adamw", in_specs=[vm] * 5 + [_any_spec()], out_specs=[vm] * 4,
        out_shape=[shape] * 4,
        scratch_shapes=[pltpu.VMEM((N_DEV, SMALL_GRAD_ROWS, LANE), F32),
                        pltpu.SemaphoreType.DMA((N_DEV - 1,)), pltpu.SemaphoreType.DMA((N_DEV - 1,))],
    )(gpart, lbpack, wpack, mpack, vpack, after)


_SMALL_NAMES = ("norm_mix_gain", "norm_mem_gain", "lb_logits_fw", "lb_logits_bw", "norm_ffn_gain",
                "hg_norm_gain", "da_q_gain", "da_k_gain", "mem_q_gain", "mem_k_gain")
_SMALL_ROW0 = {"norm_mix_gain": 0, "norm_mem_gain": 8, "lb_logits_fw": 16, "lb_logits_bw": 32, "norm_ffn_gain": 48,
               "hg_norm_gain": 56, "da_q_gain": 64, "da_k_gain": 72, "mem_q_gain": 80, "mem_k_gain": 88}
_LOSS_ROW = 96


def _pack_rows(parts, total_rows):
    rows = []
    for p in parts:
        r = p.reshape(-1, LANE)
        rows.append(jnp.pad(r, ((0, -r.shape[0] % 8), (0, 0))))
    used = sum(r.shape[0] for r in rows)
    if total_rows > used:
        rows.append(jnp.zeros((total_rows - used, LANE), F32))
    return jnp.concatenate(rows, axis=0)


def _unpack_small(pack, like):
    out = {}
    for name in _SMALL_NAMES:
        n = like[name].size // LANE
        r0 = _SMALL_ROW0[name]
        out[name] = pack[r0:r0 + n].reshape(like[name].shape)
    return out


def kernel(x, mem, norm_mix_gain, norm_mem_gain, w_in, lb_logits_fw, lb_logits_bw, hg_norm_gain, da_q_gain, da_k_gain, w_mem_kv, mem_q_gain, mem_k_gain, w_proj_hg, w_proj_da, w_proj_mem, w_out, norm_ffn_gain, w_ffn_in, w_ffn_out, loss_target, m_norm_mix_gain, m_norm_mem_gain, m_w_in, m_lb_logits_fw, m_lb_logits_bw, m_hg_norm_gain, m_da_q_gain, m_da_k_gain, m_w_mem_kv, m_mem_q_gain, m_mem_k_gain, m_w_proj_hg, m_w_proj_da, m_w_proj_mem, m_w_out, m_norm_ffn_gain, m_w_ffn_in, m_w_ffn_out, v_norm_mix_gain, v_norm_mem_gain, v_w_in, v_lb_logits_fw, v_lb_logits_bw, v_hg_norm_gain, v_da_q_gain, v_da_k_gain, v_w_mem_kv, v_mem_q_gain, v_mem_k_gain, v_w_proj_hg, v_w_proj_da, v_w_proj_mem, v_w_out, v_norm_ffn_gain, v_w_ffn_in, v_w_ffn_out):
    small_w = dict(norm_mix_gain=norm_mix_gain, norm_mem_gain=norm_mem_gain, lb_logits_fw=lb_logits_fw,
                   lb_logits_bw=lb_logits_bw, norm_ffn_gain=norm_ffn_gain, hg_norm_gain=hg_norm_gain,
                   da_q_gain=da_q_gain, da_k_gain=da_k_gain, mem_q_gain=mem_q_gain, mem_k_gain=mem_k_gain)
    small_m = dict(norm_mix_gain=m_norm_mix_gain, norm_mem_gain=m_norm_mem_gain, lb_logits_fw=m_lb_logits_fw,
                   lb_logits_bw=m_lb_logits_bw, norm_ffn_gain=m_norm_ffn_gain, hg_norm_gain=m_hg_norm_gain,
                   da_q_gain=m_da_q_gain, da_k_gain=m_da_k_gain, mem_q_gain=m_mem_q_gain, mem_k_gain=m_mem_k_gain)
    small_v = dict(norm_mix_gain=v_norm_mix_gain, norm_mem_gain=v_norm_mem_gain, lb_logits_fw=v_lb_logits_fw,
                   lb_logits_bw=v_lb_logits_bw, norm_ffn_gain=v_norm_ffn_gain, hg_norm_gain=v_hg_norm_gain,
                   da_q_gain=v_da_q_gain, da_k_gain=v_da_k_gain, mem_q_gain=v_mem_q_gain, mem_k_gain=v_mem_k_gain)
    big_names = ("w_in", "w_mem_kv", "w_proj_hg", "w_proj_da", "w_proj_mem", "w_out", "w_ffn_in", "w_ffn_out")
    big_w = dict(w_in=w_in[0], w_mem_kv=w_mem_kv[0], w_proj_hg=w_proj_hg[0], w_proj_da=w_proj_da[0],
                 w_proj_mem=w_proj_mem[0], w_out=w_out[0], w_ffn_in=w_ffn_in[0], w_ffn_out=w_ffn_out[0])
    big_m = dict(w_in=m_w_in[0], w_mem_kv=m_w_mem_kv[0], w_proj_hg=m_w_proj_hg[0], w_proj_da=m_w_proj_da[0],
                 w_proj_mem=m_w_proj_mem[0], w_out=m_w_out[0], w_ffn_in=m_w_ffn_in[0], w_ffn_out=m_w_ffn_out[0])
    big_v = dict(w_in=v_w_in[0], w_mem_kv=v_w_mem_kv[0], w_proj_hg=v_w_proj_hg[0], w_proj_da=v_w_proj_da[0],
                 w_proj_mem=v_w_proj_mem[0], w_out=v_w_out[0], w_ffn_in=v_w_ffn_in[0], w_ffn_out=v_w_ffn_out[0])

    row_tile = dict(w_in=128, w_mem_kv=128, w_proj_hg=128, w_proj_da=512, w_proj_mem=512, w_out=128,
                    w_ffn_in=128, w_ffn_out=352)
    rest = _BIG_NAMES[1:]
    shards = [big_w[n].astype(BF16) for n in rest]
    state = {}
    big_out = {}

    def reduce_start(group, stacked):
        names = tuple(stacked)
        arrays = [stacked[n] for n in names] + [lax.empty((4,) + stacked[n].shape[1:], BF16) for n in names]
        plan = _reduce_plan_sibling(len(names))
        send, recv, thru, token = _copies_start(f"grads_{group}_sibling_start", arrays, 4 * len(names), plan)
        state[group] = dict(names=names, plan=plan, send=send, recv=recv, arrays=thru)
        return token

    def reduce_middle(group, after):
        st = state[group]
        names, k = st["names"], len(st["names"])
        thru = _copies_wait(f"grads_{group}_sibling_wait", st["send"], st["recv"], st["arrays"], 4 * k, st["plan"], after)
        partials = [_chip_partials(thru[i], thru[k + i], f"chip_partials_{n}", row_tile[n]) for i, n in enumerate(names)]
        arrays = [p[1] for p in partials] + [lax.empty(p[1].shape, BF16) for p in partials]
        plan = _reduce_plan_chips(k)
        send, recv, thru2, token = _copies_start(f"grads_{group}_chips_start", arrays, 3 * k, plan)
        state[group] = dict(names=names, plan=plan, send=send, recv=recv, arrays=thru2, own=[p[0] for p in partials])
        return token

    def reduce_finish(group, after):
        st = state.pop(group)
        names, k = st["names"], len(st["names"])
        thru = _copies_wait(f"grads_{group}_chips_wait", st["send"], st["recv"], st["arrays"], 3 * k, st["plan"], after)
        for i, n in enumerate(names):
            big_out[n] = _adamw_shard(st["own"][i], thru[k + i], big_w[n], big_m[n], big_v[n], "adamw_" + n, row_tile[n])

    win_st = _all_gather([big_w["w_in"].astype(BF16)])[0]
    step = _local_step_stages(x[0], mem[0], loss_target[0], small_w, win_st)
    event, payload = next(step)
    local = None
    while True:
        reply = None
        if event == "begin":
            nr = len(rest)
            me = 4 * lax.axis_index("x") + 2 * lax.axis_index("y") + lax.axis_index("c")
            arrays = shards + [lax.dynamic_update_slice(lax.empty((N_DEV,) + s.shape, BF16), s[None], (me, 0, 0))
                               for s in shards]
            plan = _gather_plan_out(nr)
            send, recv, thru, reply = _copies_start("weights_rest_out_start", arrays, 4 * nr, plan)
            state["gather"] = dict(plan=plan, send=send, recv=recv, arrays=thru)
        elif event == "after_gla_fwd":
            st = state["gather"]
            nr = len(rest)
            thru = _copies_wait("weights_rest_out_wait", st["send"], st["recv"], st["arrays"], 4 * nr, st["plan"], payload)
            plan = _gather_plan_pass(nr)
            send, recv, lands, reply = _copies_start("weights_rest_pass_start", thru[nr:], 3 * nr, plan)
            state["gather"] = dict(plan=plan, send=send, recv=recv, arrays=lands)
        elif event == "need_weights":
            st = state.pop("gather")
            nr = len(rest)
            lands = _copies_wait("weights_rest_pass_wait", st["send"], st["recv"], st["arrays"], 3 * nr, st["plan"], payload)
            reply = dict(zip(rest, lands))
        elif event == "grads_ffn":
            reply = reduce_start("ffn", payload)
        elif event == "after_gate_merge_bwd":
            reply = reduce_middle("ffn", payload)
        elif event == "grads_mix":
            reply = reduce_start("mix", payload)
        elif event == "after_da_bwd_g0":
            reply = reduce_middle("mix", payload)
            reduce_finish("ffn", payload)
        elif event == "after_gla_bwd":
            reduce_finish("mix", payload)
        elif event == "grads_in":
            reply = reduce_start("in", payload)
        elif event == "after_proj_bwd_act":
            reply = reduce_middle("in", payload)
        elif event == "end":
            local = payload
            reduce_finish("in", [local["grad_x"]] + [big_out[n][0] for n in rest])
            break
        event, payload = step.send(reply)
    grad_x = local["grad_x"]

    wpack = _pack_rows([small_w[n] for n in _SMALL_NAMES], SMALL_ROWS)
    mpack = _pack_rows([small_m[n] for n in _SMALL_NAMES], SMALL_ROWS)
    vpack = _pack_rows([small_v[n] for n in _SMALL_NAMES], SMALL_ROWS)
    gs, ds, ms, vs = _small_all_reduce_adamw(local["gpart"], local["lbpack"], wpack, mpack, vpack, big_out["w_in"][0])
    loss = gs[_LOSS_ROW, 0]
    small_out = [_unpack_small(t, small_w) for t in (gs, ds, ms, vs)]

    order = ("norm_mix_gain", "norm_mem_gain", "w_in", "lb_logits_fw", "lb_logits_bw", "hg_norm_gain", "da_q_gain",
             "da_k_gain", "w_mem_kv", "mem_q_gain", "mem_k_gain", "w_proj_hg", "w_proj_da", "w_proj_mem", "w_out",
             "norm_ffn_gain", "w_ffn_in", "w_ffn_out")
    outs = [loss, grad_x[None]]
    for kind in range(4):
        for n in order:
            outs.append(big_out[n][kind][None] if n in big_out else small_out[kind][n])
    return tuple(outs)


_BIG_NAMES = ("w_in", "w_mem_kv", "w_proj_hg", "w_proj_da", "w_proj_mem", "w_out", "w_ffn_in", "w_ffn_out")


def _local_step(xs, mems, target, sw, wg):
    step = _local_step_stages(xs, mems, target, sw, wg["w_in"])
    stacked = {}
    event, payload = next(step)
    while event != "end":
        if event.startswith("grads_"):
            stacked.update(payload)
        event, payload = step.send(wg if event == "need_weights" else None)
    return dict(payload, stacked=stacked)


def _local_step_stages(xs, mems, target, sw, win_st):
    norm_mix_gain, norm_mem_gain, norm_ffn_gain = sw["norm_mix_gain"], sw["norm_mem_gain"], sw["norm_ffn_gain"]
    lb_logits_fw, lb_logits_bw, hg_norm_gain = sw["lb_logits_fw"], sw["lb_logits_bw"], sw["hg_norm_gain"]
    da_q_gain, da_k_gain, mem_q_gain, mem_k_gain = sw["da_q_gain"], sw["da_k_gain"], sw["mem_q_gain"], sw["mem_k_gain"]

    token = yield "begin", None
    lb_fw = _lb_table(lb_logits_fw, "lb_table_fw")
    lb_bw = _lb_table(lb_logits_bw, "lb_table_bw")
    lb = jnp.concatenate([lb_fw, lb_bw], axis=0).reshape(2, HG_HEADS, 1, HEAD_DIM)
    h = _rmsnorm_fwd(xs, _after(norm_mix_gain, token), "norm_mix_fwd", 512)
    proj = _matmul(h, win_st, "nn", F32, 512, IN_SHARD, D_MODEL, "proj_fwd", b_stacked=True, n_outer=True)
    o_hg, o_pre, states = _gla_fwd(proj, lb, hg_norm_gain)
    token = yield "after_gla_fwd", o_hg
    da = [_da_fwd(proj, _after(da_q_gain, token), da_k_gain, g) for g in range(3)]
    o_da, o_da32, lse_da = _da_merge([t[0] for t in da], [t[1] for t in da], 512)
    wg = yield "need_weights", o_da
    wkv = wg["w_mem_kv"].reshape(D_MODEL, 2 * MEM_WIDTH)
    wphg = wg["w_proj_hg"].reshape(D_MODEL, D_MODEL)
    wpda = jnp.transpose(wg["w_proj_da"], (1, 0, 2)).reshape(DA_WIDTH, D_MODEL)
    wpmem = jnp.transpose(wg["w_proj_mem"], (1, 0, 2)).reshape(MEM_WIDTH, D_MODEL)
    wout = wg["w_out"].reshape(D_MODEL, D_MODEL)
    wfin = jnp.transpose(wg["w_ffn_in"], (1, 0, 2)).reshape(D_MODEL, 2 * D_FF)
    wfout = wg["w_ffn_out"].reshape(D_FF, D_MODEL)
    mem_n = _rmsnorm_fwd(mems, norm_mem_gain, "norm_mem_fwd", N_MEM)
    kv = _matmul(mem_n, wkv, "nn", F32, N_MEM, 1024, D_MODEL, "mem_kv_fwd")
    o_mem = _mem_fwd(proj, kv, mem_q_gain, mem_k_gain, 512)
    t_hg = _matmul(o_hg, wphg, "nn", F32, 512, 1024, D_MODEL, "proj_hg_fwd")
    t_da = _matmul(o_da, wpda, "nn", F32, 512, 1024, DA_WIDTH, "proj_da_fwd")
    t_mem = _matmul(o_mem, wpmem, "nn", F32, 512, 1024, MEM_WIDTH, "proj_mem_fwd")
    merged = _gate_merge_fwd(proj, t_hg, t_da, t_mem, 256)
    u = _matmul(merged, wout, "nn", F32, 512, 1024, D_MODEL, "out_fwd")
    x1, h2 = _residual_rmsnorm_fwd(xs, u, norm_ffn_gain, "norm_ffn_fwd", 512)
    ab = _matmul(h2, wfin, "nn", F32, 512, 1408, D_MODEL, "ffn_in_fwd")
    act = _swiglu_fwd(ab, 256)
    z = _matmul(act, wfout, "nn", F32, 512, 1024, D_FF, "ffn_out_fwd")
    dy, dyb, loss_part = _loss_head(x1, z, target, 512)

    dact = _matmul(dyb, wfout, "nt", F32, 512, 1408, D_MODEL, "ffn_out_bwd_act")
    g_wfout = _matmul(act, dyb, "tn", BF16, 1408, 1024, 512, "ffn_out_bwd_w")
    dab = _swiglu_bwd(ab, dact, 256)
    dh2 = _matmul(dab, wfin, "nt", F32, 512, 1024, 1408, "ffn_in_bwd_act")
    g_wfin = _matmul(h2, dab, "tn", BF16, 1024, 1408, 512, "ffn_in_bwd_w")
    token = yield "grads_ffn", dict(
        w_ffn_in=jnp.transpose(g_wfin.reshape(D_MODEL, N_DEV, 2 * D_FF // N_DEV), (1, 0, 2)),
        w_ffn_out=g_wfout.reshape(N_DEV, D_FF // N_DEV, D_MODEL))
    dx1, dx1b, g_norm_ffn = _rmsnorm_bwd(x1, dh2, dy, _after(norm_ffn_gain, token), "norm_ffn_bwd", 256)
    dmerged = _matmul(dx1b, wout, "nt", F32, 512, 1024, D_MODEL, "out_bwd_act")
    g_wout = _matmul(merged, dx1b, "tn", BF16, 1024, 1024, 512, "out_bwd_w")
    dt_hg, dt_da, dt_mem, dgates = _gate_merge_bwd(proj, t_hg, t_da, t_mem, dmerged, 256)
    token = yield "after_gate_merge_bwd", dt_hg
    do_hg = _matmul(dt_hg, wphg, "nt", F32, 512, 1024, D_MODEL, "proj_hg_bwd_act", after=token)
    g_wphg = _matmul(o_hg, dt_hg, "tn", BF16, 1024, 1024, 512, "proj_hg_bwd_w")
    do_da = _matmul(dt_da, wpda, "nt", F32, 512, DA_WIDTH, D_MODEL, "proj_da_bwd_act")
    g_wpda = _matmul(o_da, dt_da, "tn", BF16, DA_WIDTH, LANE, 512, "proj_da_bwd_w", out_stacked=True)
    do_mem = _matmul(dt_mem, wpmem, "nt", F32, 512, MEM_WIDTH, D_MODEL, "proj_mem_bwd_act")
    g_wpmem = _matmul(o_mem, dt_mem, "tn", BF16, MEM_WIDTH, LANE, 512, "proj_mem_bwd_w", out_stacked=True)

    dmemq, dk_mem, dv_mem, g_mem_q, g_mem_k = _mem_bwd(proj, kv, mem_q_gain, mem_k_gain, do_mem, 512)
    dkv = jnp.concatenate([dk_mem, dv_mem], axis=1).astype(BF16)
    g_wkv = _matmul(mem_n, dkv, "tn", BF16, 1024, 1024, N_MEM, "mem_kv_bwd_w")
    dmem_n = _matmul(dkv, wkv, "nt", F32, N_MEM, 1024, 1024, "mem_kv_bwd_act")
    g_norm_mem = _gain_grad(mems, dmem_n, "norm_mem_bwd")
    token = yield "grads_mix", dict(
        w_mem_kv=g_wkv.reshape(N_DEV, D_MODEL // N_DEV, 2 * MEM_WIDTH),
        w_proj_hg=g_wphg.reshape(N_DEV, D_MODEL // N_DEV, D_MODEL),
        w_proj_da=g_wpda, w_proj_mem=g_wpmem,
        w_out=g_wout.reshape(N_DEV, D_MODEL // N_DEV, D_MODEL))

    dd_da = _da_rowdot(do_da, o_da32, 512)
    da_b = [_da_bwd(proj, _after(da_q_gain, token), da_k_gain, do_da, lse_da, dd_da, 0)]
    token = yield "after_da_bwd_g0", da_b[0][0]
    da_b += [_da_bwd(proj, _after(da_q_gain, token), da_k_gain, do_da, lse_da, dd_da, g) for g in (1, 2)]
    g_da_q = da_b[0][3] + da_b[1][3] + da_b[2][3]
    g_da_k = da_b[0][4] + da_b[1][4] + da_b[2][4]

    dq_hg, df_hg, dv_hg, dg_hg, dlb, g_hg_norm = _gla_bwd(proj, lb, hg_norm_gain, o_pre, states, do_hg)
    yield "after_gla_bwd", dq_hg

    dproj = jnp.concatenate(
        [dq_hg, df_hg[0], df_hg[1], dv_hg, dg_hg]
        + [t[0] for t in da_b] + [t[1] for t in da_b] + [t[2] for t in da_b] + [dmemq, dgates], axis=1)
    g_win = _matmul(h, dproj, "tn", BF16, 1024, IN_SHARD, 512, "proj_bwd_w", out_stacked=True)
    token = yield "grads_in", dict(w_in=g_win)
    dh = _matmul(dproj, win_st, "nt", F32, 512, 1024, IN_SHARD, "proj_bwd_act", b_stacked=True, after=token)
    token = yield "after_proj_bwd_act", dh
    grad_x, _, g_norm_mix = _rmsnorm_bwd(xs, dh, dx1, _after(norm_mix_gain, token), "norm_mix_bwd", 256)

    gpart = _pack_rows([g_norm_mix, g_norm_mem, dlb[0], dlb[1], g_norm_ffn, g_hg_norm, g_da_q, g_da_k,
                        g_mem_q, g_mem_k, loss_part], SMALL_GRAD_ROWS)
    lbpack = jnp.concatenate([lb_fw.reshape(8, LANE), lb_bw.reshape(8, LANE)], axis=0)
    yield "end", dict(grad_x=grad_x, gpart=gpart, lbpack=lbpack)
```

```python
import numpy as np
import jax
import jax.numpy as jnp
from jax import lax
from jax.experimental import pallas as pl
from jax.experimental.pallas import tpu as pltpu

F32 = jnp.float32
BF16 = jnp.bfloat16
MESH = pl.DeviceIdType.MESH

SEQ = 4096
D_MODEL = 1024
N_DEV = 8
N_MEM = 256
RMS_EPS = 1e-6
NEG_INF = -1e30
LANE = 128
HEAD_DIM = 128
HG_HEADS = 8
HG_CHUNK = 64
HG_SCALE = HEAD_DIM ** -0.5
DA_DILATIONS = (1, 4, 16)
DA_RADIUS = 64
DA_HEADS_PER_GROUP = 4
DA_HEADS = 12
DA_WIDTH = 512
DA_SCALE = HEAD_DIM ** -0.5
DA_QB = 128
DA_WIN = 256
DA_WAYS = 2
MEM_HEADS = 4
MEM_WIDTH = 512
MEM_SCALE = HEAD_DIM ** -0.5
D_FF = 2816
IN_COLS = 13312
IN_SHARD = IN_COLS // N_DEV
CB_HG_Q, CB_F, CB_HG_I, CB_HG_G = 0, 8, 24, 32
CB_DA_Q, CB_DA_K, CB_DA_V, CB_MEM_Q = 40, 52, 64, 76
N_CB = IN_COLS // LANE
ADAM_LR, ADAM_B1, ADAM_B2, ADAM_EPS, ADAM_WD, ADAM_STEP = 0.001, 0.9, 0.999, 1e-08, 0.01, 10
VMEM_BYTES_V7X = 64 * 1024 * 1024
SMALL_ROWS = 104
SMALL_GRAD_ROWS = 88

_NN = (((1,), (0,)), ((), ()))
_NT = (((1,), (1,)), ((), ()))
_TN = (((0,), (0,)), ((), ()))


def _dot(a, b, dims):
    return lax.dot_general(a.astype(BF16), b.astype(BF16), dims, preferred_element_type=F32)


def _dot_exact(a, b, dims):
    return lax.dot_general(a, b, dims, precision=lax.Precision.HIGHEST, preferred_element_type=F32)


def _sigmoid(x):
    return 0.5 * jnp.tanh(0.5 * x) + 0.5


def _params(semantics, est_bytes):
    limit = int(min(VMEM_BYTES_V7X - (6 << 20), max(32 << 20, est_bytes * 3 // 2)))
    return pltpu.CompilerParams(dimension_semantics=semantics, vmem_limit_bytes=limit)


def _nbytes(shape, dtype):
    return int(np.prod(shape)) * jnp.dtype(dtype).itemsize


def _alibi_slopes(n):
    return (2.0 ** (-8.0 * np.arange(1, n + 1) / n)).astype(np.float32)


def _matmul(a, b, mode, out_dtype, tm, tn, tk, name, b_stacked=False, out_stacked=False, n_outer=False, after=None):
    if mode == "tn":
        kdim, m = a.shape
    else:
        m, kdim = a.shape
    if b_stacked:
        if mode == "nn":
            n = b.shape[0] * b.shape[2]
            assert tn == b.shape[2] and tk == kdim == b.shape[1]
        else:
            assert mode == "nt" and tk == b.shape[2] and b.shape[0] * tk == kdim
            n = b.shape[1]
    else:
        n = b.shape[0] if mode == "nt" else b.shape[1]
    assert m % tm == 0 and n % tn == 0 and kdim % tk == 0
    gm, gn, gk = m // tm, n // tn, kdim // tk

    def ijk(f):
        if n_outer:
            return lambda j, i, k: f(i, j, k)
        return f

    if mode == "tn":
        a_spec = pl.BlockSpec((tk, tm), ijk(lambda i, j, k: (k, i)))
    else:
        a_spec = pl.BlockSpec((tm, tk), ijk(lambda i, j, k: (i, k)))
    if b_stacked and mode == "nn":
        b_spec = pl.BlockSpec((None, tk, tn), ijk(lambda i, j, k: (j, 0, 0)))
    elif b_stacked:
        b_spec = pl.BlockSpec((None, tn, tk), ijk(lambda i, j, k: (k, j, 0)))
    elif mode == "nt":
        b_spec = pl.BlockSpec((tn, tk), ijk(lambda i, j, k: (j, k)))
    else:
        b_spec = pl.BlockSpec((tk, tn), ijk(lambda i, j, k: (k, j)))
    if out_stacked:
        assert tm == m
        out_shape = jax.ShapeDtypeStruct((gn, m, tn), out_dtype)
        o_spec = pl.BlockSpec((None, tm, tn), ijk(lambda i, j, k: (j, i, 0)))
    else:
        out_shape = jax.ShapeDtypeStruct((m, n), out_dtype)
        o_spec = pl.BlockSpec((tm, tn), ijk(lambda i, j, k: (i, j)))
    dims = {"nn": _NN, "nt": _NT, "tn": _TN}[mode]

    def body(*refs):
        a_ref, b_ref = refs[0], refs[1]
        o_ref = refs[2 if after is None else 3]
        part = _dot(a_ref[...], b_ref[...], dims)
        if gk == 1:
            o_ref[...] = part.astype(out_dtype)
            return
        acc_ref = refs[-1]
        k = pl.program_id(2)

        @pl.when(k == 0)
        def _():
            acc_ref[...] = part

        @pl.when(k > 0)
        def _():
            acc_ref[...] += part

        @pl.when(k == gk - 1)
        def _():
            o_ref[...] = acc_ref[...].astype(out_dtype)

    a_tile = _nbytes((tm, tk), a.dtype)
    b_tile = _nbytes((tk, tn), b.dtype)
    o_tile = _nbytes((tm, tn), out_dtype)
    est = 2 * (a_tile + b_tile + o_tile) + 3 * tm * tn * 4 + (a_tile + b_tile)
    grid = (gn, gm, gk) if n_outer else (gm, gn, gk)
    operands, in_specs = [a, b], [a_spec, b_spec]
    if after is not None:
        operands.append(after)
        in_specs.append(pl.BlockSpec(memory_space=pl.ANY))
    return pl.pallas_call(
        body, name=name, grid=grid, in_specs=in_specs, out_specs=o_spec, out_shape=out_shape,
        scratch_shapes=[] if gk == 1 else [pltpu.VMEM((tm, tn), F32)],
        compiler_params=_params(("parallel", "parallel", "arbitrary"), est),
    )(*operands)


def _row_spec(tr, width, col_block=0):
    return pl.BlockSpec((tr, width), lambda i: (i, col_block))


def _bcast_spec(width):
    return pl.BlockSpec((1, width), lambda i: (0, 0))


def _rmsnorm_fwd(x, gain, name, tr):
    rows, width = x.shape

    def body(x_ref, g_ref, o_ref):
        xv = x_ref[...]
        r = lax.rsqrt(jnp.mean(xv * xv, axis=-1, keepdims=True) + RMS_EPS)
        o_ref[...] = (xv * r * g_ref[...]).astype(BF16)

    return pl.pallas_call(
        body, name=name, grid=(rows // tr,), in_specs=[_row_spec(tr, width), _bcast_spec(width)],
        out_specs=_row_spec(tr, width), out_shape=jax.ShapeDtypeStruct((rows, width), BF16),
        compiler_params=_params(("parallel",), 6 * tr * width * 4),
    )(x, gain)


def _residual_rmsnorm_fwd(x, u, gain, name, tr):
    rows, width = x.shape

    def body(x_ref, u_ref, g_ref, x1_ref, h_ref):
        xv = x_ref[...] + u_ref[...]
        x1_ref[...] = xv
        r = lax.rsqrt(jnp.mean(xv * xv, axis=-1, keepdims=True) + RMS_EPS)
        h_ref[...] = (xv * r * g_ref[...]).astype(BF16)

    return pl.pallas_call(
        body, name=name, grid=(rows // tr,),
        in_specs=[_row_spec(tr, width), _row_spec(tr, width), _bcast_spec(width)],
        out_specs=[_row_spec(tr, width), _row_spec(tr, width)],
        out_shape=[jax.ShapeDtypeStruct((rows, width), F32), jax.ShapeDtypeStruct((rows, width), BF16)],
        compiler_params=_params(("parallel",), 10 * tr * width * 4),
    )(x, u, gain)


def _rmsnorm_bwd(x, dh, dres, gain, name, tr):
    rows, width = x.shape

    def body(x_ref, dh_ref, dres_ref, g_ref, dx_ref, dxb_ref, dg_ref):
        xv = x_ref[...]
        r = lax.rsqrt(jnp.mean(xv * xv, axis=-1, keepdims=True) + RMS_EPS)
        xhat = xv * r
        dhv = dh_ref[...]
        dyg = dhv * g_ref[...]
        dx = dres_ref[...] + r * (dyg - xhat * jnp.mean(dyg * xhat, axis=-1, keepdims=True))
        dx_ref[...] = dx
        dxb_ref[...] = dx.astype(BF16)
        part = jnp.sum(dhv * xhat, axis=0, keepdims=True)

        @pl.when(pl.program_id(0) == 0)
        def _():
            dg_ref[...] = part

        @pl.when(pl.program_id(0) > 0)
        def _():
            dg_ref[...] += part

    return pl.pallas_call(
        body, name=name, grid=(rows // tr,),
        in_specs=[_row_spec(tr, width), _row_spec(tr, width), _row_spec(tr, width), _bcast_spec(width)],
        out_specs=[_row_spec(tr, width), _row_spec(tr, width), _bcast_spec(width)],
        out_shape=[jax.ShapeDtypeStruct((rows, width), F32), jax.ShapeDtypeStruct((rows, width), BF16),
                   jax.ShapeDtypeStruct((1, width), F32)],
        compiler_params=_params(("arbitrary",), 14 * tr * width * 4),
    )(x, dh, dres, gain)


def _gain_grad(x, dh, name):
    rows, width = x.shape

    def body(x_ref, dh_ref, dg_ref):
        xv = x_ref[...]
        r = lax.rsqrt(jnp.mean(xv * xv, axis=-1, keepdims=True) + RMS_EPS)
        dg_ref[...] = jnp.sum(dh_ref[...] * xv * r, axis=0, keepdims=True)

    return pl.pallas_call(
        body, name=name, grid=(1,), in_specs=[_row_spec(rows, width), _row_spec(rows, width)],
        out_specs=_bcast_spec(width), out_shape=jax.ShapeDtypeStruct((1, width), F32),
        compiler_params=_params(("arbitrary",), 6 * rows * width * 4),
    )(x, dh)


def _lb_table(logits, name):
    slots, width = logits.shape

    def body(l_ref, o_ref):
        lv = l_ref[...]
        mx = jnp.max(lv, axis=0, keepdims=True)
        e = jnp.exp(lv - mx)
        o_ref[...] = e[0:1, :] / jnp.sum(e, axis=0, keepdims=True)

    return pl.pallas_call(
        body, name=name, grid=(1,), in_specs=[pl.BlockSpec((slots, width), lambda i: (0, 0))],
        out_specs=_bcast_spec(width), out_shape=jax.ShapeDtypeStruct((1, width), F32),
    )(logits)


def _gate_merge_fwd(proj, t_hg, t_da, t_mem, tr):
    w = D_MODEL

    def body(ghg_ref, gda_ref, gmem_ref, thg_ref, tda_ref, tmem_ref, o_ref):
        acc = _sigmoid(ghg_ref[...]) * thg_ref[...]
        acc += _sigmoid(gda_ref[...]) * tda_ref[...]
        acc += _sigmoid(gmem_ref[...]) * tmem_ref[...]
        o_ref[...] = acc.astype(BF16)

    return pl.pallas_call(
        body, name="gate_merge_fwd", grid=(SEQ // tr,),
        in_specs=[_row_spec(tr, w, 10), _row_spec(tr, w, 11), _row_spec(tr, w, 12),
                  _row_spec(tr, w), _row_spec(tr, w), _row_spec(tr, w)],
        out_specs=_row_spec(tr, w), out_shape=jax.ShapeDtypeStruct((SEQ, w), BF16),
        compiler_params=_params(("parallel",), 16 * tr * w * 4),
    )(proj, proj, proj, t_hg, t_da, t_mem)


def _gate_merge_bwd(proj, t_hg, t_da, t_mem, dmerged, tr):
    w = D_MODEL

    def body(ghg_ref, gda_ref, gmem_ref, thg_ref, tda_ref, tmem_ref, dm_ref, dthg_ref, dtda_ref, dtmem_ref, dg_ref):
        dm = dm_ref[...]
        for b, (g_ref, t_ref, dt_ref) in enumerate(((ghg_ref, thg_ref, dthg_ref), (gda_ref, tda_ref, dtda_ref),
                                                    (gmem_ref, tmem_ref, dtmem_ref))):
            s = _sigmoid(g_ref[...])
            dt_ref[...] = (s * dm).astype(BF16)
            dg_ref[:, b * w:(b + 1) * w] = (dm * t_ref[...] * s * (1.0 - s)).astype(BF16)

    return pl.pallas_call(
        body, name="gate_merge_bwd", grid=(SEQ // tr,),
        in_specs=[_row_spec(tr, w, 10), _row_spec(tr, w, 11), _row_spec(tr, w, 12),
                  _row_spec(tr, w), _row_spec(tr, w), _row_spec(tr, w), _row_spec(tr, w)],
        out_specs=[_row_spec(tr, w), _row_spec(tr, w), _row_spec(tr, w), _row_spec(tr, 3 * w)],
        out_shape=[jax.ShapeDtypeStruct((SEQ, w), BF16)] * 3 + [jax.ShapeDtypeStruct((SEQ, 3 * w), BF16)],
        compiler_params=_params(("parallel",), 24 * tr * w * 4),
    )(proj, proj, proj, t_hg, t_da, t_mem, dmerged)


def _swiglu_fwd(ab, tr):
    def body(ab_ref, o_ref):
        a = ab_ref[:, :D_FF]
        b = ab_ref[:, D_FF:]
        o_ref[...] = (a * _sigmoid(a) * b).astype(BF16)

    return pl.pallas_call(
        body, name="swiglu_fwd", grid=(SEQ // tr,), in_specs=[_row_spec(tr, 2 * D_FF)],
        out_specs=_row_spec(tr, D_FF), out_shape=jax.ShapeDtypeStruct((SEQ, D_FF), BF16),
        compiler_params=_params(("parallel",), 8 * tr * 2 * D_FF * 4),
    )(ab)


def _swiglu_bwd(ab, dact, tr):
    def body(ab_ref, d_ref, o_ref):
        a = ab_ref[:, :D_FF]
        b = ab_ref[:, D_FF:]
        d = d_ref[...]
        s = _sigmoid(a)
        silu = a * s
        o_ref[:, :D_FF] = (d * b * (s + silu * (1.0 - s))).astype(BF16)
        o_ref[:, D_FF:] = (d * silu).astype(BF16)

    return pl.pallas_call(
        body, name="swiglu_bwd", grid=(SEQ // tr,), in_specs=[_row_spec(tr, 2 * D_FF), _row_spec(tr, D_FF)],
        out_specs=_row_spec(tr, 2 * D_FF), out_shape=jax.ShapeDtypeStruct((SEQ, 2 * D_FF), BF16),
        compiler_params=_params(("parallel",), 10 * tr * 2 * D_FF * 4),
    )(ab, dact)


def _loss_head(x1, z, target, tr):
    w = D_MODEL

    def body(x_ref, z_ref, t_ref, dy_ref, dyb_ref, loss_ref, acc_ref):
        err = x_ref[...] + z_ref[...] - t_ref[...]
        dy = err * (1.0 / w)
        dy_ref[...] = dy
        dyb_ref[...] = dy.astype(BF16)
        part = jnp.sum(err * err, axis=0, keepdims=True)

        @pl.when(pl.program_id(0) == 0)
        def _():
            acc_ref[...] = part

        @pl.when(pl.program_id(0) > 0)
        def _():
            acc_ref[...] += part

        @pl.when(pl.program_id(0) == SEQ // tr - 1)
        def _():
            total = jnp.sum(acc_ref[...], axis=1, keepdims=True) * (0.5 / w)
            loss_ref[...] = jnp.broadcast_to(total, (1, LANE))

    return pl.pallas_call(
        body, name="loss_head", grid=(SEQ // tr,),
        in_specs=[_row_spec(tr, w), _row_spec(tr, w), _row_spec(tr, w)],
        out_specs=[_row_spec(tr, w), _row_spec(tr, w), _bcast_spec(LANE)],
        out_shape=[jax.ShapeDtypeStruct((SEQ, w), F32), jax.ShapeDtypeStruct((SEQ, w), BF16),
                   jax.ShapeDtypeStruct((1, LANE), F32)],
        scratch_shapes=[pltpu.VMEM((1, w), F32)],
        compiler_params=_params(("arbitrary",), 12 * tr * w * 4),
    )(x1, z, target)


GLA_ROWS = 256
GLA_CPB = GLA_ROWS // HG_CHUNK
GLA_NBLK = SEQ // GLA_ROWS
GLA_WAYS = 4
GLA_TRIPS = GLA_NBLK // GLA_WAYS
GLA_GRAD_WAYS = 2
GLA_GRAD_TRIPS = GLA_NBLK // GLA_GRAD_WAYS
GLA_NCK = SEQ // HG_CHUNK


def _dot_split(m, xv, dims):
    hi = xv.astype(BF16)
    r1 = xv - hi.astype(F32)
    mid = r1.astype(BF16)
    lo = (r1 - mid.astype(F32)).astype(BF16)
    dot = lambda t: lax.dot_general(m, t, dims, preferred_element_type=F32)
    return (dot(lo) + dot(mid)) + dot(hi)


def _gla_block(qc, fc, lbv, masks):
    mask, maskb, direction = masks
    sq = _sigmoid(qc)
    q = qc * sq * HG_SCALE
    sf = _sigmoid(fc)
    forget = lbv + (1.0 - lbv) * sf
    k = 1.0 - forget
    logf = jnp.log(forget)
    b = _dot_split(maskb, logf, _NN)
    ends = []
    for j in range(GLA_CPB):
        lo, hi = j * HG_CHUNK, (j + 1) * HG_CHUNK
        end = jnp.where(direction == 0, b[hi - 1:hi, :], b[lo:lo + 1, :])
        ends.append(jnp.broadcast_to(end, (HG_CHUNK, HEAD_DIM)))
    bt = jnp.concatenate(ends, axis=0)
    eb = jnp.exp(b)
    qt = q * eb
    kt = k * jnp.exp(-b)
    kh = k * jnp.exp(bt - b)
    a = jnp.where(mask, _dot(qt, kt, _NT), 0.0)
    return dict(sq=sq, sf=sf, forget=forget, k=k, b=b, bt=bt, eb=eb, qt=qt, kt=kt, kh=kh, a=a)


def _gla_masks(direction):
    row = lax.broadcasted_iota(jnp.int32, (GLA_ROWS, GLA_ROWS), 0)
    col = lax.broadcasted_iota(jnp.int32, (GLA_ROWS, GLA_ROWS), 1)
    same = (row // HG_CHUNK) == (col // HG_CHUNK)
    mask = jnp.logical_and(same, jnp.where(direction == 0, row - col, col - row) >= 0)
    return mask, jnp.where(mask, 1.0, 0.0).astype(BF16), direction


def _gla_chunk_rows(j):
    return slice(j * HG_CHUNK, (j + 1) * HG_CHUNK)


def _gla_block_rows(b):
    return pl.ds(pl.multiple_of(b * GLA_ROWS, GLA_ROWS), GLA_ROWS)


def _head_spec(col_block0):
    return pl.BlockSpec((SEQ, HEAD_DIM), lambda h, d: (0, col_block0 + h))


def _gla_fwd(proj, lb, gain):
    nck = GLA_NCK

    def body(q_ref, f_ref, v_ref, g_ref, lb_ref, gain_ref, ohg_ref, opre_ref, st_ref, qt_scr, cs_scr, dec_scr):
        d = pl.program_id(1)
        masks = _gla_masks(d)
        lbv = lb_ref[...]

        @pl.when(d == 0)
        def _():
            opre_ref[...] = jnp.zeros_like(opre_ref)

        def intra(s, carry):
            blocks = [s + w * GLA_TRIPS for w in range(GLA_WAYS)]
            rows = [_gla_block_rows(b) for b in blocks]
            loaded = [(q_ref[r, :], f_ref[r, :], v_ref[r, :], opre_ref[r, :]) for r in rows]
            results = []
            for qc, fc, v, o_prev in loaded:
                ck = _gla_block(qc, fc, lbv, masks)
                o = o_prev + _dot(ck["a"], v, _NN)
                cs = [_dot(v[_gla_chunk_rows(j), :], ck["kh"][_gla_chunk_rows(j), :], _TN) for j in range(GLA_CPB)]
                dec = [jnp.exp(ck["bt"][j * HG_CHUNK:j * HG_CHUNK + 8, :]) for j in range(GLA_CPB)]
                results.append((o, ck["qt"].astype(BF16), cs, dec))
            for b, r, (o, qt, cs, dec) in zip(blocks, rows, results):
                opre_ref[r, :] = o
                qt_scr[r, :] = qt
                for j in range(GLA_CPB):
                    cs_scr[b * GLA_CPB + j] = cs[j]
                    dec_scr[b * GLA_CPB + j] = dec[j]
            return carry

        lax.fori_loop(0, GLA_TRIPS, intra, 0)

        def scan(i, st):
            c = jnp.where(d == 0, i, nck - 1 - i)
            st_ref[c] = st
            return st * dec_scr[c][0:1, :] + cs_scr[c]

        lax.fori_loop(0, nck, scan, jnp.zeros((HEAD_DIM, HEAD_DIM), F32), unroll=4)

        def inter(c, carry):
            rows = pl.ds(pl.multiple_of(c * HG_CHUNK, HG_CHUNK), HG_CHUNK)
            opre_ref[rows, :] += _dot(qt_scr[rows, :], st_ref[c], _NT)
            return carry

        lax.fori_loop(0, nck, inter, 0, unroll=4)

        @pl.when(d == 1)
        def _():
            o = opre_ref[...]
            r = lax.rsqrt(jnp.mean(o * o, axis=-1, keepdims=True) + RMS_EPS)
            g = g_ref[...]
            ohg_ref[...] = (o * r * gain_ref[...] * (g * _sigmoid(g))).astype(BF16)

    blk = SEQ * HEAD_DIM * 4
    return pl.pallas_call(
        body, name="gla_fwd", grid=(HG_HEADS, 2),
        in_specs=[_head_spec(CB_HG_Q),
                  pl.BlockSpec((SEQ, HEAD_DIM), lambda h, d: (0, CB_F + 8 * d + h)),
                  _head_spec(CB_HG_I), _head_spec(CB_HG_G),
                  pl.BlockSpec((None, None, 1, HEAD_DIM), lambda h, d: (d, h, 0, 0)),
                  pl.BlockSpec((1, HEAD_DIM), lambda h, d: (0, 0))],
        out_specs=[_head_spec(0), _head_spec(0),
                   pl.BlockSpec((None, None, nck, HEAD_DIM, HEAD_DIM), lambda h, d: (h, d, 0, 0, 0))],
        out_shape=[jax.ShapeDtypeStruct((SEQ, D_MODEL), BF16), jax.ShapeDtypeStruct((SEQ, D_MODEL), F32),
                   jax.ShapeDtypeStruct((HG_HEADS, 2, nck, HEAD_DIM, HEAD_DIM), F32)],
        scratch_shapes=[pltpu.VMEM((SEQ, HEAD_DIM), BF16), pltpu.VMEM((nck, HEAD_DIM, HEAD_DIM), F32),
                        pltpu.VMEM((nck, 8, HEAD_DIM), F32)],
        compiler_params=_params(("parallel", "arbitrary"), 8 * blk + 3 * blk + 2 * blk + 2 * blk + blk // 2),
    )(proj, proj, proj, proj, lb, gain)


def _gla_bwd(proj, lb, gain, o_pre, states, do_hg):
    nck = GLA_NCK
    do, dg, dgain = _gla_bwd_norm(proj, gain, o_pre, do_hg, 256)

    def body(q_ref, f_ref, v_ref, lb_ref, do_ref, st_ref, dq_ref, df_ref, dv_ref, dlb_ref,
             dq_acc, dv_acc, dst_scr, cs_scr, dec_scr):
        d = pl.program_id(1)
        masks = _gla_masks(d)
        mask, maskb, _ = masks
        lbv = lb_ref[...]

        def intra(s, carry):
            blocks = [s + w * GLA_TRIPS for w in range(GLA_WAYS)]
            loaded = [(q_ref[r, :], f_ref[r, :], do_ref[r, :]) for r in map(_gla_block_rows, blocks)]
            results = []
            for qc, fc, doc in loaded:
                ck = _gla_block(qc, fc, lbv, masks)
                cs = [_dot(doc[_gla_chunk_rows(j), :], ck["qt"][_gla_chunk_rows(j), :], _TN) for j in range(GLA_CPB)]
                dec = [jnp.exp(ck["bt"][j * HG_CHUNK:j * HG_CHUNK + 8, :]) for j in range(GLA_CPB)]
                results.append((cs, dec))
            for b, (cs, dec) in zip(blocks, results):
                for j in range(GLA_CPB):
                    cs_scr[b * GLA_CPB + j] = cs[j]
                    dec_scr[b * GLA_CPB + j] = dec[j]
            return carry

        lax.fori_loop(0, GLA_TRIPS, intra, 0)

        def scan(i, dst):
            c = jnp.where(d == 0, nck - 1 - i, i)
            dst_scr[c] = dst
            return dst * dec_scr[c][0:1, :] + cs_scr[c]

        lax.fori_loop(0, nck, scan, jnp.zeros((HEAD_DIM, HEAD_DIM), F32), unroll=4)

        @pl.when(d == 0)
        def _():
            dq_acc[...] = jnp.zeros_like(dq_acc)
            dv_acc[...] = jnp.zeros_like(dv_acc)

        def block_grads(qc, fc, v, doc, states_in, dstates, decays):
            ck = _gla_block(qc, fc, lbv, masks)
            qt, kt, kh, a = ck["qt"], ck["kt"], ck["kh"], ck["a"]
            da = jnp.where(mask, _dot(doc, v, _NT), 0.0)
            dqt_i = _dot(da, kt, _NN)
            dkt = _dot(da, qt, _TN)
            dv_i = _dot(a, doc, _TN)
            dqt_p, dv_p, dkh_p, dbt_p = [], [], [], []
            for j in range(GLA_CPB):
                cr = _gla_chunk_rows(j)
                st_in, dst = states_in[j], dstates[j]
                dqt_p.append(dqt_i[cr, :] + _dot(doc[cr, :], st_in, _NN))
                dv_p.append(dv_i[cr, :] + _dot(kh[cr, :], dst, _NT))
                dkh_j = _dot(v[cr, :], dst, _NN)
                dkh_p.append(dkh_j)
                dbt_j = (decays[j][0:1, :] * jnp.sum(dst * st_in, axis=0, keepdims=True)
                         + jnp.sum(dkh_j * kh[cr, :], axis=0, keepdims=True))
                dbt_p.append(jnp.broadcast_to(dbt_j, (HG_CHUNK, HEAD_DIM)))
            dqt = jnp.concatenate(dqt_p, axis=0)
            dv = jnp.concatenate(dv_p, axis=0)
            dkh = jnp.concatenate(dkh_p, axis=0)
            dbt = jnp.concatenate(dbt_p, axis=0)
            db = dqt * qt - dkt * kt - dkh * kh
            dq = dqt * ck["eb"]
            dk = dkt * jnp.exp(-ck["b"]) + dkh * jnp.exp(ck["bt"] - ck["b"])
            dlogf = _dot_split(maskb, db, _TN) + dbt
            dforget = dlogf / ck["forget"] - dk
            sf, sq = ck["sf"], ck["sq"]
            df = (dforget * (1.0 - lbv) * sf * (1.0 - sf)).astype(BF16)
            dqc = dq * HG_SCALE * (sq + qc * sq * (1.0 - sq))
            return df, dqc, dv, jnp.sum(dforget * (1.0 - sf), axis=0, keepdims=True)

        def grads(s, dlb):
            blocks = [s + w * GLA_GRAD_TRIPS for w in range(GLA_GRAD_WAYS)]
            rows = [_gla_block_rows(b) for b in blocks]
            loaded = []
            for b, r in zip(blocks, rows):
                chunks = [b * GLA_CPB + j for j in range(GLA_CPB)]
                loaded.append((q_ref[r, :], f_ref[r, :], v_ref[r, :], do_ref[r, :], [st_ref[c] for c in chunks],
                               [dst_scr[c] for c in chunks], [dec_scr[c] for c in chunks], dq_acc[r, :], dv_acc[r, :]))
            results = [block_grads(*t[:7]) + (t[7], t[8]) for t in loaded]
            for r, (df, dqc, dv, dlb_part, dq_prev, dv_prev) in zip(rows, results):
                df_ref[r, :] = df
                dq_acc[r, :] = dq_prev + dqc
                dv_acc[r, :] = dv_prev + dv
                dlb = dlb + dlb_part
            return dlb

        dlb_ref[...] = lax.fori_loop(0, GLA_GRAD_TRIPS, grads, jnp.zeros((1, HEAD_DIM), F32))

        @pl.when(d == 1)
        def _():
            dq_ref[...] = dq_acc[...].astype(BF16)
            dv_ref[...] = dv_acc[...].astype(BF16)

    blk = SEQ * HEAD_DIM * 4
    bshape = jax.ShapeDtypeStruct((SEQ, D_MODEL), BF16)
    state_bytes = nck * HEAD_DIM * HEAD_DIM * 4
    dq, df, dv, dlb = pl.pallas_call(
        body, name="gla_bwd", grid=(HG_HEADS, 2),
        in_specs=[_head_spec(CB_HG_Q),
                  pl.BlockSpec((SEQ, HEAD_DIM), lambda h, d: (0, CB_F + 8 * d + h)),
                  _head_spec(CB_HG_I),
                  pl.BlockSpec((None, None, 1, HEAD_DIM), lambda h, d: (d, h, 0, 0)),
                  _head_spec(0),
                  pl.BlockSpec((None, None, nck, HEAD_DIM, HEAD_DIM), lambda h, d: (h, d, 0, 0, 0))],
        out_specs=[_head_spec(0),
                   pl.BlockSpec((None, SEQ, HEAD_DIM), lambda h, d: (d, 0, h)),
                   _head_spec(0),
                   pl.BlockSpec((None, None, 1, HEAD_DIM), lambda h, d: (d, h, 0, 0))],
        out_shape=[bshape, jax.ShapeDtypeStruct((2, SEQ, D_MODEL), BF16), bshape,
                   jax.ShapeDtypeStruct((2, HG_HEADS, 1, HEAD_DIM), F32)],
        scratch_shapes=[pltpu.VMEM((SEQ, HEAD_DIM), F32)] * 2
        + [pltpu.VMEM((nck, HEAD_DIM, HEAD_DIM), F32)] * 2 + [pltpu.VMEM((nck, 8, HEAD_DIM), F32)],
        compiler_params=_params(("parallel", "arbitrary"), 8 * blk + 4 * state_bytes + 3 * blk + 2 * blk),
    )(proj, proj, proj, lb, do, states)
    return dq, df, dv, dg, dlb, dgain


def _gla_bwd_norm(proj, gain, o_pre, do_hg, tr):
    w = D_MODEL

    def body(g_ref, gain_ref, opre_ref, dohg_ref, do_ref, dg_ref, dgain_ref):
        gainv = gain_ref[...]
        part = jnp.zeros((1, HEAD_DIM), F32)
        for h in range(HG_HEADS):
            hs = slice(h * HEAD_DIM, (h + 1) * HEAD_DIM)
            o = opre_ref[:, hs]
            r = lax.rsqrt(jnp.mean(o * o, axis=-1, keepdims=True) + RMS_EPS)
            ohat = o * r
            g = g_ref[:, hs]
            sg = _sigmoid(g)
            silu = g * sg
            dout = dohg_ref[:, hs]
            dg_ref[:, hs] = (dout * ohat * gainv * (sg + silu * (1.0 - sg))).astype(BF16)
            dy = dout * silu
            part = part + jnp.sum(dy * ohat, axis=0, keepdims=True)
            dn = dy * gainv
            do_ref[:, hs] = r * (dn - ohat * jnp.mean(dn * ohat, axis=-1, keepdims=True))

        @pl.when(pl.program_id(0) == 0)
        def _():
            dgain_ref[...] = part

        @pl.when(pl.program_id(0) > 0)
        def _():
            dgain_ref[...] += part

    return pl.pallas_call(
        body, name="gla_bwd_norm", grid=(SEQ // tr,),
        in_specs=[_row_spec(tr, w, CB_HG_G * LANE // w), _bcast_spec(HEAD_DIM), _row_spec(tr, w), _row_spec(tr, w)],
        out_specs=[_row_spec(tr, w), _row_spec(tr, w), _bcast_spec(HEAD_DIM)],
        out_shape=[jax.ShapeDtypeStruct((SEQ, w), F32), jax.ShapeDtypeStruct((SEQ, w), BF16),
                   jax.ShapeDtypeStruct((1, HEAD_DIM), F32)],
        compiler_params=_params(("arbitrary",), 16 * tr * w * 4),
    )(proj, gain, o_pre, do_hg)


def _da_residue_rows(r, n0, size, d):
    if d == 1:
        return pl.ds(pl.multiple_of(n0, 8), size)
    return pl.ds(r + n0 * d, size, stride=d)


def _da_rmsnorm(x, gain):
    r = lax.rsqrt(jnp.mean(x * x, axis=-1, keepdims=True) + RMS_EPS)
    return x * r, r, x * r * gain


def _da_scores(qn_scr, kn_scr, slope, i, ld):
    w0 = jnp.clip(i * DA_QB - DA_RADIUS, 0, ld - DA_WIN)
    w0 = pl.multiple_of(w0, DA_RADIUS)
    qrows = pl.ds(pl.multiple_of(i * DA_QB, DA_QB), DA_QB)
    win = pl.ds(w0, DA_WIN)
    qb = qn_scr[qrows, :]
    kw = kn_scr[win, :]
    s = _dot(qb, kw, _NT) * DA_SCALE
    qpos = i * DA_QB + lax.broadcasted_iota(jnp.int32, (DA_QB, DA_WIN), 0)
    kpos = w0 + lax.broadcasted_iota(jnp.int32, (DA_QB, DA_WIN), 1)
    arel = jnp.abs(kpos - qpos)
    s = s - slope * arel.astype(F32)
    s = jnp.where(arel <= DA_RADIUS, s, NEG_INF)
    return s, qb, kw, qrows, win


def _da_slopes(group):
    d = DA_DILATIONS[group]
    sl = _alibi_slopes(DA_HEADS)[4 * group:4 * group + 4] * d
    return jnp.asarray(np.broadcast_to(sl[:, None, None], (4, 1, LANE)).copy())


def _da_fwd(proj, gq, gk, group):
    d = DA_DILATIONS[group]
    ld = SEQ // d
    nqb = ld // DA_QB

    def body(q_ref, k_ref, v_ref, gq_ref, gk_ref, sl_ref, o_ref, lse_ref, qn_scr, kn_scr, v_scr):
        slope = sl_ref[:, 0:1]

        def residue(r, carry):
            sel = _da_residue_rows(r, 0, ld, d)
            qn_scr[...] = _da_rmsnorm(q_ref[sel, :], gq_ref[...])[2].astype(BF16)
            kn_scr[...] = _da_rmsnorm(k_ref[sel, :], gk_ref[...])[2].astype(BF16)
            v_scr[...] = v_ref[sel, :].astype(BF16)

            def block(i):
                s, _, _, _, win = _da_scores(qn_scr, kn_scr, slope, i, ld)
                m = jnp.max(s, axis=-1, keepdims=True)
                p = jnp.exp(s - m)
                l = jnp.sum(p, axis=-1, keepdims=True)
                return _dot(p, v_scr[win, :], _NN) / l, jnp.broadcast_to(m + jnp.log(l), (DA_QB, HEAD_DIM))

            def step(i, c2):
                blocks = [i + w * (nqb // DA_WAYS) for w in range(DA_WAYS)]
                for b, (o, lse) in zip(blocks, [block(b) for b in blocks]):
                    out = _da_residue_rows(r, b * DA_QB, DA_QB, d)
                    o_ref[out, :] = o
                    lse_ref[out, :] = lse
                return c2

            return lax.fori_loop(0, nqb // DA_WAYS, step, carry)

        lax.fori_loop(0, d, residue, 0)

    seq_spec = lambda base: pl.BlockSpec((SEQ, HEAD_DIM), lambda h: (0, base + 4 * group + h))
    out_spec = pl.BlockSpec((SEQ, HEAD_DIM), lambda h: (0, h))
    gain_spec = pl.BlockSpec((1, HEAD_DIM), lambda h: (0, 0))
    blk = SEQ * HEAD_DIM * 4
    return pl.pallas_call(
        body, name=f"da_fwd_g{group}", grid=(DA_HEADS_PER_GROUP,),
        in_specs=[seq_spec(CB_DA_Q), seq_spec(CB_DA_K), seq_spec(CB_DA_V), gain_spec, gain_spec,
                  pl.BlockSpec((None, 1, LANE), lambda h: (h, 0, 0))],
        out_specs=[out_spec, out_spec],
        out_shape=[jax.ShapeDtypeStruct((SEQ, DA_WIDTH), F32)] * 2,
        scratch_shapes=[pltpu.VMEM((ld, HEAD_DIM), BF16)] * 3,
        compiler_params=_params(("parallel",), 10 * blk + 2 * blk),
    )(proj, proj, proj, gq, gk, _da_slopes(group))


def _da_merge(os, lses, tr):
    w = DA_WIDTH

    def body(o0, o1, o2, l0, l1, l2, ob_ref, of_ref, lse_ref):
        la, lb_, lc = l0[...], l1[...], l2[...]
        m = jnp.maximum(jnp.maximum(la, lb_), lc)
        ea, eb, ec = jnp.exp(la - m), jnp.exp(lb_ - m), jnp.exp(lc - m)
        tot = ea + eb + ec
        o = (ea * o0[...] + eb * o1[...] + ec * o2[...]) / tot
        of_ref[...] = o
        ob_ref[...] = o.astype(BF16)
        lse_ref[...] = m + jnp.log(tot)

    return pl.pallas_call(
        body, name="da_merge", grid=(SEQ // tr,), in_specs=[_row_spec(tr, w)] * 6,
        out_specs=[_row_spec(tr, w)] * 3,
        out_shape=[jax.ShapeDtypeStruct((SEQ, w), BF16), jax.ShapeDtypeStruct((SEQ, w), F32),
                   jax.ShapeDtypeStruct((SEQ, w), F32)],
        compiler_params=_params(("parallel",), 24 * tr * w * 4),
    )(*os, *lses)


def _da_rowdot(do, o, tr):
    w = DA_WIDTH

    def body(do_ref, o_ref, out_ref):
        prod = do_ref[...] * o_ref[...]
        for h in range(w // HEAD_DIM):
            sl = slice(h * HEAD_DIM, (h + 1) * HEAD_DIM)
            out_ref[:, sl] = jnp.broadcast_to(jnp.sum(prod[:, sl], axis=-1, keepdims=True), (tr, HEAD_DIM))

    return pl.pallas_call(
        body, name="da_rowdot", grid=(SEQ // tr,), in_specs=[_row_spec(tr, w)] * 2,
        out_specs=_row_spec(tr, w), out_shape=jax.ShapeDtypeStruct((SEQ, w), F32),
        compiler_params=_params(("parallel",), 10 * tr * w * 4),
    )(do, o)


def _da_bwd(proj, gq, gk, do, lse, dd, group):
    d = DA_DILATIONS[group]
    ld = SEQ // d
    nqb = ld // DA_QB

    def body(q_ref, k_ref, v_ref, gq_ref, gk_ref, sl_ref, do_ref, lse_ref, dd_ref,
             dq_ref, dk_ref, dv_ref, dgq_ref, dgk_ref,
             qn_scr, kn_scr, v_scr, dqn_scr, dkn_scr, dvr_scr, dq_scr, dk_scr, dv_scr):
        gqv, gkv = gq_ref[...], gk_ref[...]
        slope = sl_ref[:, 0:1]

        def residue(r, carry):
            sel = _da_residue_rows(r, 0, ld, d)
            qn_scr[...] = _da_rmsnorm(q_ref[sel, :], gqv)[2].astype(BF16)
            kn_scr[...] = _da_rmsnorm(k_ref[sel, :], gkv)[2].astype(BF16)
            v_scr[...] = v_ref[sel, :].astype(BF16)
            dkn_scr[...] = jnp.zeros_like(dkn_scr)
            dvr_scr[...] = jnp.zeros_like(dvr_scr)

            def block(i):
                s, qb, kw, qrows, win = _da_scores(qn_scr, kn_scr, slope, i, ld)
                src = _da_residue_rows(r, i * DA_QB, DA_QB, d)
                p = jnp.exp(s - lse_ref[src, :][:, 0:1])
                dob = do_ref[src, :]
                dp = _dot(dob, v_scr[win, :], _NT)
                ds = p * (dp - dd_ref[src, :][:, 0:1]) * DA_SCALE
                return qrows, win, _dot(p, dob, _TN), _dot(ds, kw, _NN), _dot(ds, qb, _TN)

            def step(i, c2):
                blocks = [i + w * (nqb // DA_WAYS) for w in range(DA_WAYS)]
                for qrows, win, dv, dqn, dkn in [block(b) for b in blocks]:
                    dvr_scr[win, :] += dv
                    dqn_scr[qrows, :] = dqn
                    dkn_scr[win, :] += dkn
                return c2

            lax.fori_loop(0, nqb // DA_WAYS, step, 0)

            parts = []
            for x_ref, gv, dn_scr, dx_scr in ((q_ref, gqv, dqn_scr, dq_scr), (k_ref, gkv, dkn_scr, dk_scr)):
                hat, rstd, _ = _da_rmsnorm(x_ref[sel, :], gv)
                dn = dn_scr[...]
                dyg = dn * gv
                dx_scr[sel, :] = rstd * (dyg - hat * jnp.mean(dyg * hat, axis=-1, keepdims=True))
                parts.append(jnp.sum(dn * hat, axis=0, keepdims=True))
            dv_scr[sel, :] = dvr_scr[...]
            return carry[0] + parts[0], carry[1] + parts[1]

        zero = jnp.zeros((1, HEAD_DIM), F32)
        pq, pk = lax.fori_loop(0, d, residue, (zero, zero))

        @pl.when(pl.program_id(0) == 0)
        def _():
            dgq_ref[...] = pq
            dgk_ref[...] = pk

        @pl.when(pl.program_id(0) > 0)
        def _():
            dgq_ref[...] += pq
            dgk_ref[...] += pk

        dq_ref[...] = dq_scr[...].astype(BF16)
        dk_ref[...] = dk_scr[...].astype(BF16)
        dv_ref[...] = dv_scr[...].astype(BF16)

    seq_spec = lambda base: pl.BlockSpec((SEQ, HEAD_DIM), lambda h: (0, base + 4 * group + h))
    out_spec = pl.BlockSpec((SEQ, HEAD_DIM), lambda h: (0, h))
    gain_spec = pl.BlockSpec((1, HEAD_DIM), lambda h: (0, 0))
    oshape = jax.ShapeDtypeStruct((SEQ, DA_WIDTH), BF16)
    gshape = jax.ShapeDtypeStruct((1, HEAD_DIM), F32)
    blk = SEQ * HEAD_DIM * 4
    return pl.pallas_call(
        body, name=f"da_bwd_g{group}", grid=(DA_HEADS_PER_GROUP,),
        in_specs=[seq_spec(CB_DA_Q), seq_spec(CB_DA_K), seq_spec(CB_DA_V), gain_spec, gain_spec,
                  pl.BlockSpec((None, 1, LANE), lambda h: (h, 0, 0)), out_spec, out_spec, out_spec],
        out_specs=[out_spec, out_spec, out_spec, gain_spec, gain_spec],
        out_shape=[oshape, oshape, oshape, gshape, gshape],
        scratch_shapes=[pltpu.VMEM((ld, HEAD_DIM), BF16)] * 3 + [pltpu.VMEM((ld, HEAD_DIM), F32)] * 3
        + [pltpu.VMEM((SEQ, HEAD_DIM), F32)] * 3,
        compiler_params=_params(("arbitrary",), 12 * blk + 3 * blk + 3 * blk + 3 * blk),
    )(proj, proj, proj, gq, gk, _da_slopes(group), do, lse, dd)


def _mem_softmax(q, k, gq, gk):
    qhat, rq, qn = _da_rmsnorm(q, gq)
    khat, rk, kn = _da_rmsnorm(k, gk)
    s = _dot(qn, kn, _NT) * MEM_SCALE
    m = jnp.max(s, axis=-1, keepdims=True)
    e = jnp.exp(s - m)
    p = e / jnp.sum(e, axis=-1, keepdims=True)
    return p, (qhat, rq, qn), (khat, rk, kn)


def _mem_fwd(proj, kv, gq, gk, tq):
    def body(q_ref, k_ref, v_ref, gq_ref, gk_ref, o_ref):
        p, _, _ = _mem_softmax(q_ref[...], k_ref[...], gq_ref[...], gk_ref[...])
        o_ref[...] = _dot(p, v_ref[...], _NN).astype(BF16)

    gain_spec = pl.BlockSpec((1, HEAD_DIM), lambda h, i: (0, 0))
    return pl.pallas_call(
        body, name="mem_fwd", grid=(MEM_HEADS, SEQ // tq),
        in_specs=[pl.BlockSpec((tq, HEAD_DIM), lambda h, i: (i, CB_MEM_Q + h)),
                  pl.BlockSpec((N_MEM, HEAD_DIM), lambda h, i: (0, h)),
                  pl.BlockSpec((N_MEM, HEAD_DIM), lambda h, i: (0, MEM_HEADS + h)), gain_spec, gain_spec],
        out_specs=pl.BlockSpec((tq, HEAD_DIM), lambda h, i: (i, h)),
        out_shape=jax.ShapeDtypeStruct((SEQ, MEM_WIDTH), BF16),
        compiler_params=_params(("parallel", "parallel"), 16 * tq * N_MEM * 4),
    )(proj, kv, kv, gq, gk)


def _mem_bwd(proj, kv, gq, gk, do, tq):
    nq = SEQ // tq

    def body(q_ref, k_ref, v_ref, gq_ref, gk_ref, do_ref, dq_ref, dk_ref, dv_ref, dgq_ref, dgk_ref, dkn_scr):
        h, i = pl.program_id(0), pl.program_id(1)
        gqv, gkv = gq_ref[...], gk_ref[...]
        p, (qhat, rq, qn), (khat, rk, kn) = _mem_softmax(q_ref[...], k_ref[...], gqv, gkv)
        dob = do_ref[...]
        dvp = _dot(p, dob, _TN)
        dp = _dot(dob, v_ref[...], _NT)
        ds = p * (dp - jnp.sum(p * dp, axis=-1, keepdims=True)) * MEM_SCALE
        dqn = _dot(ds, kn, _NN)
        dknp = _dot(ds, qn, _TN)
        dyg = dqn * gqv
        dq_ref[...] = (rq * (dyg - qhat * jnp.mean(dyg * qhat, axis=-1, keepdims=True))).astype(BF16)
        dgq_part = jnp.sum(dqn * qhat, axis=0, keepdims=True)
        first = jnp.logical_and(h == 0, i == 0)

        @pl.when(first)
        def _():
            dgq_ref[...] = dgq_part

        @pl.when(jnp.logical_not(first))
        def _():
            dgq_ref[...] += dgq_part

        @pl.when(i == 0)
        def _():
            dv_ref[...] = dvp
            dkn_scr[...] = dknp

        @pl.when(i > 0)
        def _():
            dv_ref[...] += dvp
            dkn_scr[...] += dknp

        @pl.when(i == nq - 1)
        def _():
            dkn = dkn_scr[...]
            dkg = dkn * gkv
            dk_ref[...] = rk * (dkg - khat * jnp.mean(dkg * khat, axis=-1, keepdims=True))
            dgk_part = jnp.sum(dkn * khat, axis=0, keepdims=True)

            @pl.when(h == 0)
            def _():
                dgk_ref[...] = dgk_part

            @pl.when(h > 0)
            def _():
                dgk_ref[...] += dgk_part

    gain_spec = pl.BlockSpec((1, HEAD_DIM), lambda h, i: (0, 0))
    kvout = pl.BlockSpec((N_MEM, HEAD_DIM), lambda h, i: (0, h))
    return pl.pallas_call(
        body, name="mem_bwd", grid=(MEM_HEADS, nq),
        in_specs=[pl.BlockSpec((tq, HEAD_DIM), lambda h, i: (i, CB_MEM_Q + h)),
                  pl.BlockSpec((N_MEM, HEAD_DIM), lambda h, i: (0, h)),
                  pl.BlockSpec((N_MEM, HEAD_DIM), lambda h, i: (0, MEM_HEADS + h)), gain_spec, gain_spec,
                  pl.BlockSpec((tq, HEAD_DIM), lambda h, i: (i, h))],
        out_specs=[pl.BlockSpec((tq, HEAD_DIM), lambda h, i: (i, h)), kvout, kvout, gain_spec, gain_spec],
        out_shape=[jax.ShapeDtypeStruct((SEQ, MEM_WIDTH), BF16), jax.ShapeDtypeStruct((N_MEM, MEM_WIDTH), F32),
                   jax.ShapeDtypeStruct((N_MEM, MEM_WIDTH), F32), jax.ShapeDtypeStruct((1, HEAD_DIM), F32),
                   jax.ShapeDtypeStruct((1, HEAD_DIM), F32)],
        scratch_shapes=[pltpu.VMEM((N_MEM, HEAD_DIM), F32)],
        compiler_params=_params(("arbitrary", "arbitrary"), 24 * tq * N_MEM * 4),
    )(proj, kv, kv, gq, gk, do)


def _mesh_position():
    return lax.axis_index("x"), lax.axis_index("y"), lax.axis_index("c")


def _any_spec():
    return pl.BlockSpec(memory_space=pl.ANY)


def _all_gather(shards):
    n = len(shards)

    def body(*refs):
        ins, outs = refs[:n], refs[n:2 * n]
        send_sems, recv_sems, local_sems = refs[2 * n:]
        x, y, c = _mesh_position()
        me, sibling = (x, y, c), (x, y, 1 - c)
        chips = [(1 - x, y), (x, 1 - y), (1 - x, 1 - y)]

        def copy(w, k, block, to, src=None):
            px, py, pc = block
            rows = outs[w].at[4 * px + 2 * py + pc]
            return pltpu.make_async_remote_copy(
                src_ref=rows if src is None else src, dst_ref=rows,
                send_sem=send_sems.at[7 * w + k], recv_sem=recv_sems.at[7 * w + k],
                device_id=to, device_id_type=MESH)

        started = []
        for w in range(n):
            mine = pltpu.make_async_copy(ins[w], outs[w].at[4 * x + 2 * y + c], local_sems.at[w])
            mine.start()
            started.append(mine)
        sends = []
        for w in range(n):
            first = [copy(w, 0, me, sibling, src=ins[w])]
            first += [copy(w, 1 + j, me, (*chip, c), src=ins[w]) for j, chip in enumerate(chips)]
            for cp in first:
                cp.start()
            sends += first
        for w in range(n):
            for j, chip in enumerate(chips):
                copy(w, 1 + j, (*chip, c), me).wait_recv()
                passed = copy(w, 4 + j, (*chip, c), sibling)
                passed.start()
                sends.append(passed)
        for w in range(n):
            copy(w, 0, sibling, me).wait_recv()
            for j, chip in enumerate(chips):
                copy(w, 4 + j, (*chip, 1 - c), me).wait_recv()
        for cp in sends:
            cp.wait_send()
        for mine in started:
            mine.wait()

    return pl.pallas_call(
        body, name="weights_all_gather",
        in_specs=[_any_spec()] * n, out_specs=[_any_spec()] * n,
        out_shape=[jax.ShapeDtypeStruct((N_DEV,) + s.shape, s.dtype) for s in shards],
        scratch_shapes=[pltpu.SemaphoreType.DMA((7 * n,)), pltpu.SemaphoreType.DMA((7 * n,)),
                        pltpu.SemaphoreType.DMA((n,))],
    )(*shards)


def _chip_of(j, x, y):
    return (1 - x if j & 1 else x, 1 - y if j & 2 else y)


_HBM_SPEC = pl.BlockSpec(memory_space=pltpu.HBM)
_SEM_SPEC = pl.BlockSpec(memory_space=pltpu.SEMAPHORE)
_DATAFLOW_EFFECT = pltpu.SideEffectType.DATAFLOW_SIDE_EFFECTING
TOKEN_SHAPE = (8, D_MODEL)


def _copies_start(name, arrays, n_copies, plan):
    n = len(arrays)

    def body(*refs):
        send_sems, recv_sems, token = refs[n], refs[n + 1], refs[2 * n + 2]
        copies = plan(refs[:n])
        assert len(copies) == n_copies
        for k, (src, dst, dev) in enumerate(copies):
            pltpu.make_async_remote_copy(src_ref=src, dst_ref=dst, send_sem=send_sems.at[k], recv_sem=recv_sems.at[k],
                                         device_id=dev, device_id_type=MESH).start()
        token[...] = jnp.zeros_like(token)

    outs = pl.pallas_call(
        body, name=name,
        out_shape=(pltpu.SemaphoreType.DMA((n_copies,)), pltpu.SemaphoreType.DMA((n_copies,)),
                   *[pltpu.HBM(a.shape, a.dtype) for a in arrays], jax.ShapeDtypeStruct(TOKEN_SHAPE, F32)),
        in_specs=[_HBM_SPEC] * n,
        out_specs=(_SEM_SPEC, _SEM_SPEC, *[_HBM_SPEC] * n, pl.BlockSpec(memory_space=pltpu.VMEM)),
        input_output_aliases={i: i + 2 for i in range(n)},
        compiler_params=pltpu.CompilerParams(has_side_effects=_DATAFLOW_EFFECT),
    )(*[pltpu.with_memory_space_constraint(a, pltpu.HBM) for a in arrays])
    return outs[0], outs[1], list(outs[2:2 + n]), outs[2 + n]


def _copies_wait(name, send_sems, recv_sems, arrays, n_copies, plan, after):
    n = len(arrays)
    after = list(after) if isinstance(after, (list, tuple)) else [after]

    def body(*refs):
        send_ref, recv_ref = refs[n], refs[n + 1]
        copies = plan(refs[:n])
        assert len(copies) == n_copies
        for k, (src, dst, dev) in enumerate(copies):
            cp = pltpu.make_async_remote_copy(src_ref=src, dst_ref=dst, send_sem=send_ref.at[k], recv_sem=recv_ref.at[k],
                                              device_id=dev, device_id_type=MESH)
            cp.wait_send()
            cp.wait_recv()

    outs = pl.pallas_call(
        body, name=name, out_shape=tuple(pltpu.HBM(a.shape, a.dtype) for a in arrays),
        in_specs=[_HBM_SPEC] * n + [_SEM_SPEC, _SEM_SPEC] + [pl.BlockSpec(memory_space=pl.ANY)] * len(after),
        out_specs=tuple([_HBM_SPEC] * n), input_output_aliases={i: i for i in range(n)},
        compiler_params=pltpu.CompilerParams(has_side_effects=_DATAFLOW_EFFECT),
    )(*arrays, send_sems, recv_sems, *after)
    return list(outs)


def _after(small, token):
    return small if token is None else small + token[0:1, :small.shape[-1]]


def _gather_plan_out(n):
    def plan(refs):
        x, y, c = _mesh_position()
        me = 4 * x + 2 * y + c
        copies = []
        for w in range(n):
            land = refs[n + w].at[me]
            copies.append((refs[w], land, (x, y, 1 - c)))
            for j in range(1, 4):
                copies.append((refs[w], land, (*_chip_of(j, x, y), c)))
        return copies
    return plan


def _gather_plan_pass(n):
    def plan(refs):
        x, y, c = _mesh_position()
        copies = []
        for w in range(n):
            for j in range(1, 4):
                px, py = _chip_of(j, x, y)
                rows = refs[w].at[4 * px + 2 * py + c]
                copies.append((rows, rows, (x, y, 1 - c)))
        return copies
    return plan


def _reduce_plan_sibling(n):
    def plan(refs):
        x, y, c = _mesh_position()
        copies = []
        for w in range(n):
            for j in range(4):
                px, py = _chip_of(j, x, y)
                copies.append((refs[w].at[4 * px + 2 * py + (1 - c)], refs[n + w].at[j], (x, y, 1 - c)))
        return copies
    return plan


def _reduce_plan_chips(n):
    def plan(refs):
        x, y, c = _mesh_position()
        copies = []
        for w in range(n):
            for j in range(1, 4):
                copies.append((refs[w].at[j - 1], refs[n + w].at[j - 1], (*_chip_of(j, x, y), c)))
        return copies
    return plan


def _chip_partials(grad, recv, name, tr):
    _, rows, width = grad.shape

    def body(g_ref, r_ref, own_ref, other_ref):
        x, y, c = _mesh_position()
        for j in range(4):
            px, py = _chip_of(j, x, y)
            total = g_ref[4 * px + 2 * py + c].astype(F32) + r_ref[j].astype(F32)
            if j == 0:
                own_ref[...] = total
            else:
                other_ref[j - 1] = total.astype(BF16)

    return pl.pallas_call(
        body, name=name, grid=(rows // tr,),
        in_specs=[pl.BlockSpec((N_DEV, tr, width), lambda i: (0, i, 0)),
                  pl.BlockSpec((4, tr, width), lambda i: (0, i, 0))],
        out_specs=[pl.BlockSpec((tr, width), lambda i: (i, 0)), pl.BlockSpec((3, tr, width), lambda i: (0, i, 0))],
        out_shape=[jax.ShapeDtypeStruct((rows, width), F32), jax.ShapeDtypeStruct((3, rows, width), BF16)],
        compiler_params=_params(("parallel",), 2 * 20 * tr * width * 2 + 8 * tr * width * 4),
    )(grad, recv)


def _adamw_math(w, g, m, v):
    m = ADAM_B1 * m + (1.0 - ADAM_B1) * g
    v = ADAM_B2 * v + (1.0 - ADAM_B2) * (g * g)
    m_hat = m / (1.0 - ADAM_B1 ** ADAM_STEP)
    v_hat = v / (1.0 - ADAM_B2 ** ADAM_STEP)
    delta = -ADAM_LR * (m_hat / (jnp.sqrt(v_hat) + ADAM_EPS) + ADAM_WD * w)
    return delta, m, v


def _adamw_shard(own, recv, w, m, v, name, tr):
    rows, width = own.shape

    def body(own_ref, r_ref, w_ref, m_ref, v_ref, g_ref, d_ref, nm_ref, nv_ref):
        g = own_ref[...]
        for j in range(3):
            g = g + r_ref[j].astype(F32)
        g_ref[...] = g
        d_ref[...], nm_ref[...], nv_ref[...] = _adamw_math(w_ref[...], g, m_ref[...], v_ref[...])

    spec = pl.BlockSpec((tr, width), lambda i: (i, 0))
    shape = jax.ShapeDtypeStruct((rows, width), F32)
    return pl.pallas_call(
        body, name=name, grid=(rows // tr,),
        in_specs=[spec, pl.BlockSpec((3, tr, width), lambda i: (0, i, 0)), spec, spec, spec],
        out_specs=[spec] * 4, out_shape=[shape] * 4,
        compiler_params=_params(("parallel",), 22 * tr * width * 4),
    )(own, recv, w, m, v)


def _small_all_reduce_adamw(gpart, lbpack, wpack, mpack, vpack, after):
    def body(gp_ref, lb_ref, w_ref, m_ref, v_ref, after_ref, g_ref, d_ref, nm_ref, nv_ref, gath_ref, send_sems, recv_sems):
        del after_ref
        x, y, c = _mesh_position()
        me = 4 * x + 2 * y + c
        gath_ref[me] = gp_ref[...]
        copies = []
        for j in range(1, N_DEV):
            peer = (x ^ (j >> 2), y ^ ((j >> 1) & 1), c ^ (j & 1))
            cp = pltpu.make_async_remote_copy(
                src_ref=gp_ref, dst_ref=gath_ref.at[me], send_sem=send_sems.at[j - 1], recv_sem=recv_sems.at[j - 1],
                device_id=peer, device_id_type=MESH)
            cp.start()
            copies.append(cp)
        for cp in copies:
            cp.wait()
        tot = gath_ref[0]
        for s in range(1, N_DEV):
            tot = tot + gath_ref[s]
        lb = lb_ref[...]
        dl = tot[16:32, :] * lb * (1.0 - lb)
        g = jnp.concatenate([tot[0:16, :], dl[0:8, :], -dl[0:8, :], dl[8:16, :], -dl[8:16, :],
                             tot[32:SMALL_GRAD_ROWS, :]], axis=0)
        g_ref[...] = g
        d_ref[...], nm_ref[...], nv_ref[...] = _adamw_math(w_ref[...], g, m_ref[...], v_ref[...])

    vm = pl.BlockSpec(memory_space=pltpu.VMEM)
    shape = jax.ShapeDtypeStruct((SMALL_ROWS, LANE), F32)
    return pl.pallas_call(
        body, name="small_all_reduce_adamw", in_specs=[vm] * 5 + [_any_spec()], out_specs=[vm] * 4,
        out_shape=[shape] * 4,
        scratch_shapes=[pltpu.VMEM((N_DEV, SMALL_GRAD_ROWS, LANE), F32),
                        pltpu.SemaphoreType.DMA((N_DEV - 1,)), pltpu.SemaphoreType.DMA((N_DEV - 1,))],
    )(gpart, lbpack, wpack, mpack, vpack, after)


_SMALL_NAMES = ("norm_mix_gain", "norm_mem_gain", "lb_logits_fw", "lb_logits_bw", "norm_ffn_gain",
                "hg_norm_gain", "da_q_gain", "da_k_gain", "mem_q_gain", "mem_k_gain")
_SMALL_ROW0 = {"norm_mix_gain": 0, "norm_mem_gain": 8, "lb_logits_fw": 16, "lb_logits_bw": 32, "norm_ffn_gain": 48,
               "hg_norm_gain": 56, "da_q_gain": 64, "da_k_gain": 72, "mem_q_gain": 80, "mem_k_gain": 88}
_LOSS_ROW = 96


def _pack_rows(parts, total_rows):
    rows = []
    for p in parts:
        r = p.reshape(-1, LANE)
        rows.append(jnp.pad(r, ((0, -r.shape[0] % 8), (0, 0))))
    used = sum(r.shape[0] for r in rows)
    if total_rows > used:
        rows.append(jnp.zeros((total_rows - used, LANE), F32))
    return jnp.concatenate(rows, axis=0)


def _unpack_small(pack, like):
    out = {}
    for name in _SMALL_NAMES:
        n = like[name].size // LANE
        r0 = _SMALL_ROW0[name]
        out[name] = pack[r0:r0 + n].reshape(like[name].shape)
    return out


def kernel(x, mem, norm_mix_gain, norm_mem_gain, w_in, lb_logits_fw, lb_logits_bw, hg_norm_gain, da_q_gain, da_k_gain, w_mem_kv, mem_q_gain, mem_k_gain, w_proj_hg, w_proj_da, w_proj_mem, w_out, norm_ffn_gain, w_ffn_in, w_ffn_out, loss_target, m_norm_mix_gain, m_norm_mem_gain, m_w_in, m_lb_logits_fw, m_lb_logits_bw, m_hg_norm_gain, m_da_q_gain, m_da_k_gain, m_w_mem_kv, m_mem_q_gain, m_mem_k_gain, m_w_proj_hg, m_w_proj_da, m_w_proj_mem, m_w_out, m_norm_ffn_gain, m_w_ffn_in, m_w_ffn_out, v_norm_mix_gain, v_norm_mem_gain, v_w_in, v_lb_logits_fw, v_lb_logits_bw, v_hg_norm_gain, v_da_q_gain, v_da_k_gain, v_w_mem_kv, v_mem_q_gain, v_mem_k_gain, v_w_proj_hg, v_w_proj_da, v_w_proj_mem, v_w_out, v_norm_ffn_gain, v_w_ffn_in, v_w_ffn_out):
    small_w = dict(norm_mix_gain=norm_mix_gain, norm_mem_gain=norm_mem_gain, lb_logits_fw=lb_logits_fw,
                   lb_logits_bw=lb_logits_bw, norm_ffn_gain=norm_ffn_gain, hg_norm_gain=hg_norm_gain,
                   da_q_gain=da_q_gain, da_k_gain=da_k_gain, mem_q_gain=mem_q_gain, mem_k_gain=mem_k_gain)
    small_m = dict(norm_mix_gain=m_norm_mix_gain, norm_mem_gain=m_norm_mem_gain, lb_logits_fw=m_lb_logits_fw,
                   lb_logits_bw=m_lb_logits_bw, norm_ffn_gain=m_norm_ffn_gain, hg_norm_gain=m_hg_norm_gain,
                   da_q_gain=m_da_q_gain, da_k_gain=m_da_k_gain, mem_q_gain=m_mem_q_gain, mem_k_gain=m_mem_k_gain)
    small_v = dict(norm_mix_gain=v_norm_mix_gain, norm_mem_gain=v_norm_mem_gain, lb_logits_fw=v_lb_logits_fw,
                   lb_logits_bw=v_lb_logits_bw, norm_ffn_gain=v_norm_ffn_gain, hg_norm_gain=v_hg_norm_gain,
                   da_q_gain=v_da_q_gain, da_k_gain=v_da_k_gain, mem_q_gain=v_mem_q_gain, mem_k_gain=v_mem_k_gain)
    big_names = ("w_in", "w_mem_kv", "w_proj_hg", "w_proj_da", "w_proj_mem", "w_out", "w_ffn_in", "w_ffn_out")
    big_w = dict(w_in=w_in[0], w_mem_kv=w_mem_kv[0], w_proj_hg=w_proj_hg[0], w_proj_da=w_proj_da[0],
                 w_proj_mem=w_proj_mem[0], w_out=w_out[0], w_ffn_in=w_ffn_in[0], w_ffn_out=w_ffn_out[0])
    big_m = dict(w_in=m_w_in[0], w_mem_kv=m_w_mem_kv[0], w_proj_hg=m_w_proj_hg[0], w_proj_da=m_w_proj_da[0],
                 w_proj_mem=m_w_proj_mem[0], w_out=m_w_out[0], w_ffn_in=m_w_ffn_in[0], w_ffn_out=m_w_ffn_out[0])
    big_v = dict(w_in=v_w_in[0], w_mem_kv=v_w_mem_kv[0], w_proj_hg=v_w_proj_hg[0], w_proj_da=v_w_proj_da[0],
                 w_proj_mem=v_w_proj_mem[0], w_out=v_w_out[0], w_ffn_in=v_w_ffn_in[0], w_ffn_out=v_w_ffn_out[0])

    row_tile = dict(w_in=128, w_mem_kv=128, w_proj_hg=128, w_proj_da=512, w_proj_mem=512, w_out=128,
                    w_ffn_in=128, w_ffn_out=352)
    rest = _BIG_NAMES[1:]
    shards = [big_w[n].astype(BF16) for n in rest]
    state = {}
    big_out = {}

    def reduce_start(group, stacked):
        names = tuple(stacked)
        arrays = [stacked[n] for n in names] + [lax.empty((4,) + stacked[n].shape[1:], BF16) for n in names]
        plan = _reduce_plan_sibling(len(names))
        send, recv, thru, token = _copies_start(f"grads_{group}_sibling_start", arrays, 4 * len(names), plan)
        state[group] = dict(names=names, plan=plan, send=send, recv=recv, arrays=thru)
        return token

    def reduce_middle(group, after):
        st = state[group]
        names, k = st["names"], len(st["names"])
        thru = _copies_wait(f"grads_{group}_sibling_wait", st["send"], st["recv"], st["arrays"], 4 * k, st["plan"], after)
        partials = [_chip_partials(thru[i], thru[k + i], f"chip_partials_{n}", row_tile[n]) for i, n in enumerate(names)]
        arrays = [p[1] for p in partials] + [lax.empty(p[1].shape, BF16) for p in partials]
        plan = _reduce_plan_chips(k)
        send, recv, thru2, token = _copies_start(f"grads_{group}_chips_start", arrays, 3 * k, plan)
        state[group] = dict(names=names, plan=plan, send=send, recv=recv, arrays=thru2, own=[p[0] for p in partials])
        return token

    def reduce_finish(group, after):
        st = state.pop(group)
        names, k = st["names"], len(st["names"])
        thru = _copies_wait(f"grads_{group}_chips_wait", st["send"], st["recv"], st["arrays"], 3 * k, st["plan"], after)
        for i, n in enumerate(names):
            big_out[n] = _adamw_shard(st["own"][i], thru[k + i], big_w[n], big_m[n], big_v[n], "adamw_" + n, row_tile[n])

    win_st = _all_gather([big_w["w_in"].astype(BF16)])[0]
    step = _local_step_stages(x[0], mem[0], loss_target[0], small_w, win_st)
    event, payload = next(step)
    local = None
    while True:
        reply = None
        if event == "begin":
            nr = len(rest)
            me = 4 * lax.axis_index("x") + 2 * lax.axis_index("y") + lax.axis_index("c")
            arrays = shards + [lax.dynamic_update_slice(lax.empty((N_DEV,) + s.shape, BF16), s[None], (me, 0, 0))
                               for s in shards]
            plan = _gather_plan_out(nr)
            send, recv, thru, reply = _copies_start("weights_rest_out_start", arrays, 4 * nr, plan)
            state["gather"] = dict(plan=plan, send=send, recv=recv, arrays=thru)
        elif event == "after_gla_fwd":
            st = state["gather"]
            nr = len(rest)
            thru = _copies_wait("weights_rest_out_wait", st["send"], st["recv"], st["arrays"], 4 * nr, st["plan"], payload)
            plan = _gather_plan_pass(nr)
            send, recv, lands, reply = _copies_start("weights_rest_pass_start", thru[nr:], 3 * nr, plan)
            state["gather"] = dict(plan=plan, send=send, recv=recv, arrays=lands)
        elif event == "need_weights":
            st = state.pop("gather")
            nr = len(rest)
            lands = _copies_wait("weights_rest_pass_wait", st["send"], st["recv"], st["arrays"], 3 * nr, st["plan"], payload)
            reply = dict(zip(rest, lands))
        elif event == "grads_ffn":
            reply = reduce_start("ffn", payload)
        elif event == "after_gate_merge_bwd":
            reply = reduce_middle("ffn", payload)
        elif event == "grads_mix":
            reply = reduce_start("mix", payload)
        elif event == "after_da_bwd_g0":
            reply = reduce_middle("mix", payload)
            reduce_finish("ffn", payload)
        elif event == "after_gla_bwd":
            reduce_finish("mix", payload)
        elif event == "grads_in":
            reply = reduce_start("in", payload)
        elif event == "after_proj_bwd_act":
            reply = reduce_middle("in", payload)
        elif event == "end":
            local = payload
            reduce_finish("in", [local["grad_x"]] + [big_out[n][0] for n in rest])
            break
        event, payload = step.send(reply)
    grad_x = local["grad_x"]

    wpack = _pack_rows([small_w[n] for n in _SMALL_NAMES], SMALL_ROWS)
    mpack = _pack_rows([small_m[n] for n in _SMALL_NAMES], SMALL_ROWS)
    vpack = _pack_rows([small_v[n] for n in _SMALL_NAMES], SMALL_ROWS)
    gs, ds, ms, vs = _small_all_reduce_adamw(local["gpart"], local["lbpack"], wpack, mpack, vpack, big_out["w_in"][0])
    loss = gs[_LOSS_ROW, 0]
    small_out = [_unpack_small(t, small_w) for t in (gs, ds, ms, vs)]

    order = ("norm_mix_gain", "norm_mem_gain", "w_in", "lb_logits_fw", "lb_logits_bw", "hg_norm_gain", "da_q_gain",
             "da_k_gain", "w_mem_kv", "mem_q_gain", "mem_k_gain", "w_proj_hg", "w_proj_da", "w_proj_mem", "w_out",
             "norm_ffn_gain", "w_ffn_in", "w_ffn_out")
    outs = [loss, grad_x[None]]
    for kind in range(4):
        for n in order:
            outs.append(big_out[n][kind][None] if n in big_out else small_out[kind][n])
    return tuple(outs)


_BIG_NAMES = ("w_in", "w_mem_kv", "w_proj_hg", "w_proj_da", "w_proj_mem", "w_out", "w_ffn_in", "w_ffn_out")


def _local_step(xs, mems, target, sw, wg):
    step = _local_step_stages(xs, mems, target, sw, wg["w_in"])
    stacked = {}
    event, payload = next(step)
    while event != "end":
        if event.startswith("grads_"):
            stacked.update(payload)
        event, payload = step.send(wg if event == "need_weights" else None)
    return dict(payload, stacked=stacked)


def _local_step_stages(xs, mems, target, sw, win_st):
    norm_mix_gain, norm_mem_gain, norm_ffn_gain = sw["norm_mix_gain"], sw["norm_mem_gain"], sw["norm_ffn_gain"]
    lb_logits_fw, lb_logits_bw, hg_norm_gain = sw["lb_logits_fw"], sw["lb_logits_bw"], sw["hg_norm_gain"]
    da_q_gain, da_k_gain, mem_q_gain, mem_k_gain = sw["da_q_gain"], sw["da_k_gain"], sw["mem_q_gain"], sw["mem_k_gain"]

    token = yield "begin", None
    lb_fw = _lb_table(lb_logits_fw, "lb_table_fw")
    lb_bw = _lb_table(lb_logits_bw, "lb_table_bw")
    lb = jnp.concatenate([lb_fw, lb_bw], axis=0).reshape(2, HG_HEADS, 1, HEAD_DIM)
    h = _rmsnorm_fwd(xs, _after(norm_mix_gain, token), "norm_mix_fwd", 512)
    proj = _matmul(h, win_st, "nn", F32, 512, IN_SHARD, D_MODEL, "proj_fwd", b_stacked=True, n_outer=True)
    o_hg, o_pre, states = _gla_fwd(proj, lb, hg_norm_gain)
    token = yield "after_gla_fwd", o_hg
    da = [_da_fwd(proj, _after(da_q_gain, token), da_k_gain, g) for g in range(3)]
    o_da, o_da32, lse_da = _da_merge([t[0] for t in da], [t[1] for t in da], 512)
    wg = yield "need_weights", o_da
    wkv = wg["w_mem_kv"].reshape(D_MODEL, 2 * MEM_WIDTH)
    wphg = wg["w_proj_hg"].reshape(D_MODEL, D_MODEL)
    wpda = jnp.transpose(wg["w_proj_da"], (1, 0, 2)).reshape(DA_WIDTH, D_MODEL)
    wpmem = jnp.transpose(wg["w_proj_mem"], (1, 0, 2)).reshape(MEM_WIDTH, D_MODEL)
    wout = wg["w_out"].reshape(D_MODEL, D_MODEL)
    wfin = jnp.transpose(wg["w_ffn_in"], (1, 0, 2)).reshape(D_MODEL, 2 * D_FF)
    wfout = wg["w_ffn_out"].reshape(D_FF, D_MODEL)
    mem_n = _rmsnorm_fwd(mems, norm_mem_gain, "norm_mem_fwd", N_MEM)
    kv = _matmul(mem_n, wkv, "nn", F32, N_MEM, 1024, D_MODEL, "mem_kv_fwd")
    o_mem = _mem_fwd(proj, kv, mem_q_gain, mem_k_gain, 512)
    t_hg = _matmul(o_hg, wphg, "nn", F32, 512, 1024, D_MODEL, "proj_hg_fwd")
    t_da = _matmul(o_da, wpda, "nn", F32, 512, 1024, DA_WIDTH, "proj_da_fwd")
    t_mem = _matmul(o_mem, wpmem, "nn", F32, 512, 1024, MEM_WIDTH, "proj_mem_fwd")
    merged = _gate_merge_fwd(proj, t_hg, t_da, t_mem, 256)
    u = _matmul(merged, wout, "nn", F32, 512, 1024, D_MODEL, "out_fwd")
    x1, h2 = _residual_rmsnorm_fwd(xs, u, norm_ffn_gain, "norm_ffn_fwd", 512)
    ab = _matmul(h2, wfin, "nn", F32, 512, 1408, D_MODEL, "ffn_in_fwd")
    act = _swiglu_fwd(ab, 256)
    z = _matmul(act, wfout, "nn", F32, 512, 1024, D_FF, "ffn_out_fwd")
    dy, dyb, loss_part = _loss_head(x1, z, target, 512)

    dact = _matmul(dyb, wfout, "nt", F32, 512, 1408, D_MODEL, "ffn_out_bwd_act")
    g_wfout = _matmul(act, dyb, "tn", BF16, 1408, 1024, 512, "ffn_out_bwd_w")
    dab = _swiglu_bwd(ab, dact, 256)
    dh2 = _matmul(dab, wfin, "nt", F32, 512, 1024, 1408, "ffn_in_bwd_act")
    g_wfin = _matmul(h2, dab, "tn", BF16, 1024, 1408, 512, "ffn_in_bwd_w")
    token = yield "grads_ffn", dict(
        w_ffn_in=jnp.transpose(g_wfin.reshape(D_MODEL, N_DEV, 2 * D_FF // N_DEV), (1, 0, 2)),
        w_ffn_out=g_wfout.reshape(N_DEV, D_FF // N_DEV, D_MODEL))
    dx1, dx1b, g_norm_ffn = _rmsnorm_bwd(x1, dh2, dy, _after(norm_ffn_gain, token), "norm_ffn_bwd", 256)
    dmerged = _matmul(dx1b, wout, "nt", F32, 512, 1024, D_MODEL, "out_bwd_act")
    g_wout = _matmul(merged, dx1b, "tn", BF16, 1024, 1024, 512, "out_bwd_w")
    dt_hg, dt_da, dt_mem, dgates = _gate_merge_bwd(proj, t_hg, t_da, t_mem, dmerged, 256)
    token = yield "after_gate_merge_bwd", dt_hg
    do_hg = _matmul(dt_hg, wphg, "nt", F32, 512, 1024, D_MODEL, "proj_hg_bwd_act", after=token)
    g_wphg = _matmul(o_hg, dt_hg, "tn", BF16, 1024, 1024, 512, "proj_hg_bwd_w")
    do_da = _matmul(dt_da, wpda, "nt", F32, 512, DA_WIDTH, D_MODEL, "proj_da_bwd_act")
    g_wpda = _matmul(o_da, dt_da, "tn", BF16, DA_WIDTH, LANE, 512, "proj_da_bwd_w", out_stacked=True)
    do_mem = _matmul(dt_mem, wpmem, "nt", F32, 512, MEM_WIDTH, D_MODEL, "proj_mem_bwd_act")
    g_wpmem = _matmul(o_mem, dt_mem, "tn", BF16, MEM_WIDTH, LANE, 512, "proj_mem_bwd_w", out_stacked=True)

    dmemq, dk_mem, dv_mem, g_mem_q, g_mem_k = _mem_bwd(proj, kv, mem_q_gain, mem_k_gain, do_mem, 512)
    dkv = jnp.concatenate([dk_mem, dv_mem], axis=1).astype(BF16)
    g_wkv = _matmul(mem_n, dkv, "tn", BF16, 1024, 1024, N_MEM, "mem_kv_bwd_w")
    dmem_n = _matmul(dkv, wkv, "nt", F32, N_MEM, 1024, 1024, "mem_kv_bwd_act")
    g_norm_mem = _gain_grad(mems, dmem_n, "norm_mem_bwd")
    token = yield "grads_mix", dict(
        w_mem_kv=g_wkv.reshape(N_DEV, D_MODEL // N_DEV, 2 * MEM_WIDTH),
        w_proj_hg=g_wphg.reshape(N_DEV, D_MODEL // N_DEV, D_MODEL),
        w_proj_da=g_wpda, w_proj_mem=g_wpmem,
        w_out=g_wout.reshape(N_DEV, D_MODEL // N_DEV, D_MODEL))

    dd_da = _da_rowdot(do_da, o_da32, 512)
    da_b = [_da_bwd(proj, _after(da_q_gain, token), da_k_gain, do_da, lse_da, dd_da, 0)]
    token = yield "after_da_bwd_g0", da_b[0][0]
    da_b += [_da_bwd(proj, _after(da_q_gain, token), da_k_gain, do_da, lse_da, dd_da, g) for g in (1, 2)]
    g_da_q = da_b[0][3] + da_b[1][3] + da_b[2][3]
    g_da_k = da_b[0][4] + da_b[1][4] + da_b[2][4]

    dq_hg, df_hg, dv_hg, dg_hg, dlb, g_hg_norm = _gla_bwd(proj, lb, hg_norm_gain, o_pre, states, do_hg)
    yield "after_gla_bwd", dq_hg

    dproj = jnp.concatenate(
        [dq_hg, df_hg[0], df_hg[1], dv_hg, dg_hg]
        + [t[0] for t in da_b] + [t[1] for t in da_b] + [t[2] for t in da_b] + [dmemq, dgates], axis=1)
    g_win = _matmul(h, dproj, "tn", BF16, 1024, IN_SHARD, 512, "proj_bwd_w", out_stacked=True)
    token = yield "grads_in", dict(w_in=g_win)
    dh = _matmul(dproj, win_st, "nt", F32, 512, 1024, IN_SHARD, "proj_bwd_act", b_stacked=True, after=token)
    token = yield "after_proj_bwd_act", dh
    grad_x, _, g_norm_mix = _rmsnorm_bwd(xs, dh, dx1, _after(norm_mix_gain, token), "norm_mix_bwd", 256)

    gpart = _pack_rows([g_norm_mix, g_norm_mem, dlb[0], dlb[1], g_norm_ffn, g_hg_norm, g_da_q, g_da_k,
                        g_mem_q, g_mem_k, loss_part], SMALL_GRAD_ROWS)
    lbpack = jnp.concatenate([lb_fw.reshape(8, LANE), lb_bw.reshape(8, LANE)], axis=0)
    yield "end", dict(grad_x=grad_x, gpart=gpart, lbpack=lbpack)
```

```python
import numpy as np
import jax
import jax.numpy as jnp
from jax import lax
from jax.experimental import pallas as pl
from jax.experimental.pallas import tpu as pltpu

F32 = jnp.float32
BF16 = jnp.bfloat16
MESH = pl.DeviceIdType.MESH

SEQ = 4096
D_MODEL = 1024
N_DEV = 8
N_MEM = 256
RMS_EPS = 1e-6
NEG_INF = -1e30
LANE = 128
HEAD_DIM = 128
HG_HEADS = 8
HG_CHUNK = 64
HG_SCALE = HEAD_DIM ** -0.5
DA_DILATIONS = (1, 4, 16)
DA_RADIUS = 64
DA_HEADS_PER_GROUP = 4
DA_HEADS = 12
DA_WIDTH = 512
DA_SCALE = HEAD_DIM ** -0.5
DA_QB = 128
DA_WIN = 256
DA_WAYS = 2
MEM_HEADS = 4
MEM_WIDTH = 512
MEM_SCALE = HEAD_DIM ** -0.5
D_FF = 2816
IN_COLS = 13312
IN_SHARD = IN_COLS // N_DEV
CB_HG_Q, CB_F, CB_HG_I, CB_HG_G = 0, 8, 24, 32
CB_DA_Q, CB_DA_K, CB_DA_V, CB_MEM_Q = 40, 52, 64, 76
N_CB = IN_COLS // LANE
ADAM_LR, ADAM_B1, ADAM_B2, ADAM_EPS, ADAM_WD, ADAM_STEP = 0.001, 0.9, 0.999, 1e-08, 0.01, 10
VMEM_BYTES_V7X = 64 * 1024 * 1024
SMALL_ROWS = 104
SMALL_GRAD_ROWS = 88

_NN = (((1,), (0,)), ((), ()))
_NT = (((1,), (1,)), ((), ()))
_TN = (((0,), (0,)), ((), ()))


def _dot(a, b, dims):
    return lax.dot_general(a.astype(BF16), b.astype(BF16), dims, preferred_element_type=F32)


def _dot_exact(a, b, dims):
    return lax.dot_general(a, b, dims, precision=lax.Precision.HIGHEST, preferred_element_type=F32)


def _sigmoid(x):
    return 0.5 * jnp.tanh(0.5 * x) + 0.5


def _params(semantics, est_bytes):
    limit = int(min(VMEM_BYTES_V7X - (6 << 20), max(32 << 20, est_bytes * 3 // 2)))
    return pltpu.CompilerParams(dimension_semantics=semantics, vmem_limit_bytes=limit)


def _nbytes(shape, dtype):
    return int(np.prod(shape)) * jnp.dtype(dtype).itemsize


def _alibi_slopes(n):
    return (2.0 ** (-8.0 * np.arange(1, n + 1) / n)).astype(np.float32)


def _matmul(a, b, mode, out_dtype, tm, tn, tk, name, b_stacked=False, out_stacked=False, n_outer=False, after=None):
    if mode == "tn":
        kdim, m = a.shape
    else:
        m, kdim = a.shape
    if b_stacked:
        if mode == "nn":
            n = b.shape[0] * b.shape[2]
            assert tn == b.shape[2] and tk == kdim == b.shape[1]
        else:
            assert mode == "nt" and tk == b.shape[2] and b.shape[0] * tk == kdim
            n = b.shape[1]
    else:
        n = b.shape[0] if mode == "nt" else b.shape[1]
    assert m % tm == 0 and n % tn == 0 and kdim % tk == 0
    gm, gn, gk = m // tm, n // tn, kdim // tk

    def ijk(f):
        if n_outer:
            return lambda j, i, k: f(i, j, k)
        return f

    if mode == "tn":
        a_spec = pl.BlockSpec((tk, tm), ijk(lambda i, j, k: (k, i)))
    else:
        a_spec = pl.BlockSpec((tm, tk), ijk(lambda i, j, k: (i, k)))
    if b_stacked and mode == "nn":
        b_spec = pl.BlockSpec((None, tk, tn), ijk(lambda i, j, k: (j, 0, 0)))
    elif b_stacked:
        b_spec = pl.BlockSpec((None, tn, tk), ijk(lambda i, j, k: (k, j, 0)))
    elif mode == "nt":
        b_spec = pl.BlockSpec((tn, tk), ijk(lambda i, j, k: (j, k)))
    else:
        b_spec = pl.BlockSpec((tk, tn), ijk(lambda i, j, k: (k, j)))
    if out_stacked:
        assert tm == m
        out_shape = jax.ShapeDtypeStruct((gn, m, tn), out_dtype)
        o_spec = pl.BlockSpec((None, tm, tn), ijk(lambda i, j, k: (j, i, 0)))
    else:
        out_shape = jax.ShapeDtypeStruct((m, n), out_dtype)
        o_spec = pl.BlockSpec((tm, tn), ijk(lambda i, j, k: (i, j)))
    dims = {"nn": _NN, "nt": _NT, "tn": _TN}[mode]

    def body(*refs):
        a_ref, b_ref = refs[0], refs[1]
        o_ref = refs[2 if after is None else 3]
        part = _dot(a_ref[...], b_ref[...], dims)
        if gk == 1:
            o_ref[...] = part.astype(out_dtype)
            return
        acc_ref = refs[-1]
        k = pl.program_id(2)

        @pl.when(k == 0)
        def _():
            acc_ref[...] = part

        @pl.when(jnp.logical_and(k > 0, k < gk - 1))
        def _():
            acc_ref[...] += part

        @pl.when(k == gk - 1)
        def _():
            o_ref[...] = (acc_ref[...] + part).astype(out_dtype)

    a_tile = _nbytes((tm, tk), a.dtype)
    b_tile = _nbytes((tk, tn), b.dtype)
    o_tile = _nbytes((tm, tn), out_dtype)
    est = 2 * (a_tile + b_tile + o_tile) + 3 * tm * tn * 4 + (a_tile + b_tile)
    grid = (gn, gm, gk) if n_outer else (gm, gn, gk)
    operands, in_specs = [a, b], [a_spec, b_spec]
    if after is not None:
        operands.append(after)
        in_specs.append(pl.BlockSpec(memory_space=pl.ANY))
    return pl.pallas_call(
        body, name=name, grid=grid, in_specs=in_specs, out_specs=o_spec, out_shape=out_shape,
        scratch_shapes=[] if gk == 1 else [pltpu.VMEM((tm, tn), F32)],
        compiler_params=_params(("parallel", "parallel", "arbitrary"), est),
    )(*operands)


def _row_spec(tr, width, col_block=0):
    return pl.BlockSpec((tr, width), lambda i: (i, col_block))


def _bcast_spec(width):
    return pl.BlockSpec((1, width), lambda i: (0, 0))


def _col_spec(width, tr):
    return pl.BlockSpec((width, tr), lambda i: (0, i))


def _rmsnorm_fwd(x, gain, name, tr, transposed=False):
    rows, width = x.shape

    def body(x_ref, g_ref, o_ref, *t_ref):
        xv = x_ref[...]
        r = lax.rsqrt(jnp.mean(xv * xv, axis=-1, keepdims=True) + RMS_EPS)
        h = xv * r * g_ref[...]
        o_ref[...] = h.astype(BF16)
        if transposed:
            t_ref[0][...] = h.T.astype(BF16)

    out_specs, out_shape = [_row_spec(tr, width)], [jax.ShapeDtypeStruct((rows, width), BF16)]
    if transposed:
        out_specs.append(_col_spec(width, tr))
        out_shape.append(jax.ShapeDtypeStruct((width, rows), BF16))
    out = pl.pallas_call(
        body, name=name, grid=(rows // tr,), in_specs=[_row_spec(tr, width), _bcast_spec(width)],
        out_specs=out_specs, out_shape=out_shape,
        compiler_params=_params(("parallel",), 10 * tr * width * 4),
    )(x, gain)
    return out if transposed else out[0]


def _residual_rmsnorm_fwd(x, u, gain, name, tr):
    rows, width = x.shape

    def body(x_ref, u_ref, g_ref, x1_ref, h_ref, ht_ref):
        xv = x_ref[...] + u_ref[...]
        x1_ref[...] = xv
        r = lax.rsqrt(jnp.mean(xv * xv, axis=-1, keepdims=True) + RMS_EPS)
        h = xv * r * g_ref[...]
        h_ref[...] = h.astype(BF16)
        ht_ref[...] = h.T.astype(BF16)

    return pl.pallas_call(
        body, name=name, grid=(rows // tr,),
        in_specs=[_row_spec(tr, width), _row_spec(tr, width), _bcast_spec(width)],
        out_specs=[_row_spec(tr, width), _row_spec(tr, width), _col_spec(width, tr)],
        out_shape=[jax.ShapeDtypeStruct((rows, width), F32), jax.ShapeDtypeStruct((rows, width), BF16),
                   jax.ShapeDtypeStruct((width, rows), BF16)],
        compiler_params=_params(("parallel",), 14 * tr * width * 4),
    )(x, u, gain)


def _rmsnorm_bwd(x, dh, dres, gain, name, tr):
    rows, width = x.shape

    def body(x_ref, dh_ref, dres_ref, g_ref, dx_ref, dxb_ref, dg_ref):
        xv = x_ref[...]
        r = lax.rsqrt(jnp.mean(xv * xv, axis=-1, keepdims=True) + RMS_EPS)
        xhat = xv * r
        dhv = dh_ref[...]
        dyg = dhv * g_ref[...]
        dx = dres_ref[...] + r * (dyg - xhat * jnp.mean(dyg * xhat, axis=-1, keepdims=True))
        dx_ref[...] = dx
        dxb_ref[...] = dx.astype(BF16)
        part = jnp.sum(dhv * xhat, axis=0, keepdims=True)

        @pl.when(pl.program_id(0) == 0)
        def _():
            dg_ref[...] = part

        @pl.when(pl.program_id(0) > 0)
        def _():
            dg_ref[...] += part

    return pl.pallas_call(
        body, name=name, grid=(rows // tr,),
        in_specs=[_row_spec(tr, width), _row_spec(tr, width), _row_spec(tr, width), _bcast_spec(width)],
        out_specs=[_row_spec(tr, width), _row_spec(tr, width), _bcast_spec(width)],
        out_shape=[jax.ShapeDtypeStruct((rows, width), F32), jax.ShapeDtypeStruct((rows, width), BF16),
                   jax.ShapeDtypeStruct((1, width), F32)],
        compiler_params=_params(("arbitrary",), 14 * tr * width * 4),
    )(x, dh, dres, gain)


def _gain_grad(x, dh, name):
    rows, width = x.shape

    def body(x_ref, dh_ref, dg_ref):
        xv = x_ref[...]
        r = lax.rsqrt(jnp.mean(xv * xv, axis=-1, keepdims=True) + RMS_EPS)
        dg_ref[...] = jnp.sum(dh_ref[...] * xv * r, axis=0, keepdims=True)

    return pl.pallas_call(
        body, name=name, grid=(1,), in_specs=[_row_spec(rows, width), _row_spec(rows, width)],
        out_specs=_bcast_spec(width), out_shape=jax.ShapeDtypeStruct((1, width), F32),
        compiler_params=_params(("arbitrary",), 6 * rows * width * 4),
    )(x, dh)


def _lb_table(logits, name):
    slots, width = logits.shape

    def body(l_ref, o_ref):
        lv = l_ref[...]
        mx = jnp.max(lv, axis=0, keepdims=True)
        e = jnp.exp(lv - mx)
        o_ref[...] = e[0:1, :] / jnp.sum(e, axis=0, keepdims=True)

    return pl.pallas_call(
        body, name=name, grid=(1,), in_specs=[pl.BlockSpec((slots, width), lambda i: (0, 0))],
        out_specs=_bcast_spec(width), out_shape=jax.ShapeDtypeStruct((1, width), F32),
    )(logits)


def _gate_merge_fwd(proj, t_hg, t_da, t_mem, tr):
    w = D_MODEL

    def body(ghg_ref, gda_ref, gmem_ref, thg_ref, tda_ref, tmem_ref, o_ref):
        acc = _sigmoid(ghg_ref[...]) * thg_ref[...]
        acc += _sigmoid(gda_ref[...]) * tda_ref[...]
        acc += _sigmoid(gmem_ref[...]) * tmem_ref[...]
        o_ref[...] = acc.astype(BF16)

    return pl.pallas_call(
        body, name="gate_merge_fwd", grid=(SEQ // tr,),
        in_specs=[_row_spec(tr, w, 10), _row_spec(tr, w, 11), _row_spec(tr, w, 12),
                  _row_spec(tr, w), _row_spec(tr, w), _row_spec(tr, w)],
        out_specs=_row_spec(tr, w), out_shape=jax.ShapeDtypeStruct((SEQ, w), BF16),
        compiler_params=_params(("parallel",), 16 * tr * w * 4),
    )(proj, proj, proj, t_hg, t_da, t_mem)


def _gate_merge_bwd(proj, t_hg, t_da, t_mem, dmerged, tr):
    w = D_MODEL

    def body(ghg_ref, gda_ref, gmem_ref, thg_ref, tda_ref, tmem_ref, dm_ref, dthg_ref, dtda_ref, dtmem_ref, dg_ref):
        dm = dm_ref[...]
        for b, (g_ref, t_ref, dt_ref) in enumerate(((ghg_ref, thg_ref, dthg_ref), (gda_ref, tda_ref, dtda_ref),
                                                    (gmem_ref, tmem_ref, dtmem_ref))):
            s = _sigmoid(g_ref[...])
            dt_ref[...] = (s * dm).astype(BF16)
            dg_ref[:, b * w:(b + 1) * w] = (dm * t_ref[...] * s * (1.0 - s)).astype(BF16)

    return pl.pallas_call(
        body, name="gate_merge_bwd", grid=(SEQ // tr,),
        in_specs=[_row_spec(tr, w, 10), _row_spec(tr, w, 11), _row_spec(tr, w, 12),
                  _row_spec(tr, w), _row_spec(tr, w), _row_spec(tr, w), _row_spec(tr, w)],
        out_specs=[_row_spec(tr, w), _row_spec(tr, w), _row_spec(tr, w), _row_spec(tr, 3 * w)],
        out_shape=[jax.ShapeDtypeStruct((SEQ, w), BF16)] * 3 + [jax.ShapeDtypeStruct((SEQ, 3 * w), BF16)],
        compiler_params=_params(("parallel",), 24 * tr * w * 4),
    )(proj, proj, proj, t_hg, t_da, t_mem, dmerged)


def _swiglu_fwd(ab, tr):
    def body(ab_ref, o_ref, ot_ref):
        a = ab_ref[:, :D_FF]
        b = ab_ref[:, D_FF:]
        act = a * _sigmoid(a) * b
        o_ref[...] = act.astype(BF16)
        ot_ref[...] = act.T.astype(BF16)

    return pl.pallas_call(
        body, name="swiglu_fwd", grid=(SEQ // tr,), in_specs=[_row_spec(tr, 2 * D_FF)],
        out_specs=[_row_spec(tr, D_FF), _col_spec(D_FF, tr)],
        out_shape=[jax.ShapeDtypeStruct((SEQ, D_FF), BF16), jax.ShapeDtypeStruct((D_FF, SEQ), BF16)],
        compiler_params=_params(("parallel",), 10 * tr * 2 * D_FF * 4),
    )(ab)


def _swiglu_bwd(ab, dact, tr):
    def body(ab_ref, d_ref, o_ref):
        a = ab_ref[:, :D_FF]
        b = ab_ref[:, D_FF:]
        d = d_ref[...]
        s = _sigmoid(a)
        silu = a * s
        o_ref[:, :D_FF] = (d * b * (s + silu * (1.0 - s))).astype(BF16)
        o_ref[:, D_FF:] = (d * silu).astype(BF16)

    return pl.pallas_call(
        body, name="swiglu_bwd", grid=(SEQ // tr,), in_specs=[_row_spec(tr, 2 * D_FF), _row_spec(tr, D_FF)],
        out_specs=_row_spec(tr, 2 * D_FF), out_shape=jax.ShapeDtypeStruct((SEQ, 2 * D_FF), BF16),
        compiler_params=_params(("parallel",), 10 * tr * 2 * D_FF * 4),
    )(ab, dact)


def _loss_head(x1, z, target, tr):
    w = D_MODEL

    def body(x_ref, z_ref, t_ref, dy_ref, dyb_ref, loss_ref, acc_ref):
        err = x_ref[...] + z_ref[...] - t_ref[...]
        dy = err * (1.0 / w)
        dy_ref[...] = dy
        dyb_ref[...] = dy.astype(BF16)
        part = jnp.sum(err * err, axis=0, keepdims=True)

        @pl.when(pl.program_id(0) == 0)
        def _():
            acc_ref[...] = part

        @pl.when(pl.program_id(0) > 0)
        def _():
            acc_ref[...] += part

        @pl.when(pl.program_id(0) == SEQ // tr - 1)
        def _():
            total = jnp.sum(acc_ref[...], axis=1, keepdims=True) * (0.5 / w)
            loss_ref[...] = jnp.broadcast_to(total, (1, LANE))

    return pl.pallas_call(
        body, name="loss_head", grid=(SEQ // tr,),
        in_specs=[_row_spec(tr, w), _row_spec(tr, w), _row_spec(tr, w)],
        out_specs=[_row_spec(tr, w), _row_spec(tr, w), _bcast_spec(LANE)],
        out_shape=[jax.ShapeDtypeStruct((SEQ, w), F32), jax.ShapeDtypeStruct((SEQ, w), BF16),
                   jax.ShapeDtypeStruct((1, LANE), F32)],
        scratch_shapes=[pltpu.VMEM((1, w), F32)],
        compiler_params=_params(("arbitrary",), 12 * tr * w * 4),
    )(x1, z, target)


GLA_ROWS = 256
GLA_CPB = GLA_ROWS // HG_CHUNK
GLA_NBLK = SEQ // GLA_ROWS
GLA_WAYS = 4
GLA_TRIPS = GLA_NBLK // GLA_WAYS
GLA_GRAD_WAYS = 2
GLA_GRAD_TRIPS = GLA_NBLK // GLA_GRAD_WAYS
GLA_NCK = SEQ // HG_CHUNK


def _dot_split(m, xv, dims):
    hi = xv.astype(BF16)
    r1 = xv - hi.astype(F32)
    mid = r1.astype(BF16)
    lo = (r1 - mid.astype(F32)).astype(BF16)
    dot = lambda t: lax.dot_general(m, t, dims, preferred_element_type=F32)
    return (dot(lo) + dot(mid)) + dot(hi)


def _gla_block(qc, fc, lbv, masks):
    mask, maskb, direction = masks
    sq = _sigmoid(qc)
    q = qc * sq * HG_SCALE
    sf = _sigmoid(fc)
    forget = lbv + (1.0 - lbv) * sf
    k = 1.0 - forget
    logf = jnp.log(forget)
    b = _dot_split(maskb, logf, _NN)
    ends = []
    for j in range(GLA_CPB):
        lo, hi = j * HG_CHUNK, (j + 1) * HG_CHUNK
        end = jnp.where(direction == 0, b[hi - 1:hi, :], b[lo:lo + 1, :])
        ends.append(jnp.broadcast_to(end, (HG_CHUNK, HEAD_DIM)))
    bt = jnp.concatenate(ends, axis=0)
    eb = jnp.exp(b)
    qt = q * eb
    kt = k * jnp.exp(-b)
    kh = k * jnp.exp(bt - b)
    a = jnp.where(mask, _dot(qt, kt, _NT), 0.0)
    return dict(sq=sq, sf=sf, forget=forget, k=k, b=b, bt=bt, eb=eb, qt=qt, kt=kt, kh=kh, a=a)


def _gla_masks(direction):
    row = lax.broadcasted_iota(jnp.int32, (GLA_ROWS, GLA_ROWS), 0)
    col = lax.broadcasted_iota(jnp.int32, (GLA_ROWS, GLA_ROWS), 1)
    same = (row // HG_CHUNK) == (col // HG_CHUNK)
    mask = jnp.logical_and(same, jnp.where(direction == 0, row - col, col - row) >= 0)
    return mask, jnp.where(mask, 1.0, 0.0).astype(BF16), direction


def _gla_chunk_rows(j):
    return slice(j * HG_CHUNK, (j + 1) * HG_CHUNK)


def _gla_block_rows(b):
    return pl.ds(pl.multiple_of(b * GLA_ROWS, GLA_ROWS), GLA_ROWS)


def _head_spec(col_block0):
    return pl.BlockSpec((SEQ, HEAD_DIM), lambda h, d: (0, col_block0 + h))


def _gla_fwd(proj, lb, gain):
    nck = GLA_NCK

    def body(q_ref, f_ref, v_ref, g_ref, lb_ref, gain_ref, ohg_ref, opre_ref, st_ref, qt_scr, cs_scr, dec_scr):
        d = pl.program_id(1)
        masks = _gla_masks(d)
        lbv = lb_ref[...]

        @pl.when(d == 0)
        def _():
            opre_ref[...] = jnp.zeros_like(opre_ref)

        def intra(s, carry):
            blocks = [s + w * GLA_TRIPS for w in range(GLA_WAYS)]
            rows = [_gla_block_rows(b) for b in blocks]
            loaded = [(q_ref[r, :], f_ref[r, :], v_ref[r, :], opre_ref[r, :]) for r in rows]
            results = []
            for qc, fc, v, o_prev in loaded:
                ck = _gla_block(qc, fc, lbv, masks)
                o = o_prev + _dot(ck["a"], v, _NN)
                cs = [_dot(v[_gla_chunk_rows(j), :], ck["kh"][_gla_chunk_rows(j), :], _TN) for j in range(GLA_CPB)]
                dec = [jnp.exp(ck["bt"][j * HG_CHUNK:j * HG_CHUNK + 8, :]) for j in range(GLA_CPB)]
                results.append((o, ck["qt"].astype(BF16), cs, dec))
            for b, r, (o, qt, cs, dec) in zip(blocks, rows, results):
                opre_ref[r, :] = o
                qt_scr[r, :] = qt
                for j in range(GLA_CPB):
                    cs_scr[b * GLA_CPB + j] = cs[j]
                    dec_scr[b * GLA_CPB + j] = dec[j]
            return carry

        lax.fori_loop(0, GLA_TRIPS, intra, 0)

        def scan(i, st):
            c = jnp.where(d == 0, i, nck - 1 - i)
            st_ref[c] = st
            return st * dec_scr[c][0:1, :] + cs_scr[c]

        lax.fori_loop(0, nck, scan, jnp.zeros((HEAD_DIM, HEAD_DIM), F32), unroll=4)

        def inter(c, carry):
            rows = pl.ds(pl.multiple_of(c * HG_CHUNK, HG_CHUNK), HG_CHUNK)
            opre_ref[rows, :] += _dot(qt_scr[rows, :], st_ref[c], _NT)
            return carry

        lax.fori_loop(0, nck, inter, 0, unroll=4)

        @pl.when(d == 1)
        def _():
            o = opre_ref[...]
            r = lax.rsqrt(jnp.mean(o * o, axis=-1, keepdims=True) + RMS_EPS)
            g = g_ref[...]
            ohg_ref[...] = (o * r * gain_ref[...] * (g * _sigmoid(g))).astype(BF16)

    blk = SEQ * HEAD_DIM * 4
    return pl.pallas_call(
        body, name="gla_fwd", grid=(HG_HEADS, 2),
        in_specs=[_head_spec(CB_HG_Q),
                  pl.BlockSpec((SEQ, HEAD_DIM), lambda h, d: (0, CB_F + 8 * d + h)),
                  _head_spec(CB_HG_I), _head_spec(CB_HG_G),
                  pl.BlockSpec((None, None, 1, HEAD_DIM), lambda h, d: (d, h, 0, 0)),
                  pl.BlockSpec((1, HEAD_DIM), lambda h, d: (0, 0))],
        out_specs=[_head_spec(0), _head_spec(0),
                   pl.BlockSpec((None, None, nck, HEAD_DIM, HEAD_DIM), lambda h, d: (h, d, 0, 0, 0))],
        out_shape=[jax.ShapeDtypeStruct((SEQ, D_MODEL), BF16), jax.ShapeDtypeStruct((SEQ, D_MODEL), F32),
                   jax.ShapeDtypeStruct((HG_HEADS, 2, nck, HEAD_DIM, HEAD_DIM), F32)],
        scratch_shapes=[pltpu.VMEM((SEQ, HEAD_DIM), BF16), pltpu.VMEM((nck, HEAD_DIM, HEAD_DIM), F32),
                        pltpu.VMEM((nck, 8, HEAD_DIM), F32)],
        compiler_params=_params(("parallel", "arbitrary"), 8 * blk + 3 * blk + 2 * blk + 2 * blk + blk // 2),
    )(proj, proj, proj, proj, lb, gain)


def _gla_bwd(proj, lb, gain, o_pre, states, do_hg):
    nck = GLA_NCK
    do, dg, dgain = _gla_bwd_norm(proj, gain, o_pre, do_hg, 256)

    def body(q_ref, f_ref, v_ref, lb_ref, do_ref, st_ref, dq_ref, df_ref, dv_ref, dlb_ref,
             dq_acc, dv_acc, dst_scr, cs_scr, dec_scr):
        d = pl.program_id(1)
        masks = _gla_masks(d)
        mask, maskb, _ = masks
        lbv = lb_ref[...]

        def intra(s, carry):
            blocks = [s + w * GLA_TRIPS for w in range(GLA_WAYS)]
            loaded = [(q_ref[r, :], f_ref[r, :], do_ref[r, :]) for r in map(_gla_block_rows, blocks)]
            results = []
            for qc, fc, doc in loaded:
                ck = _gla_block(qc, fc, lbv, masks)
                cs = [_dot(doc[_gla_chunk_rows(j), :], ck["qt"][_gla_chunk_rows(j), :], _TN) for j in range(GLA_CPB)]
                dec = [jnp.exp(ck["bt"][j * HG_CHUNK:j * HG_CHUNK + 8, :]) for j in range(GLA_CPB)]
                results.append((cs, dec))
            for b, (cs, dec) in zip(blocks, results):
                for j in range(GLA_CPB):
                    cs_scr[b * GLA_CPB + j] = cs[j]
                    dec_scr[b * GLA_CPB + j] = dec[j]
            return carry

        lax.fori_loop(0, GLA_TRIPS, intra, 0)

        def scan(i, dst):
            c = jnp.where(d == 0, nck - 1 - i, i)
            dst_scr[c] = dst
            return dst * dec_scr[c][0:1, :] + cs_scr[c]

        lax.fori_loop(0, nck, scan, jnp.zeros((HEAD_DIM, HEAD_DIM), F32), unroll=4)

        @pl.when(d == 0)
        def _():
            dq_acc[...] = jnp.zeros_like(dq_acc)
            dv_acc[...] = jnp.zeros_like(dv_acc)

        def block_grads(qc, fc, v, doc, states_in, dstates, decays):
            ck = _gla_block(qc, fc, lbv, masks)
            qt, kt, kh, a = ck["qt"], ck["kt"], ck["kh"], ck["a"]
            da = jnp.where(mask, _dot(doc, v, _NT), 0.0)
            dqt_i = _dot(da, kt, _NN)
            dkt = _dot(da, qt, _TN)
            dv_i = _dot(a, doc, _TN)
            dqt_p, dv_p, dkh_p, dbt_p = [], [], [], []
            for j in range(GLA_CPB):
                cr = _gla_chunk_rows(j)
                st_in, dst = states_in[j], dstates[j]
                dqt_p.append(dqt_i[cr, :] + _dot(doc[cr, :], st_in, _NN))
                dv_p.append(dv_i[cr, :] + _dot(kh[cr, :], dst, _NT))
                dkh_j = _dot(v[cr, :], dst, _NN)
                dkh_p.append(dkh_j)
                dbt_j = (decays[j][0:1, :] * jnp.sum(dst * st_in, axis=0, keepdims=True)
                         + jnp.sum(dkh_j * kh[cr, :], axis=0, keepdims=True))
                dbt_p.append(jnp.broadcast_to(dbt_j, (HG_CHUNK, HEAD_DIM)))
            dqt = jnp.concatenate(dqt_p, axis=0)
            dv = jnp.concatenate(dv_p, axis=0)
            dkh = jnp.concatenate(dkh_p, axis=0)
            dbt = jnp.concatenate(dbt_p, axis=0)
            db = dqt * qt - dkt * kt - dkh * kh
            dq = dqt * ck["eb"]
            dk = dkt * jnp.exp(-ck["b"]) + dkh * jnp.exp(ck["bt"] - ck["b"])
            dlogf = _dot_split(maskb, db, _TN) + dbt
            dforget = dlogf / ck["forget"] - dk
            sf, sq = ck["sf"], ck["sq"]
            df = (dforget * (1.0 - lbv) * sf * (1.0 - sf)).astype(BF16)
            dqc = dq * HG_SCALE * (sq + qc * sq * (1.0 - sq))
            return df, dqc, dv, jnp.sum(dforget * (1.0 - sf), axis=0, keepdims=True)

        def grads(s, dlb):
            blocks = [s + w * GLA_GRAD_TRIPS for w in range(GLA_GRAD_WAYS)]
            rows = [_gla_block_rows(b) for b in blocks]
            loaded = []
            for b, r in zip(blocks, rows):
                chunks = [b * GLA_CPB + j for j in range(GLA_CPB)]
                loaded.append((q_ref[r, :], f_ref[r, :], v_ref[r, :], do_ref[r, :], [st_ref[c] for c in chunks],
                               [dst_scr[c] for c in chunks], [dec_scr[c] for c in chunks], dq_acc[r, :], dv_acc[r, :]))
            results = [block_grads(*t[:7]) + (t[7], t[8]) for t in loaded]
            for r, (df, dqc, dv, dlb_part, dq_prev, dv_prev) in zip(rows, results):
                df_ref[r, :] = df
                dq_acc[r, :] = dq_prev + dqc
                dv_acc[r, :] = dv_prev + dv
                dlb = dlb + dlb_part
            return dlb

        dlb_ref[...] = lax.fori_loop(0, GLA_GRAD_TRIPS, grads, jnp.zeros((1, HEAD_DIM), F32))

        @pl.when(d == 1)
        def _():
            dq_ref[...] = dq_acc[...].astype(BF16)
            dv_ref[...] = dv_acc[...].astype(BF16)

    blk = SEQ * HEAD_DIM * 4
    bshape = jax.ShapeDtypeStruct((SEQ, D_MODEL), BF16)
    state_bytes = nck * HEAD_DIM * HEAD_DIM * 4
    dq, df, dv, dlb = pl.pallas_call(
        body, name="gla_bwd", grid=(HG_HEADS, 2),
        in_specs=[_head_spec(CB_HG_Q),
                  pl.BlockSpec((SEQ, HEAD_DIM), lambda h, d: (0, CB_F + 8 * d + h)),
                  _head_spec(CB_HG_I),
                  pl.BlockSpec((None, None, 1, HEAD_DIM), lambda h, d: (d, h, 0, 0)),
                  _head_spec(0),
                  pl.BlockSpec((None, None, nck, HEAD_DIM, HEAD_DIM), lambda h, d: (h, d, 0, 0, 0))],
        out_specs=[_head_spec(0),
                   pl.BlockSpec((None, SEQ, HEAD_DIM), lambda h, d: (d, 0, h)),
                   _head_spec(0),
                   pl.BlockSpec((None, None, 1, HEAD_DIM), lambda h, d: (d, h, 0, 0))],
        out_shape=[bshape, jax.ShapeDtypeStruct((2, SEQ, D_MODEL), BF16), bshape,
                   jax.ShapeDtypeStruct((2, HG_HEADS, 1, HEAD_DIM), F32)],
        scratch_shapes=[pltpu.VMEM((SEQ, HEAD_DIM), F32)] * 2
        + [pltpu.VMEM((nck, HEAD_DIM, HEAD_DIM), F32)] * 2 + [pltpu.VMEM((nck, 8, HEAD_DIM), F32)],
        compiler_params=_params(("parallel", "arbitrary"), 8 * blk + 4 * state_bytes + 3 * blk + 2 * blk),
    )(proj, proj, proj, lb, do, states)
    return dq, df, dv, dg, dlb, dgain


def _gla_bwd_norm(proj, gain, o_pre, do_hg, tr):
    w = D_MODEL

    def body(g_ref, gain_ref, opre_ref, dohg_ref, do_ref, dg_ref, dgain_ref):
        gainv = gain_ref[...]
        part = jnp.zeros((1, HEAD_DIM), F32)
        for h in range(HG_HEADS):
            hs = slice(h * HEAD_DIM, (h + 1) * HEAD_DIM)
            o = opre_ref[:, hs]
            r = lax.rsqrt(jnp.mean(o * o, axis=-1, keepdims=True) + RMS_EPS)
            ohat = o * r
            g = g_ref[:, hs]
            sg = _sigmoid(g)
            silu = g * sg
            dout = dohg_ref[:, hs]
            dg_ref[:, hs] = (dout * ohat * gainv * (sg + silu * (1.0 - sg))).astype(BF16)
            dy = dout * silu
            part = part + jnp.sum(dy * ohat, axis=0, keepdims=True)
            dn = dy * gainv
            do_ref[:, hs] = r * (dn - ohat * jnp.mean(dn * ohat, axis=-1, keepdims=True))

        @pl.when(pl.program_id(0) == 0)
        def _():
            dgain_ref[...] = part

        @pl.when(pl.program_id(0) > 0)
        def _():
            dgain_ref[...] += part

    return pl.pallas_call(
        body, name="gla_bwd_norm", grid=(SEQ // tr,),
        in_specs=[_row_spec(tr, w, CB_HG_G * LANE // w), _bcast_spec(HEAD_DIM), _row_spec(tr, w), _row_spec(tr, w)],
        out_specs=[_row_spec(tr, w), _row_spec(tr, w), _bcast_spec(HEAD_DIM)],
        out_shape=[jax.ShapeDtypeStruct((SEQ, w), F32), jax.ShapeDtypeStruct((SEQ, w), BF16),
                   jax.ShapeDtypeStruct((1, HEAD_DIM), F32)],
        compiler_params=_params(("arbitrary",), 16 * tr * w * 4),
    )(proj, gain, o_pre, do_hg)


def _da_residue_rows(r, n0, size, d):
    if d == 1:
        return pl.ds(pl.multiple_of(n0, 8), size)
    return pl.ds(r + n0 * d, size, stride=d)


def _da_rmsnorm(x, gain):
    r = lax.rsqrt(jnp.mean(x * x, axis=-1, keepdims=True) + RMS_EPS)
    return x * r, r, x * r * gain


def _da_scores(qn_scr, kn_scr, slope, i, ld):
    w0 = jnp.clip(i * DA_QB - DA_RADIUS, 0, ld - DA_WIN)
    w0 = pl.multiple_of(w0, DA_RADIUS)
    qrows = pl.ds(pl.multiple_of(i * DA_QB, DA_QB), DA_QB)
    win = pl.ds(w0, DA_WIN)
    qb = qn_scr[qrows, :]
    kw = kn_scr[win, :]
    s = _dot(qb, kw, _NT) * DA_SCALE
    qpos = i * DA_QB + lax.broadcasted_iota(jnp.int32, (DA_QB, DA_WIN), 0)
    kpos = w0 + lax.broadcasted_iota(jnp.int32, (DA_QB, DA_WIN), 1)
    arel = jnp.abs(kpos - qpos)
    s = s - slope * arel.astype(F32)
    s = jnp.where(arel <= DA_RADIUS, s, NEG_INF)
    return s, qb, kw, qrows, win


def _da_slopes(group):
    d = DA_DILATIONS[group]
    sl = _alibi_slopes(DA_HEADS)[4 * group:4 * group + 4] * d
    return jnp.asarray(np.broadcast_to(sl[:, None, None], (4, 1, LANE)).copy())


def _da_fwd(proj, gq, gk, group):
    d = DA_DILATIONS[group]
    ld = SEQ // d
    nqb = ld // DA_QB

    def body(q_ref, k_ref, v_ref, gq_ref, gk_ref, sl_ref, o_ref, lse_ref, qn_scr, kn_scr, v_scr):
        slope = sl_ref[:, 0:1]

        def residue(r, carry):
            sel = _da_residue_rows(r, 0, ld, d)
            qn_scr[...] = _da_rmsnorm(q_ref[sel, :], gq_ref[...])[2].astype(BF16)
            kn_scr[...] = _da_rmsnorm(k_ref[sel, :], gk_ref[...])[2].astype(BF16)
            v_scr[...] = v_ref[sel, :].astype(BF16)

            def block(i):
                s, _, _, _, win = _da_scores(qn_scr, kn_scr, slope, i, ld)
                m = jnp.max(s, axis=-1, keepdims=True)
                p = jnp.exp(s - m)
                l = jnp.sum(p, axis=-1, keepdims=True)
                return _dot(p, v_scr[win, :], _NN) / l, jnp.broadcast_to(m + jnp.log(l), (DA_QB, HEAD_DIM))

            def step(i, c2):
                blocks = [i + w * (nqb // DA_WAYS) for w in range(DA_WAYS)]
                for b, (o, lse) in zip(blocks, [block(b) for b in blocks]):
                    out = _da_residue_rows(r, b * DA_QB, DA_QB, d)
                    o_ref[out, :] = o
                    lse_ref[out, :] = lse
                return c2

            return lax.fori_loop(0, nqb // DA_WAYS, step, carry)

        lax.fori_loop(0, d, residue, 0)

    seq_spec = lambda base: pl.BlockSpec((SEQ, HEAD_DIM), lambda h: (0, base + 4 * group + h))
    out_spec = pl.BlockSpec((SEQ, HEAD_DIM), lambda h: (0, h))
    gain_spec = pl.BlockSpec((1, HEAD_DIM), lambda h: (0, 0))
    blk = SEQ * HEAD_DIM * 4
    return pl.pallas_call(
        body, name=f"da_fwd_g{group}", grid=(DA_HEADS_PER_GROUP,),
        in_specs=[seq_spec(CB_DA_Q), seq_spec(CB_DA_K), seq_spec(CB_DA_V), gain_spec, gain_spec,
                  pl.BlockSpec((None, 1, LANE), lambda h: (h, 0, 0))],
        out_specs=[out_spec, out_spec],
        out_shape=[jax.ShapeDtypeStruct((SEQ, DA_WIDTH), F32)] * 2,
        scratch_shapes=[pltpu.VMEM((ld, HEAD_DIM), BF16)] * 3,
        compiler_params=_params(("parallel",), 10 * blk + 2 * blk),
    )(proj, proj, proj, gq, gk, _da_slopes(group))


def _da_merge(os, lses, tr):
    w = DA_WIDTH

    def body(o0, o1, o2, l0, l1, l2, ob_ref, of_ref, lse_ref):
        la, lb_, lc = l0[...], l1[...], l2[...]
        m = jnp.maximum(jnp.maximum(la, lb_), lc)
        ea, eb, ec = jnp.exp(la - m), jnp.exp(lb_ - m), jnp.exp(lc - m)
        tot = ea + eb + ec
        o = (ea * o0[...] + eb * o1[...] + ec * o2[...]) / tot
        of_ref[...] = o
        ob_ref[...] = o.astype(BF16)
        lse_ref[...] = m + jnp.log(tot)

    return pl.pallas_call(
        body, name="da_merge", grid=(SEQ // tr,), in_specs=[_row_spec(tr, w)] * 6,
        out_specs=[_row_spec(tr, w)] * 3,
        out_shape=[jax.ShapeDtypeStruct((SEQ, w), BF16), jax.ShapeDtypeStruct((SEQ, w), F32),
                   jax.ShapeDtypeStruct((SEQ, w), F32)],
        compiler_params=_params(("parallel",), 24 * tr * w * 4),
    )(*os, *lses)


def _da_rowdot(do, o, tr):
    w = DA_WIDTH

    def body(do_ref, o_ref, out_ref):
        prod = do_ref[...] * o_ref[...]
        for h in range(w // HEAD_DIM):
            sl = slice(h * HEAD_DIM, (h + 1) * HEAD_DIM)
            out_ref[:, sl] = jnp.broadcast_to(jnp.sum(prod[:, sl], axis=-1, keepdims=True), (tr, HEAD_DIM))

    return pl.pallas_call(
        body, name="da_rowdot", grid=(SEQ // tr,), in_specs=[_row_spec(tr, w)] * 2,
        out_specs=_row_spec(tr, w), out_shape=jax.ShapeDtypeStruct((SEQ, w), F32),
        compiler_params=_params(("parallel",), 10 * tr * w * 4),
    )(do, o)


def _da_bwd(proj, gq, gk, do, lse, dd, group):
    d = DA_DILATIONS[group]
    ld = SEQ // d
    nqb = ld // DA_QB

    def body(q_ref, k_ref, v_ref, gq_ref, gk_ref, sl_ref, do_ref, lse_ref, dd_ref,
             dq_ref, dk_ref, dv_ref, dgq_ref, dgk_ref,
             qn_scr, kn_scr, v_scr, dqn_scr, dkn_scr, dvr_scr, dq_scr, dk_scr, dv_scr):
        gqv, gkv = gq_ref[...], gk_ref[...]
        slope = sl_ref[:, 0:1]

        def residue(r, carry):
            sel = _da_residue_rows(r, 0, ld, d)
            qn_scr[...] = _da_rmsnorm(q_ref[sel, :], gqv)[2].astype(BF16)
            kn_scr[...] = _da_rmsnorm(k_ref[sel, :], gkv)[2].astype(BF16)
            v_scr[...] = v_ref[sel, :].astype(BF16)
            dkn_scr[...] = jnp.zeros_like(dkn_scr)
            dvr_scr[...] = jnp.zeros_like(dvr_scr)

            def block(i):
                s, qb, kw, qrows, win = _da_scores(qn_scr, kn_scr, slope, i, ld)
                src = _da_residue_rows(r, i * DA_QB, DA_QB, d)
                p = jnp.exp(s - lse_ref[src, :][:, 0:1])
                dob = do_ref[src, :]
                dp = _dot(dob, v_scr[win, :], _NT)
                ds = p * (dp - dd_ref[src, :][:, 0:1]) * DA_SCALE
                return qrows, win, _dot(p, dob, _TN), _dot(ds, kw, _NN), _dot(ds, qb, _TN)

            def step(i, c2):
                blocks = [i + w * (nqb // DA_WAYS) for w in range(DA_WAYS)]
                for qrows, win, dv, dqn, dkn in [block(b) for b in blocks]:
                    dvr_scr[win, :] += dv
                    dqn_scr[qrows, :] = dqn
                    dkn_scr[win, :] += dkn
                return c2

            lax.fori_loop(0, nqb // DA_WAYS, step, 0)

            parts = []
            for x_ref, gv, dn_scr, dx_scr in ((q_ref, gqv, dqn_scr, dq_scr), (k_ref, gkv, dkn_scr, dk_scr)):
                hat, rstd, _ = _da_rmsnorm(x_ref[sel, :], gv)
                dn = dn_scr[...]
                dyg = dn * gv
                dx_scr[sel, :] = rstd * (dyg - hat * jnp.mean(dyg * hat, axis=-1, keepdims=True))
                parts.append(jnp.sum(dn * hat, axis=0, keepdims=True))
            dv_scr[sel, :] = dvr_scr[...]
            return carry[0] + parts[0], carry[1] + parts[1]

        zero = jnp.zeros((1, HEAD_DIM), F32)
        pq, pk = lax.fori_loop(0, d, residue, (zero, zero))

        @pl.when(pl.program_id(0) == 0)
        def _():
            dgq_ref[...] = pq
            dgk_ref[...] = pk

        @pl.when(pl.program_id(0) > 0)
        def _():
            dgq_ref[...] += pq
            dgk_ref[...] += pk

        dq_ref[...] = dq_scr[...].astype(BF16)
        dk_ref[...] = dk_scr[...].astype(BF16)
        dv_ref[...] = dv_scr[...].astype(BF16)

    seq_spec = lambda base: pl.BlockSpec((SEQ, HEAD_DIM), lambda h: (0, base + 4 * group + h))
    out_spec = pl.BlockSpec((SEQ, HEAD_DIM), lambda h: (0, h))
    gain_spec = pl.BlockSpec((1, HEAD_DIM), lambda h: (0, 0))
    oshape = jax.ShapeDtypeStruct((SEQ, DA_WIDTH), BF16)
    gshape = jax.ShapeDtypeStruct((1, HEAD_DIM), F32)
    blk = SEQ * HEAD_DIM * 4
    return pl.pallas_call(
        body, name=f"da_bwd_g{group}", grid=(DA_HEADS_PER_GROUP,),
        in_specs=[seq_spec(CB_DA_Q), seq_spec(CB_DA_K), seq_spec(CB_DA_V), gain_spec, gain_spec,
                  pl.BlockSpec((None, 1, LANE), lambda h: (h, 0, 0)), out_spec, out_spec, out_spec],
        out_specs=[out_spec, out_spec, out_spec, gain_spec, gain_spec],
        out_shape=[oshape, oshape, oshape, gshape, gshape],
        scratch_shapes=[pltpu.VMEM((ld, HEAD_DIM), BF16)] * 3 + [pltpu.VMEM((ld, HEAD_DIM), F32)] * 3
        + [pltpu.VMEM((SEQ, HEAD_DIM), F32)] * 3,
        compiler_params=_params(("arbitrary",), 12 * blk + 3 * blk + 3 * blk + 3 * blk),
    )(proj, proj, proj, gq, gk, _da_slopes(group), do, lse, dd)


def _mem_softmax(q, k, gq, gk):
    qhat, rq, qn = _da_rmsnorm(q, gq)
    khat, rk, kn = _da_rmsnorm(k, gk)
    s = _dot(qn, kn, _NT) * MEM_SCALE
    m = jnp.max(s, axis=-1, keepdims=True)
    e = jnp.exp(s - m)
    p = e / jnp.sum(e, axis=-1, keepdims=True)
    return p, (qhat, rq, qn), (khat, rk, kn)


def _mem_fwd(proj, kv, gq, gk, tq):
    def body(q_ref, k_ref, v_ref, gq_ref, gk_ref, o_ref):
        p, _, _ = _mem_softmax(q_ref[...], k_ref[...], gq_ref[...], gk_ref[...])
        o_ref[...] = _dot(p, v_ref[...], _NN).astype(BF16)

    gain_spec = pl.BlockSpec((1, HEAD_DIM), lambda h, i: (0, 0))
    return pl.pallas_call(
        body, name="mem_fwd", grid=(MEM_HEADS, SEQ // tq),
        in_specs=[pl.BlockSpec((tq, HEAD_DIM), lambda h, i: (i, CB_MEM_Q + h)),
                  pl.BlockSpec((N_MEM, HEAD_DIM), lambda h, i: (0, h)),
                  pl.BlockSpec((N_MEM, HEAD_DIM), lambda h, i: (0, MEM_HEADS + h)), gain_spec, gain_spec],
        out_specs=pl.BlockSpec((tq, HEAD_DIM), lambda h, i: (i, h)),
        out_shape=jax.ShapeDtypeStruct((SEQ, MEM_WIDTH), BF16),
        compiler_params=_params(("parallel", "parallel"), 16 * tq * N_MEM * 4),
    )(proj, kv, kv, gq, gk)


def _mem_bwd(proj, kv, gq, gk, do, tq):
    nq = SEQ // tq

    def body(q_ref, k_ref, v_ref, gq_ref, gk_ref, do_ref, dq_ref, dk_ref, dv_ref, dgq_ref, dgk_ref, dkn_scr):
        h, i = pl.program_id(0), pl.program_id(1)
        gqv, gkv = gq_ref[...], gk_ref[...]
        p, (qhat, rq, qn), (khat, rk, kn) = _mem_softmax(q_ref[...], k_ref[...], gqv, gkv)
        dob = do_ref[...]
        dvp = _dot(p, dob, _TN)
        dp = _dot(dob, v_ref[...], _NT)
        ds = p * (dp - jnp.sum(p * dp, axis=-1, keepdims=True)) * MEM_SCALE
        dqn = _dot(ds, kn, _NN)
        dknp = _dot(ds, qn, _TN)
        dyg = dqn * gqv
        dq_ref[...] = (rq * (dyg - qhat * jnp.mean(dyg * qhat, axis=-1, keepdims=True))).astype(BF16)
        dgq_part = jnp.sum(dqn * qhat, axis=0, keepdims=True)
        first = jnp.logical_and(h == 0, i == 0)

        @pl.when(first)
        def _():
            dgq_ref[...] = dgq_part

        @pl.when(jnp.logical_not(first))
        def _():
            dgq_ref[...] += dgq_part

        @pl.when(i == 0)
        def _():
            dv_ref[...] = dvp
            dkn_scr[...] = dknp

        @pl.when(i > 0)
        def _():
            dv_ref[...] += dvp
            dkn_scr[...] += dknp

        @pl.when(i == nq - 1)
        def _():
            dkn = dkn_scr[...]
            dkg = dkn * gkv
            dk_ref[...] = rk * (dkg - khat * jnp.mean(dkg * khat, axis=-1, keepdims=True))
            dgk_part = jnp.sum(dkn * khat, axis=0, keepdims=True)

            @pl.when(h == 0)
            def _():
                dgk_ref[...] = dgk_part

            @pl.when(h > 0)
            def _():
                dgk_ref[...] += dgk_part

    gain_spec = pl.BlockSpec((1, HEAD_DIM), lambda h, i: (0, 0))
    kvout = pl.BlockSpec((N_MEM, HEAD_DIM), lambda h, i: (0, h))
    return pl.pallas_call(
        body, name="mem_bwd", grid=(MEM_HEADS, nq),
        in_specs=[pl.BlockSpec((tq, HEAD_DIM), lambda h, i: (i, CB_MEM_Q + h)),
                  pl.BlockSpec((N_MEM, HEAD_DIM), lambda h, i: (0, h)),
                  pl.BlockSpec((N_MEM, HEAD_DIM), lambda h, i: (0, MEM_HEADS + h)), gain_spec, gain_spec,
                  pl.BlockSpec((tq, HEAD_DIM), lambda h, i: (i, h))],
        out_specs=[pl.BlockSpec((tq, HEAD_DIM), lambda h, i: (i, h)), kvout, kvout, gain_spec, gain_spec],
        out_shape=[jax.ShapeDtypeStruct((SEQ, MEM_WIDTH), BF16), jax.ShapeDtypeStruct((N_MEM, MEM_WIDTH), F32),
                   jax.ShapeDtypeStruct((N_MEM, MEM_WIDTH), F32), jax.ShapeDtypeStruct((1, HEAD_DIM), F32),
                   jax.ShapeDtypeStruct((1, HEAD_DIM), F32)],
        scratch_shapes=[pltpu.VMEM((N_MEM, HEAD_DIM), F32)],
        compiler_params=_params(("arbitrary", "arbitrary"), 24 * tq * N_MEM * 4),
    )(proj, kv, kv, gq, gk, do)


def _mesh_position():
    return lax.axis_index("x"), lax.axis_index("y"), lax.axis_index("c")


def _any_spec():
    return pl.BlockSpec(memory_space=pl.ANY)


def _all_gather(shards):
    n = len(shards)

    def body(*refs):
        ins, outs = refs[:n], refs[n:2 * n]
        send_sems, recv_sems, local_sems = refs[2 * n:]
        x, y, c = _mesh_position()
        me, sibling = (x, y, c), (x, y, 1 - c)
        chips = [(1 - x, y), (x, 1 - y), (1 - x, 1 - y)]

        def copy(w, k, block, to, src=None):
            px, py, pc = block
            rows = outs[w].at[4 * px + 2 * py + pc]
            return pltpu.make_async_remote_copy(
                src_ref=rows if src is None else src, dst_ref=rows,
                send_sem=send_sems.at[7 * w + k], recv_sem=recv_sems.at[7 * w + k],
                device_id=to, device_id_type=MESH)

        started = []
        for w in range(n):
            mine = pltpu.make_async_copy(ins[w], outs[w].at[4 * x + 2 * y + c], local_sems.at[w])
            mine.start()
            started.append(mine)
        sends = []
        for w in range(n):
            first = [copy(w, 0, me, sibling, src=ins[w])]
            first += [copy(w, 1 + j, me, (*chip, c), src=ins[w]) for j, chip in enumerate(chips)]
            for cp in first:
                cp.start()
            sends += first
        for w in range(n):
            for j, chip in enumerate(chips):
                copy(w, 1 + j, (*chip, c), me).wait_recv()
                passed = copy(w, 4 + j, (*chip, c), sibling)
                passed.start()
                sends.append(passed)
        for w in range(n):
            copy(w, 0, sibling, me).wait_recv()
            for j, chip in enumerate(chips):
                copy(w, 4 + j, (*chip, 1 - c), me).wait_recv()
        for cp in sends:
            cp.wait_send()
        for mine in started:
            mine.wait()

    return pl.pallas_call(
        body, name="weights_all_gather",
        in_specs=[_any_spec()] * n, out_specs=[_any_spec()] * n,
        out_shape=[jax.ShapeDtypeStruct((N_DEV,) + s.shape, s.dtype) for s in shards],
        scratch_shapes=[pltpu.SemaphoreType.DMA((7 * n,)), pltpu.SemaphoreType.DMA((7 * n,)),
                        pltpu.SemaphoreType.DMA((n,))],
    )(*shards)


def _chip_of(j, x, y):
    return (1 - x if j & 1 else x, 1 - y if j & 2 else y)


_HBM_SPEC = pl.BlockSpec(memory_space=pltpu.HBM)
_SEM_SPEC = pl.BlockSpec(memory_space=pltpu.SEMAPHORE)
_DATAFLOW_EFFECT = pltpu.SideEffectType.DATAFLOW_SIDE_EFFECTING
TOKEN_SHAPE = (8, D_MODEL)


def _copies_start(name, arrays, n_copies, plan):
    n = len(arrays)

    def body(*refs):
        send_sems, recv_sems, token = refs[n], refs[n + 1], refs[2 * n + 2]
        copies = plan(refs[:n])
        assert len(copies) == n_copies
        for k, (src, dst, dev) in enumerate(copies):
            pltpu.make_async_remote_copy(src_ref=src, dst_ref=dst, send_sem=send_sems.at[k], recv_sem=recv_sems.at[k],
                                         device_id=dev, device_id_type=MESH).start()
        token[...] = jnp.zeros_like(token)

    outs = pl.pallas_call(
        body, name=name,
        out_shape=(pltpu.SemaphoreType.DMA((n_copies,)), pltpu.SemaphoreType.DMA((n_copies,)),
                   *[pltpu.HBM(a.shape, a.dtype) for a in arrays], jax.ShapeDtypeStruct(TOKEN_SHAPE, F32)),
        in_specs=[_HBM_SPEC] * n,
        out_specs=(_SEM_SPEC, _SEM_SPEC, *[_HBM_SPEC] * n, pl.BlockSpec(memory_space=pltpu.VMEM)),
        input_output_aliases={i: i + 2 for i in range(n)},
        compiler_params=pltpu.CompilerParams(has_side_effects=_DATAFLOW_EFFECT),
    )(*[pltpu.with_memory_space_constraint(a, pltpu.HBM) for a in arrays])
    return outs[0], outs[1], list(outs[2:2 + n]), outs[2 + n]


def _copies_wait(name, send_sems, recv_sems, arrays, n_copies, plan, after):
    n = len(arrays)
    after = list(after) if isinstance(after, (list, tuple)) else [after]

    def body(*refs):
        send_ref, recv_ref = refs[n], refs[n + 1]
        copies = plan(refs[:n])
        assert len(copies) == n_copies
        for k, (src, dst, dev) in enumerate(copies):
            cp = pltpu.make_async_remote_copy(src_ref=src, dst_ref=dst, send_sem=send_ref.at[k], recv_sem=recv_ref.at[k],
                                              device_id=dev, device_id_type=MESH)
            cp.wait_send()
            cp.wait_recv()

    outs = pl.pallas_call(
        body, name=name, out_shape=tuple(pltpu.HBM(a.shape, a.dtype) for a in arrays),
        in_specs=[_HBM_SPEC] * n + [_SEM_SPEC, _SEM_SPEC] + [pl.BlockSpec(memory_space=pl.ANY)] * len(after),
        out_specs=tuple([_HBM_SPEC] * n), input_output_aliases={i: i for i in range(n)},
        compiler_params=pltpu.CompilerParams(has_side_effects=_DATAFLOW_EFFECT),
    )(*arrays, send_sems, recv_sems, *after)
    return list(outs)


def _after(small, token):
    return small if token is None else small + token[0:1, :small.shape[-1]]


def _gather_plan_out(n):
    def plan(refs):
        x, y, c = _mesh_position()
        me = 4 * x + 2 * y + c
        copies = []
        for w in range(n):
            land = refs[n + w].at[me]
            copies.append((refs[w], land, (x, y, 1 - c)))
            for j in range(1, 4):
                copies.append((refs[w], land, (*_chip_of(j, x, y), c)))
        return copies
    return plan


def _gather_plan_pass(n):
    def plan(refs):
        x, y, c = _mesh_position()
        copies = []
        for w in range(n):
            for j in range(1, 4):
                px, py = _chip_of(j, x, y)
                rows = refs[w].at[4 * px + 2 * py + c]
                copies.append((rows, rows, (x, y, 1 - c)))
        return copies
    return plan


def _reduce_plan_sibling(n):
    def plan(refs):
        x, y, c = _mesh_position()
        copies = []
        for w in range(n):
            for j in range(4):
                px, py = _chip_of(j, x, y)
                copies.append((refs[w].at[4 * px + 2 * py + (1 - c)], refs[n + w].at[j], (x, y, 1 - c)))
        return copies
    return plan


def _reduce_plan_chips(n):
    def plan(refs):
        x, y, c = _mesh_position()
        copies = []
        for w in range(n):
            for j in range(1, 4):
                copies.append((refs[w].at[j - 1], refs[n + w].at[j - 1], (*_chip_of(j, x, y), c)))
        return copies
    return plan


def _chip_partials(grad, recv, name, tr):
    _, rows, width = grad.shape

    def body(g_ref, r_ref, own_ref, other_ref):
        x, y, c = _mesh_position()
        for j in range(4):
            px, py = _chip_of(j, x, y)
            total = g_ref[4 * px + 2 * py + c].astype(F32) + r_ref[j].astype(F32)
            if j == 0:
                own_ref[...] = total
            else:
                other_ref[j - 1] = total.astype(BF16)

    return pl.pallas_call(
        body, name=name, grid=(rows // tr,),
        in_specs=[pl.BlockSpec((N_DEV, tr, width), lambda i: (0, i, 0)),
                  pl.BlockSpec((4, tr, width), lambda i: (0, i, 0))],
        out_specs=[pl.BlockSpec((tr, width), lambda i: (i, 0)), pl.BlockSpec((3, tr, width), lambda i: (0, i, 0))],
        out_shape=[jax.ShapeDtypeStruct((rows, width), F32), jax.ShapeDtypeStruct((3, rows, width), BF16)],
        compiler_params=_params(("parallel",), 2 * 20 * tr * width * 2 + 8 * tr * width * 4),
    )(grad, recv)


def _adamw_math(w, g, m, v):
    m = ADAM_B1 * m + (1.0 - ADAM_B1) * g
    v = ADAM_B2 * v + (1.0 - ADAM_B2) * (g * g)
    m_hat = m / (1.0 - ADAM_B1 ** ADAM_STEP)
    v_hat = v / (1.0 - ADAM_B2 ** ADAM_STEP)
    delta = -ADAM_LR * (m_hat / (jnp.sqrt(v_hat) + ADAM_EPS) + ADAM_WD * w)
    return delta, m, v


def _adamw_shard(own, recv, w, m, v, name, tr):
    rows, width = own.shape

    def body(own_ref, r_ref, w_ref, m_ref, v_ref, g_ref, d_ref, nm_ref, nv_ref):
        g = own_ref[...]
        for j in range(3):
            g = g + r_ref[j].astype(F32)
        g_ref[...] = g
        d_ref[...], nm_ref[...], nv_ref[...] = _adamw_math(w_ref[...], g, m_ref[...], v_ref[...])

    spec = pl.BlockSpec((tr, width), lambda i: (i, 0))
    shape = jax.ShapeDtypeStruct((rows, width), F32)
    return pl.pallas_call(
        body, name=name, grid=(rows // tr,),
        in_specs=[spec, pl.BlockSpec((3, tr, width), lambda i: (0, i, 0)), spec, spec, spec],
        out_specs=[spec] * 4, out_shape=[shape] * 4,
        compiler_params=_params(("parallel",), 22 * tr * width * 4),
    )(own, recv, w, m, v)


def _small_all_reduce_adamw(gpart, lbpack, wpack, mpack, vpack, after):
    def body(gp_ref, lb_ref, w_ref, m_ref, v_ref, after_ref, g_ref, d_ref, nm_ref, nv_ref, gath_ref, send_sems, recv_sems):
        del after_ref
        x, y, c = _mesh_position()
        me = 4 * x + 2 * y + c
        gath_ref[me] = gp_ref[...]
        copies = []
        for j in range(1, N_DEV):
            peer = (x ^ (j >> 2), y ^ ((j >> 1) & 1), c ^ (j & 1))
            cp = pltpu.make_async_remote_copy(
                src_ref=gp_ref, dst_ref=gath_ref.at[me], send_sem=send_sems.at[j - 1], recv_sem=recv_sems.at[j - 1],
                device_id=peer, device_id_type=MESH)
            cp.start()
            copies.append(cp)
        for cp in copies:
            cp.wait()
        tot = gath_ref[0]
        for s in range(1, N_DEV):
            tot = tot + gath_ref[s]
        lb = lb_ref[...]
        dl = tot[16:32, :] * lb * (1.0 - lb)
        g = jnp.concatenate([tot[0:16, :], dl[0:8, :], -dl[0:8, :], dl[8:16, :], -dl[8:16, :],
                             tot[32:SMALL_GRAD_ROWS, :]], axis=0)
        g_ref[...] = g
        d_ref[...], nm_ref[...], nv_ref[...] = _adamw_math(w_ref[...], g, m_ref[...], v_ref[...])

    vm = pl.BlockSpec(memory_space=pltpu.VMEM)
    shape = jax.ShapeDtypeStruct((SMALL_ROWS, LANE), F32)
    return pl.pallas_call(
        body, name="small_all_reduce_adamw", in_specs=[vm] * 5 + [_any_spec()], out_specs=[vm] * 4,
        out_shape=[shape] * 4,
        scratch_shapes=[pltpu.VMEM((N_DEV, SMALL_GRAD_ROWS, LANE), F32),
                        pltpu.SemaphoreType.DMA((N_DEV - 1,)), pltpu.SemaphoreType.DMA((N_DEV - 1,))],
    )(gpart, lbpack, wpack, mpack, vpack, after)


_SMALL_NAMES = ("norm_mix_gain", "norm_mem_gain", "lb_logits_fw", "lb_logits_bw", "norm_ffn_gain",
                "hg_norm_gain", "da_q_gain", "da_k_gain", "mem_q_gain", "mem_k_gain")
_SMALL_ROW0 = {"norm_mix_gain": 0, "norm_mem_gain": 8, "lb_logits_fw": 16, "lb_logits_bw": 32, "norm_ffn_gain": 48,
               "hg_norm_gain": 56, "da_q_gain": 64, "da_k_gain": 72, "mem_q_gain": 80, "mem_k_gain": 88}
_LOSS_ROW = 96


def _pack_rows(parts, total_rows):
    rows = []
    for p in parts:
        r = p.reshape(-1, LANE)
        rows.append(jnp.pad(r, ((0, -r.shape[0] % 8), (0, 0))))
    used = sum(r.shape[0] for r in rows)
    if total_rows > used:
        rows.append(jnp.zeros((total_rows - used, LANE), F32))
    return jnp.concatenate(rows, axis=0)


def _unpack_small(pack, like):
    out = {}
    for name in _SMALL_NAMES:
        n = like[name].size // LANE
        r0 = _SMALL_ROW0[name]
        out[name] = pack[r0:r0 + n].reshape(like[name].shape)
    return out


def kernel(x, mem, norm_mix_gain, norm_mem_gain, w_in, lb_logits_fw, lb_logits_bw, hg_norm_gain, da_q_gain, da_k_gain, w_mem_kv, mem_q_gain, mem_k_gain, w_proj_hg, w_proj_da, w_proj_mem, w_out, norm_ffn_gain, w_ffn_in, w_ffn_out, loss_target, m_norm_mix_gain, m_norm_mem_gain, m_w_in, m_lb_logits_fw, m_lb_logits_bw, m_hg_norm_gain, m_da_q_gain, m_da_k_gain, m_w_mem_kv, m_mem_q_gain, m_mem_k_gain, m_w_proj_hg, m_w_proj_da, m_w_proj_mem, m_w_out, m_norm_ffn_gain, m_w_ffn_in, m_w_ffn_out, v_norm_mix_gain, v_norm_mem_gain, v_w_in, v_lb_logits_fw, v_lb_logits_bw, v_hg_norm_gain, v_da_q_gain, v_da_k_gain, v_w_mem_kv, v_mem_q_gain, v_mem_k_gain, v_w_proj_hg, v_w_proj_da, v_w_proj_mem, v_w_out, v_norm_ffn_gain, v_w_ffn_in, v_w_ffn_out):
    small_w = dict(norm_mix_gain=norm_mix_gain, norm_mem_gain=norm_mem_gain, lb_logits_fw=lb_logits_fw,
                   lb_logits_bw=lb_logits_bw, norm_ffn_gain=norm_ffn_gain, hg_norm_gain=hg_norm_gain,
                   da_q_gain=da_q_gain, da_k_gain=da_k_gain, mem_q_gain=mem_q_gain, mem_k_gain=mem_k_gain)
    small_m = dict(norm_mix_gain=m_norm_mix_gain, norm_mem_gain=m_norm_mem_gain, lb_logits_fw=m_lb_logits_fw,
                   lb_logits_bw=m_lb_logits_bw, norm_ffn_gain=m_norm_ffn_gain, hg_norm_gain=m_hg_norm_gain,
                   da_q_gain=m_da_q_gain, da_k_gain=m_da_k_gain, mem_q_gain=m_mem_q_gain, mem_k_gain=m_mem_k_gain)
    small_v = dict(norm_mix_gain=v_norm_mix_gain, norm_mem_gain=v_norm_mem_gain, lb_logits_fw=v_lb_logits_fw,
                   lb_logits_bw=v_lb_logits_bw, norm_ffn_gain=v_norm_ffn_gain, hg_norm_gain=v_hg_norm_gain,
                   da_q_gain=v_da_q_gain, da_k_gain=v_da_k_gain, mem_q_gain=v_mem_q_gain, mem_k_gain=v_mem_k_gain)
    big_names = ("w_in", "w_mem_kv", "w_proj_hg", "w_proj_da", "w_proj_mem", "w_out", "w_ffn_in", "w_ffn_out")
    big_w = dict(w_in=w_in[0], w_mem_kv=w_mem_kv[0], w_proj_hg=w_proj_hg[0], w_proj_da=w_proj_da[0],
                 w_proj_mem=w_proj_mem[0], w_out=w_out[0], w_ffn_in=w_ffn_in[0], w_ffn_out=w_ffn_out[0])
    big_m = dict(w_in=m_w_in[0], w_mem_kv=m_w_mem_kv[0], w_proj_hg=m_w_proj_hg[0], w_proj_da=m_w_proj_da[0],
                 w_proj_mem=m_w_proj_mem[0], w_out=m_w_out[0], w_ffn_in=m_w_ffn_in[0], w_ffn_out=m_w_ffn_out[0])
    big_v = dict(w_in=v_w_in[0], w_mem_kv=v_w_mem_kv[0], w_proj_hg=v_w_proj_hg[0], w_proj_da=v_w_proj_da[0],
                 w_proj_mem=v_w_proj_mem[0], w_out=v_w_out[0], w_ffn_in=v_w_ffn_in[0], w_ffn_out=v_w_ffn_out[0])

    row_tile = dict(w_in=128, w_mem_kv=128, w_proj_hg=128, w_proj_da=512, w_proj_mem=512, w_out=128,
                    w_ffn_in=128, w_ffn_out=352)
    rest = _BIG_NAMES[1:]
    shards = [big_w[n].astype(BF16) for n in rest]
    state = {}
    big_out = {}

    def reduce_start(group, stacked):
        names = tuple(stacked)
        arrays = [stacked[n] for n in names] + [lax.empty((4,) + stacked[n].shape[1:], BF16) for n in names]
        plan = _reduce_plan_sibling(len(names))
        send, recv, thru, token = _copies_start(f"grads_{group}_sibling_start", arrays, 4 * len(names), plan)
        state[group] = dict(names=names, plan=plan, send=send, recv=recv, arrays=thru)
        return token

    def reduce_middle(group, after):
        st = state[group]
        names, k = st["names"], len(st["names"])
        thru = _copies_wait(f"grads_{group}_sibling_wait", st["send"], st["recv"], st["arrays"], 4 * k, st["plan"], after)
        partials = [_chip_partials(thru[i], thru[k + i], f"chip_partials_{n}", row_tile[n]) for i, n in enumerate(names)]
        arrays = [p[1] for p in partials] + [lax.empty(p[1].shape, BF16) for p in partials]
        plan = _reduce_plan_chips(k)
        send, recv, thru2, token = _copies_start(f"grads_{group}_chips_start", arrays, 3 * k, plan)
        state[group] = dict(names=names, plan=plan, send=send, recv=recv, arrays=thru2, own=[p[0] for p in partials])
        return token

    def reduce_finish(group, after):
        st = state.pop(group)
        names, k = st["names"], len(st["names"])
        thru = _copies_wait(f"grads_{group}_chips_wait", st["send"], st["recv"], st["arrays"], 3 * k, st["plan"], after)
        for i, n in enumerate(names):
            big_out[n] = _adamw_shard(st["own"][i], thru[k + i], big_w[n], big_m[n], big_v[n], "adamw_" + n, row_tile[n])

    win_st = _all_gather([big_w["w_in"].astype(BF16)])[0]
    step = _local_step_stages(x[0], mem[0], loss_target[0], small_w, win_st)
    event, payload = next(step)
    local = None
    while True:
        reply = None
        if event == "begin":
            nr = len(rest)
            me = 4 * lax.axis_index("x") + 2 * lax.axis_index("y") + lax.axis_index("c")
            arrays = shards + [lax.dynamic_update_slice(lax.empty((N_DEV,) + s.shape, BF16), s[None], (me, 0, 0))
                               for s in shards]
            plan = _gather_plan_out(nr)
            send, recv, thru, reply = _copies_start("weights_rest_out_start", arrays, 4 * nr, plan)
            state["gather"] = dict(plan=plan, send=send, recv=recv, arrays=thru)
        elif event == "after_gla_fwd":
            st = state["gather"]
            nr = len(rest)
            thru = _copies_wait("weights_rest_out_wait", st["send"], st["recv"], st["arrays"], 4 * nr, st["plan"], payload)
            plan = _gather_plan_pass(nr)
            send, recv, lands, reply = _copies_start("weights_rest_pass_start", thru[nr:], 3 * nr, plan)
            state["gather"] = dict(plan=plan, send=send, recv=recv, arrays=lands)
        elif event == "need_weights":
            st = state.pop("gather")
            nr = len(rest)
            lands = _copies_wait("weights_rest_pass_wait", st["send"], st["recv"], st["arrays"], 3 * nr, st["plan"], payload)
            reply = dict(zip(rest, lands))
        elif event == "grads_ffn":
            reply = reduce_start("ffn", payload)
        elif event == "after_gate_merge_bwd":
            reply = reduce_middle("ffn", payload)
        elif event == "grads_mix":
            reply = reduce_start("mix", payload)
        elif event == "after_da_bwd_g0":
            reply = reduce_middle("mix", payload)
            reduce_finish("ffn", payload)
        elif event == "after_gla_bwd":
            reduce_finish("mix", payload)
        elif event == "grads_in":
            reply = reduce_start("in", payload)
        elif event == "after_proj_bwd_act":
            reply = reduce_middle("in", payload)
        elif event == "end":
            local = payload
            reduce_finish("in", [local["grad_x"]] + [big_out[n][0] for n in rest])
            break
        event, payload = step.send(reply)
    grad_x = local["grad_x"]

    wpack = _pack_rows([small_w[n] for n in _SMALL_NAMES], SMALL_ROWS)
    mpack = _pack_rows([small_m[n] for n in _SMALL_NAMES], SMALL_ROWS)
    vpack = _pack_rows([small_v[n] for n in _SMALL_NAMES], SMALL_ROWS)
    gs, ds, ms, vs = _small_all_reduce_adamw(local["gpart"], local["lbpack"], wpack, mpack, vpack, big_out["w_in"][0])
    loss = gs[_LOSS_ROW, 0]
    small_out = [_unpack_small(t, small_w) for t in (gs, ds, ms, vs)]

    order = ("norm_mix_gain", "norm_mem_gain", "w_in", "lb_logits_fw", "lb_logits_bw", "hg_norm_gain", "da_q_gain",
             "da_k_gain", "w_mem_kv", "mem_q_gain", "mem_k_gain", "w_proj_hg", "w_proj_da", "w_proj_mem", "w_out",
             "norm_ffn_gain", "w_ffn_in", "w_ffn_out")
    outs = [loss, grad_x[None]]
    for kind in range(4):
        for n in order:
            outs.append(big_out[n][kind][None] if n in big_out else small_out[kind][n])
    return tuple(outs)


_BIG_NAMES = ("w_in", "w_mem_kv", "w_proj_hg", "w_proj_da", "w_proj_mem", "w_out", "w_ffn_in", "w_ffn_out")


def _local_step(xs, mems, target, sw, wg):
    step = _local_step_stages(xs, mems, target, sw, wg["w_in"])
    stacked = {}
    event, payload = next(step)
    while event != "end":
        if event.startswith("grads_"):
            stacked.update(payload)
        event, payload = step.send(wg if event == "need_weights" else None)
    return dict(payload, stacked=stacked)


def _local_step_stages(xs, mems, target, sw, win_st):
    norm_mix_gain, norm_mem_gain, norm_ffn_gain = sw["norm_mix_gain"], sw["norm_mem_gain"], sw["norm_ffn_gain"]
    lb_logits_fw, lb_logits_bw, hg_norm_gain = sw["lb_logits_fw"], sw["lb_logits_bw"], sw["hg_norm_gain"]
    da_q_gain, da_k_gain, mem_q_gain, mem_k_gain = sw["da_q_gain"], sw["da_k_gain"], sw["mem_q_gain"], sw["mem_k_gain"]

    token = yield "begin", None
    lb_fw = _lb_table(lb_logits_fw, "lb_table_fw")
    lb_bw = _lb_table(lb_logits_bw, "lb_table_bw")
    lb = jnp.concatenate([lb_fw, lb_bw], axis=0).reshape(2, HG_HEADS, 1, HEAD_DIM)
    h, h_t = _rmsnorm_fwd(xs, _after(norm_mix_gain, token), "norm_mix_fwd", 512, transposed=True)
    proj = _matmul(h, win_st, "nn", F32, 1024, IN_SHARD, D_MODEL, "proj_fwd", b_stacked=True, n_outer=True)
    o_hg, o_pre, states = _gla_fwd(proj, lb, hg_norm_gain)
    token = yield "after_gla_fwd", o_hg
    da = [_da_fwd(proj, _after(da_q_gain, token), da_k_gain, g) for g in range(3)]
    o_da, o_da32, lse_da = _da_merge([t[0] for t in da], [t[1] for t in da], 512)
    wg = yield "need_weights", o_da
    wkv = wg["w_mem_kv"].reshape(D_MODEL, 2 * MEM_WIDTH)
    wphg = wg["w_proj_hg"].reshape(D_MODEL, D_MODEL)
    wpda = jnp.transpose(wg["w_proj_da"], (1, 0, 2)).reshape(DA_WIDTH, D_MODEL)
    wpmem = jnp.transpose(wg["w_proj_mem"], (1, 0, 2)).reshape(MEM_WIDTH, D_MODEL)
    wout = wg["w_out"].reshape(D_MODEL, D_MODEL)
    wfin = jnp.transpose(wg["w_ffn_in"], (1, 0, 2)).reshape(D_MODEL, 2 * D_FF)
    wfout = wg["w_ffn_out"].reshape(D_FF, D_MODEL)
    mem_n = _rmsnorm_fwd(mems, norm_mem_gain, "norm_mem_fwd", N_MEM)
    kv = _matmul(mem_n, wkv, "nn", F32, N_MEM, 1024, D_MODEL, "mem_kv_fwd")
    o_mem = _mem_fwd(proj, kv, mem_q_gain, mem_k_gain, 512)
    t_hg = _matmul(o_hg, wphg, "nn", F32, 512, 1024, D_MODEL, "proj_hg_fwd")
    t_da = _matmul(o_da, wpda, "nn", F32, 512, 1024, DA_WIDTH, "proj_da_fwd")
    t_mem = _matmul(o_mem, wpmem, "nn", F32, 512, 1024, MEM_WIDTH, "proj_mem_fwd")
    merged = _gate_merge_fwd(proj, t_hg, t_da, t_mem, 256)
    u = _matmul(merged, wout, "nn", F32, 512, 1024, D_MODEL, "out_fwd")
    x1, h2, h2_t = _residual_rmsnorm_fwd(xs, u, norm_ffn_gain, "norm_ffn_fwd", 512)
    ab = _matmul(h2, wfin, "nn", F32, 1024, 1408, D_MODEL, "ffn_in_fwd", n_outer=True)
    act, act_t = _swiglu_fwd(ab, 256)
    z = _matmul(act, wfout, "nn", F32, 512, 1024, D_FF, "ffn_out_fwd")
    dy, dyb, loss_part = _loss_head(x1, z, target, 512)

    dact = _matmul(dyb, wfout, "nt", F32, 512, 1408, D_MODEL, "ffn_out_bwd_act")
    g_wfout = _matmul(act_t, dyb, "nn", BF16, 1408, 1024, 2048, "ffn_out_bwd_w")
    dab = _swiglu_bwd(ab, dact, 256)
    dh2 = _matmul(dab, wfin, "nt", F32, 1024, 1024, 2816, "ffn_in_bwd_act")
    g_wfin = _matmul(h2_t, dab, "nn", BF16, 1024, 1408, 2048, "ffn_in_bwd_w")
    token = yield "grads_ffn", dict(
        w_ffn_in=jnp.transpose(g_wfin.reshape(D_MODEL, N_DEV, 2 * D_FF // N_DEV), (1, 0, 2)),
        w_ffn_out=g_wfout.reshape(N_DEV, D_FF // N_DEV, D_MODEL))
    dx1, dx1b, g_norm_ffn = _rmsnorm_bwd(x1, dh2, dy, _after(norm_ffn_gain, token), "norm_ffn_bwd", 256)
    dmerged = _matmul(dx1b, wout, "nt", F32, 512, 1024, D_MODEL, "out_bwd_act")
    g_wout = _matmul(merged, dx1b, "tn", BF16, 1024, 1024, 2048, "out_bwd_w")
    dt_hg, dt_da, dt_mem, dgates = _gate_merge_bwd(proj, t_hg, t_da, t_mem, dmerged, 256)
    token = yield "after_gate_merge_bwd", dt_hg
    do_hg = _matmul(dt_hg, wphg, "nt", F32, 512, 1024, D_MODEL, "proj_hg_bwd_act", after=token)
    g_wphg = _matmul(o_hg, dt_hg, "tn", BF16, 1024, 1024, 2048, "proj_hg_bwd_w")
    do_da = _matmul(dt_da, wpda, "nt", F32, 512, DA_WIDTH, D_MODEL, "proj_da_bwd_act")
    g_wpda = _matmul(o_da, dt_da, "tn", BF16, DA_WIDTH, D_MODEL, 2048, "proj_da_bwd_w")
    do_mem = _matmul(dt_mem, wpmem, "nt", F32, 512, MEM_WIDTH, D_MODEL, "proj_mem_bwd_act")
    g_wpmem = _matmul(o_mem, dt_mem, "tn", BF16, MEM_WIDTH, D_MODEL, 2048, "proj_mem_bwd_w")
    by_owner = lambda g: jnp.transpose(g.reshape(g.shape[0], N_DEV, D_MODEL // N_DEV), (1, 0, 2))
    g_wpda, g_wpmem = by_owner(g_wpda), by_owner(g_wpmem)

    dmemq, dk_mem, dv_mem, g_mem_q, g_mem_k = _mem_bwd(proj, kv, mem_q_gain, mem_k_gain, do_mem, 512)
    dkv = jnp.concatenate([dk_mem, dv_mem], axis=1).astype(BF16)
    g_wkv = _matmul(mem_n, dkv, "tn", BF16, 1024, 1024, N_MEM, "mem_kv_bwd_w")
    dmem_n = _matmul(dkv, wkv, "nt", F32, N_MEM, 1024, 1024, "mem_kv_bwd_act")
    g_norm_mem = _gain_grad(mems, dmem_n, "norm_mem_bwd")
    token = yield "grads_mix", dict(
        w_mem_kv=g_wkv.reshape(N_DEV, D_MODEL // N_DEV, 2 * MEM_WIDTH),
        w_proj_hg=g_wphg.reshape(N_DEV, D_MODEL // N_DEV, D_MODEL),
        w_proj_da=g_wpda, w_proj_mem=g_wpmem,
        w_out=g_wout.reshape(N_DEV, D_MODEL // N_DEV, D_MODEL))

    dd_da = _da_rowdot(do_da, o_da32, 512)
    da_b = [_da_bwd(proj, _after(da_q_gain, token), da_k_gain, do_da, lse_da, dd_da, 0)]
    token = yield "after_da_bwd_g0", da_b[0][0]
    da_b += [_da_bwd(proj, _after(da_q_gain, token), da_k_gain, do_da, lse_da, dd_da, g) for g in (1, 2)]
    g_da_q = da_b[0][3] + da_b[1][3] + da_b[2][3]
    g_da_k = da_b[0][4] + da_b[1][4] + da_b[2][4]

    dq_hg, df_hg, dv_hg, dg_hg, dlb, g_hg_norm = _gla_bwd(proj, lb, hg_norm_gain, o_pre, states, do_hg)
    yield "after_gla_bwd", dq_hg

    dproj = jnp.concatenate(
        [dq_hg, df_hg[0], df_hg[1], dv_hg, dg_hg]
        + [t[0] for t in da_b] + [t[1] for t in da_b] + [t[2] for t in da_b] + [dmemq, dgates], axis=1)
    g_win = _matmul(h_t, dproj, "nn", BF16, 1024, IN_SHARD, 2048, "proj_bwd_w", out_stacked=True)
    token = yield "grads_in", dict(w_in=g_win)
    dh = _matmul(dproj, win_st, "nt", F32, 1024, 1024, IN_SHARD, "proj_bwd_act", b_stacked=True, after=token)
    token = yield "after_proj_bwd_act", dh
    grad_x, _, g_norm_mix = _rmsnorm_bwd(xs, dh, dx1, _after(norm_mix_gain, token), "norm_mix_bwd", 256)

    gpart = _pack_rows([g_norm_mix, g_norm_mem, dlb[0], dlb[1], g_norm_ffn, g_hg_norm, g_da_q, g_da_k,
                        g_mem_q, g_mem_k, loss_part], SMALL_GRAD_ROWS)
    lbpack = jnp.concatenate([lb_fw.reshape(8, LANE), lb_bw.reshape(8, LANE)], axis=0)
    yield "end", dict(grad_x=grad_x, gpart=gpart, lbpack=lbpack)
```

```python
import numpy as np
import jax
import jax.numpy as jnp
from jax import lax
from jax.experimental import pallas as pl
from jax.experimental.pallas import tpu as pltpu

F32 = jnp.float32
BF16 = jnp.bfloat16
MESH = pl.DeviceIdType.MESH

SEQ = 4096
D_MODEL = 1024
N_DEV = 8
N_MEM = 256
RMS_EPS = 1e-6
NEG_INF = -1e30
LANE = 128
HEAD_DIM = 128
HG_HEADS = 8
HG_CHUNK = 64
HG_SCALE = HEAD_DIM ** -0.5
DA_DILATIONS = (1, 4, 16)
DA_RADIUS = 64
DA_HEADS_PER_GROUP = 4
DA_HEADS = 12
DA_WIDTH = 512
DA_SCALE = HEAD_DIM ** -0.5
DA_QB = 128
DA_WIN = 256
DA_WAYS = 2
MEM_HEADS = 4
MEM_WIDTH = 512
MEM_SCALE = HEAD_DIM ** -0.5
D_FF = 2816
IN_COLS = 13312
IN_SHARD = IN_COLS // N_DEV
CB_HG_Q, CB_F, CB_HG_I, CB_HG_G = 0, 8, 24, 32
CB_DA_Q, CB_DA_K, CB_DA_V, CB_MEM_Q = 40, 52, 64, 76
N_CB = IN_COLS // LANE
ADAM_LR, ADAM_B1, ADAM_B2, ADAM_EPS, ADAM_WD, ADAM_STEP = 0.001, 0.9, 0.999, 1e-08, 0.01, 10
VMEM_BYTES_V7X = 64 * 1024 * 1024
SMALL_ROWS = 104
SMALL_GRAD_ROWS = 88

_NN = (((1,), (0,)), ((), ()))
_NT = (((1,), (1,)), ((), ()))
_TN = (((0,), (0,)), ((), ()))


def _dot(a, b, dims):
    return lax.dot_general(a.astype(BF16), b.astype(BF16), dims, preferred_element_type=F32)


def _dot_exact(a, b, dims):
    return lax.dot_general(a, b, dims, precision=lax.Precision.HIGHEST, preferred_element_type=F32)


def _sigmoid(x):
    return 0.5 * jnp.tanh(0.5 * x) + 0.5


def _params(semantics, est_bytes):
    limit = int(min(VMEM_BYTES_V7X - (6 << 20), max(56 << 20, est_bytes * 3 // 2)))
    return pltpu.CompilerParams(dimension_semantics=semantics, vmem_limit_bytes=limit)


def _nbytes(shape, dtype):
    return int(np.prod(shape)) * jnp.dtype(dtype).itemsize


def _alibi_slopes(n):
    return (2.0 ** (-8.0 * np.arange(1, n + 1) / n)).astype(np.float32)


def _matmul(a, b, mode, out_dtype, tm, tn, tk, name, b_stacked=False, out_stacked=False, n_outer=False, after=None):
    if mode == "tn":
        kdim, m = a.shape
    else:
        m, kdim = a.shape
    if b_stacked:
        if mode == "nn":
            n = b.shape[0] * b.shape[2]
            assert tn == b.shape[2] and tk == kdim == b.shape[1]
        else:
            assert mode == "nt" and tk == b.shape[2] and b.shape[0] * tk == kdim
            n = b.shape[1]
    else:
        n = b.shape[0] if mode == "nt" else b.shape[1]
    assert m % tm == 0 and n % tn == 0 and kdim % tk == 0
    gm, gn, gk = m // tm, n // tn, kdim // tk

    def ijk(f):
        if n_outer:
            return lambda j, i, k: f(i, j, k)
        return f

    if mode == "tn":
        a_spec = pl.BlockSpec((tk, tm), ijk(lambda i, j, k: (k, i)))
    else:
        a_spec = pl.BlockSpec((tm, tk), ijk(lambda i, j, k: (i, k)))
    if b_stacked and mode == "nn":
        b_spec = pl.BlockSpec((None, tk, tn), ijk(lambda i, j, k: (j, 0, 0)))
    elif b_stacked:
        b_spec = pl.BlockSpec((None, tn, tk), ijk(lambda i, j, k: (k, j, 0)))
    elif mode == "nt":
        b_spec = pl.BlockSpec((tn, tk), ijk(lambda i, j, k: (j, k)))
    else:
        b_spec = pl.BlockSpec((tk, tn), ijk(lambda i, j, k: (k, j)))
    if out_stacked:
        assert tm == m
        out_shape = jax.ShapeDtypeStruct((gn, m, tn), out_dtype)
        o_spec = pl.BlockSpec((None, tm, tn), ijk(lambda i, j, k: (j, i, 0)))
    else:
        out_shape = jax.ShapeDtypeStruct((m, n), out_dtype)
        o_spec = pl.BlockSpec((tm, tn), ijk(lambda i, j, k: (i, j)))
    dims = {"nn": _NN, "nt": _NT, "tn": _TN}[mode]

    def body(*refs):
        a_ref, b_ref = refs[0], refs[1]
        o_ref = refs[2 if after is None else 3]
        part = _dot(a_ref[...], b_ref[...], dims)
        if gk == 1:
            o_ref[...] = part.astype(out_dtype)
            return
        acc_ref = refs[-1]
        k = pl.program_id(2)

        @pl.when(k == 0)
        def _():
            acc_ref[...] = part

        @pl.when(jnp.logical_and(k > 0, k < gk - 1))
        def _():
            acc_ref[...] += part

        @pl.when(k == gk - 1)
        def _():
            o_ref[...] = (acc_ref[...] + part).astype(out_dtype)

    a_tile = _nbytes((tm, tk), a.dtype)
    b_tile = _nbytes((tk, tn), b.dtype)
    o_tile = _nbytes((tm, tn), out_dtype)
    est = 2 * (a_tile + b_tile + o_tile) + 3 * tm * tn * 4 + (a_tile + b_tile)
    grid = (gn, gm, gk) if n_outer else (gm, gn, gk)
    operands, in_specs = [a, b], [a_spec, b_spec]
    if after is not None:
        operands.append(after)
        in_specs.append(pl.BlockSpec(memory_space=pl.ANY))
    return pl.pallas_call(
        body, name=name, grid=grid, in_specs=in_specs, out_specs=o_spec, out_shape=out_shape,
        scratch_shapes=[] if gk == 1 else [pltpu.VMEM((tm, tn), F32)],
        compiler_params=_params(("parallel", "parallel", "arbitrary"), est),
    )(*operands)


def _row_spec(tr, width, col_block=0):
    return pl.BlockSpec((tr, width), lambda i: (i, col_block))


def _bcast_spec(width):
    return pl.BlockSpec((1, width), lambda i: (0, 0))


def _col_spec(width, tr):
    return pl.BlockSpec((width, tr), lambda i: (0, i))


def _rmsnorm_fwd(x, gain, name, tr, transposed=False):
    rows, width = x.shape

    def body(x_ref, g_ref, o_ref, *t_ref):
        xv = x_ref[...]
        r = lax.rsqrt(jnp.mean(xv * xv, axis=-1, keepdims=True) + RMS_EPS)
        h = xv * r * g_ref[...]
        o_ref[...] = h.astype(BF16)
        if transposed:
            t_ref[0][...] = h.T.astype(BF16)

    out_specs, out_shape = [_row_spec(tr, width)], [jax.ShapeDtypeStruct((rows, width), BF16)]
    if transposed:
        out_specs.append(_col_spec(width, tr))
        out_shape.append(jax.ShapeDtypeStruct((width, rows), BF16))
    out = pl.pallas_call(
        body, name=name, grid=(rows // tr,), in_specs=[_row_spec(tr, width), _bcast_spec(width)],
        out_specs=out_specs, out_shape=out_shape,
        compiler_params=_params(("parallel",), 10 * tr * width * 4),
    )(x, gain)
    return out if transposed else out[0]


def _residual_rmsnorm_fwd(x, u, gain, name, tr):
    rows, width = x.shape

    def body(x_ref, u_ref, g_ref, x1_ref, h_ref, ht_ref):
        xv = x_ref[...] + u_ref[...]
        x1_ref[...] = xv
        r = lax.rsqrt(jnp.mean(xv * xv, axis=-1, keepdims=True) + RMS_EPS)
        h = xv * r * g_ref[...]
        h_ref[...] = h.astype(BF16)
        ht_ref[...] = h.T.astype(BF16)

    return pl.pallas_call(
        body, name=name, grid=(rows // tr,),
        in_specs=[_row_spec(tr, width), _row_spec(tr, width), _bcast_spec(width)],
        out_specs=[_row_spec(tr, width), _row_spec(tr, width), _col_spec(width, tr)],
        out_shape=[jax.ShapeDtypeStruct((rows, width), F32), jax.ShapeDtypeStruct((rows, width), BF16),
                   jax.ShapeDtypeStruct((width, rows), BF16)],
        compiler_params=_params(("parallel",), 14 * tr * width * 4),
    )(x, u, gain)


def _rmsnorm_bwd(x, dh, dres, gain, name, tr):
    rows, width = x.shape

    def body(x_ref, dh_ref, dres_ref, g_ref, dx_ref, dxb_ref, dg_ref):
        xv = x_ref[...]
        r = lax.rsqrt(jnp.mean(xv * xv, axis=-1, keepdims=True) + RMS_EPS)
        xhat = xv * r
        dhv = dh_ref[...]
        dyg = dhv * g_ref[...]
        dx = dres_ref[...] + r * (dyg - xhat * jnp.mean(dyg * xhat, axis=-1, keepdims=True))
        dx_ref[...] = dx
        dxb_ref[...] = dx.astype(BF16)
        part = jnp.sum(dhv * xhat, axis=0, keepdims=True)

        @pl.when(pl.program_id(0) == 0)
        def _():
            dg_ref[...] = part

        @pl.when(pl.program_id(0) > 0)
        def _():
            dg_ref[...] += part

    return pl.pallas_call(
        body, name=name, grid=(rows // tr,),
        in_specs=[_row_spec(tr, width), _row_spec(tr, width), _row_spec(tr, width), _bcast_spec(width)],
        out_specs=[_row_spec(tr, width), _row_spec(tr, width), _bcast_spec(width)],
        out_shape=[jax.ShapeDtypeStruct((rows, width), F32), jax.ShapeDtypeStruct((rows, width), BF16),
                   jax.ShapeDtypeStruct((1, width), F32)],
        compiler_params=_params(("arbitrary",), 14 * tr * width * 4),
    )(x, dh, dres, gain)


def _gain_grad(x, dh, name):
    rows, width = x.shape

    def body(x_ref, dh_ref, dg_ref):
        xv = x_ref[...]
        r = lax.rsqrt(jnp.mean(xv * xv, axis=-1, keepdims=True) + RMS_EPS)
        dg_ref[...] = jnp.sum(dh_ref[...] * xv * r, axis=0, keepdims=True)

    return pl.pallas_call(
        body, name=name, grid=(1,), in_specs=[_row_spec(rows, width), _row_spec(rows, width)],
        out_specs=_bcast_spec(width), out_shape=jax.ShapeDtypeStruct((1, width), F32),
        compiler_params=_params(("arbitrary",), 6 * rows * width * 4),
    )(x, dh)


def _lb_table(logits, name):
    slots, width = logits.shape

    def body(l_ref, o_ref):
        lv = l_ref[...]
        mx = jnp.max(lv, axis=0, keepdims=True)
        e = jnp.exp(lv - mx)
        o_ref[...] = e[0:1, :] / jnp.sum(e, axis=0, keepdims=True)

    return pl.pallas_call(
        body, name=name, grid=(1,), in_specs=[pl.BlockSpec((slots, width), lambda i: (0, 0))],
        out_specs=_bcast_spec(width), out_shape=jax.ShapeDtypeStruct((1, width), F32),
    )(logits)


def _gate_merge_fwd(proj, t_hg, t_da, t_mem, tr):
    w = D_MODEL

    def body(ghg_ref, gda_ref, gmem_ref, thg_ref, tda_ref, tmem_ref, o_ref):
        acc = _sigmoid(ghg_ref[...]) * thg_ref[...]
        acc += _sigmoid(gda_ref[...]) * tda_ref[...]
        acc += _sigmoid(gmem_ref[...]) * tmem_ref[...]
        o_ref[...] = acc.astype(BF16)

    return pl.pallas_call(
        body, name="gate_merge_fwd", grid=(SEQ // tr,),
        in_specs=[_row_spec(tr, w, 10), _row_spec(tr, w, 11), _row_spec(tr, w, 12),
                  _row_spec(tr, w), _row_spec(tr, w), _row_spec(tr, w)],
        out_specs=_row_spec(tr, w), out_shape=jax.ShapeDtypeStruct((SEQ, w), BF16),
        compiler_params=_params(("parallel",), 16 * tr * w * 4),
    )(proj, proj, proj, t_hg, t_da, t_mem)


def _gate_merge_bwd(proj, t_hg, t_da, t_mem, dmerged, tr):
    w = D_MODEL

    def body(ghg_ref, gda_ref, gmem_ref, thg_ref, tda_ref, tmem_ref, dm_ref, dthg_ref, dtda_ref, dtmem_ref, dg_ref):
        dm = dm_ref[...]
        for b, (g_ref, t_ref, dt_ref) in enumerate(((ghg_ref, thg_ref, dthg_ref), (gda_ref, tda_ref, dtda_ref),
                                                    (gmem_ref, tmem_ref, dtmem_ref))):
            s = _sigmoid(g_ref[...])
            dt_ref[...] = (s * dm).astype(BF16)
            dg_ref[:, b * w:(b + 1) * w] = (dm * t_ref[...] * s * (1.0 - s)).astype(BF16)

    return pl.pallas_call(
        body, name="gate_merge_bwd", grid=(SEQ // tr,),
        in_specs=[_row_spec(tr, w, 10), _row_spec(tr, w, 11), _row_spec(tr, w, 12),
                  _row_spec(tr, w), _row_spec(tr, w), _row_spec(tr, w), _row_spec(tr, w)],
        out_specs=[_row_spec(tr, w), _row_spec(tr, w), _row_spec(tr, w), _row_spec(tr, 3 * w)],
        out_shape=[jax.ShapeDtypeStruct((SEQ, w), BF16)] * 3 + [jax.ShapeDtypeStruct((SEQ, 3 * w), BF16)],
        compiler_params=_params(("parallel",), 24 * tr * w * 4),
    )(proj, proj, proj, t_hg, t_da, t_mem, dmerged)


def _swiglu_fwd(ab, tr):
    def body(ab_ref, o_ref, ot_ref):
        a = ab_ref[:, :D_FF]
        b = ab_ref[:, D_FF:]
        act = a * _sigmoid(a) * b
        o_ref[...] = act.astype(BF16)
        ot_ref[...] = act.T.astype(BF16)

    return pl.pallas_call(
        body, name="swiglu_fwd", grid=(SEQ // tr,), in_specs=[_row_spec(tr, 2 * D_FF)],
        out_specs=[_row_spec(tr, D_FF), _col_spec(D_FF, tr)],
        out_shape=[jax.ShapeDtypeStruct((SEQ, D_FF), BF16), jax.ShapeDtypeStruct((D_FF, SEQ), BF16)],
        compiler_params=_params(("parallel",), 10 * tr * 2 * D_FF * 4),
    )(ab)


def _swiglu_bwd(ab, dact, tr):
    def body(ab_ref, d_ref, o_ref):
        a = ab_ref[:, :D_FF]
        b = ab_ref[:, D_FF:]
        d = d_ref[...]
        s = _sigmoid(a)
        silu = a * s
        o_ref[:, :D_FF] = (d * b * (s + silu * (1.0 - s))).astype(BF16)
        o_ref[:, D_FF:] = (d * silu).astype(BF16)

    return pl.pallas_call(
        body, name="swiglu_bwd", grid=(SEQ // tr,), in_specs=[_row_spec(tr, 2 * D_FF), _row_spec(tr, D_FF)],
        out_specs=_row_spec(tr, 2 * D_FF), out_shape=jax.ShapeDtypeStruct((SEQ, 2 * D_FF), BF16),
        compiler_params=_params(("parallel",), 10 * tr * 2 * D_FF * 4),
    )(ab, dact)


def _loss_head(x1, z, target, tr):
    w = D_MODEL

    def body(x_ref, z_ref, t_ref, dy_ref, dyb_ref, loss_ref, acc_ref):
        err = x_ref[...] + z_ref[...] - t_ref[...]
        dy = err * (1.0 / w)
        dy_ref[...] = dy
        dyb_ref[...] = dy.astype(BF16)
        part = jnp.sum(err * err, axis=0, keepdims=True)

        @pl.when(pl.program_id(0) == 0)
        def _():
            acc_ref[...] = part

        @pl.when(pl.program_id(0) > 0)
        def _():
            acc_ref[...] += part

        @pl.when(pl.program_id(0) == SEQ // tr - 1)
        def _():
            total = jnp.sum(acc_ref[...], axis=1, keepdims=True) * (0.5 / w)
            loss_ref[...] = jnp.broadcast_to(total, (1, LANE))

    return pl.pallas_call(
        body, name="loss_head", grid=(SEQ // tr,),
        in_specs=[_row_spec(tr, w), _row_spec(tr, w), _row_spec(tr, w)],
        out_specs=[_row_spec(tr, w), _row_spec(tr, w), _bcast_spec(LANE)],
        out_shape=[jax.ShapeDtypeStruct((SEQ, w), F32), jax.ShapeDtypeStruct((SEQ, w), BF16),
                   jax.ShapeDtypeStruct((1, LANE), F32)],
        scratch_shapes=[pltpu.VMEM((1, w), F32)],
        compiler_params=_params(("arbitrary",), 12 * tr * w * 4),
    )(x1, z, target)


GLA_ROWS = 256
GLA_CPB = GLA_ROWS // HG_CHUNK
GLA_NBLK = SEQ // GLA_ROWS
GLA_WAYS = 4
GLA_TRIPS = GLA_NBLK // GLA_WAYS
GLA_GRAD_WAYS = 4
GLA_GRAD_TRIPS = GLA_NBLK // GLA_GRAD_WAYS
GLA_NCK = SEQ // HG_CHUNK


def _dot_split(m, xv, dims):
    hi = xv.astype(BF16)
    r1 = xv - hi.astype(F32)
    mid = r1.astype(BF16)
    lo = (r1 - mid.astype(F32)).astype(BF16)
    dot = lambda t: lax.dot_general(m, t, dims, preferred_element_type=F32)
    return (dot(lo) + dot(mid)) + dot(hi)


def _gla_block(qc, fc, lbv, masks):
    mask, maskb, direction = masks
    sq = _sigmoid(qc)
    q = qc * sq * HG_SCALE
    sf = _sigmoid(fc)
    forget = lbv + (1.0 - lbv) * sf
    k = 1.0 - forget
    logf = jnp.log(forget)
    b = _dot_split(maskb, logf, _NN)
    ends = []
    for j in range(GLA_CPB):
        lo, hi = j * HG_CHUNK, (j + 1) * HG_CHUNK
        end = jnp.where(direction == 0, b[hi - 1:hi, :], b[lo:lo + 1, :])
        ends.append(jnp.broadcast_to(end, (HG_CHUNK, HEAD_DIM)))
    bt = jnp.concatenate(ends, axis=0)
    eb = jnp.exp(b)
    qt = q * eb
    kt = k * jnp.exp(-b)
    kh = k * jnp.exp(bt - b)
    a = jnp.where(mask, _dot(qt, kt, _NT), 0.0)
    return dict(sq=sq, sf=sf, forget=forget, k=k, b=b, bt=bt, eb=eb, qt=qt, kt=kt, kh=kh, a=a)


def _gla_masks(direction):
    row = lax.broadcasted_iota(jnp.int32, (GLA_ROWS, GLA_ROWS), 0)
    col = lax.broadcasted_iota(jnp.int32, (GLA_ROWS, GLA_ROWS), 1)
    same = (row // HG_CHUNK) == (col // HG_CHUNK)
    mask = jnp.logical_and(same, jnp.where(direction == 0, row - col, col - row) >= 0)
    return mask, jnp.where(mask, 1.0, 0.0).astype(BF16), direction


def _gla_chunk_rows(j):
    return slice(j * HG_CHUNK, (j + 1) * HG_CHUNK)


def _gla_block_rows(b):
    return pl.ds(pl.multiple_of(b * GLA_ROWS, GLA_ROWS), GLA_ROWS)


def _head_spec(col_block0):
    return pl.BlockSpec((SEQ, HEAD_DIM), lambda h, d: (0, col_block0 + h))


def _gla_fwd(proj, lb, gain):
    nck = GLA_NCK

    def body(q_ref, f_ref, v_ref, g_ref, lb_ref, gain_ref, ohg_ref, opre_ref, st_ref, qt_scr, cs_scr, dec_scr):
        d = pl.program_id(1)
        masks = _gla_masks(d)
        lbv = lb_ref[...]

        @pl.when(d == 0)
        def _():
            opre_ref[...] = jnp.zeros_like(opre_ref)

        def intra(s, carry):
            blocks = [s + w * GLA_TRIPS for w in range(GLA_WAYS)]
            rows = [_gla_block_rows(b) for b in blocks]
            loaded = [(q_ref[r, :], f_ref[r, :], v_ref[r, :], opre_ref[r, :]) for r in rows]
            results = []
            for qc, fc, v, o_prev in loaded:
                ck = _gla_block(qc, fc, lbv, masks)
                o = o_prev + _dot(ck["a"], v, _NN)
                cs = [_dot(v[_gla_chunk_rows(j), :], ck["kh"][_gla_chunk_rows(j), :], _TN) for j in range(GLA_CPB)]
                dec = [jnp.exp(ck["bt"][j * HG_CHUNK:j * HG_CHUNK + 8, :]) for j in range(GLA_CPB)]
                results.append((o, ck["qt"].astype(BF16), cs, dec))
            for b, r, (o, qt, cs, dec) in zip(blocks, rows, results):
                opre_ref[r, :] = o
                qt_scr[r, :] = qt
                for j in range(GLA_CPB):
                    cs_scr[b * GLA_CPB + j] = cs[j]
                    dec_scr[b * GLA_CPB + j] = dec[j]
            return carry

        lax.fori_loop(0, GLA_TRIPS, intra, 0)

        def scan(i, st):
            c = jnp.where(d == 0, i, nck - 1 - i)
            st_ref[c] = st
            return st * dec_scr[c][0:1, :] + cs_scr[c]

        lax.fori_loop(0, nck, scan, jnp.zeros((HEAD_DIM, HEAD_DIM), F32), unroll=4)

        def inter(c, carry):
            rows = pl.ds(pl.multiple_of(c * HG_CHUNK, HG_CHUNK), HG_CHUNK)
            opre_ref[rows, :] += _dot(qt_scr[rows, :], st_ref[c], _NT)
            return carry

        lax.fori_loop(0, nck, inter, 0, unroll=4)

        @pl.when(d == 1)
        def _():
            o = opre_ref[...]
            r = lax.rsqrt(jnp.mean(o * o, axis=-1, keepdims=True) + RMS_EPS)
            g = g_ref[...]
            ohg_ref[...] = (o * r * gain_ref[...] * (g * _sigmoid(g))).astype(BF16)

    blk = SEQ * HEAD_DIM * 4
    return pl.pallas_call(
        body, name="gla_fwd", grid=(HG_HEADS, 2),
        in_specs=[_head_spec(CB_HG_Q),
                  pl.BlockSpec((SEQ, HEAD_DIM), lambda h, d: (0, CB_F + 8 * d + h)),
                  _head_spec(CB_HG_I), _head_spec(CB_HG_G),
                  pl.BlockSpec((None, None, 1, HEAD_DIM), lambda h, d: (d, h, 0, 0)),
                  pl.BlockSpec((1, HEAD_DIM), lambda h, d: (0, 0))],
        out_specs=[_head_spec(0), _head_spec(0),
                   pl.BlockSpec((None, None, nck, HEAD_DIM, HEAD_DIM), lambda h, d: (h, d, 0, 0, 0))],
        out_shape=[jax.ShapeDtypeStruct((SEQ, D_MODEL), BF16), jax.ShapeDtypeStruct((SEQ, D_MODEL), F32),
                   jax.ShapeDtypeStruct((HG_HEADS, 2, nck, HEAD_DIM, HEAD_DIM), F32)],
        scratch_shapes=[pltpu.VMEM((SEQ, HEAD_DIM), BF16), pltpu.VMEM((nck, HEAD_DIM, HEAD_DIM), F32),
                        pltpu.VMEM((nck, 8, HEAD_DIM), F32)],
        compiler_params=_params(("parallel", "arbitrary"), 8 * blk + 3 * blk + 2 * blk + 2 * blk + blk // 2),
    )(proj, proj, proj, proj, lb, gain)


def _gla_bwd(proj, lb, gain, o_pre, states, do_hg):
    nck = GLA_NCK
    do, dg, dgain = _gla_bwd_norm(proj, gain, o_pre, do_hg, 256)

    def body(q_ref, f_ref, v_ref, lb_ref, do_ref, st_ref, dq_ref, df_ref, dv_ref, dlb_ref,
             dq_acc, dv_acc, dst_scr, cs_scr, dec_scr):
        d = pl.program_id(1)
        masks = _gla_masks(d)
        mask, maskb, _ = masks
        lbv = lb_ref[...]

        def intra(s, carry):
            blocks = [s + w * GLA_TRIPS for w in range(GLA_WAYS)]
            loaded = [(q_ref[r, :], f_ref[r, :], do_ref[r, :]) for r in map(_gla_block_rows, blocks)]
            results = []
            for qc, fc, doc in loaded:
                ck = _gla_block(qc, fc, lbv, masks)
                cs = [_dot(doc[_gla_chunk_rows(j), :], ck["qt"][_gla_chunk_rows(j), :], _TN) for j in range(GLA_CPB)]
                dec = [jnp.exp(ck["bt"][j * HG_CHUNK:j * HG_CHUNK + 8, :]) for j in range(GLA_CPB)]
                results.append((cs, dec))
            for b, (cs, dec) in zip(blocks, results):
                for j in range(GLA_CPB):
                    cs_scr[b * GLA_CPB + j] = cs[j]
                    dec_scr[b * GLA_CPB + j] = dec[j]
            return carry

        lax.fori_loop(0, GLA_TRIPS, intra, 0)

        def scan(i, dst):
            c = jnp.where(d == 0, nck - 1 - i, i)
            dst_scr[c] = dst
            return dst * dec_scr[c][0:1, :] + cs_scr[c]

        lax.fori_loop(0, nck, scan, jnp.zeros((HEAD_DIM, HEAD_DIM), F32), unroll=4)

        @pl.when(d == 0)
        def _():
            dq_acc[...] = jnp.zeros_like(dq_acc)
            dv_acc[...] = jnp.zeros_like(dv_acc)

        def block_grads(qc, fc, v, doc, states_in, dstates, decays):
            ck = _gla_block(qc, fc, lbv, masks)
            qt, kt, kh, a = ck["qt"], ck["kt"], ck["kh"], ck["a"]
            da = jnp.where(mask, _dot(doc, v, _NT), 0.0)
            dqt_i = _dot(da, kt, _NN)
            dkt = _dot(da, qt, _TN)
            dv_i = _dot(a, doc, _TN)
            dqt_p, dv_p, dkh_p, dbt_p = [], [], [], []
            for j in range(GLA_CPB):
                cr = _gla_chunk_rows(j)
                st_in, dst = states_in[j], dstates[j]
                dqt_p.append(dqt_i[cr, :] + _dot(doc[cr, :], st_in, _NN))
                dv_p.append(dv_i[cr, :] + _dot(kh[cr, :], dst, _NT))
                dkh_j = _dot(v[cr, :], dst, _NN)
                dkh_p.append(dkh_j)
                dbt_j = (decays[j][0:1, :] * jnp.sum(dst * st_in, axis=0, keepdims=True)
                         + jnp.sum(dkh_j * kh[cr, :], axis=0, keepdims=True))
                dbt_p.append(jnp.broadcast_to(dbt_j, (HG_CHUNK, HEAD_DIM)))
            dqt = jnp.concatenate(dqt_p, axis=0)
            dv = jnp.concatenate(dv_p, axis=0)
            dkh = jnp.concatenate(dkh_p, axis=0)
            dbt = jnp.concatenate(dbt_p, axis=0)
            db = dqt * qt - dkt * kt - dkh * kh
            dq = dqt * ck["eb"]
            dk = dkt * jnp.exp(-ck["b"]) + dkh * jnp.exp(ck["bt"] - ck["b"])
            dlogf = _dot_split(maskb, db, _TN) + dbt
            dforget = dlogf / ck["forget"] - dk
            sf, sq = ck["sf"], ck["sq"]
            df = (dforget * (1.0 - lbv) * sf * (1.0 - sf)).astype(BF16)
            dqc = dq * HG_SCALE * (sq + qc * sq * (1.0 - sq))
            return df, dqc, dv, jnp.sum(dforget * (1.0 - sf), axis=0, keepdims=True)

        def grads(s, dlb):
            blocks = [s + w * GLA_GRAD_TRIPS for w in range(GLA_GRAD_WAYS)]
            rows = [_gla_block_rows(b) for b in blocks]
            loaded = []
            for b, r in zip(blocks, rows):
                chunks = [b * GLA_CPB + j for j in range(GLA_CPB)]
                loaded.append((q_ref[r, :], f_ref[r, :], v_ref[r, :], do_ref[r, :], [st_ref[c] for c in chunks],
                               [dst_scr[c] for c in chunks], [dec_scr[c] for c in chunks], dq_acc[r, :], dv_acc[r, :]))
            results = [block_grads(*t[:7]) + (t[7], t[8]) for t in loaded]
            for r, (df, dqc, dv, dlb_part, dq_prev, dv_prev) in zip(rows, results):
                df_ref[r, :] = df
                dq_acc[r, :] = dq_prev + dqc
                dv_acc[r, :] = dv_prev + dv
                dlb = dlb + dlb_part
            return dlb

        dlb_ref[...] = lax.fori_loop(0, GLA_GRAD_TRIPS, grads, jnp.zeros((1, HEAD_DIM), F32))

        @pl.when(d == 1)
        def _():
            dq_ref[...] = dq_acc[...].astype(BF16)
            dv_ref[...] = dv_acc[...].astype(BF16)

    blk = SEQ * HEAD_DIM * 4
    bshape = jax.ShapeDtypeStruct((SEQ, D_MODEL), BF16)
    state_bytes = nck * HEAD_DIM * HEAD_DIM * 4
    dq, df, dv, dlb = pl.pallas_call(
        body, name="gla_bwd", grid=(HG_HEADS, 2),
        in_specs=[_head_spec(CB_HG_Q),
                  pl.BlockSpec((SEQ, HEAD_DIM), lambda h, d: (0, CB_F + 8 * d + h)),
                  _head_spec(CB_HG_I),
                  pl.BlockSpec((None, None, 1, HEAD_DIM), lambda h, d: (d, h, 0, 0)),
                  _head_spec(0),
                  pl.BlockSpec((None, None, nck, HEAD_DIM, HEAD_DIM), lambda h, d: (h, d, 0, 0, 0))],
        out_specs=[_head_spec(0),
                   pl.BlockSpec((None, SEQ, HEAD_DIM), lambda h, d: (d, 0, h)),
                   _head_spec(0),
                   pl.BlockSpec((None, None, 1, HEAD_DIM), lambda h, d: (d, h, 0, 0))],
        out_shape=[bshape, jax.ShapeDtypeStruct((2, SEQ, D_MODEL), BF16), bshape,
                   jax.ShapeDtypeStruct((2, HG_HEADS, 1, HEAD_DIM), F32)],
        scratch_shapes=[pltpu.VMEM((SEQ, HEAD_DIM), F32)] * 2
        + [pltpu.VMEM((nck, HEAD_DIM, HEAD_DIM), F32)] * 2 + [pltpu.VMEM((nck, 8, HEAD_DIM), F32)],
        compiler_params=_params(("parallel", "arbitrary"), 8 * blk + 4 * state_bytes + 3 * blk + 2 * blk),
    )(proj, proj, proj, lb, do, states)
    return dq, df, dv, dg, dlb, dgain


def _gla_bwd_norm(proj, gain, o_pre, do_hg, tr):
    w = D_MODEL

    def body(g_ref, gain_ref, opre_ref, dohg_ref, do_ref, dg_ref, dgain_ref):
        gainv = gain_ref[...]
        part = jnp.zeros((1, HEAD_DIM), F32)
        for h in range(HG_HEADS):
            hs = slice(h * HEAD_DIM, (h + 1) * HEAD_DIM)
            o = opre_ref[:, hs]
            r = lax.rsqrt(jnp.mean(o * o, axis=-1, keepdims=True) + RMS_EPS)
            ohat = o * r
            g = g_ref[:, hs]
            sg = _sigmoid(g)
            silu = g * sg
            dout = dohg_ref[:, hs]
            dg_ref[:, hs] = (dout * ohat * gainv * (sg + silu * (1.0 - sg))).astype(BF16)
            dy = dout * silu
            part = part + jnp.sum(dy * ohat, axis=0, keepdims=True)
            dn = dy * gainv
            do_ref[:, hs] = r * (dn - ohat * jnp.mean(dn * ohat, axis=-1, keepdims=True))

        @pl.when(pl.program_id(0) == 0)
        def _():
            dgain_ref[...] = part

        @pl.when(pl.program_id(0) > 0)
        def _():
            dgain_ref[...] += part

    return pl.pallas_call(
        body, name="gla_bwd_norm", grid=(SEQ // tr,),
        in_specs=[_row_spec(tr, w, CB_HG_G * LANE // w), _bcast_spec(HEAD_DIM), _row_spec(tr, w), _row_spec(tr, w)],
        out_specs=[_row_spec(tr, w), _row_spec(tr, w), _bcast_spec(HEAD_DIM)],
        out_shape=[jax.ShapeDtypeStruct((SEQ, w), F32), jax.ShapeDtypeStruct((SEQ, w), BF16),
                   jax.ShapeDtypeStruct((1, HEAD_DIM), F32)],
        compiler_params=_params(("arbitrary",), 16 * tr * w * 4),
    )(proj, gain, o_pre, do_hg)


def _da_residue_rows(r, n0, size, d):
    if d == 1:
        return pl.ds(pl.multiple_of(n0, 8), size)
    return pl.ds(r + n0 * d, size, stride=d)


def _da_rmsnorm(x, gain):
    r = lax.rsqrt(jnp.mean(x * x, axis=-1, keepdims=True) + RMS_EPS)
    return x * r, r, x * r * gain


def _da_scores(qn_scr, kn_scr, slope, i, ld):
    w0 = jnp.clip(i * DA_QB - DA_RADIUS, 0, ld - DA_WIN)
    w0 = pl.multiple_of(w0, DA_RADIUS)
    qrows = pl.ds(pl.multiple_of(i * DA_QB, DA_QB), DA_QB)
    win = pl.ds(w0, DA_WIN)
    qb = qn_scr[qrows, :]
    kw = kn_scr[win, :]
    s = _dot(qb, kw, _NT) * DA_SCALE
    qpos = i * DA_QB + lax.broadcasted_iota(jnp.int32, (DA_QB, DA_WIN), 0)
    kpos = w0 + lax.broadcasted_iota(jnp.int32, (DA_QB, DA_WIN), 1)
    arel = jnp.abs(kpos - qpos)
    s = s - slope * arel.astype(F32)
    s = jnp.where(arel <= DA_RADIUS, s, NEG_INF)
    return s, qb, kw, qrows, win


def _da_slopes(group):
    d = DA_DILATIONS[group]
    sl = _alibi_slopes(DA_HEADS)[4 * group:4 * group + 4] * d
    return jnp.asarray(np.broadcast_to(sl[:, None, None], (4, 1, LANE)).copy())


def _da_fwd(proj, gq, gk, group):
    d = DA_DILATIONS[group]
    ld = SEQ // d
    nqb = ld // DA_QB

    def body(q_ref, k_ref, v_ref, gq_ref, gk_ref, sl_ref, o_ref, lse_ref, qn_scr, kn_scr, v_scr):
        slope = sl_ref[:, 0:1]

        def residue(r, carry):
            sel = _da_residue_rows(r, 0, ld, d)
            qn_scr[...] = _da_rmsnorm(q_ref[sel, :], gq_ref[...])[2].astype(BF16)
            kn_scr[...] = _da_rmsnorm(k_ref[sel, :], gk_ref[...])[2].astype(BF16)
            v_scr[...] = v_ref[sel, :].astype(BF16)

            def block(i):
                s, _, _, _, win = _da_scores(qn_scr, kn_scr, slope, i, ld)
                m = jnp.max(s, axis=-1, keepdims=True)
                p = jnp.exp(s - m)
                l = jnp.sum(p, axis=-1, keepdims=True)
                return _dot(p, v_scr[win, :], _NN) / l, jnp.broadcast_to(m + jnp.log(l), (DA_QB, HEAD_DIM))

            def step(i, c2):
                blocks = [i + w * (nqb // DA_WAYS) for w in range(DA_WAYS)]
                for b, (o, lse) in zip(blocks, [block(b) for b in blocks]):
                    out = _da_residue_rows(r, b * DA_QB, DA_QB, d)
                    o_ref[out, :] = o
                    lse_ref[out, :] = lse
                return c2

            return lax.fori_loop(0, nqb // DA_WAYS, step, carry)

        lax.fori_loop(0, d, residue, 0)

    seq_spec = lambda base: pl.BlockSpec((SEQ, HEAD_DIM), lambda h: (0, base + 4 * group + h))
    out_spec = pl.BlockSpec((SEQ, HEAD_DIM), lambda h: (0, h))
    gain_spec = pl.BlockSpec((1, HEAD_DIM), lambda h: (0, 0))
    blk = SEQ * HEAD_DIM * 4
    return pl.pallas_call(
        body, name=f"da_fwd_g{group}", grid=(DA_HEADS_PER_GROUP,),
        in_specs=[seq_spec(CB_DA_Q), seq_spec(CB_DA_K), seq_spec(CB_DA_V), gain_spec, gain_spec,
                  pl.BlockSpec((None, 1, LANE), lambda h: (h, 0, 0))],
        out_specs=[out_spec, out_spec],
        out_shape=[jax.ShapeDtypeStruct((SEQ, DA_WIDTH), F32)] * 2,
        scratch_shapes=[pltpu.VMEM((ld, HEAD_DIM), BF16)] * 3,
        compiler_params=_params(("parallel",), 10 * blk + 2 * blk),
    )(proj, proj, proj, gq, gk, _da_slopes(group))


def _da_merge(os, lses, tr):
    w = DA_WIDTH

    def body(o0, o1, o2, l0, l1, l2, ob_ref, of_ref, lse_ref):
        la, lb_, lc = l0[...], l1[...], l2[...]
        m = jnp.maximum(jnp.maximum(la, lb_), lc)
        ea, eb, ec = jnp.exp(la - m), jnp.exp(lb_ - m), jnp.exp(lc - m)
        tot = ea + eb + ec
        o = (ea * o0[...] + eb * o1[...] + ec * o2[...]) / tot
        of_ref[...] = o
        ob_ref[...] = o.astype(BF16)
        lse_ref[...] = m + jnp.log(tot)

    return pl.pallas_call(
        body, name="da_merge", grid=(SEQ // tr,), in_specs=[_row_spec(tr, w)] * 6,
        out_specs=[_row_spec(tr, w)] * 3,
        out_shape=[jax.ShapeDtypeStruct((SEQ, w), BF16), jax.ShapeDtypeStruct((SEQ, w), F32),
                   jax.ShapeDtypeStruct((SEQ, w), F32)],
        compiler_params=_params(("parallel",), 24 * tr * w * 4),
    )(*os, *lses)


def _da_rowdot(do, o, tr):
    w = DA_WIDTH

    def body(do_ref, o_ref, out_ref):
        prod = do_ref[...] * o_ref[...]
        for h in range(w // HEAD_DIM):
            sl = slice(h * HEAD_DIM, (h + 1) * HEAD_DIM)
            out_ref[:, sl] = jnp.broadcast_to(jnp.sum(prod[:, sl], axis=-1, keepdims=True), (tr, HEAD_DIM))

    return pl.pallas_call(
        body, name="da_rowdot", grid=(SEQ // tr,), in_specs=[_row_spec(tr, w)] * 2,
        out_specs=_row_spec(tr, w), out_shape=jax.ShapeDtypeStruct((SEQ, w), F32),
        compiler_params=_params(("parallel",), 10 * tr * w * 4),
    )(do, o)


def _da_bwd(proj, gq, gk, do, lse, dd, group):
    d = DA_DILATIONS[group]
    ld = SEQ // d
    nqb = ld // DA_QB

    def body(q_ref, k_ref, v_ref, gq_ref, gk_ref, sl_ref, do_ref, lse_ref, dd_ref,
             dq_ref, dk_ref, dv_ref, dgq_ref, dgk_ref,
             qn_scr, kn_scr, v_scr, dqn_scr, dkn_scr, dvr_scr, dq_scr, dk_scr, dv_scr):
        gqv, gkv = gq_ref[...], gk_ref[...]
        slope = sl_ref[:, 0:1]

        def residue(r, carry):
            sel = _da_residue_rows(r, 0, ld, d)
            qn_scr[...] = _da_rmsnorm(q_ref[sel, :], gqv)[2].astype(BF16)
            kn_scr[...] = _da_rmsnorm(k_ref[sel, :], gkv)[2].astype(BF16)
            v_scr[...] = v_ref[sel, :].astype(BF16)
            dkn_scr[...] = jnp.zeros_like(dkn_scr)
            dvr_scr[...] = jnp.zeros_like(dvr_scr)

            def block(i):
                s, qb, kw, qrows, win = _da_scores(qn_scr, kn_scr, slope, i, ld)
                src = _da_residue_rows(r, i * DA_QB, DA_QB, d)
                p = jnp.exp(s - lse_ref[src, :][:, 0:1])
                dob = do_ref[src, :]
                dp = _dot(dob, v_scr[win, :], _NT)
                ds = p * (dp - dd_ref[src, :][:, 0:1]) * DA_SCALE
                return qrows, win, _dot(p, dob, _TN), _dot(ds, kw, _NN), _dot(ds, qb, _TN)

            def step(i, c2):
                blocks = [i + w * (nqb // DA_WAYS) for w in range(DA_WAYS)]
                for qrows, win, dv, dqn, dkn in [block(b) for b in blocks]:
                    dvr_scr[win, :] += dv
                    dqn_scr[qrows, :] = dqn
                    dkn_scr[win, :] += dkn
                return c2

            lax.fori_loop(0, nqb // DA_WAYS, step, 0)

            parts = []
            for x_ref, gv, dn_scr, dx_scr in ((q_ref, gqv, dqn_scr, dq_scr), (k_ref, gkv, dkn_scr, dk_scr)):
                hat, rstd, _ = _da_rmsnorm(x_ref[sel, :], gv)
                dn = dn_scr[...]
                dyg = dn * gv
                dx_scr[sel, :] = rstd * (dyg - hat * jnp.mean(dyg * hat, axis=-1, keepdims=True))
                parts.append(jnp.sum(dn * hat, axis=0, keepdims=True))
            dv_scr[sel, :] = dvr_scr[...]
            return carry[0] + parts[0], carry[1] + parts[1]

        zero = jnp.zeros((1, HEAD_DIM), F32)
        pq, pk = lax.fori_loop(0, d, residue, (zero, zero))

        @pl.when(pl.program_id(0) == 0)
        def _():
            dgq_ref[...] = pq
            dgk_ref[...] = pk

        @pl.when(pl.program_id(0) > 0)
        def _():
            dgq_ref[...] += pq
            dgk_ref[...] += pk

        dq_ref[...] = dq_scr[...].astype(BF16)
        dk_ref[...] = dk_scr[...].astype(BF16)
        dv_ref[...] = dv_scr[...].astype(BF16)

    seq_spec = lambda base: pl.BlockSpec((SEQ, HEAD_DIM), lambda h: (0, base + 4 * group + h))
    out_spec = pl.BlockSpec((SEQ, HEAD_DIM), lambda h: (0, h))
    gain_spec = pl.BlockSpec((1, HEAD_DIM), lambda h: (0, 0))
    oshape = jax.ShapeDtypeStruct((SEQ, DA_WIDTH), BF16)
    gshape = jax.ShapeDtypeStruct((1, HEAD_DIM), F32)
    blk = SEQ * HEAD_DIM * 4
    return pl.pallas_call(
        body, name=f"da_bwd_g{group}", grid=(DA_HEADS_PER_GROUP,),
        in_specs=[seq_spec(CB_DA_Q), seq_spec(CB_DA_K), seq_spec(CB_DA_V), gain_spec, gain_spec,
                  pl.BlockSpec((None, 1, LANE), lambda h: (h, 0, 0)), out_spec, out_spec, out_spec],
        out_specs=[out_spec, out_spec, out_spec, gain_spec, gain_spec],
        out_shape=[oshape, oshape, oshape, gshape, gshape],
        scratch_shapes=[pltpu.VMEM((ld, HEAD_DIM), BF16)] * 3 + [pltpu.VMEM((ld, HEAD_DIM), F32)] * 3
        + [pltpu.VMEM((SEQ, HEAD_DIM), F32)] * 3,
        compiler_params=_params(("arbitrary",), 12 * blk + 3 * blk + 3 * blk + 3 * blk),
    )(proj, proj, proj, gq, gk, _da_slopes(group), do, lse, dd)


def _mem_softmax(q, k, gq, gk):
    qhat, rq, qn = _da_rmsnorm(q, gq)
    khat, rk, kn = _da_rmsnorm(k, gk)
    s = _dot(qn, kn, _NT) * MEM_SCALE
    m = jnp.max(s, axis=-1, keepdims=True)
    e = jnp.exp(s - m)
    p = e / jnp.sum(e, axis=-1, keepdims=True)
    return p, (qhat, rq, qn), (khat, rk, kn)


def _mem_fwd(proj, kv, gq, gk, tq):
    def body(q_ref, k_ref, v_ref, gq_ref, gk_ref, o_ref):
        p, _, _ = _mem_softmax(q_ref[...], k_ref[...], gq_ref[...], gk_ref[...])
        o_ref[...] = _dot(p, v_ref[...], _NN).astype(BF16)

    gain_spec = pl.BlockSpec((1, HEAD_DIM), lambda h, i: (0, 0))
    return pl.pallas_call(
        body, name="mem_fwd", grid=(MEM_HEADS, SEQ // tq),
        in_specs=[pl.BlockSpec((tq, HEAD_DIM), lambda h, i: (i, CB_MEM_Q + h)),
                  pl.BlockSpec((N_MEM, HEAD_DIM), lambda h, i: (0, h)),
                  pl.BlockSpec((N_MEM, HEAD_DIM), lambda h, i: (0, MEM_HEADS + h)), gain_spec, gain_spec],
        out_specs=pl.BlockSpec((tq, HEAD_DIM), lambda h, i: (i, h)),
        out_shape=jax.ShapeDtypeStruct((SEQ, MEM_WIDTH), BF16),
        compiler_params=_params(("parallel", "parallel"), 16 * tq * N_MEM * 4),
    )(proj, kv, kv, gq, gk)


def _mem_bwd(proj, kv, gq, gk, do, tq):
    nq = SEQ // tq

    def body(q_ref, k_ref, v_ref, gq_ref, gk_ref, do_ref, dq_ref, dk_ref, dv_ref, dgq_ref, dgk_ref, dkn_scr):
        h, i = pl.program_id(0), pl.program_id(1)
        gqv, gkv = gq_ref[...], gk_ref[...]
        p, (qhat, rq, qn), (khat, rk, kn) = _mem_softmax(q_ref[...], k_ref[...], gqv, gkv)
        dob = do_ref[...]
        dvp = _dot(p, dob, _TN)
        dp = _dot(dob, v_ref[...], _NT)
        ds = p * (dp - jnp.sum(p * dp, axis=-1, keepdims=True)) * MEM_SCALE
        dqn = _dot(ds, kn, _NN)
        dknp = _dot(ds, qn, _TN)
        dyg = dqn * gqv
        dq_ref[...] = (rq * (dyg - qhat * jnp.mean(dyg * qhat, axis=-1, keepdims=True))).astype(BF16)
        dgq_part = jnp.sum(dqn * qhat, axis=0, keepdims=True)
        first = jnp.logical_and(h == 0, i == 0)

        @pl.when(first)
        def _():
            dgq_ref[...] = dgq_part

        @pl.when(jnp.logical_not(first))
        def _():
            dgq_ref[...] += dgq_part

        @pl.when(i == 0)
        def _():
            dv_ref[...] = dvp
            dkn_scr[...] = dknp

        @pl.when(i > 0)
        def _():
            dv_ref[...] += dvp
            dkn_scr[...] += dknp

        @pl.when(i == nq - 1)
        def _():
            dkn = dkn_scr[...]
            dkg = dkn * gkv
            dk_ref[...] = rk * (dkg - khat * jnp.mean(dkg * khat, axis=-1, keepdims=True))
            dgk_part = jnp.sum(dkn * khat, axis=0, keepdims=True)

            @pl.when(h == 0)
            def _():
                dgk_ref[...] = dgk_part

            @pl.when(h > 0)
            def _():
                dgk_ref[...] += dgk_part

    gain_spec = pl.BlockSpec((1, HEAD_DIM), lambda h, i: (0, 0))
    kvout = pl.BlockSpec((N_MEM, HEAD_DIM), lambda h, i: (0, h))
    return pl.pallas_call(
        body, name="mem_bwd", grid=(MEM_HEADS, nq),
        in_specs=[pl.BlockSpec((tq, HEAD_DIM), lambda h, i: (i, CB_MEM_Q + h)),
                  pl.BlockSpec((N_MEM, HEAD_DIM), lambda h, i: (0, h)),
                  pl.BlockSpec((N_MEM, HEAD_DIM), lambda h, i: (0, MEM_HEADS + h)), gain_spec, gain_spec,
                  pl.BlockSpec((tq, HEAD_DIM), lambda h, i: (i, h))],
        out_specs=[pl.BlockSpec((tq, HEAD_DIM), lambda h, i: (i, h)), kvout, kvout, gain_spec, gain_spec],
        out_shape=[jax.ShapeDtypeStruct((SEQ, MEM_WIDTH), BF16), jax.ShapeDtypeStruct((N_MEM, MEM_WIDTH), F32),
                   jax.ShapeDtypeStruct((N_MEM, MEM_WIDTH), F32), jax.ShapeDtypeStruct((1, HEAD_DIM), F32),
                   jax.ShapeDtypeStruct((1, HEAD_DIM), F32)],
        scratch_shapes=[pltpu.VMEM((N_MEM, HEAD_DIM), F32)],
        compiler_params=_params(("arbitrary", "arbitrary"), 24 * tq * N_MEM * 4),
    )(proj, kv, kv, gq, gk, do)


def _mesh_position():
    return lax.axis_index("x"), lax.axis_index("y"), lax.axis_index("c")


def _any_spec():
    return pl.BlockSpec(memory_space=pl.ANY)


def _all_gather(shards):
    n = len(shards)

    def body(*refs):
        ins, outs = refs[:n], refs[n:2 * n]
        send_sems, recv_sems, local_sems = refs[2 * n:]
        x, y, c = _mesh_position()
        me, sibling = (x, y, c), (x, y, 1 - c)
        chips = [(1 - x, y), (x, 1 - y), (1 - x, 1 - y)]

        def copy(w, k, block, to, src=None):
            px, py, pc = block
            rows = outs[w].at[4 * px + 2 * py + pc]
            return pltpu.make_async_remote_copy(
                src_ref=rows if src is None else src, dst_ref=rows,
                send_sem=send_sems.at[7 * w + k], recv_sem=recv_sems.at[7 * w + k],
                device_id=to, device_id_type=MESH)

        started = []
        for w in range(n):
            mine = pltpu.make_async_copy(ins[w], outs[w].at[4 * x + 2 * y + c], local_sems.at[w])
            mine.start()
            started.append(mine)
        sends = []
        for w in range(n):
            first = [copy(w, 0, me, sibling, src=ins[w])]
            first += [copy(w, 1 + j, me, (*chip, c), src=ins[w]) for j, chip in enumerate(chips)]
            for cp in first:
                cp.start()
            sends += first
        for w in range(n):
            for j, chip in enumerate(chips):
                copy(w, 1 + j, (*chip, c), me).wait_recv()
                passed = copy(w, 4 + j, (*chip, c), sibling)
                passed.start()
                sends.append(passed)
        for w in range(n):
            copy(w, 0, sibling, me).wait_recv()
            for j, chip in enumerate(chips):
                copy(w, 4 + j, (*chip, 1 - c), me).wait_recv()
        for cp in sends:
            cp.wait_send()
        for mine in started:
            mine.wait()

    return pl.pallas_call(
        body, name="weights_all_gather",
        in_specs=[_any_spec()] * n, out_specs=[_any_spec()] * n,
        out_shape=[jax.ShapeDtypeStruct((N_DEV,) + s.shape, s.dtype) for s in shards],
        scratch_shapes=[pltpu.SemaphoreType.DMA((7 * n,)), pltpu.SemaphoreType.DMA((7 * n,)),
                        pltpu.SemaphoreType.DMA((n,))],
    )(*shards)


def _chip_of(j, x, y):
    return (1 - x if j & 1 else x, 1 - y if j & 2 else y)


_HBM_SPEC = pl.BlockSpec(memory_space=pltpu.HBM)
_SEM_SPEC = pl.BlockSpec(memory_space=pltpu.SEMAPHORE)
_DATAFLOW_EFFECT = pltpu.SideEffectType.DATAFLOW_SIDE_EFFECTING
TOKEN_SHAPE = (8, D_MODEL)


def _copies_start(name, arrays, n_copies, plan):
    n = len(arrays)

    def body(*refs):
        send_sems, recv_sems, token = refs[n], refs[n + 1], refs[2 * n + 2]
        copies = plan(refs[:n])
        assert len(copies) == n_copies
        for k, (src, dst, dev) in enumerate(copies):
            pltpu.make_async_remote_copy(src_ref=src, dst_ref=dst, send_sem=send_sems.at[k], recv_sem=recv_sems.at[k],
                                         device_id=dev, device_id_type=MESH).start()
        token[...] = jnp.zeros_like(token)

    outs = pl.pallas_call(
        body, name=name,
        out_shape=(pltpu.SemaphoreType.DMA((n_copies,)), pltpu.SemaphoreType.DMA((n_copies,)),
                   *[pltpu.HBM(a.shape, a.dtype) for a in arrays], jax.ShapeDtypeStruct(TOKEN_SHAPE, F32)),
        in_specs=[_HBM_SPEC] * n,
        out_specs=(_SEM_SPEC, _SEM_SPEC, *[_HBM_SPEC] * n, pl.BlockSpec(memory_space=pltpu.VMEM)),
        input_output_aliases={i: i + 2 for i in range(n)},
        compiler_params=pltpu.CompilerParams(has_side_effects=_DATAFLOW_EFFECT),
    )(*[pltpu.with_memory_space_constraint(a, pltpu.HBM) for a in arrays])
    return outs[0], outs[1], list(outs[2:2 + n]), outs[2 + n]


def _copies_wait(name, send_sems, recv_sems, arrays, n_copies, plan, after):
    n = len(arrays)
    after = list(after) if isinstance(after, (list, tuple)) else [after]

    def body(*refs):
        send_ref, recv_ref = refs[n], refs[n + 1]
        copies = plan(refs[:n])
        assert len(copies) == n_copies
        for k, (src, dst, dev) in enumerate(copies):
            cp = pltpu.make_async_remote_copy(src_ref=src, dst_ref=dst, send_sem=send_ref.at[k], recv_sem=recv_ref.at[k],
                                              device_id=dev, device_id_type=MESH)
            cp.wait_send()
            cp.wait_recv()

    outs = pl.pallas_call(
        body, name=name, out_shape=tuple(pltpu.HBM(a.shape, a.dtype) for a in arrays),
        in_specs=[_HBM_SPEC] * n + [_SEM_SPEC, _SEM_SPEC] + [pl.BlockSpec(memory_space=pl.ANY)] * len(after),
        out_specs=tuple([_HBM_SPEC] * n), input_output_aliases={i: i for i in range(n)},
        compiler_params=pltpu.CompilerParams(has_side_effects=_DATAFLOW_EFFECT),
    )(*arrays, send_sems, recv_sems, *after)
    return list(outs)


def _after(small, token):
    return small if token is None else small + token[0:1, :small.shape[-1]]


def _gather_plan_out(n):
    def plan(refs):
        x, y, c = _mesh_position()
        me = 4 * x + 2 * y + c
        copies = []
        for w in range(n):
            land = refs[n + w].at[me]
            copies.append((refs[w], land, (x, y, 1 - c)))
            for j in range(1, 4):
                copies.append((refs[w], land, (*_chip_of(j, x, y), c)))
        return copies
    return plan


def _gather_plan_pass(n):
    def plan(refs):
        x, y, c = _mesh_position()
        copies = []
        for w in range(n):
            for j in range(1, 4):
                px, py = _chip_of(j, x, y)
                rows = refs[w].at[4 * px + 2 * py + c]
                copies.append((rows, rows, (x, y, 1 - c)))
        return copies
    return plan


def _reduce_plan_sibling(n):
    def plan(refs):
        x, y, c = _mesh_position()
        copies = []
        for w in range(n):
            for j in range(4):
                px, py = _chip_of(j, x, y)
                copies.append((refs[w].at[4 * px + 2 * py + (1 - c)], refs[n + w].at[j], (x, y, 1 - c)))
        return copies
    return plan


def _reduce_plan_chips(n):
    def plan(refs):
        x, y, c = _mesh_position()
        copies = []
        for w in range(n):
            for j in range(1, 4):
                copies.append((refs[w].at[j - 1], refs[n + w].at[j - 1], (*_chip_of(j, x, y), c)))
        return copies
    return plan


def _chip_partials(grad, recv, name, tr):
    _, rows, width = grad.shape

    def body(g_ref, r_ref, own_ref, other_ref):
        x, y, c = _mesh_position()
        for j in range(4):
            px, py = _chip_of(j, x, y)
            total = g_ref[4 * px + 2 * py + c].astype(F32) + r_ref[j].astype(F32)
            if j == 0:
                own_ref[...] = total
            else:
                other_ref[j - 1] = total.astype(BF16)

    return pl.pallas_call(
        body, name=name, grid=(rows // tr,),
        in_specs=[pl.BlockSpec((N_DEV, tr, width), lambda i: (0, i, 0)),
                  pl.BlockSpec((4, tr, width), lambda i: (0, i, 0))],
        out_specs=[pl.BlockSpec((tr, width), lambda i: (i, 0)), pl.BlockSpec((3, tr, width), lambda i: (0, i, 0))],
        out_shape=[jax.ShapeDtypeStruct((rows, width), F32), jax.ShapeDtypeStruct((3, rows, width), BF16)],
        compiler_params=_params(("parallel",), 2 * 20 * tr * width * 2 + 8 * tr * width * 4),
    )(grad, recv)


def _adamw_math(w, g, m, v):
    m = ADAM_B1 * m + (1.0 - ADAM_B1) * g
    v = ADAM_B2 * v + (1.0 - ADAM_B2) * (g * g)
    m_hat = m / (1.0 - ADAM_B1 ** ADAM_STEP)
    v_hat = v / (1.0 - ADAM_B2 ** ADAM_STEP)
    delta = -ADAM_LR * (m_hat / (jnp.sqrt(v_hat) + ADAM_EPS) + ADAM_WD * w)
    return delta, m, v


def _adamw_shard(own, recv, w, m, v, name, tr):
    rows, width = own.shape

    def body(own_ref, r_ref, w_ref, m_ref, v_ref, g_ref, d_ref, nm_ref, nv_ref):
        g = own_ref[...]
        for j in range(3):
            g = g + r_ref[j].astype(F32)
        g_ref[...] = g
        d_ref[...], nm_ref[...], nv_ref[...] = _adamw_math(w_ref[...], g, m_ref[...], v_ref[...])

    spec = pl.BlockSpec((tr, width), lambda i: (i, 0))
    shape = jax.ShapeDtypeStruct((rows, width), F32)
    return pl.pallas_call(
        body, name=name, grid=(rows // tr,),
        in_specs=[spec, pl.BlockSpec((3, tr, width), lambda i: (0, i, 0)), spec, spec, spec],
        out_specs=[spec] * 4, out_shape=[shape] * 4,
        compiler_params=_params(("parallel",), 22 * tr * width * 4),
    )(own, recv, w, m, v)


def _small_all_reduce_adamw(gpart, lbpack, wpack, mpack, vpack, after):
    def body(gp_ref, lb_ref, w_ref, m_ref, v_ref, after_ref, g_ref, d_ref, nm_ref, nv_ref, gath_ref, send_sems, recv_sems):
        del after_ref
        x, y, c = _mesh_position()
        me = 4 * x + 2 * y + c
        gath_ref[me] = gp_ref[...]
        copies = []
        for j in range(1, N_DEV):
            peer = (x ^ (j >> 2), y ^ ((j >> 1) & 1), c ^ (j & 1))
            cp = pltpu.make_async_remote_copy(
                src_ref=gp_ref, dst_ref=gath_ref.at[me], send_sem=send_sems.at[j - 1], recv_sem=recv_sems.at[j - 1],
                device_id=peer, device_id_type=MESH)
            cp.start()
            copies.append(cp)
        for cp in copies:
            cp.wait()
        tot = gath_ref[0]
        for s in range(1, N_DEV):
            tot = tot + gath_ref[s]
        lb = lb_ref[...]
        dl = tot[16:32, :] * lb * (1.0 - lb)
        g = jnp.concatenate([tot[0:16, :], dl[0:8, :], -dl[0:8, :], dl[8:16, :], -dl[8:16, :],
                             tot[32:SMALL_GRAD_ROWS, :]], axis=0)
        g_ref[...] = g
        d_ref[...], nm_ref[...], nv_ref[...] = _adamw_math(w_ref[...], g, m_ref[...], v_ref[...])

    vm = pl.BlockSpec(memory_space=pltpu.VMEM)
    shape = jax.ShapeDtypeStruct((SMALL_ROWS, LANE), F32)
    return pl.pallas_call(
        body, name="small_all_reduce_adamw", in_specs=[vm] * 5 + [_any_spec()], out_specs=[vm] * 4,
        out_shape=[shape] * 4,
        scratch_shapes=[pltpu.VMEM((N_DEV, SMALL_GRAD_ROWS, LANE), F32),
                        pltpu.SemaphoreType.DMA((N_DEV - 1,)), pltpu.SemaphoreType.DMA((N_DEV - 1,))],
    )(gpart, lbpack, wpack, mpack, vpack, after)


_SMALL_NAMES = ("norm_mix_gain", "norm_mem_gain", "lb_logits_fw", "lb_logits_bw", "norm_ffn_gain",
                "hg_norm_gain", "da_q_gain", "da_k_gain", "mem_q_gain", "mem_k_gain")
_SMALL_ROW0 = {"norm_mix_gain": 0, "norm_mem_gain": 8, "lb_logits_fw": 16, "lb_logits_bw": 32, "norm_ffn_gain": 48,
               "hg_norm_gain": 56, "da_q_gain": 64, "da_k_gain": 72, "mem_q_gain": 80, "mem_k_gain": 88}
_LOSS_ROW = 96


def _pack_rows(parts, total_rows):
    rows = []
    for p in parts:
        r = p.reshape(-1, LANE)
        rows.append(jnp.pad(r, ((0, -r.shape[0] % 8), (0, 0))))
    used = sum(r.shape[0] for r in rows)
    if total_rows > used:
        rows.append(jnp.zeros((total_rows - used, LANE), F32))
    return jnp.concatenate(rows, axis=0)


def _unpack_small(pack, like):
    out = {}
    for name in _SMALL_NAMES:
        n = like[name].size // LANE
        r0 = _SMALL_ROW0[name]
        out[name] = pack[r0:r0 + n].reshape(like[name].shape)
    return out


def kernel(x, mem, norm_mix_gain, norm_mem_gain, w_in, lb_logits_fw, lb_logits_bw, hg_norm_gain, da_q_gain, da_k_gain, w_mem_kv, mem_q_gain, mem_k_gain, w_proj_hg, w_proj_da, w_proj_mem, w_out, norm_ffn_gain, w_ffn_in, w_ffn_out, loss_target, m_norm_mix_gain, m_norm_mem_gain, m_w_in, m_lb_logits_fw, m_lb_logits_bw, m_hg_norm_gain, m_da_q_gain, m_da_k_gain, m_w_mem_kv, m_mem_q_gain, m_mem_k_gain, m_w_proj_hg, m_w_proj_da, m_w_proj_mem, m_w_out, m_norm_ffn_gain, m_w_ffn_in, m_w_ffn_out, v_norm_mix_gain, v_norm_mem_gain, v_w_in, v_lb_logits_fw, v_lb_logits_bw, v_hg_norm_gain, v_da_q_gain, v_da_k_gain, v_w_mem_kv, v_mem_q_gain, v_mem_k_gain, v_w_proj_hg, v_w_proj_da, v_w_proj_mem, v_w_out, v_norm_ffn_gain, v_w_ffn_in, v_w_ffn_out):
    small_w = dict(norm_mix_gain=norm_mix_gain, norm_mem_gain=norm_mem_gain, lb_logits_fw=lb_logits_fw,
                   lb_logits_bw=lb_logits_bw, norm_ffn_gain=norm_ffn_gain, hg_norm_gain=hg_norm_gain,
                   da_q_gain=da_q_gain, da_k_gain=da_k_gain, mem_q_gain=mem_q_gain, mem_k_gain=mem_k_gain)
    small_m = dict(norm_mix_gain=m_norm_mix_gain, norm_mem_gain=m_norm_mem_gain, lb_logits_fw=m_lb_logits_fw,
                   lb_logits_bw=m_lb_logits_bw, norm_ffn_gain=m_norm_ffn_gain, hg_norm_gain=m_hg_norm_gain,
                   da_q_gain=m_da_q_gain, da_k_gain=m_da_k_gain, mem_q_gain=m_mem_q_gain, mem_k_gain=m_mem_k_gain)
    small_v = dict(norm_mix_gain=v_norm_mix_gain, norm_mem_gain=v_norm_mem_gain, lb_logits_fw=v_lb_logits_fw,
                   lb_logits_bw=v_lb_logits_bw, norm_ffn_gain=v_norm_ffn_gain, hg_norm_gain=v_hg_norm_gain,
                   da_q_gain=v_da_q_gain, da_k_gain=v_da_k_gain, mem_q_gain=v_mem_q_gain, mem_k_gain=v_mem_k_gain)
    big_names = ("w_in", "w_mem_kv", "w_proj_hg", "w_proj_da", "w_proj_mem", "w_out", "w_ffn_in", "w_ffn_out")
    big_w = dict(w_in=w_in[0], w_mem_kv=w_mem_kv[0], w_proj_hg=w_proj_hg[0], w_proj_da=w_proj_da[0],
                 w_proj_mem=w_proj_mem[0], w_out=w_out[0], w_ffn_in=w_ffn_in[0], w_ffn_out=w_ffn_out[0])
    big_m = dict(w_in=m_w_in[0], w_mem_kv=m_w_mem_kv[0], w_proj_hg=m_w_proj_hg[0], w_proj_da=m_w_proj_da[0],
                 w_proj_mem=m_w_proj_mem[0], w_out=m_w_out[0], w_ffn_in=m_w_ffn_in[0], w_ffn_out=m_w_ffn_out[0])
    big_v = dict(w_in=v_w_in[0], w_mem_kv=v_w_mem_kv[0], w_proj_hg=v_w_proj_hg[0], w_proj_da=v_w_proj_da[0],
                 w_proj_mem=v_w_proj_mem[0], w_out=v_w_out[0], w_ffn_in=v_w_ffn_in[0], w_ffn_out=v_w_ffn_out[0])

    row_tile = dict(w_in=128, w_mem_kv=128, w_proj_hg=128, w_proj_da=512, w_proj_mem=512, w_out=128,
                    w_ffn_in=128, w_ffn_out=352)
    rest = _BIG_NAMES[1:]
    shards = [big_w[n].astype(BF16) for n in rest]
    state = {}
    big_out = {}

    def reduce_start(group, stacked):
        names = tuple(stacked)
        arrays = [stacked[n] for n in names] + [lax.empty((4,) + stacked[n].shape[1:], BF16) for n in names]
        plan = _reduce_plan_sibling(len(names))
        send, recv, thru, token = _copies_start(f"grads_{group}_sibling_start", arrays, 4 * len(names), plan)
        state[group] = dict(names=names, plan=plan, send=send, recv=recv, arrays=thru)
        return token

    def reduce_middle(group, after):
        st = state[group]
        names, k = st["names"], len(st["names"])
        thru = _copies_wait(f"grads_{group}_sibling_wait", st["send"], st["recv"], st["arrays"], 4 * k, st["plan"], after)
        partials = [_chip_partials(thru[i], thru[k + i], f"chip_partials_{n}", row_tile[n]) for i, n in enumerate(names)]
        arrays = [p[1] for p in partials] + [lax.empty(p[1].shape, BF16) for p in partials]
        plan = _reduce_plan_chips(k)
        send, recv, thru2, token = _copies_start(f"grads_{group}_chips_start", arrays, 3 * k, plan)
        state[group] = dict(names=names, plan=plan, send=send, recv=recv, arrays=thru2, own=[p[0] for p in partials])
        return token

    def reduce_finish(group, after):
        st = state.pop(group)
        names, k = st["names"], len(st["names"])
        thru = _copies_wait(f"grads_{group}_chips_wait", st["send"], st["recv"], st["arrays"], 3 * k, st["plan"], after)
        for i, n in enumerate(names):
            big_out[n] = _adamw_shard(st["own"][i], thru[k + i], big_w[n], big_m[n], big_v[n], "adamw_" + n, row_tile[n])

    win_st = _all_gather([big_w["w_in"].astype(BF16)])[0]
    step = _local_step_stages(x[0], mem[0], loss_target[0], small_w, win_st)
    event, payload = next(step)
    local = None
    while True:
        reply = None
        if event == "begin":
            nr = len(rest)
            me = 4 * lax.axis_index("x") + 2 * lax.axis_index("y") + lax.axis_index("c")
            arrays = shards + [lax.dynamic_update_slice(lax.empty((N_DEV,) + s.shape, BF16), s[None], (me, 0, 0))
                               for s in shards]
            plan = _gather_plan_out(nr)
            send, recv, thru, reply = _copies_start("weights_rest_out_start", arrays, 4 * nr, plan)
            state["gather"] = dict(plan=plan, send=send, recv=recv, arrays=thru)
        elif event == "after_gla_fwd":
            st = state["gather"]
            nr = len(rest)
            thru = _copies_wait("weights_rest_out_wait", st["send"], st["recv"], st["arrays"], 4 * nr, st["plan"], payload)
            plan = _gather_plan_pass(nr)
            send, recv, lands, reply = _copies_start("weights_rest_pass_start", thru[nr:], 3 * nr, plan)
            state["gather"] = dict(plan=plan, send=send, recv=recv, arrays=lands)
        elif event == "need_weights":
            st = state.pop("gather")
            nr = len(rest)
            lands = _copies_wait("weights_rest_pass_wait", st["send"], st["recv"], st["arrays"], 3 * nr, st["plan"], payload)
            reply = dict(zip(rest, lands))
        elif event == "grads_ffn":
            reply = reduce_start("ffn", payload)
        elif event == "after_gate_merge_bwd":
            reply = reduce_middle("ffn", payload)
        elif event == "grads_mix":
            reply = reduce_start("mix", payload)
        elif event == "after_da_bwd_g0":
            reply = reduce_middle("mix", payload)
            reduce_finish("ffn", payload)
        elif event == "after_gla_bwd":
            reduce_finish("mix", payload)
        elif event == "grads_in":
            reply = reduce_start("in", payload)
        elif event == "after_proj_bwd_act":
            reply = reduce_middle("in", payload)
        elif event == "end":
            local = payload
            reduce_finish("in", [local["grad_x"]] + [big_out[n][0] for n in rest])
            break
        event, payload = step.send(reply)
    grad_x = local["grad_x"]

    wpack = _pack_rows([small_w[n] for n in _SMALL_NAMES], SMALL_ROWS)
    mpack = _pack_rows([small_m[n] for n in _SMALL_NAMES], SMALL_ROWS)
    vpack = _pack_rows([small_v[n] for n in _SMALL_NAMES], SMALL_ROWS)
    gs, ds, ms, vs = _small_all_reduce_adamw(local["gpart"], local["lbpack"], wpack, mpack, vpack, big_out["w_in"][0])
    loss = gs[_LOSS_ROW, 0]
    small_out = [_unpack_small(t, small_w) for t in (gs, ds, ms, vs)]

    order = ("norm_mix_gain", "norm_mem_gain", "w_in", "lb_logits_fw", "lb_logits_bw", "hg_norm_gain", "da_q_gain",
             "da_k_gain", "w_mem_kv", "mem_q_gain", "mem_k_gain", "w_proj_hg", "w_proj_da", "w_proj_mem", "w_out",
             "norm_ffn_gain", "w_ffn_in", "w_ffn_out")
    outs = [loss, grad_x[None]]
    for kind in range(4):
        for n in order:
            outs.append(big_out[n][kind][None] if n in big_out else small_out[kind][n])
    return tuple(outs)


_BIG_NAMES = ("w_in", "w_mem_kv", "w_proj_hg", "w_proj_da", "w_proj_mem", "w_out", "w_ffn_in", "w_ffn_out")


def _local_step(xs, mems, target, sw, wg):
    step = _local_step_stages(xs, mems, target, sw, wg["w_in"])
    stacked = {}
    event, payload = next(step)
    while event != "end":
        if event.startswith("grads_"):
            stacked.update(payload)
        event, payload = step.send(wg if event == "need_weights" else None)
    return dict(payload, stacked=stacked)


def _local_step_stages(xs, mems, target, sw, win_st):
    norm_mix_gain, norm_mem_gain, norm_ffn_gain = sw["norm_mix_gain"], sw["norm_mem_gain"], sw["norm_ffn_gain"]
    lb_logits_fw, lb_logits_bw, hg_norm_gain = sw["lb_logits_fw"], sw["lb_logits_bw"], sw["hg_norm_gain"]
    da_q_gain, da_k_gain, mem_q_gain, mem_k_gain = sw["da_q_gain"], sw["da_k_gain"], sw["mem_q_gain"], sw["mem_k_gain"]

    token = yield "begin", None
    lb_fw = _lb_table(lb_logits_fw, "lb_table_fw")
    lb_bw = _lb_table(lb_logits_bw, "lb_table_bw")
    lb = jnp.concatenate([lb_fw, lb_bw], axis=0).reshape(2, HG_HEADS, 1, HEAD_DIM)
    h, h_t = _rmsnorm_fwd(xs, _after(norm_mix_gain, token), "norm_mix_fwd", 512, transposed=True)
    proj = _matmul(h, win_st, "nn", F32, 1024, IN_SHARD, D_MODEL, "proj_fwd", b_stacked=True, n_outer=True)
    o_hg, o_pre, states = _gla_fwd(proj, lb, hg_norm_gain)
    token = yield "after_gla_fwd", o_hg
    da = [_da_fwd(proj, _after(da_q_gain, token), da_k_gain, g) for g in range(3)]
    o_da, o_da32, lse_da = _da_merge([t[0] for t in da], [t[1] for t in da], 512)
    wg = yield "need_weights", o_da
    wkv = wg["w_mem_kv"].reshape(D_MODEL, 2 * MEM_WIDTH)
    wphg = wg["w_proj_hg"].reshape(D_MODEL, D_MODEL)
    wpda = jnp.transpose(wg["w_proj_da"], (1, 0, 2)).reshape(DA_WIDTH, D_MODEL)
    wpmem = jnp.transpose(wg["w_proj_mem"], (1, 0, 2)).reshape(MEM_WIDTH, D_MODEL)
    wout = wg["w_out"].reshape(D_MODEL, D_MODEL)
    wfin = jnp.transpose(wg["w_ffn_in"], (1, 0, 2)).reshape(D_MODEL, 2 * D_FF)
    wfout = wg["w_ffn_out"].reshape(D_FF, D_MODEL)
    mem_n = _rmsnorm_fwd(mems, norm_mem_gain, "norm_mem_fwd", N_MEM)
    kv = _matmul(mem_n, wkv, "nn", F32, N_MEM, 1024, D_MODEL, "mem_kv_fwd")
    o_mem = _mem_fwd(proj, kv, mem_q_gain, mem_k_gain, 512)
    t_hg = _matmul(o_hg, wphg, "nn", F32, 512, 1024, D_MODEL, "proj_hg_fwd")
    t_da = _matmul(o_da, wpda, "nn", F32, 512, 1024, DA_WIDTH, "proj_da_fwd")
    t_mem = _matmul(o_mem, wpmem, "nn", F32, 512, 1024, MEM_WIDTH, "proj_mem_fwd")
    merged = _gate_merge_fwd(proj, t_hg, t_da, t_mem, 256)
    u = _matmul(merged, wout, "nn", F32, 512, 1024, D_MODEL, "out_fwd")
    x1, h2, h2_t = _residual_rmsnorm_fwd(xs, u, norm_ffn_gain, "norm_ffn_fwd", 512)
    ab = _matmul(h2, wfin, "nn", F32, 1024, 1408, D_MODEL, "ffn_in_fwd", n_outer=True)
    act, act_t = _swiglu_fwd(ab, 256)
    z = _matmul(act, wfout, "nn", F32, 512, 1024, D_FF, "ffn_out_fwd")
    dy, dyb, loss_part = _loss_head(x1, z, target, 512)

    dact = _matmul(dyb, wfout, "nt", F32, 512, 1408, D_MODEL, "ffn_out_bwd_act")
    g_wfout = _matmul(act_t, dyb, "nn", BF16, 1408, 1024, 2048, "ffn_out_bwd_w")
    dab = _swiglu_bwd(ab, dact, 256)
    dh2 = _matmul(dab, wfin, "nt", F32, 1024, 1024, 2816, "ffn_in_bwd_act")
    g_wfin = _matmul(h2_t, dab, "nn", BF16, 1024, 1408, 2048, "ffn_in_bwd_w")
    token = yield "grads_ffn", dict(
        w_ffn_in=jnp.transpose(g_wfin.reshape(D_MODEL, N_DEV, 2 * D_FF // N_DEV), (1, 0, 2)),
        w_ffn_out=g_wfout.reshape(N_DEV, D_FF // N_DEV, D_MODEL))
    dx1, dx1b, g_norm_ffn = _rmsnorm_bwd(x1, dh2, dy, _after(norm_ffn_gain, token), "norm_ffn_bwd", 256)
    dmerged = _matmul(dx1b, wout, "nt", F32, 512, 1024, D_MODEL, "out_bwd_act")
    g_wout = _matmul(merged, dx1b, "tn", BF16, 1024, 1024, 2048, "out_bwd_w")
    dt_hg, dt_da, dt_mem, dgates = _gate_merge_bwd(proj, t_hg, t_da, t_mem, dmerged, 256)
    token = yield "after_gate_merge_bwd", dt_hg
    do_hg = _matmul(dt_hg, wphg, "nt", F32, 512, 1024, D_MODEL, "proj_hg_bwd_act", after=token)
    g_wphg = _matmul(o_hg, dt_hg, "tn", BF16, 1024, 1024, 2048, "proj_hg_bwd_w")
    do_da = _matmul(dt_da, wpda, "nt", F32, 512, DA_WIDTH, D_MODEL, "proj_da_bwd_act")
    g_wpda = _matmul(o_da, dt_da, "tn", BF16, DA_WIDTH, D_MODEL, 2048, "proj_da_bwd_w")
    do_mem = _matmul(dt_mem, wpmem, "nt", F32, 512, MEM_WIDTH, D_MODEL, "proj_mem_bwd_act")
    g_wpmem = _matmul(o_mem, dt_mem, "tn", BF16, MEM_WIDTH, D_MODEL, 2048, "proj_mem_bwd_w")
    by_owner = lambda g: jnp.transpose(g.reshape(g.shape[0], N_DEV, D_MODEL // N_DEV), (1, 0, 2))
    g_wpda, g_wpmem = by_owner(g_wpda), by_owner(g_wpmem)

    dmemq, dk_mem, dv_mem, g_mem_q, g_mem_k = _mem_bwd(proj, kv, mem_q_gain, mem_k_gain, do_mem, 512)
    dkv = jnp.concatenate([dk_mem, dv_mem], axis=1).astype(BF16)
    g_wkv = _matmul(mem_n, dkv, "tn", BF16, 1024, 1024, N_MEM, "mem_kv_bwd_w")
    dmem_n = _matmul(dkv, wkv, "nt", F32, N_MEM, 1024, 1024, "mem_kv_bwd_act")
    g_norm_mem = _gain_grad(mems, dmem_n, "norm_mem_bwd")
    token = yield "grads_mix", dict(
        w_mem_kv=g_wkv.reshape(N_DEV, D_MODEL // N_DEV, 2 * MEM_WIDTH),
        w_proj_hg=g_wphg.reshape(N_DEV, D_MODEL // N_DEV, D_MODEL),
        w_proj_da=g_wpda, w_proj_mem=g_wpmem,
        w_out=g_wout.reshape(N_DEV, D_MODEL // N_DEV, D_MODEL))

    dd_da = _da_rowdot(do_da, o_da32, 512)
    da_b = [_da_bwd(proj, _after(da_q_gain, token), da_k_gain, do_da, lse_da, dd_da, 0)]
    token = yield "after_da_bwd_g0", da_b[0][0]
    da_b += [_da_bwd(proj, _after(da_q_gain, token), da_k_gain, do_da, lse_da, dd_da, g) for g in (1, 2)]
    g_da_q = da_b[0][3] + da_b[1][3] + da_b[2][3]
    g_da_k = da_b[0][4] + da_b[1][4] + da_b[2][4]

    dq_hg, df_hg, dv_hg, dg_hg, dlb, g_hg_norm = _gla_bwd(proj, lb, hg_norm_gain, o_pre, states, do_hg)
    yield "after_gla_bwd", dq_hg

    dproj = jnp.concatenate(
        [dq_hg, df_hg[0], df_hg[1], dv_hg, dg_hg]
        + [t[0] for t in da_b] + [t[1] for t in da_b] + [t[2] for t in da_b] + [dmemq, dgates], axis=1)
    g_win = _matmul(h_t, dproj, "nn", BF16, 1024, IN_SHARD, 2048, "proj_bwd_w", out_stacked=True)
    token = yield "grads_in", dict(w_in=g_win)
    dh = _matmul(dproj, win_st, "nt", F32, 1024, 1024, IN_SHARD, "proj_bwd_act", b_stacked=True, after=token)
    token = yield "after_proj_bwd_act", dh
    grad_x, _, g_norm_mix = _rmsnorm_bwd(xs, dh, dx1, _after(norm_mix_gain, token), "norm_mix_bwd", 256)

    gpart = _pack_rows([g_norm_mix, g_norm_mem, dlb[0], dlb[1], g_norm_ffn, g_hg_norm, g_da_q, g_da_k,
                        g_mem_q, g_mem_k, loss_part], SMALL_GRAD_ROWS)
    lbpack = jnp.concatenate([lb_fw.reshape(8, LANE), lb_bw.reshape(8, LANE)], axis=0)
    yield "end", dict(grad_x=grad_x, gpart=gpart, lbpack=lbpack)
```

```python
import numpy as np
import jax
import jax.numpy as jnp
from jax import lax
from jax.experimental import pallas as pl
from jax.experimental.pallas import tpu as pltpu

F32 = jnp.float32
BF16 = jnp.bfloat16
MESH = pl.DeviceIdType.MESH

SEQ = 4096
D_MODEL = 1024
N_DEV = 8
N_MEM = 256
RMS_EPS = 1e-6
NEG_INF = -1e30
LANE = 128
HEAD_DIM = 128
HG_HEADS = 8
HG_CHUNK = 64
HG_SCALE = HEAD_DIM ** -0.5
DA_DILATIONS = (1, 4, 16)
DA_RADIUS = 64
DA_HEADS_PER_GROUP = 4
DA_HEADS = 12
DA_WIDTH = 512
DA_SCALE = HEAD_DIM ** -0.5
DA_QB = 128
DA_WIN = 256
DA_WAYS = 2
MEM_HEADS = 4
MEM_WIDTH = 512
MEM_SCALE = HEAD_DIM ** -0.5
D_FF = 2816
IN_COLS = 13312
IN_SHARD = IN_COLS // N_DEV
CB_HG_Q, CB_F, CB_HG_I, CB_HG_G = 0, 8, 24, 32
CB_DA_Q, CB_DA_K, CB_DA_V, CB_MEM_Q = 40, 52, 64, 76
N_CB = IN_COLS // LANE
ADAM_LR, ADAM_B1, ADAM_B2, ADAM_EPS, ADAM_WD, ADAM_STEP = 0.001, 0.9, 0.999, 1e-08, 0.01, 10
VMEM_BYTES_V7X = 64 * 1024 * 1024
SMALL_ROWS = 104
SMALL_GRAD_ROWS = 88

_NN = (((1,), (0,)), ((), ()))
_NT = (((1,), (1,)), ((), ()))
_TN = (((0,), (0,)), ((), ()))


def _dot(a, b, dims):
    return lax.dot_general(a.astype(BF16), b.astype(BF16), dims, preferred_element_type=F32)


def _dot_exact(a, b, dims):
    return lax.dot_general(a, b, dims, precision=lax.Precision.HIGHEST, preferred_element_type=F32)


def _sigmoid(x):
    return 0.5 * jnp.tanh(0.5 * x) + 0.5


def _params(semantics, est_bytes):
    limit = int(min(VMEM_BYTES_V7X - (6 << 20), max(56 << 20, est_bytes * 3 // 2)))
    return pltpu.CompilerParams(dimension_semantics=semantics, vmem_limit_bytes=limit)


def _nbytes(shape, dtype):
    return int(np.prod(shape)) * jnp.dtype(dtype).itemsize


def _alibi_slopes(n):
    return (2.0 ** (-8.0 * np.arange(1, n + 1) / n)).astype(np.float32)


def _matmul(a, b, mode, out_dtype, tm, tn, tk, name, b_stacked=False, out_stacked=False, n_outer=False, after=None,
            m_blocks=None, out_into=None):
    if mode == "tn":
        kdim, m = a.shape
    else:
        m, kdim = a.shape
    if b_stacked:
        if mode == "nn":
            n = b.shape[0] * b.shape[2]
            assert tn == b.shape[2] and tk == kdim == b.shape[1]
        else:
            assert mode == "nt" and tk == b.shape[2] and b.shape[0] * tk == kdim
            n = b.shape[1]
    else:
        n = b.shape[0] if mode == "nt" else b.shape[1]
    assert m % tm == 0 and n % tn == 0 and kdim % tk == 0
    gm, gn, gk = m // tm, n // tn, kdim // tk
    i0 = 0
    if m_blocks is not None:
        assert mode != "tn" and not out_stacked
        i0, gm = m_blocks

    def ijk(f):
        if n_outer:
            return lambda j, i, k: f(i + i0, j, k)
        return lambda i, j, k: f(i + i0, j, k)

    if mode == "tn":
        a_spec = pl.BlockSpec((tk, tm), ijk(lambda i, j, k: (k, i)))
    else:
        a_spec = pl.BlockSpec((tm, tk), ijk(lambda i, j, k: (i, k)))
    if b_stacked and mode == "nn":
        b_spec = pl.BlockSpec((None, tk, tn), ijk(lambda i, j, k: (j, 0, 0)))
    elif b_stacked:
        b_spec = pl.BlockSpec((None, tn, tk), ijk(lambda i, j, k: (k, j, 0)))
    elif mode == "nt":
        b_spec = pl.BlockSpec((tn, tk), ijk(lambda i, j, k: (j, k)))
    else:
        b_spec = pl.BlockSpec((tk, tn), ijk(lambda i, j, k: (k, j)))
    if out_stacked:
        assert tm == m
        out_shape = jax.ShapeDtypeStruct((gn, m, tn), out_dtype)
        o_spec = pl.BlockSpec((None, tm, tn), ijk(lambda i, j, k: (j, i, 0)))
    else:
        out_shape = jax.ShapeDtypeStruct((m, n), out_dtype)
        o_spec = pl.BlockSpec((tm, tn), ijk(lambda i, j, k: (i, j)))
    dims = {"nn": _NN, "nt": _NT, "tn": _TN}[mode]

    n_in = 2 + (after is not None) + (out_into is not None)

    def body(*refs):
        a_ref, b_ref, o_ref = refs[0], refs[1], refs[n_in]
        part = _dot(a_ref[...], b_ref[...], dims)
        if gk == 1:
            o_ref[...] = part.astype(out_dtype)
            return
        acc_ref = refs[-1]
        k = pl.program_id(2)

        @pl.when(k == 0)
        def _():
            acc_ref[...] = part

        @pl.when(jnp.logical_and(k > 0, k < gk - 1))
        def _():
            acc_ref[...] += part

        @pl.when(k == gk - 1)
        def _():
            o_ref[...] = (acc_ref[...] + part).astype(out_dtype)

    a_tile = _nbytes((tm, tk), a.dtype)
    b_tile = _nbytes((tk, tn), b.dtype)
    o_tile = _nbytes((tm, tn), out_dtype)
    est = 2 * (a_tile + b_tile + o_tile) + 3 * tm * tn * 4 + (a_tile + b_tile)
    grid = (gn, gm, gk) if n_outer else (gm, gn, gk)
    operands, in_specs = [a, b], [a_spec, b_spec]
    if after is not None:
        operands.append(after)
        in_specs.append(pl.BlockSpec(memory_space=pl.ANY))
    aliases = {}
    if out_into is not None:
        aliases = {len(operands): 0}
        operands.append(out_into)
        in_specs.append(pl.BlockSpec(memory_space=pl.ANY))
    return pl.pallas_call(
        body, name=name, grid=grid, in_specs=in_specs, out_specs=o_spec, out_shape=out_shape,
        scratch_shapes=[] if gk == 1 else [pltpu.VMEM((tm, tn), F32)], input_output_aliases=aliases,
        compiler_params=_params(("parallel", "parallel", "arbitrary"), est),
    )(*operands)


def _row_spec(tr, width, col_block=0):
    return pl.BlockSpec((tr, width), lambda i: (i, col_block))


def _bcast_spec(width):
    return pl.BlockSpec((1, width), lambda i: (0, 0))


def _col_spec(width, tr):
    return pl.BlockSpec((width, tr), lambda i: (0, i))


def _rmsnorm_fwd(x, gain, name, tr, transposed=False):
    rows, width = x.shape

    def body(x_ref, g_ref, o_ref, *t_ref):
        xv = x_ref[...]
        r = lax.rsqrt(jnp.mean(xv * xv, axis=-1, keepdims=True) + RMS_EPS)
        h = xv * r * g_ref[...]
        o_ref[...] = h.astype(BF16)
        if transposed:
            t_ref[0][...] = h.T.astype(BF16)

    out_specs, out_shape = [_row_spec(tr, width)], [jax.ShapeDtypeStruct((rows, width), BF16)]
    if transposed:
        out_specs.append(_col_spec(width, tr))
        out_shape.append(jax.ShapeDtypeStruct((width, rows), BF16))
    out = pl.pallas_call(
        body, name=name, grid=(rows // tr,), in_specs=[_row_spec(tr, width), _bcast_spec(width)],
        out_specs=out_specs, out_shape=out_shape,
        compiler_params=_params(("parallel",), 10 * tr * width * 4),
    )(x, gain)
    return out if transposed else out[0]


def _residual_rmsnorm_fwd(x, u, gain, name, tr):
    rows, width = x.shape

    def body(x_ref, u_ref, g_ref, x1_ref, h_ref, ht_ref):
        xv = x_ref[...] + u_ref[...]
        x1_ref[...] = xv
        r = lax.rsqrt(jnp.mean(xv * xv, axis=-1, keepdims=True) + RMS_EPS)
        h = xv * r * g_ref[...]
        h_ref[...] = h.astype(BF16)
        ht_ref[...] = h.T.astype(BF16)

    return pl.pallas_call(
        body, name=name, grid=(rows // tr,),
        in_specs=[_row_spec(tr, width), _row_spec(tr, width), _bcast_spec(width)],
        out_specs=[_row_spec(tr, width), _row_spec(tr, width), _col_spec(width, tr)],
        out_shape=[jax.ShapeDtypeStruct((rows, width), F32), jax.ShapeDtypeStruct((rows, width), BF16),
                   jax.ShapeDtypeStruct((width, rows), BF16)],
        compiler_params=_params(("parallel",), 14 * tr * width * 4),
    )(x, u, gain)


def _rmsnorm_bwd(x, dh, dres, gain, name, tr):
    rows, width = x.shape

    def body(x_ref, dh_ref, dres_ref, g_ref, dx_ref, dxb_ref, dg_ref):
        xv = x_ref[...]
        r = lax.rsqrt(jnp.mean(xv * xv, axis=-1, keepdims=True) + RMS_EPS)
        xhat = xv * r
        dhv = dh_ref[...]
        dyg = dhv * g_ref[...]
        dx = dres_ref[...] + r * (dyg - xhat * jnp.mean(dyg * xhat, axis=-1, keepdims=True))
        dx_ref[...] = dx
        dxb_ref[...] = dx.astype(BF16)
        part = jnp.sum(dhv * xhat, axis=0, keepdims=True)

        @pl.when(pl.program_id(0) == 0)
        def _():
            dg_ref[...] = part

        @pl.when(pl.program_id(0) > 0)
        def _():
            dg_ref[...] += part

    return pl.pallas_call(
        body, name=name, grid=(rows // tr,),
        in_specs=[_row_spec(tr, width), _row_spec(tr, width), _row_spec(tr, width), _bcast_spec(width)],
        out_specs=[_row_spec(tr, width), _row_spec(tr, width), _bcast_spec(width)],
        out_shape=[jax.ShapeDtypeStruct((rows, width), F32), jax.ShapeDtypeStruct((rows, width), BF16),
                   jax.ShapeDtypeStruct((1, width), F32)],
        compiler_params=_params(("arbitrary",), 14 * tr * width * 4),
    )(x, dh, dres, gain)


def _gain_grad(x, dh, name):
    rows, width = x.shape

    def body(x_ref, dh_ref, dg_ref):
        xv = x_ref[...]
        r = lax.rsqrt(jnp.mean(xv * xv, axis=-1, keepdims=True) + RMS_EPS)
        dg_ref[...] = jnp.sum(dh_ref[...] * xv * r, axis=0, keepdims=True)

    return pl.pallas_call(
        body, name=name, grid=(1,), in_specs=[_row_spec(rows, width), _row_spec(rows, width)],
        out_specs=_bcast_spec(width), out_shape=jax.ShapeDtypeStruct((1, width), F32),
        compiler_params=_params(("arbitrary",), 6 * rows * width * 4),
    )(x, dh)


def _lb_table(logits, name):
    slots, width = logits.shape

    def body(l_ref, o_ref):
        lv = l_ref[...]
        mx = jnp.max(lv, axis=0, keepdims=True)
        e = jnp.exp(lv - mx)
        o_ref[...] = e[0:1, :] / jnp.sum(e, axis=0, keepdims=True)

    return pl.pallas_call(
        body, name=name, grid=(1,), in_specs=[pl.BlockSpec((slots, width), lambda i: (0, 0))],
        out_specs=_bcast_spec(width), out_shape=jax.ShapeDtypeStruct((1, width), F32),
    )(logits)


def _gate_merge_fwd(proj, t_hg, t_da, t_mem, tr):
    w = D_MODEL

    def body(ghg_ref, gda_ref, gmem_ref, thg_ref, tda_ref, tmem_ref, o_ref):
        acc = _sigmoid(ghg_ref[...]) * thg_ref[...]
        acc += _sigmoid(gda_ref[...]) * tda_ref[...]
        acc += _sigmoid(gmem_ref[...]) * tmem_ref[...]
        o_ref[...] = acc.astype(BF16)

    return pl.pallas_call(
        body, name="gate_merge_fwd", grid=(SEQ // tr,),
        in_specs=[_row_spec(tr, w, 10), _row_spec(tr, w, 11), _row_spec(tr, w, 12),
                  _row_spec(tr, w), _row_spec(tr, w), _row_spec(tr, w)],
        out_specs=_row_spec(tr, w), out_shape=jax.ShapeDtypeStruct((SEQ, w), BF16),
        compiler_params=_params(("parallel",), 16 * tr * w * 4),
    )(proj, proj, proj, t_hg, t_da, t_mem)


def _gate_merge_bwd(proj, t_hg, t_da, t_mem, dmerged, tr):
    w = D_MODEL

    def body(ghg_ref, gda_ref, gmem_ref, thg_ref, tda_ref, tmem_ref, dm_ref, dthg_ref, dtda_ref, dtmem_ref, dg_ref):
        dm = dm_ref[...]
        for b, (g_ref, t_ref, dt_ref) in enumerate(((ghg_ref, thg_ref, dthg_ref), (gda_ref, tda_ref, dtda_ref),
                                                    (gmem_ref, tmem_ref, dtmem_ref))):
            s = _sigmoid(g_ref[...])
            dt_ref[...] = (s * dm).astype(BF16)
            dg_ref[:, b * w:(b + 1) * w] = (dm * t_ref[...] * s * (1.0 - s)).astype(BF16)

    return pl.pallas_call(
        body, name="gate_merge_bwd", grid=(SEQ // tr,),
        in_specs=[_row_spec(tr, w, 10), _row_spec(tr, w, 11), _row_spec(tr, w, 12),
                  _row_spec(tr, w), _row_spec(tr, w), _row_spec(tr, w), _row_spec(tr, w)],
        out_specs=[_row_spec(tr, w), _row_spec(tr, w), _row_spec(tr, w), _row_spec(tr, 3 * w)],
        out_shape=[jax.ShapeDtypeStruct((SEQ, w), BF16)] * 3 + [jax.ShapeDtypeStruct((SEQ, 3 * w), BF16)],
        compiler_params=_params(("parallel",), 24 * tr * w * 4),
    )(proj, proj, proj, t_hg, t_da, t_mem, dmerged)


def _ffn_in_swiglu(h2, wfin, tm, tn):
    gm, gn = SEQ // tm, D_FF // tn

    def body(h_ref, wa_ref, wb_ref, a_ref, b_ref, act_ref, actt_ref):
        h = h_ref[...]
        a = _dot(h, wa_ref[...], _NN)
        b = _dot(h, wb_ref[...], _NN)
        act = a * _sigmoid(a) * b
        a_ref[...] = a.astype(BF16)
        b_ref[...] = b.astype(BF16)
        act_ref[...] = act.astype(BF16)
        actt_ref[...] = act.T.astype(BF16)

    tile = pl.BlockSpec((tm, tn), lambda j, i: (i, j))
    shape = jax.ShapeDtypeStruct((SEQ, D_FF), BF16)
    return pl.pallas_call(
        body, name="ffn_in_swiglu_fwd", grid=(gn, gm),
        in_specs=[pl.BlockSpec((tm, D_MODEL), lambda j, i: (i, 0)),
                  pl.BlockSpec((D_MODEL, tn), lambda j, i: (0, j)),
                  pl.BlockSpec((D_MODEL, tn), lambda j, i: (0, gn + j))],
        out_specs=[tile, tile, tile, pl.BlockSpec((tn, tm), lambda j, i: (j, i))],
        out_shape=[shape, shape, shape, jax.ShapeDtypeStruct((D_FF, SEQ), BF16)],
        compiler_params=_params(("parallel", "parallel"), 4 * tm * D_MODEL + 8 * D_MODEL * tn + 16 * tm * tn
                                + 6 * tm * tn * 4),
    )(h2, wfin, wfin)


def _swiglu_bwd(a, b, dact, tr):
    def body(a_ref, b_ref, d_ref, o_ref):
        av = a_ref[...].astype(F32)
        bv = b_ref[...].astype(F32)
        d = d_ref[...]
        s = _sigmoid(av)
        silu = av * s
        o_ref[:, :D_FF] = (d * bv * (s + silu * (1.0 - s))).astype(BF16)
        o_ref[:, D_FF:] = (d * silu).astype(BF16)

    return pl.pallas_call(
        body, name="swiglu_bwd", grid=(SEQ // tr,), in_specs=[_row_spec(tr, D_FF)] * 3,
        out_specs=_row_spec(tr, 2 * D_FF), out_shape=jax.ShapeDtypeStruct((SEQ, 2 * D_FF), BF16),
        compiler_params=_params(("parallel",), 10 * tr * 2 * D_FF * 4),
    )(a, b, dact)


def _loss_head(x1, z, target, tr):
    w = D_MODEL

    def body(x_ref, z_ref, t_ref, dy_ref, dyb_ref, loss_ref, acc_ref):
        err = x_ref[...] + z_ref[...] - t_ref[...]
        dy = err * (1.0 / w)
        dy_ref[...] = dy
        dyb_ref[...] = dy.astype(BF16)
        part = jnp.sum(err * err, axis=0, keepdims=True)

        @pl.when(pl.program_id(0) == 0)
        def _():
            acc_ref[...] = part

        @pl.when(pl.program_id(0) > 0)
        def _():
            acc_ref[...] += part

        @pl.when(pl.program_id(0) == SEQ // tr - 1)
        def _():
            total = jnp.sum(acc_ref[...], axis=1, keepdims=True) * (0.5 / w)
            loss_ref[...] = jnp.broadcast_to(total, (1, LANE))

    return pl.pallas_call(
        body, name="loss_head", grid=(SEQ // tr,),
        in_specs=[_row_spec(tr, w), _row_spec(tr, w), _row_spec(tr, w)],
        out_specs=[_row_spec(tr, w), _row_spec(tr, w), _bcast_spec(LANE)],
        out_shape=[jax.ShapeDtypeStruct((SEQ, w), F32), jax.ShapeDtypeStruct((SEQ, w), BF16),
                   jax.ShapeDtypeStruct((1, LANE), F32)],
        scratch_shapes=[pltpu.VMEM((1, w), F32)],
        compiler_params=_params(("arbitrary",), 12 * tr * w * 4),
    )(x1, z, target)


GLA_ROWS = 256
GLA_CPB = GLA_ROWS // HG_CHUNK
GLA_NBLK = SEQ // GLA_ROWS
GLA_WAYS = 4
GLA_TRIPS = GLA_NBLK // GLA_WAYS
GLA_GRAD_WAYS = 4
GLA_GRAD_TRIPS = GLA_NBLK // GLA_GRAD_WAYS
GLA_NCK = SEQ // HG_CHUNK


def _dot_split(m, xv, dims):
    hi = xv.astype(BF16)
    r1 = xv - hi.astype(F32)
    mid = r1.astype(BF16)
    lo = (r1 - mid.astype(F32)).astype(BF16)
    dot = lambda t: lax.dot_general(m, t, dims, preferred_element_type=F32)
    return (dot(lo) + dot(mid)) + dot(hi)


def _gla_block(qc, fc, lbv, masks):
    mask, maskb, direction = masks
    sq = _sigmoid(qc)
    q = qc * sq * HG_SCALE
    sf = _sigmoid(fc)
    forget = lbv + (1.0 - lbv) * sf
    k = 1.0 - forget
    logf = jnp.log(forget)
    b = _dot_split(maskb, logf, _NN)
    ends = []
    for j in range(GLA_CPB):
        lo, hi = j * HG_CHUNK, (j + 1) * HG_CHUNK
        end = jnp.where(direction == 0, b[hi - 1:hi, :], b[lo:lo + 1, :])
        ends.append(jnp.broadcast_to(end, (HG_CHUNK, HEAD_DIM)))
    bt = jnp.concatenate(ends, axis=0)
    eb = jnp.exp(b)
    qt = q * eb
    kt = k * jnp.exp(-b)
    kh = k * jnp.exp(bt - b)
    a = jnp.where(mask, _dot(qt, kt, _NT), 0.0)
    return dict(sq=sq, sf=sf, forget=forget, k=k, b=b, bt=bt, eb=eb, qt=qt, kt=kt, kh=kh, a=a)


def _gla_masks(direction):
    row = lax.broadcasted_iota(jnp.int32, (GLA_ROWS, GLA_ROWS), 0)
    col = lax.broadcasted_iota(jnp.int32, (GLA_ROWS, GLA_ROWS), 1)
    same = (row // HG_CHUNK) == (col // HG_CHUNK)
    mask = jnp.logical_and(same, jnp.where(direction == 0, row - col, col - row) >= 0)
    return mask, jnp.where(mask, 1.0, 0.0).astype(BF16), direction


def _gla_chunk_rows(j):
    return slice(j * HG_CHUNK, (j + 1) * HG_CHUNK)


def _gla_block_rows(b):
    return pl.ds(pl.multiple_of(b * GLA_ROWS, GLA_ROWS), GLA_ROWS)


def _head_spec(col_block0):
    return pl.BlockSpec((SEQ, HEAD_DIM), lambda h, d: (0, col_block0 + h))


def _gla_fwd(proj, lb, gain):
    nck = GLA_NCK

    def body(q_ref, f_ref, v_ref, g_ref, lb_ref, gain_ref, ohg_ref, opre_ref, st_ref, qt_scr, cs_scr, dec_scr):
        d = pl.program_id(1)
        masks = _gla_masks(d)
        lbv = lb_ref[...]

        @pl.when(d == 0)
        def _():
            opre_ref[...] = jnp.zeros_like(opre_ref)

        def intra(s, carry):
            blocks = [s + w * GLA_TRIPS for w in range(GLA_WAYS)]
            rows = [_gla_block_rows(b) for b in blocks]
            loaded = [(q_ref[r, :], f_ref[r, :], v_ref[r, :], opre_ref[r, :]) for r in rows]
            results = []
            for qc, fc, v, o_prev in loaded:
                ck = _gla_block(qc, fc, lbv, masks)
                o = o_prev + _dot(ck["a"], v, _NN)
                cs = [_dot(v[_gla_chunk_rows(j), :], ck["kh"][_gla_chunk_rows(j), :], _TN) for j in range(GLA_CPB)]
                dec = [jnp.exp(ck["bt"][j * HG_CHUNK:j * HG_CHUNK + 8, :]) for j in range(GLA_CPB)]
                results.append((o, ck["qt"].astype(BF16), cs, dec))
            for b, r, (o, qt, cs, dec) in zip(blocks, rows, results):
                opre_ref[r, :] = o
                qt_scr[r, :] = qt
                for j in range(GLA_CPB):
                    cs_scr[b * GLA_CPB + j] = cs[j]
                    dec_scr[b * GLA_CPB + j] = dec[j]
            return carry

        lax.fori_loop(0, GLA_TRIPS, intra, 0)

        def scan(i, st):
            c = jnp.where(d == 0, i, nck - 1 - i)
            st_ref[c] = st
            return st * dec_scr[c][0:1, :] + cs_scr[c]

        lax.fori_loop(0, nck, scan, jnp.zeros((HEAD_DIM, HEAD_DIM), F32), unroll=4)

        def inter(c, carry):
            rows = pl.ds(pl.multiple_of(c * HG_CHUNK, HG_CHUNK), HG_CHUNK)
            opre_ref[rows, :] += _dot(qt_scr[rows, :], st_ref[c], _NT)
            return carry

        lax.fori_loop(0, nck, inter, 0, unroll=4)

        @pl.when(d == 1)
        def _():
            o = opre_ref[...]
            r = lax.rsqrt(jnp.mean(o * o, axis=-1, keepdims=True) + RMS_EPS)
            g = g_ref[...]
            ohg_ref[...] = (o * r * gain_ref[...] * (g * _sigmoid(g))).astype(BF16)

    blk = SEQ * HEAD_DIM * 4
    return pl.pallas_call(
        body, name="gla_fwd", grid=(HG_HEADS, 2),
        in_specs=[_head_spec(CB_HG_Q),
                  pl.BlockSpec((SEQ, HEAD_DIM), lambda h, d: (0, CB_F + 8 * d + h)),
                  _head_spec(CB_HG_I), _head_spec(CB_HG_G),
                  pl.BlockSpec((None, None, 1, HEAD_DIM), lambda h, d: (d, h, 0, 0)),
                  pl.BlockSpec((1, HEAD_DIM), lambda h, d: (0, 0))],
        out_specs=[_head_spec(0), _head_spec(0),
                   pl.BlockSpec((None, None, nck, HEAD_DIM, HEAD_DIM), lambda h, d: (h, d, 0, 0, 0))],
        out_shape=[jax.ShapeDtypeStruct((SEQ, D_MODEL), BF16), jax.ShapeDtypeStruct((SEQ, D_MODEL), F32),
                   jax.ShapeDtypeStruct((HG_HEADS, 2, nck, HEAD_DIM, HEAD_DIM), F32)],
        scratch_shapes=[pltpu.VMEM((SEQ, HEAD_DIM), BF16), pltpu.VMEM((nck, HEAD_DIM, HEAD_DIM), F32),
                        pltpu.VMEM((nck, 8, HEAD_DIM), F32)],
        compiler_params=_params(("parallel", "arbitrary"), 8 * blk + 3 * blk + 2 * blk + 2 * blk + blk // 2),
    )(proj, proj, proj, proj, lb, gain)


def _gla_bwd(proj, lb, gain, o_pre, states, do_hg):
    nck = GLA_NCK
    do, dg, dgain = _gla_bwd_norm(proj, gain, o_pre, do_hg, 256)

    def body(q_ref, f_ref, v_ref, lb_ref, do_ref, st_ref, dq_ref, df_ref, dv_ref, dlb_ref,
             dq_acc, dv_acc, dst_scr, cs_scr, dec_scr):
        d = pl.program_id(1)
        masks = _gla_masks(d)
        mask, maskb, _ = masks
        lbv = lb_ref[...]

        def intra(s, carry):
            blocks = [s + w * GLA_TRIPS for w in range(GLA_WAYS)]
            loaded = [(q_ref[r, :], f_ref[r, :], do_ref[r, :]) for r in map(_gla_block_rows, blocks)]
            results = []
            for qc, fc, doc in loaded:
                ck = _gla_block(qc, fc, lbv, masks)
                cs = [_dot(doc[_gla_chunk_rows(j), :], ck["qt"][_gla_chunk_rows(j), :], _TN) for j in range(GLA_CPB)]
                dec = [jnp.exp(ck["bt"][j * HG_CHUNK:j * HG_CHUNK + 8, :]) for j in range(GLA_CPB)]
                results.append((cs, dec))
            for b, (cs, dec) in zip(blocks, results):
                for j in range(GLA_CPB):
                    cs_scr[b * GLA_CPB + j] = cs[j]
                    dec_scr[b * GLA_CPB + j] = dec[j]
            return carry

        lax.fori_loop(0, GLA_TRIPS, intra, 0)

        def scan(i, dst):
            c = jnp.where(d == 0, nck - 1 - i, i)
            dst_scr[c] = dst
            return dst * dec_scr[c][0:1, :] + cs_scr[c]

        lax.fori_loop(0, nck, scan, jnp.zeros((HEAD_DIM, HEAD_DIM), F32), unroll=4)

        @pl.when(d == 0)
        def _():
            dq_acc[...] = jnp.zeros_like(dq_acc)
            dv_acc[...] = jnp.zeros_like(dv_acc)

        def block_grads(qc, fc, v, doc, states_in, dstates, decays):
            ck = _gla_block(qc, fc, lbv, masks)
            qt, kt, kh, a = ck["qt"], ck["kt"], ck["kh"], ck["a"]
            da = jnp.where(mask, _dot(doc, v, _NT), 0.0)
            dqt_i = _dot(da, kt, _NN)
            dkt = _dot(da, qt, _TN)
            dv_i = _dot(a, doc, _TN)
            dqt_p, dv_p, dkh_p, dbt_p = [], [], [], []
            for j in range(GLA_CPB):
                cr = _gla_chunk_rows(j)
                st_in, dst = states_in[j], dstates[j]
                dqt_p.append(dqt_i[cr, :] + _dot(doc[cr, :], st_in, _NN))
                dv_p.append(dv_i[cr, :] + _dot(kh[cr, :], dst, _NT))
                dkh_j = _dot(v[cr, :], dst, _NN)
                dkh_p.append(dkh_j)
                dbt_j = (decays[j][0:1, :] * jnp.sum(dst * st_in, axis=0, keepdims=True)
                         + jnp.sum(dkh_j * kh[cr, :], axis=0, keepdims=True))
                dbt_p.append(jnp.broadcast_to(dbt_j, (HG_CHUNK, HEAD_DIM)))
            dqt = jnp.concatenate(dqt_p, axis=0)
            dv = jnp.concatenate(dv_p, axis=0)
            dkh = jnp.concatenate(dkh_p, axis=0)
            dbt = jnp.concatenate(dbt_p, axis=0)
            db = dqt * qt - dkt * kt - dkh * kh
            dq = dqt * ck["eb"]
            dk = dkt * jnp.exp(-ck["b"]) + dkh * jnp.exp(ck["bt"] - ck["b"])
            dlogf = _dot_split(maskb, db, _TN) + dbt
            dforget = dlogf / ck["forget"] - dk
            sf, sq = ck["sf"], ck["sq"]
            df = (dforget * (1.0 - lbv) * sf * (1.0 - sf)).astype(BF16)
            dqc = dq * HG_SCALE * (sq + qc * sq * (1.0 - sq))
            return df, dqc, dv, jnp.sum(dforget * (1.0 - sf), axis=0, keepdims=True)

        def grads(s, dlb):
            blocks = [s + w * GLA_GRAD_TRIPS for w in range(GLA_GRAD_WAYS)]
            rows = [_gla_block_rows(b) for b in blocks]
            loaded = []
            for b, r in zip(blocks, rows):
                chunks = [b * GLA_CPB + j for j in range(GLA_CPB)]
                loaded.append((q_ref[r, :], f_ref[r, :], v_ref[r, :], do_ref[r, :], [st_ref[c] for c in chunks],
                               [dst_scr[c] for c in chunks], [dec_scr[c] for c in chunks], dq_acc[r, :], dv_acc[r, :]))
            results = [block_grads(*t[:7]) + (t[7], t[8]) for t in loaded]
            for r, (df, dqc, dv, dlb_part, dq_prev, dv_prev) in zip(rows, results):
                df_ref[r, :] = df
                dq_acc[r, :] = dq_prev + dqc
                dv_acc[r, :] = dv_prev + dv
                dlb = dlb + dlb_part
            return dlb

        dlb_ref[...] = lax.fori_loop(0, GLA_GRAD_TRIPS, grads, jnp.zeros((1, HEAD_DIM), F32))

        @pl.when(d == 1)
        def _():
            dq_ref[...] = dq_acc[...].astype(BF16)
            dv_ref[...] = dv_acc[...].astype(BF16)

    blk = SEQ * HEAD_DIM * 4
    bshape = jax.ShapeDtypeStruct((SEQ, D_MODEL), BF16)
    state_bytes = nck * HEAD_DIM * HEAD_DIM * 4
    dq, df, dv, dlb = pl.pallas_call(
        body, name="gla_bwd", grid=(HG_HEADS, 2),
        in_specs=[_head_spec(CB_HG_Q),
                  pl.BlockSpec((SEQ, HEAD_DIM), lambda h, d: (0, CB_F + 8 * d + h)),
                  _head_spec(CB_HG_I),
                  pl.BlockSpec((None, None, 1, HEAD_DIM), lambda h, d: (d, h, 0, 0)),
                  _head_spec(0),
                  pl.BlockSpec((None, None, nck, HEAD_DIM, HEAD_DIM), lambda h, d: (h, d, 0, 0, 0))],
        out_specs=[_head_spec(0),
                   pl.BlockSpec((None, SEQ, HEAD_DIM), lambda h, d: (d, 0, h)),
                   _head_spec(0),
                   pl.BlockSpec((None, None, 1, HEAD_DIM), lambda h, d: (d, h, 0, 0))],
        out_shape=[bshape, jax.ShapeDtypeStruct((2, SEQ, D_MODEL), BF16), bshape,
                   jax.ShapeDtypeStruct((2, HG_HEADS, 1, HEAD_DIM), F32)],
        scratch_shapes=[pltpu.VMEM((SEQ, HEAD_DIM), F32)] * 2
        + [pltpu.VMEM((nck, HEAD_DIM, HEAD_DIM), F32)] * 2 + [pltpu.VMEM((nck, 8, HEAD_DIM), F32)],
        compiler_params=_params(("parallel", "arbitrary"), 8 * blk + 4 * state_bytes + 3 * blk + 2 * blk),
    )(proj, proj, proj, lb, do, states)
    return dq, df, dv, dg, dlb, dgain


def _gla_bwd_norm(proj, gain, o_pre, do_hg, tr):
    w = D_MODEL

    def body(g_ref, gain_ref, opre_ref, dohg_ref, do_ref, dg_ref, dgain_ref):
        gainv = gain_ref[...]
        part = jnp.zeros((1, HEAD_DIM), F32)
        for h in range(HG_HEADS):
            hs = slice(h * HEAD_DIM, (h + 1) * HEAD_DIM)
            o = opre_ref[:, hs]
            r = lax.rsqrt(jnp.mean(o * o, axis=-1, keepdims=True) + RMS_EPS)
            ohat = o * r
            g = g_ref[:, hs]
            sg = _sigmoid(g)
            silu = g * sg
            dout = dohg_ref[:, hs]
            dg_ref[:, hs] = (dout * ohat * gainv * (sg + silu * (1.0 - sg))).astype(BF16)
            dy = dout * silu
            part = part + jnp.sum(dy * ohat, axis=0, keepdims=True)
            dn = dy * gainv
            do_ref[:, hs] = r * (dn - ohat * jnp.mean(dn * ohat, axis=-1, keepdims=True))

        @pl.when(pl.program_id(0) == 0)
        def _():
            dgain_ref[...] = part

        @pl.when(pl.program_id(0) > 0)
        def _():
            dgain_ref[...] += part

    return pl.pallas_call(
        body, name="gla_bwd_norm", grid=(SEQ // tr,),
        in_specs=[_row_spec(tr, w, CB_HG_G * LANE // w), _bcast_spec(HEAD_DIM), _row_spec(tr, w), _row_spec(tr, w)],
        out_specs=[_row_spec(tr, w), _row_spec(tr, w), _bcast_spec(HEAD_DIM)],
        out_shape=[jax.ShapeDtypeStruct((SEQ, w), F32), jax.ShapeDtypeStruct((SEQ, w), BF16),
                   jax.ShapeDtypeStruct((1, HEAD_DIM), F32)],
        compiler_params=_params(("arbitrary",), 16 * tr * w * 4),
    )(proj, gain, o_pre, do_hg)


def _da_residue_rows(r, n0, size, d):
    if d == 1:
        return pl.ds(pl.multiple_of(n0, 8), size)
    return pl.ds(r + n0 * d, size, stride=d)


def _da_rmsnorm(x, gain):
    r = lax.rsqrt(jnp.mean(x * x, axis=-1, keepdims=True) + RMS_EPS)
    return x * r, r, x * r * gain


def _da_scores(qn_scr, kn_scr, slope, i, ld):
    w0 = jnp.clip(i * DA_QB - DA_RADIUS, 0, ld - DA_WIN)
    w0 = pl.multiple_of(w0, DA_RADIUS)
    qrows = pl.ds(pl.multiple_of(i * DA_QB, DA_QB), DA_QB)
    win = pl.ds(w0, DA_WIN)
    qb = qn_scr[qrows, :]
    kw = kn_scr[win, :]
    s = _dot(qb, kw, _NT) * DA_SCALE
    qpos = i * DA_QB + lax.broadcasted_iota(jnp.int32, (DA_QB, DA_WIN), 0)
    kpos = w0 + lax.broadcasted_iota(jnp.int32, (DA_QB, DA_WIN), 1)
    arel = jnp.abs(kpos - qpos)
    s = s - slope * arel.astype(F32)
    s = jnp.where(arel <= DA_RADIUS, s, NEG_INF)
    return s, qb, kw, qrows, win


def _da_slopes(group):
    d = DA_DILATIONS[group]
    sl = _alibi_slopes(DA_HEADS)[4 * group:4 * group + 4] * d
    return jnp.asarray(np.broadcast_to(sl[:, None, None], (4, 1, LANE)).copy())


def _da_fwd(proj, gq, gk, group):
    d = DA_DILATIONS[group]
    ld = SEQ // d
    nqb = ld // DA_QB

    def body(q_ref, k_ref, v_ref, gq_ref, gk_ref, sl_ref, o_ref, lse_ref, qn_scr, kn_scr, v_scr):
        slope = sl_ref[:, 0:1]

        def residue(r, carry):
            sel = _da_residue_rows(r, 0, ld, d)
            qn_scr[...] = _da_rmsnorm(q_ref[sel, :], gq_ref[...])[2].astype(BF16)
            kn_scr[...] = _da_rmsnorm(k_ref[sel, :], gk_ref[...])[2].astype(BF16)
            v_scr[...] = v_ref[sel, :].astype(BF16)

            def block(i):
                s, _, _, _, win = _da_scores(qn_scr, kn_scr, slope, i, ld)
                m = jnp.max(s, axis=-1, keepdims=True)
                p = jnp.exp(s - m)
                l = jnp.sum(p, axis=-1, keepdims=True)
                return _dot(p, v_scr[win, :], _NN) / l, jnp.broadcast_to(m + jnp.log(l), (DA_QB, HEAD_DIM))

            def step(i, c2):
                blocks = [i + w * (nqb // DA_WAYS) for w in range(DA_WAYS)]
                for b, (o, lse) in zip(blocks, [block(b) for b in blocks]):
                    out = _da_residue_rows(r, b * DA_QB, DA_QB, d)
                    o_ref[out, :] = o
                    lse_ref[out, :] = lse
                return c2

            return lax.fori_loop(0, nqb // DA_WAYS, step, carry)

        lax.fori_loop(0, d, residue, 0)

    seq_spec = lambda base: pl.BlockSpec((SEQ, HEAD_DIM), lambda h: (0, base + 4 * group + h))
    out_spec = pl.BlockSpec((SEQ, HEAD_DIM), lambda h: (0, h))
    gain_spec = pl.BlockSpec((1, HEAD_DIM), lambda h: (0, 0))
    blk = SEQ * HEAD_DIM * 4
    return pl.pallas_call(
        body, name=f"da_fwd_g{group}", grid=(DA_HEADS_PER_GROUP,),
        in_specs=[seq_spec(CB_DA_Q), seq_spec(CB_DA_K), seq_spec(CB_DA_V), gain_spec, gain_spec,
                  pl.BlockSpec((None, 1, LANE), lambda h: (h, 0, 0))],
        out_specs=[out_spec, out_spec],
        out_shape=[jax.ShapeDtypeStruct((SEQ, DA_WIDTH), F32)] * 2,
        scratch_shapes=[pltpu.VMEM((ld, HEAD_DIM), BF16)] * 3,
        compiler_params=_params(("parallel",), 10 * blk + 2 * blk),
    )(proj, proj, proj, gq, gk, _da_slopes(group))


def _da_merge(os, lses, tr):
    w = DA_WIDTH

    def body(o0, o1, o2, l0, l1, l2, ob_ref, of_ref, lse_ref):
        la, lb_, lc = l0[...], l1[...], l2[...]
        m = jnp.maximum(jnp.maximum(la, lb_), lc)
        ea, eb, ec = jnp.exp(la - m), jnp.exp(lb_ - m), jnp.exp(lc - m)
        tot = ea + eb + ec
        o = (ea * o0[...] + eb * o1[...] + ec * o2[...]) / tot
        of_ref[...] = o
        ob_ref[...] = o.astype(BF16)
        lse_ref[...] = m + jnp.log(tot)

    return pl.pallas_call(
        body, name="da_merge", grid=(SEQ // tr,), in_specs=[_row_spec(tr, w)] * 6,
        out_specs=[_row_spec(tr, w)] * 3,
        out_shape=[jax.ShapeDtypeStruct((SEQ, w), BF16), jax.ShapeDtypeStruct((SEQ, w), F32),
                   jax.ShapeDtypeStruct((SEQ, w), F32)],
        compiler_params=_params(("parallel",), 24 * tr * w * 4),
    )(*os, *lses)


def _da_rowdot(do, o, tr):
    w = DA_WIDTH

    def body(do_ref, o_ref, out_ref):
        prod = do_ref[...] * o_ref[...]
        for h in range(w // HEAD_DIM):
            sl = slice(h * HEAD_DIM, (h + 1) * HEAD_DIM)
            out_ref[:, sl] = jnp.broadcast_to(jnp.sum(prod[:, sl], axis=-1, keepdims=True), (tr, HEAD_DIM))

    return pl.pallas_call(
        body, name="da_rowdot", grid=(SEQ // tr,), in_specs=[_row_spec(tr, w)] * 2,
        out_specs=_row_spec(tr, w), out_shape=jax.ShapeDtypeStruct((SEQ, w), F32),
        compiler_params=_params(("parallel",), 10 * tr * w * 4),
    )(do, o)


def _da_bwd(proj, gq, gk, do, lse, dd, group):
    d = DA_DILATIONS[group]
    ld = SEQ // d
    nqb = ld // DA_QB

    def body(q_ref, k_ref, v_ref, gq_ref, gk_ref, sl_ref, do_ref, lse_ref, dd_ref,
             dq_ref, dk_ref, dv_ref, dgq_ref, dgk_ref,
             qn_scr, kn_scr, v_scr, dqn_scr, dkn_scr, dvr_scr, dq_scr, dk_scr, dv_scr):
        gqv, gkv = gq_ref[...], gk_ref[...]
        slope = sl_ref[:, 0:1]

        def residue(r, carry):
            sel = _da_residue_rows(r, 0, ld, d)
            qn_scr[...] = _da_rmsnorm(q_ref[sel, :], gqv)[2].astype(BF16)
            kn_scr[...] = _da_rmsnorm(k_ref[sel, :], gkv)[2].astype(BF16)
            v_scr[...] = v_ref[sel, :].astype(BF16)
            dkn_scr[...] = jnp.zeros_like(dkn_scr)
            dvr_scr[...] = jnp.zeros_like(dvr_scr)

            def block(i):
                s, qb, kw, qrows, win = _da_scores(qn_scr, kn_scr, slope, i, ld)
                src = _da_residue_rows(r, i * DA_QB, DA_QB, d)
                p = jnp.exp(s - lse_ref[src, :][:, 0:1])
                dob = do_ref[src, :]
                dp = _dot(dob, v_scr[win, :], _NT)
                ds = p * (dp - dd_ref[src, :][:, 0:1]) * DA_SCALE
                return qrows, win, _dot(p, dob, _TN), _dot(ds, kw, _NN), _dot(ds, qb, _TN)

            def step(i, c2):
                blocks = [i + w * (nqb // DA_WAYS) for w in range(DA_WAYS)]
                for qrows, win, dv, dqn, dkn in [block(b) for b in blocks]:
                    dvr_scr[win, :] += dv
                    dqn_scr[qrows, :] = dqn
                    dkn_scr[win, :] += dkn
                return c2

            lax.fori_loop(0, nqb // DA_WAYS, step, 0)

            parts = []
            for x_ref, gv, dn_scr, dx_scr in ((q_ref, gqv, dqn_scr, dq_scr), (k_ref, gkv, dkn_scr, dk_scr)):
                hat, rstd, _ = _da_rmsnorm(x_ref[sel, :], gv)
                dn = dn_scr[...]
                dyg = dn * gv
                dx_scr[sel, :] = rstd * (dyg - hat * jnp.mean(dyg * hat, axis=-1, keepdims=True))
                parts.append(jnp.sum(dn * hat, axis=0, keepdims=True))
            dv_scr[sel, :] = dvr_scr[...]
            return carry[0] + parts[0], carry[1] + parts[1]

        zero = jnp.zeros((1, HEAD_DIM), F32)
        pq, pk = lax.fori_loop(0, d, residue, (zero, zero))

        @pl.when(pl.program_id(0) == 0)
        def _():
            dgq_ref[...] = pq
            dgk_ref[...] = pk

        @pl.when(pl.program_id(0) > 0)
        def _():
            dgq_ref[...] += pq
            dgk_ref[...] += pk

        dq_ref[...] = dq_scr[...].astype(BF16)
        dk_ref[...] = dk_scr[...].astype(BF16)
        dv_ref[...] = dv_scr[...].astype(BF16)

    seq_spec = lambda base: pl.BlockSpec((SEQ, HEAD_DIM), lambda h: (0, base + 4 * group + h))
    out_spec = pl.BlockSpec((SEQ, HEAD_DIM), lambda h: (0, h))
    gain_spec = pl.BlockSpec((1, HEAD_DIM), lambda h: (0, 0))
    oshape = jax.ShapeDtypeStruct((SEQ, DA_WIDTH), BF16)
    gshape = jax.ShapeDtypeStruct((1, HEAD_DIM), F32)
    blk = SEQ * HEAD_DIM * 4
    return pl.pallas_call(
        body, name=f"da_bwd_g{group}", grid=(DA_HEADS_PER_GROUP,),
        in_specs=[seq_spec(CB_DA_Q), seq_spec(CB_DA_K), seq_spec(CB_DA_V), gain_spec, gain_spec,
                  pl.BlockSpec((None, 1, LANE), lambda h: (h, 0, 0)), out_spec, out_spec, out_spec],
        out_specs=[out_spec, out_spec, out_spec, gain_spec, gain_spec],
        out_shape=[oshape, oshape, oshape, gshape, gshape],
        scratch_shapes=[pltpu.VMEM((ld, HEAD_DIM), BF16)] * 3 + [pltpu.VMEM((ld, HEAD_DIM), F32)] * 3
        + [pltpu.VMEM((SEQ, HEAD_DIM), F32)] * 3,
        compiler_params=_params(("arbitrary",), 12 * blk + 3 * blk + 3 * blk + 3 * blk),
    )(proj, proj, proj, gq, gk, _da_slopes(group), do, lse, dd)


def _mem_softmax(q, k, gq, gk):
    qhat, rq, qn = _da_rmsnorm(q, gq)
    khat, rk, kn = _da_rmsnorm(k, gk)
    s = _dot(qn, kn, _NT) * MEM_SCALE
    m = jnp.max(s, axis=-1, keepdims=True)
    e = jnp.exp(s - m)
    p = e / jnp.sum(e, axis=-1, keepdims=True)
    return p, (qhat, rq, qn), (khat, rk, kn)


def _mem_fwd(proj, kv, gq, gk, tq):
    def body(q_ref, k_ref, v_ref, gq_ref, gk_ref, o_ref):
        p, _, _ = _mem_softmax(q_ref[...], k_ref[...], gq_ref[...], gk_ref[...])
        o_ref[...] = _dot(p, v_ref[...], _NN).astype(BF16)

    gain_spec = pl.BlockSpec((1, HEAD_DIM), lambda h, i: (0, 0))
    return pl.pallas_call(
        body, name="mem_fwd", grid=(MEM_HEADS, SEQ // tq),
        in_specs=[pl.BlockSpec((tq, HEAD_DIM), lambda h, i: (i, CB_MEM_Q + h)),
                  pl.BlockSpec((N_MEM, HEAD_DIM), lambda h, i: (0, h)),
                  pl.BlockSpec((N_MEM, HEAD_DIM), lambda h, i: (0, MEM_HEADS + h)), gain_spec, gain_spec],
        out_specs=pl.BlockSpec((tq, HEAD_DIM), lambda h, i: (i, h)),
        out_shape=jax.ShapeDtypeStruct((SEQ, MEM_WIDTH), BF16),
        compiler_params=_params(("parallel", "parallel"), 16 * tq * N_MEM * 4),
    )(proj, kv, kv, gq, gk)


def _mem_bwd(proj, kv, gq, gk, do, tq):
    nq = SEQ // tq

    def body(q_ref, k_ref, v_ref, gq_ref, gk_ref, do_ref, dq_ref, dk_ref, dv_ref, dgq_ref, dgk_ref, dkn_scr):
        h, i = pl.program_id(0), pl.program_id(1)
        gqv, gkv = gq_ref[...], gk_ref[...]
        p, (qhat, rq, qn), (khat, rk, kn) = _mem_softmax(q_ref[...], k_ref[...], gqv, gkv)
        dob = do_ref[...]
        dvp = _dot(p, dob, _TN)
        dp = _dot(dob, v_ref[...], _NT)
        ds = p * (dp - jnp.sum(p * dp, axis=-1, keepdims=True)) * MEM_SCALE
        dqn = _dot(ds, kn, _NN)
        dknp = _dot(ds, qn, _TN)
        dyg = dqn * gqv
        dq_ref[...] = (rq * (dyg - qhat * jnp.mean(dyg * qhat, axis=-1, keepdims=True))).astype(BF16)
        dgq_part = jnp.sum(dqn * qhat, axis=0, keepdims=True)
        first = jnp.logical_and(h == 0, i == 0)

        @pl.when(first)
        def _():
            dgq_ref[...] = dgq_part

        @pl.when(jnp.logical_not(first))
        def _():
            dgq_ref[...] += dgq_part

        @pl.when(i == 0)
        def _():
            dv_ref[...] = dvp
            dkn_scr[...] = dknp

        @pl.when(i > 0)
        def _():
            dv_ref[...] += dvp
            dkn_scr[...] += dknp

        @pl.when(i == nq - 1)
        def _():
            dkn = dkn_scr[...]
            dkg = dkn * gkv
            dk_ref[...] = rk * (dkg - khat * jnp.mean(dkg * khat, axis=-1, keepdims=True))
            dgk_part = jnp.sum(dkn * khat, axis=0, keepdims=True)

            @pl.when(h == 0)
            def _():
                dgk_ref[...] = dgk_part

            @pl.when(h > 0)
            def _():
                dgk_ref[...] += dgk_part

    gain_spec = pl.BlockSpec((1, HEAD_DIM), lambda h, i: (0, 0))
    kvout = pl.BlockSpec((N_MEM, HEAD_DIM), lambda h, i: (0, h))
    return pl.pallas_call(
        body, name="mem_bwd", grid=(MEM_HEADS, nq),
        in_specs=[pl.BlockSpec((tq, HEAD_DIM), lambda h, i: (i, CB_MEM_Q + h)),
                  pl.BlockSpec((N_MEM, HEAD_DIM), lambda h, i: (0, h)),
                  pl.BlockSpec((N_MEM, HEAD_DIM), lambda h, i: (0, MEM_HEADS + h)), gain_spec, gain_spec,
                  pl.BlockSpec((tq, HEAD_DIM), lambda h, i: (i, h))],
        out_specs=[pl.BlockSpec((tq, HEAD_DIM), lambda h, i: (i, h)), kvout, kvout, gain_spec, gain_spec],
        out_shape=[jax.ShapeDtypeStruct((SEQ, MEM_WIDTH), BF16), jax.ShapeDtypeStruct((N_MEM, MEM_WIDTH), F32),
                   jax.ShapeDtypeStruct((N_MEM, MEM_WIDTH), F32), jax.ShapeDtypeStruct((1, HEAD_DIM), F32),
                   jax.ShapeDtypeStruct((1, HEAD_DIM), F32)],
        scratch_shapes=[pltpu.VMEM((N_MEM, HEAD_DIM), F32)],
        compiler_params=_params(("arbitrary", "arbitrary"), 24 * tq * N_MEM * 4),
    )(proj, kv, kv, gq, gk, do)


def _mesh_position():
    return lax.axis_index("x"), lax.axis_index("y"), lax.axis_index("c")


def _any_spec():
    return pl.BlockSpec(memory_space=pl.ANY)


def _all_gather(shards):
    n = len(shards)

    def body(*refs):
        ins, outs = refs[:n], refs[n:2 * n]
        send_sems, recv_sems, local_sems = refs[2 * n:]
        x, y, c = _mesh_position()
        me, sibling = (x, y, c), (x, y, 1 - c)
        chips = [(1 - x, y), (x, 1 - y), (1 - x, 1 - y)]

        def copy(w, k, block, to, src=None):
            px, py, pc = block
            rows = outs[w].at[4 * px + 2 * py + pc]
            return pltpu.make_async_remote_copy(
                src_ref=rows if src is None else src, dst_ref=rows,
                send_sem=send_sems.at[7 * w + k], recv_sem=recv_sems.at[7 * w + k],
                device_id=to, device_id_type=MESH)

        started = []
        for w in range(n):
            mine = pltpu.make_async_copy(ins[w], outs[w].at[4 * x + 2 * y + c], local_sems.at[w])
            mine.start()
            started.append(mine)
        sends = []
        for w in range(n):
            first = [copy(w, 0, me, sibling, src=ins[w])]
            first += [copy(w, 1 + j, me, (*chip, c), src=ins[w]) for j, chip in enumerate(chips)]
            for cp in first:
                cp.start()
            sends += first
        for w in range(n):
            for j, chip in enumerate(chips):
                copy(w, 1 + j, (*chip, c), me).wait_recv()
                passed = copy(w, 4 + j, (*chip, c), sibling)
                passed.start()
                sends.append(passed)
        for w in range(n):
            copy(w, 0, sibling, me).wait_recv()
            for j, chip in enumerate(chips):
                copy(w, 4 + j, (*chip, 1 - c), me).wait_recv()
        for cp in sends:
            cp.wait_send()
        for mine in started:
            mine.wait()

    return pl.pallas_call(
        body, name="weights_all_gather",
        in_specs=[_any_spec()] * n, out_specs=[_any_spec()] * n,
        out_shape=[jax.ShapeDtypeStruct((N_DEV,) + s.shape, s.dtype) for s in shards],
        scratch_shapes=[pltpu.SemaphoreType.DMA((7 * n,)), pltpu.SemaphoreType.DMA((7 * n,)),
                        pltpu.SemaphoreType.DMA((n,))],
    )(*shards)


def _chip_of(j, x, y):
    return (1 - x if j & 1 else x, 1 - y if j & 2 else y)


_HBM_SPEC = pl.BlockSpec(memory_space=pltpu.HBM)
_SEM_SPEC = pl.BlockSpec(memory_space=pltpu.SEMAPHORE)
_DATAFLOW_EFFECT = pltpu.SideEffectType.DATAFLOW_SIDE_EFFECTING
TOKEN_SHAPE = (8, D_MODEL)


def _copies_start(name, arrays, n_copies, plan):
    n = len(arrays)

    def body(*refs):
        send_sems, recv_sems, token = refs[n], refs[n + 1], refs[2 * n + 2]
        copies = plan(refs[:n])
        assert len(copies) == n_copies
        for k, (src, dst, dev) in enumerate(copies):
            pltpu.make_async_remote_copy(src_ref=src, dst_ref=dst, send_sem=send_sems.at[k], recv_sem=recv_sems.at[k],
                                         device_id=dev, device_id_type=MESH).start()
        token[...] = jnp.zeros_like(token)

    outs = pl.pallas_call(
        body, name=name,
        out_shape=(pltpu.SemaphoreType.DMA((n_copies,)), pltpu.SemaphoreType.DMA((n_copies,)),
                   *[pltpu.HBM(a.shape, a.dtype) for a in arrays], jax.ShapeDtypeStruct(TOKEN_SHAPE, F32)),
        in_specs=[_HBM_SPEC] * n,
        out_specs=(_SEM_SPEC, _SEM_SPEC, *[_HBM_SPEC] * n, pl.BlockSpec(memory_space=pltpu.VMEM)),
        input_output_aliases={i: i + 2 for i in range(n)},
        compiler_params=pltpu.CompilerParams(has_side_effects=_DATAFLOW_EFFECT),
    )(*[pltpu.with_memory_space_constraint(a, pltpu.HBM) for a in arrays])
    return outs[0], outs[1], list(outs[2:2 + n]), outs[2 + n]


def _copies_wait(name, send_sems, recv_sems, arrays, n_copies, plan, after):
    n = len(arrays)
    after = list(after) if isinstance(after, (list, tuple)) else [after]

    def body(*refs):
        send_ref, recv_ref = refs[n], refs[n + 1]
        copies = plan(refs[:n])
        assert len(copies) == n_copies
        for k, (src, dst, dev) in enumerate(copies):
            cp = pltpu.make_async_remote_copy(src_ref=src, dst_ref=dst, send_sem=send_ref.at[k], recv_sem=recv_ref.at[k],
                                              device_id=dev, device_id_type=MESH)
            cp.wait_send()
            cp.wait_recv()

    outs = pl.pallas_call(
        body, name=name, out_shape=tuple(pltpu.HBM(a.shape, a.dtype) for a in arrays),
        in_specs=[_HBM_SPEC] * n + [_SEM_SPEC, _SEM_SPEC] + [pl.BlockSpec(memory_space=pl.ANY)] * len(after),
        out_specs=tuple([_HBM_SPEC] * n), input_output_aliases={i: i for i in range(n)},
        compiler_params=pltpu.CompilerParams(has_side_effects=_DATAFLOW_EFFECT),
    )(*arrays, send_sems, recv_sems, *after)
    return list(outs)


def _after(small, token):
    return small if token is None else small + token[0:1, :small.shape[-1]]


def _gather_plan_out(n):
    def plan(refs):
        x, y, c = _mesh_position()
        me = 4 * x + 2 * y + c
        copies = []
        for w in range(n):
            land = refs[n + w].at[me]
            copies.append((refs[w], land, (x, y, 1 - c)))
            for j in range(1, 4):
                copies.append((refs[w], land, (*_chip_of(j, x, y), c)))
        return copies
    return plan


def _gather_plan_pass(n):
    def plan(refs):
        x, y, c = _mesh_position()
        copies = []
        for w in range(n):
            for j in range(1, 4):
                px, py = _chip_of(j, x, y)
                rows = refs[w].at[4 * px + 2 * py + c]
                copies.append((rows, rows, (x, y, 1 - c)))
        return copies
    return plan


def _reduce_plan_sibling(n):
    def plan(refs):
        x, y, c = _mesh_position()
        copies = []
        for w in range(n):
            for j in range(4):
                px, py = _chip_of(j, x, y)
                copies.append((refs[w].at[4 * px + 2 * py + (1 - c)], refs[n + w].at[j], (x, y, 1 - c)))
        return copies
    return plan


def _reduce_plan_chips(n):
    def plan(refs):
        x, y, c = _mesh_position()
        copies = []
        for w in range(n):
            for j in range(1, 4):
                copies.append((refs[w].at[j - 1], refs[n + w].at[j - 1], (*_chip_of(j, x, y), c)))
        return copies
    return plan


def _chip_partials(grad, recv, name, tr):
    _, rows, width = grad.shape

    def body(g_ref, r_ref, own_ref, other_ref):
        x, y, c = _mesh_position()
        for j in range(4):
            px, py = _chip_of(j, x, y)
            total = g_ref[4 * px + 2 * py + c].astype(F32) + r_ref[j].astype(F32)
            if j == 0:
                own_ref[...] = total
            else:
                other_ref[j - 1] = total.astype(BF16)

    return pl.pallas_call(
        body, name=name, grid=(rows // tr,),
        in_specs=[pl.BlockSpec((N_DEV, tr, width), lambda i: (0, i, 0)),
                  pl.BlockSpec((4, tr, width), lambda i: (0, i, 0))],
        out_specs=[pl.BlockSpec((tr, width), lambda i: (i, 0)), pl.BlockSpec((3, tr, width), lambda i: (0, i, 0))],
        out_shape=[jax.ShapeDtypeStruct((rows, width), F32), jax.ShapeDtypeStruct((3, rows, width), BF16)],
        compiler_params=_params(("parallel",), 2 * 20 * tr * width * 2 + 8 * tr * width * 4),
    )(grad, recv)


def _adamw_math(w, g, m, v):
    m = ADAM_B1 * m + (1.0 - ADAM_B1) * g
    v = ADAM_B2 * v + (1.0 - ADAM_B2) * (g * g)
    m_hat = m / (1.0 - ADAM_B1 ** ADAM_STEP)
    v_hat = v / (1.0 - ADAM_B2 ** ADAM_STEP)
    delta = -ADAM_LR * (m_hat / (jnp.sqrt(v_hat) + ADAM_EPS) + ADAM_WD * w)
    return delta, m, v


def _adamw_shard(own, recv, w, m, v, name, tr):
    rows, width = own.shape

    def body(own_ref, r_ref, w_ref, m_ref, v_ref, g_ref, d_ref, nm_ref, nv_ref):
        g = own_ref[...]
        for j in range(3):
            g = g + r_ref[j].astype(F32)
        g_ref[...] = g
        d_ref[...], nm_ref[...], nv_ref[...] = _adamw_math(w_ref[...], g, m_ref[...], v_ref[...])

    spec = pl.BlockSpec((tr, width), lambda i: (i, 0))
    shape = jax.ShapeDtypeStruct((rows, width), F32)
    return pl.pallas_call(
        body, name=name, grid=(rows // tr,),
        in_specs=[spec, pl.BlockSpec((3, tr, width), lambda i: (0, i, 0)), spec, spec, spec],
        out_specs=[spec] * 4, out_shape=[shape] * 4,
        compiler_params=_params(("parallel",), 22 * tr * width * 4),
    )(own, recv, w, m, v)


def _small_all_reduce_adamw(gpart, lbpack, wpack, mpack, vpack, after):
    def body(gp_ref, lb_ref, w_ref, m_ref, v_ref, after_ref, g_ref, d_ref, nm_ref, nv_ref, gath_ref, send_sems, recv_sems):
        del after_ref
        x, y, c = _mesh_position()
        me = 4 * x + 2 * y + c
        gath_ref[me] = gp_ref[...]
        copies = []
        for j in range(1, N_DEV):
            peer = (x ^ (j >> 2), y ^ ((j >> 1) & 1), c ^ (j & 1))
            cp = pltpu.make_async_remote_copy(
                src_ref=gp_ref, dst_ref=gath_ref.at[me], send_sem=send_sems.at[j - 1], recv_sem=recv_sems.at[j - 1],
                device_id=peer, device_id_type=MESH)
            cp.start()
            copies.append(cp)
        for cp in copies:
            cp.wait()
        tot = gath_ref[0]
        for s in range(1, N_DEV):
            tot = tot + gath_ref[s]
        lb = lb_ref[...]
        dl = tot[16:32, :] * lb * (1.0 - lb)
        g = jnp.concatenate([tot[0:16, :], dl[0:8, :], -dl[0:8, :], dl[8:16, :], -dl[8:16, :],
                             tot[32:SMALL_GRAD_ROWS, :]], axis=0)
        g_ref[...] = g
        d_ref[...], nm_ref[...], nv_ref[...] = _adamw_math(w_ref[...], g, m_ref[...], v_ref[...])

    vm = pl.BlockSpec(memory_space=pltpu.VMEM)
    shape = jax.ShapeDtypeStruct((SMALL_ROWS, LANE), F32)
    return pl.pallas_call(
        body, name="small_all_reduce_adamw", in_specs=[vm] * 5 + [_any_spec()], out_specs=[vm] * 4,
        out_shape=[shape] * 4,
        scratch_shapes=[pltpu.VMEM((N_DEV, SMALL_GRAD_ROWS, LANE), F32),
                        pltpu.SemaphoreType.DMA((N_DEV - 1,)), pltpu.SemaphoreType.DMA((N_DEV - 1,))],
    )(gpart, lbpack, wpack, mpack, vpack, after)


_SMALL_NAMES = ("norm_mix_gain", "norm_mem_gain", "lb_logits_fw", "lb_logits_bw", "norm_ffn_gain",
                "hg_norm_gain", "da_q_gain", "da_k_gain", "mem_q_gain", "mem_k_gain")
_SMALL_ROW0 = {"norm_mix_gain": 0, "norm_mem_gain": 8, "lb_logits_fw": 16, "lb_logits_bw": 32, "norm_ffn_gain": 48,
               "hg_norm_gain": 56, "da_q_gain": 64, "da_k_gain": 72, "mem_q_gain": 80, "mem_k_gain": 88}
_LOSS_ROW = 96


def _pack_rows(parts, total_rows):
    rows = []
    for p in parts:
        r = p.reshape(-1, LANE)
        rows.append(jnp.pad(r, ((0, -r.shape[0] % 8), (0, 0))))
    used = sum(r.shape[0] for r in rows)
    if total_rows > used:
        rows.append(jnp.zeros((total_rows - used, LANE), F32))
    return jnp.concatenate(rows, axis=0)


def _unpack_small(pack, like):
    out = {}
    for name in _SMALL_NAMES:
        n = like[name].size // LANE
        r0 = _SMALL_ROW0[name]
        out[name] = pack[r0:r0 + n].reshape(like[name].shape)
    return out


def kernel(x, mem, norm_mix_gain, norm_mem_gain, w_in, lb_logits_fw, lb_logits_bw, hg_norm_gain, da_q_gain, da_k_gain, w_mem_kv, mem_q_gain, mem_k_gain, w_proj_hg, w_proj_da, w_proj_mem, w_out, norm_ffn_gain, w_ffn_in, w_ffn_out, loss_target, m_norm_mix_gain, m_norm_mem_gain, m_w_in, m_lb_logits_fw, m_lb_logits_bw, m_hg_norm_gain, m_da_q_gain, m_da_k_gain, m_w_mem_kv, m_mem_q_gain, m_mem_k_gain, m_w_proj_hg, m_w_proj_da, m_w_proj_mem, m_w_out, m_norm_ffn_gain, m_w_ffn_in, m_w_ffn_out, v_norm_mix_gain, v_norm_mem_gain, v_w_in, v_lb_logits_fw, v_lb_logits_bw, v_hg_norm_gain, v_da_q_gain, v_da_k_gain, v_w_mem_kv, v_mem_q_gain, v_mem_k_gain, v_w_proj_hg, v_w_proj_da, v_w_proj_mem, v_w_out, v_norm_ffn_gain, v_w_ffn_in, v_w_ffn_out):
    small_w = dict(norm_mix_gain=norm_mix_gain, norm_mem_gain=norm_mem_gain, lb_logits_fw=lb_logits_fw,
                   lb_logits_bw=lb_logits_bw, norm_ffn_gain=norm_ffn_gain, hg_norm_gain=hg_norm_gain,
                   da_q_gain=da_q_gain, da_k_gain=da_k_gain, mem_q_gain=mem_q_gain, mem_k_gain=mem_k_gain)
    small_m = dict(norm_mix_gain=m_norm_mix_gain, norm_mem_gain=m_norm_mem_gain, lb_logits_fw=m_lb_logits_fw,
                   lb_logits_bw=m_lb_logits_bw, norm_ffn_gain=m_norm_ffn_gain, hg_norm_gain=m_hg_norm_gain,
                   da_q_gain=m_da_q_gain, da_k_gain=m_da_k_gain, mem_q_gain=m_mem_q_gain, mem_k_gain=m_mem_k_gain)
    small_v = dict(norm_mix_gain=v_norm_mix_gain, norm_mem_gain=v_norm_mem_gain, lb_logits_fw=v_lb_logits_fw,
                   lb_logits_bw=v_lb_logits_bw, norm_ffn_gain=v_norm_ffn_gain, hg_norm_gain=v_hg_norm_gain,
                   da_q_gain=v_da_q_gain, da_k_gain=v_da_k_gain, mem_q_gain=v_mem_q_gain, mem_k_gain=v_mem_k_gain)
    big_names = ("w_in", "w_mem_kv", "w_proj_hg", "w_proj_da", "w_proj_mem", "w_out", "w_ffn_in", "w_ffn_out")
    big_w = dict(w_in=w_in[0], w_mem_kv=w_mem_kv[0], w_proj_hg=w_proj_hg[0], w_proj_da=w_proj_da[0],
                 w_proj_mem=w_proj_mem[0], w_out=w_out[0], w_ffn_in=w_ffn_in[0], w_ffn_out=w_ffn_out[0])
    big_m = dict(w_in=m_w_in[0], w_mem_kv=m_w_mem_kv[0], w_proj_hg=m_w_proj_hg[0], w_proj_da=m_w_proj_da[0],
                 w_proj_mem=m_w_proj_mem[0], w_out=m_w_out[0], w_ffn_in=m_w_ffn_in[0], w_ffn_out=m_w_ffn_out[0])
    big_v = dict(w_in=v_w_in[0], w_mem_kv=v_w_mem_kv[0], w_proj_hg=v_w_proj_hg[0], w_proj_da=v_w_proj_da[0],
                 w_proj_mem=v_w_proj_mem[0], w_out=v_w_out[0], w_ffn_in=v_w_ffn_in[0], w_ffn_out=v_w_ffn_out[0])

    row_tile = dict(w_in=128, w_mem_kv=128, w_proj_hg=128, w_proj_da=512, w_proj_mem=512, w_out=128,
                    w_ffn_in=128, w_ffn_out=352)
    rest = _BIG_NAMES[1:]
    shards = [big_w[n].astype(BF16) for n in rest]
    state = {}
    big_out = {}

    def reduce_start(group, stacked):
        names = tuple(stacked)
        arrays = [stacked[n] for n in names] + [lax.empty((4,) + stacked[n].shape[1:], BF16) for n in names]
        plan = _reduce_plan_sibling(len(names))
        send, recv, thru, token = _copies_start(f"grads_{group}_sibling_start", arrays, 4 * len(names), plan)
        state[group] = dict(names=names, plan=plan, send=send, recv=recv, arrays=thru)
        return token

    def reduce_middle(group, after):
        st = state[group]
        names, k = st["names"], len(st["names"])
        thru = _copies_wait(f"grads_{group}_sibling_wait", st["send"], st["recv"], st["arrays"], 4 * k, st["plan"], after)
        partials = [_chip_partials(thru[i], thru[k + i], f"chip_partials_{n}", row_tile[n]) for i, n in enumerate(names)]
        arrays = [p[1] for p in partials] + [lax.empty(p[1].shape, BF16) for p in partials]
        plan = _reduce_plan_chips(k)
        send, recv, thru2, token = _copies_start(f"grads_{group}_chips_start", arrays, 3 * k, plan)
        state[group] = dict(names=names, plan=plan, send=send, recv=recv, arrays=thru2, own=[p[0] for p in partials])
        return token

    def reduce_finish(group, after):
        st = state.pop(group)
        names, k = st["names"], len(st["names"])
        thru = _copies_wait(f"grads_{group}_chips_wait", st["send"], st["recv"], st["arrays"], 3 * k, st["plan"], after)
        for i, n in enumerate(names):
            big_out[n] = _adamw_shard(st["own"][i], thru[k + i], big_w[n], big_m[n], big_v[n], "adamw_" + n, row_tile[n])

    win_st = _all_gather([big_w["w_in"].astype(BF16)])[0]
    step = _local_step_stages(x[0], mem[0], loss_target[0], small_w, win_st)
    event, payload = next(step)
    local = None
    while True:
        reply = None
        if event == "begin":
            nr = len(rest)
            me = 4 * lax.axis_index("x") + 2 * lax.axis_index("y") + lax.axis_index("c")
            arrays = shards + [lax.dynamic_update_slice(lax.empty((N_DEV,) + s.shape, BF16), s[None], (me, 0, 0))
                               for s in shards]
            plan = _gather_plan_out(nr)
            send, recv, thru, reply = _copies_start("weights_rest_out_start", arrays, 4 * nr, plan)
            state["gather"] = dict(plan=plan, send=send, recv=recv, arrays=thru)
        elif event == "after_gla_fwd":
            st = state["gather"]
            nr = len(rest)
            thru = _copies_wait("weights_rest_out_wait", st["send"], st["recv"], st["arrays"], 4 * nr, st["plan"], payload)
            plan = _gather_plan_pass(nr)
            send, recv, lands, reply = _copies_start("weights_rest_pass_start", thru[nr:], 3 * nr, plan)
            state["gather"] = dict(plan=plan, send=send, recv=recv, arrays=lands)
        elif event == "need_weights":
            st = state.pop("gather")
            nr = len(rest)
            lands = _copies_wait("weights_rest_pass_wait", st["send"], st["recv"], st["arrays"], 3 * nr, st["plan"], payload)
            reply = dict(zip(rest, lands))
        elif event == "grads_ffn":
            reply = reduce_start("ffn", payload)
        elif event == "after_gate_merge_bwd":
            reply = reduce_middle("ffn", payload)
        elif event == "grads_mix":
            reply = reduce_start("mix", payload)
        elif event == "after_da_bwd_g0":
            reply = reduce_middle("mix", payload)
            reduce_finish("ffn", payload)
        elif event == "after_gla_bwd":
            reduce_finish("mix", payload)
        elif event == "grads_in":
            reply = reduce_start("in", payload)
        elif event == "after_proj_bwd_act_top":
            reply = reduce_middle("in", payload)
        elif event == "end":
            local = payload
            reduce_finish("in", [local["grad_x"]] + [big_out[n][0] for n in rest])
            break
        event, payload = step.send(reply)
    grad_x = local["grad_x"]

    wpack = _pack_rows([small_w[n] for n in _SMALL_NAMES], SMALL_ROWS)
    mpack = _pack_rows([small_m[n] for n in _SMALL_NAMES], SMALL_ROWS)
    vpack = _pack_rows([small_v[n] for n in _SMALL_NAMES], SMALL_ROWS)
    gs, ds, ms, vs = _small_all_reduce_adamw(local["gpart"], local["lbpack"], wpack, mpack, vpack, big_out["w_in"][0])
    loss = gs[_LOSS_ROW, 0]
    small_out = [_unpack_small(t, small_w) for t in (gs, ds, ms, vs)]

    order = ("norm_mix_gain", "norm_mem_gain", "w_in", "lb_logits_fw", "lb_logits_bw", "hg_norm_gain", "da_q_gain",
             "da_k_gain", "w_mem_kv", "mem_q_gain", "mem_k_gain", "w_proj_hg", "w_proj_da", "w_proj_mem", "w_out",
             "norm_ffn_gain", "w_ffn_in", "w_ffn_out")
    outs = [loss, grad_x[None]]
    for kind in range(4):
        for n in order:
            outs.append(big_out[n][kind][None] if n in big_out else small_out[kind][n])
    return tuple(outs)


_BIG_NAMES = ("w_in", "w_mem_kv", "w_proj_hg", "w_proj_da", "w_proj_mem", "w_out", "w_ffn_in", "w_ffn_out")


def _local_step(xs, mems, target, sw, wg):
    step = _local_step_stages(xs, mems, target, sw, wg["w_in"])
    stacked = {}
    event, payload = next(step)
    while event != "end":
        if event.startswith("grads_"):
            stacked.update(payload)
        event, payload = step.send(wg if event == "need_weights" else None)
    return dict(payload, stacked=stacked)


def _local_step_stages(xs, mems, target, sw, win_st):
    norm_mix_gain, norm_mem_gain, norm_ffn_gain = sw["norm_mix_gain"], sw["norm_mem_gain"], sw["norm_ffn_gain"]
    lb_logits_fw, lb_logits_bw, hg_norm_gain = sw["lb_logits_fw"], sw["lb_logits_bw"], sw["hg_norm_gain"]
    da_q_gain, da_k_gain, mem_q_gain, mem_k_gain = sw["da_q_gain"], sw["da_k_gain"], sw["mem_q_gain"], sw["mem_k_gain"]

    token = yield "begin", None
    lb_fw = _lb_table(lb_logits_fw, "lb_table_fw")
    lb_bw = _lb_table(lb_logits_bw, "lb_table_bw")
    lb = jnp.concatenate([lb_fw, lb_bw], axis=0).reshape(2, HG_HEADS, 1, HEAD_DIM)
    h, h_t = _rmsnorm_fwd(xs, _after(norm_mix_gain, token), "norm_mix_fwd", 512, transposed=True)
    proj = _matmul(h, win_st, "nn", F32, 1024, IN_SHARD, D_MODEL, "proj_fwd", b_stacked=True, n_outer=True)
    o_hg, o_pre, states = _gla_fwd(proj, lb, hg_norm_gain)
    token = yield "after_gla_fwd", o_hg
    da = [_da_fwd(proj, _after(da_q_gain, token), da_k_gain, g) for g in range(3)]
    o_da, o_da32, lse_da = _da_merge([t[0] for t in da], [t[1] for t in da], 512)
    wg = yield "need_weights", o_da
    wkv = wg["w_mem_kv"].reshape(D_MODEL, 2 * MEM_WIDTH)
    wphg = wg["w_proj_hg"].reshape(D_MODEL, D_MODEL)
    wpda = jnp.transpose(wg["w_proj_da"], (1, 0, 2)).reshape(DA_WIDTH, D_MODEL)
    wpmem = jnp.transpose(wg["w_proj_mem"], (1, 0, 2)).reshape(MEM_WIDTH, D_MODEL)
    wout = wg["w_out"].reshape(D_MODEL, D_MODEL)
    wfin = jnp.transpose(wg["w_ffn_in"], (1, 0, 2)).reshape(D_MODEL, 2 * D_FF)
    wfout = wg["w_ffn_out"].reshape(D_FF, D_MODEL)
    mem_n = _rmsnorm_fwd(mems, norm_mem_gain, "norm_mem_fwd", N_MEM)
    kv = _matmul(mem_n, wkv, "nn", F32, N_MEM, 1024, D_MODEL, "mem_kv_fwd")
    o_mem = _mem_fwd(proj, kv, mem_q_gain, mem_k_gain, 512)
    t_hg = _matmul(o_hg, wphg, "nn", F32, 512, 1024, D_MODEL, "proj_hg_fwd")
    t_da = _matmul(o_da, wpda, "nn", F32, 512, 1024, DA_WIDTH, "proj_da_fwd")
    t_mem = _matmul(o_mem, wpmem, "nn", F32, 512, 1024, MEM_WIDTH, "proj_mem_fwd")
    merged = _gate_merge_fwd(proj, t_hg, t_da, t_mem, 256)
    u = _matmul(merged, wout, "nn", F32, 512, 1024, D_MODEL, "out_fwd")
    x1, h2, h2_t = _residual_rmsnorm_fwd(xs, u, norm_ffn_gain, "norm_ffn_fwd", 512)
    ffn_a, ffn_b, act, act_t = _ffn_in_swiglu(h2, wfin, 1024, 1408)
    z = _matmul(act, wfout, "nn", F32, 512, 1024, D_FF, "ffn_out_fwd")
    dy, dyb, loss_part = _loss_head(x1, z, target, 512)

    dact = _matmul(dyb, wfout, "nt", F32, 512, 1408, D_MODEL, "ffn_out_bwd_act")
    g_wfout = _matmul(act_t, dyb, "nn", BF16, 1408, 1024, 2048, "ffn_out_bwd_w")
    dab = _swiglu_bwd(ffn_a, ffn_b, dact, 256)
    dh2 = _matmul(dab, wfin, "nt", F32, 1024, 1024, 2816, "ffn_in_bwd_act")
    g_wfin = _matmul(h2_t, dab, "nn", BF16, 1024, 1408, 2048, "ffn_in_bwd_w")
    token = yield "grads_ffn", dict(
        w_ffn_in=jnp.transpose(g_wfin.reshape(D_MODEL, N_DEV, 2 * D_FF // N_DEV), (1, 0, 2)),
        w_ffn_out=g_wfout.reshape(N_DEV, D_FF // N_DEV, D_MODEL))
    dx1, dx1b, g_norm_ffn = _rmsnorm_bwd(x1, dh2, dy, _after(norm_ffn_gain, token), "norm_ffn_bwd", 256)
    dmerged = _matmul(dx1b, wout, "nt", F32, 512, 1024, D_MODEL, "out_bwd_act")
    g_wout = _matmul(merged, dx1b, "tn", BF16, 1024, 1024, 2048, "out_bwd_w")
    dt_hg, dt_da, dt_mem, dgates = _gate_merge_bwd(proj, t_hg, t_da, t_mem, dmerged, 256)
    token = yield "after_gate_merge_bwd", dt_hg
    do_hg = _matmul(dt_hg, wphg, "nt", F32, 512, 1024, D_MODEL, "proj_hg_bwd_act", after=token)
    g_wphg = _matmul(o_hg, dt_hg, "tn", BF16, 1024, 1024, 2048, "proj_hg_bwd_w")
    do_da = _matmul(dt_da, wpda, "nt", F32, 512, DA_WIDTH, D_MODEL, "proj_da_bwd_act")
    g_wpda = _matmul(o_da, dt_da, "tn", BF16, DA_WIDTH, D_MODEL, 2048, "proj_da_bwd_w")
    do_mem = _matmul(dt_mem, wpmem, "nt", F32, 512, MEM_WIDTH, D_MODEL, "proj_mem_bwd_act")
    g_wpmem = _matmul(o_mem, dt_mem, "tn", BF16, MEM_WIDTH, D_MODEL, 2048, "proj_mem_bwd_w")
    by_owner = lambda g: jnp.transpose(g.reshape(g.shape[0], N_DEV, D_MODEL // N_DEV), (1, 0, 2))
    g_wpda, g_wpmem = by_owner(g_wpda), by_owner(g_wpmem)

    dmemq, dk_mem, dv_mem, g_mem_q, g_mem_k = _mem_bwd(proj, kv, mem_q_gain, mem_k_gain, do_mem, 512)
    dkv = jnp.concatenate([dk_mem, dv_mem], axis=1).astype(BF16)
    g_wkv = _matmul(mem_n, dkv, "tn", BF16, 1024, 1024, N_MEM, "mem_kv_bwd_w")
    dmem_n = _matmul(dkv, wkv, "nt", F32, N_MEM, 1024, 1024, "mem_kv_bwd_act")
    g_norm_mem = _gain_grad(mems, dmem_n, "norm_mem_bwd")
    token = yield "grads_mix", dict(
        w_mem_kv=g_wkv.reshape(N_DEV, D_MODEL // N_DEV, 2 * MEM_WIDTH),
        w_proj_hg=g_wphg.reshape(N_DEV, D_MODEL // N_DEV, D_MODEL),
        w_proj_da=g_wpda, w_proj_mem=g_wpmem,
        w_out=g_wout.reshape(N_DEV, D_MODEL // N_DEV, D_MODEL))

    dd_da = _da_rowdot(do_da, o_da32, 512)
    da_b = [_da_bwd(proj, _after(da_q_gain, token), da_k_gain, do_da, lse_da, dd_da, 0)]
    token = yield "after_da_bwd_g0", da_b[0][0]
    da_b += [_da_bwd(proj, _after(da_q_gain, token), da_k_gain, do_da, lse_da, dd_da, g) for g in (1, 2)]
    g_da_q = da_b[0][3] + da_b[1][3] + da_b[2][3]
    g_da_k = da_b[0][4] + da_b[1][4] + da_b[2][4]

    dq_hg, df_hg, dv_hg, dg_hg, dlb, g_hg_norm = _gla_bwd(proj, lb, hg_norm_gain, o_pre, states, do_hg)
    yield "after_gla_bwd", dq_hg

    dproj = jnp.concatenate(
        [dq_hg, df_hg[0], df_hg[1], dv_hg, dg_hg]
        + [t[0] for t in da_b] + [t[1] for t in da_b] + [t[2] for t in da_b] + [dmemq, dgates], axis=1)
    g_win = _matmul(h_t, dproj, "nn", BF16, 1024, IN_SHARD, 2048, "proj_bwd_w", out_stacked=True)
    token = yield "grads_in", dict(w_in=g_win)
    dh = _matmul(dproj, win_st, "nt", F32, 1024, 1024, IN_SHARD, "proj_bwd_act_top", b_stacked=True, after=token,
                 m_blocks=(0, 2))
    token = yield "after_proj_bwd_act_top", dh
    dh = _matmul(dproj, win_st, "nt", F32, 1024, 1024, IN_SHARD, "proj_bwd_act_bottom", b_stacked=True, after=token,
                 m_blocks=(2, 2), out_into=dh)
    grad_x, _, g_norm_mix = _rmsnorm_bwd(xs, dh, dx1, _after(norm_mix_gain, token), "norm_mix_bwd", 256)

    gpart = _pack_rows([g_norm_mix, g_norm_mem, dlb[0], dlb[1], g_norm_ffn, g_hg_norm, g_da_q, g_da_k,
                        g_mem_q, g_mem_k, loss_part], SMALL_GRAD_ROWS)
    lbpack = jnp.concatenate([lb_fw.reshape(8, LANE), lb_bw.reshape(8, LANE)], axis=0)
    yield "end", dict(grad_x=grad_x, gpart=gpart, lbpack=lbpack)
```

```python
import numpy as np
import jax
import jax.numpy as jnp
from jax import lax
from jax.experimental import pallas as pl
from jax.experimental.pallas import tpu as pltpu

F32 = jnp.float32
BF16 = jnp.bfloat16
MESH = pl.DeviceIdType.MESH

SEQ = 4096
D_MODEL = 1024
N_DEV = 8
N_MEM = 256
RMS_EPS = 1e-6
NEG_INF = -1e30
LANE = 128
HEAD_DIM = 128
HG_HEADS = 8
HG_CHUNK = 64
HG_SCALE = HEAD_DIM ** -0.5
DA_DILATIONS = (1, 4, 16)
DA_RADIUS = 64
DA_HEADS_PER_GROUP = 4
DA_HEADS = 12
DA_WIDTH = 512
DA_SCALE = HEAD_DIM ** -0.5
DA_QB = 128
DA_WIN = 256
DA_WAYS = 4
DA_FWD_WAYS = 4
MEM_HEADS = 4
MEM_WIDTH = 512
MEM_SCALE = HEAD_DIM ** -0.5
D_FF = 2816
IN_COLS = 13312
IN_SHARD = IN_COLS // N_DEV
CB_HG_Q, CB_F, CB_HG_I, CB_HG_G = 0, 8, 24, 32
CB_DA_Q, CB_DA_K, CB_DA_V, CB_MEM_Q = 40, 52, 64, 76
N_CB = IN_COLS // LANE
ADAM_LR, ADAM_B1, ADAM_B2, ADAM_EPS, ADAM_WD, ADAM_STEP = 0.001, 0.9, 0.999, 1e-08, 0.01, 10
VMEM_BYTES_V7X = 64 * 1024 * 1024
SMALL_ROWS = 104
SMALL_GRAD_ROWS = 88

_NN = (((1,), (0,)), ((), ()))
_NT = (((1,), (1,)), ((), ()))
_TN = (((0,), (0,)), ((), ()))


def _dot(a, b, dims):
    return lax.dot_general(a.astype(BF16), b.astype(BF16), dims, preferred_element_type=F32)


def _dot_exact(a, b, dims):
    return lax.dot_general(a, b, dims, precision=lax.Precision.HIGHEST, preferred_element_type=F32)


def _sigmoid(x):
    return 0.5 * jnp.tanh(0.5 * x) + 0.5


def _params(semantics, est_bytes):
    limit = int(min(VMEM_BYTES_V7X - (6 << 20), max(56 << 20, est_bytes * 3 // 2)))
    return pltpu.CompilerParams(dimension_semantics=semantics, vmem_limit_bytes=limit)


def _nbytes(shape, dtype):
    return int(np.prod(shape)) * jnp.dtype(dtype).itemsize


def _alibi_slopes(n):
    return (2.0 ** (-8.0 * np.arange(1, n + 1) / n)).astype(np.float32)


def _matmul(a, b, mode, out_dtype, tm, tn, tk, name, b_stacked=False, out_stacked=False, n_outer=False, after=None,
            m_blocks=None, out_into=None):
    if mode == "tn":
        kdim, m = a.shape
    else:
        m, kdim = a.shape
    if b_stacked:
        if mode == "nn":
            n = b.shape[0] * b.shape[2]
            assert tn == b.shape[2] and tk == kdim == b.shape[1]
        else:
            assert mode == "nt" and tk == b.shape[2] and b.shape[0] * tk == kdim
            n = b.shape[1]
    else:
        n = b.shape[0] if mode == "nt" else b.shape[1]
    assert m % tm == 0 and n % tn == 0 and kdim % tk == 0
    gm, gn, gk = m // tm, n // tn, kdim // tk
    i0 = 0
    if m_blocks is not None:
        assert mode != "tn" and not out_stacked
        i0, gm = m_blocks

    def ijk(f):
        if n_outer:
            return lambda j, i, k: f(i + i0, j, k)
        return lambda i, j, k: f(i + i0, j, k)

    if mode == "tn":
        a_spec = pl.BlockSpec((tk, tm), ijk(lambda i, j, k: (k, i)))
    else:
        a_spec = pl.BlockSpec((tm, tk), ijk(lambda i, j, k: (i, k)))
    if b_stacked and mode == "nn":
        b_spec = pl.BlockSpec((None, tk, tn), ijk(lambda i, j, k: (j, 0, 0)))
    elif b_stacked:
        b_spec = pl.BlockSpec((None, tn, tk), ijk(lambda i, j, k: (k, j, 0)))
    elif mode == "nt":
        b_spec = pl.BlockSpec((tn, tk), ijk(lambda i, j, k: (j, k)))
    else:
        b_spec = pl.BlockSpec((tk, tn), ijk(lambda i, j, k: (k, j)))
    if out_stacked:
        assert tm == m
        out_shape = jax.ShapeDtypeStruct((gn, m, tn), out_dtype)
        o_spec = pl.BlockSpec((None, tm, tn), ijk(lambda i, j, k: (j, i, 0)))
    else:
        out_shape = jax.ShapeDtypeStruct((m, n), out_dtype)
        o_spec = pl.BlockSpec((tm, tn), ijk(lambda i, j, k: (i, j)))
    dims = {"nn": _NN, "nt": _NT, "tn": _TN}[mode]

    n_in = 2 + (after is not None) + (out_into is not None)

    def body(*refs):
        a_ref, b_ref, o_ref = refs[0], refs[1], refs[n_in]
        part = _dot(a_ref[...], b_ref[...], dims)
        if gk == 1:
            o_ref[...] = part.astype(out_dtype)
            return
        acc_ref = refs[-1]
        k = pl.program_id(2)

        @pl.when(k == 0)
        def _():
            acc_ref[...] = part

        @pl.when(jnp.logical_and(k > 0, k < gk - 1))
        def _():
            acc_ref[...] += part

        @pl.when(k == gk - 1)
        def _():
            o_ref[...] = (acc_ref[...] + part).astype(out_dtype)

    a_tile = _nbytes((tm, tk), a.dtype)
    b_tile = _nbytes((tk, tn), b.dtype)
    o_tile = _nbytes((tm, tn), out_dtype)
    est = 2 * (a_tile + b_tile + o_tile) + 3 * tm * tn * 4 + (a_tile + b_tile)
    grid = (gn, gm, gk) if n_outer else (gm, gn, gk)
    operands, in_specs = [a, b], [a_spec, b_spec]
    if after is not None:
        operands.append(after)
        in_specs.append(pl.BlockSpec(memory_space=pl.ANY))
    aliases = {}
    if out_into is not None:
        aliases = {len(operands): 0}
        operands.append(out_into)
        in_specs.append(pl.BlockSpec(memory_space=pl.ANY))
    return pl.pallas_call(
        body, name=name, grid=grid, in_specs=in_specs, out_specs=o_spec, out_shape=out_shape,
        scratch_shapes=[] if gk == 1 else [pltpu.VMEM((tm, tn), F32)], input_output_aliases=aliases,
        compiler_params=_params(("parallel", "parallel", "arbitrary"), est),
    )(*operands)


def _row_spec(tr, width, col_block=0):
    return pl.BlockSpec((tr, width), lambda i: (i, col_block))


def _bcast_spec(width):
    return pl.BlockSpec((1, width), lambda i: (0, 0))


def _col_spec(width, tr):
    return pl.BlockSpec((width, tr), lambda i: (0, i))


def _rmsnorm_fwd(x, gain, name, tr, transposed=False):
    rows, width = x.shape

    def body(x_ref, g_ref, o_ref, *t_ref):
        xv = x_ref[...]
        r = lax.rsqrt(jnp.mean(xv * xv, axis=-1, keepdims=True) + RMS_EPS)
        h = xv * r * g_ref[...]
        o_ref[...] = h.astype(BF16)
        if transposed:
            t_ref[0][...] = h.T.astype(BF16)

    out_specs, out_shape = [_row_spec(tr, width)], [jax.ShapeDtypeStruct((rows, width), BF16)]
    if transposed:
        out_specs.append(_col_spec(width, tr))
        out_shape.append(jax.ShapeDtypeStruct((width, rows), BF16))
    out = pl.pallas_call(
        body, name=name, grid=(rows // tr,), in_specs=[_row_spec(tr, width), _bcast_spec(width)],
        out_specs=out_specs, out_shape=out_shape,
        compiler_params=_params(("parallel",), 10 * tr * width * 4),
    )(x, gain)
    return out if transposed else out[0]


def _residual_rmsnorm_fwd(x, u, gain, name, tr):
    rows, width = x.shape

    def body(x_ref, u_ref, g_ref, x1_ref, h_ref, ht_ref):
        xv = x_ref[...] + u_ref[...]
        x1_ref[...] = xv
        r = lax.rsqrt(jnp.mean(xv * xv, axis=-1, keepdims=True) + RMS_EPS)
        h = xv * r * g_ref[...]
        h_ref[...] = h.astype(BF16)
        ht_ref[...] = h.T.astype(BF16)

    return pl.pallas_call(
        body, name=name, grid=(rows // tr,),
        in_specs=[_row_spec(tr, width), _row_spec(tr, width), _bcast_spec(width)],
        out_specs=[_row_spec(tr, width), _row_spec(tr, width), _col_spec(width, tr)],
        out_shape=[jax.ShapeDtypeStruct((rows, width), F32), jax.ShapeDtypeStruct((rows, width), BF16),
                   jax.ShapeDtypeStruct((width, rows), BF16)],
        compiler_params=_params(("parallel",), 14 * tr * width * 4),
    )(x, u, gain)


def _rmsnorm_bwd(x, dh, dres, gain, name, tr):
    rows, width = x.shape

    def body(x_ref, dh_ref, dres_ref, g_ref, dx_ref, dxb_ref, dg_ref):
        xv = x_ref[...]
        r = lax.rsqrt(jnp.mean(xv * xv, axis=-1, keepdims=True) + RMS_EPS)
        xhat = xv * r
        dhv = dh_ref[...]
        dyg = dhv * g_ref[...]
        dx = dres_ref[...] + r * (dyg - xhat * jnp.mean(dyg * xhat, axis=-1, keepdims=True))
        dx_ref[...] = dx
        dxb_ref[...] = dx.astype(BF16)
        part = jnp.sum(dhv * xhat, axis=0, keepdims=True)

        @pl.when(pl.program_id(0) == 0)
        def _():
            dg_ref[...] = part

        @pl.when(pl.program_id(0) > 0)
        def _():
            dg_ref[...] += part

    return pl.pallas_call(
        body, name=name, grid=(rows // tr,),
        in_specs=[_row_spec(tr, width), _row_spec(tr, width), _row_spec(tr, width), _bcast_spec(width)],
        out_specs=[_row_spec(tr, width), _row_spec(tr, width), _bcast_spec(width)],
        out_shape=[jax.ShapeDtypeStruct((rows, width), F32), jax.ShapeDtypeStruct((rows, width), BF16),
                   jax.ShapeDtypeStruct((1, width), F32)],
        compiler_params=_params(("arbitrary",), 14 * tr * width * 4),
    )(x, dh, dres, gain)


def _gain_grad(x, dh, name):
    rows, width = x.shape

    def body(x_ref, dh_ref, dg_ref):
        xv = x_ref[...]
        r = lax.rsqrt(jnp.mean(xv * xv, axis=-1, keepdims=True) + RMS_EPS)
        dg_ref[...] = jnp.sum(dh_ref[...] * xv * r, axis=0, keepdims=True)

    return pl.pallas_call(
        body, name=name, grid=(1,), in_specs=[_row_spec(rows, width), _row_spec(rows, width)],
        out_specs=_bcast_spec(width), out_shape=jax.ShapeDtypeStruct((1, width), F32),
        compiler_params=_params(("arbitrary",), 6 * rows * width * 4),
    )(x, dh)


def _lb_table(logits, name):
    slots, width = logits.shape

    def body(l_ref, o_ref):
        lv = l_ref[...]
        mx = jnp.max(lv, axis=0, keepdims=True)
        e = jnp.exp(lv - mx)
        o_ref[...] = e[0:1, :] / jnp.sum(e, axis=0, keepdims=True)

    return pl.pallas_call(
        body, name=name, grid=(1,), in_specs=[pl.BlockSpec((slots, width), lambda i: (0, 0))],
        out_specs=_bcast_spec(width), out_shape=jax.ShapeDtypeStruct((1, width), F32),
    )(logits)


def _gate_merge_fwd(proj, t_hg, t_da, t_mem, tr):
    w = D_MODEL

    def body(ghg_ref, gda_ref, gmem_ref, thg_ref, tda_ref, tmem_ref, o_ref):
        acc = _sigmoid(ghg_ref[...]) * thg_ref[...]
        acc += _sigmoid(gda_ref[...]) * tda_ref[...]
        acc += _sigmoid(gmem_ref[...]) * tmem_ref[...]
        o_ref[...] = acc.astype(BF16)

    return pl.pallas_call(
        body, name="gate_merge_fwd", grid=(SEQ // tr,),
        in_specs=[_row_spec(tr, w, 10), _row_spec(tr, w, 11), _row_spec(tr, w, 12),
                  _row_spec(tr, w), _row_spec(tr, w), _row_spec(tr, w)],
        out_specs=_row_spec(tr, w), out_shape=jax.ShapeDtypeStruct((SEQ, w), BF16),
        compiler_params=_params(("parallel",), 16 * tr * w * 4),
    )(proj, proj, proj, t_hg, t_da, t_mem)


def _dproj_buffer():
    return lax.empty((SEQ, IN_COLS), BF16)


def _gate_bwd(proj, t, dmerged, dproj, branch, tr):
    w = D_MODEL
    gate_block = 10 + branch

    def body(g_ref, t_ref, dm_ref, dproj_in, dt_ref, dg_ref):
        del dproj_in
        dm = dm_ref[...]
        s = _sigmoid(g_ref[...])
        dt_ref[...] = (s * dm).astype(BF16)
        dg_ref[...] = (dm * t_ref[...] * s * (1.0 - s)).astype(BF16)

    return pl.pallas_call(
        body, name=f"gate_bwd_{branch}", grid=(SEQ // tr,),
        in_specs=[_row_spec(tr, w, gate_block), _row_spec(tr, w), _row_spec(tr, w), _any_spec()],
        out_specs=[_row_spec(tr, w), _row_spec(tr, w, gate_block)],
        out_shape=[jax.ShapeDtypeStruct((SEQ, w), BF16), jax.ShapeDtypeStruct((SEQ, IN_COLS), BF16)],
        input_output_aliases={3: 1},
        compiler_params=_params(("parallel",), 12 * tr * w * 4),
    )(proj, t, dmerged, dproj)


def _ffn_in_swiglu(h2, wfin, tm, tn):
    gm, gn = SEQ // tm, D_FF // tn

    def body(h_ref, wa_ref, wb_ref, a_ref, b_ref, act_ref, actt_ref):
        h = h_ref[...]
        a = _dot(h, wa_ref[...], _NN)
        b = _dot(h, wb_ref[...], _NN)
        act = a * _sigmoid(a) * b
        a_ref[...] = a.astype(BF16)
        b_ref[...] = b.astype(BF16)
        act_ref[...] = act.astype(BF16)
        actt_ref[...] = act.T.astype(BF16)

    tile = pl.BlockSpec((tm, tn), lambda j, i: (i, j))
    shape = jax.ShapeDtypeStruct((SEQ, D_FF), BF16)
    return pl.pallas_call(
        body, name="ffn_in_swiglu_fwd", grid=(gn, gm),
        in_specs=[pl.BlockSpec((tm, D_MODEL), lambda j, i: (i, 0)),
                  pl.BlockSpec((D_MODEL, tn), lambda j, i: (0, j)),
                  pl.BlockSpec((D_MODEL, tn), lambda j, i: (0, gn + j))],
        out_specs=[tile, tile, tile, pl.BlockSpec((tn, tm), lambda j, i: (j, i))],
        out_shape=[shape, shape, shape, jax.ShapeDtypeStruct((D_FF, SEQ), BF16)],
        compiler_params=_params(("parallel", "parallel"), 4 * tm * D_MODEL + 8 * D_MODEL * tn + 16 * tm * tn
                                + 6 * tm * tn * 4),
    )(h2, wfin, wfin)


def _swiglu_bwd(a, b, dact, tr):
    def body(a_ref, b_ref, d_ref, o_ref):
        av = a_ref[...].astype(F32)
        bv = b_ref[...].astype(F32)
        d = d_ref[...]
        s = _sigmoid(av)
        silu = av * s
        o_ref[:, :D_FF] = (d * bv * (s + silu * (1.0 - s))).astype(BF16)
        o_ref[:, D_FF:] = (d * silu).astype(BF16)

    return pl.pallas_call(
        body, name="swiglu_bwd", grid=(SEQ // tr,), in_specs=[_row_spec(tr, D_FF)] * 3,
        out_specs=_row_spec(tr, 2 * D_FF), out_shape=jax.ShapeDtypeStruct((SEQ, 2 * D_FF), BF16),
        compiler_params=_params(("parallel",), 10 * tr * 2 * D_FF * 4),
    )(a, b, dact)


def _loss_head(x1, z, target, tr):
    w = D_MODEL

    def body(x_ref, z_ref, t_ref, dy_ref, dyb_ref, loss_ref, acc_ref):
        err = x_ref[...] + z_ref[...] - t_ref[...]
        dy = err * (1.0 / w)
        dy_ref[...] = dy
        dyb_ref[...] = dy.astype(BF16)
        part = jnp.sum(err * err, axis=0, keepdims=True)

        @pl.when(pl.program_id(0) == 0)
        def _():
            acc_ref[...] = part

        @pl.when(pl.program_id(0) > 0)
        def _():
            acc_ref[...] += part

        @pl.when(pl.program_id(0) == SEQ // tr - 1)
        def _():
            total = jnp.sum(acc_ref[...], axis=1, keepdims=True) * (0.5 / w)
            loss_ref[...] = jnp.broadcast_to(total, (1, LANE))

    return pl.pallas_call(
        body, name="loss_head", grid=(SEQ // tr,),
        in_specs=[_row_spec(tr, w), _row_spec(tr, w), _row_spec(tr, w)],
        out_specs=[_row_spec(tr, w), _row_spec(tr, w), _bcast_spec(LANE)],
        out_shape=[jax.ShapeDtypeStruct((SEQ, w), F32), jax.ShapeDtypeStruct((SEQ, w), BF16),
                   jax.ShapeDtypeStruct((1, LANE), F32)],
        scratch_shapes=[pltpu.VMEM((1, w), F32)],
        compiler_params=_params(("arbitrary",), 12 * tr * w * 4),
    )(x1, z, target)


GLA_ROWS = 256
GLA_CPB = GLA_ROWS // HG_CHUNK
GLA_NBLK = SEQ // GLA_ROWS
GLA_WAYS = 4
GLA_TRIPS = GLA_NBLK // GLA_WAYS
GLA_GRAD_WAYS = 4
GLA_GRAD_TRIPS = GLA_NBLK // GLA_GRAD_WAYS
GLA_NCK = SEQ // HG_CHUNK


def _dot_split(m, xv, dims):
    hi = xv.astype(BF16)
    r1 = xv - hi.astype(F32)
    mid = r1.astype(BF16)
    lo = (r1 - mid.astype(F32)).astype(BF16)
    dot = lambda t: lax.dot_general(m, t, dims, preferred_element_type=F32)
    return (dot(lo) + dot(mid)) + dot(hi)


def _gla_block(qc, fc, lbv, masks):
    mask, maskb, direction = masks
    sq = _sigmoid(qc)
    q = qc * sq * HG_SCALE
    sf = _sigmoid(fc)
    forget = lbv + (1.0 - lbv) * sf
    k = 1.0 - forget
    logf = jnp.log(forget)
    b = _dot_split(maskb, logf, _NN)
    ends = []
    for j in range(GLA_CPB):
        lo, hi = j * HG_CHUNK, (j + 1) * HG_CHUNK
        end = jnp.where(direction == 0, b[hi - 1:hi, :], b[lo:lo + 1, :])
        ends.append(jnp.broadcast_to(end, (HG_CHUNK, HEAD_DIM)))
    bt = jnp.concatenate(ends, axis=0)
    eb = jnp.exp(b)
    qt = q * eb
    kt = k * jnp.exp(-b)
    kh = k * jnp.exp(bt - b)
    a = jnp.where(mask, _dot(qt, kt, _NT), 0.0)
    return dict(sq=sq, sf=sf, forget=forget, k=k, b=b, bt=bt, eb=eb, qt=qt, kt=kt, kh=kh, a=a)


def _gla_masks(direction):
    row = lax.broadcasted_iota(jnp.int32, (GLA_ROWS, GLA_ROWS), 0)
    col = lax.broadcasted_iota(jnp.int32, (GLA_ROWS, GLA_ROWS), 1)
    same = (row // HG_CHUNK) == (col // HG_CHUNK)
    mask = jnp.logical_and(same, jnp.where(direction == 0, row - col, col - row) >= 0)
    return mask, jnp.where(mask, 1.0, 0.0).astype(BF16), direction


def _gla_chunk_rows(j):
    return slice(j * HG_CHUNK, (j + 1) * HG_CHUNK)


def _gla_block_rows(b):
    return pl.ds(pl.multiple_of(b * GLA_ROWS, GLA_ROWS), GLA_ROWS)


def _head_spec(col_block0):
    return pl.BlockSpec((SEQ, HEAD_DIM), lambda h, d: (0, col_block0 + h))


def _gla_fwd(proj, lb, gain):
    nck = GLA_NCK

    def body(q_ref, f_ref, v_ref, g_ref, lb_ref, gain_ref, ohg_ref, opre_ref, st_ref, qt_scr, cs_scr, dec_scr):
        d = pl.program_id(1)
        masks = _gla_masks(d)
        lbv = lb_ref[...]

        @pl.when(d == 0)
        def _():
            opre_ref[...] = jnp.zeros_like(opre_ref)

        def intra(s, carry):
            blocks = [s + w * GLA_TRIPS for w in range(GLA_WAYS)]
            rows = [_gla_block_rows(b) for b in blocks]
            loaded = [(q_ref[r, :], f_ref[r, :], v_ref[r, :], opre_ref[r, :]) for r in rows]
            results = []
            for qc, fc, v, o_prev in loaded:
                ck = _gla_block(qc, fc, lbv, masks)
                o = o_prev + _dot(ck["a"], v, _NN)
                cs = [_dot(v[_gla_chunk_rows(j), :], ck["kh"][_gla_chunk_rows(j), :], _TN) for j in range(GLA_CPB)]
                dec = [jnp.exp(ck["bt"][j * HG_CHUNK:j * HG_CHUNK + 8, :]) for j in range(GLA_CPB)]
                results.append((o, ck["qt"].astype(BF16), cs, dec))
            for b, r, (o, qt, cs, dec) in zip(blocks, rows, results):
                opre_ref[r, :] = o
                qt_scr[r, :] = qt
                for j in range(GLA_CPB):
                    cs_scr[b * GLA_CPB + j] = cs[j]
                    dec_scr[b * GLA_CPB + j] = dec[j]
            return carry

        lax.fori_loop(0, GLA_TRIPS, intra, 0)

        def scan(i, st):
            c = jnp.where(d == 0, i, nck - 1 - i)
            st_ref[c] = st
            return st * dec_scr[c][0:1, :] + cs_scr[c]

        lax.fori_loop(0, nck, scan, jnp.zeros((HEAD_DIM, HEAD_DIM), F32), unroll=4)

        def inter(c, carry):
            rows = pl.ds(pl.multiple_of(c * HG_CHUNK, HG_CHUNK), HG_CHUNK)
            opre_ref[rows, :] += _dot(qt_scr[rows, :], st_ref[c], _NT)
            return carry

        lax.fori_loop(0, nck, inter, 0, unroll=4)

        @pl.when(d == 1)
        def _():
            o = opre_ref[...]
            r = lax.rsqrt(jnp.mean(o * o, axis=-1, keepdims=True) + RMS_EPS)
            g = g_ref[...]
            ohg_ref[...] = (o * r * gain_ref[...] * (g * _sigmoid(g))).astype(BF16)

    blk = SEQ * HEAD_DIM * 4
    return pl.pallas_call(
        body, name="gla_fwd", grid=(HG_HEADS, 2),
        in_specs=[_head_spec(CB_HG_Q),
                  pl.BlockSpec((SEQ, HEAD_DIM), lambda h, d: (0, CB_F + 8 * d + h)),
                  _head_spec(CB_HG_I), _head_spec(CB_HG_G),
                  pl.BlockSpec((None, None, 1, HEAD_DIM), lambda h, d: (d, h, 0, 0)),
                  pl.BlockSpec((1, HEAD_DIM), lambda h, d: (0, 0))],
        out_specs=[_head_spec(0), _head_spec(0),
                   pl.BlockSpec((None, None, nck, HEAD_DIM, HEAD_DIM), lambda h, d: (h, d, 0, 0, 0))],
        out_shape=[jax.ShapeDtypeStruct((SEQ, D_MODEL), BF16), jax.ShapeDtypeStruct((SEQ, D_MODEL), F32),
                   jax.ShapeDtypeStruct((HG_HEADS, 2, nck, HEAD_DIM, HEAD_DIM), F32)],
        scratch_shapes=[pltpu.VMEM((SEQ, HEAD_DIM), BF16), pltpu.VMEM((nck, HEAD_DIM, HEAD_DIM), F32),
                        pltpu.VMEM((nck, 8, HEAD_DIM), F32)],
        compiler_params=_params(("parallel", "arbitrary"), 8 * blk + 3 * blk + 2 * blk + 2 * blk + blk // 2),
    )(proj, proj, proj, proj, lb, gain)


def _gla_bwd(proj, lb, gain, o_pre, states, do_hg, dproj):
    nck = GLA_NCK
    do, dproj, dgain = _gla_bwd_norm(proj, gain, o_pre, do_hg, dproj, 256)

    def body(q_ref, f_ref, v_ref, lb_ref, do_ref, st_ref, dproj_in, dproj_ref, dlb_ref,
             dq_acc, dv_acc, dst_scr, cs_scr, dec_scr, df_out, dq_out, dv_out, sems):
        del dproj_in
        h, d = pl.program_id(0), pl.program_id(1)
        masks = _gla_masks(d)
        mask, maskb, _ = masks
        lbv = lb_ref[...]

        def column_copy(k, staging, col_block):
            cols = pl.ds(pl.multiple_of(col_block * LANE, LANE), LANE)
            return pltpu.make_async_copy(staging, dproj_ref.at[:, cols], sems.at[k])

        df_copy = column_copy(0, df_out, CB_F + 8 * d + h)
        dq_copy = column_copy(1, dq_out, CB_HG_Q + h)
        dv_copy = column_copy(2, dv_out, CB_HG_I + h)
        last = jnp.logical_and(h == HG_HEADS - 1, d == 1)

        def intra(s, carry):
            blocks = [s + w * GLA_TRIPS for w in range(GLA_WAYS)]
            loaded = [(q_ref[r, :], f_ref[r, :], do_ref[r, :]) for r in map(_gla_block_rows, blocks)]
            results = []
            for qc, fc, doc in loaded:
                ck = _gla_block(qc, fc, lbv, masks)
                cs = [_dot(doc[_gla_chunk_rows(j), :], ck["qt"][_gla_chunk_rows(j), :], _TN) for j in range(GLA_CPB)]
                dec = [jnp.exp(ck["bt"][j * HG_CHUNK:j * HG_CHUNK + 8, :]) for j in range(GLA_CPB)]
                results.append((cs, dec))
            for b, (cs, dec) in zip(blocks, results):
                for j in range(GLA_CPB):
                    cs_scr[b * GLA_CPB + j] = cs[j]
                    dec_scr[b * GLA_CPB + j] = dec[j]
            return carry

        lax.fori_loop(0, GLA_TRIPS, intra, 0)

        def scan(i, dst):
            c = jnp.where(d == 0, nck - 1 - i, i)
            dst_scr[c] = dst
            return dst * dec_scr[c][0:1, :] + cs_scr[c]

        lax.fori_loop(0, nck, scan, jnp.zeros((HEAD_DIM, HEAD_DIM), F32), unroll=4)

        @pl.when(d == 0)
        def _():
            dq_acc[...] = jnp.zeros_like(dq_acc)
            dv_acc[...] = jnp.zeros_like(dv_acc)

        def block_grads(qc, fc, v, doc, states_in, dstates, decays):
            ck = _gla_block(qc, fc, lbv, masks)
            qt, kt, kh, a = ck["qt"], ck["kt"], ck["kh"], ck["a"]
            da = jnp.where(mask, _dot(doc, v, _NT), 0.0)
            dqt_i = _dot(da, kt, _NN)
            dkt = _dot(da, qt, _TN)
            dv_i = _dot(a, doc, _TN)
            dqt_p, dv_p, dkh_p, dbt_p = [], [], [], []
            for j in range(GLA_CPB):
                cr = _gla_chunk_rows(j)
                st_in, dst = states_in[j], dstates[j]
                dqt_p.append(dqt_i[cr, :] + _dot(doc[cr, :], st_in, _NN))
                dv_p.append(dv_i[cr, :] + _dot(kh[cr, :], dst, _NT))
                dkh_j = _dot(v[cr, :], dst, _NN)
                dkh_p.append(dkh_j)
                dbt_j = (decays[j][0:1, :] * jnp.sum(dst * st_in, axis=0, keepdims=True)
                         + jnp.sum(dkh_j * kh[cr, :], axis=0, keepdims=True))
                dbt_p.append(jnp.broadcast_to(dbt_j, (HG_CHUNK, HEAD_DIM)))
            dqt = jnp.concatenate(dqt_p, axis=0)
            dv = jnp.concatenate(dv_p, axis=0)
            dkh = jnp.concatenate(dkh_p, axis=0)
            dbt = jnp.concatenate(dbt_p, axis=0)
            db = dqt * qt - dkt * kt - dkh * kh
            dq = dqt * ck["eb"]
            dk = dkt * jnp.exp(-ck["b"]) + dkh * jnp.exp(ck["bt"] - ck["b"])
            dlogf = _dot_split(maskb, db, _TN) + dbt
            dforget = dlogf / ck["forget"] - dk
            sf, sq = ck["sf"], ck["sq"]
            df = (dforget * (1.0 - lbv) * sf * (1.0 - sf)).astype(BF16)
            dqc = dq * HG_SCALE * (sq + qc * sq * (1.0 - sq))
            return df, dqc, dv, jnp.sum(dforget * (1.0 - sf), axis=0, keepdims=True)

        def grads(s, dlb):
            blocks = [s + w * GLA_GRAD_TRIPS for w in range(GLA_GRAD_WAYS)]
            rows = [_gla_block_rows(b) for b in blocks]
            loaded = []
            for b, r in zip(blocks, rows):
                chunks = [b * GLA_CPB + j for j in range(GLA_CPB)]
                loaded.append((q_ref[r, :], f_ref[r, :], v_ref[r, :], do_ref[r, :], [st_ref[c] for c in chunks],
                               [dst_scr[c] for c in chunks], [dec_scr[c] for c in chunks], dq_acc[r, :], dv_acc[r, :]))
            results = [block_grads(*t[:7]) + (t[7], t[8]) for t in loaded]
            for r, (df, dqc, dv, dlb_part, dq_prev, dv_prev) in zip(rows, results):
                df_out[r, :] = df
                dq_acc[r, :] = dq_prev + dqc
                dv_acc[r, :] = dv_prev + dv
                dlb = dlb + dlb_part
            return dlb

        @pl.when(jnp.logical_or(h > 0, d > 0))
        def _():
            df_copy.wait()

        dlb_ref[...] = lax.fori_loop(0, GLA_GRAD_TRIPS, grads, jnp.zeros((1, HEAD_DIM), F32))
        df_copy.start()

        @pl.when(d == 1)
        def _():
            @pl.when(h > 0)
            def _():
                dq_copy.wait()
                dv_copy.wait()

            dq_out[...] = dq_acc[...].astype(BF16)
            dv_out[...] = dv_acc[...].astype(BF16)
            dq_copy.start()
            dv_copy.start()

        @pl.when(last)
        def _():
            df_copy.wait()
            dq_copy.wait()
            dv_copy.wait()

    blk = SEQ * HEAD_DIM * 4
    state_bytes = nck * HEAD_DIM * HEAD_DIM * 4
    dproj, dlb = pl.pallas_call(
        body, name="gla_bwd", grid=(HG_HEADS, 2),
        in_specs=[_head_spec(CB_HG_Q),
                  pl.BlockSpec((SEQ, HEAD_DIM), lambda h, d: (0, CB_F + 8 * d + h)),
                  _head_spec(CB_HG_I),
                  pl.BlockSpec((None, None, 1, HEAD_DIM), lambda h, d: (d, h, 0, 0)),
                  _head_spec(0),
                  pl.BlockSpec((None, None, nck, HEAD_DIM, HEAD_DIM), lambda h, d: (h, d, 0, 0, 0)),
                  _any_spec()],
        out_specs=[_any_spec(), pl.BlockSpec((None, None, 1, HEAD_DIM), lambda h, d: (d, h, 0, 0))],
        out_shape=[jax.ShapeDtypeStruct((SEQ, IN_COLS), BF16), jax.ShapeDtypeStruct((2, HG_HEADS, 1, HEAD_DIM), F32)],
        scratch_shapes=[pltpu.VMEM((SEQ, HEAD_DIM), F32)] * 2
        + [pltpu.VMEM((nck, HEAD_DIM, HEAD_DIM), F32)] * 2 + [pltpu.VMEM((nck, 8, HEAD_DIM), F32)]
        + [pltpu.VMEM((SEQ, HEAD_DIM), BF16)] * 3 + [pltpu.SemaphoreType.DMA((3,))],
        input_output_aliases={6: 0},
        compiler_params=_params(("arbitrary", "arbitrary"), 8 * blk + 4 * state_bytes + 3 * blk + 2 * blk),
    )(proj, proj, proj, lb, do, states, dproj)
    return dproj, dlb, dgain


def _gla_bwd_norm(proj, gain, o_pre, do_hg, dproj, tr):
    w = D_MODEL

    def body(g_ref, gain_ref, opre_ref, dohg_ref, dproj_in, do_ref, dg_ref, dgain_ref):
        del dproj_in
        gainv = gain_ref[...]
        part = jnp.zeros((1, HEAD_DIM), F32)
        for h in range(HG_HEADS):
            hs = slice(h * HEAD_DIM, (h + 1) * HEAD_DIM)
            o = opre_ref[:, hs]
            r = lax.rsqrt(jnp.mean(o * o, axis=-1, keepdims=True) + RMS_EPS)
            ohat = o * r
            g = g_ref[:, hs]
            sg = _sigmoid(g)
            silu = g * sg
            dout = dohg_ref[:, hs]
            dg_ref[:, hs] = (dout * ohat * gainv * (sg + silu * (1.0 - sg))).astype(BF16)
            dy = dout * silu
            part = part + jnp.sum(dy * ohat, axis=0, keepdims=True)
            dn = dy * gainv
            do_ref[:, hs] = r * (dn - ohat * jnp.mean(dn * ohat, axis=-1, keepdims=True))

        @pl.when(pl.program_id(0) == 0)
        def _():
            dgain_ref[...] = part

        @pl.when(pl.program_id(0) > 0)
        def _():
            dgain_ref[...] += part

    return pl.pallas_call(
        body, name="gla_bwd_norm", grid=(SEQ // tr,),
        in_specs=[_row_spec(tr, w, CB_HG_G * LANE // w), _bcast_spec(HEAD_DIM), _row_spec(tr, w), _row_spec(tr, w),
                  _any_spec()],
        out_specs=[_row_spec(tr, w), _row_spec(tr, w, CB_HG_G * LANE // w), _bcast_spec(HEAD_DIM)],
        out_shape=[jax.ShapeDtypeStruct((SEQ, w), F32), jax.ShapeDtypeStruct((SEQ, IN_COLS), BF16),
                   jax.ShapeDtypeStruct((1, HEAD_DIM), F32)],
        input_output_aliases={4: 1},
        compiler_params=_params(("arbitrary",), 16 * tr * w * 4),
    )(proj, gain, o_pre, do_hg, dproj)


def _da_residue_rows(r, n0, size, d):
    if d == 1:
        return pl.ds(pl.multiple_of(n0, 8), size)
    return pl.ds(r + n0 * d, size, stride=d)


def _da_rmsnorm(x, gain):
    r = lax.rsqrt(jnp.mean(x * x, axis=-1, keepdims=True) + RMS_EPS)
    return x * r, r, x * r * gain


def _da_scores(qn_scr, kn_scr, slope, i, ld):
    w0 = jnp.clip(i * DA_QB - DA_RADIUS, 0, ld - DA_WIN)
    w0 = pl.multiple_of(w0, DA_RADIUS)
    qrows = pl.ds(pl.multiple_of(i * DA_QB, DA_QB), DA_QB)
    win = pl.ds(w0, DA_WIN)
    qb = qn_scr[qrows, :]
    kw = kn_scr[win, :]
    s = _dot(qb, kw, _NT) * DA_SCALE
    qpos = i * DA_QB + lax.broadcasted_iota(jnp.int32, (DA_QB, DA_WIN), 0)
    kpos = w0 + lax.broadcasted_iota(jnp.int32, (DA_QB, DA_WIN), 1)
    arel = jnp.abs(kpos - qpos)
    s = s - slope * arel.astype(F32)
    s = jnp.where(arel <= DA_RADIUS, s, NEG_INF)
    return s, qb, kw, qrows, win


def _da_slopes(group):
    d = DA_DILATIONS[group]
    sl = _alibi_slopes(DA_HEADS)[4 * group:4 * group + 4] * d
    return jnp.asarray(np.broadcast_to(sl[:, None, None], (4, 1, LANE)).copy())


def _da_fwd(proj, gq, gk, group):
    d = DA_DILATIONS[group]
    ld = SEQ // d
    nqb = ld // DA_QB

    def body(q_ref, k_ref, v_ref, gq_ref, gk_ref, sl_ref, o_ref, lse_ref, qn_scr, kn_scr, v_scr):
        slope = sl_ref[:, 0:1]

        def residue(r, carry):
            sel = _da_residue_rows(r, 0, ld, d)
            qn_scr[...] = _da_rmsnorm(q_ref[sel, :], gq_ref[...])[2].astype(BF16)
            kn_scr[...] = _da_rmsnorm(k_ref[sel, :], gk_ref[...])[2].astype(BF16)
            v_scr[...] = v_ref[sel, :].astype(BF16)

            def block(i):
                s, _, _, _, win = _da_scores(qn_scr, kn_scr, slope, i, ld)
                m = jnp.max(s, axis=-1, keepdims=True)
                p = jnp.exp(s - m)
                l = jnp.sum(p, axis=-1, keepdims=True)
                return _dot(p, v_scr[win, :], _NN) / l, jnp.broadcast_to(m + jnp.log(l), (DA_QB, HEAD_DIM))

            ways = min(DA_FWD_WAYS, nqb)

            def step(i, c2):
                blocks = [i + w * (nqb // ways) for w in range(ways)]
                for b, (o, lse) in zip(blocks, [block(b) for b in blocks]):
                    out = _da_residue_rows(r, b * DA_QB, DA_QB, d)
                    o_ref[out, :] = o
                    lse_ref[out, :] = lse
                return c2

            return lax.fori_loop(0, nqb // ways, step, carry)

        lax.fori_loop(0, d, residue, 0)

    seq_spec = lambda base: pl.BlockSpec((SEQ, HEAD_DIM), lambda h: (0, base + 4 * group + h))
    out_spec = pl.BlockSpec((SEQ, HEAD_DIM), lambda h: (0, h))
    gain_spec = pl.BlockSpec((1, HEAD_DIM), lambda h: (0, 0))
    blk = SEQ * HEAD_DIM * 4
    return pl.pallas_call(
        body, name=f"da_fwd_g{group}", grid=(DA_HEADS_PER_GROUP,),
        in_specs=[seq_spec(CB_DA_Q), seq_spec(CB_DA_K), seq_spec(CB_DA_V), gain_spec, gain_spec,
                  pl.BlockSpec((None, 1, LANE), lambda h: (h, 0, 0))],
        out_specs=[out_spec, out_spec],
        out_shape=[jax.ShapeDtypeStruct((SEQ, DA_WIDTH), F32)] * 2,
        scratch_shapes=[pltpu.VMEM((ld, HEAD_DIM), BF16)] * 3,
        compiler_params=_params(("parallel",), 10 * blk + 2 * blk),
    )(proj, proj, proj, gq, gk, _da_slopes(group))


def _da_merge(os, lses, tr):
    w = DA_WIDTH

    def body(o0, o1, o2, l0, l1, l2, ob_ref, of_ref, lse_ref):
        la, lb_, lc = l0[...], l1[...], l2[...]
        m = jnp.maximum(jnp.maximum(la, lb_), lc)
        ea, eb, ec = jnp.exp(la - m), jnp.exp(lb_ - m), jnp.exp(lc - m)
        tot = ea + eb + ec
        o = (ea * o0[...] + eb * o1[...] + ec * o2[...]) / tot
        of_ref[...] = o
        ob_ref[...] = o.astype(BF16)
        lse_ref[...] = m + jnp.log(tot)

    return pl.pallas_call(
        body, name="da_merge", grid=(SEQ // tr,), in_specs=[_row_spec(tr, w)] * 6,
        out_specs=[_row_spec(tr, w)] * 3,
        out_shape=[jax.ShapeDtypeStruct((SEQ, w), BF16), jax.ShapeDtypeStruct((SEQ, w), F32),
                   jax.ShapeDtypeStruct((SEQ, w), F32)],
        compiler_params=_params(("parallel",), 24 * tr * w * 4),
    )(*os, *lses)


def _da_rowdot(do, o, tr):
    w = DA_WIDTH

    def body(do_ref, o_ref, out_ref):
        prod = do_ref[...] * o_ref[...]
        for h in range(w // HEAD_DIM):
            sl = slice(h * HEAD_DIM, (h + 1) * HEAD_DIM)
            out_ref[:, sl] = jnp.broadcast_to(jnp.sum(prod[:, sl], axis=-1, keepdims=True), (tr, HEAD_DIM))

    return pl.pallas_call(
        body, name="da_rowdot", grid=(SEQ // tr,), in_specs=[_row_spec(tr, w)] * 2,
        out_specs=_row_spec(tr, w), out_shape=jax.ShapeDtypeStruct((SEQ, w), F32),
        compiler_params=_params(("parallel",), 10 * tr * w * 4),
    )(do, o)


def _da_bwd(proj, gq, gk, do, lse, dd, dproj, group):
    d = DA_DILATIONS[group]
    ld = SEQ // d
    nqb = ld // DA_QB

    def body(q_ref, k_ref, v_ref, gq_ref, gk_ref, sl_ref, do_ref, lse_ref, dd_ref, dproj_in,
             dproj_ref, dgq_ref, dgk_ref,
             qn_scr, kn_scr, v_scr, dqn_scr, dkn_scr, dvr_scr, dq_scr, dk_scr, dv_scr, dq_out, dk_out, dv_out, sems):
        del dproj_in
        head = pl.program_id(0)
        gqv, gkv = gq_ref[...], gk_ref[...]
        slope = sl_ref[:, 0:1]

        copies = []
        for k, (staging, base) in enumerate(((dq_out, CB_DA_Q), (dk_out, CB_DA_K), (dv_out, CB_DA_V))):
            cols = pl.ds(pl.multiple_of((base + 4 * group + head) * LANE, LANE), LANE)
            copies.append(pltpu.make_async_copy(staging, dproj_ref.at[:, cols], sems.at[k]))

        def residue(r, carry):
            sel = _da_residue_rows(r, 0, ld, d)
            qn_scr[...] = _da_rmsnorm(q_ref[sel, :], gqv)[2].astype(BF16)
            kn_scr[...] = _da_rmsnorm(k_ref[sel, :], gkv)[2].astype(BF16)
            v_scr[...] = v_ref[sel, :].astype(BF16)
            dkn_scr[...] = jnp.zeros_like(dkn_scr)
            dvr_scr[...] = jnp.zeros_like(dvr_scr)

            def block(i):
                s, qb, kw, qrows, win = _da_scores(qn_scr, kn_scr, slope, i, ld)
                src = _da_residue_rows(r, i * DA_QB, DA_QB, d)
                p = jnp.exp(s - lse_ref[src, :][:, 0:1])
                dob = do_ref[src, :]
                dp = _dot(dob, v_scr[win, :], _NT)
                ds = p * (dp - dd_ref[src, :][:, 0:1]) * DA_SCALE
                return qrows, win, _dot(p, dob, _TN), _dot(ds, kw, _NN), _dot(ds, qb, _TN)

            ways = min(DA_WAYS, nqb)

            def step(i, c2):
                blocks = [i + w * (nqb // ways) for w in range(ways)]
                for qrows, win, dv, dqn, dkn in [block(b) for b in blocks]:
                    dvr_scr[win, :] += dv
                    dqn_scr[qrows, :] = dqn
                    dkn_scr[win, :] += dkn
                return c2

            lax.fori_loop(0, nqb // ways, step, 0)

            parts = []
            for x_ref, gv, dn_scr, dx_scr in ((q_ref, gqv, dqn_scr, dq_scr), (k_ref, gkv, dkn_scr, dk_scr)):
                hat, rstd, _ = _da_rmsnorm(x_ref[sel, :], gv)
                dn = dn_scr[...]
                dyg = dn * gv
                dx_scr[sel, :] = rstd * (dyg - hat * jnp.mean(dyg * hat, axis=-1, keepdims=True))
                parts.append(jnp.sum(dn * hat, axis=0, keepdims=True))
            dv_scr[sel, :] = dvr_scr[...]
            return carry[0] + parts[0], carry[1] + parts[1]

        zero = jnp.zeros((1, HEAD_DIM), F32)
        pq, pk = lax.fori_loop(0, d, residue, (zero, zero))

        @pl.when(pl.program_id(0) == 0)
        def _():
            dgq_ref[...] = pq
            dgk_ref[...] = pk

        @pl.when(pl.program_id(0) > 0)
        def _():
            dgq_ref[...] += pq
            dgk_ref[...] += pk

        @pl.when(head > 0)
        def _():
            for cp in copies:
                cp.wait()

        dq_out[...] = dq_scr[...].astype(BF16)
        dk_out[...] = dk_scr[...].astype(BF16)
        dv_out[...] = dv_scr[...].astype(BF16)
        for cp in copies:
            cp.start()

        @pl.when(head == DA_HEADS_PER_GROUP - 1)
        def _():
            for cp in copies:
                cp.wait()

    seq_spec = lambda base: pl.BlockSpec((SEQ, HEAD_DIM), lambda h: (0, base + 4 * group + h))
    out_spec = pl.BlockSpec((SEQ, HEAD_DIM), lambda h: (0, h))
    gain_spec = pl.BlockSpec((1, HEAD_DIM), lambda h: (0, 0))
    gshape = jax.ShapeDtypeStruct((1, HEAD_DIM), F32)
    blk = SEQ * HEAD_DIM * 4
    return pl.pallas_call(
        body, name=f"da_bwd_g{group}", grid=(DA_HEADS_PER_GROUP,),
        in_specs=[seq_spec(CB_DA_Q), seq_spec(CB_DA_K), seq_spec(CB_DA_V), gain_spec, gain_spec,
                  pl.BlockSpec((None, 1, LANE), lambda h: (h, 0, 0)), out_spec, out_spec, out_spec, _any_spec()],
        out_specs=[_any_spec(), gain_spec, gain_spec],
        out_shape=[jax.ShapeDtypeStruct((SEQ, IN_COLS), BF16), gshape, gshape],
        scratch_shapes=[pltpu.VMEM((ld, HEAD_DIM), BF16)] * 3 + [pltpu.VMEM((ld, HEAD_DIM), F32)] * 3
        + [pltpu.VMEM((SEQ, HEAD_DIM), F32)] * 3 + [pltpu.VMEM((SEQ, HEAD_DIM), BF16)] * 3
        + [pltpu.SemaphoreType.DMA((3,))],
        input_output_aliases={9: 0},
        compiler_params=_params(("arbitrary",), 12 * blk + 3 * blk + 3 * blk + 3 * blk + 2 * blk),
    )(proj, proj, proj, gq, gk, _da_slopes(group), do, lse, dd, dproj)


def _mem_softmax(q, k, gq, gk):
    qhat, rq, qn = _da_rmsnorm(q, gq)
    khat, rk, kn = _da_rmsnorm(k, gk)
    s = _dot(qn, kn, _NT) * MEM_SCALE
    m = jnp.max(s, axis=-1, keepdims=True)
    e = jnp.exp(s - m)
    p = e / jnp.sum(e, axis=-1, keepdims=True)
    return p, (qhat, rq, qn), (khat, rk, kn)


def _mem_fwd(proj, kv, gq, gk, tq):
    def body(q_ref, k_ref, v_ref, gq_ref, gk_ref, o_ref):
        p, _, _ = _mem_softmax(q_ref[...], k_ref[...], gq_ref[...], gk_ref[...])
        o_ref[...] = _dot(p, v_ref[...], _NN).astype(BF16)

    gain_spec = pl.BlockSpec((1, HEAD_DIM), lambda h, i: (0, 0))
    return pl.pallas_call(
        body, name="mem_fwd", grid=(MEM_HEADS, SEQ // tq),
        in_specs=[pl.BlockSpec((tq, HEAD_DIM), lambda h, i: (i, CB_MEM_Q + h)),
                  pl.BlockSpec((N_MEM, HEAD_DIM), lambda h, i: (0, h)),
                  pl.BlockSpec((N_MEM, HEAD_DIM), lambda h, i: (0, MEM_HEADS + h)), gain_spec, gain_spec],
        out_specs=pl.BlockSpec((tq, HEAD_DIM), lambda h, i: (i, h)),
        out_shape=jax.ShapeDtypeStruct((SEQ, MEM_WIDTH), BF16),
        compiler_params=_params(("parallel", "parallel"), 16 * tq * N_MEM * 4),
    )(proj, kv, kv, gq, gk)


def _mem_bwd(proj, kv, gq, gk, do, dproj, tq):
    nq = SEQ // tq

    def body(q_ref, k_ref, v_ref, gq_ref, gk_ref, do_ref, dproj_in, dq_ref, dk_ref, dv_ref, dgq_ref, dgk_ref, dkn_scr):
        del dproj_in
        h, i = pl.program_id(0), pl.program_id(1)
        gqv, gkv = gq_ref[...], gk_ref[...]
        p, (qhat, rq, qn), (khat, rk, kn) = _mem_softmax(q_ref[...], k_ref[...], gqv, gkv)
        dob = do_ref[...]
        dvp = _dot(p, dob, _TN)
        dp = _dot(dob, v_ref[...], _NT)
        ds = p * (dp - jnp.sum(p * dp, axis=-1, keepdims=True)) * MEM_SCALE
        dqn = _dot(ds, kn, _NN)
        dknp = _dot(ds, qn, _TN)
        dyg = dqn * gqv
        dq_ref[...] = (rq * (dyg - qhat * jnp.mean(dyg * qhat, axis=-1, keepdims=True))).astype(BF16)
        dgq_part = jnp.sum(dqn * qhat, axis=0, keepdims=True)
        first = jnp.logical_and(h == 0, i == 0)

        @pl.when(first)
        def _():
            dgq_ref[...] = dgq_part

        @pl.when(jnp.logical_not(first))
        def _():
            dgq_ref[...] += dgq_part

        @pl.when(i == 0)
        def _():
            dv_ref[...] = dvp
            dkn_scr[...] = dknp

        @pl.when(i > 0)
        def _():
            dv_ref[...] += dvp
            dkn_scr[...] += dknp

        @pl.when(i == nq - 1)
        def _():
            dkn = dkn_scr[...]
            dkg = dkn * gkv
            dk_ref[...] = rk * (dkg - khat * jnp.mean(dkg * khat, axis=-1, keepdims=True))
            dgk_part = jnp.sum(dkn * khat, axis=0, keepdims=True)

            @pl.when(h == 0)
            def _():
                dgk_ref[...] = dgk_part

            @pl.when(h > 0)
            def _():
                dgk_ref[...] += dgk_part

    gain_spec = pl.BlockSpec((1, HEAD_DIM), lambda h, i: (0, 0))
    kvout = pl.BlockSpec((N_MEM, HEAD_DIM), lambda h, i: (0, h))
    return pl.pallas_call(
        body, name="mem_bwd", grid=(MEM_HEADS, nq),
        in_specs=[pl.BlockSpec((tq, HEAD_DIM), lambda h, i: (i, CB_MEM_Q + h)),
                  pl.BlockSpec((N_MEM, HEAD_DIM), lambda h, i: (0, h)),
                  pl.BlockSpec((N_MEM, HEAD_DIM), lambda h, i: (0, MEM_HEADS + h)), gain_spec, gain_spec,
                  pl.BlockSpec((tq, HEAD_DIM), lambda h, i: (i, h)), _any_spec()],
        out_specs=[pl.BlockSpec((tq, HEAD_DIM), lambda h, i: (i, CB_MEM_Q + h)), kvout, kvout, gain_spec, gain_spec],
        out_shape=[jax.ShapeDtypeStruct((SEQ, IN_COLS), BF16), jax.ShapeDtypeStruct((N_MEM, MEM_WIDTH), F32),
                   jax.ShapeDtypeStruct((N_MEM, MEM_WIDTH), F32), jax.ShapeDtypeStruct((1, HEAD_DIM), F32),
                   jax.ShapeDtypeStruct((1, HEAD_DIM), F32)],
        scratch_shapes=[pltpu.VMEM((N_MEM, HEAD_DIM), F32)], input_output_aliases={6: 0},
        compiler_params=_params(("arbitrary", "arbitrary"), 24 * tq * N_MEM * 4),
    )(proj, kv, kv, gq, gk, do, dproj)


def _mesh_position():
    return lax.axis_index("x"), lax.axis_index("y"), lax.axis_index("c")


def _any_spec():
    return pl.BlockSpec(memory_space=pl.ANY)


def _all_gather(shards):
    n = len(shards)

    def body(*refs):
        ins, outs = refs[:n], refs[n:2 * n]
        send_sems, recv_sems, local_sems = refs[2 * n:]
        x, y, c = _mesh_position()
        me, sibling = (x, y, c), (x, y, 1 - c)
        chips = [(1 - x, y), (x, 1 - y), (1 - x, 1 - y)]

        def copy(w, k, block, to, src=None):
            px, py, pc = block
            rows = outs[w].at[4 * px + 2 * py + pc]
            return pltpu.make_async_remote_copy(
                src_ref=rows if src is None else src, dst_ref=rows,
                send_sem=send_sems.at[7 * w + k], recv_sem=recv_sems.at[7 * w + k],
                device_id=to, device_id_type=MESH)

        started = []
        for w in range(n):
            mine = pltpu.make_async_copy(ins[w], outs[w].at[4 * x + 2 * y + c], local_sems.at[w])
            mine.start()
            started.append(mine)
        sends = []
        for w in range(n):
            first = [copy(w, 0, me, sibling, src=ins[w])]
            first += [copy(w, 1 + j, me, (*chip, c), src=ins[w]) for j, chip in enumerate(chips)]
            for cp in first:
                cp.start()
            sends += first
        for w in range(n):
            for j, chip in enumerate(chips):
                copy(w, 1 + j, (*chip, c), me).wait_recv()
                passed = copy(w, 4 + j, (*chip, c), sibling)
                passed.start()
                sends.append(passed)
        for w in range(n):
            copy(w, 0, sibling, me).wait_recv()
            for j, chip in enumerate(chips):
                copy(w, 4 + j, (*chip, 1 - c), me).wait_recv()
        for cp in sends:
            cp.wait_send()
        for mine in started:
            mine.wait()

    return pl.pallas_call(
        body, name="weights_all_gather",
        in_specs=[_any_spec()] * n, out_specs=[_any_spec()] * n,
        out_shape=[jax.ShapeDtypeStruct((N_DEV,) + s.shape, s.dtype) for s in shards],
        scratch_shapes=[pltpu.SemaphoreType.DMA((7 * n,)), pltpu.SemaphoreType.DMA((7 * n,)),
                        pltpu.SemaphoreType.DMA((n,))],
    )(*shards)


def _chip_of(j, x, y):
    return (1 - x if j & 1 else x, 1 - y if j & 2 else y)


_HBM_SPEC = pl.BlockSpec(memory_space=pltpu.HBM)
_SEM_SPEC = pl.BlockSpec(memory_space=pltpu.SEMAPHORE)
_DATAFLOW_EFFECT = pltpu.SideEffectType.DATAFLOW_SIDE_EFFECTING
TOKEN_SHAPE = (8, D_MODEL)


def _copies_start(name, arrays, n_copies, plan):
    n = len(arrays)

    def body(*refs):
        send_sems, recv_sems, token = refs[n], refs[n + 1], refs[2 * n + 2]
        copies = plan(refs[:n])
        assert len(copies) == n_copies
        for k, (src, dst, dev) in enumerate(copies):
            pltpu.make_async_remote_copy(src_ref=src, dst_ref=dst, send_sem=send_sems.at[k], recv_sem=recv_sems.at[k],
                                         device_id=dev, device_id_type=MESH).start()
        token[...] = jnp.zeros_like(token)

    outs = pl.pallas_call(
        body, name=name,
        out_shape=(pltpu.SemaphoreType.DMA((n_copies,)), pltpu.SemaphoreType.DMA((n_copies,)),
                   *[pltpu.HBM(a.shape, a.dtype) for a in arrays], jax.ShapeDtypeStruct(TOKEN_SHAPE, F32)),
        in_specs=[_HBM_SPEC] * n,
        out_specs=(_SEM_SPEC, _SEM_SPEC, *[_HBM_SPEC] * n, pl.BlockSpec(memory_space=pltpu.VMEM)),
        input_output_aliases={i: i + 2 for i in range(n)},
        compiler_params=pltpu.CompilerParams(has_side_effects=_DATAFLOW_EFFECT),
    )(*[pltpu.with_memory_space_constraint(a, pltpu.HBM) for a in arrays])
    return outs[0], outs[1], list(outs[2:2 + n]), outs[2 + n]


def _copies_wait(name, send_sems, recv_sems, arrays, n_copies, plan, after):
    n = len(arrays)
    after = list(after) if isinstance(after, (list, tuple)) else [after]

    def body(*refs):
        send_ref, recv_ref = refs[n], refs[n + 1]
        copies = plan(refs[:n])
        assert len(copies) == n_copies
        for k, (src, dst, dev) in enumerate(copies):
            cp = pltpu.make_async_remote_copy(src_ref=src, dst_ref=dst, send_sem=send_ref.at[k], recv_sem=recv_ref.at[k],
                                              device_id=dev, device_id_type=MESH)
            cp.wait_send()
            cp.wait_recv()

    outs = pl.pallas_call(
        body, name=name, out_shape=tuple(pltpu.HBM(a.shape, a.dtype) for a in arrays),
        in_specs=[_HBM_SPEC] * n + [_SEM_SPEC, _SEM_SPEC] + [pl.BlockSpec(memory_space=pl.ANY)] * len(after),
        out_specs=tuple([_HBM_SPEC] * n), input_output_aliases={i: i for i in range(n)},
        compiler_params=pltpu.CompilerParams(has_side_effects=_DATAFLOW_EFFECT),
    )(*arrays, send_sems, recv_sems, *after)
    return list(outs)


def _after(small, token):
    return small if token is None else small + token[0:1, :small.shape[-1]]


def _gather_plan_out(n):
    def plan(refs):
        x, y, c = _mesh_position()
        me = 4 * x + 2 * y + c
        copies = []
        for w in range(n):
            land = refs[n + w].at[me]
            copies.append((refs[w], land, (x, y, 1 - c)))
            for j in range(1, 4):
                copies.append((refs[w], land, (*_chip_of(j, x, y), c)))
        return copies
    return plan


def _gather_plan_pass(n):
    def plan(refs):
        x, y, c = _mesh_position()
        copies = []
        for w in range(n):
            for j in range(1, 4):
                px, py = _chip_of(j, x, y)
                rows = refs[w].at[4 * px + 2 * py + c]
                copies.append((rows, rows, (x, y, 1 - c)))
        return copies
    return plan


def _reduce_plan_sibling(n):
    def plan(refs):
        x, y, c = _mesh_position()
        copies = []
        for w in range(n):
            for j in range(4):
                px, py = _chip_of(j, x, y)
                copies.append((refs[w].at[4 * px + 2 * py + (1 - c)], refs[n + w].at[j], (x, y, 1 - c)))
        return copies
    return plan


def _reduce_plan_chips(n):
    def plan(refs):
        x, y, c = _mesh_position()
        copies = []
        for w in range(n):
            for j in range(1, 4):
                copies.append((refs[w].at[j - 1], refs[n + w].at[j - 1], (*_chip_of(j, x, y), c)))
        return copies
    return plan


def _chip_partials(grad, recv, name, tr):
    _, rows, width = grad.shape

    def body(g_ref, r_ref, own_ref, other_ref):
        x, y, c = _mesh_position()
        for j in range(4):
            px, py = _chip_of(j, x, y)
            total = g_ref[4 * px + 2 * py + c].astype(F32) + r_ref[j].astype(F32)
            if j == 0:
                own_ref[...] = total
            else:
                other_ref[j - 1] = total.astype(BF16)

    return pl.pallas_call(
        body, name=name, grid=(rows // tr,),
        in_specs=[pl.BlockSpec((N_DEV, tr, width), lambda i: (0, i, 0)),
                  pl.BlockSpec((4, tr, width), lambda i: (0, i, 0))],
        out_specs=[pl.BlockSpec((tr, width), lambda i: (i, 0)), pl.BlockSpec((3, tr, width), lambda i: (0, i, 0))],
        out_shape=[jax.ShapeDtypeStruct((rows, width), F32), jax.ShapeDtypeStruct((3, rows, width), BF16)],
        compiler_params=_params(("parallel",), 2 * 20 * tr * width * 2 + 8 * tr * width * 4),
    )(grad, recv)


def _adamw_math(w, g, m, v):
    m = ADAM_B1 * m + (1.0 - ADAM_B1) * g
    v = ADAM_B2 * v + (1.0 - ADAM_B2) * (g * g)
    m_hat = m / (1.0 - ADAM_B1 ** ADAM_STEP)
    v_hat = v / (1.0 - ADAM_B2 ** ADAM_STEP)
    delta = -ADAM_LR * (m_hat / (jnp.sqrt(v_hat) + ADAM_EPS) + ADAM_WD * w)
    return delta, m, v


def _adamw_shard(own, recv, w, m, v, name, tr):
    rows, width = own.shape

    def body(own_ref, r_ref, w_ref, m_ref, v_ref, g_ref, d_ref, nm_ref, nv_ref):
        g = own_ref[...]
        for j in range(3):
            g = g + r_ref[j].astype(F32)
        g_ref[...] = g
        d_ref[...], nm_ref[...], nv_ref[...] = _adamw_math(w_ref[...], g, m_ref[...], v_ref[...])

    spec = pl.BlockSpec((tr, width), lambda i: (i, 0))
    shape = jax.ShapeDtypeStruct((rows, width), F32)
    return pl.pallas_call(
        body, name=name, grid=(rows // tr,),
        in_specs=[spec, pl.BlockSpec((3, tr, width), lambda i: (0, i, 0)), spec, spec, spec],
        out_specs=[spec] * 4, out_shape=[shape] * 4,
        compiler_params=_params(("parallel",), 22 * tr * width * 4),
    )(own, recv, w, m, v)


def _small_all_reduce_adamw(gpart, lbpack, wpack, mpack, vpack, after):
    def body(gp_ref, lb_ref, w_ref, m_ref, v_ref, after_ref, g_ref, d_ref, nm_ref, nv_ref, gath_ref, send_sems, recv_sems):
        del after_ref
        x, y, c = _mesh_position()
        me = 4 * x + 2 * y + c
        gath_ref[me] = gp_ref[...]
        copies = []
        for j in range(1, N_DEV):
            peer = (x ^ (j >> 2), y ^ ((j >> 1) & 1), c ^ (j & 1))
            cp = pltpu.make_async_remote_copy(
                src_ref=gp_ref, dst_ref=gath_ref.at[me], send_sem=send_sems.at[j - 1], recv_sem=recv_sems.at[j - 1],
                device_id=peer, device_id_type=MESH)
            cp.start()
            copies.append(cp)
        for cp in copies:
            cp.wait()
        tot = gath_ref[0]
        for s in range(1, N_DEV):
            tot = tot + gath_ref[s]
        lb = lb_ref[...]
        dl = tot[16:32, :] * lb * (1.0 - lb)
        g = jnp.concatenate([tot[0:16, :], dl[0:8, :], -dl[0:8, :], dl[8:16, :], -dl[8:16, :],
                             tot[32:SMALL_GRAD_ROWS, :]], axis=0)
        g_ref[...] = g
        d_ref[...], nm_ref[...], nv_ref[...] = _adamw_math(w_ref[...], g, m_ref[...], v_ref[...])

    vm = pl.BlockSpec(memory_space=pltpu.VMEM)
    shape = jax.ShapeDtypeStruct((SMALL_ROWS, LANE), F32)
    return pl.pallas_call(
        body, name="small_all_reduce_adamw", in_specs=[vm] * 5 + [_any_spec()], out_specs=[vm] * 4,
        out_shape=[shape] * 4,
        scratch_shapes=[pltpu.VMEM((N_DEV, SMALL_GRAD_ROWS, LANE), F32),
                        pltpu.SemaphoreType.DMA((N_DEV - 1,)), pltpu.SemaphoreType.DMA((N_DEV - 1,))],
    )(gpart, lbpack, wpack, mpack, vpack, after)


_SMALL_NAMES = ("norm_mix_gain", "norm_mem_gain", "lb_logits_fw", "lb_logits_bw", "norm_ffn_gain",
                "hg_norm_gain", "da_q_gain", "da_k_gain", "mem_q_gain", "mem_k_gain")
_SMALL_ROW0 = {"norm_mix_gain": 0, "norm_mem_gain": 8, "lb_logits_fw": 16, "lb_logits_bw": 32, "norm_ffn_gain": 48,
               "hg_norm_gain": 56, "da_q_gain": 64, "da_k_gain": 72, "mem_q_gain": 80, "mem_k_gain": 88}
_LOSS_ROW = 96


def _pack_rows(parts, total_rows):
    rows = []
    for p in parts:
        r = p.reshape(-1, LANE)
        rows.append(jnp.pad(r, ((0, -r.shape[0] % 8), (0, 0))))
    used = sum(r.shape[0] for r in rows)
    if total_rows > used:
        rows.append(jnp.zeros((total_rows - used, LANE), F32))
    return jnp.concatenate(rows, axis=0)


def _unpack_small(pack, like):
    out = {}
    for name in _SMALL_NAMES:
        n = like[name].size // LANE
        r0 = _SMALL_ROW0[name]
        out[name] = pack[r0:r0 + n].reshape(like[name].shape)
    return out


def kernel(x, mem, norm_mix_gain, norm_mem_gain, w_in, lb_logits_fw, lb_logits_bw, hg_norm_gain, da_q_gain, da_k_gain, w_mem_kv, mem_q_gain, mem_k_gain, w_proj_hg, w_proj_da, w_proj_mem, w_out, norm_ffn_gain, w_ffn_in, w_ffn_out, loss_target, m_norm_mix_gain, m_norm_mem_gain, m_w_in, m_lb_logits_fw, m_lb_logits_bw, m_hg_norm_gain, m_da_q_gain, m_da_k_gain, m_w_mem_kv, m_mem_q_gain, m_mem_k_gain, m_w_proj_hg, m_w_proj_da, m_w_proj_mem, m_w_out, m_norm_ffn_gain, m_w_ffn_in, m_w_ffn_out, v_norm_mix_gain, v_norm_mem_gain, v_w_in, v_lb_logits_fw, v_lb_logits_bw, v_hg_norm_gain, v_da_q_gain, v_da_k_gain, v_w_mem_kv, v_mem_q_gain, v_mem_k_gain, v_w_proj_hg, v_w_proj_da, v_w_proj_mem, v_w_out, v_norm_ffn_gain, v_w_ffn_in, v_w_ffn_out):
    small_w = dict(norm_mix_gain=norm_mix_gain, norm_mem_gain=norm_mem_gain, lb_logits_fw=lb_logits_fw,
                   lb_logits_bw=lb_logits_bw, norm_ffn_gain=norm_ffn_gain, hg_norm_gain=hg_norm_gain,
                   da_q_gain=da_q_gain, da_k_gain=da_k_gain, mem_q_gain=mem_q_gain, mem_k_gain=mem_k_gain)
    small_m = dict(norm_mix_gain=m_norm_mix_gain, norm_mem_gain=m_norm_mem_gain, lb_logits_fw=m_lb_logits_fw,
                   lb_logits_bw=m_lb_logits_bw, norm_ffn_gain=m_norm_ffn_gain, hg_norm_gain=m_hg_norm_gain,
                   da_q_gain=m_da_q_gain, da_k_gain=m_da_k_gain, mem_q_gain=m_mem_q_gain, mem_k_gain=m_mem_k_gain)
    small_v = dict(norm_mix_gain=v_norm_mix_gain, norm_mem_gain=v_norm_mem_gain, lb_logits_fw=v_lb_logits_fw,
                   lb_logits_bw=v_lb_logits_bw, norm_ffn_gain=v_norm_ffn_gain, hg_norm_gain=v_hg_norm_gain,
                   da_q_gain=v_da_q_gain, da_k_gain=v_da_k_gain, mem_q_gain=v_mem_q_gain, mem_k_gain=v_mem_k_gain)
    big_names = ("w_in", "w_mem_kv", "w_proj_hg", "w_proj_da", "w_proj_mem", "w_out", "w_ffn_in", "w_ffn_out")
    big_w = dict(w_in=w_in[0], w_mem_kv=w_mem_kv[0], w_proj_hg=w_proj_hg[0], w_proj_da=w_proj_da[0],
                 w_proj_mem=w_proj_mem[0], w_out=w_out[0], w_ffn_in=w_ffn_in[0], w_ffn_out=w_ffn_out[0])
    big_m = dict(w_in=m_w_in[0], w_mem_kv=m_w_mem_kv[0], w_proj_hg=m_w_proj_hg[0], w_proj_da=m_w_proj_da[0],
                 w_proj_mem=m_w_proj_mem[0], w_out=m_w_out[0], w_ffn_in=m_w_ffn_in[0], w_ffn_out=m_w_ffn_out[0])
    big_v = dict(w_in=v_w_in[0], w_mem_kv=v_w_mem_kv[0], w_proj_hg=v_w_proj_hg[0], w_proj_da=v_w_proj_da[0],
                 w_proj_mem=v_w_proj_mem[0], w_out=v_w_out[0], w_ffn_in=v_w_ffn_in[0], w_ffn_out=v_w_ffn_out[0])

    row_tile = dict(w_in=128, w_mem_kv=128, w_proj_hg=128, w_proj_da=512, w_proj_mem=512, w_out=128,
                    w_ffn_in=128, w_ffn_out=352)
    rest = _BIG_NAMES[1:]
    shards = [big_w[n].astype(BF16) for n in rest]
    state = {}
    big_out = {}

    def reduce_start(group, stacked):
        names = tuple(stacked)
        arrays = [stacked[n] for n in names] + [lax.empty((4,) + stacked[n].shape[1:], BF16) for n in names]
        plan = _reduce_plan_sibling(len(names))
        send, recv, thru, token = _copies_start(f"grads_{group}_sibling_start", arrays, 4 * len(names), plan)
        state[group] = dict(names=names, plan=plan, send=send, recv=recv, arrays=thru)
        return token

    def reduce_middle(group, after):
        st = state[group]
        names, k = st["names"], len(st["names"])
        thru = _copies_wait(f"grads_{group}_sibling_wait", st["send"], st["recv"], st["arrays"], 4 * k, st["plan"], after)
        partials = [_chip_partials(thru[i], thru[k + i], f"chip_partials_{n}", row_tile[n]) for i, n in enumerate(names)]
        arrays = [p[1] for p in partials] + [lax.empty(p[1].shape, BF16) for p in partials]
        plan = _reduce_plan_chips(k)
        send, recv, thru2, token = _copies_start(f"grads_{group}_chips_start", arrays, 3 * k, plan)
        state[group] = dict(names=names, plan=plan, send=send, recv=recv, arrays=thru2, own=[p[0] for p in partials])
        return token

    def reduce_finish(group, after):
        st = state.pop(group)
        names, k = st["names"], len(st["names"])
        thru = _copies_wait(f"grads_{group}_chips_wait", st["send"], st["recv"], st["arrays"], 3 * k, st["plan"], after)
        for i, n in enumerate(names):
            big_out[n] = _adamw_shard(st["own"][i], thru[k + i], big_w[n], big_m[n], big_v[n], "adamw_" + n, row_tile[n])

    win_st = _all_gather([big_w["w_in"].astype(BF16)])[0]
    step = _local_step_stages(x[0], mem[0], loss_target[0], small_w, win_st)
    event, payload = next(step)
    local = None
    while True:
        reply = None
        if event == "begin":
            nr = len(rest)
            me = 4 * lax.axis_index("x") + 2 * lax.axis_index("y") + lax.axis_index("c")
            arrays = shards + [lax.dynamic_update_slice(lax.empty((N_DEV,) + s.shape, BF16), s[None], (me, 0, 0))
                               for s in shards]
            plan = _gather_plan_out(nr)
            send, recv, thru, reply = _copies_start("weights_rest_out_start", arrays, 4 * nr, plan)
            state["gather"] = dict(plan=plan, send=send, recv=recv, arrays=thru)
        elif event == "after_gla_fwd":
            st = state["gather"]
            nr = len(rest)
            thru = _copies_wait("weights_rest_out_wait", st["send"], st["recv"], st["arrays"], 4 * nr, st["plan"], payload)
            plan = _gather_plan_pass(nr)
            send, recv, lands, reply = _copies_start("weights_rest_pass_start", thru[nr:], 3 * nr, plan)
            state["gather"] = dict(plan=plan, send=send, recv=recv, arrays=lands)
        elif event == "need_weights":
            st = state.pop("gather")
            nr = len(rest)
            lands = _copies_wait("weights_rest_pass_wait", st["send"], st["recv"], st["arrays"], 3 * nr, st["plan"], payload)
            reply = dict(zip(rest, lands))
        elif event == "grads_ffn":
            reply = reduce_start("ffn", payload)
        elif event == "after_gate_merge_bwd":
            reply = reduce_middle("ffn", payload)
        elif event == "grads_mix":
            reply = reduce_start("mix", payload)
        elif event == "after_da_bwd_g0":
            reply = reduce_middle("mix", payload)
            reduce_finish("ffn", payload)
        elif event == "after_gla_bwd":
            reduce_finish("mix", payload)
        elif event == "grads_in":
            reply = reduce_start("in", payload)
        elif event == "after_proj_bwd_act_top":
            reply = reduce_middle("in", payload)
        elif event == "end":
            local = payload
            reduce_finish("in", [local["grad_x"]] + [big_out[n][0] for n in rest])
            break
        event, payload = step.send(reply)
    grad_x = local["grad_x"]

    wpack = _pack_rows([small_w[n] for n in _SMALL_NAMES], SMALL_ROWS)
    mpack = _pack_rows([small_m[n] for n in _SMALL_NAMES], SMALL_ROWS)
    vpack = _pack_rows([small_v[n] for n in _SMALL_NAMES], SMALL_ROWS)
    gs, ds, ms, vs = _small_all_reduce_adamw(local["gpart"], local["lbpack"], wpack, mpack, vpack, big_out["w_in"][0])
    loss = gs[_LOSS_ROW, 0]
    small_out = [_unpack_small(t, small_w) for t in (gs, ds, ms, vs)]

    order = ("norm_mix_gain", "norm_mem_gain", "w_in", "lb_logits_fw", "lb_logits_bw", "hg_norm_gain", "da_q_gain",
             "da_k_gain", "w_mem_kv", "mem_q_gain", "mem_k_gain", "w_proj_hg", "w_proj_da", "w_proj_mem", "w_out",
             "norm_ffn_gain", "w_ffn_in", "w_ffn_out")
    outs = [loss, grad_x[None]]
    for kind in range(4):
        for n in order:
            outs.append(big_out[n][kind][None] if n in big_out else small_out[kind][n])
    return tuple(outs)


_BIG_NAMES = ("w_in", "w_mem_kv", "w_proj_hg", "w_proj_da", "w_proj_mem", "w_out", "w_ffn_in", "w_ffn_out")


def _local_step(xs, mems, target, sw, wg):
    step = _local_step_stages(xs, mems, target, sw, wg["w_in"])
    stacked = {}
    event, payload = next(step)
    while event != "end":
        if event.startswith("grads_"):
            stacked.update(payload)
        event, payload = step.send(wg if event == "need_weights" else None)
    return dict(payload, stacked=stacked)


def _local_step_stages(xs, mems, target, sw, win_st):
    norm_mix_gain, norm_mem_gain, norm_ffn_gain = sw["norm_mix_gain"], sw["norm_mem_gain"], sw["norm_ffn_gain"]
    lb_logits_fw, lb_logits_bw, hg_norm_gain = sw["lb_logits_fw"], sw["lb_logits_bw"], sw["hg_norm_gain"]
    da_q_gain, da_k_gain, mem_q_gain, mem_k_gain = sw["da_q_gain"], sw["da_k_gain"], sw["mem_q_gain"], sw["mem_k_gain"]

    token = yield "begin", None
    lb_fw = _lb_table(lb_logits_fw, "lb_table_fw")
    lb_bw = _lb_table(lb_logits_bw, "lb_table_bw")
    lb = jnp.concatenate([lb_fw, lb_bw], axis=0).reshape(2, HG_HEADS, 1, HEAD_DIM)
    h, h_t = _rmsnorm_fwd(xs, _after(norm_mix_gain, token), "norm_mix_fwd", 512, transposed=True)
    proj = _matmul(h, win_st, "nn", F32, 1024, IN_SHARD, D_MODEL, "proj_fwd", b_stacked=True, n_outer=True)
    o_hg, o_pre, states = _gla_fwd(proj, lb, hg_norm_gain)
    token = yield "after_gla_fwd", o_hg
    da = [_da_fwd(proj, _after(da_q_gain, token), da_k_gain, g) for g in range(3)]
    o_da, o_da32, lse_da = _da_merge([t[0] for t in da], [t[1] for t in da], 512)
    wg = yield "need_weights", o_da
    wkv = wg["w_mem_kv"].reshape(D_MODEL, 2 * MEM_WIDTH)
    wphg = wg["w_proj_hg"].reshape(D_MODEL, D_MODEL)
    wpda = jnp.transpose(wg["w_proj_da"], (1, 0, 2)).reshape(DA_WIDTH, D_MODEL)
    wpmem = jnp.transpose(wg["w_proj_mem"], (1, 0, 2)).reshape(MEM_WIDTH, D_MODEL)
    wout = wg["w_out"].reshape(D_MODEL, D_MODEL)
    wfin = jnp.transpose(wg["w_ffn_in"], (1, 0, 2)).reshape(D_MODEL, 2 * D_FF)
    wfout = wg["w_ffn_out"].reshape(D_FF, D_MODEL)
    mem_n = _rmsnorm_fwd(mems, norm_mem_gain, "norm_mem_fwd", N_MEM)
    kv = _matmul(mem_n, wkv, "nn", F32, N_MEM, 1024, D_MODEL, "mem_kv_fwd")
    o_mem = _mem_fwd(proj, kv, mem_q_gain, mem_k_gain, 512)
    t_hg = _matmul(o_hg, wphg, "nn", F32, 512, 1024, D_MODEL, "proj_hg_fwd")
    t_da = _matmul(o_da, wpda, "nn", F32, 512, 1024, DA_WIDTH, "proj_da_fwd")
    t_mem = _matmul(o_mem, wpmem, "nn", F32, 512, 1024, MEM_WIDTH, "proj_mem_fwd")
    merged = _gate_merge_fwd(proj, t_hg, t_da, t_mem, 256)
    u = _matmul(merged, wout, "nn", F32, 512, 1024, D_MODEL, "out_fwd")
    x1, h2, h2_t = _residual_rmsnorm_fwd(xs, u, norm_ffn_gain, "norm_ffn_fwd", 512)
    ffn_a, ffn_b, act, act_t = _ffn_in_swiglu(h2, wfin, 1024, 1408)
    z = _matmul(act, wfout, "nn", F32, 512, 1024, D_FF, "ffn_out_fwd")
    dy, dyb, loss_part = _loss_head(x1, z, target, 512)

    dact = _matmul(dyb, wfout, "nt", F32, 512, 1408, D_MODEL, "ffn_out_bwd_act")
    g_wfout = _matmul(act_t, dyb, "nn", BF16, 1408, 1024, 2048, "ffn_out_bwd_w")
    dab = _swiglu_bwd(ffn_a, ffn_b, dact, 256)
    dh2 = _matmul(dab, wfin, "nt", F32, 1024, 1024, 2816, "ffn_in_bwd_act")
    g_wfin = _matmul(h2_t, dab, "nn", BF16, 1024, 1408, 2048, "ffn_in_bwd_w")
    token = yield "grads_ffn", dict(
        w_ffn_in=jnp.transpose(g_wfin.reshape(D_MODEL, N_DEV, 2 * D_FF // N_DEV), (1, 0, 2)),
        w_ffn_out=g_wfout.reshape(N_DEV, D_FF // N_DEV, D_MODEL))
    dx1, dx1b, g_norm_ffn = _rmsnorm_bwd(x1, dh2, dy, _after(norm_ffn_gain, token), "norm_ffn_bwd", 256)
    dmerged = _matmul(dx1b, wout, "nt", F32, 512, 1024, D_MODEL, "out_bwd_act")
    g_wout = _matmul(merged, dx1b, "tn", BF16, 1024, 1024, 2048, "out_bwd_w")
    dproj = _dproj_buffer()
    dt_hg, dproj = _gate_bwd(proj, t_hg, dmerged, dproj, 0, 256)
    dt_da, dproj = _gate_bwd(proj, t_da, dmerged, dproj, 1, 256)
    dt_mem, dproj = _gate_bwd(proj, t_mem, dmerged, dproj, 2, 256)
    token = yield "after_gate_merge_bwd", dt_hg
    do_hg = _matmul(dt_hg, wphg, "nt", F32, 512, 1024, D_MODEL, "proj_hg_bwd_act", after=token)
    g_wphg = _matmul(o_hg, dt_hg, "tn", BF16, 1024, 1024, 2048, "proj_hg_bwd_w")
    do_da = _matmul(dt_da, wpda, "nt", F32, 512, DA_WIDTH, D_MODEL, "proj_da_bwd_act")
    g_wpda = _matmul(o_da, dt_da, "tn", BF16, DA_WIDTH, D_MODEL, 2048, "proj_da_bwd_w")
    do_mem = _matmul(dt_mem, wpmem, "nt", F32, 512, MEM_WIDTH, D_MODEL, "proj_mem_bwd_act")
    g_wpmem = _matmul(o_mem, dt_mem, "tn", BF16, MEM_WIDTH, D_MODEL, 2048, "proj_mem_bwd_w")
    by_owner = lambda g: jnp.transpose(g.reshape(g.shape[0], N_DEV, D_MODEL // N_DEV), (1, 0, 2))
    g_wpda, g_wpmem = by_owner(g_wpda), by_owner(g_wpmem)

    dproj, dk_mem, dv_mem, g_mem_q, g_mem_k = _mem_bwd(proj, kv, mem_q_gain, mem_k_gain, do_mem, dproj, 512)
    dkv = jnp.concatenate([dk_mem, dv_mem], axis=1).astype(BF16)
    g_wkv = _matmul(mem_n, dkv, "tn", BF16, 1024, 1024, N_MEM, "mem_kv_bwd_w")
    dmem_n = _matmul(dkv, wkv, "nt", F32, N_MEM, 1024, 1024, "mem_kv_bwd_act")
    g_norm_mem = _gain_grad(mems, dmem_n, "norm_mem_bwd")
    token = yield "grads_mix", dict(
        w_mem_kv=g_wkv.reshape(N_DEV, D_MODEL // N_DEV, 2 * MEM_WIDTH),
        w_proj_hg=g_wphg.reshape(N_DEV, D_MODEL // N_DEV, D_MODEL),
        w_proj_da=g_wpda, w_proj_mem=g_wpmem,
        w_out=g_wout.reshape(N_DEV, D_MODEL // N_DEV, D_MODEL))

    dd_da = _da_rowdot(do_da, o_da32, 512)
    dproj, g_da_q, g_da_k = _da_bwd(proj, _after(da_q_gain, token), da_k_gain, do_da, lse_da, dd_da, dproj, 0)
    token = yield "after_da_bwd_g0", g_da_q
    for g in (1, 2):
        dproj, gq_part, gk_part = _da_bwd(proj, _after(da_q_gain, token), da_k_gain, do_da, lse_da, dd_da, dproj, g)
        g_da_q, g_da_k = g_da_q + gq_part, g_da_k + gk_part

    dproj, dlb, g_hg_norm = _gla_bwd(proj, lb, hg_norm_gain, o_pre, states, do_hg, dproj)
    yield "after_gla_bwd", dlb

    g_win =_matmul(h_t, dproj, "nn", BF16, 1024, IN_SHARD, 2048, "proj_bwd_w", out_stacked=True)
    token = yield "grads_in", dict(w_in=g_win)
    dh = _matmul(dproj, win_st, "nt", F32, 1024, 1024, IN_SHARD, "proj_bwd_act_top", b_stacked=True, after=token,
                 m_blocks=(0, 2))
    token = yield "after_proj_bwd_act_top", dh
    dh = _matmul(dproj, win_st, "nt", F32, 1024, 1024, IN_SHARD, "proj_bwd_act_bottom", b_stacked=True, after=token,
                 m_blocks=(2, 2), out_into=dh)
    grad_x, _, g_norm_mix = _rmsnorm_bwd(xs, dh, dx1, _after(norm_mix_gain, token), "norm_mix_bwd", 256)

    gpart = _pack_rows([g_norm_mix, g_norm_mem, dlb[0], dlb[1], g_norm_ffn, g_hg_norm, g_da_q, g_da_k,
                        g_mem_q, g_mem_k, loss_part], SMALL_GRAD_ROWS)
    lbpack = jnp.concatenate([lb_fw.reshape(8, LANE), lb_bw.reshape(8, LANE)], axis=0)
    yield "end", dict(grad_x=grad_x, gpart=gpart, lbpack=lbpack)
```

```python
import numpy as np
import jax
import jax.numpy as jnp
from jax import lax
from jax.experimental import pallas as pl
from jax.experimental.pallas import tpu as pltpu

F32 = jnp.float32
BF16 = jnp.bfloat16
MESH = pl.DeviceIdType.MESH

SEQ = 4096
D_MODEL = 1024
N_DEV = 8
N_MEM = 256
RMS_EPS = 1e-6
NEG_INF = -1e30
LANE = 128
HEAD_DIM = 128
HG_HEADS = 8
HG_CHUNK = 64
HG_SCALE = HEAD_DIM ** -0.5
DA_DILATIONS = (1, 4, 16)
DA_RADIUS = 64
DA_HEADS_PER_GROUP = 4
DA_HEADS = 12
DA_WIDTH = 512
DA_SCALE = HEAD_DIM ** -0.5
DA_QB = 128
DA_WIN = 256
DA_WAYS = 4
DA_FWD_WAYS = 4
MEM_HEADS = 4
MEM_WIDTH = 512
MEM_SCALE = HEAD_DIM ** -0.5
D_FF = 2816
IN_COLS = 13312
IN_SHARD = IN_COLS // N_DEV
CB_HG_Q, CB_F, CB_HG_I, CB_HG_G = 0, 8, 24, 32
CB_DA_Q, CB_DA_K, CB_DA_V, CB_MEM_Q = 40, 52, 64, 76
N_CB = IN_COLS // LANE
ADAM_LR, ADAM_B1, ADAM_B2, ADAM_EPS, ADAM_WD, ADAM_STEP = 0.001, 0.9, 0.999, 1e-08, 0.01, 10
VMEM_BYTES_V7X = 64 * 1024 * 1024
SMALL_ROWS = 104
SMALL_GRAD_ROWS = 88

_NN = (((1,), (0,)), ((), ()))
_NT = (((1,), (1,)), ((), ()))
_TN = (((0,), (0,)), ((), ()))


def _dot(a, b, dims):
    return lax.dot_general(a.astype(BF16), b.astype(BF16), dims, preferred_element_type=F32)


def _dot_exact(a, b, dims):
    return lax.dot_general(a, b, dims, precision=lax.Precision.HIGHEST, preferred_element_type=F32)


def _sigmoid(x):
    return 0.5 * jnp.tanh(0.5 * x) + 0.5


def _params(semantics, est_bytes):
    limit = int(min(VMEM_BYTES_V7X - (6 << 20), max(56 << 20, est_bytes * 3 // 2)))
    return pltpu.CompilerParams(dimension_semantics=semantics, vmem_limit_bytes=limit)


def _nbytes(shape, dtype):
    return int(np.prod(shape)) * jnp.dtype(dtype).itemsize


def _alibi_slopes(n):
    return (2.0 ** (-8.0 * np.arange(1, n + 1) / n)).astype(np.float32)


def _matmul(a, b, mode, out_dtype, tm, tn, tk, name, b_stacked=False, out_stacked=False, n_outer=False, after=None,
            m_blocks=None, out_into=None):
    if mode == "tn":
        kdim, m = a.shape
    else:
        m, kdim = a.shape
    if b_stacked:
        if mode == "nn":
            n = b.shape[0] * b.shape[2]
            assert tn == b.shape[2] and tk == kdim == b.shape[1]
        else:
            assert mode == "nt" and tk == b.shape[2] and b.shape[0] * tk == kdim
            n = b.shape[1]
    else:
        n = b.shape[0] if mode == "nt" else b.shape[1]
    assert m % tm == 0 and n % tn == 0 and kdim % tk == 0
    gm, gn, gk = m // tm, n // tn, kdim // tk
    i0 = 0
    if m_blocks is not None:
        assert mode != "tn" and not out_stacked
        i0, gm = m_blocks

    def ijk(f):
        if n_outer:
            return lambda j, i, k: f(i + i0, j, k)
        return lambda i, j, k: f(i + i0, j, k)

    if mode == "tn":
        a_spec = pl.BlockSpec((tk, tm), ijk(lambda i, j, k: (k, i)))
    else:
        a_spec = pl.BlockSpec((tm, tk), ijk(lambda i, j, k: (i, k)))
    if b_stacked and mode == "nn":
        b_spec = pl.BlockSpec((None, tk, tn), ijk(lambda i, j, k: (j, 0, 0)))
    elif b_stacked:
        b_spec = pl.BlockSpec((None, tn, tk), ijk(lambda i, j, k: (k, j, 0)))
    elif mode == "nt":
        b_spec = pl.BlockSpec((tn, tk), ijk(lambda i, j, k: (j, k)))
    else:
        b_spec = pl.BlockSpec((tk, tn), ijk(lambda i, j, k: (k, j)))
    if out_stacked:
        assert tm == m
        out_shape = jax.ShapeDtypeStruct((gn, m, tn), out_dtype)
        o_spec = pl.BlockSpec((None, tm, tn), ijk(lambda i, j, k: (j, i, 0)))
    else:
        out_shape = jax.ShapeDtypeStruct((m, n), out_dtype)
        o_spec = pl.BlockSpec((tm, tn), ijk(lambda i, j, k: (i, j)))
    dims = {"nn": _NN, "nt": _NT, "tn": _TN}[mode]

    n_in = 2 + (after is not None) + (out_into is not None)

    def body(*refs):
        a_ref, b_ref, o_ref = refs[0], refs[1], refs[n_in]
        part = _dot(a_ref[...], b_ref[...], dims)
        if gk == 1:
            o_ref[...] = part.astype(out_dtype)
            return
        acc_ref = refs[-1]
        k = pl.program_id(2)

        @pl.when(k == 0)
        def _():
            acc_ref[...] = part

        @pl.when(jnp.logical_and(k > 0, k < gk - 1))
        def _():
            acc_ref[...] += part

        @pl.when(k == gk - 1)
        def _():
            o_ref[...] = (acc_ref[...] + part).astype(out_dtype)

    a_tile = _nbytes((tm, tk), a.dtype)
    b_tile = _nbytes((tk, tn), b.dtype)
    o_tile = _nbytes((tm, tn), out_dtype)
    est = 2 * (a_tile + b_tile + o_tile) + 3 * tm * tn * 4 + (a_tile + b_tile)
    grid = (gn, gm, gk) if n_outer else (gm, gn, gk)
    operands, in_specs = [a, b], [a_spec, b_spec]
    if after is not None:
        operands.append(after)
        in_specs.append(pl.BlockSpec(memory_space=pl.ANY))
    aliases = {}
    if out_into is not None:
        aliases = {len(operands): 0}
        operands.append(out_into)
        in_specs.append(pl.BlockSpec(memory_space=pl.ANY))
    return pl.pallas_call(
        body, name=name, grid=grid, in_specs=in_specs, out_specs=o_spec, out_shape=out_shape,
        scratch_shapes=[] if gk == 1 else [pltpu.VMEM((tm, tn), F32)], input_output_aliases=aliases,
        compiler_params=_params(("parallel", "parallel", "arbitrary"), est),
    )(*operands)


def _row_spec(tr, width, col_block=0):
    return pl.BlockSpec((tr, width), lambda i: (i, col_block))


def _bcast_spec(width):
    return pl.BlockSpec((1, width), lambda i: (0, 0))


def _col_spec(width, tr):
    return pl.BlockSpec((width, tr), lambda i: (0, i))


def _rmsnorm_fwd(x, gain, name, tr, transposed=False):
    rows, width = x.shape

    def body(x_ref, g_ref, o_ref, *t_ref):
        xv = x_ref[...]
        r = lax.rsqrt(jnp.mean(xv * xv, axis=-1, keepdims=True) + RMS_EPS)
        h = xv * r * g_ref[...]
        o_ref[...] = h.astype(BF16)
        if transposed:
            t_ref[0][...] = h.T.astype(BF16)

    out_specs, out_shape = [_row_spec(tr, width)], [jax.ShapeDtypeStruct((rows, width), BF16)]
    if transposed:
        out_specs.append(_col_spec(width, tr))
        out_shape.append(jax.ShapeDtypeStruct((width, rows), BF16))
    out = pl.pallas_call(
        body, name=name, grid=(rows // tr,), in_specs=[_row_spec(tr, width), _bcast_spec(width)],
        out_specs=out_specs, out_shape=out_shape,
        compiler_params=_params(("parallel",), 10 * tr * width * 4),
    )(x, gain)
    return out if transposed else out[0]


def _residual_rmsnorm_fwd(x, u, gain, name, tr):
    rows, width = x.shape

    def body(x_ref, u_ref, g_ref, x1_ref, h_ref, ht_ref):
        xv = x_ref[...] + u_ref[...]
        x1_ref[...] = xv
        r = lax.rsqrt(jnp.mean(xv * xv, axis=-1, keepdims=True) + RMS_EPS)
        h = xv * r * g_ref[...]
        h_ref[...] = h.astype(BF16)
        ht_ref[...] = h.T.astype(BF16)

    return pl.pallas_call(
        body, name=name, grid=(rows // tr,),
        in_specs=[_row_spec(tr, width), _row_spec(tr, width), _bcast_spec(width)],
        out_specs=[_row_spec(tr, width), _row_spec(tr, width), _col_spec(width, tr)],
        out_shape=[jax.ShapeDtypeStruct((rows, width), F32), jax.ShapeDtypeStruct((rows, width), BF16),
                   jax.ShapeDtypeStruct((width, rows), BF16)],
        compiler_params=_params(("parallel",), 14 * tr * width * 4),
    )(x, u, gain)


def _rmsnorm_bwd(x, dh, dres, gain, name, tr):
    rows, width = x.shape

    def body(x_ref, dh_ref, dres_ref, g_ref, dx_ref, dxb_ref, dg_ref):
        xv = x_ref[...]
        r = lax.rsqrt(jnp.mean(xv * xv, axis=-1, keepdims=True) + RMS_EPS)
        xhat = xv * r
        dhv = dh_ref[...]
        dyg = dhv * g_ref[...]
        dx = dres_ref[...] + r * (dyg - xhat * jnp.mean(dyg * xhat, axis=-1, keepdims=True))
        dx_ref[...] = dx
        dxb_ref[...] = dx.astype(BF16)
        part = jnp.sum(dhv * xhat, axis=0, keepdims=True)

        @pl.when(pl.program_id(0) == 0)
        def _():
            dg_ref[...] = part

        @pl.when(pl.program_id(0) > 0)
        def _():
            dg_ref[...] += part

    return pl.pallas_call(
        body, name=name, grid=(rows // tr,),
        in_specs=[_row_spec(tr, width), _row_spec(tr, width), _row_spec(tr, width), _bcast_spec(width)],
        out_specs=[_row_spec(tr, width), _row_spec(tr, width), _bcast_spec(width)],
        out_shape=[jax.ShapeDtypeStruct((rows, width), F32), jax.ShapeDtypeStruct((rows, width), BF16),
                   jax.ShapeDtypeStruct((1, width), F32)],
        compiler_params=_params(("arbitrary",), 14 * tr * width * 4),
    )(x, dh, dres, gain)


def _gain_grad(x, dh, name):
    rows, width = x.shape

    def body(x_ref, dh_ref, dg_ref):
        xv = x_ref[...]
        r = lax.rsqrt(jnp.mean(xv * xv, axis=-1, keepdims=True) + RMS_EPS)
        dg_ref[...] = jnp.sum(dh_ref[...] * xv * r, axis=0, keepdims=True)

    return pl.pallas_call(
        body, name=name, grid=(1,), in_specs=[_row_spec(rows, width), _row_spec(rows, width)],
        out_specs=_bcast_spec(width), out_shape=jax.ShapeDtypeStruct((1, width), F32),
        compiler_params=_params(("arbitrary",), 6 * rows * width * 4),
    )(x, dh)


def _lb_table(logits, name):
    slots, width = logits.shape

    def body(l_ref, o_ref):
        lv = l_ref[...]
        mx = jnp.max(lv, axis=0, keepdims=True)
        e = jnp.exp(lv - mx)
        o_ref[...] = e[0:1, :] / jnp.sum(e, axis=0, keepdims=True)

    return pl.pallas_call(
        body, name=name, grid=(1,), in_specs=[pl.BlockSpec((slots, width), lambda i: (0, 0))],
        out_specs=_bcast_spec(width), out_shape=jax.ShapeDtypeStruct((1, width), F32),
    )(logits)


def _gate_merge_fwd(proj, t_hg, t_da, t_mem, tr):
    w = D_MODEL

    def body(ghg_ref, gda_ref, gmem_ref, thg_ref, tda_ref, tmem_ref, o_ref):
        acc = _sigmoid(ghg_ref[...]) * thg_ref[...]
        acc += _sigmoid(gda_ref[...]) * tda_ref[...]
        acc += _sigmoid(gmem_ref[...]) * tmem_ref[...]
        o_ref[...] = acc.astype(BF16)

    return pl.pallas_call(
        body, name="gate_merge_fwd", grid=(SEQ // tr,),
        in_specs=[_row_spec(tr, w, 10), _row_spec(tr, w, 11), _row_spec(tr, w, 12),
                  _row_spec(tr, w), _row_spec(tr, w), _row_spec(tr, w)],
        out_specs=_row_spec(tr, w), out_shape=jax.ShapeDtypeStruct((SEQ, w), BF16),
        compiler_params=_params(("parallel",), 16 * tr * w * 4),
    )(proj, proj, proj, t_hg, t_da, t_mem)


def _dproj_buffer():
    return lax.empty((SEQ, IN_COLS), BF16)


def _gate_bwd(proj, t, dmerged, dproj, branch, tr):
    w = D_MODEL
    gate_block = 10 + branch

    def body(g_ref, t_ref, dm_ref, dproj_in, dt_ref, dg_ref):
        del dproj_in
        dm = dm_ref[...]
        s = _sigmoid(g_ref[...])
        dt_ref[...] = (s * dm).astype(BF16)
        dg_ref[...] = (dm * t_ref[...] * s * (1.0 - s)).astype(BF16)

    return pl.pallas_call(
        body, name=f"gate_bwd_{branch}", grid=(SEQ // tr,),
        in_specs=[_row_spec(tr, w, gate_block), _row_spec(tr, w), _row_spec(tr, w), _any_spec()],
        out_specs=[_row_spec(tr, w), _row_spec(tr, w, gate_block)],
        out_shape=[jax.ShapeDtypeStruct((SEQ, w), BF16), jax.ShapeDtypeStruct((SEQ, IN_COLS), BF16)],
        input_output_aliases={3: 1},
        compiler_params=_params(("parallel",), 12 * tr * w * 4),
    )(proj, t, dmerged, dproj)


def _ffn_in_swiglu(h2, wfin, tm, tn):
    gm, gn = SEQ // tm, D_FF // tn

    def body(h_ref, wa_ref, wb_ref, a_ref, b_ref, act_ref, actt_ref):
        h = h_ref[...]
        a = _dot(h, wa_ref[...], _NN)
        b = _dot(h, wb_ref[...], _NN)
        act = a * _sigmoid(a) * b
        a_ref[...] = a.astype(BF16)
        b_ref[...] = b.astype(BF16)
        act_ref[...] = act.astype(BF16)
        actt_ref[...] = act.T.astype(BF16)

    tile = pl.BlockSpec((tm, tn), lambda j, i: (i, j))
    shape = jax.ShapeDtypeStruct((SEQ, D_FF), BF16)
    return pl.pallas_call(
        body, name="ffn_in_swiglu_fwd", grid=(gn, gm),
        in_specs=[pl.BlockSpec((tm, D_MODEL), lambda j, i: (i, 0)),
                  pl.BlockSpec((D_MODEL, tn), lambda j, i: (0, j)),
                  pl.BlockSpec((D_MODEL, tn), lambda j, i: (0, gn + j))],
        out_specs=[tile, tile, tile, pl.BlockSpec((tn, tm), lambda j, i: (j, i))],
        out_shape=[shape, shape, shape, jax.ShapeDtypeStruct((D_FF, SEQ), BF16)],
        compiler_params=_params(("parallel", "parallel"), 4 * tm * D_MODEL + 8 * D_MODEL * tn + 16 * tm * tn
                                + 6 * tm * tn * 4),
    )(h2, wfin, wfin)


def _swiglu_bwd(a, b, dact, tr):
    def body(a_ref, b_ref, d_ref, o_ref):
        av = a_ref[...].astype(F32)
        bv = b_ref[...].astype(F32)
        d = d_ref[...]
        s = _sigmoid(av)
        silu = av * s
        o_ref[:, :D_FF] = (d * bv * (s + silu * (1.0 - s))).astype(BF16)
        o_ref[:, D_FF:] = (d * silu).astype(BF16)

    return pl.pallas_call(
        body, name="swiglu_bwd", grid=(SEQ // tr,), in_specs=[_row_spec(tr, D_FF)] * 3,
        out_specs=_row_spec(tr, 2 * D_FF), out_shape=jax.ShapeDtypeStruct((SEQ, 2 * D_FF), BF16),
        compiler_params=_params(("parallel",), 10 * tr * 2 * D_FF * 4),
    )(a, b, dact)


def _loss_head(x1, z, target, tr):
    w = D_MODEL

    def body(x_ref, z_ref, t_ref, dy_ref, dyb_ref, loss_ref, acc_ref):
        err = x_ref[...] + z_ref[...] - t_ref[...]
        dy = err * (1.0 / w)
        dy_ref[...] = dy
        dyb_ref[...] = dy.astype(BF16)
        part = jnp.sum(err * err, axis=0, keepdims=True)

        @pl.when(pl.program_id(0) == 0)
        def _():
            acc_ref[...] = part

        @pl.when(pl.program_id(0) > 0)
        def _():
            acc_ref[...] += part

        @pl.when(pl.program_id(0) == SEQ // tr - 1)
        def _():
            total = jnp.sum(acc_ref[...], axis=1, keepdims=True) * (0.5 / w)
            loss_ref[...] = jnp.broadcast_to(total, (1, LANE))

    return pl.pallas_call(
        body, name="loss_head", grid=(SEQ // tr,),
        in_specs=[_row_spec(tr, w), _row_spec(tr, w), _row_spec(tr, w)],
        out_specs=[_row_spec(tr, w), _row_spec(tr, w), _bcast_spec(LANE)],
        out_shape=[jax.ShapeDtypeStruct((SEQ, w), F32), jax.ShapeDtypeStruct((SEQ, w), BF16),
                   jax.ShapeDtypeStruct((1, LANE), F32)],
        scratch_shapes=[pltpu.VMEM((1, w), F32)],
        compiler_params=_params(("arbitrary",), 12 * tr * w * 4),
    )(x1, z, target)


GLA_ROWS = 256
GLA_CPB = GLA_ROWS // HG_CHUNK
GLA_NBLK = SEQ // GLA_ROWS
GLA_WAYS = 4
GLA_TRIPS = GLA_NBLK // GLA_WAYS
GLA_GRAD_WAYS = 4
GLA_GRAD_TRIPS = GLA_NBLK // GLA_GRAD_WAYS
GLA_NCK = SEQ // HG_CHUNK


def _dot_split(m, xv, dims):
    hi = xv.astype(BF16)
    r1 = xv - hi.astype(F32)
    mid = r1.astype(BF16)
    lo = (r1 - mid.astype(F32)).astype(BF16)
    dot = lambda t: lax.dot_general(m, t, dims, preferred_element_type=F32)
    return (dot(lo) + dot(mid)) + dot(hi)


def _gla_block(qc, fc, lbv, masks):
    mask, maskb, direction = masks
    sq = _sigmoid(qc)
    q = qc * sq * HG_SCALE
    sf = _sigmoid(fc)
    forget = lbv + (1.0 - lbv) * sf
    k = 1.0 - forget
    logf = jnp.log(forget)
    b = _dot_split(maskb, logf, _NN)
    ends = []
    for j in range(GLA_CPB):
        lo, hi = j * HG_CHUNK, (j + 1) * HG_CHUNK
        end = jnp.where(direction == 0, b[hi - 1:hi, :], b[lo:lo + 1, :])
        ends.append(jnp.broadcast_to(end, (HG_CHUNK, HEAD_DIM)))
    bt = jnp.concatenate(ends, axis=0)
    eb = jnp.exp(b)
    qt = q * eb
    kt = k * jnp.exp(-b)
    kh = k * jnp.exp(bt - b)
    a = jnp.where(mask, _dot(qt, kt, _NT), 0.0)
    return dict(sq=sq, sf=sf, forget=forget, k=k, b=b, bt=bt, eb=eb, qt=qt, kt=kt, kh=kh, a=a)


def _gla_masks(direction):
    row = lax.broadcasted_iota(jnp.int32, (GLA_ROWS, GLA_ROWS), 0)
    col = lax.broadcasted_iota(jnp.int32, (GLA_ROWS, GLA_ROWS), 1)
    same = (row // HG_CHUNK) == (col // HG_CHUNK)
    mask = jnp.logical_and(same, jnp.where(direction == 0, row - col, col - row) >= 0)
    return mask, jnp.where(mask, 1.0, 0.0).astype(BF16), direction


def _gla_chunk_rows(j):
    return slice(j * HG_CHUNK, (j + 1) * HG_CHUNK)


def _gla_block_rows(b):
    return pl.ds(pl.multiple_of(b * GLA_ROWS, GLA_ROWS), GLA_ROWS)


def _head_spec(col_block0):
    return pl.BlockSpec((SEQ, HEAD_DIM), lambda h, d: (0, col_block0 + h))


def _gla_fwd(proj, lb, gain):
    nck = GLA_NCK

    def body(q_ref, f_ref, v_ref, g_ref, lb_ref, gain_ref, ohg_ref, opre_ref, st_ref, qt_scr, dec_scr, cs_scr):
        d = pl.program_id(1)
        masks = _gla_masks(d)
        lbv = lb_ref[...]

        @pl.when(d == 0)
        def _():
            opre_ref[...] = jnp.zeros_like(opre_ref)

        def intra(s, carry):
            blocks = [s + w * GLA_TRIPS for w in range(GLA_WAYS)]
            rows = [_gla_block_rows(b) for b in blocks]
            loaded = [(q_ref[r, :], f_ref[r, :], v_ref[r, :], opre_ref[r, :]) for r in rows]
            results = []
            for qc, fc, v, o_prev in loaded:
                ck = _gla_block(qc, fc, lbv, masks)
                o = o_prev + _dot(ck["a"], v, _NN)
                cs = [_dot(v[_gla_chunk_rows(j), :], ck["kh"][_gla_chunk_rows(j), :], _TN) for j in range(GLA_CPB)]
                dec = [jnp.exp(ck["bt"][j * HG_CHUNK:j * HG_CHUNK + 8, :]) for j in range(GLA_CPB)]
                results.append((o, ck["qt"].astype(BF16), cs, dec))
            for b, r, (o, qt, cs, dec) in zip(blocks, rows, results):
                opre_ref[r, :] = o
                qt_scr[r, :] = qt
                for j in range(GLA_CPB):
                    cs_scr[b * GLA_CPB + j] = cs[j]
                    dec_scr[b * GLA_CPB + j] = dec[j]
            return carry

        lax.fori_loop(0, GLA_TRIPS, intra, 0)

        def scan(i, st):
            c = jnp.where(d == 0, i, nck - 1 - i)
            st_ref[c] = st
            return st * dec_scr[c][0:1, :] + cs_scr[c]

        lax.fori_loop(0, nck, scan, jnp.zeros((HEAD_DIM, HEAD_DIM), F32), unroll=4)

        def inter(c, carry):
            rows = pl.ds(pl.multiple_of(c * HG_CHUNK, HG_CHUNK), HG_CHUNK)
            opre_ref[rows, :] += _dot(qt_scr[rows, :], st_ref[c], _NT)
            return carry

        lax.fori_loop(0, nck, inter, 0, unroll=4)

        @pl.when(d == 1)
        def _():
            o = opre_ref[...]
            r = lax.rsqrt(jnp.mean(o * o, axis=-1, keepdims=True) + RMS_EPS)
            g = g_ref[...]
            ohg_ref[...] = (o * r * gain_ref[...] * (g * _sigmoid(g))).astype(BF16)

    blk = SEQ * HEAD_DIM * 4
    return pl.pallas_call(
        body, name="gla_fwd", grid=(HG_HEADS, 2),
        in_specs=[_head_spec(CB_HG_Q),
                  pl.BlockSpec((SEQ, HEAD_DIM), lambda h, d: (0, CB_F + 8 * d + h)),
                  _head_spec(CB_HG_I), _head_spec(CB_HG_G),
                  pl.BlockSpec((None, None, 1, HEAD_DIM), lambda h, d: (d, h, 0, 0)),
                  pl.BlockSpec((1, HEAD_DIM), lambda h, d: (0, 0))],
        out_specs=[_head_spec(0), _head_spec(0),
                   pl.BlockSpec((None, None, nck, HEAD_DIM, HEAD_DIM), lambda h, d: (h, d, 0, 0, 0)),
                   pl.BlockSpec((None, SEQ, HEAD_DIM), lambda h, d: (d, 0, h)),
                   pl.BlockSpec((None, None, nck, 8, HEAD_DIM), lambda h, d: (h, d, 0, 0, 0))],
        out_shape=[jax.ShapeDtypeStruct((SEQ, D_MODEL), BF16), jax.ShapeDtypeStruct((SEQ, D_MODEL), F32),
                   jax.ShapeDtypeStruct((HG_HEADS, 2, nck, HEAD_DIM, HEAD_DIM), F32),
                   jax.ShapeDtypeStruct((2, SEQ, D_MODEL), BF16),
                   jax.ShapeDtypeStruct((HG_HEADS, 2, nck, 8, HEAD_DIM), F32)],
        scratch_shapes=[pltpu.VMEM((nck, HEAD_DIM, HEAD_DIM), F32)],
        compiler_params=_params(("parallel", "arbitrary"), 8 * blk + 3 * blk + 2 * blk + 2 * blk + blk),
    )(proj, proj, proj, proj, lb, gain)


def _gla_bwd(proj, lb, gain, o_pre, states, qt_all, dec_all, do_hg, dproj):
    nck = GLA_NCK
    do, dproj, dgain = _gla_bwd_norm(proj, gain, o_pre, do_hg, dproj, 256)

    def body(q_ref, f_ref, v_ref, lb_ref, do_ref, st_ref, qt_ref, dec_scr, dproj_in, dproj_ref, dlb_ref,
             dq_acc, dv_acc, dst_scr, cs_scr, df_out, dq_out, dv_out, sems):
        del dproj_in
        h, d = pl.program_id(0), pl.program_id(1)
        masks = _gla_masks(d)
        mask, maskb, _ = masks
        lbv = lb_ref[...]

        def column_copy(k, staging, col_block):
            cols = pl.ds(pl.multiple_of(col_block * LANE, LANE), LANE)
            return pltpu.make_async_copy(staging, dproj_ref.at[:, cols], sems.at[k])

        df_copy = column_copy(0, df_out, CB_F + 8 * d + h)
        dq_copy = column_copy(1, dq_out, CB_HG_Q + h)
        dv_copy = column_copy(2, dv_out, CB_HG_I + h)
        last = jnp.logical_and(h == HG_HEADS - 1, d == 1)

        def intra(s, carry):
            blocks = [s + w * GLA_TRIPS for w in range(GLA_WAYS)]
            loaded = [(qt_ref[r, :], do_ref[r, :]) for r in map(_gla_block_rows, blocks)]
            results = [[_dot(doc[_gla_chunk_rows(j), :], qt[_gla_chunk_rows(j), :], _TN) for j in range(GLA_CPB)]
                       for qt, doc in loaded]
            for b, cs in zip(blocks, results):
                for j in range(GLA_CPB):
                    cs_scr[b * GLA_CPB + j] = cs[j]
            return carry

        lax.fori_loop(0, GLA_TRIPS, intra, 0)

        def scan(i, dst):
            c = jnp.where(d == 0, nck - 1 - i, i)
            dst_scr[c] = dst
            return dst * dec_scr[c][0:1, :] + cs_scr[c]

        lax.fori_loop(0, nck, scan, jnp.zeros((HEAD_DIM, HEAD_DIM), F32), unroll=4)

        @pl.when(d == 0)
        def _():
            dq_acc[...] = jnp.zeros_like(dq_acc)
            dv_acc[...] = jnp.zeros_like(dv_acc)

        def block_grads(qc, fc, v, doc, states_in, dstates, decays):
            ck = _gla_block(qc, fc, lbv, masks)
            qt, kt, kh, a = ck["qt"], ck["kt"], ck["kh"], ck["a"]
            da = jnp.where(mask, _dot(doc, v, _NT), 0.0)
            dqt_i = _dot(da, kt, _NN)
            dkt = _dot(da, qt, _TN)
            dv_i = _dot(a, doc, _TN)
            dqt_p, dv_p, dkh_p, dbt_p = [], [], [], []
            for j in range(GLA_CPB):
                cr = _gla_chunk_rows(j)
                st_in, dst = states_in[j], dstates[j]
                dqt_p.append(dqt_i[cr, :] + _dot(doc[cr, :], st_in, _NN))
                dv_p.append(dv_i[cr, :] + _dot(kh[cr, :], dst, _NT))
                dkh_j = _dot(v[cr, :], dst, _NN)
                dkh_p.append(dkh_j)
                dbt_j = (decays[j][0:1, :] * jnp.sum(dst * st_in, axis=0, keepdims=True)
                         + jnp.sum(dkh_j * kh[cr, :], axis=0, keepdims=True))
                dbt_p.append(jnp.broadcast_to(dbt_j, (HG_CHUNK, HEAD_DIM)))
            dqt = jnp.concatenate(dqt_p, axis=0)
            dv = jnp.concatenate(dv_p, axis=0)
            dkh = jnp.concatenate(dkh_p, axis=0)
            dbt = jnp.concatenate(dbt_p, axis=0)
            db = dqt * qt - dkt * kt - dkh * kh
            dq = dqt * ck["eb"]
            dk = dkt * jnp.exp(-ck["b"]) + dkh * jnp.exp(ck["bt"] - ck["b"])
            dlogf = _dot_split(maskb, db, _TN) + dbt
            dforget = dlogf / ck["forget"] - dk
            sf, sq = ck["sf"], ck["sq"]
            df = (dforget * (1.0 - lbv) * sf * (1.0 - sf)).astype(BF16)
            dqc = dq * HG_SCALE * (sq + qc * sq * (1.0 - sq))
            return df, dqc, dv, jnp.sum(dforget * (1.0 - sf), axis=0, keepdims=True)

        def grads(s, dlb):
            blocks = [s + w * GLA_GRAD_TRIPS for w in range(GLA_GRAD_WAYS)]
            rows = [_gla_block_rows(b) for b in blocks]
            loaded = []
            for b, r in zip(blocks, rows):
                chunks = [b * GLA_CPB + j for j in range(GLA_CPB)]
                loaded.append((q_ref[r, :], f_ref[r, :], v_ref[r, :], do_ref[r, :], [st_ref[c] for c in chunks],
                               [dst_scr[c] for c in chunks], [dec_scr[c] for c in chunks], dq_acc[r, :], dv_acc[r, :]))
            results = [block_grads(*t[:7]) + (t[7], t[8]) for t in loaded]
            for r, (df, dqc, dv, dlb_part, dq_prev, dv_prev) in zip(rows, results):
                df_out[r, :] = df
                dq_acc[r, :] = dq_prev + dqc
                dv_acc[r, :] = dv_prev + dv
                dlb = dlb + dlb_part
            return dlb

        @pl.when(jnp.logical_or(h > 0, d > 0))
        def _():
            df_copy.wait()

        dlb_ref[...] = lax.fori_loop(0, GLA_GRAD_TRIPS, grads, jnp.zeros((1, HEAD_DIM), F32))
        df_copy.start()

        @pl.when(d == 1)
        def _():
            @pl.when(h > 0)
            def _():
                dq_copy.wait()
                dv_copy.wait()

            dq_out[...] = dq_acc[...].astype(BF16)
            dv_out[...] = dv_acc[...].astype(BF16)
            dq_copy.start()
            dv_copy.start()

        @pl.when(last)
        def _():
            df_copy.wait()
            dq_copy.wait()
            dv_copy.wait()

    blk = SEQ * HEAD_DIM * 4
    state_bytes = nck * HEAD_DIM * HEAD_DIM * 4
    dproj, dlb = pl.pallas_call(
        body, name="gla_bwd", grid=(HG_HEADS, 2),
        in_specs=[_head_spec(CB_HG_Q),
                  pl.BlockSpec((SEQ, HEAD_DIM), lambda h, d: (0, CB_F + 8 * d + h)),
                  _head_spec(CB_HG_I),
                  pl.BlockSpec((None, None, 1, HEAD_DIM), lambda h, d: (d, h, 0, 0)),
                  _head_spec(0),
                  pl.BlockSpec((None, None, nck, HEAD_DIM, HEAD_DIM), lambda h, d: (h, d, 0, 0, 0)),
                  pl.BlockSpec((None, SEQ, HEAD_DIM), lambda h, d: (d, 0, h)),
                  pl.BlockSpec((None, None, nck, 8, HEAD_DIM), lambda h, d: (h, d, 0, 0, 0)),
                  _any_spec()],
        out_specs=[_any_spec(), pl.BlockSpec((None, None, 1, HEAD_DIM), lambda h, d: (d, h, 0, 0))],
        out_shape=[jax.ShapeDtypeStruct((SEQ, IN_COLS), BF16), jax.ShapeDtypeStruct((2, HG_HEADS, 1, HEAD_DIM), F32)],
        scratch_shapes=[pltpu.VMEM((SEQ, HEAD_DIM), F32)] * 2
        + [pltpu.VMEM((nck, HEAD_DIM, HEAD_DIM), F32)] * 2
        + [pltpu.VMEM((SEQ, HEAD_DIM), BF16)] * 3 + [pltpu.SemaphoreType.DMA((3,))],
        input_output_aliases={8: 0},
        compiler_params=_params(("arbitrary", "arbitrary"), 8 * blk + 4 * state_bytes + 3 * blk + 3 * blk),
    )(proj, proj, proj, lb, do, states, qt_all, dec_all, dproj)
    return dproj, dlb, dgain


def _gla_bwd_norm(proj, gain, o_pre, do_hg, dproj, tr):
    w = D_MODEL

    def body(g_ref, gain_ref, opre_ref, dohg_ref, dproj_in, do_ref, dg_ref, dgain_ref):
        del dproj_in
        gainv = gain_ref[...]
        part = jnp.zeros((1, HEAD_DIM), F32)
        for h in range(HG_HEADS):
            hs = slice(h * HEAD_DIM, (h + 1) * HEAD_DIM)
            o = opre_ref[:, hs]
            r = lax.rsqrt(jnp.mean(o * o, axis=-1, keepdims=True) + RMS_EPS)
            ohat = o * r
            g = g_ref[:, hs]
            sg = _sigmoid(g)
            silu = g * sg
            dout = dohg_ref[:, hs]
            dg_ref[:, hs] = (dout * ohat * gainv * (sg + silu * (1.0 - sg))).astype(BF16)
            dy = dout * silu
            part = part + jnp.sum(dy * ohat, axis=0, keepdims=True)
            dn = dy * gainv
            do_ref[:, hs] = r * (dn - ohat * jnp.mean(dn * ohat, axis=-1, keepdims=True))

        @pl.when(pl.program_id(0) == 0)
        def _():
            dgain_ref[...] = part

        @pl.when(pl.program_id(0) > 0)
        def _():
            dgain_ref[...] += part

    return pl.pallas_call(
        body, name="gla_bwd_norm", grid=(SEQ // tr,),
        in_specs=[_row_spec(tr, w, CB_HG_G * LANE // w), _bcast_spec(HEAD_DIM), _row_spec(tr, w), _row_spec(tr, w),
                  _any_spec()],
        out_specs=[_row_spec(tr, w), _row_spec(tr, w, CB_HG_G * LANE // w), _bcast_spec(HEAD_DIM)],
        out_shape=[jax.ShapeDtypeStruct((SEQ, w), F32), jax.ShapeDtypeStruct((SEQ, IN_COLS), BF16),
                   jax.ShapeDtypeStruct((1, HEAD_DIM), F32)],
        input_output_aliases={4: 1},
        compiler_params=_params(("arbitrary",), 16 * tr * w * 4),
    )(proj, gain, o_pre, do_hg, dproj)


def _da_residue_rows(r, n0, size, d):
    if d == 1:
        return pl.ds(pl.multiple_of(n0, 8), size)
    return pl.ds(r + n0 * d, size, stride=d)


def _da_rmsnorm(x, gain):
    r = lax.rsqrt(jnp.mean(x * x, axis=-1, keepdims=True) + RMS_EPS)
    return x * r, r, x * r * gain


def _da_scores(qn_scr, kn_scr, slope, i, ld):
    w0 = jnp.clip(i * DA_QB - DA_RADIUS, 0, ld - DA_WIN)
    w0 = pl.multiple_of(w0, DA_RADIUS)
    qrows = pl.ds(pl.multiple_of(i * DA_QB, DA_QB), DA_QB)
    win = pl.ds(w0, DA_WIN)
    qb = qn_scr[qrows, :]
    kw = kn_scr[win, :]
    s = _dot(qb, kw, _NT) * DA_SCALE
    qpos = i * DA_QB + lax.broadcasted_iota(jnp.int32, (DA_QB, DA_WIN), 0)
    kpos = w0 + lax.broadcasted_iota(jnp.int32, (DA_QB, DA_WIN), 1)
    arel = jnp.abs(kpos - qpos)
    s = s - slope * arel.astype(F32)
    s = jnp.where(arel <= DA_RADIUS, s, NEG_INF)
    return s, qb, kw, qrows, win


def _da_slopes(group):
    d = DA_DILATIONS[group]
    sl = _alibi_slopes(DA_HEADS)[4 * group:4 * group + 4] * d
    return jnp.asarray(np.broadcast_to(sl[:, None, None], (4, 1, LANE)).copy())


def _da_fwd(proj, gq, gk, group):
    d = DA_DILATIONS[group]
    ld = SEQ // d
    nqb = ld // DA_QB

    def body(q_ref, k_ref, v_ref, gq_ref, gk_ref, sl_ref, o_ref, lse_ref, qn_scr, kn_scr, v_scr):
        slope = sl_ref[:, 0:1]

        def residue(r, carry):
            sel = _da_residue_rows(r, 0, ld, d)
            qn_scr[...] = _da_rmsnorm(q_ref[sel, :], gq_ref[...])[2].astype(BF16)
            kn_scr[...] = _da_rmsnorm(k_ref[sel, :], gk_ref[...])[2].astype(BF16)
            v_scr[...] = v_ref[sel, :].astype(BF16)

            def block(i):
                s, _, _, _, win = _da_scores(qn_scr, kn_scr, slope, i, ld)
                m = jnp.max(s, axis=-1, keepdims=True)
                p = jnp.exp(s - m)
                l = jnp.sum(p, axis=-1, keepdims=True)
                return _dot(p, v_scr[win, :], _NN) / l, jnp.broadcast_to(m + jnp.log(l), (DA_QB, HEAD_DIM))

            ways = min(DA_FWD_WAYS, nqb)

            def step(i, c2):
                blocks = [i + w * (nqb // ways) for w in range(ways)]
                for b, (o, lse) in zip(blocks, [block(b) for b in blocks]):
                    out = _da_residue_rows(r, b * DA_QB, DA_QB, d)
                    o_ref[out, :] = o
                    lse_ref[out, :] = lse
                return c2

            return lax.fori_loop(0, nqb // ways, step, carry)

        lax.fori_loop(0, d, residue, 0)

    seq_spec = lambda base: pl.BlockSpec((SEQ, HEAD_DIM), lambda h: (0, base + 4 * group + h))
    out_spec = pl.BlockSpec((SEQ, HEAD_DIM), lambda h: (0, h))
    gain_spec = pl.BlockSpec((1, HEAD_DIM), lambda h: (0, 0))
    blk = SEQ * HEAD_DIM * 4
    return pl.pallas_call(
        body, name=f"da_fwd_g{group}", grid=(DA_HEADS_PER_GROUP,),
        in_specs=[seq_spec(CB_DA_Q), seq_spec(CB_DA_K), seq_spec(CB_DA_V), gain_spec, gain_spec,
                  pl.BlockSpec((None, 1, LANE), lambda h: (h, 0, 0))],
        out_specs=[out_spec, out_spec],
        out_shape=[jax.ShapeDtypeStruct((SEQ, DA_WIDTH), F32)] * 2,
        scratch_shapes=[pltpu.VMEM((ld, HEAD_DIM), BF16)] * 3,
        compiler_params=_params(("parallel",), 10 * blk + 2 * blk),
    )(proj, proj, proj, gq, gk, _da_slopes(group))


def _da_merge(os, lses, tr):
    w = DA_WIDTH

    def body(o0, o1, o2, l0, l1, l2, ob_ref, of_ref, lse_ref):
        la, lb_, lc = l0[...], l1[...], l2[...]
        m = jnp.maximum(jnp.maximum(la, lb_), lc)
        ea, eb, ec = jnp.exp(la - m), jnp.exp(lb_ - m), jnp.exp(lc - m)
        tot = ea + eb + ec
        o = (ea * o0[...] + eb * o1[...] + ec * o2[...]) / tot
        of_ref[...] = o
        ob_ref[...] = o.astype(BF16)
        lse_ref[...] = m + jnp.log(tot)

    return pl.pallas_call(
        body, name="da_merge", grid=(SEQ // tr,), in_specs=[_row_spec(tr, w)] * 6,
        out_specs=[_row_spec(tr, w)] * 3,
        out_shape=[jax.ShapeDtypeStruct((SEQ, w), BF16), jax.ShapeDtypeStruct((SEQ, w), F32),
                   jax.ShapeDtypeStruct((SEQ, w), F32)],
        compiler_params=_params(("parallel",), 24 * tr * w * 4),
    )(*os, *lses)


def _da_rowdot(do, o, tr):
    w = DA_WIDTH

    def body(do_ref, o_ref, out_ref):
        prod = do_ref[...] * o_ref[...]
        for h in range(w // HEAD_DIM):
            sl = slice(h * HEAD_DIM, (h + 1) * HEAD_DIM)
            out_ref[:, sl] = jnp.broadcast_to(jnp.sum(prod[:, sl], axis=-1, keepdims=True), (tr, HEAD_DIM))

    return pl.pallas_call(
        body, name="da_rowdot", grid=(SEQ // tr,), in_specs=[_row_spec(tr, w)] * 2,
        out_specs=_row_spec(tr, w), out_shape=jax.ShapeDtypeStruct((SEQ, w), F32),
        compiler_params=_params(("parallel",), 10 * tr * w * 4),
    )(do, o)


def _da_bwd(proj, gq, gk, do, lse, dd, dproj, group):
    d = DA_DILATIONS[group]
    ld = SEQ // d
    nqb = ld // DA_QB

    def body(q_ref, k_ref, v_ref, gq_ref, gk_ref, sl_ref, do_ref, lse_ref, dd_ref, dproj_in,
             dproj_ref, dgq_ref, dgk_ref,
             qn_scr, kn_scr, v_scr, dqn_scr, dkn_scr, dvr_scr, dq_scr, dk_scr, dv_scr, dq_out, dk_out, dv_out, sems):
        del dproj_in
        head = pl.program_id(0)
        gqv, gkv = gq_ref[...], gk_ref[...]
        slope = sl_ref[:, 0:1]

        copies = []
        for k, (staging, base) in enumerate(((dq_out, CB_DA_Q), (dk_out, CB_DA_K), (dv_out, CB_DA_V))):
            cols = pl.ds(pl.multiple_of((base + 4 * group + head) * LANE, LANE), LANE)
            copies.append(pltpu.make_async_copy(staging, dproj_ref.at[:, cols], sems.at[k]))

        def residue(r, carry):
            sel = _da_residue_rows(r, 0, ld, d)
            qn_scr[...] = _da_rmsnorm(q_ref[sel, :], gqv)[2].astype(BF16)
            kn_scr[...] = _da_rmsnorm(k_ref[sel, :], gkv)[2].astype(BF16)
            v_scr[...] = v_ref[sel, :].astype(BF16)
            dkn_scr[...] = jnp.zeros_like(dkn_scr)
            dvr_scr[...] = jnp.zeros_like(dvr_scr)

            def block(i):
                s, qb, kw, qrows, win = _da_scores(qn_scr, kn_scr, slope, i, ld)
                src = _da_residue_rows(r, i * DA_QB, DA_QB, d)
                p = jnp.exp(s - lse_ref[src, :][:, 0:1])
                dob = do_ref[src, :]
                dp = _dot(dob, v_scr[win, :], _NT)
                ds = p * (dp - dd_ref[src, :][:, 0:1]) * DA_SCALE
                return qrows, win, _dot(p, dob, _TN), _dot(ds, kw, _NN), _dot(ds, qb, _TN)

            ways = min(DA_WAYS, nqb)

            def step(i, c2):
                blocks = [i + w * (nqb // ways) for w in range(ways)]
                for qrows, win, dv, dqn, dkn in [block(b) for b in blocks]:
                    dvr_scr[win, :] += dv
                    dqn_scr[qrows, :] = dqn
                    dkn_scr[win, :] += dkn
                return c2

            lax.fori_loop(0, nqb // ways, step, 0)

            parts = []
            for x_ref, gv, dn_scr, dx_scr in ((q_ref, gqv, dqn_scr, dq_scr), (k_ref, gkv, dkn_scr, dk_scr)):
                hat, rstd, _ = _da_rmsnorm(x_ref[sel, :], gv)
                dn = dn_scr[...]
                dyg = dn * gv
                dx_scr[sel, :] = rstd * (dyg - hat * jnp.mean(dyg * hat, axis=-1, keepdims=True))
                parts.append(jnp.sum(dn * hat, axis=0, keepdims=True))
            dv_scr[sel, :] = dvr_scr[...]
            return carry[0] + parts[0], carry[1] + parts[1]

        zero = jnp.zeros((1, HEAD_DIM), F32)
        pq, pk = lax.fori_loop(0, d, residue, (zero, zero))

        @pl.when(pl.program_id(0) == 0)
        def _():
            dgq_ref[...] = pq
            dgk_ref[...] = pk

        @pl.when(pl.program_id(0) > 0)
        def _():
            dgq_ref[...] += pq
            dgk_ref[...] += pk

        @pl.when(head > 0)
        def _():
            for cp in copies:
                cp.wait()

        dq_out[...] = dq_scr[...].astype(BF16)
        dk_out[...] = dk_scr[...].astype(BF16)
        dv_out[...] = dv_scr[...].astype(BF16)
        for cp in copies:
            cp.start()

        @pl.when(head == DA_HEADS_PER_GROUP - 1)
        def _():
            for cp in copies:
                cp.wait()

    seq_spec = lambda base: pl.BlockSpec((SEQ, HEAD_DIM), lambda h: (0, base + 4 * group + h))
    out_spec = pl.BlockSpec((SEQ, HEAD_DIM), lambda h: (0, h))
    gain_spec = pl.BlockSpec((1, HEAD_DIM), lambda h: (0, 0))
    gshape = jax.ShapeDtypeStruct((1, HEAD_DIM), F32)
    blk = SEQ * HEAD_DIM * 4
    return pl.pallas_call(
        body, name=f"da_bwd_g{group}", grid=(DA_HEADS_PER_GROUP,),
        in_specs=[seq_spec(CB_DA_Q), seq_spec(CB_DA_K), seq_spec(CB_DA_V), gain_spec, gain_spec,
                  pl.BlockSpec((None, 1, LANE), lambda h: (h, 0, 0)), out_spec, out_spec, out_spec, _any_spec()],
        out_specs=[_any_spec(), gain_spec, gain_spec],
        out_shape=[jax.ShapeDtypeStruct((SEQ, IN_COLS), BF16), gshape, gshape],
        scratch_shapes=[pltpu.VMEM((ld, HEAD_DIM), BF16)] * 3 + [pltpu.VMEM((ld, HEAD_DIM), F32)] * 3
        + [pltpu.VMEM((SEQ, HEAD_DIM), F32)] * 3 + [pltpu.VMEM((SEQ, HEAD_DIM), BF16)] * 3
        + [pltpu.SemaphoreType.DMA((3,))],
        input_output_aliases={9: 0},
        compiler_params=_params(("arbitrary",), 12 * blk + 3 * blk + 3 * blk + 3 * blk + 2 * blk),
    )(proj, proj, proj, gq, gk, _da_slopes(group), do, lse, dd, dproj)


def _mem_softmax(q, k, gq, gk):
    qhat, rq, qn = _da_rmsnorm(q, gq)
    khat, rk, kn = _da_rmsnorm(k, gk)
    s = _dot(qn, kn, _NT) * MEM_SCALE
    m = jnp.max(s, axis=-1, keepdims=True)
    e = jnp.exp(s - m)
    p = e / jnp.sum(e, axis=-1, keepdims=True)
    return p, (qhat, rq, qn), (khat, rk, kn)


def _mem_fwd(proj, kv, gq, gk, tq):
    def body(q_ref, k_ref, v_ref, gq_ref, gk_ref, o_ref):
        p, _, _ = _mem_softmax(q_ref[...], k_ref[...], gq_ref[...], gk_ref[...])
        o_ref[...] = _dot(p, v_ref[...], _NN).astype(BF16)

    gain_spec = pl.BlockSpec((1, HEAD_DIM), lambda h, i: (0, 0))
    return pl.pallas_call(
        body, name="mem_fwd", grid=(MEM_HEADS, SEQ // tq),
        in_specs=[pl.BlockSpec((tq, HEAD_DIM), lambda h, i: (i, CB_MEM_Q + h)),
                  pl.BlockSpec((N_MEM, HEAD_DIM), lambda h, i: (0, h)),
                  pl.BlockSpec((N_MEM, HEAD_DIM), lambda h, i: (0, MEM_HEADS + h)), gain_spec, gain_spec],
        out_specs=pl.BlockSpec((tq, HEAD_DIM), lambda h, i: (i, h)),
        out_shape=jax.ShapeDtypeStruct((SEQ, MEM_WIDTH), BF16),
        compiler_params=_params(("parallel", "parallel"), 16 * tq * N_MEM * 4),
    )(proj, kv, kv, gq, gk)


def _mem_bwd(proj, kv, gq, gk, do, dproj, tq):
    nq = SEQ // tq

    def body(q_ref, k_ref, v_ref, gq_ref, gk_ref, do_ref, dproj_in, dq_ref, dk_ref, dv_ref, dgq_ref, dgk_ref, dkn_scr):
        del dproj_in
        h, i = pl.program_id(0), pl.program_id(1)
        gqv, gkv = gq_ref[...], gk_ref[...]
        p, (qhat, rq, qn), (khat, rk, kn) = _mem_softmax(q_ref[...], k_ref[...], gqv, gkv)
        dob = do_ref[...]
        dvp = _dot(p, dob, _TN)
        dp = _dot(dob, v_ref[...], _NT)
        ds = p * (dp - jnp.sum(p * dp, axis=-1, keepdims=True)) * MEM_SCALE
        dqn = _dot(ds, kn, _NN)
        dknp = _dot(ds, qn, _TN)
        dyg = dqn * gqv
        dq_ref[...] = (rq * (dyg - qhat * jnp.mean(dyg * qhat, axis=-1, keepdims=True))).astype(BF16)
        dgq_part = jnp.sum(dqn * qhat, axis=0, keepdims=True)
        first = jnp.logical_and(h == 0, i == 0)

        @pl.when(first)
        def _():
            dgq_ref[...] = dgq_part

        @pl.when(jnp.logical_not(first))
        def _():
            dgq_ref[...] += dgq_part

        @pl.when(i == 0)
        def _():
            dv_ref[...] = dvp
            dkn_scr[...] = dknp

        @pl.when(i > 0)
        def _():
            dv_ref[...] += dvp
            dkn_scr[...] += dknp

        @pl.when(i == nq - 1)
        def _():
            dkn = dkn_scr[...]
            dkg = dkn * gkv
            dk_ref[...] = rk * (dkg - khat * jnp.mean(dkg * khat, axis=-1, keepdims=True))
            dgk_part = jnp.sum(dkn * khat, axis=0, keepdims=True)

            @pl.when(h == 0)
            def _():
                dgk_ref[...] = dgk_part

            @pl.when(h > 0)
            def _():
                dgk_ref[...] += dgk_part

    gain_spec = pl.BlockSpec((1, HEAD_DIM), lambda h, i: (0, 0))
    kvout = pl.BlockSpec((N_MEM, HEAD_DIM), lambda h, i: (0, h))
    return pl.pallas_call(
        body, name="mem_bwd", grid=(MEM_HEADS, nq),
        in_specs=[pl.BlockSpec((tq, HEAD_DIM), lambda h, i: (i, CB_MEM_Q + h)),
                  pl.BlockSpec((N_MEM, HEAD_DIM), lambda h, i: (0, h)),
                  pl.BlockSpec((N_MEM, HEAD_DIM), lambda h, i: (0, MEM_HEADS + h)), gain_spec, gain_spec,
                  pl.BlockSpec((tq, HEAD_DIM), lambda h, i: (i, h)), _any_spec()],
        out_specs=[pl.BlockSpec((tq, HEAD_DIM), lambda h, i: (i, CB_MEM_Q + h)), kvout, kvout, gain_spec, gain_spec],
        out_shape=[jax.ShapeDtypeStruct((SEQ, IN_COLS), BF16), jax.ShapeDtypeStruct((N_MEM, MEM_WIDTH), F32),
                   jax.ShapeDtypeStruct((N_MEM, MEM_WIDTH), F32), jax.ShapeDtypeStruct((1, HEAD_DIM), F32),
                   jax.ShapeDtypeStruct((1, HEAD_DIM), F32)],
        scratch_shapes=[pltpu.VMEM((N_MEM, HEAD_DIM), F32)], input_output_aliases={6: 0},
        compiler_params=_params(("arbitrary", "arbitrary"), 24 * tq * N_MEM * 4),
    )(proj, kv, kv, gq, gk, do, dproj)


def _mesh_position():
    return lax.axis_index("x"), lax.axis_index("y"), lax.axis_index("c")


def _any_spec():
    return pl.BlockSpec(memory_space=pl.ANY)


def _all_gather(shards):
    n = len(shards)

    def body(*refs):
        ins, outs = refs[:n], refs[n:2 * n]
        send_sems, recv_sems, local_sems = refs[2 * n:]
        x, y, c = _mesh_position()
        me, sibling = (x, y, c), (x, y, 1 - c)
        first = (jnp.where(c == 0, 1 - x, x), jnp.where(c == 0, y, 1 - y), c)
        second = (jnp.where(c == 0, x, 1 - x), jnp.where(c == 0, 1 - y, y), c)
        diagonal = (1 - x, 1 - y, c)

        def copy(w, k, block, to, src=None):
            px, py, pc = block
            rows = outs[w].at[4 * px + 2 * py + pc]
            return pltpu.make_async_remote_copy(
                src_ref=rows if src is None else src, dst_ref=rows,
                send_sem=send_sems.at[7 * w + k], recv_sem=recv_sems.at[7 * w + k],
                device_id=to, device_id_type=MESH)

        started = []
        for w in range(n):
            mine = pltpu.make_async_copy(ins[w], outs[w].at[4 * x + 2 * y + c], local_sems.at[w])
            mine.start()
            started.append(mine)
        sends = []
        for w in range(n):
            own = [copy(w, 0, me, sibling, src=ins[w]), copy(w, 1, me, first, src=ins[w]),
                   copy(w, 2, me, second, src=ins[w])]
            for cp in own:
                cp.start()
            sends += own
        for w in range(n):
            copy(w, 1, first, me).wait_recv()
            follow = [copy(w, 3, first, second), copy(w, 4, first, sibling)]
            for cp in follow:
                cp.start()
            copy(w, 2, second, me).wait_recv()
            follow.append(copy(w, 5, second, sibling))
            follow[-1].start()
            sends += follow
        for w in range(n):
            copy(w, 3, diagonal, me).wait_recv()
            passed = copy(w, 6, diagonal, sibling)
            passed.start()
            sends.append(passed)
        for w in range(n):
            for k in (0, 4, 5, 6):
                copy(w, k, sibling, me).wait_recv()
        for cp in sends:
            cp.wait_send()
        for mine in started:
            mine.wait()

    return pl.pallas_call(
        body, name="weights_all_gather",
        in_specs=[_any_spec()] * n, out_specs=[_any_spec()] * n,
        out_shape=[jax.ShapeDtypeStruct((N_DEV,) + s.shape, s.dtype) for s in shards],
        scratch_shapes=[pltpu.SemaphoreType.DMA((7 * n,)), pltpu.SemaphoreType.DMA((7 * n,)),
                        pltpu.SemaphoreType.DMA((n,))],
    )(*shards)


def _chip_of(j, x, y):
    return (1 - x if j & 1 else x, 1 - y if j & 2 else y)


_HBM_SPEC = pl.BlockSpec(memory_space=pltpu.HBM)
_SEM_SPEC = pl.BlockSpec(memory_space=pltpu.SEMAPHORE)
_DATAFLOW_EFFECT = pltpu.SideEffectType.DATAFLOW_SIDE_EFFECTING
TOKEN_SHAPE = (8, D_MODEL)


def _copies_start(name, arrays, n_copies, plan):
    n = len(arrays)

    def body(*refs):
        send_sems, recv_sems, token = refs[n], refs[n + 1], refs[2 * n + 2]
        copies = plan(refs[:n])
        assert len(copies) == n_copies
        for k, (src, dst, dev) in enumerate(copies):
            pltpu.make_async_remote_copy(src_ref=src, dst_ref=dst, send_sem=send_sems.at[k], recv_sem=recv_sems.at[k],
                                         device_id=dev, device_id_type=MESH).start()
        token[...] = jnp.zeros_like(token)

    outs = pl.pallas_call(
        body, name=name,
        out_shape=(pltpu.SemaphoreType.DMA((n_copies,)), pltpu.SemaphoreType.DMA((n_copies,)),
                   *[pltpu.HBM(a.shape, a.dtype) for a in arrays], jax.ShapeDtypeStruct(TOKEN_SHAPE, F32)),
        in_specs=[_HBM_SPEC] * n,
        out_specs=(_SEM_SPEC, _SEM_SPEC, *[_HBM_SPEC] * n, pl.BlockSpec(memory_space=pltpu.VMEM)),
        input_output_aliases={i: i + 2 for i in range(n)},
        compiler_params=pltpu.CompilerParams(has_side_effects=_DATAFLOW_EFFECT),
    )(*[pltpu.with_memory_space_constraint(a, pltpu.HBM) for a in arrays])
    return outs[0], outs[1], list(outs[2:2 + n]), outs[2 + n]


def _copies_wait(name, send_sems, recv_sems, arrays, n_copies, plan, after):
    n = len(arrays)
    after = list(after) if isinstance(after, (list, tuple)) else [after]

    def body(*refs):
        send_ref, recv_ref = refs[n], refs[n + 1]
        copies = plan(refs[:n])
        assert len(copies) == n_copies
        for k, (src, dst, dev) in enumerate(copies):
            cp = pltpu.make_async_remote_copy(src_ref=src, dst_ref=dst, send_sem=send_ref.at[k], recv_sem=recv_ref.at[k],
                                              device_id=dev, device_id_type=MESH)
            cp.wait_send()
            cp.wait_recv()

    outs = pl.pallas_call(
        body, name=name, out_shape=tuple(pltpu.HBM(a.shape, a.dtype) for a in arrays),
        in_specs=[_HBM_SPEC] * n + [_SEM_SPEC, _SEM_SPEC] + [pl.BlockSpec(memory_space=pl.ANY)] * len(after),
        out_specs=tuple([_HBM_SPEC] * n), input_output_aliases={i: i for i in range(n)},
        compiler_params=pltpu.CompilerParams(has_side_effects=_DATAFLOW_EFFECT),
    )(*arrays, send_sems, recv_sems, *after)
    return list(outs)


def _after(small, token):
    return small if token is None else small + token[0:1, :small.shape[-1]]


def _gather_plan_out(n):
    def plan(refs):
        x, y, c = _mesh_position()
        me = 4 * x + 2 * y + c
        copies = []
        for w in range(n):
            land = refs[n + w].at[me]
            copies.append((refs[w], land, (x, y, 1 - c)))
            for j in range(1, 4):
                copies.append((refs[w], land, (*_chip_of(j, x, y), c)))
        return copies
    return plan


def _gather_plan_pass(n):
    def plan(refs):
        x, y, c = _mesh_position()
        copies = []
        for w in range(n):
            for j in range(1, 4):
                px, py = _chip_of(j, x, y)
                rows = refs[w].at[4 * px + 2 * py + c]
                copies.append((rows, rows, (x, y, 1 - c)))
        return copies
    return plan


def _reduce_plan_sibling(n):
    def plan(refs):
        x, y, c = _mesh_position()
        copies = []
        for w in range(n):
            for j in range(4):
                px, py = _chip_of(j, x, y)
                copies.append((refs[w].at[4 * px + 2 * py + (1 - c)], refs[n + w].at[j], (x, y, 1 - c)))
        return copies
    return plan


def _reduce_plan_chips(n):
    def plan(refs):
        x, y, c = _mesh_position()
        copies = []
        for w in range(n):
            for j in range(1, 4):
                copies.append((refs[w].at[j - 1], refs[n + w].at[j - 1], (*_chip_of(j, x, y), c)))
        return copies
    return plan


def _chip_partials(grad, recv, name, tr):
    _, rows, width = grad.shape

    def body(g_ref, r_ref, own_ref, other_ref):
        x, y, c = _mesh_position()
        for j in range(4):
            px, py = _chip_of(j, x, y)
            total = g_ref[4 * px + 2 * py + c].astype(F32) + r_ref[j].astype(F32)
            if j == 0:
                own_ref[...] = total
            else:
                other_ref[j - 1] = total.astype(BF16)

    return pl.pallas_call(
        body, name=name, grid=(rows // tr,),
        in_specs=[pl.BlockSpec((N_DEV, tr, width), lambda i: (0, i, 0)),
                  pl.BlockSpec((4, tr, width), lambda i: (0, i, 0))],
        out_specs=[pl.BlockSpec((tr, width), lambda i: (i, 0)), pl.BlockSpec((3, tr, width), lambda i: (0, i, 0))],
        out_shape=[jax.ShapeDtypeStruct((rows, width), F32), jax.ShapeDtypeStruct((3, rows, width), BF16)],
        compiler_params=_params(("parallel",), 2 * 20 * tr * width * 2 + 8 * tr * width * 4),
    )(grad, recv)


def _adamw_math(w, g, m, v):
    m = ADAM_B1 * m + (1.0 - ADAM_B1) * g
    v = ADAM_B2 * v + (1.0 - ADAM_B2) * (g * g)
    m_hat = m / (1.0 - ADAM_B1 ** ADAM_STEP)
    v_hat = v / (1.0 - ADAM_B2 ** ADAM_STEP)
    delta = -ADAM_LR * (m_hat / (jnp.sqrt(v_hat) + ADAM_EPS) + ADAM_WD * w)
    return delta, m, v


def _adamw_shard(own, recv, w, m, v, name, tr):
    rows, width = own.shape

    def body(own_ref, r_ref, w_ref, m_ref, v_ref, g_ref, d_ref, nm_ref, nv_ref):
        g = own_ref[...]
        for j in range(3):
            g = g + r_ref[j].astype(F32)
        g_ref[...] = g
        d_ref[...], nm_ref[...], nv_ref[...] = _adamw_math(w_ref[...], g, m_ref[...], v_ref[...])

    spec = pl.BlockSpec((tr, width), lambda i: (i, 0))
    shape = jax.ShapeDtypeStruct((rows, width), F32)
    return pl.pallas_call(
        body, name=name, grid=(rows // tr,),
        in_specs=[spec, pl.BlockSpec((3, tr, width), lambda i: (0, i, 0)), spec, spec, spec],
        out_specs=[spec] * 4, out_shape=[shape] * 4,
        compiler_params=_params(("parallel",), 22 * tr * width * 4),
    )(own, recv, w, m, v)


def _small_all_reduce_adamw(gpart, lbpack, wpack, mpack, vpack, after):
    def body(gp_ref, lb_ref, w_ref, m_ref, v_ref, after_ref, g_ref, d_ref, nm_ref, nv_ref, gath_ref, send_sems, recv_sems):
        del after_ref
        x, y, c = _mesh_position()
        me = 4 * x + 2 * y + c
        gath_ref[me] = gp_ref[...]
        copies = []
        for j in range(1, N_DEV):
            peer = (x ^ (j >> 2), y ^ ((j >> 1) & 1), c ^ (j & 1))
            cp = pltpu.make_async_remote_copy(
                src_ref=gp_ref, dst_ref=gath_ref.at[me], send_sem=send_sems.at[j - 1], recv_sem=recv_sems.at[j - 1],
                device_id=peer, device_id_type=MESH)
            cp.start()
            copies.append(cp)
        for cp in copies:
            cp.wait()
        tot = gath_ref[0]
        for s in range(1, N_DEV):
            tot = tot + gath_ref[s]
        lb = lb_ref[...]
        dl = tot[16:32, :] * lb * (1.0 - lb)
        g = jnp.concatenate([tot[0:16, :], dl[0:8, :], -dl[0:8, :], dl[8:16, :], -dl[8:16, :],
                             tot[32:SMALL_GRAD_ROWS, :]], axis=0)
        g_ref[...] = g
        d_ref[...], nm_ref[...], nv_ref[...] = _adamw_math(w_ref[...], g, m_ref[...], v_ref[...])

    vm = pl.BlockSpec(memory_space=pltpu.VMEM)
    shape = jax.ShapeDtypeStruct((SMALL_ROWS, LANE), F32)
    return pl.pallas_call(
        body, name="small_all_reduce_adamw", in_specs=[vm] * 5 + [_any_spec()], out_specs=[vm] * 4,
        out_shape=[shape] * 4,
        scratch_shapes=[pltpu.VMEM((N_DEV, SMALL_GRAD_ROWS, LANE), F32),
                        pltpu.SemaphoreType.DMA((N_DEV - 1,)), pltpu.SemaphoreType.DMA((N_DEV - 1,))],
    )(gpart, lbpack, wpack, mpack, vpack, after)


_SMALL_NAMES = ("norm_mix_gain", "norm_mem_gain", "lb_logits_fw", "lb_logits_bw", "norm_ffn_gain",
                "hg_norm_gain", "da_q_gain", "da_k_gain", "mem_q_gain", "mem_k_gain")
_SMALL_ROW0 = {"norm_mix_gain": 0, "norm_mem_gain": 8, "lb_logits_fw": 16, "lb_logits_bw": 32, "norm_ffn_gain": 48,
               "hg_norm_gain": 56, "da_q_gain": 64, "da_k_gain": 72, "mem_q_gain": 80, "mem_k_gain": 88}
_LOSS_ROW = 96


def _pack_rows(parts, total_rows):
    rows = []
    for p in parts:
        r = p.reshape(-1, LANE)
        rows.append(jnp.pad(r, ((0, -r.shape[0] % 8), (0, 0))))
    used = sum(r.shape[0] for r in rows)
    if total_rows > used:
        rows.append(jnp.zeros((total_rows - used, LANE), F32))
    return jnp.concatenate(rows, axis=0)


def _unpack_small(pack, like):
    out = {}
    for name in _SMALL_NAMES:
        n = like[name].size // LANE
        r0 = _SMALL_ROW0[name]
        out[name] = pack[r0:r0 + n].reshape(like[name].shape)
    return out


def kernel(x, mem, norm_mix_gain, norm_mem_gain, w_in, lb_logits_fw, lb_logits_bw, hg_norm_gain, da_q_gain, da_k_gain, w_mem_kv, mem_q_gain, mem_k_gain, w_proj_hg, w_proj_da, w_proj_mem, w_out, norm_ffn_gain, w_ffn_in, w_ffn_out, loss_target, m_norm_mix_gain, m_norm_mem_gain, m_w_in, m_lb_logits_fw, m_lb_logits_bw, m_hg_norm_gain, m_da_q_gain, m_da_k_gain, m_w_mem_kv, m_mem_q_gain, m_mem_k_gain, m_w_proj_hg, m_w_proj_da, m_w_proj_mem, m_w_out, m_norm_ffn_gain, m_w_ffn_in, m_w_ffn_out, v_norm_mix_gain, v_norm_mem_gain, v_w_in, v_lb_logits_fw, v_lb_logits_bw, v_hg_norm_gain, v_da_q_gain, v_da_k_gain, v_w_mem_kv, v_mem_q_gain, v_mem_k_gain, v_w_proj_hg, v_w_proj_da, v_w_proj_mem, v_w_out, v_norm_ffn_gain, v_w_ffn_in, v_w_ffn_out):
    small_w = dict(norm_mix_gain=norm_mix_gain, norm_mem_gain=norm_mem_gain, lb_logits_fw=lb_logits_fw,
                   lb_logits_bw=lb_logits_bw, norm_ffn_gain=norm_ffn_gain, hg_norm_gain=hg_norm_gain,
                   da_q_gain=da_q_gain, da_k_gain=da_k_gain, mem_q_gain=mem_q_gain, mem_k_gain=mem_k_gain)
    small_m = dict(norm_mix_gain=m_norm_mix_gain, norm_mem_gain=m_norm_mem_gain, lb_logits_fw=m_lb_logits_fw,
                   lb_logits_bw=m_lb_logits_bw, norm_ffn_gain=m_norm_ffn_gain, hg_norm_gain=m_hg_norm_gain,
                   da_q_gain=m_da_q_gain, da_k_gain=m_da_k_gain, mem_q_gain=m_mem_q_gain, mem_k_gain=m_mem_k_gain)
    small_v = dict(norm_mix_gain=v_norm_mix_gain, norm_mem_gain=v_norm_mem_gain, lb_logits_fw=v_lb_logits_fw,
                   lb_logits_bw=v_lb_logits_bw, norm_ffn_gain=v_norm_ffn_gain, hg_norm_gain=v_hg_norm_gain,
                   da_q_gain=v_da_q_gain, da_k_gain=v_da_k_gain, mem_q_gain=v_mem_q_gain, mem_k_gain=v_mem_k_gain)
    big_names = ("w_in", "w_mem_kv", "w_proj_hg", "w_proj_da", "w_proj_mem", "w_out", "w_ffn_in", "w_ffn_out")
    big_w = dict(w_in=w_in[0], w_mem_kv=w_mem_kv[0], w_proj_hg=w_proj_hg[0], w_proj_da=w_proj_da[0],
                 w_proj_mem=w_proj_mem[0], w_out=w_out[0], w_ffn_in=w_ffn_in[0], w_ffn_out=w_ffn_out[0])
    big_m = dict(w_in=m_w_in[0], w_mem_kv=m_w_mem_kv[0], w_proj_hg=m_w_proj_hg[0], w_proj_da=m_w_proj_da[0],
                 w_proj_mem=m_w_proj_mem[0], w_out=m_w_out[0], w_ffn_in=m_w_ffn_in[0], w_ffn_out=m_w_ffn_out[0])
    big_v = dict(w_in=v_w_in[0], w_mem_kv=v_w_mem_kv[0], w_proj_hg=v_w_proj_hg[0], w_proj_da=v_w_proj_da[0],
                 w_proj_mem=v_w_proj_mem[0], w_out=v_w_out[0], w_ffn_in=v_w_ffn_in[0], w_ffn_out=v_w_ffn_out[0])

    row_tile = dict(w_in=128, w_mem_kv=128, w_proj_hg=128, w_proj_da=512, w_proj_mem=512, w_out=128,
                    w_ffn_in=128, w_ffn_out=352)
    rest = _BIG_NAMES[1:]
    shards = [big_w[n].astype(BF16) for n in rest]
    state = {}
    big_out = {}

    def reduce_start(group, stacked):
        names = tuple(stacked)
        arrays = [stacked[n] for n in names] + [lax.empty((4,) + stacked[n].shape[1:], BF16) for n in names]
        plan = _reduce_plan_sibling(len(names))
        send, recv, thru, token = _copies_start(f"grads_{group}_sibling_start", arrays, 4 * len(names), plan)
        state[group] = dict(names=names, plan=plan, send=send, recv=recv, arrays=thru)
        return token

    def reduce_middle(group, after):
        st = state[group]
        names, k = st["names"], len(st["names"])
        thru = _copies_wait(f"grads_{group}_sibling_wait", st["send"], st["recv"], st["arrays"], 4 * k, st["plan"], after)
        partials = [_chip_partials(thru[i], thru[k + i], f"chip_partials_{n}", row_tile[n]) for i, n in enumerate(names)]
        arrays = [p[1] for p in partials] + [lax.empty(p[1].shape, BF16) for p in partials]
        plan = _reduce_plan_chips(k)
        send, recv, thru2, token = _copies_start(f"grads_{group}_chips_start", arrays, 3 * k, plan)
        state[group] = dict(names=names, plan=plan, send=send, recv=recv, arrays=thru2, own=[p[0] for p in partials])
        return token

    def reduce_finish(group, after):
        st = state.pop(group)
        names, k = st["names"], len(st["names"])
        thru = _copies_wait(f"grads_{group}_chips_wait", st["send"], st["recv"], st["arrays"], 3 * k, st["plan"], after)
        for i, n in enumerate(names):
            big_out[n] = _adamw_shard(st["own"][i], thru[k + i], big_w[n], big_m[n], big_v[n], "adamw_" + n, row_tile[n])

    win_st = _all_gather([big_w["w_in"].astype(BF16)])[0]
    step = _local_step_stages(x[0], mem[0], loss_target[0], small_w, win_st)
    event, payload = next(step)
    local = None
    while True:
        reply = None
        if event == "begin":
            nr = len(rest)
            me = 4 * lax.axis_index("x") + 2 * lax.axis_index("y") + lax.axis_index("c")
            arrays = shards + [lax.dynamic_update_slice(lax.empty((N_DEV,) + s.shape, BF16), s[None], (me, 0, 0))
                               for s in shards]
            plan = _gather_plan_out(nr)
            send, recv, thru, reply = _copies_start("weights_rest_out_start", arrays, 4 * nr, plan)
            state["gather"] = dict(plan=plan, send=send, recv=recv, arrays=thru)
        elif event == "after_gla_fwd":
            st = state["gather"]
            nr = len(rest)
            thru = _copies_wait("weights_rest_out_wait", st["send"], st["recv"], st["arrays"], 4 * nr, st["plan"], payload)
            plan = _gather_plan_pass(nr)
            send, recv, lands, reply = _copies_start("weights_rest_pass_start", thru[nr:], 3 * nr, plan)
            state["gather"] = dict(plan=plan, send=send, recv=recv, arrays=lands)
        elif event == "need_weights":
            st = state.pop("gather")
            nr = len(rest)
            lands = _copies_wait("weights_rest_pass_wait", st["send"], st["recv"], st["arrays"], 3 * nr, st["plan"], payload)
            reply = dict(zip(rest, lands))
        elif event == "grads_ffn":
            reply = reduce_start("ffn", payload)
        elif event == "after_gate_merge_bwd":
            reply = reduce_middle("ffn", payload)
        elif event == "grads_mix":
            reply = reduce_start("mix", payload)
        elif event == "after_da_bwd_g0":
            reply = reduce_middle("mix", payload)
            reduce_finish("ffn", payload)
        elif event == "after_gla_bwd":
            reduce_finish("mix", payload)
        elif event == "grads_in":
            reply = reduce_start("in", payload)
        elif event == "after_proj_bwd_act_top":
            reply = reduce_middle("in", payload)
        elif event == "end":
            local = payload
            reduce_finish("in", [local["grad_x"]] + [big_out[n][0] for n in rest])
            break
        event, payload = step.send(reply)
    grad_x = local["grad_x"]

    wpack = _pack_rows([small_w[n] for n in _SMALL_NAMES], SMALL_ROWS)
    mpack = _pack_rows([small_m[n] for n in _SMALL_NAMES], SMALL_ROWS)
    vpack = _pack_rows([small_v[n] for n in _SMALL_NAMES], SMALL_ROWS)
    gs, ds, ms, vs = _small_all_reduce_adamw(local["gpart"], local["lbpack"], wpack, mpack, vpack, big_out["w_in"][0])
    loss = gs[_LOSS_ROW, 0]
    small_out = [_unpack_small(t, small_w) for t in (gs, ds, ms, vs)]

    order = ("norm_mix_gain", "norm_mem_gain", "w_in", "lb_logits_fw", "lb_logits_bw", "hg_norm_gain", "da_q_gain",
             "da_k_gain", "w_mem_kv", "mem_q_gain", "mem_k_gain", "w_proj_hg", "w_proj_da", "w_proj_mem", "w_out",
             "norm_ffn_gain", "w_ffn_in", "w_ffn_out")
    outs = [loss, grad_x[None]]
    for kind in range(4):
        for n in order:
            outs.append(big_out[n][kind][None] if n in big_out else small_out[kind][n])
    return tuple(outs)


_BIG_NAMES = ("w_in", "w_mem_kv", "w_proj_hg", "w_proj_da", "w_proj_mem", "w_out", "w_ffn_in", "w_ffn_out")


def _local_step(xs, mems, target, sw, wg):
    step = _local_step_stages(xs, mems, target, sw, wg["w_in"])
    stacked = {}
    event, payload = next(step)
    while event != "end":
        if event.startswith("grads_"):
            stacked.update(payload)
        event, payload = step.send(wg if event == "need_weights" else None)
    return dict(payload, stacked=stacked)


def _local_step_stages(xs, mems, target, sw, win_st):
    norm_mix_gain, norm_mem_gain, norm_ffn_gain = sw["norm_mix_gain"], sw["norm_mem_gain"], sw["norm_ffn_gain"]
    lb_logits_fw, lb_logits_bw, hg_norm_gain = sw["lb_logits_fw"], sw["lb_logits_bw"], sw["hg_norm_gain"]
    da_q_gain, da_k_gain, mem_q_gain, mem_k_gain = sw["da_q_gain"], sw["da_k_gain"], sw["mem_q_gain"], sw["mem_k_gain"]

    token = yield "begin", None
    lb_fw = _lb_table(lb_logits_fw, "lb_table_fw")
    lb_bw = _lb_table(lb_logits_bw, "lb_table_bw")
    lb = jnp.concatenate([lb_fw, lb_bw], axis=0).reshape(2, HG_HEADS, 1, HEAD_DIM)
    h, h_t = _rmsnorm_fwd(xs, _after(norm_mix_gain, token), "norm_mix_fwd", 512, transposed=True)
    proj = _matmul(h, win_st, "nn", F32, 1024, IN_SHARD, D_MODEL, "proj_fwd", b_stacked=True, n_outer=True)
    o_hg, o_pre, states, hg_qt, hg_decay = _gla_fwd(proj, lb, hg_norm_gain)
    token = yield "after_gla_fwd", o_hg
    da = [_da_fwd(proj, _after(da_q_gain, token), da_k_gain, g) for g in range(3)]
    o_da, o_da32, lse_da = _da_merge([t[0] for t in da], [t[1] for t in da], 512)
    wg = yield "need_weights", o_da
    wkv = wg["w_mem_kv"].reshape(D_MODEL, 2 * MEM_WIDTH)
    wphg = wg["w_proj_hg"].reshape(D_MODEL, D_MODEL)
    wpda = jnp.transpose(wg["w_proj_da"], (1, 0, 2)).reshape(DA_WIDTH, D_MODEL)
    wpmem = jnp.transpose(wg["w_proj_mem"], (1, 0, 2)).reshape(MEM_WIDTH, D_MODEL)
    wout = wg["w_out"].reshape(D_MODEL, D_MODEL)
    wfin = jnp.transpose(wg["w_ffn_in"], (1, 0, 2)).reshape(D_MODEL, 2 * D_FF)
    wfout = wg["w_ffn_out"].reshape(D_FF, D_MODEL)
    mem_n = _rmsnorm_fwd(mems, norm_mem_gain, "norm_mem_fwd", N_MEM)
    kv = _matmul(mem_n, wkv, "nn", F32, N_MEM, 1024, D_MODEL, "mem_kv_fwd")
    o_mem = _mem_fwd(proj, kv, mem_q_gain, mem_k_gain, 512)
    t_hg = _matmul(o_hg, wphg, "nn", F32, 512, 1024, D_MODEL, "proj_hg_fwd")
    t_da = _matmul(o_da, wpda, "nn", F32, 512, 1024, DA_WIDTH, "proj_da_fwd")
    t_mem = _matmul(o_mem, wpmem, "nn", F32, 512, 1024, MEM_WIDTH, "proj_mem_fwd")
    merged = _gate_merge_fwd(proj, t_hg, t_da, t_mem, 256)
    u = _matmul(merged, wout, "nn", F32, 512, 1024, D_MODEL, "out_fwd")
    x1, h2, h2_t = _residual_rmsnorm_fwd(xs, u, norm_ffn_gain, "norm_ffn_fwd", 512)
    ffn_a, ffn_b, act, act_t = _ffn_in_swiglu(h2, wfin, 1024, 1408)
    z = _matmul(act, wfout, "nn", F32, 512, 1024, D_FF, "ffn_out_fwd")
    dy, dyb, loss_part = _loss_head(x1, z, target, 512)

    dact = _matmul(dyb, wfout, "nt", F32, 512, 1408, D_MODEL, "ffn_out_bwd_act")
    g_wfout = _matmul(act_t, dyb, "nn", BF16, 1408, 1024, 2048, "ffn_out_bwd_w")
    dab = _swiglu_bwd(ffn_a, ffn_b, dact, 256)
    dh2 = _matmul(dab, wfin, "nt", F32, 1024, 1024, 2816, "ffn_in_bwd_act")
    g_wfin = _matmul(h2_t, dab, "nn", BF16, 1024, 1408, 2048, "ffn_in_bwd_w")
    token = yield "grads_ffn", dict(
        w_ffn_in=jnp.transpose(g_wfin.reshape(D_MODEL, N_DEV, 2 * D_FF // N_DEV), (1, 0, 2)),
        w_ffn_out=g_wfout.reshape(N_DEV, D_FF // N_DEV, D_MODEL))
    dx1, dx1b, g_norm_ffn = _rmsnorm_bwd(x1, dh2, dy, _after(norm_ffn_gain, token), "norm_ffn_bwd", 256)
    dmerged = _matmul(dx1b, wout, "nt", F32, 512, 1024, D_MODEL, "out_bwd_act")
    g_wout = _matmul(merged, dx1b, "tn", BF16, 1024, 1024, 2048, "out_bwd_w")
    dproj = _dproj_buffer()
    dt_hg, dproj = _gate_bwd(proj, t_hg, dmerged, dproj, 0, 256)
    dt_da, dproj = _gate_bwd(proj, t_da, dmerged, dproj, 1, 256)
    dt_mem, dproj = _gate_bwd(proj, t_mem, dmerged, dproj, 2, 256)
    token = yield "after_gate_merge_bwd", dt_hg
    do_hg = _matmul(dt_hg, wphg, "nt", F32, 512, 1024, D_MODEL, "proj_hg_bwd_act", after=token)
    g_wphg = _matmul(o_hg, dt_hg, "tn", BF16, 1024, 1024, 2048, "proj_hg_bwd_w")
    do_da = _matmul(dt_da, wpda, "nt", F32, 512, DA_WIDTH, D_MODEL, "proj_da_bwd_act")
    g_wpda = _matmul(o_da, dt_da, "tn", BF16, DA_WIDTH, D_MODEL, 2048, "proj_da_bwd_w")
    do_mem = _matmul(dt_mem, wpmem, "nt", F32, 512, MEM_WIDTH, D_MODEL, "proj_mem_bwd_act")
    g_wpmem = _matmul(o_mem, dt_mem, "tn", BF16, MEM_WIDTH, D_MODEL, 2048, "proj_mem_bwd_w")
    by_owner = lambda g: jnp.transpose(g.reshape(g.shape[0], N_DEV, D_MODEL // N_DEV), (1, 0, 2))
    g_wpda, g_wpmem = by_owner(g_wpda), by_owner(g_wpmem)

    dproj, dk_mem, dv_mem, g_mem_q, g_mem_k = _mem_bwd(proj, kv, mem_q_gain, mem_k_gain, do_mem, dproj, 512)
    dkv = jnp.concatenate([dk_mem, dv_mem], axis=1).astype(BF16)
    g_wkv = _matmul(mem_n, dkv, "tn", BF16, 1024, 1024, N_MEM, "mem_kv_bwd_w")
    dmem_n = _matmul(dkv, wkv, "nt", F32, N_MEM, 1024, 1024, "mem_kv_bwd_act")
    g_norm_mem = _gain_grad(mems, dmem_n, "norm_mem_bwd")
    token = yield "grads_mix", dict(
        w_mem_kv=g_wkv.reshape(N_DEV, D_MODEL // N_DEV, 2 * MEM_WIDTH),
        w_proj_hg=g_wphg.reshape(N_DEV, D_MODEL // N_DEV, D_MODEL),
        w_proj_da=g_wpda, w_proj_mem=g_wpmem,
        w_out=g_wout.reshape(N_DEV, D_MODEL // N_DEV, D_MODEL))

    dd_da = _da_rowdot(do_da, o_da32, 512)
    dproj, g_da_q, g_da_k = _da_bwd(proj, _after(da_q_gain, token), da_k_gain, do_da, lse_da, dd_da, dproj, 0)
    token = yield "after_da_bwd_g0", g_da_q
    for g in (1, 2):
        dproj, gq_part, gk_part = _da_bwd(proj, _after(da_q_gain, token), da_k_gain, do_da, lse_da, dd_da, dproj, g)
        g_da_q, g_da_k = g_da_q + gq_part, g_da_k + gk_part

    dproj, dlb, g_hg_norm = _gla_bwd(proj, lb, hg_norm_gain, o_pre, states, hg_qt, hg_decay, do_hg, dproj)
    yield "after_gla_bwd", dlb

    g_win =_matmul(h_t, dproj, "nn", BF16, 1024, IN_SHARD, 2048, "proj_bwd_w", out_stacked=True)
    token = yield "grads_in", dict(w_in=g_win)
    dh = _matmul(dproj, win_st, "nt", F32, 1024, 1024, IN_SHARD, "proj_bwd_act_top", b_stacked=True, after=token,
                 m_blocks=(0, 2))
    token = yield "after_proj_bwd_act_top", dh
    dh = _matmul(dproj, win_st, "nt", F32, 1024, 1024, IN_SHARD, "proj_bwd_act_bottom", b_stacked=True, after=token,
                 m_blocks=(2, 2), out_into=dh)
    grad_x, _, g_norm_mix = _rmsnorm_bwd(xs, dh, dx1, _after(norm_mix_gain, token), "norm_mix_bwd", 256)

    gpart = _pack_rows([g_norm_mix, g_norm_mem, dlb[0], dlb[1], g_norm_ffn, g_hg_norm, g_da_q, g_da_k,
                        g_mem_q, g_mem_k, loss_part], SMALL_GRAD_ROWS)
    lbpack = jnp.concatenate([lb_fw.reshape(8, LANE), lb_bw.reshape(8, LANE)], axis=0)
    yield "end", dict(grad_x=grad_x, gpart=gpart, lbpack=lbpack)
```

```python
import numpy as np
import jax
import jax.numpy as jnp
from jax import lax
from jax.experimental import pallas as pl
from jax.experimental.pallas import tpu as pltpu

F32 = jnp.float32
BF16 = jnp.bfloat16
MESH = pl.DeviceIdType.MESH

SEQ = 4096
D_MODEL = 1024
N_DEV = 8
N_MEM = 256
RMS_EPS = 1e-6
NEG_INF = -1e30
LANE = 128
HEAD_DIM = 128
HG_HEADS = 8
HG_CHUNK = 64
HG_SCALE = HEAD_DIM ** -0.5
DA_DILATIONS = (1, 4, 16)
DA_RADIUS = 64
DA_HEADS_PER_GROUP = 4
DA_HEADS = 12
DA_WIDTH = 512
DA_SCALE = HEAD_DIM ** -0.5
DA_QB = 128
DA_WIN = 256
DA_WAYS = 4
DA_FWD_WAYS = 4
MEM_HEADS = 4
MEM_WIDTH = 512
MEM_SCALE = HEAD_DIM ** -0.5
D_FF = 2816
IN_COLS = 13312
IN_SHARD = IN_COLS // N_DEV
CB_HG_Q, CB_F, CB_HG_I, CB_HG_G = 0, 8, 24, 32
CB_DA_Q, CB_DA_K, CB_DA_V, CB_MEM_Q = 40, 52, 64, 76
N_CB = IN_COLS // LANE
ADAM_LR, ADAM_B1, ADAM_B2, ADAM_EPS, ADAM_WD, ADAM_STEP = 0.001, 0.9, 0.999, 1e-08, 0.01, 10
VMEM_BYTES_V7X = 64 * 1024 * 1024
SMALL_ROWS = 104
SMALL_GRAD_ROWS = 88

_NN = (((1,), (0,)), ((), ()))
_NT = (((1,), (1,)), ((), ()))
_TN = (((0,), (0,)), ((), ()))


def _dot(a, b, dims):
    return lax.dot_general(a.astype(BF16), b.astype(BF16), dims, preferred_element_type=F32)


def _dot_exact(a, b, dims):
    return lax.dot_general(a, b, dims, precision=lax.Precision.HIGHEST, preferred_element_type=F32)


def _sigmoid(x):
    return 0.5 * jnp.tanh(0.5 * x) + 0.5


def _params(semantics, est_bytes):
    limit = int(min(VMEM_BYTES_V7X - (6 << 20), max(56 << 20, est_bytes * 3 // 2)))
    return pltpu.CompilerParams(dimension_semantics=semantics, vmem_limit_bytes=limit)


def _nbytes(shape, dtype):
    return int(np.prod(shape)) * jnp.dtype(dtype).itemsize


def _alibi_slopes(n):
    return (2.0 ** (-8.0 * np.arange(1, n + 1) / n)).astype(np.float32)


def _matmul(a, b, mode, out_dtype, tm, tn, tk, name, b_stacked=False, out_stacked=False, n_outer=False, after=None,
            m_blocks=None, out_into=None):
    if mode == "tn":
        kdim, m = a.shape
    else:
        m, kdim = a.shape
    if b_stacked:
        if mode == "nn":
            n = b.shape[0] * b.shape[2]
            assert tn == b.shape[2] and tk == kdim == b.shape[1]
        else:
            assert mode == "nt" and tk == b.shape[2] and b.shape[0] * tk == kdim
            n = b.shape[1]
    else:
        n = b.shape[0] if mode == "nt" else b.shape[1]
    assert m % tm == 0 and n % tn == 0 and kdim % tk == 0
    gm, gn, gk = m // tm, n // tn, kdim // tk
    i0 = 0
    if m_blocks is not None:
        assert mode != "tn" and not out_stacked
        i0, gm = m_blocks

    def ijk(f):
        if n_outer:
            return lambda j, i, k: f(i + i0, j, k)
        return lambda i, j, k: f(i + i0, j, k)

    if mode == "tn":
        a_spec = pl.BlockSpec((tk, tm), ijk(lambda i, j, k: (k, i)))
    else:
        a_spec = pl.BlockSpec((tm, tk), ijk(lambda i, j, k: (i, k)))
    if b_stacked and mode == "nn":
        b_spec = pl.BlockSpec((None, tk, tn), ijk(lambda i, j, k: (j, 0, 0)))
    elif b_stacked:
        b_spec = pl.BlockSpec((None, tn, tk), ijk(lambda i, j, k: (k, j, 0)))
    elif mode == "nt":
        b_spec = pl.BlockSpec((tn, tk), ijk(lambda i, j, k: (j, k)))
    else:
        b_spec = pl.BlockSpec((tk, tn), ijk(lambda i, j, k: (k, j)))
    if out_stacked:
        assert tm == m
        out_shape = jax.ShapeDtypeStruct((gn, m, tn), out_dtype)
        o_spec = pl.BlockSpec((None, tm, tn), ijk(lambda i, j, k: (j, i, 0)))
    else:
        out_shape = jax.ShapeDtypeStruct((m, n), out_dtype)
        o_spec = pl.BlockSpec((tm, tn), ijk(lambda i, j, k: (i, j)))
    dims = {"nn": _NN, "nt": _NT, "tn": _TN}[mode]

    n_in = 2 + (after is not None) + (out_into is not None)

    def body(*refs):
        a_ref, b_ref, o_ref = refs[0], refs[1], refs[n_in]
        part = _dot(a_ref[...], b_ref[...], dims)
        if gk == 1:
            o_ref[...] = part.astype(out_dtype)
            return
        acc_ref = refs[-1]
        k = pl.program_id(2)

        @pl.when(k == 0)
        def _():
            acc_ref[...] = part

        @pl.when(jnp.logical_and(k > 0, k < gk - 1))
        def _():
            acc_ref[...] += part

        @pl.when(k == gk - 1)
        def _():
            o_ref[...] = (acc_ref[...] + part).astype(out_dtype)

    a_tile = _nbytes((tm, tk), a.dtype)
    b_tile = _nbytes((tk, tn), b.dtype)
    o_tile = _nbytes((tm, tn), out_dtype)
    est = 2 * (a_tile + b_tile + o_tile) + 3 * tm * tn * 4 + (a_tile + b_tile)
    grid = (gn, gm, gk) if n_outer else (gm, gn, gk)
    operands, in_specs = [a, b], [a_spec, b_spec]
    if after is not None:
        operands.append(after)
        in_specs.append(pl.BlockSpec(memory_space=pl.ANY))
    aliases = {}
    if out_into is not None:
        aliases = {len(operands): 0}
        operands.append(out_into)
        in_specs.append(pl.BlockSpec(memory_space=pl.ANY))
    return pl.pallas_call(
        body, name=name, grid=grid, in_specs=in_specs, out_specs=o_spec, out_shape=out_shape,
        scratch_shapes=[] if gk == 1 else [pltpu.VMEM((tm, tn), F32)], input_output_aliases=aliases,
        compiler_params=_params(("parallel", "parallel", "arbitrary"), est),
    )(*operands)


def _row_spec(tr, width, col_block=0):
    return pl.BlockSpec((tr, width), lambda i: (i, col_block))


def _bcast_spec(width):
    return pl.BlockSpec((1, width), lambda i: (0, 0))


def _col_spec(width, tr):
    return pl.BlockSpec((width, tr), lambda i: (0, i))


def _rmsnorm_fwd(x, gain, name, tr, transposed=False):
    rows, width = x.shape

    def body(x_ref, g_ref, o_ref, *t_ref):
        xv = x_ref[...]
        r = lax.rsqrt(jnp.mean(xv * xv, axis=-1, keepdims=True) + RMS_EPS)
        h = xv * r * g_ref[...]
        o_ref[...] = h.astype(BF16)
        if transposed:
            t_ref[0][...] = h.T.astype(BF16)

    out_specs, out_shape = [_row_spec(tr, width)], [jax.ShapeDtypeStruct((rows, width), BF16)]
    if transposed:
        out_specs.append(_col_spec(width, tr))
        out_shape.append(jax.ShapeDtypeStruct((width, rows), BF16))
    out = pl.pallas_call(
        body, name=name, grid=(rows // tr,), in_specs=[_row_spec(tr, width), _bcast_spec(width)],
        out_specs=out_specs, out_shape=out_shape,
        compiler_params=_params(("parallel",), 10 * tr * width * 4),
    )(x, gain)
    return out if transposed else out[0]


def _residual_rmsnorm_fwd(x, u, gain, name, tr):
    rows, width = x.shape

    def body(x_ref, u_ref, g_ref, x1_ref, h_ref, ht_ref):
        xv = x_ref[...] + u_ref[...]
        x1_ref[...] = xv
        r = lax.rsqrt(jnp.mean(xv * xv, axis=-1, keepdims=True) + RMS_EPS)
        h = xv * r * g_ref[...]
        h_ref[...] = h.astype(BF16)
        ht_ref[...] = h.T.astype(BF16)

    return pl.pallas_call(
        body, name=name, grid=(rows // tr,),
        in_specs=[_row_spec(tr, width), _row_spec(tr, width), _bcast_spec(width)],
        out_specs=[_row_spec(tr, width), _row_spec(tr, width), _col_spec(width, tr)],
        out_shape=[jax.ShapeDtypeStruct((rows, width), F32), jax.ShapeDtypeStruct((rows, width), BF16),
                   jax.ShapeDtypeStruct((width, rows), BF16)],
        compiler_params=_params(("parallel",), 14 * tr * width * 4),
    )(x, u, gain)


def _rmsnorm_bwd(x, dh, dres, gain, name, tr):
    rows, width = x.shape

    def body(x_ref, dh_ref, dres_ref, g_ref, dx_ref, dxb_ref, dg_ref):
        xv = x_ref[...]
        r = lax.rsqrt(jnp.mean(xv * xv, axis=-1, keepdims=True) + RMS_EPS)
        xhat = xv * r
        dhv = dh_ref[...]
        dyg = dhv * g_ref[...]
        dx = dres_ref[...] + r * (dyg - xhat * jnp.mean(dyg * xhat, axis=-1, keepdims=True))
        dx_ref[...] = dx
        dxb_ref[...] = dx.astype(BF16)
        part = jnp.sum(dhv * xhat, axis=0, keepdims=True)

        @pl.when(pl.program_id(0) == 0)
        def _():
            dg_ref[...] = part

        @pl.when(pl.program_id(0) > 0)
        def _():
            dg_ref[...] += part

    return pl.pallas_call(
        body, name=name, grid=(rows // tr,),
        in_specs=[_row_spec(tr, width), _row_spec(tr, width), _row_spec(tr, width), _bcast_spec(width)],
        out_specs=[_row_spec(tr, width), _row_spec(tr, width), _bcast_spec(width)],
        out_shape=[jax.ShapeDtypeStruct((rows, width), F32), jax.ShapeDtypeStruct((rows, width), BF16),
                   jax.ShapeDtypeStruct((1, width), F32)],
        compiler_params=_params(("arbitrary",), 14 * tr * width * 4),
    )(x, dh, dres, gain)


def _gain_grad(x, dh, name):
    rows, width = x.shape

    def body(x_ref, dh_ref, dg_ref):
        xv = x_ref[...]
        r = lax.rsqrt(jnp.mean(xv * xv, axis=-1, keepdims=True) + RMS_EPS)
        dg_ref[...] = jnp.sum(dh_ref[...] * xv * r, axis=0, keepdims=True)

    return pl.pallas_call(
        body, name=name, grid=(1,), in_specs=[_row_spec(rows, width), _row_spec(rows, width)],
        out_specs=_bcast_spec(width), out_shape=jax.ShapeDtypeStruct((1, width), F32),
        compiler_params=_params(("arbitrary",), 6 * rows * width * 4),
    )(x, dh)


def _lb_table(logits, name):
    slots, width = logits.shape

    def body(l_ref, o_ref):
        lv = l_ref[...]
        mx = jnp.max(lv, axis=0, keepdims=True)
        e = jnp.exp(lv - mx)
        o_ref[...] = e[0:1, :] / jnp.sum(e, axis=0, keepdims=True)

    return pl.pallas_call(
        body, name=name, grid=(1,), in_specs=[pl.BlockSpec((slots, width), lambda i: (0, 0))],
        out_specs=_bcast_spec(width), out_shape=jax.ShapeDtypeStruct((1, width), F32),
    )(logits)


def _gate_merge_fwd(proj, t_hg, t_da, t_mem, tr):
    w = D_MODEL

    def body(ghg_ref, gda_ref, gmem_ref, thg_ref, tda_ref, tmem_ref, o_ref):
        acc = _sigmoid(ghg_ref[...]) * thg_ref[...]
        acc += _sigmoid(gda_ref[...]) * tda_ref[...]
        acc += _sigmoid(gmem_ref[...]) * tmem_ref[...]
        o_ref[...] = acc.astype(BF16)

    return pl.pallas_call(
        body, name="gate_merge_fwd", grid=(SEQ // tr,),
        in_specs=[_row_spec(tr, w, 10), _row_spec(tr, w, 11), _row_spec(tr, w, 12),
                  _row_spec(tr, w), _row_spec(tr, w), _row_spec(tr, w)],
        out_specs=_row_spec(tr, w), out_shape=jax.ShapeDtypeStruct((SEQ, w), BF16),
        compiler_params=_params(("parallel",), 16 * tr * w * 4),
    )(proj, proj, proj, t_hg, t_da, t_mem)


def _dproj_buffer():
    return lax.empty((SEQ, IN_COLS), BF16)


def _gate_merge_bwd(proj, t_hg, t_da, t_mem, dmerged, dproj, tr):
    w = D_MODEL
    steps = SEQ // tr
    gate_col0 = 10 * w

    def body(ghg_ref, gda_ref, gmem_ref, thg_ref, tda_ref, tmem_ref, dm_ref, dproj_in,
             dthg_ref, dtda_ref, dtmem_ref, dproj_ref, stage, sems):
        del dproj_in
        i = pl.program_id(0)
        slot = i % 2

        def slot_copy(s):
            rows = pl.ds(pl.multiple_of(i * tr, tr), tr)
            return pltpu.make_async_copy(stage.at[s], dproj_ref.at[rows, pl.ds(gate_col0, 3 * w)], sems.at[s])

        @pl.when(i >= 2)
        def _():
            slot_copy(slot).wait()

        dm = dm_ref[...]
        for b, (g_ref, t_ref, dt_ref) in enumerate(((ghg_ref, thg_ref, dthg_ref), (gda_ref, tda_ref, dtda_ref),
                                                    (gmem_ref, tmem_ref, dtmem_ref))):
            s = _sigmoid(g_ref[...])
            dt_ref[...] = (s * dm).astype(BF16)
            stage[slot, :, b * w:(b + 1) * w] = (dm * t_ref[...] * s * (1.0 - s)).astype(BF16)
        slot_copy(slot).start()

        @pl.when(i == steps - 1)
        def _():
            slot_copy(1 - slot).wait()
            slot_copy(slot).wait()

    assert steps >= 2
    return pl.pallas_call(
        body, name="gate_merge_bwd", grid=(steps,),
        in_specs=[_row_spec(tr, w, 10), _row_spec(tr, w, 11), _row_spec(tr, w, 12),
                  _row_spec(tr, w), _row_spec(tr, w), _row_spec(tr, w), _row_spec(tr, w), _any_spec()],
        out_specs=[_row_spec(tr, w), _row_spec(tr, w), _row_spec(tr, w), _any_spec()],
        out_shape=[jax.ShapeDtypeStruct((SEQ, w), BF16)] * 3 + [jax.ShapeDtypeStruct((SEQ, IN_COLS), BF16)],
        scratch_shapes=[pltpu.VMEM((2, tr, 3 * w), BF16), pltpu.SemaphoreType.DMA((2,))],
        input_output_aliases={7: 3},
        compiler_params=_params(("arbitrary",), 26 * tr * w * 4),
    )(proj, proj, proj, t_hg, t_da, t_mem, dmerged, dproj)


def _ffn_in_swiglu(h2, wfin, tm, tn):
    gm, gn = SEQ // tm, D_FF // tn

    def body(h_ref, wa_ref, wb_ref, a_ref, b_ref, act_ref, actt_ref):
        h = h_ref[...]
        a = _dot(h, wa_ref[...], _NN)
        b = _dot(h, wb_ref[...], _NN)
        act = a * _sigmoid(a) * b
        a_ref[...] = a.astype(BF16)
        b_ref[...] = b.astype(BF16)
        act_ref[...] = act.astype(BF16)
        actt_ref[...] = act.T.astype(BF16)

    tile = pl.BlockSpec((tm, tn), lambda j, i: (i, j))
    shape = jax.ShapeDtypeStruct((SEQ, D_FF), BF16)
    return pl.pallas_call(
        body, name="ffn_in_swiglu_fwd", grid=(gn, gm),
        in_specs=[pl.BlockSpec((tm, D_MODEL), lambda j, i: (i, 0)),
                  pl.BlockSpec((D_MODEL, tn), lambda j, i: (0, j)),
                  pl.BlockSpec((D_MODEL, tn), lambda j, i: (0, gn + j))],
        out_specs=[tile, tile, tile, pl.BlockSpec((tn, tm), lambda j, i: (j, i))],
        out_shape=[shape, shape, shape, jax.ShapeDtypeStruct((D_FF, SEQ), BF16)],
        compiler_params=_params(("parallel", "parallel"), 4 * tm * D_MODEL + 8 * D_MODEL * tn + 16 * tm * tn
                                + 6 * tm * tn * 4),
    )(h2, wfin, wfin)


def _swiglu_bwd(a, b, dact, tr):
    def body(a_ref, b_ref, d_ref, o_ref):
        av = a_ref[...].astype(F32)
        bv = b_ref[...].astype(F32)
        d = d_ref[...]
        s = _sigmoid(av)
        silu = av * s
        o_ref[:, :D_FF] = (d * bv * (s + silu * (1.0 - s))).astype(BF16)
        o_ref[:, D_FF:] = (d * silu).astype(BF16)

    return pl.pallas_call(
        body, name="swiglu_bwd", grid=(SEQ // tr,), in_specs=[_row_spec(tr, D_FF)] * 3,
        out_specs=_row_spec(tr, 2 * D_FF), out_shape=jax.ShapeDtypeStruct((SEQ, 2 * D_FF), BF16),
        compiler_params=_params(("parallel",), 10 * tr * 2 * D_FF * 4),
    )(a, b, dact)


def _loss_head(x1, z, target, tr):
    w = D_MODEL

    def body(x_ref, z_ref, t_ref, dy_ref, dyb_ref, loss_ref, acc_ref):
        err = x_ref[...] + z_ref[...] - t_ref[...]
        dy = err * (1.0 / w)
        dy_ref[...] = dy
        dyb_ref[...] = dy.astype(BF16)
        part = jnp.sum(err * err, axis=0, keepdims=True)

        @pl.when(pl.program_id(0) == 0)
        def _():
            acc_ref[...] = part

        @pl.when(pl.program_id(0) > 0)
        def _():
            acc_ref[...] += part

        @pl.when(pl.program_id(0) == SEQ // tr - 1)
        def _():
            total = jnp.sum(acc_ref[...], axis=1, keepdims=True) * (0.5 / w)
            loss_ref[...] = jnp.broadcast_to(total, (1, LANE))

    return pl.pallas_call(
        body, name="loss_head", grid=(SEQ // tr,),
        in_specs=[_row_spec(tr, w), _row_spec(tr, w), _row_spec(tr, w)],
        out_specs=[_row_spec(tr, w), _row_spec(tr, w), _bcast_spec(LANE)],
        out_shape=[jax.ShapeDtypeStruct((SEQ, w), F32), jax.ShapeDtypeStruct((SEQ, w), BF16),
                   jax.ShapeDtypeStruct((1, LANE), F32)],
        scratch_shapes=[pltpu.VMEM((1, w), F32)],
        compiler_params=_params(("arbitrary",), 12 * tr * w * 4),
    )(x1, z, target)


GLA_ROWS = 256
GLA_CPB = GLA_ROWS // HG_CHUNK
GLA_NBLK = SEQ // GLA_ROWS
GLA_WAYS = 4
GLA_TRIPS = GLA_NBLK // GLA_WAYS
GLA_GRAD_WAYS = 4
GLA_GRAD_TRIPS = GLA_NBLK // GLA_GRAD_WAYS
GLA_NCK = SEQ // HG_CHUNK


def _dot_split(m, xv, dims):
    hi = xv.astype(BF16)
    r1 = xv - hi.astype(F32)
    mid = r1.astype(BF16)
    lo = (r1 - mid.astype(F32)).astype(BF16)
    dot = lambda t: lax.dot_general(m, t, dims, preferred_element_type=F32)
    return (dot(lo) + dot(mid)) + dot(hi)


def _gla_block(qc, fc, lbv, masks):
    mask, maskb, direction = masks
    sq = _sigmoid(qc)
    q = qc * sq * HG_SCALE
    sf = _sigmoid(fc)
    forget = lbv + (1.0 - lbv) * sf
    k = 1.0 - forget
    logf = jnp.log(forget)
    b = _dot_split(maskb, logf, _NN)
    ends = []
    for j in range(GLA_CPB):
        lo, hi = j * HG_CHUNK, (j + 1) * HG_CHUNK
        end = jnp.where(direction == 0, b[hi - 1:hi, :], b[lo:lo + 1, :])
        ends.append(jnp.broadcast_to(end, (HG_CHUNK, HEAD_DIM)))
    bt = jnp.concatenate(ends, axis=0)
    eb = jnp.exp(b)
    qt = q * eb
    kt = k * jnp.exp(-b)
    kh = k * jnp.exp(bt - b)
    a = jnp.where(mask, _dot(qt, kt, _NT), 0.0)
    return dict(sq=sq, sf=sf, forget=forget, k=k, b=b, bt=bt, eb=eb, qt=qt, kt=kt, kh=kh, a=a)


def _gla_masks(direction):
    row = lax.broadcasted_iota(jnp.int32, (GLA_ROWS, GLA_ROWS), 0)
    col = lax.broadcasted_iota(jnp.int32, (GLA_ROWS, GLA_ROWS), 1)
    same = (row // HG_CHUNK) == (col // HG_CHUNK)
    mask = jnp.logical_and(same, jnp.where(direction == 0, row - col, col - row) >= 0)
    return mask, jnp.where(mask, 1.0, 0.0).astype(BF16), direction


def _gla_chunk_rows(j):
    return slice(j * HG_CHUNK, (j + 1) * HG_CHUNK)


def _gla_block_rows(b):
    return pl.ds(pl.multiple_of(b * GLA_ROWS, GLA_ROWS), GLA_ROWS)


def _head_spec(col_block0):
    return pl.BlockSpec((SEQ, HEAD_DIM), lambda h, d: (0, col_block0 + h))


def _gla_fwd(proj, lb, gain):
    nck = GLA_NCK

    def body(q_ref, f_ref, v_ref, g_ref, lb_ref, gain_ref, ohg_ref, opre_ref, st_ref, qt_scr, dec_scr, cs_scr):
        d = pl.program_id(1)
        masks = _gla_masks(d)
        lbv = lb_ref[...]

        @pl.when(d == 0)
        def _():
            opre_ref[...] = jnp.zeros_like(opre_ref)

        def intra(s, carry):
            blocks = [s + w * GLA_TRIPS for w in range(GLA_WAYS)]
            rows = [_gla_block_rows(b) for b in blocks]
            loaded = [(q_ref[r, :], f_ref[r, :], v_ref[r, :], opre_ref[r, :]) for r in rows]
            results = []
            for qc, fc, v, o_prev in loaded:
                ck = _gla_block(qc, fc, lbv, masks)
                o = o_prev + _dot(ck["a"], v, _NN)
                cs = [_dot(v[_gla_chunk_rows(j), :], ck["kh"][_gla_chunk_rows(j), :], _TN) for j in range(GLA_CPB)]
                dec = [jnp.exp(ck["bt"][j * HG_CHUNK:j * HG_CHUNK + 8, :]) for j in range(GLA_CPB)]
                results.append((o, ck["qt"].astype(BF16), cs, dec))
            for b, r, (o, qt, cs, dec) in zip(blocks, rows, results):
                opre_ref[r, :] = o
                qt_scr[r, :] = qt
                for j in range(GLA_CPB):
                    cs_scr[b * GLA_CPB + j] = cs[j]
                    dec_scr[b * GLA_CPB + j] = dec[j]
            return carry

        lax.fori_loop(0, GLA_TRIPS, intra, 0)

        def scan(i, st):
            c = jnp.where(d == 0, i, nck - 1 - i)
            st_ref[c] = st
            return st * dec_scr[c][0:1, :] + cs_scr[c]

        lax.fori_loop(0, nck, scan, jnp.zeros((HEAD_DIM, HEAD_DIM), F32), unroll=4)

        def inter(c, carry):
            rows = pl.ds(pl.multiple_of(c * HG_CHUNK, HG_CHUNK), HG_CHUNK)
            opre_ref[rows, :] += _dot(qt_scr[rows, :], st_ref[c], _NT)
            return carry

        lax.fori_loop(0, nck, inter, 0, unroll=4)

        @pl.when(d == 1)
        def _():
            o = opre_ref[...]
            r = lax.rsqrt(jnp.mean(o * o, axis=-1, keepdims=True) + RMS_EPS)
            g = g_ref[...]
            ohg_ref[...] = (o * r * gain_ref[...] * (g * _sigmoid(g))).astype(BF16)

    blk = SEQ * HEAD_DIM * 4
    return pl.pallas_call(
        body, name="gla_fwd", grid=(HG_HEADS, 2),
        in_specs=[_head_spec(CB_HG_Q),
                  pl.BlockSpec((SEQ, HEAD_DIM), lambda h, d: (0, CB_F + 8 * d + h)),
                  _head_spec(CB_HG_I), _head_spec(CB_HG_G),
                  pl.BlockSpec((None, None, 1, HEAD_DIM), lambda h, d: (d, h, 0, 0)),
                  pl.BlockSpec((1, HEAD_DIM), lambda h, d: (0, 0))],
        out_specs=[_head_spec(0), _head_spec(0),
                   pl.BlockSpec((None, None, nck, HEAD_DIM, HEAD_DIM), lambda h, d: (h, d, 0, 0, 0)),
                   pl.BlockSpec((None, SEQ, HEAD_DIM), lambda h, d: (d, 0, h)),
                   pl.BlockSpec((None, None, nck, 8, HEAD_DIM), lambda h, d: (h, d, 0, 0, 0))],
        out_shape=[jax.ShapeDtypeStruct((SEQ, D_MODEL), BF16), jax.ShapeDtypeStruct((SEQ, D_MODEL), F32),
                   jax.ShapeDtypeStruct((HG_HEADS, 2, nck, HEAD_DIM, HEAD_DIM), F32),
                   jax.ShapeDtypeStruct((2, SEQ, D_MODEL), BF16),
                   jax.ShapeDtypeStruct((HG_HEADS, 2, nck, 8, HEAD_DIM), F32)],
        scratch_shapes=[pltpu.VMEM((nck, HEAD_DIM, HEAD_DIM), F32)],
        compiler_params=_params(("parallel", "arbitrary"), 8 * blk + 3 * blk + 2 * blk + 2 * blk + blk),
    )(proj, proj, proj, proj, lb, gain)


def _gla_bwd(proj, lb, gain, o_pre, states, qt_all, dec_all, do_hg, dproj):
    nck = GLA_NCK
    do, dproj, dgain = _gla_bwd_norm(proj, gain, o_pre, do_hg, dproj, 256)

    def body(q_ref, f_ref, v_ref, lb_ref, do_ref, st_ref, qt_ref, dec_scr, dproj_in, dproj_ref, dlb_ref,
             dq_acc, dv_acc, dst_scr, cs_scr, df_out, dq_out, dv_out, sems):
        del dproj_in
        h, d = pl.program_id(0), pl.program_id(1)
        masks = _gla_masks(d)
        mask, maskb, _ = masks
        lbv = lb_ref[...]

        def column_copy(k, staging, col_block):
            cols = pl.ds(pl.multiple_of(col_block * LANE, LANE), LANE)
            return pltpu.make_async_copy(staging, dproj_ref.at[:, cols], sems.at[k])

        df_copy = column_copy(0, df_out, CB_F + 8 * d + h)
        dq_copy = column_copy(1, dq_out, CB_HG_Q + h)
        dv_copy = column_copy(2, dv_out, CB_HG_I + h)
        last = jnp.logical_and(h == HG_HEADS - 1, d == 1)

        def intra(s, carry):
            blocks = [s + w * GLA_TRIPS for w in range(GLA_WAYS)]
            loaded = [(qt_ref[r, :], do_ref[r, :]) for r in map(_gla_block_rows, blocks)]
            results = [[_dot(doc[_gla_chunk_rows(j), :], qt[_gla_chunk_rows(j), :], _TN) for j in range(GLA_CPB)]
                       for qt, doc in loaded]
            for b, cs in zip(blocks, results):
                for j in range(GLA_CPB):
                    cs_scr[b * GLA_CPB + j] = cs[j]
            return carry

        lax.fori_loop(0, GLA_TRIPS, intra, 0)

        def scan(i, dst):
            c = jnp.where(d == 0, nck - 1 - i, i)
            dst_scr[c] = dst
            return dst * dec_scr[c][0:1, :] + cs_scr[c]

        lax.fori_loop(0, nck, scan, jnp.zeros((HEAD_DIM, HEAD_DIM), F32), unroll=4)

        @pl.when(d == 0)
        def _():
            dq_acc[...] = jnp.zeros_like(dq_acc)
            dv_acc[...] = jnp.zeros_like(dv_acc)

        def block_grads(qc, fc, v, doc, states_in, dstates, decays):
            ck = _gla_block(qc, fc, lbv, masks)
            qt, kt, kh, a = ck["qt"], ck["kt"], ck["kh"], ck["a"]
            da = jnp.where(mask, _dot(doc, v, _NT), 0.0)
            dqt_i = _dot(da, kt, _NN)
            dkt = _dot(da, qt, _TN)
            dv_i = _dot(a, doc, _TN)
            dqt_p, dv_p, dkh_p, dbt_p = [], [], [], []
            for j in range(GLA_CPB):
                cr = _gla_chunk_rows(j)
                st_in, dst = states_in[j], dstates[j]
                dqt_p.append(dqt_i[cr, :] + _dot(doc[cr, :], st_in, _NN))
                dv_p.append(dv_i[cr, :] + _dot(kh[cr, :], dst, _NT))
                dkh_j = _dot(v[cr, :], dst, _NN)
                dkh_p.append(dkh_j)
                dbt_j = (decays[j][0:1, :] * jnp.sum(dst * st_in, axis=0, keepdims=True)
                         + jnp.sum(dkh_j * kh[cr, :], axis=0, keepdims=True))
                dbt_p.append(jnp.broadcast_to(dbt_j, (HG_CHUNK, HEAD_DIM)))
            dqt = jnp.concatenate(dqt_p, axis=0)
            dv = jnp.concatenate(dv_p, axis=0)
            dkh = jnp.concatenate(dkh_p, axis=0)
            dbt = jnp.concatenate(dbt_p, axis=0)
            db = dqt * qt - dkt * kt - dkh * kh
            dq = dqt * ck["eb"]
            dk = dkt * jnp.exp(-ck["b"]) + dkh * jnp.exp(ck["bt"] - ck["b"])
            dlogf = _dot_split(maskb, db, _TN) + dbt
            dforget = dlogf / ck["forget"] - dk
            sf, sq = ck["sf"], ck["sq"]
            df = (dforget * (1.0 - lbv) * sf * (1.0 - sf)).astype(BF16)
            dqc = dq * HG_SCALE * (sq + qc * sq * (1.0 - sq))
            return df, dqc, dv, jnp.sum(dforget * (1.0 - sf), axis=0, keepdims=True)

        def grads(s, dlb):
            blocks = [s + w * GLA_GRAD_TRIPS for w in range(GLA_GRAD_WAYS)]
            rows = [_gla_block_rows(b) for b in blocks]
            loaded = []
            for b, r in zip(blocks, rows):
                chunks = [b * GLA_CPB + j for j in range(GLA_CPB)]
                loaded.append((q_ref[r, :], f_ref[r, :], v_ref[r, :], do_ref[r, :], [st_ref[c] for c in chunks],
                               [dst_scr[c] for c in chunks], [dec_scr[c] for c in chunks], dq_acc[r, :], dv_acc[r, :]))
            results = [block_grads(*t[:7]) + (t[7], t[8]) for t in loaded]
            for r, (df, dqc, dv, dlb_part, dq_prev, dv_prev) in zip(rows, results):
                df_out[r, :] = df
                dq_acc[r, :] = dq_prev + dqc
                dv_acc[r, :] = dv_prev + dv
                dlb = dlb + dlb_part
            return dlb

        @pl.when(jnp.logical_or(h > 0, d > 0))
        def _():
            df_copy.wait()

        dlb_ref[...] = lax.fori_loop(0, GLA_GRAD_TRIPS, grads, jnp.zeros((1, HEAD_DIM), F32))
        df_copy.start()

        @pl.when(d == 1)
        def _():
            @pl.when(h > 0)
            def _():
                dq_copy.wait()
                dv_copy.wait()

            dq_out[...] = dq_acc[...].astype(BF16)
            dv_out[...] = dv_acc[...].astype(BF16)
            dq_copy.start()
            dv_copy.start()

        @pl.when(last)
        def _():
            df_copy.wait()
            dq_copy.wait()
            dv_copy.wait()

    blk = SEQ * HEAD_DIM * 4
    state_bytes = nck * HEAD_DIM * HEAD_DIM * 4
    dproj, dlb = pl.pallas_call(
        body, name="gla_bwd", grid=(HG_HEADS, 2),
        in_specs=[_head_spec(CB_HG_Q),
                  pl.BlockSpec((SEQ, HEAD_DIM), lambda h, d: (0, CB_F + 8 * d + h)),
                  _head_spec(CB_HG_I),
                  pl.BlockSpec((None, None, 1, HEAD_DIM), lambda h, d: (d, h, 0, 0)),
                  _head_spec(0),
                  pl.BlockSpec((None, None, nck, HEAD_DIM, HEAD_DIM), lambda h, d: (h, d, 0, 0, 0)),
                  pl.BlockSpec((None, SEQ, HEAD_DIM), lambda h, d: (d, 0, h)),
                  pl.BlockSpec((None, None, nck, 8, HEAD_DIM), lambda h, d: (h, d, 0, 0, 0)),
                  _any_spec()],
        out_specs=[_any_spec(), pl.BlockSpec((None, None, 1, HEAD_DIM), lambda h, d: (d, h, 0, 0))],
        out_shape=[jax.ShapeDtypeStruct((SEQ, IN_COLS), BF16), jax.ShapeDtypeStruct((2, HG_HEADS, 1, HEAD_DIM), F32)],
        scratch_shapes=[pltpu.VMEM((SEQ, HEAD_DIM), F32)] * 2
        + [pltpu.VMEM((nck, HEAD_DIM, HEAD_DIM), F32)] * 2
        + [pltpu.VMEM((SEQ, HEAD_DIM), BF16)] * 3 + [pltpu.SemaphoreType.DMA((3,))],
        input_output_aliases={8: 0},
        compiler_params=_params(("arbitrary", "arbitrary"), 8 * blk + 4 * state_bytes + 3 * blk + 3 * blk),
    )(proj, proj, proj, lb, do, states, qt_all, dec_all, dproj)
    return dproj, dlb, dgain


def _gla_bwd_norm(proj, gain, o_pre, do_hg, dproj, tr):
    w = D_MODEL

    def body(g_ref, gain_ref, opre_ref, dohg_ref, dproj_in, do_ref, dg_ref, dgain_ref):
        del dproj_in
        gainv = gain_ref[...]
        part = jnp.zeros((1, HEAD_DIM), F32)
        for h in range(HG_HEADS):
            hs = slice(h * HEAD_DIM, (h + 1) * HEAD_DIM)
            o = opre_ref[:, hs]
            r = lax.rsqrt(jnp.mean(o * o, axis=-1, keepdims=True) + RMS_EPS)
            ohat = o * r
            g = g_ref[:, hs]
            sg = _sigmoid(g)
            silu = g * sg
            dout = dohg_ref[:, hs]
            dg_ref[:, hs] = (dout * ohat * gainv * (sg + silu * (1.0 - sg))).astype(BF16)
            dy = dout * silu
            part = part + jnp.sum(dy * ohat, axis=0, keepdims=True)
            dn = dy * gainv
            do_ref[:, hs] = r * (dn - ohat * jnp.mean(dn * ohat, axis=-1, keepdims=True))

        @pl.when(pl.program_id(0) == 0)
        def _():
            dgain_ref[...] = part

        @pl.when(pl.program_id(0) > 0)
        def _():
            dgain_ref[...] += part

    return pl.pallas_call(
        body, name="gla_bwd_norm", grid=(SEQ // tr,),
        in_specs=[_row_spec(tr, w, CB_HG_G * LANE // w), _bcast_spec(HEAD_DIM), _row_spec(tr, w), _row_spec(tr, w),
                  _any_spec()],
        out_specs=[_row_spec(tr, w), _row_spec(tr, w, CB_HG_G * LANE // w), _bcast_spec(HEAD_DIM)],
        out_shape=[jax.ShapeDtypeStruct((SEQ, w), F32), jax.ShapeDtypeStruct((SEQ, IN_COLS), BF16),
                   jax.ShapeDtypeStruct((1, HEAD_DIM), F32)],
        input_output_aliases={4: 1},
        compiler_params=_params(("arbitrary",), 16 * tr * w * 4),
    )(proj, gain, o_pre, do_hg, dproj)


def _da_residue_rows(r, n0, size, d):
    if d == 1:
        return pl.ds(pl.multiple_of(n0, 8), size)
    return pl.ds(r + n0 * d, size, stride=d)


def _da_residue_ways(d, nqb):
    return min(d, 4) if nqb <= 2 else 1


def _da_rmsnorm(x, gain):
    r = lax.rsqrt(jnp.mean(x * x, axis=-1, keepdims=True) + RMS_EPS)
    return x * r, r, x * r * gain


def _da_scores(qn_scr, kn_scr, slope, i, ld):
    w0 = jnp.clip(i * DA_QB - DA_RADIUS, 0, ld - DA_WIN)
    w0 = pl.multiple_of(w0, DA_RADIUS)
    qrows = pl.ds(pl.multiple_of(i * DA_QB, DA_QB), DA_QB)
    win = pl.ds(w0, DA_WIN)
    qb = qn_scr[qrows, :]
    kw = kn_scr[win, :]
    s = _dot(qb, kw, _NT) * DA_SCALE
    qpos = i * DA_QB + lax.broadcasted_iota(jnp.int32, (DA_QB, DA_WIN), 0)
    kpos = w0 + lax.broadcasted_iota(jnp.int32, (DA_QB, DA_WIN), 1)
    arel = jnp.abs(kpos - qpos)
    s = s - slope * arel.astype(F32)
    s = jnp.where(arel <= DA_RADIUS, s, NEG_INF)
    return s, qb, kw, qrows, win


def _da_slopes(group):
    d = DA_DILATIONS[group]
    sl = _alibi_slopes(DA_HEADS)[4 * group:4 * group + 4] * d
    return jnp.asarray(np.broadcast_to(sl[:, None, None], (4, 1, LANE)).copy())


def _da_fwd(proj, gq, gk, group):
    d = DA_DILATIONS[group]
    ld = SEQ // d
    nqb = ld // DA_QB

    ways = min(DA_FWD_WAYS, nqb)
    rways = _da_residue_ways(d, nqb)

    def body(q_ref, k_ref, v_ref, gq_ref, gk_ref, sl_ref, o_ref, lse_ref, qn_scr, kn_scr, v_scr):
        slope = sl_ref[:, 0:1]

        def residues(t, carry):
            rs = [t * rways + u for u in range(rways)]
            for u, r in enumerate(rs):
                sel = _da_residue_rows(r, 0, ld, d)
                qn_scr[u] = _da_rmsnorm(q_ref[sel, :], gq_ref[...])[2].astype(BF16)
                kn_scr[u] = _da_rmsnorm(k_ref[sel, :], gk_ref[...])[2].astype(BF16)
                v_scr[u] = v_ref[sel, :].astype(BF16)

            def block(u, i):
                s, _, _, _, win = _da_scores(qn_scr.at[u], kn_scr.at[u], slope, i, ld)
                m = jnp.max(s, axis=-1, keepdims=True)
                p = jnp.exp(s - m)
                l = jnp.sum(p, axis=-1, keepdims=True)
                return _dot(p, v_scr[u, win, :], _NN) / l, jnp.broadcast_to(m + jnp.log(l), (DA_QB, HEAD_DIM))

            def step(i, c2):
                todo = [(u, r, i + w * (nqb // ways)) for u, r in enumerate(rs) for w in range(ways)]
                for (u, r, b), (o, lse) in zip(todo, [block(u, b) for u, _, b in todo]):
                    out = _da_residue_rows(r, b * DA_QB, DA_QB, d)
                    o_ref[out, :] = o
                    lse_ref[out, :] = lse
                return c2

            return lax.fori_loop(0, nqb // ways, step, carry)

        lax.fori_loop(0, d // rways, residues, 0)

    seq_spec = lambda base: pl.BlockSpec((SEQ, HEAD_DIM), lambda h: (0, base + 4 * group + h))
    out_spec = pl.BlockSpec((SEQ, HEAD_DIM), lambda h: (0, h))
    gain_spec = pl.BlockSpec((1, HEAD_DIM), lambda h: (0, 0))
    blk = SEQ * HEAD_DIM * 4
    return pl.pallas_call(
        body, name=f"da_fwd_g{group}", grid=(DA_HEADS_PER_GROUP,),
        in_specs=[seq_spec(CB_DA_Q), seq_spec(CB_DA_K), seq_spec(CB_DA_V), gain_spec, gain_spec,
                  pl.BlockSpec((None, 1, LANE), lambda h: (h, 0, 0))],
        out_specs=[out_spec, out_spec],
        out_shape=[jax.ShapeDtypeStruct((SEQ, DA_WIDTH), F32)] * 2,
        scratch_shapes=[pltpu.VMEM((rways, ld, HEAD_DIM), BF16)] * 3,
        compiler_params=_params(("parallel",), 10 * blk + 2 * blk),
    )(proj, proj, proj, gq, gk, _da_slopes(group))


def _da_merge(os, lses, tr):
    w = DA_WIDTH

    def body(o0, o1, o2, l0, l1, l2, ob_ref, of_ref, lse_ref):
        la, lb_, lc = l0[...], l1[...], l2[...]
        m = jnp.maximum(jnp.maximum(la, lb_), lc)
        ea, eb, ec = jnp.exp(la - m), jnp.exp(lb_ - m), jnp.exp(lc - m)
        tot = ea + eb + ec
        o = (ea * o0[...] + eb * o1[...] + ec * o2[...]) / tot
        of_ref[...] = o
        ob_ref[...] = o.astype(BF16)
        lse_ref[...] = m + jnp.log(tot)

    return pl.pallas_call(
        body, name="da_merge", grid=(SEQ // tr,), in_specs=[_row_spec(tr, w)] * 6,
        out_specs=[_row_spec(tr, w)] * 3,
        out_shape=[jax.ShapeDtypeStruct((SEQ, w), BF16), jax.ShapeDtypeStruct((SEQ, w), F32),
                   jax.ShapeDtypeStruct((SEQ, w), F32)],
        compiler_params=_params(("parallel",), 24 * tr * w * 4),
    )(*os, *lses)


def _da_rowdot(do, o, tr):
    w = DA_WIDTH

    def body(do_ref, o_ref, out_ref):
        prod = do_ref[...] * o_ref[...]
        for h in range(w // HEAD_DIM):
            sl = slice(h * HEAD_DIM, (h + 1) * HEAD_DIM)
            out_ref[:, sl] = jnp.broadcast_to(jnp.sum(prod[:, sl], axis=-1, keepdims=True), (tr, HEAD_DIM))

    return pl.pallas_call(
        body, name="da_rowdot", grid=(SEQ // tr,), in_specs=[_row_spec(tr, w)] * 2,
        out_specs=_row_spec(tr, w), out_shape=jax.ShapeDtypeStruct((SEQ, w), F32),
        compiler_params=_params(("parallel",), 10 * tr * w * 4),
    )(do, o)


def _da_bwd(proj, gq, gk, do, lse, dd, dproj, group):
    d = DA_DILATIONS[group]
    ld = SEQ // d
    nqb = ld // DA_QB
    ways = min(DA_WAYS, nqb)
    rways = _da_residue_ways(d, nqb)

    def body(q_ref, k_ref, v_ref, gq_ref, gk_ref, sl_ref, do_ref, lse_ref, dd_ref, dproj_in,
             dproj_ref, dgq_ref, dgk_ref,
             qn_scr, kn_scr, v_scr, dqn_scr, dkn_scr, dvr_scr, dq_scr, dk_scr, dv_scr, dq_out, dk_out, dv_out, sems):
        del dproj_in
        head = pl.program_id(0)
        gqv, gkv = gq_ref[...], gk_ref[...]
        slope = sl_ref[:, 0:1]

        copies = []
        for k, (staging, base) in enumerate(((dq_out, CB_DA_Q), (dk_out, CB_DA_K), (dv_out, CB_DA_V))):
            cols = pl.ds(pl.multiple_of((base + 4 * group + head) * LANE, LANE), LANE)
            copies.append(pltpu.make_async_copy(staging, dproj_ref.at[:, cols], sems.at[k]))

        def residues(t, carry):
            rs = [t * rways + u for u in range(rways)]
            sels = [_da_residue_rows(r, 0, ld, d) for r in rs]
            for u, sel in enumerate(sels):
                qn_scr[u] = _da_rmsnorm(q_ref[sel, :], gqv)[2].astype(BF16)
                kn_scr[u] = _da_rmsnorm(k_ref[sel, :], gkv)[2].astype(BF16)
                v_scr[u] = v_ref[sel, :].astype(BF16)
            dkn_scr[...] = jnp.zeros_like(dkn_scr)
            dvr_scr[...] = jnp.zeros_like(dvr_scr)

            def block(u, r, i):
                s, qb, kw, qrows, win = _da_scores(qn_scr.at[u], kn_scr.at[u], slope, i, ld)
                src = _da_residue_rows(r, i * DA_QB, DA_QB, d)
                p = jnp.exp(s - lse_ref[src, :][:, 0:1])
                dob = do_ref[src, :]
                dp = _dot(dob, v_scr[u, win, :], _NT)
                ds = p * (dp - dd_ref[src, :][:, 0:1]) * DA_SCALE
                return u, qrows, win, _dot(p, dob, _TN), _dot(ds, kw, _NN), _dot(ds, qb, _TN)

            def step(i, c2):
                todo = [(u, r, i + w * (nqb // ways)) for u, r in enumerate(rs) for w in range(ways)]
                for u, qrows, win, dv, dqn, dkn in [block(*t) for t in todo]:
                    dvr_scr[u, win, :] += dv
                    dqn_scr[u, qrows, :] = dqn
                    dkn_scr[u, win, :] += dkn
                return c2

            lax.fori_loop(0, nqb // ways, step, 0)

            pq, pk = carry
            for u, sel in enumerate(sels):
                parts = []
                for x_ref, gv, dn_scr, dx_scr in ((q_ref, gqv, dqn_scr, dq_scr), (k_ref, gkv, dkn_scr, dk_scr)):
                    hat, rstd, _ = _da_rmsnorm(x_ref[sel, :], gv)
                    dn = dn_scr[u]
                    dyg = dn * gv
                    dx_scr[sel, :] = rstd * (dyg - hat * jnp.mean(dyg * hat, axis=-1, keepdims=True))
                    parts.append(jnp.sum(dn * hat, axis=0, keepdims=True))
                dv_scr[sel, :] = dvr_scr[u]
                pq, pk = pq + parts[0], pk + parts[1]
            return pq, pk

        zero = jnp.zeros((1, HEAD_DIM), F32)
        pq, pk = lax.fori_loop(0, d // rways, residues, (zero, zero))

        @pl.when(pl.program_id(0) == 0)
        def _():
            dgq_ref[...] = pq
            dgk_ref[...] = pk

        @pl.when(pl.program_id(0) > 0)
        def _():
            dgq_ref[...] += pq
            dgk_ref[...] += pk

        @pl.when(head > 0)
        def _():
            for cp in copies:
                cp.wait()

        dq_out[...] = dq_scr[...].astype(BF16)
        dk_out[...] = dk_scr[...].astype(BF16)
        dv_out[...] = dv_scr[...].astype(BF16)
        for cp in copies:
            cp.start()

        @pl.when(head == DA_HEADS_PER_GROUP - 1)
        def _():
            for cp in copies:
                cp.wait()

    seq_spec = lambda base: pl.BlockSpec((SEQ, HEAD_DIM), lambda h: (0, base + 4 * group + h))
    out_spec = pl.BlockSpec((SEQ, HEAD_DIM), lambda h: (0, h))
    gain_spec = pl.BlockSpec((1, HEAD_DIM), lambda h: (0, 0))
    gshape = jax.ShapeDtypeStruct((1, HEAD_DIM), F32)
    blk = SEQ * HEAD_DIM * 4
    return pl.pallas_call(
        body, name=f"da_bwd_g{group}", grid=(DA_HEADS_PER_GROUP,),
        in_specs=[seq_spec(CB_DA_Q), seq_spec(CB_DA_K), seq_spec(CB_DA_V), gain_spec, gain_spec,
                  pl.BlockSpec((None, 1, LANE), lambda h: (h, 0, 0)), out_spec, out_spec, out_spec, _any_spec()],
        out_specs=[_any_spec(), gain_spec, gain_spec],
        out_shape=[jax.ShapeDtypeStruct((SEQ, IN_COLS), BF16), gshape, gshape],
        scratch_shapes=[pltpu.VMEM((rways, ld, HEAD_DIM), BF16)] * 3 + [pltpu.VMEM((rways, ld, HEAD_DIM), F32)] * 3
        + [pltpu.VMEM((SEQ, HEAD_DIM), F32)] * 3 + [pltpu.VMEM((SEQ, HEAD_DIM), BF16)] * 3
        + [pltpu.SemaphoreType.DMA((3,))],
        input_output_aliases={9: 0},
        compiler_params=_params(("arbitrary",), 12 * blk + 3 * blk + 3 * blk + 3 * blk + 2 * blk),
    )(proj, proj, proj, gq, gk, _da_slopes(group), do, lse, dd, dproj)


MEM_WAYS = 4


def _mem_probs(q, kn, gq):
    qhat, rq, qn = _da_rmsnorm(q, gq)
    s = _dot(qn, kn, _NT) * MEM_SCALE
    m = jnp.max(s, axis=-1, keepdims=True)
    e = jnp.exp(s - m)
    p = e / jnp.sum(e, axis=-1, keepdims=True)
    return p, qhat, rq, qn


def _mem_pieces(tq):
    rows = tq // MEM_WAYS
    return [slice(w * rows, (w + 1) * rows) for w in range(MEM_WAYS)]


def _mem_fwd(proj, kv, gq, gk, tq):
    def body(q_ref, k_ref, v_ref, gq_ref, gk_ref, o_ref):
        kn = _da_rmsnorm(k_ref[...], gk_ref[...])[2]
        v, gqv = v_ref[...], gq_ref[...]
        pieces = _mem_pieces(tq)
        outs = [_dot(_mem_probs(q_ref[rs, :], kn, gqv)[0], v, _NN) for rs in pieces]
        for rs, o in zip(pieces, outs):
            o_ref[rs, :] = o.astype(BF16)

    gain_spec = pl.BlockSpec((1, HEAD_DIM), lambda h, i: (0, 0))
    return pl.pallas_call(
        body, name="mem_fwd", grid=(MEM_HEADS, SEQ // tq),
        in_specs=[pl.BlockSpec((tq, HEAD_DIM), lambda h, i: (i, CB_MEM_Q + h)),
                  pl.BlockSpec((N_MEM, HEAD_DIM), lambda h, i: (0, h)),
                  pl.BlockSpec((N_MEM, HEAD_DIM), lambda h, i: (0, MEM_HEADS + h)), gain_spec, gain_spec],
        out_specs=pl.BlockSpec((tq, HEAD_DIM), lambda h, i: (i, h)),
        out_shape=jax.ShapeDtypeStruct((SEQ, MEM_WIDTH), BF16),
        compiler_params=_params(("parallel", "parallel"), 16 * tq * N_MEM * 4),
    )(proj, kv, kv, gq, gk)


def _mem_bwd(proj, kv, gq, gk, do, dproj, tq):
    nq = SEQ // tq

    def body(q_ref, k_ref, v_ref, gq_ref, gk_ref, do_ref, dproj_in, dq_ref, dk_ref, dv_ref, dgq_ref, dgk_ref, dkn_scr):
        del dproj_in
        h, i = pl.program_id(0), pl.program_id(1)
        gqv, gkv = gq_ref[...], gk_ref[...]
        khat, rk, kn = _da_rmsnorm(k_ref[...], gkv)
        v = v_ref[...]

        def piece(rs):
            p, qhat, rq, qn = _mem_probs(q_ref[rs, :], kn, gqv)
            dob = do_ref[rs, :]
            dp = _dot(dob, v, _NT)
            ds = p * (dp - jnp.sum(p * dp, axis=-1, keepdims=True)) * MEM_SCALE
            dqn = _dot(ds, kn, _NN)
            dyg = dqn * gqv
            dq = (rq * (dyg - qhat * jnp.mean(dyg * qhat, axis=-1, keepdims=True))).astype(BF16)
            return dq, _dot(p, dob, _TN), _dot(ds, qn, _TN), jnp.sum(dqn * qhat, axis=0, keepdims=True)

        pieces = _mem_pieces(tq)
        results = [piece(rs) for rs in pieces]
        for rs, res in zip(pieces, results):
            dq_ref[rs, :] = res[0]
        dvp = sum((res[1] for res in results[1:]), results[0][1])
        dknp = sum((res[2] for res in results[1:]), results[0][2])
        dgq_part = sum((res[3] for res in results[1:]), results[0][3])
        first = jnp.logical_and(h == 0, i == 0)

        @pl.when(first)
        def _():
            dgq_ref[...] = dgq_part

        @pl.when(jnp.logical_not(first))
        def _():
            dgq_ref[...] += dgq_part

        @pl.when(i == 0)
        def _():
            dv_ref[...] = dvp
            dkn_scr[...] = dknp

        @pl.when(i > 0)
        def _():
            dv_ref[...] += dvp
            dkn_scr[...] += dknp

        @pl.when(i == nq - 1)
        def _():
            dkn = dkn_scr[...]
            dkg = dkn * gkv
            dk_ref[...] = rk * (dkg - khat * jnp.mean(dkg * khat, axis=-1, keepdims=True))
            dgk_part = jnp.sum(dkn * khat, axis=0, keepdims=True)

            @pl.when(h == 0)
            def _():
                dgk_ref[...] = dgk_part

            @pl.when(h > 0)
            def _():
                dgk_ref[...] += dgk_part

    gain_spec = pl.BlockSpec((1, HEAD_DIM), lambda h, i: (0, 0))
    kvout = pl.BlockSpec((N_MEM, HEAD_DIM), lambda h, i: (0, h))
    return pl.pallas_call(
        body, name="mem_bwd", grid=(MEM_HEADS, nq),
        in_specs=[pl.BlockSpec((tq, HEAD_DIM), lambda h, i: (i, CB_MEM_Q + h)),
                  pl.BlockSpec((N_MEM, HEAD_DIM), lambda h, i: (0, h)),
                  pl.BlockSpec((N_MEM, HEAD_DIM), lambda h, i: (0, MEM_HEADS + h)), gain_spec, gain_spec,
                  pl.BlockSpec((tq, HEAD_DIM), lambda h, i: (i, h)), _any_spec()],
        out_specs=[pl.BlockSpec((tq, HEAD_DIM), lambda h, i: (i, CB_MEM_Q + h)), kvout, kvout, gain_spec, gain_spec],
        out_shape=[jax.ShapeDtypeStruct((SEQ, IN_COLS), BF16), jax.ShapeDtypeStruct((N_MEM, MEM_WIDTH), F32),
                   jax.ShapeDtypeStruct((N_MEM, MEM_WIDTH), F32), jax.ShapeDtypeStruct((1, HEAD_DIM), F32),
                   jax.ShapeDtypeStruct((1, HEAD_DIM), F32)],
        scratch_shapes=[pltpu.VMEM((N_MEM, HEAD_DIM), F32)], input_output_aliases={6: 0},
        compiler_params=_params(("arbitrary", "arbitrary"), 24 * tq * N_MEM * 4),
    )(proj, kv, kv, gq, gk, do, dproj)


def _mesh_position():
    return lax.axis_index("x"), lax.axis_index("y"), lax.axis_index("c")


def _any_spec():
    return pl.BlockSpec(memory_space=pl.ANY)


def _all_gather(shards):
    n = len(shards)

    def body(*refs):
        ins, outs = refs[:n], refs[n:2 * n]
        send_sems, recv_sems, local_sems = refs[2 * n:]
        x, y, c = _mesh_position()
        me, sibling = (x, y, c), (x, y, 1 - c)
        first = (jnp.where(c == 0, 1 - x, x), jnp.where(c == 0, y, 1 - y), c)
        second = (jnp.where(c == 0, x, 1 - x), jnp.where(c == 0, 1 - y, y), c)
        diagonal = (1 - x, 1 - y, c)

        def copy(w, k, block, to, src=None):
            px, py, pc = block
            rows = outs[w].at[4 * px + 2 * py + pc]
            return pltpu.make_async_remote_copy(
                src_ref=rows if src is None else src, dst_ref=rows,
                send_sem=send_sems.at[7 * w + k], recv_sem=recv_sems.at[7 * w + k],
                device_id=to, device_id_type=MESH)

        started = []
        for w in range(n):
            mine = pltpu.make_async_copy(ins[w], outs[w].at[4 * x + 2 * y + c], local_sems.at[w])
            mine.start()
            started.append(mine)
        sends = []
        for w in range(n):
            own = [copy(w, 0, me, sibling, src=ins[w]), copy(w, 1, me, first, src=ins[w]),
                   copy(w, 2, me, second, src=ins[w])]
            for cp in own:
                cp.start()
            sends += own
        for w in range(n):
            copy(w, 1, first, me).wait_recv()
            follow = [copy(w, 3, first, second), copy(w, 4, first, sibling)]
            for cp in follow:
                cp.start()
            copy(w, 2, second, me).wait_recv()
            follow.append(copy(w, 5, second, sibling))
            follow[-1].start()
            sends += follow
        for w in range(n):
            copy(w, 3, diagonal, me).wait_recv()
            passed = copy(w, 6, diagonal, sibling)
            passed.start()
            sends.append(passed)
        for w in range(n):
            for k in (0, 4, 5, 6):
                copy(w, k, sibling, me).wait_recv()
        for cp in sends:
            cp.wait_send()
        for mine in started:
            mine.wait()

    return pl.pallas_call(
        body, name="weights_all_gather",
        in_specs=[_any_spec()] * n, out_specs=[_any_spec()] * n,
        out_shape=[jax.ShapeDtypeStruct((N_DEV,) + s.shape, s.dtype) for s in shards],
        scratch_shapes=[pltpu.SemaphoreType.DMA((7 * n,)), pltpu.SemaphoreType.DMA((7 * n,)),
                        pltpu.SemaphoreType.DMA((n,))],
    )(*shards)


def _chip_of(j, x, y):
    return (1 - x if j & 1 else x, 1 - y if j & 2 else y)


_HBM_SPEC = pl.BlockSpec(memory_space=pltpu.HBM)
_SEM_SPEC = pl.BlockSpec(memory_space=pltpu.SEMAPHORE)
_DATAFLOW_EFFECT = pltpu.SideEffectType.DATAFLOW_SIDE_EFFECTING
TOKEN_SHAPE = (8, D_MODEL)


def _copies_start(name, arrays, n_copies, plan):
    n = len(arrays)

    def body(*refs):
        send_sems, recv_sems, token = refs[n], refs[n + 1], refs[2 * n + 2]
        copies = plan(refs[:n])
        assert len(copies) == n_copies
        for k, (src, dst, dev) in enumerate(copies):
            pltpu.make_async_remote_copy(src_ref=src, dst_ref=dst, send_sem=send_sems.at[k], recv_sem=recv_sems.at[k],
                                         device_id=dev, device_id_type=MESH).start()
        token[...] = jnp.zeros_like(token)

    outs = pl.pallas_call(
        body, name=name,
        out_shape=(pltpu.SemaphoreType.DMA((n_copies,)), pltpu.SemaphoreType.DMA((n_copies,)),
                   *[pltpu.HBM(a.shape, a.dtype) for a in arrays], jax.ShapeDtypeStruct(TOKEN_SHAPE, F32)),
        in_specs=[_HBM_SPEC] * n,
        out_specs=(_SEM_SPEC, _SEM_SPEC, *[_HBM_SPEC] * n, pl.BlockSpec(memory_space=pltpu.VMEM)),
        input_output_aliases={i: i + 2 for i in range(n)},
        compiler_params=pltpu.CompilerParams(has_side_effects=_DATAFLOW_EFFECT),
    )(*[pltpu.with_memory_space_constraint(a, pltpu.HBM) for a in arrays])
    return outs[0], outs[1], list(outs[2:2 + n]), outs[2 + n]


def _copies_wait(name, send_sems, recv_sems, arrays, n_copies, plan, after):
    n = len(arrays)
    after = list(after) if isinstance(after, (list, tuple)) else [after]

    def body(*refs):
        send_ref, recv_ref = refs[n], refs[n + 1]
        copies = plan(refs[:n])
        assert len(copies) == n_copies
        for k, (src, dst, dev) in enumerate(copies):
            cp = pltpu.make_async_remote_copy(src_ref=src, dst_ref=dst, send_sem=send_ref.at[k], recv_sem=recv_ref.at[k],
                                              device_id=dev, device_id_type=MESH)
            cp.wait_send()
            cp.wait_recv()

    outs = pl.pallas_call(
        body, name=name, out_shape=tuple(pltpu.HBM(a.shape, a.dtype) for a in arrays),
        in_specs=[_HBM_SPEC] * n + [_SEM_SPEC, _SEM_SPEC] + [pl.BlockSpec(memory_space=pl.ANY)] * len(after),
        out_specs=tuple([_HBM_SPEC] * n), input_output_aliases={i: i for i in range(n)},
        compiler_params=pltpu.CompilerParams(has_side_effects=_DATAFLOW_EFFECT),
    )(*arrays, send_sems, recv_sems, *after)
    return list(outs)


def _after(small, token):
    return small if token is None else small + token[0:1, :small.shape[-1]]


def _gather_plan_out(n):
    def plan(refs):
        x, y, c = _mesh_position()
        me = 4 * x + 2 * y + c
        copies = []
        for w in range(n):
            land = refs[n + w].at[me]
            copies.append((refs[w], land, (x, y, 1 - c)))
            for j in range(1, 4):
                copies.append((refs[w], land, (*_chip_of(j, x, y), c)))
        return copies
    return plan


def _gather_plan_pass(n):
    def plan(refs):
        x, y, c = _mesh_position()
        copies = []
        for w in range(n):
            for j in range(1, 4):
                px, py = _chip_of(j, x, y)
                rows = refs[w].at[4 * px + 2 * py + c]
                copies.append((rows, rows, (x, y, 1 - c)))
        return copies
    return plan


def _reduce_plan_sibling(n):
    def plan(refs):
        x, y, c = _mesh_position()
        copies = []
        for w in range(n):
            for j in range(4):
                px, py = _chip_of(j, x, y)
                copies.append((refs[w].at[4 * px + 2 * py + (1 - c)], refs[n + w].at[j], (x, y, 1 - c)))
        return copies
    return plan


def _reduce_plan_chips(n):
    def plan(refs):
        x, y, c = _mesh_position()
        copies = []
        for w in range(n):
            for j in range(1, 4):
                copies.append((refs[w].at[j - 1], refs[n + w].at[j - 1], (*_chip_of(j, x, y), c)))
        return copies
    return plan


def _chip_partials(grad, recv, name, tr):
    _, rows, width = grad.shape

    def body(g_ref, r_ref, own_ref, other_ref):
        x, y, c = _mesh_position()
        for j in range(4):
            px, py = _chip_of(j, x, y)
            total = g_ref[4 * px + 2 * py + c].astype(F32) + r_ref[j].astype(F32)
            if j == 0:
                own_ref[...] = total
            else:
                other_ref[j - 1] = total.astype(BF16)

    return pl.pallas_call(
        body, name=name, grid=(rows // tr,),
        in_specs=[pl.BlockSpec((N_DEV, tr, width), lambda i: (0, i, 0)),
                  pl.BlockSpec((4, tr, width), lambda i: (0, i, 0))],
        out_specs=[pl.BlockSpec((tr, width), lambda i: (i, 0)), pl.BlockSpec((3, tr, width), lambda i: (0, i, 0))],
        out_shape=[jax.ShapeDtypeStruct((rows, width), F32), jax.ShapeDtypeStruct((3, rows, width), BF16)],
        compiler_params=_params(("parallel",), 2 * 20 * tr * width * 2 + 8 * tr * width * 4),
    )(grad, recv)


def _adamw_math(w, g, m, v):
    m = ADAM_B1 * m + (1.0 - ADAM_B1) * g
    v = ADAM_B2 * v + (1.0 - ADAM_B2) * (g * g)
    m_hat = m / (1.0 - ADAM_B1 ** ADAM_STEP)
    v_hat = v / (1.0 - ADAM_B2 ** ADAM_STEP)
    delta = -ADAM_LR * (m_hat / (jnp.sqrt(v_hat) + ADAM_EPS) + ADAM_WD * w)
    return delta, m, v


def _adamw_shard(own, recv, w, m, v, name, tr):
    rows, width = own.shape

    def body(own_ref, r_ref, w_ref, m_ref, v_ref, g_ref, d_ref, nm_ref, nv_ref):
        g = own_ref[...]
        for j in range(3):
            g = g + r_ref[j].astype(F32)
        g_ref[...] = g
        d_ref[...], nm_ref[...], nv_ref[...] = _adamw_math(w_ref[...], g, m_ref[...], v_ref[...])

    spec = pl.BlockSpec((tr, width), lambda i: (i, 0))
    shape = jax.ShapeDtypeStruct((rows, width), F32)
    return pl.pallas_call(
        body, name=name, grid=(rows // tr,),
        in_specs=[spec, pl.BlockSpec((3, tr, width), lambda i: (0, i, 0)), spec, spec, spec],
        out_specs=[spec] * 4, out_shape=[shape] * 4,
        compiler_params=_params(("parallel",), 22 * tr * width * 4),
    )(own, recv, w, m, v)


def _small_all_reduce_adamw(gpart, lbpack, wpack, mpack, vpack, after):
    def body(gp_ref, lb_ref, w_ref, m_ref, v_ref, after_ref, g_ref, d_ref, nm_ref, nv_ref, gath_ref, send_sems, recv_sems):
        del after_ref
        x, y, c = _mesh_position()
        me = 4 * x + 2 * y + c
        gath_ref[me] = gp_ref[...]
        copies = []
        for j in range(1, N_DEV):
            peer = (x ^ (j >> 2), y ^ ((j >> 1) & 1), c ^ (j & 1))
            cp = pltpu.make_async_remote_copy(
                src_ref=gp_ref, dst_ref=gath_ref.at[me], send_sem=send_sems.at[j - 1], recv_sem=recv_sems.at[j - 1],
                device_id=peer, device_id_type=MESH)
            cp.start()
            copies.append(cp)
        for cp in copies:
            cp.wait()
        tot = gath_ref[0]
        for s in range(1, N_DEV):
            tot = tot + gath_ref[s]
        lb = lb_ref[...]
        dl = tot[16:32, :] * lb * (1.0 - lb)
        g = jnp.concatenate([tot[0:16, :], dl[0:8, :], -dl[0:8, :], dl[8:16, :], -dl[8:16, :],
                             tot[32:SMALL_GRAD_ROWS, :]], axis=0)
        g_ref[...] = g
        d_ref[...], nm_ref[...], nv_ref[...] = _adamw_math(w_ref[...], g, m_ref[...], v_ref[...])

    vm = pl.BlockSpec(memory_space=pltpu.VMEM)
    shape = jax.ShapeDtypeStruct((SMALL_ROWS, LANE), F32)
    return pl.pallas_call(
        body, name="small_all_reduce_adamw", in_specs=[vm] * 5 + [_any_spec()], out_specs=[vm] * 4,
        out_shape=[shape] * 4,
        scratch_shapes=[pltpu.VMEM((N_DEV, SMALL_GRAD_ROWS, LANE), F32),
                        pltpu.SemaphoreType.DMA((N_DEV - 1,)), pltpu.SemaphoreType.DMA((N_DEV - 1,))],
    )(gpart, lbpack, wpack, mpack, vpack, after)


_SMALL_NAMES = ("norm_mix_gain", "norm_mem_gain", "lb_logits_fw", "lb_logits_bw", "norm_ffn_gain",
                "hg_norm_gain", "da_q_gain", "da_k_gain", "mem_q_gain", "mem_k_gain")
_SMALL_ROW0 = {"norm_mix_gain": 0, "norm_mem_gain": 8, "lb_logits_fw": 16, "lb_logits_bw": 32, "norm_ffn_gain": 48,
               "hg_norm_gain": 56, "da_q_gain": 64, "da_k_gain": 72, "mem_q_gain": 80, "mem_k_gain": 88}
_LOSS_ROW = 96


def _pack_rows(parts, total_rows):
    rows = []
    for p in parts:
        r = p.reshape(-1, LANE)
        rows.append(jnp.pad(r, ((0, -r.shape[0] % 8), (0, 0))))
    used = sum(r.shape[0] for r in rows)
    if total_rows > used:
        rows.append(jnp.zeros((total_rows - used, LANE), F32))
    return jnp.concatenate(rows, axis=0)


def _unpack_small(pack, like):
    out = {}
    for name in _SMALL_NAMES:
        n = like[name].size // LANE
        r0 = _SMALL_ROW0[name]
        out[name] = pack[r0:r0 + n].reshape(like[name].shape)
    return out


def kernel(x, mem, norm_mix_gain, norm_mem_gain, w_in, lb_logits_fw, lb_logits_bw, hg_norm_gain, da_q_gain, da_k_gain, w_mem_kv, mem_q_gain, mem_k_gain, w_proj_hg, w_proj_da, w_proj_mem, w_out, norm_ffn_gain, w_ffn_in, w_ffn_out, loss_target, m_norm_mix_gain, m_norm_mem_gain, m_w_in, m_lb_logits_fw, m_lb_logits_bw, m_hg_norm_gain, m_da_q_gain, m_da_k_gain, m_w_mem_kv, m_mem_q_gain, m_mem_k_gain, m_w_proj_hg, m_w_proj_da, m_w_proj_mem, m_w_out, m_norm_ffn_gain, m_w_ffn_in, m_w_ffn_out, v_norm_mix_gain, v_norm_mem_gain, v_w_in, v_lb_logits_fw, v_lb_logits_bw, v_hg_norm_gain, v_da_q_gain, v_da_k_gain, v_w_mem_kv, v_mem_q_gain, v_mem_k_gain, v_w_proj_hg, v_w_proj_da, v_w_proj_mem, v_w_out, v_norm_ffn_gain, v_w_ffn_in, v_w_ffn_out):
    small_w = dict(norm_mix_gain=norm_mix_gain, norm_mem_gain=norm_mem_gain, lb_logits_fw=lb_logits_fw,
                   lb_logits_bw=lb_logits_bw, norm_ffn_gain=norm_ffn_gain, hg_norm_gain=hg_norm_gain,
                   da_q_gain=da_q_gain, da_k_gain=da_k_gain, mem_q_gain=mem_q_gain, mem_k_gain=mem_k_gain)
    small_m = dict(norm_mix_gain=m_norm_mix_gain, norm_mem_gain=m_norm_mem_gain, lb_logits_fw=m_lb_logits_fw,
                   lb_logits_bw=m_lb_logits_bw, norm_ffn_gain=m_norm_ffn_gain, hg_norm_gain=m_hg_norm_gain,
                   da_q_gain=m_da_q_gain, da_k_gain=m_da_k_gain, mem_q_gain=m_mem_q_gain, mem_k_gain=m_mem_k_gain)
    small_v = dict(norm_mix_gain=v_norm_mix_gain, norm_mem_gain=v_norm_mem_gain, lb_logits_fw=v_lb_logits_fw,
                   lb_logits_bw=v_lb_logits_bw, norm_ffn_gain=v_norm_ffn_gain, hg_norm_gain=v_hg_norm_gain,
                   da_q_gain=v_da_q_gain, da_k_gain=v_da_k_gain, mem_q_gain=v_mem_q_gain, mem_k_gain=v_mem_k_gain)
    big_names = ("w_in", "w_mem_kv", "w_proj_hg", "w_proj_da", "w_proj_mem", "w_out", "w_ffn_in", "w_ffn_out")
    big_w = dict(w_in=w_in[0], w_mem_kv=w_mem_kv[0], w_proj_hg=w_proj_hg[0], w_proj_da=w_proj_da[0],
                 w_proj_mem=w_proj_mem[0], w_out=w_out[0], w_ffn_in=w_ffn_in[0], w_ffn_out=w_ffn_out[0])
    big_m = dict(w_in=m_w_in[0], w_mem_kv=m_w_mem_kv[0], w_proj_hg=m_w_proj_hg[0], w_proj_da=m_w_proj_da[0],
                 w_proj_mem=m_w_proj_mem[0], w_out=m_w_out[0], w_ffn_in=m_w_ffn_in[0], w_ffn_out=m_w_ffn_out[0])
    big_v = dict(w_in=v_w_in[0], w_mem_kv=v_w_mem_kv[0], w_proj_hg=v_w_proj_hg[0], w_proj_da=v_w_proj_da[0],
                 w_proj_mem=v_w_proj_mem[0], w_out=v_w_out[0], w_ffn_in=v_w_ffn_in[0], w_ffn_out=v_w_ffn_out[0])

    row_tile = dict(w_in=128, w_mem_kv=128, w_proj_hg=128, w_proj_da=512, w_proj_mem=512, w_out=128,
                    w_ffn_in=128, w_ffn_out=352)
    rest = _BIG_NAMES[1:]
    shards = [big_w[n].astype(BF16) for n in rest]
    state = {}
    big_out = {}

    def reduce_start(group, stacked):
        names = tuple(stacked)
        arrays = [stacked[n] for n in names] + [lax.empty((4,) + stacked[n].shape[1:], BF16) for n in names]
        plan = _reduce_plan_sibling(len(names))
        send, recv, thru, token = _copies_start(f"grads_{group}_sibling_start", arrays, 4 * len(names), plan)
        state[group] = dict(names=names, plan=plan, send=send, recv=recv, arrays=thru)
        return token

    def reduce_middle(group, after):
        st = state[group]
        names, k = st["names"], len(st["names"])
        thru = _copies_wait(f"grads_{group}_sibling_wait", st["send"], st["recv"], st["arrays"], 4 * k, st["plan"], after)
        partials = [_chip_partials(thru[i], thru[k + i], f"chip_partials_{n}", row_tile[n]) for i, n in enumerate(names)]
        arrays = [p[1] for p in partials] + [lax.empty(p[1].shape, BF16) for p in partials]
        plan = _reduce_plan_chips(k)
        send, recv, thru2, token = _copies_start(f"grads_{group}_chips_start", arrays, 3 * k, plan)
        state[group] = dict(names=names, plan=plan, send=send, recv=recv, arrays=thru2, own=[p[0] for p in partials])
        return token

    def reduce_finish(group, after):
        st = state.pop(group)
        names, k = st["names"], len(st["names"])
        thru = _copies_wait(f"grads_{group}_chips_wait", st["send"], st["recv"], st["arrays"], 3 * k, st["plan"], after)
        for i, n in enumerate(names):
            big_out[n] = _adamw_shard(st["own"][i], thru[k + i], big_w[n], big_m[n], big_v[n], "adamw_" + n, row_tile[n])

    win_st = _all_gather([big_w["w_in"].astype(BF16)])[0]
    step = _local_step_stages(x[0], mem[0], loss_target[0], small_w, win_st)
    event, payload = next(step)
    local = None
    while True:
        reply = None
        if event == "begin":
            nr = len(rest)
            me = 4 * lax.axis_index("x") + 2 * lax.axis_index("y") + lax.axis_index("c")
            arrays = shards + [lax.dynamic_update_slice(lax.empty((N_DEV,) + s.shape, BF16), s[None], (me, 0, 0))
                               for s in shards]
            plan = _gather_plan_out(nr)
            send, recv, thru, reply = _copies_start("weights_rest_out_start", arrays, 4 * nr, plan)
            state["gather"] = dict(plan=plan, send=send, recv=recv, arrays=thru)
        elif event == "after_gla_fwd":
            st = state["gather"]
            nr = len(rest)
            thru = _copies_wait("weights_rest_out_wait", st["send"], st["recv"], st["arrays"], 4 * nr, st["plan"], payload)
            plan = _gather_plan_pass(nr)
            send, recv, lands, reply = _copies_start("weights_rest_pass_start", thru[nr:], 3 * nr, plan)
            state["gather"] = dict(plan=plan, send=send, recv=recv, arrays=lands)
        elif event == "need_weights":
            st = state.pop("gather")
            nr = len(rest)
            lands = _copies_wait("weights_rest_pass_wait", st["send"], st["recv"], st["arrays"], 3 * nr, st["plan"], payload)
            reply = dict(zip(rest, lands))
        elif event == "grads_ffn":
            reply = reduce_start("ffn", payload)
        elif event == "after_gate_merge_bwd":
            reply = reduce_middle("ffn", payload)
        elif event == "grads_mix":
            reply = reduce_start("mix", payload)
        elif event == "after_da_bwd_g0":
            reply = reduce_middle("mix", payload)
            reduce_finish("ffn", payload)
        elif event == "after_gla_bwd":
            reduce_finish("mix", payload)
        elif event == "grads_in":
            reply = reduce_start("in", payload)
        elif event == "after_proj_bwd_act_top":
            reply = reduce_middle("in", payload)
        elif event == "end":
            local = payload
            reduce_finish("in", [local["grad_x"]] + [big_out[n][0] for n in rest])
            break
        event, payload = step.send(reply)
    grad_x = local["grad_x"]

    wpack = _pack_rows([small_w[n] for n in _SMALL_NAMES], SMALL_ROWS)
    mpack = _pack_rows([small_m[n] for n in _SMALL_NAMES], SMALL_ROWS)
    vpack = _pack_rows([small_v[n] for n in _SMALL_NAMES], SMALL_ROWS)
    gs, ds, ms, vs = _small_all_reduce_adamw(local["gpart"], local["lbpack"], wpack, mpack, vpack, big_out["w_in"][0])
    loss = gs[_LOSS_ROW, 0]
    small_out = [_unpack_small(t, small_w) for t in (gs, ds, ms, vs)]

    order = ("norm_mix_gain", "norm_mem_gain", "w_in", "lb_logits_fw", "lb_logits_bw", "hg_norm_gain", "da_q_gain",
             "da_k_gain", "w_mem_kv", "mem_q_gain", "mem_k_gain", "w_proj_hg", "w_proj_da", "w_proj_mem", "w_out",
             "norm_ffn_gain", "w_ffn_in", "w_ffn_out")
    outs = [loss, grad_x[None]]
    for kind in range(4):
        for n in order:
            outs.append(big_out[n][kind][None] if n in big_out else small_out[kind][n])
    return tuple(outs)


_BIG_NAMES = ("w_in", "w_mem_kv", "w_proj_hg", "w_proj_da", "w_proj_mem", "w_out", "w_ffn_in", "w_ffn_out")


def _local_step(xs, mems, target, sw, wg):
    step = _local_step_stages(xs, mems, target, sw, wg["w_in"])
    stacked = {}
    event, payload = next(step)
    while event != "end":
        if event.startswith("grads_"):
            stacked.update(payload)
        event, payload = step.send(wg if event == "need_weights" else None)
    return dict(payload, stacked=stacked)


def _local_step_stages(xs, mems, target, sw, win_st):
    norm_mix_gain, norm_mem_gain, norm_ffn_gain = sw["norm_mix_gain"], sw["norm_mem_gain"], sw["norm_ffn_gain"]
    lb_logits_fw, lb_logits_bw, hg_norm_gain = sw["lb_logits_fw"], sw["lb_logits_bw"], sw["hg_norm_gain"]
    da_q_gain, da_k_gain, mem_q_gain, mem_k_gain = sw["da_q_gain"], sw["da_k_gain"], sw["mem_q_gain"], sw["mem_k_gain"]

    token = yield "begin", None
    lb_fw = _lb_table(lb_logits_fw, "lb_table_fw")
    lb_bw = _lb_table(lb_logits_bw, "lb_table_bw")
    lb = jnp.concatenate([lb_fw, lb_bw], axis=0).reshape(2, HG_HEADS, 1, HEAD_DIM)
    h, h_t = _rmsnorm_fwd(xs, _after(norm_mix_gain, token), "norm_mix_fwd", 512, transposed=True)
    proj = _matmul(h, win_st, "nn", F32, 1024, IN_SHARD, D_MODEL, "proj_fwd", b_stacked=True, n_outer=True)
    o_hg, o_pre, states, hg_qt, hg_decay = _gla_fwd(proj, lb, hg_norm_gain)
    token = yield "after_gla_fwd", o_hg
    da = [_da_fwd(proj, _after(da_q_gain, token), da_k_gain, g) for g in range(3)]
    o_da, o_da32, lse_da = _da_merge([t[0] for t in da], [t[1] for t in da], 512)
    wg = yield "need_weights", o_da
    wkv = wg["w_mem_kv"].reshape(D_MODEL, 2 * MEM_WIDTH)
    wphg = wg["w_proj_hg"].reshape(D_MODEL, D_MODEL)
    wpda = jnp.transpose(wg["w_proj_da"], (1, 0, 2)).reshape(DA_WIDTH, D_MODEL)
    wpmem = jnp.transpose(wg["w_proj_mem"], (1, 0, 2)).reshape(MEM_WIDTH, D_MODEL)
    wout = wg["w_out"].reshape(D_MODEL, D_MODEL)
    wfin = jnp.transpose(wg["w_ffn_in"], (1, 0, 2)).reshape(D_MODEL, 2 * D_FF)
    wfout = wg["w_ffn_out"].reshape(D_FF, D_MODEL)
    mem_n = _rmsnorm_fwd(mems, norm_mem_gain, "norm_mem_fwd", N_MEM)
    kv = _matmul(mem_n, wkv, "nn", F32, N_MEM, 1024, D_MODEL, "mem_kv_fwd")
    o_mem = _mem_fwd(proj, kv, mem_q_gain, mem_k_gain, 1024)
    t_hg = _matmul(o_hg, wphg, "nn", F32, 512, 1024, D_MODEL, "proj_hg_fwd")
    t_da = _matmul(o_da, wpda, "nn", F32, 512, 1024, DA_WIDTH, "proj_da_fwd")
    t_mem = _matmul(o_mem, wpmem, "nn", F32, 512, 1024, MEM_WIDTH, "proj_mem_fwd")
    merged = _gate_merge_fwd(proj, t_hg, t_da, t_mem, 256)
    u = _matmul(merged, wout, "nn", F32, 512, 1024, D_MODEL, "out_fwd")
    x1, h2, h2_t = _residual_rmsnorm_fwd(xs, u, norm_ffn_gain, "norm_ffn_fwd", 512)
    ffn_a, ffn_b, act, act_t = _ffn_in_swiglu(h2, wfin, 1024, 1408)
    z = _matmul(act, wfout, "nn", F32, 512, 1024, D_FF, "ffn_out_fwd")
    dy, dyb, loss_part = _loss_head(x1, z, target, 512)

    dact = _matmul(dyb, wfout, "nt", F32, 512, 1408, D_MODEL, "ffn_out_bwd_act")
    g_wfout = _matmul(act_t, dyb, "nn", BF16, 1408, 1024, 2048, "ffn_out_bwd_w")
    dab = _swiglu_bwd(ffn_a, ffn_b, dact, 256)
    dh2 = _matmul(dab, wfin, "nt", F32, 1024, 1024, 2816, "ffn_in_bwd_act")
    g_wfin = _matmul(h2_t, dab, "nn", BF16, 1024, 1408, 2048, "ffn_in_bwd_w")
    token = yield "grads_ffn", dict(
        w_ffn_in=jnp.transpose(g_wfin.reshape(D_MODEL, N_DEV, 2 * D_FF // N_DEV), (1, 0, 2)),
        w_ffn_out=g_wfout.reshape(N_DEV, D_FF // N_DEV, D_MODEL))
    dx1, dx1b, g_norm_ffn = _rmsnorm_bwd(x1, dh2, dy, _after(norm_ffn_gain, token), "norm_ffn_bwd", 256)
    dmerged = _matmul(dx1b, wout, "nt", F32, 512, 1024, D_MODEL, "out_bwd_act")
    g_wout = _matmul(merged, dx1b, "tn", BF16, 1024, 1024, 2048, "out_bwd_w")
    dt_hg, dt_da, dt_mem, dproj = _gate_merge_bwd(proj, t_hg, t_da, t_mem, dmerged, _dproj_buffer(), 256)
    token = yield "after_gate_merge_bwd", dt_hg
    do_hg = _matmul(dt_hg, wphg, "nt", F32, 512, 1024, D_MODEL, "proj_hg_bwd_act", after=token)
    g_wphg = _matmul(o_hg, dt_hg, "tn", BF16, 1024, 1024, 2048, "proj_hg_bwd_w")
    do_da = _matmul(dt_da, wpda, "nt", F32, 512, DA_WIDTH, D_MODEL, "proj_da_bwd_act")
    g_wpda = _matmul(o_da, dt_da, "tn", BF16, DA_WIDTH, D_MODEL, 2048, "proj_da_bwd_w")
    do_mem = _matmul(dt_mem, wpmem, "nt", F32, 512, MEM_WIDTH, D_MODEL, "proj_mem_bwd_act")
    g_wpmem = _matmul(o_mem, dt_mem, "tn", BF16, MEM_WIDTH, D_MODEL, 2048, "proj_mem_bwd_w")
    by_owner = lambda g: jnp.transpose(g.reshape(g.shape[0], N_DEV, D_MODEL // N_DEV), (1, 0, 2))
    g_wpda, g_wpmem = by_owner(g_wpda), by_owner(g_wpmem)

    dproj, dk_mem, dv_mem, g_mem_q, g_mem_k = _mem_bwd(proj, kv, mem_q_gain, mem_k_gain, do_mem, dproj, 1024)
    dkv = jnp.concatenate([dk_mem, dv_mem], axis=1).astype(BF16)
    g_wkv = _matmul(mem_n, dkv, "tn", BF16, 1024, 1024, N_MEM, "mem_kv_bwd_w")
    dmem_n = _matmul(dkv, wkv, "nt", F32, N_MEM, 1024, 1024, "mem_kv_bwd_act")
    g_norm_mem = _gain_grad(mems, dmem_n, "norm_mem_bwd")
    token = yield "grads_mix", dict(
        w_mem_kv=g_wkv.reshape(N_DEV, D_MODEL // N_DEV, 2 * MEM_WIDTH),
        w_proj_hg=g_wphg.reshape(N_DEV, D_MODEL // N_DEV, D_MODEL),
        w_proj_da=g_wpda, w_proj_mem=g_wpmem,
        w_out=g_wout.reshape(N_DEV, D_MODEL // N_DEV, D_MODEL))

    dd_da = _da_rowdot(do_da, o_da32, 512)
    dproj, g_da_q, g_da_k = _da_bwd(proj, _after(da_q_gain, token), da_k_gain, do_da, lse_da, dd_da, dproj, 0)
    token = yield "after_da_bwd_g0", g_da_q
    for g in (1, 2):
        dproj, gq_part, gk_part = _da_bwd(proj, _after(da_q_gain, token), da_k_gain, do_da, lse_da, dd_da, dproj, g)
        g_da_q, g_da_k = g_da_q + gq_part, g_da_k + gk_part

    dproj, dlb, g_hg_norm = _gla_bwd(proj, lb, hg_norm_gain, o_pre, states, hg_qt, hg_decay, do_hg, dproj)
    yield "after_gla_bwd", dlb

    g_win =_matmul(h_t, dproj, "nn", BF16, 1024, IN_SHARD, 2048, "proj_bwd_w", out_stacked=True)
    token = yield "grads_in", dict(w_in=g_win)
    dh = _matmul(dproj, win_st, "nt", F32, 1024, 1024, IN_SHARD, "proj_bwd_act_top", b_stacked=True, after=token,
                 m_blocks=(0, 2))
    token = yield "after_proj_bwd_act_top", dh
    dh = _matmul(dproj, win_st, "nt", F32, 1024, 1024, IN_SHARD, "proj_bwd_act_bottom", b_stacked=True, after=token,
                 m_blocks=(2, 2), out_into=dh)
    grad_x, _, g_norm_mix = _rmsnorm_bwd(xs, dh, dx1, _after(norm_mix_gain, token), "norm_mix_bwd", 256)

    gpart = _pack_rows([g_norm_mix, g_norm_mem, dlb[0], dlb[1], g_norm_ffn, g_hg_norm, g_da_q, g_da_k,
                        g_mem_q, g_mem_k, loss_part], SMALL_GRAD_ROWS)
    lbpack = jnp.concatenate([lb_fw.reshape(8, LANE), lb_bw.reshape(8, LANE)], axis=0)
    yield "end", dict(grad_x=grad_x, gpart=gpart, lbpack=lbpack)
```

```python
import numpy as np
import jax
import jax.numpy as jnp
from jax import lax
from jax.experimental import pallas as pl
from jax.experimental.pallas import tpu as pltpu

F32 = jnp.float32
BF16 = jnp.bfloat16
MESH = pl.DeviceIdType.MESH

SEQ = 4096
D_MODEL = 1024
N_DEV = 8
N_MEM = 256
RMS_EPS = 1e-6
NEG_INF = -1e30
LANE = 128
HEAD_DIM = 128
HG_HEADS = 8
HG_CHUNK = 64
HG_SCALE = HEAD_DIM ** -0.5
DA_DILATIONS = (1, 4, 16)
DA_RADIUS = 64
DA_HEADS_PER_GROUP = 4
DA_HEADS = 12
DA_WIDTH = 512
DA_SCALE = HEAD_DIM ** -0.5
DA_QB = 128
DA_WIN = 256
DA_WAYS = 4
DA_FWD_WAYS = 4
MEM_HEADS = 4
MEM_WIDTH = 512
MEM_SCALE = HEAD_DIM ** -0.5
D_FF = 2816
IN_COLS = 13312
IN_SHARD = IN_COLS // N_DEV
CB_HG_Q, CB_F, CB_HG_I, CB_HG_G = 0, 8, 24, 32
CB_DA_Q, CB_DA_K, CB_DA_V, CB_MEM_Q = 40, 52, 64, 76
N_CB = IN_COLS // LANE
ADAM_LR, ADAM_B1, ADAM_B2, ADAM_EPS, ADAM_WD, ADAM_STEP = 0.001, 0.9, 0.999, 1e-08, 0.01, 10
VMEM_BYTES_V7X = 64 * 1024 * 1024
SMALL_ROWS = 104
SMALL_GRAD_ROWS = 88

_NN = (((1,), (0,)), ((), ()))
_NT = (((1,), (1,)), ((), ()))
_TN = (((0,), (0,)), ((), ()))


def _dot(a, b, dims):
    return lax.dot_general(a.astype(BF16), b.astype(BF16), dims, preferred_element_type=F32)


def _dot_exact(a, b, dims):
    return lax.dot_general(a, b, dims, precision=lax.Precision.HIGHEST, preferred_element_type=F32)


def _sigmoid(x):
    return 0.5 * jnp.tanh(0.5 * x) + 0.5


def _params(semantics, est_bytes):
    limit = int(min(VMEM_BYTES_V7X - (6 << 20), max(56 << 20, est_bytes * 3 // 2)))
    return pltpu.CompilerParams(dimension_semantics=semantics, vmem_limit_bytes=limit)


def _nbytes(shape, dtype):
    return int(np.prod(shape)) * jnp.dtype(dtype).itemsize


def _alibi_slopes(n):
    return (2.0 ** (-8.0 * np.arange(1, n + 1) / n)).astype(np.float32)


def _matmul(a, b, mode, out_dtype, tm, tn, tk, name, b_stacked=False, out_stacked=False, n_outer=False, after=None,
            m_blocks=None, out_into=None, a_parts=False, b_parts=False):
    if a_parts:
        assert mode == "nt" and tk == a.shape[2]
        m, kdim = a.shape[1], a.shape[0] * a.shape[2]
    elif mode == "tn":
        kdim, m = a.shape
    else:
        m, kdim = a.shape
    if b_parts:
        assert mode == "nn" and not b_stacked and b.shape[2] % tn == 0
        n = b.shape[0] * b.shape[2]
    elif b_stacked:
        if mode == "nn":
            n = b.shape[0] * b.shape[2]
            assert tn == b.shape[2] and tk == kdim == b.shape[1]
        else:
            assert mode == "nt" and tk == b.shape[2] and b.shape[0] * tk == kdim
            n = b.shape[1]
    else:
        n = b.shape[0] if mode == "nt" else b.shape[1]
    assert m % tm == 0 and n % tn == 0 and kdim % tk == 0
    gm, gn, gk = m // tm, n // tn, kdim // tk
    i0 = 0
    if m_blocks is not None:
        assert mode != "tn" and not out_stacked
        i0, gm = m_blocks

    def ijk(f):
        if n_outer:
            return lambda j, i, k: f(i + i0, j, k)
        return lambda i, j, k: f(i + i0, j, k)

    if a_parts:
        a_spec = pl.BlockSpec((None, tm, tk), ijk(lambda i, j, k: (k, i, 0)))
    elif mode == "tn":
        a_spec = pl.BlockSpec((tk, tm), ijk(lambda i, j, k: (k, i)))
    else:
        a_spec = pl.BlockSpec((tm, tk), ijk(lambda i, j, k: (i, k)))
    if b_parts:
        per_part = b.shape[2] // tn
        b_spec = pl.BlockSpec((None, tk, tn), ijk(lambda i, j, k: (j // per_part, k, j % per_part)))
    elif b_stacked and mode == "nn":
        b_spec = pl.BlockSpec((None, tk, tn), ijk(lambda i, j, k: (j, 0, 0)))
    elif b_stacked:
        b_spec = pl.BlockSpec((None, tn, tk), ijk(lambda i, j, k: (k, j, 0)))
    elif mode == "nt":
        b_spec = pl.BlockSpec((tn, tk), ijk(lambda i, j, k: (j, k)))
    else:
        b_spec = pl.BlockSpec((tk, tn), ijk(lambda i, j, k: (k, j)))
    if out_stacked:
        assert tm == m
        out_shape = jax.ShapeDtypeStruct((gn, m, tn), out_dtype)
        o_spec = pl.BlockSpec((None, tm, tn), ijk(lambda i, j, k: (j, i, 0)))
    else:
        out_shape = jax.ShapeDtypeStruct((m, n), out_dtype)
        o_spec = pl.BlockSpec((tm, tn), ijk(lambda i, j, k: (i, j)))
    dims = {"nn": _NN, "nt": _NT, "tn": _TN}[mode]

    n_in = 2 + (after is not None) + (out_into is not None)

    def body(*refs):
        a_ref, b_ref, o_ref = refs[0], refs[1], refs[n_in]
        part = _dot(a_ref[...], b_ref[...], dims)
        if gk == 1:
            o_ref[...] = part.astype(out_dtype)
            return
        acc_ref = refs[-1]
        k = pl.program_id(2)

        @pl.when(k == 0)
        def _():
            acc_ref[...] = part

        @pl.when(jnp.logical_and(k > 0, k < gk - 1))
        def _():
            acc_ref[...] += part

        @pl.when(k == gk - 1)
        def _():
            o_ref[...] = (acc_ref[...] + part).astype(out_dtype)

    a_tile = _nbytes((tm, tk), a.dtype)
    b_tile = _nbytes((tk, tn), b.dtype)
    o_tile = _nbytes((tm, tn), out_dtype)
    est = 2 * (a_tile + b_tile + o_tile) + 3 * tm * tn * 4 + (a_tile + b_tile)
    grid = (gn, gm, gk) if n_outer else (gm, gn, gk)
    operands, in_specs = [a, b], [a_spec, b_spec]
    if after is not None:
        operands.append(after)
        in_specs.append(pl.BlockSpec(memory_space=pl.ANY))
    aliases = {}
    if out_into is not None:
        aliases = {len(operands): 0}
        operands.append(out_into)
        in_specs.append(pl.BlockSpec(memory_space=pl.ANY))
    return pl.pallas_call(
        body, name=name, grid=grid, in_specs=in_specs, out_specs=o_spec, out_shape=out_shape,
        scratch_shapes=[] if gk == 1 else [pltpu.VMEM((tm, tn), F32)], input_output_aliases=aliases,
        compiler_params=_params(("parallel", "parallel", "arbitrary"), est),
    )(*operands)


def _row_spec(tr, width, col_block=0):
    return pl.BlockSpec((tr, width), lambda i: (i, col_block))


def _bcast_spec(width):
    return pl.BlockSpec((1, width), lambda i: (0, 0))


def _col_spec(width, tr):
    return pl.BlockSpec((width, tr), lambda i: (0, i))


def _rmsnorm_fwd(x, gain, name, tr, transposed=False):
    rows, width = x.shape

    def body(x_ref, g_ref, o_ref, *t_ref):
        xv = x_ref[...]
        r = lax.rsqrt(jnp.mean(xv * xv, axis=-1, keepdims=True) + RMS_EPS)
        h = xv * r * g_ref[...]
        o_ref[...] = h.astype(BF16)
        if transposed:
            t_ref[0][...] = h.T.astype(BF16)

    out_specs, out_shape = [_row_spec(tr, width)], [jax.ShapeDtypeStruct((rows, width), BF16)]
    if transposed:
        out_specs.append(_col_spec(width, tr))
        out_shape.append(jax.ShapeDtypeStruct((width, rows), BF16))
    out = pl.pallas_call(
        body, name=name, grid=(rows // tr,), in_specs=[_row_spec(tr, width), _bcast_spec(width)],
        out_specs=out_specs, out_shape=out_shape,
        compiler_params=_params(("parallel",), 10 * tr * width * 4),
    )(x, gain)
    return out if transposed else out[0]


def _residual_rmsnorm_fwd(x, merged, wout, gain, name, tr):
    rows, width = x.shape

    def body(x_ref, m_ref, w_ref, g_ref, x1_ref, h_ref, ht_ref):
        xv = x_ref[...] + _dot(m_ref[...], w_ref[...], _NN)
        x1_ref[...] = xv
        r = lax.rsqrt(jnp.mean(xv * xv, axis=-1, keepdims=True) + RMS_EPS)
        h = xv * r * g_ref[...]
        h_ref[...] = h.astype(BF16)
        ht_ref[...] = h.T.astype(BF16)

    return pl.pallas_call(
        body, name=name, grid=(rows // tr,),
        in_specs=[_row_spec(tr, width), _row_spec(tr, merged.shape[1]),
                  pl.BlockSpec(wout.shape, lambda i: (0, 0)), _bcast_spec(width)],
        out_specs=[_row_spec(tr, width), _row_spec(tr, width), _col_spec(width, tr)],
        out_shape=[jax.ShapeDtypeStruct((rows, width), F32), jax.ShapeDtypeStruct((rows, width), BF16),
                   jax.ShapeDtypeStruct((width, rows), BF16)],
        compiler_params=_params(("parallel",), 14 * tr * width * 4),
    )(x, merged, wout, gain)


def _matmul_rmsnorm_bwd(a, b, x, dres, gain, name, tm, tk, a_parts=False, b_stacked=False, after=None,
                        m_blocks=None, out_into=None, with_bf16=True):
    width = x.shape[1]
    m = a.shape[1] if a_parts else a.shape[0]
    kdim = a.shape[0] * a.shape[2] if a_parts else a.shape[1]
    gk = kdim // tk
    i0, gm = (0, m // tm) if m_blocks is None else m_blocks
    n_in = 5 + (after is not None) + (out_into is not None)

    def body(*refs):
        a_ref, b_ref, x_ref, dres_ref, g_ref = refs[:5]
        outs = refs[n_in:]
        dx_ref, dg_ref, acc_ref = outs[0], outs[-2], outs[-1]
        i, k = pl.program_id(0), pl.program_id(1)
        part = _dot(a_ref[...], b_ref[...], _NT)

        @pl.when(k == 0)
        def _():
            acc_ref[...] = part

        @pl.when(jnp.logical_and(k > 0, k < gk - 1))
        def _():
            acc_ref[...] += part

        @pl.when(k == gk - 1)
        def _():
            dhv = acc_ref[...] + part if gk > 1 else part
            xv = x_ref[...]
            r = lax.rsqrt(jnp.mean(xv * xv, axis=-1, keepdims=True) + RMS_EPS)
            xhat = xv * r
            dyg = dhv * g_ref[...]
            dx = dres_ref[...] + r * (dyg - xhat * jnp.mean(dyg * xhat, axis=-1, keepdims=True))
            dx_ref[...] = dx
            if with_bf16:
                outs[1][...] = dx.astype(BF16)
            gpart = jnp.sum(dhv * xhat, axis=0, keepdims=True)

            @pl.when(i == 0)
            def _():
                dg_ref[...] = gpart

            @pl.when(i > 0)
            def _():
                dg_ref[...] += gpart

    rows = lambda width_: pl.BlockSpec((tm, width_), lambda i, k: (i + i0, 0))
    if a_parts:
        a_spec = pl.BlockSpec((None, tm, tk), lambda i, k: (k, i + i0, 0))
    else:
        a_spec = pl.BlockSpec((tm, tk), lambda i, k: (i + i0, k))
    if b_stacked:
        b_spec = pl.BlockSpec((None, width, tk), lambda i, k: (k, 0, 0))
    else:
        b_spec = pl.BlockSpec((width, tk), lambda i, k: (0, k))
    operands = [a, b, x, dres, gain]
    in_specs = [a_spec, b_spec, rows(width), rows(width), pl.BlockSpec((1, width), lambda i, k: (0, 0))]
    for extra in (after, out_into):
        if extra is not None:
            operands.append(extra)
            in_specs.append(pl.BlockSpec(memory_space=pl.ANY))
    aliases = {} if out_into is None else {len(operands) - 1: 0}
    out_specs = [rows(width)] + ([rows(width)] if with_bf16 else []) + [pl.BlockSpec((1, width), lambda i, k: (0, 0))]
    out_shape = [jax.ShapeDtypeStruct((m, width), F32)] + ([jax.ShapeDtypeStruct((m, width), BF16)] if with_bf16 else [])
    out_shape.append(jax.ShapeDtypeStruct((1, width), F32))
    est = 4 * tm * tk + 4 * width * tk + 12 * tm * width * 4
    return pl.pallas_call(
        body, name=name, grid=(gm, gk), in_specs=in_specs, out_specs=out_specs, out_shape=out_shape,
        scratch_shapes=[pltpu.VMEM((tm, width), F32)], input_output_aliases=aliases,
        compiler_params=_params(("arbitrary", "arbitrary"), est),
    )(*operands)


def _gain_grad(x, dh, name):
    rows, width = x.shape

    def body(x_ref, dh_ref, dg_ref):
        xv = x_ref[...]
        r = lax.rsqrt(jnp.mean(xv * xv, axis=-1, keepdims=True) + RMS_EPS)
        dg_ref[...] = jnp.sum(dh_ref[...] * xv * r, axis=0, keepdims=True)

    return pl.pallas_call(
        body, name=name, grid=(1,), in_specs=[_row_spec(rows, width), _row_spec(rows, width)],
        out_specs=_bcast_spec(width), out_shape=jax.ShapeDtypeStruct((1, width), F32),
        compiler_params=_params(("arbitrary",), 6 * rows * width * 4),
    )(x, dh)


def _lb_table(logits, name):
    slots, width = logits.shape

    def body(l_ref, o_ref):
        lv = l_ref[...]
        mx = jnp.max(lv, axis=0, keepdims=True)
        e = jnp.exp(lv - mx)
        o_ref[...] = e[0:1, :] / jnp.sum(e, axis=0, keepdims=True)

    return pl.pallas_call(
        body, name=name, grid=(1,), in_specs=[pl.BlockSpec((slots, width), lambda i: (0, 0))],
        out_specs=_bcast_spec(width), out_shape=jax.ShapeDtypeStruct((1, width), F32),
    )(logits)


def _gate_merge_fwd(proj, t_hg, t_da, t_mem, tr):
    w = D_MODEL

    def body(ghg_ref, gda_ref, gmem_ref, thg_ref, tda_ref, tmem_ref, o_ref):
        acc = _sigmoid(ghg_ref[...]) * thg_ref[...]
        acc += _sigmoid(gda_ref[...]) * tda_ref[...]
        acc += _sigmoid(gmem_ref[...]) * tmem_ref[...]
        o_ref[...] = acc.astype(BF16)

    return pl.pallas_call(
        body, name="gate_merge_fwd", grid=(SEQ // tr,),
        in_specs=[_row_spec(tr, w, 10), _row_spec(tr, w, 11), _row_spec(tr, w, 12),
                  _row_spec(tr, w), _row_spec(tr, w), _row_spec(tr, w)],
        out_specs=_row_spec(tr, w), out_shape=jax.ShapeDtypeStruct((SEQ, w), BF16),
        compiler_params=_params(("parallel",), 16 * tr * w * 4),
    )(proj, proj, proj, t_hg, t_da, t_mem)


def _dproj_buffer():
    return lax.empty((SEQ, IN_COLS), BF16)


def _gate_merge_bwd(proj, t_hg, t_da, t_mem, dmerged, dproj, tr):
    w = D_MODEL
    steps = SEQ // tr
    gate_col0 = 10 * w

    def body(ghg_ref, gda_ref, gmem_ref, thg_ref, tda_ref, tmem_ref, dm_ref, dproj_in,
             dthg_ref, dtda_ref, dtmem_ref, dproj_ref, stage, sems):
        del dproj_in
        i = pl.program_id(0)
        slot = i % 2

        def slot_copy(s):
            rows = pl.ds(pl.multiple_of(i * tr, tr), tr)
            return pltpu.make_async_copy(stage.at[s], dproj_ref.at[rows, pl.ds(gate_col0, 3 * w)], sems.at[s])

        @pl.when(i >= 2)
        def _():
            slot_copy(slot).wait()

        dm = dm_ref[...]
        for b, (g_ref, t_ref, dt_ref) in enumerate(((ghg_ref, thg_ref, dthg_ref), (gda_ref, tda_ref, dtda_ref),
                                                    (gmem_ref, tmem_ref, dtmem_ref))):
            s = _sigmoid(g_ref[...])
            dt_ref[...] = (s * dm).astype(BF16)
            stage[slot, :, b * w:(b + 1) * w] = (dm * t_ref[...] * s * (1.0 - s)).astype(BF16)
        slot_copy(slot).start()

        @pl.when(i == steps - 1)
        def _():
            slot_copy(1 - slot).wait()
            slot_copy(slot).wait()

    assert steps >= 2
    return pl.pallas_call(
        body, name="gate_merge_bwd", grid=(steps,),
        in_specs=[_row_spec(tr, w, 10), _row_spec(tr, w, 11), _row_spec(tr, w, 12),
                  _row_spec(tr, w), _row_spec(tr, w), _row_spec(tr, w), _row_spec(tr, w), _any_spec()],
        out_specs=[_row_spec(tr, w), _row_spec(tr, w), _row_spec(tr, w), _any_spec()],
        out_shape=[jax.ShapeDtypeStruct((SEQ, w), BF16)] * 3 + [jax.ShapeDtypeStruct((SEQ, IN_COLS), BF16)],
        scratch_shapes=[pltpu.VMEM((2, tr, 3 * w), BF16), pltpu.SemaphoreType.DMA((2,))],
        input_output_aliases={7: 3},
        compiler_params=_params(("arbitrary",), 26 * tr * w * 4),
    )(proj, proj, proj, t_hg, t_da, t_mem, dmerged, dproj)


def _ffn_in_swiglu(h2, wfin, tm, tn):
    gm, gn = SEQ // tm, D_FF // tn

    def body(h_ref, wa_ref, wb_ref, a_ref, b_ref, act_ref, actt_ref):
        h = h_ref[...]
        a = _dot(h, wa_ref[...], _NN)
        b = _dot(h, wb_ref[...], _NN)
        act = a * _sigmoid(a) * b
        a_ref[...] = a.astype(BF16)
        b_ref[...] = b.astype(BF16)
        act_ref[...] = act.astype(BF16)
        actt_ref[...] = act.T.astype(BF16)

    tile = pl.BlockSpec((tm, tn), lambda j, i: (i, j))
    shape = jax.ShapeDtypeStruct((SEQ, D_FF), BF16)
    return pl.pallas_call(
        body, name="ffn_in_swiglu_fwd", grid=(gn, gm),
        in_specs=[pl.BlockSpec((tm, D_MODEL), lambda j, i: (i, 0)),
                  pl.BlockSpec((D_MODEL, tn), lambda j, i: (0, j)),
                  pl.BlockSpec((D_MODEL, tn), lambda j, i: (0, gn + j))],
        out_specs=[tile, tile, tile, pl.BlockSpec((tn, tm), lambda j, i: (j, i))],
        out_shape=[shape, shape, shape, jax.ShapeDtypeStruct((D_FF, SEQ), BF16)],
        compiler_params=_params(("parallel", "parallel"), 4 * tm * D_MODEL + 8 * D_MODEL * tn + 16 * tm * tn
                                + 6 * tm * tn * 4),
    )(h2, wfin, wfin)


def _ffn_out_bwd_swiglu(dyb, wfout, a, b, tm, tn):
    gm, gn = SEQ // tm, D_FF // tn

    def body(dy_ref, w_ref, a_ref, b_ref, o_ref):
        d = _dot(dy_ref[...], w_ref[...], _NT)
        av = a_ref[...].astype(F32)
        bv = b_ref[...].astype(F32)
        s = _sigmoid(av)
        silu = av * s
        o_ref[0] = (d * bv * (s + silu * (1.0 - s))).astype(BF16)
        o_ref[1] = (d * silu).astype(BF16)

    tile = pl.BlockSpec((tm, tn), lambda i, j: (i, j))
    return pl.pallas_call(
        body, name="ffn_out_bwd_swiglu", grid=(gm, gn),
        in_specs=[pl.BlockSpec((tm, D_MODEL), lambda i, j: (i, 0)), pl.BlockSpec((tn, D_MODEL), lambda i, j: (j, 0)),
                  tile, tile],
        out_specs=pl.BlockSpec((2, tm, tn), lambda i, j: (0, i, j)),
        out_shape=jax.ShapeDtypeStruct((2, SEQ, D_FF), BF16),
        compiler_params=_params(("parallel", "parallel"), 4 * tm * D_MODEL + 4 * tn * D_MODEL + 16 * tm * tn
                                + 6 * tm * tn * 4),
    )(dyb, wfout, a, b)


def _ffn_out_loss_head(x1, act, wfout, target, tr):
    w = D_MODEL

    def body(x_ref, a_ref, w_ref, t_ref, dy_ref, dyb_ref, loss_ref, acc_ref):
        err = x_ref[...] + _dot(a_ref[...], w_ref[...], _NN) - t_ref[...]
        dy = err * (1.0 / w)
        dy_ref[...] = dy
        dyb_ref[...] = dy.astype(BF16)
        part = jnp.sum(err * err, axis=0, keepdims=True)

        @pl.when(pl.program_id(0) == 0)
        def _():
            acc_ref[...] = part

        @pl.when(pl.program_id(0) > 0)
        def _():
            acc_ref[...] += part

        @pl.when(pl.program_id(0) == SEQ // tr - 1)
        def _():
            total = jnp.sum(acc_ref[...], axis=1, keepdims=True) * (0.5 / w)
            loss_ref[...] = jnp.broadcast_to(total, (1, LANE))

    return pl.pallas_call(
        body, name="ffn_out_loss_head", grid=(SEQ // tr,),
        in_specs=[_row_spec(tr, w), _row_spec(tr, D_FF), pl.BlockSpec((D_FF, w), lambda i: (0, 0)), _row_spec(tr, w)],
        out_specs=[_row_spec(tr, w), _row_spec(tr, w), _bcast_spec(LANE)],
        out_shape=[jax.ShapeDtypeStruct((SEQ, w), F32), jax.ShapeDtypeStruct((SEQ, w), BF16),
                   jax.ShapeDtypeStruct((1, LANE), F32)],
        scratch_shapes=[pltpu.VMEM((1, w), F32)],
        compiler_params=_params(("arbitrary",), 12 * tr * w * 4 + 4 * D_FF * w),
    )(x1, act, wfout, target)


GLA_ROWS = 256
GLA_CPB = GLA_ROWS // HG_CHUNK
GLA_NBLK = SEQ // GLA_ROWS
GLA_WAYS = 4
GLA_TRIPS = GLA_NBLK // GLA_WAYS
GLA_GRAD_WAYS = 4
GLA_GRAD_TRIPS = GLA_NBLK // GLA_GRAD_WAYS
GLA_NCK = SEQ // HG_CHUNK


def _dot_split(m, xv, dims):
    hi = xv.astype(BF16)
    r1 = xv - hi.astype(F32)
    mid = r1.astype(BF16)
    lo = (r1 - mid.astype(F32)).astype(BF16)
    dot = lambda t: lax.dot_general(m, t, dims, preferred_element_type=F32)
    return (dot(lo) + dot(mid)) + dot(hi)


def _gla_block(qc, fc, lbv, masks):
    mask, maskb, direction = masks
    sq = _sigmoid(qc)
    q = qc * sq * HG_SCALE
    sf = _sigmoid(fc)
    forget = lbv + (1.0 - lbv) * sf
    k = 1.0 - forget
    logf = jnp.log(forget)
    b = _dot_split(maskb, logf, _NN)
    ends = []
    for j in range(GLA_CPB):
        lo, hi = j * HG_CHUNK, (j + 1) * HG_CHUNK
        end = jnp.where(direction == 0, b[hi - 1:hi, :], b[lo:lo + 1, :])
        ends.append(jnp.broadcast_to(end, (HG_CHUNK, HEAD_DIM)))
    bt = jnp.concatenate(ends, axis=0)
    eb = jnp.exp(b)
    qt = q * eb
    kt = k * jnp.exp(-b)
    kh = k * jnp.exp(bt - b)
    a = jnp.where(mask, _dot(qt, kt, _NT), 0.0)
    return dict(sq=sq, sf=sf, forget=forget, k=k, b=b, bt=bt, eb=eb, qt=qt, kt=kt, kh=kh, a=a)


def _gla_masks(direction):
    row = lax.broadcasted_iota(jnp.int32, (GLA_ROWS, GLA_ROWS), 0)
    col = lax.broadcasted_iota(jnp.int32, (GLA_ROWS, GLA_ROWS), 1)
    same = (row // HG_CHUNK) == (col // HG_CHUNK)
    mask = jnp.logical_and(same, jnp.where(direction == 0, row - col, col - row) >= 0)
    return mask, jnp.where(mask, 1.0, 0.0).astype(BF16), direction


def _gla_chunk_rows(j):
    return slice(j * HG_CHUNK, (j + 1) * HG_CHUNK)


def _gla_block_rows(b):
    return pl.ds(pl.multiple_of(b * GLA_ROWS, GLA_ROWS), GLA_ROWS)


def _head_spec(col_block0):
    return pl.BlockSpec((SEQ, HEAD_DIM), lambda h, d: (0, col_block0 + h))


def _gla_fwd(proj, lb, gain):
    nck = GLA_NCK

    def body(q_ref, f_ref, v_ref, g_ref, lb_ref, gain_ref, ohg_ref, opre_ref, st_ref, qt_scr, dec_scr, cs_scr):
        d = pl.program_id(1)
        masks = _gla_masks(d)
        lbv = lb_ref[...]

        @pl.when(d == 0)
        def _():
            opre_ref[...] = jnp.zeros_like(opre_ref)

        def intra(s, carry):
            blocks = [s + w * GLA_TRIPS for w in range(GLA_WAYS)]
            rows = [_gla_block_rows(b) for b in blocks]
            loaded = [(q_ref[r, :], f_ref[r, :], v_ref[r, :], opre_ref[r, :]) for r in rows]
            results = []
            for qc, fc, v, o_prev in loaded:
                ck = _gla_block(qc, fc, lbv, masks)
                o = o_prev + _dot(ck["a"], v, _NN)
                cs = [_dot(v[_gla_chunk_rows(j), :], ck["kh"][_gla_chunk_rows(j), :], _TN) for j in range(GLA_CPB)]
                dec = [jnp.exp(ck["bt"][j * HG_CHUNK:j * HG_CHUNK + 8, :]) for j in range(GLA_CPB)]
                results.append((o, ck["qt"].astype(BF16), cs, dec))
            for b, r, (o, qt, cs, dec) in zip(blocks, rows, results):
                opre_ref[r, :] = o
                qt_scr[r, :] = qt
                for j in range(GLA_CPB):
                    cs_scr[b * GLA_CPB + j] = cs[j]
                    dec_scr[b * GLA_CPB + j] = dec[j]
            return carry

        lax.fori_loop(0, GLA_TRIPS, intra, 0)

        def scan(i, st):
            c = jnp.where(d == 0, i, nck - 1 - i)
            st_ref[c] = st
            return st * dec_scr[c][0:1, :] + cs_scr[c]

        lax.fori_loop(0, nck, scan, jnp.zeros((HEAD_DIM, HEAD_DIM), F32), unroll=4)

        def inter(c, carry):
            rows = pl.ds(pl.multiple_of(c * HG_CHUNK, HG_CHUNK), HG_CHUNK)
            opre_ref[rows, :] += _dot(qt_scr[rows, :], st_ref[c], _NT)
            return carry

        lax.fori_loop(0, nck, inter, 0, unroll=4)

        @pl.when(d == 1)
        def _():
            o = opre_ref[...]
            r = lax.rsqrt(jnp.mean(o * o, axis=-1, keepdims=True) + RMS_EPS)
            g = g_ref[...]
            ohg_ref[...] = (o * r * gain_ref[...] * (g * _sigmoid(g))).astype(BF16)

    blk = SEQ * HEAD_DIM * 4
    return pl.pallas_call(
        body, name="gla_fwd", grid=(HG_HEADS, 2),
        in_specs=[_head_spec(CB_HG_Q),
                  pl.BlockSpec((SEQ, HEAD_DIM), lambda h, d: (0, CB_F + 8 * d + h)),
                  _head_spec(CB_HG_I), _head_spec(CB_HG_G),
                  pl.BlockSpec((None, None, 1, HEAD_DIM), lambda h, d: (d, h, 0, 0)),
                  pl.BlockSpec((1, HEAD_DIM), lambda h, d: (0, 0))],
        out_specs=[_head_spec(0), _head_spec(0),
                   pl.BlockSpec((None, None, nck, HEAD_DIM, HEAD_DIM), lambda h, d: (h, d, 0, 0, 0)),
                   pl.BlockSpec((None, SEQ, HEAD_DIM), lambda h, d: (d, 0, h)),
                   pl.BlockSpec((None, None, nck, 8, HEAD_DIM), lambda h, d: (h, d, 0, 0, 0))],
        out_shape=[jax.ShapeDtypeStruct((SEQ, D_MODEL), BF16), jax.ShapeDtypeStruct((SEQ, D_MODEL), F32),
                   jax.ShapeDtypeStruct((HG_HEADS, 2, nck, HEAD_DIM, HEAD_DIM), F32),
                   jax.ShapeDtypeStruct((2, SEQ, D_MODEL), BF16),
                   jax.ShapeDtypeStruct((HG_HEADS, 2, nck, 8, HEAD_DIM), F32)],
        scratch_shapes=[pltpu.VMEM((nck, HEAD_DIM, HEAD_DIM), F32)],
        compiler_params=_params(("parallel", "arbitrary"), 8 * blk + 3 * blk + 2 * blk + 2 * blk + blk),
    )(proj, proj, proj, proj, lb, gain)


def _gla_bwd(proj, lb, gain, o_pre, states, qt_all, dec_all, do_hg, dproj):
    nck = GLA_NCK
    do, dproj, dgain = _gla_bwd_norm(proj, gain, o_pre, do_hg, dproj, 256)

    def body(q_ref, f_ref, v_ref, lb_ref, do_ref, st_ref, qt_ref, dec_scr, dproj_in, dproj_ref, dlb_ref,
             dq_acc, dv_acc, dst_scr, cs_scr, df_out, dq_out, dv_out, sems):
        del dproj_in
        h, d = pl.program_id(0), pl.program_id(1)
        masks = _gla_masks(d)
        mask, maskb, _ = masks
        lbv = lb_ref[...]

        def column_copy(k, staging, col_block):
            cols = pl.ds(pl.multiple_of(col_block * LANE, LANE), LANE)
            return pltpu.make_async_copy(staging, dproj_ref.at[:, cols], sems.at[k])

        df_copy = column_copy(0, df_out, CB_F + 8 * d + h)
        dq_copy = column_copy(1, dq_out, CB_HG_Q + h)
        dv_copy = column_copy(2, dv_out, CB_HG_I + h)
        last = jnp.logical_and(h == HG_HEADS - 1, d == 1)

        def intra(s, carry):
            blocks = [s + w * GLA_TRIPS for w in range(GLA_WAYS)]
            loaded = [(qt_ref[r, :], do_ref[r, :]) for r in map(_gla_block_rows, blocks)]
            results = [[_dot(doc[_gla_chunk_rows(j), :], qt[_gla_chunk_rows(j), :], _TN) for j in range(GLA_CPB)]
                       for qt, doc in loaded]
            for b, cs in zip(blocks, results):
                for j in range(GLA_CPB):
                    cs_scr[b * GLA_CPB + j] = cs[j]
            return carry

        lax.fori_loop(0, GLA_TRIPS, intra, 0)

        def scan(i, dst):
            c = jnp.where(d == 0, nck - 1 - i, i)
            dst_scr[c] = dst
            return dst * dec_scr[c][0:1, :] + cs_scr[c]

        lax.fori_loop(0, nck, scan, jnp.zeros((HEAD_DIM, HEAD_DIM), F32), unroll=4)

        @pl.when(d == 0)
        def _():
            dq_acc[...] = jnp.zeros_like(dq_acc)
            dv_acc[...] = jnp.zeros_like(dv_acc)

        def block_grads(qc, fc, v, doc, states_in, dstates, decays):
            ck = _gla_block(qc, fc, lbv, masks)
            qt, kt, kh, a = ck["qt"], ck["kt"], ck["kh"], ck["a"]
            da = jnp.where(mask, _dot(doc, v, _NT), 0.0)
            dqt_i = _dot(da, kt, _NN)
            dkt = _dot(da, qt, _TN)
            dv_i = _dot(a, doc, _TN)
            dqt_p, dv_p, dkh_p, dbt_p = [], [], [], []
            for j in range(GLA_CPB):
                cr = _gla_chunk_rows(j)
                st_in, dst = states_in[j], dstates[j]
                dqt_p.append(dqt_i[cr, :] + _dot(doc[cr, :], st_in, _NN))
                dv_p.append(dv_i[cr, :] + _dot(kh[cr, :], dst, _NT))
                dkh_j = _dot(v[cr, :], dst, _NN)
                dkh_p.append(dkh_j)
                dbt_j = (decays[j][0:1, :] * jnp.sum(dst * st_in, axis=0, keepdims=True)
                         + jnp.sum(dkh_j * kh[cr, :], axis=0, keepdims=True))
                dbt_p.append(jnp.broadcast_to(dbt_j, (HG_CHUNK, HEAD_DIM)))
            dqt = jnp.concatenate(dqt_p, axis=0)
            dv = jnp.concatenate(dv_p, axis=0)
            dkh = jnp.concatenate(dkh_p, axis=0)
            dbt = jnp.concatenate(dbt_p, axis=0)
            db = dqt * qt - dkt * kt - dkh * kh
            dq = dqt * ck["eb"]
            dk = dkt * jnp.exp(-ck["b"]) + dkh * jnp.exp(ck["bt"] - ck["b"])
            dlogf = _dot_split(maskb, db, _TN) + dbt
            dforget = dlogf / ck["forget"] - dk
            sf, sq = ck["sf"], ck["sq"]
            df = (dforget * (1.0 - lbv) * sf * (1.0 - sf)).astype(BF16)
            dqc = dq * HG_SCALE * (sq + qc * sq * (1.0 - sq))
            return df, dqc, dv, jnp.sum(dforget * (1.0 - sf), axis=0, keepdims=True)

        def grads(s, dlb):
            blocks = [s + w * GLA_GRAD_TRIPS for w in range(GLA_GRAD_WAYS)]
            rows = [_gla_block_rows(b) for b in blocks]
            loaded = []
            for b, r in zip(blocks, rows):
                chunks = [b * GLA_CPB + j for j in range(GLA_CPB)]
                loaded.append((q_ref[r, :], f_ref[r, :], v_ref[r, :], do_ref[r, :], [st_ref[c] for c in chunks],
                               [dst_scr[c] for c in chunks], [dec_scr[c] for c in chunks], dq_acc[r, :], dv_acc[r, :]))
            results = [block_grads(*t[:7]) + (t[7], t[8]) for t in loaded]
            for r, (df, dqc, dv, dlb_part, dq_prev, dv_prev) in zip(rows, results):
                df_out[r, :] = df
                dq_acc[r, :] = dq_prev + dqc
                dv_acc[r, :] = dv_prev + dv
                dlb = dlb + dlb_part
            return dlb

        @pl.when(jnp.logical_or(h > 0, d > 0))
        def _():
            df_copy.wait()

        dlb_ref[...] = lax.fori_loop(0, GLA_GRAD_TRIPS, grads, jnp.zeros((1, HEAD_DIM), F32))
        df_copy.start()

        @pl.when(d == 1)
        def _():
            @pl.when(h > 0)
            def _():
                dq_copy.wait()
                dv_copy.wait()

            dq_out[...] = dq_acc[...].astype(BF16)
            dv_out[...] = dv_acc[...].astype(BF16)
            dq_copy.start()
            dv_copy.start()

        @pl.when(last)
        def _():
            df_copy.wait()
            dq_copy.wait()
            dv_copy.wait()

    blk = SEQ * HEAD_DIM * 4
    state_bytes = nck * HEAD_DIM * HEAD_DIM * 4
    dproj, dlb = pl.pallas_call(
        body, name="gla_bwd", grid=(HG_HEADS, 2),
        in_specs=[_head_spec(CB_HG_Q),
                  pl.BlockSpec((SEQ, HEAD_DIM), lambda h, d: (0, CB_F + 8 * d + h)),
                  _head_spec(CB_HG_I),
                  pl.BlockSpec((None, None, 1, HEAD_DIM), lambda h, d: (d, h, 0, 0)),
                  _head_spec(0),
                  pl.BlockSpec((None, None, nck, HEAD_DIM, HEAD_DIM), lambda h, d: (h, d, 0, 0, 0)),
                  pl.BlockSpec((None, SEQ, HEAD_DIM), lambda h, d: (d, 0, h)),
                  pl.BlockSpec((None, None, nck, 8, HEAD_DIM), lambda h, d: (h, d, 0, 0, 0)),
                  _any_spec()],
        out_specs=[_any_spec(), pl.BlockSpec((None, None, 1, HEAD_DIM), lambda h, d: (d, h, 0, 0))],
        out_shape=[jax.ShapeDtypeStruct((SEQ, IN_COLS), BF16), jax.ShapeDtypeStruct((2, HG_HEADS, 1, HEAD_DIM), F32)],
        scratch_shapes=[pltpu.VMEM((SEQ, HEAD_DIM), F32)] * 2
        + [pltpu.VMEM((nck, HEAD_DIM, HEAD_DIM), F32)] * 2
        + [pltpu.VMEM((SEQ, HEAD_DIM), BF16)] * 3 + [pltpu.SemaphoreType.DMA((3,))],
        input_output_aliases={8: 0},
        compiler_params=_params(("arbitrary", "arbitrary"), 8 * blk + 4 * state_bytes + 3 * blk + 3 * blk),
    )(proj, proj, proj, lb, do, states, qt_all, dec_all, dproj)
    return dproj, dlb, dgain


def _gla_bwd_norm(proj, gain, o_pre, do_hg, dproj, tr):
    w = D_MODEL

    def body(g_ref, gain_ref, opre_ref, dohg_ref, dproj_in, do_ref, dg_ref, dgain_ref):
        del dproj_in
        gainv = gain_ref[...]
        part = jnp.zeros((1, HEAD_DIM), F32)
        for h in range(HG_HEADS):
            hs = slice(h * HEAD_DIM, (h + 1) * HEAD_DIM)
            o = opre_ref[:, hs]
            r = lax.rsqrt(jnp.mean(o * o, axis=-1, keepdims=True) + RMS_EPS)
            ohat = o * r
            g = g_ref[:, hs]
            sg = _sigmoid(g)
            silu = g * sg
            dout = dohg_ref[:, hs]
            dg_ref[:, hs] = (dout * ohat * gainv * (sg + silu * (1.0 - sg))).astype(BF16)
            dy = dout * silu
            part = part + jnp.sum(dy * ohat, axis=0, keepdims=True)
            dn = dy * gainv
            do_ref[:, hs] = r * (dn - ohat * jnp.mean(dn * ohat, axis=-1, keepdims=True))

        @pl.when(pl.program_id(0) == 0)
        def _():
            dgain_ref[...] = part

        @pl.when(pl.program_id(0) > 0)
        def _():
            dgain_ref[...] += part

    return pl.pallas_call(
        body, name="gla_bwd_norm", grid=(SEQ // tr,),
        in_specs=[_row_spec(tr, w, CB_HG_G * LANE // w), _bcast_spec(HEAD_DIM), _row_spec(tr, w), _row_spec(tr, w),
                  _any_spec()],
        out_specs=[_row_spec(tr, w), _row_spec(tr, w, CB_HG_G * LANE // w), _bcast_spec(HEAD_DIM)],
        out_shape=[jax.ShapeDtypeStruct((SEQ, w), F32), jax.ShapeDtypeStruct((SEQ, IN_COLS), BF16),
                   jax.ShapeDtypeStruct((1, HEAD_DIM), F32)],
        input_output_aliases={4: 1},
        compiler_params=_params(("arbitrary",), 16 * tr * w * 4),
    )(proj, gain, o_pre, do_hg, dproj)


def _da_residue_rows(r, n0, size, d):
    if d == 1:
        return pl.ds(pl.multiple_of(n0, 8), size)
    return pl.ds(r + n0 * d, size, stride=d)


def _da_residue_ways(d, nqb):
    return min(d, 4) if nqb <= 2 else 1


def _da_rmsnorm(x, gain):
    r = lax.rsqrt(jnp.mean(x * x, axis=-1, keepdims=True) + RMS_EPS)
    return x * r, r, x * r * gain


def _da_scores(qn_scr, kn_scr, slope, i, ld):
    w0 = jnp.clip(i * DA_QB - DA_RADIUS, 0, ld - DA_WIN)
    w0 = pl.multiple_of(w0, DA_RADIUS)
    qrows = pl.ds(pl.multiple_of(i * DA_QB, DA_QB), DA_QB)
    win = pl.ds(w0, DA_WIN)
    qb = qn_scr[qrows, :]
    kw = kn_scr[win, :]
    s = _dot(qb, kw, _NT) * DA_SCALE
    qpos = i * DA_QB + lax.broadcasted_iota(jnp.int32, (DA_QB, DA_WIN), 0)
    kpos = w0 + lax.broadcasted_iota(jnp.int32, (DA_QB, DA_WIN), 1)
    arel = jnp.abs(kpos - qpos)
    s = s - slope * arel.astype(F32)
    s = jnp.where(arel <= DA_RADIUS, s, NEG_INF)
    return s, qb, kw, qrows, win


def _da_slopes(group):
    d = DA_DILATIONS[group]
    sl = _alibi_slopes(DA_HEADS)[4 * group:4 * group + 4] * d
    return jnp.asarray(np.broadcast_to(sl[:, None, None], (4, 1, LANE)).copy())


def _da_fwd(proj, gq, gk, group):
    d = DA_DILATIONS[group]
    ld = SEQ // d
    nqb = ld // DA_QB

    ways = min(DA_FWD_WAYS, nqb)
    rways = _da_residue_ways(d, nqb)

    def body(q_ref, k_ref, v_ref, gq_ref, gk_ref, sl_ref, o_ref, lse_ref, qn_scr, kn_scr, v_scr):
        slope = sl_ref[:, 0:1]

        def residues(t, carry):
            rs = [t * rways + u for u in range(rways)]
            for u, r in enumerate(rs):
                sel = _da_residue_rows(r, 0, ld, d)
                qn_scr[u] = _da_rmsnorm(q_ref[sel, :], gq_ref[...])[2].astype(BF16)
                kn_scr[u] = _da_rmsnorm(k_ref[sel, :], gk_ref[...])[2].astype(BF16)
                v_scr[u] = v_ref[sel, :].astype(BF16)

            def block(u, i):
                s, _, _, _, win = _da_scores(qn_scr.at[u], kn_scr.at[u], slope, i, ld)
                m = jnp.max(s, axis=-1, keepdims=True)
                p = jnp.exp(s - m)
                l = jnp.sum(p, axis=-1, keepdims=True)
                return _dot(p, v_scr[u, win, :], _NN) / l, jnp.broadcast_to(m + jnp.log(l), (DA_QB, HEAD_DIM))

            def step(i, c2):
                todo = [(u, r, i + w * (nqb // ways)) for u, r in enumerate(rs) for w in range(ways)]
                for (u, r, b), (o, lse) in zip(todo, [block(u, b) for u, _, b in todo]):
                    out = _da_residue_rows(r, b * DA_QB, DA_QB, d)
                    o_ref[out, :] = o
                    lse_ref[out, :] = lse
                return c2

            return lax.fori_loop(0, nqb // ways, step, carry)

        lax.fori_loop(0, d // rways, residues, 0)

    seq_spec = lambda base: pl.BlockSpec((SEQ, HEAD_DIM), lambda h: (0, base + 4 * group + h))
    out_spec = pl.BlockSpec((SEQ, HEAD_DIM), lambda h: (0, h))
    gain_spec = pl.BlockSpec((1, HEAD_DIM), lambda h: (0, 0))
    blk = SEQ * HEAD_DIM * 4
    return pl.pallas_call(
        body, name=f"da_fwd_g{group}", grid=(DA_HEADS_PER_GROUP,),
        in_specs=[seq_spec(CB_DA_Q), seq_spec(CB_DA_K), seq_spec(CB_DA_V), gain_spec, gain_spec,
                  pl.BlockSpec((None, 1, LANE), lambda h: (h, 0, 0))],
        out_specs=[out_spec, out_spec],
        out_shape=[jax.ShapeDtypeStruct((SEQ, DA_WIDTH), F32)] * 2,
        scratch_shapes=[pltpu.VMEM((rways, ld, HEAD_DIM), BF16)] * 3,
        compiler_params=_params(("parallel",), 10 * blk + 2 * blk),
    )(proj, proj, proj, gq, gk, _da_slopes(group))


def _da_merge(os, lses, tr):
    w = DA_WIDTH

    def body(o0, o1, o2, l0, l1, l2, ob_ref, of_ref, lse_ref):
        la, lb_, lc = l0[...], l1[...], l2[...]
        m = jnp.maximum(jnp.maximum(la, lb_), lc)
        ea, eb, ec = jnp.exp(la - m), jnp.exp(lb_ - m), jnp.exp(lc - m)
        tot = ea + eb + ec
        o = (ea * o0[...] + eb * o1[...] + ec * o2[...]) / tot
        of_ref[...] = o
        ob_ref[...] = o.astype(BF16)
        lse_ref[...] = m + jnp.log(tot)

    return pl.pallas_call(
        body, name="da_merge", grid=(SEQ // tr,), in_specs=[_row_spec(tr, w)] * 6,
        out_specs=[_row_spec(tr, w)] * 3,
        out_shape=[jax.ShapeDtypeStruct((SEQ, w), BF16), jax.ShapeDtypeStruct((SEQ, w), F32),
                   jax.ShapeDtypeStruct((SEQ, w), F32)],
        compiler_params=_params(("parallel",), 24 * tr * w * 4),
    )(*os, *lses)


def _da_rowdot(do, o, tr):
    w = DA_WIDTH

    def body(do_ref, o_ref, out_ref):
        prod = do_ref[...] * o_ref[...]
        for h in range(w // HEAD_DIM):
            sl = slice(h * HEAD_DIM, (h + 1) * HEAD_DIM)
            out_ref[:, sl] = jnp.broadcast_to(jnp.sum(prod[:, sl], axis=-1, keepdims=True), (tr, HEAD_DIM))

    return pl.pallas_call(
        body, name="da_rowdot", grid=(SEQ // tr,), in_specs=[_row_spec(tr, w)] * 2,
        out_specs=_row_spec(tr, w), out_shape=jax.ShapeDtypeStruct((SEQ, w), F32),
        compiler_params=_params(("parallel",), 10 * tr * w * 4),
    )(do, o)


def _da_bwd(proj, gq, gk, do, lse, dd, dproj, group):
    d = DA_DILATIONS[group]
    ld = SEQ // d
    nqb = ld // DA_QB
    ways = min(DA_WAYS, nqb)
    rways = _da_residue_ways(d, nqb)

    def body(q_ref, k_ref, v_ref, gq_ref, gk_ref, sl_ref, do_ref, lse_ref, dd_ref, dproj_in,
             dproj_ref, dgq_ref, dgk_ref,
             qn_scr, kn_scr, v_scr, dqn_scr, dkn_scr, dvr_scr, dq_scr, dk_scr, dv_scr, dq_out, dk_out, dv_out, sems):
        del dproj_in
        head = pl.program_id(0)
        gqv, gkv = gq_ref[...], gk_ref[...]
        slope = sl_ref[:, 0:1]

        copies = []
        for k, (staging, base) in enumerate(((dq_out, CB_DA_Q), (dk_out, CB_DA_K), (dv_out, CB_DA_V))):
            cols = pl.ds(pl.multiple_of((base + 4 * group + head) * LANE, LANE), LANE)
            copies.append(pltpu.make_async_copy(staging, dproj_ref.at[:, cols], sems.at[k]))

        def residues(t, carry):
            rs = [t * rways + u for u in range(rways)]
            sels = [_da_residue_rows(r, 0, ld, d) for r in rs]
            for u, sel in enumerate(sels):
                qn_scr[u] = _da_rmsnorm(q_ref[sel, :], gqv)[2].astype(BF16)
                kn_scr[u] = _da_rmsnorm(k_ref[sel, :], gkv)[2].astype(BF16)
                v_scr[u] = v_ref[sel, :].astype(BF16)
            dkn_scr[...] = jnp.zeros_like(dkn_scr)
            dvr_scr[...] = jnp.zeros_like(dvr_scr)

            def block(u, r, i):
                s, qb, kw, qrows, win = _da_scores(qn_scr.at[u], kn_scr.at[u], slope, i, ld)
                src = _da_residue_rows(r, i * DA_QB, DA_QB, d)
                p = jnp.exp(s - lse_ref[src, :][:, 0:1])
                dob = do_ref[src, :]
                dp = _dot(dob, v_scr[u, win, :], _NT)
                ds = p * (dp - dd_ref[src, :][:, 0:1]) * DA_SCALE
                return u, qrows, win, _dot(p, dob, _TN), _dot(ds, kw, _NN), _dot(ds, qb, _TN)

            def step(i, c2):
                todo = [(u, r, i + w * (nqb // ways)) for u, r in enumerate(rs) for w in range(ways)]
                for u, qrows, win, dv, dqn, dkn in [block(*t) for t in todo]:
                    dvr_scr[u, win, :] += dv
                    dqn_scr[u, qrows, :] = dqn
                    dkn_scr[u, win, :] += dkn
                return c2

            lax.fori_loop(0, nqb // ways, step, 0)

            pq, pk = carry
            for u, sel in enumerate(sels):
                parts = []
                for x_ref, gv, dn_scr, dx_scr in ((q_ref, gqv, dqn_scr, dq_scr), (k_ref, gkv, dkn_scr, dk_scr)):
                    hat, rstd, _ = _da_rmsnorm(x_ref[sel, :], gv)
                    dn = dn_scr[u]
                    dyg = dn * gv
                    dx_scr[sel, :] = rstd * (dyg - hat * jnp.mean(dyg * hat, axis=-1, keepdims=True))
                    parts.append(jnp.sum(dn * hat, axis=0, keepdims=True))
                dv_scr[sel, :] = dvr_scr[u]
                pq, pk = pq + parts[0], pk + parts[1]
            return pq, pk

        zero = jnp.zeros((1, HEAD_DIM), F32)
        pq, pk = lax.fori_loop(0, d // rways, residues, (zero, zero))

        @pl.when(pl.program_id(0) == 0)
        def _():
            dgq_ref[...] = pq
            dgk_ref[...] = pk

        @pl.when(pl.program_id(0) > 0)
        def _():
            dgq_ref[...] += pq
            dgk_ref[...] += pk

        @pl.when(head > 0)
        def _():
            for cp in copies:
                cp.wait()

        dq_out[...] = dq_scr[...].astype(BF16)
        dk_out[...] = dk_scr[...].astype(BF16)
        dv_out[...] = dv_scr[...].astype(BF16)
        for cp in copies:
            cp.start()

        @pl.when(head == DA_HEADS_PER_GROUP - 1)
        def _():
            for cp in copies:
                cp.wait()

    seq_spec = lambda base: pl.BlockSpec((SEQ, HEAD_DIM), lambda h: (0, base + 4 * group + h))
    out_spec = pl.BlockSpec((SEQ, HEAD_DIM), lambda h: (0, h))
    gain_spec = pl.BlockSpec((1, HEAD_DIM), lambda h: (0, 0))
    gshape = jax.ShapeDtypeStruct((1, HEAD_DIM), F32)
    blk = SEQ * HEAD_DIM * 4
    return pl.pallas_call(
        body, name=f"da_bwd_g{group}", grid=(DA_HEADS_PER_GROUP,),
        in_specs=[seq_spec(CB_DA_Q), seq_spec(CB_DA_K), seq_spec(CB_DA_V), gain_spec, gain_spec,
                  pl.BlockSpec((None, 1, LANE), lambda h: (h, 0, 0)), out_spec, out_spec, out_spec, _any_spec()],
        out_specs=[_any_spec(), gain_spec, gain_spec],
        out_shape=[jax.ShapeDtypeStruct((SEQ, IN_COLS), BF16), gshape, gshape],
        scratch_shapes=[pltpu.VMEM((rways, ld, HEAD_DIM), BF16)] * 3 + [pltpu.VMEM((rways, ld, HEAD_DIM), F32)] * 3
        + [pltpu.VMEM((SEQ, HEAD_DIM), F32)] * 3 + [pltpu.VMEM((SEQ, HEAD_DIM), BF16)] * 3
        + [pltpu.SemaphoreType.DMA((3,))],
        input_output_aliases={9: 0},
        compiler_params=_params(("arbitrary",), 12 * blk + 3 * blk + 3 * blk + 3 * blk + 2 * blk),
    )(proj, proj, proj, gq, gk, _da_slopes(group), do, lse, dd, dproj)


MEM_WAYS = 4


def _mem_probs(q, kn, gq):
    qhat, rq, qn = _da_rmsnorm(q, gq)
    s = _dot(qn, kn, _NT) * MEM_SCALE
    m = jnp.max(s, axis=-1, keepdims=True)
    e = jnp.exp(s - m)
    p = e / jnp.sum(e, axis=-1, keepdims=True)
    return p, qhat, rq, qn


def _mem_pieces(tq):
    rows = tq // MEM_WAYS
    return [slice(w * rows, (w + 1) * rows) for w in range(MEM_WAYS)]


def _mem_fwd(proj, kv, gq, gk, tq):
    def body(q_ref, k_ref, v_ref, gq_ref, gk_ref, o_ref):
        kn = _da_rmsnorm(k_ref[...], gk_ref[...])[2]
        v, gqv = v_ref[...], gq_ref[...]
        pieces = _mem_pieces(tq)
        outs = [_dot(_mem_probs(q_ref[rs, :], kn, gqv)[0], v, _NN) for rs in pieces]
        for rs, o in zip(pieces, outs):
            o_ref[rs, :] = o.astype(BF16)

    gain_spec = pl.BlockSpec((1, HEAD_DIM), lambda h, i: (0, 0))
    return pl.pallas_call(
        body, name="mem_fwd", grid=(MEM_HEADS, SEQ // tq),
        in_specs=[pl.BlockSpec((tq, HEAD_DIM), lambda h, i: (i, CB_MEM_Q + h)),
                  pl.BlockSpec((N_MEM, HEAD_DIM), lambda h, i: (0, h)),
                  pl.BlockSpec((N_MEM, HEAD_DIM), lambda h, i: (0, MEM_HEADS + h)), gain_spec, gain_spec],
        out_specs=pl.BlockSpec((tq, HEAD_DIM), lambda h, i: (i, h)),
        out_shape=jax.ShapeDtypeStruct((SEQ, MEM_WIDTH), BF16),
        compiler_params=_params(("parallel", "parallel"), 16 * tq * N_MEM * 4),
    )(proj, kv, kv, gq, gk)


def _mem_bwd(proj, kv, gq, gk, do, dproj, tq):
    nq = SEQ // tq

    def body(q_ref, k_ref, v_ref, gq_ref, gk_ref, do_ref, dproj_in, dq_ref, dk_ref, dv_ref, dgq_ref, dgk_ref, dkn_scr):
        del dproj_in
        h, i = pl.program_id(0), pl.program_id(1)
        gqv, gkv = gq_ref[...], gk_ref[...]
        khat, rk, kn = _da_rmsnorm(k_ref[...], gkv)
        v = v_ref[...]

        def piece(rs):
            p, qhat, rq, qn = _mem_probs(q_ref[rs, :], kn, gqv)
            dob = do_ref[rs, :]
            dp = _dot(dob, v, _NT)
            ds = p * (dp - jnp.sum(p * dp, axis=-1, keepdims=True)) * MEM_SCALE
            dqn = _dot(ds, kn, _NN)
            dyg = dqn * gqv
            dq = (rq * (dyg - qhat * jnp.mean(dyg * qhat, axis=-1, keepdims=True))).astype(BF16)
            return dq, _dot(p, dob, _TN), _dot(ds, qn, _TN), jnp.sum(dqn * qhat, axis=0, keepdims=True)

        pieces = _mem_pieces(tq)
        results = [piece(rs) for rs in pieces]
        for rs, res in zip(pieces, results):
            dq_ref[rs, :] = res[0]
        dvp = sum((res[1] for res in results[1:]), results[0][1])
        dknp = sum((res[2] for res in results[1:]), results[0][2])
        dgq_part = sum((res[3] for res in results[1:]), results[0][3])
        first = jnp.logical_and(h == 0, i == 0)

        @pl.when(first)
        def _():
            dgq_ref[...] = dgq_part

        @pl.when(jnp.logical_not(first))
        def _():
            dgq_ref[...] += dgq_part

        @pl.when(i == 0)
        def _():
            dv_ref[...] = dvp
            dkn_scr[...] = dknp

        @pl.when(i > 0)
        def _():
            dv_ref[...] += dvp
            dkn_scr[...] += dknp

        @pl.when(i == nq - 1)
        def _():
            dkn = dkn_scr[...]
            dkg = dkn * gkv
            dk_ref[...] = rk * (dkg - khat * jnp.mean(dkg * khat, axis=-1, keepdims=True))
            dgk_part = jnp.sum(dkn * khat, axis=0, keepdims=True)

            @pl.when(h == 0)
            def _():
                dgk_ref[...] = dgk_part

            @pl.when(h > 0)
            def _():
                dgk_ref[...] += dgk_part

    gain_spec = pl.BlockSpec((1, HEAD_DIM), lambda h, i: (0, 0))
    kvout = pl.BlockSpec((N_MEM, HEAD_DIM), lambda h, i: (0, h))
    return pl.pallas_call(
        body, name="mem_bwd", grid=(MEM_HEADS, nq),
        in_specs=[pl.BlockSpec((tq, HEAD_DIM), lambda h, i: (i, CB_MEM_Q + h)),
                  pl.BlockSpec((N_MEM, HEAD_DIM), lambda h, i: (0, h)),
                  pl.BlockSpec((N_MEM, HEAD_DIM), lambda h, i: (0, MEM_HEADS + h)), gain_spec, gain_spec,
                  pl.BlockSpec((tq, HEAD_DIM), lambda h, i: (i, h)), _any_spec()],
        out_specs=[pl.BlockSpec((tq, HEAD_DIM), lambda h, i: (i, CB_MEM_Q + h)), kvout, kvout, gain_spec, gain_spec],
        out_shape=[jax.ShapeDtypeStruct((SEQ, IN_COLS), BF16), jax.ShapeDtypeStruct((N_MEM, MEM_WIDTH), F32),
                   jax.ShapeDtypeStruct((N_MEM, MEM_WIDTH), F32), jax.ShapeDtypeStruct((1, HEAD_DIM), F32),
                   jax.ShapeDtypeStruct((1, HEAD_DIM), F32)],
        scratch_shapes=[pltpu.VMEM((N_MEM, HEAD_DIM), F32)], input_output_aliases={6: 0},
        compiler_params=_params(("arbitrary", "arbitrary"), 24 * tq * N_MEM * 4),
    )(proj, kv, kv, gq, gk, do, dproj)


def _mesh_position():
    return lax.axis_index("x"), lax.axis_index("y"), lax.axis_index("c")


def _any_spec():
    return pl.BlockSpec(memory_space=pl.ANY)


def _all_gather(shards):
    n = len(shards)

    def body(*refs):
        ins, outs = refs[:n], refs[n:2 * n]
        send_sems, recv_sems, local_sems = refs[2 * n:]
        x, y, c = _mesh_position()
        me, sibling = (x, y, c), (x, y, 1 - c)
        first = (jnp.where(c == 0, 1 - x, x), jnp.where(c == 0, y, 1 - y), c)
        second = (jnp.where(c == 0, x, 1 - x), jnp.where(c == 0, 1 - y, y), c)
        diagonal = (1 - x, 1 - y, c)

        def copy(w, k, block, to, src=None):
            px, py, pc = block
            rows = outs[w].at[4 * px + 2 * py + pc]
            return pltpu.make_async_remote_copy(
                src_ref=rows if src is None else src, dst_ref=rows,
                send_sem=send_sems.at[7 * w + k], recv_sem=recv_sems.at[7 * w + k],
                device_id=to, device_id_type=MESH)

        started = []
        for w in range(n):
            mine = pltpu.make_async_copy(ins[w], outs[w].at[4 * x + 2 * y + c], local_sems.at[w])
            mine.start()
            started.append(mine)
        sends = []
        for w in range(n):
            own = [copy(w, 0, me, sibling, src=ins[w]), copy(w, 1, me, first, src=ins[w]),
                   copy(w, 2, me, second, src=ins[w])]
            for cp in own:
                cp.start()
            sends += own
        for w in range(n):
            copy(w, 1, first, me).wait_recv()
            follow = [copy(w, 3, first, second), copy(w, 4, first, sibling)]
            for cp in follow:
                cp.start()
            copy(w, 2, second, me).wait_recv()
            follow.append(copy(w, 5, second, sibling))
            follow[-1].start()
            sends += follow
        for w in range(n):
            copy(w, 3, diagonal, me).wait_recv()
            passed = copy(w, 6, diagonal, sibling)
            passed.start()
            sends.append(passed)
        for w in range(n):
            for k in (0, 4, 5, 6):
                copy(w, k, sibling, me).wait_recv()
        for cp in sends:
            cp.wait_send()
        for mine in started:
            mine.wait()

    return pl.pallas_call(
        body, name="weights_all_gather",
        in_specs=[_any_spec()] * n, out_specs=[_any_spec()] * n,
        out_shape=[jax.ShapeDtypeStruct((N_DEV,) + s.shape, s.dtype) for s in shards],
        scratch_shapes=[pltpu.SemaphoreType.DMA((7 * n,)), pltpu.SemaphoreType.DMA((7 * n,)),
                        pltpu.SemaphoreType.DMA((n,))],
    )(*shards)


def _chip_of(j, x, y):
    return (1 - x if j & 1 else x, 1 - y if j & 2 else y)


_HBM_SPEC = pl.BlockSpec(memory_space=pltpu.HBM)
_SEM_SPEC = pl.BlockSpec(memory_space=pltpu.SEMAPHORE)
_DATAFLOW_EFFECT = pltpu.SideEffectType.DATAFLOW_SIDE_EFFECTING
TOKEN_SHAPE = (8, D_MODEL)


def _copies_start(name, arrays, n_copies, plan):
    n = len(arrays)

    def body(*refs):
        send_sems, recv_sems, token = refs[n], refs[n + 1], refs[2 * n + 2]
        copies = plan(refs[:n])
        assert len(copies) == n_copies
        for k, (src, dst, dev) in enumerate(copies):
            pltpu.make_async_remote_copy(src_ref=src, dst_ref=dst, send_sem=send_sems.at[k], recv_sem=recv_sems.at[k],
                                         device_id=dev, device_id_type=MESH).start()
        token[...] = jnp.zeros_like(token)

    outs = pl.pallas_call(
        body, name=name,
        out_shape=(pltpu.SemaphoreType.DMA((n_copies,)), pltpu.SemaphoreType.DMA((n_copies,)),
                   *[pltpu.HBM(a.shape, a.dtype) for a in arrays], jax.ShapeDtypeStruct(TOKEN_SHAPE, F32)),
        in_specs=[_HBM_SPEC] * n,
        out_specs=(_SEM_SPEC, _SEM_SPEC, *[_HBM_SPEC] * n, pl.BlockSpec(memory_space=pltpu.VMEM)),
        input_output_aliases={i: i + 2 for i in range(n)},
        compiler_params=pltpu.CompilerParams(has_side_effects=_DATAFLOW_EFFECT),
    )(*[pltpu.with_memory_space_constraint(a, pltpu.HBM) for a in arrays])
    return outs[0], outs[1], list(outs[2:2 + n]), outs[2 + n]


def _copies_wait(name, send_sems, recv_sems, arrays, n_copies, plan, after):
    n = len(arrays)
    after = list(after) if isinstance(after, (list, tuple)) else [after]

    def body(*refs):
        send_ref, recv_ref = refs[n], refs[n + 1]
        copies = plan(refs[:n])
        assert len(copies) == n_copies
        for k, (src, dst, dev) in enumerate(copies):
            cp = pltpu.make_async_remote_copy(src_ref=src, dst_ref=dst, send_sem=send_ref.at[k], recv_sem=recv_ref.at[k],
                                              device_id=dev, device_id_type=MESH)
            cp.wait_send()
            cp.wait_recv()

    outs = pl.pallas_call(
        body, name=name, out_shape=tuple(pltpu.HBM(a.shape, a.dtype) for a in arrays),
        in_specs=[_HBM_SPEC] * n + [_SEM_SPEC, _SEM_SPEC] + [pl.BlockSpec(memory_space=pl.ANY)] * len(after),
        out_specs=tuple([_HBM_SPEC] * n), input_output_aliases={i: i for i in range(n)},
        compiler_params=pltpu.CompilerParams(has_side_effects=_DATAFLOW_EFFECT),
    )(*arrays, send_sems, recv_sems, *after)
    return list(outs)


def _after(small, token):
    return small if token is None else small + token[0:1, :small.shape[-1]]


def _gather_plan_out(n):
    def plan(refs):
        x, y, c = _mesh_position()
        me = 4 * x + 2 * y + c
        copies = []
        for w in range(n):
            land = refs[n + w].at[me]
            copies.append((refs[w], land, (x, y, 1 - c)))
            for j in range(1, 4):
                copies.append((refs[w], land, (*_chip_of(j, x, y), c)))
        return copies
    return plan


def _gather_plan_pass(n):
    def plan(refs):
        x, y, c = _mesh_position()
        copies = []
        for w in range(n):
            for j in range(1, 4):
                px, py = _chip_of(j, x, y)
                rows = refs[w].at[4 * px + 2 * py + c]
                copies.append((rows, rows, (x, y, 1 - c)))
        return copies
    return plan


def _reduce_plan_sibling(n):
    def plan(refs):
        x, y, c = _mesh_position()
        copies = []
        for w in range(n):
            for j in range(4):
                px, py = _chip_of(j, x, y)
                copies.append((refs[w].at[4 * px + 2 * py + (1 - c)], refs[n + w].at[j], (x, y, 1 - c)))
        return copies
    return plan


def _reduce_plan_chips(n):
    def plan(refs):
        x, y, c = _mesh_position()
        copies = []
        for w in range(n):
            for j in range(1, 4):
                copies.append((refs[w].at[j - 1], refs[n + w].at[j - 1], (*_chip_of(j, x, y), c)))
        return copies
    return plan


def _chip_partials(grad, recv, name, tr):
    _, rows, width = grad.shape

    def body(g_ref, r_ref, own_ref, other_ref):
        x, y, c = _mesh_position()
        for j in range(4):
            px, py = _chip_of(j, x, y)
            total = g_ref[4 * px + 2 * py + c].astype(F32) + r_ref[j].astype(F32)
            if j == 0:
                own_ref[...] = total
            else:
                other_ref[j - 1] = total.astype(BF16)

    return pl.pallas_call(
        body, name=name, grid=(rows // tr,),
        in_specs=[pl.BlockSpec((N_DEV, tr, width), lambda i: (0, i, 0)),
                  pl.BlockSpec((4, tr, width), lambda i: (0, i, 0))],
        out_specs=[pl.BlockSpec((tr, width), lambda i: (i, 0)), pl.BlockSpec((3, tr, width), lambda i: (0, i, 0))],
        out_shape=[jax.ShapeDtypeStruct((rows, width), F32), jax.ShapeDtypeStruct((3, rows, width), BF16)],
        compiler_params=_params(("parallel",), 2 * 20 * tr * width * 2 + 8 * tr * width * 4),
    )(grad, recv)


def _adamw_math(w, g, m, v):
    m = ADAM_B1 * m + (1.0 - ADAM_B1) * g
    v = ADAM_B2 * v + (1.0 - ADAM_B2) * (g * g)
    m_hat = m / (1.0 - ADAM_B1 ** ADAM_STEP)
    v_hat = v / (1.0 - ADAM_B2 ** ADAM_STEP)
    delta = -ADAM_LR * (m_hat / (jnp.sqrt(v_hat) + ADAM_EPS) + ADAM_WD * w)
    return delta, m, v


def _adamw_shard(own, recv, w, m, v, name, tr):
    rows, width = own.shape

    def body(own_ref, r_ref, w_ref, m_ref, v_ref, g_ref, d_ref, nm_ref, nv_ref):
        g = own_ref[...]
        for j in range(3):
            g = g + r_ref[j].astype(F32)
        g_ref[...] = g
        d_ref[...], nm_ref[...], nv_ref[...] = _adamw_math(w_ref[...], g, m_ref[...], v_ref[...])

    spec = pl.BlockSpec((tr, width), lambda i: (i, 0))
    shape = jax.ShapeDtypeStruct((rows, width), F32)
    return pl.pallas_call(
        body, name=name, grid=(rows // tr,),
        in_specs=[spec, pl.BlockSpec((3, tr, width), lambda i: (0, i, 0)), spec, spec, spec],
        out_specs=[spec] * 4, out_shape=[shape] * 4,
        compiler_params=_params(("parallel",), 22 * tr * width * 4),
    )(own, recv, w, m, v)


def _small_all_reduce_adamw(gpart, lbpack, wpack, mpack, vpack, after):
    def body(gp_ref, lb_ref, w_ref, m_ref, v_ref, after_ref, g_ref, d_ref, nm_ref, nv_ref, gath_ref, send_sems, recv_sems):
        del after_ref
        x, y, c = _mesh_position()
        me = 4 * x + 2 * y + c
        gath_ref[me] = gp_ref[...]
        copies = []
        for j in range(1, N_DEV):
            peer = (x ^ (j >> 2), y ^ ((j >> 1) & 1), c ^ (j & 1))
            cp = pltpu.make_async_remote_copy(
                src_ref=gp_ref, dst_ref=gath_ref.at[me], send_sem=send_sems.at[j - 1], recv_sem=recv_sems.at[j - 1],
                device_id=peer, device_id_type=MESH)
            cp.start()
            copies.append(cp)
        for cp in copies:
            cp.wait()
        tot = gath_ref[0]
        for s in range(1, N_DEV):
            tot = tot + gath_ref[s]
        lb = lb_ref[...]
        dl = tot[16:32, :] * lb * (1.0 - lb)
        g = jnp.concatenate([tot[0:16, :], dl[0:8, :], -dl[0:8, :], dl[8:16, :], -dl[8:16, :],
                             tot[32:SMALL_GRAD_ROWS, :]], axis=0)
        g_ref[...] = g
        d_ref[...], nm_ref[...], nv_ref[...] = _adamw_math(w_ref[...], g, m_ref[...], v_ref[...])

    vm = pl.BlockSpec(memory_space=pltpu.VMEM)
    shape = jax.ShapeDtypeStruct((SMALL_ROWS, LANE), F32)
    return pl.pallas_call(
        body, name="small_all_reduce_adamw", in_specs=[vm] * 5 + [_any_spec()], out_specs=[vm] * 4,
        out_shape=[shape] * 4,
        scratch_shapes=[pltpu.VMEM((N_DEV, SMALL_GRAD_ROWS, LANE), F32),
                        pltpu.SemaphoreType.DMA((N_DEV - 1,)), pltpu.SemaphoreType.DMA((N_DEV - 1,))],
    )(gpart, lbpack, wpack, mpack, vpack, after)


_SMALL_NAMES = ("norm_mix_gain", "norm_mem_gain", "lb_logits_fw", "lb_logits_bw", "norm_ffn_gain",
                "hg_norm_gain", "da_q_gain", "da_k_gain", "mem_q_gain", "mem_k_gain")
_SMALL_ROW0 = {"norm_mix_gain": 0, "norm_mem_gain": 8, "lb_logits_fw": 16, "lb_logits_bw": 32, "norm_ffn_gain": 48,
               "hg_norm_gain": 56, "da_q_gain": 64, "da_k_gain": 72, "mem_q_gain": 80, "mem_k_gain": 88}
_LOSS_ROW = 96


def _pack_rows(parts, total_rows):
    rows = []
    for p in parts:
        r = p.reshape(-1, LANE)
        rows.append(jnp.pad(r, ((0, -r.shape[0] % 8), (0, 0))))
    used = sum(r.shape[0] for r in rows)
    if total_rows > used:
        rows.append(jnp.zeros((total_rows - used, LANE), F32))
    return jnp.concatenate(rows, axis=0)


def _unpack_small(pack, like):
    out = {}
    for name in _SMALL_NAMES:
        n = like[name].size // LANE
        r0 = _SMALL_ROW0[name]
        out[name] = pack[r0:r0 + n].reshape(like[name].shape)
    return out


def kernel(x, mem, norm_mix_gain, norm_mem_gain, w_in, lb_logits_fw, lb_logits_bw, hg_norm_gain, da_q_gain, da_k_gain, w_mem_kv, mem_q_gain, mem_k_gain, w_proj_hg, w_proj_da, w_proj_mem, w_out, norm_ffn_gain, w_ffn_in, w_ffn_out, loss_target, m_norm_mix_gain, m_norm_mem_gain, m_w_in, m_lb_logits_fw, m_lb_logits_bw, m_hg_norm_gain, m_da_q_gain, m_da_k_gain, m_w_mem_kv, m_mem_q_gain, m_mem_k_gain, m_w_proj_hg, m_w_proj_da, m_w_proj_mem, m_w_out, m_norm_ffn_gain, m_w_ffn_in, m_w_ffn_out, v_norm_mix_gain, v_norm_mem_gain, v_w_in, v_lb_logits_fw, v_lb_logits_bw, v_hg_norm_gain, v_da_q_gain, v_da_k_gain, v_w_mem_kv, v_mem_q_gain, v_mem_k_gain, v_w_proj_hg, v_w_proj_da, v_w_proj_mem, v_w_out, v_norm_ffn_gain, v_w_ffn_in, v_w_ffn_out):
    small_w = dict(norm_mix_gain=norm_mix_gain, norm_mem_gain=norm_mem_gain, lb_logits_fw=lb_logits_fw,
                   lb_logits_bw=lb_logits_bw, norm_ffn_gain=norm_ffn_gain, hg_norm_gain=hg_norm_gain,
                   da_q_gain=da_q_gain, da_k_gain=da_k_gain, mem_q_gain=mem_q_gain, mem_k_gain=mem_k_gain)
    small_m = dict(norm_mix_gain=m_norm_mix_gain, norm_mem_gain=m_norm_mem_gain, lb_logits_fw=m_lb_logits_fw,
                   lb_logits_bw=m_lb_logits_bw, norm_ffn_gain=m_norm_ffn_gain, hg_norm_gain=m_hg_norm_gain,
                   da_q_gain=m_da_q_gain, da_k_gain=m_da_k_gain, mem_q_gain=m_mem_q_gain, mem_k_gain=m_mem_k_gain)
    small_v = dict(norm_mix_gain=v_norm_mix_gain, norm_mem_gain=v_norm_mem_gain, lb_logits_fw=v_lb_logits_fw,
                   lb_logits_bw=v_lb_logits_bw, norm_ffn_gain=v_norm_ffn_gain, hg_norm_gain=v_hg_norm_gain,
                   da_q_gain=v_da_q_gain, da_k_gain=v_da_k_gain, mem_q_gain=v_mem_q_gain, mem_k_gain=v_mem_k_gain)
    big_names = ("w_in", "w_mem_kv", "w_proj_hg", "w_proj_da", "w_proj_mem", "w_out", "w_ffn_in", "w_ffn_out")
    big_w = dict(w_in=w_in[0], w_mem_kv=w_mem_kv[0], w_proj_hg=w_proj_hg[0], w_proj_da=w_proj_da[0],
                 w_proj_mem=w_proj_mem[0], w_out=w_out[0], w_ffn_in=w_ffn_in[0], w_ffn_out=w_ffn_out[0])
    big_m = dict(w_in=m_w_in[0], w_mem_kv=m_w_mem_kv[0], w_proj_hg=m_w_proj_hg[0], w_proj_da=m_w_proj_da[0],
                 w_proj_mem=m_w_proj_mem[0], w_out=m_w_out[0], w_ffn_in=m_w_ffn_in[0], w_ffn_out=m_w_ffn_out[0])
    big_v = dict(w_in=v_w_in[0], w_mem_kv=v_w_mem_kv[0], w_proj_hg=v_w_proj_hg[0], w_proj_da=v_w_proj_da[0],
                 w_proj_mem=v_w_proj_mem[0], w_out=v_w_out[0], w_ffn_in=v_w_ffn_in[0], w_ffn_out=v_w_ffn_out[0])

    row_tile = dict(w_in=128, w_mem_kv=128, w_proj_hg=128, w_proj_da=512, w_proj_mem=512, w_out=128,
                    w_ffn_in=128, w_ffn_out=352)
    rest = _BIG_NAMES[1:]
    shards = [big_w[n].astype(BF16) for n in rest]
    state = {}
    big_out = {}

    def reduce_start(group, stacked):
        names = tuple(stacked)
        arrays = [stacked[n] for n in names] + [lax.empty((4,) + stacked[n].shape[1:], BF16) for n in names]
        plan = _reduce_plan_sibling(len(names))
        send, recv, thru, token = _copies_start(f"grads_{group}_sibling_start", arrays, 4 * len(names), plan)
        state[group] = dict(names=names, plan=plan, send=send, recv=recv, arrays=thru)
        return token

    def reduce_middle(group, after):
        st = state[group]
        names, k = st["names"], len(st["names"])
        thru = _copies_wait(f"grads_{group}_sibling_wait", st["send"], st["recv"], st["arrays"], 4 * k, st["plan"], after)
        partials = [_chip_partials(thru[i], thru[k + i], f"chip_partials_{n}", row_tile[n]) for i, n in enumerate(names)]
        arrays = [p[1] for p in partials] + [lax.empty(p[1].shape, BF16) for p in partials]
        plan = _reduce_plan_chips(k)
        send, recv, thru2, token = _copies_start(f"grads_{group}_chips_start", arrays, 3 * k, plan)
        state[group] = dict(names=names, plan=plan, send=send, recv=recv, arrays=thru2, own=[p[0] for p in partials])
        return token

    def reduce_finish(group, after):
        st = state.pop(group)
        names, k = st["names"], len(st["names"])
        thru = _copies_wait(f"grads_{group}_chips_wait", st["send"], st["recv"], st["arrays"], 3 * k, st["plan"], after)
        for i, n in enumerate(names):
            big_out[n] = _adamw_shard(st["own"][i], thru[k + i], big_w[n], big_m[n], big_v[n], "adamw_" + n, row_tile[n])

    win_st = _all_gather([big_w["w_in"].astype(BF16)])[0]
    step = _local_step_stages(x[0], mem[0], loss_target[0], small_w, win_st)
    event, payload = next(step)
    local = None
    while True:
        reply = None
        if event == "begin":
            nr = len(rest)
            me = 4 * lax.axis_index("x") + 2 * lax.axis_index("y") + lax.axis_index("c")
            arrays = shards + [lax.dynamic_update_slice(lax.empty((N_DEV,) + s.shape, BF16), s[None], (me, 0, 0))
                               for s in shards]
            plan = _gather_plan_out(nr)
            send, recv, thru, reply = _copies_start("weights_rest_out_start", arrays, 4 * nr, plan)
            state["gather"] = dict(plan=plan, send=send, recv=recv, arrays=thru)
        elif event == "after_gla_fwd":
            st = state["gather"]
            nr = len(rest)
            thru = _copies_wait("weights_rest_out_wait", st["send"], st["recv"], st["arrays"], 4 * nr, st["plan"], payload)
            plan = _gather_plan_pass(nr)
            send, recv, lands, reply = _copies_start("weights_rest_pass_start", thru[nr:], 3 * nr, plan)
            state["gather"] = dict(plan=plan, send=send, recv=recv, arrays=lands)
        elif event == "need_weights":
            st = state.pop("gather")
            nr = len(rest)
            lands = _copies_wait("weights_rest_pass_wait", st["send"], st["recv"], st["arrays"], 3 * nr, st["plan"], payload)
            reply = dict(zip(rest, lands))
        elif event == "grads_ffn":
            reply = reduce_start("ffn", payload)
        elif event == "after_gate_merge_bwd":
            reply = reduce_middle("ffn", payload)
        elif event == "grads_mix":
            reply = reduce_start("mix", payload)
        elif event == "after_da_bwd_g0":
            reply = reduce_middle("mix", payload)
            reduce_finish("ffn", payload)
        elif event == "after_gla_bwd":
            reduce_finish("mix", payload)
        elif event == "grads_in":
            reply = reduce_start("in", payload)
        elif event == "after_proj_bwd_act_top":
            reply = reduce_middle("in", payload)
        elif event == "end":
            local = payload
            reduce_finish("in", [local["grad_x"]] + [big_out[n][0] for n in rest])
            break
        event, payload = step.send(reply)
    grad_x = local["grad_x"]

    wpack = _pack_rows([small_w[n] for n in _SMALL_NAMES], SMALL_ROWS)
    mpack = _pack_rows([small_m[n] for n in _SMALL_NAMES], SMALL_ROWS)
    vpack = _pack_rows([small_v[n] for n in _SMALL_NAMES], SMALL_ROWS)
    gs, ds, ms, vs = _small_all_reduce_adamw(local["gpart"], local["lbpack"], wpack, mpack, vpack, big_out["w_in"][0])
    loss = gs[_LOSS_ROW, 0]
    small_out = [_unpack_small(t, small_w) for t in (gs, ds, ms, vs)]

    order = ("norm_mix_gain", "norm_mem_gain", "w_in", "lb_logits_fw", "lb_logits_bw", "hg_norm_gain", "da_q_gain",
             "da_k_gain", "w_mem_kv", "mem_q_gain", "mem_k_gain", "w_proj_hg", "w_proj_da", "w_proj_mem", "w_out",
             "norm_ffn_gain", "w_ffn_in", "w_ffn_out")
    outs = [loss, grad_x[None]]
    for kind in range(4):
        for n in order:
            outs.append(big_out[n][kind][None] if n in big_out else small_out[kind][n])
    return tuple(outs)


_BIG_NAMES = ("w_in", "w_mem_kv", "w_proj_hg", "w_proj_da", "w_proj_mem", "w_out", "w_ffn_in", "w_ffn_out")


def _local_step(xs, mems, target, sw, wg):
    step = _local_step_stages(xs, mems, target, sw, wg["w_in"])
    stacked = {}
    event, payload = next(step)
    while event != "end":
        if event.startswith("grads_"):
            stacked.update(payload)
        event, payload = step.send(wg if event == "need_weights" else None)
    return dict(payload, stacked=stacked)


def _local_step_stages(xs, mems, target, sw, win_st):
    norm_mix_gain, norm_mem_gain, norm_ffn_gain = sw["norm_mix_gain"], sw["norm_mem_gain"], sw["norm_ffn_gain"]
    lb_logits_fw, lb_logits_bw, hg_norm_gain = sw["lb_logits_fw"], sw["lb_logits_bw"], sw["hg_norm_gain"]
    da_q_gain, da_k_gain, mem_q_gain, mem_k_gain = sw["da_q_gain"], sw["da_k_gain"], sw["mem_q_gain"], sw["mem_k_gain"]

    token = yield "begin", None
    lb_fw = _lb_table(lb_logits_fw, "lb_table_fw")
    lb_bw = _lb_table(lb_logits_bw, "lb_table_bw")
    lb = jnp.concatenate([lb_fw, lb_bw], axis=0).reshape(2, HG_HEADS, 1, HEAD_DIM)
    h, h_t = _rmsnorm_fwd(xs, _after(norm_mix_gain, token), "norm_mix_fwd", 512, transposed=True)
    proj = _matmul(h, win_st, "nn", F32, 1024, IN_SHARD, D_MODEL, "proj_fwd", b_stacked=True, n_outer=True)
    o_hg, o_pre, states, hg_qt, hg_decay = _gla_fwd(proj, lb, hg_norm_gain)
    token = yield "after_gla_fwd", o_hg
    da = [_da_fwd(proj, _after(da_q_gain, token), da_k_gain, g) for g in range(3)]
    o_da, o_da32, lse_da = _da_merge([t[0] for t in da], [t[1] for t in da], 512)
    wg = yield "need_weights", o_da
    wkv = wg["w_mem_kv"].reshape(D_MODEL, 2 * MEM_WIDTH)
    wphg = wg["w_proj_hg"].reshape(D_MODEL, D_MODEL)
    wpda = jnp.transpose(wg["w_proj_da"], (1, 0, 2)).reshape(DA_WIDTH, D_MODEL)
    wpmem = jnp.transpose(wg["w_proj_mem"], (1, 0, 2)).reshape(MEM_WIDTH, D_MODEL)
    wout = wg["w_out"].reshape(D_MODEL, D_MODEL)
    wfin = jnp.transpose(wg["w_ffn_in"], (1, 0, 2)).reshape(D_MODEL, 2 * D_FF)
    wfout = wg["w_ffn_out"].reshape(D_FF, D_MODEL)
    mem_n = _rmsnorm_fwd(mems, norm_mem_gain, "norm_mem_fwd", N_MEM)
    kv = _matmul(mem_n, wkv, "nn", F32, N_MEM, 1024, D_MODEL, "mem_kv_fwd")
    o_mem = _mem_fwd(proj, kv, mem_q_gain, mem_k_gain, 1024)
    t_hg = _matmul(o_hg, wphg, "nn", F32, 512, 1024, D_MODEL, "proj_hg_fwd")
    t_da = _matmul(o_da, wpda, "nn", F32, 512, 1024, DA_WIDTH, "proj_da_fwd")
    t_mem = _matmul(o_mem, wpmem, "nn", F32, 512, 1024, MEM_WIDTH, "proj_mem_fwd")
    merged = _gate_merge_fwd(proj, t_hg, t_da, t_mem, 256)
    x1, h2, h2_t = _residual_rmsnorm_fwd(xs, merged, wout, norm_ffn_gain, "out_norm_ffn_fwd", 512)
    ffn_a, ffn_b, act, act_t = _ffn_in_swiglu(h2, wfin, 1024, 1408)
    dy, dyb, loss_part = _ffn_out_loss_head(x1, act, wfout, target, 512)

    dab = _ffn_out_bwd_swiglu(dyb, wfout, ffn_a, ffn_b, 1024, 1408)
    g_wfout = _matmul(act_t, dyb, "nn", BF16, 1408, 1024, 2048, "ffn_out_bwd_w")
    g_wfin = _matmul(h2_t, dab, "nn", BF16, 1024, 1408, 2048, "ffn_in_bwd_w", b_parts=True)
    token = yield "grads_ffn", dict(
        w_ffn_in=jnp.transpose(g_wfin.reshape(D_MODEL, N_DEV, 2 * D_FF // N_DEV), (1, 0, 2)),
        w_ffn_out=g_wfout.reshape(N_DEV, D_FF // N_DEV, D_MODEL))
    dx1, dx1b, g_norm_ffn = _matmul_rmsnorm_bwd(dab, wfin, x1, dy, _after(norm_ffn_gain, token),
                                                "ffn_in_bwd_act_norm", 512, D_FF, a_parts=True)
    dmerged = _matmul(dx1b, wout, "nt", F32, 512, 1024, D_MODEL, "out_bwd_act")
    g_wout = _matmul(merged, dx1b, "tn", BF16, 1024, 1024, 2048, "out_bwd_w")
    dt_hg, dt_da, dt_mem, dproj = _gate_merge_bwd(proj, t_hg, t_da, t_mem, dmerged, _dproj_buffer(), 256)
    token = yield "after_gate_merge_bwd", dt_hg
    do_hg = _matmul(dt_hg, wphg, "nt", F32, 512, 1024, D_MODEL, "proj_hg_bwd_act", after=token)
    g_wphg = _matmul(o_hg, dt_hg, "tn", BF16, 1024, 1024, 2048, "proj_hg_bwd_w")
    do_da = _matmul(dt_da, wpda, "nt", F32, 512, DA_WIDTH, D_MODEL, "proj_da_bwd_act")
    g_wpda = _matmul(o_da, dt_da, "tn", BF16, DA_WIDTH, D_MODEL, 2048, "proj_da_bwd_w")
    do_mem = _matmul(dt_mem, wpmem, "nt", F32, 512, MEM_WIDTH, D_MODEL, "proj_mem_bwd_act")
    g_wpmem = _matmul(o_mem, dt_mem, "tn", BF16, MEM_WIDTH, D_MODEL, 2048, "proj_mem_bwd_w")
    by_owner = lambda g: jnp.transpose(g.reshape(g.shape[0], N_DEV, D_MODEL // N_DEV), (1, 0, 2))
    g_wpda, g_wpmem = by_owner(g_wpda), by_owner(g_wpmem)

    dproj, dk_mem, dv_mem, g_mem_q, g_mem_k = _mem_bwd(proj, kv, mem_q_gain, mem_k_gain, do_mem, dproj, 1024)
    dkv = jnp.concatenate([dk_mem, dv_mem], axis=1).astype(BF16)
    g_wkv = _matmul(mem_n, dkv, "tn", BF16, 1024, 1024, N_MEM, "mem_kv_bwd_w")
    dmem_n = _matmul(dkv, wkv, "nt", F32, N_MEM, 1024, 1024, "mem_kv_bwd_act")
    g_norm_mem = _gain_grad(mems, dmem_n, "norm_mem_bwd")
    token = yield "grads_mix", dict(
        w_mem_kv=g_wkv.reshape(N_DEV, D_MODEL // N_DEV, 2 * MEM_WIDTH),
        w_proj_hg=g_wphg.reshape(N_DEV, D_MODEL // N_DEV, D_MODEL),
        w_proj_da=g_wpda, w_proj_mem=g_wpmem,
        w_out=g_wout.reshape(N_DEV, D_MODEL // N_DEV, D_MODEL))

    dd_da = _da_rowdot(do_da, o_da32, 512)
    dproj, g_da_q, g_da_k = _da_bwd(proj, _after(da_q_gain, token), da_k_gain, do_da, lse_da, dd_da, dproj, 0)
    token = yield "after_da_bwd_g0", g_da_q
    for g in (1, 2):
        dproj, gq_part, gk_part = _da_bwd(proj, _after(da_q_gain, token), da_k_gain, do_da, lse_da, dd_da, dproj, g)
        g_da_q, g_da_k = g_da_q + gq_part, g_da_k + gk_part

    dproj, dlb, g_hg_norm = _gla_bwd(proj, lb, hg_norm_gain, o_pre, states, hg_qt, hg_decay, do_hg, dproj)
    yield "after_gla_bwd", dlb

    g_win =_matmul(h_t, dproj, "nn", BF16, 1024, IN_SHARD, 2048, "proj_bwd_w", out_stacked=True)
    token = yield "grads_in", dict(w_in=g_win)
    grad_x, g_mix_top = _matmul_rmsnorm_bwd(dproj, win_st, xs, dx1, norm_mix_gain, "proj_bwd_act_norm_top", 512,
                                            IN_SHARD, b_stacked=True, after=token, m_blocks=(0, 4), with_bf16=False)
    token = yield "after_proj_bwd_act_top", g_mix_top
    grad_x, g_mix_bottom = _matmul_rmsnorm_bwd(dproj, win_st, xs, dx1, norm_mix_gain, "proj_bwd_act_norm_bottom", 512,
                                               IN_SHARD, b_stacked=True, after=token, m_blocks=(4, 4),
                                               out_into=grad_x, with_bf16=False)
    g_norm_mix = g_mix_top + g_mix_bottom

    gpart = _pack_rows([g_norm_mix, g_norm_mem, dlb[0], dlb[1], g_norm_ffn, g_hg_norm, g_da_q, g_da_k,
                        g_mem_q, g_mem_k, loss_part], SMALL_GRAD_ROWS)
    lbpack = jnp.concatenate([lb_fw.reshape(8, LANE), lb_bw.reshape(8, LANE)], axis=0)
    yield "end", dict(grad_x=grad_x, gpart=gpart, lbpack=lbpack)
```

```python
import numpy as np
import jax
import jax.numpy as jnp
from jax import lax
from jax.experimental import pallas as pl
from jax.experimental.pallas import tpu as pltpu

F32 = jnp.float32
BF16 = jnp.bfloat16
MESH = pl.DeviceIdType.MESH

SEQ = 4096
D_MODEL = 1024
N_DEV = 8
N_MEM = 256
RMS_EPS = 1e-6
NEG_INF = -1e30
LANE = 128
HEAD_DIM = 128
HG_HEADS = 8
HG_CHUNK = 64
HG_SCALE = HEAD_DIM ** -0.5
DA_DILATIONS = (1, 4, 16)
DA_RADIUS = 64
DA_HEADS_PER_GROUP = 4
DA_HEADS = 12
DA_WIDTH = 512
DA_SCALE = HEAD_DIM ** -0.5
DA_QB = 128
DA_WIN = 256
DA_WAYS = 4
DA_FWD_WAYS = 4
MEM_HEADS = 4
MEM_WIDTH = 512
MEM_SCALE = HEAD_DIM ** -0.5
D_FF = 2816
IN_COLS = 13312
IN_SHARD = IN_COLS // N_DEV
CB_HG_Q, CB_F, CB_HG_I, CB_HG_G = 0, 8, 24, 32
CB_DA_Q, CB_DA_K, CB_DA_V, CB_MEM_Q = 40, 52, 64, 76
ADAM_LR, ADAM_B1, ADAM_B2, ADAM_EPS, ADAM_WD, ADAM_STEP = 0.001, 0.9, 0.999, 1e-08, 0.01, 10
VMEM_BYTES_V7X = 64 * 1024 * 1024
SMALL_ROWS = 104
SMALL_GRAD_ROWS = 88

_NN = (((1,), (0,)), ((), ()))
_NT = (((1,), (1,)), ((), ()))
_TN = (((0,), (0,)), ((), ()))


def _dot(a, b, dims):
    return lax.dot_general(a.astype(BF16), b.astype(BF16), dims, preferred_element_type=F32)


def _sigmoid(x):
    return 0.5 * jnp.tanh(0.5 * x) + 0.5


def _params(semantics, est_bytes):
    limit = int(min(VMEM_BYTES_V7X - (6 << 20), max(56 << 20, est_bytes * 3 // 2)))
    return pltpu.CompilerParams(dimension_semantics=semantics, vmem_limit_bytes=limit)


def _nbytes(shape, dtype):
    return int(np.prod(shape)) * jnp.dtype(dtype).itemsize


def _alibi_slopes(n):
    return (2.0 ** (-8.0 * np.arange(1, n + 1) / n)).astype(np.float32)


def _matmul(a, b, mode, out_dtype, tm, tn, tk, name, b_stacked=False, out_stacked=False, n_outer=False, after=None,
            m_blocks=None, out_into=None, a_parts=False, b_parts=False):
    if a_parts:
        assert mode == "nt" and tk == a.shape[2]
        m, kdim = a.shape[1], a.shape[0] * a.shape[2]
    elif mode == "tn":
        kdim, m = a.shape
    else:
        m, kdim = a.shape
    if b_parts:
        assert mode == "nn" and not b_stacked and b.shape[2] % tn == 0
        n = b.shape[0] * b.shape[2]
    elif b_stacked:
        if mode == "nn":
            n = b.shape[0] * b.shape[2]
            assert tn == b.shape[2] and tk == kdim == b.shape[1]
        else:
            assert mode == "nt" and tk == b.shape[2] and b.shape[0] * tk == kdim
            n = b.shape[1]
    else:
        n = b.shape[0] if mode == "nt" else b.shape[1]
    assert m % tm == 0 and n % tn == 0 and kdim % tk == 0
    gm, gn, gk = m // tm, n // tn, kdim // tk
    i0 = 0
    if m_blocks is not None:
        assert mode != "tn" and not out_stacked
        i0, gm = m_blocks

    def ijk(f):
        if n_outer:
            return lambda j, i, k: f(i + i0, j, k)
        return lambda i, j, k: f(i + i0, j, k)

    if a_parts:
        a_spec = pl.BlockSpec((None, tm, tk), ijk(lambda i, j, k: (k, i, 0)))
    elif mode == "tn":
        a_spec = pl.BlockSpec((tk, tm), ijk(lambda i, j, k: (k, i)))
    else:
        a_spec = pl.BlockSpec((tm, tk), ijk(lambda i, j, k: (i, k)))
    if b_parts:
        per_part = b.shape[2] // tn
        b_spec = pl.BlockSpec((None, tk, tn), ijk(lambda i, j, k: (j // per_part, k, j % per_part)))
    elif b_stacked and mode == "nn":
        b_spec = pl.BlockSpec((None, tk, tn), ijk(lambda i, j, k: (j, 0, 0)))
    elif b_stacked:
        b_spec = pl.BlockSpec((None, tn, tk), ijk(lambda i, j, k: (k, j, 0)))
    elif mode == "nt":
        b_spec = pl.BlockSpec((tn, tk), ijk(lambda i, j, k: (j, k)))
    else:
        b_spec = pl.BlockSpec((tk, tn), ijk(lambda i, j, k: (k, j)))
    if out_stacked:
        assert tm == m
        out_shape = jax.ShapeDtypeStruct((gn, m, tn), out_dtype)
        o_spec = pl.BlockSpec((None, tm, tn), ijk(lambda i, j, k: (j, i, 0)))
    else:
        out_shape = jax.ShapeDtypeStruct((m, n), out_dtype)
        o_spec = pl.BlockSpec((tm, tn), ijk(lambda i, j, k: (i, j)))
    dims = {"nn": _NN, "nt": _NT, "tn": _TN}[mode]

    n_in = 2 + (after is not None) + (out_into is not None)

    def body(*refs):
        a_ref, b_ref, o_ref = refs[0], refs[1], refs[n_in]
        part = _dot(a_ref[...], b_ref[...], dims)
        if gk == 1:
            o_ref[...] = part.astype(out_dtype)
            return
        acc_ref = refs[-1]
        k = pl.program_id(2)

        @pl.when(k == 0)
        def _():
            acc_ref[...] = part

        @pl.when(jnp.logical_and(k > 0, k < gk - 1))
        def _():
            acc_ref[...] += part

        @pl.when(k == gk - 1)
        def _():
            o_ref[...] = (acc_ref[...] + part).astype(out_dtype)

    a_tile = _nbytes((tm, tk), a.dtype)
    b_tile = _nbytes((tk, tn), b.dtype)
    o_tile = _nbytes((tm, tn), out_dtype)
    est = 2 * (a_tile + b_tile + o_tile) + 3 * tm * tn * 4 + (a_tile + b_tile)
    grid = (gn, gm, gk) if n_outer else (gm, gn, gk)
    operands, in_specs = [a, b], [a_spec, b_spec]
    if after is not None:
        operands.append(after)
        in_specs.append(pl.BlockSpec(memory_space=pl.ANY))
    aliases = {}
    if out_into is not None:
        aliases = {len(operands): 0}
        operands.append(out_into)
        in_specs.append(pl.BlockSpec(memory_space=pl.ANY))
    return pl.pallas_call(
        body, name=name, grid=grid, in_specs=in_specs, out_specs=o_spec, out_shape=out_shape,
        scratch_shapes=[] if gk == 1 else [pltpu.VMEM((tm, tn), F32)], input_output_aliases=aliases,
        compiler_params=_params(("parallel", "parallel", "arbitrary"), est),
    )(*operands)


def _row_spec(tr, width, col_block=0):
    return pl.BlockSpec((tr, width), lambda i: (i, col_block))


def _bcast_spec(width):
    return pl.BlockSpec((1, width), lambda i: (0, 0))


def _col_spec(width, tr):
    return pl.BlockSpec((width, tr), lambda i: (0, i))


def _rmsnorm_fwd(x, gain, name, tr, transposed=False):
    rows, width = x.shape

    def body(x_ref, g_ref, o_ref, *t_ref):
        xv = x_ref[...]
        r = lax.rsqrt(jnp.mean(xv * xv, axis=-1, keepdims=True) + RMS_EPS)
        h = xv * r * g_ref[...]
        o_ref[...] = h.astype(BF16)
        if transposed:
            t_ref[0][...] = h.T.astype(BF16)

    out_specs, out_shape = [_row_spec(tr, width)], [jax.ShapeDtypeStruct((rows, width), BF16)]
    if transposed:
        out_specs.append(_col_spec(width, tr))
        out_shape.append(jax.ShapeDtypeStruct((width, rows), BF16))
    out = pl.pallas_call(
        body, name=name, grid=(rows // tr,), in_specs=[_row_spec(tr, width), _bcast_spec(width)],
        out_specs=out_specs, out_shape=out_shape,
        compiler_params=_params(("parallel",), 10 * tr * width * 4),
    )(x, gain)
    return out if transposed else out[0]


def _residual_rmsnorm_fwd(x, merged, wout, gain, name, tr):
    rows, width = x.shape

    def body(x_ref, m_ref, w_ref, g_ref, x1_ref, h_ref, ht_ref):
        xv = x_ref[...] + _dot(m_ref[...], w_ref[...], _NN)
        x1_ref[...] = xv
        r = lax.rsqrt(jnp.mean(xv * xv, axis=-1, keepdims=True) + RMS_EPS)
        h = xv * r * g_ref[...]
        h_ref[...] = h.astype(BF16)
        ht_ref[...] = h.T.astype(BF16)

    return pl.pallas_call(
        body, name=name, grid=(rows // tr,),
        in_specs=[_row_spec(tr, width), _row_spec(tr, merged.shape[1]),
                  pl.BlockSpec(wout.shape, lambda i: (0, 0)), _bcast_spec(width)],
        out_specs=[_row_spec(tr, width), _row_spec(tr, width), _col_spec(width, tr)],
        out_shape=[jax.ShapeDtypeStruct((rows, width), F32), jax.ShapeDtypeStruct((rows, width), BF16),
                   jax.ShapeDtypeStruct((width, rows), BF16)],
        compiler_params=_params(("parallel",), 14 * tr * width * 4),
    )(x, merged, wout, gain)


def _matmul_rmsnorm_bwd(a, b, x, dres, gain, name, tm, tk, a_parts=False, b_stacked=False, after=None,
                        m_blocks=None, out_into=None, with_bf16=True):
    width = x.shape[1]
    m = a.shape[1] if a_parts else a.shape[0]
    kdim = a.shape[0] * a.shape[2] if a_parts else a.shape[1]
    gk = kdim // tk
    i0, gm = (0, m // tm) if m_blocks is None else m_blocks
    n_in = 5 + (after is not None) + (out_into is not None)

    def body(*refs):
        a_ref, b_ref, x_ref, dres_ref, g_ref = refs[:5]
        outs = refs[n_in:]
        dx_ref, dg_ref, acc_ref = outs[0], outs[-2], outs[-1]
        i, k = pl.program_id(0), pl.program_id(1)
        part = _dot(a_ref[...], b_ref[...], _NT)

        @pl.when(k == 0)
        def _():
            acc_ref[...] = part

        @pl.when(jnp.logical_and(k > 0, k < gk - 1))
        def _():
            acc_ref[...] += part

        @pl.when(k == gk - 1)
        def _():
            dhv = acc_ref[...] + part if gk > 1 else part
            xv = x_ref[...]
            r = lax.rsqrt(jnp.mean(xv * xv, axis=-1, keepdims=True) + RMS_EPS)
            xhat = xv * r
            dyg = dhv * g_ref[...]
            dx = dres_ref[...] + r * (dyg - xhat * jnp.mean(dyg * xhat, axis=-1, keepdims=True))
            dx_ref[...] = dx
            if with_bf16:
                outs[1][...] = dx.astype(BF16)
            gpart = jnp.sum(dhv * xhat, axis=0, keepdims=True)

            @pl.when(i == 0)
            def _():
                dg_ref[...] = gpart

            @pl.when(i > 0)
            def _():
                dg_ref[...] += gpart

    rows = lambda width_: pl.BlockSpec((tm, width_), lambda i, k: (i + i0, 0))
    if a_parts:
        a_spec = pl.BlockSpec((None, tm, tk), lambda i, k: (k, i + i0, 0))
    else:
        a_spec = pl.BlockSpec((tm, tk), lambda i, k: (i + i0, k))
    if b_stacked:
        b_spec = pl.BlockSpec((None, width, tk), lambda i, k: (k, 0, 0))
    else:
        b_spec = pl.BlockSpec((width, tk), lambda i, k: (0, k))
    operands = [a, b, x, dres, gain]
    in_specs = [a_spec, b_spec, rows(width), rows(width), pl.BlockSpec((1, width), lambda i, k: (0, 0))]
    for extra in (after, out_into):
        if extra is not None:
            operands.append(extra)
            in_specs.append(pl.BlockSpec(memory_space=pl.ANY))
    aliases = {} if out_into is None else {len(operands) - 1: 0}
    out_specs = [rows(width)] + ([rows(width)] if with_bf16 else []) + [pl.BlockSpec((1, width), lambda i, k: (0, 0))]
    out_shape = [jax.ShapeDtypeStruct((m, width), F32)] + ([jax.ShapeDtypeStruct((m, width), BF16)] if with_bf16 else [])
    out_shape.append(jax.ShapeDtypeStruct((1, width), F32))
    est = 4 * tm * tk + 4 * width * tk + 12 * tm * width * 4
    return pl.pallas_call(
        body, name=name, grid=(gm, gk), in_specs=in_specs, out_specs=out_specs, out_shape=out_shape,
        scratch_shapes=[pltpu.VMEM((tm, width), F32)], input_output_aliases=aliases,
        compiler_params=_params(("arbitrary", "arbitrary"), est),
    )(*operands)


def _gain_grad(x, dh, name):
    rows, width = x.shape

    def body(x_ref, dh_ref, dg_ref):
        xv = x_ref[...]
        r = lax.rsqrt(jnp.mean(xv * xv, axis=-1, keepdims=True) + RMS_EPS)
        dg_ref[...] = jnp.sum(dh_ref[...] * xv * r, axis=0, keepdims=True)

    return pl.pallas_call(
        body, name=name, grid=(1,), in_specs=[_row_spec(rows, width), _row_spec(rows, width)],
        out_specs=_bcast_spec(width), out_shape=jax.ShapeDtypeStruct((1, width), F32),
        compiler_params=_params(("arbitrary",), 6 * rows * width * 4),
    )(x, dh)


def _lb_table(logits, name):
    slots, width = logits.shape

    def body(l_ref, o_ref):
        lv = l_ref[...]
        mx = jnp.max(lv, axis=0, keepdims=True)
        e = jnp.exp(lv - mx)
        o_ref[...] = e[0:1, :] / jnp.sum(e, axis=0, keepdims=True)

    return pl.pallas_call(
        body, name=name, grid=(1,), in_specs=[pl.BlockSpec((slots, width), lambda i: (0, 0))],
        out_specs=_bcast_spec(width), out_shape=jax.ShapeDtypeStruct((1, width), F32),
    )(logits)


def _branch_merge_fwd(proj, outs, weights, tr):
    w = D_MODEL

    def body(ghg_ref, gda_ref, gmem_ref, ohg_ref, oda_ref, omem_ref, whg_ref, wda_ref, wmem_ref,
             m_ref, thg_ref, tda_ref, tmem_ref):
        acc = None
        for g_ref, o_ref, w_ref, t_ref in ((ghg_ref, ohg_ref, whg_ref, thg_ref), (gda_ref, oda_ref, wda_ref, tda_ref),
                                           (gmem_ref, omem_ref, wmem_ref, tmem_ref)):
            t = _dot(o_ref[...], w_ref[...], _NN)
            t_ref[...] = t.astype(BF16)
            term = _sigmoid(g_ref[...]) * t
            acc = term if acc is None else acc + term
        m_ref[...] = acc.astype(BF16)

    whole = lambda a: pl.BlockSpec(a.shape, lambda i: (0, 0))
    shape = jax.ShapeDtypeStruct((SEQ, w), BF16)
    return pl.pallas_call(
        body, name="branch_merge_fwd", grid=(SEQ // tr,),
        in_specs=[_row_spec(tr, w, 10), _row_spec(tr, w, 11), _row_spec(tr, w, 12)]
        + [_row_spec(tr, o.shape[1]) for o in outs] + [whole(wt) for wt in weights],
        out_specs=[_row_spec(tr, w)] * 4, out_shape=[shape] * 4,
        compiler_params=_params(("parallel",), 20 * tr * w * 4),
    )(proj, proj, proj, *outs, *weights)


def _dproj_buffer():
    return lax.empty((SEQ, IN_COLS), BF16)


def _gate_merge_bwd(proj, ts, weights, dmerged, dproj, tr):
    w = D_MODEL
    steps = SEQ // tr
    gate_col0 = 10 * w

    def body(ghg_ref, gda_ref, gmem_ref, thg_ref, tda_ref, tmem_ref, whg_ref, wda_ref, wmem_ref, dm_ref, dproj_in,
             dthg_ref, dtda_ref, dtmem_ref, dohg_ref, doda_ref, domem_ref, dproj_ref, stage, sems):
        del dproj_in
        i = pl.program_id(0)
        slot = i % 2

        def slot_copy(s):
            rows = pl.ds(pl.multiple_of(i * tr, tr), tr)
            return pltpu.make_async_copy(stage.at[s], dproj_ref.at[rows, pl.ds(gate_col0, 3 * w)], sems.at[s])

        @pl.when(i >= 2)
        def _():
            slot_copy(slot).wait()

        dm = dm_ref[...]
        branches = ((ghg_ref, thg_ref, whg_ref, dthg_ref, dohg_ref), (gda_ref, tda_ref, wda_ref, dtda_ref, doda_ref),
                    (gmem_ref, tmem_ref, wmem_ref, dtmem_ref, domem_ref))
        for b, (g_ref, t_ref, w_ref, dt_ref, do_ref) in enumerate(branches):
            s = _sigmoid(g_ref[...])
            dt = (s * dm).astype(BF16)
            dt_ref[...] = dt
            do_ref[...] = _dot(dt, w_ref[...], _NT)
            stage[slot, :, b * w:(b + 1) * w] = (dm * t_ref[...].astype(F32) * s * (1.0 - s)).astype(BF16)
        slot_copy(slot).start()

        @pl.when(i == steps - 1)
        def _():
            slot_copy(1 - slot).wait()
            slot_copy(slot).wait()

    assert steps >= 2
    whole = lambda a: pl.BlockSpec(a.shape, lambda i: (0, 0))
    return pl.pallas_call(
        body, name="gate_merge_bwd", grid=(steps,),
        in_specs=[_row_spec(tr, w, 10), _row_spec(tr, w, 11), _row_spec(tr, w, 12)] + [_row_spec(tr, w)] * 3
        + [whole(wt) for wt in weights] + [_row_spec(tr, w), _any_spec()],
        out_specs=[_row_spec(tr, w)] * 3 + [_row_spec(tr, wt.shape[0]) for wt in weights] + [_any_spec()],
        out_shape=[jax.ShapeDtypeStruct((SEQ, w), BF16)] * 3
        + [jax.ShapeDtypeStruct((SEQ, wt.shape[0]), F32) for wt in weights]
        + [jax.ShapeDtypeStruct((SEQ, IN_COLS), BF16)],
        scratch_shapes=[pltpu.VMEM((2, tr, 3 * w), BF16), pltpu.SemaphoreType.DMA((2,))],
        input_output_aliases={10: 6},
        compiler_params=_params(("arbitrary",), 34 * tr * w * 4),
    )(proj, proj, proj, *ts, *weights, dmerged, dproj)


def _ffn_in_swiglu(h2, wfin, tm, tn):
    gm, gn = SEQ // tm, D_FF // tn

    def body(h_ref, wa_ref, wb_ref, a_ref, b_ref, act_ref, actt_ref):
        h = h_ref[...]
        a = _dot(h, wa_ref[...], _NN)
        b = _dot(h, wb_ref[...], _NN)
        act = a * _sigmoid(a) * b
        a_ref[...] = a.astype(BF16)
        b_ref[...] = b.astype(BF16)
        act_ref[...] = act.astype(BF16)
        actt_ref[...] = act.T.astype(BF16)

    tile = pl.BlockSpec((tm, tn), lambda j, i: (i, j))
    shape = jax.ShapeDtypeStruct((SEQ, D_FF), BF16)
    return pl.pallas_call(
        body, name="ffn_in_swiglu_fwd", grid=(gn, gm),
        in_specs=[pl.BlockSpec((tm, D_MODEL), lambda j, i: (i, 0)),
                  pl.BlockSpec((D_MODEL, tn), lambda j, i: (0, j)),
                  pl.BlockSpec((D_MODEL, tn), lambda j, i: (0, gn + j))],
        out_specs=[tile, tile, tile, pl.BlockSpec((tn, tm), lambda j, i: (j, i))],
        out_shape=[shape, shape, shape, jax.ShapeDtypeStruct((D_FF, SEQ), BF16)],
        compiler_params=_params(("parallel", "parallel"), 4 * tm * D_MODEL + 8 * D_MODEL * tn + 16 * tm * tn
                                + 6 * tm * tn * 4),
    )(h2, wfin, wfin)


def _ffn_out_bwd_swiglu(dyb, wfout, a, b, tm, tn):
    gm, gn = SEQ // tm, D_FF // tn

    def body(dy_ref, w_ref, a_ref, b_ref, o_ref):
        d = _dot(dy_ref[...], w_ref[...], _NT)
        av = a_ref[...].astype(F32)
        bv = b_ref[...].astype(F32)
        s = _sigmoid(av)
        silu = av * s
        o_ref[0] = (d * bv * (s + silu * (1.0 - s))).astype(BF16)
        o_ref[1] = (d * silu).astype(BF16)

    tile = pl.BlockSpec((tm, tn), lambda i, j: (i, j))
    return pl.pallas_call(
        body, name="ffn_out_bwd_swiglu", grid=(gm, gn),
        in_specs=[pl.BlockSpec((tm, D_MODEL), lambda i, j: (i, 0)), pl.BlockSpec((tn, D_MODEL), lambda i, j: (j, 0)),
                  tile, tile],
        out_specs=pl.BlockSpec((2, tm, tn), lambda i, j: (0, i, j)),
        out_shape=jax.ShapeDtypeStruct((2, SEQ, D_FF), BF16),
        compiler_params=_params(("parallel", "parallel"), 4 * tm * D_MODEL + 4 * tn * D_MODEL + 16 * tm * tn
                                + 6 * tm * tn * 4),
    )(dyb, wfout, a, b)


def _ffn_out_loss_head(x1, act, wfout, target, tr):
    w = D_MODEL

    def body(x_ref, a_ref, w_ref, t_ref, dy_ref, dyb_ref, loss_ref, acc_ref):
        err = x_ref[...] + _dot(a_ref[...], w_ref[...], _NN) - t_ref[...]
        dy = err * (1.0 / w)
        dy_ref[...] = dy
        dyb_ref[...] = dy.astype(BF16)
        part = jnp.sum(err * err, axis=0, keepdims=True)

        @pl.when(pl.program_id(0) == 0)
        def _():
            acc_ref[...] = part

        @pl.when(pl.program_id(0) > 0)
        def _():
            acc_ref[...] += part

        @pl.when(pl.program_id(0) == SEQ // tr - 1)
        def _():
            total = jnp.sum(acc_ref[...], axis=1, keepdims=True) * (0.5 / w)
            loss_ref[...] = jnp.broadcast_to(total, (1, LANE))

    return pl.pallas_call(
        body, name="ffn_out_loss_head", grid=(SEQ // tr,),
        in_specs=[_row_spec(tr, w), _row_spec(tr, D_FF), pl.BlockSpec((D_FF, w), lambda i: (0, 0)), _row_spec(tr, w)],
        out_specs=[_row_spec(tr, w), _row_spec(tr, w), _bcast_spec(LANE)],
        out_shape=[jax.ShapeDtypeStruct((SEQ, w), F32), jax.ShapeDtypeStruct((SEQ, w), BF16),
                   jax.ShapeDtypeStruct((1, LANE), F32)],
        scratch_shapes=[pltpu.VMEM((1, w), F32)],
        compiler_params=_params(("arbitrary",), 12 * tr * w * 4 + 4 * D_FF * w),
    )(x1, act, wfout, target)


GLA_ROWS = 256
GLA_CPB = GLA_ROWS // HG_CHUNK
GLA_NBLK = SEQ // GLA_ROWS
GLA_WAYS = 4
GLA_TRIPS = GLA_NBLK // GLA_WAYS
GLA_GRAD_WAYS = 4
GLA_GRAD_TRIPS = GLA_NBLK // GLA_GRAD_WAYS
GLA_NCK = SEQ // HG_CHUNK


def _dot_split(m, xv, dims):
    hi = xv.astype(BF16)
    r1 = xv - hi.astype(F32)
    mid = r1.astype(BF16)
    lo = (r1 - mid.astype(F32)).astype(BF16)
    dot = lambda t: lax.dot_general(m, t, dims, preferred_element_type=F32)
    return (dot(lo) + dot(mid)) + dot(hi)


def _gla_block(qc, fc, lbv, masks):
    mask, maskb, direction = masks
    sq = _sigmoid(qc)
    q = qc * sq * HG_SCALE
    sf = _sigmoid(fc)
    forget = lbv + (1.0 - lbv) * sf
    k = 1.0 - forget
    logf = jnp.log(forget)
    b = _dot_split(maskb, logf, _NN)
    ends = []
    for j in range(GLA_CPB):
        lo, hi = j * HG_CHUNK, (j + 1) * HG_CHUNK
        end = jnp.where(direction == 0, b[hi - 1:hi, :], b[lo:lo + 1, :])
        ends.append(jnp.broadcast_to(end, (HG_CHUNK, HEAD_DIM)))
    bt = jnp.concatenate(ends, axis=0)
    eb = jnp.exp(b)
    qt = q * eb
    kt = k * jnp.exp(-b)
    kh = k * jnp.exp(bt - b)
    a = jnp.where(mask, _dot(qt, kt, _NT), 0.0)
    return dict(sq=sq, sf=sf, forget=forget, k=k, b=b, bt=bt, eb=eb, qt=qt, kt=kt, kh=kh, a=a)


def _gla_masks(direction):
    row = lax.broadcasted_iota(jnp.int32, (GLA_ROWS, GLA_ROWS), 0)
    col = lax.broadcasted_iota(jnp.int32, (GLA_ROWS, GLA_ROWS), 1)
    same = (row // HG_CHUNK) == (col // HG_CHUNK)
    mask = jnp.logical_and(same, jnp.where(direction == 0, row - col, col - row) >= 0)
    return mask, jnp.where(mask, 1.0, 0.0).astype(BF16), direction


def _gla_chunk_rows(j):
    return slice(j * HG_CHUNK, (j + 1) * HG_CHUNK)


def _gla_block_rows(b):
    return pl.ds(pl.multiple_of(b * GLA_ROWS, GLA_ROWS), GLA_ROWS)


def _head_spec(col_block0):
    return pl.BlockSpec((SEQ, HEAD_DIM), lambda h, d: (0, col_block0 + h))


def _gla_fwd(proj, lb, gain):
    nck = GLA_NCK

    def body(q_ref, f_ref, v_ref, g_ref, lb_ref, gain_ref, ohg_ref, opre_ref, st_ref, qt_scr, dec_scr, cs_scr):
        d = pl.program_id(1)
        masks = _gla_masks(d)
        lbv = lb_ref[...]

        @pl.when(d == 0)
        def _():
            opre_ref[...] = jnp.zeros_like(opre_ref)

        def intra(s, carry):
            blocks = [s + w * GLA_TRIPS for w in range(GLA_WAYS)]
            rows = [_gla_block_rows(b) for b in blocks]
            loaded = [(q_ref[r, :], f_ref[r, :], v_ref[r, :], opre_ref[r, :]) for r in rows]
            results = []
            for qc, fc, v, o_prev in loaded:
                ck = _gla_block(qc, fc, lbv, masks)
                o = o_prev + _dot(ck["a"], v, _NN)
                cs = [_dot(v[_gla_chunk_rows(j), :], ck["kh"][_gla_chunk_rows(j), :], _TN) for j in range(GLA_CPB)]
                dec = [jnp.exp(ck["bt"][j * HG_CHUNK:j * HG_CHUNK + 8, :]) for j in range(GLA_CPB)]
                results.append((o, ck["qt"].astype(BF16), cs, dec))
            for b, r, (o, qt, cs, dec) in zip(blocks, rows, results):
                opre_ref[r, :] = o
                qt_scr[r, :] = qt
                for j in range(GLA_CPB):
                    cs_scr[b * GLA_CPB + j] = cs[j]
                    dec_scr[b * GLA_CPB + j] = dec[j]
            return carry

        lax.fori_loop(0, GLA_TRIPS, intra, 0)

        def scan(i, st):
            c = jnp.where(d == 0, i, nck - 1 - i)
            st_ref[c] = st
            return st * dec_scr[c][0:1, :] + cs_scr[c]

        lax.fori_loop(0, nck, scan, jnp.zeros((HEAD_DIM, HEAD_DIM), F32), unroll=4)

        def inter(c, carry):
            rows = pl.ds(pl.multiple_of(c * HG_CHUNK, HG_CHUNK), HG_CHUNK)
            opre_ref[rows, :] += _dot(qt_scr[rows, :], st_ref[c], _NT)
            return carry

        lax.fori_loop(0, nck, inter, 0, unroll=4)

        @pl.when(d == 1)
        def _():
            o = opre_ref[...]
            r = lax.rsqrt(jnp.mean(o * o, axis=-1, keepdims=True) + RMS_EPS)
            g = g_ref[...]
            ohg_ref[...] = (o * r * gain_ref[...] * (g * _sigmoid(g))).astype(BF16)

    blk = SEQ * HEAD_DIM * 4
    return pl.pallas_call(
        body, name="gla_fwd", grid=(HG_HEADS, 2),
        in_specs=[_head_spec(CB_HG_Q),
                  pl.BlockSpec((SEQ, HEAD_DIM), lambda h, d: (0, CB_F + 8 * d + h)),
                  _head_spec(CB_HG_I), _head_spec(CB_HG_G),
                  pl.BlockSpec((None, None, 1, HEAD_DIM), lambda h, d: (d, h, 0, 0)),
                  pl.BlockSpec((1, HEAD_DIM), lambda h, d: (0, 0))],
        out_specs=[_head_spec(0), _head_spec(0),
                   pl.BlockSpec((None, None, nck, HEAD_DIM, HEAD_DIM), lambda h, d: (h, d, 0, 0, 0)),
                   pl.BlockSpec((None, SEQ, HEAD_DIM), lambda h, d: (d, 0, h)),
                   pl.BlockSpec((None, None, nck, 8, HEAD_DIM), lambda h, d: (h, d, 0, 0, 0))],
        out_shape=[jax.ShapeDtypeStruct((SEQ, D_MODEL), BF16), jax.ShapeDtypeStruct((SEQ, D_MODEL), F32),
                   jax.ShapeDtypeStruct((HG_HEADS, 2, nck, HEAD_DIM, HEAD_DIM), F32),
                   jax.ShapeDtypeStruct((2, SEQ, D_MODEL), BF16),
                   jax.ShapeDtypeStruct((HG_HEADS, 2, nck, 8, HEAD_DIM), F32)],
        scratch_shapes=[pltpu.VMEM((nck, HEAD_DIM, HEAD_DIM), F32)],
        compiler_params=_params(("parallel", "arbitrary"), 8 * blk + 3 * blk + 2 * blk + 2 * blk + blk),
    )(proj, proj, proj, proj, lb, gain)


def _gla_bwd(proj, lb, gain, o_pre, states, qt_all, dec_all, do_hg, dproj):
    nck = GLA_NCK
    do, dproj, dgain = _gla_bwd_norm(proj, gain, o_pre, do_hg, dproj, 256)

    def body(q_ref, f_ref, v_ref, lb_ref, do_ref, st_ref, qt_ref, dec_scr, dproj_in, dproj_ref, dlb_ref,
             dq_acc, dv_acc, dst_scr, cs_scr, df_out, dq_out, dv_out, sems):
        del dproj_in
        h, d = pl.program_id(0), pl.program_id(1)
        masks = _gla_masks(d)
        mask, maskb, _ = masks
        lbv = lb_ref[...]

        def column_copy(k, staging, col_block):
            cols = pl.ds(pl.multiple_of(col_block * LANE, LANE), LANE)
            return pltpu.make_async_copy(staging, dproj_ref.at[:, cols], sems.at[k])

        df_copy = column_copy(0, df_out, CB_F + 8 * d + h)
        dq_copy = column_copy(1, dq_out, CB_HG_Q + h)
        dv_copy = column_copy(2, dv_out, CB_HG_I + h)
        last = jnp.logical_and(h == HG_HEADS - 1, d == 1)

        def intra(s, carry):
            blocks = [s + w * GLA_TRIPS for w in range(GLA_WAYS)]
            loaded = [(qt_ref[r, :], do_ref[r, :]) for r in map(_gla_block_rows, blocks)]
            results = [[_dot(doc[_gla_chunk_rows(j), :], qt[_gla_chunk_rows(j), :], _TN) for j in range(GLA_CPB)]
                       for qt, doc in loaded]
            for b, cs in zip(blocks, results):
                for j in range(GLA_CPB):
                    cs_scr[b * GLA_CPB + j] = cs[j]
            return carry

        lax.fori_loop(0, GLA_TRIPS, intra, 0)

        def scan(i, dst):
            c = jnp.where(d == 0, nck - 1 - i, i)
            dst_scr[c] = dst
            return dst * dec_scr[c][0:1, :] + cs_scr[c]

        lax.fori_loop(0, nck, scan, jnp.zeros((HEAD_DIM, HEAD_DIM), F32), unroll=4)

        @pl.when(d == 0)
        def _():
            dq_acc[...] = jnp.zeros_like(dq_acc)
            dv_acc[...] = jnp.zeros_like(dv_acc)

        def block_grads(qc, fc, v, doc, states_in, dstates, decays):
            ck = _gla_block(qc, fc, lbv, masks)
            qt, kt, kh, a = ck["qt"], ck["kt"], ck["kh"], ck["a"]
            da = jnp.where(mask, _dot(doc, v, _NT), 0.0)
            dqt_i = _dot(da, kt, _NN)
            dkt = _dot(da, qt, _TN)
            dv_i = _dot(a, doc, _TN)
            dqt_p, dv_p, dkh_p, dbt_p = [], [], [], []
            for j in range(GLA_CPB):
                cr = _gla_chunk_rows(j)
                st_in, dst = states_in[j], dstates[j]
                dqt_p.append(dqt_i[cr, :] + _dot(doc[cr, :], st_in, _NN))
                dv_p.append(dv_i[cr, :] + _dot(kh[cr, :], dst, _NT))
                dkh_j = _dot(v[cr, :], dst, _NN)
                dkh_p.append(dkh_j)
                dbt_j = (decays[j][0:1, :] * jnp.sum(dst * st_in, axis=0, keepdims=True)
                         + jnp.sum(dkh_j * kh[cr, :], axis=0, keepdims=True))
                dbt_p.append(jnp.broadcast_to(dbt_j, (HG_CHUNK, HEAD_DIM)))
            dqt = jnp.concatenate(dqt_p, axis=0)
            dv = jnp.concatenate(dv_p, axis=0)
            dkh = jnp.concatenate(dkh_p, axis=0)
            dbt = jnp.concatenate(dbt_p, axis=0)
            db = dqt * qt - dkt * kt - dkh * kh
            dq = dqt * ck["eb"]
            dk = dkt * jnp.exp(-ck["b"]) + dkh * jnp.exp(ck["bt"] - ck["b"])
            dlogf = _dot_split(maskb, db, _TN) + dbt
            dforget = dlogf / ck["forget"] - dk
            sf, sq = ck["sf"], ck["sq"]
            df = (dforget * (1.0 - lbv) * sf * (1.0 - sf)).astype(BF16)
            dqc = dq * HG_SCALE * (sq + qc * sq * (1.0 - sq))
            return df, dqc, dv, jnp.sum(dforget * (1.0 - sf), axis=0, keepdims=True)

        def grads(s, dlb):
            blocks = [s + w * GLA_GRAD_TRIPS for w in range(GLA_GRAD_WAYS)]
            rows = [_gla_block_rows(b) for b in blocks]
            loaded = []
            for b, r in zip(blocks, rows):
                chunks = [b * GLA_CPB + j for j in range(GLA_CPB)]
                loaded.append((q_ref[r, :], f_ref[r, :], v_ref[r, :], do_ref[r, :], [st_ref[c] for c in chunks],
                               [dst_scr[c] for c in chunks], [dec_scr[c] for c in chunks], dq_acc[r, :], dv_acc[r, :]))
            results = [block_grads(*t[:7]) + (t[7], t[8]) for t in loaded]
            for r, (df, dqc, dv, dlb_part, dq_prev, dv_prev) in zip(rows, results):
                df_out[r, :] = df
                dq_acc[r, :] = dq_prev + dqc
                dv_acc[r, :] = dv_prev + dv
                dlb = dlb + dlb_part
            return dlb

        @pl.when(jnp.logical_or(h > 0, d > 0))
        def _():
            df_copy.wait()

        dlb_ref[...] = lax.fori_loop(0, GLA_GRAD_TRIPS, grads, jnp.zeros((1, HEAD_DIM), F32))
        df_copy.start()

        @pl.when(d == 1)
        def _():
            @pl.when(h > 0)
            def _():
                dq_copy.wait()
                dv_copy.wait()

            dq_out[...] = dq_acc[...].astype(BF16)
            dv_out[...] = dv_acc[...].astype(BF16)
            dq_copy.start()
            dv_copy.start()

        @pl.when(last)
        def _():
            df_copy.wait()
            dq_copy.wait()
            dv_copy.wait()

    blk = SEQ * HEAD_DIM * 4
    state_bytes = nck * HEAD_DIM * HEAD_DIM * 4
    dproj, dlb = pl.pallas_call(
        body, name="gla_bwd", grid=(HG_HEADS, 2),
        in_specs=[_head_spec(CB_HG_Q),
                  pl.BlockSpec((SEQ, HEAD_DIM), lambda h, d: (0, CB_F + 8 * d + h)),
                  _head_spec(CB_HG_I),
                  pl.BlockSpec((None, None, 1, HEAD_DIM), lambda h, d: (d, h, 0, 0)),
                  _head_spec(0),
                  pl.BlockSpec((None, None, nck, HEAD_DIM, HEAD_DIM), lambda h, d: (h, d, 0, 0, 0)),
                  pl.BlockSpec((None, SEQ, HEAD_DIM), lambda h, d: (d, 0, h)),
                  pl.BlockSpec((None, None, nck, 8, HEAD_DIM), lambda h, d: (h, d, 0, 0, 0)),
                  _any_spec()],
        out_specs=[_any_spec(), pl.BlockSpec((None, None, 1, HEAD_DIM), lambda h, d: (d, h, 0, 0))],
        out_shape=[jax.ShapeDtypeStruct((SEQ, IN_COLS), BF16), jax.ShapeDtypeStruct((2, HG_HEADS, 1, HEAD_DIM), F32)],
        scratch_shapes=[pltpu.VMEM((SEQ, HEAD_DIM), F32)] * 2
        + [pltpu.VMEM((nck, HEAD_DIM, HEAD_DIM), F32)] * 2
        + [pltpu.VMEM((SEQ, HEAD_DIM), BF16)] * 3 + [pltpu.SemaphoreType.DMA((3,))],
        input_output_aliases={8: 0},
        compiler_params=_params(("arbitrary", "arbitrary"), 8 * blk + 4 * state_bytes + 3 * blk + 3 * blk),
    )(proj, proj, proj, lb, do, states, qt_all, dec_all, dproj)
    return dproj, dlb, dgain


def _gla_bwd_norm(proj, gain, o_pre, do_hg, dproj, tr):
    w = D_MODEL

    def body(g_ref, gain_ref, opre_ref, dohg_ref, dproj_in, do_ref, dg_ref, dgain_ref):
        del dproj_in
        gainv = gain_ref[...]
        part = jnp.zeros((1, HEAD_DIM), F32)
        for h in range(HG_HEADS):
            hs = slice(h * HEAD_DIM, (h + 1) * HEAD_DIM)
            o = opre_ref[:, hs]
            r = lax.rsqrt(jnp.mean(o * o, axis=-1, keepdims=True) + RMS_EPS)
            ohat = o * r
            g = g_ref[:, hs]
            sg = _sigmoid(g)
            silu = g * sg
            dout = dohg_ref[:, hs]
            dg_ref[:, hs] = (dout * ohat * gainv * (sg + silu * (1.0 - sg))).astype(BF16)
            dy = dout * silu
            part = part + jnp.sum(dy * ohat, axis=0, keepdims=True)
            dn = dy * gainv
            do_ref[:, hs] = r * (dn - ohat * jnp.mean(dn * ohat, axis=-1, keepdims=True))

        @pl.when(pl.program_id(0) == 0)
        def _():
            dgain_ref[...] = part

        @pl.when(pl.program_id(0) > 0)
        def _():
            dgain_ref[...] += part

    return pl.pallas_call(
        body, name="gla_bwd_norm", grid=(SEQ // tr,),
        in_specs=[_row_spec(tr, w, CB_HG_G * LANE // w), _bcast_spec(HEAD_DIM), _row_spec(tr, w), _row_spec(tr, w),
                  _any_spec()],
        out_specs=[_row_spec(tr, w), _row_spec(tr, w, CB_HG_G * LANE // w), _bcast_spec(HEAD_DIM)],
        out_shape=[jax.ShapeDtypeStruct((SEQ, w), F32), jax.ShapeDtypeStruct((SEQ, IN_COLS), BF16),
                   jax.ShapeDtypeStruct((1, HEAD_DIM), F32)],
        input_output_aliases={4: 1},
        compiler_params=_params(("arbitrary",), 16 * tr * w * 4),
    )(proj, gain, o_pre, do_hg, dproj)


def _da_residue_rows(r, n0, size, d):
    if d == 1:
        return pl.ds(pl.multiple_of(n0, 8), size)
    return pl.ds(r + n0 * d, size, stride=d)


def _da_residue_ways(d, nqb):
    return min(d, 4) if nqb <= 2 else 1


def _da_rmsnorm(x, gain):
    r = lax.rsqrt(jnp.mean(x * x, axis=-1, keepdims=True) + RMS_EPS)
    return x * r, r, x * r * gain


def _da_scores(qn_scr, kn_scr, slope, i, ld):
    w0 = jnp.clip(i * DA_QB - DA_RADIUS, 0, ld - DA_WIN)
    w0 = pl.multiple_of(w0, DA_RADIUS)
    qrows = pl.ds(pl.multiple_of(i * DA_QB, DA_QB), DA_QB)
    win = pl.ds(w0, DA_WIN)
    qb = qn_scr[qrows, :]
    kw = kn_scr[win, :]
    s = _dot(qb, kw, _NT) * DA_SCALE
    qpos = i * DA_QB + lax.broadcasted_iota(jnp.int32, (DA_QB, DA_WIN), 0)
    kpos = w0 + lax.broadcasted_iota(jnp.int32, (DA_QB, DA_WIN), 1)
    arel = jnp.abs(kpos - qpos)
    s = s - slope * arel.astype(F32)
    s = jnp.where(arel <= DA_RADIUS, s, NEG_INF)
    return s, qb, kw, qrows, win


def _da_slopes(group):
    d = DA_DILATIONS[group]
    sl = _alibi_slopes(DA_HEADS)[4 * group:4 * group + 4] * d
    return jnp.asarray(np.broadcast_to(sl[:, None, None], (4, 1, LANE)).copy())


def _da_fwd(proj, gq, gk, group):
    d = DA_DILATIONS[group]
    ld = SEQ // d
    nqb = ld // DA_QB

    ways = min(DA_FWD_WAYS, nqb)
    rways = _da_residue_ways(d, nqb)

    def body(q_ref, k_ref, v_ref, gq_ref, gk_ref, sl_ref, o_ref, lse_ref, qn_scr, kn_scr, v_scr):
        slope = sl_ref[:, 0:1]

        def residues(t, carry):
            rs = [t * rways + u for u in range(rways)]
            for u, r in enumerate(rs):
                sel = _da_residue_rows(r, 0, ld, d)
                qn_scr[u] = _da_rmsnorm(q_ref[sel, :], gq_ref[...])[2].astype(BF16)
                kn_scr[u] = _da_rmsnorm(k_ref[sel, :], gk_ref[...])[2].astype(BF16)
                v_scr[u] = v_ref[sel, :].astype(BF16)

            def block(u, i):
                s, _, _, _, win = _da_scores(qn_scr.at[u], kn_scr.at[u], slope, i, ld)
                m = jnp.max(s, axis=-1, keepdims=True)
                p = jnp.exp(s - m)
                l = jnp.sum(p, axis=-1, keepdims=True)
                return _dot(p, v_scr[u, win, :], _NN) / l, jnp.broadcast_to(m + jnp.log(l), (DA_QB, HEAD_DIM))

            def step(i, c2):
                todo = [(u, r, i + w * (nqb // ways)) for u, r in enumerate(rs) for w in range(ways)]
                for (u, r, b), (o, lse) in zip(todo, [block(u, b) for u, _, b in todo]):
                    out = _da_residue_rows(r, b * DA_QB, DA_QB, d)
                    o_ref[out, :] = o
                    lse_ref[out, :] = lse
                return c2

            return lax.fori_loop(0, nqb // ways, step, carry)

        lax.fori_loop(0, d // rways, residues, 0)

    seq_spec = lambda base: pl.BlockSpec((SEQ, HEAD_DIM), lambda h: (0, base + 4 * group + h))
    out_spec = pl.BlockSpec((SEQ, HEAD_DIM), lambda h: (0, h))
    gain_spec = pl.BlockSpec((1, HEAD_DIM), lambda h: (0, 0))
    blk = SEQ * HEAD_DIM * 4
    return pl.pallas_call(
        body, name=f"da_fwd_g{group}", grid=(DA_HEADS_PER_GROUP,),
        in_specs=[seq_spec(CB_DA_Q), seq_spec(CB_DA_K), seq_spec(CB_DA_V), gain_spec, gain_spec,
                  pl.BlockSpec((None, 1, LANE), lambda h: (h, 0, 0))],
        out_specs=[out_spec, out_spec],
        out_shape=[jax.ShapeDtypeStruct((SEQ, DA_WIDTH), F32)] * 2,
        scratch_shapes=[pltpu.VMEM((rways, ld, HEAD_DIM), BF16)] * 3,
        compiler_params=_params(("parallel",), 10 * blk + 2 * blk),
    )(proj, proj, proj, gq, gk, _da_slopes(group))


def _da_merge(os, lses, tr):
    w = DA_WIDTH

    def body(o0, o1, o2, l0, l1, l2, ob_ref, of_ref, lse_ref):
        la, lb_, lc = l0[...], l1[...], l2[...]
        m = jnp.maximum(jnp.maximum(la, lb_), lc)
        ea, eb, ec = jnp.exp(la - m), jnp.exp(lb_ - m), jnp.exp(lc - m)
        tot = ea + eb + ec
        o = (ea * o0[...] + eb * o1[...] + ec * o2[...]) / tot
        of_ref[...] = o
        ob_ref[...] = o.astype(BF16)
        lse_ref[...] = m + jnp.log(tot)

    return pl.pallas_call(
        body, name="da_merge", grid=(SEQ // tr,), in_specs=[_row_spec(tr, w)] * 6,
        out_specs=[_row_spec(tr, w)] * 3,
        out_shape=[jax.ShapeDtypeStruct((SEQ, w), BF16), jax.ShapeDtypeStruct((SEQ, w), F32),
                   jax.ShapeDtypeStruct((SEQ, w), F32)],
        compiler_params=_params(("parallel",), 24 * tr * w * 4),
    )(*os, *lses)


def _da_rowdot(do, o, tr):
    w = DA_WIDTH

    def body(do_ref, o_ref, out_ref):
        prod = do_ref[...] * o_ref[...]
        for h in range(w // HEAD_DIM):
            sl = slice(h * HEAD_DIM, (h + 1) * HEAD_DIM)
            out_ref[:, sl] = jnp.broadcast_to(jnp.sum(prod[:, sl], axis=-1, keepdims=True), (tr, HEAD_DIM))

    return pl.pallas_call(
        body, name="da_rowdot", grid=(SEQ // tr,), in_specs=[_row_spec(tr, w)] * 2,
        out_specs=_row_spec(tr, w), out_shape=jax.ShapeDtypeStruct((SEQ, w), F32),
        compiler_params=_params(("parallel",), 10 * tr * w * 4),
    )(do, o)


def _da_bwd(proj, gq, gk, do, lse, dd, dproj, group):
    d = DA_DILATIONS[group]
    ld = SEQ // d
    nqb = ld // DA_QB
    ways = min(DA_WAYS, nqb)
    rways = _da_residue_ways(d, nqb)

    def body(q_ref, k_ref, v_ref, gq_ref, gk_ref, sl_ref, do_ref, lse_ref, dd_ref, dproj_in,
             dproj_ref, dgq_ref, dgk_ref,
             qn_scr, kn_scr, v_scr, dqn_scr, dkn_scr, dvr_scr, dq_scr, dk_scr, dv_scr, dq_out, dk_out, dv_out, sems):
        del dproj_in
        head = pl.program_id(0)
        gqv, gkv = gq_ref[...], gk_ref[...]
        slope = sl_ref[:, 0:1]

        copies = []
        for k, (staging, base) in enumerate(((dq_out, CB_DA_Q), (dk_out, CB_DA_K), (dv_out, CB_DA_V))):
            cols = pl.ds(pl.multiple_of((base + 4 * group + head) * LANE, LANE), LANE)
            copies.append(pltpu.make_async_copy(staging, dproj_ref.at[:, cols], sems.at[k]))

        def residues(t, carry):
            rs = [t * rways + u for u in range(rways)]
            sels = [_da_residue_rows(r, 0, ld, d) for r in rs]
            for u, sel in enumerate(sels):
                qn_scr[u] = _da_rmsnorm(q_ref[sel, :], gqv)[2].astype(BF16)
                kn_scr[u] = _da_rmsnorm(k_ref[sel, :], gkv)[2].astype(BF16)
                v_scr[u] = v_ref[sel, :].astype(BF16)
            dkn_scr[...] = jnp.zeros_like(dkn_scr)
            dvr_scr[...] = jnp.zeros_like(dvr_scr)

            def block(u, r, i):
                s, qb, kw, qrows, win = _da_scores(qn_scr.at[u], kn_scr.at[u], slope, i, ld)
                src = _da_residue_rows(r, i * DA_QB, DA_QB, d)
                p = jnp.exp(s - lse_ref[src, :][:, 0:1])
                dob = do_ref[src, :]
                dp = _dot(dob, v_scr[u, win, :], _NT)
                ds = p * (dp - dd_ref[src, :][:, 0:1]) * DA_SCALE
                return u, qrows, win, _dot(p, dob, _TN), _dot(ds, kw, _NN), _dot(ds, qb, _TN)

            def step(i, c2):
                todo = [(u, r, i + w * (nqb // ways)) for u, r in enumerate(rs) for w in range(ways)]
                for u, qrows, win, dv, dqn, dkn in [block(*t) for t in todo]:
                    dvr_scr[u, win, :] += dv
                    dqn_scr[u, qrows, :] = dqn
                    dkn_scr[u, win, :] += dkn
                return c2

            lax.fori_loop(0, nqb // ways, step, 0)

            pq, pk = carry
            for u, sel in enumerate(sels):
                parts = []
                for x_ref, gv, dn_scr, dx_scr in ((q_ref, gqv, dqn_scr, dq_scr), (k_ref, gkv, dkn_scr, dk_scr)):
                    hat, rstd, _ = _da_rmsnorm(x_ref[sel, :], gv)
                    dn = dn_scr[u]
                    dyg = dn * gv
                    dx_scr[sel, :] = rstd * (dyg - hat * jnp.mean(dyg * hat, axis=-1, keepdims=True))
                    parts.append(jnp.sum(dn * hat, axis=0, keepdims=True))
                dv_scr[sel, :] = dvr_scr[u]
                pq, pk = pq + parts[0], pk + parts[1]
            return pq, pk

        zero = jnp.zeros((1, HEAD_DIM), F32)
        pq, pk = lax.fori_loop(0, d // rways, residues, (zero, zero))

        @pl.when(pl.program_id(0) == 0)
        def _():
            dgq_ref[...] = pq
            dgk_ref[...] = pk

        @pl.when(pl.program_id(0) > 0)
        def _():
            dgq_ref[...] += pq
            dgk_ref[...] += pk

        @pl.when(head > 0)
        def _():
            for cp in copies:
                cp.wait()

        dq_out[...] = dq_scr[...].astype(BF16)
        dk_out[...] = dk_scr[...].astype(BF16)
        dv_out[...] = dv_scr[...].astype(BF16)
        for cp in copies:
            cp.start()

        @pl.when(head == DA_HEADS_PER_GROUP - 1)
        def _():
            for cp in copies:
                cp.wait()

    seq_spec = lambda base: pl.BlockSpec((SEQ, HEAD_DIM), lambda h: (0, base + 4 * group + h))
    out_spec = pl.BlockSpec((SEQ, HEAD_DIM), lambda h: (0, h))
    gain_spec = pl.BlockSpec((1, HEAD_DIM), lambda h: (0, 0))
    gshape = jax.ShapeDtypeStruct((1, HEAD_DIM), F32)
    blk = SEQ * HEAD_DIM * 4
    return pl.pallas_call(
        body, name=f"da_bwd_g{group}", grid=(DA_HEADS_PER_GROUP,),
        in_specs=[seq_spec(CB_DA_Q), seq_spec(CB_DA_K), seq_spec(CB_DA_V), gain_spec, gain_spec,
                  pl.BlockSpec((None, 1, LANE), lambda h: (h, 0, 0)), out_spec, out_spec, out_spec, _any_spec()],
        out_specs=[_any_spec(), gain_spec, gain_spec],
        out_shape=[jax.ShapeDtypeStruct((SEQ, IN_COLS), BF16), gshape, gshape],
        scratch_shapes=[pltpu.VMEM((rways, ld, HEAD_DIM), BF16)] * 3 + [pltpu.VMEM((rways, ld, HEAD_DIM), F32)] * 3
        + [pltpu.VMEM((SEQ, HEAD_DIM), F32)] * 3 + [pltpu.VMEM((SEQ, HEAD_DIM), BF16)] * 3
        + [pltpu.SemaphoreType.DMA((3,))],
        input_output_aliases={9: 0},
        compiler_params=_params(("arbitrary",), 12 * blk + 3 * blk + 3 * blk + 3 * blk + 2 * blk),
    )(proj, proj, proj, gq, gk, _da_slopes(group), do, lse, dd, dproj)


MEM_WAYS = 4


def _mem_probs(q, kn, gq):
    qhat, rq, qn = _da_rmsnorm(q, gq)
    s = _dot(qn, kn, _NT) * MEM_SCALE
    m = jnp.max(s, axis=-1, keepdims=True)
    e = jnp.exp(s - m)
    p = e / jnp.sum(e, axis=-1, keepdims=True)
    return p, qhat, rq, qn


def _mem_pieces(tq):
    rows = tq // MEM_WAYS
    return [slice(w * rows, (w + 1) * rows) for w in range(MEM_WAYS)]


def _mem_fwd(proj, kv, gq, gk, tq):
    def body(q_ref, k_ref, v_ref, gq_ref, gk_ref, o_ref):
        kn = _da_rmsnorm(k_ref[...], gk_ref[...])[2]
        v, gqv = v_ref[...], gq_ref[...]
        pieces = _mem_pieces(tq)
        outs = [_dot(_mem_probs(q_ref[rs, :], kn, gqv)[0], v, _NN) for rs in pieces]
        for rs, o in zip(pieces, outs):
            o_ref[rs, :] = o.astype(BF16)

    gain_spec = pl.BlockSpec((1, HEAD_DIM), lambda h, i: (0, 0))
    return pl.pallas_call(
        body, name="mem_fwd", grid=(MEM_HEADS, SEQ // tq),
        in_specs=[pl.BlockSpec((tq, HEAD_DIM), lambda h, i: (i, CB_MEM_Q + h)),
                  pl.BlockSpec((N_MEM, HEAD_DIM), lambda h, i: (0, h)),
                  pl.BlockSpec((N_MEM, HEAD_DIM), lambda h, i: (0, MEM_HEADS + h)), gain_spec, gain_spec],
        out_specs=pl.BlockSpec((tq, HEAD_DIM), lambda h, i: (i, h)),
        out_shape=jax.ShapeDtypeStruct((SEQ, MEM_WIDTH), BF16),
        compiler_params=_params(("parallel", "parallel"), 16 * tq * N_MEM * 4),
    )(proj, kv, kv, gq, gk)


def _mem_bwd(proj, kv, gq, gk, do, dproj, tq):
    nq = SEQ // tq

    def body(q_ref, k_ref, v_ref, gq_ref, gk_ref, do_ref, dproj_in, dq_ref, dk_ref, dv_ref, dgq_ref, dgk_ref, dkn_scr):
        del dproj_in
        h, i = pl.program_id(0), pl.program_id(1)
        gqv, gkv = gq_ref[...], gk_ref[...]
        khat, rk, kn = _da_rmsnorm(k_ref[...], gkv)
        v = v_ref[...]

        def piece(rs):
            p, qhat, rq, qn = _mem_probs(q_ref[rs, :], kn, gqv)
            dob = do_ref[rs, :]
            dp = _dot(dob, v, _NT)
            ds = p * (dp - jnp.sum(p * dp, axis=-1, keepdims=True)) * MEM_SCALE
            dqn = _dot(ds, kn, _NN)
            dyg = dqn * gqv
            dq = (rq * (dyg - qhat * jnp.mean(dyg * qhat, axis=-1, keepdims=True))).astype(BF16)
            return dq, _dot(p, dob, _TN), _dot(ds, qn, _TN), jnp.sum(dqn * qhat, axis=0, keepdims=True)

        pieces = _mem_pieces(tq)
        results = [piece(rs) for rs in pieces]
        for rs, res in zip(pieces, results):
            dq_ref[rs, :] = res[0]
        dvp = sum((res[1] for res in results[1:]), results[0][1])
        dknp = sum((res[2] for res in results[1:]), results[0][2])
        dgq_part = sum((res[3] for res in results[1:]), results[0][3])
        first = jnp.logical_and(h == 0, i == 0)

        @pl.when(first)
        def _():
            dgq_ref[...] = dgq_part

        @pl.when(jnp.logical_not(first))
        def _():
            dgq_ref[...] += dgq_part

        @pl.when(i == 0)
        def _():
            dv_ref[...] = dvp
            dkn_scr[...] = dknp

        @pl.when(i > 0)
        def _():
            dv_ref[...] += dvp
            dkn_scr[...] += dknp

        @pl.when(i == nq - 1)
        def _():
            dkn = dkn_scr[...]
            dkg = dkn * gkv
            dk_ref[...] = rk * (dkg - khat * jnp.mean(dkg * khat, axis=-1, keepdims=True))
            dgk_part = jnp.sum(dkn * khat, axis=0, keepdims=True)

            @pl.when(h == 0)
            def _():
                dgk_ref[...] = dgk_part

            @pl.when(h > 0)
            def _():
                dgk_ref[...] += dgk_part

    gain_spec = pl.BlockSpec((1, HEAD_DIM), lambda h, i: (0, 0))
    kvout = pl.BlockSpec((N_MEM, HEAD_DIM), lambda h, i: (0, h))
    return pl.pallas_call(
        body, name="mem_bwd", grid=(MEM_HEADS, nq),
        in_specs=[pl.BlockSpec((tq, HEAD_DIM), lambda h, i: (i, CB_MEM_Q + h)),
                  pl.BlockSpec((N_MEM, HEAD_DIM), lambda h, i: (0, h)),
                  pl.BlockSpec((N_MEM, HEAD_DIM), lambda h, i: (0, MEM_HEADS + h)), gain_spec, gain_spec,
                  pl.BlockSpec((tq, HEAD_DIM), lambda h, i: (i, h)), _any_spec()],
        out_specs=[pl.BlockSpec((tq, HEAD_DIM), lambda h, i: (i, CB_MEM_Q + h)), kvout, kvout, gain_spec, gain_spec],
        out_shape=[jax.ShapeDtypeStruct((SEQ, IN_COLS), BF16), jax.ShapeDtypeStruct((N_MEM, MEM_WIDTH), F32),
                   jax.ShapeDtypeStruct((N_MEM, MEM_WIDTH), F32), jax.ShapeDtypeStruct((1, HEAD_DIM), F32),
                   jax.ShapeDtypeStruct((1, HEAD_DIM), F32)],
        scratch_shapes=[pltpu.VMEM((N_MEM, HEAD_DIM), F32)], input_output_aliases={6: 0},
        compiler_params=_params(("arbitrary", "arbitrary"), 24 * tq * N_MEM * 4),
    )(proj, kv, kv, gq, gk, do, dproj)


def _mesh_position():
    return lax.axis_index("x"), lax.axis_index("y"), lax.axis_index("c")


def _any_spec():
    return pl.BlockSpec(memory_space=pl.ANY)


def _all_gather(shards):
    n = len(shards)

    def body(*refs):
        ins, outs = refs[:n], refs[n:2 * n]
        send_sems, recv_sems, local_sems = refs[2 * n:]
        x, y, c = _mesh_position()
        me, sibling = (x, y, c), (x, y, 1 - c)
        first = (jnp.where(c == 0, 1 - x, x), jnp.where(c == 0, y, 1 - y), c)
        second = (jnp.where(c == 0, x, 1 - x), jnp.where(c == 0, 1 - y, y), c)
        diagonal = (1 - x, 1 - y, c)

        def copy(w, k, block, to, src=None):
            px, py, pc = block
            rows = outs[w].at[4 * px + 2 * py + pc]
            return pltpu.make_async_remote_copy(
                src_ref=rows if src is None else src, dst_ref=rows,
                send_sem=send_sems.at[7 * w + k], recv_sem=recv_sems.at[7 * w + k],
                device_id=to, device_id_type=MESH)

        started = []
        for w in range(n):
            mine = pltpu.make_async_copy(ins[w], outs[w].at[4 * x + 2 * y + c], local_sems.at[w])
            mine.start()
            started.append(mine)
        sends = []
        for w in range(n):
            own = [copy(w, 0, me, sibling, src=ins[w]), copy(w, 1, me, first, src=ins[w]),
                   copy(w, 2, me, second, src=ins[w])]
            for cp in own:
                cp.start()
            sends += own
        for w in range(n):
            copy(w, 1, first, me).wait_recv()
            follow = [copy(w, 3, first, second), copy(w, 4, first, sibling)]
            for cp in follow:
                cp.start()
            copy(w, 2, second, me).wait_recv()
            follow.append(copy(w, 5, second, sibling))
            follow[-1].start()
            sends += follow
        for w in range(n):
            copy(w, 3, diagonal, me).wait_recv()
            passed = copy(w, 6, diagonal, sibling)
            passed.start()
            sends.append(passed)
        for w in range(n):
            for k in (0, 4, 5, 6):
                copy(w, k, sibling, me).wait_recv()
        for cp in sends:
            cp.wait_send()
        for mine in started:
            mine.wait()

    return pl.pallas_call(
        body, name="weights_all_gather",
        in_specs=[_any_spec()] * n, out_specs=[_any_spec()] * n,
        out_shape=[jax.ShapeDtypeStruct((N_DEV,) + s.shape, s.dtype) for s in shards],
        scratch_shapes=[pltpu.SemaphoreType.DMA((7 * n,)), pltpu.SemaphoreType.DMA((7 * n,)),
                        pltpu.SemaphoreType.DMA((n,))],
    )(*shards)


def _chip_of(j, x, y):
    return (1 - x if j & 1 else x, 1 - y if j & 2 else y)


_HBM_SPEC = pl.BlockSpec(memory_space=pltpu.HBM)
_SEM_SPEC = pl.BlockSpec(memory_space=pltpu.SEMAPHORE)
_DATAFLOW_EFFECT = pltpu.SideEffectType.DATAFLOW_SIDE_EFFECTING
TOKEN_SHAPE = (8, D_MODEL)


def _copies_start(name, arrays, n_copies, plan):
    n = len(arrays)

    def body(*refs):
        send_sems, recv_sems, token = refs[n], refs[n + 1], refs[2 * n + 2]
        copies = plan(refs[:n])
        assert len(copies) == n_copies
        for k, (src, dst, dev) in enumerate(copies):
            pltpu.make_async_remote_copy(src_ref=src, dst_ref=dst, send_sem=send_sems.at[k], recv_sem=recv_sems.at[k],
                                         device_id=dev, device_id_type=MESH).start()
        token[...] = jnp.zeros_like(token)

    outs = pl.pallas_call(
        body, name=name,
        out_shape=(pltpu.SemaphoreType.DMA((n_copies,)), pltpu.SemaphoreType.DMA((n_copies,)),
                   *[pltpu.HBM(a.shape, a.dtype) for a in arrays], jax.ShapeDtypeStruct(TOKEN_SHAPE, F32)),
        in_specs=[_HBM_SPEC] * n,
        out_specs=(_SEM_SPEC, _SEM_SPEC, *[_HBM_SPEC] * n, pl.BlockSpec(memory_space=pltpu.VMEM)),
        input_output_aliases={i: i + 2 for i in range(n)},
        compiler_params=pltpu.CompilerParams(has_side_effects=_DATAFLOW_EFFECT),
    )(*[pltpu.with_memory_space_constraint(a, pltpu.HBM) for a in arrays])
    return outs[0], outs[1], list(outs[2:2 + n]), outs[2 + n]


def _copies_wait(name, send_sems, recv_sems, arrays, n_copies, plan, after):
    n = len(arrays)
    after = list(after) if isinstance(after, (list, tuple)) else [after]

    def body(*refs):
        send_ref, recv_ref = refs[n], refs[n + 1]
        copies = plan(refs[:n])
        assert len(copies) == n_copies
        for k, (src, dst, dev) in enumerate(copies):
            cp = pltpu.make_async_remote_copy(src_ref=src, dst_ref=dst, send_sem=send_ref.at[k], recv_sem=recv_ref.at[k],
                                              device_id=dev, device_id_type=MESH)
            cp.wait_send()
            cp.wait_recv()

    outs = pl.pallas_call(
        body, name=name, out_shape=tuple(pltpu.HBM(a.shape, a.dtype) for a in arrays),
        in_specs=[_HBM_SPEC] * n + [_SEM_SPEC, _SEM_SPEC] + [pl.BlockSpec(memory_space=pl.ANY)] * len(after),
        out_specs=tuple([_HBM_SPEC] * n), input_output_aliases={i: i for i in range(n)},
        compiler_params=pltpu.CompilerParams(has_side_effects=_DATAFLOW_EFFECT),
    )(*arrays, send_sems, recv_sems, *after)
    return list(outs)


def _after(small, token):
    return small if token is None else small + token[0:1, :small.shape[-1]]


def _gather_plan_out(n):
    def plan(refs):
        x, y, c = _mesh_position()
        me = 4 * x + 2 * y + c
        copies = []
        for w in range(n):
            land = refs[n + w].at[me]
            copies.append((refs[w], land, (x, y, 1 - c)))
            for j in range(1, 4):
                copies.append((refs[w], land, (*_chip_of(j, x, y), c)))
        return copies
    return plan


def _gather_plan_pass(n):
    def plan(refs):
        x, y, c = _mesh_position()
        copies = []
        for w in range(n):
            for j in range(1, 4):
                px, py = _chip_of(j, x, y)
                rows = refs[w].at[4 * px + 2 * py + c]
                copies.append((rows, rows, (x, y, 1 - c)))
        return copies
    return plan


def _reduce_plan_sibling(n):
    def plan(refs):
        x, y, c = _mesh_position()
        copies = []
        for w in range(n):
            for j in range(4):
                px, py = _chip_of(j, x, y)
                copies.append((refs[w].at[4 * px + 2 * py + (1 - c)], refs[n + w].at[j], (x, y, 1 - c)))
        return copies
    return plan


def _reduce_plan_chips(n):
    def plan(refs):
        x, y, c = _mesh_position()
        copies = []
        for w in range(n):
            for j in range(1, 4):
                copies.append((refs[w].at[j - 1], refs[n + w].at[j - 1], (*_chip_of(j, x, y), c)))
        return copies
    return plan


def _chip_partials(grad, recv, name, tr):
    _, rows, width = grad.shape

    def body(g_ref, r_ref, own_ref, other_ref):
        x, y, c = _mesh_position()
        for j in range(4):
            px, py = _chip_of(j, x, y)
            total = g_ref[4 * px + 2 * py + c].astype(F32) + r_ref[j].astype(F32)
            if j == 0:
                own_ref[...] = total
            else:
                other_ref[j - 1] = total.astype(BF16)

    return pl.pallas_call(
        body, name=name, grid=(rows // tr,),
        in_specs=[pl.BlockSpec((N_DEV, tr, width), lambda i: (0, i, 0)),
                  pl.BlockSpec((4, tr, width), lambda i: (0, i, 0))],
        out_specs=[pl.BlockSpec((tr, width), lambda i: (i, 0)), pl.BlockSpec((3, tr, width), lambda i: (0, i, 0))],
        out_shape=[jax.ShapeDtypeStruct((rows, width), F32), jax.ShapeDtypeStruct((3, rows, width), BF16)],
        compiler_params=_params(("parallel",), 2 * 20 * tr * width * 2 + 8 * tr * width * 4),
    )(grad, recv)


def _adamw_math(w, g, m, v):
    m = ADAM_B1 * m + (1.0 - ADAM_B1) * g
    v = ADAM_B2 * v + (1.0 - ADAM_B2) * (g * g)
    m_hat = m / (1.0 - ADAM_B1 ** ADAM_STEP)
    v_hat = v / (1.0 - ADAM_B2 ** ADAM_STEP)
    delta = -ADAM_LR * (m_hat / (jnp.sqrt(v_hat) + ADAM_EPS) + ADAM_WD * w)
    return delta, m, v


def _adamw_shard(own, recv, w, m, v, name, tr):
    rows, width = own.shape

    def body(own_ref, r_ref, w_ref, m_ref, v_ref, g_ref, d_ref, nm_ref, nv_ref):
        g = own_ref[...]
        for j in range(3):
            g = g + r_ref[j].astype(F32)
        g_ref[...] = g
        d_ref[...], nm_ref[...], nv_ref[...] = _adamw_math(w_ref[...], g, m_ref[...], v_ref[...])

    spec = pl.BlockSpec((tr, width), lambda i: (i, 0))
    shape = jax.ShapeDtypeStruct((rows, width), F32)
    return pl.pallas_call(
        body, name=name, grid=(rows // tr,),
        in_specs=[spec, pl.BlockSpec((3, tr, width), lambda i: (0, i, 0)), spec, spec, spec],
        out_specs=[spec] * 4, out_shape=[shape] * 4,
        compiler_params=_params(("parallel",), 22 * tr * width * 4),
    )(own, recv, w, m, v)


def _small_all_reduce_adamw(gpart, lbpack, wpack, mpack, vpack, after):
    def body(gp_ref, lb_ref, w_ref, m_ref, v_ref, after_ref, g_ref, d_ref, nm_ref, nv_ref, gath_ref, send_sems, recv_sems):
        del after_ref
        x, y, c = _mesh_position()
        me = 4 * x + 2 * y + c
        gath_ref[me] = gp_ref[...]
        copies = []
        for j in range(1, N_DEV):
            peer = (x ^ (j >> 2), y ^ ((j >> 1) & 1), c ^ (j & 1))
            cp = pltpu.make_async_remote_copy(
                src_ref=gp_ref, dst_ref=gath_ref.at[me], send_sem=send_sems.at[j - 1], recv_sem=recv_sems.at[j - 1],
                device_id=peer, device_id_type=MESH)
            cp.start()
            copies.append(cp)
        for cp in copies:
            cp.wait()
        tot = gath_ref[0]
        for s in range(1, N_DEV):
            tot = tot + gath_ref[s]
        lb = lb_ref[...]
        dl = tot[16:32, :] * lb * (1.0 - lb)
        g = jnp.concatenate([tot[0:16, :], dl[0:8, :], -dl[0:8, :], dl[8:16, :], -dl[8:16, :],
                             tot[32:SMALL_GRAD_ROWS, :]], axis=0)
        g_ref[...] = g
        d_ref[...], nm_ref[...], nv_ref[...] = _adamw_math(w_ref[...], g, m_ref[...], v_ref[...])

    vm = pl.BlockSpec(memory_space=pltpu.VMEM)
    shape = jax.ShapeDtypeStruct((SMALL_ROWS, LANE), F32)
    return pl.pallas_call(
        body, name="small_all_reduce_adamw", in_specs=[vm] * 5 + [_any_spec()], out_specs=[vm] * 4,
        out_shape=[shape] * 4,
        scratch_shapes=[pltpu.VMEM((N_DEV, SMALL_GRAD_ROWS, LANE), F32),
                        pltpu.SemaphoreType.DMA((N_DEV - 1,)), pltpu.SemaphoreType.DMA((N_DEV - 1,))],
    )(gpart, lbpack, wpack, mpack, vpack, after)


_SMALL_NAMES = ("norm_mix_gain", "norm_mem_gain", "lb_logits_fw", "lb_logits_bw", "norm_ffn_gain",
                "hg_norm_gain", "da_q_gain", "da_k_gain", "mem_q_gain", "mem_k_gain")
_SMALL_ROW0 = {"norm_mix_gain": 0, "norm_mem_gain": 8, "lb_logits_fw": 16, "lb_logits_bw": 32, "norm_ffn_gain": 48,
               "hg_norm_gain": 56, "da_q_gain": 64, "da_k_gain": 72, "mem_q_gain": 80, "mem_k_gain": 88}
_LOSS_ROW = 96


def _pack_rows(parts, total_rows):
    rows = []
    for p in parts:
        r = p.reshape(-1, LANE)
        rows.append(jnp.pad(r, ((0, -r.shape[0] % 8), (0, 0))))
    used = sum(r.shape[0] for r in rows)
    if total_rows > used:
        rows.append(jnp.zeros((total_rows - used, LANE), F32))
    return jnp.concatenate(rows, axis=0)


def _unpack_small(pack, like):
    out = {}
    for name in _SMALL_NAMES:
        n = like[name].size // LANE
        r0 = _SMALL_ROW0[name]
        out[name] = pack[r0:r0 + n].reshape(like[name].shape)
    return out


def kernel(x, mem, norm_mix_gain, norm_mem_gain, w_in, lb_logits_fw, lb_logits_bw, hg_norm_gain, da_q_gain, da_k_gain, w_mem_kv, mem_q_gain, mem_k_gain, w_proj_hg, w_proj_da, w_proj_mem, w_out, norm_ffn_gain, w_ffn_in, w_ffn_out, loss_target, m_norm_mix_gain, m_norm_mem_gain, m_w_in, m_lb_logits_fw, m_lb_logits_bw, m_hg_norm_gain, m_da_q_gain, m_da_k_gain, m_w_mem_kv, m_mem_q_gain, m_mem_k_gain, m_w_proj_hg, m_w_proj_da, m_w_proj_mem, m_w_out, m_norm_ffn_gain, m_w_ffn_in, m_w_ffn_out, v_norm_mix_gain, v_norm_mem_gain, v_w_in, v_lb_logits_fw, v_lb_logits_bw, v_hg_norm_gain, v_da_q_gain, v_da_k_gain, v_w_mem_kv, v_mem_q_gain, v_mem_k_gain, v_w_proj_hg, v_w_proj_da, v_w_proj_mem, v_w_out, v_norm_ffn_gain, v_w_ffn_in, v_w_ffn_out):
    small_w = dict(norm_mix_gain=norm_mix_gain, norm_mem_gain=norm_mem_gain, lb_logits_fw=lb_logits_fw,
                   lb_logits_bw=lb_logits_bw, norm_ffn_gain=norm_ffn_gain, hg_norm_gain=hg_norm_gain,
                   da_q_gain=da_q_gain, da_k_gain=da_k_gain, mem_q_gain=mem_q_gain, mem_k_gain=mem_k_gain)
    small_m = dict(norm_mix_gain=m_norm_mix_gain, norm_mem_gain=m_norm_mem_gain, lb_logits_fw=m_lb_logits_fw,
                   lb_logits_bw=m_lb_logits_bw, norm_ffn_gain=m_norm_ffn_gain, hg_norm_gain=m_hg_norm_gain,
                   da_q_gain=m_da_q_gain, da_k_gain=m_da_k_gain, mem_q_gain=m_mem_q_gain, mem_k_gain=m_mem_k_gain)
    small_v = dict(norm_mix_gain=v_norm_mix_gain, norm_mem_gain=v_norm_mem_gain, lb_logits_fw=v_lb_logits_fw,
                   lb_logits_bw=v_lb_logits_bw, norm_ffn_gain=v_norm_ffn_gain, hg_norm_gain=v_hg_norm_gain,
                   da_q_gain=v_da_q_gain, da_k_gain=v_da_k_gain, mem_q_gain=v_mem_q_gain, mem_k_gain=v_mem_k_gain)
    big_w = dict(w_in=w_in[0], w_mem_kv=w_mem_kv[0], w_proj_hg=w_proj_hg[0], w_proj_da=w_proj_da[0],
                 w_proj_mem=w_proj_mem[0], w_out=w_out[0], w_ffn_in=w_ffn_in[0], w_ffn_out=w_ffn_out[0])
    big_m = dict(w_in=m_w_in[0], w_mem_kv=m_w_mem_kv[0], w_proj_hg=m_w_proj_hg[0], w_proj_da=m_w_proj_da[0],
                 w_proj_mem=m_w_proj_mem[0], w_out=m_w_out[0], w_ffn_in=m_w_ffn_in[0], w_ffn_out=m_w_ffn_out[0])
    big_v = dict(w_in=v_w_in[0], w_mem_kv=v_w_mem_kv[0], w_proj_hg=v_w_proj_hg[0], w_proj_da=v_w_proj_da[0],
                 w_proj_mem=v_w_proj_mem[0], w_out=v_w_out[0], w_ffn_in=v_w_ffn_in[0], w_ffn_out=v_w_ffn_out[0])

    row_tile = dict(w_in=128, w_mem_kv=128, w_proj_hg=128, w_proj_da=512, w_proj_mem=512, w_out=128,
                    w_ffn_in=128, w_ffn_out=352)
    rest = _BIG_NAMES[1:]
    shards = [big_w[n].astype(BF16) for n in rest]
    state = {}
    big_out = {}

    def reduce_start(group, stacked):
        names = tuple(stacked)
        arrays = [stacked[n] for n in names] + [lax.empty((4,) + stacked[n].shape[1:], BF16) for n in names]
        plan = _reduce_plan_sibling(len(names))
        send, recv, thru, token = _copies_start(f"grads_{group}_sibling_start", arrays, 4 * len(names), plan)
        state[group] = dict(names=names, plan=plan, send=send, recv=recv, arrays=thru)
        return token

    def reduce_middle(group, after):
        st = state[group]
        names, k = st["names"], len(st["names"])
        thru = _copies_wait(f"grads_{group}_sibling_wait", st["send"], st["recv"], st["arrays"], 4 * k, st["plan"], after)
        partials = [_chip_partials(thru[i], thru[k + i], f"chip_partials_{n}", row_tile[n]) for i, n in enumerate(names)]
        arrays = [p[1] for p in partials] + [lax.empty(p[1].shape, BF16) for p in partials]
        plan = _reduce_plan_chips(k)
        send, recv, thru2, token = _copies_start(f"grads_{group}_chips_start", arrays, 3 * k, plan)
        state[group] = dict(names=names, plan=plan, send=send, recv=recv, arrays=thru2, own=[p[0] for p in partials])
        return token

    def reduce_finish(group, after):
        st = state.pop(group)
        names, k = st["names"], len(st["names"])
        thru = _copies_wait(f"grads_{group}_chips_wait", st["send"], st["recv"], st["arrays"], 3 * k, st["plan"], after)
        for i, n in enumerate(names):
            big_out[n] = _adamw_shard(st["own"][i], thru[k + i], big_w[n], big_m[n], big_v[n], "adamw_" + n, row_tile[n])

    win_st = _all_gather([big_w["w_in"].astype(BF16)])[0]
    step = _local_step_stages(x[0], mem[0], loss_target[0], small_w, win_st)
    event, payload = next(step)
    local = None
    while True:
        reply = None
        if event == "begin":
            nr = len(rest)
            me = 4 * lax.axis_index("x") + 2 * lax.axis_index("y") + lax.axis_index("c")
            arrays = shards + [lax.dynamic_update_slice(lax.empty((N_DEV,) + s.shape, BF16), s[None], (me, 0, 0))
                               for s in shards]
            plan = _gather_plan_out(nr)
            send, recv, thru, reply = _copies_start("weights_rest_out_start", arrays, 4 * nr, plan)
            state["gather"] = dict(plan=plan, send=send, recv=recv, arrays=thru)
        elif event == "after_gla_fwd":
            st = state["gather"]
            nr = len(rest)
            thru = _copies_wait("weights_rest_out_wait", st["send"], st["recv"], st["arrays"], 4 * nr, st["plan"], payload)
            plan = _gather_plan_pass(nr)
            send, recv, lands, reply = _copies_start("weights_rest_pass_start", thru[nr:], 3 * nr, plan)
            state["gather"] = dict(plan=plan, send=send, recv=recv, arrays=lands)
        elif event == "need_weights":
            st = state.pop("gather")
            nr = len(rest)
            lands = _copies_wait("weights_rest_pass_wait", st["send"], st["recv"], st["arrays"], 3 * nr, st["plan"], payload)
            reply = dict(zip(rest, lands))
        elif event == "grads_ffn":
            reply = reduce_start("ffn", payload)
        elif event == "after_gate_merge_bwd":
            reply = reduce_middle("ffn", payload)
        elif event == "grads_mix":
            reply = reduce_start("mix", payload)
        elif event == "after_da_bwd_g0":
            reply = reduce_middle("mix", payload)
            reduce_finish("ffn", payload)
        elif event == "after_gla_bwd":
            reduce_finish("mix", payload)
        elif event == "grads_in":
            reply = reduce_start("in", payload)
        elif event == "after_proj_bwd_act_top":
            reply = reduce_middle("in", payload)
        elif event == "end":
            local = payload
            reduce_finish("in", [local["grad_x"]] + [big_out[n][0] for n in rest])
            break
        event, payload = step.send(reply)
    grad_x = local["grad_x"]

    wpack = _pack_rows([small_w[n] for n in _SMALL_NAMES], SMALL_ROWS)
    mpack = _pack_rows([small_m[n] for n in _SMALL_NAMES], SMALL_ROWS)
    vpack = _pack_rows([small_v[n] for n in _SMALL_NAMES], SMALL_ROWS)
    gs, ds, ms, vs = _small_all_reduce_adamw(local["gpart"], local["lbpack"], wpack, mpack, vpack, big_out["w_in"][0])
    loss = gs[_LOSS_ROW, 0]
    small_out = [_unpack_small(t, small_w) for t in (gs, ds, ms, vs)]

    order = ("norm_mix_gain", "norm_mem_gain", "w_in", "lb_logits_fw", "lb_logits_bw", "hg_norm_gain", "da_q_gain",
             "da_k_gain", "w_mem_kv", "mem_q_gain", "mem_k_gain", "w_proj_hg", "w_proj_da", "w_proj_mem", "w_out",
             "norm_ffn_gain", "w_ffn_in", "w_ffn_out")
    outs = [loss, grad_x[None]]
    for kind in range(4):
        for n in order:
            outs.append(big_out[n][kind][None] if n in big_out else small_out[kind][n])
    return tuple(outs)


_BIG_NAMES = ("w_in", "w_mem_kv", "w_proj_hg", "w_proj_da", "w_proj_mem", "w_out", "w_ffn_in", "w_ffn_out")


def _local_step(xs, mems, target, sw, wg):
    step = _local_step_stages(xs, mems, target, sw, wg["w_in"])
    stacked = {}
    event, payload = next(step)
    while event != "end":
        if event.startswith("grads_"):
            stacked.update(payload)
        event, payload = step.send(wg if event == "need_weights" else None)
    return dict(payload, stacked=stacked)


def _local_step_stages(xs, mems, target, sw, win_st):
    norm_mix_gain, norm_mem_gain, norm_ffn_gain = sw["norm_mix_gain"], sw["norm_mem_gain"], sw["norm_ffn_gain"]
    lb_logits_fw, lb_logits_bw, hg_norm_gain = sw["lb_logits_fw"], sw["lb_logits_bw"], sw["hg_norm_gain"]
    da_q_gain, da_k_gain, mem_q_gain, mem_k_gain = sw["da_q_gain"], sw["da_k_gain"], sw["mem_q_gain"], sw["mem_k_gain"]

    token = yield "begin", None
    lb_fw = _lb_table(lb_logits_fw, "lb_table_fw")
    lb_bw = _lb_table(lb_logits_bw, "lb_table_bw")
    lb = jnp.concatenate([lb_fw, lb_bw], axis=0).reshape(2, HG_HEADS, 1, HEAD_DIM)
    h, h_t = _rmsnorm_fwd(xs, _after(norm_mix_gain, token), "norm_mix_fwd", 512, transposed=True)
    proj = _matmul(h, win_st, "nn", F32, 1024, IN_SHARD, D_MODEL, "proj_fwd", b_stacked=True, n_outer=True)
    o_hg, o_pre, states, hg_qt, hg_decay = _gla_fwd(proj, lb, hg_norm_gain)
    token = yield "after_gla_fwd", o_hg
    da = [_da_fwd(proj, _after(da_q_gain, token), da_k_gain, g) for g in range(3)]
    o_da, o_da32, lse_da = _da_merge([t[0] for t in da], [t[1] for t in da], 512)
    wg = yield "need_weights", o_da
    wkv = wg["w_mem_kv"].reshape(D_MODEL, 2 * MEM_WIDTH)
    wphg = wg["w_proj_hg"].reshape(D_MODEL, D_MODEL)
    wpda = jnp.transpose(wg["w_proj_da"], (1, 0, 2)).reshape(DA_WIDTH, D_MODEL)
    wpmem = jnp.transpose(wg["w_proj_mem"], (1, 0, 2)).reshape(MEM_WIDTH, D_MODEL)
    wout = wg["w_out"].reshape(D_MODEL, D_MODEL)
    wfin = jnp.transpose(wg["w_ffn_in"], (1, 0, 2)).reshape(D_MODEL, 2 * D_FF)
    wfout = wg["w_ffn_out"].reshape(D_FF, D_MODEL)
    mem_n = _rmsnorm_fwd(mems, norm_mem_gain, "norm_mem_fwd", N_MEM)
    kv = _matmul(mem_n, wkv, "nn", F32, N_MEM, 1024, D_MODEL, "mem_kv_fwd")
    o_mem = _mem_fwd(proj, kv, mem_q_gain, mem_k_gain, 1024)
    branch_w = (wphg, wpda, wpmem)
    merged, t_hg, t_da, t_mem = _branch_merge_fwd(proj, (o_hg, o_da, o_mem), branch_w, 512)
    x1, h2, h2_t = _residual_rmsnorm_fwd(xs, merged, wout, norm_ffn_gain, "out_norm_ffn_fwd", 512)
    ffn_a, ffn_b, act, act_t = _ffn_in_swiglu(h2, wfin, 1024, 1408)
    dy, dyb, loss_part = _ffn_out_loss_head(x1, act, wfout, target, 512)

    dab = _ffn_out_bwd_swiglu(dyb, wfout, ffn_a, ffn_b, 1024, 1408)
    g_wfout = _matmul(act_t, dyb, "nn", BF16, 1408, 1024, 2048, "ffn_out_bwd_w")
    g_wfin = _matmul(h2_t, dab, "nn", BF16, 1024, 1408, 2048, "ffn_in_bwd_w", b_parts=True)
    token = yield "grads_ffn", dict(
        w_ffn_in=jnp.transpose(g_wfin.reshape(D_MODEL, N_DEV, 2 * D_FF // N_DEV), (1, 0, 2)),
        w_ffn_out=g_wfout.reshape(N_DEV, D_FF // N_DEV, D_MODEL))
    dx1, dx1b, g_norm_ffn = _matmul_rmsnorm_bwd(dab, wfin, x1, dy, _after(norm_ffn_gain, token),
                                                "ffn_in_bwd_act_norm", 512, D_FF, a_parts=True)
    dmerged = _matmul(dx1b, wout, "nt", F32, 512, 1024, D_MODEL, "out_bwd_act")
    g_wout = _matmul(merged, dx1b, "tn", BF16, 1024, 1024, 2048, "out_bwd_w")
    dt_hg, dt_da, dt_mem, do_hg, do_da, do_mem, dproj = _gate_merge_bwd(
        proj, (t_hg, t_da, t_mem), branch_w, dmerged, _dproj_buffer(), 256)
    token = yield "after_gate_merge_bwd", dt_hg
    g_wphg = _matmul(o_hg, dt_hg, "tn", BF16, 1024, 1024, 2048, "proj_hg_bwd_w", after=token)
    g_wpda = _matmul(o_da, dt_da, "tn", BF16, DA_WIDTH, D_MODEL, 2048, "proj_da_bwd_w")
    g_wpmem = _matmul(o_mem, dt_mem, "tn", BF16, MEM_WIDTH, D_MODEL, 2048, "proj_mem_bwd_w")
    by_owner = lambda g: jnp.transpose(g.reshape(g.shape[0], N_DEV, D_MODEL // N_DEV), (1, 0, 2))
    g_wpda, g_wpmem = by_owner(g_wpda), by_owner(g_wpmem)

    dproj, dk_mem, dv_mem, g_mem_q, g_mem_k = _mem_bwd(proj, kv, mem_q_gain, mem_k_gain, do_mem, dproj, 1024)
    dkv = jnp.concatenate([dk_mem, dv_mem], axis=1).astype(BF16)
    g_wkv = _matmul(mem_n, dkv, "tn", BF16, 1024, 1024, N_MEM, "mem_kv_bwd_w")
    dmem_n = _matmul(dkv, wkv, "nt", F32, N_MEM, 1024, 1024, "mem_kv_bwd_act")
    g_norm_mem = _gain_grad(mems, dmem_n, "norm_mem_bwd")
    token = yield "grads_mix", dict(
        w_mem_kv=g_wkv.reshape(N_DEV, D_MODEL // N_DEV, 2 * MEM_WIDTH),
        w_proj_hg=g_wphg.reshape(N_DEV, D_MODEL // N_DEV, D_MODEL),
        w_proj_da=g_wpda, w_proj_mem=g_wpmem,
        w_out=g_wout.reshape(N_DEV, D_MODEL // N_DEV, D_MODEL))

    dd_da = _da_rowdot(do_da, o_da32, 512)
    dproj, g_da_q, g_da_k = _da_bwd(proj, _after(da_q_gain, token), da_k_gain, do_da, lse_da, dd_da, dproj, 0)
    token = yield "after_da_bwd_g0", g_da_q
    for g in (1, 2):
        dproj, gq_part, gk_part = _da_bwd(proj, _after(da_q_gain, token), da_k_gain, do_da, lse_da, dd_da, dproj, g)
        g_da_q, g_da_k = g_da_q + gq_part, g_da_k + gk_part

    dproj, dlb, g_hg_norm = _gla_bwd(proj, lb, hg_norm_gain, o_pre, states, hg_qt, hg_decay, do_hg, dproj)
    yield "after_gla_bwd", dlb

    g_win =_matmul(h_t, dproj, "nn", BF16, 1024, IN_SHARD, 2048, "proj_bwd_w", out_stacked=True)
    token = yield "grads_in", dict(w_in=g_win)
    grad_x, g_mix_top = _matmul_rmsnorm_bwd(dproj, win_st, xs, dx1, norm_mix_gain, "proj_bwd_act_norm_top", 512,
                                            IN_SHARD, b_stacked=True, after=token, m_blocks=(0, 4), with_bf16=False)
    token = yield "after_proj_bwd_act_top", g_mix_top
    grad_x, g_mix_bottom = _matmul_rmsnorm_bwd(dproj, win_st, xs, dx1, norm_mix_gain, "proj_bwd_act_norm_bottom", 512,
                                               IN_SHARD, b_stacked=True, after=token, m_blocks=(4, 4),
                                               out_into=grad_x, with_bf16=False)
    g_norm_mix = g_mix_top + g_mix_bottom

    gpart = _pack_rows([g_norm_mix, g_norm_mem, dlb[0], dlb[1], g_norm_ffn, g_hg_norm, g_da_q, g_da_k,
                        g_mem_q, g_mem_k, loss_part], SMALL_GRAD_ROWS)
    lbpack = jnp.concatenate([lb_fw.reshape(8, LANE), lb_bw.reshape(8, LANE)], axis=0)
    yield "end", dict(grad_x=grad_x, gpart=gpart, lbpack=lbpack)
```

```python
import numpy as np
import jax
import jax.numpy as jnp
from jax import lax
from jax.experimental import pallas as pl
from jax.experimental.pallas import tpu as pltpu

F32 = jnp.float32
BF16 = jnp.bfloat16
MESH = pl.DeviceIdType.MESH

SEQ = 4096
D_MODEL = 1024
N_DEV = 8
N_MEM = 256
RMS_EPS = 1e-6
NEG_INF = -1e30
LANE = 128
HEAD_DIM = 128
HG_HEADS = 8
HG_CHUNK = 64
HG_SCALE = HEAD_DIM ** -0.5
DA_DILATIONS = (1, 4, 16)
DA_RADIUS = 64
DA_HEADS_PER_GROUP = 4
DA_HEADS = 12
DA_WIDTH = 512
DA_SCALE = HEAD_DIM ** -0.5
DA_QB = 128
DA_WIN = 256
DA_WAYS = 4
DA_FWD_WAYS = 4
MEM_HEADS = 4
MEM_WIDTH = 512
MEM_SCALE = HEAD_DIM ** -0.5
D_FF = 2816
IN_COLS = 13312
IN_SHARD = IN_COLS // N_DEV
CB_HG_Q, CB_F, CB_HG_I, CB_HG_G = 0, 8, 24, 32
CB_DA_Q, CB_DA_K, CB_DA_V, CB_MEM_Q = 40, 52, 64, 76
ADAM_LR, ADAM_B1, ADAM_B2, ADAM_EPS, ADAM_WD, ADAM_STEP = 0.001, 0.9, 0.999, 1e-08, 0.01, 10
VMEM_BYTES_V7X = 64 * 1024 * 1024
SMALL_ROWS = 104
SMALL_GRAD_ROWS = 88

_NN = (((1,), (0,)), ((), ()))
_NT = (((1,), (1,)), ((), ()))
_TN = (((0,), (0,)), ((), ()))


def _dot(a, b, dims):
    return lax.dot_general(a.astype(BF16), b.astype(BF16), dims, preferred_element_type=F32)


def _sigmoid(x):
    return 0.5 * jnp.tanh(0.5 * x) + 0.5


def _params(semantics, est_bytes):
    limit = int(min(VMEM_BYTES_V7X - (6 << 20), max(56 << 20, est_bytes * 3 // 2)))
    return pltpu.CompilerParams(dimension_semantics=semantics, vmem_limit_bytes=limit)


def _nbytes(shape, dtype):
    return int(np.prod(shape)) * jnp.dtype(dtype).itemsize


def _alibi_slopes(n):
    return (2.0 ** (-8.0 * np.arange(1, n + 1) / n)).astype(np.float32)


def _matmul(a, b, mode, out_dtype, tm, tn, tk, name, b_stacked=False, out_stacked=False, n_outer=False, after=None,
            m_blocks=None, out_into=None, a_parts=False, b_parts=False):
    if a_parts:
        assert mode == "nt" and tk == a.shape[2]
        m, kdim = a.shape[1], a.shape[0] * a.shape[2]
    elif mode == "tn":
        kdim, m = a.shape
    else:
        m, kdim = a.shape
    if b_parts:
        assert mode == "nn" and not b_stacked and b.shape[2] % tn == 0
        n = b.shape[0] * b.shape[2]
    elif b_stacked:
        if mode == "nn":
            n = b.shape[0] * b.shape[2]
            assert tn == b.shape[2] and tk == kdim == b.shape[1]
        else:
            assert mode == "nt" and tk == b.shape[2] and b.shape[0] * tk == kdim
            n = b.shape[1]
    else:
        n = b.shape[0] if mode == "nt" else b.shape[1]
    assert m % tm == 0 and n % tn == 0 and kdim % tk == 0
    gm, gn, gk = m // tm, n // tn, kdim // tk
    i0 = 0
    if m_blocks is not None:
        assert mode != "tn" and not out_stacked
        i0, gm = m_blocks

    def ijk(f):
        if n_outer:
            return lambda j, i, k: f(i + i0, j, k)
        return lambda i, j, k: f(i + i0, j, k)

    if a_parts:
        a_spec = pl.BlockSpec((None, tm, tk), ijk(lambda i, j, k: (k, i, 0)))
    elif mode == "tn":
        a_spec = pl.BlockSpec((tk, tm), ijk(lambda i, j, k: (k, i)))
    else:
        a_spec = pl.BlockSpec((tm, tk), ijk(lambda i, j, k: (i, k)))
    if b_parts:
        per_part = b.shape[2] // tn
        b_spec = pl.BlockSpec((None, tk, tn), ijk(lambda i, j, k: (j // per_part, k, j % per_part)))
    elif b_stacked and mode == "nn":
        b_spec = pl.BlockSpec((None, tk, tn), ijk(lambda i, j, k: (j, 0, 0)))
    elif b_stacked:
        b_spec = pl.BlockSpec((None, tn, tk), ijk(lambda i, j, k: (k, j, 0)))
    elif mode == "nt":
        b_spec = pl.BlockSpec((tn, tk), ijk(lambda i, j, k: (j, k)))
    else:
        b_spec = pl.BlockSpec((tk, tn), ijk(lambda i, j, k: (k, j)))
    if out_stacked:
        assert tm == m
        out_shape = jax.ShapeDtypeStruct((gn, m, tn), out_dtype)
        o_spec = pl.BlockSpec((None, tm, tn), ijk(lambda i, j, k: (j, i, 0)))
    else:
        out_shape = jax.ShapeDtypeStruct((m, n), out_dtype)
        o_spec = pl.BlockSpec((tm, tn), ijk(lambda i, j, k: (i, j)))
    dims = {"nn": _NN, "nt": _NT, "tn": _TN}[mode]

    n_in = 2 + (after is not None) + (out_into is not None)

    def body(*refs):
        a_ref, b_ref, o_ref = refs[0], refs[1], refs[n_in]
        part = _dot(a_ref[...], b_ref[...], dims)
        if gk == 1:
            o_ref[...] = part.astype(out_dtype)
            return
        acc_ref = refs[-1]
        k = pl.program_id(2)

        @pl.when(k == 0)
        def _():
            acc_ref[...] = part

        @pl.when(jnp.logical_and(k > 0, k < gk - 1))
        def _():
            acc_ref[...] += part

        @pl.when(k == gk - 1)
        def _():
            o_ref[...] = (acc_ref[...] + part).astype(out_dtype)

    a_tile = _nbytes((tm, tk), a.dtype)
    b_tile = _nbytes((tk, tn), b.dtype)
    o_tile = _nbytes((tm, tn), out_dtype)
    est = 2 * (a_tile + b_tile + o_tile) + 3 * tm * tn * 4 + (a_tile + b_tile)
    grid = (gn, gm, gk) if n_outer else (gm, gn, gk)
    operands, in_specs = [a, b], [a_spec, b_spec]
    if after is not None:
        operands.append(after)
        in_specs.append(pl.BlockSpec(memory_space=pl.ANY))
    aliases = {}
    if out_into is not None:
        aliases = {len(operands): 0}
        operands.append(out_into)
        in_specs.append(pl.BlockSpec(memory_space=pl.ANY))
    return pl.pallas_call(
        body, name=name, grid=grid, in_specs=in_specs, out_specs=o_spec, out_shape=out_shape,
        scratch_shapes=[] if gk == 1 else [pltpu.VMEM((tm, tn), F32)], input_output_aliases=aliases,
        compiler_params=_params(("parallel", "parallel", "arbitrary"), est),
    )(*operands)


def _row_spec(tr, width, col_block=0):
    return pl.BlockSpec((tr, width), lambda i: (i, col_block))


def _bcast_spec(width):
    return pl.BlockSpec((1, width), lambda i: (0, 0))


def _col_spec(width, tr):
    return pl.BlockSpec((width, tr), lambda i: (0, i))


def _rmsnorm_fwd(x, gain, name, tr, transposed=False):
    rows, width = x.shape

    def body(x_ref, g_ref, o_ref, *t_ref):
        xv = x_ref[...]
        r = lax.rsqrt(jnp.mean(xv * xv, axis=-1, keepdims=True) + RMS_EPS)
        h = xv * r * g_ref[...]
        o_ref[...] = h.astype(BF16)
        if transposed:
            t_ref[0][...] = h.T.astype(BF16)

    out_specs, out_shape = [_row_spec(tr, width)], [jax.ShapeDtypeStruct((rows, width), BF16)]
    if transposed:
        out_specs.append(_col_spec(width, tr))
        out_shape.append(jax.ShapeDtypeStruct((width, rows), BF16))
    out = pl.pallas_call(
        body, name=name, grid=(rows // tr,), in_specs=[_row_spec(tr, width), _bcast_spec(width)],
        out_specs=out_specs, out_shape=out_shape,
        compiler_params=_params(("parallel",), 10 * tr * width * 4),
    )(x, gain)
    return out if transposed else out[0]


def _residual_rmsnorm_fwd(x, merged, wout, gain, name, tr):
    rows, width = x.shape

    def body(x_ref, m_ref, w_ref, g_ref, x1_ref, h_ref, ht_ref):
        xv = x_ref[...] + _dot(m_ref[...], w_ref[...], _NN)
        x1_ref[...] = xv
        r = lax.rsqrt(jnp.mean(xv * xv, axis=-1, keepdims=True) + RMS_EPS)
        h = xv * r * g_ref[...]
        h_ref[...] = h.astype(BF16)
        ht_ref[...] = h.T.astype(BF16)

    return pl.pallas_call(
        body, name=name, grid=(rows // tr,),
        in_specs=[_row_spec(tr, width), _row_spec(tr, merged.shape[1]),
                  pl.BlockSpec(wout.shape, lambda i: (0, 0)), _bcast_spec(width)],
        out_specs=[_row_spec(tr, width), _row_spec(tr, width), _col_spec(width, tr)],
        out_shape=[jax.ShapeDtypeStruct((rows, width), F32), jax.ShapeDtypeStruct((rows, width), BF16),
                   jax.ShapeDtypeStruct((width, rows), BF16)],
        compiler_params=_params(("parallel",), 14 * tr * width * 4),
    )(x, merged, wout, gain)


def _matmul_rmsnorm_bwd(a, b, x, dres, gain, name, tm, tk, a_parts=False, b_stacked=False, after=None,
                        m_blocks=None, out_into=None, with_bf16=True):
    width = x.shape[1]
    m = a.shape[1] if a_parts else a.shape[0]
    kdim = a.shape[0] * a.shape[2] if a_parts else a.shape[1]
    gk = kdim // tk
    i0, gm = (0, m // tm) if m_blocks is None else m_blocks
    n_in = 5 + (after is not None) + (out_into is not None)

    def body(*refs):
        a_ref, b_ref, x_ref, dres_ref, g_ref = refs[:5]
        outs = refs[n_in:]
        dx_ref, dg_ref, acc_ref = outs[0], outs[-2], outs[-1]
        i, k = pl.program_id(0), pl.program_id(1)
        part = _dot(a_ref[...], b_ref[...], _NT)

        @pl.when(k == 0)
        def _():
            acc_ref[...] = part

        @pl.when(jnp.logical_and(k > 0, k < gk - 1))
        def _():
            acc_ref[...] += part

        @pl.when(k == gk - 1)
        def _():
            dhv = acc_ref[...] + part if gk > 1 else part
            xv = x_ref[...]
            r = lax.rsqrt(jnp.mean(xv * xv, axis=-1, keepdims=True) + RMS_EPS)
            xhat = xv * r
            dyg = dhv * g_ref[...]
            dx = dres_ref[...] + r * (dyg - xhat * jnp.mean(dyg * xhat, axis=-1, keepdims=True))
            dx_ref[...] = dx
            if with_bf16:
                outs[1][...] = dx.astype(BF16)
            gpart = jnp.sum(dhv * xhat, axis=0, keepdims=True)

            @pl.when(i == 0)
            def _():
                dg_ref[...] = gpart

            @pl.when(i > 0)
            def _():
                dg_ref[...] += gpart

    rows = lambda width_: pl.BlockSpec((tm, width_), lambda i, k: (i + i0, 0))
    if a_parts:
        a_spec = pl.BlockSpec((None, tm, tk), lambda i, k: (k, i + i0, 0))
    else:
        a_spec = pl.BlockSpec((tm, tk), lambda i, k: (i + i0, k))
    if b_stacked:
        b_spec = pl.BlockSpec((None, width, tk), lambda i, k: (k, 0, 0))
    else:
        b_spec = pl.BlockSpec((width, tk), lambda i, k: (0, k))
    operands = [a, b, x, dres, gain]
    in_specs = [a_spec, b_spec, rows(width), rows(width), pl.BlockSpec((1, width), lambda i, k: (0, 0))]
    for extra in (after, out_into):
        if extra is not None:
            operands.append(extra)
            in_specs.append(pl.BlockSpec(memory_space=pl.ANY))
    aliases = {} if out_into is None else {len(operands) - 1: 0}
    out_specs = [rows(width)] + ([rows(width)] if with_bf16 else []) + [pl.BlockSpec((1, width), lambda i, k: (0, 0))]
    out_shape = [jax.ShapeDtypeStruct((m, width), F32)] + ([jax.ShapeDtypeStruct((m, width), BF16)] if with_bf16 else [])
    out_shape.append(jax.ShapeDtypeStruct((1, width), F32))
    est = 4 * tm * tk + 4 * width * tk + 12 * tm * width * 4
    return pl.pallas_call(
        body, name=name, grid=(gm, gk), in_specs=in_specs, out_specs=out_specs, out_shape=out_shape,
        scratch_shapes=[pltpu.VMEM((tm, width), F32)], input_output_aliases=aliases,
        compiler_params=_params(("arbitrary", "arbitrary"), est),
    )(*operands)


def _gain_grad(x, dh, name):
    rows, width = x.shape

    def body(x_ref, dh_ref, dg_ref):
        xv = x_ref[...]
        r = lax.rsqrt(jnp.mean(xv * xv, axis=-1, keepdims=True) + RMS_EPS)
        dg_ref[...] = jnp.sum(dh_ref[...] * xv * r, axis=0, keepdims=True)

    return pl.pallas_call(
        body, name=name, grid=(1,), in_specs=[_row_spec(rows, width), _row_spec(rows, width)],
        out_specs=_bcast_spec(width), out_shape=jax.ShapeDtypeStruct((1, width), F32),
        compiler_params=_params(("arbitrary",), 6 * rows * width * 4),
    )(x, dh)


def _lb_table(logits, name):
    slots, width = logits.shape

    def body(l_ref, o_ref):
        lv = l_ref[...]
        mx = jnp.max(lv, axis=0, keepdims=True)
        e = jnp.exp(lv - mx)
        o_ref[...] = e[0:1, :] / jnp.sum(e, axis=0, keepdims=True)

    return pl.pallas_call(
        body, name=name, grid=(1,), in_specs=[pl.BlockSpec((slots, width), lambda i: (0, 0))],
        out_specs=_bcast_spec(width), out_shape=jax.ShapeDtypeStruct((1, width), F32),
    )(logits)


def _branch_merge_fwd(proj, outs, weights, tr):
    w = D_MODEL

    def body(ghg_ref, gda_ref, gmem_ref, ohg_ref, oda_ref, omem_ref, whg_ref, wda_ref, wmem_ref,
             m_ref, thg_ref, tda_ref, tmem_ref):
        acc = None
        for g_ref, o_ref, w_ref, t_ref in ((ghg_ref, ohg_ref, whg_ref, thg_ref), (gda_ref, oda_ref, wda_ref, tda_ref),
                                           (gmem_ref, omem_ref, wmem_ref, tmem_ref)):
            t = _dot(o_ref[...], w_ref[...], _NN)
            t_ref[...] = t.astype(BF16)
            term = _sigmoid(g_ref[...]) * t
            acc = term if acc is None else acc + term
        m_ref[...] = acc.astype(BF16)

    whole = lambda a: pl.BlockSpec(a.shape, lambda i: (0, 0))
    shape = jax.ShapeDtypeStruct((SEQ, w), BF16)
    return pl.pallas_call(
        body, name="branch_merge_fwd", grid=(SEQ // tr,),
        in_specs=[_row_spec(tr, w, 10), _row_spec(tr, w, 11), _row_spec(tr, w, 12)]
        + [_row_spec(tr, o.shape[1]) for o in outs] + [whole(wt) for wt in weights],
        out_specs=[_row_spec(tr, w)] * 4, out_shape=[shape] * 4,
        compiler_params=_params(("parallel",), 20 * tr * w * 4),
    )(proj, proj, proj, *outs, *weights)


def _dproj_buffer():
    return lax.empty((SEQ, IN_COLS), BF16)


def _gate_merge_bwd(proj, ts, weights, dmerged, dproj, tr):
    w = D_MODEL
    steps = SEQ // tr
    gate_col0 = 10 * w

    def body(ghg_ref, gda_ref, gmem_ref, thg_ref, tda_ref, tmem_ref, whg_ref, wda_ref, wmem_ref, dm_ref, dproj_in,
             dthg_ref, dtda_ref, dtmem_ref, dohg_ref, doda_ref, domem_ref, dproj_ref, stage, sems):
        del dproj_in
        i = pl.program_id(0)
        slot = i % 2

        def slot_copy(s):
            rows = pl.ds(pl.multiple_of(i * tr, tr), tr)
            return pltpu.make_async_copy(stage.at[s], dproj_ref.at[rows, pl.ds(gate_col0, 3 * w)], sems.at[s])

        @pl.when(i >= 2)
        def _():
            slot_copy(slot).wait()

        dm = dm_ref[...]
        branches = ((ghg_ref, thg_ref, whg_ref, dthg_ref, dohg_ref), (gda_ref, tda_ref, wda_ref, dtda_ref, doda_ref),
                    (gmem_ref, tmem_ref, wmem_ref, dtmem_ref, domem_ref))
        for b, (g_ref, t_ref, w_ref, dt_ref, do_ref) in enumerate(branches):
            s = _sigmoid(g_ref[...])
            dt = (s * dm).astype(BF16)
            dt_ref[...] = dt
            do_ref[...] = _dot(dt, w_ref[...], _NT)
            stage[slot, :, b * w:(b + 1) * w] = (dm * t_ref[...].astype(F32) * s * (1.0 - s)).astype(BF16)
        slot_copy(slot).start()

        @pl.when(i == steps - 1)
        def _():
            slot_copy(1 - slot).wait()
            slot_copy(slot).wait()

    assert steps >= 2
    whole = lambda a: pl.BlockSpec(a.shape, lambda i: (0, 0))
    return pl.pallas_call(
        body, name="gate_merge_bwd", grid=(steps,),
        in_specs=[_row_spec(tr, w, 10), _row_spec(tr, w, 11), _row_spec(tr, w, 12)] + [_row_spec(tr, w)] * 3
        + [whole(wt) for wt in weights] + [_row_spec(tr, w), _any_spec()],
        out_specs=[_row_spec(tr, w)] * 3 + [_row_spec(tr, wt.shape[0]) for wt in weights] + [_any_spec()],
        out_shape=[jax.ShapeDtypeStruct((SEQ, w), BF16)] * 3
        + [jax.ShapeDtypeStruct((SEQ, wt.shape[0]), F32) for wt in weights]
        + [jax.ShapeDtypeStruct((SEQ, IN_COLS), BF16)],
        scratch_shapes=[pltpu.VMEM((2, tr, 3 * w), BF16), pltpu.SemaphoreType.DMA((2,))],
        input_output_aliases={10: 6},
        compiler_params=_params(("arbitrary",), 34 * tr * w * 4),
    )(proj, proj, proj, *ts, *weights, dmerged, dproj)


def _ffn_in_swiglu(h2, wfin, tm, tn):
    gm, gn = SEQ // tm, D_FF // tn

    def body(h_ref, wa_ref, wb_ref, a_ref, b_ref, act_ref, actt_ref):
        h = h_ref[...]
        a = _dot(h, wa_ref[...], _NN)
        b = _dot(h, wb_ref[...], _NN)
        act = a * _sigmoid(a) * b
        a_ref[...] = a.astype(BF16)
        b_ref[...] = b.astype(BF16)
        act_ref[...] = act.astype(BF16)
        actt_ref[...] = act.T.astype(BF16)

    tile = pl.BlockSpec((tm, tn), lambda j, i: (i, j))
    shape = jax.ShapeDtypeStruct((SEQ, D_FF), BF16)
    return pl.pallas_call(
        body, name="ffn_in_swiglu_fwd", grid=(gn, gm),
        in_specs=[pl.BlockSpec((tm, D_MODEL), lambda j, i: (i, 0)),
                  pl.BlockSpec((D_MODEL, tn), lambda j, i: (0, j)),
                  pl.BlockSpec((D_MODEL, tn), lambda j, i: (0, gn + j))],
        out_specs=[tile, tile, tile, pl.BlockSpec((tn, tm), lambda j, i: (j, i))],
        out_shape=[shape, shape, shape, jax.ShapeDtypeStruct((D_FF, SEQ), BF16)],
        compiler_params=_params(("parallel", "parallel"), 4 * tm * D_MODEL + 8 * D_MODEL * tn + 16 * tm * tn
                                + 6 * tm * tn * 4),
    )(h2, wfin, wfin)


def _ffn_out_bwd_swiglu(dyb, wfout, a, b, tm, tn):
    gm, gn = SEQ // tm, D_FF // tn

    def body(dy_ref, w_ref, a_ref, b_ref, o_ref):
        d = _dot(dy_ref[...], w_ref[...], _NT)
        av = a_ref[...].astype(F32)
        bv = b_ref[...].astype(F32)
        s = _sigmoid(av)
        silu = av * s
        o_ref[0] = (d * bv * (s + silu * (1.0 - s))).astype(BF16)
        o_ref[1] = (d * silu).astype(BF16)

    tile = pl.BlockSpec((tm, tn), lambda i, j: (i, j))
    return pl.pallas_call(
        body, name="ffn_out_bwd_swiglu", grid=(gm, gn),
        in_specs=[pl.BlockSpec((tm, D_MODEL), lambda i, j: (i, 0)), pl.BlockSpec((tn, D_MODEL), lambda i, j: (j, 0)),
                  tile, tile],
        out_specs=pl.BlockSpec((2, tm, tn), lambda i, j: (0, i, j)),
        out_shape=jax.ShapeDtypeStruct((2, SEQ, D_FF), BF16),
        compiler_params=_params(("parallel", "parallel"), 4 * tm * D_MODEL + 4 * tn * D_MODEL + 16 * tm * tn
                                + 6 * tm * tn * 4),
    )(dyb, wfout, a, b)


def _ffn_out_loss_head(x1, act, wfout, target, tr):
    w = D_MODEL

    def body(x_ref, a_ref, w_ref, t_ref, dy_ref, dyb_ref, loss_ref, acc_ref):
        err = x_ref[...] + _dot(a_ref[...], w_ref[...], _NN) - t_ref[...]
        dy = err * (1.0 / w)
        dy_ref[...] = dy
        dyb_ref[...] = dy.astype(BF16)
        part = jnp.sum(err * err, axis=0, keepdims=True)

        @pl.when(pl.program_id(0) == 0)
        def _():
            acc_ref[...] = part

        @pl.when(pl.program_id(0) > 0)
        def _():
            acc_ref[...] += part

        @pl.when(pl.program_id(0) == SEQ // tr - 1)
        def _():
            total = jnp.sum(acc_ref[...], axis=1, keepdims=True) * (0.5 / w)
            loss_ref[...] = jnp.broadcast_to(total, (1, LANE))

    return pl.pallas_call(
        body, name="ffn_out_loss_head", grid=(SEQ // tr,),
        in_specs=[_row_spec(tr, w), _row_spec(tr, D_FF), pl.BlockSpec((D_FF, w), lambda i: (0, 0)), _row_spec(tr, w)],
        out_specs=[_row_spec(tr, w), _row_spec(tr, w), _bcast_spec(LANE)],
        out_shape=[jax.ShapeDtypeStruct((SEQ, w), F32), jax.ShapeDtypeStruct((SEQ, w), BF16),
                   jax.ShapeDtypeStruct((1, LANE), F32)],
        scratch_shapes=[pltpu.VMEM((1, w), F32)],
        compiler_params=_params(("arbitrary",), 12 * tr * w * 4 + 4 * D_FF * w),
    )(x1, act, wfout, target)


GLA_ROWS = 256
GLA_CPB = GLA_ROWS // HG_CHUNK
GLA_NBLK = SEQ // GLA_ROWS
GLA_WAYS = 4
GLA_TRIPS = GLA_NBLK // GLA_WAYS
GLA_GRAD_WAYS = 4
GLA_GRAD_TRIPS = GLA_NBLK // GLA_GRAD_WAYS
GLA_NCK = SEQ // HG_CHUNK


def _dot_split(m, xv, dims):
    hi = xv.astype(BF16)
    r1 = xv - hi.astype(F32)
    mid = r1.astype(BF16)
    lo = (r1 - mid.astype(F32)).astype(BF16)
    dot = lambda t: lax.dot_general(m, t, dims, preferred_element_type=F32)
    return (dot(lo) + dot(mid)) + dot(hi)


def _gla_block(qc, fc, lbv, masks):
    mask, maskb, direction = masks
    sq = _sigmoid(qc)
    q = qc * sq * HG_SCALE
    sf = _sigmoid(fc)
    forget = lbv + (1.0 - lbv) * sf
    k = 1.0 - forget
    logf = jnp.log(forget)
    b = _dot_split(maskb, logf, _NN)
    ends = []
    for j in range(GLA_CPB):
        lo, hi = j * HG_CHUNK, (j + 1) * HG_CHUNK
        end = jnp.where(direction == 0, b[hi - 1:hi, :], b[lo:lo + 1, :])
        ends.append(jnp.broadcast_to(end, (HG_CHUNK, HEAD_DIM)))
    bt = jnp.concatenate(ends, axis=0)
    eb = jnp.exp(b)
    qt = q * eb
    kt = k * jnp.exp(-b)
    kh = k * jnp.exp(bt - b)
    a = jnp.where(mask, _dot(qt, kt, _NT), 0.0)
    return dict(sq=sq, sf=sf, forget=forget, k=k, b=b, bt=bt, eb=eb, qt=qt, kt=kt, kh=kh, a=a)


def _gla_masks(direction):
    row = lax.broadcasted_iota(jnp.int32, (GLA_ROWS, GLA_ROWS), 0)
    col = lax.broadcasted_iota(jnp.int32, (GLA_ROWS, GLA_ROWS), 1)
    same = (row // HG_CHUNK) == (col // HG_CHUNK)
    mask = jnp.logical_and(same, jnp.where(direction == 0, row - col, col - row) >= 0)
    return mask, jnp.where(mask, 1.0, 0.0).astype(BF16), direction


def _gla_chunk_rows(j):
    return slice(j * HG_CHUNK, (j + 1) * HG_CHUNK)


def _gla_block_rows(b):
    return pl.ds(pl.multiple_of(b * GLA_ROWS, GLA_ROWS), GLA_ROWS)


def _head_spec(col_block0):
    return pl.BlockSpec((SEQ, HEAD_DIM), lambda h, d: (0, col_block0 + h))


def _gla_fwd(proj, lb, gain):
    nck = GLA_NCK

    def body(q_ref, f_ref, v_ref, g_ref, lb_ref, gain_ref, ohg_ref, opre_ref, st_ref, qt_scr, dec_scr, cs_scr):
        d = pl.program_id(1)
        masks = _gla_masks(d)
        lbv = lb_ref[...]

        @pl.when(d == 0)
        def _():
            opre_ref[...] = jnp.zeros_like(opre_ref)

        def intra(s, carry):
            blocks = [s + w * GLA_TRIPS for w in range(GLA_WAYS)]
            rows = [_gla_block_rows(b) for b in blocks]
            loaded = [(q_ref[r, :], f_ref[r, :], v_ref[r, :], opre_ref[r, :]) for r in rows]
            results = []
            for qc, fc, v, o_prev in loaded:
                ck = _gla_block(qc, fc, lbv, masks)
                o = o_prev + _dot(ck["a"], v, _NN)
                cs = [_dot(v[_gla_chunk_rows(j), :], ck["kh"][_gla_chunk_rows(j), :], _TN) for j in range(GLA_CPB)]
                dec = [jnp.exp(ck["bt"][j * HG_CHUNK:j * HG_CHUNK + 8, :]) for j in range(GLA_CPB)]
                results.append((o, ck["qt"].astype(BF16), cs, dec))
            for b, r, (o, qt, cs, dec) in zip(blocks, rows, results):
                opre_ref[r, :] = o
                qt_scr[r, :] = qt
                for j in range(GLA_CPB):
                    cs_scr[b * GLA_CPB + j] = cs[j]
                    dec_scr[b * GLA_CPB + j] = dec[j]
            return carry

        lax.fori_loop(0, GLA_TRIPS, intra, 0)

        def scan(i, st):
            c = jnp.where(d == 0, i, nck - 1 - i)
            st_ref[c] = st
            return st * dec_scr[c][0:1, :] + cs_scr[c]

        lax.fori_loop(0, nck, scan, jnp.zeros((HEAD_DIM, HEAD_DIM), F32), unroll=4)

        def inter(c, carry):
            rows = pl.ds(pl.multiple_of(c * HG_CHUNK, HG_CHUNK), HG_CHUNK)
            opre_ref[rows, :] += _dot(qt_scr[rows, :], st_ref[c], _NT)
            return carry

        lax.fori_loop(0, nck, inter, 0, unroll=4)

        @pl.when(d == 1)
        def _():
            o = opre_ref[...]
            r = lax.rsqrt(jnp.mean(o * o, axis=-1, keepdims=True) + RMS_EPS)
            g = g_ref[...]
            ohg_ref[...] = (o * r * gain_ref[...] * (g * _sigmoid(g))).astype(BF16)

    blk = SEQ * HEAD_DIM * 4
    return pl.pallas_call(
        body, name="gla_fwd", grid=(HG_HEADS, 2),
        in_specs=[_head_spec(CB_HG_Q),
                  pl.BlockSpec((SEQ, HEAD_DIM), lambda h, d: (0, CB_F + 8 * d + h)),
                  _head_spec(CB_HG_I), _head_spec(CB_HG_G),
                  pl.BlockSpec((None, None, 1, HEAD_DIM), lambda h, d: (d, h, 0, 0)),
                  pl.BlockSpec((1, HEAD_DIM), lambda h, d: (0, 0))],
        out_specs=[_head_spec(0), _head_spec(0),
                   pl.BlockSpec((None, None, nck, HEAD_DIM, HEAD_DIM), lambda h, d: (h, d, 0, 0, 0)),
                   pl.BlockSpec((None, SEQ, HEAD_DIM), lambda h, d: (d, 0, h)),
                   pl.BlockSpec((None, None, nck, 8, HEAD_DIM), lambda h, d: (h, d, 0, 0, 0))],
        out_shape=[jax.ShapeDtypeStruct((SEQ, D_MODEL), BF16), jax.ShapeDtypeStruct((SEQ, D_MODEL), F32),
                   jax.ShapeDtypeStruct((HG_HEADS, 2, nck, HEAD_DIM, HEAD_DIM), F32),
                   jax.ShapeDtypeStruct((2, SEQ, D_MODEL), BF16),
                   jax.ShapeDtypeStruct((HG_HEADS, 2, nck, 8, HEAD_DIM), F32)],
        scratch_shapes=[pltpu.VMEM((nck, HEAD_DIM, HEAD_DIM), F32)],
        compiler_params=_params(("parallel", "arbitrary"), 8 * blk + 3 * blk + 2 * blk + 2 * blk + blk),
    )(proj, proj, proj, proj, lb, gain)


def _gla_bwd(proj, lb, gain, o_pre, states, qt_all, dec_all, do_hg, dproj):
    nck = GLA_NCK
    do, dproj, dgain = _gla_bwd_norm(proj, gain, o_pre, do_hg, dproj, 256)

    def body(q_ref, f_ref, v_ref, lb_ref, do_ref, st_ref, qt_ref, dec_scr, dproj_in, dproj_ref, dlb_ref,
             dq_acc, dv_acc, dst_scr, cs_scr, df_out, dq_out, dv_out, sems):
        del dproj_in
        h, d = pl.program_id(0), pl.program_id(1)
        masks = _gla_masks(d)
        mask, maskb, _ = masks
        lbv = lb_ref[...]

        def column_copy(k, staging, col_block):
            cols = pl.ds(pl.multiple_of(col_block * LANE, LANE), LANE)
            return pltpu.make_async_copy(staging, dproj_ref.at[:, cols], sems.at[k])

        df_copy = column_copy(0, df_out, CB_F + 8 * d + h)
        dq_copy = column_copy(1, dq_out, CB_HG_Q + h)
        dv_copy = column_copy(2, dv_out, CB_HG_I + h)
        last = jnp.logical_and(h == HG_HEADS - 1, d == 1)

        def intra(s, carry):
            blocks = [s + w * GLA_TRIPS for w in range(GLA_WAYS)]
            loaded = [(qt_ref[r, :], do_ref[r, :]) for r in map(_gla_block_rows, blocks)]
            results = [[_dot(doc[_gla_chunk_rows(j), :], qt[_gla_chunk_rows(j), :], _TN) for j in range(GLA_CPB)]
                       for qt, doc in loaded]
            for b, cs in zip(blocks, results):
                for j in range(GLA_CPB):
                    cs_scr[b * GLA_CPB + j] = cs[j]
            return carry

        lax.fori_loop(0, GLA_TRIPS, intra, 0)

        def scan(i, dst):
            c = jnp.where(d == 0, nck - 1 - i, i)
            dst_scr[c] = dst
            return dst * dec_scr[c][0:1, :] + cs_scr[c]

        lax.fori_loop(0, nck, scan, jnp.zeros((HEAD_DIM, HEAD_DIM), F32), unroll=4)

        @pl.when(d == 0)
        def _():
            dq_acc[...] = jnp.zeros_like(dq_acc)
            dv_acc[...] = jnp.zeros_like(dv_acc)

        def block_grads(qc, fc, v, doc, states_in, dstates, decays):
            ck = _gla_block(qc, fc, lbv, masks)
            qt, kt, kh, a = ck["qt"], ck["kt"], ck["kh"], ck["a"]
            da = jnp.where(mask, _dot(doc, v, _NT), 0.0)
            dqt_i = _dot(da, kt, _NN)
            dkt = _dot(da, qt, _TN)
            dv_i = _dot(a, doc, _TN)
            dqt_p, dv_p, dkh_p, dbt_p = [], [], [], []
            for j in range(GLA_CPB):
                cr = _gla_chunk_rows(j)
                st_in, dst = states_in[j], dstates[j]
                dqt_p.append(dqt_i[cr, :] + _dot(doc[cr, :], st_in, _NN))
                dv_p.append(dv_i[cr, :] + _dot(kh[cr, :], dst, _NT))
                dkh_j = _dot(v[cr, :], dst, _NN)
                dkh_p.append(dkh_j)
                dbt_j = (decays[j][0:1, :] * jnp.sum(dst * st_in, axis=0, keepdims=True)
                         + jnp.sum(dkh_j * kh[cr, :], axis=0, keepdims=True))
                dbt_p.append(jnp.broadcast_to(dbt_j, (HG_CHUNK, HEAD_DIM)))
            dqt = jnp.concatenate(dqt_p, axis=0)
            dv = jnp.concatenate(dv_p, axis=0)
            dkh = jnp.concatenate(dkh_p, axis=0)
            dbt = jnp.concatenate(dbt_p, axis=0)
            db = dqt * qt - dkt * kt - dkh * kh
            dq = dqt * ck["eb"]
            dk = dkt * jnp.exp(-ck["b"]) + dkh * jnp.exp(ck["bt"] - ck["b"])
            dlogf = _dot_split(maskb, db, _TN) + dbt
            dforget = dlogf / ck["forget"] - dk
            sf, sq = ck["sf"], ck["sq"]
            df = (dforget * (1.0 - lbv) * sf * (1.0 - sf)).astype(BF16)
            dqc = dq * HG_SCALE * (sq + qc * sq * (1.0 - sq))
            return df, dqc, dv, jnp.sum(dforget * (1.0 - sf), axis=0, keepdims=True)

        def grads(s, dlb):
            blocks = [s + w * GLA_GRAD_TRIPS for w in range(GLA_GRAD_WAYS)]
            rows = [_gla_block_rows(b) for b in blocks]
            loaded = []
            for b, r in zip(blocks, rows):
                chunks = [b * GLA_CPB + j for j in range(GLA_CPB)]
                loaded.append((q_ref[r, :], f_ref[r, :], v_ref[r, :], do_ref[r, :], [st_ref[c] for c in chunks],
                               [dst_scr[c] for c in chunks], [dec_scr[c] for c in chunks], dq_acc[r, :], dv_acc[r, :]))
            results = [block_grads(*t[:7]) + (t[7], t[8]) for t in loaded]
            for r, (df, dqc, dv, dlb_part, dq_prev, dv_prev) in zip(rows, results):
                df_out[r, :] = df
                dq_acc[r, :] = dq_prev + dqc
                dv_acc[r, :] = dv_prev + dv
                dlb = dlb + dlb_part
            return dlb

        @pl.when(jnp.logical_or(h > 0, d > 0))
        def _():
            df_copy.wait()

        dlb_ref[...] = lax.fori_loop(0, GLA_GRAD_TRIPS, grads, jnp.zeros((1, HEAD_DIM), F32))
        df_copy.start()

        @pl.when(d == 1)
        def _():
            @pl.when(h > 0)
            def _():
                dq_copy.wait()
                dv_copy.wait()

            dq_out[...] = dq_acc[...].astype(BF16)
            dv_out[...] = dv_acc[...].astype(BF16)
            dq_copy.start()
            dv_copy.start()

        @pl.when(last)
        def _():
            df_copy.wait()
            dq_copy.wait()
            dv_copy.wait()

    blk = SEQ * HEAD_DIM * 4
    state_bytes = nck * HEAD_DIM * HEAD_DIM * 4
    dproj, dlb = pl.pallas_call(
        body, name="gla_bwd", grid=(HG_HEADS, 2),
        in_specs=[_head_spec(CB_HG_Q),
                  pl.BlockSpec((SEQ, HEAD_DIM), lambda h, d: (0, CB_F + 8 * d + h)),
                  _head_spec(CB_HG_I),
                  pl.BlockSpec((None, None, 1, HEAD_DIM), lambda h, d: (d, h, 0, 0)),
                  _head_spec(0),
                  pl.BlockSpec((None, None, nck, HEAD_DIM, HEAD_DIM), lambda h, d: (h, d, 0, 0, 0)),
                  pl.BlockSpec((None, SEQ, HEAD_DIM), lambda h, d: (d, 0, h)),
                  pl.BlockSpec((None, None, nck, 8, HEAD_DIM), lambda h, d: (h, d, 0, 0, 0)),
                  _any_spec()],
        out_specs=[_any_spec(), pl.BlockSpec((None, None, 1, HEAD_DIM), lambda h, d: (d, h, 0, 0))],
        out_shape=[jax.ShapeDtypeStruct((SEQ, IN_COLS), BF16), jax.ShapeDtypeStruct((2, HG_HEADS, 1, HEAD_DIM), F32)],
        scratch_shapes=[pltpu.VMEM((SEQ, HEAD_DIM), F32)] * 2
        + [pltpu.VMEM((nck, HEAD_DIM, HEAD_DIM), F32)] * 2
        + [pltpu.VMEM((SEQ, HEAD_DIM), BF16)] * 3 + [pltpu.SemaphoreType.DMA((3,))],
        input_output_aliases={8: 0},
        compiler_params=_params(("arbitrary", "arbitrary"), 8 * blk + 4 * state_bytes + 3 * blk + 3 * blk),
    )(proj, proj, proj, lb, do, states, qt_all, dec_all, dproj)
    return dproj, dlb, dgain


def _gla_bwd_norm(proj, gain, o_pre, do_hg, dproj, tr):
    w = D_MODEL

    def body(g_ref, gain_ref, opre_ref, dohg_ref, dproj_in, do_ref, dg_ref, dgain_ref):
        del dproj_in
        gainv = gain_ref[...]
        part = jnp.zeros((1, HEAD_DIM), F32)
        for h in range(HG_HEADS):
            hs = slice(h * HEAD_DIM, (h + 1) * HEAD_DIM)
            o = opre_ref[:, hs]
            r = lax.rsqrt(jnp.mean(o * o, axis=-1, keepdims=True) + RMS_EPS)
            ohat = o * r
            g = g_ref[:, hs]
            sg = _sigmoid(g)
            silu = g * sg
            dout = dohg_ref[:, hs]
            dg_ref[:, hs] = (dout * ohat * gainv * (sg + silu * (1.0 - sg))).astype(BF16)
            dy = dout * silu
            part = part + jnp.sum(dy * ohat, axis=0, keepdims=True)
            dn = dy * gainv
            do_ref[:, hs] = r * (dn - ohat * jnp.mean(dn * ohat, axis=-1, keepdims=True))

        @pl.when(pl.program_id(0) == 0)
        def _():
            dgain_ref[...] = part

        @pl.when(pl.program_id(0) > 0)
        def _():
            dgain_ref[...] += part

    return pl.pallas_call(
        body, name="gla_bwd_norm", grid=(SEQ // tr,),
        in_specs=[_row_spec(tr, w, CB_HG_G * LANE // w), _bcast_spec(HEAD_DIM), _row_spec(tr, w), _row_spec(tr, w),
                  _any_spec()],
        out_specs=[_row_spec(tr, w), _row_spec(tr, w, CB_HG_G * LANE // w), _bcast_spec(HEAD_DIM)],
        out_shape=[jax.ShapeDtypeStruct((SEQ, w), F32), jax.ShapeDtypeStruct((SEQ, IN_COLS), BF16),
                   jax.ShapeDtypeStruct((1, HEAD_DIM), F32)],
        input_output_aliases={4: 1},
        compiler_params=_params(("arbitrary",), 16 * tr * w * 4),
    )(proj, gain, o_pre, do_hg, dproj)


def _da_residue_rows(r, n0, size, d):
    if d == 1:
        return pl.ds(pl.multiple_of(n0, 8), size)
    return pl.ds(r + n0 * d, size, stride=d)


def _da_residue_ways(d, nqb):
    return min(d, 4) if nqb <= 2 else 1


def _da_rmsnorm(x, gain):
    r = lax.rsqrt(jnp.mean(x * x, axis=-1, keepdims=True) + RMS_EPS)
    return x * r, r, x * r * gain


def _da_scores(qn_scr, kn_scr, slope, i, ld):
    w0 = jnp.clip(i * DA_QB - DA_RADIUS, 0, ld - DA_WIN)
    w0 = pl.multiple_of(w0, DA_RADIUS)
    qrows = pl.ds(pl.multiple_of(i * DA_QB, DA_QB), DA_QB)
    win = pl.ds(w0, DA_WIN)
    qb = qn_scr[qrows, :]
    kw = kn_scr[win, :]
    s = _dot(qb, kw, _NT) * DA_SCALE
    qpos = i * DA_QB + lax.broadcasted_iota(jnp.int32, (DA_QB, DA_WIN), 0)
    kpos = w0 + lax.broadcasted_iota(jnp.int32, (DA_QB, DA_WIN), 1)
    arel = jnp.abs(kpos - qpos)
    s = s - slope * arel.astype(F32)
    s = jnp.where(arel <= DA_RADIUS, s, NEG_INF)
    return s, qb, kw, qrows, win


def _da_slopes(group):
    d = DA_DILATIONS[group]
    sl = _alibi_slopes(DA_HEADS)[4 * group:4 * group + 4] * d
    return jnp.asarray(np.broadcast_to(sl[:, None, None], (4, 1, LANE)).copy())


def _da_fwd(proj, gq, gk, group):
    d = DA_DILATIONS[group]
    ld = SEQ // d
    nqb = ld // DA_QB

    ways = min(DA_FWD_WAYS, nqb)
    rways = _da_residue_ways(d, nqb)

    def body(q_ref, k_ref, v_ref, gq_ref, gk_ref, sl_ref, o_ref, lse_ref, qn_scr, kn_scr, v_scr):
        slope = sl_ref[:, 0:1]

        def residues(t, carry):
            rs = [t * rways + u for u in range(rways)]
            for u, r in enumerate(rs):
                sel = _da_residue_rows(r, 0, ld, d)
                qn_scr[u] = _da_rmsnorm(q_ref[sel, :], gq_ref[...])[2].astype(BF16)
                kn_scr[u] = _da_rmsnorm(k_ref[sel, :], gk_ref[...])[2].astype(BF16)
                v_scr[u] = v_ref[sel, :].astype(BF16)

            def block(u, i):
                s, _, _, _, win = _da_scores(qn_scr.at[u], kn_scr.at[u], slope, i, ld)
                m = jnp.max(s, axis=-1, keepdims=True)
                p = jnp.exp(s - m)
                l = jnp.sum(p, axis=-1, keepdims=True)
                return _dot(p, v_scr[u, win, :], _NN) / l, jnp.broadcast_to(m + jnp.log(l), (DA_QB, HEAD_DIM))

            def step(i, c2):
                todo = [(u, r, i + w * (nqb // ways)) for u, r in enumerate(rs) for w in range(ways)]
                for (u, r, b), (o, lse) in zip(todo, [block(u, b) for u, _, b in todo]):
                    out = _da_residue_rows(r, b * DA_QB, DA_QB, d)
                    o_ref[out, :] = o
                    lse_ref[out, :] = lse
                return c2

            return lax.fori_loop(0, nqb // ways, step, carry)

        lax.fori_loop(0, d // rways, residues, 0)

    seq_spec = lambda base: pl.BlockSpec((SEQ, HEAD_DIM), lambda h: (0, base + 4 * group + h))
    out_spec = pl.BlockSpec((SEQ, HEAD_DIM), lambda h: (0, h))
    gain_spec = pl.BlockSpec((1, HEAD_DIM), lambda h: (0, 0))
    blk = SEQ * HEAD_DIM * 4
    return pl.pallas_call(
        body, name=f"da_fwd_g{group}", grid=(DA_HEADS_PER_GROUP,),
        in_specs=[seq_spec(CB_DA_Q), seq_spec(CB_DA_K), seq_spec(CB_DA_V), gain_spec, gain_spec,
                  pl.BlockSpec((None, 1, LANE), lambda h: (h, 0, 0))],
        out_specs=[out_spec, out_spec],
        out_shape=[jax.ShapeDtypeStruct((SEQ, DA_WIDTH), F32)] * 2,
        scratch_shapes=[pltpu.VMEM((rways, ld, HEAD_DIM), BF16)] * 3,
        compiler_params=_params(("parallel",), 10 * blk + 2 * blk),
    )(proj, proj, proj, gq, gk, _da_slopes(group))


def _da_merge(os, lses, tr):
    w = DA_WIDTH

    def body(o0, o1, o2, l0, l1, l2, ob_ref, of_ref, lse_ref):
        la, lb_, lc = l0[...], l1[...], l2[...]
        m = jnp.maximum(jnp.maximum(la, lb_), lc)
        ea, eb, ec = jnp.exp(la - m), jnp.exp(lb_ - m), jnp.exp(lc - m)
        tot = ea + eb + ec
        o = (ea * o0[...] + eb * o1[...] + ec * o2[...]) / tot
        of_ref[...] = o
        ob_ref[...] = o.astype(BF16)
        lse_ref[...] = m + jnp.log(tot)

    return pl.pallas_call(
        body, name="da_merge", grid=(SEQ // tr,), in_specs=[_row_spec(tr, w)] * 6,
        out_specs=[_row_spec(tr, w)] * 3,
        out_shape=[jax.ShapeDtypeStruct((SEQ, w), BF16), jax.ShapeDtypeStruct((SEQ, w), F32),
                   jax.ShapeDtypeStruct((SEQ, w), F32)],
        compiler_params=_params(("parallel",), 24 * tr * w * 4),
    )(*os, *lses)


def _da_rowdot(do, o, tr):
    w = DA_WIDTH

    def body(do_ref, o_ref, out_ref):
        prod = do_ref[...] * o_ref[...]
        for h in range(w // HEAD_DIM):
            sl = slice(h * HEAD_DIM, (h + 1) * HEAD_DIM)
            out_ref[:, sl] = jnp.broadcast_to(jnp.sum(prod[:, sl], axis=-1, keepdims=True), (tr, HEAD_DIM))

    return pl.pallas_call(
        body, name="da_rowdot", grid=(SEQ // tr,), in_specs=[_row_spec(tr, w)] * 2,
        out_specs=_row_spec(tr, w), out_shape=jax.ShapeDtypeStruct((SEQ, w), F32),
        compiler_params=_params(("parallel",), 10 * tr * w * 4),
    )(do, o)


def _da_bwd(proj, gq, gk, do, lse, dd, dproj, group):
    d = DA_DILATIONS[group]
    ld = SEQ // d
    nqb = ld // DA_QB
    ways = min(DA_WAYS, nqb)
    rways = _da_residue_ways(d, nqb)

    def body(q_ref, k_ref, v_ref, gq_ref, gk_ref, sl_ref, do_ref, lse_ref, dd_ref, dproj_in,
             dproj_ref, dgq_ref, dgk_ref,
             qn_scr, kn_scr, v_scr, dqn_scr, dkn_scr, dvr_scr, dq_scr, dk_scr, dv_scr, dq_out, dk_out, dv_out, sems):
        del dproj_in
        head = pl.program_id(0)
        gqv, gkv = gq_ref[...], gk_ref[...]
        slope = sl_ref[:, 0:1]

        copies = []
        for k, (staging, base) in enumerate(((dq_out, CB_DA_Q), (dk_out, CB_DA_K), (dv_out, CB_DA_V))):
            cols = pl.ds(pl.multiple_of((base + 4 * group + head) * LANE, LANE), LANE)
            copies.append(pltpu.make_async_copy(staging, dproj_ref.at[:, cols], sems.at[k]))

        def residues(t, carry):
            rs = [t * rways + u for u in range(rways)]
            sels = [_da_residue_rows(r, 0, ld, d) for r in rs]
            for u, sel in enumerate(sels):
                qn_scr[u] = _da_rmsnorm(q_ref[sel, :], gqv)[2].astype(BF16)
                kn_scr[u] = _da_rmsnorm(k_ref[sel, :], gkv)[2].astype(BF16)
                v_scr[u] = v_ref[sel, :].astype(BF16)
            dkn_scr[...] = jnp.zeros_like(dkn_scr)
            dvr_scr[...] = jnp.zeros_like(dvr_scr)

            def block(u, r, i):
                s, qb, kw, qrows, win = _da_scores(qn_scr.at[u], kn_scr.at[u], slope, i, ld)
                src = _da_residue_rows(r, i * DA_QB, DA_QB, d)
                p = jnp.exp(s - lse_ref[src, :][:, 0:1])
                dob = do_ref[src, :]
                dp = _dot(dob, v_scr[u, win, :], _NT)
                ds = p * (dp - dd_ref[src, :][:, 0:1]) * DA_SCALE
                return u, qrows, win, _dot(p, dob, _TN), _dot(ds, kw, _NN), _dot(ds, qb, _TN)

            def step(i, c2):
                todo = [(u, r, i + w * (nqb // ways)) for u, r in enumerate(rs) for w in range(ways)]
                for u, qrows, win, dv, dqn, dkn in [block(*t) for t in todo]:
                    dvr_scr[u, win, :] += dv
                    dqn_scr[u, qrows, :] = dqn
                    dkn_scr[u, win, :] += dkn
                return c2

            lax.fori_loop(0, nqb // ways, step, 0)

            pq, pk = carry
            for u, sel in enumerate(sels):
                parts = []
                for x_ref, gv, dn_scr, dx_scr in ((q_ref, gqv, dqn_scr, dq_scr), (k_ref, gkv, dkn_scr, dk_scr)):
                    hat, rstd, _ = _da_rmsnorm(x_ref[sel, :], gv)
                    dn = dn_scr[u]
                    dyg = dn * gv
                    dx_scr[sel, :] = rstd * (dyg - hat * jnp.mean(dyg * hat, axis=-1, keepdims=True))
                    parts.append(jnp.sum(dn * hat, axis=0, keepdims=True))
                dv_scr[sel, :] = dvr_scr[u]
                pq, pk = pq + parts[0], pk + parts[1]
            return pq, pk

        zero = jnp.zeros((1, HEAD_DIM), F32)
        pq, pk = lax.fori_loop(0, d // rways, residues, (zero, zero))

        @pl.when(pl.program_id(0) == 0)
        def _():
            dgq_ref[...] = pq
            dgk_ref[...] = pk

        @pl.when(pl.program_id(0) > 0)
        def _():
            dgq_ref[...] += pq
            dgk_ref[...] += pk

        @pl.when(head > 0)
        def _():
            for cp in copies:
                cp.wait()

        dq_out[...] = dq_scr[...].astype(BF16)
        dk_out[...] = dk_scr[...].astype(BF16)
        dv_out[...] = dv_scr[...].astype(BF16)
        for cp in copies:
            cp.start()

        @pl.when(head == DA_HEADS_PER_GROUP - 1)
        def _():
            for cp in copies:
                cp.wait()

    seq_spec = lambda base: pl.BlockSpec((SEQ, HEAD_DIM), lambda h: (0, base + 4 * group + h))
    out_spec = pl.BlockSpec((SEQ, HEAD_DIM), lambda h: (0, h))
    gain_spec = pl.BlockSpec((1, HEAD_DIM), lambda h: (0, 0))
    gshape = jax.ShapeDtypeStruct((1, HEAD_DIM), F32)
    blk = SEQ * HEAD_DIM * 4
    return pl.pallas_call(
        body, name=f"da_bwd_g{group}", grid=(DA_HEADS_PER_GROUP,),
        in_specs=[seq_spec(CB_DA_Q), seq_spec(CB_DA_K), seq_spec(CB_DA_V), gain_spec, gain_spec,
                  pl.BlockSpec((None, 1, LANE), lambda h: (h, 0, 0)), out_spec, out_spec, out_spec, _any_spec()],
        out_specs=[_any_spec(), gain_spec, gain_spec],
        out_shape=[jax.ShapeDtypeStruct((SEQ, IN_COLS), BF16), gshape, gshape],
        scratch_shapes=[pltpu.VMEM((rways, ld, HEAD_DIM), BF16)] * 3 + [pltpu.VMEM((rways, ld, HEAD_DIM), F32)] * 3
        + [pltpu.VMEM((SEQ, HEAD_DIM), F32)] * 3 + [pltpu.VMEM((SEQ, HEAD_DIM), BF16)] * 3
        + [pltpu.SemaphoreType.DMA((3,))],
        input_output_aliases={9: 0},
        compiler_params=_params(("arbitrary",), 12 * blk + 3 * blk + 3 * blk + 3 * blk + 2 * blk),
    )(proj, proj, proj, gq, gk, _da_slopes(group), do, lse, dd, dproj)


MEM_WAYS = 4


def _mem_probs(q, kn, gq):
    qhat, rq, qn = _da_rmsnorm(q, gq)
    s = _dot(qn, kn, _NT) * MEM_SCALE
    m = jnp.max(s, axis=-1, keepdims=True)
    e = jnp.exp(s - m)
    p = e / jnp.sum(e, axis=-1, keepdims=True)
    return p, qhat, rq, qn


def _mem_pieces(tq):
    rows = tq // MEM_WAYS
    return [slice(w * rows, (w + 1) * rows) for w in range(MEM_WAYS)]


def _mem_fwd(proj, kv, gq, gk, tq):
    def body(q_ref, k_ref, v_ref, gq_ref, gk_ref, o_ref):
        kn = _da_rmsnorm(k_ref[...], gk_ref[...])[2]
        v, gqv = v_ref[...], gq_ref[...]
        pieces = _mem_pieces(tq)
        outs = [_dot(_mem_probs(q_ref[rs, :], kn, gqv)[0], v, _NN) for rs in pieces]
        for rs, o in zip(pieces, outs):
            o_ref[rs, :] = o.astype(BF16)

    gain_spec = pl.BlockSpec((1, HEAD_DIM), lambda h, i: (0, 0))
    return pl.pallas_call(
        body, name="mem_fwd", grid=(MEM_HEADS, SEQ // tq),
        in_specs=[pl.BlockSpec((tq, HEAD_DIM), lambda h, i: (i, CB_MEM_Q + h)),
                  pl.BlockSpec((N_MEM, HEAD_DIM), lambda h, i: (0, h)),
                  pl.BlockSpec((N_MEM, HEAD_DIM), lambda h, i: (0, MEM_HEADS + h)), gain_spec, gain_spec],
        out_specs=pl.BlockSpec((tq, HEAD_DIM), lambda h, i: (i, h)),
        out_shape=jax.ShapeDtypeStruct((SEQ, MEM_WIDTH), BF16),
        compiler_params=_params(("parallel", "parallel"), 16 * tq * N_MEM * 4),
    )(proj, kv, kv, gq, gk)


def _mem_bwd(proj, kv, gq, gk, do, dproj, tq):
    nq = SEQ // tq

    def body(q_ref, k_ref, v_ref, gq_ref, gk_ref, do_ref, dproj_in, dq_ref, dk_ref, dv_ref, dgq_ref, dgk_ref, dkn_scr):
        del dproj_in
        h, i = pl.program_id(0), pl.program_id(1)
        gqv, gkv = gq_ref[...], gk_ref[...]
        khat, rk, kn = _da_rmsnorm(k_ref[...], gkv)
        v = v_ref[...]

        def piece(rs):
            p, qhat, rq, qn = _mem_probs(q_ref[rs, :], kn, gqv)
            dob = do_ref[rs, :]
            dp = _dot(dob, v, _NT)
            ds = p * (dp - jnp.sum(p * dp, axis=-1, keepdims=True)) * MEM_SCALE
            dqn = _dot(ds, kn, _NN)
            dyg = dqn * gqv
            dq = (rq * (dyg - qhat * jnp.mean(dyg * qhat, axis=-1, keepdims=True))).astype(BF16)
            return dq, _dot(p, dob, _TN), _dot(ds, qn, _TN), jnp.sum(dqn * qhat, axis=0, keepdims=True)

        pieces = _mem_pieces(tq)
        results = [piece(rs) for rs in pieces]
        for rs, res in zip(pieces, results):
            dq_ref[rs, :] = res[0]
        dvp = sum((res[1] for res in results[1:]), results[0][1])
        dknp = sum((res[2] for res in results[1:]), results[0][2])
        dgq_part = sum((res[3] for res in results[1:]), results[0][3])
        first = jnp.logical_and(h == 0, i == 0)

        @pl.when(first)
        def _():
            dgq_ref[...] = dgq_part

        @pl.when(jnp.logical_not(first))
        def _():
            dgq_ref[...] += dgq_part

        @pl.when(i == 0)
        def _():
            dv_ref[...] = dvp
            dkn_scr[...] = dknp

        @pl.when(i > 0)
        def _():
            dv_ref[...] += dvp
            dkn_scr[...] += dknp

        @pl.when(i == nq - 1)
        def _():
            dkn = dkn_scr[...]
            dkg = dkn * gkv
            dk_ref[...] = rk * (dkg - khat * jnp.mean(dkg * khat, axis=-1, keepdims=True))
            dgk_part = jnp.sum(dkn * khat, axis=0, keepdims=True)

            @pl.when(h == 0)
            def _():
                dgk_ref[...] = dgk_part

            @pl.when(h > 0)
            def _():
                dgk_ref[...] += dgk_part

    gain_spec = pl.BlockSpec((1, HEAD_DIM), lambda h, i: (0, 0))
    kvout = pl.BlockSpec((N_MEM, HEAD_DIM), lambda h, i: (0, h))
    return pl.pallas_call(
        body, name="mem_bwd", grid=(MEM_HEADS, nq),
        in_specs=[pl.BlockSpec((tq, HEAD_DIM), lambda h, i: (i, CB_MEM_Q + h)),
                  pl.BlockSpec((N_MEM, HEAD_DIM), lambda h, i: (0, h)),
                  pl.BlockSpec((N_MEM, HEAD_DIM), lambda h, i: (0, MEM_HEADS + h)), gain_spec, gain_spec,
                  pl.BlockSpec((tq, HEAD_DIM), lambda h, i: (i, h)), _any_spec()],
        out_specs=[pl.BlockSpec((tq, HEAD_DIM), lambda h, i: (i, CB_MEM_Q + h)), kvout, kvout, gain_spec, gain_spec],
        out_shape=[jax.ShapeDtypeStruct((SEQ, IN_COLS), BF16), jax.ShapeDtypeStruct((N_MEM, MEM_WIDTH), F32),
                   jax.ShapeDtypeStruct((N_MEM, MEM_WIDTH), F32), jax.ShapeDtypeStruct((1, HEAD_DIM), F32),
                   jax.ShapeDtypeStruct((1, HEAD_DIM), F32)],
        scratch_shapes=[pltpu.VMEM((N_MEM, HEAD_DIM), F32)], input_output_aliases={6: 0},
        compiler_params=_params(("arbitrary", "arbitrary"), 24 * tq * N_MEM * 4),
    )(proj, kv, kv, gq, gk, do, dproj)


def _mesh_position():
    return lax.axis_index("x"), lax.axis_index("y"), lax.axis_index("c")


def _any_spec():
    return pl.BlockSpec(memory_space=pl.ANY)


def _all_gather(shards):
    n = len(shards)

    def body(*refs):
        ins, outs = refs[:n], refs[n:2 * n]
        send_sems, recv_sems, local_sems = refs[2 * n:]
        x, y, c = _mesh_position()
        me, sibling = (x, y, c), (x, y, 1 - c)
        first = (jnp.where(c == 0, 1 - x, x), jnp.where(c == 0, y, 1 - y), c)
        second = (jnp.where(c == 0, x, 1 - x), jnp.where(c == 0, 1 - y, y), c)
        diagonal = (1 - x, 1 - y, c)

        def copy(w, k, block, to, src=None):
            px, py, pc = block
            rows = outs[w].at[4 * px + 2 * py + pc]
            return pltpu.make_async_remote_copy(
                src_ref=rows if src is None else src, dst_ref=rows,
                send_sem=send_sems.at[7 * w + k], recv_sem=recv_sems.at[7 * w + k],
                device_id=to, device_id_type=MESH)

        started = []
        for w in range(n):
            mine = pltpu.make_async_copy(ins[w], outs[w].at[4 * x + 2 * y + c], local_sems.at[w])
            mine.start()
            started.append(mine)
        sends = []
        for w in range(n):
            own = [copy(w, 0, me, sibling, src=ins[w]), copy(w, 1, me, first, src=ins[w]),
                   copy(w, 2, me, second, src=ins[w])]
            for cp in own:
                cp.start()
            sends += own
        for w in range(n):
            copy(w, 1, first, me).wait_recv()
            follow = [copy(w, 3, first, second), copy(w, 4, first, sibling)]
            for cp in follow:
                cp.start()
            copy(w, 2, second, me).wait_recv()
            follow.append(copy(w, 5, second, sibling))
            follow[-1].start()
            sends += follow
        for w in range(n):
            copy(w, 3, diagonal, me).wait_recv()
            passed = copy(w, 6, diagonal, sibling)
            passed.start()
            sends.append(passed)
        for w in range(n):
            for k in (0, 4, 5, 6):
                copy(w, k, sibling, me).wait_recv()
        for cp in sends:
            cp.wait_send()
        for mine in started:
            mine.wait()

    return pl.pallas_call(
        body, name="weights_all_gather",
        in_specs=[_any_spec()] * n, out_specs=[_any_spec()] * n,
        out_shape=[jax.ShapeDtypeStruct((N_DEV,) + s.shape, s.dtype) for s in shards],
        scratch_shapes=[pltpu.SemaphoreType.DMA((7 * n,)), pltpu.SemaphoreType.DMA((7 * n,)),
                        pltpu.SemaphoreType.DMA((n,))],
    )(*shards)


def _chip_of(j, x, y):
    return (1 - x if j & 1 else x, 1 - y if j & 2 else y)


_HBM_SPEC = pl.BlockSpec(memory_space=pltpu.HBM)
_SEM_SPEC = pl.BlockSpec(memory_space=pltpu.SEMAPHORE)
_DATAFLOW_EFFECT = pltpu.SideEffectType.DATAFLOW_SIDE_EFFECTING
TOKEN_SHAPE = (8, D_MODEL)


def _copies_start(name, arrays, n_copies, plan):
    n = len(arrays)

    def body(*refs):
        send_sems, recv_sems, token = refs[n], refs[n + 1], refs[2 * n + 2]
        copies = plan(refs[:n])
        assert len(copies) == n_copies
        for k, (src, dst, dev) in enumerate(copies):
            pltpu.make_async_remote_copy(src_ref=src, dst_ref=dst, send_sem=send_sems.at[k], recv_sem=recv_sems.at[k],
                                         device_id=dev, device_id_type=MESH).start()
        token[...] = jnp.zeros_like(token)

    outs = pl.pallas_call(
        body, name=name,
        out_shape=(pltpu.SemaphoreType.DMA((n_copies,)), pltpu.SemaphoreType.DMA((n_copies,)),
                   *[pltpu.HBM(a.shape, a.dtype) for a in arrays], jax.ShapeDtypeStruct(TOKEN_SHAPE, F32)),
        in_specs=[_HBM_SPEC] * n,
        out_specs=(_SEM_SPEC, _SEM_SPEC, *[_HBM_SPEC] * n, pl.BlockSpec(memory_space=pltpu.VMEM)),
        input_output_aliases={i: i + 2 for i in range(n)},
        compiler_params=pltpu.CompilerParams(has_side_effects=_DATAFLOW_EFFECT),
    )(*[pltpu.with_memory_space_constraint(a, pltpu.HBM) for a in arrays])
    return outs[0], outs[1], list(outs[2:2 + n]), outs[2 + n]


def _copies_wait(name, send_sems, recv_sems, arrays, n_copies, plan, after):
    n = len(arrays)
    after = list(after) if isinstance(after, (list, tuple)) else [after]

    def body(*refs):
        send_ref, recv_ref = refs[n], refs[n + 1]
        copies = plan(refs[:n])
        assert len(copies) == n_copies
        for k, (src, dst, dev) in enumerate(copies):
            cp = pltpu.make_async_remote_copy(src_ref=src, dst_ref=dst, send_sem=send_ref.at[k], recv_sem=recv_ref.at[k],
                                              device_id=dev, device_id_type=MESH)
            cp.wait_send()
            cp.wait_recv()

    outs = pl.pallas_call(
        body, name=name, out_shape=tuple(pltpu.HBM(a.shape, a.dtype) for a in arrays),
        in_specs=[_HBM_SPEC] * n + [_SEM_SPEC, _SEM_SPEC] + [pl.BlockSpec(memory_space=pl.ANY)] * len(after),
        out_specs=tuple([_HBM_SPEC] * n), input_output_aliases={i: i for i in range(n)},
        compiler_params=pltpu.CompilerParams(has_side_effects=_DATAFLOW_EFFECT),
    )(*arrays, send_sems, recv_sems, *after)
    return list(outs)


def _after(small, token):
    return small if token is None else small + token[0:1, :small.shape[-1]]


def _gather_plan_out(n):
    def plan(refs):
        x, y, c = _mesh_position()
        me = 4 * x + 2 * y + c
        copies = []
        for w in range(n):
            land = refs[n + w].at[me]
            copies.append((refs[w], land, (x, y, 1 - c)))
            for j in range(1, 4):
                copies.append((refs[w], land, (*_chip_of(j, x, y), c)))
        return copies
    return plan


def _gather_plan_pass(n):
    def plan(refs):
        x, y, c = _mesh_position()
        copies = []
        for w in range(n):
            for j in range(1, 4):
                px, py = _chip_of(j, x, y)
                rows = refs[w].at[4 * px + 2 * py + c]
                copies.append((rows, rows, (x, y, 1 - c)))
        return copies
    return plan


def _reduce_plan_sibling(n):
    def plan(refs):
        x, y, c = _mesh_position()
        copies = []
        for w in range(n):
            for j in range(4):
                px, py = _chip_of(j, x, y)
                copies.append((refs[w].at[4 * px + 2 * py + (1 - c)], refs[n + w].at[j], (x, y, 1 - c)))
        return copies
    return plan


def _reduce_plan_chips(n):
    def plan(refs):
        x, y, c = _mesh_position()
        copies = []
        for w in range(n):
            for j in range(1, 4):
                copies.append((refs[w].at[j - 1], refs[n + w].at[j - 1], (*_chip_of(j, x, y), c)))
        return copies
    return plan


def _chip_partials(grad, recv, name, tr):
    _, rows, width = grad.shape

    def body(g_ref, r_ref, own_ref, other_ref):
        x, y, c = _mesh_position()
        for j in range(4):
            px, py = _chip_of(j, x, y)
            total = g_ref[4 * px + 2 * py + c].astype(F32) + r_ref[j].astype(F32)
            if j == 0:
                own_ref[...] = total
            else:
                other_ref[j - 1] = total.astype(BF16)

    return pl.pallas_call(
        body, name=name, grid=(rows // tr,),
        in_specs=[pl.BlockSpec((N_DEV, tr, width), lambda i: (0, i, 0)),
                  pl.BlockSpec((4, tr, width), lambda i: (0, i, 0))],
        out_specs=[pl.BlockSpec((tr, width), lambda i: (i, 0)), pl.BlockSpec((3, tr, width), lambda i: (0, i, 0))],
        out_shape=[jax.ShapeDtypeStruct((rows, width), F32), jax.ShapeDtypeStruct((3, rows, width), BF16)],
        compiler_params=_params(("parallel",), 2 * 20 * tr * width * 2 + 8 * tr * width * 4),
    )(grad, recv)


def _adamw_math(w, g, m, v):
    m = ADAM_B1 * m + (1.0 - ADAM_B1) * g
    v = ADAM_B2 * v + (1.0 - ADAM_B2) * (g * g)
    m_hat = m / (1.0 - ADAM_B1 ** ADAM_STEP)
    v_hat = v / (1.0 - ADAM_B2 ** ADAM_STEP)
    delta = -ADAM_LR * (m_hat / (jnp.sqrt(v_hat) + ADAM_EPS) + ADAM_WD * w)
    return delta, m, v


def _adamw_shard(own, recv, w, m, v, name, tr):
    rows, width = own.shape

    def body(own_ref, r_ref, w_ref, m_ref, v_ref, g_ref, d_ref, nm_ref, nv_ref):
        g = own_ref[...]
        for j in range(3):
            g = g + r_ref[j].astype(F32)
        g_ref[...] = g
        d_ref[...], nm_ref[...], nv_ref[...] = _adamw_math(w_ref[...], g, m_ref[...], v_ref[...])

    spec = pl.BlockSpec((tr, width), lambda i: (i, 0))
    shape = jax.ShapeDtypeStruct((rows, width), F32)
    return pl.pallas_call(
        body, name=name, grid=(rows // tr,),
        in_specs=[spec, pl.BlockSpec((3, tr, width), lambda i: (0, i, 0)), spec, spec, spec],
        out_specs=[spec] * 4, out_shape=[shape] * 4,
        compiler_params=_params(("parallel",), 22 * tr * width * 4),
    )(own, recv, w, m, v)


def _small_all_reduce_adamw(gpart, lbpack, wpack, mpack, vpack, after):
    def body(gp_ref, lb_ref, w_ref, m_ref, v_ref, after_ref, g_ref, d_ref, nm_ref, nv_ref, gath_ref, send_sems, recv_sems):
        del after_ref
        x, y, c = _mesh_position()
        me = 4 * x + 2 * y + c
        gath_ref[me] = gp_ref[...]
        copies = []
        for j in range(1, N_DEV):
            peer = (x ^ (j >> 2), y ^ ((j >> 1) & 1), c ^ (j & 1))
            cp = pltpu.make_async_remote_copy(
                src_ref=gp_ref, dst_ref=gath_ref.at[me], send_sem=send_sems.at[j - 1], recv_sem=recv_sems.at[j - 1],
                device_id=peer, device_id_type=MESH)
            cp.start()
            copies.append(cp)
        for cp in copies:
            cp.wait()
        tot = gath_ref[0]
        for s in range(1, N_DEV):
            tot = tot + gath_ref[s]
        lb = lb_ref[...]
        dl = tot[16:32, :] * lb * (1.0 - lb)
        g = jnp.concatenate([tot[0:16, :], dl[0:8, :], -dl[0:8, :], dl[8:16, :], -dl[8:16, :],
                             tot[32:SMALL_GRAD_ROWS, :]], axis=0)
        g_ref[...] = g
        d_ref[...], nm_ref[...], nv_ref[...] = _adamw_math(w_ref[...], g, m_ref[...], v_ref[...])

    vm = pl.BlockSpec(memory_space=pltpu.VMEM)
    shape = jax.ShapeDtypeStruct((SMALL_ROWS, LANE), F32)
    return pl.pallas_call(
        body, name="small_all_reduce_adamw", in_specs=[vm] * 5 + [_any_spec()], out_specs=[vm] * 4,
        out_shape=[shape] * 4,
        scratch_shapes=[pltpu.VMEM((N_DEV, SMALL_GRAD_ROWS, LANE), F32),
                        pltpu.SemaphoreType.DMA((N_DEV - 1,)), pltpu.SemaphoreType.DMA((N_DEV - 1,))],
    )(gpart, lbpack, wpack, mpack, vpack, after)


_SMALL_NAMES = ("norm_mix_gain", "norm_mem_gain", "lb_logits_fw", "lb_logits_bw", "norm_ffn_gain",
                "hg_norm_gain", "da_q_gain", "da_k_gain", "mem_q_gain", "mem_k_gain")
_SMALL_ROW0 = {"norm_mix_gain": 0, "norm_mem_gain": 8, "lb_logits_fw": 16, "lb_logits_bw": 32, "norm_ffn_gain": 48,
               "hg_norm_gain": 56, "da_q_gain": 64, "da_k_gain": 72, "mem_q_gain": 80, "mem_k_gain": 88}
_LOSS_ROW = 96


def _pack_rows(parts, total_rows):
    rows = []
    for p in parts:
        r = p.reshape(-1, LANE)
        rows.append(jnp.pad(r, ((0, -r.shape[0] % 8), (0, 0))))
    used = sum(r.shape[0] for r in rows)
    if total_rows > used:
        rows.append(jnp.zeros((total_rows - used, LANE), F32))
    return jnp.concatenate(rows, axis=0)


def _unpack_small(pack, like):
    out = {}
    for name in _SMALL_NAMES:
        n = like[name].size // LANE
        r0 = _SMALL_ROW0[name]
        out[name] = pack[r0:r0 + n].reshape(like[name].shape)
    return out


def kernel(x, mem, norm_mix_gain, norm_mem_gain, w_in, lb_logits_fw, lb_logits_bw, hg_norm_gain, da_q_gain, da_k_gain, w_mem_kv, mem_q_gain, mem_k_gain, w_proj_hg, w_proj_da, w_proj_mem, w_out, norm_ffn_gain, w_ffn_in, w_ffn_out, loss_target, m_norm_mix_gain, m_norm_mem_gain, m_w_in, m_lb_logits_fw, m_lb_logits_bw, m_hg_norm_gain, m_da_q_gain, m_da_k_gain, m_w_mem_kv, m_mem_q_gain, m_mem_k_gain, m_w_proj_hg, m_w_proj_da, m_w_proj_mem, m_w_out, m_norm_ffn_gain, m_w_ffn_in, m_w_ffn_out, v_norm_mix_gain, v_norm_mem_gain, v_w_in, v_lb_logits_fw, v_lb_logits_bw, v_hg_norm_gain, v_da_q_gain, v_da_k_gain, v_w_mem_kv, v_mem_q_gain, v_mem_k_gain, v_w_proj_hg, v_w_proj_da, v_w_proj_mem, v_w_out, v_norm_ffn_gain, v_w_ffn_in, v_w_ffn_out):
    small_w = dict(norm_mix_gain=norm_mix_gain, norm_mem_gain=norm_mem_gain, lb_logits_fw=lb_logits_fw,
                   lb_logits_bw=lb_logits_bw, norm_ffn_gain=norm_ffn_gain, hg_norm_gain=hg_norm_gain,
                   da_q_gain=da_q_gain, da_k_gain=da_k_gain, mem_q_gain=mem_q_gain, mem_k_gain=mem_k_gain)
    small_m = dict(norm_mix_gain=m_norm_mix_gain, norm_mem_gain=m_norm_mem_gain, lb_logits_fw=m_lb_logits_fw,
                   lb_logits_bw=m_lb_logits_bw, norm_ffn_gain=m_norm_ffn_gain, hg_norm_gain=m_hg_norm_gain,
                   da_q_gain=m_da_q_gain, da_k_gain=m_da_k_gain, mem_q_gain=m_mem_q_gain, mem_k_gain=m_mem_k_gain)
    small_v = dict(norm_mix_gain=v_norm_mix_gain, norm_mem_gain=v_norm_mem_gain, lb_logits_fw=v_lb_logits_fw,
                   lb_logits_bw=v_lb_logits_bw, norm_ffn_gain=v_norm_ffn_gain, hg_norm_gain=v_hg_norm_gain,
                   da_q_gain=v_da_q_gain, da_k_gain=v_da_k_gain, mem_q_gain=v_mem_q_gain, mem_k_gain=v_mem_k_gain)
    big_w = dict(w_in=w_in[0], w_mem_kv=w_mem_kv[0], w_proj_hg=w_proj_hg[0], w_proj_da=w_proj_da[0],
                 w_proj_mem=w_proj_mem[0], w_out=w_out[0], w_ffn_in=w_ffn_in[0], w_ffn_out=w_ffn_out[0])
    big_m = dict(w_in=m_w_in[0], w_mem_kv=m_w_mem_kv[0], w_proj_hg=m_w_proj_hg[0], w_proj_da=m_w_proj_da[0],
                 w_proj_mem=m_w_proj_mem[0], w_out=m_w_out[0], w_ffn_in=m_w_ffn_in[0], w_ffn_out=m_w_ffn_out[0])
    big_v = dict(w_in=v_w_in[0], w_mem_kv=v_w_mem_kv[0], w_proj_hg=v_w_proj_hg[0], w_proj_da=v_w_proj_da[0],
                 w_proj_mem=v_w_proj_mem[0], w_out=v_w_out[0], w_ffn_in=v_w_ffn_in[0], w_ffn_out=v_w_ffn_out[0])

    row_tile = dict(w_in=128, w_mem_kv=128, w_proj_hg=128, w_proj_da=512, w_proj_mem=512, w_out=128,
                    w_ffn_in=128, w_ffn_out=352)
    rest = _BIG_NAMES[1:]
    shards = [big_w[n].astype(BF16) for n in rest]
    state = {}
    big_out = {}

    def reduce_start(group, stacked):
        names = tuple(stacked)
        arrays = [stacked[n] for n in names] + [lax.empty((4,) + stacked[n].shape[1:], BF16) for n in names]
        plan = _reduce_plan_sibling(len(names))
        send, recv, thru, token = _copies_start(f"grads_{group}_sibling_start", arrays, 4 * len(names), plan)
        state[group] = dict(names=names, plan=plan, send=send, recv=recv, arrays=thru)
        return token

    def reduce_middle(group, after):
        st = state[group]
        names, k = st["names"], len(st["names"])
        thru = _copies_wait(f"grads_{group}_sibling_wait", st["send"], st["recv"], st["arrays"], 4 * k, st["plan"], after)
        partials = [_chip_partials(thru[i], thru[k + i], f"chip_partials_{n}", row_tile[n]) for i, n in enumerate(names)]
        arrays = [p[1] for p in partials] + [lax.empty(p[1].shape, BF16) for p in partials]
        plan = _reduce_plan_chips(k)
        send, recv, thru2, token = _copies_start(f"grads_{group}_chips_start", arrays, 3 * k, plan)
        state[group] = dict(names=names, plan=plan, send=send, recv=recv, arrays=thru2, own=[p[0] for p in partials])
        return token

    def reduce_finish(group, after):
        st = state.pop(group)
        names, k = st["names"], len(st["names"])
        thru = _copies_wait(f"grads_{group}_chips_wait", st["send"], st["recv"], st["arrays"], 3 * k, st["plan"], after)
        for i, n in enumerate(names):
            big_out[n] = _adamw_shard(st["own"][i], thru[k + i], big_w[n], big_m[n], big_v[n], "adamw_" + n, row_tile[n])

    win_st = _all_gather([big_w["w_in"].astype(BF16)])[0]
    step = _local_step_stages(x[0], mem[0], loss_target[0], small_w, win_st)
    event, payload = next(step)
    local = None
    while True:
        reply = None
        if event == "begin":
            nr = len(rest)
            me = 4 * lax.axis_index("x") + 2 * lax.axis_index("y") + lax.axis_index("c")
            arrays = shards + [lax.dynamic_update_slice(lax.empty((N_DEV,) + s.shape, BF16), s[None], (me, 0, 0))
                               for s in shards]
            plan = _gather_plan_out(nr)
            send, recv, thru, reply = _copies_start("weights_rest_out_start", arrays, 4 * nr, plan)
            state["gather"] = dict(plan=plan, send=send, recv=recv, arrays=thru)
        elif event == "after_gla_fwd":
            st = state["gather"]
            nr = len(rest)
            thru = _copies_wait("weights_rest_out_wait", st["send"], st["recv"], st["arrays"], 4 * nr, st["plan"], payload)
            plan = _gather_plan_pass(nr)
            send, recv, lands, reply = _copies_start("weights_rest_pass_start", thru[nr:], 3 * nr, plan)
            state["gather"] = dict(plan=plan, send=send, recv=recv, arrays=lands)
        elif event == "need_weights":
            st = state.pop("gather")
            nr = len(rest)
            lands = _copies_wait("weights_rest_pass_wait", st["send"], st["recv"], st["arrays"], 3 * nr, st["plan"], payload)
            reply = dict(zip(rest, lands))
        elif event == "grads_ffn":
            reply = reduce_start("ffn", payload)
        elif event == "after_gate_merge_bwd":
            reply = reduce_middle("ffn", payload)
        elif event == "grads_mix":
            reply = reduce_start("mix", payload)
        elif event == "after_da_bwd_g0":
            reply = reduce_middle("mix", payload)
            reduce_finish("ffn", payload)
        elif event == "after_gla_bwd":
            reduce_finish("mix", payload)
        elif event == "grads_in":
            reply = reduce_start("in", payload)
        elif event == "after_proj_bwd_act_top":
            reply = reduce_middle("in", payload)
        elif event == "end":
            local = payload
            reduce_finish("in", [local["grad_x"]] + [big_out[n][0] for n in rest])
            break
        event, payload = step.send(reply)
    grad_x = local["grad_x"]

    wpack = _pack_rows([small_w[n] for n in _SMALL_NAMES], SMALL_ROWS)
    mpack = _pack_rows([small_m[n] for n in _SMALL_NAMES], SMALL_ROWS)
    vpack = _pack_rows([small_v[n] for n in _SMALL_NAMES], SMALL_ROWS)
    gs, ds, ms, vs = _small_all_reduce_adamw(local["gpart"], local["lbpack"], wpack, mpack, vpack, big_out["w_in"][0])
    loss = gs[_LOSS_ROW, 0]
    small_out = [_unpack_small(t, small_w) for t in (gs, ds, ms, vs)]

    order = ("norm_mix_gain", "norm_mem_gain", "w_in", "lb_logits_fw", "lb_logits_bw", "hg_norm_gain", "da_q_gain",
             "da_k_gain", "w_mem_kv", "mem_q_gain", "mem_k_gain", "w_proj_hg", "w_proj_da", "w_proj_mem", "w_out",
             "norm_ffn_gain", "w_ffn_in", "w_ffn_out")
    outs = [loss, grad_x[None]]
    for kind in range(4):
        for n in order:
            outs.append(big_out[n][kind][None] if n in big_out else small_out[kind][n])
    return tuple(outs)


_BIG_NAMES = ("w_in", "w_mem_kv", "w_proj_hg", "w_proj_da", "w_proj_mem", "w_out", "w_ffn_in", "w_ffn_out")


def _local_step(xs, mems, target, sw, wg):
    step = _local_step_stages(xs, mems, target, sw, wg["w_in"])
    stacked = {}
    event, payload = next(step)
    while event != "end":
        if event.startswith("grads_"):
            stacked.update(payload)
        event, payload = step.send(wg if event == "need_weights" else None)
    return dict(payload, stacked=stacked)


def _local_step_stages(xs, mems, target, sw, win_st):
    norm_mix_gain, norm_mem_gain, norm_ffn_gain = sw["norm_mix_gain"], sw["norm_mem_gain"], sw["norm_ffn_gain"]
    lb_logits_fw, lb_logits_bw, hg_norm_gain = sw["lb_logits_fw"], sw["lb_logits_bw"], sw["hg_norm_gain"]
    da_q_gain, da_k_gain, mem_q_gain, mem_k_gain = sw["da_q_gain"], sw["da_k_gain"], sw["mem_q_gain"], sw["mem_k_gain"]

    token = yield "begin", None
    lb_fw = _lb_table(lb_logits_fw, "lb_table_fw")
    lb_bw = _lb_table(lb_logits_bw, "lb_table_bw")
    lb = jnp.concatenate([lb_fw, lb_bw], axis=0).reshape(2, HG_HEADS, 1, HEAD_DIM)
    h, h_t = _rmsnorm_fwd(xs, _after(norm_mix_gain, token), "norm_mix_fwd", 512, transposed=True)
    proj = _matmul(h, win_st, "nn", F32, 1024, IN_SHARD, D_MODEL, "proj_fwd", b_stacked=True, n_outer=True)
    o_hg, o_pre, states, hg_qt, hg_decay = _gla_fwd(proj, lb, hg_norm_gain)
    token = yield "after_gla_fwd", o_hg
    da = [_da_fwd(proj, _after(da_q_gain, token), da_k_gain, g) for g in range(3)]
    o_da, o_da32, lse_da = _da_merge([t[0] for t in da], [t[1] for t in da], 512)
    wg = yield "need_weights", o_da
    wkv = wg["w_mem_kv"].reshape(D_MODEL, 2 * MEM_WIDTH)
    wphg = wg["w_proj_hg"].reshape(D_MODEL, D_MODEL)
    wpda = jnp.transpose(wg["w_proj_da"], (1, 0, 2)).reshape(DA_WIDTH, D_MODEL)
    wpmem = jnp.transpose(wg["w_proj_mem"], (1, 0, 2)).reshape(MEM_WIDTH, D_MODEL)
    wout = wg["w_out"].reshape(D_MODEL, D_MODEL)
    wfin = jnp.transpose(wg["w_ffn_in"], (1, 0, 2)).reshape(D_MODEL, 2 * D_FF)
    wfout = wg["w_ffn_out"].reshape(D_FF, D_MODEL)
    mem_n = _rmsnorm_fwd(mems, norm_mem_gain, "norm_mem_fwd", N_MEM)
    kv = _matmul(mem_n, wkv, "nn", F32, N_MEM, 1024, D_MODEL, "mem_kv_fwd")
    o_mem = _mem_fwd(proj, kv, mem_q_gain, mem_k_gain, 1024)
    branch_w = (wphg, wpda, wpmem)
    merged, t_hg, t_da, t_mem = _branch_merge_fwd(proj, (o_hg, o_da, o_mem), branch_w, 512)
    x1, h2, h2_t = _residual_rmsnorm_fwd(xs, merged, wout, norm_ffn_gain, "out_norm_ffn_fwd", 512)
    ffn_a, ffn_b, act, act_t = _ffn_in_swiglu(h2, wfin, 1024, 1408)
    dy, dyb, loss_part = _ffn_out_loss_head(x1, act, wfout, target, 512)

    dab = _ffn_out_bwd_swiglu(dyb, wfout, ffn_a, ffn_b, 1024, 1408)
    g_wfout = _matmul(act_t, dyb, "nn", BF16, 1408, 1024, 2048, "ffn_out_bwd_w")
    g_wfin = _matmul(h2_t, dab, "nn", BF16, 1024, 1408, 2048, "ffn_in_bwd_w", b_parts=True)
    token = yield "grads_ffn", dict(
        w_ffn_in=jnp.transpose(g_wfin.reshape(D_MODEL, N_DEV, 2 * D_FF // N_DEV), (1, 0, 2)),
        w_ffn_out=g_wfout.reshape(N_DEV, D_FF // N_DEV, D_MODEL))
    dx1, dx1b, g_norm_ffn = _matmul_rmsnorm_bwd(dab, wfin, x1, dy, _after(norm_ffn_gain, token),
                                                "ffn_in_bwd_act_norm", 512, D_FF, a_parts=True)
    dmerged = _matmul(dx1b, wout, "nt", F32, 512, 1024, D_MODEL, "out_bwd_act")
    g_wout = _matmul(merged, dx1b, "tn", BF16, 1024, 1024, 2048, "out_bwd_w")
    dt_hg, dt_da, dt_mem, do_hg, do_da, do_mem, dproj = _gate_merge_bwd(
        proj, (t_hg, t_da, t_mem), branch_w, dmerged, _dproj_buffer(), 256)
    token = yield "after_gate_merge_bwd", dt_hg
    g_wphg = _matmul(o_hg, dt_hg, "tn", BF16, 1024, 1024, 2048, "proj_hg_bwd_w", after=token)
    g_wpda = _matmul(o_da, dt_da, "tn", BF16, DA_WIDTH, D_MODEL, 2048, "proj_da_bwd_w")
    g_wpmem = _matmul(o_mem, dt_mem, "tn", BF16, MEM_WIDTH, D_MODEL, 2048, "proj_mem_bwd_w")
    by_owner = lambda g: jnp.transpose(g.reshape(g.shape[0], N_DEV, D_MODEL // N_DEV), (1, 0, 2))
    g_wpda, g_wpmem = by_owner(g_wpda), by_owner(g_wpmem)

    dproj, dk_mem, dv_mem, g_mem_q, g_mem_k = _mem_bwd(proj, kv, mem_q_gain, mem_k_gain, do_mem, dproj, 1024)
    dkv = jnp.concatenate([dk_mem, dv_mem], axis=1).astype(BF16)
    g_wkv = _matmul(mem_n, dkv, "tn", BF16, 1024, 1024, N_MEM, "mem_kv_bwd_w")
    dmem_n = _matmul(dkv, wkv, "nt", F32, N_MEM, 1024, 1024, "mem_kv_bwd_act")
    g_norm_mem = _gain_grad(mems, dmem_n, "norm_mem_bwd")
    token = yield "grads_mix", dict(
        w_mem_kv=g_wkv.reshape(N_DEV, D_MODEL // N_DEV, 2 * MEM_WIDTH),
        w_proj_hg=g_wphg.reshape(N_DEV, D_MODEL // N_DEV, D_MODEL),
        w_proj_da=g_wpda, w_proj_mem=g_wpmem,
        w_out=g_wout.reshape(N_DEV, D_MODEL // N_DEV, D_MODEL))

    dd_da = _da_rowdot(do_da, o_da32, 512)
    dproj, g_da_q, g_da_k = _da_bwd(proj, _after(da_q_gain, token), da_k_gain, do_da, lse_da, dd_da, dproj, 0)
    token = yield "after_da_bwd_g0", g_da_q
    for g in (1, 2):
        dproj, gq_part, gk_part = _da_bwd(proj, _after(da_q_gain, token), da_k_gain, do_da, lse_da, dd_da, dproj, g)
        g_da_q, g_da_k = g_da_q + gq_part, g_da_k + gk_part

    dproj, dlb, g_hg_norm = _gla_bwd(proj, lb, hg_norm_gain, o_pre, states, hg_qt, hg_decay, do_hg, dproj)
    yield "after_gla_bwd", dlb

    g_win =_matmul(h_t, dproj, "nn", BF16, 1024, IN_SHARD, 2048, "proj_bwd_w", out_stacked=True)
    token = yield "grads_in", dict(w_in=g_win)
    grad_x, g_mix_top = _matmul_rmsnorm_bwd(dproj, win_st, xs, dx1, norm_mix_gain, "proj_bwd_act_norm_top", 1024,
                                            IN_SHARD, b_stacked=True, after=token, m_blocks=(0, 2), with_bf16=False)
    token = yield "after_proj_bwd_act_top", g_mix_top
    grad_x, g_mix_bottom = _matmul_rmsnorm_bwd(dproj, win_st, xs, dx1, norm_mix_gain, "proj_bwd_act_norm_bottom", 1024,
                                               IN_SHARD, b_stacked=True, after=token, m_blocks=(2, 2),
                                               out_into=grad_x, with_bf16=False)
    g_norm_mix = g_mix_top + g_mix_bottom

    gpart = _pack_rows([g_norm_mix, g_norm_mem, dlb[0], dlb[1], g_norm_ffn, g_hg_norm, g_da_q, g_da_k,
                        g_mem_q, g_mem_k, loss_part], SMALL_GRAD_ROWS)
    lbpack = jnp.concatenate([lb_fw.reshape(8, LANE), lb_bw.reshape(8, LANE)], axis=0)
    yield "end", dict(grad_x=grad_x, gpart=gpart, lbpack=lbpack)
```

```python
import numpy as np
import jax
import jax.numpy as jnp
from jax import lax
from jax.experimental import pallas as pl
from jax.experimental.pallas import tpu as pltpu

F32 = jnp.float32
BF16 = jnp.bfloat16
MESH = pl.DeviceIdType.MESH

SEQ = 4096
D_MODEL = 1024
N_DEV = 8
N_MEM = 256
RMS_EPS = 1e-6
NEG_INF = -1e30
LANE = 128
HEAD_DIM = 128
HG_HEADS = 8
HG_CHUNK = 64
HG_SCALE = HEAD_DIM ** -0.5
DA_DILATIONS = (1, 4, 16)
DA_RADIUS = 64
DA_HEADS_PER_GROUP = 4
DA_HEADS = 12
DA_WIDTH = 512
DA_SCALE = HEAD_DIM ** -0.5
DA_QB = 128
DA_WIN = 256
DA_WAYS = 4
DA_FWD_WAYS = 4
MEM_HEADS = 4
MEM_WIDTH = 512
MEM_SCALE = HEAD_DIM ** -0.5
D_FF = 2816
IN_COLS = 13312
IN_SHARD = IN_COLS // N_DEV
CB_HG_Q, CB_F, CB_HG_I, CB_HG_G = 0, 8, 24, 32
CB_DA_Q, CB_DA_K, CB_DA_V, CB_MEM_Q = 40, 52, 64, 76
ADAM_LR, ADAM_B1, ADAM_B2, ADAM_EPS, ADAM_WD, ADAM_STEP = 0.001, 0.9, 0.999, 1e-08, 0.01, 10
VMEM_BYTES_V7X = 64 * 1024 * 1024
SMALL_ROWS = 104
SMALL_GRAD_ROWS = 88

_NN = (((1,), (0,)), ((), ()))
_NT = (((1,), (1,)), ((), ()))
_TN = (((0,), (0,)), ((), ()))


def _dot(a, b, dims):
    return lax.dot_general(a.astype(BF16), b.astype(BF16), dims, preferred_element_type=F32)


def _sigmoid(x):
    return 0.5 * jnp.tanh(0.5 * x) + 0.5


def _params(semantics, est_bytes):
    limit = int(min(VMEM_BYTES_V7X - (6 << 20), max(56 << 20, est_bytes * 3 // 2)))
    return pltpu.CompilerParams(dimension_semantics=semantics, vmem_limit_bytes=limit)


def _nbytes(shape, dtype):
    return int(np.prod(shape)) * jnp.dtype(dtype).itemsize


def _alibi_slopes(n):
    return (2.0 ** (-8.0 * np.arange(1, n + 1) / n)).astype(np.float32)


def _matmul(a, b, mode, out_dtype, tm, tn, tk, name, b_stacked=False, out_stacked=False, n_outer=False, after=None,
            m_blocks=None, out_into=None, a_parts=False, b_parts=False):
    if a_parts:
        assert mode == "nt" and tk == a.shape[2]
        m, kdim = a.shape[1], a.shape[0] * a.shape[2]
    elif mode == "tn":
        kdim, m = a.shape
    else:
        m, kdim = a.shape
    if b_parts:
        assert mode == "nn" and not b_stacked and b.shape[2] % tn == 0
        n = b.shape[0] * b.shape[2]
    elif b_stacked:
        if mode == "nn":
            n = b.shape[0] * b.shape[2]
            assert tn == b.shape[2] and tk == kdim == b.shape[1]
        else:
            assert mode == "nt" and tk == b.shape[2] and b.shape[0] * tk == kdim
            n = b.shape[1]
    else:
        n = b.shape[0] if mode == "nt" else b.shape[1]
    assert m % tm == 0 and n % tn == 0 and kdim % tk == 0
    gm, gn, gk = m // tm, n // tn, kdim // tk
    i0 = 0
    if m_blocks is not None:
        assert mode != "tn" and not out_stacked
        i0, gm = m_blocks

    def ijk(f):
        if n_outer:
            return lambda j, i, k: f(i + i0, j, k)
        return lambda i, j, k: f(i + i0, j, k)

    if a_parts:
        a_spec = pl.BlockSpec((None, tm, tk), ijk(lambda i, j, k: (k, i, 0)))
    elif mode == "tn":
        a_spec = pl.BlockSpec((tk, tm), ijk(lambda i, j, k: (k, i)))
    else:
        a_spec = pl.BlockSpec((tm, tk), ijk(lambda i, j, k: (i, k)))
    if b_parts:
        per_part = b.shape[2] // tn
        b_spec = pl.BlockSpec((None, tk, tn), ijk(lambda i, j, k: (j // per_part, k, j % per_part)))
    elif b_stacked and mode == "nn":
        b_spec = pl.BlockSpec((None, tk, tn), ijk(lambda i, j, k: (j, 0, 0)))
    elif b_stacked:
        b_spec = pl.BlockSpec((None, tn, tk), ijk(lambda i, j, k: (k, j, 0)))
    elif mode == "nt":
        b_spec = pl.BlockSpec((tn, tk), ijk(lambda i, j, k: (j, k)))
    else:
        b_spec = pl.BlockSpec((tk, tn), ijk(lambda i, j, k: (k, j)))
    if out_stacked:
        assert tm == m
        out_shape = jax.ShapeDtypeStruct((gn, m, tn), out_dtype)
        o_spec = pl.BlockSpec((None, tm, tn), ijk(lambda i, j, k: (j, i, 0)))
    else:
        out_shape = jax.ShapeDtypeStruct((m, n), out_dtype)
        o_spec = pl.BlockSpec((tm, tn), ijk(lambda i, j, k: (i, j)))
    dims = {"nn": _NN, "nt": _NT, "tn": _TN}[mode]

    n_in = 2 + (after is not None) + (out_into is not None)

    def body(*refs):
        a_ref, b_ref, o_ref = refs[0], refs[1], refs[n_in]
        part = _dot(a_ref[...], b_ref[...], dims)
        if gk == 1:
            o_ref[...] = part.astype(out_dtype)
            return
        acc_ref = refs[-1]
        k = pl.program_id(2)

        @pl.when(k == 0)
        def _():
            acc_ref[...] = part

        @pl.when(jnp.logical_and(k > 0, k < gk - 1))
        def _():
            acc_ref[...] += part

        @pl.when(k == gk - 1)
        def _():
            o_ref[...] = (acc_ref[...] + part).astype(out_dtype)

    a_tile = _nbytes((tm, tk), a.dtype)
    b_tile = _nbytes((tk, tn), b.dtype)
    o_tile = _nbytes((tm, tn), out_dtype)
    est = 2 * (a_tile + b_tile + o_tile) + 3 * tm * tn * 4 + (a_tile + b_tile)
    grid = (gn, gm, gk) if n_outer else (gm, gn, gk)
    operands, in_specs = [a, b], [a_spec, b_spec]
    if after is not None:
        operands.append(after)
        in_specs.append(pl.BlockSpec(memory_space=pl.ANY))
    aliases = {}
    if out_into is not None:
        aliases = {len(operands): 0}
        operands.append(out_into)
        in_specs.append(pl.BlockSpec(memory_space=pl.ANY))
    return pl.pallas_call(
        body, name=name, grid=grid, in_specs=in_specs, out_specs=o_spec, out_shape=out_shape,
        scratch_shapes=[] if gk == 1 else [pltpu.VMEM((tm, tn), F32)], input_output_aliases=aliases,
        compiler_params=_params(("parallel", "parallel", "arbitrary"), est),
    )(*operands)


def _row_spec(tr, width, col_block=0):
    return pl.BlockSpec((tr, width), lambda i: (i, col_block))


def _bcast_spec(width):
    return pl.BlockSpec((1, width), lambda i: (0, 0))


def _col_spec(width, tr):
    return pl.BlockSpec((width, tr), lambda i: (0, i))


def _rmsnorm_fwd(x, gain, name, tr, transposed=False):
    rows, width = x.shape

    def body(x_ref, g_ref, o_ref, *t_ref):
        xv = x_ref[...]
        r = lax.rsqrt(jnp.mean(xv * xv, axis=-1, keepdims=True) + RMS_EPS)
        h = xv * r * g_ref[...]
        o_ref[...] = h.astype(BF16)
        if transposed:
            t_ref[0][...] = h.T.astype(BF16)

    out_specs, out_shape = [_row_spec(tr, width)], [jax.ShapeDtypeStruct((rows, width), BF16)]
    if transposed:
        out_specs.append(_col_spec(width, tr))
        out_shape.append(jax.ShapeDtypeStruct((width, rows), BF16))
    out = pl.pallas_call(
        body, name=name, grid=(rows // tr,), in_specs=[_row_spec(tr, width), _bcast_spec(width)],
        out_specs=out_specs, out_shape=out_shape,
        compiler_params=_params(("parallel",), 10 * tr * width * 4),
    )(x, gain)
    return out if transposed else out[0]


def _residual_rmsnorm_fwd(x, merged, wout, gain, name, tr):
    rows, width = x.shape

    def body(x_ref, m_ref, w_ref, g_ref, x1_ref, h_ref, ht_ref):
        xv = x_ref[...] + _dot(m_ref[...], w_ref[...], _NN)
        x1_ref[...] = xv
        r = lax.rsqrt(jnp.mean(xv * xv, axis=-1, keepdims=True) + RMS_EPS)
        h = xv * r * g_ref[...]
        h_ref[...] = h.astype(BF16)
        ht_ref[...] = h.T.astype(BF16)

    return pl.pallas_call(
        body, name=name, grid=(rows // tr,),
        in_specs=[_row_spec(tr, width), _row_spec(tr, merged.shape[1]),
                  pl.BlockSpec(wout.shape, lambda i: (0, 0)), _bcast_spec(width)],
        out_specs=[_row_spec(tr, width), _row_spec(tr, width), _col_spec(width, tr)],
        out_shape=[jax.ShapeDtypeStruct((rows, width), F32), jax.ShapeDtypeStruct((rows, width), BF16),
                   jax.ShapeDtypeStruct((width, rows), BF16)],
        compiler_params=_params(("parallel",), 14 * tr * width * 4),
    )(x, merged, wout, gain)


def _matmul_rmsnorm_bwd(a, b, x, dres, gain, name, tm, tk, a_parts=False, b_stacked=False, after=None,
                        m_blocks=None, out_into=None, with_bf16=True):
    width = x.shape[1]
    m = a.shape[1] if a_parts else a.shape[0]
    kdim = a.shape[0] * a.shape[2] if a_parts else a.shape[1]
    gk = kdim // tk
    i0, gm = (0, m // tm) if m_blocks is None else m_blocks
    n_in = 5 + (after is not None) + (out_into is not None)

    def body(*refs):
        a_ref, b_ref, x_ref, dres_ref, g_ref = refs[:5]
        outs = refs[n_in:]
        dx_ref, dg_ref, acc_ref = outs[0], outs[-2], outs[-1]
        i, k = pl.program_id(0), pl.program_id(1)
        part = _dot(a_ref[...], b_ref[...], _NT)

        @pl.when(k == 0)
        def _():
            acc_ref[...] = part

        @pl.when(jnp.logical_and(k > 0, k < gk - 1))
        def _():
            acc_ref[...] += part

        @pl.when(k == gk - 1)
        def _():
            dhv = acc_ref[...] + part if gk > 1 else part
            xv = x_ref[...]
            r = lax.rsqrt(jnp.mean(xv * xv, axis=-1, keepdims=True) + RMS_EPS)
            xhat = xv * r
            dyg = dhv * g_ref[...]
            dx = dres_ref[...] + r * (dyg - xhat * jnp.mean(dyg * xhat, axis=-1, keepdims=True))
            dx_ref[...] = dx
            if with_bf16:
                outs[1][...] = dx.astype(BF16)
            gpart = jnp.sum(dhv * xhat, axis=0, keepdims=True)

            @pl.when(i == 0)
            def _():
                dg_ref[...] = gpart

            @pl.when(i > 0)
            def _():
                dg_ref[...] += gpart

    rows = lambda width_: pl.BlockSpec((tm, width_), lambda i, k: (i + i0, 0))
    if a_parts:
        a_spec = pl.BlockSpec((None, tm, tk), lambda i, k: (k, i + i0, 0))
    else:
        a_spec = pl.BlockSpec((tm, tk), lambda i, k: (i + i0, k))
    if b_stacked:
        b_spec = pl.BlockSpec((None, width, tk), lambda i, k: (k, 0, 0))
    else:
        b_spec = pl.BlockSpec((width, tk), lambda i, k: (0, k))
    operands = [a, b, x, dres, gain]
    in_specs = [a_spec, b_spec, rows(width), rows(width), pl.BlockSpec((1, width), lambda i, k: (0, 0))]
    for extra in (after, out_into):
        if extra is not None:
            operands.append(extra)
            in_specs.append(pl.BlockSpec(memory_space=pl.ANY))
    aliases = {} if out_into is None else {len(operands) - 1: 0}
    out_specs = [rows(width)] + ([rows(width)] if with_bf16 else []) + [pl.BlockSpec((1, width), lambda i, k: (0, 0))]
    out_shape = [jax.ShapeDtypeStruct((m, width), F32)] + ([jax.ShapeDtypeStruct((m, width), BF16)] if with_bf16 else [])
    out_shape.append(jax.ShapeDtypeStruct((1, width), F32))
    est = 4 * tm * tk + 4 * width * tk + 12 * tm * width * 4
    return pl.pallas_call(
        body, name=name, grid=(gm, gk), in_specs=in_specs, out_specs=out_specs, out_shape=out_shape,
        scratch_shapes=[pltpu.VMEM((tm, width), F32)], input_output_aliases=aliases,
        compiler_params=_params(("arbitrary", "arbitrary"), est),
    )(*operands)


def _gain_grad(x, dh, name):
    rows, width = x.shape

    def body(x_ref, dh_ref, dg_ref):
        xv = x_ref[...]
        r = lax.rsqrt(jnp.mean(xv * xv, axis=-1, keepdims=True) + RMS_EPS)
        dg_ref[...] = jnp.sum(dh_ref[...] * xv * r, axis=0, keepdims=True)

    return pl.pallas_call(
        body, name=name, grid=(1,), in_specs=[_row_spec(rows, width), _row_spec(rows, width)],
        out_specs=_bcast_spec(width), out_shape=jax.ShapeDtypeStruct((1, width), F32),
        compiler_params=_params(("arbitrary",), 6 * rows * width * 4),
    )(x, dh)


def _lb_table(logits, name):
    slots, width = logits.shape

    def body(l_ref, o_ref):
        lv = l_ref[...]
        mx = jnp.max(lv, axis=0, keepdims=True)
        e = jnp.exp(lv - mx)
        o_ref[...] = e[0:1, :] / jnp.sum(e, axis=0, keepdims=True)

    return pl.pallas_call(
        body, name=name, grid=(1,), in_specs=[pl.BlockSpec((slots, width), lambda i: (0, 0))],
        out_specs=_bcast_spec(width), out_shape=jax.ShapeDtypeStruct((1, width), F32),
    )(logits)


def _branch_merge_fwd(proj, outs, weights, tr):
    w = D_MODEL

    def body(ghg_ref, gda_ref, gmem_ref, ohg_ref, oda_ref, omem_ref, whg_ref, wda_ref, wmem_ref,
             m_ref, thg_ref, tda_ref, tmem_ref):
        acc = None
        for g_ref, o_ref, w_ref, t_ref in ((ghg_ref, ohg_ref, whg_ref, thg_ref), (gda_ref, oda_ref, wda_ref, tda_ref),
                                           (gmem_ref, omem_ref, wmem_ref, tmem_ref)):
            t = _dot(o_ref[...], w_ref[...], _NN)
            t_ref[...] = t.astype(BF16)
            term = _sigmoid(g_ref[...]) * t
            acc = term if acc is None else acc + term
        m_ref[...] = acc.astype(BF16)

    whole = lambda a: pl.BlockSpec(a.shape, lambda i: (0, 0))
    shape = jax.ShapeDtypeStruct((SEQ, w), BF16)
    return pl.pallas_call(
        body, name="branch_merge_fwd", grid=(SEQ // tr,),
        in_specs=[_row_spec(tr, w, 10), _row_spec(tr, w, 11), _row_spec(tr, w, 12)]
        + [_row_spec(tr, o.shape[1]) for o in outs] + [whole(wt) for wt in weights],
        out_specs=[_row_spec(tr, w)] * 4, out_shape=[shape] * 4,
        compiler_params=_params(("parallel",), 20 * tr * w * 4),
    )(proj, proj, proj, *outs, *weights)


def _dproj_buffer():
    return lax.empty((SEQ, IN_COLS), BF16)


def _gate_merge_bwd(proj, ts, weights, dmerged, dproj, tr):
    w = D_MODEL
    steps = SEQ // tr
    gate_col0 = 10 * w

    def body(ghg_ref, gda_ref, gmem_ref, thg_ref, tda_ref, tmem_ref, whg_ref, wda_ref, wmem_ref, dm_ref, dproj_in,
             dthg_ref, dtda_ref, dtmem_ref, dohg_ref, doda_ref, domem_ref, dproj_ref, stage, sems):
        del dproj_in
        i = pl.program_id(0)
        slot = i % 2

        def slot_copy(s):
            rows = pl.ds(pl.multiple_of(i * tr, tr), tr)
            return pltpu.make_async_copy(stage.at[s], dproj_ref.at[rows, pl.ds(gate_col0, 3 * w)], sems.at[s])

        @pl.when(i >= 2)
        def _():
            slot_copy(slot).wait()

        dm = dm_ref[...]
        branches = ((ghg_ref, thg_ref, whg_ref, dthg_ref, dohg_ref), (gda_ref, tda_ref, wda_ref, dtda_ref, doda_ref),
                    (gmem_ref, tmem_ref, wmem_ref, dtmem_ref, domem_ref))
        for b, (g_ref, t_ref, w_ref, dt_ref, do_ref) in enumerate(branches):
            s = _sigmoid(g_ref[...])
            dt = (s * dm).astype(BF16)
            dt_ref[...] = dt
            do_ref[...] = _dot(dt, w_ref[...], _NT)
            stage[slot, :, b * w:(b + 1) * w] = (dm * t_ref[...].astype(F32) * s * (1.0 - s)).astype(BF16)
        slot_copy(slot).start()

        @pl.when(i == steps - 1)
        def _():
            slot_copy(1 - slot).wait()
            slot_copy(slot).wait()

    assert steps >= 2
    whole = lambda a: pl.BlockSpec(a.shape, lambda i: (0, 0))
    return pl.pallas_call(
        body, name="gate_merge_bwd", grid=(steps,),
        in_specs=[_row_spec(tr, w, 10), _row_spec(tr, w, 11), _row_spec(tr, w, 12)] + [_row_spec(tr, w)] * 3
        + [whole(wt) for wt in weights] + [_row_spec(tr, w), _any_spec()],
        out_specs=[_row_spec(tr, w)] * 3 + [_row_spec(tr, wt.shape[0]) for wt in weights] + [_any_spec()],
        out_shape=[jax.ShapeDtypeStruct((SEQ, w), BF16)] * 3
        + [jax.ShapeDtypeStruct((SEQ, wt.shape[0]), F32) for wt in weights]
        + [jax.ShapeDtypeStruct((SEQ, IN_COLS), BF16)],
        scratch_shapes=[pltpu.VMEM((2, tr, 3 * w), BF16), pltpu.SemaphoreType.DMA((2,))],
        input_output_aliases={10: 6},
        compiler_params=_params(("arbitrary",), 34 * tr * w * 4),
    )(proj, proj, proj, *ts, *weights, dmerged, dproj)


def _ffn_in_swiglu(h2, wfin, tm, tn):
    gm, gn = SEQ // tm, D_FF // tn

    def body(h_ref, wa_ref, wb_ref, a_ref, b_ref, act_ref, actt_ref):
        h = h_ref[...]
        a = _dot(h, wa_ref[...], _NN)
        b = _dot(h, wb_ref[...], _NN)
        act = a * _sigmoid(a) * b
        a_ref[...] = a.astype(BF16)
        b_ref[...] = b.astype(BF16)
        act_ref[...] = act.astype(BF16)
        actt_ref[...] = act.T.astype(BF16)

    tile = pl.BlockSpec((tm, tn), lambda j, i: (i, j))
    shape = jax.ShapeDtypeStruct((SEQ, D_FF), BF16)
    return pl.pallas_call(
        body, name="ffn_in_swiglu_fwd", grid=(gn, gm),
        in_specs=[pl.BlockSpec((tm, D_MODEL), lambda j, i: (i, 0)),
                  pl.BlockSpec((D_MODEL, tn), lambda j, i: (0, j)),
                  pl.BlockSpec((D_MODEL, tn), lambda j, i: (0, gn + j))],
        out_specs=[tile, tile, tile, pl.BlockSpec((tn, tm), lambda j, i: (j, i))],
        out_shape=[shape, shape, shape, jax.ShapeDtypeStruct((D_FF, SEQ), BF16)],
        compiler_params=_params(("parallel", "parallel"), 4 * tm * D_MODEL + 8 * D_MODEL * tn + 16 * tm * tn
                                + 6 * tm * tn * 4),
    )(h2, wfin, wfin)


def _ffn_out_bwd_swiglu(dyb, wfout, a, b, tm, tn):
    gm, gn = SEQ // tm, D_FF // tn

    def body(dy_ref, w_ref, a_ref, b_ref, o_ref):
        d = _dot(dy_ref[...], w_ref[...], _NT)
        av = a_ref[...].astype(F32)
        bv = b_ref[...].astype(F32)
        s = _sigmoid(av)
        silu = av * s
        o_ref[0] = (d * bv * (s + silu * (1.0 - s))).astype(BF16)
        o_ref[1] = (d * silu).astype(BF16)

    tile = pl.BlockSpec((tm, tn), lambda i, j: (i, j))
    return pl.pallas_call(
        body, name="ffn_out_bwd_swiglu", grid=(gm, gn),
        in_specs=[pl.BlockSpec((tm, D_MODEL), lambda i, j: (i, 0)), pl.BlockSpec((tn, D_MODEL), lambda i, j: (j, 0)),
                  tile, tile],
        out_specs=pl.BlockSpec((2, tm, tn), lambda i, j: (0, i, j)),
        out_shape=jax.ShapeDtypeStruct((2, SEQ, D_FF), BF16),
        compiler_params=_params(("parallel", "parallel"), 4 * tm * D_MODEL + 4 * tn * D_MODEL + 16 * tm * tn
                                + 6 * tm * tn * 4),
    )(dyb, wfout, a, b)


def _ffn_out_loss_head(x1, act, wfout, target, tr):
    w = D_MODEL

    def body(x_ref, a_ref, w_ref, t_ref, dy_ref, dyb_ref, loss_ref, acc_ref):
        err = x_ref[...] + _dot(a_ref[...], w_ref[...], _NN) - t_ref[...]
        dy = err * (1.0 / w)
        dy_ref[...] = dy
        dyb_ref[...] = dy.astype(BF16)
        part = jnp.sum(err * err, axis=0, keepdims=True)

        @pl.when(pl.program_id(0) == 0)
        def _():
            acc_ref[...] = part

        @pl.when(pl.program_id(0) > 0)
        def _():
            acc_ref[...] += part

        @pl.when(pl.program_id(0) == SEQ // tr - 1)
        def _():
            total = jnp.sum(acc_ref[...], axis=1, keepdims=True) * (0.5 / w)
            loss_ref[...] = jnp.broadcast_to(total, (1, LANE))

    return pl.pallas_call(
        body, name="ffn_out_loss_head", grid=(SEQ // tr,),
        in_specs=[_row_spec(tr, w), _row_spec(tr, D_FF), pl.BlockSpec((D_FF, w), lambda i: (0, 0)), _row_spec(tr, w)],
        out_specs=[_row_spec(tr, w), _row_spec(tr, w), _bcast_spec(LANE)],
        out_shape=[jax.ShapeDtypeStruct((SEQ, w), F32), jax.ShapeDtypeStruct((SEQ, w), BF16),
                   jax.ShapeDtypeStruct((1, LANE), F32)],
        scratch_shapes=[pltpu.VMEM((1, w), F32)],
        compiler_params=_params(("arbitrary",), 12 * tr * w * 4 + 4 * D_FF * w),
    )(x1, act, wfout, target)


GLA_ROWS = 256
GLA_CPB = GLA_ROWS // HG_CHUNK
GLA_NBLK = SEQ // GLA_ROWS
GLA_WAYS = 4
GLA_TRIPS = GLA_NBLK // GLA_WAYS
GLA_GRAD_WAYS = 4
GLA_GRAD_TRIPS = GLA_NBLK // GLA_GRAD_WAYS
GLA_NCK = SEQ // HG_CHUNK
GLA_INTER_WAYS = 8


def _dot_split(m, xv, dims, terms=3):
    dot = lambda t: lax.dot_general(m, t, dims, preferred_element_type=F32)
    hi = xv.astype(BF16)
    r1 = xv - hi.astype(F32)
    mid = r1.astype(BF16)
    if terms == 2:
        return dot(mid) + dot(hi)
    lo = (r1 - mid.astype(F32)).astype(BF16)
    return (dot(lo) + dot(mid)) + dot(hi)


def _gla_block(qc, fc, lbv, masks):
    mask, maskb, direction = masks
    sq = _sigmoid(qc)
    q = qc * sq * HG_SCALE
    sf = _sigmoid(fc)
    forget = lbv + (1.0 - lbv) * sf
    k = 1.0 - forget
    logf = jnp.log(forget)
    b = _dot_split(maskb, logf, _NN)
    ends = []
    for j in range(GLA_CPB):
        lo, hi = j * HG_CHUNK, (j + 1) * HG_CHUNK
        end = jnp.where(direction == 0, b[hi - 1:hi, :], b[lo:lo + 1, :])
        ends.append(jnp.broadcast_to(end, (HG_CHUNK, HEAD_DIM)))
    bt = jnp.concatenate(ends, axis=0)
    eb = jnp.exp(b)
    qt = q * eb
    kt = k * jnp.exp(-b)
    kh = k * jnp.exp(bt - b)
    a = jnp.where(mask, _dot(qt, kt, _NT), 0.0)
    return dict(sq=sq, sf=sf, forget=forget, k=k, b=b, bt=bt, eb=eb, qt=qt, kt=kt, kh=kh, a=a)


def _gla_masks(direction):
    row = lax.broadcasted_iota(jnp.int32, (GLA_ROWS, GLA_ROWS), 0)
    col = lax.broadcasted_iota(jnp.int32, (GLA_ROWS, GLA_ROWS), 1)
    same = (row // HG_CHUNK) == (col // HG_CHUNK)
    mask = jnp.logical_and(same, jnp.where(direction == 0, row - col, col - row) >= 0)
    return mask, jnp.where(mask, 1.0, 0.0).astype(BF16), direction


def _gla_chunk_rows(j):
    return slice(j * HG_CHUNK, (j + 1) * HG_CHUNK)


def _gla_block_rows(b):
    return pl.ds(pl.multiple_of(b * GLA_ROWS, GLA_ROWS), GLA_ROWS)


def _head_spec(col_block0):
    return pl.BlockSpec((SEQ, HEAD_DIM), lambda h, d: (0, col_block0 + h))


def _gla_fwd(proj, lb, gain):
    nck = GLA_NCK

    def body(q_ref, f_ref, v_ref, g_ref, lb_ref, gain_ref, ohg_ref, opre_ref, st_ref, qt_scr, dec_scr, cs_scr):
        d = pl.program_id(1)
        masks = _gla_masks(d)
        lbv = lb_ref[...]

        @pl.when(d == 0)
        def _():
            opre_ref[...] = jnp.zeros_like(opre_ref)

        def intra(s, carry):
            blocks = [s + w * GLA_TRIPS for w in range(GLA_WAYS)]
            rows = [_gla_block_rows(b) for b in blocks]
            loaded = [(q_ref[r, :], f_ref[r, :], v_ref[r, :], opre_ref[r, :]) for r in rows]
            results = []
            for qc, fc, v, o_prev in loaded:
                ck = _gla_block(qc, fc, lbv, masks)
                o = o_prev + _dot(ck["a"], v, _NN)
                cs = [_dot(v[_gla_chunk_rows(j), :], ck["kh"][_gla_chunk_rows(j), :], _TN) for j in range(GLA_CPB)]
                dec = [jnp.exp(ck["bt"][j * HG_CHUNK:j * HG_CHUNK + 8, :]) for j in range(GLA_CPB)]
                results.append((o, ck["qt"].astype(BF16), cs, dec))
            for b, r, (o, qt, cs, dec) in zip(blocks, rows, results):
                opre_ref[r, :] = o
                qt_scr[r, :] = qt
                for j in range(GLA_CPB):
                    cs_scr[b * GLA_CPB + j] = cs[j]
                    dec_scr[b * GLA_CPB + j] = dec[j]
            return carry

        lax.fori_loop(0, GLA_TRIPS, intra, 0)

        def scan(i, st):
            c = jnp.where(d == 0, i, nck - 1 - i)
            st_ref[c] = st
            return st * dec_scr[c][0:1, :] + cs_scr[c]

        lax.fori_loop(0, nck, scan, jnp.zeros((HEAD_DIM, HEAD_DIM), F32), unroll=4)

        def inter(t, carry):
            chunks = [t + w * (nck // GLA_INTER_WAYS) for w in range(GLA_INTER_WAYS)]
            rows = [pl.ds(pl.multiple_of(c * HG_CHUNK, HG_CHUNK), HG_CHUNK) for c in chunks]
            loaded = [(opre_ref[r, :], qt_scr[r, :], st_ref[c]) for r, c in zip(rows, chunks)]
            for r, o in zip(rows, [o_prev + _dot(qt, st, _NT) for o_prev, qt, st in loaded]):
                opre_ref[r, :] = o
            return carry

        lax.fori_loop(0, nck // GLA_INTER_WAYS, inter, 0)

        @pl.when(d == 1)
        def _():
            o = opre_ref[...]
            r = lax.rsqrt(jnp.mean(o * o, axis=-1, keepdims=True) + RMS_EPS)
            g = g_ref[...]
            ohg_ref[...] = (o * r * gain_ref[...] * (g * _sigmoid(g))).astype(BF16)

    blk = SEQ * HEAD_DIM * 4
    return pl.pallas_call(
        body, name="gla_fwd", grid=(HG_HEADS, 2),
        in_specs=[_head_spec(CB_HG_Q),
                  pl.BlockSpec((SEQ, HEAD_DIM), lambda h, d: (0, CB_F + 8 * d + h)),
                  _head_spec(CB_HG_I), _head_spec(CB_HG_G),
                  pl.BlockSpec((None, None, 1, HEAD_DIM), lambda h, d: (d, h, 0, 0)),
                  pl.BlockSpec((1, HEAD_DIM), lambda h, d: (0, 0))],
        out_specs=[_head_spec(0), _head_spec(0),
                   pl.BlockSpec((None, None, nck, HEAD_DIM, HEAD_DIM), lambda h, d: (h, d, 0, 0, 0)),
                   pl.BlockSpec((None, SEQ, HEAD_DIM), lambda h, d: (d, 0, h)),
                   pl.BlockSpec((None, None, nck, 8, HEAD_DIM), lambda h, d: (h, d, 0, 0, 0))],
        out_shape=[jax.ShapeDtypeStruct((SEQ, D_MODEL), BF16), jax.ShapeDtypeStruct((SEQ, D_MODEL), F32),
                   jax.ShapeDtypeStruct((HG_HEADS, 2, nck, HEAD_DIM, HEAD_DIM), F32),
                   jax.ShapeDtypeStruct((2, SEQ, D_MODEL), BF16),
                   jax.ShapeDtypeStruct((HG_HEADS, 2, nck, 8, HEAD_DIM), F32)],
        scratch_shapes=[pltpu.VMEM((nck, HEAD_DIM, HEAD_DIM), F32)],
        compiler_params=_params(("parallel", "arbitrary"), 8 * blk + 3 * blk + 2 * blk + 2 * blk + blk),
    )(proj, proj, proj, proj, lb, gain)


def _gla_bwd(proj, lb, gain, o_pre, states, qt_all, dec_all, do_hg, dproj):
    nck = GLA_NCK
    do, dproj, dgain = _gla_bwd_norm(proj, gain, o_pre, do_hg, dproj, 256)

    def body(q_ref, f_ref, v_ref, lb_ref, do_ref, st_ref, qt_ref, dec_scr, dproj_in, dproj_ref, dlb_ref,
             dq_acc, dv_acc, dst_scr, cs_scr, df_out, dq_out, dv_out, sems):
        del dproj_in
        h, d = pl.program_id(0), pl.program_id(1)
        masks = _gla_masks(d)
        mask, maskb, _ = masks
        lbv = lb_ref[...]

        def column_copy(k, staging, col_block):
            cols = pl.ds(pl.multiple_of(col_block * LANE, LANE), LANE)
            return pltpu.make_async_copy(staging, dproj_ref.at[:, cols], sems.at[k])

        df_copy = column_copy(0, df_out, CB_F + 8 * d + h)
        dq_copy = column_copy(1, dq_out, CB_HG_Q + h)
        dv_copy = column_copy(2, dv_out, CB_HG_I + h)
        last = jnp.logical_and(h == HG_HEADS - 1, d == 1)

        def intra(s, carry):
            blocks = [s + w * GLA_TRIPS for w in range(GLA_WAYS)]
            loaded = [(qt_ref[r, :], do_ref[r, :]) for r in map(_gla_block_rows, blocks)]
            results = [[_dot(doc[_gla_chunk_rows(j), :], qt[_gla_chunk_rows(j), :], _TN) for j in range(GLA_CPB)]
                       for qt, doc in loaded]
            for b, cs in zip(blocks, results):
                for j in range(GLA_CPB):
                    cs_scr[b * GLA_CPB + j] = cs[j]
            return carry

        lax.fori_loop(0, GLA_TRIPS, intra, 0)

        def scan(i, dst):
            c = jnp.where(d == 0, nck - 1 - i, i)
            dst_scr[c] = dst
            return dst * dec_scr[c][0:1, :] + cs_scr[c]

        lax.fori_loop(0, nck, scan, jnp.zeros((HEAD_DIM, HEAD_DIM), F32), unroll=4)

        @pl.when(d == 0)
        def _():
            dq_acc[...] = jnp.zeros_like(dq_acc)
            dv_acc[...] = jnp.zeros_like(dv_acc)

        def block_grads(qc, fc, v, doc, states_in, dstates, decays):
            ck = _gla_block(qc, fc, lbv, masks)
            qt, kt, kh, a = ck["qt"], ck["kt"], ck["kh"], ck["a"]
            da = jnp.where(mask, _dot(doc, v, _NT), 0.0)
            dqt_i = _dot(da, kt, _NN)
            dkt = _dot(da, qt, _TN)
            dv_i = _dot(a, doc, _TN)
            dqt_p, dv_p, dkh_p, dbt_p = [], [], [], []
            for j in range(GLA_CPB):
                cr = _gla_chunk_rows(j)
                st_in, dst = states_in[j], dstates[j]
                dqt_p.append(dqt_i[cr, :] + _dot(doc[cr, :], st_in, _NN))
                dv_p.append(dv_i[cr, :] + _dot(kh[cr, :], dst, _NT))
                dkh_j = _dot(v[cr, :], dst, _NN)
                dkh_p.append(dkh_j)
                dbt_j = (decays[j][0:1, :] * jnp.sum(dst * st_in, axis=0, keepdims=True)
                         + jnp.sum(dkh_j * kh[cr, :], axis=0, keepdims=True))
                dbt_p.append(jnp.broadcast_to(dbt_j, (HG_CHUNK, HEAD_DIM)))
            dqt = jnp.concatenate(dqt_p, axis=0)
            dv = jnp.concatenate(dv_p, axis=0)
            dkh = jnp.concatenate(dkh_p, axis=0)
            dbt = jnp.concatenate(dbt_p, axis=0)
            db = dqt * qt - dkt * kt - dkh * kh
            dq = dqt * ck["eb"]
            dk = dkt * jnp.exp(-ck["b"]) + dkh * jnp.exp(ck["bt"] - ck["b"])
            dlogf = _dot_split(maskb, db, _TN, terms=2) + dbt
            dforget = dlogf / ck["forget"] - dk
            sf, sq = ck["sf"], ck["sq"]
            df = (dforget * (1.0 - lbv) * sf * (1.0 - sf)).astype(BF16)
            dqc = dq * HG_SCALE * (sq + qc * sq * (1.0 - sq))
            return df, dqc, dv, jnp.sum(dforget * (1.0 - sf), axis=0, keepdims=True)

        def grads(s, dlb):
            blocks = [s + w * GLA_GRAD_TRIPS for w in range(GLA_GRAD_WAYS)]
            rows = [_gla_block_rows(b) for b in blocks]
            loaded = []
            for b, r in zip(blocks, rows):
                chunks = [b * GLA_CPB + j for j in range(GLA_CPB)]
                loaded.append((q_ref[r, :], f_ref[r, :], v_ref[r, :], do_ref[r, :], [st_ref[c] for c in chunks],
                               [dst_scr[c] for c in chunks], [dec_scr[c] for c in chunks], dq_acc[r, :], dv_acc[r, :]))
            results = [block_grads(*t[:7]) + (t[7], t[8]) for t in loaded]
            for r, (df, dqc, dv, dlb_part, dq_prev, dv_prev) in zip(rows, results):
                df_out[r, :] = df
                dq_acc[r, :] = dq_prev + dqc
                dv_acc[r, :] = dv_prev + dv
                dlb = dlb + dlb_part
            return dlb

        @pl.when(jnp.logical_or(h > 0, d > 0))
        def _():
            df_copy.wait()

        dlb_ref[...] = lax.fori_loop(0, GLA_GRAD_TRIPS, grads, jnp.zeros((1, HEAD_DIM), F32))
        df_copy.start()

        @pl.when(d == 1)
        def _():
            @pl.when(h > 0)
            def _():
                dq_copy.wait()
                dv_copy.wait()

            dq_out[...] = dq_acc[...].astype(BF16)
            dv_out[...] = dv_acc[...].astype(BF16)
            dq_copy.start()
            dv_copy.start()

        @pl.when(last)
        def _():
            df_copy.wait()
            dq_copy.wait()
            dv_copy.wait()

    blk = SEQ * HEAD_DIM * 4
    state_bytes = nck * HEAD_DIM * HEAD_DIM * 4
    dproj, dlb = pl.pallas_call(
        body, name="gla_bwd", grid=(HG_HEADS, 2),
        in_specs=[_head_spec(CB_HG_Q),
                  pl.BlockSpec((SEQ, HEAD_DIM), lambda h, d: (0, CB_F + 8 * d + h)),
                  _head_spec(CB_HG_I),
                  pl.BlockSpec((None, None, 1, HEAD_DIM), lambda h, d: (d, h, 0, 0)),
                  _head_spec(0),
                  pl.BlockSpec((None, None, nck, HEAD_DIM, HEAD_DIM), lambda h, d: (h, d, 0, 0, 0)),
                  pl.BlockSpec((None, SEQ, HEAD_DIM), lambda h, d: (d, 0, h)),
                  pl.BlockSpec((None, None, nck, 8, HEAD_DIM), lambda h, d: (h, d, 0, 0, 0)),
                  _any_spec()],
        out_specs=[_any_spec(), pl.BlockSpec((None, None, 1, HEAD_DIM), lambda h, d: (d, h, 0, 0))],
        out_shape=[jax.ShapeDtypeStruct((SEQ, IN_COLS), BF16), jax.ShapeDtypeStruct((2, HG_HEADS, 1, HEAD_DIM), F32)],
        scratch_shapes=[pltpu.VMEM((SEQ, HEAD_DIM), F32)] * 2
        + [pltpu.VMEM((nck, HEAD_DIM, HEAD_DIM), F32)] * 2
        + [pltpu.VMEM((SEQ, HEAD_DIM), BF16)] * 3 + [pltpu.SemaphoreType.DMA((3,))],
        input_output_aliases={8: 0},
        compiler_params=_params(("arbitrary", "arbitrary"), 8 * blk + 4 * state_bytes + 3 * blk + 3 * blk),
    )(proj, proj, proj, lb, do, states, qt_all, dec_all, dproj)
    return dproj, dlb, dgain


def _gla_bwd_norm(proj, gain, o_pre, do_hg, dproj, tr):
    w = D_MODEL

    def body(g_ref, gain_ref, opre_ref, dohg_ref, dproj_in, do_ref, dg_ref, dgain_ref):
        del dproj_in
        gainv = gain_ref[...]
        part = jnp.zeros((1, HEAD_DIM), F32)
        for h in range(HG_HEADS):
            hs = slice(h * HEAD_DIM, (h + 1) * HEAD_DIM)
            o = opre_ref[:, hs]
            r = lax.rsqrt(jnp.mean(o * o, axis=-1, keepdims=True) + RMS_EPS)
            ohat = o * r
            g = g_ref[:, hs]
            sg = _sigmoid(g)
            silu = g * sg
            dout = dohg_ref[:, hs]
            dg_ref[:, hs] = (dout * ohat * gainv * (sg + silu * (1.0 - sg))).astype(BF16)
            dy = dout * silu
            part = part + jnp.sum(dy * ohat, axis=0, keepdims=True)
            dn = dy * gainv
            do_ref[:, hs] = r * (dn - ohat * jnp.mean(dn * ohat, axis=-1, keepdims=True))

        @pl.when(pl.program_id(0) == 0)
        def _():
            dgain_ref[...] = part

        @pl.when(pl.program_id(0) > 0)
        def _():
            dgain_ref[...] += part

    return pl.pallas_call(
        body, name="gla_bwd_norm", grid=(SEQ // tr,),
        in_specs=[_row_spec(tr, w, CB_HG_G * LANE // w), _bcast_spec(HEAD_DIM), _row_spec(tr, w), _row_spec(tr, w),
                  _any_spec()],
        out_specs=[_row_spec(tr, w), _row_spec(tr, w, CB_HG_G * LANE // w), _bcast_spec(HEAD_DIM)],
        out_shape=[jax.ShapeDtypeStruct((SEQ, w), F32), jax.ShapeDtypeStruct((SEQ, IN_COLS), BF16),
                   jax.ShapeDtypeStruct((1, HEAD_DIM), F32)],
        input_output_aliases={4: 1},
        compiler_params=_params(("arbitrary",), 16 * tr * w * 4),
    )(proj, gain, o_pre, do_hg, dproj)


def _da_residue_rows(r, n0, size, d):
    if d == 1:
        return pl.ds(pl.multiple_of(n0, 8), size)
    return pl.ds(r + n0 * d, size, stride=d)


def _da_residue_ways(d, nqb):
    return min(d, 4) if nqb <= 2 else 1


def _da_rmsnorm(x, gain):
    r = lax.rsqrt(jnp.mean(x * x, axis=-1, keepdims=True) + RMS_EPS)
    return x * r, r, x * r * gain


def _da_scores(qn_scr, kn_scr, slope, i, ld):
    w0 = jnp.clip(i * DA_QB - DA_RADIUS, 0, ld - DA_WIN)
    w0 = pl.multiple_of(w0, DA_RADIUS)
    qrows = pl.ds(pl.multiple_of(i * DA_QB, DA_QB), DA_QB)
    win = pl.ds(w0, DA_WIN)
    qb = qn_scr[qrows, :]
    kw = kn_scr[win, :]
    s = _dot(qb, kw, _NT) * DA_SCALE
    qpos = i * DA_QB + lax.broadcasted_iota(jnp.int32, (DA_QB, DA_WIN), 0)
    kpos = w0 + lax.broadcasted_iota(jnp.int32, (DA_QB, DA_WIN), 1)
    arel = jnp.abs(kpos - qpos)
    s = s - slope * arel.astype(F32)
    s = jnp.where(arel <= DA_RADIUS, s, NEG_INF)
    return s, qb, kw, qrows, win


def _da_slopes(group):
    d = DA_DILATIONS[group]
    sl = _alibi_slopes(DA_HEADS)[4 * group:4 * group + 4] * d
    return jnp.asarray(np.broadcast_to(sl[:, None, None], (4, 1, LANE)).copy())


def _da_fwd(proj, gq, gk, group):
    d = DA_DILATIONS[group]
    ld = SEQ // d
    nqb = ld // DA_QB

    ways = min(DA_FWD_WAYS, nqb)
    rways = _da_residue_ways(d, nqb)

    def body(q_ref, k_ref, v_ref, gq_ref, gk_ref, sl_ref, o_ref, lse_ref, qn_scr, kn_scr, v_scr):
        slope = sl_ref[:, 0:1]

        def residues(t, carry):
            rs = [t * rways + u for u in range(rways)]
            for u, r in enumerate(rs):
                sel = _da_residue_rows(r, 0, ld, d)
                qn_scr[u] = _da_rmsnorm(q_ref[sel, :], gq_ref[...])[2].astype(BF16)
                kn_scr[u] = _da_rmsnorm(k_ref[sel, :], gk_ref[...])[2].astype(BF16)
                v_scr[u] = v_ref[sel, :].astype(BF16)

            def block(u, i):
                s, _, _, _, win = _da_scores(qn_scr.at[u], kn_scr.at[u], slope, i, ld)
                m = jnp.max(s, axis=-1, keepdims=True)
                p = jnp.exp(s - m)
                l = jnp.sum(p, axis=-1, keepdims=True)
                return _dot(p, v_scr[u, win, :], _NN) / l, jnp.broadcast_to(m + jnp.log(l), (DA_QB, HEAD_DIM))

            def step(i, c2):
                todo = [(u, r, i + w * (nqb // ways)) for u, r in enumerate(rs) for w in range(ways)]
                for (u, r, b), (o, lse) in zip(todo, [block(u, b) for u, _, b in todo]):
                    out = _da_residue_rows(r, b * DA_QB, DA_QB, d)
                    o_ref[out, :] = o
                    lse_ref[out, :] = lse
                return c2

            return lax.fori_loop(0, nqb // ways, step, carry)

        lax.fori_loop(0, d // rways, residues, 0)

    seq_spec = lambda base: pl.BlockSpec((SEQ, HEAD_DIM), lambda h: (0, base + 4 * group + h))
    out_spec = pl.BlockSpec((SEQ, HEAD_DIM), lambda h: (0, h))
    gain_spec = pl.BlockSpec((1, HEAD_DIM), lambda h: (0, 0))
    blk = SEQ * HEAD_DIM * 4
    return pl.pallas_call(
        body, name=f"da_fwd_g{group}", grid=(DA_HEADS_PER_GROUP,),
        in_specs=[seq_spec(CB_DA_Q), seq_spec(CB_DA_K), seq_spec(CB_DA_V), gain_spec, gain_spec,
                  pl.BlockSpec((None, 1, LANE), lambda h: (h, 0, 0))],
        out_specs=[out_spec, out_spec],
        out_shape=[jax.ShapeDtypeStruct((SEQ, DA_WIDTH), F32)] * 2,
        scratch_shapes=[pltpu.VMEM((rways, ld, HEAD_DIM), BF16)] * 3,
        compiler_params=_params(("parallel",), 10 * blk + 2 * blk),
    )(proj, proj, proj, gq, gk, _da_slopes(group))


def _da_merge(os, lses, tr):
    w = DA_WIDTH

    def body(o0, o1, o2, l0, l1, l2, ob_ref, of_ref, lse_ref):
        la, lb_, lc = l0[...], l1[...], l2[...]
        m = jnp.maximum(jnp.maximum(la, lb_), lc)
        ea, eb, ec = jnp.exp(la - m), jnp.exp(lb_ - m), jnp.exp(lc - m)
        tot = ea + eb + ec
        o = (ea * o0[...] + eb * o1[...] + ec * o2[...]) / tot
        of_ref[...] = o
        ob_ref[...] = o.astype(BF16)
        lse_ref[...] = m + jnp.log(tot)

    return pl.pallas_call(
        body, name="da_merge", grid=(SEQ // tr,), in_specs=[_row_spec(tr, w)] * 6,
        out_specs=[_row_spec(tr, w)] * 3,
        out_shape=[jax.ShapeDtypeStruct((SEQ, w), BF16), jax.ShapeDtypeStruct((SEQ, w), F32),
                   jax.ShapeDtypeStruct((SEQ, w), F32)],
        compiler_params=_params(("parallel",), 24 * tr * w * 4),
    )(*os, *lses)


def _da_rowdot(do, o, tr):
    w = DA_WIDTH

    def body(do_ref, o_ref, out_ref):
        prod = do_ref[...] * o_ref[...]
        for h in range(w // HEAD_DIM):
            sl = slice(h * HEAD_DIM, (h + 1) * HEAD_DIM)
            out_ref[:, sl] = jnp.broadcast_to(jnp.sum(prod[:, sl], axis=-1, keepdims=True), (tr, HEAD_DIM))

    return pl.pallas_call(
        body, name="da_rowdot", grid=(SEQ // tr,), in_specs=[_row_spec(tr, w)] * 2,
        out_specs=_row_spec(tr, w), out_shape=jax.ShapeDtypeStruct((SEQ, w), F32),
        compiler_params=_params(("parallel",), 10 * tr * w * 4),
    )(do, o)


def _da_bwd(proj, gq, gk, do, lse, dd, dproj, group):
    d = DA_DILATIONS[group]
    ld = SEQ // d
    nqb = ld // DA_QB
    ways = min(DA_WAYS, nqb)
    rways = _da_residue_ways(d, nqb)

    def body(q_ref, k_ref, v_ref, gq_ref, gk_ref, sl_ref, do_ref, lse_ref, dd_ref, dproj_in,
             dproj_ref, dgq_ref, dgk_ref,
             qn_scr, kn_scr, v_scr, dqn_scr, dkn_scr, dvr_scr, dq_scr, dk_scr, dv_scr, dq_out, dk_out, dv_out, sems):
        del dproj_in
        head = pl.program_id(0)
        gqv, gkv = gq_ref[...], gk_ref[...]
        slope = sl_ref[:, 0:1]

        copies = []
        for k, (staging, base) in enumerate(((dq_out, CB_DA_Q), (dk_out, CB_DA_K), (dv_out, CB_DA_V))):
            cols = pl.ds(pl.multiple_of((base + 4 * group + head) * LANE, LANE), LANE)
            copies.append(pltpu.make_async_copy(staging, dproj_ref.at[:, cols], sems.at[k]))

        def residues(t, carry):
            rs = [t * rways + u for u in range(rways)]
            sels = [_da_residue_rows(r, 0, ld, d) for r in rs]
            for u, sel in enumerate(sels):
                qn_scr[u] = _da_rmsnorm(q_ref[sel, :], gqv)[2].astype(BF16)
                kn_scr[u] = _da_rmsnorm(k_ref[sel, :], gkv)[2].astype(BF16)
                v_scr[u] = v_ref[sel, :].astype(BF16)
            dkn_scr[...] = jnp.zeros_like(dkn_scr)
            dvr_scr[...] = jnp.zeros_like(dvr_scr)

            def block(u, r, i):
                s, qb, kw, qrows, win = _da_scores(qn_scr.at[u], kn_scr.at[u], slope, i, ld)
                src = _da_residue_rows(r, i * DA_QB, DA_QB, d)
                p = jnp.exp(s - lse_ref[src, :][:, 0:1])
                dob = do_ref[src, :]
                dp = _dot(dob, v_scr[u, win, :], _NT)
                ds = p * (dp - dd_ref[src, :][:, 0:1]) * DA_SCALE
                return u, qrows, win, _dot(p, dob, _TN), _dot(ds, kw, _NN), _dot(ds, qb, _TN)

            def step(i, c2):
                todo = [(u, r, i + w * (nqb // ways)) for u, r in enumerate(rs) for w in range(ways)]
                for u, qrows, win, dv, dqn, dkn in [block(*t) for t in todo]:
                    dvr_scr[u, win, :] += dv
                    dqn_scr[u, qrows, :] = dqn
                    dkn_scr[u, win, :] += dkn
                return c2

            lax.fori_loop(0, nqb // ways, step, 0)

            pq, pk = carry
            for u, sel in enumerate(sels):
                parts = []
                for x_ref, gv, dn_scr, dx_scr in ((q_ref, gqv, dqn_scr, dq_scr), (k_ref, gkv, dkn_scr, dk_scr)):
                    hat, rstd, _ = _da_rmsnorm(x_ref[sel, :], gv)
                    dn = dn_scr[u]
                    dyg = dn * gv
                    dx_scr[sel, :] = rstd * (dyg - hat * jnp.mean(dyg * hat, axis=-1, keepdims=True))
                    parts.append(jnp.sum(dn * hat, axis=0, keepdims=True))
                dv_scr[sel, :] = dvr_scr[u]
                pq, pk = pq + parts[0], pk + parts[1]
            return pq, pk

        zero = jnp.zeros((1, HEAD_DIM), F32)
        pq, pk = lax.fori_loop(0, d // rways, residues, (zero, zero))

        @pl.when(pl.program_id(0) == 0)
        def _():
            dgq_ref[...] = pq
            dgk_ref[...] = pk

        @pl.when(pl.program_id(0) > 0)
        def _():
            dgq_ref[...] += pq
            dgk_ref[...] += pk

        @pl.when(head > 0)
        def _():
            for cp in copies:
                cp.wait()

        dq_out[...] = dq_scr[...].astype(BF16)
        dk_out[...] = dk_scr[...].astype(BF16)
        dv_out[...] = dv_scr[...].astype(BF16)
        for cp in copies:
            cp.start()

        @pl.when(head == DA_HEADS_PER_GROUP - 1)
        def _():
            for cp in copies:
                cp.wait()

    seq_spec = lambda base: pl.BlockSpec((SEQ, HEAD_DIM), lambda h: (0, base + 4 * group + h))
    out_spec = pl.BlockSpec((SEQ, HEAD_DIM), lambda h: (0, h))
    gain_spec = pl.BlockSpec((1, HEAD_DIM), lambda h: (0, 0))
    gshape = jax.ShapeDtypeStruct((1, HEAD_DIM), F32)
    blk = SEQ * HEAD_DIM * 4
    return pl.pallas_call(
        body, name=f"da_bwd_g{group}", grid=(DA_HEADS_PER_GROUP,),
        in_specs=[seq_spec(CB_DA_Q), seq_spec(CB_DA_K), seq_spec(CB_DA_V), gain_spec, gain_spec,
                  pl.BlockSpec((None, 1, LANE), lambda h: (h, 0, 0)), out_spec, out_spec, out_spec, _any_spec()],
        out_specs=[_any_spec(), gain_spec, gain_spec],
        out_shape=[jax.ShapeDtypeStruct((SEQ, IN_COLS), BF16), gshape, gshape],
        scratch_shapes=[pltpu.VMEM((rways, ld, HEAD_DIM), BF16)] * 3 + [pltpu.VMEM((rways, ld, HEAD_DIM), F32)] * 3
        + [pltpu.VMEM((SEQ, HEAD_DIM), F32)] * 3 + [pltpu.VMEM((SEQ, HEAD_DIM), BF16)] * 3
        + [pltpu.SemaphoreType.DMA((3,))],
        input_output_aliases={9: 0},
        compiler_params=_params(("arbitrary",), 12 * blk + 3 * blk + 3 * blk + 3 * blk + 2 * blk),
    )(proj, proj, proj, gq, gk, _da_slopes(group), do, lse, dd, dproj)


MEM_WAYS = 4


def _mem_probs(q, kn, gq):
    qhat, rq, qn = _da_rmsnorm(q, gq)
    s = _dot(qn, kn, _NT) * MEM_SCALE
    m = jnp.max(s, axis=-1, keepdims=True)
    e = jnp.exp(s - m)
    p = e / jnp.sum(e, axis=-1, keepdims=True)
    return p, qhat, rq, qn


def _mem_pieces(tq):
    rows = tq // MEM_WAYS
    return [slice(w * rows, (w + 1) * rows) for w in range(MEM_WAYS)]


def _mem_fwd(proj, kv, gq, gk, tq):
    def body(q_ref, k_ref, v_ref, gq_ref, gk_ref, o_ref):
        kn = _da_rmsnorm(k_ref[...], gk_ref[...])[2]
        v, gqv = v_ref[...], gq_ref[...]
        pieces = _mem_pieces(tq)
        outs = [_dot(_mem_probs(q_ref[rs, :], kn, gqv)[0], v, _NN) for rs in pieces]
        for rs, o in zip(pieces, outs):
            o_ref[rs, :] = o.astype(BF16)

    gain_spec = pl.BlockSpec((1, HEAD_DIM), lambda h, i: (0, 0))
    return pl.pallas_call(
        body, name="mem_fwd", grid=(MEM_HEADS, SEQ // tq),
        in_specs=[pl.BlockSpec((tq, HEAD_DIM), lambda h, i: (i, CB_MEM_Q + h)),
                  pl.BlockSpec((N_MEM, HEAD_DIM), lambda h, i: (0, h)),
                  pl.BlockSpec((N_MEM, HEAD_DIM), lambda h, i: (0, MEM_HEADS + h)), gain_spec, gain_spec],
        out_specs=pl.BlockSpec((tq, HEAD_DIM), lambda h, i: (i, h)),
        out_shape=jax.ShapeDtypeStruct((SEQ, MEM_WIDTH), BF16),
        compiler_params=_params(("parallel", "parallel"), 16 * tq * N_MEM * 4),
    )(proj, kv, kv, gq, gk)


def _mem_bwd(proj, kv, gq, gk, do, dproj, tq):
    nq = SEQ // tq

    def body(q_ref, k_ref, v_ref, gq_ref, gk_ref, do_ref, dproj_in, dq_ref, dk_ref, dv_ref, dgq_ref, dgk_ref, dkn_scr):
        del dproj_in
        h, i = pl.program_id(0), pl.program_id(1)
        gqv, gkv = gq_ref[...], gk_ref[...]
        khat, rk, kn = _da_rmsnorm(k_ref[...], gkv)
        v = v_ref[...]

        def piece(rs):
            p, qhat, rq, qn = _mem_probs(q_ref[rs, :], kn, gqv)
            dob = do_ref[rs, :]
            dp = _dot(dob, v, _NT)
            ds = p * (dp - jnp.sum(p * dp, axis=-1, keepdims=True)) * MEM_SCALE
            dqn = _dot(ds, kn, _NN)
            dyg = dqn * gqv
            dq = (rq * (dyg - qhat * jnp.mean(dyg * qhat, axis=-1, keepdims=True))).astype(BF16)
            return dq, _dot(p, dob, _TN), _dot(ds, qn, _TN), jnp.sum(dqn * qhat, axis=0, keepdims=True)

        pieces = _mem_pieces(tq)
        results = [piece(rs) for rs in pieces]
        for rs, res in zip(pieces, results):
            dq_ref[rs, :] = res[0]
        dvp = sum((res[1] for res in results[1:]), results[0][1])
        dknp = sum((res[2] for res in results[1:]), results[0][2])
        dgq_part = sum((res[3] for res in results[1:]), results[0][3])
        first = jnp.logical_and(h == 0, i == 0)

        @pl.when(first)
        def _():
            dgq_ref[...] = dgq_part

        @pl.when(jnp.logical_not(first))
        def _():
            dgq_ref[...] += dgq_part

        @pl.when(i == 0)
        def _():
            dv_ref[...] = dvp
            dkn_scr[...] = dknp

        @pl.when(i > 0)
        def _():
            dv_ref[...] += dvp
            dkn_scr[...] += dknp

        @pl.when(i == nq - 1)
        def _():
            dkn = dkn_scr[...]
            dkg = dkn * gkv
            dk_ref[...] = rk * (dkg - khat * jnp.mean(dkg * khat, axis=-1, keepdims=True))
            dgk_part = jnp.sum(dkn * khat, axis=0, keepdims=True)

            @pl.when(h == 0)
            def _():
                dgk_ref[...] = dgk_part

            @pl.when(h > 0)
            def _():
                dgk_ref[...] += dgk_part

    gain_spec = pl.BlockSpec((1, HEAD_DIM), lambda h, i: (0, 0))
    kvout = pl.BlockSpec((N_MEM, HEAD_DIM), lambda h, i: (0, h))
    return pl.pallas_call(
        body, name="mem_bwd", grid=(MEM_HEADS, nq),
        in_specs=[pl.BlockSpec((tq, HEAD_DIM), lambda h, i: (i, CB_MEM_Q + h)),
                  pl.BlockSpec((N_MEM, HEAD_DIM), lambda h, i: (0, h)),
                  pl.BlockSpec((N_MEM, HEAD_DIM), lambda h, i: (0, MEM_HEADS + h)), gain_spec, gain_spec,
                  pl.BlockSpec((tq, HEAD_DIM), lambda h, i: (i, h)), _any_spec()],
        out_specs=[pl.BlockSpec((tq, HEAD_DIM), lambda h, i: (i, CB_MEM_Q + h)), kvout, kvout, gain_spec, gain_spec],
        out_shape=[jax.ShapeDtypeStruct((SEQ, IN_COLS), BF16), jax.ShapeDtypeStruct((N_MEM, MEM_WIDTH), F32),
                   jax.ShapeDtypeStruct((N_MEM, MEM_WIDTH), F32), jax.ShapeDtypeStruct((1, HEAD_DIM), F32),
                   jax.ShapeDtypeStruct((1, HEAD_DIM), F32)],
        scratch_shapes=[pltpu.VMEM((N_MEM, HEAD_DIM), F32)], input_output_aliases={6: 0},
        compiler_params=_params(("arbitrary", "arbitrary"), 24 * tq * N_MEM * 4),
    )(proj, kv, kv, gq, gk, do, dproj)


def _mesh_position():
    return lax.axis_index("x"), lax.axis_index("y"), lax.axis_index("c")


def _any_spec():
    return pl.BlockSpec(memory_space=pl.ANY)


def _all_gather(shards):
    n = len(shards)

    def body(*refs):
        ins, outs = refs[:n], refs[n:2 * n]
        send_sems, recv_sems, local_sems = refs[2 * n:]
        x, y, c = _mesh_position()
        me, sibling = (x, y, c), (x, y, 1 - c)
        first = (jnp.where(c == 0, 1 - x, x), jnp.where(c == 0, y, 1 - y), c)
        second = (jnp.where(c == 0, x, 1 - x), jnp.where(c == 0, 1 - y, y), c)
        diagonal = (1 - x, 1 - y, c)

        def copy(w, k, block, to, src=None):
            px, py, pc = block
            rows = outs[w].at[4 * px + 2 * py + pc]
            return pltpu.make_async_remote_copy(
                src_ref=rows if src is None else src, dst_ref=rows,
                send_sem=send_sems.at[7 * w + k], recv_sem=recv_sems.at[7 * w + k],
                device_id=to, device_id_type=MESH)

        started = []
        for w in range(n):
            mine = pltpu.make_async_copy(ins[w], outs[w].at[4 * x + 2 * y + c], local_sems.at[w])
            mine.start()
            started.append(mine)
        sends = []
        for w in range(n):
            own = [copy(w, 0, me, sibling, src=ins[w]), copy(w, 1, me, first, src=ins[w]),
                   copy(w, 2, me, second, src=ins[w])]
            for cp in own:
                cp.start()
            sends += own
        for w in range(n):
            copy(w, 1, first, me).wait_recv()
            follow = [copy(w, 3, first, second), copy(w, 4, first, sibling)]
            for cp in follow:
                cp.start()
            copy(w, 2, second, me).wait_recv()
            follow.append(copy(w, 5, second, sibling))
            follow[-1].start()
            sends += follow
        for w in range(n):
            copy(w, 3, diagonal, me).wait_recv()
            passed = copy(w, 6, diagonal, sibling)
            passed.start()
            sends.append(passed)
        for w in range(n):
            for k in (0, 4, 5, 6):
                copy(w, k, sibling, me).wait_recv()
        for cp in sends:
            cp.wait_send()
        for mine in started:
            mine.wait()

    return pl.pallas_call(
        body, name="weights_all_gather",
        in_specs=[_any_spec()] * n, out_specs=[_any_spec()] * n,
        out_shape=[jax.ShapeDtypeStruct((N_DEV,) + s.shape, s.dtype) for s in shards],
        scratch_shapes=[pltpu.SemaphoreType.DMA((7 * n,)), pltpu.SemaphoreType.DMA((7 * n,)),
                        pltpu.SemaphoreType.DMA((n,))],
    )(*shards)


def _chip_of(j, x, y):
    return (1 - x if j & 1 else x, 1 - y if j & 2 else y)


_HBM_SPEC = pl.BlockSpec(memory_space=pltpu.HBM)
_SEM_SPEC = pl.BlockSpec(memory_space=pltpu.SEMAPHORE)
_DATAFLOW_EFFECT = pltpu.SideEffectType.DATAFLOW_SIDE_EFFECTING
TOKEN_SHAPE = (8, D_MODEL)


def _copies_start(name, arrays, n_copies, plan):
    n = len(arrays)

    def body(*refs):
        send_sems, recv_sems, token = refs[n], refs[n + 1], refs[2 * n + 2]
        copies = plan(refs[:n])
        assert len(copies) == n_copies
        for k, (src, dst, dev) in enumerate(copies):
            pltpu.make_async_remote_copy(src_ref=src, dst_ref=dst, send_sem=send_sems.at[k], recv_sem=recv_sems.at[k],
                                         device_id=dev, device_id_type=MESH).start()
        token[...] = jnp.zeros_like(token)

    outs = pl.pallas_call(
        body, name=name,
        out_shape=(pltpu.SemaphoreType.DMA((n_copies,)), pltpu.SemaphoreType.DMA((n_copies,)),
                   *[pltpu.HBM(a.shape, a.dtype) for a in arrays], jax.ShapeDtypeStruct(TOKEN_SHAPE, F32)),
        in_specs=[_HBM_SPEC] * n,
        out_specs=(_SEM_SPEC, _SEM_SPEC, *[_HBM_SPEC] * n, pl.BlockSpec(memory_space=pltpu.VMEM)),
        input_output_aliases={i: i + 2 for i in range(n)},
        compiler_params=pltpu.CompilerParams(has_side_effects=_DATAFLOW_EFFECT),
    )(*[pltpu.with_memory_space_constraint(a, pltpu.HBM) for a in arrays])
    return outs[0], outs[1], list(outs[2:2 + n]), outs[2 + n]


def _copies_wait(name, send_sems, recv_sems, arrays, n_copies, plan, after):
    n = len(arrays)
    after = list(after) if isinstance(after, (list, tuple)) else [after]

    def body(*refs):
        send_ref, recv_ref = refs[n], refs[n + 1]
        copies = plan(refs[:n])
        assert len(copies) == n_copies
        for k, (src, dst, dev) in enumerate(copies):
            cp = pltpu.make_async_remote_copy(src_ref=src, dst_ref=dst, send_sem=send_ref.at[k], recv_sem=recv_ref.at[k],
                                              device_id=dev, device_id_type=MESH)
            cp.wait_send()
            cp.wait_recv()

    outs = pl.pallas_call(
        body, name=name, out_shape=tuple(pltpu.HBM(a.shape, a.dtype) for a in arrays),
        in_specs=[_HBM_SPEC] * n + [_SEM_SPEC, _SEM_SPEC] + [pl.BlockSpec(memory_space=pl.ANY)] * len(after),
        out_specs=tuple([_HBM_SPEC] * n), input_output_aliases={i: i for i in range(n)},
        compiler_params=pltpu.CompilerParams(has_side_effects=_DATAFLOW_EFFECT),
    )(*arrays, send_sems, recv_sems, *after)
    return list(outs)


def _after(small, token):
    return small if token is None else small + token[0:1, :small.shape[-1]]


def _gather_plan_out(n):
    def plan(refs):
        x, y, c = _mesh_position()
        me = 4 * x + 2 * y + c
        copies = []
        for w in range(n):
            land = refs[n + w].at[me]
            copies.append((refs[w], land, (x, y, 1 - c)))
            for j in range(1, 4):
                copies.append((refs[w], land, (*_chip_of(j, x, y), c)))
        return copies
    return plan


def _gather_plan_pass(n):
    def plan(refs):
        x, y, c = _mesh_position()
        copies = []
        for w in range(n):
            for j in range(1, 4):
                px, py = _chip_of(j, x, y)
                rows = refs[w].at[4 * px + 2 * py + c]
                copies.append((rows, rows, (x, y, 1 - c)))
        return copies
    return plan


def _reduce_plan_sibling(n):
    def plan(refs):
        x, y, c = _mesh_position()
        copies = []
        for w in range(n):
            for j in range(4):
                px, py = _chip_of(j, x, y)
                copies.append((refs[w].at[4 * px + 2 * py + (1 - c)], refs[n + w].at[j], (x, y, 1 - c)))
        return copies
    return plan


def _reduce_plan_chips(n):
    def plan(refs):
        x, y, c = _mesh_position()
        copies = []
        for w in range(n):
            for j in range(1, 4):
                copies.append((refs[w].at[j - 1], refs[n + w].at[j - 1], (*_chip_of(j, x, y), c)))
        return copies
    return plan


def _chip_partials(grad, recv, name, tr):
    _, rows, width = grad.shape

    def body(g_ref, r_ref, own_ref, other_ref):
        x, y, c = _mesh_position()
        for j in range(4):
            px, py = _chip_of(j, x, y)
            total = g_ref[4 * px + 2 * py + c].astype(F32) + r_ref[j].astype(F32)
            if j == 0:
                own_ref[...] = total
            else:
                other_ref[j - 1] = total.astype(BF16)

    return pl.pallas_call(
        body, name=name, grid=(rows // tr,),
        in_specs=[pl.BlockSpec((N_DEV, tr, width), lambda i: (0, i, 0)),
                  pl.BlockSpec((4, tr, width), lambda i: (0, i, 0))],
        out_specs=[pl.BlockSpec((tr, width), lambda i: (i, 0)), pl.BlockSpec((3, tr, width), lambda i: (0, i, 0))],
        out_shape=[jax.ShapeDtypeStruct((rows, width), F32), jax.ShapeDtypeStruct((3, rows, width), BF16)],
        compiler_params=_params(("parallel",), 2 * 20 * tr * width * 2 + 8 * tr * width * 4),
    )(grad, recv)


def _adamw_math(w, g, m, v):
    m = ADAM_B1 * m + (1.0 - ADAM_B1) * g
    v = ADAM_B2 * v + (1.0 - ADAM_B2) * (g * g)
    m_hat = m / (1.0 - ADAM_B1 ** ADAM_STEP)
    v_hat = v / (1.0 - ADAM_B2 ** ADAM_STEP)
    delta = -ADAM_LR * (m_hat / (jnp.sqrt(v_hat) + ADAM_EPS) + ADAM_WD * w)
    return delta, m, v


def _adamw_shard(own, recv, w, m, v, name, tr):
    rows, width = own.shape

    def body(own_ref, r_ref, w_ref, m_ref, v_ref, g_ref, d_ref, nm_ref, nv_ref):
        g = own_ref[...]
        for j in range(3):
            g = g + r_ref[j].astype(F32)
        g_ref[...] = g
        d_ref[...], nm_ref[...], nv_ref[...] = _adamw_math(w_ref[...], g, m_ref[...], v_ref[...])

    spec = pl.BlockSpec((tr, width), lambda i: (i, 0))
    shape = jax.ShapeDtypeStruct((rows, width), F32)
    return pl.pallas_call(
        body, name=name, grid=(rows // tr,),
        in_specs=[spec, pl.BlockSpec((3, tr, width), lambda i: (0, i, 0)), spec, spec, spec],
        out_specs=[spec] * 4, out_shape=[shape] * 4,
        compiler_params=_params(("parallel",), 22 * tr * width * 4),
    )(own, recv, w, m, v)


def _small_all_reduce_adamw(gpart, lbpack, wpack, mpack, vpack, after):
    def body(gp_ref, lb_ref, w_ref, m_ref, v_ref, after_ref, g_ref, d_ref, nm_ref, nv_ref, gath_ref, send_sems, recv_sems):
        del after_ref
        x, y, c = _mesh_position()
        me = 4 * x + 2 * y + c
        gath_ref[me] = gp_ref[...]
        copies = []
        for j in range(1, N_DEV):
            peer = (x ^ (j >> 2), y ^ ((j >> 1) & 1), c ^ (j & 1))
            cp = pltpu.make_async_remote_copy(
                src_ref=gp_ref, dst_ref=gath_ref.at[me], send_sem=send_sems.at[j - 1], recv_sem=recv_sems.at[j - 1],
                device_id=peer, device_id_type=MESH)
            cp.start()
            copies.append(cp)
        for cp in copies:
            cp.wait()
        tot = gath_ref[0]
        for s in range(1, N_DEV):
            tot = tot + gath_ref[s]
        lb = lb_ref[...]
        dl = tot[16:32, :] * lb * (1.0 - lb)
        g = jnp.concatenate([tot[0:16, :], dl[0:8, :], -dl[0:8, :], dl[8:16, :], -dl[8:16, :],
                             tot[32:SMALL_GRAD_ROWS, :]], axis=0)
        g_ref[...] = g
        d_ref[...], nm_ref[...], nv_ref[...] = _adamw_math(w_ref[...], g, m_ref[...], v_ref[...])

    vm = pl.BlockSpec(memory_space=pltpu.VMEM)
    shape = jax.ShapeDtypeStruct((SMALL_ROWS, LANE), F32)
    return pl.pallas_call(
        body, name="small_all_reduce_adamw", in_specs=[vm] * 5 + [_any_spec()], out_specs=[vm] * 4,
        out_shape=[shape] * 4,
        scratch_shapes=[pltpu.VMEM((N_DEV, SMALL_GRAD_ROWS, LANE), F32),
                        pltpu.SemaphoreType.DMA((N_DEV - 1,)), pltpu.SemaphoreType.DMA((N_DEV - 1,))],
    )(gpart, lbpack, wpack, mpack, vpack, after)


_SMALL_NAMES = ("norm_mix_gain", "norm_mem_gain", "lb_logits_fw", "lb_logits_bw", "norm_ffn_gain",
                "hg_norm_gain", "da_q_gain", "da_k_gain", "mem_q_gain", "mem_k_gain")
_SMALL_ROW0 = {"norm_mix_gain": 0, "norm_mem_gain": 8, "lb_logits_fw": 16, "lb_logits_bw": 32, "norm_ffn_gain": 48,
               "hg_norm_gain": 56, "da_q_gain": 64, "da_k_gain": 72, "mem_q_gain": 80, "mem_k_gain": 88}
_LOSS_ROW = 96


def _pack_rows(parts, total_rows):
    rows = []
    for p in parts:
        r = p.reshape(-1, LANE)
        rows.append(jnp.pad(r, ((0, -r.shape[0] % 8), (0, 0))))
    used = sum(r.shape[0] for r in rows)
    if total_rows > used:
        rows.append(jnp.zeros((total_rows - used, LANE), F32))
    return jnp.concatenate(rows, axis=0)


def _unpack_small(pack, like):
    out = {}
    for name in _SMALL_NAMES:
        n = like[name].size // LANE
        r0 = _SMALL_ROW0[name]
        out[name] = pack[r0:r0 + n].reshape(like[name].shape)
    return out


def kernel(x, mem, norm_mix_gain, norm_mem_gain, w_in, lb_logits_fw, lb_logits_bw, hg_norm_gain, da_q_gain, da_k_gain, w_mem_kv, mem_q_gain, mem_k_gain, w_proj_hg, w_proj_da, w_proj_mem, w_out, norm_ffn_gain, w_ffn_in, w_ffn_out, loss_target, m_norm_mix_gain, m_norm_mem_gain, m_w_in, m_lb_logits_fw, m_lb_logits_bw, m_hg_norm_gain, m_da_q_gain, m_da_k_gain, m_w_mem_kv, m_mem_q_gain, m_mem_k_gain, m_w_proj_hg, m_w_proj_da, m_w_proj_mem, m_w_out, m_norm_ffn_gain, m_w_ffn_in, m_w_ffn_out, v_norm_mix_gain, v_norm_mem_gain, v_w_in, v_lb_logits_fw, v_lb_logits_bw, v_hg_norm_gain, v_da_q_gain, v_da_k_gain, v_w_mem_kv, v_mem_q_gain, v_mem_k_gain, v_w_proj_hg, v_w_proj_da, v_w_proj_mem, v_w_out, v_norm_ffn_gain, v_w_ffn_in, v_w_ffn_out):
    small_w = dict(norm_mix_gain=norm_mix_gain, norm_mem_gain=norm_mem_gain, lb_logits_fw=lb_logits_fw,
                   lb_logits_bw=lb_logits_bw, norm_ffn_gain=norm_ffn_gain, hg_norm_gain=hg_norm_gain,
                   da_q_gain=da_q_gain, da_k_gain=da_k_gain, mem_q_gain=mem_q_gain, mem_k_gain=mem_k_gain)
    small_m = dict(norm_mix_gain=m_norm_mix_gain, norm_mem_gain=m_norm_mem_gain, lb_logits_fw=m_lb_logits_fw,
                   lb_logits_bw=m_lb_logits_bw, norm_ffn_gain=m_norm_ffn_gain, hg_norm_gain=m_hg_norm_gain,
                   da_q_gain=m_da_q_gain, da_k_gain=m_da_k_gain, mem_q_gain=m_mem_q_gain, mem_k_gain=m_mem_k_gain)
    small_v = dict(norm_mix_gain=v_norm_mix_gain, norm_mem_gain=v_norm_mem_gain, lb_logits_fw=v_lb_logits_fw,
                   lb_logits_bw=v_lb_logits_bw, norm_ffn_gain=v_norm_ffn_gain, hg_norm_gain=v_hg_norm_gain,
                   da_q_gain=v_da_q_gain, da_k_gain=v_da_k_gain, mem_q_gain=v_mem_q_gain, mem_k_gain=v_mem_k_gain)
    big_w = dict(w_in=w_in[0], w_mem_kv=w_mem_kv[0], w_proj_hg=w_proj_hg[0], w_proj_da=w_proj_da[0],
                 w_proj_mem=w_proj_mem[0], w_out=w_out[0], w_ffn_in=w_ffn_in[0], w_ffn_out=w_ffn_out[0])
    big_m = dict(w_in=m_w_in[0], w_mem_kv=m_w_mem_kv[0], w_proj_hg=m_w_proj_hg[0], w_proj_da=m_w_proj_da[0],
                 w_proj_mem=m_w_proj_mem[0], w_out=m_w_out[0], w_ffn_in=m_w_ffn_in[0], w_ffn_out=m_w_ffn_out[0])
    big_v = dict(w_in=v_w_in[0], w_mem_kv=v_w_mem_kv[0], w_proj_hg=v_w_proj_hg[0], w_proj_da=v_w_proj_da[0],
                 w_proj_mem=v_w_proj_mem[0], w_out=v_w_out[0], w_ffn_in=v_w_ffn_in[0], w_ffn_out=v_w_ffn_out[0])

    row_tile = dict(w_in=128, w_mem_kv=128, w_proj_hg=128, w_proj_da=512, w_proj_mem=512, w_out=128,
                    w_ffn_in=128, w_ffn_out=352)
    rest = _BIG_NAMES[1:]
    shards = [big_w[n].astype(BF16) for n in rest]
    state = {}
    big_out = {}

    def reduce_start(group, stacked):
        names = tuple(stacked)
        arrays = [stacked[n] for n in names] + [lax.empty((4,) + stacked[n].shape[1:], BF16) for n in names]
        plan = _reduce_plan_sibling(len(names))
        send, recv, thru, token = _copies_start(f"grads_{group}_sibling_start", arrays, 4 * len(names), plan)
        state[group] = dict(names=names, plan=plan, send=send, recv=recv, arrays=thru)
        return token

    def reduce_middle(group, after):
        st = state[group]
        names, k = st["names"], len(st["names"])
        thru = _copies_wait(f"grads_{group}_sibling_wait", st["send"], st["recv"], st["arrays"], 4 * k, st["plan"], after)
        partials = [_chip_partials(thru[i], thru[k + i], f"chip_partials_{n}", row_tile[n]) for i, n in enumerate(names)]
        arrays = [p[1] for p in partials] + [lax.empty(p[1].shape, BF16) for p in partials]
        plan = _reduce_plan_chips(k)
        send, recv, thru2, token = _copies_start(f"grads_{group}_chips_start", arrays, 3 * k, plan)
        state[group] = dict(names=names, plan=plan, send=send, recv=recv, arrays=thru2, own=[p[0] for p in partials])
        return token

    def reduce_finish(group, after):
        st = state.pop(group)
        names, k = st["names"], len(st["names"])
        thru = _copies_wait(f"grads_{group}_chips_wait", st["send"], st["recv"], st["arrays"], 3 * k, st["plan"], after)
        for i, n in enumerate(names):
            big_out[n] = _adamw_shard(st["own"][i], thru[k + i], big_w[n], big_m[n], big_v[n], "adamw_" + n, row_tile[n])

    win_st = _all_gather([big_w["w_in"].astype(BF16)])[0]
    step = _local_step_stages(x[0], mem[0], loss_target[0], small_w, win_st)
    event, payload = next(step)
    local = None
    while True:
        reply = None
        if event == "begin":
            nr = len(rest)
            me = 4 * lax.axis_index("x") + 2 * lax.axis_index("y") + lax.axis_index("c")
            arrays = shards + [lax.dynamic_update_slice(lax.empty((N_DEV,) + s.shape, BF16), s[None], (me, 0, 0))
                               for s in shards]
            plan = _gather_plan_out(nr)
            send, recv, thru, reply = _copies_start("weights_rest_out_start", arrays, 4 * nr, plan)
            state["gather"] = dict(plan=plan, send=send, recv=recv, arrays=thru)
        elif event == "after_gla_fwd":
            st = state["gather"]
            nr = len(rest)
            thru = _copies_wait("weights_rest_out_wait", st["send"], st["recv"], st["arrays"], 4 * nr, st["plan"], payload)
            plan = _gather_plan_pass(nr)
            send, recv, lands, reply = _copies_start("weights_rest_pass_start", thru[nr:], 3 * nr, plan)
            state["gather"] = dict(plan=plan, send=send, recv=recv, arrays=lands)
        elif event == "need_weights":
            st = state.pop("gather")
            nr = len(rest)
            lands = _copies_wait("weights_rest_pass_wait", st["send"], st["recv"], st["arrays"], 3 * nr, st["plan"], payload)
            reply = dict(zip(rest, lands))
        elif event == "grads_ffn":
            reply = reduce_start("ffn", payload)
        elif event == "after_gate_merge_bwd":
            reply = reduce_middle("ffn", payload)
        elif event == "grads_mix":
            reply = reduce_start("mix", payload)
        elif event == "after_da_bwd_g0":
            reply = reduce_middle("mix", payload)
            reduce_finish("ffn", payload)
        elif event == "after_gla_bwd":
            reduce_finish("mix", payload)
        elif event == "grads_in":
            reply = reduce_start("in", payload)
        elif event == "after_proj_bwd_act_top":
            reply = reduce_middle("in", payload)
        elif event == "end":
            local = payload
            reduce_finish("in", [local["grad_x"]] + [big_out[n][0] for n in rest])
            break
        event, payload = step.send(reply)
    grad_x = local["grad_x"]

    wpack = _pack_rows([small_w[n] for n in _SMALL_NAMES], SMALL_ROWS)
    mpack = _pack_rows([small_m[n] for n in _SMALL_NAMES], SMALL_ROWS)
    vpack = _pack_rows([small_v[n] for n in _SMALL_NAMES], SMALL_ROWS)
    gs, ds, ms, vs = _small_all_reduce_adamw(local["gpart"], local["lbpack"], wpack, mpack, vpack, big_out["w_in"][0])
    loss = gs[_LOSS_ROW, 0]
    small_out = [_unpack_small(t, small_w) for t in (gs, ds, ms, vs)]

    order = ("norm_mix_gain", "norm_mem_gain", "w_in", "lb_logits_fw", "lb_logits_bw", "hg_norm_gain", "da_q_gain",
             "da_k_gain", "w_mem_kv", "mem_q_gain", "mem_k_gain", "w_proj_hg", "w_proj_da", "w_proj_mem", "w_out",
             "norm_ffn_gain", "w_ffn_in", "w_ffn_out")
    outs = [loss, grad_x[None]]
    for kind in range(4):
        for n in order:
            outs.append(big_out[n][kind][None] if n in big_out else small_out[kind][n])
    return tuple(outs)


_BIG_NAMES = ("w_in", "w_mem_kv", "w_proj_hg", "w_proj_da", "w_proj_mem", "w_out", "w_ffn_in", "w_ffn_out")


def _local_step(xs, mems, target, sw, wg):
    step = _local_step_stages(xs, mems, target, sw, wg["w_in"])
    stacked = {}
    event, payload = next(step)
    while event != "end":
        if event.startswith("grads_"):
            stacked.update(payload)
        event, payload = step.send(wg if event == "need_weights" else None)
    return dict(payload, stacked=stacked)


def _local_step_stages(xs, mems, target, sw, win_st):
    norm_mix_gain, norm_mem_gain, norm_ffn_gain = sw["norm_mix_gain"], sw["norm_mem_gain"], sw["norm_ffn_gain"]
    lb_logits_fw, lb_logits_bw, hg_norm_gain = sw["lb_logits_fw"], sw["lb_logits_bw"], sw["hg_norm_gain"]
    da_q_gain, da_k_gain, mem_q_gain, mem_k_gain = sw["da_q_gain"], sw["da_k_gain"], sw["mem_q_gain"], sw["mem_k_gain"]

    token = yield "begin", None
    lb_fw = _lb_table(lb_logits_fw, "lb_table_fw")
    lb_bw = _lb_table(lb_logits_bw, "lb_table_bw")
    lb = jnp.concatenate([lb_fw, lb_bw], axis=0).reshape(2, HG_HEADS, 1, HEAD_DIM)
    h, h_t = _rmsnorm_fwd(xs, _after(norm_mix_gain, token), "norm_mix_fwd", 512, transposed=True)
    proj = _matmul(h, win_st, "nn", F32, 1024, IN_SHARD, D_MODEL, "proj_fwd", b_stacked=True, n_outer=True)
    o_hg, o_pre, states, hg_qt, hg_decay = _gla_fwd(proj, lb, hg_norm_gain)
    token = yield "after_gla_fwd", o_hg
    da = [_da_fwd(proj, _after(da_q_gain, token), da_k_gain, g) for g in range(3)]
    o_da, o_da32, lse_da = _da_merge([t[0] for t in da], [t[1] for t in da], 512)
    wg = yield "need_weights", o_da
    wkv = wg["w_mem_kv"].reshape(D_MODEL, 2 * MEM_WIDTH)
    wphg = wg["w_proj_hg"].reshape(D_MODEL, D_MODEL)
    wpda = jnp.transpose(wg["w_proj_da"], (1, 0, 2)).reshape(DA_WIDTH, D_MODEL)
    wpmem = jnp.transpose(wg["w_proj_mem"], (1, 0, 2)).reshape(MEM_WIDTH, D_MODEL)
    wout = wg["w_out"].reshape(D_MODEL, D_MODEL)
    wfin = jnp.transpose(wg["w_ffn_in"], (1, 0, 2)).reshape(D_MODEL, 2 * D_FF)
    wfout = wg["w_ffn_out"].reshape(D_FF, D_MODEL)
    mem_n = _rmsnorm_fwd(mems, norm_mem_gain, "norm_mem_fwd", N_MEM)
    kv = _matmul(mem_n, wkv, "nn", F32, N_MEM, 1024, D_MODEL, "mem_kv_fwd")
    o_mem = _mem_fwd(proj, kv, mem_q_gain, mem_k_gain, 1024)
    branch_w = (wphg, wpda, wpmem)
    merged, t_hg, t_da, t_mem = _branch_merge_fwd(proj, (o_hg, o_da, o_mem), branch_w, 512)
    x1, h2, h2_t = _residual_rmsnorm_fwd(xs, merged, wout, norm_ffn_gain, "out_norm_ffn_fwd", 512)
    ffn_a, ffn_b, act, act_t = _ffn_in_swiglu(h2, wfin, 1024, 1408)
    dy, dyb, loss_part = _ffn_out_loss_head(x1, act, wfout, target, 512)

    dab = _ffn_out_bwd_swiglu(dyb, wfout, ffn_a, ffn_b, 1024, 1408)
    g_wfout = _matmul(act_t, dyb, "nn", BF16, 1408, 1024, 2048, "ffn_out_bwd_w")
    g_wfin = _matmul(h2_t, dab, "nn", BF16, 1024, 1408, 2048, "ffn_in_bwd_w", b_parts=True)
    token = yield "grads_ffn", dict(
        w_ffn_in=jnp.transpose(g_wfin.reshape(D_MODEL, N_DEV, 2 * D_FF // N_DEV), (1, 0, 2)),
        w_ffn_out=g_wfout.reshape(N_DEV, D_FF // N_DEV, D_MODEL))
    dx1, dx1b, g_norm_ffn = _matmul_rmsnorm_bwd(dab, wfin, x1, dy, _after(norm_ffn_gain, token),
                                                "ffn_in_bwd_act_norm", 512, D_FF, a_parts=True)
    dmerged = _matmul(dx1b, wout, "nt", F32, 512, 1024, D_MODEL, "out_bwd_act")
    g_wout = _matmul(merged, dx1b, "tn", BF16, 1024, 1024, 2048, "out_bwd_w")
    dt_hg, dt_da, dt_mem, do_hg, do_da, do_mem, dproj = _gate_merge_bwd(
        proj, (t_hg, t_da, t_mem), branch_w, dmerged, _dproj_buffer(), 256)
    token = yield "after_gate_merge_bwd", dt_hg
    g_wphg = _matmul(o_hg, dt_hg, "tn", BF16, 1024, 1024, 2048, "proj_hg_bwd_w", after=token)
    g_wpda = _matmul(o_da, dt_da, "tn", BF16, DA_WIDTH, D_MODEL, 2048, "proj_da_bwd_w")
    g_wpmem = _matmul(o_mem, dt_mem, "tn", BF16, MEM_WIDTH, D_MODEL, 2048, "proj_mem_bwd_w")
    by_owner = lambda g: jnp.transpose(g.reshape(g.shape[0], N_DEV, D_MODEL // N_DEV), (1, 0, 2))
    g_wpda, g_wpmem = by_owner(g_wpda), by_owner(g_wpmem)

    dproj, dk_mem, dv_mem, g_mem_q, g_mem_k = _mem_bwd(proj, kv, mem_q_gain, mem_k_gain, do_mem, dproj, 1024)
    dkv = jnp.concatenate([dk_mem, dv_mem], axis=1).astype(BF16)
    g_wkv = _matmul(mem_n, dkv, "tn", BF16, 1024, 1024, N_MEM, "mem_kv_bwd_w")
    dmem_n = _matmul(dkv, wkv, "nt", F32, N_MEM, 1024, 1024, "mem_kv_bwd_act")
    g_norm_mem = _gain_grad(mems, dmem_n, "norm_mem_bwd")
    token = yield "grads_mix", dict(
        w_mem_kv=g_wkv.reshape(N_DEV, D_MODEL // N_DEV, 2 * MEM_WIDTH),
        w_proj_hg=g_wphg.reshape(N_DEV, D_MODEL // N_DEV, D_MODEL),
        w_proj_da=g_wpda, w_proj_mem=g_wpmem,
        w_out=g_wout.reshape(N_DEV, D_MODEL // N_DEV, D_MODEL))

    dd_da = _da_rowdot(do_da, o_da32, 512)
    dproj, g_da_q, g_da_k = _da_bwd(proj, _after(da_q_gain, token), da_k_gain, do_da, lse_da, dd_da, dproj, 0)
    token = yield "after_da_bwd_g0", g_da_q
    for g in (1, 2):
        dproj, gq_part, gk_part = _da_bwd(proj, _after(da_q_gain, token), da_k_gain, do_da, lse_da, dd_da, dproj, g)
        g_da_q, g_da_k = g_da_q + gq_part, g_da_k + gk_part

    dproj, dlb, g_hg_norm = _gla_bwd(proj, lb, hg_norm_gain, o_pre, states, hg_qt, hg_decay, do_hg, dproj)
    yield "after_gla_bwd", dlb

    g_win =_matmul(h_t, dproj, "nn", BF16, 1024, IN_SHARD, 2048, "proj_bwd_w", out_stacked=True)
    token = yield "grads_in", dict(w_in=g_win)
    grad_x, g_mix_top = _matmul_rmsnorm_bwd(dproj, win_st, xs, dx1, norm_mix_gain, "proj_bwd_act_norm_top", 1024,
                                            IN_SHARD, b_stacked=True, after=token, m_blocks=(0, 2), with_bf16=False)
    token = yield "after_proj_bwd_act_top", g_mix_top
    grad_x, g_mix_bottom = _matmul_rmsnorm_bwd(dproj, win_st, xs, dx1, norm_mix_gain, "proj_bwd_act_norm_bottom", 1024,
                                               IN_SHARD, b_stacked=True, after=token, m_blocks=(2, 2),
                                               out_into=grad_x, with_bf16=False)
    g_norm_mix = g_mix_top + g_mix_bottom

    gpart = _pack_rows([g_norm_mix, g_norm_mem, dlb[0], dlb[1], g_norm_ffn, g_hg_norm, g_da_q, g_da_k,
                        g_mem_q, g_mem_k, loss_part], SMALL_GRAD_ROWS)
    lbpack = jnp.concatenate([lb_fw.reshape(8, LANE), lb_bw.reshape(8, LANE)], axis=0)
    yield "end", dict(grad_x=grad_x, gpart=gpart, lbpack=lbpack)
```

```python
import numpy as np
import jax
import jax.numpy as jnp
from jax import lax
from jax.experimental import pallas as pl
from jax.experimental.pallas import tpu as pltpu

F32 = jnp.float32
BF16 = jnp.bfloat16
MESH = pl.DeviceIdType.MESH

SEQ = 4096
D_MODEL = 1024
N_DEV = 8
N_MEM = 256
RMS_EPS = 1e-6
NEG_INF = -1e30
LANE = 128
HEAD_DIM = 128
HG_HEADS = 8
HG_CHUNK = 64
HG_SCALE = HEAD_DIM ** -0.5
DA_DILATIONS = (1, 4, 16)
DA_RADIUS = 64
DA_HEADS_PER_GROUP = 4
DA_HEADS = 12
DA_WIDTH = 512
DA_SCALE = HEAD_DIM ** -0.5
DA_QB = 128
DA_WIN = 256
DA_WAYS = 4
DA_FWD_WAYS = 4
MEM_HEADS = 4
MEM_WIDTH = 512
MEM_SCALE = HEAD_DIM ** -0.5
D_FF = 2816
IN_COLS = 13312
IN_SHARD = IN_COLS // N_DEV
CB_HG_Q, CB_F, CB_HG_I, CB_HG_G = 0, 8, 24, 32
CB_DA_Q, CB_DA_K, CB_DA_V, CB_MEM_Q = 40, 52, 64, 76
ADAM_LR, ADAM_B1, ADAM_B2, ADAM_EPS, ADAM_WD, ADAM_STEP = 0.001, 0.9, 0.999, 1e-08, 0.01, 10
VMEM_BYTES_V7X = 64 * 1024 * 1024
SMALL_ROWS = 104
SMALL_GRAD_ROWS = 88

_NN = (((1,), (0,)), ((), ()))
_NT = (((1,), (1,)), ((), ()))
_TN = (((0,), (0,)), ((), ()))


def _dot(a, b, dims):
    return lax.dot_general(a.astype(BF16), b.astype(BF16), dims, preferred_element_type=F32)


def _sigmoid(x):
    return 0.5 * jnp.tanh(0.5 * x) + 0.5


def _params(semantics, est_bytes):
    limit = int(min(VMEM_BYTES_V7X - (6 << 20), max(56 << 20, est_bytes * 3 // 2)))
    return pltpu.CompilerParams(dimension_semantics=semantics, vmem_limit_bytes=limit)


def _nbytes(shape, dtype):
    return int(np.prod(shape)) * jnp.dtype(dtype).itemsize


def _alibi_slopes(n):
    return (2.0 ** (-8.0 * np.arange(1, n + 1) / n)).astype(np.float32)


def _matmul(a, b, mode, out_dtype, tm, tn, tk, name, b_stacked=False, out_stacked=False, n_outer=False, after=None,
            m_blocks=None, out_into=None, a_parts=False, b_parts=False):
    if a_parts:
        assert mode == "nt" and tk == a.shape[2]
        m, kdim = a.shape[1], a.shape[0] * a.shape[2]
    elif mode == "tn":
        kdim, m = a.shape
    else:
        m, kdim = a.shape
    if b_parts:
        assert mode == "nn" and not b_stacked and b.shape[2] % tn == 0
        n = b.shape[0] * b.shape[2]
    elif b_stacked:
        if mode == "nn":
            n = b.shape[0] * b.shape[2]
            assert tn == b.shape[2] and tk == kdim == b.shape[1]
        else:
            assert mode == "nt" and tk == b.shape[2] and b.shape[0] * tk == kdim
            n = b.shape[1]
    else:
        n = b.shape[0] if mode == "nt" else b.shape[1]
    assert m % tm == 0 and n % tn == 0 and kdim % tk == 0
    gm, gn, gk = m // tm, n // tn, kdim // tk
    i0 = 0
    if m_blocks is not None:
        assert mode != "tn" and not out_stacked
        i0, gm = m_blocks

    def ijk(f):
        if n_outer:
            return lambda j, i, k: f(i + i0, j, k)
        return lambda i, j, k: f(i + i0, j, k)

    if a_parts:
        a_spec = pl.BlockSpec((None, tm, tk), ijk(lambda i, j, k: (k, i, 0)))
    elif mode == "tn":
        a_spec = pl.BlockSpec((tk, tm), ijk(lambda i, j, k: (k, i)))
    else:
        a_spec = pl.BlockSpec((tm, tk), ijk(lambda i, j, k: (i, k)))
    if b_parts:
        per_part = b.shape[2] // tn
        b_spec = pl.BlockSpec((None, tk, tn), ijk(lambda i, j, k: (j // per_part, k, j % per_part)))
    elif b_stacked and mode == "nn":
        b_spec = pl.BlockSpec((None, tk, tn), ijk(lambda i, j, k: (j, 0, 0)))
    elif b_stacked:
        b_spec = pl.BlockSpec((None, tn, tk), ijk(lambda i, j, k: (k, j, 0)))
    elif mode == "nt":
        b_spec = pl.BlockSpec((tn, tk), ijk(lambda i, j, k: (j, k)))
    else:
        b_spec = pl.BlockSpec((tk, tn), ijk(lambda i, j, k: (k, j)))
    if out_stacked:
        assert tm == m
        out_shape = jax.ShapeDtypeStruct((gn, m, tn), out_dtype)
        o_spec = pl.BlockSpec((None, tm, tn), ijk(lambda i, j, k: (j, i, 0)))
    else:
        out_shape = jax.ShapeDtypeStruct((m, n), out_dtype)
        o_spec = pl.BlockSpec((tm, tn), ijk(lambda i, j, k: (i, j)))
    dims = {"nn": _NN, "nt": _NT, "tn": _TN}[mode]

    n_in = 2 + (after is not None) + (out_into is not None)

    def body(*refs):
        a_ref, b_ref, o_ref = refs[0], refs[1], refs[n_in]
        part = _dot(a_ref[...], b_ref[...], dims)
        if gk == 1:
            o_ref[...] = part.astype(out_dtype)
            return
        acc_ref = refs[-1]
        k = pl.program_id(2)

        @pl.when(k == 0)
        def _():
            acc_ref[...] = part

        @pl.when(jnp.logical_and(k > 0, k < gk - 1))
        def _():
            acc_ref[...] += part

        @pl.when(k == gk - 1)
        def _():
            o_ref[...] = (acc_ref[...] + part).astype(out_dtype)

    a_tile = _nbytes((tm, tk), a.dtype)
    b_tile = _nbytes((tk, tn), b.dtype)
    o_tile = _nbytes((tm, tn), out_dtype)
    est = 2 * (a_tile + b_tile + o_tile) + 3 * tm * tn * 4 + (a_tile + b_tile)
    grid = (gn, gm, gk) if n_outer else (gm, gn, gk)
    operands, in_specs = [a, b], [a_spec, b_spec]
    if after is not None:
        operands.append(after)
        in_specs.append(pl.BlockSpec(memory_space=pl.ANY))
    aliases = {}
    if out_into is not None:
        aliases = {len(operands): 0}
        operands.append(out_into)
        in_specs.append(pl.BlockSpec(memory_space=pl.ANY))
    return pl.pallas_call(
        body, name=name, grid=grid, in_specs=in_specs, out_specs=o_spec, out_shape=out_shape,
        scratch_shapes=[] if gk == 1 else [pltpu.VMEM((tm, tn), F32)], input_output_aliases=aliases,
        compiler_params=_params(("parallel", "parallel", "arbitrary"), est),
    )(*operands)


def _row_spec(tr, width, col_block=0):
    return pl.BlockSpec((tr, width), lambda i: (i, col_block))


def _bcast_spec(width):
    return pl.BlockSpec((1, width), lambda i: (0, 0))


def _col_spec(width, tr):
    return pl.BlockSpec((width, tr), lambda i: (0, i))


def _rmsnorm_fwd(x, gain, name, tr, transposed=False):
    rows, width = x.shape

    def body(x_ref, g_ref, o_ref, *t_ref):
        xv = x_ref[...]
        r = lax.rsqrt(jnp.mean(xv * xv, axis=-1, keepdims=True) + RMS_EPS)
        h = xv * r * g_ref[...]
        o_ref[...] = h.astype(BF16)
        if transposed:
            t_ref[0][...] = h.T.astype(BF16)

    out_specs, out_shape = [_row_spec(tr, width)], [jax.ShapeDtypeStruct((rows, width), BF16)]
    if transposed:
        out_specs.append(_col_spec(width, tr))
        out_shape.append(jax.ShapeDtypeStruct((width, rows), BF16))
    out = pl.pallas_call(
        body, name=name, grid=(rows // tr,), in_specs=[_row_spec(tr, width), _bcast_spec(width)],
        out_specs=out_specs, out_shape=out_shape,
        compiler_params=_params(("parallel",), 10 * tr * width * 4),
    )(x, gain)
    return out if transposed else out[0]


def _residual_rmsnorm_fwd(x, merged, wout, gain, name, tr):
    rows, width = x.shape

    def body(x_ref, m_ref, w_ref, g_ref, x1_ref, h_ref, ht_ref):
        xv = x_ref[...] + _dot(m_ref[...], w_ref[...], _NN)
        x1_ref[...] = xv
        r = lax.rsqrt(jnp.mean(xv * xv, axis=-1, keepdims=True) + RMS_EPS)
        h = xv * r * g_ref[...]
        h_ref[...] = h.astype(BF16)
        ht_ref[...] = h.T.astype(BF16)

    return pl.pallas_call(
        body, name=name, grid=(rows // tr,),
        in_specs=[_row_spec(tr, width), _row_spec(tr, merged.shape[1]),
                  pl.BlockSpec(wout.shape, lambda i: (0, 0)), _bcast_spec(width)],
        out_specs=[_row_spec(tr, width), _row_spec(tr, width), _col_spec(width, tr)],
        out_shape=[jax.ShapeDtypeStruct((rows, width), F32), jax.ShapeDtypeStruct((rows, width), BF16),
                   jax.ShapeDtypeStruct((width, rows), BF16)],
        compiler_params=_params(("parallel",), 14 * tr * width * 4),
    )(x, merged, wout, gain)


def _matmul_rmsnorm_bwd(a, b, x, dres, gain, name, tm, tk, a_parts=False, b_stacked=False, after=None,
                        m_blocks=None, out_into=None, with_bf16=True):
    width = x.shape[1]
    m = a.shape[1] if a_parts else a.shape[0]
    kdim = a.shape[0] * a.shape[2] if a_parts else a.shape[1]
    gk = kdim // tk
    i0, gm = (0, m // tm) if m_blocks is None else m_blocks
    n_in = 5 + (after is not None) + (out_into is not None)

    def body(*refs):
        a_ref, b_ref, x_ref, dres_ref, g_ref = refs[:5]
        outs = refs[n_in:]
        dx_ref, dg_ref, acc_ref = outs[0], outs[-2], outs[-1]
        i, k = pl.program_id(0), pl.program_id(1)
        part = _dot(a_ref[...], b_ref[...], _NT)

        @pl.when(k == 0)
        def _():
            acc_ref[...] = part

        @pl.when(jnp.logical_and(k > 0, k < gk - 1))
        def _():
            acc_ref[...] += part

        @pl.when(k == gk - 1)
        def _():
            dhv = acc_ref[...] + part if gk > 1 else part
            xv = x_ref[...]
            r = lax.rsqrt(jnp.mean(xv * xv, axis=-1, keepdims=True) + RMS_EPS)
            xhat = xv * r
            dyg = dhv * g_ref[...]
            dx = dres_ref[...] + r * (dyg - xhat * jnp.mean(dyg * xhat, axis=-1, keepdims=True))
            dx_ref[...] = dx
            if with_bf16:
                outs[1][...] = dx.astype(BF16)
            gpart = jnp.sum(dhv * xhat, axis=0, keepdims=True)

            @pl.when(i == 0)
            def _():
                dg_ref[...] = gpart

            @pl.when(i > 0)
            def _():
                dg_ref[...] += gpart

    rows = lambda width_: pl.BlockSpec((tm, width_), lambda i, k: (i + i0, 0))
    if a_parts:
        a_spec = pl.BlockSpec((None, tm, tk), lambda i, k: (k, i + i0, 0))
    else:
        a_spec = pl.BlockSpec((tm, tk), lambda i, k: (i + i0, k))
    if b_stacked:
        b_spec = pl.BlockSpec((None, width, tk), lambda i, k: (k, 0, 0))
    else:
        b_spec = pl.BlockSpec((width, tk), lambda i, k: (0, k))
    operands = [a, b, x, dres, gain]
    in_specs = [a_spec, b_spec, rows(width), rows(width), pl.BlockSpec((1, width), lambda i, k: (0, 0))]
    for extra in (after, out_into):
        if extra is not None:
            operands.append(extra)
            in_specs.append(pl.BlockSpec(memory_space=pl.ANY))
    aliases = {} if out_into is None else {len(operands) - 1: 0}
    out_specs = [rows(width)] + ([rows(width)] if with_bf16 else []) + [pl.BlockSpec((1, width), lambda i, k: (0, 0))]
    out_shape = [jax.ShapeDtypeStruct((m, width), F32)] + ([jax.ShapeDtypeStruct((m, width), BF16)] if with_bf16 else [])
    out_shape.append(jax.ShapeDtypeStruct((1, width), F32))
    est = 4 * tm * tk + 4 * width * tk + 12 * tm * width * 4
    return pl.pallas_call(
        body, name=name, grid=(gm, gk), in_specs=in_specs, out_specs=out_specs, out_shape=out_shape,
        scratch_shapes=[pltpu.VMEM((tm, width), F32)], input_output_aliases=aliases,
        compiler_params=_params(("arbitrary", "arbitrary"), est),
    )(*operands)


def _gain_grad(x, dh, name):
    rows, width = x.shape

    def body(x_ref, dh_ref, dg_ref):
        xv = x_ref[...]
        r = lax.rsqrt(jnp.mean(xv * xv, axis=-1, keepdims=True) + RMS_EPS)
        dg_ref[...] = jnp.sum(dh_ref[...] * xv * r, axis=0, keepdims=True)

    return pl.pallas_call(
        body, name=name, grid=(1,), in_specs=[_row_spec(rows, width), _row_spec(rows, width)],
        out_specs=_bcast_spec(width), out_shape=jax.ShapeDtypeStruct((1, width), F32),
        compiler_params=_params(("arbitrary",), 6 * rows * width * 4),
    )(x, dh)


def _lb_table(logits, name):
    slots, width = logits.shape

    def body(l_ref, o_ref):
        lv = l_ref[...]
        mx = jnp.max(lv, axis=0, keepdims=True)
        e = jnp.exp(lv - mx)
        o_ref[...] = e[0:1, :] / jnp.sum(e, axis=0, keepdims=True)

    return pl.pallas_call(
        body, name=name, grid=(1,), in_specs=[pl.BlockSpec((slots, width), lambda i: (0, 0))],
        out_specs=_bcast_spec(width), out_shape=jax.ShapeDtypeStruct((1, width), F32),
    )(logits)


def _branch_merge_fwd(proj, outs, weights, tr):
    w = D_MODEL

    def body(ghg_ref, gda_ref, gmem_ref, ohg_ref, oda_ref, omem_ref, whg_ref, wda_ref, wmem_ref,
             m_ref, thg_ref, tda_ref, tmem_ref):
        acc = None
        for g_ref, o_ref, w_ref, t_ref in ((ghg_ref, ohg_ref, whg_ref, thg_ref), (gda_ref, oda_ref, wda_ref, tda_ref),
                                           (gmem_ref, omem_ref, wmem_ref, tmem_ref)):
            t = _dot(o_ref[...], w_ref[...], _NN)
            t_ref[...] = t.astype(BF16)
            term = _sigmoid(g_ref[...]) * t
            acc = term if acc is None else acc + term
        m_ref[...] = acc.astype(BF16)

    whole = lambda a: pl.BlockSpec(a.shape, lambda i: (0, 0))
    shape = jax.ShapeDtypeStruct((SEQ, w), BF16)
    return pl.pallas_call(
        body, name="branch_merge_fwd", grid=(SEQ // tr,),
        in_specs=[_row_spec(tr, w, 10), _row_spec(tr, w, 11), _row_spec(tr, w, 12)]
        + [_row_spec(tr, o.shape[1]) for o in outs] + [whole(wt) for wt in weights],
        out_specs=[_row_spec(tr, w)] * 4, out_shape=[shape] * 4,
        compiler_params=_params(("parallel",), 20 * tr * w * 4),
    )(proj, proj, proj, *outs, *weights)


def _dproj_buffer():
    return lax.empty((SEQ, IN_COLS), BF16)


def _gate_merge_bwd(proj, ts, weights, dmerged, dproj, tr):
    w = D_MODEL
    steps = SEQ // tr
    gate_col0 = 10 * w

    def body(ghg_ref, gda_ref, gmem_ref, thg_ref, tda_ref, tmem_ref, whg_ref, wda_ref, wmem_ref, dm_ref, dproj_in,
             dthg_ref, dtda_ref, dtmem_ref, dohg_ref, doda_ref, domem_ref, dproj_ref, stage, sems):
        del dproj_in
        i = pl.program_id(0)
        slot = i % 2

        def slot_copy(s):
            rows = pl.ds(pl.multiple_of(i * tr, tr), tr)
            return pltpu.make_async_copy(stage.at[s], dproj_ref.at[rows, pl.ds(gate_col0, 3 * w)], sems.at[s])

        @pl.when(i >= 2)
        def _():
            slot_copy(slot).wait()

        dm = dm_ref[...]
        branches = ((ghg_ref, thg_ref, whg_ref, dthg_ref, dohg_ref), (gda_ref, tda_ref, wda_ref, dtda_ref, doda_ref),
                    (gmem_ref, tmem_ref, wmem_ref, dtmem_ref, domem_ref))
        for b, (g_ref, t_ref, w_ref, dt_ref, do_ref) in enumerate(branches):
            s = _sigmoid(g_ref[...])
            dt = (s * dm).astype(BF16)
            dt_ref[...] = dt
            do_ref[...] = _dot(dt, w_ref[...], _NT)
            stage[slot, :, b * w:(b + 1) * w] = (dm * t_ref[...].astype(F32) * s * (1.0 - s)).astype(BF16)
        slot_copy(slot).start()

        @pl.when(i == steps - 1)
        def _():
            slot_copy(1 - slot).wait()
            slot_copy(slot).wait()

    assert steps >= 2
    whole = lambda a: pl.BlockSpec(a.shape, lambda i: (0, 0))
    return pl.pallas_call(
        body, name="gate_merge_bwd", grid=(steps,),
        in_specs=[_row_spec(tr, w, 10), _row_spec(tr, w, 11), _row_spec(tr, w, 12)] + [_row_spec(tr, w)] * 3
        + [whole(wt) for wt in weights] + [_row_spec(tr, w), _any_spec()],
        out_specs=[_row_spec(tr, w)] * 3 + [_row_spec(tr, wt.shape[0]) for wt in weights] + [_any_spec()],
        out_shape=[jax.ShapeDtypeStruct((SEQ, w), BF16)] * 3
        + [jax.ShapeDtypeStruct((SEQ, wt.shape[0]), F32) for wt in weights]
        + [jax.ShapeDtypeStruct((SEQ, IN_COLS), BF16)],
        scratch_shapes=[pltpu.VMEM((2, tr, 3 * w), BF16), pltpu.SemaphoreType.DMA((2,))],
        input_output_aliases={10: 6},
        compiler_params=_params(("arbitrary",), 34 * tr * w * 4),
    )(proj, proj, proj, *ts, *weights, dmerged, dproj)


def _ffn_in_swiglu(h2, wfin, tm, tn):
    gm, gn = SEQ // tm, D_FF // tn

    def body(h_ref, wa_ref, wb_ref, a_ref, b_ref, act_ref, actt_ref):
        h = h_ref[...]
        a = _dot(h, wa_ref[...], _NN)
        b = _dot(h, wb_ref[...], _NN)
        act = a * _sigmoid(a) * b
        a_ref[...] = a.astype(BF16)
        b_ref[...] = b.astype(BF16)
        act_ref[...] = act.astype(BF16)
        actt_ref[...] = act.T.astype(BF16)

    tile = pl.BlockSpec((tm, tn), lambda j, i: (i, j))
    shape = jax.ShapeDtypeStruct((SEQ, D_FF), BF16)
    return pl.pallas_call(
        body, name="ffn_in_swiglu_fwd", grid=(gn, gm),
        in_specs=[pl.BlockSpec((tm, D_MODEL), lambda j, i: (i, 0)),
                  pl.BlockSpec((D_MODEL, tn), lambda j, i: (0, j)),
                  pl.BlockSpec((D_MODEL, tn), lambda j, i: (0, gn + j))],
        out_specs=[tile, tile, tile, pl.BlockSpec((tn, tm), lambda j, i: (j, i))],
        out_shape=[shape, shape, shape, jax.ShapeDtypeStruct((D_FF, SEQ), BF16)],
        compiler_params=_params(("parallel", "parallel"), 4 * tm * D_MODEL + 8 * D_MODEL * tn + 16 * tm * tn
                                + 6 * tm * tn * 4),
    )(h2, wfin, wfin)


def _ffn_out_bwd_swiglu(dyb, wfout, a, b, tm, tn):
    gm, gn = SEQ // tm, D_FF // tn

    def body(dy_ref, w_ref, a_ref, b_ref, o_ref):
        d = _dot(dy_ref[...], w_ref[...], _NT)
        av = a_ref[...].astype(F32)
        bv = b_ref[...].astype(F32)
        s = _sigmoid(av)
        silu = av * s
        o_ref[0] = (d * bv * (s + silu * (1.0 - s))).astype(BF16)
        o_ref[1] = (d * silu).astype(BF16)

    tile = pl.BlockSpec((tm, tn), lambda i, j: (i, j))
    return pl.pallas_call(
        body, name="ffn_out_bwd_swiglu", grid=(gm, gn),
        in_specs=[pl.BlockSpec((tm, D_MODEL), lambda i, j: (i, 0)), pl.BlockSpec((tn, D_MODEL), lambda i, j: (j, 0)),
                  tile, tile],
        out_specs=pl.BlockSpec((2, tm, tn), lambda i, j: (0, i, j)),
        out_shape=jax.ShapeDtypeStruct((2, SEQ, D_FF), BF16),
        compiler_params=_params(("parallel", "parallel"), 4 * tm * D_MODEL + 4 * tn * D_MODEL + 16 * tm * tn
                                + 6 * tm * tn * 4),
    )(dyb, wfout, a, b)


def _ffn_out_loss_head(x1, act, wfout, target, tr):
    w = D_MODEL

    def body(x_ref, a_ref, w_ref, t_ref, dy_ref, dyb_ref, loss_ref, acc_ref):
        err = x_ref[...] + _dot(a_ref[...], w_ref[...], _NN) - t_ref[...]
        dy = err * (1.0 / w)
        dy_ref[...] = dy
        dyb_ref[...] = dy.astype(BF16)
        part = jnp.sum(err * err, axis=0, keepdims=True)

        @pl.when(pl.program_id(0) == 0)
        def _():
            acc_ref[...] = part

        @pl.when(pl.program_id(0) > 0)
        def _():
            acc_ref[...] += part

        @pl.when(pl.program_id(0) == SEQ // tr - 1)
        def _():
            total = jnp.sum(acc_ref[...], axis=1, keepdims=True) * (0.5 / w)
            loss_ref[...] = jnp.broadcast_to(total, (1, LANE))

    return pl.pallas_call(
        body, name="ffn_out_loss_head", grid=(SEQ // tr,),
        in_specs=[_row_spec(tr, w), _row_spec(tr, D_FF), pl.BlockSpec((D_FF, w), lambda i: (0, 0)), _row_spec(tr, w)],
        out_specs=[_row_spec(tr, w), _row_spec(tr, w), _bcast_spec(LANE)],
        out_shape=[jax.ShapeDtypeStruct((SEQ, w), F32), jax.ShapeDtypeStruct((SEQ, w), BF16),
                   jax.ShapeDtypeStruct((1, LANE), F32)],
        scratch_shapes=[pltpu.VMEM((1, w), F32)],
        compiler_params=_params(("arbitrary",), 12 * tr * w * 4 + 4 * D_FF * w),
    )(x1, act, wfout, target)


GLA_ROWS = 256
GLA_CPB = GLA_ROWS // HG_CHUNK
GLA_NBLK = SEQ // GLA_ROWS
GLA_WAYS = 4
GLA_TRIPS = GLA_NBLK // GLA_WAYS
GLA_GRAD_WAYS = 4
GLA_GRAD_TRIPS = GLA_NBLK // GLA_GRAD_WAYS
GLA_NCK = SEQ // HG_CHUNK
GLA_INTER_WAYS = 8


def _dot_split(m, xv, dims, terms=3):
    dot = lambda t: lax.dot_general(m, t, dims, preferred_element_type=F32)
    hi = xv.astype(BF16)
    r1 = xv - hi.astype(F32)
    mid = r1.astype(BF16)
    if terms == 2:
        return dot(mid) + dot(hi)
    lo = (r1 - mid.astype(F32)).astype(BF16)
    return (dot(lo) + dot(mid)) + dot(hi)


def _gla_block(qc, fc, lbv, masks):
    mask, maskb, direction = masks
    sq = _sigmoid(qc)
    q = qc * sq * HG_SCALE
    sf = _sigmoid(fc)
    forget = lbv + (1.0 - lbv) * sf
    k = 1.0 - forget
    logf = jnp.log(forget)
    b = _dot_split(maskb, logf, _NN)
    ends = []
    for j in range(GLA_CPB):
        lo, hi = j * HG_CHUNK, (j + 1) * HG_CHUNK
        end = jnp.where(direction == 0, b[hi - 1:hi, :], b[lo:lo + 1, :])
        ends.append(jnp.broadcast_to(end, (HG_CHUNK, HEAD_DIM)))
    bt = jnp.concatenate(ends, axis=0)
    eb = jnp.exp(b)
    qt = q * eb
    kt = k * jnp.exp(-b)
    kh = k * jnp.exp(bt - b)
    a = jnp.where(mask, _dot(qt, kt, _NT), 0.0)
    return dict(sq=sq, sf=sf, forget=forget, k=k, b=b, bt=bt, eb=eb, qt=qt, kt=kt, kh=kh, a=a)


def _gla_masks(direction):
    row = lax.broadcasted_iota(jnp.int32, (GLA_ROWS, GLA_ROWS), 0)
    col = lax.broadcasted_iota(jnp.int32, (GLA_ROWS, GLA_ROWS), 1)
    same = (row // HG_CHUNK) == (col // HG_CHUNK)
    mask = jnp.logical_and(same, jnp.where(direction == 0, row - col, col - row) >= 0)
    return mask, jnp.where(mask, 1.0, 0.0).astype(BF16), direction


def _gla_chunk_rows(j):
    return slice(j * HG_CHUNK, (j + 1) * HG_CHUNK)


def _gla_block_rows(b):
    return pl.ds(pl.multiple_of(b * GLA_ROWS, GLA_ROWS), GLA_ROWS)


def _head_spec(col_block0):
    return pl.BlockSpec((SEQ, HEAD_DIM), lambda h, d: (0, col_block0 + h))


def _gla_fwd(proj, lb, gain):
    nck = GLA_NCK

    def body(q_ref, f_ref, v_ref, g_ref, lb_ref, gain_ref, ohg_ref, opre_ref, st_ref, qt_scr, dec_scr, cs_scr):
        d = pl.program_id(1)
        masks = _gla_masks(d)
        lbv = lb_ref[...]

        @pl.when(d == 0)
        def _():
            opre_ref[...] = jnp.zeros_like(opre_ref)

        def intra(s, carry):
            blocks = [s + w * GLA_TRIPS for w in range(GLA_WAYS)]
            rows = [_gla_block_rows(b) for b in blocks]
            loaded = [(q_ref[r, :], f_ref[r, :], v_ref[r, :], opre_ref[r, :]) for r in rows]
            results = []
            for qc, fc, v, o_prev in loaded:
                ck = _gla_block(qc, fc, lbv, masks)
                o = o_prev + _dot(ck["a"], v, _NN)
                cs = [_dot(v[_gla_chunk_rows(j), :], ck["kh"][_gla_chunk_rows(j), :], _TN) for j in range(GLA_CPB)]
                dec = [jnp.exp(ck["bt"][j * HG_CHUNK:j * HG_CHUNK + 8, :]) for j in range(GLA_CPB)]
                results.append((o, ck["qt"].astype(BF16), cs, dec))
            for b, r, (o, qt, cs, dec) in zip(blocks, rows, results):
                opre_ref[r, :] = o
                qt_scr[r, :] = qt
                for j in range(GLA_CPB):
                    cs_scr[b * GLA_CPB + j] = cs[j]
                    dec_scr[b * GLA_CPB + j] = dec[j]
            return carry

        lax.fori_loop(0, GLA_TRIPS, intra, 0)

        def scan(i, st):
            c = jnp.where(d == 0, i, nck - 1 - i)
            st_ref[c] = st
            return st * dec_scr[c][0:1, :] + cs_scr[c]

        lax.fori_loop(0, nck, scan, jnp.zeros((HEAD_DIM, HEAD_DIM), F32), unroll=4)

        def inter(t, carry):
            chunks = [t + w * (nck // GLA_INTER_WAYS) for w in range(GLA_INTER_WAYS)]
            rows = [pl.ds(pl.multiple_of(c * HG_CHUNK, HG_CHUNK), HG_CHUNK) for c in chunks]
            loaded = [(opre_ref[r, :], qt_scr[r, :], st_ref[c]) for r, c in zip(rows, chunks)]
            for r, o in zip(rows, [o_prev + _dot(qt, st, _NT) for o_prev, qt, st in loaded]):
                opre_ref[r, :] = o
            return carry

        lax.fori_loop(0, nck // GLA_INTER_WAYS, inter, 0)

        @pl.when(d == 1)
        def _():
            o = opre_ref[...]
            r = lax.rsqrt(jnp.mean(o * o, axis=-1, keepdims=True) + RMS_EPS)
            g = g_ref[...]
            ohg_ref[...] = (o * r * gain_ref[...] * (g * _sigmoid(g))).astype(BF16)

    blk = SEQ * HEAD_DIM * 4
    return pl.pallas_call(
        body, name="gla_fwd", grid=(HG_HEADS, 2),
        in_specs=[_head_spec(CB_HG_Q),
                  pl.BlockSpec((SEQ, HEAD_DIM), lambda h, d: (0, CB_F + 8 * d + h)),
                  _head_spec(CB_HG_I), _head_spec(CB_HG_G),
                  pl.BlockSpec((None, None, 1, HEAD_DIM), lambda h, d: (d, h, 0, 0)),
                  pl.BlockSpec((1, HEAD_DIM), lambda h, d: (0, 0))],
        out_specs=[_head_spec(0), _head_spec(0),
                   pl.BlockSpec((None, None, nck, HEAD_DIM, HEAD_DIM), lambda h, d: (h, d, 0, 0, 0)),
                   pl.BlockSpec((None, SEQ, HEAD_DIM), lambda h, d: (d, 0, h)),
                   pl.BlockSpec((None, None, nck, 8, HEAD_DIM), lambda h, d: (h, d, 0, 0, 0))],
        out_shape=[jax.ShapeDtypeStruct((SEQ, D_MODEL), BF16), jax.ShapeDtypeStruct((SEQ, D_MODEL), F32),
                   jax.ShapeDtypeStruct((HG_HEADS, 2, nck, HEAD_DIM, HEAD_DIM), F32),
                   jax.ShapeDtypeStruct((2, SEQ, D_MODEL), BF16),
                   jax.ShapeDtypeStruct((HG_HEADS, 2, nck, 8, HEAD_DIM), F32)],
        scratch_shapes=[pltpu.VMEM((nck, HEAD_DIM, HEAD_DIM), F32)],
        compiler_params=_params(("parallel", "arbitrary"), 8 * blk + 3 * blk + 2 * blk + 2 * blk + blk),
    )(proj, proj, proj, proj, lb, gain)


def _gla_bwd(proj, lb, gain, o_pre, states, qt_all, dec_all, do_hg, dproj):
    nck = GLA_NCK
    do, dproj, dgain = _gla_bwd_norm(proj, gain, o_pre, do_hg, dproj, 256)

    def body(q_ref, f_ref, v_ref, lb_ref, do_ref, st_ref, qt_ref, dec_scr, dproj_in, dproj_ref, dlb_ref,
             dq_acc, dv_acc, dst_scr, cs_scr, df_out, dq_out, dv_out, sems):
        del dproj_in
        h, d = pl.program_id(0), pl.program_id(1)
        masks = _gla_masks(d)
        mask, maskb, _ = masks
        lbv = lb_ref[...]

        def column_copy(k, staging, col_block):
            cols = pl.ds(pl.multiple_of(col_block * LANE, LANE), LANE)
            return pltpu.make_async_copy(staging, dproj_ref.at[:, cols], sems.at[k])

        df_copy = column_copy(0, df_out, CB_F + 8 * d + h)
        dq_copy = column_copy(1, dq_out, CB_HG_Q + h)
        dv_copy = column_copy(2, dv_out, CB_HG_I + h)
        last = jnp.logical_and(h == HG_HEADS - 1, d == 1)

        def intra(s, carry):
            blocks = [s + w * GLA_TRIPS for w in range(GLA_WAYS)]
            loaded = [(qt_ref[r, :], do_ref[r, :]) for r in map(_gla_block_rows, blocks)]
            results = [[_dot(doc[_gla_chunk_rows(j), :], qt[_gla_chunk_rows(j), :], _TN) for j in range(GLA_CPB)]
                       for qt, doc in loaded]
            for b, cs in zip(blocks, results):
                for j in range(GLA_CPB):
                    cs_scr[b * GLA_CPB + j] = cs[j]
            return carry

        lax.fori_loop(0, GLA_TRIPS, intra, 0)

        def scan(i, dst):
            c = jnp.where(d == 0, nck - 1 - i, i)
            dst_scr[c] = dst
            return dst * dec_scr[c][0:1, :] + cs_scr[c]

        lax.fori_loop(0, nck, scan, jnp.zeros((HEAD_DIM, HEAD_DIM), F32), unroll=4)

        @pl.when(d == 0)
        def _():
            dq_acc[...] = jnp.zeros_like(dq_acc)
            dv_acc[...] = jnp.zeros_like(dv_acc)

        def block_grads(qc, fc, v, doc, states_in, dstates, decays):
            ck = _gla_block(qc, fc, lbv, masks)
            qt, kt, kh, a = ck["qt"], ck["kt"], ck["kh"], ck["a"]
            da = jnp.where(mask, _dot(doc, v, _NT), 0.0)
            dqt_i = _dot(da, kt, _NN)
            dkt = _dot(da, qt, _TN)
            dv_i = _dot(a, doc, _TN)
            dqt_p, dv_p, dkh_p, dbt_p = [], [], [], []
            for j in range(GLA_CPB):
                cr = _gla_chunk_rows(j)
                st_in, dst = states_in[j], dstates[j]
                dqt_p.append(dqt_i[cr, :] + _dot(doc[cr, :], st_in, _NN))
                dv_p.append(dv_i[cr, :] + _dot(kh[cr, :], dst, _NT))
                dkh_j = _dot(v[cr, :], dst, _NN)
                dkh_p.append(dkh_j)
                dbt_j = (decays[j][0:1, :] * jnp.sum(dst * st_in, axis=0, keepdims=True)
                         + jnp.sum(dkh_j * kh[cr, :], axis=0, keepdims=True))
                dbt_p.append(jnp.broadcast_to(dbt_j, (HG_CHUNK, HEAD_DIM)))
            dqt = jnp.concatenate(dqt_p, axis=0)
            dv = jnp.concatenate(dv_p, axis=0)
            dkh = jnp.concatenate(dkh_p, axis=0)
            dbt = jnp.concatenate(dbt_p, axis=0)
            db = dqt * qt - dkt * kt - dkh * kh
            dq = dqt * ck["eb"]
            dk = dkt * jnp.exp(-ck["b"]) + dkh * jnp.exp(ck["bt"] - ck["b"])
            dlogf = _dot_split(maskb, db, _TN, terms=2) + dbt
            dforget = dlogf / ck["forget"] - dk
            sf, sq = ck["sf"], ck["sq"]
            df = (dforget * (1.0 - lbv) * sf * (1.0 - sf)).astype(BF16)
            dqc = dq * HG_SCALE * (sq + qc * sq * (1.0 - sq))
            return df, dqc, dv, jnp.sum(dforget * (1.0 - sf), axis=0, keepdims=True)

        def grads(s, dlb):
            blocks = [s + w * GLA_GRAD_TRIPS for w in range(GLA_GRAD_WAYS)]
            rows = [_gla_block_rows(b) for b in blocks]
            loaded = []
            for b, r in zip(blocks, rows):
                chunks = [b * GLA_CPB + j for j in range(GLA_CPB)]
                loaded.append((q_ref[r, :], f_ref[r, :], v_ref[r, :], do_ref[r, :], [st_ref[c] for c in chunks],
                               [dst_scr[c] for c in chunks], [dec_scr[c] for c in chunks], dq_acc[r, :], dv_acc[r, :]))
            results = [block_grads(*t[:7]) + (t[7], t[8]) for t in loaded]
            for r, (df, dqc, dv, dlb_part, dq_prev, dv_prev) in zip(rows, results):
                df_out[r, :] = df
                dq_acc[r, :] = dq_prev + dqc
                dv_acc[r, :] = dv_prev + dv
                dlb = dlb + dlb_part
            return dlb

        @pl.when(jnp.logical_or(h > 0, d > 0))
        def _():
            df_copy.wait()

        dlb_ref[...] = lax.fori_loop(0, GLA_GRAD_TRIPS, grads, jnp.zeros((1, HEAD_DIM), F32))
        df_copy.start()

        @pl.when(d == 1)
        def _():
            @pl.when(h > 0)
            def _():
                dq_copy.wait()
                dv_copy.wait()

            dq_out[...] = dq_acc[...].astype(BF16)
            dv_out[...] = dv_acc[...].astype(BF16)
            dq_copy.start()
            dv_copy.start()

        @pl.when(last)
        def _():
            df_copy.wait()
            dq_copy.wait()
            dv_copy.wait()

    blk = SEQ * HEAD_DIM * 4
    state_bytes = nck * HEAD_DIM * HEAD_DIM * 4
    dproj, dlb = pl.pallas_call(
        body, name="gla_bwd", grid=(HG_HEADS, 2),
        in_specs=[_head_spec(CB_HG_Q),
                  pl.BlockSpec((SEQ, HEAD_DIM), lambda h, d: (0, CB_F + 8 * d + h)),
                  _head_spec(CB_HG_I),
                  pl.BlockSpec((None, None, 1, HEAD_DIM), lambda h, d: (d, h, 0, 0)),
                  _head_spec(0),
                  pl.BlockSpec((None, None, nck, HEAD_DIM, HEAD_DIM), lambda h, d: (h, d, 0, 0, 0)),
                  pl.BlockSpec((None, SEQ, HEAD_DIM), lambda h, d: (d, 0, h)),
                  pl.BlockSpec((None, None, nck, 8, HEAD_DIM), lambda h, d: (h, d, 0, 0, 0)),
                  _any_spec()],
        out_specs=[_any_spec(), pl.BlockSpec((None, None, 1, HEAD_DIM), lambda h, d: (d, h, 0, 0))],
        out_shape=[jax.ShapeDtypeStruct((SEQ, IN_COLS), BF16), jax.ShapeDtypeStruct((2, HG_HEADS, 1, HEAD_DIM), F32)],
        scratch_shapes=[pltpu.VMEM((SEQ, HEAD_DIM), F32)] * 2
        + [pltpu.VMEM((nck, HEAD_DIM, HEAD_DIM), F32)] * 2
        + [pltpu.VMEM((SEQ, HEAD_DIM), BF16)] * 3 + [pltpu.SemaphoreType.DMA((3,))],
        input_output_aliases={8: 0},
        compiler_params=_params(("arbitrary", "arbitrary"), 8 * blk + 4 * state_bytes + 3 * blk + 3 * blk),
    )(proj, proj, proj, lb, do, states, qt_all, dec_all, dproj)
    return dproj, dlb, dgain


def _gla_bwd_norm(proj, gain, o_pre, do_hg, dproj, tr):
    w = D_MODEL

    def body(g_ref, gain_ref, opre_ref, dohg_ref, dproj_in, do_ref, dg_ref, dgain_ref):
        del dproj_in
        gainv = gain_ref[...]
        part = jnp.zeros((1, HEAD_DIM), F32)
        for h in range(HG_HEADS):
            hs = slice(h * HEAD_DIM, (h + 1) * HEAD_DIM)
            o = opre_ref[:, hs]
            r = lax.rsqrt(jnp.mean(o * o, axis=-1, keepdims=True) + RMS_EPS)
            ohat = o * r
            g = g_ref[:, hs]
            sg = _sigmoid(g)
            silu = g * sg
            dout = dohg_ref[:, hs]
            dg_ref[:, hs] = (dout * ohat * gainv * (sg + silu * (1.0 - sg))).astype(BF16)
            dy = dout * silu
            part = part + jnp.sum(dy * ohat, axis=0, keepdims=True)
            dn = dy * gainv
            do_ref[:, hs] = r * (dn - ohat * jnp.mean(dn * ohat, axis=-1, keepdims=True))

        @pl.when(pl.program_id(0) == 0)
        def _():
            dgain_ref[...] = part

        @pl.when(pl.program_id(0) > 0)
        def _():
            dgain_ref[...] += part

    return pl.pallas_call(
        body, name="gla_bwd_norm", grid=(SEQ // tr,),
        in_specs=[_row_spec(tr, w, CB_HG_G * LANE // w), _bcast_spec(HEAD_DIM), _row_spec(tr, w), _row_spec(tr, w),
                  _any_spec()],
        out_specs=[_row_spec(tr, w), _row_spec(tr, w, CB_HG_G * LANE // w), _bcast_spec(HEAD_DIM)],
        out_shape=[jax.ShapeDtypeStruct((SEQ, w), F32), jax.ShapeDtypeStruct((SEQ, IN_COLS), BF16),
                   jax.ShapeDtypeStruct((1, HEAD_DIM), F32)],
        input_output_aliases={4: 1},
        compiler_params=_params(("arbitrary",), 16 * tr * w * 4),
    )(proj, gain, o_pre, do_hg, dproj)


def _da_residue_rows(r, n0, size, d):
    if d == 1:
        return pl.ds(pl.multiple_of(n0, 8), size)
    return pl.ds(r + n0 * d, size, stride=d)


def _da_residue_ways(d, nqb):
    return min(d, 4) if nqb <= 2 else 1


def _da_rmsnorm(x, gain):
    r = lax.rsqrt(jnp.mean(x * x, axis=-1, keepdims=True) + RMS_EPS)
    return x * r, r, x * r * gain


def _da_scores(qn_scr, kn_scr, slope, i, ld):
    w0 = jnp.clip(i * DA_QB - DA_RADIUS, 0, ld - DA_WIN)
    w0 = pl.multiple_of(w0, DA_RADIUS)
    qrows = pl.ds(pl.multiple_of(i * DA_QB, DA_QB), DA_QB)
    win = pl.ds(w0, DA_WIN)
    qb = qn_scr[qrows, :]
    kw = kn_scr[win, :]
    s = _dot(qb, kw, _NT) * DA_SCALE
    qpos = i * DA_QB + lax.broadcasted_iota(jnp.int32, (DA_QB, DA_WIN), 0)
    kpos = w0 + lax.broadcasted_iota(jnp.int32, (DA_QB, DA_WIN), 1)
    arel = jnp.abs(kpos - qpos)
    s = s - slope * arel.astype(F32)
    s = jnp.where(arel <= DA_RADIUS, s, NEG_INF)
    return s, qb, kw, qrows, win


def _da_slopes(group):
    d = DA_DILATIONS[group]
    sl = _alibi_slopes(DA_HEADS)[4 * group:4 * group + 4] * d
    return jnp.asarray(np.broadcast_to(sl[:, None, None], (4, 1, LANE)).copy())


def _da_fwd(proj, gq, gk, group):
    d = DA_DILATIONS[group]
    ld = SEQ // d
    nqb = ld // DA_QB

    ways = min(DA_FWD_WAYS, nqb)
    rways = _da_residue_ways(d, nqb)

    def body(q_ref, k_ref, v_ref, gq_ref, gk_ref, sl_ref, o_ref, lse_ref, qn_scr, kn_scr, v_scr):
        slope = sl_ref[:, 0:1]

        def residues(t, carry):
            rs = [t * rways + u for u in range(rways)]
            for u, r in enumerate(rs):
                sel = _da_residue_rows(r, 0, ld, d)
                qn_scr[u] = _da_rmsnorm(q_ref[sel, :], gq_ref[...])[2].astype(BF16)
                kn_scr[u] = _da_rmsnorm(k_ref[sel, :], gk_ref[...])[2].astype(BF16)
                v_scr[u] = v_ref[sel, :].astype(BF16)

            def block(u, i):
                s, _, _, _, win = _da_scores(qn_scr.at[u], kn_scr.at[u], slope, i, ld)
                m = jnp.max(s, axis=-1, keepdims=True)
                p = jnp.exp(s - m)
                l = jnp.sum(p, axis=-1, keepdims=True)
                return _dot(p, v_scr[u, win, :], _NN) / l, jnp.broadcast_to(m + jnp.log(l), (DA_QB, HEAD_DIM))

            def step(i, c2):
                todo = [(u, r, i + w * (nqb // ways)) for u, r in enumerate(rs) for w in range(ways)]
                for (u, r, b), (o, lse) in zip(todo, [block(u, b) for u, _, b in todo]):
                    out = _da_residue_rows(r, b * DA_QB, DA_QB, d)
                    o_ref[out, :] = o
                    lse_ref[out, :] = lse
                return c2

            return lax.fori_loop(0, nqb // ways, step, carry)

        lax.fori_loop(0, d // rways, residues, 0)

    seq_spec = lambda base: pl.BlockSpec((SEQ, HEAD_DIM), lambda h: (0, base + 4 * group + h))
    out_spec = pl.BlockSpec((SEQ, HEAD_DIM), lambda h: (0, h))
    gain_spec = pl.BlockSpec((1, HEAD_DIM), lambda h: (0, 0))
    blk = SEQ * HEAD_DIM * 4
    return pl.pallas_call(
        body, name=f"da_fwd_g{group}", grid=(DA_HEADS_PER_GROUP,),
        in_specs=[seq_spec(CB_DA_Q), seq_spec(CB_DA_K), seq_spec(CB_DA_V), gain_spec, gain_spec,
                  pl.BlockSpec((None, 1, LANE), lambda h: (h, 0, 0))],
        out_specs=[out_spec, out_spec],
        out_shape=[jax.ShapeDtypeStruct((SEQ, DA_WIDTH), F32)] * 2,
        scratch_shapes=[pltpu.VMEM((rways, ld, HEAD_DIM), BF16)] * 3,
        compiler_params=_params(("parallel",), 10 * blk + 2 * blk),
    )(proj, proj, proj, gq, gk, _da_slopes(group))


def _da_merge(os, lses, tr):
    w = DA_WIDTH

    def body(o0, o1, o2, l0, l1, l2, ob_ref, of_ref, lse_ref):
        la, lb_, lc = l0[...], l1[...], l2[...]
        m = jnp.maximum(jnp.maximum(la, lb_), lc)
        ea, eb, ec = jnp.exp(la - m), jnp.exp(lb_ - m), jnp.exp(lc - m)
        tot = ea + eb + ec
        o = (ea * o0[...] + eb * o1[...] + ec * o2[...]) / tot
        of_ref[...] = o
        ob_ref[...] = o.astype(BF16)
        lse_ref[...] = m + jnp.log(tot)

    return pl.pallas_call(
        body, name="da_merge", grid=(SEQ // tr,), in_specs=[_row_spec(tr, w)] * 6,
        out_specs=[_row_spec(tr, w)] * 3,
        out_shape=[jax.ShapeDtypeStruct((SEQ, w), BF16), jax.ShapeDtypeStruct((SEQ, w), F32),
                   jax.ShapeDtypeStruct((SEQ, w), F32)],
        compiler_params=_params(("parallel",), 24 * tr * w * 4),
    )(*os, *lses)


def _da_rowdot(do, o, tr):
    w = DA_WIDTH

    def body(do_ref, o_ref, out_ref):
        prod = do_ref[...] * o_ref[...]
        for h in range(w // HEAD_DIM):
            sl = slice(h * HEAD_DIM, (h + 1) * HEAD_DIM)
            out_ref[:, sl] = jnp.broadcast_to(jnp.sum(prod[:, sl], axis=-1, keepdims=True), (tr, HEAD_DIM))

    return pl.pallas_call(
        body, name="da_rowdot", grid=(SEQ // tr,), in_specs=[_row_spec(tr, w)] * 2,
        out_specs=_row_spec(tr, w), out_shape=jax.ShapeDtypeStruct((SEQ, w), F32),
        compiler_params=_params(("parallel",), 10 * tr * w * 4),
    )(do, o)


def _da_bwd(proj, gq, gk, do, lse, dd, dproj, group):
    d = DA_DILATIONS[group]
    ld = SEQ // d
    nqb = ld // DA_QB
    ways = min(DA_WAYS, nqb)
    rways = _da_residue_ways(d, nqb)

    def body(q_ref, k_ref, v_ref, gq_ref, gk_ref, sl_ref, do_ref, lse_ref, dd_ref, dproj_in,
             dproj_ref, dgq_ref, dgk_ref,
             qn_scr, kn_scr, v_scr, dqn_scr, dkn_scr, dvr_scr, dq_scr, dk_scr, dv_scr, dq_out, dk_out, dv_out, sems):
        del dproj_in
        head = pl.program_id(0)
        gqv, gkv = gq_ref[...], gk_ref[...]
        slope = sl_ref[:, 0:1]

        copies = []
        for k, (staging, base) in enumerate(((dq_out, CB_DA_Q), (dk_out, CB_DA_K), (dv_out, CB_DA_V))):
            cols = pl.ds(pl.multiple_of((base + 4 * group + head) * LANE, LANE), LANE)
            copies.append(pltpu.make_async_copy(staging, dproj_ref.at[:, cols], sems.at[k]))

        def residues(t, carry):
            rs = [t * rways + u for u in range(rways)]
            sels = [_da_residue_rows(r, 0, ld, d) for r in rs]
            for u, sel in enumerate(sels):
                qn_scr[u] = _da_rmsnorm(q_ref[sel, :], gqv)[2].astype(BF16)
                kn_scr[u] = _da_rmsnorm(k_ref[sel, :], gkv)[2].astype(BF16)
                v_scr[u] = v_ref[sel, :].astype(BF16)
            dkn_scr[...] = jnp.zeros_like(dkn_scr)
            dvr_scr[...] = jnp.zeros_like(dvr_scr)

            def block(u, r, i):
                s, qb, kw, qrows, win = _da_scores(qn_scr.at[u], kn_scr.at[u], slope, i, ld)
                src = _da_residue_rows(r, i * DA_QB, DA_QB, d)
                p = jnp.exp(s - lse_ref[src, :][:, 0:1])
                dob = do_ref[src, :]
                dp = _dot(dob, v_scr[u, win, :], _NT)
                ds = p * (dp - dd_ref[src, :][:, 0:1]) * DA_SCALE
                return u, qrows, win, _dot(p, dob, _TN), _dot(ds, kw, _NN), _dot(ds, qb, _TN)

            def step(i, c2):
                todo = [(u, r, i + w * (nqb // ways)) for u, r in enumerate(rs) for w in range(ways)]
                for u, qrows, win, dv, dqn, dkn in [block(*t) for t in todo]:
                    dvr_scr[u, win, :] += dv
                    dqn_scr[u, qrows, :] = dqn
                    dkn_scr[u, win, :] += dkn
                return c2

            lax.fori_loop(0, nqb // ways, step, 0)

            pq, pk = carry
            for u, sel in enumerate(sels):
                parts = []
                for x_ref, gv, dn_scr, dx_scr in ((q_ref, gqv, dqn_scr, dq_scr), (k_ref, gkv, dkn_scr, dk_scr)):
                    hat, rstd, _ = _da_rmsnorm(x_ref[sel, :], gv)
                    dn = dn_scr[u]
                    dyg = dn * gv
                    dx_scr[sel, :] = rstd * (dyg - hat * jnp.mean(dyg * hat, axis=-1, keepdims=True))
                    parts.append(jnp.sum(dn * hat, axis=0, keepdims=True))
                dv_scr[sel, :] = dvr_scr[u]
                pq, pk = pq + parts[0], pk + parts[1]
            return pq, pk

        zero = jnp.zeros((1, HEAD_DIM), F32)
        pq, pk = lax.fori_loop(0, d // rways, residues, (zero, zero))

        @pl.when(pl.program_id(0) == 0)
        def _():
            dgq_ref[...] = pq
            dgk_ref[...] = pk

        @pl.when(pl.program_id(0) > 0)
        def _():
            dgq_ref[...] += pq
            dgk_ref[...] += pk

        @pl.when(head > 0)
        def _():
            for cp in copies:
                cp.wait()

        dq_out[...] = dq_scr[...].astype(BF16)
        dk_out[...] = dk_scr[...].astype(BF16)
        dv_out[...] = dv_scr[...].astype(BF16)
        for cp in copies:
            cp.start()

        @pl.when(head == DA_HEADS_PER_GROUP - 1)
        def _():
            for cp in copies:
                cp.wait()

    seq_spec = lambda base: pl.BlockSpec((SEQ, HEAD_DIM), lambda h: (0, base + 4 * group + h))
    out_spec = pl.BlockSpec((SEQ, HEAD_DIM), lambda h: (0, h))
    gain_spec = pl.BlockSpec((1, HEAD_DIM), lambda h: (0, 0))
    gshape = jax.ShapeDtypeStruct((1, HEAD_DIM), F32)
    blk = SEQ * HEAD_DIM * 4
    return pl.pallas_call(
        body, name=f"da_bwd_g{group}", grid=(DA_HEADS_PER_GROUP,),
        in_specs=[seq_spec(CB_DA_Q), seq_spec(CB_DA_K), seq_spec(CB_DA_V), gain_spec, gain_spec,
                  pl.BlockSpec((None, 1, LANE), lambda h: (h, 0, 0)), out_spec, out_spec, out_spec, _any_spec()],
        out_specs=[_any_spec(), gain_spec, gain_spec],
        out_shape=[jax.ShapeDtypeStruct((SEQ, IN_COLS), BF16), gshape, gshape],
        scratch_shapes=[pltpu.VMEM((rways, ld, HEAD_DIM), BF16)] * 3 + [pltpu.VMEM((rways, ld, HEAD_DIM), F32)] * 3
        + [pltpu.VMEM((SEQ, HEAD_DIM), F32)] * 3 + [pltpu.VMEM((SEQ, HEAD_DIM), BF16)] * 3
        + [pltpu.SemaphoreType.DMA((3,))],
        input_output_aliases={9: 0},
        compiler_params=_params(("arbitrary",), 12 * blk + 3 * blk + 3 * blk + 3 * blk + 2 * blk),
    )(proj, proj, proj, gq, gk, _da_slopes(group), do, lse, dd, dproj)


MEM_WAYS = 4


def _mem_probs(q, kn, gq):
    qhat, rq, qn = _da_rmsnorm(q, gq)
    s = _dot(qn, kn, _NT) * MEM_SCALE
    m = jnp.max(s, axis=-1, keepdims=True)
    e = jnp.exp(s - m)
    p = e / jnp.sum(e, axis=-1, keepdims=True)
    return p, qhat, rq, qn


def _mem_pieces(tq):
    rows = tq // MEM_WAYS
    return [slice(w * rows, (w + 1) * rows) for w in range(MEM_WAYS)]


def _mem_fwd(proj, kv, gq, gk, tq):
    def body(q_ref, k_ref, v_ref, gq_ref, gk_ref, o_ref):
        kn = _da_rmsnorm(k_ref[...], gk_ref[...])[2]
        v, gqv = v_ref[...], gq_ref[...]
        pieces = _mem_pieces(tq)
        outs = [_dot(_mem_probs(q_ref[rs, :], kn, gqv)[0], v, _NN) for rs in pieces]
        for rs, o in zip(pieces, outs):
            o_ref[rs, :] = o.astype(BF16)

    gain_spec = pl.BlockSpec((1, HEAD_DIM), lambda h, i: (0, 0))
    return pl.pallas_call(
        body, name="mem_fwd", grid=(MEM_HEADS, SEQ // tq),
        in_specs=[pl.BlockSpec((tq, HEAD_DIM), lambda h, i: (i, CB_MEM_Q + h)),
                  pl.BlockSpec((N_MEM, HEAD_DIM), lambda h, i: (0, h)),
                  pl.BlockSpec((N_MEM, HEAD_DIM), lambda h, i: (0, MEM_HEADS + h)), gain_spec, gain_spec],
        out_specs=pl.BlockSpec((tq, HEAD_DIM), lambda h, i: (i, h)),
        out_shape=jax.ShapeDtypeStruct((SEQ, MEM_WIDTH), BF16),
        compiler_params=_params(("parallel", "parallel"), 16 * tq * N_MEM * 4),
    )(proj, kv, kv, gq, gk)


def _mem_bwd(proj, kv, gq, gk, do, dproj, tq):
    nq = SEQ // tq

    def body(q_ref, k_ref, v_ref, gq_ref, gk_ref, do_ref, dproj_in, dq_ref, dk_ref, dv_ref, dgq_ref, dgk_ref, dkn_scr):
        del dproj_in
        h, i = pl.program_id(0), pl.program_id(1)
        gqv, gkv = gq_ref[...], gk_ref[...]
        khat, rk, kn = _da_rmsnorm(k_ref[...], gkv)
        v = v_ref[...]

        def piece(rs):
            p, qhat, rq, qn = _mem_probs(q_ref[rs, :], kn, gqv)
            dob = do_ref[rs, :]
            dp = _dot(dob, v, _NT)
            ds = p * (dp - jnp.sum(p * dp, axis=-1, keepdims=True)) * MEM_SCALE
            dqn = _dot(ds, kn, _NN)
            dyg = dqn * gqv
            dq = (rq * (dyg - qhat * jnp.mean(dyg * qhat, axis=-1, keepdims=True))).astype(BF16)
            return dq, _dot(p, dob, _TN), _dot(ds, qn, _TN), jnp.sum(dqn * qhat, axis=0, keepdims=True)

        pieces = _mem_pieces(tq)
        results = [piece(rs) for rs in pieces]
        for rs, res in zip(pieces, results):
            dq_ref[rs, :] = res[0]
        dvp = sum((res[1] for res in results[1:]), results[0][1])
        dknp = sum((res[2] for res in results[1:]), results[0][2])
        dgq_part = sum((res[3] for res in results[1:]), results[0][3])
        first = jnp.logical_and(h == 0, i == 0)

        @pl.when(first)
        def _():
            dgq_ref[...] = dgq_part

        @pl.when(jnp.logical_not(first))
        def _():
            dgq_ref[...] += dgq_part

        @pl.when(i == 0)
        def _():
            dv_ref[...] = dvp
            dkn_scr[...] = dknp

        @pl.when(i > 0)
        def _():
            dv_ref[...] += dvp
            dkn_scr[...] += dknp

        @pl.when(i == nq - 1)
        def _():
            dkn = dkn_scr[...]
            dkg = dkn * gkv
            dk_ref[...] = rk * (dkg - khat * jnp.mean(dkg * khat, axis=-1, keepdims=True))
            dgk_part = jnp.sum(dkn * khat, axis=0, keepdims=True)

            @pl.when(h == 0)
            def _():
                dgk_ref[...] = dgk_part

            @pl.when(h > 0)
            def _():
                dgk_ref[...] += dgk_part

    gain_spec = pl.BlockSpec((1, HEAD_DIM), lambda h, i: (0, 0))
    kvout = pl.BlockSpec((N_MEM, HEAD_DIM), lambda h, i: (0, h))
    return pl.pallas_call(
        body, name="mem_bwd", grid=(MEM_HEADS, nq),
        in_specs=[pl.BlockSpec((tq, HEAD_DIM), lambda h, i: (i, CB_MEM_Q + h)),
                  pl.BlockSpec((N_MEM, HEAD_DIM), lambda h, i: (0, h)),
                  pl.BlockSpec((N_MEM, HEAD_DIM), lambda h, i: (0, MEM_HEADS + h)), gain_spec, gain_spec,
                  pl.BlockSpec((tq, HEAD_DIM), lambda h, i: (i, h)), _any_spec()],
        out_specs=[pl.BlockSpec((tq, HEAD_DIM), lambda h, i: (i, CB_MEM_Q + h)), kvout, kvout, gain_spec, gain_spec],
        out_shape=[jax.ShapeDtypeStruct((SEQ, IN_COLS), BF16), jax.ShapeDtypeStruct((N_MEM, MEM_WIDTH), F32),
                   jax.ShapeDtypeStruct((N_MEM, MEM_WIDTH), F32), jax.ShapeDtypeStruct((1, HEAD_DIM), F32),
                   jax.ShapeDtypeStruct((1, HEAD_DIM), F32)],
        scratch_shapes=[pltpu.VMEM((N_MEM, HEAD_DIM), F32)], input_output_aliases={6: 0},
        compiler_params=_params(("arbitrary", "arbitrary"), 24 * tq * N_MEM * 4),
    )(proj, kv, kv, gq, gk, do, dproj)


def _mesh_position():
    return lax.axis_index("x"), lax.axis_index("y"), lax.axis_index("c")


def _any_spec():
    return pl.BlockSpec(memory_space=pl.ANY)


def _all_gather(shards):
    n = len(shards)

    def body(*refs):
        ins, outs = refs[:n], refs[n:2 * n]
        send_sems, recv_sems, local_sems = refs[2 * n:]
        x, y, c = _mesh_position()
        me, sibling = (x, y, c), (x, y, 1 - c)
        first = (jnp.where(c == 0, 1 - x, x), jnp.where(c == 0, y, 1 - y), c)
        second = (jnp.where(c == 0, x, 1 - x), jnp.where(c == 0, 1 - y, y), c)
        diagonal = (1 - x, 1 - y, c)

        def copy(w, k, block, to, src=None):
            px, py, pc = block
            rows = outs[w].at[4 * px + 2 * py + pc]
            return pltpu.make_async_remote_copy(
                src_ref=rows if src is None else src, dst_ref=rows,
                send_sem=send_sems.at[7 * w + k], recv_sem=recv_sems.at[7 * w + k],
                device_id=to, device_id_type=MESH)

        started = []
        for w in range(n):
            mine = pltpu.make_async_copy(ins[w], outs[w].at[4 * x + 2 * y + c], local_sems.at[w])
            mine.start()
            started.append(mine)
        sends = []
        for w in range(n):
            own = [copy(w, 0, me, sibling, src=ins[w]), copy(w, 1, me, first, src=ins[w]),
                   copy(w, 2, me, second, src=ins[w])]
            for cp in own:
                cp.start()
            sends += own
        for w in range(n):
            copy(w, 1, first, me).wait_recv()
            follow = [copy(w, 3, first, second), copy(w, 4, first, sibling)]
            for cp in follow:
                cp.start()
            copy(w, 2, second, me).wait_recv()
            follow.append(copy(w, 5, second, sibling))
            follow[-1].start()
            sends += follow
        for w in range(n):
            copy(w, 3, diagonal, me).wait_recv()
            passed = copy(w, 6, diagonal, sibling)
            passed.start()
            sends.append(passed)
        for w in range(n):
            for k in (0, 4, 5, 6):
                copy(w, k, sibling, me).wait_recv()
        for cp in sends:
            cp.wait_send()
        for mine in started:
            mine.wait()

    return pl.pallas_call(
        body, name="weights_all_gather",
        in_specs=[_any_spec()] * n, out_specs=[_any_spec()] * n,
        out_shape=[jax.ShapeDtypeStruct((N_DEV,) + s.shape, s.dtype) for s in shards],
        scratch_shapes=[pltpu.SemaphoreType.DMA((7 * n,)), pltpu.SemaphoreType.DMA((7 * n,)),
                        pltpu.SemaphoreType.DMA((n,))],
    )(*shards)


def _chip_of(j, x, y):
    return (1 - x if j & 1 else x, 1 - y if j & 2 else y)


_HBM_SPEC = pl.BlockSpec(memory_space=pltpu.HBM)
_SEM_SPEC = pl.BlockSpec(memory_space=pltpu.SEMAPHORE)
_DATAFLOW_EFFECT = pltpu.SideEffectType.DATAFLOW_SIDE_EFFECTING
TOKEN_SHAPE = (8, D_MODEL)


def _copies_start(name, arrays, n_copies, plan):
    n = len(arrays)

    def body(*refs):
        send_sems, recv_sems, token = refs[n], refs[n + 1], refs[2 * n + 2]
        copies = plan(refs[:n])
        assert len(copies) == n_copies
        for k, (src, dst, dev) in enumerate(copies):
            pltpu.make_async_remote_copy(src_ref=src, dst_ref=dst, send_sem=send_sems.at[k], recv_sem=recv_sems.at[k],
                                         device_id=dev, device_id_type=MESH).start()
        token[...] = jnp.zeros_like(token)

    outs = pl.pallas_call(
        body, name=name,
        out_shape=(pltpu.SemaphoreType.DMA((n_copies,)), pltpu.SemaphoreType.DMA((n_copies,)),
                   *[pltpu.HBM(a.shape, a.dtype) for a in arrays], jax.ShapeDtypeStruct(TOKEN_SHAPE, F32)),
        in_specs=[_HBM_SPEC] * n,
        out_specs=(_SEM_SPEC, _SEM_SPEC, *[_HBM_SPEC] * n, pl.BlockSpec(memory_space=pltpu.VMEM)),
        input_output_aliases={i: i + 2 for i in range(n)},
        compiler_params=pltpu.CompilerParams(has_side_effects=_DATAFLOW_EFFECT),
    )(*[pltpu.with_memory_space_constraint(a, pltpu.HBM) for a in arrays])
    return outs[0], outs[1], list(outs[2:2 + n]), outs[2 + n]


def _copies_wait(name, send_sems, recv_sems, arrays, n_copies, plan, after):
    n = len(arrays)
    after = list(after) if isinstance(after, (list, tuple)) else [after]

    def body(*refs):
        send_ref, recv_ref = refs[n], refs[n + 1]
        copies = plan(refs[:n])
        assert len(copies) == n_copies
        for k, (src, dst, dev) in enumerate(copies):
            cp = pltpu.make_async_remote_copy(src_ref=src, dst_ref=dst, send_sem=send_ref.at[k], recv_sem=recv_ref.at[k],
                                              device_id=dev, device_id_type=MESH)
            cp.wait_send()
            cp.wait_recv()

    outs = pl.pallas_call(
        body, name=name, out_shape=tuple(pltpu.HBM(a.shape, a.dtype) for a in arrays),
        in_specs=[_HBM_SPEC] * n + [_SEM_SPEC, _SEM_SPEC] + [pl.BlockSpec(memory_space=pl.ANY)] * len(after),
        out_specs=tuple([_HBM_SPEC] * n), input_output_aliases={i: i for i in range(n)},
        compiler_params=pltpu.CompilerParams(has_side_effects=_DATAFLOW_EFFECT),
    )(*arrays, send_sems, recv_sems, *after)
    return list(outs)


def _after(small, token):
    return small if token is None else small + token[0:1, :small.shape[-1]]


def _gather_plan_out(n):
    def plan(refs):
        x, y, c = _mesh_position()
        me = 4 * x + 2 * y + c
        copies = []
        for w in range(n):
            land = refs[n + w].at[me]
            copies.append((refs[w], land, (x, y, 1 - c)))
            for j in range(1, 4):
                copies.append((refs[w], land, (*_chip_of(j, x, y), c)))
        return copies
    return plan


def _gather_plan_pass(n):
    def plan(refs):
        x, y, c = _mesh_position()
        copies = []
        for w in range(n):
            for j in range(1, 4):
                px, py = _chip_of(j, x, y)
                rows = refs[w].at[4 * px + 2 * py + c]
                copies.append((rows, rows, (x, y, 1 - c)))
        return copies
    return plan


def _reduce_plan_sibling(n):
    def plan(refs):
        x, y, c = _mesh_position()
        copies = []
        for w in range(n):
            for j in range(4):
                px, py = _chip_of(j, x, y)
                copies.append((refs[w].at[4 * px + 2 * py + (1 - c)], refs[n + w].at[j], (x, y, 1 - c)))
        return copies
    return plan


def _reduce_plan_chips(n):
    def plan(refs):
        x, y, c = _mesh_position()
        copies = []
        for w in range(n):
            for j in range(1, 4):
                copies.append((refs[w].at[j - 1], refs[n + w].at[j - 1], (*_chip_of(j, x, y), c)))
        return copies
    return plan


def _chip_partials(grad, recv, name, tr):
    _, rows, width = grad.shape

    def body(g_ref, r_ref, own_ref, other_ref):
        x, y, c = _mesh_position()
        for j in range(4):
            px, py = _chip_of(j, x, y)
            total = g_ref[4 * px + 2 * py + c].astype(F32) + r_ref[j].astype(F32)
            if j == 0:
                own_ref[...] = total
            else:
                other_ref[j - 1] = total.astype(BF16)

    return pl.pallas_call(
        body, name=name, grid=(rows // tr,),
        in_specs=[pl.BlockSpec((N_DEV, tr, width), lambda i: (0, i, 0)),
                  pl.BlockSpec((4, tr, width), lambda i: (0, i, 0))],
        out_specs=[pl.BlockSpec((tr, width), lambda i: (i, 0)), pl.BlockSpec((3, tr, width), lambda i: (0, i, 0))],
        out_shape=[jax.ShapeDtypeStruct((rows, width), F32), jax.ShapeDtypeStruct((3, rows, width), BF16)],
        compiler_params=_params(("parallel",), 2 * 20 * tr * width * 2 + 8 * tr * width * 4),
    )(grad, recv)


def _adamw_math(w, g, m, v):
    m = ADAM_B1 * m + (1.0 - ADAM_B1) * g
    v = ADAM_B2 * v + (1.0 - ADAM_B2) * (g * g)
    m_hat = m / (1.0 - ADAM_B1 ** ADAM_STEP)
    v_hat = v / (1.0 - ADAM_B2 ** ADAM_STEP)
    delta = -ADAM_LR * (m_hat / (jnp.sqrt(v_hat) + ADAM_EPS) + ADAM_WD * w)
    return delta, m, v


def _adamw_shard(own, recv, w, m, v, name, tr):
    rows, width = own.shape

    def body(own_ref, r_ref, w_ref, m_ref, v_ref, g_ref, d_ref, nm_ref, nv_ref):
        g = own_ref[...]
        for j in range(3):
            g = g + r_ref[j].astype(F32)
        g_ref[...] = g
        d_ref[...], nm_ref[...], nv_ref[...] = _adamw_math(w_ref[...], g, m_ref[...], v_ref[...])

    spec = pl.BlockSpec((tr, width), lambda i: (i, 0))
    shape = jax.ShapeDtypeStruct((rows, width), F32)
    return pl.pallas_call(
        body, name=name, grid=(rows // tr,),
        in_specs=[spec, pl.BlockSpec((3, tr, width), lambda i: (0, i, 0)), spec, spec, spec],
        out_specs=[spec] * 4, out_shape=[shape] * 4,
        compiler_params=_params(("parallel",), 22 * tr * width * 4),
    )(own, recv, w, m, v)


def _small_all_reduce_adamw(gpart, lbpack, wpack, mpack, vpack, after):
    def body(gp_ref, lb_ref, w_ref, m_ref, v_ref, after_ref, g_ref, d_ref, nm_ref, nv_ref, gath_ref, send_sems, recv_sems):
        del after_ref
        x, y, c = _mesh_position()
        me = 4 * x + 2 * y + c
        gath_ref[me] = gp_ref[...]
        copies = []
        for j in range(1, N_DEV):
            peer = (x ^ (j >> 2), y ^ ((j >> 1) & 1), c ^ (j & 1))
            cp = pltpu.make_async_remote_copy(
                src_ref=gp_ref, dst_ref=gath_ref.at[me], send_sem=send_sems.at[j - 1], recv_sem=recv_sems.at[j - 1],
                device_id=peer, device_id_type=MESH)
            cp.start()
            copies.append(cp)
        for cp in copies:
            cp.wait()
        tot = gath_ref[0]
        for s in range(1, N_DEV):
            tot = tot + gath_ref[s]
        lb = lb_ref[...]
        dl = tot[16:32, :] * lb * (1.0 - lb)
        g = jnp.concatenate([tot[0:16, :], dl[0:8, :], -dl[0:8, :], dl[8:16, :], -dl[8:16, :],
                             tot[32:SMALL_GRAD_ROWS, :]], axis=0)
        g_ref[...] = g
        d_ref[...], nm_ref[...], nv_ref[...] = _adamw_math(w_ref[...], g, m_ref[...], v_ref[...])

    vm = pl.BlockSpec(memory_space=pltpu.VMEM)
    shape = jax.ShapeDtypeStruct((SMALL_ROWS, LANE), F32)
    return pl.pallas_call(
        body, name="small_all_reduce_adamw", in_specs=[vm] * 5 + [_any_spec()], out_specs=[vm] * 4,
        out_shape=[shape] * 4,
        scratch_shapes=[pltpu.VMEM((N_DEV, SMALL_GRAD_ROWS, LANE), F32),
                        pltpu.SemaphoreType.DMA((N_DEV - 1,)), pltpu.SemaphoreType.DMA((N_DEV - 1,))],
    )(gpart, lbpack, wpack, mpack, vpack, after)


_SMALL_NAMES = ("norm_mix_gain", "norm_mem_gain", "lb_logits_fw", "lb_logits_bw", "norm_ffn_gain",
                "hg_norm_gain", "da_q_gain", "da_k_gain", "mem_q_gain", "mem_k_gain")
_SMALL_ROW0 = {"norm_mix_gain": 0, "norm_mem_gain": 8, "lb_logits_fw": 16, "lb_logits_bw": 32, "norm_ffn_gain": 48,
               "hg_norm_gain": 56, "da_q_gain": 64, "da_k_gain": 72, "mem_q_gain": 80, "mem_k_gain": 88}
_LOSS_ROW = 96


def _pack_rows(parts, total_rows):
    rows = []
    for p in parts:
        r = p.reshape(-1, LANE)
        rows.append(jnp.pad(r, ((0, -r.shape[0] % 8), (0, 0))))
    used = sum(r.shape[0] for r in rows)
    if total_rows > used:
        rows.append(jnp.zeros((total_rows - used, LANE), F32))
    return jnp.concatenate(rows, axis=0)


def _unpack_small(pack, like):
    out = {}
    for name in _SMALL_NAMES:
        n = like[name].size // LANE
        r0 = _SMALL_ROW0[name]
        out[name] = pack[r0:r0 + n].reshape(like[name].shape)
    return out


def kernel(x, mem, norm_mix_gain, norm_mem_gain, w_in, lb_logits_fw, lb_logits_bw, hg_norm_gain, da_q_gain, da_k_gain, w_mem_kv, mem_q_gain, mem_k_gain, w_proj_hg, w_proj_da, w_proj_mem, w_out, norm_ffn_gain, w_ffn_in, w_ffn_out, loss_target, m_norm_mix_gain, m_norm_mem_gain, m_w_in, m_lb_logits_fw, m_lb_logits_bw, m_hg_norm_gain, m_da_q_gain, m_da_k_gain, m_w_mem_kv, m_mem_q_gain, m_mem_k_gain, m_w_proj_hg, m_w_proj_da, m_w_proj_mem, m_w_out, m_norm_ffn_gain, m_w_ffn_in, m_w_ffn_out, v_norm_mix_gain, v_norm_mem_gain, v_w_in, v_lb_logits_fw, v_lb_logits_bw, v_hg_norm_gain, v_da_q_gain, v_da_k_gain, v_w_mem_kv, v_mem_q_gain, v_mem_k_gain, v_w_proj_hg, v_w_proj_da, v_w_proj_mem, v_w_out, v_norm_ffn_gain, v_w_ffn_in, v_w_ffn_out):
    small_w = dict(norm_mix_gain=norm_mix_gain, norm_mem_gain=norm_mem_gain, lb_logits_fw=lb_logits_fw,
                   lb_logits_bw=lb_logits_bw, norm_ffn_gain=norm_ffn_gain, hg_norm_gain=hg_norm_gain,
                   da_q_gain=da_q_gain, da_k_gain=da_k_gain, mem_q_gain=mem_q_gain, mem_k_gain=mem_k_gain)
    small_m = dict(norm_mix_gain=m_norm_mix_gain, norm_mem_gain=m_norm_mem_gain, lb_logits_fw=m_lb_logits_fw,
                   lb_logits_bw=m_lb_logits_bw, norm_ffn_gain=m_norm_ffn_gain, hg_norm_gain=m_hg_norm_gain,
                   da_q_gain=m_da_q_gain, da_k_gain=m_da_k_gain, mem_q_gain=m_mem_q_gain, mem_k_gain=m_mem_k_gain)
    small_v = dict(norm_mix_gain=v_norm_mix_gain, norm_mem_gain=v_norm_mem_gain, lb_logits_fw=v_lb_logits_fw,
                   lb_logits_bw=v_lb_logits_bw, norm_ffn_gain=v_norm_ffn_gain, hg_norm_gain=v_hg_norm_gain,
                   da_q_gain=v_da_q_gain, da_k_gain=v_da_k_gain, mem_q_gain=v_mem_q_gain, mem_k_gain=v_mem_k_gain)
    big_w = dict(w_in=w_in[0], w_mem_kv=w_mem_kv[0], w_proj_hg=w_proj_hg[0], w_proj_da=w_proj_da[0],
                 w_proj_mem=w_proj_mem[0], w_out=w_out[0], w_ffn_in=w_ffn_in[0], w_ffn_out=w_ffn_out[0])
    big_m = dict(w_in=m_w_in[0], w_mem_kv=m_w_mem_kv[0], w_proj_hg=m_w_proj_hg[0], w_proj_da=m_w_proj_da[0],
                 w_proj_mem=m_w_proj_mem[0], w_out=m_w_out[0], w_ffn_in=m_w_ffn_in[0], w_ffn_out=m_w_ffn_out[0])
    big_v = dict(w_in=v_w_in[0], w_mem_kv=v_w_mem_kv[0], w_proj_hg=v_w_proj_hg[0], w_proj_da=v_w_proj_da[0],
                 w_proj_mem=v_w_proj_mem[0], w_out=v_w_out[0], w_ffn_in=v_w_ffn_in[0], w_ffn_out=v_w_ffn_out[0])

    row_tile = dict(w_in=128, w_mem_kv=128, w_proj_hg=128, w_proj_da=512, w_proj_mem=512, w_out=128,
                    w_ffn_in=128, w_ffn_out=352)
    rest = _BIG_NAMES[1:]
    shards = [big_w[n].astype(BF16) for n in rest]
    state = {}
    big_out = {}

    def reduce_start(group, stacked):
        names = tuple(stacked)
        arrays = [stacked[n] for n in names] + [lax.empty((4,) + stacked[n].shape[1:], BF16) for n in names]
        plan = _reduce_plan_sibling(len(names))
        send, recv, thru, token = _copies_start(f"grads_{group}_sibling_start", arrays, 4 * len(names), plan)
        state[group] = dict(names=names, plan=plan, send=send, recv=recv, arrays=thru)
        return token

    def reduce_middle(group, after):
        st = state[group]
        names, k = st["names"], len(st["names"])
        thru = _copies_wait(f"grads_{group}_sibling_wait", st["send"], st["recv"], st["arrays"], 4 * k, st["plan"], after)
        partials = [_chip_partials(thru[i], thru[k + i], f"chip_partials_{n}", row_tile[n]) for i, n in enumerate(names)]
        arrays = [p[1] for p in partials] + [lax.empty(p[1].shape, BF16) for p in partials]
        plan = _reduce_plan_chips(k)
        send, recv, thru2, token = _copies_start(f"grads_{group}_chips_start", arrays, 3 * k, plan)
        state[group] = dict(names=names, plan=plan, send=send, recv=recv, arrays=thru2, own=[p[0] for p in partials])
        return token

    def reduce_finish(group, after):
        st = state.pop(group)
        names, k = st["names"], len(st["names"])
        thru = _copies_wait(f"grads_{group}_chips_wait", st["send"], st["recv"], st["arrays"], 3 * k, st["plan"], after)
        for i, n in enumerate(names):
            big_out[n] = _adamw_shard(st["own"][i], thru[k + i], big_w[n], big_m[n], big_v[n], "adamw_" + n, row_tile[n])

    win_st = _all_gather([big_w["w_in"].astype(BF16)])[0]
    step = _local_step_stages(x[0], mem[0], loss_target[0], small_w, win_st)
    event, payload = next(step)
    local = None
    while True:
        reply = None
        if event == "begin":
            nr = len(rest)
            me = 4 * lax.axis_index("x") + 2 * lax.axis_index("y") + lax.axis_index("c")
            arrays = shards + [lax.dynamic_update_slice(lax.empty((N_DEV,) + s.shape, BF16), s[None], (me, 0, 0))
                               for s in shards]
            plan = _gather_plan_out(nr)
            send, recv, thru, reply = _copies_start("weights_rest_out_start", arrays, 4 * nr, plan)
            state["gather"] = dict(plan=plan, send=send, recv=recv, arrays=thru)
        elif event == "after_gla_fwd":
            st = state["gather"]
            nr = len(rest)
            thru = _copies_wait("weights_rest_out_wait", st["send"], st["recv"], st["arrays"], 4 * nr, st["plan"], payload)
            plan = _gather_plan_pass(nr)
            send, recv, lands, reply = _copies_start("weights_rest_pass_start", thru[nr:], 3 * nr, plan)
            state["gather"] = dict(plan=plan, send=send, recv=recv, arrays=lands)
        elif event == "need_weights":
            st = state.pop("gather")
            nr = len(rest)
            lands = _copies_wait("weights_rest_pass_wait", st["send"], st["recv"], st["arrays"], 3 * nr, st["plan"], payload)
            reply = dict(zip(rest, lands))
        elif event == "grads_ffn":
            reply = reduce_start("ffn", payload)
        elif event == "after_gate_merge_bwd":
            reply = reduce_middle("ffn", payload)
        elif event == "grads_mix":
            reply = reduce_start("mix", payload)
        elif event == "after_da_bwd_g0":
            reply = reduce_middle("mix", payload)
            reduce_finish("ffn", payload)
        elif event == "after_gla_bwd":
            reduce_finish("mix", payload)
        elif event == "grads_in":
            reply = reduce_start("in", payload)
        elif event == "after_proj_bwd_act_top":
            reply = reduce_middle("in", payload)
        elif event == "end":
            local = payload
            reduce_finish("in", [local["grad_x"]] + [big_out[n][0] for n in rest])
            break
        event, payload = step.send(reply)
    grad_x = local["grad_x"]

    wpack = _pack_rows([small_w[n] for n in _SMALL_NAMES], SMALL_ROWS)
    mpack = _pack_rows([small_m[n] for n in _SMALL_NAMES], SMALL_ROWS)
    vpack = _pack_rows([small_v[n] for n in _SMALL_NAMES], SMALL_ROWS)
    gs, ds, ms, vs = _small_all_reduce_adamw(local["gpart"], local["lbpack"], wpack, mpack, vpack, big_out["w_in"][0])
    loss = gs[_LOSS_ROW, 0]
    small_out = [_unpack_small(t, small_w) for t in (gs, ds, ms, vs)]

    order = ("norm_mix_gain", "norm_mem_gain", "w_in", "lb_logits_fw", "lb_logits_bw", "hg_norm_gain", "da_q_gain",
             "da_k_gain", "w_mem_kv", "mem_q_gain", "mem_k_gain", "w_proj_hg", "w_proj_da", "w_proj_mem", "w_out",
             "norm_ffn_gain", "w_ffn_in", "w_ffn_out")
    outs = [loss, grad_x[None]]
    for kind in range(4):
        for n in order:
            outs.append(big_out[n][kind][None] if n in big_out else small_out[kind][n])
    return tuple(outs)


_BIG_NAMES = ("w_in", "w_mem_kv", "w_proj_hg", "w_proj_da", "w_proj_mem", "w_out", "w_ffn_in", "w_ffn_out")


def _local_step(xs, mems, target, sw, wg):
    step = _local_step_stages(xs, mems, target, sw, wg["w_in"])
    stacked = {}
    event, payload = next(step)
    while event != "end":
        if event.startswith("grads_"):
            stacked.update(payload)
        event, payload = step.send(wg if event == "need_weights" else None)
    return dict(payload, stacked=stacked)


def _local_step_stages(xs, mems, target, sw, win_st):
    norm_mix_gain, norm_mem_gain, norm_ffn_gain = sw["norm_mix_gain"], sw["norm_mem_gain"], sw["norm_ffn_gain"]
    lb_logits_fw, lb_logits_bw, hg_norm_gain = sw["lb_logits_fw"], sw["lb_logits_bw"], sw["hg_norm_gain"]
    da_q_gain, da_k_gain, mem_q_gain, mem_k_gain = sw["da_q_gain"], sw["da_k_gain"], sw["mem_q_gain"], sw["mem_k_gain"]

    token = yield "begin", None
    lb_fw = _lb_table(lb_logits_fw, "lb_table_fw")
    lb_bw = _lb_table(lb_logits_bw, "lb_table_bw")
    lb = jnp.concatenate([lb_fw, lb_bw], axis=0).reshape(2, HG_HEADS, 1, HEAD_DIM)
    h, h_t = _rmsnorm_fwd(xs, _after(norm_mix_gain, token), "norm_mix_fwd", 512, transposed=True)
    proj = _matmul(h, win_st, "nn", F32, 1024, IN_SHARD, D_MODEL, "proj_fwd", b_stacked=True, n_outer=True)
    o_hg, o_pre, states, hg_qt, hg_decay = _gla_fwd(proj, lb, hg_norm_gain)
    token = yield "after_gla_fwd", o_hg
    da = [_da_fwd(proj, _after(da_q_gain, token), da_k_gain, g) for g in range(3)]
    o_da, o_da32, lse_da = _da_merge([t[0] for t in da], [t[1] for t in da], 512)
    wg = yield "need_weights", o_da
    wkv = wg["w_mem_kv"].reshape(D_MODEL, 2 * MEM_WIDTH)
    wphg = wg["w_proj_hg"].reshape(D_MODEL, D_MODEL)
    wpda = jnp.transpose(wg["w_proj_da"], (1, 0, 2)).reshape(DA_WIDTH, D_MODEL)
    wpmem = jnp.transpose(wg["w_proj_mem"], (1, 0, 2)).reshape(MEM_WIDTH, D_MODEL)
    wout = wg["w_out"].reshape(D_MODEL, D_MODEL)
    wfin = jnp.transpose(wg["w_ffn_in"], (1, 0, 2)).reshape(D_MODEL, 2 * D_FF)
    wfout = wg["w_ffn_out"].reshape(D_FF, D_MODEL)
    mem_n = _rmsnorm_fwd(mems, norm_mem_gain, "norm_mem_fwd", N_MEM)
    kv = _matmul(mem_n, wkv, "nn", F32, N_MEM, 1024, D_MODEL, "mem_kv_fwd")
    o_mem = _mem_fwd(proj, kv, mem_q_gain, mem_k_gain, 1024)
    branch_w = (wphg, wpda, wpmem)
    merged, t_hg, t_da, t_mem = _branch_merge_fwd(proj, (o_hg, o_da, o_mem), branch_w, 512)
    x1, h2, h2_t = _residual_rmsnorm_fwd(xs, merged, wout, norm_ffn_gain, "out_norm_ffn_fwd", 512)
    ffn_a, ffn_b, act, act_t = _ffn_in_swiglu(h2, wfin, 1024, 1408)
    dy, dyb, loss_part = _ffn_out_loss_head(x1, act, wfout, target, 512)

    dab = _ffn_out_bwd_swiglu(dyb, wfout, ffn_a, ffn_b, 1024, 1408)
    g_wfout = _matmul(act_t, dyb, "nn", BF16, 1408, 1024, 2048, "ffn_out_bwd_w")
    g_wfin = _matmul(h2_t, dab, "nn", BF16, 1024, 1408, 2048, "ffn_in_bwd_w", b_parts=True)
    token = yield "grads_ffn", dict(
        w_ffn_in=jnp.transpose(g_wfin.reshape(D_MODEL, N_DEV, 2 * D_FF // N_DEV), (1, 0, 2)),
        w_ffn_out=g_wfout.reshape(N_DEV, D_FF // N_DEV, D_MODEL))
    dx1, dx1b, g_norm_ffn = _matmul_rmsnorm_bwd(dab, wfin, x1, dy, _after(norm_ffn_gain, token),
                                                "ffn_in_bwd_act_norm", 512, D_FF, a_parts=True)
    dmerged = _matmul(dx1b, wout, "nt", F32, 512, 1024, D_MODEL, "out_bwd_act")
    g_wout = _matmul(merged, dx1b, "tn", BF16, 1024, 1024, 2048, "out_bwd_w")
    dt_hg, dt_da, dt_mem, do_hg, do_da, do_mem, dproj = _gate_merge_bwd(
        proj, (t_hg, t_da, t_mem), branch_w, dmerged, _dproj_buffer(), 256)
    token = yield "after_gate_merge_bwd", dt_hg
    g_wphg = _matmul(o_hg, dt_hg, "tn", BF16, 1024, 1024, 2048, "proj_hg_bwd_w", after=token)
    g_wpda = _matmul(o_da, dt_da, "tn", BF16, DA_WIDTH, D_MODEL, 2048, "proj_da_bwd_w")
    g_wpmem = _matmul(o_mem, dt_mem, "tn", BF16, MEM_WIDTH, D_MODEL, 2048, "proj_mem_bwd_w")
    by_owner = lambda g: jnp.transpose(g.reshape(g.shape[0], N_DEV, D_MODEL // N_DEV), (1, 0, 2))
    g_wpda, g_wpmem = by_owner(g_wpda), by_owner(g_wpmem)

    dproj, dk_mem, dv_mem, g_mem_q, g_mem_k = _mem_bwd(proj, kv, mem_q_gain, mem_k_gain, do_mem, dproj, 1024)
    dkv = jnp.concatenate([dk_mem, dv_mem], axis=1).astype(BF16)
    g_wkv = _matmul(mem_n, dkv, "tn", BF16, 1024, 1024, N_MEM, "mem_kv_bwd_w")
    dmem_n = _matmul(dkv, wkv, "nt", F32, N_MEM, 1024, 1024, "mem_kv_bwd_act")
    g_norm_mem = _gain_grad(mems, dmem_n, "norm_mem_bwd")
    token = yield "grads_mix", dict(
        w_mem_kv=g_wkv.reshape(N_DEV, D_MODEL // N_DEV, 2 * MEM_WIDTH),
        w_proj_hg=g_wphg.reshape(N_DEV, D_MODEL // N_DEV, D_MODEL),
        w_proj_da=g_wpda, w_proj_mem=g_wpmem,
        w_out=g_wout.reshape(N_DEV, D_MODEL // N_DEV, D_MODEL))

    dd_da = _da_rowdot(do_da, o_da32, 512)
    dproj, g_da_q, g_da_k = _da_bwd(proj, _after(da_q_gain, token), da_k_gain, do_da, lse_da, dd_da, dproj, 0)
    token = yield "after_da_bwd_g0", g_da_q
    for g in (1, 2):
        dproj, gq_part, gk_part = _da_bwd(proj, _after(da_q_gain, token), da_k_gain, do_da, lse_da, dd_da, dproj, g)
        g_da_q, g_da_k = g_da_q + gq_part, g_da_k + gk_part

    dproj, dlb, g_hg_norm = _gla_bwd(proj, lb, hg_norm_gain, o_pre, states, hg_qt, hg_decay, do_hg, dproj)
    yield "after_gla_bwd", dlb

    g_win =_matmul(h_t, dproj, "nn", BF16, 1024, IN_SHARD, 2048, "proj_bwd_w", out_stacked=True)
    token = yield "grads_in", dict(w_in=g_win)
    grad_x, g_mix_top = _matmul_rmsnorm_bwd(dproj, win_st, xs, dx1, norm_mix_gain, "proj_bwd_act_norm_top", 1024,
                                            IN_SHARD, b_stacked=True, after=token, m_blocks=(0, 1), with_bf16=False)
    token = yield "after_proj_bwd_act_top", g_mix_top
    grad_x, g_mix_bottom = _matmul_rmsnorm_bwd(dproj, win_st, xs, dx1, norm_mix_gain, "proj_bwd_act_norm_bottom", 1024,
                                               IN_SHARD, b_stacked=True, after=token, m_blocks=(1, 3),
                                               out_into=grad_x, with_bf16=False)
    g_norm_mix = g_mix_top + g_mix_bottom

    gpart = _pack_rows([g_norm_mix, g_norm_mem, dlb[0], dlb[1], g_norm_ffn, g_hg_norm, g_da_q, g_da_k,
                        g_mem_q, g_mem_k, loss_part], SMALL_GRAD_ROWS)
    lbpack = jnp.concatenate([lb_fw.reshape(8, LANE), lb_bw.reshape(8, LANE)], axis=0)
    yield "end", dict(grad_x=grad_x, gpart=gpart, lbpack=lbpack)
```

```python
import numpy as np
import jax
import jax.numpy as jnp
from jax import lax
from jax.experimental import pallas as pl
from jax.experimental.pallas import tpu as pltpu

F32 = jnp.float32
BF16 = jnp.bfloat16
MESH = pl.DeviceIdType.MESH

SEQ = 4096
D_MODEL = 1024
N_DEV = 8
N_MEM = 256
RMS_EPS = 1e-6
NEG_INF = -1e30
LANE = 128
HEAD_DIM = 128
HG_HEADS = 8
HG_CHUNK = 64
HG_SCALE = HEAD_DIM ** -0.5
DA_DILATIONS = (1, 4, 16)
DA_RADIUS = 64
DA_HEADS_PER_GROUP = 4
DA_HEADS = 12
DA_WIDTH = 512
DA_SCALE = HEAD_DIM ** -0.5
DA_QB = 128
DA_WIN = 256
DA_WAYS = 4
DA_FWD_WAYS = 4
MEM_HEADS = 4
MEM_WIDTH = 512
MEM_SCALE = HEAD_DIM ** -0.5
D_FF = 2816
IN_COLS = 13312
IN_SHARD = IN_COLS // N_DEV
CB_HG_Q, CB_F, CB_HG_I, CB_HG_G = 0, 8, 24, 32
CB_DA_Q, CB_DA_K, CB_DA_V, CB_MEM_Q = 40, 52, 64, 76
ADAM_LR, ADAM_B1, ADAM_B2, ADAM_EPS, ADAM_WD, ADAM_STEP = 0.001, 0.9, 0.999, 1e-08, 0.01, 10
VMEM_BYTES_V7X = 64 * 1024 * 1024
SMALL_ROWS = 104
SMALL_GRAD_ROWS = 88

_NN = (((1,), (0,)), ((), ()))
_NT = (((1,), (1,)), ((), ()))
_TN = (((0,), (0,)), ((), ()))


def _dot(a, b, dims):
    return lax.dot_general(a.astype(BF16), b.astype(BF16), dims, preferred_element_type=F32)


def _sigmoid(x):
    return 0.5 * jnp.tanh(0.5 * x) + 0.5


def _params(semantics, est_bytes):
    limit = int(min(VMEM_BYTES_V7X - (6 << 20), max(56 << 20, est_bytes * 3 // 2)))
    return pltpu.CompilerParams(dimension_semantics=semantics, vmem_limit_bytes=limit)


def _nbytes(shape, dtype):
    return int(np.prod(shape)) * jnp.dtype(dtype).itemsize


def _alibi_slopes(n):
    return (2.0 ** (-8.0 * np.arange(1, n + 1) / n)).astype(np.float32)


def _matmul(a, b, mode, out_dtype, tm, tn, tk, name, b_stacked=False, out_stacked=False, n_outer=False, after=None,
            m_blocks=None, out_into=None, a_parts=False, b_parts=False):
    if a_parts:
        assert mode == "nt" and tk == a.shape[2]
        m, kdim = a.shape[1], a.shape[0] * a.shape[2]
    elif mode == "tn":
        kdim, m = a.shape
    else:
        m, kdim = a.shape
    if b_parts:
        assert mode == "nn" and not b_stacked and b.shape[2] % tn == 0
        n = b.shape[0] * b.shape[2]
    elif b_stacked:
        if mode == "nn":
            n = b.shape[0] * b.shape[2]
            assert tn == b.shape[2] and tk == kdim == b.shape[1]
        else:
            assert mode == "nt" and tk == b.shape[2] and b.shape[0] * tk == kdim
            n = b.shape[1]
    else:
        n = b.shape[0] if mode == "nt" else b.shape[1]
    assert m % tm == 0 and n % tn == 0 and kdim % tk == 0
    gm, gn, gk = m // tm, n // tn, kdim // tk
    i0 = 0
    if m_blocks is not None:
        assert mode != "tn" and not out_stacked
        i0, gm = m_blocks

    def ijk(f):
        if n_outer:
            return lambda j, i, k: f(i + i0, j, k)
        return lambda i, j, k: f(i + i0, j, k)

    if a_parts:
        a_spec = pl.BlockSpec((None, tm, tk), ijk(lambda i, j, k: (k, i, 0)))
    elif mode == "tn":
        a_spec = pl.BlockSpec((tk, tm), ijk(lambda i, j, k: (k, i)))
    else:
        a_spec = pl.BlockSpec((tm, tk), ijk(lambda i, j, k: (i, k)))
    if b_parts:
        per_part = b.shape[2] // tn
        b_spec = pl.BlockSpec((None, tk, tn), ijk(lambda i, j, k: (j // per_part, k, j % per_part)))
    elif b_stacked and mode == "nn":
        b_spec = pl.BlockSpec((None, tk, tn), ijk(lambda i, j, k: (j, 0, 0)))
    elif b_stacked:
        b_spec = pl.BlockSpec((None, tn, tk), ijk(lambda i, j, k: (k, j, 0)))
    elif mode == "nt":
        b_spec = pl.BlockSpec((tn, tk), ijk(lambda i, j, k: (j, k)))
    else:
        b_spec = pl.BlockSpec((tk, tn), ijk(lambda i, j, k: (k, j)))
    if out_stacked:
        assert tm == m
        out_shape = jax.ShapeDtypeStruct((gn, m, tn), out_dtype)
        o_spec = pl.BlockSpec((None, tm, tn), ijk(lambda i, j, k: (j, i, 0)))
    else:
        out_shape = jax.ShapeDtypeStruct((m, n), out_dtype)
        o_spec = pl.BlockSpec((tm, tn), ijk(lambda i, j, k: (i, j)))
    dims = {"nn": _NN, "nt": _NT, "tn": _TN}[mode]

    n_in = 2 + (after is not None) + (out_into is not None)

    def body(*refs):
        a_ref, b_ref, o_ref = refs[0], refs[1], refs[n_in]
        part = _dot(a_ref[...], b_ref[...], dims)
        if gk == 1:
            o_ref[...] = part.astype(out_dtype)
            return
        acc_ref = refs[-1]
        k = pl.program_id(2)

        @pl.when(k == 0)
        def _():
            acc_ref[...] = part

        @pl.when(jnp.logical_and(k > 0, k < gk - 1))
        def _():
            acc_ref[...] += part

        @pl.when(k == gk - 1)
        def _():
            o_ref[...] = (acc_ref[...] + part).astype(out_dtype)

    a_tile = _nbytes((tm, tk), a.dtype)
    b_tile = _nbytes((tk, tn), b.dtype)
    o_tile = _nbytes((tm, tn), out_dtype)
    est = 2 * (a_tile + b_tile + o_tile) + 3 * tm * tn * 4 + (a_tile + b_tile)
    grid = (gn, gm, gk) if n_outer else (gm, gn, gk)
    operands, in_specs = [a, b], [a_spec, b_spec]
    if after is not None:
        operands.append(after)
        in_specs.append(pl.BlockSpec(memory_space=pl.ANY))
    aliases = {}
    if out_into is not None:
        aliases = {len(operands): 0}
        operands.append(out_into)
        in_specs.append(pl.BlockSpec(memory_space=pl.ANY))
    return pl.pallas_call(
        body, name=name, grid=grid, in_specs=in_specs, out_specs=o_spec, out_shape=out_shape,
        scratch_shapes=[] if gk == 1 else [pltpu.VMEM((tm, tn), F32)], input_output_aliases=aliases,
        compiler_params=_params(("parallel", "parallel", "arbitrary"), est),
    )(*operands)


def _row_spec(tr, width, col_block=0):
    return pl.BlockSpec((tr, width), lambda i: (i, col_block))


def _bcast_spec(width):
    return pl.BlockSpec((1, width), lambda i: (0, 0))


def _col_spec(width, tr):
    return pl.BlockSpec((width, tr), lambda i: (0, i))


def _rmsnorm_fwd(x, gain, name, tr, transposed=False):
    rows, width = x.shape

    def body(x_ref, g_ref, o_ref, *t_ref):
        xv = x_ref[...]
        r = lax.rsqrt(jnp.mean(xv * xv, axis=-1, keepdims=True) + RMS_EPS)
        h = xv * r * g_ref[...]
        o_ref[...] = h.astype(BF16)
        if transposed:
            t_ref[0][...] = h.T.astype(BF16)

    out_specs, out_shape = [_row_spec(tr, width)], [jax.ShapeDtypeStruct((rows, width), BF16)]
    if transposed:
        out_specs.append(_col_spec(width, tr))
        out_shape.append(jax.ShapeDtypeStruct((width, rows), BF16))
    out = pl.pallas_call(
        body, name=name, grid=(rows // tr,), in_specs=[_row_spec(tr, width), _bcast_spec(width)],
        out_specs=out_specs, out_shape=out_shape,
        compiler_params=_params(("parallel",), 10 * tr * width * 4),
    )(x, gain)
    return out if transposed else out[0]


def _residual_rmsnorm_fwd(x, merged, wout, gain, name, tr):
    rows, width = x.shape

    def body(x_ref, m_ref, w_ref, g_ref, x1_ref, h_ref, ht_ref):
        xv = x_ref[...] + _dot(m_ref[...], w_ref[...], _NN)
        x1_ref[...] = xv
        r = lax.rsqrt(jnp.mean(xv * xv, axis=-1, keepdims=True) + RMS_EPS)
        h = xv * r * g_ref[...]
        h_ref[...] = h.astype(BF16)
        ht_ref[...] = h.T.astype(BF16)

    return pl.pallas_call(
        body, name=name, grid=(rows // tr,),
        in_specs=[_row_spec(tr, width), _row_spec(tr, merged.shape[1]),
                  pl.BlockSpec(wout.shape, lambda i: (0, 0)), _bcast_spec(width)],
        out_specs=[_row_spec(tr, width), _row_spec(tr, width), _col_spec(width, tr)],
        out_shape=[jax.ShapeDtypeStruct((rows, width), F32), jax.ShapeDtypeStruct((rows, width), BF16),
                   jax.ShapeDtypeStruct((width, rows), BF16)],
        compiler_params=_params(("parallel",), 14 * tr * width * 4),
    )(x, merged, wout, gain)


def _matmul_rmsnorm_bwd(a, b, x, dres, gain, name, tm, tk, a_parts=False, b_stacked=False, after=None,
                        m_blocks=None, out_into=None, with_bf16=True):
    width = x.shape[1]
    m = a.shape[1] if a_parts else a.shape[0]
    kdim = a.shape[0] * a.shape[2] if a_parts else a.shape[1]
    gk = kdim // tk
    i0, gm = (0, m // tm) if m_blocks is None else m_blocks
    n_in = 5 + (after is not None) + (out_into is not None)

    def body(*refs):
        a_ref, b_ref, x_ref, dres_ref, g_ref = refs[:5]
        outs = refs[n_in:]
        dx_ref, dg_ref, acc_ref = outs[0], outs[-2], outs[-1]
        i, k = pl.program_id(0), pl.program_id(1)
        part = _dot(a_ref[...], b_ref[...], _NT)

        @pl.when(k == 0)
        def _():
            acc_ref[...] = part

        @pl.when(jnp.logical_and(k > 0, k < gk - 1))
        def _():
            acc_ref[...] += part

        @pl.when(k == gk - 1)
        def _():
            dhv = acc_ref[...] + part if gk > 1 else part
            xv = x_ref[...]
            r = lax.rsqrt(jnp.mean(xv * xv, axis=-1, keepdims=True) + RMS_EPS)
            xhat = xv * r
            dyg = dhv * g_ref[...]
            dx = dres_ref[...] + r * (dyg - xhat * jnp.mean(dyg * xhat, axis=-1, keepdims=True))
            dx_ref[...] = dx
            if with_bf16:
                outs[1][...] = dx.astype(BF16)
            gpart = jnp.sum(dhv * xhat, axis=0, keepdims=True)

            @pl.when(i == 0)
            def _():
                dg_ref[...] = gpart

            @pl.when(i > 0)
            def _():
                dg_ref[...] += gpart

    rows = lambda width_: pl.BlockSpec((tm, width_), lambda i, k: (i + i0, 0))
    if a_parts:
        a_spec = pl.BlockSpec((None, tm, tk), lambda i, k: (k, i + i0, 0))
    else:
        a_spec = pl.BlockSpec((tm, tk), lambda i, k: (i + i0, k))
    if b_stacked:
        b_spec = pl.BlockSpec((None, width, tk), lambda i, k: (k, 0, 0))
    else:
        b_spec = pl.BlockSpec((width, tk), lambda i, k: (0, k))
    operands = [a, b, x, dres, gain]
    in_specs = [a_spec, b_spec, rows(width), rows(width), pl.BlockSpec((1, width), lambda i, k: (0, 0))]
    for extra in (after, out_into):
        if extra is not None:
            operands.append(extra)
            in_specs.append(pl.BlockSpec(memory_space=pl.ANY))
    aliases = {} if out_into is None else {len(operands) - 1: 0}
    out_specs = [rows(width)] + ([rows(width)] if with_bf16 else []) + [pl.BlockSpec((1, width), lambda i, k: (0, 0))]
    out_shape = [jax.ShapeDtypeStruct((m, width), F32)] + ([jax.ShapeDtypeStruct((m, width), BF16)] if with_bf16 else [])
    out_shape.append(jax.ShapeDtypeStruct((1, width), F32))
    est = 4 * tm * tk + 4 * width * tk + 12 * tm * width * 4
    return pl.pallas_call(
        body, name=name, grid=(gm, gk), in_specs=in_specs, out_specs=out_specs, out_shape=out_shape,
        scratch_shapes=[pltpu.VMEM((tm, width), F32)], input_output_aliases=aliases,
        compiler_params=_params(("arbitrary", "arbitrary"), est),
    )(*operands)


def _gain_grad(x, dh, name):
    rows, width = x.shape

    def body(x_ref, dh_ref, dg_ref):
        xv = x_ref[...]
        r = lax.rsqrt(jnp.mean(xv * xv, axis=-1, keepdims=True) + RMS_EPS)
        dg_ref[...] = jnp.sum(dh_ref[...] * xv * r, axis=0, keepdims=True)

    return pl.pallas_call(
        body, name=name, grid=(1,), in_specs=[_row_spec(rows, width), _row_spec(rows, width)],
        out_specs=_bcast_spec(width), out_shape=jax.ShapeDtypeStruct((1, width), F32),
        compiler_params=_params(("arbitrary",), 6 * rows * width * 4),
    )(x, dh)


def _lb_table(logits, name):
    slots, width = logits.shape

    def body(l_ref, o_ref):
        lv = l_ref[...]
        mx = jnp.max(lv, axis=0, keepdims=True)
        e = jnp.exp(lv - mx)
        o_ref[...] = e[0:1, :] / jnp.sum(e, axis=0, keepdims=True)

    return pl.pallas_call(
        body, name=name, grid=(1,), in_specs=[pl.BlockSpec((slots, width), lambda i: (0, 0))],
        out_specs=_bcast_spec(width), out_shape=jax.ShapeDtypeStruct((1, width), F32),
    )(logits)


def _branch_merge_fwd(proj, outs, weights, tr):
    w = D_MODEL

    def body(ghg_ref, gda_ref, gmem_ref, ohg_ref, oda_ref, omem_ref, whg_ref, wda_ref, wmem_ref,
             m_ref, thg_ref, tda_ref, tmem_ref):
        acc = None
        for g_ref, o_ref, w_ref, t_ref in ((ghg_ref, ohg_ref, whg_ref, thg_ref), (gda_ref, oda_ref, wda_ref, tda_ref),
                                           (gmem_ref, omem_ref, wmem_ref, tmem_ref)):
            t = _dot(o_ref[...], w_ref[...], _NN)
            t_ref[...] = t.astype(BF16)
            term = _sigmoid(g_ref[...]) * t
            acc = term if acc is None else acc + term
        m_ref[...] = acc.astype(BF16)

    whole = lambda a: pl.BlockSpec(a.shape, lambda i: (0, 0))
    shape = jax.ShapeDtypeStruct((SEQ, w), BF16)
    return pl.pallas_call(
        body, name="branch_merge_fwd", grid=(SEQ // tr,),
        in_specs=[_row_spec(tr, w, 10), _row_spec(tr, w, 11), _row_spec(tr, w, 12)]
        + [_row_spec(tr, o.shape[1]) for o in outs] + [whole(wt) for wt in weights],
        out_specs=[_row_spec(tr, w)] * 4, out_shape=[shape] * 4,
        compiler_params=_params(("parallel",), 20 * tr * w * 4),
    )(proj, proj, proj, *outs, *weights)


def _dproj_buffer():
    return lax.empty((SEQ, IN_COLS), BF16)


def _gate_merge_bwd(proj, ts, weights, dmerged, dproj, tr):
    w = D_MODEL
    steps = SEQ // tr
    gate_col0 = 10 * w

    def body(ghg_ref, gda_ref, gmem_ref, thg_ref, tda_ref, tmem_ref, whg_ref, wda_ref, wmem_ref, dm_ref, dproj_in,
             dthg_ref, dtda_ref, dtmem_ref, dohg_ref, doda_ref, domem_ref, dproj_ref, stage, sems):
        del dproj_in
        i = pl.program_id(0)
        slot = i % 2

        def slot_copy(s):
            rows = pl.ds(pl.multiple_of(i * tr, tr), tr)
            return pltpu.make_async_copy(stage.at[s], dproj_ref.at[rows, pl.ds(gate_col0, 3 * w)], sems.at[s])

        @pl.when(i >= 2)
        def _():
            slot_copy(slot).wait()

        dm = dm_ref[...]
        branches = ((ghg_ref, thg_ref, whg_ref, dthg_ref, dohg_ref), (gda_ref, tda_ref, wda_ref, dtda_ref, doda_ref),
                    (gmem_ref, tmem_ref, wmem_ref, dtmem_ref, domem_ref))
        for b, (g_ref, t_ref, w_ref, dt_ref, do_ref) in enumerate(branches):
            s = _sigmoid(g_ref[...])
            dt = (s * dm).astype(BF16)
            dt_ref[...] = dt
            do_ref[...] = _dot(dt, w_ref[...], _NT)
            stage[slot, :, b * w:(b + 1) * w] = (dm * t_ref[...].astype(F32) * s * (1.0 - s)).astype(BF16)
        slot_copy(slot).start()

        @pl.when(i == steps - 1)
        def _():
            slot_copy(1 - slot).wait()
            slot_copy(slot).wait()

    assert steps >= 2
    whole = lambda a: pl.BlockSpec(a.shape, lambda i: (0, 0))
    return pl.pallas_call(
        body, name="gate_merge_bwd", grid=(steps,),
        in_specs=[_row_spec(tr, w, 10), _row_spec(tr, w, 11), _row_spec(tr, w, 12)] + [_row_spec(tr, w)] * 3
        + [whole(wt) for wt in weights] + [_row_spec(tr, w), _any_spec()],
        out_specs=[_row_spec(tr, w)] * 3 + [_row_spec(tr, wt.shape[0]) for wt in weights] + [_any_spec()],
        out_shape=[jax.ShapeDtypeStruct((SEQ, w), BF16)] * 3
        + [jax.ShapeDtypeStruct((SEQ, wt.shape[0]), F32) for wt in weights]
        + [jax.ShapeDtypeStruct((SEQ, IN_COLS), BF16)],
        scratch_shapes=[pltpu.VMEM((2, tr, 3 * w), BF16), pltpu.SemaphoreType.DMA((2,))],
        input_output_aliases={10: 6},
        compiler_params=_params(("arbitrary",), 34 * tr * w * 4),
    )(proj, proj, proj, *ts, *weights, dmerged, dproj)


def _ffn_in_swiglu(h2, wfin, tm, tn):
    gm, gn = SEQ // tm, D_FF // tn

    def body(h_ref, wa_ref, wb_ref, a_ref, b_ref, act_ref, actt_ref):
        h = h_ref[...]
        a = _dot(h, wa_ref[...], _NN)
        b = _dot(h, wb_ref[...], _NN)
        act = a * _sigmoid(a) * b
        a_ref[...] = a.astype(BF16)
        b_ref[...] = b.astype(BF16)
        act_ref[...] = act.astype(BF16)
        actt_ref[...] = act.T.astype(BF16)

    tile = pl.BlockSpec((tm, tn), lambda j, i: (i, j))
    shape = jax.ShapeDtypeStruct((SEQ, D_FF), BF16)
    return pl.pallas_call(
        body, name="ffn_in_swiglu_fwd", grid=(gn, gm),
        in_specs=[pl.BlockSpec((tm, D_MODEL), lambda j, i: (i, 0)),
                  pl.BlockSpec((D_MODEL, tn), lambda j, i: (0, j)),
                  pl.BlockSpec((D_MODEL, tn), lambda j, i: (0, gn + j))],
        out_specs=[tile, tile, tile, pl.BlockSpec((tn, tm), lambda j, i: (j, i))],
        out_shape=[shape, shape, shape, jax.ShapeDtypeStruct((D_FF, SEQ), BF16)],
        compiler_params=_params(("parallel", "parallel"), 4 * tm * D_MODEL + 8 * D_MODEL * tn + 16 * tm * tn
                                + 6 * tm * tn * 4),
    )(h2, wfin, wfin)


def _ffn_out_bwd_swiglu(dyb, wfout, a, b, tm, tn):
    gm, gn = SEQ // tm, D_FF // tn

    def body(dy_ref, w_ref, a_ref, b_ref, o_ref):
        d = _dot(dy_ref[...], w_ref[...], _NT)
        av = a_ref[...].astype(F32)
        bv = b_ref[...].astype(F32)
        s = _sigmoid(av)
        silu = av * s
        o_ref[0] = (d * bv * (s + silu * (1.0 - s))).astype(BF16)
        o_ref[1] = (d * silu).astype(BF16)

    tile = pl.BlockSpec((tm, tn), lambda i, j: (i, j))
    return pl.pallas_call(
        body, name="ffn_out_bwd_swiglu", grid=(gm, gn),
        in_specs=[pl.BlockSpec((tm, D_MODEL), lambda i, j: (i, 0)), pl.BlockSpec((tn, D_MODEL), lambda i, j: (j, 0)),
                  tile, tile],
        out_specs=pl.BlockSpec((2, tm, tn), lambda i, j: (0, i, j)),
        out_shape=jax.ShapeDtypeStruct((2, SEQ, D_FF), BF16),
        compiler_params=_params(("parallel", "parallel"), 4 * tm * D_MODEL + 4 * tn * D_MODEL + 16 * tm * tn
                                + 6 * tm * tn * 4),
    )(dyb, wfout, a, b)


def _ffn_out_loss_head(x1, act, wfout, target, tr):
    w = D_MODEL

    def body(x_ref, a_ref, w_ref, t_ref, dy_ref, dyb_ref, loss_ref, acc_ref):
        err = x_ref[...] + _dot(a_ref[...], w_ref[...], _NN) - t_ref[...]
        dy = err * (1.0 / w)
        dy_ref[...] = dy
        dyb_ref[...] = dy.astype(BF16)
        part = jnp.sum(err * err, axis=0, keepdims=True)

        @pl.when(pl.program_id(0) == 0)
        def _():
            acc_ref[...] = part

        @pl.when(pl.program_id(0) > 0)
        def _():
            acc_ref[...] += part

        @pl.when(pl.program_id(0) == SEQ // tr - 1)
        def _():
            total = jnp.sum(acc_ref[...], axis=1, keepdims=True) * (0.5 / w)
            loss_ref[...] = jnp.broadcast_to(total, (1, LANE))

    return pl.pallas_call(
        body, name="ffn_out_loss_head", grid=(SEQ // tr,),
        in_specs=[_row_spec(tr, w), _row_spec(tr, D_FF), pl.BlockSpec((D_FF, w), lambda i: (0, 0)), _row_spec(tr, w)],
        out_specs=[_row_spec(tr, w), _row_spec(tr, w), _bcast_spec(LANE)],
        out_shape=[jax.ShapeDtypeStruct((SEQ, w), F32), jax.ShapeDtypeStruct((SEQ, w), BF16),
                   jax.ShapeDtypeStruct((1, LANE), F32)],
        scratch_shapes=[pltpu.VMEM((1, w), F32)],
        compiler_params=_params(("arbitrary",), 12 * tr * w * 4 + 4 * D_FF * w),
    )(x1, act, wfout, target)


GLA_ROWS = 256
GLA_CPB = GLA_ROWS // HG_CHUNK
GLA_NBLK = SEQ // GLA_ROWS
GLA_WAYS = 4
GLA_TRIPS = GLA_NBLK // GLA_WAYS
GLA_GRAD_WAYS = 4
GLA_GRAD_TRIPS = GLA_NBLK // GLA_GRAD_WAYS
GLA_NCK = SEQ // HG_CHUNK
GLA_INTER_WAYS = 8


def _dot_split(m, xv, dims, terms=3):
    dot = lambda t: lax.dot_general(m, t, dims, preferred_element_type=F32)
    hi = xv.astype(BF16)
    r1 = xv - hi.astype(F32)
    mid = r1.astype(BF16)
    if terms == 2:
        return dot(mid) + dot(hi)
    lo = (r1 - mid.astype(F32)).astype(BF16)
    return (dot(lo) + dot(mid)) + dot(hi)


def _gla_block(qc, fc, lbv, masks):
    mask, maskb, direction = masks
    sq = _sigmoid(qc)
    q = qc * sq * HG_SCALE
    sf = _sigmoid(fc)
    forget = lbv + (1.0 - lbv) * sf
    k = 1.0 - forget
    logf = jnp.log(forget)
    b = _dot_split(maskb, logf, _NN)
    ends = []
    for j in range(GLA_CPB):
        lo, hi = j * HG_CHUNK, (j + 1) * HG_CHUNK
        end = jnp.where(direction == 0, b[hi - 1:hi, :], b[lo:lo + 1, :])
        ends.append(jnp.broadcast_to(end, (HG_CHUNK, HEAD_DIM)))
    bt = jnp.concatenate(ends, axis=0)
    eb = jnp.exp(b)
    qt = q * eb
    kt = k * jnp.exp(-b)
    kh = k * jnp.exp(bt - b)
    a = jnp.where(mask, _dot(qt, kt, _NT), 0.0)
    return dict(sq=sq, sf=sf, forget=forget, k=k, b=b, bt=bt, eb=eb, qt=qt, kt=kt, kh=kh, a=a)


def _gla_masks(direction):
    row = lax.broadcasted_iota(jnp.int32, (GLA_ROWS, GLA_ROWS), 0)
    col = lax.broadcasted_iota(jnp.int32, (GLA_ROWS, GLA_ROWS), 1)
    same = (row // HG_CHUNK) == (col // HG_CHUNK)
    mask = jnp.logical_and(same, jnp.where(direction == 0, row - col, col - row) >= 0)
    return mask, jnp.where(mask, 1.0, 0.0).astype(BF16), direction


def _gla_chunk_rows(j):
    return slice(j * HG_CHUNK, (j + 1) * HG_CHUNK)


def _gla_block_rows(b):
    return pl.ds(pl.multiple_of(b * GLA_ROWS, GLA_ROWS), GLA_ROWS)


def _head_spec(col_block0):
    return pl.BlockSpec((SEQ, HEAD_DIM), lambda h, d: (0, col_block0 + h))


def _gla_fwd(proj, lb, gain):
    nck = GLA_NCK

    def body(q_ref, f_ref, v_ref, g_ref, lb_ref, gain_ref, ohg_ref, opre_ref, st_ref, qt_scr, dec_scr, cs_scr):
        d = pl.program_id(1)
        masks = _gla_masks(d)
        lbv = lb_ref[...]

        @pl.when(d == 0)
        def _():
            opre_ref[...] = jnp.zeros_like(opre_ref)

        def intra(s, carry):
            blocks = [s + w * GLA_TRIPS for w in range(GLA_WAYS)]
            rows = [_gla_block_rows(b) for b in blocks]
            loaded = [(q_ref[r, :], f_ref[r, :], v_ref[r, :], opre_ref[r, :]) for r in rows]
            results = []
            for qc, fc, v, o_prev in loaded:
                ck = _gla_block(qc, fc, lbv, masks)
                o = o_prev + _dot(ck["a"], v, _NN)
                cs = [_dot(v[_gla_chunk_rows(j), :], ck["kh"][_gla_chunk_rows(j), :], _TN) for j in range(GLA_CPB)]
                dec = [jnp.exp(ck["bt"][j * HG_CHUNK:j * HG_CHUNK + 8, :]) for j in range(GLA_CPB)]
                results.append((o, ck["qt"].astype(BF16), cs, dec))
            for b, r, (o, qt, cs, dec) in zip(blocks, rows, results):
                opre_ref[r, :] = o
                qt_scr[r, :] = qt
                for j in range(GLA_CPB):
                    cs_scr[b * GLA_CPB + j] = cs[j]
                    dec_scr[b * GLA_CPB + j] = dec[j]
            return carry

        lax.fori_loop(0, GLA_TRIPS, intra, 0)

        def scan(i, st):
            c = jnp.where(d == 0, i, nck - 1 - i)
            st_ref[c] = st
            return st * dec_scr[c][0:1, :] + cs_scr[c]

        lax.fori_loop(0, nck, scan, jnp.zeros((HEAD_DIM, HEAD_DIM), F32), unroll=4)

        def inter(t, carry):
            chunks = [t + w * (nck // GLA_INTER_WAYS) for w in range(GLA_INTER_WAYS)]
            rows = [pl.ds(pl.multiple_of(c * HG_CHUNK, HG_CHUNK), HG_CHUNK) for c in chunks]
            loaded = [(opre_ref[r, :], qt_scr[r, :], st_ref[c]) for r, c in zip(rows, chunks)]
            for r, o in zip(rows, [o_prev + _dot(qt, st, _NT) for o_prev, qt, st in loaded]):
                opre_ref[r, :] = o
            return carry

        lax.fori_loop(0, nck // GLA_INTER_WAYS, inter, 0)

        @pl.when(d == 1)
        def _():
            o = opre_ref[...]
            r = lax.rsqrt(jnp.mean(o * o, axis=-1, keepdims=True) + RMS_EPS)
            g = g_ref[...]
            ohg_ref[...] = (o * r * gain_ref[...] * (g * _sigmoid(g))).astype(BF16)

    blk = SEQ * HEAD_DIM * 4
    return pl.pallas_call(
        body, name="gla_fwd", grid=(HG_HEADS, 2),
        in_specs=[_head_spec(CB_HG_Q),
                  pl.BlockSpec((SEQ, HEAD_DIM), lambda h, d: (0, CB_F + 8 * d + h)),
                  _head_spec(CB_HG_I), _head_spec(CB_HG_G),
                  pl.BlockSpec((None, None, 1, HEAD_DIM), lambda h, d: (d, h, 0, 0)),
                  pl.BlockSpec((1, HEAD_DIM), lambda h, d: (0, 0))],
        out_specs=[_head_spec(0), _head_spec(0),
                   pl.BlockSpec((None, None, nck, HEAD_DIM, HEAD_DIM), lambda h, d: (h, d, 0, 0, 0)),
                   pl.BlockSpec((None, SEQ, HEAD_DIM), lambda h, d: (d, 0, h)),
                   pl.BlockSpec((None, None, nck, 8, HEAD_DIM), lambda h, d: (h, d, 0, 0, 0))],
        out_shape=[jax.ShapeDtypeStruct((SEQ, D_MODEL), BF16), jax.ShapeDtypeStruct((SEQ, D_MODEL), F32),
                   jax.ShapeDtypeStruct((HG_HEADS, 2, nck, HEAD_DIM, HEAD_DIM), F32),
                   jax.ShapeDtypeStruct((2, SEQ, D_MODEL), BF16),
                   jax.ShapeDtypeStruct((HG_HEADS, 2, nck, 8, HEAD_DIM), F32)],
        scratch_shapes=[pltpu.VMEM((nck, HEAD_DIM, HEAD_DIM), F32)],
        compiler_params=_params(("parallel", "arbitrary"), 8 * blk + 3 * blk + 2 * blk + 2 * blk + blk),
    )(proj, proj, proj, proj, lb, gain)


def _gla_bwd(proj, lb, gain, o_pre, states, qt_all, dec_all, do_hg, dproj):
    nck = GLA_NCK
    do, dproj, dgain = _gla_bwd_norm(proj, gain, o_pre, do_hg, dproj, 256)

    def body(q_ref, f_ref, v_ref, lb_ref, do_ref, st_ref, qt_ref, dec_scr, dproj_in, dproj_ref, dlb_ref,
             dq_acc, dv_acc, dst_scr, cs_scr, df_out, dq_out, dv_out, sems):
        del dproj_in
        h, d = pl.program_id(0), pl.program_id(1)
        masks = _gla_masks(d)
        mask, maskb, _ = masks
        lbv = lb_ref[...]

        def column_copy(k, staging, col_block):
            cols = pl.ds(pl.multiple_of(col_block * LANE, LANE), LANE)
            return pltpu.make_async_copy(staging, dproj_ref.at[:, cols], sems.at[k])

        df_copy = column_copy(0, df_out, CB_F + 8 * d + h)
        dq_copy = column_copy(1, dq_out, CB_HG_Q + h)
        dv_copy = column_copy(2, dv_out, CB_HG_I + h)
        last = jnp.logical_and(h == HG_HEADS - 1, d == 1)

        def intra(s, carry):
            blocks = [s + w * GLA_TRIPS for w in range(GLA_WAYS)]
            loaded = [(qt_ref[r, :], do_ref[r, :]) for r in map(_gla_block_rows, blocks)]
            results = [[_dot(doc[_gla_chunk_rows(j), :], qt[_gla_chunk_rows(j), :], _TN) for j in range(GLA_CPB)]
                       for qt, doc in loaded]
            for b, cs in zip(blocks, results):
                for j in range(GLA_CPB):
                    cs_scr[b * GLA_CPB + j] = cs[j]
            return carry

        lax.fori_loop(0, GLA_TRIPS, intra, 0)

        def scan(i, dst):
            c = jnp.where(d == 0, nck - 1 - i, i)
            dst_scr[c] = dst
            return dst * dec_scr[c][0:1, :] + cs_scr[c]

        lax.fori_loop(0, nck, scan, jnp.zeros((HEAD_DIM, HEAD_DIM), F32), unroll=4)

        @pl.when(d == 0)
        def _():
            dq_acc[...] = jnp.zeros_like(dq_acc)
            dv_acc[...] = jnp.zeros_like(dv_acc)

        def block_grads(qc, fc, v, doc, states_in, dstates, decays):
            ck = _gla_block(qc, fc, lbv, masks)
            qt, kt, kh, a = ck["qt"], ck["kt"], ck["kh"], ck["a"]
            da = jnp.where(mask, _dot(doc, v, _NT), 0.0)
            dqt_i = _dot(da, kt, _NN)
            dkt = _dot(da, qt, _TN)
            dv_i = _dot(a, doc, _TN)
            dqt_p, dv_p, dkh_p, dbt_p = [], [], [], []
            for j in range(GLA_CPB):
                cr = _gla_chunk_rows(j)
                st_in, dst = states_in[j], dstates[j]
                dqt_p.append(dqt_i[cr, :] + _dot(doc[cr, :], st_in, _NN))
                dv_p.append(dv_i[cr, :] + _dot(kh[cr, :], dst, _NT))
                dkh_j = _dot(v[cr, :], dst, _NN)
                dkh_p.append(dkh_j)
                dbt_j = (decays[j][0:1, :] * jnp.sum(dst * st_in, axis=0, keepdims=True)
                         + jnp.sum(dkh_j * kh[cr, :], axis=0, keepdims=True))
                dbt_p.append(jnp.broadcast_to(dbt_j, (HG_CHUNK, HEAD_DIM)))
            dqt = jnp.concatenate(dqt_p, axis=0)
            dv = jnp.concatenate(dv_p, axis=0)
            dkh = jnp.concatenate(dkh_p, axis=0)
            dbt = jnp.concatenate(dbt_p, axis=0)
            db = dqt * qt - dkt * kt - dkh * kh
            dq = dqt * ck["eb"]
            dk = dkt * jnp.exp(-ck["b"]) + dkh * jnp.exp(ck["bt"] - ck["b"])
            dlogf = _dot_split(maskb, db, _TN, terms=2) + dbt
            dforget = dlogf / ck["forget"] - dk
            sf, sq = ck["sf"], ck["sq"]
            df = (dforget * (1.0 - lbv) * sf * (1.0 - sf)).astype(BF16)
            dqc = dq * HG_SCALE * (sq + qc * sq * (1.0 - sq))
            return df, dqc, dv, jnp.sum(dforget * (1.0 - sf), axis=0, keepdims=True)

        def grads(s, dlb):
            blocks = [s + w * GLA_GRAD_TRIPS for w in range(GLA_GRAD_WAYS)]
            rows = [_gla_block_rows(b) for b in blocks]
            loaded = []
            for b, r in zip(blocks, rows):
                chunks = [b * GLA_CPB + j for j in range(GLA_CPB)]
                loaded.append((q_ref[r, :], f_ref[r, :], v_ref[r, :], do_ref[r, :], [st_ref[c] for c in chunks],
                               [dst_scr[c] for c in chunks], [dec_scr[c] for c in chunks], dq_acc[r, :], dv_acc[r, :]))
            results = [block_grads(*t[:7]) + (t[7], t[8]) for t in loaded]
            for r, (df, dqc, dv, dlb_part, dq_prev, dv_prev) in zip(rows, results):
                df_out[r, :] = df
                dq_acc[r, :] = dq_prev + dqc
                dv_acc[r, :] = dv_prev + dv
                dlb = dlb + dlb_part
            return dlb

        @pl.when(jnp.logical_or(h > 0, d > 0))
        def _():
            df_copy.wait()

        dlb_ref[...] = lax.fori_loop(0, GLA_GRAD_TRIPS, grads, jnp.zeros((1, HEAD_DIM), F32))
        df_copy.start()

        @pl.when(d == 1)
        def _():
            @pl.when(h > 0)
            def _():
                dq_copy.wait()
                dv_copy.wait()

            dq_out[...] = dq_acc[...].astype(BF16)
            dv_out[...] = dv_acc[...].astype(BF16)
            dq_copy.start()
            dv_copy.start()

        @pl.when(last)
        def _():
            df_copy.wait()
            dq_copy.wait()
            dv_copy.wait()

    blk = SEQ * HEAD_DIM * 4
    state_bytes = nck * HEAD_DIM * HEAD_DIM * 4
    dproj, dlb = pl.pallas_call(
        body, name="gla_bwd", grid=(HG_HEADS, 2),
        in_specs=[_head_spec(CB_HG_Q),
                  pl.BlockSpec((SEQ, HEAD_DIM), lambda h, d: (0, CB_F + 8 * d + h)),
                  _head_spec(CB_HG_I),
                  pl.BlockSpec((None, None, 1, HEAD_DIM), lambda h, d: (d, h, 0, 0)),
                  _head_spec(0),
                  pl.BlockSpec((None, None, nck, HEAD_DIM, HEAD_DIM), lambda h, d: (h, d, 0, 0, 0)),
                  pl.BlockSpec((None, SEQ, HEAD_DIM), lambda h, d: (d, 0, h)),
                  pl.BlockSpec((None, None, nck, 8, HEAD_DIM), lambda h, d: (h, d, 0, 0, 0)),
                  _any_spec()],
        out_specs=[_any_spec(), pl.BlockSpec((None, None, 1, HEAD_DIM), lambda h, d: (d, h, 0, 0))],
        out_shape=[jax.ShapeDtypeStruct((SEQ, IN_COLS), BF16), jax.ShapeDtypeStruct((2, HG_HEADS, 1, HEAD_DIM), F32)],
        scratch_shapes=[pltpu.VMEM((SEQ, HEAD_DIM), F32)] * 2
        + [pltpu.VMEM((nck, HEAD_DIM, HEAD_DIM), F32)] * 2
        + [pltpu.VMEM((SEQ, HEAD_DIM), BF16)] * 3 + [pltpu.SemaphoreType.DMA((3,))],
        input_output_aliases={8: 0},
        compiler_params=_params(("arbitrary", "arbitrary"), 8 * blk + 4 * state_bytes + 3 * blk + 3 * blk),
    )(proj, proj, proj, lb, do, states, qt_all, dec_all, dproj)
    return dproj, dlb, dgain


def _gla_bwd_norm(proj, gain, o_pre, do_hg, dproj, tr):
    w = D_MODEL

    def body(g_ref, gain_ref, opre_ref, dohg_ref, dproj_in, do_ref, dg_ref, dgain_ref):
        del dproj_in
        gainv = gain_ref[...]
        part = jnp.zeros((1, HEAD_DIM), F32)
        for h in range(HG_HEADS):
            hs = slice(h * HEAD_DIM, (h + 1) * HEAD_DIM)
            o = opre_ref[:, hs]
            r = lax.rsqrt(jnp.mean(o * o, axis=-1, keepdims=True) + RMS_EPS)
            ohat = o * r
            g = g_ref[:, hs]
            sg = _sigmoid(g)
            silu = g * sg
            dout = dohg_ref[:, hs]
            dg_ref[:, hs] = (dout * ohat * gainv * (sg + silu * (1.0 - sg))).astype(BF16)
            dy = dout * silu
            part = part + jnp.sum(dy * ohat, axis=0, keepdims=True)
            dn = dy * gainv
            do_ref[:, hs] = r * (dn - ohat * jnp.mean(dn * ohat, axis=-1, keepdims=True))

        @pl.when(pl.program_id(0) == 0)
        def _():
            dgain_ref[...] = part

        @pl.when(pl.program_id(0) > 0)
        def _():
            dgain_ref[...] += part

    return pl.pallas_call(
        body, name="gla_bwd_norm", grid=(SEQ // tr,),
        in_specs=[_row_spec(tr, w, CB_HG_G * LANE // w), _bcast_spec(HEAD_DIM), _row_spec(tr, w), _row_spec(tr, w),
                  _any_spec()],
        out_specs=[_row_spec(tr, w), _row_spec(tr, w, CB_HG_G * LANE // w), _bcast_spec(HEAD_DIM)],
        out_shape=[jax.ShapeDtypeStruct((SEQ, w), F32), jax.ShapeDtypeStruct((SEQ, IN_COLS), BF16),
                   jax.ShapeDtypeStruct((1, HEAD_DIM), F32)],
        input_output_aliases={4: 1},
        compiler_params=_params(("arbitrary",), 16 * tr * w * 4),
    )(proj, gain, o_pre, do_hg, dproj)


def _da_residue_rows(r, n0, size, d):
    if d == 1:
        return pl.ds(pl.multiple_of(n0, 8), size)
    return pl.ds(r + n0 * d, size, stride=d)


def _da_residue_ways(d, nqb):
    return min(d, 4) if nqb <= 2 else 1


def _da_rmsnorm(x, gain):
    r = lax.rsqrt(jnp.mean(x * x, axis=-1, keepdims=True) + RMS_EPS)
    return x * r, r, x * r * gain


def _da_scores(qn_scr, kn_scr, slope, i, ld):
    w0 = jnp.clip(i * DA_QB - DA_RADIUS, 0, ld - DA_WIN)
    w0 = pl.multiple_of(w0, DA_RADIUS)
    qrows = pl.ds(pl.multiple_of(i * DA_QB, DA_QB), DA_QB)
    win = pl.ds(w0, DA_WIN)
    qb = qn_scr[qrows, :]
    kw = kn_scr[win, :]
    s = _dot(qb, kw, _NT) * DA_SCALE
    qpos = i * DA_QB + lax.broadcasted_iota(jnp.int32, (DA_QB, DA_WIN), 0)
    kpos = w0 + lax.broadcasted_iota(jnp.int32, (DA_QB, DA_WIN), 1)
    arel = jnp.abs(kpos - qpos)
    s = s - slope * arel.astype(F32)
    s = jnp.where(arel <= DA_RADIUS, s, NEG_INF)
    return s, qb, kw, qrows, win


def _da_slopes(group):
    d = DA_DILATIONS[group]
    sl = _alibi_slopes(DA_HEADS)[4 * group:4 * group + 4] * d
    return jnp.asarray(np.broadcast_to(sl[:, None, None], (4, 1, LANE)).copy())


def _da_fwd(proj, gq, gk, group):
    d = DA_DILATIONS[group]
    ld = SEQ // d
    nqb = ld // DA_QB

    ways = min(DA_FWD_WAYS, nqb)
    rways = _da_residue_ways(d, nqb)

    def body(q_ref, k_ref, v_ref, gq_ref, gk_ref, sl_ref, o_ref, lse_ref, qn_scr, kn_scr, v_scr):
        slope = sl_ref[:, 0:1]

        def residues(t, carry):
            rs = [t * rways + u for u in range(rways)]
            for u, r in enumerate(rs):
                sel = _da_residue_rows(r, 0, ld, d)
                qn_scr[u] = _da_rmsnorm(q_ref[sel, :], gq_ref[...])[2].astype(BF16)
                kn_scr[u] = _da_rmsnorm(k_ref[sel, :], gk_ref[...])[2].astype(BF16)
                v_scr[u] = v_ref[sel, :].astype(BF16)

            def block(u, i):
                s, _, _, _, win = _da_scores(qn_scr.at[u], kn_scr.at[u], slope, i, ld)
                m = jnp.max(s, axis=-1, keepdims=True)
                p = jnp.exp(s - m)
                l = jnp.sum(p, axis=-1, keepdims=True)
                return _dot(p, v_scr[u, win, :], _NN) / l, jnp.broadcast_to(m + jnp.log(l), (DA_QB, HEAD_DIM))

            def step(i, c2):
                todo = [(u, r, i + w * (nqb // ways)) for u, r in enumerate(rs) for w in range(ways)]
                for (u, r, b), (o, lse) in zip(todo, [block(u, b) for u, _, b in todo]):
                    out = _da_residue_rows(r, b * DA_QB, DA_QB, d)
                    o_ref[out, :] = o
                    lse_ref[out, :] = lse
                return c2

            return lax.fori_loop(0, nqb // ways, step, carry)

        lax.fori_loop(0, d // rways, residues, 0)

    seq_spec = lambda base: pl.BlockSpec((SEQ, HEAD_DIM), lambda h: (0, base + 4 * group + h))
    out_spec = pl.BlockSpec((SEQ, HEAD_DIM), lambda h: (0, h))
    gain_spec = pl.BlockSpec((1, HEAD_DIM), lambda h: (0, 0))
    blk = SEQ * HEAD_DIM * 4
    return pl.pallas_call(
        body, name=f"da_fwd_g{group}", grid=(DA_HEADS_PER_GROUP,),
        in_specs=[seq_spec(CB_DA_Q), seq_spec(CB_DA_K), seq_spec(CB_DA_V), gain_spec, gain_spec,
                  pl.BlockSpec((None, 1, LANE), lambda h: (h, 0, 0))],
        out_specs=[out_spec, out_spec],
        out_shape=[jax.ShapeDtypeStruct((SEQ, DA_WIDTH), F32)] * 2,
        scratch_shapes=[pltpu.VMEM((rways, ld, HEAD_DIM), BF16)] * 3,
        compiler_params=_params(("parallel",), 10 * blk + 2 * blk),
    )(proj, proj, proj, gq, gk, _da_slopes(group))


def _da_merge(os, lses, tr):
    w = DA_WIDTH

    def body(o0, o1, o2, l0, l1, l2, ob_ref, of_ref, lse_ref):
        la, lb_, lc = l0[...], l1[...], l2[...]
        m = jnp.maximum(jnp.maximum(la, lb_), lc)
        ea, eb, ec = jnp.exp(la - m), jnp.exp(lb_ - m), jnp.exp(lc - m)
        tot = ea + eb + ec
        o = (ea * o0[...] + eb * o1[...] + ec * o2[...]) / tot
        of_ref[...] = o
        ob_ref[...] = o.astype(BF16)
        lse_ref[...] = m + jnp.log(tot)

    return pl.pallas_call(
        body, name="da_merge", grid=(SEQ // tr,), in_specs=[_row_spec(tr, w)] * 6,
        out_specs=[_row_spec(tr, w)] * 3,
        out_shape=[jax.ShapeDtypeStruct((SEQ, w), BF16), jax.ShapeDtypeStruct((SEQ, w), F32),
                   jax.ShapeDtypeStruct((SEQ, w), F32)],
        compiler_params=_params(("parallel",), 24 * tr * w * 4),
    )(*os, *lses)


def _da_rowdot(do, o, tr):
    w = DA_WIDTH

    def body(do_ref, o_ref, out_ref):
        prod = do_ref[...] * o_ref[...]
        for h in range(w // HEAD_DIM):
            sl = slice(h * HEAD_DIM, (h + 1) * HEAD_DIM)
            out_ref[:, sl] = jnp.broadcast_to(jnp.sum(prod[:, sl], axis=-1, keepdims=True), (tr, HEAD_DIM))

    return pl.pallas_call(
        body, name="da_rowdot", grid=(SEQ // tr,), in_specs=[_row_spec(tr, w)] * 2,
        out_specs=_row_spec(tr, w), out_shape=jax.ShapeDtypeStruct((SEQ, w), F32),
        compiler_params=_params(("parallel",), 10 * tr * w * 4),
    )(do, o)


def _da_bwd(proj, gq, gk, do, lse, dd, dproj, group):
    d = DA_DILATIONS[group]
    ld = SEQ // d
    nqb = ld // DA_QB
    ways = min(DA_WAYS, nqb)
    rways = _da_residue_ways(d, nqb)

    def body(q_ref, k_ref, v_ref, gq_ref, gk_ref, sl_ref, do_ref, lse_ref, dd_ref, dproj_in,
             dproj_ref, dgq_ref, dgk_ref,
             qn_scr, kn_scr, v_scr, dqn_scr, dkn_scr, dvr_scr, dq_scr, dk_scr, dv_scr, dq_out, dk_out, dv_out, sems):
        del dproj_in
        head = pl.program_id(0)
        gqv, gkv = gq_ref[...], gk_ref[...]
        slope = sl_ref[:, 0:1]

        copies = []
        for k, (staging, base) in enumerate(((dq_out, CB_DA_Q), (dk_out, CB_DA_K), (dv_out, CB_DA_V))):
            cols = pl.ds(pl.multiple_of((base + 4 * group + head) * LANE, LANE), LANE)
            copies.append(pltpu.make_async_copy(staging, dproj_ref.at[:, cols], sems.at[k]))

        def residues(t, carry):
            rs = [t * rways + u for u in range(rways)]
            sels = [_da_residue_rows(r, 0, ld, d) for r in rs]
            for u, sel in enumerate(sels):
                qn_scr[u] = _da_rmsnorm(q_ref[sel, :], gqv)[2].astype(BF16)
                kn_scr[u] = _da_rmsnorm(k_ref[sel, :], gkv)[2].astype(BF16)
                v_scr[u] = v_ref[sel, :].astype(BF16)
            dkn_scr[...] = jnp.zeros_like(dkn_scr)
            dvr_scr[...] = jnp.zeros_like(dvr_scr)

            def block(u, r, i):
                s, qb, kw, qrows, win = _da_scores(qn_scr.at[u], kn_scr.at[u], slope, i, ld)
                src = _da_residue_rows(r, i * DA_QB, DA_QB, d)
                p = jnp.exp(s - lse_ref[src, :][:, 0:1])
                dob = do_ref[src, :]
                dp = _dot(dob, v_scr[u, win, :], _NT)
                ds = p * (dp - dd_ref[src, :][:, 0:1]) * DA_SCALE
                return u, qrows, win, _dot(p, dob, _TN), _dot(ds, kw, _NN), _dot(ds, qb, _TN)

            def step(i, c2):
                todo = [(u, r, i + w * (nqb // ways)) for u, r in enumerate(rs) for w in range(ways)]
                for u, qrows, win, dv, dqn, dkn in [block(*t) for t in todo]:
                    dvr_scr[u, win, :] += dv
                    dqn_scr[u, qrows, :] = dqn
                    dkn_scr[u, win, :] += dkn
                return c2

            lax.fori_loop(0, nqb // ways, step, 0)

            pq, pk = carry
            for u, sel in enumerate(sels):
                parts = []
                for x_ref, gv, dn_scr, dx_scr in ((q_ref, gqv, dqn_scr, dq_scr), (k_ref, gkv, dkn_scr, dk_scr)):
                    hat, rstd, _ = _da_rmsnorm(x_ref[sel, :], gv)
                    dn = dn_scr[u]
                    dyg = dn * gv
                    dx_scr[sel, :] = rstd * (dyg - hat * jnp.mean(dyg * hat, axis=-1, keepdims=True))
                    parts.append(jnp.sum(dn * hat, axis=0, keepdims=True))
                dv_scr[sel, :] = dvr_scr[u]
                pq, pk = pq + parts[0], pk + parts[1]
            return pq, pk

        zero = jnp.zeros((1, HEAD_DIM), F32)
        pq, pk = lax.fori_loop(0, d // rways, residues, (zero, zero))

        @pl.when(pl.program_id(0) == 0)
        def _():
            dgq_ref[...] = pq
            dgk_ref[...] = pk

        @pl.when(pl.program_id(0) > 0)
        def _():
            dgq_ref[...] += pq
            dgk_ref[...] += pk

        @pl.when(head > 0)
        def _():
            for cp in copies:
                cp.wait()

        dq_out[...] = dq_scr[...].astype(BF16)
        dk_out[...] = dk_scr[...].astype(BF16)
        dv_out[...] = dv_scr[...].astype(BF16)
        for cp in copies:
            cp.start()

        @pl.when(head == DA_HEADS_PER_GROUP - 1)
        def _():
            for cp in copies:
                cp.wait()

    seq_spec = lambda base: pl.BlockSpec((SEQ, HEAD_DIM), lambda h: (0, base + 4 * group + h))
    out_spec = pl.BlockSpec((SEQ, HEAD_DIM), lambda h: (0, h))
    gain_spec = pl.BlockSpec((1, HEAD_DIM), lambda h: (0, 0))
    gshape = jax.ShapeDtypeStruct((1, HEAD_DIM), F32)
    blk = SEQ * HEAD_DIM * 4
    return pl.pallas_call(
        body, name=f"da_bwd_g{group}", grid=(DA_HEADS_PER_GROUP,),
        in_specs=[seq_spec(CB_DA_Q), seq_spec(CB_DA_K), seq_spec(CB_DA_V), gain_spec, gain_spec,
                  pl.BlockSpec((None, 1, LANE), lambda h: (h, 0, 0)), out_spec, out_spec, out_spec, _any_spec()],
        out_specs=[_any_spec(), gain_spec, gain_spec],
        out_shape=[jax.ShapeDtypeStruct((SEQ, IN_COLS), BF16), gshape, gshape],
        scratch_shapes=[pltpu.VMEM((rways, ld, HEAD_DIM), BF16)] * 3 + [pltpu.VMEM((rways, ld, HEAD_DIM), F32)] * 3
        + [pltpu.VMEM((SEQ, HEAD_DIM), F32)] * 3 + [pltpu.VMEM((SEQ, HEAD_DIM), BF16)] * 3
        + [pltpu.SemaphoreType.DMA((3,))],
        input_output_aliases={9: 0},
        compiler_params=_params(("arbitrary",), 12 * blk + 3 * blk + 3 * blk + 3 * blk + 2 * blk),
    )(proj, proj, proj, gq, gk, _da_slopes(group), do, lse, dd, dproj)


MEM_WAYS = 4


def _mem_probs(q, kn, gq):
    qhat, rq, qn = _da_rmsnorm(q, gq)
    s = _dot(qn, kn, _NT) * MEM_SCALE
    m = jnp.max(s, axis=-1, keepdims=True)
    e = jnp.exp(s - m)
    p = e / jnp.sum(e, axis=-1, keepdims=True)
    return p, qhat, rq, qn


def _mem_pieces(tq):
    rows = tq // MEM_WAYS
    return [slice(w * rows, (w + 1) * rows) for w in range(MEM_WAYS)]


def _mem_fwd(proj, kv, gq, gk, tq):
    def body(q_ref, k_ref, v_ref, gq_ref, gk_ref, o_ref):
        kn = _da_rmsnorm(k_ref[...], gk_ref[...])[2]
        v, gqv = v_ref[...], gq_ref[...]
        pieces = _mem_pieces(tq)
        outs = [_dot(_mem_probs(q_ref[rs, :], kn, gqv)[0], v, _NN) for rs in pieces]
        for rs, o in zip(pieces, outs):
            o_ref[rs, :] = o.astype(BF16)

    gain_spec = pl.BlockSpec((1, HEAD_DIM), lambda h, i: (0, 0))
    return pl.pallas_call(
        body, name="mem_fwd", grid=(MEM_HEADS, SEQ // tq),
        in_specs=[pl.BlockSpec((tq, HEAD_DIM), lambda h, i: (i, CB_MEM_Q + h)),
                  pl.BlockSpec((N_MEM, HEAD_DIM), lambda h, i: (0, h)),
                  pl.BlockSpec((N_MEM, HEAD_DIM), lambda h, i: (0, MEM_HEADS + h)), gain_spec, gain_spec],
        out_specs=pl.BlockSpec((tq, HEAD_DIM), lambda h, i: (i, h)),
        out_shape=jax.ShapeDtypeStruct((SEQ, MEM_WIDTH), BF16),
        compiler_params=_params(("parallel", "parallel"), 16 * tq * N_MEM * 4),
    )(proj, kv, kv, gq, gk)


def _mem_bwd(proj, kv, gq, gk, do, dproj, tq):
    nq = SEQ // tq

    def body(q_ref, k_ref, v_ref, gq_ref, gk_ref, do_ref, dproj_in, dq_ref, dk_ref, dv_ref, dgq_ref, dgk_ref, dkn_scr):
        del dproj_in
        h, i = pl.program_id(0), pl.program_id(1)
        gqv, gkv = gq_ref[...], gk_ref[...]
        khat, rk, kn = _da_rmsnorm(k_ref[...], gkv)
        v = v_ref[...]

        def piece(rs):
            p, qhat, rq, qn = _mem_probs(q_ref[rs, :], kn, gqv)
            dob = do_ref[rs, :]
            dp = _dot(dob, v, _NT)
            ds = p * (dp - jnp.sum(p * dp, axis=-1, keepdims=True)) * MEM_SCALE
            dqn = _dot(ds, kn, _NN)
            dyg = dqn * gqv
            dq = (rq * (dyg - qhat * jnp.mean(dyg * qhat, axis=-1, keepdims=True))).astype(BF16)
            return dq, _dot(p, dob, _TN), _dot(ds, qn, _TN), jnp.sum(dqn * qhat, axis=0, keepdims=True)

        pieces = _mem_pieces(tq)
        results = [piece(rs) for rs in pieces]
        for rs, res in zip(pieces, results):
            dq_ref[rs, :] = res[0]
        dvp = sum((res[1] for res in results[1:]), results[0][1])
        dknp = sum((res[2] for res in results[1:]), results[0][2])
        dgq_part = sum((res[3] for res in results[1:]), results[0][3])
        first = jnp.logical_and(h == 0, i == 0)

        @pl.when(first)
        def _():
            dgq_ref[...] = dgq_part

        @pl.when(jnp.logical_not(first))
        def _():
            dgq_ref[...] += dgq_part

        @pl.when(i == 0)
        def _():
            dv_ref[...] = dvp
            dkn_scr[...] = dknp

        @pl.when(i > 0)
        def _():
            dv_ref[...] += dvp
            dkn_scr[...] += dknp

        @pl.when(i == nq - 1)
        def _():
            dkn = dkn_scr[...]
            dkg = dkn * gkv
            dk_ref[...] = rk * (dkg - khat * jnp.mean(dkg * khat, axis=-1, keepdims=True))
            dgk_part = jnp.sum(dkn * khat, axis=0, keepdims=True)

            @pl.when(h == 0)
            def _():
                dgk_ref[...] = dgk_part

            @pl.when(h > 0)
            def _():
                dgk_ref[...] += dgk_part

    gain_spec = pl.BlockSpec((1, HEAD_DIM), lambda h, i: (0, 0))
    kvout = pl.BlockSpec((N_MEM, HEAD_DIM), lambda h, i: (0, h))
    return pl.pallas_call(
        body, name="mem_bwd", grid=(MEM_HEADS, nq),
        in_specs=[pl.BlockSpec((tq, HEAD_DIM), lambda h, i: (i, CB_MEM_Q + h)),
                  pl.BlockSpec((N_MEM, HEAD_DIM), lambda h, i: (0, h)),
                  pl.BlockSpec((N_MEM, HEAD_DIM), lambda h, i: (0, MEM_HEADS + h)), gain_spec, gain_spec,
                  pl.BlockSpec((tq, HEAD_DIM), lambda h, i: (i, h)), _any_spec()],
        out_specs=[pl.BlockSpec((tq, HEAD_DIM), lambda h, i: (i, CB_MEM_Q + h)), kvout, kvout, gain_spec, gain_spec],
        out_shape=[jax.ShapeDtypeStruct((SEQ, IN_COLS), BF16), jax.ShapeDtypeStruct((N_MEM, MEM_WIDTH), F32),
                   jax.ShapeDtypeStruct((N_MEM, MEM_WIDTH), F32), jax.ShapeDtypeStruct((1, HEAD_DIM), F32),
                   jax.ShapeDtypeStruct((1, HEAD_DIM), F32)],
        scratch_shapes=[pltpu.VMEM((N_MEM, HEAD_DIM), F32)], input_output_aliases={6: 0},
        compiler_params=_params(("arbitrary", "arbitrary"), 24 * tq * N_MEM * 4),
    )(proj, kv, kv, gq, gk, do, dproj)


def _mesh_position():
    return lax.axis_index("x"), lax.axis_index("y"), lax.axis_index("c")


def _any_spec():
    return pl.BlockSpec(memory_space=pl.ANY)


def _all_gather(shards):
    n = len(shards)

    def body(*refs):
        ins, outs = refs[:n], refs[n:2 * n]
        send_sems, recv_sems, local_sems = refs[2 * n:]
        x, y, c = _mesh_position()
        me, sibling = (x, y, c), (x, y, 1 - c)
        first = (jnp.where(c == 0, 1 - x, x), jnp.where(c == 0, y, 1 - y), c)
        second = (jnp.where(c == 0, x, 1 - x), jnp.where(c == 0, 1 - y, y), c)
        diagonal = (1 - x, 1 - y, c)

        def copy(w, k, block, to, src=None):
            px, py, pc = block
            rows = outs[w].at[4 * px + 2 * py + pc]
            return pltpu.make_async_remote_copy(
                src_ref=rows if src is None else src, dst_ref=rows,
                send_sem=send_sems.at[7 * w + k], recv_sem=recv_sems.at[7 * w + k],
                device_id=to, device_id_type=MESH)

        started = []
        for w in range(n):
            mine = pltpu.make_async_copy(ins[w], outs[w].at[4 * x + 2 * y + c], local_sems.at[w])
            mine.start()
            started.append(mine)
        sends = []
        for w in range(n):
            own = [copy(w, 0, me, sibling, src=ins[w]), copy(w, 1, me, first, src=ins[w]),
                   copy(w, 2, me, second, src=ins[w])]
            for cp in own:
                cp.start()
            sends += own
        for w in range(n):
            copy(w, 1, first, me).wait_recv()
            follow = [copy(w, 3, first, second), copy(w, 4, first, sibling)]
            for cp in follow:
                cp.start()
            copy(w, 2, second, me).wait_recv()
            follow.append(copy(w, 5, second, sibling))
            follow[-1].start()
            sends += follow
        for w in range(n):
            copy(w, 3, diagonal, me).wait_recv()
            passed = copy(w, 6, diagonal, sibling)
            passed.start()
            sends.append(passed)
        for w in range(n):
            for k in (0, 4, 5, 6):
                copy(w, k, sibling, me).wait_recv()
        for cp in sends:
            cp.wait_send()
        for mine in started:
            mine.wait()

    return pl.pallas_call(
        body, name="weights_all_gather",
        in_specs=[_any_spec()] * n, out_specs=[_any_spec()] * n,
        out_shape=[jax.ShapeDtypeStruct((N_DEV,) + s.shape, s.dtype) for s in shards],
        scratch_shapes=[pltpu.SemaphoreType.DMA((7 * n,)), pltpu.SemaphoreType.DMA((7 * n,)),
                        pltpu.SemaphoreType.DMA((n,))],
    )(*shards)


PROJ_ROWS = 512


def _gather_proj(shard, h):
    n_tiles = SEQ // PROJ_ROWS

    def body(shard_ref, h_ref, gath_ref, proj_ref, wbuf, obuf, send_sems, recv_sems, local_sem, wsem, osems):
        x, y, c = _mesh_position()
        me, sibling = (x, y, c), (x, y, 1 - c)
        first = (jnp.where(c == 0, 1 - x, x), jnp.where(c == 0, y, 1 - y), c)
        second = (jnp.where(c == 0, x, 1 - x), jnp.where(c == 0, 1 - y, y), c)
        diagonal = (1 - x, 1 - y, c)
        other_core = lambda dev: (dev[0], dev[1], 1 - c)
        index = lambda dev: 4 * dev[0] + 2 * dev[1] + dev[2]

        def copy(k, block, to, src=None):
            rows = gath_ref.at[index(block)]
            return pltpu.make_async_remote_copy(
                src_ref=rows if src is None else src, dst_ref=rows, send_sem=send_sems.at[k], recv_sem=recv_sems.at[k],
                device_id=to, device_id_type=MESH)

        def out_copy(slot, tile, block):
            rows = pl.ds(pl.multiple_of(tile * PROJ_ROWS, PROJ_ROWS), PROJ_ROWS)
            cols = pl.ds(pl.multiple_of(block * IN_SHARD, LANE), IN_SHARD)
            return pltpu.make_async_copy(obuf.at[slot], proj_ref.at[rows, cols], osems.at[slot])

        def project(block, weights, first_call):
            load = pltpu.make_async_copy(weights, wbuf, wsem)
            load.start()
            load.wait()

            def tile(i, carry):
                slot = i % 2

                @pl.when(i >= (2 if first_call else 0))
                def _():
                    out_copy(slot, i, block).wait()

                rows = pl.ds(pl.multiple_of(i * PROJ_ROWS, PROJ_ROWS), PROJ_ROWS)
                obuf[slot] = _dot(h_ref[rows, :], wbuf[...], _NN)
                out_copy(slot, i, block).start()
                return carry

            lax.fori_loop(0, n_tiles, tile, 0)

        mine = pltpu.make_async_copy(shard_ref, gath_ref.at[index(me)], local_sem)
        mine.start()
        sends = [copy(0, me, sibling, src=shard_ref), copy(1, me, first, src=shard_ref),
                 copy(2, me, second, src=shard_ref)]
        for cp in sends:
            cp.start()
        project(index(me), shard_ref, True)

        copy(0, sibling, me).wait_recv()
        project(index(sibling), gath_ref.at[index(sibling)], False)

        copy(1, first, me).wait_recv()
        follow = [copy(3, first, second), copy(4, first, sibling)]
        for cp in follow:
            cp.start()
        project(index(first), gath_ref.at[index(first)], False)

        copy(2, second, me).wait_recv()
        follow.append(copy(5, second, sibling))
        follow[-1].start()
        project(index(second), gath_ref.at[index(second)], False)

        copy(4, sibling, me).wait_recv()
        project(index(other_core(second)), gath_ref.at[index(other_core(second))], False)
        copy(5, sibling, me).wait_recv()
        project(index(other_core(first)), gath_ref.at[index(other_core(first))], False)

        copy(3, diagonal, me).wait_recv()
        follow.append(copy(6, diagonal, sibling))
        follow[-1].start()
        project(index(diagonal), gath_ref.at[index(diagonal)], False)
        copy(6, sibling, me).wait_recv()
        project(index(other_core(diagonal)), gath_ref.at[index(other_core(diagonal))], False)

        out_copy(0, 0, 0).wait()
        out_copy(1, 0, 0).wait()
        for cp in sends + follow:
            cp.wait_send()
        mine.wait()

    return pl.pallas_call(
        body, name="weights_gather_proj",
        in_specs=[_any_spec(), pl.BlockSpec(memory_space=pltpu.VMEM)], out_specs=[_any_spec(), _any_spec()],
        out_shape=[jax.ShapeDtypeStruct((N_DEV,) + shard.shape, BF16), jax.ShapeDtypeStruct((SEQ, IN_COLS), F32)],
        scratch_shapes=[pltpu.VMEM(shard.shape, BF16), pltpu.VMEM((2, PROJ_ROWS, IN_SHARD), F32),
                        pltpu.SemaphoreType.DMA((7,)), pltpu.SemaphoreType.DMA((7,)), pltpu.SemaphoreType.DMA,
                        pltpu.SemaphoreType.DMA, pltpu.SemaphoreType.DMA((2,))],
        compiler_params=pltpu.CompilerParams(vmem_limit_bytes=48 << 20),
    )(shard, h)


def _chip_of(j, x, y):
    return (1 - x if j & 1 else x, 1 - y if j & 2 else y)


_HBM_SPEC = pl.BlockSpec(memory_space=pltpu.HBM)
_SEM_SPEC = pl.BlockSpec(memory_space=pltpu.SEMAPHORE)
_DATAFLOW_EFFECT = pltpu.SideEffectType.DATAFLOW_SIDE_EFFECTING
TOKEN_SHAPE = (8, D_MODEL)


def _copies_start(name, arrays, n_copies, plan):
    n = len(arrays)

    def body(*refs):
        send_sems, recv_sems, token = refs[n], refs[n + 1], refs[2 * n + 2]
        copies = plan(refs[:n])
        assert len(copies) == n_copies
        for k, (src, dst, dev) in enumerate(copies):
            pltpu.make_async_remote_copy(src_ref=src, dst_ref=dst, send_sem=send_sems.at[k], recv_sem=recv_sems.at[k],
                                         device_id=dev, device_id_type=MESH).start()
        token[...] = jnp.zeros_like(token)

    outs = pl.pallas_call(
        body, name=name,
        out_shape=(pltpu.SemaphoreType.DMA((n_copies,)), pltpu.SemaphoreType.DMA((n_copies,)),
                   *[pltpu.HBM(a.shape, a.dtype) for a in arrays], jax.ShapeDtypeStruct(TOKEN_SHAPE, F32)),
        in_specs=[_HBM_SPEC] * n,
        out_specs=(_SEM_SPEC, _SEM_SPEC, *[_HBM_SPEC] * n, pl.BlockSpec(memory_space=pltpu.VMEM)),
        input_output_aliases={i: i + 2 for i in range(n)},
        compiler_params=pltpu.CompilerParams(has_side_effects=_DATAFLOW_EFFECT),
    )(*[pltpu.with_memory_space_constraint(a, pltpu.HBM) for a in arrays])
    return outs[0], outs[1], list(outs[2:2 + n]), outs[2 + n]


def _copies_wait(name, send_sems, recv_sems, arrays, n_copies, plan, after):
    n = len(arrays)
    after = list(after) if isinstance(after, (list, tuple)) else [after]

    def body(*refs):
        send_ref, recv_ref = refs[n], refs[n + 1]
        copies = plan(refs[:n])
        assert len(copies) == n_copies
        for k, (src, dst, dev) in enumerate(copies):
            cp = pltpu.make_async_remote_copy(src_ref=src, dst_ref=dst, send_sem=send_ref.at[k], recv_sem=recv_ref.at[k],
                                              device_id=dev, device_id_type=MESH)
            cp.wait_send()
            cp.wait_recv()

    outs = pl.pallas_call(
        body, name=name, out_shape=tuple(pltpu.HBM(a.shape, a.dtype) for a in arrays),
        in_specs=[_HBM_SPEC] * n + [_SEM_SPEC, _SEM_SPEC] + [pl.BlockSpec(memory_space=pl.ANY)] * len(after),
        out_specs=tuple([_HBM_SPEC] * n), input_output_aliases={i: i for i in range(n)},
        compiler_params=pltpu.CompilerParams(has_side_effects=_DATAFLOW_EFFECT),
    )(*arrays, send_sems, recv_sems, *after)
    return list(outs)


def _after(small, token):
    return small if token is None else small + token[0:1, :small.shape[-1]]


def _gather_plan_out(n):
    def plan(refs):
        x, y, c = _mesh_position()
        me = 4 * x + 2 * y + c
        copies = []
        for w in range(n):
            land = refs[n + w].at[me]
            copies.append((refs[w], land, (x, y, 1 - c)))
            for j in range(1, 4):
                copies.append((refs[w], land, (*_chip_of(j, x, y), c)))
        return copies
    return plan


def _gather_plan_pass(n):
    def plan(refs):
        x, y, c = _mesh_position()
        copies = []
        for w in range(n):
            for j in range(1, 4):
                px, py = _chip_of(j, x, y)
                rows = refs[w].at[4 * px + 2 * py + c]
                copies.append((rows, rows, (x, y, 1 - c)))
        return copies
    return plan


def _reduce_plan_sibling(n):
    def plan(refs):
        x, y, c = _mesh_position()
        copies = []
        for w in range(n):
            for j in range(4):
                px, py = _chip_of(j, x, y)
                copies.append((refs[w].at[4 * px + 2 * py + (1 - c)], refs[n + w].at[j], (x, y, 1 - c)))
        return copies
    return plan


def _reduce_plan_chips(n):
    def plan(refs):
        x, y, c = _mesh_position()
        copies = []
        for w in range(n):
            for j in range(1, 4):
                copies.append((refs[w].at[j - 1], refs[n + w].at[j - 1], (*_chip_of(j, x, y), c)))
        return copies
    return plan


def _chip_partials(grad, recv, name, tr):
    _, rows, width = grad.shape

    def body(g_ref, r_ref, own_ref, other_ref):
        x, y, c = _mesh_position()
        for j in range(4):
            px, py = _chip_of(j, x, y)
            total = g_ref[4 * px + 2 * py + c].astype(F32) + r_ref[j].astype(F32)
            if j == 0:
                own_ref[...] = total
            else:
                other_ref[j - 1] = total.astype(BF16)

    return pl.pallas_call(
        body, name=name, grid=(rows // tr,),
        in_specs=[pl.BlockSpec((N_DEV, tr, width), lambda i: (0, i, 0)),
                  pl.BlockSpec((4, tr, width), lambda i: (0, i, 0))],
        out_specs=[pl.BlockSpec((tr, width), lambda i: (i, 0)), pl.BlockSpec((3, tr, width), lambda i: (0, i, 0))],
        out_shape=[jax.ShapeDtypeStruct((rows, width), F32), jax.ShapeDtypeStruct((3, rows, width), BF16)],
        compiler_params=_params(("parallel",), 2 * 20 * tr * width * 2 + 8 * tr * width * 4),
    )(grad, recv)


def _adamw_math(w, g, m, v):
    m = ADAM_B1 * m + (1.0 - ADAM_B1) * g
    v = ADAM_B2 * v + (1.0 - ADAM_B2) * (g * g)
    m_hat = m / (1.0 - ADAM_B1 ** ADAM_STEP)
    v_hat = v / (1.0 - ADAM_B2 ** ADAM_STEP)
    delta = -ADAM_LR * (m_hat / (jnp.sqrt(v_hat) + ADAM_EPS) + ADAM_WD * w)
    return delta, m, v


def _adamw_shard(own, recv, w, m, v, name, tr):
    rows, width = own.shape

    def body(own_ref, r_ref, w_ref, m_ref, v_ref, g_ref, d_ref, nm_ref, nv_ref):
        g = own_ref[...]
        for j in range(3):
            g = g + r_ref[j].astype(F32)
        g_ref[...] = g
        d_ref[...], nm_ref[...], nv_ref[...] = _adamw_math(w_ref[...], g, m_ref[...], v_ref[...])

    spec = pl.BlockSpec((tr, width), lambda i: (i, 0))
    shape = jax.ShapeDtypeStruct((rows, width), F32)
    return pl.pallas_call(
        body, name=name, grid=(rows // tr,),
        in_specs=[spec, pl.BlockSpec((3, tr, width), lambda i: (0, i, 0)), spec, spec, spec],
        out_specs=[spec] * 4, out_shape=[shape] * 4,
        compiler_params=_params(("parallel",), 22 * tr * width * 4),
    )(own, recv, w, m, v)


def _small_all_reduce_adamw(gpart, lbpack, wpack, mpack, vpack, after):
    def body(gp_ref, lb_ref, w_ref, m_ref, v_ref, after_ref, g_ref, d_ref, nm_ref, nv_ref, gath_ref, send_sems, recv_sems):
        del after_ref
        x, y, c = _mesh_position()
        me = 4 * x + 2 * y + c
        gath_ref[me] = gp_ref[...]
        copies = []
        for j in range(1, N_DEV):
            peer = (x ^ (j >> 2), y ^ ((j >> 1) & 1), c ^ (j & 1))
            cp = pltpu.make_async_remote_copy(
                src_ref=gp_ref, dst_ref=gath_ref.at[me], send_sem=send_sems.at[j - 1], recv_sem=recv_sems.at[j - 1],
                device_id=peer, device_id_type=MESH)
            cp.start()
            copies.append(cp)
        for cp in copies:
            cp.wait()
        tot = gath_ref[0]
        for s in range(1, N_DEV):
            tot = tot + gath_ref[s]
        lb = lb_ref[...]
        dl = tot[16:32, :] * lb * (1.0 - lb)
        g = jnp.concatenate([tot[0:16, :], dl[0:8, :], -dl[0:8, :], dl[8:16, :], -dl[8:16, :],
                             tot[32:SMALL_GRAD_ROWS, :]], axis=0)
        g_ref[...] = g
        d_ref[...], nm_ref[...], nv_ref[...] = _adamw_math(w_ref[...], g, m_ref[...], v_ref[...])

    vm = pl.BlockSpec(memory_space=pltpu.VMEM)
    shape = jax.ShapeDtypeStruct((SMALL_ROWS, LANE), F32)
    return pl.pallas_call(
        body, name="small_all_reduce_adamw", in_specs=[vm] * 5 + [_any_spec()], out_specs=[vm] * 4,
        out_shape=[shape] * 4,
        scratch_shapes=[pltpu.VMEM((N_DEV, SMALL_GRAD_ROWS, LANE), F32),
                        pltpu.SemaphoreType.DMA((N_DEV - 1,)), pltpu.SemaphoreType.DMA((N_DEV - 1,))],
    )(gpart, lbpack, wpack, mpack, vpack, after)


_SMALL_NAMES = ("norm_mix_gain", "norm_mem_gain", "lb_logits_fw", "lb_logits_bw", "norm_ffn_gain",
                "hg_norm_gain", "da_q_gain", "da_k_gain", "mem_q_gain", "mem_k_gain")
_SMALL_ROW0 = {"norm_mix_gain": 0, "norm_mem_gain": 8, "lb_logits_fw": 16, "lb_logits_bw": 32, "norm_ffn_gain": 48,
               "hg_norm_gain": 56, "da_q_gain": 64, "da_k_gain": 72, "mem_q_gain": 80, "mem_k_gain": 88}
_LOSS_ROW = 96


def _pack_rows(parts, total_rows):
    rows = []
    for p in parts:
        r = p.reshape(-1, LANE)
        rows.append(jnp.pad(r, ((0, -r.shape[0] % 8), (0, 0))))
    used = sum(r.shape[0] for r in rows)
    if total_rows > used:
        rows.append(jnp.zeros((total_rows - used, LANE), F32))
    return jnp.concatenate(rows, axis=0)


def _unpack_small(pack, like):
    out = {}
    for name in _SMALL_NAMES:
        n = like[name].size // LANE
        r0 = _SMALL_ROW0[name]
        out[name] = pack[r0:r0 + n].reshape(like[name].shape)
    return out


def kernel(x, mem, norm_mix_gain, norm_mem_gain, w_in, lb_logits_fw, lb_logits_bw, hg_norm_gain, da_q_gain, da_k_gain, w_mem_kv, mem_q_gain, mem_k_gain, w_proj_hg, w_proj_da, w_proj_mem, w_out, norm_ffn_gain, w_ffn_in, w_ffn_out, loss_target, m_norm_mix_gain, m_norm_mem_gain, m_w_in, m_lb_logits_fw, m_lb_logits_bw, m_hg_norm_gain, m_da_q_gain, m_da_k_gain, m_w_mem_kv, m_mem_q_gain, m_mem_k_gain, m_w_proj_hg, m_w_proj_da, m_w_proj_mem, m_w_out, m_norm_ffn_gain, m_w_ffn_in, m_w_ffn_out, v_norm_mix_gain, v_norm_mem_gain, v_w_in, v_lb_logits_fw, v_lb_logits_bw, v_hg_norm_gain, v_da_q_gain, v_da_k_gain, v_w_mem_kv, v_mem_q_gain, v_mem_k_gain, v_w_proj_hg, v_w_proj_da, v_w_proj_mem, v_w_out, v_norm_ffn_gain, v_w_ffn_in, v_w_ffn_out):
    small_w = dict(norm_mix_gain=norm_mix_gain, norm_mem_gain=norm_mem_gain, lb_logits_fw=lb_logits_fw,
                   lb_logits_bw=lb_logits_bw, norm_ffn_gain=norm_ffn_gain, hg_norm_gain=hg_norm_gain,
                   da_q_gain=da_q_gain, da_k_gain=da_k_gain, mem_q_gain=mem_q_gain, mem_k_gain=mem_k_gain)
    small_m = dict(norm_mix_gain=m_norm_mix_gain, norm_mem_gain=m_norm_mem_gain, lb_logits_fw=m_lb_logits_fw,
                   lb_logits_bw=m_lb_logits_bw, norm_ffn_gain=m_norm_ffn_gain, hg_norm_gain=m_hg_norm_gain,
                   da_q_gain=m_da_q_gain, da_k_gain=m_da_k_gain, mem_q_gain=m_mem_q_gain, mem_k_gain=m_mem_k_gain)
    small_v = dict(norm_mix_gain=v_norm_mix_gain, norm_mem_gain=v_norm_mem_gain, lb_logits_fw=v_lb_logits_fw,
                   lb_logits_bw=v_lb_logits_bw, norm_ffn_gain=v_norm_ffn_gain, hg_norm_gain=v_hg_norm_gain,
                   da_q_gain=v_da_q_gain, da_k_gain=v_da_k_gain, mem_q_gain=v_mem_q_gain, mem_k_gain=v_mem_k_gain)
    big_w = dict(w_in=w_in[0], w_mem_kv=w_mem_kv[0], w_proj_hg=w_proj_hg[0], w_proj_da=w_proj_da[0],
                 w_proj_mem=w_proj_mem[0], w_out=w_out[0], w_ffn_in=w_ffn_in[0], w_ffn_out=w_ffn_out[0])
    big_m = dict(w_in=m_w_in[0], w_mem_kv=m_w_mem_kv[0], w_proj_hg=m_w_proj_hg[0], w_proj_da=m_w_proj_da[0],
                 w_proj_mem=m_w_proj_mem[0], w_out=m_w_out[0], w_ffn_in=m_w_ffn_in[0], w_ffn_out=m_w_ffn_out[0])
    big_v = dict(w_in=v_w_in[0], w_mem_kv=v_w_mem_kv[0], w_proj_hg=v_w_proj_hg[0], w_proj_da=v_w_proj_da[0],
                 w_proj_mem=v_w_proj_mem[0], w_out=v_w_out[0], w_ffn_in=v_w_ffn_in[0], w_ffn_out=v_w_ffn_out[0])

    row_tile = dict(w_in=128, w_mem_kv=128, w_proj_hg=128, w_proj_da=512, w_proj_mem=512, w_out=128,
                    w_ffn_in=128, w_ffn_out=352)
    rest = _BIG_NAMES[1:]
    shards = [big_w[n].astype(BF16) for n in rest]
    state = {}
    big_out = {}

    def reduce_start(group, stacked):
        names = tuple(stacked)
        arrays = [stacked[n] for n in names] + [lax.empty((4,) + stacked[n].shape[1:], BF16) for n in names]
        plan = _reduce_plan_sibling(len(names))
        send, recv, thru, token = _copies_start(f"grads_{group}_sibling_start", arrays, 4 * len(names), plan)
        state[group] = dict(names=names, plan=plan, send=send, recv=recv, arrays=thru)
        return token

    def reduce_middle(group, after):
        st = state[group]
        names, k = st["names"], len(st["names"])
        thru = _copies_wait(f"grads_{group}_sibling_wait", st["send"], st["recv"], st["arrays"], 4 * k, st["plan"], after)
        partials = [_chip_partials(thru[i], thru[k + i], f"chip_partials_{n}", row_tile[n]) for i, n in enumerate(names)]
        arrays = [p[1] for p in partials] + [lax.empty(p[1].shape, BF16) for p in partials]
        plan = _reduce_plan_chips(k)
        send, recv, thru2, token = _copies_start(f"grads_{group}_chips_start", arrays, 3 * k, plan)
        state[group] = dict(names=names, plan=plan, send=send, recv=recv, arrays=thru2, own=[p[0] for p in partials])
        return token

    def reduce_finish(group, after):
        st = state.pop(group)
        names, k = st["names"], len(st["names"])
        thru = _copies_wait(f"grads_{group}_chips_wait", st["send"], st["recv"], st["arrays"], 3 * k, st["plan"], after)
        for i, n in enumerate(names):
            big_out[n] = _adamw_shard(st["own"][i], thru[k + i], big_w[n], big_m[n], big_v[n], "adamw_" + n, row_tile[n])

    step = _local_step_stages(x[0], mem[0], loss_target[0], small_w)
    event, payload = next(step)
    local = None
    while True:
        reply = None
        if event == "gather_proj":
            reply = _gather_proj(big_w["w_in"].astype(BF16), payload)
        elif event == "begin":
            nr = len(rest)
            me = 4 * lax.axis_index("x") + 2 * lax.axis_index("y") + lax.axis_index("c")
            arrays = shards + [lax.dynamic_update_slice(lax.empty((N_DEV,) + s.shape, BF16), s[None], (me, 0, 0))
                               for s in shards]
            plan = _gather_plan_out(nr)
            send, recv, thru, reply = _copies_start("weights_rest_out_start", arrays, 4 * nr, plan)
            state["gather"] = dict(plan=plan, send=send, recv=recv, arrays=thru)
        elif event == "after_gla_fwd":
            st = state["gather"]
            nr = len(rest)
            thru = _copies_wait("weights_rest_out_wait", st["send"], st["recv"], st["arrays"], 4 * nr, st["plan"], payload)
            plan = _gather_plan_pass(nr)
            send, recv, lands, reply = _copies_start("weights_rest_pass_start", thru[nr:], 3 * nr, plan)
            state["gather"] = dict(plan=plan, send=send, recv=recv, arrays=lands)
        elif event == "need_weights":
            st = state.pop("gather")
            nr = len(rest)
            lands = _copies_wait("weights_rest_pass_wait", st["send"], st["recv"], st["arrays"], 3 * nr, st["plan"], payload)
            reply = dict(zip(rest, lands))
        elif event == "grads_ffn":
            reply = reduce_start("ffn", payload)
        elif event == "after_gate_merge_bwd":
            reply = reduce_middle("ffn", payload)
        elif event == "grads_mix":
            reply = reduce_start("mix", payload)
        elif event == "after_da_bwd_g0":
            reply = reduce_middle("mix", payload)
            reduce_finish("ffn", payload)
        elif event == "after_gla_bwd":
            reduce_finish("mix", payload)
        elif event == "grads_in":
            reply = reduce_start("in", payload)
        elif event == "after_proj_bwd_act_top":
            reply = reduce_middle("in", payload)
        elif event == "end":
            local = payload
            reduce_finish("in", [local["grad_x"]] + [big_out[n][0] for n in rest])
            break
        event, payload = step.send(reply)
    grad_x = local["grad_x"]

    wpack = _pack_rows([small_w[n] for n in _SMALL_NAMES], SMALL_ROWS)
    mpack = _pack_rows([small_m[n] for n in _SMALL_NAMES], SMALL_ROWS)
    vpack = _pack_rows([small_v[n] for n in _SMALL_NAMES], SMALL_ROWS)
    gs, ds, ms, vs = _small_all_reduce_adamw(local["gpart"], local["lbpack"], wpack, mpack, vpack, big_out["w_in"][0])
    loss = gs[_LOSS_ROW, 0]
    small_out = [_unpack_small(t, small_w) for t in (gs, ds, ms, vs)]

    order = ("norm_mix_gain", "norm_mem_gain", "w_in", "lb_logits_fw", "lb_logits_bw", "hg_norm_gain", "da_q_gain",
             "da_k_gain", "w_mem_kv", "mem_q_gain", "mem_k_gain", "w_proj_hg", "w_proj_da", "w_proj_mem", "w_out",
             "norm_ffn_gain", "w_ffn_in", "w_ffn_out")
    outs = [loss, grad_x[None]]
    for kind in range(4):
        for n in order:
            outs.append(big_out[n][kind][None] if n in big_out else small_out[kind][n])
    return tuple(outs)


_BIG_NAMES = ("w_in", "w_mem_kv", "w_proj_hg", "w_proj_da", "w_proj_mem", "w_out", "w_ffn_in", "w_ffn_out")


def _local_step(xs, mems, target, sw, wg):
    step = _local_step_stages(xs, mems, target, sw)
    stacked = {}
    event, payload = next(step)
    while event != "end":
        reply = None
        if event.startswith("grads_"):
            stacked.update(payload)
        elif event == "need_weights":
            reply = wg
        elif event == "gather_proj":
            reply = (wg["w_in"], _matmul(payload, wg["w_in"], "nn", F32, 1024, IN_SHARD, D_MODEL, "proj_fwd",
                                         b_stacked=True, n_outer=True))
        event, payload = step.send(reply)
    return dict(payload, stacked=stacked)


def _local_step_stages(xs, mems, target, sw):
    norm_mix_gain, norm_mem_gain, norm_ffn_gain = sw["norm_mix_gain"], sw["norm_mem_gain"], sw["norm_ffn_gain"]
    lb_logits_fw, lb_logits_bw, hg_norm_gain = sw["lb_logits_fw"], sw["lb_logits_bw"], sw["hg_norm_gain"]
    da_q_gain, da_k_gain, mem_q_gain, mem_k_gain = sw["da_q_gain"], sw["da_k_gain"], sw["mem_q_gain"], sw["mem_k_gain"]

    lb_fw = _lb_table(lb_logits_fw, "lb_table_fw")
    lb_bw = _lb_table(lb_logits_bw, "lb_table_bw")
    lb = jnp.concatenate([lb_fw, lb_bw], axis=0).reshape(2, HG_HEADS, 1, HEAD_DIM)
    h, h_t = _rmsnorm_fwd(xs, norm_mix_gain, "norm_mix_fwd", 512, transposed=True)
    win_st, proj = yield "gather_proj", h
    token = yield "begin", proj
    o_hg, o_pre, states, hg_qt, hg_decay = _gla_fwd(proj, lb, _after(hg_norm_gain, token))
    token = yield "after_gla_fwd", o_hg
    da = [_da_fwd(proj, _after(da_q_gain, token), da_k_gain, g) for g in range(3)]
    o_da, o_da32, lse_da = _da_merge([t[0] for t in da], [t[1] for t in da], 512)
    wg = yield "need_weights", o_da
    wkv = wg["w_mem_kv"].reshape(D_MODEL, 2 * MEM_WIDTH)
    wphg = wg["w_proj_hg"].reshape(D_MODEL, D_MODEL)
    wpda = jnp.transpose(wg["w_proj_da"], (1, 0, 2)).reshape(DA_WIDTH, D_MODEL)
    wpmem = jnp.transpose(wg["w_proj_mem"], (1, 0, 2)).reshape(MEM_WIDTH, D_MODEL)
    wout = wg["w_out"].reshape(D_MODEL, D_MODEL)
    wfin = jnp.transpose(wg["w_ffn_in"], (1, 0, 2)).reshape(D_MODEL, 2 * D_FF)
    wfout = wg["w_ffn_out"].reshape(D_FF, D_MODEL)
    mem_n = _rmsnorm_fwd(mems, norm_mem_gain, "norm_mem_fwd", N_MEM)
    kv = _matmul(mem_n, wkv, "nn", F32, N_MEM, 1024, D_MODEL, "mem_kv_fwd")
    o_mem = _mem_fwd(proj, kv, mem_q_gain, mem_k_gain, 1024)
    branch_w = (wphg, wpda, wpmem)
    merged, t_hg, t_da, t_mem = _branch_merge_fwd(proj, (o_hg, o_da, o_mem), branch_w, 512)
    x1, h2, h2_t = _residual_rmsnorm_fwd(xs, merged, wout, norm_ffn_gain, "out_norm_ffn_fwd", 512)
    ffn_a, ffn_b, act, act_t = _ffn_in_swiglu(h2, wfin, 1024, 1408)
    dy, dyb, loss_part = _ffn_out_loss_head(x1, act, wfout, target, 512)

    dab = _ffn_out_bwd_swiglu(dyb, wfout, ffn_a, ffn_b, 1024, 1408)
    g_wfout = _matmul(act_t, dyb, "nn", BF16, 1408, 1024, 2048, "ffn_out_bwd_w")
    g_wfin = _matmul(h2_t, dab, "nn", BF16, 1024, 1408, 2048, "ffn_in_bwd_w", b_parts=True)
    token = yield "grads_ffn", dict(
        w_ffn_in=jnp.transpose(g_wfin.reshape(D_MODEL, N_DEV, 2 * D_FF // N_DEV), (1, 0, 2)),
        w_ffn_out=g_wfout.reshape(N_DEV, D_FF // N_DEV, D_MODEL))
    dx1, dx1b, g_norm_ffn = _matmul_rmsnorm_bwd(dab, wfin, x1, dy, _after(norm_ffn_gain, token),
                                                "ffn_in_bwd_act_norm", 512, D_FF, a_parts=True)
    dmerged = _matmul(dx1b, wout, "nt", F32, 512, 1024, D_MODEL, "out_bwd_act")
    g_wout = _matmul(merged, dx1b, "tn", BF16, 1024, 1024, 2048, "out_bwd_w")
    dt_hg, dt_da, dt_mem, do_hg, do_da, do_mem, dproj = _gate_merge_bwd(
        proj, (t_hg, t_da, t_mem), branch_w, dmerged, _dproj_buffer(), 256)
    token = yield "after_gate_merge_bwd", dt_hg
    g_wphg = _matmul(o_hg, dt_hg, "tn", BF16, 1024, 1024, 2048, "proj_hg_bwd_w", after=token)
    g_wpda = _matmul(o_da, dt_da, "tn", BF16, DA_WIDTH, D_MODEL, 2048, "proj_da_bwd_w")
    g_wpmem = _matmul(o_mem, dt_mem, "tn", BF16, MEM_WIDTH, D_MODEL, 2048, "proj_mem_bwd_w")
    by_owner = lambda g: jnp.transpose(g.reshape(g.shape[0], N_DEV, D_MODEL // N_DEV), (1, 0, 2))
    g_wpda, g_wpmem = by_owner(g_wpda), by_owner(g_wpmem)

    dproj, dk_mem, dv_mem, g_mem_q, g_mem_k = _mem_bwd(proj, kv, mem_q_gain, mem_k_gain, do_mem, dproj, 1024)
    dkv = jnp.concatenate([dk_mem, dv_mem], axis=1).astype(BF16)
    g_wkv = _matmul(mem_n, dkv, "tn", BF16, 1024, 1024, N_MEM, "mem_kv_bwd_w")
    dmem_n = _matmul(dkv, wkv, "nt", F32, N_MEM, 1024, 1024, "mem_kv_bwd_act")
    g_norm_mem = _gain_grad(mems, dmem_n, "norm_mem_bwd")
    token = yield "grads_mix", dict(
        w_mem_kv=g_wkv.reshape(N_DEV, D_MODEL // N_DEV, 2 * MEM_WIDTH),
        w_proj_hg=g_wphg.reshape(N_DEV, D_MODEL // N_DEV, D_MODEL),
        w_proj_da=g_wpda, w_proj_mem=g_wpmem,
        w_out=g_wout.reshape(N_DEV, D_MODEL // N_DEV, D_MODEL))

    dd_da = _da_rowdot(do_da, o_da32, 512)
    dproj, g_da_q, g_da_k = _da_bwd(proj, _after(da_q_gain, token), da_k_gain, do_da, lse_da, dd_da, dproj, 0)
    token = yield "after_da_bwd_g0", g_da_q
    for g in (1, 2):
        dproj, gq_part, gk_part = _da_bwd(proj, _after(da_q_gain, token), da_k_gain, do_da, lse_da, dd_da, dproj, g)
        g_da_q, g_da_k = g_da_q + gq_part, g_da_k + gk_part

    dproj, dlb, g_hg_norm = _gla_bwd(proj, lb, hg_norm_gain, o_pre, states, hg_qt, hg_decay, do_hg, dproj)
    yield "after_gla_bwd", dlb

    g_win =_matmul(h_t, dproj, "nn", BF16, 1024, IN_SHARD, 2048, "proj_bwd_w", out_stacked=True)
    token = yield "grads_in", dict(w_in=g_win)
    grad_x, g_mix_top = _matmul_rmsnorm_bwd(dproj, win_st, xs, dx1, norm_mix_gain, "proj_bwd_act_norm_top", 1024,
                                            IN_SHARD, b_stacked=True, after=token, m_blocks=(0, 1), with_bf16=False)
    token = yield "after_proj_bwd_act_top", g_mix_top
    grad_x, g_mix_bottom = _matmul_rmsnorm_bwd(dproj, win_st, xs, dx1, norm_mix_gain, "proj_bwd_act_norm_bottom", 1024,
                                               IN_SHARD, b_stacked=True, after=token, m_blocks=(1, 3),
                                               out_into=grad_x, with_bf16=False)
    g_norm_mix = g_mix_top + g_mix_bottom

    gpart = _pack_rows([g_norm_mix, g_norm_mem, dlb[0], dlb[1], g_norm_ffn, g_hg_norm, g_da_q, g_da_k,
                        g_mem_q, g_mem_k, loss_part], SMALL_GRAD_ROWS)
    lbpack = jnp.concatenate([lb_fw.reshape(8, LANE), lb_bw.reshape(8, LANE)], axis=0)
    yield "end", dict(grad_x=grad_x, gpart=gpart, lbpack=lbpack)
```

```python
import numpy as np
import jax
import jax.numpy as jnp
from jax import lax
from jax.experimental import pallas as pl
from jax.experimental.pallas import tpu as pltpu

F32 = jnp.float32
BF16 = jnp.bfloat16
MESH = pl.DeviceIdType.MESH

SEQ = 4096
D_MODEL = 1024
N_DEV = 8
N_MEM = 256
RMS_EPS = 1e-6
NEG_INF = -1e30
LANE = 128
HEAD_DIM = 128
HG_HEADS = 8
HG_CHUNK = 64
HG_SCALE = HEAD_DIM ** -0.5
DA_DILATIONS = (1, 4, 16)
DA_RADIUS = 64
DA_HEADS_PER_GROUP = 4
DA_HEADS = 12
DA_WIDTH = 512
DA_SCALE = HEAD_DIM ** -0.5
DA_QB = 128
DA_WIN = 256
DA_WAYS = 4
DA_FWD_WAYS = 4
MEM_HEADS = 4
MEM_WIDTH = 512
MEM_SCALE = HEAD_DIM ** -0.5
D_FF = 2816
IN_COLS = 13312
IN_SHARD = IN_COLS // N_DEV
CB_HG_Q, CB_F, CB_HG_I, CB_HG_G = 0, 8, 24, 32
CB_DA_Q, CB_DA_K, CB_DA_V, CB_MEM_Q = 40, 52, 64, 76
ADAM_LR, ADAM_B1, ADAM_B2, ADAM_EPS, ADAM_WD, ADAM_STEP = 0.001, 0.9, 0.999, 1e-08, 0.01, 10
VMEM_BYTES_V7X = 64 * 1024 * 1024
SMALL_ROWS = 104
SMALL_GRAD_ROWS = 88

_NN = (((1,), (0,)), ((), ()))
_NT = (((1,), (1,)), ((), ()))
_TN = (((0,), (0,)), ((), ()))


def _dot(a, b, dims):
    return lax.dot_general(a.astype(BF16), b.astype(BF16), dims, preferred_element_type=F32)


def _sigmoid(x):
    return 0.5 * jnp.tanh(0.5 * x) + 0.5


def _params(semantics, est_bytes):
    limit = int(min(VMEM_BYTES_V7X - (6 << 20), max(56 << 20, est_bytes * 3 // 2)))
    return pltpu.CompilerParams(dimension_semantics=semantics, vmem_limit_bytes=limit)


def _nbytes(shape, dtype):
    return int(np.prod(shape)) * jnp.dtype(dtype).itemsize


def _alibi_slopes(n):
    return (2.0 ** (-8.0 * np.arange(1, n + 1) / n)).astype(np.float32)


def _matmul(a, b, mode, out_dtype, tm, tn, tk, name, b_stacked=False, out_stacked=False, n_outer=False, after=None,
            m_blocks=None, out_into=None, a_parts=False, b_parts=False):
    if a_parts:
        assert mode == "nt" and tk == a.shape[2]
        m, kdim = a.shape[1], a.shape[0] * a.shape[2]
    elif mode == "tn":
        kdim, m = a.shape
    else:
        m, kdim = a.shape
    if b_parts:
        assert mode == "nn" and not b_stacked and b.shape[2] % tn == 0
        n = b.shape[0] * b.shape[2]
    elif b_stacked:
        if mode == "nn":
            n = b.shape[0] * b.shape[2]
            assert tn == b.shape[2] and tk == kdim == b.shape[1]
        else:
            assert mode == "nt" and tk == b.shape[2] and b.shape[0] * tk == kdim
            n = b.shape[1]
    else:
        n = b.shape[0] if mode == "nt" else b.shape[1]
    assert m % tm == 0 and n % tn == 0 and kdim % tk == 0
    gm, gn, gk = m // tm, n // tn, kdim // tk
    i0 = 0
    if m_blocks is not None:
        assert mode != "tn" and not out_stacked
        i0, gm = m_blocks

    def ijk(f):
        if n_outer:
            return lambda j, i, k: f(i + i0, j, k)
        return lambda i, j, k: f(i + i0, j, k)

    if a_parts:
        a_spec = pl.BlockSpec((None, tm, tk), ijk(lambda i, j, k: (k, i, 0)))
    elif mode == "tn":
        a_spec = pl.BlockSpec((tk, tm), ijk(lambda i, j, k: (k, i)))
    else:
        a_spec = pl.BlockSpec((tm, tk), ijk(lambda i, j, k: (i, k)))
    if b_parts:
        per_part = b.shape[2] // tn
        b_spec = pl.BlockSpec((None, tk, tn), ijk(lambda i, j, k: (j // per_part, k, j % per_part)))
    elif b_stacked and mode == "nn":
        b_spec = pl.BlockSpec((None, tk, tn), ijk(lambda i, j, k: (j, 0, 0)))
    elif b_stacked:
        b_spec = pl.BlockSpec((None, tn, tk), ijk(lambda i, j, k: (k, j, 0)))
    elif mode == "nt":
        b_spec = pl.BlockSpec((tn, tk), ijk(lambda i, j, k: (j, k)))
    else:
        b_spec = pl.BlockSpec((tk, tn), ijk(lambda i, j, k: (k, j)))
    if out_stacked:
        assert tm == m
        out_shape = jax.ShapeDtypeStruct((gn, m, tn), out_dtype)
        o_spec = pl.BlockSpec((None, tm, tn), ijk(lambda i, j, k: (j, i, 0)))
    else:
        out_shape = jax.ShapeDtypeStruct((m, n), out_dtype)
        o_spec = pl.BlockSpec((tm, tn), ijk(lambda i, j, k: (i, j)))
    dims = {"nn": _NN, "nt": _NT, "tn": _TN}[mode]

    n_in = 2 + (after is not None) + (out_into is not None)

    def body(*refs):
        a_ref, b_ref, o_ref = refs[0], refs[1], refs[n_in]
        part = _dot(a_ref[...], b_ref[...], dims)
        if gk == 1:
            o_ref[...] = part.astype(out_dtype)
            return
        acc_ref = refs[-1]
        k = pl.program_id(2)

        @pl.when(k == 0)
        def _():
            acc_ref[...] = part

        @pl.when(jnp.logical_and(k > 0, k < gk - 1))
        def _():
            acc_ref[...] += part

        @pl.when(k == gk - 1)
        def _():
            o_ref[...] = (acc_ref[...] + part).astype(out_dtype)

    a_tile = _nbytes((tm, tk), a.dtype)
    b_tile = _nbytes((tk, tn), b.dtype)
    o_tile = _nbytes((tm, tn), out_dtype)
    est = 2 * (a_tile + b_tile + o_tile) + 3 * tm * tn * 4 + (a_tile + b_tile)
    grid = (gn, gm, gk) if n_outer else (gm, gn, gk)
    operands, in_specs = [a, b], [a_spec, b_spec]
    if after is not None:
        operands.append(after)
        in_specs.append(pl.BlockSpec(memory_space=pl.ANY))
    aliases = {}
    if out_into is not None:
        aliases = {len(operands): 0}
        operands.append(out_into)
        in_specs.append(pl.BlockSpec(memory_space=pl.ANY))
    return pl.pallas_call(
        body, name=name, grid=grid, in_specs=in_specs, out_specs=o_spec, out_shape=out_shape,
        scratch_shapes=[] if gk == 1 else [pltpu.VMEM((tm, tn), F32)], input_output_aliases=aliases,
        compiler_params=_params(("parallel", "parallel", "arbitrary"), est),
    )(*operands)


def _row_spec(tr, width, col_block=0):
    return pl.BlockSpec((tr, width), lambda i: (i, col_block))


def _bcast_spec(width):
    return pl.BlockSpec((1, width), lambda i: (0, 0))


def _col_spec(width, tr):
    return pl.BlockSpec((width, tr), lambda i: (0, i))


def _rmsnorm_fwd(x, gain, name, tr, transposed=False):
    rows, width = x.shape

    def body(x_ref, g_ref, o_ref, *t_ref):
        xv = x_ref[...]
        r = lax.rsqrt(jnp.mean(xv * xv, axis=-1, keepdims=True) + RMS_EPS)
        h = xv * r * g_ref[...]
        o_ref[...] = h.astype(BF16)
        if transposed:
            t_ref[0][...] = h.T.astype(BF16)

    out_specs, out_shape = [_row_spec(tr, width)], [jax.ShapeDtypeStruct((rows, width), BF16)]
    if transposed:
        out_specs.append(_col_spec(width, tr))
        out_shape.append(jax.ShapeDtypeStruct((width, rows), BF16))
    out = pl.pallas_call(
        body, name=name, grid=(rows // tr,), in_specs=[_row_spec(tr, width), _bcast_spec(width)],
        out_specs=out_specs, out_shape=out_shape,
        compiler_params=_params(("parallel",), 10 * tr * width * 4),
    )(x, gain)
    return out if transposed else out[0]


def _residual_rmsnorm_fwd(x, merged, wout, gain, name, tr):
    rows, width = x.shape

    def body(x_ref, m_ref, w_ref, g_ref, x1_ref, h_ref, ht_ref):
        xv = x_ref[...] + _dot(m_ref[...], w_ref[...], _NN)
        x1_ref[...] = xv
        r = lax.rsqrt(jnp.mean(xv * xv, axis=-1, keepdims=True) + RMS_EPS)
        h = xv * r * g_ref[...]
        h_ref[...] = h.astype(BF16)
        ht_ref[...] = h.T.astype(BF16)

    return pl.pallas_call(
        body, name=name, grid=(rows // tr,),
        in_specs=[_row_spec(tr, width), _row_spec(tr, merged.shape[1]),
                  pl.BlockSpec(wout.shape, lambda i: (0, 0)), _bcast_spec(width)],
        out_specs=[_row_spec(tr, width), _row_spec(tr, width), _col_spec(width, tr)],
        out_shape=[jax.ShapeDtypeStruct((rows, width), F32), jax.ShapeDtypeStruct((rows, width), BF16),
                   jax.ShapeDtypeStruct((width, rows), BF16)],
        compiler_params=_params(("parallel",), 14 * tr * width * 4),
    )(x, merged, wout, gain)


def _matmul_rmsnorm_bwd(a, b, x, dres, gain, name, tm, tk, a_parts=False, b_stacked=False, after=None,
                        m_blocks=None, out_into=None, with_bf16=True):
    width = x.shape[1]
    m = a.shape[1] if a_parts else a.shape[0]
    kdim = a.shape[0] * a.shape[2] if a_parts else a.shape[1]
    gk = kdim // tk
    i0, gm = (0, m // tm) if m_blocks is None else m_blocks
    n_in = 5 + (after is not None) + (out_into is not None)

    def body(*refs):
        a_ref, b_ref, x_ref, dres_ref, g_ref = refs[:5]
        outs = refs[n_in:]
        dx_ref, dg_ref, acc_ref = outs[0], outs[-2], outs[-1]
        i, k = pl.program_id(0), pl.program_id(1)
        part = _dot(a_ref[...], b_ref[...], _NT)

        @pl.when(k == 0)
        def _():
            acc_ref[...] = part

        @pl.when(jnp.logical_and(k > 0, k < gk - 1))
        def _():
            acc_ref[...] += part

        @pl.when(k == gk - 1)
        def _():
            dhv = acc_ref[...] + part if gk > 1 else part
            xv = x_ref[...]
            r = lax.rsqrt(jnp.mean(xv * xv, axis=-1, keepdims=True) + RMS_EPS)
            xhat = xv * r
            dyg = dhv * g_ref[...]
            dx = dres_ref[...] + r * (dyg - xhat * jnp.mean(dyg * xhat, axis=-1, keepdims=True))
            dx_ref[...] = dx
            if with_bf16:
                outs[1][...] = dx.astype(BF16)
            gpart = jnp.sum(dhv * xhat, axis=0, keepdims=True)

            @pl.when(i == 0)
            def _():
                dg_ref[...] = gpart

            @pl.when(i > 0)
            def _():
                dg_ref[...] += gpart

    rows = lambda width_: pl.BlockSpec((tm, width_), lambda i, k: (i + i0, 0))
    if a_parts:
        a_spec = pl.BlockSpec((None, tm, tk), lambda i, k: (k, i + i0, 0))
    else:
        a_spec = pl.BlockSpec((tm, tk), lambda i, k: (i + i0, k))
    if b_stacked:
        b_spec = pl.BlockSpec((None, width, tk), lambda i, k: (k, 0, 0))
    else:
        b_spec = pl.BlockSpec((width, tk), lambda i, k: (0, k))
    operands = [a, b, x, dres, gain]
    in_specs = [a_spec, b_spec, rows(width), rows(width), pl.BlockSpec((1, width), lambda i, k: (0, 0))]
    for extra in (after, out_into):
        if extra is not None:
            operands.append(extra)
            in_specs.append(pl.BlockSpec(memory_space=pl.ANY))
    aliases = {} if out_into is None else {len(operands) - 1: 0}
    out_specs = [rows(width)] + ([rows(width)] if with_bf16 else []) + [pl.BlockSpec((1, width), lambda i, k: (0, 0))]
    out_shape = [jax.ShapeDtypeStruct((m, width), F32)] + ([jax.ShapeDtypeStruct((m, width), BF16)] if with_bf16 else [])
    out_shape.append(jax.ShapeDtypeStruct((1, width), F32))
    est = 4 * tm * tk + 4 * width * tk + 12 * tm * width * 4
    return pl.pallas_call(
        body, name=name, grid=(gm, gk), in_specs=in_specs, out_specs=out_specs, out_shape=out_shape,
        scratch_shapes=[pltpu.VMEM((tm, width), F32)], input_output_aliases=aliases,
        compiler_params=_params(("arbitrary", "arbitrary"), est),
    )(*operands)


def _gain_grad(x, dh, name):
    rows, width = x.shape

    def body(x_ref, dh_ref, dg_ref):
        xv = x_ref[...]
        r = lax.rsqrt(jnp.mean(xv * xv, axis=-1, keepdims=True) + RMS_EPS)
        dg_ref[...] = jnp.sum(dh_ref[...] * xv * r, axis=0, keepdims=True)

    return pl.pallas_call(
        body, name=name, grid=(1,), in_specs=[_row_spec(rows, width), _row_spec(rows, width)],
        out_specs=_bcast_spec(width), out_shape=jax.ShapeDtypeStruct((1, width), F32),
        compiler_params=_params(("arbitrary",), 6 * rows * width * 4),
    )(x, dh)


def _lb_table(logits, name):
    slots, width = logits.shape

    def body(l_ref, o_ref):
        lv = l_ref[...]
        mx = jnp.max(lv, axis=0, keepdims=True)
        e = jnp.exp(lv - mx)
        o_ref[...] = e[0:1, :] / jnp.sum(e, axis=0, keepdims=True)

    return pl.pallas_call(
        body, name=name, grid=(1,), in_specs=[pl.BlockSpec((slots, width), lambda i: (0, 0))],
        out_specs=_bcast_spec(width), out_shape=jax.ShapeDtypeStruct((1, width), F32),
    )(logits)


def _branch_merge_fwd(proj, outs, weights, tr):
    w = D_MODEL

    def body(ghg_ref, gda_ref, gmem_ref, ohg_ref, oda_ref, omem_ref, whg_ref, wda_ref, wmem_ref,
             m_ref, thg_ref, tda_ref, tmem_ref):
        acc = None
        for g_ref, o_ref, w_ref, t_ref in ((ghg_ref, ohg_ref, whg_ref, thg_ref), (gda_ref, oda_ref, wda_ref, tda_ref),
                                           (gmem_ref, omem_ref, wmem_ref, tmem_ref)):
            t = _dot(o_ref[...], w_ref[...], _NN)
            t_ref[...] = t.astype(BF16)
            term = _sigmoid(g_ref[...]) * t
            acc = term if acc is None else acc + term
        m_ref[...] = acc.astype(BF16)

    whole = lambda a: pl.BlockSpec(a.shape, lambda i: (0, 0))
    shape = jax.ShapeDtypeStruct((SEQ, w), BF16)
    return pl.pallas_call(
        body, name="branch_merge_fwd", grid=(SEQ // tr,),
        in_specs=[_row_spec(tr, w, 10), _row_spec(tr, w, 11), _row_spec(tr, w, 12)]
        + [_row_spec(tr, o.shape[1]) for o in outs] + [whole(wt) for wt in weights],
        out_specs=[_row_spec(tr, w)] * 4, out_shape=[shape] * 4,
        compiler_params=_params(("parallel",), 20 * tr * w * 4),
    )(proj, proj, proj, *outs, *weights)


def _dproj_buffer():
    return lax.empty((SEQ, IN_COLS), BF16)


def _gate_merge_bwd(proj, ts, weights, dmerged, dproj, tr):
    w = D_MODEL
    steps = SEQ // tr
    gate_col0 = 10 * w

    def body(ghg_ref, gda_ref, gmem_ref, thg_ref, tda_ref, tmem_ref, whg_ref, wda_ref, wmem_ref, dm_ref, dproj_in,
             dthg_ref, dtda_ref, dtmem_ref, dohg_ref, doda_ref, domem_ref, dproj_ref, stage, sems):
        del dproj_in
        i = pl.program_id(0)
        slot = i % 2

        def slot_copy(s):
            rows = pl.ds(pl.multiple_of(i * tr, tr), tr)
            return pltpu.make_async_copy(stage.at[s], dproj_ref.at[rows, pl.ds(gate_col0, 3 * w)], sems.at[s])

        @pl.when(i >= 2)
        def _():
            slot_copy(slot).wait()

        dm = dm_ref[...]
        branches = ((ghg_ref, thg_ref, whg_ref, dthg_ref, dohg_ref), (gda_ref, tda_ref, wda_ref, dtda_ref, doda_ref),
                    (gmem_ref, tmem_ref, wmem_ref, dtmem_ref, domem_ref))
        for b, (g_ref, t_ref, w_ref, dt_ref, do_ref) in enumerate(branches):
            s = _sigmoid(g_ref[...])
            dt = (s * dm).astype(BF16)
            dt_ref[...] = dt
            do_ref[...] = _dot(dt, w_ref[...], _NT)
            stage[slot, :, b * w:(b + 1) * w] = (dm * t_ref[...].astype(F32) * s * (1.0 - s)).astype(BF16)
        slot_copy(slot).start()

        @pl.when(i == steps - 1)
        def _():
            slot_copy(1 - slot).wait()
            slot_copy(slot).wait()

    assert steps >= 2
    whole = lambda a: pl.BlockSpec(a.shape, lambda i: (0, 0))
    return pl.pallas_call(
        body, name="gate_merge_bwd", grid=(steps,),
        in_specs=[_row_spec(tr, w, 10), _row_spec(tr, w, 11), _row_spec(tr, w, 12)] + [_row_spec(tr, w)] * 3
        + [whole(wt) for wt in weights] + [_row_spec(tr, w), _any_spec()],
        out_specs=[_row_spec(tr, w)] * 3 + [_row_spec(tr, wt.shape[0]) for wt in weights] + [_any_spec()],
        out_shape=[jax.ShapeDtypeStruct((SEQ, w), BF16)] * 3
        + [jax.ShapeDtypeStruct((SEQ, wt.shape[0]), F32) for wt in weights]
        + [jax.ShapeDtypeStruct((SEQ, IN_COLS), BF16)],
        scratch_shapes=[pltpu.VMEM((2, tr, 3 * w), BF16), pltpu.SemaphoreType.DMA((2,))],
        input_output_aliases={10: 6},
        compiler_params=_params(("arbitrary",), 34 * tr * w * 4),
    )(proj, proj, proj, *ts, *weights, dmerged, dproj)


def _ffn_in_swiglu(h2, wfin, tm, tn):
    gm, gn = SEQ // tm, D_FF // tn

    def body(h_ref, wa_ref, wb_ref, a_ref, b_ref, act_ref, actt_ref):
        h = h_ref[...]
        a = _dot(h, wa_ref[...], _NN)
        b = _dot(h, wb_ref[...], _NN)
        act = a * _sigmoid(a) * b
        a_ref[...] = a.astype(BF16)
        b_ref[...] = b.astype(BF16)
        act_ref[...] = act.astype(BF16)
        actt_ref[...] = act.T.astype(BF16)

    tile = pl.BlockSpec((tm, tn), lambda j, i: (i, j))
    shape = jax.ShapeDtypeStruct((SEQ, D_FF), BF16)
    return pl.pallas_call(
        body, name="ffn_in_swiglu_fwd", grid=(gn, gm),
        in_specs=[pl.BlockSpec((tm, D_MODEL), lambda j, i: (i, 0)),
                  pl.BlockSpec((D_MODEL, tn), lambda j, i: (0, j)),
                  pl.BlockSpec((D_MODEL, tn), lambda j, i: (0, gn + j))],
        out_specs=[tile, tile, tile, pl.BlockSpec((tn, tm), lambda j, i: (j, i))],
        out_shape=[shape, shape, shape, jax.ShapeDtypeStruct((D_FF, SEQ), BF16)],
        compiler_params=_params(("parallel", "parallel"), 4 * tm * D_MODEL + 8 * D_MODEL * tn + 16 * tm * tn
                                + 6 * tm * tn * 4),
    )(h2, wfin, wfin)


def _ffn_out_bwd_swiglu(dyb, wfout, a, b, tm, tn):
    gm, gn = SEQ // tm, D_FF // tn

    def body(dy_ref, w_ref, a_ref, b_ref, o_ref):
        d = _dot(dy_ref[...], w_ref[...], _NT)
        av = a_ref[...].astype(F32)
        bv = b_ref[...].astype(F32)
        s = _sigmoid(av)
        silu = av * s
        o_ref[0] = (d * bv * (s + silu * (1.0 - s))).astype(BF16)
        o_ref[1] = (d * silu).astype(BF16)

    tile = pl.BlockSpec((tm, tn), lambda i, j: (i, j))
    return pl.pallas_call(
        body, name="ffn_out_bwd_swiglu", grid=(gm, gn),
        in_specs=[pl.BlockSpec((tm, D_MODEL), lambda i, j: (i, 0)), pl.BlockSpec((tn, D_MODEL), lambda i, j: (j, 0)),
                  tile, tile],
        out_specs=pl.BlockSpec((2, tm, tn), lambda i, j: (0, i, j)),
        out_shape=jax.ShapeDtypeStruct((2, SEQ, D_FF), BF16),
        compiler_params=_params(("parallel", "parallel"), 4 * tm * D_MODEL + 4 * tn * D_MODEL + 16 * tm * tn
                                + 6 * tm * tn * 4),
    )(dyb, wfout, a, b)


def _ffn_out_loss_head(x1, act, wfout, target, tr):
    w = D_MODEL

    def body(x_ref, a_ref, w_ref, t_ref, dy_ref, dyb_ref, loss_ref, acc_ref):
        err = x_ref[...] + _dot(a_ref[...], w_ref[...], _NN) - t_ref[...]
        dy = err * (1.0 / w)
        dy_ref[...] = dy
        dyb_ref[...] = dy.astype(BF16)
        part = jnp.sum(err * err, axis=0, keepdims=True)

        @pl.when(pl.program_id(0) == 0)
        def _():
            acc_ref[...] = part

        @pl.when(pl.program_id(0) > 0)
        def _():
            acc_ref[...] += part

        @pl.when(pl.program_id(0) == SEQ // tr - 1)
        def _():
            total = jnp.sum(acc_ref[...], axis=1, keepdims=True) * (0.5 / w)
            loss_ref[...] = jnp.broadcast_to(total, (1, LANE))

    return pl.pallas_call(
        body, name="ffn_out_loss_head", grid=(SEQ // tr,),
        in_specs=[_row_spec(tr, w), _row_spec(tr, D_FF), pl.BlockSpec((D_FF, w), lambda i: (0, 0)), _row_spec(tr, w)],
        out_specs=[_row_spec(tr, w), _row_spec(tr, w), _bcast_spec(LANE)],
        out_shape=[jax.ShapeDtypeStruct((SEQ, w), F32), jax.ShapeDtypeStruct((SEQ, w), BF16),
                   jax.ShapeDtypeStruct((1, LANE), F32)],
        scratch_shapes=[pltpu.VMEM((1, w), F32)],
        compiler_params=_params(("arbitrary",), 12 * tr * w * 4 + 4 * D_FF * w),
    )(x1, act, wfout, target)


GLA_ROWS = 256
GLA_CPB = GLA_ROWS // HG_CHUNK
GLA_NBLK = SEQ // GLA_ROWS
GLA_WAYS = 4
GLA_TRIPS = GLA_NBLK // GLA_WAYS
GLA_GRAD_WAYS = 4
GLA_GRAD_TRIPS = GLA_NBLK // GLA_GRAD_WAYS
GLA_NCK = SEQ // HG_CHUNK
GLA_INTER_WAYS = 8


def _dot_split(m, xv, dims, terms=3):
    dot = lambda t: lax.dot_general(m, t, dims, preferred_element_type=F32)
    hi = xv.astype(BF16)
    r1 = xv - hi.astype(F32)
    mid = r1.astype(BF16)
    if terms == 2:
        return dot(mid) + dot(hi)
    lo = (r1 - mid.astype(F32)).astype(BF16)
    return (dot(lo) + dot(mid)) + dot(hi)


def _gla_block(qc, fc, lbv, masks):
    mask, maskb, direction = masks
    sq = _sigmoid(qc)
    q = qc * sq * HG_SCALE
    sf = _sigmoid(fc)
    forget = lbv + (1.0 - lbv) * sf
    k = 1.0 - forget
    logf = jnp.log(forget)
    b = _dot_split(maskb, logf, _NN)
    ends = []
    for j in range(GLA_CPB):
        lo, hi = j * HG_CHUNK, (j + 1) * HG_CHUNK
        end = jnp.where(direction == 0, b[hi - 1:hi, :], b[lo:lo + 1, :])
        ends.append(jnp.broadcast_to(end, (HG_CHUNK, HEAD_DIM)))
    bt = jnp.concatenate(ends, axis=0)
    eb = jnp.exp(b)
    qt = q * eb
    kt = k * jnp.exp(-b)
    kh = k * jnp.exp(bt - b)
    a = jnp.where(mask, _dot(qt, kt, _NT), 0.0)
    return dict(sq=sq, sf=sf, forget=forget, k=k, b=b, bt=bt, eb=eb, qt=qt, kt=kt, kh=kh, a=a)


def _gla_masks(direction):
    row = lax.broadcasted_iota(jnp.int32, (GLA_ROWS, GLA_ROWS), 0)
    col = lax.broadcasted_iota(jnp.int32, (GLA_ROWS, GLA_ROWS), 1)
    same = (row // HG_CHUNK) == (col // HG_CHUNK)
    mask = jnp.logical_and(same, jnp.where(direction == 0, row - col, col - row) >= 0)
    return mask, jnp.where(mask, 1.0, 0.0).astype(BF16), direction


def _gla_chunk_rows(j):
    return slice(j * HG_CHUNK, (j + 1) * HG_CHUNK)


def _gla_block_rows(b):
    return pl.ds(pl.multiple_of(b * GLA_ROWS, GLA_ROWS), GLA_ROWS)


def _head_spec(col_block0):
    return pl.BlockSpec((SEQ, HEAD_DIM), lambda h, d: (0, col_block0 + h))


def _gla_fwd(proj, lb, gain):
    nck = GLA_NCK

    def body(q_ref, f_ref, v_ref, g_ref, lb_ref, gain_ref, ohg_ref, opre_ref, st_ref, qt_scr, dec_scr, cs_scr):
        d = pl.program_id(1)
        masks = _gla_masks(d)
        lbv = lb_ref[...]

        @pl.when(d == 0)
        def _():
            opre_ref[...] = jnp.zeros_like(opre_ref)

        def intra(s, carry):
            blocks = [s + w * GLA_TRIPS for w in range(GLA_WAYS)]
            rows = [_gla_block_rows(b) for b in blocks]
            loaded = [(q_ref[r, :], f_ref[r, :], v_ref[r, :], opre_ref[r, :]) for r in rows]
            results = []
            for qc, fc, v, o_prev in loaded:
                ck = _gla_block(qc, fc, lbv, masks)
                o = o_prev + _dot(ck["a"], v, _NN)
                cs = [_dot(v[_gla_chunk_rows(j), :], ck["kh"][_gla_chunk_rows(j), :], _TN) for j in range(GLA_CPB)]
                dec = [jnp.exp(ck["bt"][j * HG_CHUNK:j * HG_CHUNK + 8, :]) for j in range(GLA_CPB)]
                results.append((o, ck["qt"].astype(BF16), cs, dec))
            for b, r, (o, qt, cs, dec) in zip(blocks, rows, results):
                opre_ref[r, :] = o
                qt_scr[r, :] = qt
                for j in range(GLA_CPB):
                    cs_scr[b * GLA_CPB + j] = cs[j]
                    dec_scr[b * GLA_CPB + j] = dec[j]
            return carry

        lax.fori_loop(0, GLA_TRIPS, intra, 0)

        def scan(i, st):
            c = jnp.where(d == 0, i, nck - 1 - i)
            st_ref[c] = st
            return st * dec_scr[c][0:1, :] + cs_scr[c]

        lax.fori_loop(0, nck, scan, jnp.zeros((HEAD_DIM, HEAD_DIM), F32), unroll=4)

        def inter(t, carry):
            chunks = [t + w * (nck // GLA_INTER_WAYS) for w in range(GLA_INTER_WAYS)]
            rows = [pl.ds(pl.multiple_of(c * HG_CHUNK, HG_CHUNK), HG_CHUNK) for c in chunks]
            loaded = [(opre_ref[r, :], qt_scr[r, :], st_ref[c]) for r, c in zip(rows, chunks)]
            for r, o in zip(rows, [o_prev + _dot(qt, st, _NT) for o_prev, qt, st in loaded]):
                opre_ref[r, :] = o
            return carry

        lax.fori_loop(0, nck // GLA_INTER_WAYS, inter, 0)

        @pl.when(d == 1)
        def _():
            o = opre_ref[...]
            r = lax.rsqrt(jnp.mean(o * o, axis=-1, keepdims=True) + RMS_EPS)
            g = g_ref[...]
            ohg_ref[...] = (o * r * gain_ref[...] * (g * _sigmoid(g))).astype(BF16)

    blk = SEQ * HEAD_DIM * 4
    return pl.pallas_call(
        body, name="gla_fwd", grid=(HG_HEADS, 2),
        in_specs=[_head_spec(CB_HG_Q),
                  pl.BlockSpec((SEQ, HEAD_DIM), lambda h, d: (0, CB_F + 8 * d + h)),
                  _head_spec(CB_HG_I), _head_spec(CB_HG_G),
                  pl.BlockSpec((None, None, 1, HEAD_DIM), lambda h, d: (d, h, 0, 0)),
                  pl.BlockSpec((1, HEAD_DIM), lambda h, d: (0, 0))],
        out_specs=[_head_spec(0), _head_spec(0),
                   pl.BlockSpec((None, None, nck, HEAD_DIM, HEAD_DIM), lambda h, d: (h, d, 0, 0, 0)),
                   pl.BlockSpec((None, SEQ, HEAD_DIM), lambda h, d: (d, 0, h)),
                   pl.BlockSpec((None, None, nck, 8, HEAD_DIM), lambda h, d: (h, d, 0, 0, 0))],
        out_shape=[jax.ShapeDtypeStruct((SEQ, D_MODEL), BF16), jax.ShapeDtypeStruct((SEQ, D_MODEL), F32),
                   jax.ShapeDtypeStruct((HG_HEADS, 2, nck, HEAD_DIM, HEAD_DIM), F32),
                   jax.ShapeDtypeStruct((2, SEQ, D_MODEL), BF16),
                   jax.ShapeDtypeStruct((HG_HEADS, 2, nck, 8, HEAD_DIM), F32)],
        scratch_shapes=[pltpu.VMEM((nck, HEAD_DIM, HEAD_DIM), F32)],
        compiler_params=_params(("parallel", "arbitrary"), 8 * blk + 3 * blk + 2 * blk + 2 * blk + blk),
    )(proj, proj, proj, proj, lb, gain)


def _gla_bwd(proj, lb, gain, o_pre, states, qt_all, dec_all, do_hg, dproj):
    nck = GLA_NCK
    do, dproj, dgain = _gla_bwd_norm(proj, gain, o_pre, do_hg, dproj, 256)

    def body(q_ref, f_ref, v_ref, lb_ref, do_ref, st_ref, qt_ref, dec_scr, dproj_in, dproj_ref, dlb_ref,
             dq_acc, dv_acc, dst_scr, cs_scr, df_out, dq_out, dv_out, sems):
        del dproj_in
        h, d = pl.program_id(0), pl.program_id(1)
        masks = _gla_masks(d)
        mask, maskb, _ = masks
        lbv = lb_ref[...]

        def column_copy(k, staging, col_block):
            cols = pl.ds(pl.multiple_of(col_block * LANE, LANE), LANE)
            return pltpu.make_async_copy(staging, dproj_ref.at[:, cols], sems.at[k])

        df_copy = column_copy(0, df_out, CB_F + 8 * d + h)
        dq_copy = column_copy(1, dq_out, CB_HG_Q + h)
        dv_copy = column_copy(2, dv_out, CB_HG_I + h)
        last = jnp.logical_and(h == HG_HEADS - 1, d == 1)

        def intra(s, carry):
            blocks = [s + w * GLA_TRIPS for w in range(GLA_WAYS)]
            loaded = [(qt_ref[r, :], do_ref[r, :]) for r in map(_gla_block_rows, blocks)]
            results = [[_dot(doc[_gla_chunk_rows(j), :], qt[_gla_chunk_rows(j), :], _TN) for j in range(GLA_CPB)]
                       for qt, doc in loaded]
            for b, cs in zip(blocks, results):
                for j in range(GLA_CPB):
                    cs_scr[b * GLA_CPB + j] = cs[j]
            return carry

        lax.fori_loop(0, GLA_TRIPS, intra, 0)

        def scan(i, dst):
            c = jnp.where(d == 0, nck - 1 - i, i)
            dst_scr[c] = dst
            return dst * dec_scr[c][0:1, :] + cs_scr[c]

        lax.fori_loop(0, nck, scan, jnp.zeros((HEAD_DIM, HEAD_DIM), F32), unroll=4)

        @pl.when(d == 0)
        def _():
            dq_acc[...] = jnp.zeros_like(dq_acc)
            dv_acc[...] = jnp.zeros_like(dv_acc)

        def block_grads(qc, fc, v, doc, states_in, dstates, decays):
            ck = _gla_block(qc, fc, lbv, masks)
            qt, kt, kh, a = ck["qt"], ck["kt"], ck["kh"], ck["a"]
            da = jnp.where(mask, _dot(doc, v, _NT), 0.0)
            dqt_i = _dot(da, kt, _NN)
            dkt = _dot(da, qt, _TN)
            dv_i = _dot(a, doc, _TN)
            dqt_p, dv_p, dkh_p, dbt_p = [], [], [], []
            for j in range(GLA_CPB):
                cr = _gla_chunk_rows(j)
                st_in, dst = states_in[j], dstates[j]
                dqt_p.append(dqt_i[cr, :] + _dot(doc[cr, :], st_in, _NN))
                dv_p.append(dv_i[cr, :] + _dot(kh[cr, :], dst, _NT))
                dkh_j = _dot(v[cr, :], dst, _NN)
                dkh_p.append(dkh_j)
                dbt_j = (decays[j][0:1, :] * jnp.sum(dst * st_in, axis=0, keepdims=True)
                         + jnp.sum(dkh_j * kh[cr, :], axis=0, keepdims=True))
                dbt_p.append(jnp.broadcast_to(dbt_j, (HG_CHUNK, HEAD_DIM)))
            dqt = jnp.concatenate(dqt_p, axis=0)
            dv = jnp.concatenate(dv_p, axis=0)
            dkh = jnp.concatenate(dkh_p, axis=0)
            dbt = jnp.concatenate(dbt_p, axis=0)
            db = dqt * qt - dkt * kt - dkh * kh
            dq = dqt * ck["eb"]
            dk = dkt * jnp.exp(-ck["b"]) + dkh * jnp.exp(ck["bt"] - ck["b"])
            dlogf = _dot_split(maskb, db, _TN, terms=2) + dbt
            dforget = dlogf / ck["forget"] - dk
            sf, sq = ck["sf"], ck["sq"]
            df = (dforget * (1.0 - lbv) * sf * (1.0 - sf)).astype(BF16)
            dqc = dq * HG_SCALE * (sq + qc * sq * (1.0 - sq))
            return df, dqc, dv, jnp.sum(dforget * (1.0 - sf), axis=0, keepdims=True)

        def grads(s, dlb):
            blocks = [s + w * GLA_GRAD_TRIPS for w in range(GLA_GRAD_WAYS)]
            rows = [_gla_block_rows(b) for b in blocks]
            loaded = []
            for b, r in zip(blocks, rows):
                chunks = [b * GLA_CPB + j for j in range(GLA_CPB)]
                loaded.append((q_ref[r, :], f_ref[r, :], v_ref[r, :], do_ref[r, :], [st_ref[c] for c in chunks],
                               [dst_scr[c] for c in chunks], [dec_scr[c] for c in chunks], dq_acc[r, :], dv_acc[r, :]))
            results = [block_grads(*t[:7]) + (t[7], t[8]) for t in loaded]
            for r, (df, dqc, dv, dlb_part, dq_prev, dv_prev) in zip(rows, results):
                df_out[r, :] = df
                dq_acc[r, :] = dq_prev + dqc
                dv_acc[r, :] = dv_prev + dv
                dlb = dlb + dlb_part
            return dlb

        @pl.when(jnp.logical_or(h > 0, d > 0))
        def _():
            df_copy.wait()

        dlb_ref[...] = lax.fori_loop(0, GLA_GRAD_TRIPS, grads, jnp.zeros((1, HEAD_DIM), F32))
        df_copy.start()

        @pl.when(d == 1)
        def _():
            @pl.when(h > 0)
            def _():
                dq_copy.wait()
                dv_copy.wait()

            dq_out[...] = dq_acc[...].astype(BF16)
            dv_out[...] = dv_acc[...].astype(BF16)
            dq_copy.start()
            dv_copy.start()

        @pl.when(last)
        def _():
            df_copy.wait()
            dq_copy.wait()
            dv_copy.wait()

    blk = SEQ * HEAD_DIM * 4
    state_bytes = nck * HEAD_DIM * HEAD_DIM * 4
    dproj, dlb = pl.pallas_call(
        body, name="gla_bwd", grid=(HG_HEADS, 2),
        in_specs=[_head_spec(CB_HG_Q),
                  pl.BlockSpec((SEQ, HEAD_DIM), lambda h, d: (0, CB_F + 8 * d + h)),
                  _head_spec(CB_HG_I),
                  pl.BlockSpec((None, None, 1, HEAD_DIM), lambda h, d: (d, h, 0, 0)),
                  _head_spec(0),
                  pl.BlockSpec((None, None, nck, HEAD_DIM, HEAD_DIM), lambda h, d: (h, d, 0, 0, 0)),
                  pl.BlockSpec((None, SEQ, HEAD_DIM), lambda h, d: (d, 0, h)),
                  pl.BlockSpec((None, None, nck, 8, HEAD_DIM), lambda h, d: (h, d, 0, 0, 0)),
                  _any_spec()],
        out_specs=[_any_spec(), pl.BlockSpec((None, None, 1, HEAD_DIM), lambda h, d: (d, h, 0, 0))],
        out_shape=[jax.ShapeDtypeStruct((SEQ, IN_COLS), BF16), jax.ShapeDtypeStruct((2, HG_HEADS, 1, HEAD_DIM), F32)],
        scratch_shapes=[pltpu.VMEM((SEQ, HEAD_DIM), F32)] * 2
        + [pltpu.VMEM((nck, HEAD_DIM, HEAD_DIM), F32)] * 2
        + [pltpu.VMEM((SEQ, HEAD_DIM), BF16)] * 3 + [pltpu.SemaphoreType.DMA((3,))],
        input_output_aliases={8: 0},
        compiler_params=_params(("arbitrary", "arbitrary"), 8 * blk + 4 * state_bytes + 3 * blk + 3 * blk),
    )(proj, proj, proj, lb, do, states, qt_all, dec_all, dproj)
    return dproj, dlb, dgain


def _gla_bwd_norm(proj, gain, o_pre, do_hg, dproj, tr):
    w = D_MODEL

    def body(g_ref, gain_ref, opre_ref, dohg_ref, dproj_in, do_ref, dg_ref, dgain_ref):
        del dproj_in
        gainv = gain_ref[...]
        part = jnp.zeros((1, HEAD_DIM), F32)
        for h in range(HG_HEADS):
            hs = slice(h * HEAD_DIM, (h + 1) * HEAD_DIM)
            o = opre_ref[:, hs]
            r = lax.rsqrt(jnp.mean(o * o, axis=-1, keepdims=True) + RMS_EPS)
            ohat = o * r
            g = g_ref[:, hs]
            sg = _sigmoid(g)
            silu = g * sg
            dout = dohg_ref[:, hs]
            dg_ref[:, hs] = (dout * ohat * gainv * (sg + silu * (1.0 - sg))).astype(BF16)
            dy = dout * silu
            part = part + jnp.sum(dy * ohat, axis=0, keepdims=True)
            dn = dy * gainv
            do_ref[:, hs] = r * (dn - ohat * jnp.mean(dn * ohat, axis=-1, keepdims=True))

        @pl.when(pl.program_id(0) == 0)
        def _():
            dgain_ref[...] = part

        @pl.when(pl.program_id(0) > 0)
        def _():
            dgain_ref[...] += part

    return pl.pallas_call(
        body, name="gla_bwd_norm", grid=(SEQ // tr,),
        in_specs=[_row_spec(tr, w, CB_HG_G * LANE // w), _bcast_spec(HEAD_DIM), _row_spec(tr, w), _row_spec(tr, w),
                  _any_spec()],
        out_specs=[_row_spec(tr, w), _row_spec(tr, w, CB_HG_G * LANE // w), _bcast_spec(HEAD_DIM)],
        out_shape=[jax.ShapeDtypeStruct((SEQ, w), F32), jax.ShapeDtypeStruct((SEQ, IN_COLS), BF16),
                   jax.ShapeDtypeStruct((1, HEAD_DIM), F32)],
        input_output_aliases={4: 1},
        compiler_params=_params(("arbitrary",), 16 * tr * w * 4),
    )(proj, gain, o_pre, do_hg, dproj)


def _da_residue_rows(r, n0, size, d):
    if d == 1:
        return pl.ds(pl.multiple_of(n0, 8), size)
    return pl.ds(r + n0 * d, size, stride=d)


def _da_residue_ways(d, nqb):
    return min(d, 4) if nqb <= 2 else 1


def _da_rmsnorm(x, gain):
    r = lax.rsqrt(jnp.mean(x * x, axis=-1, keepdims=True) + RMS_EPS)
    return x * r, r, x * r * gain


def _da_scores(qn_scr, kn_scr, slope, i, ld):
    w0 = jnp.clip(i * DA_QB - DA_RADIUS, 0, ld - DA_WIN)
    w0 = pl.multiple_of(w0, DA_RADIUS)
    qrows = pl.ds(pl.multiple_of(i * DA_QB, DA_QB), DA_QB)
    win = pl.ds(w0, DA_WIN)
    qb = qn_scr[qrows, :]
    kw = kn_scr[win, :]
    s = _dot(qb, kw, _NT) * DA_SCALE
    qpos = i * DA_QB + lax.broadcasted_iota(jnp.int32, (DA_QB, DA_WIN), 0)
    kpos = w0 + lax.broadcasted_iota(jnp.int32, (DA_QB, DA_WIN), 1)
    arel = jnp.abs(kpos - qpos)
    s = s - slope * arel.astype(F32)
    s = jnp.where(arel <= DA_RADIUS, s, NEG_INF)
    return s, qb, kw, qrows, win


def _da_slopes(group):
    d = DA_DILATIONS[group]
    sl = _alibi_slopes(DA_HEADS)[4 * group:4 * group + 4] * d
    return jnp.asarray(np.broadcast_to(sl[:, None, None], (4, 1, LANE)).copy())


def _da_fwd(proj, gq, gk, group):
    d = DA_DILATIONS[group]
    ld = SEQ // d
    nqb = ld // DA_QB

    ways = min(DA_FWD_WAYS, nqb)
    rways = _da_residue_ways(d, nqb)

    def body(q_ref, k_ref, v_ref, gq_ref, gk_ref, sl_ref, o_ref, lse_ref, qn_scr, kn_scr, v_scr):
        slope = sl_ref[:, 0:1]

        def residues(t, carry):
            rs = [t * rways + u for u in range(rways)]
            for u, r in enumerate(rs):
                sel = _da_residue_rows(r, 0, ld, d)
                qn_scr[u] = _da_rmsnorm(q_ref[sel, :], gq_ref[...])[2].astype(BF16)
                kn_scr[u] = _da_rmsnorm(k_ref[sel, :], gk_ref[...])[2].astype(BF16)
                v_scr[u] = v_ref[sel, :].astype(BF16)

            def block(u, i):
                s, _, _, _, win = _da_scores(qn_scr.at[u], kn_scr.at[u], slope, i, ld)
                m = jnp.max(s, axis=-1, keepdims=True)
                p = jnp.exp(s - m)
                l = jnp.sum(p, axis=-1, keepdims=True)
                return _dot(p, v_scr[u, win, :], _NN) / l, jnp.broadcast_to(m + jnp.log(l), (DA_QB, HEAD_DIM))

            def step(i, c2):
                todo = [(u, r, i + w * (nqb // ways)) for u, r in enumerate(rs) for w in range(ways)]
                for (u, r, b), (o, lse) in zip(todo, [block(u, b) for u, _, b in todo]):
                    out = _da_residue_rows(r, b * DA_QB, DA_QB, d)
                    o_ref[out, :] = o
                    lse_ref[out, :] = lse
                return c2

            return lax.fori_loop(0, nqb // ways, step, carry)

        lax.fori_loop(0, d // rways, residues, 0)

    seq_spec = lambda base: pl.BlockSpec((SEQ, HEAD_DIM), lambda h: (0, base + 4 * group + h))
    out_spec = pl.BlockSpec((SEQ, HEAD_DIM), lambda h: (0, h))
    gain_spec = pl.BlockSpec((1, HEAD_DIM), lambda h: (0, 0))
    blk = SEQ * HEAD_DIM * 4
    return pl.pallas_call(
        body, name=f"da_fwd_g{group}", grid=(DA_HEADS_PER_GROUP,),
        in_specs=[seq_spec(CB_DA_Q), seq_spec(CB_DA_K), seq_spec(CB_DA_V), gain_spec, gain_spec,
                  pl.BlockSpec((None, 1, LANE), lambda h: (h, 0, 0))],
        out_specs=[out_spec, out_spec],
        out_shape=[jax.ShapeDtypeStruct((SEQ, DA_WIDTH), F32)] * 2,
        scratch_shapes=[pltpu.VMEM((rways, ld, HEAD_DIM), BF16)] * 3,
        compiler_params=_params(("parallel",), 10 * blk + 2 * blk),
    )(proj, proj, proj, gq, gk, _da_slopes(group))


def _da_merge(os, lses, tr):
    w = DA_WIDTH

    def body(o0, o1, o2, l0, l1, l2, ob_ref, of_ref, lse_ref):
        la, lb_, lc = l0[...], l1[...], l2[...]
        m = jnp.maximum(jnp.maximum(la, lb_), lc)
        ea, eb, ec = jnp.exp(la - m), jnp.exp(lb_ - m), jnp.exp(lc - m)
        tot = ea + eb + ec
        o = (ea * o0[...] + eb * o1[...] + ec * o2[...]) / tot
        of_ref[...] = o
        ob_ref[...] = o.astype(BF16)
        lse_ref[...] = m + jnp.log(tot)

    return pl.pallas_call(
        body, name="da_merge", grid=(SEQ // tr,), in_specs=[_row_spec(tr, w)] * 6,
        out_specs=[_row_spec(tr, w)] * 3,
        out_shape=[jax.ShapeDtypeStruct((SEQ, w), BF16), jax.ShapeDtypeStruct((SEQ, w), F32),
                   jax.ShapeDtypeStruct((SEQ, w), F32)],
        compiler_params=_params(("parallel",), 24 * tr * w * 4),
    )(*os, *lses)


def _da_rowdot(do, o, tr):
    w = DA_WIDTH

    def body(do_ref, o_ref, out_ref):
        prod = do_ref[...] * o_ref[...]
        for h in range(w // HEAD_DIM):
            sl = slice(h * HEAD_DIM, (h + 1) * HEAD_DIM)
            out_ref[:, sl] = jnp.broadcast_to(jnp.sum(prod[:, sl], axis=-1, keepdims=True), (tr, HEAD_DIM))

    return pl.pallas_call(
        body, name="da_rowdot", grid=(SEQ // tr,), in_specs=[_row_spec(tr, w)] * 2,
        out_specs=_row_spec(tr, w), out_shape=jax.ShapeDtypeStruct((SEQ, w), F32),
        compiler_params=_params(("parallel",), 10 * tr * w * 4),
    )(do, o)


def _da_bwd(proj, gq, gk, do, lse, dd, dproj, group):
    d = DA_DILATIONS[group]
    ld = SEQ // d
    nqb = ld // DA_QB
    ways = min(DA_WAYS, nqb)
    rways = _da_residue_ways(d, nqb)

    def body(q_ref, k_ref, v_ref, gq_ref, gk_ref, sl_ref, do_ref, lse_ref, dd_ref, dproj_in,
             dproj_ref, dgq_ref, dgk_ref,
             qn_scr, kn_scr, v_scr, dqn_scr, dkn_scr, dvr_scr, dq_scr, dk_scr, dv_scr, dq_out, dk_out, dv_out, sems):
        del dproj_in
        head = pl.program_id(0)
        gqv, gkv = gq_ref[...], gk_ref[...]
        slope = sl_ref[:, 0:1]

        copies = []
        for k, (staging, base) in enumerate(((dq_out, CB_DA_Q), (dk_out, CB_DA_K), (dv_out, CB_DA_V))):
            cols = pl.ds(pl.multiple_of((base + 4 * group + head) * LANE, LANE), LANE)
            copies.append(pltpu.make_async_copy(staging, dproj_ref.at[:, cols], sems.at[k]))

        def residues(t, carry):
            rs = [t * rways + u for u in range(rways)]
            sels = [_da_residue_rows(r, 0, ld, d) for r in rs]
            for u, sel in enumerate(sels):
                qn_scr[u] = _da_rmsnorm(q_ref[sel, :], gqv)[2].astype(BF16)
                kn_scr[u] = _da_rmsnorm(k_ref[sel, :], gkv)[2].astype(BF16)
                v_scr[u] = v_ref[sel, :].astype(BF16)
            dkn_scr[...] = jnp.zeros_like(dkn_scr)
            dvr_scr[...] = jnp.zeros_like(dvr_scr)

            def block(u, r, i):
                s, qb, kw, qrows, win = _da_scores(qn_scr.at[u], kn_scr.at[u], slope, i, ld)
                src = _da_residue_rows(r, i * DA_QB, DA_QB, d)
                p = jnp.exp(s - lse_ref[src, :][:, 0:1])
                dob = do_ref[src, :]
                dp = _dot(dob, v_scr[u, win, :], _NT)
                ds = p * (dp - dd_ref[src, :][:, 0:1]) * DA_SCALE
                return u, qrows, win, _dot(p, dob, _TN), _dot(ds, kw, _NN), _dot(ds, qb, _TN)

            def step(i, c2):
                todo = [(u, r, i + w * (nqb // ways)) for u, r in enumerate(rs) for w in range(ways)]
                for u, qrows, win, dv, dqn, dkn in [block(*t) for t in todo]:
                    dvr_scr[u, win, :] += dv
                    dqn_scr[u, qrows, :] = dqn
                    dkn_scr[u, win, :] += dkn
                return c2

            lax.fori_loop(0, nqb // ways, step, 0)

            pq, pk = carry
            for u, sel in enumerate(sels):
                parts = []
                for x_ref, gv, dn_scr, dx_scr in ((q_ref, gqv, dqn_scr, dq_scr), (k_ref, gkv, dkn_scr, dk_scr)):
                    hat, rstd, _ = _da_rmsnorm(x_ref[sel, :], gv)
                    dn = dn_scr[u]
                    dyg = dn * gv
                    dx_scr[sel, :] = rstd * (dyg - hat * jnp.mean(dyg * hat, axis=-1, keepdims=True))
                    parts.append(jnp.sum(dn * hat, axis=0, keepdims=True))
                dv_scr[sel, :] = dvr_scr[u]
                pq, pk = pq + parts[0], pk + parts[1]
            return pq, pk

        zero = jnp.zeros((1, HEAD_DIM), F32)
        pq, pk = lax.fori_loop(0, d // rways, residues, (zero, zero))

        @pl.when(pl.program_id(0) == 0)
        def _():
            dgq_ref[...] = pq
            dgk_ref[...] = pk

        @pl.when(pl.program_id(0) > 0)
        def _():
            dgq_ref[...] += pq
            dgk_ref[...] += pk

        @pl.when(head > 0)
        def _():
            for cp in copies:
                cp.wait()

        dq_out[...] = dq_scr[...].astype(BF16)
        dk_out[...] = dk_scr[...].astype(BF16)
        dv_out[...] = dv_scr[...].astype(BF16)
        for cp in copies:
            cp.start()

        @pl.when(head == DA_HEADS_PER_GROUP - 1)
        def _():
            for cp in copies:
                cp.wait()

    seq_spec = lambda base: pl.BlockSpec((SEQ, HEAD_DIM), lambda h: (0, base + 4 * group + h))
    out_spec = pl.BlockSpec((SEQ, HEAD_DIM), lambda h: (0, h))
    gain_spec = pl.BlockSpec((1, HEAD_DIM), lambda h: (0, 0))
    gshape = jax.ShapeDtypeStruct((1, HEAD_DIM), F32)
    blk = SEQ * HEAD_DIM * 4
    return pl.pallas_call(
        body, name=f"da_bwd_g{group}", grid=(DA_HEADS_PER_GROUP,),
        in_specs=[seq_spec(CB_DA_Q), seq_spec(CB_DA_K), seq_spec(CB_DA_V), gain_spec, gain_spec,
                  pl.BlockSpec((None, 1, LANE), lambda h: (h, 0, 0)), out_spec, out_spec, out_spec, _any_spec()],
        out_specs=[_any_spec(), gain_spec, gain_spec],
        out_shape=[jax.ShapeDtypeStruct((SEQ, IN_COLS), BF16), gshape, gshape],
        scratch_shapes=[pltpu.VMEM((rways, ld, HEAD_DIM), BF16)] * 3 + [pltpu.VMEM((rways, ld, HEAD_DIM), F32)] * 3
        + [pltpu.VMEM((SEQ, HEAD_DIM), F32)] * 3 + [pltpu.VMEM((SEQ, HEAD_DIM), BF16)] * 3
        + [pltpu.SemaphoreType.DMA((3,))],
        input_output_aliases={9: 0},
        compiler_params=_params(("arbitrary",), 12 * blk + 3 * blk + 3 * blk + 3 * blk + 2 * blk),
    )(proj, proj, proj, gq, gk, _da_slopes(group), do, lse, dd, dproj)


MEM_WAYS = 4


def _mem_probs(q, kn, gq):
    qhat, rq, qn = _da_rmsnorm(q, gq)
    s = _dot(qn, kn, _NT) * MEM_SCALE
    m = jnp.max(s, axis=-1, keepdims=True)
    e = jnp.exp(s - m)
    p = e / jnp.sum(e, axis=-1, keepdims=True)
    return p, qhat, rq, qn


def _mem_pieces(tq):
    rows = tq // MEM_WAYS
    return [slice(w * rows, (w + 1) * rows) for w in range(MEM_WAYS)]


def _mem_fwd(proj, kv, gq, gk, tq):
    def body(q_ref, k_ref, v_ref, gq_ref, gk_ref, o_ref):
        kn = _da_rmsnorm(k_ref[...], gk_ref[...])[2]
        v, gqv = v_ref[...], gq_ref[...]
        pieces = _mem_pieces(tq)
        outs = [_dot(_mem_probs(q_ref[rs, :], kn, gqv)[0], v, _NN) for rs in pieces]
        for rs, o in zip(pieces, outs):
            o_ref[rs, :] = o.astype(BF16)

    gain_spec = pl.BlockSpec((1, HEAD_DIM), lambda h, i: (0, 0))
    return pl.pallas_call(
        body, name="mem_fwd", grid=(MEM_HEADS, SEQ // tq),
        in_specs=[pl.BlockSpec((tq, HEAD_DIM), lambda h, i: (i, CB_MEM_Q + h)),
                  pl.BlockSpec((N_MEM, HEAD_DIM), lambda h, i: (0, h)),
                  pl.BlockSpec((N_MEM, HEAD_DIM), lambda h, i: (0, MEM_HEADS + h)), gain_spec, gain_spec],
        out_specs=pl.BlockSpec((tq, HEAD_DIM), lambda h, i: (i, h)),
        out_shape=jax.ShapeDtypeStruct((SEQ, MEM_WIDTH), BF16),
        compiler_params=_params(("parallel", "parallel"), 16 * tq * N_MEM * 4),
    )(proj, kv, kv, gq, gk)


def _mem_bwd(proj, kv, gq, gk, do, dproj, tq):
    nq = SEQ // tq

    def body(q_ref, k_ref, v_ref, gq_ref, gk_ref, do_ref, dproj_in, dq_ref, dk_ref, dv_ref, dgq_ref, dgk_ref, dkn_scr):
        del dproj_in
        h, i = pl.program_id(0), pl.program_id(1)
        gqv, gkv = gq_ref[...], gk_ref[...]
        khat, rk, kn = _da_rmsnorm(k_ref[...], gkv)
        v = v_ref[...]

        def piece(rs):
            p, qhat, rq, qn = _mem_probs(q_ref[rs, :], kn, gqv)
            dob = do_ref[rs, :]
            dp = _dot(dob, v, _NT)
            ds = p * (dp - jnp.sum(p * dp, axis=-1, keepdims=True)) * MEM_SCALE
            dqn = _dot(ds, kn, _NN)
            dyg = dqn * gqv
            dq = (rq * (dyg - qhat * jnp.mean(dyg * qhat, axis=-1, keepdims=True))).astype(BF16)
            return dq, _dot(p, dob, _TN), _dot(ds, qn, _TN), jnp.sum(dqn * qhat, axis=0, keepdims=True)

        pieces = _mem_pieces(tq)
        results = [piece(rs) for rs in pieces]
        for rs, res in zip(pieces, results):
            dq_ref[rs, :] = res[0]
        dvp = sum((res[1] for res in results[1:]), results[0][1])
        dknp = sum((res[2] for res in results[1:]), results[0][2])
        dgq_part = sum((res[3] for res in results[1:]), results[0][3])
        first = jnp.logical_and(h == 0, i == 0)

        @pl.when(first)
        def _():
            dgq_ref[...] = dgq_part

        @pl.when(jnp.logical_not(first))
        def _():
            dgq_ref[...] += dgq_part

        @pl.when(i == 0)
        def _():
            dv_ref[...] = dvp
            dkn_scr[...] = dknp

        @pl.when(i > 0)
        def _():
            dv_ref[...] += dvp
            dkn_scr[...] += dknp

        @pl.when(i == nq - 1)
        def _():
            dkn = dkn_scr[...]
            dkg = dkn * gkv
            dk_ref[...] = rk * (dkg - khat * jnp.mean(dkg * khat, axis=-1, keepdims=True))
            dgk_part = jnp.sum(dkn * khat, axis=0, keepdims=True)

            @pl.when(h == 0)
            def _():
                dgk_ref[...] = dgk_part

            @pl.when(h > 0)
            def _():
                dgk_ref[...] += dgk_part

    gain_spec = pl.BlockSpec((1, HEAD_DIM), lambda h, i: (0, 0))
    kvout = pl.BlockSpec((N_MEM, HEAD_DIM), lambda h, i: (0, h))
    return pl.pallas_call(
        body, name="mem_bwd", grid=(MEM_HEADS, nq),
        in_specs=[pl.BlockSpec((tq, HEAD_DIM), lambda h, i: (i, CB_MEM_Q + h)),
                  pl.BlockSpec((N_MEM, HEAD_DIM), lambda h, i: (0, h)),
                  pl.BlockSpec((N_MEM, HEAD_DIM), lambda h, i: (0, MEM_HEADS + h)), gain_spec, gain_spec,
                  pl.BlockSpec((tq, HEAD_DIM), lambda h, i: (i, h)), _any_spec()],
        out_specs=[pl.BlockSpec((tq, HEAD_DIM), lambda h, i: (i, CB_MEM_Q + h)), kvout, kvout, gain_spec, gain_spec],
        out_shape=[jax.ShapeDtypeStruct((SEQ, IN_COLS), BF16), jax.ShapeDtypeStruct((N_MEM, MEM_WIDTH), F32),
                   jax.ShapeDtypeStruct((N_MEM, MEM_WIDTH), F32), jax.ShapeDtypeStruct((1, HEAD_DIM), F32),
                   jax.ShapeDtypeStruct((1, HEAD_DIM), F32)],
        scratch_shapes=[pltpu.VMEM((N_MEM, HEAD_DIM), F32)], input_output_aliases={6: 0},
        compiler_params=_params(("arbitrary", "arbitrary"), 24 * tq * N_MEM * 4),
    )(proj, kv, kv, gq, gk, do, dproj)


def _mesh_position():
    return lax.axis_index("x"), lax.axis_index("y"), lax.axis_index("c")


def _any_spec():
    return pl.BlockSpec(memory_space=pl.ANY)


def _all_gather(shards):
    n = len(shards)

    def body(*refs):
        ins, outs = refs[:n], refs[n:2 * n]
        send_sems, recv_sems, local_sems = refs[2 * n:]
        x, y, c = _mesh_position()
        me, sibling = (x, y, c), (x, y, 1 - c)
        first = (jnp.where(c == 0, 1 - x, x), jnp.where(c == 0, y, 1 - y), c)
        second = (jnp.where(c == 0, x, 1 - x), jnp.where(c == 0, 1 - y, y), c)
        diagonal = (1 - x, 1 - y, c)

        def copy(w, k, block, to, src=None):
            px, py, pc = block
            rows = outs[w].at[4 * px + 2 * py + pc]
            return pltpu.make_async_remote_copy(
                src_ref=rows if src is None else src, dst_ref=rows,
                send_sem=send_sems.at[7 * w + k], recv_sem=recv_sems.at[7 * w + k],
                device_id=to, device_id_type=MESH)

        started = []
        for w in range(n):
            mine = pltpu.make_async_copy(ins[w], outs[w].at[4 * x + 2 * y + c], local_sems.at[w])
            mine.start()
            started.append(mine)
        sends = []
        for w in range(n):
            own = [copy(w, 0, me, sibling, src=ins[w]), copy(w, 1, me, first, src=ins[w]),
                   copy(w, 2, me, second, src=ins[w])]
            for cp in own:
                cp.start()
            sends += own
        for w in range(n):
            copy(w, 1, first, me).wait_recv()
            follow = [copy(w, 3, first, second), copy(w, 4, first, sibling)]
            for cp in follow:
                cp.start()
            copy(w, 2, second, me).wait_recv()
            follow.append(copy(w, 5, second, sibling))
            follow[-1].start()
            sends += follow
        for w in range(n):
            copy(w, 3, diagonal, me).wait_recv()
            passed = copy(w, 6, diagonal, sibling)
            passed.start()
            sends.append(passed)
        for w in range(n):
            for k in (0, 4, 5, 6):
                copy(w, k, sibling, me).wait_recv()
        for cp in sends:
            cp.wait_send()
        for mine in started:
            mine.wait()

    return pl.pallas_call(
        body, name="weights_all_gather",
        in_specs=[_any_spec()] * n, out_specs=[_any_spec()] * n,
        out_shape=[jax.ShapeDtypeStruct((N_DEV,) + s.shape, s.dtype) for s in shards],
        scratch_shapes=[pltpu.SemaphoreType.DMA((7 * n,)), pltpu.SemaphoreType.DMA((7 * n,)),
                        pltpu.SemaphoreType.DMA((n,))],
    )(*shards)


PROJ_ROWS = 512
LOCAL_DMA_PRIORITY = 1


def _gather_proj(shard, h):
    n_tiles = SEQ // PROJ_ROWS

    def body(shard_ref, h_ref, gath_ref, proj_ref, wbuf, obuf, send_sems, recv_sems, local_sem, wsem, osems):
        x, y, c = _mesh_position()
        me, sibling = (x, y, c), (x, y, 1 - c)
        first = (jnp.where(c == 0, 1 - x, x), jnp.where(c == 0, y, 1 - y), c)
        second = (jnp.where(c == 0, x, 1 - x), jnp.where(c == 0, 1 - y, y), c)
        diagonal = (1 - x, 1 - y, c)
        other_core = lambda dev: (dev[0], dev[1], 1 - c)
        index = lambda dev: 4 * dev[0] + 2 * dev[1] + dev[2]

        def copy(k, block, to, src=None):
            rows = gath_ref.at[index(block)]
            return pltpu.make_async_remote_copy(
                src_ref=rows if src is None else src, dst_ref=rows, send_sem=send_sems.at[k], recv_sem=recv_sems.at[k],
                device_id=to, device_id_type=MESH)

        def out_copy(slot, tile, block):
            rows = pl.ds(pl.multiple_of(tile * PROJ_ROWS, PROJ_ROWS), PROJ_ROWS)
            cols = pl.ds(pl.multiple_of(block * IN_SHARD, LANE), IN_SHARD)
            return pltpu.make_async_copy(obuf.at[slot], proj_ref.at[rows, cols], osems.at[slot])

        def project(block, weights, first_call):
            load = pltpu.make_async_copy(weights, wbuf, wsem)
            load.start(priority=LOCAL_DMA_PRIORITY)
            load.wait()

            def tile(i, carry):
                slot = i % 2

                @pl.when(i >= (2 if first_call else 0))
                def _():
                    out_copy(slot, i, block).wait()

                rows = pl.ds(pl.multiple_of(i * PROJ_ROWS, PROJ_ROWS), PROJ_ROWS)
                obuf[slot] = _dot(h_ref[rows, :], wbuf[...], _NN)
                out_copy(slot, i, block).start(priority=LOCAL_DMA_PRIORITY)
                return carry

            lax.fori_loop(0, n_tiles, tile, 0)

        mine = pltpu.make_async_copy(shard_ref, gath_ref.at[index(me)], local_sem)
        mine.start(priority=LOCAL_DMA_PRIORITY)
        sends = [copy(0, me, sibling, src=shard_ref), copy(1, me, first, src=shard_ref),
                 copy(2, me, second, src=shard_ref)]
        for cp in sends:
            cp.start()
        project(index(me), shard_ref, True)

        copy(0, sibling, me).wait_recv()
        project(index(sibling), gath_ref.at[index(sibling)], False)

        copy(1, first, me).wait_recv()
        follow = [copy(3, first, second), copy(4, first, sibling)]
        for cp in follow:
            cp.start()
        project(index(first), gath_ref.at[index(first)], False)

        copy(2, second, me).wait_recv()
        follow.append(copy(5, second, sibling))
        follow[-1].start()
        project(index(second), gath_ref.at[index(second)], False)

        copy(4, sibling, me).wait_recv()
        project(index(other_core(second)), gath_ref.at[index(other_core(second))], False)
        copy(5, sibling, me).wait_recv()
        project(index(other_core(first)), gath_ref.at[index(other_core(first))], False)

        copy(3, diagonal, me).wait_recv()
        follow.append(copy(6, diagonal, sibling))
        follow[-1].start()
        project(index(diagonal), gath_ref.at[index(diagonal)], False)
        copy(6, sibling, me).wait_recv()
        project(index(other_core(diagonal)), gath_ref.at[index(other_core(diagonal))], False)

        out_copy(0, 0, 0).wait()
        out_copy(1, 0, 0).wait()
        for cp in sends + follow:
            cp.wait_send()
        mine.wait()

    return pl.pallas_call(
        body, name="weights_gather_proj",
        in_specs=[_any_spec(), pl.BlockSpec(memory_space=pltpu.VMEM)], out_specs=[_any_spec(), _any_spec()],
        out_shape=[jax.ShapeDtypeStruct((N_DEV,) + shard.shape, BF16), jax.ShapeDtypeStruct((SEQ, IN_COLS), F32)],
        scratch_shapes=[pltpu.VMEM(shard.shape, BF16), pltpu.VMEM((2, PROJ_ROWS, IN_SHARD), F32),
                        pltpu.SemaphoreType.DMA((7,)), pltpu.SemaphoreType.DMA((7,)), pltpu.SemaphoreType.DMA,
                        pltpu.SemaphoreType.DMA, pltpu.SemaphoreType.DMA((2,))],
        compiler_params=pltpu.CompilerParams(vmem_limit_bytes=48 << 20),
    )(shard, h)


def _chip_of(j, x, y):
    return (1 - x if j & 1 else x, 1 - y if j & 2 else y)


_HBM_SPEC = pl.BlockSpec(memory_space=pltpu.HBM)
_SEM_SPEC = pl.BlockSpec(memory_space=pltpu.SEMAPHORE)
_DATAFLOW_EFFECT = pltpu.SideEffectType.DATAFLOW_SIDE_EFFECTING
TOKEN_SHAPE = (8, D_MODEL)


def _copies_start(name, arrays, n_copies, plan):
    n = len(arrays)

    def body(*refs):
        send_sems, recv_sems, token = refs[n], refs[n + 1], refs[2 * n + 2]
        copies = plan(refs[:n])
        assert len(copies) == n_copies
        for k, (src, dst, dev) in enumerate(copies):
            pltpu.make_async_remote_copy(src_ref=src, dst_ref=dst, send_sem=send_sems.at[k], recv_sem=recv_sems.at[k],
                                         device_id=dev, device_id_type=MESH).start()
        token[...] = jnp.zeros_like(token)

    outs = pl.pallas_call(
        body, name=name,
        out_shape=(pltpu.SemaphoreType.DMA((n_copies,)), pltpu.SemaphoreType.DMA((n_copies,)),
                   *[pltpu.HBM(a.shape, a.dtype) for a in arrays], jax.ShapeDtypeStruct(TOKEN_SHAPE, F32)),
        in_specs=[_HBM_SPEC] * n,
        out_specs=(_SEM_SPEC, _SEM_SPEC, *[_HBM_SPEC] * n, pl.BlockSpec(memory_space=pltpu.VMEM)),
        input_output_aliases={i: i + 2 for i in range(n)},
        compiler_params=pltpu.CompilerParams(has_side_effects=_DATAFLOW_EFFECT),
    )(*[pltpu.with_memory_space_constraint(a, pltpu.HBM) for a in arrays])
    return outs[0], outs[1], list(outs[2:2 + n]), outs[2 + n]


def _copies_wait(name, send_sems, recv_sems, arrays, n_copies, plan, after):
    n = len(arrays)
    after = list(after) if isinstance(after, (list, tuple)) else [after]

    def body(*refs):
        send_ref, recv_ref = refs[n], refs[n + 1]
        copies = plan(refs[:n])
        assert len(copies) == n_copies
        for k, (src, dst, dev) in enumerate(copies):
            cp = pltpu.make_async_remote_copy(src_ref=src, dst_ref=dst, send_sem=send_ref.at[k], recv_sem=recv_ref.at[k],
                                              device_id=dev, device_id_type=MESH)
            cp.wait_send()
            cp.wait_recv()

    outs = pl.pallas_call(
        body, name=name, out_shape=tuple(pltpu.HBM(a.shape, a.dtype) for a in arrays),
        in_specs=[_HBM_SPEC] * n + [_SEM_SPEC, _SEM_SPEC] + [pl.BlockSpec(memory_space=pl.ANY)] * len(after),
        out_specs=tuple([_HBM_SPEC] * n), input_output_aliases={i: i for i in range(n)},
        compiler_params=pltpu.CompilerParams(has_side_effects=_DATAFLOW_EFFECT),
    )(*arrays, send_sems, recv_sems, *after)
    return list(outs)


def _after(small, token):
    return small if token is None else small + token[0:1, :small.shape[-1]]


def _gather_plan_out(n):
    def plan(refs):
        x, y, c = _mesh_position()
        me = 4 * x + 2 * y + c
        copies = []
        for w in range(n):
            land = refs[n + w].at[me]
            copies.append((refs[w], land, (x, y, 1 - c)))
            for j in range(1, 4):
                copies.append((refs[w], land, (*_chip_of(j, x, y), c)))
        return copies
    return plan


def _gather_plan_pass(n):
    def plan(refs):
        x, y, c = _mesh_position()
        copies = []
        for w in range(n):
            for j in range(1, 4):
                px, py = _chip_of(j, x, y)
                rows = refs[w].at[4 * px + 2 * py + c]
                copies.append((rows, rows, (x, y, 1 - c)))
        return copies
    return plan


def _reduce_plan_sibling(n):
    def plan(refs):
        x, y, c = _mesh_position()
        copies = []
        for w in range(n):
            for j in range(4):
                px, py = _chip_of(j, x, y)
                copies.append((refs[w].at[4 * px + 2 * py + (1 - c)], refs[n + w].at[j], (x, y, 1 - c)))
        return copies
    return plan


def _reduce_plan_chips(n):
    def plan(refs):
        x, y, c = _mesh_position()
        copies = []
        for w in range(n):
            for j in range(1, 4):
                copies.append((refs[w].at[j - 1], refs[n + w].at[j - 1], (*_chip_of(j, x, y), c)))
        return copies
    return plan


def _chip_partials(grad, recv, name, tr):
    _, rows, width = grad.shape

    def body(g_ref, r_ref, own_ref, other_ref):
        x, y, c = _mesh_position()
        for j in range(4):
            px, py = _chip_of(j, x, y)
            total = g_ref[4 * px + 2 * py + c].astype(F32) + r_ref[j].astype(F32)
            if j == 0:
                own_ref[...] = total
            else:
                other_ref[j - 1] = total.astype(BF16)

    return pl.pallas_call(
        body, name=name, grid=(rows // tr,),
        in_specs=[pl.BlockSpec((N_DEV, tr, width), lambda i: (0, i, 0)),
                  pl.BlockSpec((4, tr, width), lambda i: (0, i, 0))],
        out_specs=[pl.BlockSpec((tr, width), lambda i: (i, 0)), pl.BlockSpec((3, tr, width), lambda i: (0, i, 0))],
        out_shape=[jax.ShapeDtypeStruct((rows, width), F32), jax.ShapeDtypeStruct((3, rows, width), BF16)],
        compiler_params=_params(("parallel",), 2 * 20 * tr * width * 2 + 8 * tr * width * 4),
    )(grad, recv)


def _adamw_math(w, g, m, v):
    m = ADAM_B1 * m + (1.0 - ADAM_B1) * g
    v = ADAM_B2 * v + (1.0 - ADAM_B2) * (g * g)
    m_hat = m / (1.0 - ADAM_B1 ** ADAM_STEP)
    v_hat = v / (1.0 - ADAM_B2 ** ADAM_STEP)
    delta = -ADAM_LR * (m_hat / (jnp.sqrt(v_hat) + ADAM_EPS) + ADAM_WD * w)
    return delta, m, v


def _adamw_shard(own, recv, w, m, v, name, tr):
    rows, width = own.shape

    def body(own_ref, r_ref, w_ref, m_ref, v_ref, g_ref, d_ref, nm_ref, nv_ref):
        g = own_ref[...]
        for j in range(3):
            g = g + r_ref[j].astype(F32)
        g_ref[...] = g
        d_ref[...], nm_ref[...], nv_ref[...] = _adamw_math(w_ref[...], g, m_ref[...], v_ref[...])

    spec = pl.BlockSpec((tr, width), lambda i: (i, 0))
    shape = jax.ShapeDtypeStruct((rows, width), F32)
    return pl.pallas_call(
        body, name=name, grid=(rows // tr,),
        in_specs=[spec, pl.BlockSpec((3, tr, width), lambda i: (0, i, 0)), spec, spec, spec],
        out_specs=[spec] * 4, out_shape=[shape] * 4,
        compiler_params=_params(("parallel",), 22 * tr * width * 4),
    )(own, recv, w, m, v)


def _small_all_reduce_adamw(gpart, lbpack, wpack, mpack, vpack, after):
    def body(gp_ref, lb_ref, w_ref, m_ref, v_ref, after_ref, g_ref, d_ref, nm_ref, nv_ref, gath_ref, send_sems, recv_sems):
        del after_ref
        x, y, c = _mesh_position()
        me = 4 * x + 2 * y + c
        gath_ref[me] = gp_ref[...]
        copies = []
        for j in range(1, N_DEV):
            peer = (x ^ (j >> 2), y ^ ((j >> 1) & 1), c ^ (j & 1))
            cp = pltpu.make_async_remote_copy(
                src_ref=gp_ref, dst_ref=gath_ref.at[me], send_sem=send_sems.at[j - 1], recv_sem=recv_sems.at[j - 1],
                device_id=peer, device_id_type=MESH)
            cp.start()
            copies.append(cp)
        for cp in copies:
            cp.wait()
        tot = gath_ref[0]
        for s in range(1, N_DEV):
            tot = tot + gath_ref[s]
        lb = lb_ref[...]
        dl = tot[16:32, :] * lb * (1.0 - lb)
        g = jnp.concatenate([tot[0:16, :], dl[0:8, :], -dl[0:8, :], dl[8:16, :], -dl[8:16, :],
                             tot[32:SMALL_GRAD_ROWS, :]], axis=0)
        g_ref[...] = g
        d_ref[...], nm_ref[...], nv_ref[...] = _adamw_math(w_ref[...], g, m_ref[...], v_ref[...])

    vm = pl.BlockSpec(memory_space=pltpu.VMEM)
    shape = jax.ShapeDtypeStruct((SMALL_ROWS, LANE), F32)
    return pl.pallas_call(
        body, name="small_all_reduce_adamw", in_specs=[vm] * 5 + [_any_spec()], out_specs=[vm] * 4,
        out_shape=[shape] * 4,
        scratch_shapes=[pltpu.VMEM((N_DEV, SMALL_GRAD_ROWS, LANE), F32),
                        pltpu.SemaphoreType.DMA((N_DEV - 1,)), pltpu.SemaphoreType.DMA((N_DEV - 1,))],
    )(gpart, lbpack, wpack, mpack, vpack, after)


_SMALL_NAMES = ("norm_mix_gain", "norm_mem_gain", "lb_logits_fw", "lb_logits_bw", "norm_ffn_gain",
                "hg_norm_gain", "da_q_gain", "da_k_gain", "mem_q_gain", "mem_k_gain")
_SMALL_ROW0 = {"norm_mix_gain": 0, "norm_mem_gain": 8, "lb_logits_fw": 16, "lb_logits_bw": 32, "norm_ffn_gain": 48,
               "hg_norm_gain": 56, "da_q_gain": 64, "da_k_gain": 72, "mem_q_gain": 80, "mem_k_gain": 88}
_LOSS_ROW = 96


def _pack_rows(parts, total_rows):
    rows = []
    for p in parts:
        r = p.reshape(-1, LANE)
        rows.append(jnp.pad(r, ((0, -r.shape[0] % 8), (0, 0))))
    used = sum(r.shape[0] for r in rows)
    if total_rows > used:
        rows.append(jnp.zeros((total_rows - used, LANE), F32))
    return jnp.concatenate(rows, axis=0)


def _unpack_small(pack, like):
    out = {}
    for name in _SMALL_NAMES:
        n = like[name].size // LANE
        r0 = _SMALL_ROW0[name]
        out[name] = pack[r0:r0 + n].reshape(like[name].shape)
    return out


def kernel(x, mem, norm_mix_gain, norm_mem_gain, w_in, lb_logits_fw, lb_logits_bw, hg_norm_gain, da_q_gain, da_k_gain, w_mem_kv, mem_q_gain, mem_k_gain, w_proj_hg, w_proj_da, w_proj_mem, w_out, norm_ffn_gain, w_ffn_in, w_ffn_out, loss_target, m_norm_mix_gain, m_norm_mem_gain, m_w_in, m_lb_logits_fw, m_lb_logits_bw, m_hg_norm_gain, m_da_q_gain, m_da_k_gain, m_w_mem_kv, m_mem_q_gain, m_mem_k_gain, m_w_proj_hg, m_w_proj_da, m_w_proj_mem, m_w_out, m_norm_ffn_gain, m_w_ffn_in, m_w_ffn_out, v_norm_mix_gain, v_norm_mem_gain, v_w_in, v_lb_logits_fw, v_lb_logits_bw, v_hg_norm_gain, v_da_q_gain, v_da_k_gain, v_w_mem_kv, v_mem_q_gain, v_mem_k_gain, v_w_proj_hg, v_w_proj_da, v_w_proj_mem, v_w_out, v_norm_ffn_gain, v_w_ffn_in, v_w_ffn_out):
    small_w = dict(norm_mix_gain=norm_mix_gain, norm_mem_gain=norm_mem_gain, lb_logits_fw=lb_logits_fw,
                   lb_logits_bw=lb_logits_bw, norm_ffn_gain=norm_ffn_gain, hg_norm_gain=hg_norm_gain,
                   da_q_gain=da_q_gain, da_k_gain=da_k_gain, mem_q_gain=mem_q_gain, mem_k_gain=mem_k_gain)
    small_m = dict(norm_mix_gain=m_norm_mix_gain, norm_mem_gain=m_norm_mem_gain, lb_logits_fw=m_lb_logits_fw,
                   lb_logits_bw=m_lb_logits_bw, norm_ffn_gain=m_norm_ffn_gain, hg_norm_gain=m_hg_norm_gain,
                   da_q_gain=m_da_q_gain, da_k_gain=m_da_k_gain, mem_q_gain=m_mem_q_gain, mem_k_gain=m_mem_k_gain)
    small_v = dict(norm_mix_gain=v_norm_mix_gain, norm_mem_gain=v_norm_mem_gain, lb_logits_fw=v_lb_logits_fw,
                   lb_logits_bw=v_lb_logits_bw, norm_ffn_gain=v_norm_ffn_gain, hg_norm_gain=v_hg_norm_gain,
                   da_q_gain=v_da_q_gain, da_k_gain=v_da_k_gain, mem_q_gain=v_mem_q_gain, mem_k_gain=v_mem_k_gain)
    big_w = dict(w_in=w_in[0], w_mem_kv=w_mem_kv[0], w_proj_hg=w_proj_hg[0], w_proj_da=w_proj_da[0],
                 w_proj_mem=w_proj_mem[0], w_out=w_out[0], w_ffn_in=w_ffn_in[0], w_ffn_out=w_ffn_out[0])
    big_m = dict(w_in=m_w_in[0], w_mem_kv=m_w_mem_kv[0], w_proj_hg=m_w_proj_hg[0], w_proj_da=m_w_proj_da[0],
                 w_proj_mem=m_w_proj_mem[0], w_out=m_w_out[0], w_ffn_in=m_w_ffn_in[0], w_ffn_out=m_w_ffn_out[0])
    big_v = dict(w_in=v_w_in[0], w_mem_kv=v_w_mem_kv[0], w_proj_hg=v_w_proj_hg[0], w_proj_da=v_w_proj_da[0],
                 w_proj_mem=v_w_proj_mem[0], w_out=v_w_out[0], w_ffn_in=v_w_ffn_in[0], w_ffn_out=v_w_ffn_out[0])

    row_tile = dict(w_in=128, w_mem_kv=128, w_proj_hg=128, w_proj_da=512, w_proj_mem=512, w_out=128,
                    w_ffn_in=128, w_ffn_out=352)
    rest = _BIG_NAMES[1:]
    shards = [big_w[n].astype(BF16) for n in rest]
    state = {}
    big_out = {}

    def reduce_start(group, stacked):
        names = tuple(stacked)
        arrays = [stacked[n] for n in names] + [lax.empty((4,) + stacked[n].shape[1:], BF16) for n in names]
        plan = _reduce_plan_sibling(len(names))
        send, recv, thru, token = _copies_start(f"grads_{group}_sibling_start", arrays, 4 * len(names), plan)
        state[group] = dict(names=names, plan=plan, send=send, recv=recv, arrays=thru)
        return token

    def reduce_middle(group, after):
        st = state[group]
        names, k = st["names"], len(st["names"])
        thru = _copies_wait(f"grads_{group}_sibling_wait", st["send"], st["recv"], st["arrays"], 4 * k, st["plan"], after)
        partials = [_chip_partials(thru[i], thru[k + i], f"chip_partials_{n}", row_tile[n]) for i, n in enumerate(names)]
        arrays = [p[1] for p in partials] + [lax.empty(p[1].shape, BF16) for p in partials]
        plan = _reduce_plan_chips(k)
        send, recv, thru2, token = _copies_start(f"grads_{group}_chips_start", arrays, 3 * k, plan)
        state[group] = dict(names=names, plan=plan, send=send, recv=recv, arrays=thru2, own=[p[0] for p in partials])
        return token

    def reduce_finish(group, after):
        st = state.pop(group)
        names, k = st["names"], len(st["names"])
        thru = _copies_wait(f"grads_{group}_chips_wait", st["send"], st["recv"], st["arrays"], 3 * k, st["plan"], after)
        for i, n in enumerate(names):
            big_out[n] = _adamw_shard(st["own"][i], thru[k + i], big_w[n], big_m[n], big_v[n], "adamw_" + n, row_tile[n])

    step = _local_step_stages(x[0], mem[0], loss_target[0], small_w)
    event, payload = next(step)
    local = None
    while True:
        reply = None
        if event == "gather_proj":
            reply = _gather_proj(big_w["w_in"].astype(BF16), payload)
        elif event == "begin":
            nr = len(rest)
            me = 4 * lax.axis_index("x") + 2 * lax.axis_index("y") + lax.axis_index("c")
            arrays = shards + [lax.dynamic_update_slice(lax.empty((N_DEV,) + s.shape, BF16), s[None], (me, 0, 0))
                               for s in shards]
            plan = _gather_plan_out(nr)
            send, recv, thru, reply = _copies_start("weights_rest_out_start", arrays, 4 * nr, plan)
            state["gather"] = dict(plan=plan, send=send, recv=recv, arrays=thru)
        elif event == "after_gla_fwd":
            st = state["gather"]
            nr = len(rest)
            thru = _copies_wait("weights_rest_out_wait", st["send"], st["recv"], st["arrays"], 4 * nr, st["plan"], payload)
            plan = _gather_plan_pass(nr)
            send, recv, lands, reply = _copies_start("weights_rest_pass_start", thru[nr:], 3 * nr, plan)
            state["gather"] = dict(plan=plan, send=send, recv=recv, arrays=lands)
        elif event == "need_weights":
            st = state.pop("gather")
            nr = len(rest)
            lands = _copies_wait("weights_rest_pass_wait", st["send"], st["recv"], st["arrays"], 3 * nr, st["plan"], payload)
            reply = dict(zip(rest, lands))
        elif event == "grads_ffn":
            reply = reduce_start("ffn", payload)
        elif event == "after_gate_merge_bwd":
            reply = reduce_middle("ffn", payload)
        elif event == "grads_mix":
            reply = reduce_start("mix", payload)
        elif event == "after_da_bwd_g0":
            reply = reduce_middle("mix", payload)
            reduce_finish("ffn", payload)
        elif event == "after_gla_bwd":
            reduce_finish("mix", payload)
        elif event == "grads_in":
            reply = reduce_start("in", payload)
        elif event == "after_proj_bwd_act_top":
            reply = reduce_middle("in", payload)
        elif event == "end":
            local = payload
            reduce_finish("in", [local["grad_x"]] + [big_out[n][0] for n in rest])
            break
        event, payload = step.send(reply)
    grad_x = local["grad_x"]

    wpack = _pack_rows([small_w[n] for n in _SMALL_NAMES], SMALL_ROWS)
    mpack = _pack_rows([small_m[n] for n in _SMALL_NAMES], SMALL_ROWS)
    vpack = _pack_rows([small_v[n] for n in _SMALL_NAMES], SMALL_ROWS)
    gs, ds, ms, vs = _small_all_reduce_adamw(local["gpart"], local["lbpack"], wpack, mpack, vpack, big_out["w_in"][0])
    loss = gs[_LOSS_ROW, 0]
    small_out = [_unpack_small(t, small_w) for t in (gs, ds, ms, vs)]

    order = ("norm_mix_gain", "norm_mem_gain", "w_in", "lb_logits_fw", "lb_logits_bw", "hg_norm_gain", "da_q_gain",
             "da_k_gain", "w_mem_kv", "mem_q_gain", "mem_k_gain", "w_proj_hg", "w_proj_da", "w_proj_mem", "w_out",
             "norm_ffn_gain", "w_ffn_in", "w_ffn_out")
    outs = [loss, grad_x[None]]
    for kind in range(4):
        for n in order:
            outs.append(big_out[n][kind][None] if n in big_out else small_out[kind][n])
    return tuple(outs)


_BIG_NAMES = ("w_in", "w_mem_kv", "w_proj_hg", "w_proj_da", "w_proj_mem", "w_out", "w_ffn_in", "w_ffn_out")


def _local_step(xs, mems, target, sw, wg):
    step = _local_step_stages(xs, mems, target, sw)
    stacked = {}
    event, payload = next(step)
    while event != "end":
        reply = None
        if event.startswith("grads_"):
            stacked.update(payload)
        elif event == "need_weights":
            reply = wg
        elif event == "gather_proj":
            reply = (wg["w_in"], _matmul(payload, wg["w_in"], "nn", F32, 1024, IN_SHARD, D_MODEL, "proj_fwd",
                                         b_stacked=True, n_outer=True))
        event, payload = step.send(reply)
    return dict(payload, stacked=stacked)


def _local_step_stages(xs, mems, target, sw):
    norm_mix_gain, norm_mem_gain, norm_ffn_gain = sw["norm_mix_gain"], sw["norm_mem_gain"], sw["norm_ffn_gain"]
    lb_logits_fw, lb_logits_bw, hg_norm_gain = sw["lb_logits_fw"], sw["lb_logits_bw"], sw["hg_norm_gain"]
    da_q_gain, da_k_gain, mem_q_gain, mem_k_gain = sw["da_q_gain"], sw["da_k_gain"], sw["mem_q_gain"], sw["mem_k_gain"]

    lb_fw = _lb_table(lb_logits_fw, "lb_table_fw")
    lb_bw = _lb_table(lb_logits_bw, "lb_table_bw")
    lb = jnp.concatenate([lb_fw, lb_bw], axis=0).reshape(2, HG_HEADS, 1, HEAD_DIM)
    h, h_t = _rmsnorm_fwd(xs, norm_mix_gain, "norm_mix_fwd", 512, transposed=True)
    win_st, proj = yield "gather_proj", h
    token = yield "begin", proj
    o_hg, o_pre, states, hg_qt, hg_decay = _gla_fwd(proj, lb, _after(hg_norm_gain, token))
    token = yield "after_gla_fwd", o_hg
    da = [_da_fwd(proj, _after(da_q_gain, token), da_k_gain, g) for g in range(3)]
    o_da, o_da32, lse_da = _da_merge([t[0] for t in da], [t[1] for t in da], 512)
    wg = yield "need_weights", o_da
    wkv = wg["w_mem_kv"].reshape(D_MODEL, 2 * MEM_WIDTH)
    wphg = wg["w_proj_hg"].reshape(D_MODEL, D_MODEL)
    wpda = jnp.transpose(wg["w_proj_da"], (1, 0, 2)).reshape(DA_WIDTH, D_MODEL)
    wpmem = jnp.transpose(wg["w_proj_mem"], (1, 0, 2)).reshape(MEM_WIDTH, D_MODEL)
    wout = wg["w_out"].reshape(D_MODEL, D_MODEL)
    wfin = jnp.transpose(wg["w_ffn_in"], (1, 0, 2)).reshape(D_MODEL, 2 * D_FF)
    wfout = wg["w_ffn_out"].reshape(D_FF, D_MODEL)
    mem_n = _rmsnorm_fwd(mems, norm_mem_gain, "norm_mem_fwd", N_MEM)
    kv = _matmul(mem_n, wkv, "nn", F32, N_MEM, 1024, D_MODEL, "mem_kv_fwd")
    o_mem = _mem_fwd(proj, kv, mem_q_gain, mem_k_gain, 1024)
    branch_w = (wphg, wpda, wpmem)
    merged, t_hg, t_da, t_mem = _branch_merge_fwd(proj, (o_hg, o_da, o_mem), branch_w, 512)
    x1, h2, h2_t = _residual_rmsnorm_fwd(xs, merged, wout, norm_ffn_gain, "out_norm_ffn_fwd", 512)
    ffn_a, ffn_b, act, act_t = _ffn_in_swiglu(h2, wfin, 1024, 1408)
    dy, dyb, loss_part = _ffn_out_loss_head(x1, act, wfout, target, 512)

    dab = _ffn_out_bwd_swiglu(dyb, wfout, ffn_a, ffn_b, 1024, 1408)
    g_wfout = _matmul(act_t, dyb, "nn", BF16, 1408, 1024, 2048, "ffn_out_bwd_w")
    g_wfin = _matmul(h2_t, dab, "nn", BF16, 1024, 1408, 2048, "ffn_in_bwd_w", b_parts=True)
    token = yield "grads_ffn", dict(
        w_ffn_in=jnp.transpose(g_wfin.reshape(D_MODEL, N_DEV, 2 * D_FF // N_DEV), (1, 0, 2)),
        w_ffn_out=g_wfout.reshape(N_DEV, D_FF // N_DEV, D_MODEL))
    dx1, dx1b, g_norm_ffn = _matmul_rmsnorm_bwd(dab, wfin, x1, dy, _after(norm_ffn_gain, token),
                                                "ffn_in_bwd_act_norm", 512, D_FF, a_parts=True)
    dmerged = _matmul(dx1b, wout, "nt", F32, 512, 1024, D_MODEL, "out_bwd_act")
    g_wout = _matmul(merged, dx1b, "tn", BF16, 1024, 1024, 2048, "out_bwd_w")
    dt_hg, dt_da, dt_mem, do_hg, do_da, do_mem, dproj = _gate_merge_bwd(
        proj, (t_hg, t_da, t_mem), branch_w, dmerged, _dproj_buffer(), 256)
    token = yield "after_gate_merge_bwd", dt_hg
    g_wphg = _matmul(o_hg, dt_hg, "tn", BF16, 1024, 1024, 2048, "proj_hg_bwd_w", after=token)
    g_wpda = _matmul(o_da, dt_da, "tn", BF16, DA_WIDTH, D_MODEL, 2048, "proj_da_bwd_w")
    g_wpmem = _matmul(o_mem, dt_mem, "tn", BF16, MEM_WIDTH, D_MODEL, 2048, "proj_mem_bwd_w")
    by_owner = lambda g: jnp.transpose(g.reshape(g.shape[0], N_DEV, D_MODEL // N_DEV), (1, 0, 2))
    g_wpda, g_wpmem = by_owner(g_wpda), by_owner(g_wpmem)

    dproj, dk_mem, dv_mem, g_mem_q, g_mem_k = _mem_bwd(proj, kv, mem_q_gain, mem_k_gain, do_mem, dproj, 1024)
    dkv = jnp.concatenate([dk_mem, dv_mem], axis=1).astype(BF16)
    g_wkv = _matmul(mem_n, dkv, "tn", BF16, 1024, 1024, N_MEM, "mem_kv_bwd_w")
    dmem_n = _matmul(dkv, wkv, "nt", F32, N_MEM, 1024, 1024, "mem_kv_bwd_act")
    g_norm_mem = _gain_grad(mems, dmem_n, "norm_mem_bwd")
    token = yield "grads_mix", dict(
        w_mem_kv=g_wkv.reshape(N_DEV, D_MODEL // N_DEV, 2 * MEM_WIDTH),
        w_proj_hg=g_wphg.reshape(N_DEV, D_MODEL // N_DEV, D_MODEL),
        w_proj_da=g_wpda, w_proj_mem=g_wpmem,
        w_out=g_wout.reshape(N_DEV, D_MODEL // N_DEV, D_MODEL))

    dd_da = _da_rowdot(do_da, o_da32, 512)
    dproj, g_da_q, g_da_k = _da_bwd(proj, _after(da_q_gain, token), da_k_gain, do_da, lse_da, dd_da, dproj, 0)
    token = yield "after_da_bwd_g0", g_da_q
    for g in (1, 2):
        dproj, gq_part, gk_part = _da_bwd(proj, _after(da_q_gain, token), da_k_gain, do_da, lse_da, dd_da, dproj, g)
        g_da_q, g_da_k = g_da_q + gq_part, g_da_k + gk_part

    dproj, dlb, g_hg_norm = _gla_bwd(proj, lb, hg_norm_gain, o_pre, states, hg_qt, hg_decay, do_hg, dproj)
    yield "after_gla_bwd", dlb

    g_win =_matmul(h_t, dproj, "nn", BF16, 1024, IN_SHARD, 2048, "proj_bwd_w", out_stacked=True)
    token = yield "grads_in", dict(w_in=g_win)
    grad_x, g_mix_top = _matmul_rmsnorm_bwd(dproj, win_st, xs, dx1, norm_mix_gain, "proj_bwd_act_norm_top", 1024,
                                            IN_SHARD, b_stacked=True, after=token, m_blocks=(0, 1), with_bf16=False)
    token = yield "after_proj_bwd_act_top", g_mix_top
    grad_x, g_mix_bottom = _matmul_rmsnorm_bwd(dproj, win_st, xs, dx1, norm_mix_gain, "proj_bwd_act_norm_bottom", 1024,
                                               IN_SHARD, b_stacked=True, after=token, m_blocks=(1, 3),
                                               out_into=grad_x, with_bf16=False)
    g_norm_mix = g_mix_top + g_mix_bottom

    gpart = _pack_rows([g_norm_mix, g_norm_mem, dlb[0], dlb[1], g_norm_ffn, g_hg_norm, g_da_q, g_da_k,
                        g_mem_q, g_mem_k, loss_part], SMALL_GRAD_ROWS)
    lbpack = jnp.concatenate([lb_fw.reshape(8, LANE), lb_bw.reshape(8, LANE)], axis=0)
    yield "end", dict(grad_x=grad_x, gpart=gpart, lbpack=lbpack)
```

```python
import numpy as np
import jax
import jax.numpy as jnp
from jax import lax
from jax.experimental import pallas as pl
from jax.experimental.pallas import tpu as pltpu

F32 = jnp.float32
BF16 = jnp.bfloat16
MESH = pl.DeviceIdType.MESH

SEQ = 4096
D_MODEL = 1024
N_DEV = 8
N_MEM = 256
RMS_EPS = 1e-6
NEG_INF = -1e30
LANE = 128
HEAD_DIM = 128
HG_HEADS = 8
HG_CHUNK = 64
HG_SCALE = HEAD_DIM ** -0.5
DA_DILATIONS = (1, 4, 16)
DA_RADIUS = 64
DA_HEADS_PER_GROUP = 4
DA_HEADS = 12
DA_WIDTH = 512
DA_SCALE = HEAD_DIM ** -0.5
DA_QB = 128
DA_WIN = 256
DA_WAYS = 4
DA_FWD_WAYS = 4
MEM_HEADS = 4
MEM_WIDTH = 512
MEM_SCALE = HEAD_DIM ** -0.5
D_FF = 2816
IN_COLS = 13312
IN_SHARD = IN_COLS // N_DEV
CB_HG_Q, CB_F, CB_HG_I, CB_HG_G = 0, 8, 24, 32
CB_DA_Q, CB_DA_K, CB_DA_V, CB_MEM_Q = 40, 52, 64, 76
ADAM_LR, ADAM_B1, ADAM_B2, ADAM_EPS, ADAM_WD, ADAM_STEP = 0.001, 0.9, 0.999, 1e-08, 0.01, 10
VMEM_BYTES_V7X = 64 * 1024 * 1024
SMALL_ROWS = 104
SMALL_GRAD_ROWS = 88

_NN = (((1,), (0,)), ((), ()))
_NT = (((1,), (1,)), ((), ()))
_TN = (((0,), (0,)), ((), ()))


def _dot(a, b, dims):
    return lax.dot_general(a.astype(BF16), b.astype(BF16), dims, preferred_element_type=F32)


def _sigmoid(x):
    return 0.5 * jnp.tanh(0.5 * x) + 0.5


def _params(semantics, est_bytes):
    limit = int(min(VMEM_BYTES_V7X - (6 << 20), max(56 << 20, est_bytes * 3 // 2)))
    return pltpu.CompilerParams(dimension_semantics=semantics, vmem_limit_bytes=limit)


def _nbytes(shape, dtype):
    return int(np.prod(shape)) * jnp.dtype(dtype).itemsize


def _alibi_slopes(n):
    return (2.0 ** (-8.0 * np.arange(1, n + 1) / n)).astype(np.float32)


def _matmul(a, b, mode, out_dtype, tm, tn, tk, name, b_stacked=False, out_stacked=False, n_outer=False, after=None,
            m_blocks=None, out_into=None, a_parts=False, b_parts=False):
    if a_parts:
        assert mode == "nt" and tk == a.shape[2]
        m, kdim = a.shape[1], a.shape[0] * a.shape[2]
    elif mode == "tn":
        kdim, m = a.shape
    else:
        m, kdim = a.shape
    if b_parts:
        assert mode == "nn" and not b_stacked and b.shape[2] % tn == 0
        n = b.shape[0] * b.shape[2]
    elif b_stacked:
        if mode == "nn":
            n = b.shape[0] * b.shape[2]
            assert tn == b.shape[2] and tk == kdim == b.shape[1]
        else:
            assert mode == "nt" and tk == b.shape[2] and b.shape[0] * tk == kdim
            n = b.shape[1]
    else:
        n = b.shape[0] if mode == "nt" else b.shape[1]
    assert m % tm == 0 and n % tn == 0 and kdim % tk == 0
    gm, gn, gk = m // tm, n // tn, kdim // tk
    i0 = 0
    if m_blocks is not None:
        assert mode != "tn" and not out_stacked
        i0, gm = m_blocks

    def ijk(f):
        if n_outer:
            return lambda j, i, k: f(i + i0, j, k)
        return lambda i, j, k: f(i + i0, j, k)

    if a_parts:
        a_spec = pl.BlockSpec((None, tm, tk), ijk(lambda i, j, k: (k, i, 0)))
    elif mode == "tn":
        a_spec = pl.BlockSpec((tk, tm), ijk(lambda i, j, k: (k, i)))
    else:
        a_spec = pl.BlockSpec((tm, tk), ijk(lambda i, j, k: (i, k)))
    if b_parts:
        per_part = b.shape[2] // tn
        b_spec = pl.BlockSpec((None, tk, tn), ijk(lambda i, j, k: (j // per_part, k, j % per_part)))
    elif b_stacked and mode == "nn":
        b_spec = pl.BlockSpec((None, tk, tn), ijk(lambda i, j, k: (j, 0, 0)))
    elif b_stacked:
        b_spec = pl.BlockSpec((None, tn, tk), ijk(lambda i, j, k: (k, j, 0)))
    elif mode == "nt":
        b_spec = pl.BlockSpec((tn, tk), ijk(lambda i, j, k: (j, k)))
    else:
        b_spec = pl.BlockSpec((tk, tn), ijk(lambda i, j, k: (k, j)))
    if out_stacked:
        assert tm == m
        out_shape = jax.ShapeDtypeStruct((gn, m, tn), out_dtype)
        o_spec = pl.BlockSpec((None, tm, tn), ijk(lambda i, j, k: (j, i, 0)))
    else:
        out_shape = jax.ShapeDtypeStruct((m, n), out_dtype)
        o_spec = pl.BlockSpec((tm, tn), ijk(lambda i, j, k: (i, j)))
    dims = {"nn": _NN, "nt": _NT, "tn": _TN}[mode]

    n_in = 2 + (after is not None) + (out_into is not None)

    def body(*refs):
        a_ref, b_ref, o_ref = refs[0], refs[1], refs[n_in]
        part = _dot(a_ref[...], b_ref[...], dims)
        if gk == 1:
            o_ref[...] = part.astype(out_dtype)
            return
        acc_ref = refs[-1]
        k = pl.program_id(2)

        @pl.when(k == 0)
        def _():
            acc_ref[...] = part

        @pl.when(jnp.logical_and(k > 0, k < gk - 1))
        def _():
            acc_ref[...] += part

        @pl.when(k == gk - 1)
        def _():
            o_ref[...] = (acc_ref[...] + part).astype(out_dtype)

    a_tile = _nbytes((tm, tk), a.dtype)
    b_tile = _nbytes((tk, tn), b.dtype)
    o_tile = _nbytes((tm, tn), out_dtype)
    est = 2 * (a_tile + b_tile + o_tile) + 3 * tm * tn * 4 + (a_tile + b_tile)
    grid = (gn, gm, gk) if n_outer else (gm, gn, gk)
    operands, in_specs = [a, b], [a_spec, b_spec]
    if after is not None:
        operands.append(after)
        in_specs.append(pl.BlockSpec(memory_space=pl.ANY))
    aliases = {}
    if out_into is not None:
        aliases = {len(operands): 0}
        operands.append(out_into)
        in_specs.append(pl.BlockSpec(memory_space=pl.ANY))
    return pl.pallas_call(
        body, name=name, grid=grid, in_specs=in_specs, out_specs=o_spec, out_shape=out_shape,
        scratch_shapes=[] if gk == 1 else [pltpu.VMEM((tm, tn), F32)], input_output_aliases=aliases,
        compiler_params=_params(("parallel", "parallel", "arbitrary"), est),
    )(*operands)


def _row_spec(tr, width, col_block=0):
    return pl.BlockSpec((tr, width), lambda i: (i, col_block))


def _bcast_spec(width):
    return pl.BlockSpec((1, width), lambda i: (0, 0))


def _col_spec(width, tr):
    return pl.BlockSpec((width, tr), lambda i: (0, i))


def _rmsnorm_fwd(x, gain, name, tr, transposed=False):
    rows, width = x.shape

    def body(x_ref, g_ref, o_ref, *t_ref):
        xv = x_ref[...]
        r = lax.rsqrt(jnp.mean(xv * xv, axis=-1, keepdims=True) + RMS_EPS)
        h = xv * r * g_ref[...]
        o_ref[...] = h.astype(BF16)
        if transposed:
            t_ref[0][...] = h.T.astype(BF16)

    out_specs, out_shape = [_row_spec(tr, width)], [jax.ShapeDtypeStruct((rows, width), BF16)]
    if transposed:
        out_specs.append(_col_spec(width, tr))
        out_shape.append(jax.ShapeDtypeStruct((width, rows), BF16))
    out = pl.pallas_call(
        body, name=name, grid=(rows // tr,), in_specs=[_row_spec(tr, width), _bcast_spec(width)],
        out_specs=out_specs, out_shape=out_shape,
        compiler_params=_params(("parallel",), 10 * tr * width * 4),
    )(x, gain)
    return out if transposed else out[0]


def _residual_rmsnorm_fwd(x, merged, wout, gain, name, tr):
    rows, width = x.shape

    def body(x_ref, m_ref, w_ref, g_ref, x1_ref, h_ref, ht_ref):
        xv = x_ref[...] + _dot(m_ref[...], w_ref[...], _NN)
        x1_ref[...] = xv
        r = lax.rsqrt(jnp.mean(xv * xv, axis=-1, keepdims=True) + RMS_EPS)
        h = xv * r * g_ref[...]
        h_ref[...] = h.astype(BF16)
        ht_ref[...] = h.T.astype(BF16)

    return pl.pallas_call(
        body, name=name, grid=(rows // tr,),
        in_specs=[_row_spec(tr, width), _row_spec(tr, merged.shape[1]),
                  pl.BlockSpec(wout.shape, lambda i: (0, 0)), _bcast_spec(width)],
        out_specs=[_row_spec(tr, width), _row_spec(tr, width), _col_spec(width, tr)],
        out_shape=[jax.ShapeDtypeStruct((rows, width), F32), jax.ShapeDtypeStruct((rows, width), BF16),
                   jax.ShapeDtypeStruct((width, rows), BF16)],
        compiler_params=_params(("parallel",), 14 * tr * width * 4),
    )(x, merged, wout, gain)


def _matmul_rmsnorm_bwd(a, b, x, dres, gain, name, tm, tk, a_parts=False, b_stacked=False, after=None,
                        m_blocks=None, out_into=None, with_bf16=True):
    width = x.shape[1]
    m = a.shape[1] if a_parts else a.shape[0]
    kdim = a.shape[0] * a.shape[2] if a_parts else a.shape[1]
    gk = kdim // tk
    i0, gm = (0, m // tm) if m_blocks is None else m_blocks
    n_in = 5 + (after is not None) + (out_into is not None)

    def body(*refs):
        a_ref, b_ref, x_ref, dres_ref, g_ref = refs[:5]
        outs = refs[n_in:]
        dx_ref, dg_ref, acc_ref = outs[0], outs[-2], outs[-1]
        i, k = pl.program_id(0), pl.program_id(1)
        part = _dot(a_ref[...], b_ref[...], _NT)

        @pl.when(k == 0)
        def _():
            acc_ref[...] = part

        @pl.when(jnp.logical_and(k > 0, k < gk - 1))
        def _():
            acc_ref[...] += part

        @pl.when(k == gk - 1)
        def _():
            dhv = acc_ref[...] + part if gk > 1 else part
            xv = x_ref[...]
            r = lax.rsqrt(jnp.mean(xv * xv, axis=-1, keepdims=True) + RMS_EPS)
            xhat = xv * r
            dyg = dhv * g_ref[...]
            dx = dres_ref[...] + r * (dyg - xhat * jnp.mean(dyg * xhat, axis=-1, keepdims=True))
            dx_ref[...] = dx
            if with_bf16:
                outs[1][...] = dx.astype(BF16)
            gpart = jnp.sum(dhv * xhat, axis=0, keepdims=True)

            @pl.when(i == 0)
            def _():
                dg_ref[...] = gpart

            @pl.when(i > 0)
            def _():
                dg_ref[...] += gpart

    rows = lambda width_: pl.BlockSpec((tm, width_), lambda i, k: (i + i0, 0))
    if a_parts:
        a_spec = pl.BlockSpec((None, tm, tk), lambda i, k: (k, i + i0, 0))
    else:
        a_spec = pl.BlockSpec((tm, tk), lambda i, k: (i + i0, k))
    if b_stacked:
        b_spec = pl.BlockSpec((None, width, tk), lambda i, k: (k, 0, 0))
    else:
        b_spec = pl.BlockSpec((width, tk), lambda i, k: (0, k))
    operands = [a, b, x, dres, gain]
    in_specs = [a_spec, b_spec, rows(width), rows(width), pl.BlockSpec((1, width), lambda i, k: (0, 0))]
    for extra in (after, out_into):
        if extra is not None:
            operands.append(extra)
            in_specs.append(pl.BlockSpec(memory_space=pl.ANY))
    aliases = {} if out_into is None else {len(operands) - 1: 0}
    out_specs = [rows(width)] + ([rows(width)] if with_bf16 else []) + [pl.BlockSpec((1, width), lambda i, k: (0, 0))]
    out_shape = [jax.ShapeDtypeStruct((m, width), F32)] + ([jax.ShapeDtypeStruct((m, width), BF16)] if with_bf16 else [])
    out_shape.append(jax.ShapeDtypeStruct((1, width), F32))
    est = 4 * tm * tk + 4 * width * tk + 12 * tm * width * 4
    return pl.pallas_call(
        body, name=name, grid=(gm, gk), in_specs=in_specs, out_specs=out_specs, out_shape=out_shape,
        scratch_shapes=[pltpu.VMEM((tm, width), F32)], input_output_aliases=aliases,
        compiler_params=_params(("arbitrary", "arbitrary"), est),
    )(*operands)


def _gain_grad(x, dh, name):
    rows, width = x.shape

    def body(x_ref, dh_ref, dg_ref):
        xv = x_ref[...]
        r = lax.rsqrt(jnp.mean(xv * xv, axis=-1, keepdims=True) + RMS_EPS)
        dg_ref[...] = jnp.sum(dh_ref[...] * xv * r, axis=0, keepdims=True)

    return pl.pallas_call(
        body, name=name, grid=(1,), in_specs=[_row_spec(rows, width), _row_spec(rows, width)],
        out_specs=_bcast_spec(width), out_shape=jax.ShapeDtypeStruct((1, width), F32),
        compiler_params=_params(("arbitrary",), 6 * rows * width * 4),
    )(x, dh)


def _lb_table(logits, name):
    slots, width = logits.shape

    def body(l_ref, o_ref):
        lv = l_ref[...]
        mx = jnp.max(lv, axis=0, keepdims=True)
        e = jnp.exp(lv - mx)
        o_ref[...] = e[0:1, :] / jnp.sum(e, axis=0, keepdims=True)

    return pl.pallas_call(
        body, name=name, grid=(1,), in_specs=[pl.BlockSpec((slots, width), lambda i: (0, 0))],
        out_specs=_bcast_spec(width), out_shape=jax.ShapeDtypeStruct((1, width), F32),
    )(logits)


def _branch_merge_fwd(proj, outs, weights, tr):
    w = D_MODEL

    def body(ghg_ref, gda_ref, gmem_ref, ohg_ref, oda_ref, omem_ref, whg_ref, wda_ref, wmem_ref,
             m_ref, thg_ref, tda_ref, tmem_ref):
        acc = None
        for g_ref, o_ref, w_ref, t_ref in ((ghg_ref, ohg_ref, whg_ref, thg_ref), (gda_ref, oda_ref, wda_ref, tda_ref),
                                           (gmem_ref, omem_ref, wmem_ref, tmem_ref)):
            t = _dot(o_ref[...], w_ref[...], _NN)
            t_ref[...] = t.astype(BF16)
            term = _sigmoid(g_ref[...]) * t
            acc = term if acc is None else acc + term
        m_ref[...] = acc.astype(BF16)

    whole = lambda a: pl.BlockSpec(a.shape, lambda i: (0, 0))
    shape = jax.ShapeDtypeStruct((SEQ, w), BF16)
    return pl.pallas_call(
        body, name="branch_merge_fwd", grid=(SEQ // tr,),
        in_specs=[_row_spec(tr, w, 10), _row_spec(tr, w, 11), _row_spec(tr, w, 12)]
        + [_row_spec(tr, o.shape[1]) for o in outs] + [whole(wt) for wt in weights],
        out_specs=[_row_spec(tr, w)] * 4, out_shape=[shape] * 4,
        compiler_params=_params(("parallel",), 20 * tr * w * 4),
    )(proj, proj, proj, *outs, *weights)


def _dproj_buffer():
    return lax.empty((SEQ, IN_COLS), BF16)


def _gate_merge_bwd(proj, ts, weights, dmerged, dproj, tr):
    w = D_MODEL
    steps = SEQ // tr
    gate_col0 = 10 * w

    def body(ghg_ref, gda_ref, gmem_ref, thg_ref, tda_ref, tmem_ref, whg_ref, wda_ref, wmem_ref, dm_ref, dproj_in,
             dthg_ref, dtda_ref, dtmem_ref, dohg_ref, doda_ref, domem_ref, dproj_ref, stage, sems):
        del dproj_in
        i = pl.program_id(0)
        slot = i % 2

        def slot_copy(s):
            rows = pl.ds(pl.multiple_of(i * tr, tr), tr)
            return pltpu.make_async_copy(stage.at[s], dproj_ref.at[rows, pl.ds(gate_col0, 3 * w)], sems.at[s])

        @pl.when(i >= 2)
        def _():
            slot_copy(slot).wait()

        dm = dm_ref[...]
        branches = ((ghg_ref, thg_ref, whg_ref, dthg_ref, dohg_ref), (gda_ref, tda_ref, wda_ref, dtda_ref, doda_ref),
                    (gmem_ref, tmem_ref, wmem_ref, dtmem_ref, domem_ref))
        for b, (g_ref, t_ref, w_ref, dt_ref, do_ref) in enumerate(branches):
            s = _sigmoid(g_ref[...])
            dt = (s * dm).astype(BF16)
            dt_ref[...] = dt
            do_ref[...] = _dot(dt, w_ref[...], _NT)
            stage[slot, :, b * w:(b + 1) * w] = (dm * t_ref[...].astype(F32) * s * (1.0 - s)).astype(BF16)
        slot_copy(slot).start()

        @pl.when(i == steps - 1)
        def _():
            slot_copy(1 - slot).wait()
            slot_copy(slot).wait()

    assert steps >= 2
    whole = lambda a: pl.BlockSpec(a.shape, lambda i: (0, 0))
    return pl.pallas_call(
        body, name="gate_merge_bwd", grid=(steps,),
        in_specs=[_row_spec(tr, w, 10), _row_spec(tr, w, 11), _row_spec(tr, w, 12)] + [_row_spec(tr, w)] * 3
        + [whole(wt) for wt in weights] + [_row_spec(tr, w), _any_spec()],
        out_specs=[_row_spec(tr, w)] * 3 + [_row_spec(tr, wt.shape[0]) for wt in weights] + [_any_spec()],
        out_shape=[jax.ShapeDtypeStruct((SEQ, w), BF16)] * 3
        + [jax.ShapeDtypeStruct((SEQ, wt.shape[0]), F32) for wt in weights]
        + [jax.ShapeDtypeStruct((SEQ, IN_COLS), BF16)],
        scratch_shapes=[pltpu.VMEM((2, tr, 3 * w), BF16), pltpu.SemaphoreType.DMA((2,))],
        input_output_aliases={10: 6},
        compiler_params=_params(("arbitrary",), 34 * tr * w * 4),
    )(proj, proj, proj, *ts, *weights, dmerged, dproj)


def _ffn_in_swiglu(h2, wfin, tm, tn):
    gm, gn = SEQ // tm, D_FF // tn

    def body(h_ref, wa_ref, wb_ref, a_ref, b_ref, act_ref, actt_ref):
        h = h_ref[...]
        a = _dot(h, wa_ref[...], _NN)
        b = _dot(h, wb_ref[...], _NN)
        act = a * _sigmoid(a) * b
        a_ref[...] = a.astype(BF16)
        b_ref[...] = b.astype(BF16)
        act_ref[...] = act.astype(BF16)
        actt_ref[...] = act.T.astype(BF16)

    tile = pl.BlockSpec((tm, tn), lambda j, i: (i, j))
    shape = jax.ShapeDtypeStruct((SEQ, D_FF), BF16)
    return pl.pallas_call(
        body, name="ffn_in_swiglu_fwd", grid=(gn, gm),
        in_specs=[pl.BlockSpec((tm, D_MODEL), lambda j, i: (i, 0)),
                  pl.BlockSpec((D_MODEL, tn), lambda j, i: (0, j)),
                  pl.BlockSpec((D_MODEL, tn), lambda j, i: (0, gn + j))],
        out_specs=[tile, tile, tile, pl.BlockSpec((tn, tm), lambda j, i: (j, i))],
        out_shape=[shape, shape, shape, jax.ShapeDtypeStruct((D_FF, SEQ), BF16)],
        compiler_params=_params(("parallel", "parallel"), 4 * tm * D_MODEL + 8 * D_MODEL * tn + 16 * tm * tn
                                + 6 * tm * tn * 4),
    )(h2, wfin, wfin)


def _ffn_out_bwd_swiglu(dyb, wfout, a, b, tm, tn):
    gm, gn = SEQ // tm, D_FF // tn

    def body(dy_ref, w_ref, a_ref, b_ref, o_ref):
        d = _dot(dy_ref[...], w_ref[...], _NT)
        av = a_ref[...].astype(F32)
        bv = b_ref[...].astype(F32)
        s = _sigmoid(av)
        silu = av * s
        o_ref[0] = (d * bv * (s + silu * (1.0 - s))).astype(BF16)
        o_ref[1] = (d * silu).astype(BF16)

    tile = pl.BlockSpec((tm, tn), lambda i, j: (i, j))
    return pl.pallas_call(
        body, name="ffn_out_bwd_swiglu", grid=(gm, gn),
        in_specs=[pl.BlockSpec((tm, D_MODEL), lambda i, j: (i, 0)), pl.BlockSpec((tn, D_MODEL), lambda i, j: (j, 0)),
                  tile, tile],
        out_specs=pl.BlockSpec((2, tm, tn), lambda i, j: (0, i, j)),
        out_shape=jax.ShapeDtypeStruct((2, SEQ, D_FF), BF16),
        compiler_params=_params(("parallel", "parallel"), 4 * tm * D_MODEL + 4 * tn * D_MODEL + 16 * tm * tn
                                + 6 * tm * tn * 4),
    )(dyb, wfout, a, b)


def _ffn_out_loss_head(x1, act, wfout, target, tr):
    w = D_MODEL

    def body(x_ref, a_ref, w_ref, t_ref, dy_ref, dyb_ref, loss_ref, acc_ref):
        err = x_ref[...] + _dot(a_ref[...], w_ref[...], _NN) - t_ref[...]
        dy = err * (1.0 / w)
        dy_ref[...] = dy
        dyb_ref[...] = dy.astype(BF16)
        part = jnp.sum(err * err, axis=0, keepdims=True)

        @pl.when(pl.program_id(0) == 0)
        def _():
            acc_ref[...] = part

        @pl.when(pl.program_id(0) > 0)
        def _():
            acc_ref[...] += part

        @pl.when(pl.program_id(0) == SEQ // tr - 1)
        def _():
            total = jnp.sum(acc_ref[...], axis=1, keepdims=True) * (0.5 / w)
            loss_ref[...] = jnp.broadcast_to(total, (1, LANE))

    return pl.pallas_call(
        body, name="ffn_out_loss_head", grid=(SEQ // tr,),
        in_specs=[_row_spec(tr, w), _row_spec(tr, D_FF), pl.BlockSpec((D_FF, w), lambda i: (0, 0)), _row_spec(tr, w)],
        out_specs=[_row_spec(tr, w), _row_spec(tr, w), _bcast_spec(LANE)],
        out_shape=[jax.ShapeDtypeStruct((SEQ, w), F32), jax.ShapeDtypeStruct((SEQ, w), BF16),
                   jax.ShapeDtypeStruct((1, LANE), F32)],
        scratch_shapes=[pltpu.VMEM((1, w), F32)],
        compiler_params=_params(("arbitrary",), 12 * tr * w * 4 + 4 * D_FF * w),
    )(x1, act, wfout, target)


GLA_ROWS = 256
GLA_CPB = GLA_ROWS // HG_CHUNK
GLA_NBLK = SEQ // GLA_ROWS
GLA_WAYS = 4
GLA_TRIPS = GLA_NBLK // GLA_WAYS
GLA_GRAD_WAYS = 4
GLA_GRAD_TRIPS = GLA_NBLK // GLA_GRAD_WAYS
GLA_NCK = SEQ // HG_CHUNK
GLA_INTER_WAYS = 8


def _dot_split(m, xv, dims, terms=3):
    dot = lambda t: lax.dot_general(m, t, dims, preferred_element_type=F32)
    hi = xv.astype(BF16)
    r1 = xv - hi.astype(F32)
    mid = r1.astype(BF16)
    if terms == 2:
        return dot(mid) + dot(hi)
    lo = (r1 - mid.astype(F32)).astype(BF16)
    return (dot(lo) + dot(mid)) + dot(hi)


def _gla_block(qc, fc, lbv, masks):
    mask, maskb, direction = masks
    sq = _sigmoid(qc)
    q = qc * sq * HG_SCALE
    sf = _sigmoid(fc)
    forget = lbv + (1.0 - lbv) * sf
    k = 1.0 - forget
    logf = jnp.log(forget)
    b = _dot_split(maskb, logf, _NN)
    ends = []
    for j in range(GLA_CPB):
        lo, hi = j * HG_CHUNK, (j + 1) * HG_CHUNK
        end = jnp.where(direction == 0, b[hi - 1:hi, :], b[lo:lo + 1, :])
        ends.append(jnp.broadcast_to(end, (HG_CHUNK, HEAD_DIM)))
    bt = jnp.concatenate(ends, axis=0)
    eb = jnp.exp(b)
    qt = q * eb
    kt = k * jnp.exp(-b)
    kh = k * jnp.exp(bt - b)
    a = jnp.where(mask, _dot(qt, kt, _NT), 0.0)
    return dict(sq=sq, sf=sf, forget=forget, k=k, b=b, bt=bt, eb=eb, qt=qt, kt=kt, kh=kh, a=a)


def _gla_masks(direction):
    row = lax.broadcasted_iota(jnp.int32, (GLA_ROWS, GLA_ROWS), 0)
    col = lax.broadcasted_iota(jnp.int32, (GLA_ROWS, GLA_ROWS), 1)
    same = (row // HG_CHUNK) == (col // HG_CHUNK)
    mask = jnp.logical_and(same, jnp.where(direction == 0, row - col, col - row) >= 0)
    return mask, jnp.where(mask, 1.0, 0.0).astype(BF16), direction


def _gla_chunk_rows(j):
    return slice(j * HG_CHUNK, (j + 1) * HG_CHUNK)


def _gla_block_rows(b):
    return pl.ds(pl.multiple_of(b * GLA_ROWS, GLA_ROWS), GLA_ROWS)


def _head_spec(col_block0):
    return pl.BlockSpec((SEQ, HEAD_DIM), lambda h, d: (0, col_block0 + h))


def _gla_fwd(proj, lb, gain):
    nck = GLA_NCK

    def body(q_ref, f_ref, v_ref, g_ref, lb_ref, gain_ref, ohg_ref, opre_ref, st_ref, qt_scr, dec_scr, cs_scr):
        d = pl.program_id(1)
        masks = _gla_masks(d)
        lbv = lb_ref[...]

        @pl.when(d == 0)
        def _():
            opre_ref[...] = jnp.zeros_like(opre_ref)

        def intra(s, carry):
            blocks = [s + w * GLA_TRIPS for w in range(GLA_WAYS)]
            rows = [_gla_block_rows(b) for b in blocks]
            loaded = [(q_ref[r, :], f_ref[r, :], v_ref[r, :], opre_ref[r, :]) for r in rows]
            results = []
            for qc, fc, v, o_prev in loaded:
                ck = _gla_block(qc, fc, lbv, masks)
                o = o_prev + _dot(ck["a"], v, _NN)
                cs = [_dot(v[_gla_chunk_rows(j), :], ck["kh"][_gla_chunk_rows(j), :], _TN) for j in range(GLA_CPB)]
                dec = [jnp.exp(ck["bt"][j * HG_CHUNK:j * HG_CHUNK + 8, :]) for j in range(GLA_CPB)]
                results.append((o, ck["qt"].astype(BF16), cs, dec))
            for b, r, (o, qt, cs, dec) in zip(blocks, rows, results):
                opre_ref[r, :] = o
                qt_scr[r, :] = qt
                for j in range(GLA_CPB):
                    cs_scr[b * GLA_CPB + j] = cs[j]
                    dec_scr[b * GLA_CPB + j] = dec[j]
            return carry

        lax.fori_loop(0, GLA_TRIPS, intra, 0)

        def scan(i, st):
            c = jnp.where(d == 0, i, nck - 1 - i)
            st_ref[c] = st
            return st * dec_scr[c][0:1, :] + cs_scr[c]

        lax.fori_loop(0, nck, scan, jnp.zeros((HEAD_DIM, HEAD_DIM), F32), unroll=4)

        def inter(t, carry):
            chunks = [t + w * (nck // GLA_INTER_WAYS) for w in range(GLA_INTER_WAYS)]
            rows = [pl.ds(pl.multiple_of(c * HG_CHUNK, HG_CHUNK), HG_CHUNK) for c in chunks]
            loaded = [(opre_ref[r, :], qt_scr[r, :], st_ref[c]) for r, c in zip(rows, chunks)]
            for r, o in zip(rows, [o_prev + _dot(qt, st, _NT) for o_prev, qt, st in loaded]):
                opre_ref[r, :] = o
            return carry

        lax.fori_loop(0, nck // GLA_INTER_WAYS, inter, 0)

        @pl.when(d == 1)
        def _():
            o = opre_ref[...]
            r = lax.rsqrt(jnp.mean(o * o, axis=-1, keepdims=True) + RMS_EPS)
            g = g_ref[...]
            ohg_ref[...] = (o * r * gain_ref[...] * (g * _sigmoid(g))).astype(BF16)

    blk = SEQ * HEAD_DIM * 4
    return pl.pallas_call(
        body, name="gla_fwd", grid=(HG_HEADS, 2),
        in_specs=[_head_spec(CB_HG_Q),
                  pl.BlockSpec((SEQ, HEAD_DIM), lambda h, d: (0, CB_F + 8 * d + h)),
                  _head_spec(CB_HG_I), _head_spec(CB_HG_G),
                  pl.BlockSpec((None, None, 1, HEAD_DIM), lambda h, d: (d, h, 0, 0)),
                  pl.BlockSpec((1, HEAD_DIM), lambda h, d: (0, 0))],
        out_specs=[_head_spec(0), _head_spec(0),
                   pl.BlockSpec((None, None, nck, HEAD_DIM, HEAD_DIM), lambda h, d: (h, d, 0, 0, 0)),
                   pl.BlockSpec((None, SEQ, HEAD_DIM), lambda h, d: (d, 0, h)),
                   pl.BlockSpec((None, None, nck, 8, HEAD_DIM), lambda h, d: (h, d, 0, 0, 0))],
        out_shape=[jax.ShapeDtypeStruct((SEQ, D_MODEL), BF16), jax.ShapeDtypeStruct((SEQ, D_MODEL), F32),
                   jax.ShapeDtypeStruct((HG_HEADS, 2, nck, HEAD_DIM, HEAD_DIM), F32),
                   jax.ShapeDtypeStruct((2, SEQ, D_MODEL), BF16),
                   jax.ShapeDtypeStruct((HG_HEADS, 2, nck, 8, HEAD_DIM), F32)],
        scratch_shapes=[pltpu.VMEM((nck, HEAD_DIM, HEAD_DIM), F32)],
        compiler_params=_params(("parallel", "arbitrary"), 8 * blk + 3 * blk + 2 * blk + 2 * blk + blk),
    )(proj, proj, proj, proj, lb, gain)


def _gla_bwd(proj, lb, gain, o_pre, states, qt_all, dec_all, do_hg, dproj):
    nck = GLA_NCK
    do, dproj, dgain = _gla_bwd_norm(proj, gain, o_pre, do_hg, dproj, 256)

    def body(q_ref, f_ref, v_ref, lb_ref, do_ref, st_ref, qt_ref, dec_scr, dproj_in, dproj_ref, dlb_ref,
             dq_acc, dv_acc, dst_scr, cs_scr, df_out, dq_out, dv_out, sems):
        del dproj_in
        h, d = pl.program_id(0), pl.program_id(1)
        masks = _gla_masks(d)
        mask, maskb, _ = masks
        lbv = lb_ref[...]

        def column_copy(k, staging, col_block):
            cols = pl.ds(pl.multiple_of(col_block * LANE, LANE), LANE)
            return pltpu.make_async_copy(staging, dproj_ref.at[:, cols], sems.at[k])

        df_copy = column_copy(0, df_out, CB_F + 8 * d + h)
        dq_copy = column_copy(1, dq_out, CB_HG_Q + h)
        dv_copy = column_copy(2, dv_out, CB_HG_I + h)
        last = jnp.logical_and(h == HG_HEADS - 1, d == 1)

        def intra(s, carry):
            blocks = [s + w * GLA_TRIPS for w in range(GLA_WAYS)]
            loaded = [(qt_ref[r, :], do_ref[r, :]) for r in map(_gla_block_rows, blocks)]
            results = [[_dot(doc[_gla_chunk_rows(j), :], qt[_gla_chunk_rows(j), :], _TN) for j in range(GLA_CPB)]
                       for qt, doc in loaded]
            for b, cs in zip(blocks, results):
                for j in range(GLA_CPB):
                    cs_scr[b * GLA_CPB + j] = cs[j]
            return carry

        lax.fori_loop(0, GLA_TRIPS, intra, 0)

        def scan(i, dst):
            c = jnp.where(d == 0, nck - 1 - i, i)
            dst_scr[c] = dst
            return dst * dec_scr[c][0:1, :] + cs_scr[c]

        lax.fori_loop(0, nck, scan, jnp.zeros((HEAD_DIM, HEAD_DIM), F32), unroll=4)

        @pl.when(d == 0)
        def _():
            dq_acc[...] = jnp.zeros_like(dq_acc)
            dv_acc[...] = jnp.zeros_like(dv_acc)

        def block_grads(qc, fc, v, doc, states_in, dstates, decays):
            ck = _gla_block(qc, fc, lbv, masks)
            qt, kt, kh, a = ck["qt"], ck["kt"], ck["kh"], ck["a"]
            da = jnp.where(mask, _dot(doc, v, _NT), 0.0)
            dqt_i = _dot(da, kt, _NN)
            dkt = _dot(da, qt, _TN)
            dv_i = _dot(a, doc, _TN)
            dqt_p, dv_p, dkh_p, dbt_p = [], [], [], []
            for j in range(GLA_CPB):
                cr = _gla_chunk_rows(j)
                st_in, dst = states_in[j], dstates[j]
                dqt_p.append(dqt_i[cr, :] + _dot(doc[cr, :], st_in, _NN))
                dv_p.append(dv_i[cr, :] + _dot(kh[cr, :], dst, _NT))
                dkh_j = _dot(v[cr, :], dst, _NN)
                dkh_p.append(dkh_j)
                dbt_j = (decays[j][0:1, :] * jnp.sum(dst * st_in, axis=0, keepdims=True)
                         + jnp.sum(dkh_j * kh[cr, :], axis=0, keepdims=True))
                dbt_p.append(jnp.broadcast_to(dbt_j, (HG_CHUNK, HEAD_DIM)))
            dqt = jnp.concatenate(dqt_p, axis=0)
            dv = jnp.concatenate(dv_p, axis=0)
            dkh = jnp.concatenate(dkh_p, axis=0)
            dbt = jnp.concatenate(dbt_p, axis=0)
            db = dqt * qt - dkt * kt - dkh * kh
            dq = dqt * ck["eb"]
            dk = dkt * jnp.exp(-ck["b"]) + dkh * jnp.exp(ck["bt"] - ck["b"])
            dlogf = _dot_split(maskb, db, _TN, terms=2) + dbt
            dforget = dlogf / ck["forget"] - dk
            sf, sq = ck["sf"], ck["sq"]
            df = (dforget * (1.0 - lbv) * sf * (1.0 - sf)).astype(BF16)
            dqc = dq * HG_SCALE * (sq + qc * sq * (1.0 - sq))
            return df, dqc, dv, jnp.sum(dforget * (1.0 - sf), axis=0, keepdims=True)

        def grads(s, dlb):
            blocks = [s + w * GLA_GRAD_TRIPS for w in range(GLA_GRAD_WAYS)]
            rows = [_gla_block_rows(b) for b in blocks]
            loaded = []
            for b, r in zip(blocks, rows):
                chunks = [b * GLA_CPB + j for j in range(GLA_CPB)]
                loaded.append((q_ref[r, :], f_ref[r, :], v_ref[r, :], do_ref[r, :], [st_ref[c] for c in chunks],
                               [dst_scr[c] for c in chunks], [dec_scr[c] for c in chunks], dq_acc[r, :], dv_acc[r, :]))
            results = [block_grads(*t[:7]) + (t[7], t[8]) for t in loaded]
            for r, (df, dqc, dv, dlb_part, dq_prev, dv_prev) in zip(rows, results):
                df_out[r, :] = df
                dq_acc[r, :] = dq_prev + dqc
                dv_acc[r, :] = dv_prev + dv
                dlb = dlb + dlb_part
            return dlb

        @pl.when(jnp.logical_or(h > 0, d > 0))
        def _():
            df_copy.wait()

        dlb_ref[...] = lax.fori_loop(0, GLA_GRAD_TRIPS, grads, jnp.zeros((1, HEAD_DIM), F32))
        df_copy.start()

        @pl.when(d == 1)
        def _():
            @pl.when(h > 0)
            def _():
                dq_copy.wait()
                dv_copy.wait()

            dq_out[...] = dq_acc[...].astype(BF16)
            dv_out[...] = dv_acc[...].astype(BF16)
            dq_copy.start()
            dv_copy.start()

        @pl.when(last)
        def _():
            df_copy.wait()
            dq_copy.wait()
            dv_copy.wait()

    blk = SEQ * HEAD_DIM * 4
    state_bytes = nck * HEAD_DIM * HEAD_DIM * 4
    dproj, dlb = pl.pallas_call(
        body, name="gla_bwd", grid=(HG_HEADS, 2),
        in_specs=[_head_spec(CB_HG_Q),
                  pl.BlockSpec((SEQ, HEAD_DIM), lambda h, d: (0, CB_F + 8 * d + h)),
                  _head_spec(CB_HG_I),
                  pl.BlockSpec((None, None, 1, HEAD_DIM), lambda h, d: (d, h, 0, 0)),
                  _head_spec(0),
                  pl.BlockSpec((None, None, nck, HEAD_DIM, HEAD_DIM), lambda h, d: (h, d, 0, 0, 0)),
                  pl.BlockSpec((None, SEQ, HEAD_DIM), lambda h, d: (d, 0, h)),
                  pl.BlockSpec((None, None, nck, 8, HEAD_DIM), lambda h, d: (h, d, 0, 0, 0)),
                  _any_spec()],
        out_specs=[_any_spec(), pl.BlockSpec((None, None, 1, HEAD_DIM), lambda h, d: (d, h, 0, 0))],
        out_shape=[jax.ShapeDtypeStruct((SEQ, IN_COLS), BF16), jax.ShapeDtypeStruct((2, HG_HEADS, 1, HEAD_DIM), F32)],
        scratch_shapes=[pltpu.VMEM((SEQ, HEAD_DIM), F32)] * 2
        + [pltpu.VMEM((nck, HEAD_DIM, HEAD_DIM), F32)] * 2
        + [pltpu.VMEM((SEQ, HEAD_DIM), BF16)] * 3 + [pltpu.SemaphoreType.DMA((3,))],
        input_output_aliases={8: 0},
        compiler_params=_params(("arbitrary", "arbitrary"), 8 * blk + 4 * state_bytes + 3 * blk + 3 * blk),
    )(proj, proj, proj, lb, do, states, qt_all, dec_all, dproj)
    return dproj, dlb, dgain


def _gla_bwd_norm(proj, gain, o_pre, do_hg, dproj, tr):
    w = D_MODEL

    def body(g_ref, gain_ref, opre_ref, dohg_ref, dproj_in, do_ref, dg_ref, dgain_ref):
        del dproj_in
        gainv = gain_ref[...]
        part = jnp.zeros((1, HEAD_DIM), F32)
        for h in range(HG_HEADS):
            hs = slice(h * HEAD_DIM, (h + 1) * HEAD_DIM)
            o = opre_ref[:, hs]
            r = lax.rsqrt(jnp.mean(o * o, axis=-1, keepdims=True) + RMS_EPS)
            ohat = o * r
            g = g_ref[:, hs]
            sg = _sigmoid(g)
            silu = g * sg
            dout = dohg_ref[:, hs]
            dg_ref[:, hs] = (dout * ohat * gainv * (sg + silu * (1.0 - sg))).astype(BF16)
            dy = dout * silu
            part = part + jnp.sum(dy * ohat, axis=0, keepdims=True)
            dn = dy * gainv
            do_ref[:, hs] = r * (dn - ohat * jnp.mean(dn * ohat, axis=-1, keepdims=True))

        @pl.when(pl.program_id(0) == 0)
        def _():
            dgain_ref[...] = part

        @pl.when(pl.program_id(0) > 0)
        def _():
            dgain_ref[...] += part

    return pl.pallas_call(
        body, name="gla_bwd_norm", grid=(SEQ // tr,),
        in_specs=[_row_spec(tr, w, CB_HG_G * LANE // w), _bcast_spec(HEAD_DIM), _row_spec(tr, w), _row_spec(tr, w),
                  _any_spec()],
        out_specs=[_row_spec(tr, w), _row_spec(tr, w, CB_HG_G * LANE // w), _bcast_spec(HEAD_DIM)],
        out_shape=[jax.ShapeDtypeStruct((SEQ, w), F32), jax.ShapeDtypeStruct((SEQ, IN_COLS), BF16),
                   jax.ShapeDtypeStruct((1, HEAD_DIM), F32)],
        input_output_aliases={4: 1},
        compiler_params=_params(("arbitrary",), 16 * tr * w * 4),
    )(proj, gain, o_pre, do_hg, dproj)


def _da_residue_rows(r, n0, size, d):
    if d == 1:
        return pl.ds(pl.multiple_of(n0, 8), size)
    return pl.ds(r + n0 * d, size, stride=d)


def _da_residue_ways(d, nqb):
    return min(d, 4) if nqb <= 2 else 1


def _da_rmsnorm(x, gain):
    r = lax.rsqrt(jnp.mean(x * x, axis=-1, keepdims=True) + RMS_EPS)
    return x * r, r, x * r * gain


def _da_scores(qn_scr, kn_scr, slope, i, ld):
    w0 = jnp.clip(i * DA_QB - DA_RADIUS, 0, ld - DA_WIN)
    w0 = pl.multiple_of(w0, DA_RADIUS)
    qrows = pl.ds(pl.multiple_of(i * DA_QB, DA_QB), DA_QB)
    win = pl.ds(w0, DA_WIN)
    qb = qn_scr[qrows, :]
    kw = kn_scr[win, :]
    s = _dot(qb, kw, _NT) * DA_SCALE
    qpos = i * DA_QB + lax.broadcasted_iota(jnp.int32, (DA_QB, DA_WIN), 0)
    kpos = w0 + lax.broadcasted_iota(jnp.int32, (DA_QB, DA_WIN), 1)
    arel = jnp.abs(kpos - qpos)
    s = s - slope * arel.astype(F32)
    s = jnp.where(arel <= DA_RADIUS, s, NEG_INF)
    return s, qb, kw, qrows, win


def _da_slopes(group):
    d = DA_DILATIONS[group]
    sl = _alibi_slopes(DA_HEADS)[4 * group:4 * group + 4] * d
    return jnp.asarray(np.broadcast_to(sl[:, None, None], (4, 1, LANE)).copy())


def _da_fwd(proj, gq, gk, group, others=()):
    d = DA_DILATIONS[group]
    ld = SEQ // d
    nqb = ld // DA_QB
    n_other = 2 * len(others)

    ways = min(DA_FWD_WAYS, nqb)
    rways = _da_residue_ways(d, nqb)

    def body(q_ref, k_ref, v_ref, gq_ref, gk_ref, sl_ref, *refs):
        other_refs, refs = refs[:n_other], refs[n_other:]
        if others:
            ob_ref, of_ref, lsej_ref, qn_scr, kn_scr, v_scr, o_ref, lse_ref = refs
        else:
            o_ref, lse_ref, qn_scr, kn_scr, v_scr = refs
        slope = sl_ref[:, 0:1]

        def residues(t, carry):
            rs = [t * rways + u for u in range(rways)]
            for u, r in enumerate(rs):
                sel = _da_residue_rows(r, 0, ld, d)
                qn_scr[u] = _da_rmsnorm(q_ref[sel, :], gq_ref[...])[2].astype(BF16)
                kn_scr[u] = _da_rmsnorm(k_ref[sel, :], gk_ref[...])[2].astype(BF16)
                v_scr[u] = v_ref[sel, :].astype(BF16)

            def block(u, i):
                s, _, _, _, win = _da_scores(qn_scr.at[u], kn_scr.at[u], slope, i, ld)
                m = jnp.max(s, axis=-1, keepdims=True)
                p = jnp.exp(s - m)
                l = jnp.sum(p, axis=-1, keepdims=True)
                return _dot(p, v_scr[u, win, :], _NN) / l, jnp.broadcast_to(m + jnp.log(l), (DA_QB, HEAD_DIM))

            def step(i, c2):
                todo = [(u, r, i + w * (nqb // ways)) for u, r in enumerate(rs) for w in range(ways)]
                for (u, r, b), (o, lse) in zip(todo, [block(u, b) for u, _, b in todo]):
                    out = _da_residue_rows(r, b * DA_QB, DA_QB, d)
                    o_ref[out, :] = o
                    lse_ref[out, :] = lse
                return c2

            return lax.fori_loop(0, nqb // ways, step, carry)

        lax.fori_loop(0, d // rways, residues, 0)

        if others:
            pieces = 8
            rows = SEQ // pieces

            def merge(t, carry):
                r = pl.ds(pl.multiple_of(t * rows, rows), rows)
                outs = [ref[r, :] for ref in other_refs[0::2]] + [o_ref[r, :]]
                lses = [ref[r, :] for ref in other_refs[1::2]] + [lse_ref[r, :]]
                m = lses[0]
                for l in lses[1:]:
                    m = jnp.maximum(m, l)
                es = [jnp.exp(l - m) for l in lses]
                tot = sum(es[1:], es[0])
                o = sum((e * v for e, v in zip(es[1:], outs[1:])), es[0] * outs[0]) / tot
                of_ref[r, :] = o
                ob_ref[r, :] = o.astype(BF16)
                lsej_ref[r, :] = m + jnp.log(tot)
                return carry

            lax.fori_loop(0, pieces, merge, 0)

    seq_spec = lambda base: pl.BlockSpec((SEQ, HEAD_DIM), lambda h: (0, base + 4 * group + h))
    out_spec = pl.BlockSpec((SEQ, HEAD_DIM), lambda h: (0, h))
    gain_spec = pl.BlockSpec((1, HEAD_DIM), lambda h: (0, 0))
    blk = SEQ * HEAD_DIM * 4
    fshape = jax.ShapeDtypeStruct((SEQ, DA_WIDTH), F32)
    scratch = [pltpu.VMEM((rways, ld, HEAD_DIM), BF16)] * 3
    if others:
        out_shape = [jax.ShapeDtypeStruct((SEQ, DA_WIDTH), BF16), fshape, fshape]
        scratch += [pltpu.VMEM((SEQ, HEAD_DIM), F32)] * 2
    else:
        out_shape = [fshape, fshape]
    return pl.pallas_call(
        body, name=f"da_fwd_g{group}", grid=(DA_HEADS_PER_GROUP,),
        in_specs=[seq_spec(CB_DA_Q), seq_spec(CB_DA_K), seq_spec(CB_DA_V), gain_spec, gain_spec,
                  pl.BlockSpec((None, 1, LANE), lambda h: (h, 0, 0))] + [out_spec] * n_other,
        out_specs=[out_spec] * len(out_shape), out_shape=out_shape, scratch_shapes=scratch,
        compiler_params=_params(("parallel",), 10 * blk + 2 * blk + (n_other + 3) * blk),
    )(proj, proj, proj, gq, gk, _da_slopes(group), *[a for pair in others for a in pair])


def _da_rowdot(do, o, tr):
    w = DA_WIDTH

    def body(do_ref, o_ref, out_ref):
        prod = do_ref[...] * o_ref[...]
        for h in range(w // HEAD_DIM):
            sl = slice(h * HEAD_DIM, (h + 1) * HEAD_DIM)
            out_ref[:, sl] = jnp.broadcast_to(jnp.sum(prod[:, sl], axis=-1, keepdims=True), (tr, HEAD_DIM))

    return pl.pallas_call(
        body, name="da_rowdot", grid=(SEQ // tr,), in_specs=[_row_spec(tr, w)] * 2,
        out_specs=_row_spec(tr, w), out_shape=jax.ShapeDtypeStruct((SEQ, w), F32),
        compiler_params=_params(("parallel",), 10 * tr * w * 4),
    )(do, o)


def _da_bwd(proj, gq, gk, do, lse, dd, dproj, group):
    d = DA_DILATIONS[group]
    ld = SEQ // d
    nqb = ld // DA_QB
    ways = min(DA_WAYS, nqb)
    rways = _da_residue_ways(d, nqb)

    def body(q_ref, k_ref, v_ref, gq_ref, gk_ref, sl_ref, do_ref, lse_ref, dd_ref, dproj_in,
             dproj_ref, dgq_ref, dgk_ref,
             qn_scr, kn_scr, v_scr, dqn_scr, dkn_scr, dvr_scr, dq_scr, dk_scr, dv_scr, dq_out, dk_out, dv_out, sems):
        del dproj_in
        head = pl.program_id(0)
        gqv, gkv = gq_ref[...], gk_ref[...]
        slope = sl_ref[:, 0:1]

        copies = []
        for k, (staging, base) in enumerate(((dq_out, CB_DA_Q), (dk_out, CB_DA_K), (dv_out, CB_DA_V))):
            cols = pl.ds(pl.multiple_of((base + 4 * group + head) * LANE, LANE), LANE)
            copies.append(pltpu.make_async_copy(staging, dproj_ref.at[:, cols], sems.at[k]))

        def residues(t, carry):
            rs = [t * rways + u for u in range(rways)]
            sels = [_da_residue_rows(r, 0, ld, d) for r in rs]
            for u, sel in enumerate(sels):
                qn_scr[u] = _da_rmsnorm(q_ref[sel, :], gqv)[2].astype(BF16)
                kn_scr[u] = _da_rmsnorm(k_ref[sel, :], gkv)[2].astype(BF16)
                v_scr[u] = v_ref[sel, :].astype(BF16)
            dkn_scr[...] = jnp.zeros_like(dkn_scr)
            dvr_scr[...] = jnp.zeros_like(dvr_scr)

            def block(u, r, i):
                s, qb, kw, qrows, win = _da_scores(qn_scr.at[u], kn_scr.at[u], slope, i, ld)
                src = _da_residue_rows(r, i * DA_QB, DA_QB, d)
                p = jnp.exp(s - lse_ref[src, :][:, 0:1])
                dob = do_ref[src, :]
                dp = _dot(dob, v_scr[u, win, :], _NT)
                ds = p * (dp - dd_ref[src, :][:, 0:1]) * DA_SCALE
                return u, qrows, win, _dot(p, dob, _TN), _dot(ds, kw, _NN), _dot(ds, qb, _TN)

            def step(i, c2):
                todo = [(u, r, i + w * (nqb // ways)) for u, r in enumerate(rs) for w in range(ways)]
                for u, qrows, win, dv, dqn, dkn in [block(*t) for t in todo]:
                    dvr_scr[u, win, :] += dv
                    dqn_scr[u, qrows, :] = dqn
                    dkn_scr[u, win, :] += dkn
                return c2

            lax.fori_loop(0, nqb // ways, step, 0)

            pq, pk = carry
            for u, sel in enumerate(sels):
                parts = []
                for x_ref, gv, dn_scr, dx_scr in ((q_ref, gqv, dqn_scr, dq_scr), (k_ref, gkv, dkn_scr, dk_scr)):
                    hat, rstd, _ = _da_rmsnorm(x_ref[sel, :], gv)
                    dn = dn_scr[u]
                    dyg = dn * gv
                    dx_scr[sel, :] = rstd * (dyg - hat * jnp.mean(dyg * hat, axis=-1, keepdims=True))
                    parts.append(jnp.sum(dn * hat, axis=0, keepdims=True))
                dv_scr[sel, :] = dvr_scr[u]
                pq, pk = pq + parts[0], pk + parts[1]
            return pq, pk

        zero = jnp.zeros((1, HEAD_DIM), F32)
        pq, pk = lax.fori_loop(0, d // rways, residues, (zero, zero))

        @pl.when(pl.program_id(0) == 0)
        def _():
            dgq_ref[...] = pq
            dgk_ref[...] = pk

        @pl.when(pl.program_id(0) > 0)
        def _():
            dgq_ref[...] += pq
            dgk_ref[...] += pk

        @pl.when(head > 0)
        def _():
            for cp in copies:
                cp.wait()

        dq_out[...] = dq_scr[...].astype(BF16)
        dk_out[...] = dk_scr[...].astype(BF16)
        dv_out[...] = dv_scr[...].astype(BF16)
        for cp in copies:
            cp.start()

        @pl.when(head == DA_HEADS_PER_GROUP - 1)
        def _():
            for cp in copies:
                cp.wait()

    seq_spec = lambda base: pl.BlockSpec((SEQ, HEAD_DIM), lambda h: (0, base + 4 * group + h))
    out_spec = pl.BlockSpec((SEQ, HEAD_DIM), lambda h: (0, h))
    gain_spec = pl.BlockSpec((1, HEAD_DIM), lambda h: (0, 0))
    gshape = jax.ShapeDtypeStruct((1, HEAD_DIM), F32)
    blk = SEQ * HEAD_DIM * 4
    return pl.pallas_call(
        body, name=f"da_bwd_g{group}", grid=(DA_HEADS_PER_GROUP,),
        in_specs=[seq_spec(CB_DA_Q), seq_spec(CB_DA_K), seq_spec(CB_DA_V), gain_spec, gain_spec,
                  pl.BlockSpec((None, 1, LANE), lambda h: (h, 0, 0)), out_spec, out_spec, out_spec, _any_spec()],
        out_specs=[_any_spec(), gain_spec, gain_spec],
        out_shape=[jax.ShapeDtypeStruct((SEQ, IN_COLS), BF16), gshape, gshape],
        scratch_shapes=[pltpu.VMEM((rways, ld, HEAD_DIM), BF16)] * 3 + [pltpu.VMEM((rways, ld, HEAD_DIM), F32)] * 3
        + [pltpu.VMEM((SEQ, HEAD_DIM), F32)] * 3 + [pltpu.VMEM((SEQ, HEAD_DIM), BF16)] * 3
        + [pltpu.SemaphoreType.DMA((3,))],
        input_output_aliases={9: 0},
        compiler_params=_params(("arbitrary",), 12 * blk + 3 * blk + 3 * blk + 3 * blk + 2 * blk),
    )(proj, proj, proj, gq, gk, _da_slopes(group), do, lse, dd, dproj)


MEM_WAYS = 4


def _mem_probs(q, kn, gq):
    qhat, rq, qn = _da_rmsnorm(q, gq)
    s = _dot(qn, kn, _NT) * MEM_SCALE
    m = jnp.max(s, axis=-1, keepdims=True)
    e = jnp.exp(s - m)
    p = e / jnp.sum(e, axis=-1, keepdims=True)
    return p, qhat, rq, qn


def _mem_pieces(tq):
    rows = tq // MEM_WAYS
    return [slice(w * rows, (w + 1) * rows) for w in range(MEM_WAYS)]


def _mem_fwd(proj, kv, gq, gk, tq):
    def body(q_ref, k_ref, v_ref, gq_ref, gk_ref, o_ref):
        kn = _da_rmsnorm(k_ref[...], gk_ref[...])[2]
        v, gqv = v_ref[...], gq_ref[...]
        pieces = _mem_pieces(tq)
        outs = [_dot(_mem_probs(q_ref[rs, :], kn, gqv)[0], v, _NN) for rs in pieces]
        for rs, o in zip(pieces, outs):
            o_ref[rs, :] = o.astype(BF16)

    gain_spec = pl.BlockSpec((1, HEAD_DIM), lambda h, i: (0, 0))
    return pl.pallas_call(
        body, name="mem_fwd", grid=(MEM_HEADS, SEQ // tq),
        in_specs=[pl.BlockSpec((tq, HEAD_DIM), lambda h, i: (i, CB_MEM_Q + h)),
                  pl.BlockSpec((N_MEM, HEAD_DIM), lambda h, i: (0, h)),
                  pl.BlockSpec((N_MEM, HEAD_DIM), lambda h, i: (0, MEM_HEADS + h)), gain_spec, gain_spec],
        out_specs=pl.BlockSpec((tq, HEAD_DIM), lambda h, i: (i, h)),
        out_shape=jax.ShapeDtypeStruct((SEQ, MEM_WIDTH), BF16),
        compiler_params=_params(("parallel", "parallel"), 16 * tq * N_MEM * 4),
    )(proj, kv, kv, gq, gk)


def _mem_bwd(proj, kv, gq, gk, do, dproj, tq):
    nq = SEQ // tq

    def body(q_ref, k_ref, v_ref, gq_ref, gk_ref, do_ref, dproj_in, dq_ref, dk_ref, dv_ref, dgq_ref, dgk_ref, dkn_scr):
        del dproj_in
        h, i = pl.program_id(0), pl.program_id(1)
        gqv, gkv = gq_ref[...], gk_ref[...]
        khat, rk, kn = _da_rmsnorm(k_ref[...], gkv)
        v = v_ref[...]

        def piece(rs):
            p, qhat, rq, qn = _mem_probs(q_ref[rs, :], kn, gqv)
            dob = do_ref[rs, :]
            dp = _dot(dob, v, _NT)
            ds = p * (dp - jnp.sum(p * dp, axis=-1, keepdims=True)) * MEM_SCALE
            dqn = _dot(ds, kn, _NN)
            dyg = dqn * gqv
            dq = (rq * (dyg - qhat * jnp.mean(dyg * qhat, axis=-1, keepdims=True))).astype(BF16)
            return dq, _dot(p, dob, _TN), _dot(ds, qn, _TN), jnp.sum(dqn * qhat, axis=0, keepdims=True)

        pieces = _mem_pieces(tq)
        results = [piece(rs) for rs in pieces]
        for rs, res in zip(pieces, results):
            dq_ref[rs, :] = res[0]
        dvp = sum((res[1] for res in results[1:]), results[0][1])
        dknp = sum((res[2] for res in results[1:]), results[0][2])
        dgq_part = sum((res[3] for res in results[1:]), results[0][3])
        first = jnp.logical_and(h == 0, i == 0)

        @pl.when(first)
        def _():
            dgq_ref[...] = dgq_part

        @pl.when(jnp.logical_not(first))
        def _():
            dgq_ref[...] += dgq_part

        @pl.when(i == 0)
        def _():
            dv_ref[...] = dvp
            dkn_scr[...] = dknp

        @pl.when(i > 0)
        def _():
            dv_ref[...] += dvp
            dkn_scr[...] += dknp

        @pl.when(i == nq - 1)
        def _():
            dkn = dkn_scr[...]
            dkg = dkn * gkv
            dk_ref[...] = rk * (dkg - khat * jnp.mean(dkg * khat, axis=-1, keepdims=True))
            dgk_part = jnp.sum(dkn * khat, axis=0, keepdims=True)

            @pl.when(h == 0)
            def _():
                dgk_ref[...] = dgk_part

            @pl.when(h > 0)
            def _():
                dgk_ref[...] += dgk_part

    gain_spec = pl.BlockSpec((1, HEAD_DIM), lambda h, i: (0, 0))
    kvout = pl.BlockSpec((N_MEM, HEAD_DIM), lambda h, i: (0, h))
    return pl.pallas_call(
        body, name="mem_bwd", grid=(MEM_HEADS, nq),
        in_specs=[pl.BlockSpec((tq, HEAD_DIM), lambda h, i: (i, CB_MEM_Q + h)),
                  pl.BlockSpec((N_MEM, HEAD_DIM), lambda h, i: (0, h)),
                  pl.BlockSpec((N_MEM, HEAD_DIM), lambda h, i: (0, MEM_HEADS + h)), gain_spec, gain_spec,
                  pl.BlockSpec((tq, HEAD_DIM), lambda h, i: (i, h)), _any_spec()],
        out_specs=[pl.BlockSpec((tq, HEAD_DIM), lambda h, i: (i, CB_MEM_Q + h)), kvout, kvout, gain_spec, gain_spec],
        out_shape=[jax.ShapeDtypeStruct((SEQ, IN_COLS), BF16), jax.ShapeDtypeStruct((N_MEM, MEM_WIDTH), F32),
                   jax.ShapeDtypeStruct((N_MEM, MEM_WIDTH), F32), jax.ShapeDtypeStruct((1, HEAD_DIM), F32),
                   jax.ShapeDtypeStruct((1, HEAD_DIM), F32)],
        scratch_shapes=[pltpu.VMEM((N_MEM, HEAD_DIM), F32)], input_output_aliases={6: 0},
        compiler_params=_params(("arbitrary", "arbitrary"), 24 * tq * N_MEM * 4),
    )(proj, kv, kv, gq, gk, do, dproj)


def _mesh_position():
    return lax.axis_index("x"), lax.axis_index("y"), lax.axis_index("c")


def _any_spec():
    return pl.BlockSpec(memory_space=pl.ANY)


def _all_gather(shards):
    n = len(shards)

    def body(*refs):
        ins, outs = refs[:n], refs[n:2 * n]
        send_sems, recv_sems, local_sems = refs[2 * n:]
        x, y, c = _mesh_position()
        me, sibling = (x, y, c), (x, y, 1 - c)
        first = (jnp.where(c == 0, 1 - x, x), jnp.where(c == 0, y, 1 - y), c)
        second = (jnp.where(c == 0, x, 1 - x), jnp.where(c == 0, 1 - y, y), c)
        diagonal = (1 - x, 1 - y, c)

        def copy(w, k, block, to, src=None):
            px, py, pc = block
            rows = outs[w].at[4 * px + 2 * py + pc]
            return pltpu.make_async_remote_copy(
                src_ref=rows if src is None else src, dst_ref=rows,
                send_sem=send_sems.at[7 * w + k], recv_sem=recv_sems.at[7 * w + k],
                device_id=to, device_id_type=MESH)

        started = []
        for w in range(n):
            mine = pltpu.make_async_copy(ins[w], outs[w].at[4 * x + 2 * y + c], local_sems.at[w])
            mine.start()
            started.append(mine)
        sends = []
        for w in range(n):
            own = [copy(w, 0, me, sibling, src=ins[w]), copy(w, 1, me, first, src=ins[w]),
                   copy(w, 2, me, second, src=ins[w])]
            for cp in own:
                cp.start()
            sends += own
        for w in range(n):
            copy(w, 1, first, me).wait_recv()
            follow = [copy(w, 3, first, second), copy(w, 4, first, sibling)]
            for cp in follow:
                cp.start()
            copy(w, 2, second, me).wait_recv()
            follow.append(copy(w, 5, second, sibling))
            follow[-1].start()
            sends += follow
        for w in range(n):
            copy(w, 3, diagonal, me).wait_recv()
            passed = copy(w, 6, diagonal, sibling)
            passed.start()
            sends.append(passed)
        for w in range(n):
            for k in (0, 4, 5, 6):
                copy(w, k, sibling, me).wait_recv()
        for cp in sends:
            cp.wait_send()
        for mine in started:
            mine.wait()

    return pl.pallas_call(
        body, name="weights_all_gather",
        in_specs=[_any_spec()] * n, out_specs=[_any_spec()] * n,
        out_shape=[jax.ShapeDtypeStruct((N_DEV,) + s.shape, s.dtype) for s in shards],
        scratch_shapes=[pltpu.SemaphoreType.DMA((7 * n,)), pltpu.SemaphoreType.DMA((7 * n,)),
                        pltpu.SemaphoreType.DMA((n,))],
    )(*shards)


def _chip_of(j, x, y):
    return (1 - x if j & 1 else x, 1 - y if j & 2 else y)


_HBM_SPEC = pl.BlockSpec(memory_space=pltpu.HBM)
_SEM_SPEC = pl.BlockSpec(memory_space=pltpu.SEMAPHORE)
_DATAFLOW_EFFECT = pltpu.SideEffectType.DATAFLOW_SIDE_EFFECTING
TOKEN_SHAPE = (8, D_MODEL)


def _copies_start(name, arrays, n_copies, plan):
    n = len(arrays)

    def body(*refs):
        send_sems, recv_sems, token = refs[n], refs[n + 1], refs[2 * n + 2]
        copies = plan(refs[:n])
        assert len(copies) == n_copies
        for k, (src, dst, dev) in enumerate(copies):
            pltpu.make_async_remote_copy(src_ref=src, dst_ref=dst, send_sem=send_sems.at[k], recv_sem=recv_sems.at[k],
                                         device_id=dev, device_id_type=MESH).start()
        token[...] = jnp.zeros_like(token)

    outs = pl.pallas_call(
        body, name=name,
        out_shape=(pltpu.SemaphoreType.DMA((n_copies,)), pltpu.SemaphoreType.DMA((n_copies,)),
                   *[pltpu.HBM(a.shape, a.dtype) for a in arrays], jax.ShapeDtypeStruct(TOKEN_SHAPE, F32)),
        in_specs=[_HBM_SPEC] * n,
        out_specs=(_SEM_SPEC, _SEM_SPEC, *[_HBM_SPEC] * n, pl.BlockSpec(memory_space=pltpu.VMEM)),
        input_output_aliases={i: i + 2 for i in range(n)},
        compiler_params=pltpu.CompilerParams(has_side_effects=_DATAFLOW_EFFECT),
    )(*[pltpu.with_memory_space_constraint(a, pltpu.HBM) for a in arrays])
    return outs[0], outs[1], list(outs[2:2 + n]), outs[2 + n]


def _copies_wait(name, send_sems, recv_sems, arrays, n_copies, plan, after):
    n = len(arrays)
    after = list(after) if isinstance(after, (list, tuple)) else [after]

    def body(*refs):
        send_ref, recv_ref = refs[n], refs[n + 1]
        copies = plan(refs[:n])
        assert len(copies) == n_copies
        for k, (src, dst, dev) in enumerate(copies):
            cp = pltpu.make_async_remote_copy(src_ref=src, dst_ref=dst, send_sem=send_ref.at[k], recv_sem=recv_ref.at[k],
                                              device_id=dev, device_id_type=MESH)
            cp.wait_send()
            cp.wait_recv()

    outs = pl.pallas_call(
        body, name=name, out_shape=tuple(pltpu.HBM(a.shape, a.dtype) for a in arrays),
        in_specs=[_HBM_SPEC] * n + [_SEM_SPEC, _SEM_SPEC] + [pl.BlockSpec(memory_space=pl.ANY)] * len(after),
        out_specs=tuple([_HBM_SPEC] * n), input_output_aliases={i: i for i in range(n)},
        compiler_params=pltpu.CompilerParams(has_side_effects=_DATAFLOW_EFFECT),
    )(*arrays, send_sems, recv_sems, *after)
    return list(outs)


def _after(small, token):
    return small if token is None else small + token[0:1, :small.shape[-1]]


def _gather_plan_out(n):
    def plan(refs):
        x, y, c = _mesh_position()
        me = 4 * x + 2 * y + c
        copies = []
        for w in range(n):
            land = refs[n + w].at[me]
            copies.append((refs[w], land, (x, y, 1 - c)))
            for j in range(1, 4):
                copies.append((refs[w], land, (*_chip_of(j, x, y), c)))
        return copies
    return plan


def _gather_plan_pass(n):
    def plan(refs):
        x, y, c = _mesh_position()
        copies = []
        for w in range(n):
            for j in range(1, 4):
                px, py = _chip_of(j, x, y)
                rows = refs[w].at[4 * px + 2 * py + c]
                copies.append((rows, rows, (x, y, 1 - c)))
        return copies
    return plan


def _reduce_plan_sibling(n):
    def plan(refs):
        x, y, c = _mesh_position()
        copies = []
        for w in range(n):
            for j in range(4):
                px, py = _chip_of(j, x, y)
                copies.append((refs[w].at[4 * px + 2 * py + (1 - c)], refs[n + w].at[j], (x, y, 1 - c)))
        return copies
    return plan


def _reduce_plan_chips(n):
    def plan(refs):
        x, y, c = _mesh_position()
        copies = []
        for w in range(n):
            for j in range(1, 4):
                copies.append((refs[w].at[j - 1], refs[n + w].at[j - 1], (*_chip_of(j, x, y), c)))
        return copies
    return plan


def _chip_partials(grad, recv, name, tr):
    _, rows, width = grad.shape

    def body(g_ref, r_ref, own_ref, other_ref):
        x, y, c = _mesh_position()
        for j in range(4):
            px, py = _chip_of(j, x, y)
            total = g_ref[4 * px + 2 * py + c].astype(F32) + r_ref[j].astype(F32)
            if j == 0:
                own_ref[...] = total
            else:
                other_ref[j - 1] = total.astype(BF16)

    return pl.pallas_call(
        body, name=name, grid=(rows // tr,),
        in_specs=[pl.BlockSpec((N_DEV, tr, width), lambda i: (0, i, 0)),
                  pl.BlockSpec((4, tr, width), lambda i: (0, i, 0))],
        out_specs=[pl.BlockSpec((tr, width), lambda i: (i, 0)), pl.BlockSpec((3, tr, width), lambda i: (0, i, 0))],
        out_shape=[jax.ShapeDtypeStruct((rows, width), F32), jax.ShapeDtypeStruct((3, rows, width), BF16)],
        compiler_params=_params(("parallel",), 2 * 20 * tr * width * 2 + 8 * tr * width * 4),
    )(grad, recv)


def _adamw_math(w, g, m, v):
    m = ADAM_B1 * m + (1.0 - ADAM_B1) * g
    v = ADAM_B2 * v + (1.0 - ADAM_B2) * (g * g)
    m_hat = m / (1.0 - ADAM_B1 ** ADAM_STEP)
    v_hat = v / (1.0 - ADAM_B2 ** ADAM_STEP)
    delta = -ADAM_LR * (m_hat / (jnp.sqrt(v_hat) + ADAM_EPS) + ADAM_WD * w)
    return delta, m, v


def _adamw_shard(own, recv, w, m, v, name, tr):
    rows, width = own.shape

    def body(own_ref, r_ref, w_ref, m_ref, v_ref, g_ref, d_ref, nm_ref, nv_ref):
        g = own_ref[...]
        for j in range(3):
            g = g + r_ref[j].astype(F32)
        g_ref[...] = g
        d_ref[...], nm_ref[...], nv_ref[...] = _adamw_math(w_ref[...], g, m_ref[...], v_ref[...])

    spec = pl.BlockSpec((tr, width), lambda i: (i, 0))
    shape = jax.ShapeDtypeStruct((rows, width), F32)
    return pl.pallas_call(
        body, name=name, grid=(rows // tr,),
        in_specs=[spec, pl.BlockSpec((3, tr, width), lambda i: (0, i, 0)), spec, spec, spec],
        out_specs=[spec] * 4, out_shape=[shape] * 4,
        compiler_params=_params(("parallel",), 22 * tr * width * 4),
    )(own, recv, w, m, v)


def _small_all_reduce_adamw(gpart, lbpack, wpack, mpack, vpack, after):
    def body(gp_ref, lb_ref, w_ref, m_ref, v_ref, after_ref, g_ref, d_ref, nm_ref, nv_ref, gath_ref, send_sems, recv_sems):
        del after_ref
        x, y, c = _mesh_position()
        me = 4 * x + 2 * y + c
        gath_ref[me] = gp_ref[...]
        copies = []
        for j in range(1, N_DEV):
            peer = (x ^ (j >> 2), y ^ ((j >> 1) & 1), c ^ (j & 1))
            cp = pltpu.make_async_remote_copy(
                src_ref=gp_ref, dst_ref=gath_ref.at[me], send_sem=send_sems.at[j - 1], recv_sem=recv_sems.at[j - 1],
                device_id=peer, device_id_type=MESH)
            cp.start()
            copies.append(cp)
        for cp in copies:
            cp.wait()
        tot = gath_ref[0]
        for s in range(1, N_DEV):
            tot = tot + gath_ref[s]
        lb = lb_ref[...]
        dl = tot[16:32, :] * lb * (1.0 - lb)
        g = jnp.concatenate([tot[0:16, :], dl[0:8, :], -dl[0:8, :], dl[8:16, :], -dl[8:16, :],
                             tot[32:SMALL_GRAD_ROWS, :]], axis=0)
        g_ref[...] = g
        d_ref[...], nm_ref[...], nv_ref[...] = _adamw_math(w_ref[...], g, m_ref[...], v_ref[...])

    vm = pl.BlockSpec(memory_space=pltpu.VMEM)
    shape = jax.ShapeDtypeStruct((SMALL_ROWS, LANE), F32)
    return pl.pallas_call(
        body, name="small_all_reduce_adamw", in_specs=[vm] * 5 + [_any_spec()], out_specs=[vm] * 4,
        out_shape=[shape] * 4,
        scratch_shapes=[pltpu.VMEM((N_DEV, SMALL_GRAD_ROWS, LANE), F32),
                        pltpu.SemaphoreType.DMA((N_DEV - 1,)), pltpu.SemaphoreType.DMA((N_DEV - 1,))],
    )(gpart, lbpack, wpack, mpack, vpack, after)


_SMALL_NAMES = ("norm_mix_gain", "norm_mem_gain", "lb_logits_fw", "lb_logits_bw", "norm_ffn_gain",
                "hg_norm_gain", "da_q_gain", "da_k_gain", "mem_q_gain", "mem_k_gain")
_SMALL_ROW0 = {"norm_mix_gain": 0, "norm_mem_gain": 8, "lb_logits_fw": 16, "lb_logits_bw": 32, "norm_ffn_gain": 48,
               "hg_norm_gain": 56, "da_q_gain": 64, "da_k_gain": 72, "mem_q_gain": 80, "mem_k_gain": 88}
_LOSS_ROW = 96


def _pack_rows(parts, total_rows):
    rows = []
    for p in parts:
        r = p.reshape(-1, LANE)
        rows.append(jnp.pad(r, ((0, -r.shape[0] % 8), (0, 0))))
    used = sum(r.shape[0] for r in rows)
    if total_rows > used:
        rows.append(jnp.zeros((total_rows - used, LANE), F32))
    return jnp.concatenate(rows, axis=0)


def _unpack_small(pack, like):
    out = {}
    for name in _SMALL_NAMES:
        n = like[name].size // LANE
        r0 = _SMALL_ROW0[name]
        out[name] = pack[r0:r0 + n].reshape(like[name].shape)
    return out


def kernel(x, mem, norm_mix_gain, norm_mem_gain, w_in, lb_logits_fw, lb_logits_bw, hg_norm_gain, da_q_gain, da_k_gain, w_mem_kv, mem_q_gain, mem_k_gain, w_proj_hg, w_proj_da, w_proj_mem, w_out, norm_ffn_gain, w_ffn_in, w_ffn_out, loss_target, m_norm_mix_gain, m_norm_mem_gain, m_w_in, m_lb_logits_fw, m_lb_logits_bw, m_hg_norm_gain, m_da_q_gain, m_da_k_gain, m_w_mem_kv, m_mem_q_gain, m_mem_k_gain, m_w_proj_hg, m_w_proj_da, m_w_proj_mem, m_w_out, m_norm_ffn_gain, m_w_ffn_in, m_w_ffn_out, v_norm_mix_gain, v_norm_mem_gain, v_w_in, v_lb_logits_fw, v_lb_logits_bw, v_hg_norm_gain, v_da_q_gain, v_da_k_gain, v_w_mem_kv, v_mem_q_gain, v_mem_k_gain, v_w_proj_hg, v_w_proj_da, v_w_proj_mem, v_w_out, v_norm_ffn_gain, v_w_ffn_in, v_w_ffn_out):
    small_w = dict(norm_mix_gain=norm_mix_gain, norm_mem_gain=norm_mem_gain, lb_logits_fw=lb_logits_fw,
                   lb_logits_bw=lb_logits_bw, norm_ffn_gain=norm_ffn_gain, hg_norm_gain=hg_norm_gain,
                   da_q_gain=da_q_gain, da_k_gain=da_k_gain, mem_q_gain=mem_q_gain, mem_k_gain=mem_k_gain)
    small_m = dict(norm_mix_gain=m_norm_mix_gain, norm_mem_gain=m_norm_mem_gain, lb_logits_fw=m_lb_logits_fw,
                   lb_logits_bw=m_lb_logits_bw, norm_ffn_gain=m_norm_ffn_gain, hg_norm_gain=m_hg_norm_gain,
                   da_q_gain=m_da_q_gain, da_k_gain=m_da_k_gain, mem_q_gain=m_mem_q_gain, mem_k_gain=m_mem_k_gain)
    small_v = dict(norm_mix_gain=v_norm_mix_gain, norm_mem_gain=v_norm_mem_gain, lb_logits_fw=v_lb_logits_fw,
                   lb_logits_bw=v_lb_logits_bw, norm_ffn_gain=v_norm_ffn_gain, hg_norm_gain=v_hg_norm_gain,
                   da_q_gain=v_da_q_gain, da_k_gain=v_da_k_gain, mem_q_gain=v_mem_q_gain, mem_k_gain=v_mem_k_gain)
    big_w = dict(w_in=w_in[0], w_mem_kv=w_mem_kv[0], w_proj_hg=w_proj_hg[0], w_proj_da=w_proj_da[0],
                 w_proj_mem=w_proj_mem[0], w_out=w_out[0], w_ffn_in=w_ffn_in[0], w_ffn_out=w_ffn_out[0])
    big_m = dict(w_in=m_w_in[0], w_mem_kv=m_w_mem_kv[0], w_proj_hg=m_w_proj_hg[0], w_proj_da=m_w_proj_da[0],
                 w_proj_mem=m_w_proj_mem[0], w_out=m_w_out[0], w_ffn_in=m_w_ffn_in[0], w_ffn_out=m_w_ffn_out[0])
    big_v = dict(w_in=v_w_in[0], w_mem_kv=v_w_mem_kv[0], w_proj_hg=v_w_proj_hg[0], w_proj_da=v_w_proj_da[0],
                 w_proj_mem=v_w_proj_mem[0], w_out=v_w_out[0], w_ffn_in=v_w_ffn_in[0], w_ffn_out=v_w_ffn_out[0])

    row_tile = dict(w_in=128, w_mem_kv=128, w_proj_hg=128, w_proj_da=512, w_proj_mem=512, w_out=128,
                    w_ffn_in=128, w_ffn_out=352)
    rest = _BIG_NAMES[1:]
    shards = [big_w[n].astype(BF16) for n in rest]
    state = {}
    big_out = {}

    def reduce_start(group, stacked):
        names = tuple(stacked)
        arrays = [stacked[n] for n in names] + [lax.empty((4,) + stacked[n].shape[1:], BF16) for n in names]
        plan = _reduce_plan_sibling(len(names))
        send, recv, thru, token = _copies_start(f"grads_{group}_sibling_start", arrays, 4 * len(names), plan)
        state[group] = dict(names=names, plan=plan, send=send, recv=recv, arrays=thru)
        return token

    def reduce_middle(group, after):
        st = state[group]
        names, k = st["names"], len(st["names"])
        thru = _copies_wait(f"grads_{group}_sibling_wait", st["send"], st["recv"], st["arrays"], 4 * k, st["plan"], after)
        partials = [_chip_partials(thru[i], thru[k + i], f"chip_partials_{n}", row_tile[n]) for i, n in enumerate(names)]
        arrays = [p[1] for p in partials] + [lax.empty(p[1].shape, BF16) for p in partials]
        plan = _reduce_plan_chips(k)
        send, recv, thru2, token = _copies_start(f"grads_{group}_chips_start", arrays, 3 * k, plan)
        state[group] = dict(names=names, plan=plan, send=send, recv=recv, arrays=thru2, own=[p[0] for p in partials])
        return token

    def reduce_finish(group, after):
        st = state.pop(group)
        names, k = st["names"], len(st["names"])
        thru = _copies_wait(f"grads_{group}_chips_wait", st["send"], st["recv"], st["arrays"], 3 * k, st["plan"], after)
        for i, n in enumerate(names):
            big_out[n] = _adamw_shard(st["own"][i], thru[k + i], big_w[n], big_m[n], big_v[n], "adamw_" + n, row_tile[n])

    step = _local_step_stages(x[0], mem[0], loss_target[0], small_w)
    event, payload = next(step)
    local = None
    while True:
        reply = None
        if event == "gather_w_in":
            reply = _all_gather([big_w["w_in"].astype(BF16)])[0]
        elif event == "begin":
            nr = len(rest)
            me = 4 * lax.axis_index("x") + 2 * lax.axis_index("y") + lax.axis_index("c")
            arrays = shards + [lax.dynamic_update_slice(lax.empty((N_DEV,) + s.shape, BF16), s[None], (me, 0, 0))
                               for s in shards]
            plan = _gather_plan_out(nr)
            send, recv, thru, reply = _copies_start("weights_rest_out_start", arrays, 4 * nr, plan)
            state["gather"] = dict(plan=plan, send=send, recv=recv, arrays=thru)
        elif event == "after_gla_fwd":
            st = state["gather"]
            nr = len(rest)
            thru = _copies_wait("weights_rest_out_wait", st["send"], st["recv"], st["arrays"], 4 * nr, st["plan"], payload)
            plan = _gather_plan_pass(nr)
            send, recv, lands, reply = _copies_start("weights_rest_pass_start", thru[nr:], 3 * nr, plan)
            state["gather"] = dict(plan=plan, send=send, recv=recv, arrays=lands)
        elif event == "need_weights":
            st = state.pop("gather")
            nr = len(rest)
            lands = _copies_wait("weights_rest_pass_wait", st["send"], st["recv"], st["arrays"], 3 * nr, st["plan"], payload)
            reply = dict(zip(rest, lands))
        elif event == "grads_ffn":
            reply = reduce_start("ffn", payload)
        elif event == "after_gate_merge_bwd":
            reply = reduce_middle("ffn", payload)
        elif event == "grads_mix":
            reply = reduce_start("mix", payload)
        elif event == "after_da_bwd_g0":
            reply = reduce_middle("mix", payload)
            reduce_finish("ffn", payload)
        elif event == "after_gla_bwd":
            reduce_finish("mix", payload)
        elif event == "grads_in":
            reply = reduce_start("in", payload)
        elif event == "after_proj_bwd_act_top":
            reply = reduce_middle("in", payload)
        elif event == "end":
            local = payload
            reduce_finish("in", [local["grad_x"]] + [big_out[n][0] for n in rest])
            break
        event, payload = step.send(reply)
    grad_x = local["grad_x"]

    wpack = _pack_rows([small_w[n] for n in _SMALL_NAMES], SMALL_ROWS)
    mpack = _pack_rows([small_m[n] for n in _SMALL_NAMES], SMALL_ROWS)
    vpack = _pack_rows([small_v[n] for n in _SMALL_NAMES], SMALL_ROWS)
    gs, ds, ms, vs = _small_all_reduce_adamw(local["gpart"], local["lbpack"], wpack, mpack, vpack, big_out["w_in"][0])
    loss = gs[_LOSS_ROW, 0]
    small_out = [_unpack_small(t, small_w) for t in (gs, ds, ms, vs)]

    order = ("norm_mix_gain", "norm_mem_gain", "w_in", "lb_logits_fw", "lb_logits_bw", "hg_norm_gain", "da_q_gain",
             "da_k_gain", "w_mem_kv", "mem_q_gain", "mem_k_gain", "w_proj_hg", "w_proj_da", "w_proj_mem", "w_out",
             "norm_ffn_gain", "w_ffn_in", "w_ffn_out")
    outs = [loss, grad_x[None]]
    for kind in range(4):
        for n in order:
            outs.append(big_out[n][kind][None] if n in big_out else small_out[kind][n])
    return tuple(outs)


_BIG_NAMES = ("w_in", "w_mem_kv", "w_proj_hg", "w_proj_da", "w_proj_mem", "w_out", "w_ffn_in", "w_ffn_out")


def _local_step(xs, mems, target, sw, wg):
    step = _local_step_stages(xs, mems, target, sw)
    stacked = {}
    event, payload = next(step)
    while event != "end":
        reply = None
        if event.startswith("grads_"):
            stacked.update(payload)
        elif event == "need_weights":
            reply = wg
        elif event == "gather_w_in":
            reply = wg["w_in"]
        event, payload = step.send(reply)
    return dict(payload, stacked=stacked)


def _local_step_stages(xs, mems, target, sw):
    norm_mix_gain, norm_mem_gain, norm_ffn_gain = sw["norm_mix_gain"], sw["norm_mem_gain"], sw["norm_ffn_gain"]
    lb_logits_fw, lb_logits_bw, hg_norm_gain = sw["lb_logits_fw"], sw["lb_logits_bw"], sw["hg_norm_gain"]
    da_q_gain, da_k_gain, mem_q_gain, mem_k_gain = sw["da_q_gain"], sw["da_k_gain"], sw["mem_q_gain"], sw["mem_k_gain"]

    win_st = yield "gather_w_in", None
    token = yield "begin", None
    lb_fw = _lb_table(lb_logits_fw, "lb_table_fw")
    lb_bw = _lb_table(lb_logits_bw, "lb_table_bw")
    lb = jnp.concatenate([lb_fw, lb_bw], axis=0).reshape(2, HG_HEADS, 1, HEAD_DIM)
    h, h_t = _rmsnorm_fwd(xs, _after(norm_mix_gain, token), "norm_mix_fwd", 512, transposed=True)
    proj = _matmul(h, win_st, "nn", F32, 1024, IN_SHARD, D_MODEL, "proj_fwd", b_stacked=True, n_outer=True)
    o_hg, o_pre, states, hg_qt, hg_decay = _gla_fwd(proj, lb, hg_norm_gain)
    token = yield "after_gla_fwd", o_hg
    da01 = [_da_fwd(proj, _after(da_q_gain, token), da_k_gain, g) for g in (0, 1)]
    o_da, o_da32, lse_da = _da_fwd(proj, _after(da_q_gain, token), da_k_gain, 2, others=da01)
    wg = yield "need_weights", o_da
    wkv = wg["w_mem_kv"].reshape(D_MODEL, 2 * MEM_WIDTH)
    wphg = wg["w_proj_hg"].reshape(D_MODEL, D_MODEL)
    wpda = jnp.transpose(wg["w_proj_da"], (1, 0, 2)).reshape(DA_WIDTH, D_MODEL)
    wpmem = jnp.transpose(wg["w_proj_mem"], (1, 0, 2)).reshape(MEM_WIDTH, D_MODEL)
    wout = wg["w_out"].reshape(D_MODEL, D_MODEL)
    wfin = jnp.transpose(wg["w_ffn_in"], (1, 0, 2)).reshape(D_MODEL, 2 * D_FF)
    wfout = wg["w_ffn_out"].reshape(D_FF, D_MODEL)
    mem_n = _rmsnorm_fwd(mems, norm_mem_gain, "norm_mem_fwd", N_MEM)
    kv = _matmul(mem_n, wkv, "nn", F32, N_MEM, 1024, D_MODEL, "mem_kv_fwd")
    o_mem = _mem_fwd(proj, kv, mem_q_gain, mem_k_gain, 1024)
    branch_w = (wphg, wpda, wpmem)
    merged, t_hg, t_da, t_mem = _branch_merge_fwd(proj, (o_hg, o_da, o_mem), branch_w, 512)
    x1, h2, h2_t = _residual_rmsnorm_fwd(xs, merged, wout, norm_ffn_gain, "out_norm_ffn_fwd", 512)
    ffn_a, ffn_b, act, act_t = _ffn_in_swiglu(h2, wfin, 1024, 1408)
    dy, dyb, loss_part = _ffn_out_loss_head(x1, act, wfout, target, 512)

    dab = _ffn_out_bwd_swiglu(dyb, wfout, ffn_a, ffn_b, 1024, 1408)
    g_wfout = _matmul(act_t, dyb, "nn", BF16, 1408, 1024, 2048, "ffn_out_bwd_w")
    g_wfin = _matmul(h2_t, dab, "nn", BF16, 1024, 1408, 2048, "ffn_in_bwd_w", b_parts=True)
    token = yield "grads_ffn", dict(
        w_ffn_in=jnp.transpose(g_wfin.reshape(D_MODEL, N_DEV, 2 * D_FF // N_DEV), (1, 0, 2)),
        w_ffn_out=g_wfout.reshape(N_DEV, D_FF // N_DEV, D_MODEL))
    dx1, dx1b, g_norm_ffn = _matmul_rmsnorm_bwd(dab, wfin, x1, dy, _after(norm_ffn_gain, token),
                                                "ffn_in_bwd_act_norm", 512, D_FF, a_parts=True)
    dmerged = _matmul(dx1b, wout, "nt", F32, 512, 1024, D_MODEL, "out_bwd_act")
    g_wout = _matmul(merged, dx1b, "tn", BF16, 1024, 1024, 2048, "out_bwd_w")
    dt_hg, dt_da, dt_mem, do_hg, do_da, do_mem, dproj = _gate_merge_bwd(
        proj, (t_hg, t_da, t_mem), branch_w, dmerged, _dproj_buffer(), 256)
    token = yield "after_gate_merge_bwd", dt_hg
    g_wphg = _matmul(o_hg, dt_hg, "tn", BF16, 1024, 1024, 2048, "proj_hg_bwd_w", after=token)
    g_wpda = _matmul(o_da, dt_da, "tn", BF16, DA_WIDTH, D_MODEL, 2048, "proj_da_bwd_w")
    g_wpmem = _matmul(o_mem, dt_mem, "tn", BF16, MEM_WIDTH, D_MODEL, 2048, "proj_mem_bwd_w")
    by_owner = lambda g: jnp.transpose(g.reshape(g.shape[0], N_DEV, D_MODEL // N_DEV), (1, 0, 2))
    g_wpda, g_wpmem = by_owner(g_wpda), by_owner(g_wpmem)

    dproj, dk_mem, dv_mem, g_mem_q, g_mem_k = _mem_bwd(proj, kv, mem_q_gain, mem_k_gain, do_mem, dproj, 1024)
    dkv = jnp.concatenate([dk_mem, dv_mem], axis=1).astype(BF16)
    g_wkv = _matmul(mem_n, dkv, "tn", BF16, 1024, 1024, N_MEM, "mem_kv_bwd_w")
    dmem_n = _matmul(dkv, wkv, "nt", F32, N_MEM, 1024, 1024, "mem_kv_bwd_act")
    g_norm_mem = _gain_grad(mems, dmem_n, "norm_mem_bwd")
    token = yield "grads_mix", dict(
        w_mem_kv=g_wkv.reshape(N_DEV, D_MODEL // N_DEV, 2 * MEM_WIDTH),
        w_proj_hg=g_wphg.reshape(N_DEV, D_MODEL // N_DEV, D_MODEL),
        w_proj_da=g_wpda, w_proj_mem=g_wpmem,
        w_out=g_wout.reshape(N_DEV, D_MODEL // N_DEV, D_MODEL))

    dd_da = _da_rowdot(do_da, o_da32, 512)
    dproj, g_da_q, g_da_k = _da_bwd(proj, _after(da_q_gain, token), da_k_gain, do_da, lse_da, dd_da, dproj, 0)
    token = yield "after_da_bwd_g0", g_da_q
    for g in (1, 2):
        dproj, gq_part, gk_part = _da_bwd(proj, _after(da_q_gain, token), da_k_gain, do_da, lse_da, dd_da, dproj, g)
        g_da_q, g_da_k = g_da_q + gq_part, g_da_k + gk_part

    dproj, dlb, g_hg_norm = _gla_bwd(proj, lb, hg_norm_gain, o_pre, states, hg_qt, hg_decay, do_hg, dproj)
    yield "after_gla_bwd", dlb

    g_win =_matmul(h_t, dproj, "nn", BF16, 1024, IN_SHARD, 2048, "proj_bwd_w", out_stacked=True)
    token = yield "grads_in", dict(w_in=g_win)
    grad_x, g_mix_top = _matmul_rmsnorm_bwd(dproj, win_st, xs, dx1, norm_mix_gain, "proj_bwd_act_norm_top", 1024,
                                            IN_SHARD, b_stacked=True, after=token, m_blocks=(0, 1), with_bf16=False)
    token = yield "after_proj_bwd_act_top", g_mix_top
    grad_x, g_mix_bottom = _matmul_rmsnorm_bwd(dproj, win_st, xs, dx1, norm_mix_gain, "proj_bwd_act_norm_bottom", 1024,
                                               IN_SHARD, b_stacked=True, after=token, m_blocks=(1, 3),
                                               out_into=grad_x, with_bf16=False)
    g_norm_mix = g_mix_top + g_mix_bottom

    gpart = _pack_rows([g_norm_mix, g_norm_mem, dlb[0], dlb[1], g_norm_ffn, g_hg_norm, g_da_q, g_da_k,
                        g_mem_q, g_mem_k, loss_part], SMALL_GRAD_ROWS)
    lbpack = jnp.concatenate([lb_fw.reshape(8, LANE), lb_bw.reshape(8, LANE)], axis=0)
    yield "end", dict(grad_x=grad_x, gpart=gpart, lbpack=lbpack)
```

```python
import numpy as np
import jax
import jax.numpy as jnp
from jax import lax
from jax.experimental import pallas as pl
from jax.experimental.pallas import tpu as pltpu

F32 = jnp.float32
BF16 = jnp.bfloat16
MESH = pl.DeviceIdType.MESH

SEQ = 4096
D_MODEL = 1024
N_DEV = 8
N_MEM = 256
RMS_EPS = 1e-6
NEG_INF = -1e30
LANE = 128
HEAD_DIM = 128
HG_HEADS = 8
HG_CHUNK = 64
HG_SCALE = HEAD_DIM ** -0.5
DA_DILATIONS = (1, 4, 16)
DA_RADIUS = 64
DA_HEADS_PER_GROUP = 4
DA_HEADS = 12
DA_WIDTH = 512
DA_SCALE = HEAD_DIM ** -0.5
DA_QB = 128
DA_WIN = 256
DA_WAYS = 4
DA_FWD_WAYS = 4
MEM_HEADS = 4
MEM_WIDTH = 512
MEM_SCALE = HEAD_DIM ** -0.5
D_FF = 2816
IN_COLS = 13312
IN_SHARD = IN_COLS // N_DEV
CB_HG_Q, CB_F, CB_HG_I, CB_HG_G = 0, 8, 24, 32
CB_DA_Q, CB_DA_K, CB_DA_V, CB_MEM_Q = 40, 52, 64, 76
ADAM_LR, ADAM_B1, ADAM_B2, ADAM_EPS, ADAM_WD, ADAM_STEP = 0.001, 0.9, 0.999, 1e-08, 0.01, 10
VMEM_BYTES_V7X = 64 * 1024 * 1024
SMALL_ROWS = 104
SMALL_GRAD_ROWS = 88

_NN = (((1,), (0,)), ((), ()))
_NT = (((1,), (1,)), ((), ()))
_TN = (((0,), (0,)), ((), ()))


def _dot(a, b, dims):
    return lax.dot_general(a.astype(BF16), b.astype(BF16), dims, preferred_element_type=F32)


def _sigmoid(x):
    return 0.5 * jnp.tanh(0.5 * x) + 0.5


def _params(semantics, est_bytes):
    limit = int(min(VMEM_BYTES_V7X - (6 << 20), max(56 << 20, est_bytes * 3 // 2)))
    return pltpu.CompilerParams(dimension_semantics=semantics, vmem_limit_bytes=limit)


def _nbytes(shape, dtype):
    return int(np.prod(shape)) * jnp.dtype(dtype).itemsize


def _alibi_slopes(n):
    return (2.0 ** (-8.0 * np.arange(1, n + 1) / n)).astype(np.float32)


def _matmul(a, b, mode, out_dtype, tm, tn, tk, name, b_stacked=False, out_stacked=False, n_outer=False, after=None,
            m_blocks=None, out_into=None, a_parts=False, b_parts=False):
    if a_parts:
        assert mode == "nt" and tk == a.shape[2]
        m, kdim = a.shape[1], a.shape[0] * a.shape[2]
    elif mode == "tn":
        kdim, m = a.shape
    else:
        m, kdim = a.shape
    if b_parts:
        assert mode == "nn" and not b_stacked and b.shape[2] % tn == 0
        n = b.shape[0] * b.shape[2]
    elif b_stacked:
        if mode == "nn":
            n = b.shape[0] * b.shape[2]
            assert tn == b.shape[2] and tk == kdim == b.shape[1]
        else:
            assert mode == "nt" and tk == b.shape[2] and b.shape[0] * tk == kdim
            n = b.shape[1]
    else:
        n = b.shape[0] if mode == "nt" else b.shape[1]
    assert m % tm == 0 and n % tn == 0 and kdim % tk == 0
    gm, gn, gk = m // tm, n // tn, kdim // tk
    i0 = 0
    if m_blocks is not None:
        assert mode != "tn" and not out_stacked
        i0, gm = m_blocks

    def ijk(f):
        if n_outer:
            return lambda j, i, k: f(i + i0, j, k)
        return lambda i, j, k: f(i + i0, j, k)

    if a_parts:
        a_spec = pl.BlockSpec((None, tm, tk), ijk(lambda i, j, k: (k, i, 0)))
    elif mode == "tn":
        a_spec = pl.BlockSpec((tk, tm), ijk(lambda i, j, k: (k, i)))
    else:
        a_spec = pl.BlockSpec((tm, tk), ijk(lambda i, j, k: (i, k)))
    if b_parts:
        per_part = b.shape[2] // tn
        b_spec = pl.BlockSpec((None, tk, tn), ijk(lambda i, j, k: (j // per_part, k, j % per_part)))
    elif b_stacked and mode == "nn":
        b_spec = pl.BlockSpec((None, tk, tn), ijk(lambda i, j, k: (j, 0, 0)))
    elif b_stacked:
        b_spec = pl.BlockSpec((None, tn, tk), ijk(lambda i, j, k: (k, j, 0)))
    elif mode == "nt":
        b_spec = pl.BlockSpec((tn, tk), ijk(lambda i, j, k: (j, k)))
    else:
        b_spec = pl.BlockSpec((tk, tn), ijk(lambda i, j, k: (k, j)))
    if out_stacked:
        assert tm == m
        out_shape = jax.ShapeDtypeStruct((gn, m, tn), out_dtype)
        o_spec = pl.BlockSpec((None, tm, tn), ijk(lambda i, j, k: (j, i, 0)))
    else:
        out_shape = jax.ShapeDtypeStruct((m, n), out_dtype)
        o_spec = pl.BlockSpec((tm, tn), ijk(lambda i, j, k: (i, j)))
    dims = {"nn": _NN, "nt": _NT, "tn": _TN}[mode]

    n_in = 2 + (after is not None) + (out_into is not None)

    def body(*refs):
        a_ref, b_ref, o_ref = refs[0], refs[1], refs[n_in]
        part = _dot(a_ref[...], b_ref[...], dims)
        if gk == 1:
            o_ref[...] = part.astype(out_dtype)
            return
        acc_ref = refs[-1]
        k = pl.program_id(2)

        @pl.when(k == 0)
        def _():
            acc_ref[...] = part

        @pl.when(jnp.logical_and(k > 0, k < gk - 1))
        def _():
            acc_ref[...] += part

        @pl.when(k == gk - 1)
        def _():
            o_ref[...] = (acc_ref[...] + part).astype(out_dtype)

    a_tile = _nbytes((tm, tk), a.dtype)
    b_tile = _nbytes((tk, tn), b.dtype)
    o_tile = _nbytes((tm, tn), out_dtype)
    est = 2 * (a_tile + b_tile + o_tile) + 3 * tm * tn * 4 + (a_tile + b_tile)
    grid = (gn, gm, gk) if n_outer else (gm, gn, gk)
    operands, in_specs = [a, b], [a_spec, b_spec]
    if after is not None:
        operands.append(after)
        in_specs.append(pl.BlockSpec(memory_space=pl.ANY))
    aliases = {}
    if out_into is not None:
        aliases = {len(operands): 0}
        operands.append(out_into)
        in_specs.append(pl.BlockSpec(memory_space=pl.ANY))
    return pl.pallas_call(
        body, name=name, grid=grid, in_specs=in_specs, out_specs=o_spec, out_shape=out_shape,
        scratch_shapes=[] if gk == 1 else [pltpu.VMEM((tm, tn), F32)], input_output_aliases=aliases,
        compiler_params=_params(("parallel", "parallel", "arbitrary"), est),
    )(*operands)


def _row_spec(tr, width, col_block=0):
    return pl.BlockSpec((tr, width), lambda i: (i, col_block))


def _bcast_spec(width):
    return pl.BlockSpec((1, width), lambda i: (0, 0))


def _col_spec(width, tr):
    return pl.BlockSpec((width, tr), lambda i: (0, i))


def _rmsnorm_fwd(x, gain, name, tr, transposed=False):
    rows, width = x.shape

    def body(x_ref, g_ref, o_ref, *t_ref):
        xv = x_ref[...]
        r = lax.rsqrt(jnp.mean(xv * xv, axis=-1, keepdims=True) + RMS_EPS)
        h = xv * r * g_ref[...]
        o_ref[...] = h.astype(BF16)
        if transposed:
            t_ref[0][...] = h.T.astype(BF16)

    out_specs, out_shape = [_row_spec(tr, width)], [jax.ShapeDtypeStruct((rows, width), BF16)]
    if transposed:
        out_specs.append(_col_spec(width, tr))
        out_shape.append(jax.ShapeDtypeStruct((width, rows), BF16))
    out = pl.pallas_call(
        body, name=name, grid=(rows // tr,), in_specs=[_row_spec(tr, width), _bcast_spec(width)],
        out_specs=out_specs, out_shape=out_shape,
        compiler_params=_params(("parallel",), 10 * tr * width * 4),
    )(x, gain)
    return out if transposed else out[0]


def _residual_rmsnorm_fwd(x, merged, wout, gain, name, tr):
    rows, width = x.shape

    def body(x_ref, m_ref, w_ref, g_ref, x1_ref, h_ref, ht_ref):
        xv = x_ref[...] + _dot(m_ref[...], w_ref[...], _NN)
        x1_ref[...] = xv
        r = lax.rsqrt(jnp.mean(xv * xv, axis=-1, keepdims=True) + RMS_EPS)
        h = xv * r * g_ref[...]
        h_ref[...] = h.astype(BF16)
        ht_ref[...] = h.T.astype(BF16)

    return pl.pallas_call(
        body, name=name, grid=(rows // tr,),
        in_specs=[_row_spec(tr, width), _row_spec(tr, merged.shape[1]),
                  pl.BlockSpec(wout.shape, lambda i: (0, 0)), _bcast_spec(width)],
        out_specs=[_row_spec(tr, width), _row_spec(tr, width), _col_spec(width, tr)],
        out_shape=[jax.ShapeDtypeStruct((rows, width), F32), jax.ShapeDtypeStruct((rows, width), BF16),
                   jax.ShapeDtypeStruct((width, rows), BF16)],
        compiler_params=_params(("parallel",), 14 * tr * width * 4),
    )(x, merged, wout, gain)


def _matmul_rmsnorm_bwd(a, b, x, dres, gain, name, tm, tk, a_parts=False, b_stacked=False, after=None,
                        m_blocks=None, out_into=None, with_bf16=True):
    width = x.shape[1]
    m = a.shape[1] if a_parts else a.shape[0]
    kdim = a.shape[0] * a.shape[2] if a_parts else a.shape[1]
    gk = kdim // tk
    i0, gm = (0, m // tm) if m_blocks is None else m_blocks
    n_in = 5 + (after is not None) + (out_into is not None)

    def body(*refs):
        a_ref, b_ref, x_ref, dres_ref, g_ref = refs[:5]
        outs = refs[n_in:]
        dx_ref, dg_ref, acc_ref = outs[0], outs[-2], outs[-1]
        i, k = pl.program_id(0), pl.program_id(1)
        part = _dot(a_ref[...], b_ref[...], _NT)

        @pl.when(k == 0)
        def _():
            acc_ref[...] = part

        @pl.when(jnp.logical_and(k > 0, k < gk - 1))
        def _():
            acc_ref[...] += part

        @pl.when(k == gk - 1)
        def _():
            dhv = acc_ref[...] + part if gk > 1 else part
            xv = x_ref[...]
            r = lax.rsqrt(jnp.mean(xv * xv, axis=-1, keepdims=True) + RMS_EPS)
            xhat = xv * r
            dyg = dhv * g_ref[...]
            dx = dres_ref[...] + r * (dyg - xhat * jnp.mean(dyg * xhat, axis=-1, keepdims=True))
            dx_ref[...] = dx
            if with_bf16:
                outs[1][...] = dx.astype(BF16)
            gpart = jnp.sum(dhv * xhat, axis=0, keepdims=True)

            @pl.when(i == 0)
            def _():
                dg_ref[...] = gpart

            @pl.when(i > 0)
            def _():
                dg_ref[...] += gpart

    rows = lambda width_: pl.BlockSpec((tm, width_), lambda i, k: (i + i0, 0))
    if a_parts:
        a_spec = pl.BlockSpec((None, tm, tk), lambda i, k: (k, i + i0, 0))
    else:
        a_spec = pl.BlockSpec((tm, tk), lambda i, k: (i + i0, k))
    if b_stacked:
        b_spec = pl.BlockSpec((None, width, tk), lambda i, k: (k, 0, 0))
    else:
        b_spec = pl.BlockSpec((width, tk), lambda i, k: (0, k))
    operands = [a, b, x, dres, gain]
    in_specs = [a_spec, b_spec, rows(width), rows(width), pl.BlockSpec((1, width), lambda i, k: (0, 0))]
    for extra in (after, out_into):
        if extra is not None:
            operands.append(extra)
            in_specs.append(pl.BlockSpec(memory_space=pl.ANY))
    aliases = {} if out_into is None else {len(operands) - 1: 0}
    out_specs = [rows(width)] + ([rows(width)] if with_bf16 else []) + [pl.BlockSpec((1, width), lambda i, k: (0, 0))]
    out_shape = [jax.ShapeDtypeStruct((m, width), F32)] + ([jax.ShapeDtypeStruct((m, width), BF16)] if with_bf16 else [])
    out_shape.append(jax.ShapeDtypeStruct((1, width), F32))
    est = 4 * tm * tk + 4 * width * tk + 12 * tm * width * 4
    return pl.pallas_call(
        body, name=name, grid=(gm, gk), in_specs=in_specs, out_specs=out_specs, out_shape=out_shape,
        scratch_shapes=[pltpu.VMEM((tm, width), F32)], input_output_aliases=aliases,
        compiler_params=_params(("arbitrary", "arbitrary"), est),
    )(*operands)


def _gain_grad(x, dh, name):
    rows, width = x.shape

    def body(x_ref, dh_ref, dg_ref):
        xv = x_ref[...]
        r = lax.rsqrt(jnp.mean(xv * xv, axis=-1, keepdims=True) + RMS_EPS)
        dg_ref[...] = jnp.sum(dh_ref[...] * xv * r, axis=0, keepdims=True)

    return pl.pallas_call(
        body, name=name, grid=(1,), in_specs=[_row_spec(rows, width), _row_spec(rows, width)],
        out_specs=_bcast_spec(width), out_shape=jax.ShapeDtypeStruct((1, width), F32),
        compiler_params=_params(("arbitrary",), 6 * rows * width * 4),
    )(x, dh)


def _lb_table(logits, name):
    slots, width = logits.shape

    def body(l_ref, o_ref):
        lv = l_ref[...]
        mx = jnp.max(lv, axis=0, keepdims=True)
        e = jnp.exp(lv - mx)
        o_ref[...] = e[0:1, :] / jnp.sum(e, axis=0, keepdims=True)

    return pl.pallas_call(
        body, name=name, grid=(1,), in_specs=[pl.BlockSpec((slots, width), lambda i: (0, 0))],
        out_specs=_bcast_spec(width), out_shape=jax.ShapeDtypeStruct((1, width), F32),
    )(logits)


def _branch_merge_fwd(proj, outs, weights, tr):
    w = D_MODEL

    def body(ghg_ref, gda_ref, gmem_ref, ohg_ref, oda_ref, omem_ref, whg_ref, wda_ref, wmem_ref,
             m_ref, thg_ref, tda_ref, tmem_ref):
        acc = None
        for g_ref, o_ref, w_ref, t_ref in ((ghg_ref, ohg_ref, whg_ref, thg_ref), (gda_ref, oda_ref, wda_ref, tda_ref),
                                           (gmem_ref, omem_ref, wmem_ref, tmem_ref)):
            t = _dot(o_ref[...], w_ref[...], _NN)
            t_ref[...] = t.astype(BF16)
            term = _sigmoid(g_ref[...]) * t
            acc = term if acc is None else acc + term
        m_ref[...] = acc.astype(BF16)

    whole = lambda a: pl.BlockSpec(a.shape, lambda i: (0, 0))
    shape = jax.ShapeDtypeStruct((SEQ, w), BF16)
    return pl.pallas_call(
        body, name="branch_merge_fwd", grid=(SEQ // tr,),
        in_specs=[_row_spec(tr, w, 10), _row_spec(tr, w, 11), _row_spec(tr, w, 12)]
        + [_row_spec(tr, o.shape[1]) for o in outs] + [whole(wt) for wt in weights],
        out_specs=[_row_spec(tr, w)] * 4, out_shape=[shape] * 4,
        compiler_params=_params(("parallel",), 20 * tr * w * 4),
    )(proj, proj, proj, *outs, *weights)


def _dproj_buffer():
    return lax.empty((SEQ, IN_COLS), BF16)


def _gate_merge_bwd(proj, ts, weights, dx1b, wout, dproj, tr):
    w = D_MODEL
    steps = SEQ // tr
    gate_col0 = 10 * w

    def body(ghg_ref, gda_ref, gmem_ref, thg_ref, tda_ref, tmem_ref, whg_ref, wda_ref, wmem_ref, dx_ref, wout_ref,
             dproj_in, dthg_ref, dtda_ref, dtmem_ref, dohg_ref, doda_ref, domem_ref, dproj_ref, stage, sems):
        del dproj_in
        i = pl.program_id(0)
        slot = i % 2

        def slot_copy(s):
            rows = pl.ds(pl.multiple_of(i * tr, tr), tr)
            return pltpu.make_async_copy(stage.at[s], dproj_ref.at[rows, pl.ds(gate_col0, 3 * w)], sems.at[s])

        @pl.when(i >= 2)
        def _():
            slot_copy(slot).wait()

        dm = _dot(dx_ref[...], wout_ref[...], _NT)
        branches = ((ghg_ref, thg_ref, whg_ref, dthg_ref, dohg_ref), (gda_ref, tda_ref, wda_ref, dtda_ref, doda_ref),
                    (gmem_ref, tmem_ref, wmem_ref, dtmem_ref, domem_ref))
        for b, (g_ref, t_ref, w_ref, dt_ref, do_ref) in enumerate(branches):
            s = _sigmoid(g_ref[...])
            dt = (s * dm).astype(BF16)
            dt_ref[...] = dt
            do_ref[...] = _dot(dt, w_ref[...], _NT)
            stage[slot, :, b * w:(b + 1) * w] = (dm * t_ref[...].astype(F32) * s * (1.0 - s)).astype(BF16)
        slot_copy(slot).start()

        @pl.when(i == steps - 1)
        def _():
            slot_copy(1 - slot).wait()
            slot_copy(slot).wait()

    assert steps >= 2
    whole = lambda a: pl.BlockSpec(a.shape, lambda i: (0, 0))
    return pl.pallas_call(
        body, name="gate_merge_bwd", grid=(steps,),
        in_specs=[_row_spec(tr, w, 10), _row_spec(tr, w, 11), _row_spec(tr, w, 12)] + [_row_spec(tr, w)] * 3
        + [whole(wt) for wt in weights] + [_row_spec(tr, w), whole(wout), _any_spec()],
        out_specs=[_row_spec(tr, w)] * 3 + [_row_spec(tr, wt.shape[0]) for wt in weights] + [_any_spec()],
        out_shape=[jax.ShapeDtypeStruct((SEQ, w), BF16)] * 3
        + [jax.ShapeDtypeStruct((SEQ, wt.shape[0]), F32) for wt in weights]
        + [jax.ShapeDtypeStruct((SEQ, IN_COLS), BF16)],
        scratch_shapes=[pltpu.VMEM((2, tr, 3 * w), BF16), pltpu.SemaphoreType.DMA((2,))],
        input_output_aliases={11: 6},
        compiler_params=_params(("arbitrary",), 34 * tr * w * 4),
    )(proj, proj, proj, *ts, *weights, dx1b, wout, dproj)


def _ffn_in_swiglu(h2, wfin, tm, tn):
    gm, gn = SEQ // tm, D_FF // tn

    def body(h_ref, wa_ref, wb_ref, a_ref, b_ref, act_ref, actt_ref):
        h = h_ref[...]
        a = _dot(h, wa_ref[...], _NN)
        b = _dot(h, wb_ref[...], _NN)
        act = a * _sigmoid(a) * b
        a_ref[...] = a.astype(BF16)
        b_ref[...] = b.astype(BF16)
        act_ref[...] = act.astype(BF16)
        actt_ref[...] = act.T.astype(BF16)

    tile = pl.BlockSpec((tm, tn), lambda j, i: (i, j))
    shape = jax.ShapeDtypeStruct((SEQ, D_FF), BF16)
    return pl.pallas_call(
        body, name="ffn_in_swiglu_fwd", grid=(gn, gm),
        in_specs=[pl.BlockSpec((tm, D_MODEL), lambda j, i: (i, 0)),
                  pl.BlockSpec((D_MODEL, tn), lambda j, i: (0, j)),
                  pl.BlockSpec((D_MODEL, tn), lambda j, i: (0, gn + j))],
        out_specs=[tile, tile, tile, pl.BlockSpec((tn, tm), lambda j, i: (j, i))],
        out_shape=[shape, shape, shape, jax.ShapeDtypeStruct((D_FF, SEQ), BF16)],
        compiler_params=_params(("parallel", "parallel"), 4 * tm * D_MODEL + 8 * D_MODEL * tn + 16 * tm * tn
                                + 6 * tm * tn * 4),
    )(h2, wfin, wfin)


def _ffn_out_bwd_swiglu(dyb, wfout, a, b, tm, tn):
    gm, gn = SEQ // tm, D_FF // tn

    def body(dy_ref, w_ref, a_ref, b_ref, o_ref):
        d = _dot(dy_ref[...], w_ref[...], _NT)
        av = a_ref[...].astype(F32)
        bv = b_ref[...].astype(F32)
        s = _sigmoid(av)
        silu = av * s
        o_ref[0] = (d * bv * (s + silu * (1.0 - s))).astype(BF16)
        o_ref[1] = (d * silu).astype(BF16)

    tile = pl.BlockSpec((tm, tn), lambda i, j: (i, j))
    return pl.pallas_call(
        body, name="ffn_out_bwd_swiglu", grid=(gm, gn),
        in_specs=[pl.BlockSpec((tm, D_MODEL), lambda i, j: (i, 0)), pl.BlockSpec((tn, D_MODEL), lambda i, j: (j, 0)),
                  tile, tile],
        out_specs=pl.BlockSpec((2, tm, tn), lambda i, j: (0, i, j)),
        out_shape=jax.ShapeDtypeStruct((2, SEQ, D_FF), BF16),
        compiler_params=_params(("parallel", "parallel"), 4 * tm * D_MODEL + 4 * tn * D_MODEL + 16 * tm * tn
                                + 6 * tm * tn * 4),
    )(dyb, wfout, a, b)


def _ffn_out_loss_head(x1, act, wfout, target, tr):
    w = D_MODEL

    def body(x_ref, a_ref, w_ref, t_ref, dy_ref, dyb_ref, loss_ref, acc_ref):
        err = x_ref[...] + _dot(a_ref[...], w_ref[...], _NN) - t_ref[...]
        dy = err * (1.0 / w)
        dy_ref[...] = dy
        dyb_ref[...] = dy.astype(BF16)
        part = jnp.sum(err * err, axis=0, keepdims=True)

        @pl.when(pl.program_id(0) == 0)
        def _():
            acc_ref[...] = part

        @pl.when(pl.program_id(0) > 0)
        def _():
            acc_ref[...] += part

        @pl.when(pl.program_id(0) == SEQ // tr - 1)
        def _():
            total = jnp.sum(acc_ref[...], axis=1, keepdims=True) * (0.5 / w)
            loss_ref[...] = jnp.broadcast_to(total, (1, LANE))

    return pl.pallas_call(
        body, name="ffn_out_loss_head", grid=(SEQ // tr,),
        in_specs=[_row_spec(tr, w), _row_spec(tr, D_FF), pl.BlockSpec((D_FF, w), lambda i: (0, 0)), _row_spec(tr, w)],
        out_specs=[_row_spec(tr, w), _row_spec(tr, w), _bcast_spec(LANE)],
        out_shape=[jax.ShapeDtypeStruct((SEQ, w), F32), jax.ShapeDtypeStruct((SEQ, w), BF16),
                   jax.ShapeDtypeStruct((1, LANE), F32)],
        scratch_shapes=[pltpu.VMEM((1, w), F32)],
        compiler_params=_params(("arbitrary",), 12 * tr * w * 4 + 4 * D_FF * w),
    )(x1, act, wfout, target)


GLA_ROWS = 256
GLA_CPB = GLA_ROWS // HG_CHUNK
GLA_NBLK = SEQ // GLA_ROWS
GLA_WAYS = 4
GLA_TRIPS = GLA_NBLK // GLA_WAYS
GLA_GRAD_WAYS = 4
GLA_GRAD_TRIPS = GLA_NBLK // GLA_GRAD_WAYS
GLA_NCK = SEQ // HG_CHUNK
GLA_INTER_WAYS = 8


def _dot_split(m, xv, dims, terms=3):
    dot = lambda t: lax.dot_general(m, t, dims, preferred_element_type=F32)
    hi = xv.astype(BF16)
    r1 = xv - hi.astype(F32)
    mid = r1.astype(BF16)
    if terms == 2:
        return dot(mid) + dot(hi)
    lo = (r1 - mid.astype(F32)).astype(BF16)
    return (dot(lo) + dot(mid)) + dot(hi)


def _gla_block(qc, fc, lbv, masks):
    mask, maskb, direction = masks
    sq = _sigmoid(qc)
    q = qc * sq * HG_SCALE
    sf = _sigmoid(fc)
    forget = lbv + (1.0 - lbv) * sf
    k = 1.0 - forget
    logf = jnp.log(forget)
    b = _dot_split(maskb, logf, _NN)
    ends = []
    for j in range(GLA_CPB):
        lo, hi = j * HG_CHUNK, (j + 1) * HG_CHUNK
        end = jnp.where(direction == 0, b[hi - 1:hi, :], b[lo:lo + 1, :])
        ends.append(jnp.broadcast_to(end, (HG_CHUNK, HEAD_DIM)))
    bt = jnp.concatenate(ends, axis=0)
    eb = jnp.exp(b)
    qt = q * eb
    kt = k * jnp.exp(-b)
    kh = k * jnp.exp(bt - b)
    a = jnp.where(mask, _dot(qt, kt, _NT), 0.0)
    return dict(sq=sq, sf=sf, forget=forget, k=k, b=b, bt=bt, eb=eb, qt=qt, kt=kt, kh=kh, a=a)


def _gla_masks(direction):
    row = lax.broadcasted_iota(jnp.int32, (GLA_ROWS, GLA_ROWS), 0)
    col = lax.broadcasted_iota(jnp.int32, (GLA_ROWS, GLA_ROWS), 1)
    same = (row // HG_CHUNK) == (col // HG_CHUNK)
    mask = jnp.logical_and(same, jnp.where(direction == 0, row - col, col - row) >= 0)
    return mask, jnp.where(mask, 1.0, 0.0).astype(BF16), direction


def _gla_chunk_rows(j):
    return slice(j * HG_CHUNK, (j + 1) * HG_CHUNK)


def _gla_block_rows(b):
    return pl.ds(pl.multiple_of(b * GLA_ROWS, GLA_ROWS), GLA_ROWS)


def _head_spec(col_block0):
    return pl.BlockSpec((SEQ, HEAD_DIM), lambda h, d: (0, col_block0 + h))


def _gla_fwd(proj, lb, gain):
    nck = GLA_NCK

    def body(q_ref, f_ref, v_ref, g_ref, lb_ref, gain_ref, ohg_ref, opre_ref, st_ref, qt_scr, dec_scr, cs_scr):
        d = pl.program_id(1)
        masks = _gla_masks(d)
        lbv = lb_ref[...]

        @pl.when(d == 0)
        def _():
            opre_ref[...] = jnp.zeros_like(opre_ref)

        def intra(s, carry):
            blocks = [s + w * GLA_TRIPS for w in range(GLA_WAYS)]
            rows = [_gla_block_rows(b) for b in blocks]
            loaded = [(q_ref[r, :], f_ref[r, :], v_ref[r, :], opre_ref[r, :]) for r in rows]
            results = []
            for qc, fc, v, o_prev in loaded:
                ck = _gla_block(qc, fc, lbv, masks)
                o = o_prev + _dot(ck["a"], v, _NN)
                cs = [_dot(v[_gla_chunk_rows(j), :], ck["kh"][_gla_chunk_rows(j), :], _TN) for j in range(GLA_CPB)]
                dec = [jnp.exp(ck["bt"][j * HG_CHUNK:j * HG_CHUNK + 8, :]) for j in range(GLA_CPB)]
                results.append((o, ck["qt"].astype(BF16), cs, dec))
            for b, r, (o, qt, cs, dec) in zip(blocks, rows, results):
                opre_ref[r, :] = o
                qt_scr[r, :] = qt
                for j in range(GLA_CPB):
                    cs_scr[b * GLA_CPB + j] = cs[j]
                    dec_scr[b * GLA_CPB + j] = dec[j]
            return carry

        lax.fori_loop(0, GLA_TRIPS, intra, 0)

        def scan(i, st):
            c = jnp.where(d == 0, i, nck - 1 - i)
            st_ref[c] = st
            return st * dec_scr[c][0:1, :] + cs_scr[c]

        lax.fori_loop(0, nck, scan, jnp.zeros((HEAD_DIM, HEAD_DIM), F32), unroll=4)

        def inter(t, carry):
            chunks = [t + w * (nck // GLA_INTER_WAYS) for w in range(GLA_INTER_WAYS)]
            rows = [pl.ds(pl.multiple_of(c * HG_CHUNK, HG_CHUNK), HG_CHUNK) for c in chunks]
            loaded = [(opre_ref[r, :], qt_scr[r, :], st_ref[c]) for r, c in zip(rows, chunks)]
            for r, o in zip(rows, [o_prev + _dot(qt, st, _NT) for o_prev, qt, st in loaded]):
                opre_ref[r, :] = o
            return carry

        lax.fori_loop(0, nck // GLA_INTER_WAYS, inter, 0)

        @pl.when(d == 1)
        def _():
            o = opre_ref[...]
            r = lax.rsqrt(jnp.mean(o * o, axis=-1, keepdims=True) + RMS_EPS)
            g = g_ref[...]
            ohg_ref[...] = (o * r * gain_ref[...] * (g * _sigmoid(g))).astype(BF16)

    blk = SEQ * HEAD_DIM * 4
    return pl.pallas_call(
        body, name="gla_fwd", grid=(HG_HEADS, 2),
        in_specs=[_head_spec(CB_HG_Q),
                  pl.BlockSpec((SEQ, HEAD_DIM), lambda h, d: (0, CB_F + 8 * d + h)),
                  _head_spec(CB_HG_I), _head_spec(CB_HG_G),
                  pl.BlockSpec((None, None, 1, HEAD_DIM), lambda h, d: (d, h, 0, 0)),
                  pl.BlockSpec((1, HEAD_DIM), lambda h, d: (0, 0))],
        out_specs=[_head_spec(0), _head_spec(0),
                   pl.BlockSpec((None, None, nck, HEAD_DIM, HEAD_DIM), lambda h, d: (h, d, 0, 0, 0)),
                   pl.BlockSpec((None, SEQ, HEAD_DIM), lambda h, d: (d, 0, h)),
                   pl.BlockSpec((None, None, nck, 8, HEAD_DIM), lambda h, d: (h, d, 0, 0, 0))],
        out_shape=[jax.ShapeDtypeStruct((SEQ, D_MODEL), BF16), jax.ShapeDtypeStruct((SEQ, D_MODEL), F32),
                   jax.ShapeDtypeStruct((HG_HEADS, 2, nck, HEAD_DIM, HEAD_DIM), F32),
                   jax.ShapeDtypeStruct((2, SEQ, D_MODEL), BF16),
                   jax.ShapeDtypeStruct((HG_HEADS, 2, nck, 8, HEAD_DIM), F32)],
        scratch_shapes=[pltpu.VMEM((nck, HEAD_DIM, HEAD_DIM), F32)],
        compiler_params=_params(("parallel", "arbitrary"), 8 * blk + 3 * blk + 2 * blk + 2 * blk + blk),
    )(proj, proj, proj, proj, lb, gain)


def _gla_bwd(proj, lb, gain, o_pre, states, qt_all, dec_all, do_hg, dproj):
    nck = GLA_NCK
    do, dproj, dgain = _gla_bwd_norm(proj, gain, o_pre, do_hg, dproj, 256)

    def body(q_ref, f_ref, v_ref, lb_ref, do_ref, st_ref, qt_ref, dec_scr, dproj_in, dproj_ref, dlb_ref,
             dq_acc, dv_acc, dst_scr, cs_scr, df_out, dq_out, dv_out, sems):
        del dproj_in
        h, d = pl.program_id(0), pl.program_id(1)
        masks = _gla_masks(d)
        mask, maskb, _ = masks
        lbv = lb_ref[...]

        def column_copy(k, staging, col_block):
            cols = pl.ds(pl.multiple_of(col_block * LANE, LANE), LANE)
            return pltpu.make_async_copy(staging, dproj_ref.at[:, cols], sems.at[k])

        df_copy = column_copy(0, df_out, CB_F + 8 * d + h)
        dq_copy = column_copy(1, dq_out, CB_HG_Q + h)
        dv_copy = column_copy(2, dv_out, CB_HG_I + h)
        last = jnp.logical_and(h == HG_HEADS - 1, d == 1)

        def intra(s, carry):
            blocks = [s + w * GLA_TRIPS for w in range(GLA_WAYS)]
            loaded = [(qt_ref[r, :], do_ref[r, :]) for r in map(_gla_block_rows, blocks)]
            results = [[_dot(doc[_gla_chunk_rows(j), :], qt[_gla_chunk_rows(j), :], _TN) for j in range(GLA_CPB)]
                       for qt, doc in loaded]
            for b, cs in zip(blocks, results):
                for j in range(GLA_CPB):
                    cs_scr[b * GLA_CPB + j] = cs[j]
            return carry

        lax.fori_loop(0, GLA_TRIPS, intra, 0)

        def scan(i, dst):
            c = jnp.where(d == 0, nck - 1 - i, i)
            dst_scr[c] = dst
            return dst * dec_scr[c][0:1, :] + cs_scr[c]

        lax.fori_loop(0, nck, scan, jnp.zeros((HEAD_DIM, HEAD_DIM), F32), unroll=4)

        @pl.when(d == 0)
        def _():
            dq_acc[...] = jnp.zeros_like(dq_acc)
            dv_acc[...] = jnp.zeros_like(dv_acc)

        def block_grads(qc, fc, v, doc, states_in, dstates, decays):
            ck = _gla_block(qc, fc, lbv, masks)
            qt, kt, kh, a = ck["qt"], ck["kt"], ck["kh"], ck["a"]
            da = jnp.where(mask, _dot(doc, v, _NT), 0.0)
            dqt_i = _dot(da, kt, _NN)
            dkt = _dot(da, qt, _TN)
            dv_i = _dot(a, doc, _TN)
            dqt_p, dv_p, dkh_p, dbt_p = [], [], [], []
            for j in range(GLA_CPB):
                cr = _gla_chunk_rows(j)
                st_in, dst = states_in[j], dstates[j]
                dqt_p.append(dqt_i[cr, :] + _dot(doc[cr, :], st_in, _NN))
                dv_p.append(dv_i[cr, :] + _dot(kh[cr, :], dst, _NT))
                dkh_j = _dot(v[cr, :], dst, _NN)
                dkh_p.append(dkh_j)
                dbt_j = (decays[j][0:1, :] * jnp.sum(dst * st_in, axis=0, keepdims=True)
                         + jnp.sum(dkh_j * kh[cr, :], axis=0, keepdims=True))
                dbt_p.append(jnp.broadcast_to(dbt_j, (HG_CHUNK, HEAD_DIM)))
            dqt = jnp.concatenate(dqt_p, axis=0)
            dv = jnp.concatenate(dv_p, axis=0)
            dkh = jnp.concatenate(dkh_p, axis=0)
            dbt = jnp.concatenate(dbt_p, axis=0)
            db = dqt * qt - dkt * kt - dkh * kh
            dq = dqt * ck["eb"]
            dk = dkt * jnp.exp(-ck["b"]) + dkh * jnp.exp(ck["bt"] - ck["b"])
            dlogf = _dot_split(maskb, db, _TN, terms=2) + dbt
            dforget = dlogf / ck["forget"] - dk
            sf, sq = ck["sf"], ck["sq"]
            df = (dforget * (1.0 - lbv) * sf * (1.0 - sf)).astype(BF16)
            dqc = dq * HG_SCALE * (sq + qc * sq * (1.0 - sq))
            return df, dqc, dv, jnp.sum(dforget * (1.0 - sf), axis=0, keepdims=True)

        def grads(s, dlb):
            blocks = [s + w * GLA_GRAD_TRIPS for w in range(GLA_GRAD_WAYS)]
            rows = [_gla_block_rows(b) for b in blocks]
            loaded = []
            for b, r in zip(blocks, rows):
                chunks = [b * GLA_CPB + j for j in range(GLA_CPB)]
                loaded.append((q_ref[r, :], f_ref[r, :], v_ref[r, :], do_ref[r, :], [st_ref[c] for c in chunks],
                               [dst_scr[c] for c in chunks], [dec_scr[c] for c in chunks], dq_acc[r, :], dv_acc[r, :]))
            results = [block_grads(*t[:7]) + (t[7], t[8]) for t in loaded]
            for r, (df, dqc, dv, dlb_part, dq_prev, dv_prev) in zip(rows, results):
                df_out[r, :] = df
                dq_acc[r, :] = dq_prev + dqc
                dv_acc[r, :] = dv_prev + dv
                dlb = dlb + dlb_part
            return dlb

        @pl.when(jnp.logical_or(h > 0, d > 0))
        def _():
            df_copy.wait()

        dlb_ref[...] = lax.fori_loop(0, GLA_GRAD_TRIPS, grads, jnp.zeros((1, HEAD_DIM), F32))
        df_copy.start()

        @pl.when(d == 1)
        def _():
            @pl.when(h > 0)
            def _():
                dq_copy.wait()
                dv_copy.wait()

            dq_out[...] = dq_acc[...].astype(BF16)
            dv_out[...] = dv_acc[...].astype(BF16)
            dq_copy.start()
            dv_copy.start()

        @pl.when(last)
        def _():
            df_copy.wait()
            dq_copy.wait()
            dv_copy.wait()

    blk = SEQ * HEAD_DIM * 4
    state_bytes = nck * HEAD_DIM * HEAD_DIM * 4
    dproj, dlb = pl.pallas_call(
        body, name="gla_bwd", grid=(HG_HEADS, 2),
        in_specs=[_head_spec(CB_HG_Q),
                  pl.BlockSpec((SEQ, HEAD_DIM), lambda h, d: (0, CB_F + 8 * d + h)),
                  _head_spec(CB_HG_I),
                  pl.BlockSpec((None, None, 1, HEAD_DIM), lambda h, d: (d, h, 0, 0)),
                  _head_spec(0),
                  pl.BlockSpec((None, None, nck, HEAD_DIM, HEAD_DIM), lambda h, d: (h, d, 0, 0, 0)),
                  pl.BlockSpec((None, SEQ, HEAD_DIM), lambda h, d: (d, 0, h)),
                  pl.BlockSpec((None, None, nck, 8, HEAD_DIM), lambda h, d: (h, d, 0, 0, 0)),
                  _any_spec()],
        out_specs=[_any_spec(), pl.BlockSpec((None, None, 1, HEAD_DIM), lambda h, d: (d, h, 0, 0))],
        out_shape=[jax.ShapeDtypeStruct((SEQ, IN_COLS), BF16), jax.ShapeDtypeStruct((2, HG_HEADS, 1, HEAD_DIM), F32)],
        scratch_shapes=[pltpu.VMEM((SEQ, HEAD_DIM), F32)] * 2
        + [pltpu.VMEM((nck, HEAD_DIM, HEAD_DIM), F32)] * 2
        + [pltpu.VMEM((SEQ, HEAD_DIM), BF16)] * 3 + [pltpu.SemaphoreType.DMA((3,))],
        input_output_aliases={8: 0},
        compiler_params=_params(("arbitrary", "arbitrary"), 8 * blk + 4 * state_bytes + 3 * blk + 3 * blk),
    )(proj, proj, proj, lb, do, states, qt_all, dec_all, dproj)
    return dproj, dlb, dgain


def _gla_bwd_norm(proj, gain, o_pre, do_hg, dproj, tr):
    w = D_MODEL

    def body(g_ref, gain_ref, opre_ref, dohg_ref, dproj_in, do_ref, dg_ref, dgain_ref):
        del dproj_in
        gainv = gain_ref[...]
        part = jnp.zeros((1, HEAD_DIM), F32)
        for h in range(HG_HEADS):
            hs = slice(h * HEAD_DIM, (h + 1) * HEAD_DIM)
            o = opre_ref[:, hs]
            r = lax.rsqrt(jnp.mean(o * o, axis=-1, keepdims=True) + RMS_EPS)
            ohat = o * r
            g = g_ref[:, hs]
            sg = _sigmoid(g)
            silu = g * sg
            dout = dohg_ref[:, hs]
            dg_ref[:, hs] = (dout * ohat * gainv * (sg + silu * (1.0 - sg))).astype(BF16)
            dy = dout * silu
            part = part + jnp.sum(dy * ohat, axis=0, keepdims=True)
            dn = dy * gainv
            do_ref[:, hs] = r * (dn - ohat * jnp.mean(dn * ohat, axis=-1, keepdims=True))

        @pl.when(pl.program_id(0) == 0)
        def _():
            dgain_ref[...] = part

        @pl.when(pl.program_id(0) > 0)
        def _():
            dgain_ref[...] += part

    return pl.pallas_call(
        body, name="gla_bwd_norm", grid=(SEQ // tr,),
        in_specs=[_row_spec(tr, w, CB_HG_G * LANE // w), _bcast_spec(HEAD_DIM), _row_spec(tr, w), _row_spec(tr, w),
                  _any_spec()],
        out_specs=[_row_spec(tr, w), _row_spec(tr, w, CB_HG_G * LANE // w), _bcast_spec(HEAD_DIM)],
        out_shape=[jax.ShapeDtypeStruct((SEQ, w), F32), jax.ShapeDtypeStruct((SEQ, IN_COLS), BF16),
                   jax.ShapeDtypeStruct((1, HEAD_DIM), F32)],
        input_output_aliases={4: 1},
        compiler_params=_params(("arbitrary",), 16 * tr * w * 4),
    )(proj, gain, o_pre, do_hg, dproj)


def _da_residue_rows(r, n0, size, d):
    if d == 1:
        return pl.ds(pl.multiple_of(n0, 8), size)
    return pl.ds(r + n0 * d, size, stride=d)


def _da_residue_ways(d, nqb):
    return min(d, 4) if nqb <= 2 else 1


def _da_rmsnorm(x, gain):
    r = lax.rsqrt(jnp.mean(x * x, axis=-1, keepdims=True) + RMS_EPS)
    return x * r, r, x * r * gain


def _da_scores(qn_scr, kn_scr, slope, i, ld):
    w0 = jnp.clip(i * DA_QB - DA_RADIUS, 0, ld - DA_WIN)
    w0 = pl.multiple_of(w0, DA_RADIUS)
    qrows = pl.ds(pl.multiple_of(i * DA_QB, DA_QB), DA_QB)
    win = pl.ds(w0, DA_WIN)
    qb = qn_scr[qrows, :]
    kw = kn_scr[win, :]
    s = _dot(qb, kw, _NT) * DA_SCALE
    qpos = i * DA_QB + lax.broadcasted_iota(jnp.int32, (DA_QB, DA_WIN), 0)
    kpos = w0 + lax.broadcasted_iota(jnp.int32, (DA_QB, DA_WIN), 1)
    arel = jnp.abs(kpos - qpos)
    s = s - slope * arel.astype(F32)
    s = jnp.where(arel <= DA_RADIUS, s, NEG_INF)
    return s, qb, kw, qrows, win


def _da_slopes(group):
    d = DA_DILATIONS[group]
    sl = _alibi_slopes(DA_HEADS)[4 * group:4 * group + 4] * d
    return jnp.asarray(np.broadcast_to(sl[:, None, None], (4, 1, LANE)).copy())


def _da_fwd(proj, gq, gk, group, others=()):
    d = DA_DILATIONS[group]
    ld = SEQ // d
    nqb = ld // DA_QB
    n_other = 2 * len(others)

    ways = min(DA_FWD_WAYS, nqb)
    rways = _da_residue_ways(d, nqb)

    def body(q_ref, k_ref, v_ref, gq_ref, gk_ref, sl_ref, *refs):
        other_refs, refs = refs[:n_other], refs[n_other:]
        if others:
            ob_ref, of_ref, lsej_ref, qn_scr, kn_scr, v_scr, o_ref, lse_ref = refs
        else:
            o_ref, lse_ref, qn_scr, kn_scr, v_scr = refs
        slope = sl_ref[:, 0:1]

        def residues(t, carry):
            rs = [t * rways + u for u in range(rways)]
            for u, r in enumerate(rs):
                sel = _da_residue_rows(r, 0, ld, d)
                qn_scr[u] = _da_rmsnorm(q_ref[sel, :], gq_ref[...])[2].astype(BF16)
                kn_scr[u] = _da_rmsnorm(k_ref[sel, :], gk_ref[...])[2].astype(BF16)
                v_scr[u] = v_ref[sel, :].astype(BF16)

            def block(u, i):
                s, _, _, _, win = _da_scores(qn_scr.at[u], kn_scr.at[u], slope, i, ld)
                m = jnp.max(s, axis=-1, keepdims=True)
                p = jnp.exp(s - m)
                l = jnp.sum(p, axis=-1, keepdims=True)
                return _dot(p, v_scr[u, win, :], _NN) / l, jnp.broadcast_to(m + jnp.log(l), (DA_QB, HEAD_DIM))

            def step(i, c2):
                todo = [(u, r, i + w * (nqb // ways)) for u, r in enumerate(rs) for w in range(ways)]
                for (u, r, b), (o, lse) in zip(todo, [block(u, b) for u, _, b in todo]):
                    out = _da_residue_rows(r, b * DA_QB, DA_QB, d)
                    o_ref[out, :] = o
                    lse_ref[out, :] = lse
                return c2

            return lax.fori_loop(0, nqb // ways, step, carry)

        lax.fori_loop(0, d // rways, residues, 0)

        if others:
            pieces = 8
            rows = SEQ // pieces

            def merge(t, carry):
                r = pl.ds(pl.multiple_of(t * rows, rows), rows)
                outs = [ref[r, :] for ref in other_refs[0::2]] + [o_ref[r, :]]
                lses = [ref[r, :] for ref in other_refs[1::2]] + [lse_ref[r, :]]
                m = lses[0]
                for l in lses[1:]:
                    m = jnp.maximum(m, l)
                es = [jnp.exp(l - m) for l in lses]
                tot = sum(es[1:], es[0])
                o = sum((e * v for e, v in zip(es[1:], outs[1:])), es[0] * outs[0]) / tot
                of_ref[r, :] = o
                ob_ref[r, :] = o.astype(BF16)
                lsej_ref[r, :] = m + jnp.log(tot)
                return carry

            lax.fori_loop(0, pieces, merge, 0)

    seq_spec = lambda base: pl.BlockSpec((SEQ, HEAD_DIM), lambda h: (0, base + 4 * group + h))
    out_spec = pl.BlockSpec((SEQ, HEAD_DIM), lambda h: (0, h))
    gain_spec = pl.BlockSpec((1, HEAD_DIM), lambda h: (0, 0))
    blk = SEQ * HEAD_DIM * 4
    fshape = jax.ShapeDtypeStruct((SEQ, DA_WIDTH), F32)
    scratch = [pltpu.VMEM((rways, ld, HEAD_DIM), BF16)] * 3
    if others:
        out_shape = [jax.ShapeDtypeStruct((SEQ, DA_WIDTH), BF16), fshape, fshape]
        scratch += [pltpu.VMEM((SEQ, HEAD_DIM), F32)] * 2
    else:
        out_shape = [fshape, fshape]
    return pl.pallas_call(
        body, name=f"da_fwd_g{group}", grid=(DA_HEADS_PER_GROUP,),
        in_specs=[seq_spec(CB_DA_Q), seq_spec(CB_DA_K), seq_spec(CB_DA_V), gain_spec, gain_spec,
                  pl.BlockSpec((None, 1, LANE), lambda h: (h, 0, 0))] + [out_spec] * n_other,
        out_specs=[out_spec] * len(out_shape), out_shape=out_shape, scratch_shapes=scratch,
        compiler_params=_params(("parallel",), 10 * blk + 2 * blk + (n_other + 3) * blk),
    )(proj, proj, proj, gq, gk, _da_slopes(group), *[a for pair in others for a in pair])


def _da_rowdot(do, o, tr):
    w = DA_WIDTH

    def body(do_ref, o_ref, out_ref):
        prod = do_ref[...] * o_ref[...]
        for h in range(w // HEAD_DIM):
            sl = slice(h * HEAD_DIM, (h + 1) * HEAD_DIM)
            out_ref[:, sl] = jnp.broadcast_to(jnp.sum(prod[:, sl], axis=-1, keepdims=True), (tr, HEAD_DIM))

    return pl.pallas_call(
        body, name="da_rowdot", grid=(SEQ // tr,), in_specs=[_row_spec(tr, w)] * 2,
        out_specs=_row_spec(tr, w), out_shape=jax.ShapeDtypeStruct((SEQ, w), F32),
        compiler_params=_params(("parallel",), 10 * tr * w * 4),
    )(do, o)


def _da_bwd(proj, gq, gk, do, lse, dd, dproj, group):
    d = DA_DILATIONS[group]
    ld = SEQ // d
    nqb = ld // DA_QB
    ways = min(DA_WAYS, nqb)
    rways = _da_residue_ways(d, nqb)

    def body(q_ref, k_ref, v_ref, gq_ref, gk_ref, sl_ref, do_ref, lse_ref, dd_ref, dproj_in,
             dproj_ref, dgq_ref, dgk_ref,
             qn_scr, kn_scr, v_scr, dqn_scr, dkn_scr, dvr_scr, rq_scr, rk_scr, dq_scr, dk_scr, dv_scr,
             dq_out, dk_out, dv_out, sems):
        del dproj_in
        head = pl.program_id(0)
        gqv, gkv = gq_ref[...], gk_ref[...]
        slope = sl_ref[:, 0:1]

        copies = []
        for k, (staging, base) in enumerate(((dq_out, CB_DA_Q), (dk_out, CB_DA_K), (dv_out, CB_DA_V))):
            cols = pl.ds(pl.multiple_of((base + 4 * group + head) * LANE, LANE), LANE)
            copies.append(pltpu.make_async_copy(staging, dproj_ref.at[:, cols], sems.at[k]))

        def residues(t, carry):
            rs = [t * rways + u for u in range(rways)]
            sels = [_da_residue_rows(r, 0, ld, d) for r in rs]
            for u, sel in enumerate(sels):
                for x_ref, gv, n_scr, r_scr in ((q_ref, gqv, qn_scr, rq_scr), (k_ref, gkv, kn_scr, rk_scr)):
                    _, rstd, normed = _da_rmsnorm(x_ref[sel, :], gv)
                    n_scr[u] = normed.astype(BF16)
                    r_scr[u] = jnp.broadcast_to(rstd, (ld, HEAD_DIM))
                v_scr[u] = v_ref[sel, :].astype(BF16)
            dkn_scr[...] = jnp.zeros_like(dkn_scr)
            dvr_scr[...] = jnp.zeros_like(dvr_scr)

            def block(u, r, i):
                s, qb, kw, qrows, win = _da_scores(qn_scr.at[u], kn_scr.at[u], slope, i, ld)
                src = _da_residue_rows(r, i * DA_QB, DA_QB, d)
                p = jnp.exp(s - lse_ref[src, :][:, 0:1])
                dob = do_ref[src, :]
                dp = _dot(dob, v_scr[u, win, :], _NT)
                ds = p * (dp - dd_ref[src, :][:, 0:1]) * DA_SCALE
                return u, qrows, win, _dot(p, dob, _TN), _dot(ds, kw, _NN), _dot(ds, qb, _TN)

            def step(i, c2):
                todo = [(u, r, i + w * (nqb // ways)) for u, r in enumerate(rs) for w in range(ways)]
                for u, qrows, win, dv, dqn, dkn in [block(*t) for t in todo]:
                    dvr_scr[u, win, :] += dv
                    dqn_scr[u, qrows, :] = dqn
                    dkn_scr[u, win, :] += dkn
                return c2

            lax.fori_loop(0, nqb // ways, step, 0)

            pq, pk = carry
            for u, sel in enumerate(sels):
                parts = []
                for x_ref, gv, dn_scr, dx_scr, r_scr in ((q_ref, gqv, dqn_scr, dq_scr, rq_scr),
                                                         (k_ref, gkv, dkn_scr, dk_scr, rk_scr)):
                    rstd = r_scr[u]
                    hat = x_ref[sel, :] * rstd
                    dn = dn_scr[u]
                    dyg = dn * gv
                    dx_scr[sel, :] = rstd * (dyg - hat * jnp.mean(dyg * hat, axis=-1, keepdims=True))
                    parts.append(jnp.sum(dn * hat, axis=0, keepdims=True))
                dv_scr[sel, :] = dvr_scr[u]
                pq, pk = pq + parts[0], pk + parts[1]
            return pq, pk

        zero = jnp.zeros((1, HEAD_DIM), F32)
        pq, pk = lax.fori_loop(0, d // rways, residues, (zero, zero))

        @pl.when(pl.program_id(0) == 0)
        def _():
            dgq_ref[...] = pq
            dgk_ref[...] = pk

        @pl.when(pl.program_id(0) > 0)
        def _():
            dgq_ref[...] += pq
            dgk_ref[...] += pk

        @pl.when(head > 0)
        def _():
            for cp in copies:
                cp.wait()

        dq_out[...] = dq_scr[...].astype(BF16)
        dk_out[...] = dk_scr[...].astype(BF16)
        dv_out[...] = dv_scr[...].astype(BF16)
        for cp in copies:
            cp.start()

        @pl.when(head == DA_HEADS_PER_GROUP - 1)
        def _():
            for cp in copies:
                cp.wait()

    seq_spec = lambda base: pl.BlockSpec((SEQ, HEAD_DIM), lambda h: (0, base + 4 * group + h))
    out_spec = pl.BlockSpec((SEQ, HEAD_DIM), lambda h: (0, h))
    gain_spec = pl.BlockSpec((1, HEAD_DIM), lambda h: (0, 0))
    gshape = jax.ShapeDtypeStruct((1, HEAD_DIM), F32)
    blk = SEQ * HEAD_DIM * 4
    return pl.pallas_call(
        body, name=f"da_bwd_g{group}", grid=(DA_HEADS_PER_GROUP,),
        in_specs=[seq_spec(CB_DA_Q), seq_spec(CB_DA_K), seq_spec(CB_DA_V), gain_spec, gain_spec,
                  pl.BlockSpec((None, 1, LANE), lambda h: (h, 0, 0)), out_spec, out_spec, out_spec, _any_spec()],
        out_specs=[_any_spec(), gain_spec, gain_spec],
        out_shape=[jax.ShapeDtypeStruct((SEQ, IN_COLS), BF16), gshape, gshape],
        scratch_shapes=[pltpu.VMEM((rways, ld, HEAD_DIM), BF16)] * 3 + [pltpu.VMEM((rways, ld, HEAD_DIM), F32)] * 5
        + [pltpu.VMEM((SEQ, HEAD_DIM), F32)] * 3 + [pltpu.VMEM((SEQ, HEAD_DIM), BF16)] * 3
        + [pltpu.SemaphoreType.DMA((3,))],
        input_output_aliases={9: 0},
        compiler_params=_params(("arbitrary",), 12 * blk + 3 * blk + 3 * blk + 3 * blk + 2 * blk),
    )(proj, proj, proj, gq, gk, _da_slopes(group), do, lse, dd, dproj)


MEM_WAYS = 4


def _mem_probs(q, kn, gq):
    qhat, rq, qn = _da_rmsnorm(q, gq)
    s = _dot(qn, kn, _NT) * MEM_SCALE
    m = jnp.max(s, axis=-1, keepdims=True)
    e = jnp.exp(s - m)
    p = e / jnp.sum(e, axis=-1, keepdims=True)
    return p, qhat, rq, qn


def _mem_pieces(tq):
    rows = tq // MEM_WAYS
    return [slice(w * rows, (w + 1) * rows) for w in range(MEM_WAYS)]


def _mem_fwd(proj, kv, gq, gk, tq):
    def body(q_ref, k_ref, v_ref, gq_ref, gk_ref, o_ref):
        kn = _da_rmsnorm(k_ref[...], gk_ref[...])[2]
        v, gqv = v_ref[...], gq_ref[...]
        pieces = _mem_pieces(tq)
        outs = [_dot(_mem_probs(q_ref[rs, :], kn, gqv)[0], v, _NN) for rs in pieces]
        for rs, o in zip(pieces, outs):
            o_ref[rs, :] = o.astype(BF16)

    gain_spec = pl.BlockSpec((1, HEAD_DIM), lambda h, i: (0, 0))
    return pl.pallas_call(
        body, name="mem_fwd", grid=(MEM_HEADS, SEQ // tq),
        in_specs=[pl.BlockSpec((tq, HEAD_DIM), lambda h, i: (i, CB_MEM_Q + h)),
                  pl.BlockSpec((N_MEM, HEAD_DIM), lambda h, i: (0, h)),
                  pl.BlockSpec((N_MEM, HEAD_DIM), lambda h, i: (0, MEM_HEADS + h)), gain_spec, gain_spec],
        out_specs=pl.BlockSpec((tq, HEAD_DIM), lambda h, i: (i, h)),
        out_shape=jax.ShapeDtypeStruct((SEQ, MEM_WIDTH), BF16),
        compiler_params=_params(("parallel", "parallel"), 16 * tq * N_MEM * 4),
    )(proj, kv, kv, gq, gk)


def _mem_bwd(proj, kv, gq, gk, do, dproj, tq):
    nq = SEQ // tq

    def body(q_ref, k_ref, v_ref, gq_ref, gk_ref, do_ref, dproj_in, dq_ref, dk_ref, dv_ref, dgq_ref, dgk_ref, dkn_scr):
        del dproj_in
        h, i = pl.program_id(0), pl.program_id(1)
        gqv, gkv = gq_ref[...], gk_ref[...]
        khat, rk, kn = _da_rmsnorm(k_ref[...], gkv)
        v = v_ref[...]

        def piece(rs):
            p, qhat, rq, qn = _mem_probs(q_ref[rs, :], kn, gqv)
            dob = do_ref[rs, :]
            dp = _dot(dob, v, _NT)
            ds = p * (dp - jnp.sum(p * dp, axis=-1, keepdims=True)) * MEM_SCALE
            dqn = _dot(ds, kn, _NN)
            dyg = dqn * gqv
            dq = (rq * (dyg - qhat * jnp.mean(dyg * qhat, axis=-1, keepdims=True))).astype(BF16)
            return dq, _dot(p, dob, _TN), _dot(ds, qn, _TN), jnp.sum(dqn * qhat, axis=0, keepdims=True)

        pieces = _mem_pieces(tq)
        results = [piece(rs) for rs in pieces]
        for rs, res in zip(pieces, results):
            dq_ref[rs, :] = res[0]
        dvp = sum((res[1] for res in results[1:]), results[0][1])
        dknp = sum((res[2] for res in results[1:]), results[0][2])
        dgq_part = sum((res[3] for res in results[1:]), results[0][3])
        first = jnp.logical_and(h == 0, i == 0)

        @pl.when(first)
        def _():
            dgq_ref[...] = dgq_part

        @pl.when(jnp.logical_not(first))
        def _():
            dgq_ref[...] += dgq_part

        @pl.when(i == 0)
        def _():
            dv_ref[...] = dvp
            dkn_scr[...] = dknp

        @pl.when(i > 0)
        def _():
            dv_ref[...] += dvp
            dkn_scr[...] += dknp

        @pl.when(i == nq - 1)
        def _():
            dkn = dkn_scr[...]
            dkg = dkn * gkv
            dk_ref[...] = rk * (dkg - khat * jnp.mean(dkg * khat, axis=-1, keepdims=True))
            dgk_part = jnp.sum(dkn * khat, axis=0, keepdims=True)

            @pl.when(h == 0)
            def _():
                dgk_ref[...] = dgk_part

            @pl.when(h > 0)
            def _():
                dgk_ref[...] += dgk_part

    gain_spec = pl.BlockSpec((1, HEAD_DIM), lambda h, i: (0, 0))
    kvout = pl.BlockSpec((N_MEM, HEAD_DIM), lambda h, i: (0, h))
    return pl.pallas_call(
        body, name="mem_bwd", grid=(MEM_HEADS, nq),
        in_specs=[pl.BlockSpec((tq, HEAD_DIM), lambda h, i: (i, CB_MEM_Q + h)),
                  pl.BlockSpec((N_MEM, HEAD_DIM), lambda h, i: (0, h)),
                  pl.BlockSpec((N_MEM, HEAD_DIM), lambda h, i: (0, MEM_HEADS + h)), gain_spec, gain_spec,
                  pl.BlockSpec((tq, HEAD_DIM), lambda h, i: (i, h)), _any_spec()],
        out_specs=[pl.BlockSpec((tq, HEAD_DIM), lambda h, i: (i, CB_MEM_Q + h)), kvout, kvout, gain_spec, gain_spec],
        out_shape=[jax.ShapeDtypeStruct((SEQ, IN_COLS), BF16), jax.ShapeDtypeStruct((N_MEM, MEM_WIDTH), F32),
                   jax.ShapeDtypeStruct((N_MEM, MEM_WIDTH), F32), jax.ShapeDtypeStruct((1, HEAD_DIM), F32),
                   jax.ShapeDtypeStruct((1, HEAD_DIM), F32)],
        scratch_shapes=[pltpu.VMEM((N_MEM, HEAD_DIM), F32)], input_output_aliases={6: 0},
        compiler_params=_params(("arbitrary", "arbitrary"), 24 * tq * N_MEM * 4),
    )(proj, kv, kv, gq, gk, do, dproj)


def _mesh_position():
    return lax.axis_index("x"), lax.axis_index("y"), lax.axis_index("c")


def _any_spec():
    return pl.BlockSpec(memory_space=pl.ANY)


def _all_gather(shards):
    n = len(shards)

    def body(*refs):
        ins, outs = refs[:n], refs[n:2 * n]
        send_sems, recv_sems, local_sems = refs[2 * n:]
        x, y, c = _mesh_position()
        me, sibling = (x, y, c), (x, y, 1 - c)
        first = (jnp.where(c == 0, 1 - x, x), jnp.where(c == 0, y, 1 - y), c)
        second = (jnp.where(c == 0, x, 1 - x), jnp.where(c == 0, 1 - y, y), c)
        diagonal = (1 - x, 1 - y, c)

        def copy(w, k, block, to, src=None):
            px, py, pc = block
            rows = outs[w].at[4 * px + 2 * py + pc]
            return pltpu.make_async_remote_copy(
                src_ref=rows if src is None else src, dst_ref=rows,
                send_sem=send_sems.at[7 * w + k], recv_sem=recv_sems.at[7 * w + k],
                device_id=to, device_id_type=MESH)

        started = []
        for w in range(n):
            mine = pltpu.make_async_copy(ins[w], outs[w].at[4 * x + 2 * y + c], local_sems.at[w])
            mine.start()
            started.append(mine)
        sends = []
        for w in range(n):
            own = [copy(w, 0, me, sibling, src=ins[w]), copy(w, 1, me, first, src=ins[w]),
                   copy(w, 2, me, second, src=ins[w])]
            for cp in own:
                cp.start()
            sends += own
        for w in range(n):
            copy(w, 1, first, me).wait_recv()
            follow = [copy(w, 3, first, second), copy(w, 4, first, sibling)]
            for cp in follow:
                cp.start()
            copy(w, 2, second, me).wait_recv()
            follow.append(copy(w, 5, second, sibling))
            follow[-1].start()
            sends += follow
        for w in range(n):
            copy(w, 3, diagonal, me).wait_recv()
            passed = copy(w, 6, diagonal, sibling)
            passed.start()
            sends.append(passed)
        for w in range(n):
            for k in (0, 4, 5, 6):
                copy(w, k, sibling, me).wait_recv()
        for cp in sends:
            cp.wait_send()
        for mine in started:
            mine.wait()

    return pl.pallas_call(
        body, name="weights_all_gather",
        in_specs=[_any_spec()] * n, out_specs=[_any_spec()] * n,
        out_shape=[jax.ShapeDtypeStruct((N_DEV,) + s.shape, s.dtype) for s in shards],
        scratch_shapes=[pltpu.SemaphoreType.DMA((7 * n,)), pltpu.SemaphoreType.DMA((7 * n,)),
                        pltpu.SemaphoreType.DMA((n,))],
    )(*shards)


def _chip_of(j, x, y):
    return (1 - x if j & 1 else x, 1 - y if j & 2 else y)


_HBM_SPEC = pl.BlockSpec(memory_space=pltpu.HBM)
_SEM_SPEC = pl.BlockSpec(memory_space=pltpu.SEMAPHORE)
_DATAFLOW_EFFECT = pltpu.SideEffectType.DATAFLOW_SIDE_EFFECTING
TOKEN_SHAPE = (8, D_MODEL)


def _copies_start(name, arrays, n_copies, plan):
    n = len(arrays)

    def body(*refs):
        send_sems, recv_sems, token = refs[n], refs[n + 1], refs[2 * n + 2]
        copies = plan(refs[:n])
        assert len(copies) == n_copies
        for k, (src, dst, dev) in enumerate(copies):
            pltpu.make_async_remote_copy(src_ref=src, dst_ref=dst, send_sem=send_sems.at[k], recv_sem=recv_sems.at[k],
                                         device_id=dev, device_id_type=MESH).start()
        token[...] = jnp.zeros_like(token)

    outs = pl.pallas_call(
        body, name=name,
        out_shape=(pltpu.SemaphoreType.DMA((n_copies,)), pltpu.SemaphoreType.DMA((n_copies,)),
                   *[pltpu.HBM(a.shape, a.dtype) for a in arrays], jax.ShapeDtypeStruct(TOKEN_SHAPE, F32)),
        in_specs=[_HBM_SPEC] * n,
        out_specs=(_SEM_SPEC, _SEM_SPEC, *[_HBM_SPEC] * n, pl.BlockSpec(memory_space=pltpu.VMEM)),
        input_output_aliases={i: i + 2 for i in range(n)},
        compiler_params=pltpu.CompilerParams(has_side_effects=_DATAFLOW_EFFECT),
    )(*[pltpu.with_memory_space_constraint(a, pltpu.HBM) for a in arrays])
    return outs[0], outs[1], list(outs[2:2 + n]), outs[2 + n]


def _copies_wait(name, send_sems, recv_sems, arrays, n_copies, plan, after):
    n = len(arrays)
    after = list(after) if isinstance(after, (list, tuple)) else [after]

    def body(*refs):
        send_ref, recv_ref = refs[n], refs[n + 1]
        copies = plan(refs[:n])
        assert len(copies) == n_copies
        for k, (src, dst, dev) in enumerate(copies):
            cp = pltpu.make_async_remote_copy(src_ref=src, dst_ref=dst, send_sem=send_ref.at[k], recv_sem=recv_ref.at[k],
                                              device_id=dev, device_id_type=MESH)
            cp.wait_send()
            cp.wait_recv()

    outs = pl.pallas_call(
        body, name=name, out_shape=tuple(pltpu.HBM(a.shape, a.dtype) for a in arrays),
        in_specs=[_HBM_SPEC] * n + [_SEM_SPEC, _SEM_SPEC] + [pl.BlockSpec(memory_space=pl.ANY)] * len(after),
        out_specs=tuple([_HBM_SPEC] * n), input_output_aliases={i: i for i in range(n)},
        compiler_params=pltpu.CompilerParams(has_side_effects=_DATAFLOW_EFFECT),
    )(*arrays, send_sems, recv_sems, *after)
    return list(outs)


def _after(small, token):
    return small if token is None else small + token[0:1, :small.shape[-1]]


def _gather_plan_out(n):
    def plan(refs):
        x, y, c = _mesh_position()
        me = 4 * x + 2 * y + c
        copies = []
        for w in range(n):
            land = refs[n + w].at[me]
            copies.append((refs[w], land, (x, y, 1 - c)))
            for j in range(1, 4):
                copies.append((refs[w], land, (*_chip_of(j, x, y), c)))
        return copies
    return plan


def _gather_plan_pass(n):
    def plan(refs):
        x, y, c = _mesh_position()
        copies = []
        for w in range(n):
            for j in range(1, 4):
                px, py = _chip_of(j, x, y)
                rows = refs[w].at[4 * px + 2 * py + c]
                copies.append((rows, rows, (x, y, 1 - c)))
        return copies
    return plan


def _reduce_plan_sibling(n):
    def plan(refs):
        x, y, c = _mesh_position()
        copies = []
        for w in range(n):
            for j in range(4):
                px, py = _chip_of(j, x, y)
                copies.append((refs[w].at[4 * px + 2 * py + (1 - c)], refs[n + w].at[j], (x, y, 1 - c)))
        return copies
    return plan


def _reduce_plan_chips(n):
    def plan(refs):
        x, y, c = _mesh_position()
        copies = []
        for w in range(n):
            for j in range(1, 4):
                copies.append((refs[w].at[j - 1], refs[n + w].at[j - 1], (*_chip_of(j, x, y), c)))
        return copies
    return plan


def _chip_partials(grad, recv, name, tr):
    _, rows, width = grad.shape

    def body(g_ref, r_ref, own_ref, other_ref):
        x, y, c = _mesh_position()
        for j in range(4):
            px, py = _chip_of(j, x, y)
            total = g_ref[4 * px + 2 * py + c].astype(F32) + r_ref[j].astype(F32)
            if j == 0:
                own_ref[...] = total
            else:
                other_ref[j - 1] = total.astype(BF16)

    return pl.pallas_call(
        body, name=name, grid=(rows // tr,),
        in_specs=[pl.BlockSpec((N_DEV, tr, width), lambda i: (0, i, 0)),
                  pl.BlockSpec((4, tr, width), lambda i: (0, i, 0))],
        out_specs=[pl.BlockSpec((tr, width), lambda i: (i, 0)), pl.BlockSpec((3, tr, width), lambda i: (0, i, 0))],
        out_shape=[jax.ShapeDtypeStruct((rows, width), F32), jax.ShapeDtypeStruct((3, rows, width), BF16)],
        compiler_params=_params(("parallel",), 2 * 20 * tr * width * 2 + 8 * tr * width * 4),
    )(grad, recv)


def _adamw_math(w, g, m, v):
    m = ADAM_B1 * m + (1.0 - ADAM_B1) * g
    v = ADAM_B2 * v + (1.0 - ADAM_B2) * (g * g)
    m_hat = m / (1.0 - ADAM_B1 ** ADAM_STEP)
    v_hat = v / (1.0 - ADAM_B2 ** ADAM_STEP)
    delta = -ADAM_LR * (m_hat / (jnp.sqrt(v_hat) + ADAM_EPS) + ADAM_WD * w)
    return delta, m, v


def _adamw_shard(own, recv, w, m, v, name, tr):
    rows, width = own.shape

    def body(own_ref, r_ref, w_ref, m_ref, v_ref, g_ref, d_ref, nm_ref, nv_ref):
        g = own_ref[...]
        for j in range(3):
            g = g + r_ref[j].astype(F32)
        g_ref[...] = g
        d_ref[...], nm_ref[...], nv_ref[...] = _adamw_math(w_ref[...], g, m_ref[...], v_ref[...])

    spec = pl.BlockSpec((tr, width), lambda i: (i, 0))
    shape = jax.ShapeDtypeStruct((rows, width), F32)
    return pl.pallas_call(
        body, name=name, grid=(rows // tr,),
        in_specs=[spec, pl.BlockSpec((3, tr, width), lambda i: (0, i, 0)), spec, spec, spec],
        out_specs=[spec] * 4, out_shape=[shape] * 4,
        compiler_params=_params(("parallel",), 22 * tr * width * 4),
    )(own, recv, w, m, v)


def _small_all_reduce_adamw(gpart, lbpack, wpack, mpack, vpack, after):
    def body(gp_ref, lb_ref, w_ref, m_ref, v_ref, after_ref, g_ref, d_ref, nm_ref, nv_ref, gath_ref, send_sems, recv_sems):
        del after_ref
        x, y, c = _mesh_position()
        me = 4 * x + 2 * y + c
        gath_ref[me] = gp_ref[...]
        copies = []
        for j in range(1, N_DEV):
            peer = (x ^ (j >> 2), y ^ ((j >> 1) & 1), c ^ (j & 1))
            cp = pltpu.make_async_remote_copy(
                src_ref=gp_ref, dst_ref=gath_ref.at[me], send_sem=send_sems.at[j - 1], recv_sem=recv_sems.at[j - 1],
                device_id=peer, device_id_type=MESH)
            cp.start()
            copies.append(cp)
        for cp in copies:
            cp.wait()
        tot = gath_ref[0]
        for s in range(1, N_DEV):
            tot = tot + gath_ref[s]
        lb = lb_ref[...]
        dl = tot[16:32, :] * lb * (1.0 - lb)
        g = jnp.concatenate([tot[0:16, :], dl[0:8, :], -dl[0:8, :], dl[8:16, :], -dl[8:16, :],
                             tot[32:SMALL_GRAD_ROWS, :]], axis=0)
        g_ref[...] = g
        d_ref[...], nm_ref[...], nv_ref[...] = _adamw_math(w_ref[...], g, m_ref[...], v_ref[...])

    vm = pl.BlockSpec(memory_space=pltpu.VMEM)
    shape = jax.ShapeDtypeStruct((SMALL_ROWS, LANE), F32)
    return pl.pallas_call(
        body, name="small_all_reduce_adamw", in_specs=[vm] * 5 + [_any_spec()], out_specs=[vm] * 4,
        out_shape=[shape] * 4,
        scratch_shapes=[pltpu.VMEM((N_DEV, SMALL_GRAD_ROWS, LANE), F32),
                        pltpu.SemaphoreType.DMA((N_DEV - 1,)), pltpu.SemaphoreType.DMA((N_DEV - 1,))],
    )(gpart, lbpack, wpack, mpack, vpack, after)


_SMALL_NAMES = ("norm_mix_gain", "norm_mem_gain", "lb_logits_fw", "lb_logits_bw", "norm_ffn_gain",
                "hg_norm_gain", "da_q_gain", "da_k_gain", "mem_q_gain", "mem_k_gain")
_SMALL_ROW0 = {"norm_mix_gain": 0, "norm_mem_gain": 8, "lb_logits_fw": 16, "lb_logits_bw": 32, "norm_ffn_gain": 48,
               "hg_norm_gain": 56, "da_q_gain": 64, "da_k_gain": 72, "mem_q_gain": 80, "mem_k_gain": 88}
_LOSS_ROW = 96


def _pack_rows(parts, total_rows):
    rows = []
    for p in parts:
        r = p.reshape(-1, LANE)
        rows.append(jnp.pad(r, ((0, -r.shape[0] % 8), (0, 0))))
    used = sum(r.shape[0] for r in rows)
    if total_rows > used:
        rows.append(jnp.zeros((total_rows - used, LANE), F32))
    return jnp.concatenate(rows, axis=0)


def _unpack_small(pack, like):
    out = {}
    for name in _SMALL_NAMES:
        n = like[name].size // LANE
        r0 = _SMALL_ROW0[name]
        out[name] = pack[r0:r0 + n].reshape(like[name].shape)
    return out


def kernel(x, mem, norm_mix_gain, norm_mem_gain, w_in, lb_logits_fw, lb_logits_bw, hg_norm_gain, da_q_gain, da_k_gain, w_mem_kv, mem_q_gain, mem_k_gain, w_proj_hg, w_proj_da, w_proj_mem, w_out, norm_ffn_gain, w_ffn_in, w_ffn_out, loss_target, m_norm_mix_gain, m_norm_mem_gain, m_w_in, m_lb_logits_fw, m_lb_logits_bw, m_hg_norm_gain, m_da_q_gain, m_da_k_gain, m_w_mem_kv, m_mem_q_gain, m_mem_k_gain, m_w_proj_hg, m_w_proj_da, m_w_proj_mem, m_w_out, m_norm_ffn_gain, m_w_ffn_in, m_w_ffn_out, v_norm_mix_gain, v_norm_mem_gain, v_w_in, v_lb_logits_fw, v_lb_logits_bw, v_hg_norm_gain, v_da_q_gain, v_da_k_gain, v_w_mem_kv, v_mem_q_gain, v_mem_k_gain, v_w_proj_hg, v_w_proj_da, v_w_proj_mem, v_w_out, v_norm_ffn_gain, v_w_ffn_in, v_w_ffn_out):
    small_w = dict(norm_mix_gain=norm_mix_gain, norm_mem_gain=norm_mem_gain, lb_logits_fw=lb_logits_fw,
                   lb_logits_bw=lb_logits_bw, norm_ffn_gain=norm_ffn_gain, hg_norm_gain=hg_norm_gain,
                   da_q_gain=da_q_gain, da_k_gain=da_k_gain, mem_q_gain=mem_q_gain, mem_k_gain=mem_k_gain)
    small_m = dict(norm_mix_gain=m_norm_mix_gain, norm_mem_gain=m_norm_mem_gain, lb_logits_fw=m_lb_logits_fw,
                   lb_logits_bw=m_lb_logits_bw, norm_ffn_gain=m_norm_ffn_gain, hg_norm_gain=m_hg_norm_gain,
                   da_q_gain=m_da_q_gain, da_k_gain=m_da_k_gain, mem_q_gain=m_mem_q_gain, mem_k_gain=m_mem_k_gain)
    small_v = dict(norm_mix_gain=v_norm_mix_gain, norm_mem_gain=v_norm_mem_gain, lb_logits_fw=v_lb_logits_fw,
                   lb_logits_bw=v_lb_logits_bw, norm_ffn_gain=v_norm_ffn_gain, hg_norm_gain=v_hg_norm_gain,
                   da_q_gain=v_da_q_gain, da_k_gain=v_da_k_gain, mem_q_gain=v_mem_q_gain, mem_k_gain=v_mem_k_gain)
    big_w = dict(w_in=w_in[0], w_mem_kv=w_mem_kv[0], w_proj_hg=w_proj_hg[0], w_proj_da=w_proj_da[0],
                 w_proj_mem=w_proj_mem[0], w_out=w_out[0], w_ffn_in=w_ffn_in[0], w_ffn_out=w_ffn_out[0])
    big_m = dict(w_in=m_w_in[0], w_mem_kv=m_w_mem_kv[0], w_proj_hg=m_w_proj_hg[0], w_proj_da=m_w_proj_da[0],
                 w_proj_mem=m_w_proj_mem[0], w_out=m_w_out[0], w_ffn_in=m_w_ffn_in[0], w_ffn_out=m_w_ffn_out[0])
    big_v = dict(w_in=v_w_in[0], w_mem_kv=v_w_mem_kv[0], w_proj_hg=v_w_proj_hg[0], w_proj_da=v_w_proj_da[0],
                 w_proj_mem=v_w_proj_mem[0], w_out=v_w_out[0], w_ffn_in=v_w_ffn_in[0], w_ffn_out=v_w_ffn_out[0])

    row_tile = dict(w_in=128, w_mem_kv=128, w_proj_hg=128, w_proj_da=512, w_proj_mem=512, w_out=128,
                    w_ffn_in=128, w_ffn_out=352)
    rest = _BIG_NAMES[1:]
    shards = [big_w[n].astype(BF16) for n in rest]
    state = {}
    big_out = {}

    def reduce_start(group, stacked):
        names = tuple(stacked)
        arrays = [stacked[n] for n in names] + [lax.empty((4,) + stacked[n].shape[1:], BF16) for n in names]
        plan = _reduce_plan_sibling(len(names))
        send, recv, thru, token = _copies_start(f"grads_{group}_sibling_start", arrays, 4 * len(names), plan)
        state[group] = dict(names=names, plan=plan, send=send, recv=recv, arrays=thru)
        return token

    def reduce_middle(group, after):
        st = state[group]
        names, k = st["names"], len(st["names"])
        thru = _copies_wait(f"grads_{group}_sibling_wait", st["send"], st["recv"], st["arrays"], 4 * k, st["plan"], after)
        partials = [_chip_partials(thru[i], thru[k + i], f"chip_partials_{n}", row_tile[n]) for i, n in enumerate(names)]
        arrays = [p[1] for p in partials] + [lax.empty(p[1].shape, BF16) for p in partials]
        plan = _reduce_plan_chips(k)
        send, recv, thru2, token = _copies_start(f"grads_{group}_chips_start", arrays, 3 * k, plan)
        state[group] = dict(names=names, plan=plan, send=send, recv=recv, arrays=thru2, own=[p[0] for p in partials])
        return token

    def reduce_finish(group, after):
        st = state.pop(group)
        names, k = st["names"], len(st["names"])
        thru = _copies_wait(f"grads_{group}_chips_wait", st["send"], st["recv"], st["arrays"], 3 * k, st["plan"], after)
        for i, n in enumerate(names):
            big_out[n] = _adamw_shard(st["own"][i], thru[k + i], big_w[n], big_m[n], big_v[n], "adamw_" + n, row_tile[n])

    step = _local_step_stages(x[0], mem[0], loss_target[0], small_w)
    event, payload = next(step)
    local = None
    while True:
        reply = None
        if event == "gather_w_in":
            reply = _all_gather([big_w["w_in"].astype(BF16)])[0]
        elif event == "begin":
            nr = len(rest)
            me = 4 * lax.axis_index("x") + 2 * lax.axis_index("y") + lax.axis_index("c")
            arrays = shards + [lax.dynamic_update_slice(lax.empty((N_DEV,) + s.shape, BF16), s[None], (me, 0, 0))
                               for s in shards]
            plan = _gather_plan_out(nr)
            send, recv, thru, reply = _copies_start("weights_rest_out_start", arrays, 4 * nr, plan)
            state["gather"] = dict(plan=plan, send=send, recv=recv, arrays=thru)
        elif event == "after_gla_fwd":
            st = state["gather"]
            nr = len(rest)
            thru = _copies_wait("weights_rest_out_wait", st["send"], st["recv"], st["arrays"], 4 * nr, st["plan"], payload)
            plan = _gather_plan_pass(nr)
            send, recv, lands, reply = _copies_start("weights_rest_pass_start", thru[nr:], 3 * nr, plan)
            state["gather"] = dict(plan=plan, send=send, recv=recv, arrays=lands)
        elif event == "need_weights":
            st = state.pop("gather")
            nr = len(rest)
            lands = _copies_wait("weights_rest_pass_wait", st["send"], st["recv"], st["arrays"], 3 * nr, st["plan"], payload)
            reply = dict(zip(rest, lands))
        elif event == "grads_ffn":
            reply = reduce_start("ffn", payload)
        elif event == "after_gate_merge_bwd":
            reply = reduce_middle("ffn", payload)
        elif event == "grads_mix":
            reply = reduce_start("mix", payload)
        elif event == "after_da_bwd_g0":
            reply = reduce_middle("mix", payload)
            reduce_finish("ffn", payload)
        elif event == "after_gla_bwd":
            reduce_finish("mix", payload)
        elif event == "grads_in":
            reply = reduce_start("in", payload)
        elif event == "after_proj_bwd_act_top":
            reply = reduce_middle("in", payload)
        elif event == "end":
            local = payload
            reduce_finish("in", [local["grad_x"]] + [big_out[n][0] for n in rest])
            break
        event, payload = step.send(reply)
    grad_x = local["grad_x"]

    wpack = _pack_rows([small_w[n] for n in _SMALL_NAMES], SMALL_ROWS)
    mpack = _pack_rows([small_m[n] for n in _SMALL_NAMES], SMALL_ROWS)
    vpack = _pack_rows([small_v[n] for n in _SMALL_NAMES], SMALL_ROWS)
    gs, ds, ms, vs = _small_all_reduce_adamw(local["gpart"], local["lbpack"], wpack, mpack, vpack, big_out["w_in"][0])
    loss = gs[_LOSS_ROW, 0]
    small_out = [_unpack_small(t, small_w) for t in (gs, ds, ms, vs)]

    order = ("norm_mix_gain", "norm_mem_gain", "w_in", "lb_logits_fw", "lb_logits_bw", "hg_norm_gain", "da_q_gain",
             "da_k_gain", "w_mem_kv", "mem_q_gain", "mem_k_gain", "w_proj_hg", "w_proj_da", "w_proj_mem", "w_out",
             "norm_ffn_gain", "w_ffn_in", "w_ffn_out")
    outs = [loss, grad_x[None]]
    for kind in range(4):
        for n in order:
            outs.append(big_out[n][kind][None] if n in big_out else small_out[kind][n])
    return tuple(outs)


_BIG_NAMES = ("w_in", "w_mem_kv", "w_proj_hg", "w_proj_da", "w_proj_mem", "w_out", "w_ffn_in", "w_ffn_out")


def _local_step(xs, mems, target, sw, wg):
    step = _local_step_stages(xs, mems, target, sw)
    stacked = {}
    event, payload = next(step)
    while event != "end":
        reply = None
        if event.startswith("grads_"):
            stacked.update(payload)
        elif event == "need_weights":
            reply = wg
        elif event == "gather_w_in":
            reply = wg["w_in"]
        event, payload = step.send(reply)
    return dict(payload, stacked=stacked)


def _local_step_stages(xs, mems, target, sw):
    norm_mix_gain, norm_mem_gain, norm_ffn_gain = sw["norm_mix_gain"], sw["norm_mem_gain"], sw["norm_ffn_gain"]
    lb_logits_fw, lb_logits_bw, hg_norm_gain = sw["lb_logits_fw"], sw["lb_logits_bw"], sw["hg_norm_gain"]
    da_q_gain, da_k_gain, mem_q_gain, mem_k_gain = sw["da_q_gain"], sw["da_k_gain"], sw["mem_q_gain"], sw["mem_k_gain"]

    win_st = yield "gather_w_in", None
    token = yield "begin", None
    lb_fw = _lb_table(lb_logits_fw, "lb_table_fw")
    lb_bw = _lb_table(lb_logits_bw, "lb_table_bw")
    lb = jnp.concatenate([lb_fw, lb_bw], axis=0).reshape(2, HG_HEADS, 1, HEAD_DIM)
    h, h_t = _rmsnorm_fwd(xs, _after(norm_mix_gain, token), "norm_mix_fwd", 512, transposed=True)
    proj = _matmul(h, win_st, "nn", F32, 1024, IN_SHARD, D_MODEL, "proj_fwd", b_stacked=True, n_outer=True)
    o_hg, o_pre, states, hg_qt, hg_decay = _gla_fwd(proj, lb, hg_norm_gain)
    token = yield "after_gla_fwd", o_hg
    da01 = [_da_fwd(proj, _after(da_q_gain, token), da_k_gain, g) for g in (0, 1)]
    o_da, o_da32, lse_da = _da_fwd(proj, _after(da_q_gain, token), da_k_gain, 2, others=da01)
    wg = yield "need_weights", o_da
    wkv = wg["w_mem_kv"].reshape(D_MODEL, 2 * MEM_WIDTH)
    wphg = wg["w_proj_hg"].reshape(D_MODEL, D_MODEL)
    wpda = jnp.transpose(wg["w_proj_da"], (1, 0, 2)).reshape(DA_WIDTH, D_MODEL)
    wpmem = jnp.transpose(wg["w_proj_mem"], (1, 0, 2)).reshape(MEM_WIDTH, D_MODEL)
    wout = wg["w_out"].reshape(D_MODEL, D_MODEL)
    wfin = jnp.transpose(wg["w_ffn_in"], (1, 0, 2)).reshape(D_MODEL, 2 * D_FF)
    wfout = wg["w_ffn_out"].reshape(D_FF, D_MODEL)
    mem_n = _rmsnorm_fwd(mems, norm_mem_gain, "norm_mem_fwd", N_MEM)
    kv = _matmul(mem_n, wkv, "nn", F32, N_MEM, 1024, D_MODEL, "mem_kv_fwd")
    o_mem = _mem_fwd(proj, kv, mem_q_gain, mem_k_gain, 1024)
    branch_w = (wphg, wpda, wpmem)
    merged, t_hg, t_da, t_mem = _branch_merge_fwd(proj, (o_hg, o_da, o_mem), branch_w, 512)
    x1, h2, h2_t = _residual_rmsnorm_fwd(xs, merged, wout, norm_ffn_gain, "out_norm_ffn_fwd", 512)
    ffn_a, ffn_b, act, act_t = _ffn_in_swiglu(h2, wfin, 1024, 1408)
    dy, dyb, loss_part = _ffn_out_loss_head(x1, act, wfout, target, 512)

    dab = _ffn_out_bwd_swiglu(dyb, wfout, ffn_a, ffn_b, 1024, 1408)
    g_wfout = _matmul(act_t, dyb, "nn", BF16, 1408, 1024, 2048, "ffn_out_bwd_w")
    g_wfin = _matmul(h2_t, dab, "nn", BF16, 1024, 1408, 2048, "ffn_in_bwd_w", b_parts=True)
    token = yield "grads_ffn", dict(
        w_ffn_in=jnp.transpose(g_wfin.reshape(D_MODEL, N_DEV, 2 * D_FF // N_DEV), (1, 0, 2)),
        w_ffn_out=g_wfout.reshape(N_DEV, D_FF // N_DEV, D_MODEL))
    dx1, dx1b, g_norm_ffn = _matmul_rmsnorm_bwd(dab, wfin, x1, dy, _after(norm_ffn_gain, token),
                                                "ffn_in_bwd_act_norm", 512, D_FF, a_parts=True)
    g_wout = _matmul(merged, dx1b, "tn", BF16, 1024, 1024, 2048, "out_bwd_w")
    dt_hg, dt_da, dt_mem, do_hg, do_da, do_mem, dproj = _gate_merge_bwd(
        proj, (t_hg, t_da, t_mem), branch_w, dx1b, wout, _dproj_buffer(), 256)
    token = yield "after_gate_merge_bwd", dt_hg
    g_wphg = _matmul(o_hg, dt_hg, "tn", BF16, 1024, 1024, 2048, "proj_hg_bwd_w", after=token)
    g_wpda = _matmul(o_da, dt_da, "tn", BF16, DA_WIDTH, D_MODEL, 2048, "proj_da_bwd_w")
    g_wpmem = _matmul(o_mem, dt_mem, "tn", BF16, MEM_WIDTH, D_MODEL, 2048, "proj_mem_bwd_w")
    by_owner = lambda g: jnp.transpose(g.reshape(g.shape[0], N_DEV, D_MODEL // N_DEV), (1, 0, 2))
    g_wpda, g_wpmem = by_owner(g_wpda), by_owner(g_wpmem)

    dproj, dk_mem, dv_mem, g_mem_q, g_mem_k = _mem_bwd(proj, kv, mem_q_gain, mem_k_gain, do_mem, dproj, 1024)
    dkv = jnp.concatenate([dk_mem, dv_mem], axis=1).astype(BF16)
    g_wkv = _matmul(mem_n, dkv, "tn", BF16, 1024, 1024, N_MEM, "mem_kv_bwd_w")
    dmem_n = _matmul(dkv, wkv, "nt", F32, N_MEM, 1024, 1024, "mem_kv_bwd_act")
    g_norm_mem = _gain_grad(mems, dmem_n, "norm_mem_bwd")
    token = yield "grads_mix", dict(
        w_mem_kv=g_wkv.reshape(N_DEV, D_MODEL // N_DEV, 2 * MEM_WIDTH),
        w_proj_hg=g_wphg.reshape(N_DEV, D_MODEL // N_DEV, D_MODEL),
        w_proj_da=g_wpda, w_proj_mem=g_wpmem,
        w_out=g_wout.reshape(N_DEV, D_MODEL // N_DEV, D_MODEL))

    dd_da = _da_rowdot(do_da, o_da32, 512)
    dproj, g_da_q, g_da_k = _da_bwd(proj, _after(da_q_gain, token), da_k_gain, do_da, lse_da, dd_da, dproj, 0)
    token = yield "after_da_bwd_g0", g_da_q
    for g in (1, 2):
        dproj, gq_part, gk_part = _da_bwd(proj, _after(da_q_gain, token), da_k_gain, do_da, lse_da, dd_da, dproj, g)
        g_da_q, g_da_k = g_da_q + gq_part, g_da_k + gk_part

    dproj, dlb, g_hg_norm = _gla_bwd(proj, lb, hg_norm_gain, o_pre, states, hg_qt, hg_decay, do_hg, dproj)
    yield "after_gla_bwd", dlb

    g_win =_matmul(h_t, dproj, "nn", BF16, 1024, IN_SHARD, 2048, "proj_bwd_w", out_stacked=True)
    token = yield "grads_in", dict(w_in=g_win)
    grad_x, g_mix_top = _matmul_rmsnorm_bwd(dproj, win_st, xs, dx1, norm_mix_gain, "proj_bwd_act_norm_top", 1024,
                                            IN_SHARD, b_stacked=True, after=token, m_blocks=(0, 1), with_bf16=False)
    token = yield "after_proj_bwd_act_top", g_mix_top
    grad_x, g_mix_bottom = _matmul_rmsnorm_bwd(dproj, win_st, xs, dx1, norm_mix_gain, "proj_bwd_act_norm_bottom", 1024,
                                               IN_SHARD, b_stacked=True, after=token, m_blocks=(1, 3),
                                               out_into=grad_x, with_bf16=False)
    g_norm_mix = g_mix_top + g_mix_bottom

    gpart = _pack_rows([g_norm_mix, g_norm_mem, dlb[0], dlb[1], g_norm_ffn, g_hg_norm, g_da_q, g_da_k,
                        g_mem_q, g_mem_k, loss_part], SMALL_GRAD_ROWS)
    lbpack = jnp.concatenate([lb_fw.reshape(8, LANE), lb_bw.reshape(8, LANE)], axis=0)
    yield "end", dict(grad_x=grad_x, gpart=gpart, lbpack=lbpack)
```

```python
import numpy as np
import jax
import jax.numpy as jnp
from jax import lax
from jax.experimental import pallas as pl
from jax.experimental.pallas import tpu as pltpu

F32 = jnp.float32
BF16 = jnp.bfloat16
MESH = pl.DeviceIdType.MESH

SEQ = 4096
D_MODEL = 1024
N_DEV = 8
N_MEM = 256
RMS_EPS = 1e-6
NEG_INF = -1e30
LANE = 128
HEAD_DIM = 128
HG_HEADS = 8
HG_CHUNK = 64
HG_SCALE = HEAD_DIM ** -0.5
DA_DILATIONS = (1, 4, 16)
DA_RADIUS = 64
DA_HEADS_PER_GROUP = 4
DA_HEADS = 12
DA_WIDTH = 512
DA_SCALE = HEAD_DIM ** -0.5
DA_QB = 128
DA_WIN = 256
DA_WAYS = 4
DA_FWD_WAYS = 4
MEM_HEADS = 4
MEM_WIDTH = 512
MEM_SCALE = HEAD_DIM ** -0.5
D_FF = 2816
IN_COLS = 13312
IN_SHARD = IN_COLS // N_DEV
CB_HG_Q, CB_F, CB_HG_I, CB_HG_G = 0, 8, 24, 32
CB_DA_Q, CB_DA_K, CB_DA_V, CB_MEM_Q = 40, 52, 64, 76
ADAM_LR, ADAM_B1, ADAM_B2, ADAM_EPS, ADAM_WD, ADAM_STEP = 0.001, 0.9, 0.999, 1e-08, 0.01, 10
VMEM_BYTES_V7X = 64 * 1024 * 1024
SMALL_ROWS = 104
SMALL_GRAD_ROWS = 88

_NN = (((1,), (0,)), ((), ()))
_NT = (((1,), (1,)), ((), ()))
_TN = (((0,), (0,)), ((), ()))


def _dot(a, b, dims):
    return lax.dot_general(a.astype(BF16), b.astype(BF16), dims, preferred_element_type=F32)


def _sigmoid(x):
    return 0.5 * jnp.tanh(0.5 * x) + 0.5


def _params(semantics, est_bytes):
    limit = int(min(VMEM_BYTES_V7X - (6 << 20), max(56 << 20, est_bytes * 3 // 2)))
    return pltpu.CompilerParams(dimension_semantics=semantics, vmem_limit_bytes=limit)


def _nbytes(shape, dtype):
    return int(np.prod(shape)) * jnp.dtype(dtype).itemsize


def _alibi_slopes(n):
    return (2.0 ** (-8.0 * np.arange(1, n + 1) / n)).astype(np.float32)


def _matmul(a, b, mode, out_dtype, tm, tn, tk, name, b_stacked=False, out_stacked=False, n_outer=False, after=None,
            m_blocks=None, out_into=None, a_parts=False, b_parts=False):
    if a_parts:
        assert mode == "nt" and tk == a.shape[2]
        m, kdim = a.shape[1], a.shape[0] * a.shape[2]
    elif mode == "tn":
        kdim, m = a.shape
    else:
        m, kdim = a.shape
    if b_parts:
        assert mode == "nn" and not b_stacked and b.shape[2] % tn == 0
        n = b.shape[0] * b.shape[2]
    elif b_stacked:
        if mode == "nn":
            n = b.shape[0] * b.shape[2]
            assert tn == b.shape[2] and tk == kdim == b.shape[1]
        else:
            assert mode == "nt" and tk == b.shape[2] and b.shape[0] * tk == kdim
            n = b.shape[1]
    else:
        n = b.shape[0] if mode == "nt" else b.shape[1]
    assert m % tm == 0 and n % tn == 0 and kdim % tk == 0
    gm, gn, gk = m // tm, n // tn, kdim // tk
    i0 = 0
    if m_blocks is not None:
        assert mode != "tn" and not out_stacked
        i0, gm = m_blocks

    def ijk(f):
        if n_outer:
            return lambda j, i, k: f(i + i0, j, k)
        return lambda i, j, k: f(i + i0, j, k)

    if a_parts:
        a_spec = pl.BlockSpec((None, tm, tk), ijk(lambda i, j, k: (k, i, 0)))
    elif mode == "tn":
        a_spec = pl.BlockSpec((tk, tm), ijk(lambda i, j, k: (k, i)))
    else:
        a_spec = pl.BlockSpec((tm, tk), ijk(lambda i, j, k: (i, k)))
    if b_parts:
        per_part = b.shape[2] // tn
        b_spec = pl.BlockSpec((None, tk, tn), ijk(lambda i, j, k: (j // per_part, k, j % per_part)))
    elif b_stacked and mode == "nn":
        b_spec = pl.BlockSpec((None, tk, tn), ijk(lambda i, j, k: (j, 0, 0)))
    elif b_stacked:
        b_spec = pl.BlockSpec((None, tn, tk), ijk(lambda i, j, k: (k, j, 0)))
    elif mode == "nt":
        b_spec = pl.BlockSpec((tn, tk), ijk(lambda i, j, k: (j, k)))
    else:
        b_spec = pl.BlockSpec((tk, tn), ijk(lambda i, j, k: (k, j)))
    if out_stacked:
        assert tm == m
        out_shape = jax.ShapeDtypeStruct((gn, m, tn), out_dtype)
        o_spec = pl.BlockSpec((None, tm, tn), ijk(lambda i, j, k: (j, i, 0)))
    else:
        out_shape = jax.ShapeDtypeStruct((m, n), out_dtype)
        o_spec = pl.BlockSpec((tm, tn), ijk(lambda i, j, k: (i, j)))
    dims = {"nn": _NN, "nt": _NT, "tn": _TN}[mode]

    n_in = 2 + (after is not None) + (out_into is not None)

    def body(*refs):
        a_ref, b_ref, o_ref = refs[0], refs[1], refs[n_in]
        part = _dot(a_ref[...], b_ref[...], dims)
        if gk == 1:
            o_ref[...] = part.astype(out_dtype)
            return
        acc_ref = refs[-1]
        k = pl.program_id(2)

        @pl.when(k == 0)
        def _():
            acc_ref[...] = part

        @pl.when(jnp.logical_and(k > 0, k < gk - 1))
        def _():
            acc_ref[...] += part

        @pl.when(k == gk - 1)
        def _():
            o_ref[...] = (acc_ref[...] + part).astype(out_dtype)

    a_tile = _nbytes((tm, tk), a.dtype)
    b_tile = _nbytes((tk, tn), b.dtype)
    o_tile = _nbytes((tm, tn), out_dtype)
    est = 2 * (a_tile + b_tile + o_tile) + 3 * tm * tn * 4 + (a_tile + b_tile)
    grid = (gn, gm, gk) if n_outer else (gm, gn, gk)
    operands, in_specs = [a, b], [a_spec, b_spec]
    if after is not None:
        operands.append(after)
        in_specs.append(pl.BlockSpec(memory_space=pl.ANY))
    aliases = {}
    if out_into is not None:
        aliases = {len(operands): 0}
        operands.append(out_into)
        in_specs.append(pl.BlockSpec(memory_space=pl.ANY))
    return pl.pallas_call(
        body, name=name, grid=grid, in_specs=in_specs, out_specs=o_spec, out_shape=out_shape,
        scratch_shapes=[] if gk == 1 else [pltpu.VMEM((tm, tn), F32)], input_output_aliases=aliases,
        compiler_params=_params(("parallel", "parallel", "arbitrary"), est),
    )(*operands)


def _row_spec(tr, width, col_block=0):
    return pl.BlockSpec((tr, width), lambda i: (i, col_block))


def _bcast_spec(width):
    return pl.BlockSpec((1, width), lambda i: (0, 0))


def _col_spec(width, tr):
    return pl.BlockSpec((width, tr), lambda i: (0, i))


def _rmsnorm_fwd(x, gain, name, tr, transposed=False):
    rows, width = x.shape

    def body(x_ref, g_ref, o_ref, *t_ref):
        xv = x_ref[...]
        r = lax.rsqrt(jnp.mean(xv * xv, axis=-1, keepdims=True) + RMS_EPS)
        h = xv * r * g_ref[...]
        o_ref[...] = h.astype(BF16)
        if transposed:
            t_ref[0][...] = h.T.astype(BF16)

    out_specs, out_shape = [_row_spec(tr, width)], [jax.ShapeDtypeStruct((rows, width), BF16)]
    if transposed:
        out_specs.append(_col_spec(width, tr))
        out_shape.append(jax.ShapeDtypeStruct((width, rows), BF16))
    out = pl.pallas_call(
        body, name=name, grid=(rows // tr,), in_specs=[_row_spec(tr, width), _bcast_spec(width)],
        out_specs=out_specs, out_shape=out_shape,
        compiler_params=_params(("parallel",), 10 * tr * width * 4),
    )(x, gain)
    return out if transposed else out[0]


def _residual_rmsnorm_fwd(x, merged, wout, gain, name, tr):
    rows, width = x.shape

    def body(x_ref, m_ref, w_ref, g_ref, x1_ref, h_ref, ht_ref):
        xv = x_ref[...] + _dot(m_ref[...], w_ref[...], _NN)
        x1_ref[...] = xv
        r = lax.rsqrt(jnp.mean(xv * xv, axis=-1, keepdims=True) + RMS_EPS)
        h = xv * r * g_ref[...]
        h_ref[...] = h.astype(BF16)
        ht_ref[...] = h.T.astype(BF16)

    return pl.pallas_call(
        body, name=name, grid=(rows // tr,),
        in_specs=[_row_spec(tr, width), _row_spec(tr, merged.shape[1]),
                  pl.BlockSpec(wout.shape, lambda i: (0, 0)), _bcast_spec(width)],
        out_specs=[_row_spec(tr, width), _row_spec(tr, width), _col_spec(width, tr)],
        out_shape=[jax.ShapeDtypeStruct((rows, width), F32), jax.ShapeDtypeStruct((rows, width), BF16),
                   jax.ShapeDtypeStruct((width, rows), BF16)],
        compiler_params=_params(("parallel",), 14 * tr * width * 4),
    )(x, merged, wout, gain)


def _matmul_rmsnorm_bwd(a, b, x, dres, gain, name, tm, tk, a_parts=False, b_stacked=False, after=None,
                        m_blocks=None, out_into=None, with_bf16=True):
    width = x.shape[1]
    m = a.shape[1] if a_parts else a.shape[0]
    kdim = a.shape[0] * a.shape[2] if a_parts else a.shape[1]
    gk = kdim // tk
    i0, gm = (0, m // tm) if m_blocks is None else m_blocks
    n_in = 5 + (after is not None) + (out_into is not None)

    def body(*refs):
        a_ref, b_ref, x_ref, dres_ref, g_ref = refs[:5]
        outs = refs[n_in:]
        dx_ref, dg_ref, acc_ref = outs[0], outs[-2], outs[-1]
        i, k = pl.program_id(0), pl.program_id(1)
        part = _dot(a_ref[...], b_ref[...], _NT)

        @pl.when(k == 0)
        def _():
            acc_ref[...] = part

        @pl.when(jnp.logical_and(k > 0, k < gk - 1))
        def _():
            acc_ref[...] += part

        @pl.when(k == gk - 1)
        def _():
            dhv = acc_ref[...] + part if gk > 1 else part
            xv = x_ref[...]
            r = lax.rsqrt(jnp.mean(xv * xv, axis=-1, keepdims=True) + RMS_EPS)
            xhat = xv * r
            dyg = dhv * g_ref[...]
            dx = dres_ref[...] + r * (dyg - xhat * jnp.mean(dyg * xhat, axis=-1, keepdims=True))
            dx_ref[...] = dx
            if with_bf16:
                outs[1][...] = dx.astype(BF16)
            gpart = jnp.sum(dhv * xhat, axis=0, keepdims=True)

            @pl.when(i == 0)
            def _():
                dg_ref[...] = gpart

            @pl.when(i > 0)
            def _():
                dg_ref[...] += gpart

    rows = lambda width_: pl.BlockSpec((tm, width_), lambda i, k: (i + i0, 0))
    if a_parts:
        a_spec = pl.BlockSpec((None, tm, tk), lambda i, k: (k, i + i0, 0))
    else:
        a_spec = pl.BlockSpec((tm, tk), lambda i, k: (i + i0, k))
    if b_stacked:
        b_spec = pl.BlockSpec((None, width, tk), lambda i, k: (k, 0, 0))
    else:
        b_spec = pl.BlockSpec((width, tk), lambda i, k: (0, k))
    operands = [a, b, x, dres, gain]
    in_specs = [a_spec, b_spec, rows(width), rows(width), pl.BlockSpec((1, width), lambda i, k: (0, 0))]
    for extra in (after, out_into):
        if extra is not None:
            operands.append(extra)
            in_specs.append(pl.BlockSpec(memory_space=pl.ANY))
    aliases = {} if out_into is None else {len(operands) - 1: 0}
    out_specs = [rows(width)] + ([rows(width)] if with_bf16 else []) + [pl.BlockSpec((1, width), lambda i, k: (0, 0))]
    out_shape = [jax.ShapeDtypeStruct((m, width), F32)] + ([jax.ShapeDtypeStruct((m, width), BF16)] if with_bf16 else [])
    out_shape.append(jax.ShapeDtypeStruct((1, width), F32))
    est = 4 * tm * tk + 4 * width * tk + 12 * tm * width * 4
    return pl.pallas_call(
        body, name=name, grid=(gm, gk), in_specs=in_specs, out_specs=out_specs, out_shape=out_shape,
        scratch_shapes=[pltpu.VMEM((tm, width), F32)], input_output_aliases=aliases,
        compiler_params=_params(("arbitrary", "arbitrary"), est),
    )(*operands)


def _gain_grad(x, dh, name):
    rows, width = x.shape

    def body(x_ref, dh_ref, dg_ref):
        xv = x_ref[...]
        r = lax.rsqrt(jnp.mean(xv * xv, axis=-1, keepdims=True) + RMS_EPS)
        dg_ref[...] = jnp.sum(dh_ref[...] * xv * r, axis=0, keepdims=True)

    return pl.pallas_call(
        body, name=name, grid=(1,), in_specs=[_row_spec(rows, width), _row_spec(rows, width)],
        out_specs=_bcast_spec(width), out_shape=jax.ShapeDtypeStruct((1, width), F32),
        compiler_params=_params(("arbitrary",), 6 * rows * width * 4),
    )(x, dh)


def _lb_table(logits, name):
    slots, width = logits.shape

    def body(l_ref, o_ref):
        lv = l_ref[...]
        mx = jnp.max(lv, axis=0, keepdims=True)
        e = jnp.exp(lv - mx)
        o_ref[...] = e[0:1, :] / jnp.sum(e, axis=0, keepdims=True)

    return pl.pallas_call(
        body, name=name, grid=(1,), in_specs=[pl.BlockSpec((slots, width), lambda i: (0, 0))],
        out_specs=_bcast_spec(width), out_shape=jax.ShapeDtypeStruct((1, width), F32),
    )(logits)


def _branch_merge_fwd(proj, outs, weights, tr):
    w = D_MODEL

    def body(ghg_ref, gda_ref, gmem_ref, ohg_ref, oda_ref, omem_ref, whg_ref, wda_ref, wmem_ref,
             m_ref, thg_ref, tda_ref, tmem_ref):
        acc = None
        for g_ref, o_ref, w_ref, t_ref in ((ghg_ref, ohg_ref, whg_ref, thg_ref), (gda_ref, oda_ref, wda_ref, tda_ref),
                                           (gmem_ref, omem_ref, wmem_ref, tmem_ref)):
            t = _dot(o_ref[...], w_ref[...], _NN)
            t_ref[...] = t.astype(BF16)
            term = _sigmoid(g_ref[...]) * t
            acc = term if acc is None else acc + term
        m_ref[...] = acc.astype(BF16)

    whole = lambda a: pl.BlockSpec(a.shape, lambda i: (0, 0))
    shape = jax.ShapeDtypeStruct((SEQ, w), BF16)
    return pl.pallas_call(
        body, name="branch_merge_fwd", grid=(SEQ // tr,),
        in_specs=[_row_spec(tr, w, 10), _row_spec(tr, w, 11), _row_spec(tr, w, 12)]
        + [_row_spec(tr, o.shape[1]) for o in outs] + [whole(wt) for wt in weights],
        out_specs=[_row_spec(tr, w)] * 4, out_shape=[shape] * 4,
        compiler_params=_params(("parallel",), 20 * tr * w * 4),
    )(proj, proj, proj, *outs, *weights)


def _dproj_buffer():
    return lax.empty((SEQ, IN_COLS), BF16)


def _gate_merge_bwd(proj, ts, weights, dx1b, wout, o_pre, hg_gain, o_da32, dproj, tr):
    w = D_MODEL
    steps = SEQ // tr
    gate_col0 = 10 * w
    hg_gate_col0 = CB_HG_G * LANE

    def body(ghg_ref, gda_ref, gmem_ref, hgg_ref, thg_ref, tda_ref, tmem_ref, whg_ref, wda_ref, wmem_ref, dx_ref,
             wout_ref, opre_ref, gain_ref, oda_ref, dproj_in,
             dthg_ref, dtda_ref, dtmem_ref, dopre_ref, doda_ref, domem_ref, dd_ref, dgain_ref, dproj_ref,
             stage, stage_hg, sems):
        del dproj_in
        i = pl.program_id(0)
        slot = i % 2
        rows = pl.ds(pl.multiple_of(i * tr, tr), tr)

        def slot_copies(s):
            return (pltpu.make_async_copy(stage.at[s], dproj_ref.at[rows, pl.ds(gate_col0, 3 * w)], sems.at[s]),
                    pltpu.make_async_copy(stage_hg.at[s], dproj_ref.at[rows, pl.ds(hg_gate_col0, w)], sems.at[2 + s]))

        @pl.when(i >= 2)
        def _():
            for cp in slot_copies(slot):
                cp.wait()

        dm = _dot(dx_ref[...], wout_ref[...], _NT)
        branches = ((ghg_ref, thg_ref, whg_ref, dthg_ref), (gda_ref, tda_ref, wda_ref, dtda_ref),
                    (gmem_ref, tmem_ref, wmem_ref, dtmem_ref))
        dos = []
        for b, (g_ref, t_ref, w_ref, dt_ref) in enumerate(branches):
            s = _sigmoid(g_ref[...])
            dt = (s * dm).astype(BF16)
            dt_ref[...] = dt
            dos.append(_dot(dt, w_ref[...], _NT))
            stage[slot, :, b * w:(b + 1) * w] = (dm * t_ref[...].astype(F32) * s * (1.0 - s)).astype(BF16)
        do_hg, do_da, do_mem = dos
        doda_ref[...] = do_da
        domem_ref[...] = do_mem

        gainv = gain_ref[...]
        part = jnp.zeros((1, HEAD_DIM), F32)
        for h in range(HG_HEADS):
            hs = slice(h * HEAD_DIM, (h + 1) * HEAD_DIM)
            o = opre_ref[:, hs]
            r = lax.rsqrt(jnp.mean(o * o, axis=-1, keepdims=True) + RMS_EPS)
            ohat = o * r
            g = hgg_ref[:, hs]
            sg = _sigmoid(g)
            silu = g * sg
            dout = do_hg[:, hs]
            stage_hg[slot, :, hs] = (dout * ohat * gainv * (sg + silu * (1.0 - sg))).astype(BF16)
            dy = dout * silu
            part = part + jnp.sum(dy * ohat, axis=0, keepdims=True)
            dn = dy * gainv
            dopre_ref[:, hs] = r * (dn - ohat * jnp.mean(dn * ohat, axis=-1, keepdims=True))
        for cp in slot_copies(slot):
            cp.start()

        prod = do_da * oda_ref[...]
        for h in range(DA_WIDTH // HEAD_DIM):
            hs = slice(h * HEAD_DIM, (h + 1) * HEAD_DIM)
            dd_ref[:, hs] = jnp.broadcast_to(jnp.sum(prod[:, hs], axis=-1, keepdims=True), (tr, HEAD_DIM))

        @pl.when(i == 0)
        def _():
            dgain_ref[...] = part

        @pl.when(i > 0)
        def _():
            dgain_ref[...] += part

        @pl.when(i == steps - 1)
        def _():
            for cp in slot_copies(1 - slot) + slot_copies(slot):
                cp.wait()

    assert steps >= 2
    whole = lambda a: pl.BlockSpec(a.shape, lambda i: (0, 0))
    widths = [wt.shape[0] for wt in weights]
    return pl.pallas_call(
        body, name="gate_merge_bwd", grid=(steps,),
        in_specs=[_row_spec(tr, w, 10), _row_spec(tr, w, 11), _row_spec(tr, w, 12), _row_spec(tr, w, hg_gate_col0 // w)]
        + [_row_spec(tr, w)] * 3 + [whole(wt) for wt in weights]
        + [_row_spec(tr, w), whole(wout), _row_spec(tr, w), _bcast_spec(HEAD_DIM), _row_spec(tr, DA_WIDTH), _any_spec()],
        out_specs=[_row_spec(tr, w)] * 4 + [_row_spec(tr, widths[1]), _row_spec(tr, widths[2]),
                                            _row_spec(tr, DA_WIDTH), _bcast_spec(HEAD_DIM), _any_spec()],
        out_shape=[jax.ShapeDtypeStruct((SEQ, w), BF16)] * 3 + [jax.ShapeDtypeStruct((SEQ, w), F32)]
        + [jax.ShapeDtypeStruct((SEQ, widths[1]), F32), jax.ShapeDtypeStruct((SEQ, widths[2]), F32),
           jax.ShapeDtypeStruct((SEQ, DA_WIDTH), F32), jax.ShapeDtypeStruct((1, HEAD_DIM), F32),
           jax.ShapeDtypeStruct((SEQ, IN_COLS), BF16)],
        scratch_shapes=[pltpu.VMEM((2, tr, 3 * w), BF16), pltpu.VMEM((2, tr, w), BF16), pltpu.SemaphoreType.DMA((4,))],
        input_output_aliases={15: 8},
        compiler_params=_params(("arbitrary",), 44 * tr * w * 4),
    )(proj, proj, proj, proj, *ts, *weights, dx1b, wout, o_pre, hg_gain, o_da32, dproj)


def _ffn_in_swiglu(h2, wfin, tm, tn):
    gm, gn = SEQ // tm, D_FF // tn

    def body(h_ref, wa_ref, wb_ref, a_ref, b_ref, act_ref, actt_ref):
        h = h_ref[...]
        a = _dot(h, wa_ref[...], _NN)
        b = _dot(h, wb_ref[...], _NN)
        act = a * _sigmoid(a) * b
        a_ref[...] = a.astype(BF16)
        b_ref[...] = b.astype(BF16)
        act_ref[...] = act.astype(BF16)
        actt_ref[...] = act.T.astype(BF16)

    tile = pl.BlockSpec((tm, tn), lambda j, i: (i, j))
    shape = jax.ShapeDtypeStruct((SEQ, D_FF), BF16)
    return pl.pallas_call(
        body, name="ffn_in_swiglu_fwd", grid=(gn, gm),
        in_specs=[pl.BlockSpec((tm, D_MODEL), lambda j, i: (i, 0)),
                  pl.BlockSpec((D_MODEL, tn), lambda j, i: (0, j)),
                  pl.BlockSpec((D_MODEL, tn), lambda j, i: (0, gn + j))],
        out_specs=[tile, tile, tile, pl.BlockSpec((tn, tm), lambda j, i: (j, i))],
        out_shape=[shape, shape, shape, jax.ShapeDtypeStruct((D_FF, SEQ), BF16)],
        compiler_params=_params(("parallel", "parallel"), 4 * tm * D_MODEL + 8 * D_MODEL * tn + 16 * tm * tn
                                + 6 * tm * tn * 4),
    )(h2, wfin, wfin)


def _ffn_out_bwd_swiglu(dyb, wfout, a, b, tm, tn):
    gm, gn = SEQ // tm, D_FF // tn

    def body(dy_ref, w_ref, a_ref, b_ref, o_ref):
        d = _dot(dy_ref[...], w_ref[...], _NT)
        av = a_ref[...].astype(F32)
        bv = b_ref[...].astype(F32)
        s = _sigmoid(av)
        silu = av * s
        o_ref[0] = (d * bv * (s + silu * (1.0 - s))).astype(BF16)
        o_ref[1] = (d * silu).astype(BF16)

    tile = pl.BlockSpec((tm, tn), lambda i, j: (i, j))
    return pl.pallas_call(
        body, name="ffn_out_bwd_swiglu", grid=(gm, gn),
        in_specs=[pl.BlockSpec((tm, D_MODEL), lambda i, j: (i, 0)), pl.BlockSpec((tn, D_MODEL), lambda i, j: (j, 0)),
                  tile, tile],
        out_specs=pl.BlockSpec((2, tm, tn), lambda i, j: (0, i, j)),
        out_shape=jax.ShapeDtypeStruct((2, SEQ, D_FF), BF16),
        compiler_params=_params(("parallel", "parallel"), 4 * tm * D_MODEL + 4 * tn * D_MODEL + 16 * tm * tn
                                + 6 * tm * tn * 4),
    )(dyb, wfout, a, b)


def _ffn_out_loss_head(x1, act, wfout, target, tr):
    w = D_MODEL

    def body(x_ref, a_ref, w_ref, t_ref, dy_ref, dyb_ref, loss_ref, acc_ref):
        err = x_ref[...] + _dot(a_ref[...], w_ref[...], _NN) - t_ref[...]
        dy = err * (1.0 / w)
        dy_ref[...] = dy
        dyb_ref[...] = dy.astype(BF16)
        part = jnp.sum(err * err, axis=0, keepdims=True)

        @pl.when(pl.program_id(0) == 0)
        def _():
            acc_ref[...] = part

        @pl.when(pl.program_id(0) > 0)
        def _():
            acc_ref[...] += part

        @pl.when(pl.program_id(0) == SEQ // tr - 1)
        def _():
            total = jnp.sum(acc_ref[...], axis=1, keepdims=True) * (0.5 / w)
            loss_ref[...] = jnp.broadcast_to(total, (1, LANE))

    return pl.pallas_call(
        body, name="ffn_out_loss_head", grid=(SEQ // tr,),
        in_specs=[_row_spec(tr, w), _row_spec(tr, D_FF), pl.BlockSpec((D_FF, w), lambda i: (0, 0)), _row_spec(tr, w)],
        out_specs=[_row_spec(tr, w), _row_spec(tr, w), _bcast_spec(LANE)],
        out_shape=[jax.ShapeDtypeStruct((SEQ, w), F32), jax.ShapeDtypeStruct((SEQ, w), BF16),
                   jax.ShapeDtypeStruct((1, LANE), F32)],
        scratch_shapes=[pltpu.VMEM((1, w), F32)],
        compiler_params=_params(("arbitrary",), 12 * tr * w * 4 + 4 * D_FF * w),
    )(x1, act, wfout, target)


GLA_ROWS = 256
GLA_CPB = GLA_ROWS // HG_CHUNK
GLA_NBLK = SEQ // GLA_ROWS
GLA_WAYS = 4
GLA_TRIPS = GLA_NBLK // GLA_WAYS
GLA_GRAD_WAYS = 4
GLA_GRAD_TRIPS = GLA_NBLK // GLA_GRAD_WAYS
GLA_NCK = SEQ // HG_CHUNK
GLA_INTER_WAYS = 8


def _dot_split(m, xv, dims, terms=3):
    dot = lambda t: lax.dot_general(m, t, dims, preferred_element_type=F32)
    hi = xv.astype(BF16)
    r1 = xv - hi.astype(F32)
    mid = r1.astype(BF16)
    if terms == 2:
        return dot(mid) + dot(hi)
    lo = (r1 - mid.astype(F32)).astype(BF16)
    return (dot(lo) + dot(mid)) + dot(hi)


def _gla_block(qc, fc, lbv, masks):
    mask, maskb, direction = masks
    sq = _sigmoid(qc)
    q = qc * sq * HG_SCALE
    sf = _sigmoid(fc)
    forget = lbv + (1.0 - lbv) * sf
    k = 1.0 - forget
    logf = jnp.log(forget)
    b = _dot_split(maskb, logf, _NN)
    ends = []
    for j in range(GLA_CPB):
        lo, hi = j * HG_CHUNK, (j + 1) * HG_CHUNK
        end = jnp.where(direction == 0, b[hi - 1:hi, :], b[lo:lo + 1, :])
        ends.append(jnp.broadcast_to(end, (HG_CHUNK, HEAD_DIM)))
    bt = jnp.concatenate(ends, axis=0)
    eb = jnp.exp(b)
    qt = q * eb
    kt = k * jnp.exp(-b)
    kh = k * jnp.exp(bt - b)
    a = jnp.where(mask, _dot(qt, kt, _NT), 0.0)
    return dict(sq=sq, sf=sf, forget=forget, k=k, b=b, bt=bt, eb=eb, qt=qt, kt=kt, kh=kh, a=a)


def _gla_masks(direction):
    row = lax.broadcasted_iota(jnp.int32, (GLA_ROWS, GLA_ROWS), 0)
    col = lax.broadcasted_iota(jnp.int32, (GLA_ROWS, GLA_ROWS), 1)
    same = (row // HG_CHUNK) == (col // HG_CHUNK)
    mask = jnp.logical_and(same, jnp.where(direction == 0, row - col, col - row) >= 0)
    return mask, jnp.where(mask, 1.0, 0.0).astype(BF16), direction


def _gla_chunk_rows(j):
    return slice(j * HG_CHUNK, (j + 1) * HG_CHUNK)


def _gla_block_rows(b):
    return pl.ds(pl.multiple_of(b * GLA_ROWS, GLA_ROWS), GLA_ROWS)


def _head_spec(col_block0):
    return pl.BlockSpec((SEQ, HEAD_DIM), lambda h, d: (0, col_block0 + h))


def _gla_fwd(proj, lb, gain):
    nck = GLA_NCK

    def body(q_ref, f_ref, v_ref, g_ref, lb_ref, gain_ref, ohg_ref, opre_ref, st_ref, qt_scr, dec_scr, cs_scr):
        d = pl.program_id(1)
        masks = _gla_masks(d)
        lbv = lb_ref[...]

        @pl.when(d == 0)
        def _():
            opre_ref[...] = jnp.zeros_like(opre_ref)

        def intra(s, carry):
            blocks = [s + w * GLA_TRIPS for w in range(GLA_WAYS)]
            rows = [_gla_block_rows(b) for b in blocks]
            loaded = [(q_ref[r, :], f_ref[r, :], v_ref[r, :], opre_ref[r, :]) for r in rows]
            results = []
            for qc, fc, v, o_prev in loaded:
                ck = _gla_block(qc, fc, lbv, masks)
                o = o_prev + _dot(ck["a"], v, _NN)
                cs = [_dot(v[_gla_chunk_rows(j), :], ck["kh"][_gla_chunk_rows(j), :], _TN) for j in range(GLA_CPB)]
                dec = [jnp.exp(ck["bt"][j * HG_CHUNK:j * HG_CHUNK + 8, :]) for j in range(GLA_CPB)]
                results.append((o, ck["qt"].astype(BF16), cs, dec))
            for b, r, (o, qt, cs, dec) in zip(blocks, rows, results):
                opre_ref[r, :] = o
                qt_scr[r, :] = qt
                for j in range(GLA_CPB):
                    cs_scr[b * GLA_CPB + j] = cs[j]
                    dec_scr[b * GLA_CPB + j] = dec[j]
            return carry

        lax.fori_loop(0, GLA_TRIPS, intra, 0)

        def scan(i, st):
            c = jnp.where(d == 0, i, nck - 1 - i)
            st_ref[c] = st
            return st * dec_scr[c][0:1, :] + cs_scr[c]

        lax.fori_loop(0, nck, scan, jnp.zeros((HEAD_DIM, HEAD_DIM), F32), unroll=4)

        def inter(t, carry):
            chunks = [t + w * (nck // GLA_INTER_WAYS) for w in range(GLA_INTER_WAYS)]
            rows = [pl.ds(pl.multiple_of(c * HG_CHUNK, HG_CHUNK), HG_CHUNK) for c in chunks]
            loaded = [(opre_ref[r, :], qt_scr[r, :], st_ref[c]) for r, c in zip(rows, chunks)]
            for r, o in zip(rows, [o_prev + _dot(qt, st, _NT) for o_prev, qt, st in loaded]):
                opre_ref[r, :] = o
            return carry

        lax.fori_loop(0, nck // GLA_INTER_WAYS, inter, 0)

        @pl.when(d == 1)
        def _():
            o = opre_ref[...]
            r = lax.rsqrt(jnp.mean(o * o, axis=-1, keepdims=True) + RMS_EPS)
            g = g_ref[...]
            ohg_ref[...] = (o * r * gain_ref[...] * (g * _sigmoid(g))).astype(BF16)

    blk = SEQ * HEAD_DIM * 4
    return pl.pallas_call(
        body, name="gla_fwd", grid=(HG_HEADS, 2),
        in_specs=[_head_spec(CB_HG_Q),
                  pl.BlockSpec((SEQ, HEAD_DIM), lambda h, d: (0, CB_F + 8 * d + h)),
                  _head_spec(CB_HG_I), _head_spec(CB_HG_G),
                  pl.BlockSpec((None, None, 1, HEAD_DIM), lambda h, d: (d, h, 0, 0)),
                  pl.BlockSpec((1, HEAD_DIM), lambda h, d: (0, 0))],
        out_specs=[_head_spec(0), _head_spec(0),
                   pl.BlockSpec((None, None, nck, HEAD_DIM, HEAD_DIM), lambda h, d: (h, d, 0, 0, 0)),
                   pl.BlockSpec((None, SEQ, HEAD_DIM), lambda h, d: (d, 0, h)),
                   pl.BlockSpec((None, None, nck, 8, HEAD_DIM), lambda h, d: (h, d, 0, 0, 0))],
        out_shape=[jax.ShapeDtypeStruct((SEQ, D_MODEL), BF16), jax.ShapeDtypeStruct((SEQ, D_MODEL), F32),
                   jax.ShapeDtypeStruct((HG_HEADS, 2, nck, HEAD_DIM, HEAD_DIM), F32),
                   jax.ShapeDtypeStruct((2, SEQ, D_MODEL), BF16),
                   jax.ShapeDtypeStruct((HG_HEADS, 2, nck, 8, HEAD_DIM), F32)],
        scratch_shapes=[pltpu.VMEM((nck, HEAD_DIM, HEAD_DIM), F32)],
        compiler_params=_params(("parallel", "arbitrary"), 8 * blk + 3 * blk + 2 * blk + 2 * blk + blk),
    )(proj, proj, proj, proj, lb, gain)


def _gla_bwd(proj, lb, states, qt_all, dec_all, do, dproj):
    nck = GLA_NCK

    def body(q_ref, f_ref, v_ref, lb_ref, do_ref, st_ref, qt_ref, dec_scr, dproj_in, dproj_ref, dlb_ref,
             dq_acc, dv_acc, dst_scr, cs_scr, df_out, dq_out, dv_out, sems):
        del dproj_in
        h, d = pl.program_id(0), pl.program_id(1)
        masks = _gla_masks(d)
        mask, maskb, _ = masks
        lbv = lb_ref[...]

        def column_copy(k, staging, col_block):
            cols = pl.ds(pl.multiple_of(col_block * LANE, LANE), LANE)
            return pltpu.make_async_copy(staging, dproj_ref.at[:, cols], sems.at[k])

        df_copy = column_copy(0, df_out, CB_F + 8 * d + h)
        dq_copy = column_copy(1, dq_out, CB_HG_Q + h)
        dv_copy = column_copy(2, dv_out, CB_HG_I + h)
        last = jnp.logical_and(h == HG_HEADS - 1, d == 1)

        def intra(s, carry):
            blocks = [s + w * GLA_TRIPS for w in range(GLA_WAYS)]
            loaded = [(qt_ref[r, :], do_ref[r, :]) for r in map(_gla_block_rows, blocks)]
            results = [[_dot(doc[_gla_chunk_rows(j), :], qt[_gla_chunk_rows(j), :], _TN) for j in range(GLA_CPB)]
                       for qt, doc in loaded]
            for b, cs in zip(blocks, results):
                for j in range(GLA_CPB):
                    cs_scr[b * GLA_CPB + j] = cs[j]
            return carry

        lax.fori_loop(0, GLA_TRIPS, intra, 0)

        def scan(i, dst):
            c = jnp.where(d == 0, nck - 1 - i, i)
            dst_scr[c] = dst
            return dst * dec_scr[c][0:1, :] + cs_scr[c]

        lax.fori_loop(0, nck, scan, jnp.zeros((HEAD_DIM, HEAD_DIM), F32), unroll=4)

        @pl.when(d == 0)
        def _():
            dq_acc[...] = jnp.zeros_like(dq_acc)
            dv_acc[...] = jnp.zeros_like(dv_acc)

        def block_grads(qc, fc, v, doc, states_in, dstates, decays):
            ck = _gla_block(qc, fc, lbv, masks)
            qt, kt, kh, a = ck["qt"], ck["kt"], ck["kh"], ck["a"]
            da = jnp.where(mask, _dot(doc, v, _NT), 0.0)
            dqt_i = _dot(da, kt, _NN)
            dkt = _dot(da, qt, _TN)
            dv_i = _dot(a, doc, _TN)
            dqt_p, dv_p, dkh_p, dbt_p = [], [], [], []
            for j in range(GLA_CPB):
                cr = _gla_chunk_rows(j)
                st_in, dst = states_in[j], dstates[j]
                dqt_p.append(dqt_i[cr, :] + _dot(doc[cr, :], st_in, _NN))
                dv_p.append(dv_i[cr, :] + _dot(kh[cr, :], dst, _NT))
                dkh_j = _dot(v[cr, :], dst, _NN)
                dkh_p.append(dkh_j)
                dbt_j = (decays[j][0:1, :] * jnp.sum(dst * st_in, axis=0, keepdims=True)
                         + jnp.sum(dkh_j * kh[cr, :], axis=0, keepdims=True))
                dbt_p.append(jnp.broadcast_to(dbt_j, (HG_CHUNK, HEAD_DIM)))
            dqt = jnp.concatenate(dqt_p, axis=0)
            dv = jnp.concatenate(dv_p, axis=0)
            dkh = jnp.concatenate(dkh_p, axis=0)
            dbt = jnp.concatenate(dbt_p, axis=0)
            db = dqt * qt - dkt * kt - dkh * kh
            dq = dqt * ck["eb"]
            dk = dkt * jnp.exp(-ck["b"]) + dkh * jnp.exp(ck["bt"] - ck["b"])
            dlogf = _dot_split(maskb, db, _TN, terms=2) + dbt
            dforget = dlogf / ck["forget"] - dk
            sf, sq = ck["sf"], ck["sq"]
            df = (dforget * (1.0 - lbv) * sf * (1.0 - sf)).astype(BF16)
            dqc = dq * HG_SCALE * (sq + qc * sq * (1.0 - sq))
            return df, dqc, dv, jnp.sum(dforget * (1.0 - sf), axis=0, keepdims=True)

        def grads(s, dlb):
            blocks = [s + w * GLA_GRAD_TRIPS for w in range(GLA_GRAD_WAYS)]
            rows = [_gla_block_rows(b) for b in blocks]
            loaded = []
            for b, r in zip(blocks, rows):
                chunks = [b * GLA_CPB + j for j in range(GLA_CPB)]
                loaded.append((q_ref[r, :], f_ref[r, :], v_ref[r, :], do_ref[r, :], [st_ref[c] for c in chunks],
                               [dst_scr[c] for c in chunks], [dec_scr[c] for c in chunks], dq_acc[r, :], dv_acc[r, :]))
            results = [block_grads(*t[:7]) + (t[7], t[8]) for t in loaded]
            for r, (df, dqc, dv, dlb_part, dq_prev, dv_prev) in zip(rows, results):
                df_out[r, :] = df
                dq_acc[r, :] = dq_prev + dqc
                dv_acc[r, :] = dv_prev + dv
                dlb = dlb + dlb_part
            return dlb

        @pl.when(jnp.logical_or(h > 0, d > 0))
        def _():
            df_copy.wait()

        dlb_ref[...] = lax.fori_loop(0, GLA_GRAD_TRIPS, grads, jnp.zeros((1, HEAD_DIM), F32))
        df_copy.start()

        @pl.when(d == 1)
        def _():
            @pl.when(h > 0)
            def _():
                dq_copy.wait()
                dv_copy.wait()

            dq_out[...] = dq_acc[...].astype(BF16)
            dv_out[...] = dv_acc[...].astype(BF16)
            dq_copy.start()
            dv_copy.start()

        @pl.when(last)
        def _():
            df_copy.wait()
            dq_copy.wait()
            dv_copy.wait()

    blk = SEQ * HEAD_DIM * 4
    state_bytes = nck * HEAD_DIM * HEAD_DIM * 4
    dproj, dlb = pl.pallas_call(
        body, name="gla_bwd", grid=(HG_HEADS, 2),
        in_specs=[_head_spec(CB_HG_Q),
                  pl.BlockSpec((SEQ, HEAD_DIM), lambda h, d: (0, CB_F + 8 * d + h)),
                  _head_spec(CB_HG_I),
                  pl.BlockSpec((None, None, 1, HEAD_DIM), lambda h, d: (d, h, 0, 0)),
                  _head_spec(0),
                  pl.BlockSpec((None, None, nck, HEAD_DIM, HEAD_DIM), lambda h, d: (h, d, 0, 0, 0)),
                  pl.BlockSpec((None, SEQ, HEAD_DIM), lambda h, d: (d, 0, h)),
                  pl.BlockSpec((None, None, nck, 8, HEAD_DIM), lambda h, d: (h, d, 0, 0, 0)),
                  _any_spec()],
        out_specs=[_any_spec(), pl.BlockSpec((None, None, 1, HEAD_DIM), lambda h, d: (d, h, 0, 0))],
        out_shape=[jax.ShapeDtypeStruct((SEQ, IN_COLS), BF16), jax.ShapeDtypeStruct((2, HG_HEADS, 1, HEAD_DIM), F32)],
        scratch_shapes=[pltpu.VMEM((SEQ, HEAD_DIM), F32)] * 2
        + [pltpu.VMEM((nck, HEAD_DIM, HEAD_DIM), F32)] * 2
        + [pltpu.VMEM((SEQ, HEAD_DIM), BF16)] * 3 + [pltpu.SemaphoreType.DMA((3,))],
        input_output_aliases={8: 0},
        compiler_params=_params(("arbitrary", "arbitrary"), 8 * blk + 4 * state_bytes + 3 * blk + 3 * blk),
    )(proj, proj, proj, lb, do, states, qt_all, dec_all, dproj)
    return dproj, dlb


def _da_residue_rows(r, n0, size, d):
    if d == 1:
        return pl.ds(pl.multiple_of(n0, 8), size)
    return pl.ds(r + n0 * d, size, stride=d)


def _da_residue_ways(d, nqb):
    return min(d, 4) if nqb <= 2 else 1


def _da_rmsnorm(x, gain):
    r = lax.rsqrt(jnp.mean(x * x, axis=-1, keepdims=True) + RMS_EPS)
    return x * r, r, x * r * gain


def _da_scores(qn_scr, kn_scr, slope, i, ld):
    w0 = jnp.clip(i * DA_QB - DA_RADIUS, 0, ld - DA_WIN)
    w0 = pl.multiple_of(w0, DA_RADIUS)
    qrows = pl.ds(pl.multiple_of(i * DA_QB, DA_QB), DA_QB)
    win = pl.ds(w0, DA_WIN)
    qb = qn_scr[qrows, :]
    kw = kn_scr[win, :]
    s = _dot(qb, kw, _NT) * DA_SCALE
    qpos = i * DA_QB + lax.broadcasted_iota(jnp.int32, (DA_QB, DA_WIN), 0)
    kpos = w0 + lax.broadcasted_iota(jnp.int32, (DA_QB, DA_WIN), 1)
    arel = jnp.abs(kpos - qpos)
    s = s - slope * arel.astype(F32)
    s = jnp.where(arel <= DA_RADIUS, s, NEG_INF)
    return s, qb, kw, qrows, win


def _da_slopes(group):
    d = DA_DILATIONS[group]
    sl = _alibi_slopes(DA_HEADS)[4 * group:4 * group + 4] * d
    return jnp.asarray(np.broadcast_to(sl[:, None, None], (4, 1, LANE)).copy())


def _da_fwd(proj, gq, gk, group, others=()):
    d = DA_DILATIONS[group]
    ld = SEQ // d
    nqb = ld // DA_QB
    n_other = 2 * len(others)

    ways = min(DA_FWD_WAYS, nqb)
    rways = _da_residue_ways(d, nqb)

    def body(q_ref, k_ref, v_ref, gq_ref, gk_ref, sl_ref, *refs):
        other_refs, refs = refs[:n_other], refs[n_other:]
        if others:
            ob_ref, of_ref, lsej_ref, qn_scr, kn_scr, v_scr, o_ref, lse_ref = refs
        else:
            o_ref, lse_ref, qn_scr, kn_scr, v_scr = refs
        slope = sl_ref[:, 0:1]

        def residues(t, carry):
            rs = [t * rways + u for u in range(rways)]
            for u, r in enumerate(rs):
                sel = _da_residue_rows(r, 0, ld, d)
                qn_scr[u] = _da_rmsnorm(q_ref[sel, :], gq_ref[...])[2].astype(BF16)
                kn_scr[u] = _da_rmsnorm(k_ref[sel, :], gk_ref[...])[2].astype(BF16)
                v_scr[u] = v_ref[sel, :].astype(BF16)

            def block(u, i):
                s, _, _, _, win = _da_scores(qn_scr.at[u], kn_scr.at[u], slope, i, ld)
                m = jnp.max(s, axis=-1, keepdims=True)
                p = jnp.exp(s - m)
                l = jnp.sum(p, axis=-1, keepdims=True)
                return _dot(p, v_scr[u, win, :], _NN) / l, jnp.broadcast_to(m + jnp.log(l), (DA_QB, HEAD_DIM))

            def step(i, c2):
                todo = [(u, r, i + w * (nqb // ways)) for u, r in enumerate(rs) for w in range(ways)]
                for (u, r, b), (o, lse) in zip(todo, [block(u, b) for u, _, b in todo]):
                    out = _da_residue_rows(r, b * DA_QB, DA_QB, d)
                    o_ref[out, :] = o
                    lse_ref[out, :] = lse
                return c2

            return lax.fori_loop(0, nqb // ways, step, carry)

        lax.fori_loop(0, d // rways, residues, 0)

        if others:
            pieces = 8
            rows = SEQ // pieces

            def merge(t, carry):
                r = pl.ds(pl.multiple_of(t * rows, rows), rows)
                outs = [ref[r, :] for ref in other_refs[0::2]] + [o_ref[r, :]]
                lses = [ref[r, :] for ref in other_refs[1::2]] + [lse_ref[r, :]]
                m = lses[0]
                for l in lses[1:]:
                    m = jnp.maximum(m, l)
                es = [jnp.exp(l - m) for l in lses]
                tot = sum(es[1:], es[0])
                o = sum((e * v for e, v in zip(es[1:], outs[1:])), es[0] * outs[0]) / tot
                of_ref[r, :] = o
                ob_ref[r, :] = o.astype(BF16)
                lsej_ref[r, :] = m + jnp.log(tot)
                return carry

            lax.fori_loop(0, pieces, merge, 0)

    seq_spec = lambda base: pl.BlockSpec((SEQ, HEAD_DIM), lambda h: (0, base + 4 * group + h))
    out_spec = pl.BlockSpec((SEQ, HEAD_DIM), lambda h: (0, h))
    gain_spec = pl.BlockSpec((1, HEAD_DIM), lambda h: (0, 0))
    blk = SEQ * HEAD_DIM * 4
    fshape = jax.ShapeDtypeStruct((SEQ, DA_WIDTH), F32)
    scratch = [pltpu.VMEM((rways, ld, HEAD_DIM), BF16)] * 3
    if others:
        out_shape = [jax.ShapeDtypeStruct((SEQ, DA_WIDTH), BF16), fshape, fshape]
        scratch += [pltpu.VMEM((SEQ, HEAD_DIM), F32)] * 2
    else:
        out_shape = [fshape, fshape]
    return pl.pallas_call(
        body, name=f"da_fwd_g{group}", grid=(DA_HEADS_PER_GROUP,),
        in_specs=[seq_spec(CB_DA_Q), seq_spec(CB_DA_K), seq_spec(CB_DA_V), gain_spec, gain_spec,
                  pl.BlockSpec((None, 1, LANE), lambda h: (h, 0, 0))] + [out_spec] * n_other,
        out_specs=[out_spec] * len(out_shape), out_shape=out_shape, scratch_shapes=scratch,
        compiler_params=_params(("parallel",), 10 * blk + 2 * blk + (n_other + 3) * blk),
    )(proj, proj, proj, gq, gk, _da_slopes(group), *[a for pair in others for a in pair])


def _da_bwd(proj, gq, gk, do, lse, dd, dproj, group):
    d = DA_DILATIONS[group]
    ld = SEQ // d
    nqb = ld // DA_QB
    ways = min(DA_WAYS, nqb)
    rways = _da_residue_ways(d, nqb)

    def body(q_ref, k_ref, v_ref, gq_ref, gk_ref, sl_ref, do_ref, lse_ref, dd_ref, dproj_in,
             dproj_ref, dgq_ref, dgk_ref,
             qn_scr, kn_scr, v_scr, dqn_scr, dkn_scr, dvr_scr, rq_scr, rk_scr, dq_scr, dk_scr, dv_scr,
             dq_out, dk_out, dv_out, sems):
        del dproj_in
        head = pl.program_id(0)
        gqv, gkv = gq_ref[...], gk_ref[...]
        slope = sl_ref[:, 0:1]

        copies = []
        for k, (staging, base) in enumerate(((dq_out, CB_DA_Q), (dk_out, CB_DA_K), (dv_out, CB_DA_V))):
            cols = pl.ds(pl.multiple_of((base + 4 * group + head) * LANE, LANE), LANE)
            copies.append(pltpu.make_async_copy(staging, dproj_ref.at[:, cols], sems.at[k]))

        def residues(t, carry):
            rs = [t * rways + u for u in range(rways)]
            sels = [_da_residue_rows(r, 0, ld, d) for r in rs]
            for u, sel in enumerate(sels):
                for x_ref, gv, n_scr, r_scr in ((q_ref, gqv, qn_scr, rq_scr), (k_ref, gkv, kn_scr, rk_scr)):
                    _, rstd, normed = _da_rmsnorm(x_ref[sel, :], gv)
                    n_scr[u] = normed.astype(BF16)
                    r_scr[u] = jnp.broadcast_to(rstd, (ld, HEAD_DIM))
                v_scr[u] = v_ref[sel, :].astype(BF16)
            dkn_scr[...] = jnp.zeros_like(dkn_scr)
            dvr_scr[...] = jnp.zeros_like(dvr_scr)

            def block(u, r, i):
                s, qb, kw, qrows, win = _da_scores(qn_scr.at[u], kn_scr.at[u], slope, i, ld)
                src = _da_residue_rows(r, i * DA_QB, DA_QB, d)
                p = jnp.exp(s - lse_ref[src, :][:, 0:1])
                dob = do_ref[src, :]
                dp = _dot(dob, v_scr[u, win, :], _NT)
                ds = p * (dp - dd_ref[src, :][:, 0:1]) * DA_SCALE
                return u, qrows, win, _dot(p, dob, _TN), _dot(ds, kw, _NN), _dot(ds, qb, _TN)

            def step(i, c2):
                todo = [(u, r, i + w * (nqb // ways)) for u, r in enumerate(rs) for w in range(ways)]
                for u, qrows, win, dv, dqn, dkn in [block(*t) for t in todo]:
                    dvr_scr[u, win, :] += dv
                    dqn_scr[u, qrows, :] = dqn
                    dkn_scr[u, win, :] += dkn
                return c2

            lax.fori_loop(0, nqb // ways, step, 0)

            pq, pk = carry
            for u, sel in enumerate(sels):
                parts = []
                for x_ref, gv, dn_scr, dx_scr, r_scr in ((q_ref, gqv, dqn_scr, dq_scr, rq_scr),
                                                         (k_ref, gkv, dkn_scr, dk_scr, rk_scr)):
                    rstd = r_scr[u]
                    hat = x_ref[sel, :] * rstd
                    dn = dn_scr[u]
                    dyg = dn * gv
                    dx_scr[sel, :] = rstd * (dyg - hat * jnp.mean(dyg * hat, axis=-1, keepdims=True))
                    parts.append(jnp.sum(dn * hat, axis=0, keepdims=True))
                dv_scr[sel, :] = dvr_scr[u]
                pq, pk = pq + parts[0], pk + parts[1]
            return pq, pk

        zero = jnp.zeros((1, HEAD_DIM), F32)
        pq, pk = lax.fori_loop(0, d // rways, residues, (zero, zero))

        @pl.when(pl.program_id(0) == 0)
        def _():
            dgq_ref[...] = pq
            dgk_ref[...] = pk

        @pl.when(pl.program_id(0) > 0)
        def _():
            dgq_ref[...] += pq
            dgk_ref[...] += pk

        @pl.when(head > 0)
        def _():
            for cp in copies:
                cp.wait()

        dq_out[...] = dq_scr[...].astype(BF16)
        dk_out[...] = dk_scr[...].astype(BF16)
        dv_out[...] = dv_scr[...].astype(BF16)
        for cp in copies:
            cp.start()

        @pl.when(head == DA_HEADS_PER_GROUP - 1)
        def _():
            for cp in copies:
                cp.wait()

    seq_spec = lambda base: pl.BlockSpec((SEQ, HEAD_DIM), lambda h: (0, base + 4 * group + h))
    out_spec = pl.BlockSpec((SEQ, HEAD_DIM), lambda h: (0, h))
    gain_spec = pl.BlockSpec((1, HEAD_DIM), lambda h: (0, 0))
    gshape = jax.ShapeDtypeStruct((1, HEAD_DIM), F32)
    blk = SEQ * HEAD_DIM * 4
    return pl.pallas_call(
        body, name=f"da_bwd_g{group}", grid=(DA_HEADS_PER_GROUP,),
        in_specs=[seq_spec(CB_DA_Q), seq_spec(CB_DA_K), seq_spec(CB_DA_V), gain_spec, gain_spec,
                  pl.BlockSpec((None, 1, LANE), lambda h: (h, 0, 0)), out_spec, out_spec, out_spec, _any_spec()],
        out_specs=[_any_spec(), gain_spec, gain_spec],
        out_shape=[jax.ShapeDtypeStruct((SEQ, IN_COLS), BF16), gshape, gshape],
        scratch_shapes=[pltpu.VMEM((rways, ld, HEAD_DIM), BF16)] * 3 + [pltpu.VMEM((rways, ld, HEAD_DIM), F32)] * 5
        + [pltpu.VMEM((SEQ, HEAD_DIM), F32)] * 3 + [pltpu.VMEM((SEQ, HEAD_DIM), BF16)] * 3
        + [pltpu.SemaphoreType.DMA((3,))],
        input_output_aliases={9: 0},
        compiler_params=_params(("arbitrary",), 12 * blk + 3 * blk + 3 * blk + 3 * blk + 2 * blk),
    )(proj, proj, proj, gq, gk, _da_slopes(group), do, lse, dd, dproj)


MEM_WAYS = 4


def _mem_probs(q, kn, gq):
    qhat, rq, qn = _da_rmsnorm(q, gq)
    s = _dot(qn, kn, _NT) * MEM_SCALE
    m = jnp.max(s, axis=-1, keepdims=True)
    e = jnp.exp(s - m)
    p = e / jnp.sum(e, axis=-1, keepdims=True)
    return p, qhat, rq, qn


def _mem_pieces(tq):
    rows = tq // MEM_WAYS
    return [slice(w * rows, (w + 1) * rows) for w in range(MEM_WAYS)]


def _mem_fwd(proj, kv, gq, gk, tq):
    def body(q_ref, k_ref, v_ref, gq_ref, gk_ref, o_ref):
        kn = _da_rmsnorm(k_ref[...], gk_ref[...])[2]
        v, gqv = v_ref[...], gq_ref[...]
        pieces = _mem_pieces(tq)
        outs = [_dot(_mem_probs(q_ref[rs, :], kn, gqv)[0], v, _NN) for rs in pieces]
        for rs, o in zip(pieces, outs):
            o_ref[rs, :] = o.astype(BF16)

    gain_spec = pl.BlockSpec((1, HEAD_DIM), lambda h, i: (0, 0))
    return pl.pallas_call(
        body, name="mem_fwd", grid=(MEM_HEADS, SEQ // tq),
        in_specs=[pl.BlockSpec((tq, HEAD_DIM), lambda h, i: (i, CB_MEM_Q + h)),
                  pl.BlockSpec((N_MEM, HEAD_DIM), lambda h, i: (0, h)),
                  pl.BlockSpec((N_MEM, HEAD_DIM), lambda h, i: (0, MEM_HEADS + h)), gain_spec, gain_spec],
        out_specs=pl.BlockSpec((tq, HEAD_DIM), lambda h, i: (i, h)),
        out_shape=jax.ShapeDtypeStruct((SEQ, MEM_WIDTH), BF16),
        compiler_params=_params(("parallel", "parallel"), 16 * tq * N_MEM * 4),
    )(proj, kv, kv, gq, gk)


def _mem_bwd(proj, kv, gq, gk, do, dproj, tq):
    nq = SEQ // tq

    def body(q_ref, k_ref, v_ref, gq_ref, gk_ref, do_ref, dproj_in, dq_ref, dk_ref, dv_ref, dgq_ref, dgk_ref, dkn_scr):
        del dproj_in
        h, i = pl.program_id(0), pl.program_id(1)
        gqv, gkv = gq_ref[...], gk_ref[...]
        khat, rk, kn = _da_rmsnorm(k_ref[...], gkv)
        v = v_ref[...]

        def piece(rs):
            p, qhat, rq, qn = _mem_probs(q_ref[rs, :], kn, gqv)
            dob = do_ref[rs, :]
            dp = _dot(dob, v, _NT)
            ds = p * (dp - jnp.sum(p * dp, axis=-1, keepdims=True)) * MEM_SCALE
            dqn = _dot(ds, kn, _NN)
            dyg = dqn * gqv
            dq = (rq * (dyg - qhat * jnp.mean(dyg * qhat, axis=-1, keepdims=True))).astype(BF16)
            return dq, _dot(p, dob, _TN), _dot(ds, qn, _TN), jnp.sum(dqn * qhat, axis=0, keepdims=True)

        pieces = _mem_pieces(tq)
        results = [piece(rs) for rs in pieces]
        for rs, res in zip(pieces, results):
            dq_ref[rs, :] = res[0]
        dvp = sum((res[1] for res in results[1:]), results[0][1])
        dknp = sum((res[2] for res in results[1:]), results[0][2])
        dgq_part = sum((res[3] for res in results[1:]), results[0][3])
        first = jnp.logical_and(h == 0, i == 0)

        @pl.when(first)
        def _():
            dgq_ref[...] = dgq_part

        @pl.when(jnp.logical_not(first))
        def _():
            dgq_ref[...] += dgq_part

        @pl.when(i == 0)
        def _():
            dv_ref[...] = dvp
            dkn_scr[...] = dknp

        @pl.when(i > 0)
        def _():
            dv_ref[...] += dvp
            dkn_scr[...] += dknp

        @pl.when(i == nq - 1)
        def _():
            dkn = dkn_scr[...]
            dkg = dkn * gkv
            dk_ref[...] = rk * (dkg - khat * jnp.mean(dkg * khat, axis=-1, keepdims=True))
            dgk_part = jnp.sum(dkn * khat, axis=0, keepdims=True)

            @pl.when(h == 0)
            def _():
                dgk_ref[...] = dgk_part

            @pl.when(h > 0)
            def _():
                dgk_ref[...] += dgk_part

    gain_spec = pl.BlockSpec((1, HEAD_DIM), lambda h, i: (0, 0))
    kvout = pl.BlockSpec((N_MEM, HEAD_DIM), lambda h, i: (0, h))
    return pl.pallas_call(
        body, name="mem_bwd", grid=(MEM_HEADS, nq),
        in_specs=[pl.BlockSpec((tq, HEAD_DIM), lambda h, i: (i, CB_MEM_Q + h)),
                  pl.BlockSpec((N_MEM, HEAD_DIM), lambda h, i: (0, h)),
                  pl.BlockSpec((N_MEM, HEAD_DIM), lambda h, i: (0, MEM_HEADS + h)), gain_spec, gain_spec,
                  pl.BlockSpec((tq, HEAD_DIM), lambda h, i: (i, h)), _any_spec()],
        out_specs=[pl.BlockSpec((tq, HEAD_DIM), lambda h, i: (i, CB_MEM_Q + h)), kvout, kvout, gain_spec, gain_spec],
        out_shape=[jax.ShapeDtypeStruct((SEQ, IN_COLS), BF16), jax.ShapeDtypeStruct((N_MEM, MEM_WIDTH), F32),
                   jax.ShapeDtypeStruct((N_MEM, MEM_WIDTH), F32), jax.ShapeDtypeStruct((1, HEAD_DIM), F32),
                   jax.ShapeDtypeStruct((1, HEAD_DIM), F32)],
        scratch_shapes=[pltpu.VMEM((N_MEM, HEAD_DIM), F32)], input_output_aliases={6: 0},
        compiler_params=_params(("arbitrary", "arbitrary"), 24 * tq * N_MEM * 4),
    )(proj, kv, kv, gq, gk, do, dproj)


def _mesh_position():
    return lax.axis_index("x"), lax.axis_index("y"), lax.axis_index("c")


def _any_spec():
    return pl.BlockSpec(memory_space=pl.ANY)


def _all_gather(shards):
    n = len(shards)

    def body(*refs):
        ins, outs = refs[:n], refs[n:2 * n]
        send_sems, recv_sems, local_sems = refs[2 * n:]
        x, y, c = _mesh_position()
        me, sibling = (x, y, c), (x, y, 1 - c)
        first = (jnp.where(c == 0, 1 - x, x), jnp.where(c == 0, y, 1 - y), c)
        second = (jnp.where(c == 0, x, 1 - x), jnp.where(c == 0, 1 - y, y), c)
        diagonal = (1 - x, 1 - y, c)

        def copy(w, k, block, to, src=None):
            px, py, pc = block
            rows = outs[w].at[4 * px + 2 * py + pc]
            return pltpu.make_async_remote_copy(
                src_ref=rows if src is None else src, dst_ref=rows,
                send_sem=send_sems.at[7 * w + k], recv_sem=recv_sems.at[7 * w + k],
                device_id=to, device_id_type=MESH)

        started = []
        for w in range(n):
            mine = pltpu.make_async_copy(ins[w], outs[w].at[4 * x + 2 * y + c], local_sems.at[w])
            mine.start()
            started.append(mine)
        sends = []
        for w in range(n):
            own = [copy(w, 0, me, sibling, src=ins[w]), copy(w, 1, me, first, src=ins[w]),
                   copy(w, 2, me, second, src=ins[w])]
            for cp in own:
                cp.start()
            sends += own
        for w in range(n):
            copy(w, 1, first, me).wait_recv()
            follow = [copy(w, 3, first, second), copy(w, 4, first, sibling)]
            for cp in follow:
                cp.start()
            copy(w, 2, second, me).wait_recv()
            follow.append(copy(w, 5, second, sibling))
            follow[-1].start()
            sends += follow
        for w in range(n):
            copy(w, 3, diagonal, me).wait_recv()
            passed = copy(w, 6, diagonal, sibling)
            passed.start()
            sends.append(passed)
        for w in range(n):
            for k in (0, 4, 5, 6):
                copy(w, k, sibling, me).wait_recv()
        for cp in sends:
            cp.wait_send()
        for mine in started:
            mine.wait()

    return pl.pallas_call(
        body, name="weights_all_gather",
        in_specs=[_any_spec()] * n, out_specs=[_any_spec()] * n,
        out_shape=[jax.ShapeDtypeStruct((N_DEV,) + s.shape, s.dtype) for s in shards],
        scratch_shapes=[pltpu.SemaphoreType.DMA((7 * n,)), pltpu.SemaphoreType.DMA((7 * n,)),
                        pltpu.SemaphoreType.DMA((n,))],
    )(*shards)


def _chip_of(j, x, y):
    return (1 - x if j & 1 else x, 1 - y if j & 2 else y)


_HBM_SPEC = pl.BlockSpec(memory_space=pltpu.HBM)
_SEM_SPEC = pl.BlockSpec(memory_space=pltpu.SEMAPHORE)
_DATAFLOW_EFFECT = pltpu.SideEffectType.DATAFLOW_SIDE_EFFECTING
TOKEN_SHAPE = (8, D_MODEL)


def _copies_start(name, arrays, n_copies, plan):
    n = len(arrays)

    def body(*refs):
        send_sems, recv_sems, token = refs[n], refs[n + 1], refs[2 * n + 2]
        copies = plan(refs[:n])
        assert len(copies) == n_copies
        for k, (src, dst, dev) in enumerate(copies):
            pltpu.make_async_remote_copy(src_ref=src, dst_ref=dst, send_sem=send_sems.at[k], recv_sem=recv_sems.at[k],
                                         device_id=dev, device_id_type=MESH).start()
        token[...] = jnp.zeros_like(token)

    outs = pl.pallas_call(
        body, name=name,
        out_shape=(pltpu.SemaphoreType.DMA((n_copies,)), pltpu.SemaphoreType.DMA((n_copies,)),
                   *[pltpu.HBM(a.shape, a.dtype) for a in arrays], jax.ShapeDtypeStruct(TOKEN_SHAPE, F32)),
        in_specs=[_HBM_SPEC] * n,
        out_specs=(_SEM_SPEC, _SEM_SPEC, *[_HBM_SPEC] * n, pl.BlockSpec(memory_space=pltpu.VMEM)),
        input_output_aliases={i: i + 2 for i in range(n)},
        compiler_params=pltpu.CompilerParams(has_side_effects=_DATAFLOW_EFFECT),
    )(*[pltpu.with_memory_space_constraint(a, pltpu.HBM) for a in arrays])
    return outs[0], outs[1], list(outs[2:2 + n]), outs[2 + n]


def _copies_wait(name, send_sems, recv_sems, arrays, n_copies, plan, after):
    n = len(arrays)
    after = list(after) if isinstance(after, (list, tuple)) else [after]

    def body(*refs):
        send_ref, recv_ref = refs[n], refs[n + 1]
        copies = plan(refs[:n])
        assert len(copies) == n_copies
        for k, (src, dst, dev) in enumerate(copies):
            cp = pltpu.make_async_remote_copy(src_ref=src, dst_ref=dst, send_sem=send_ref.at[k], recv_sem=recv_ref.at[k],
                                              device_id=dev, device_id_type=MESH)
            cp.wait_send()
            cp.wait_recv()

    outs = pl.pallas_call(
        body, name=name, out_shape=tuple(pltpu.HBM(a.shape, a.dtype) for a in arrays),
        in_specs=[_HBM_SPEC] * n + [_SEM_SPEC, _SEM_SPEC] + [pl.BlockSpec(memory_space=pl.ANY)] * len(after),
        out_specs=tuple([_HBM_SPEC] * n), input_output_aliases={i: i for i in range(n)},
        compiler_params=pltpu.CompilerParams(has_side_effects=_DATAFLOW_EFFECT),
    )(*arrays, send_sems, recv_sems, *after)
    return list(outs)


def _after(small, token):
    return small if token is None else small + token[0:1, :small.shape[-1]]


def _gather_plan_out(n):
    def plan(refs):
        x, y, c = _mesh_position()
        me = 4 * x + 2 * y + c
        copies = []
        for w in range(n):
            land = refs[n + w].at[me]
            copies.append((refs[w], land, (x, y, 1 - c)))
            for j in range(1, 4):
                copies.append((refs[w], land, (*_chip_of(j, x, y), c)))
        return copies
    return plan


def _gather_plan_pass(n):
    def plan(refs):
        x, y, c = _mesh_position()
        copies = []
        for w in range(n):
            for j in range(1, 4):
                px, py = _chip_of(j, x, y)
                rows = refs[w].at[4 * px + 2 * py + c]
                copies.append((rows, rows, (x, y, 1 - c)))
        return copies
    return plan


def _reduce_plan_sibling(n):
    def plan(refs):
        x, y, c = _mesh_position()
        copies = []
        for w in range(n):
            for j in range(4):
                px, py = _chip_of(j, x, y)
                copies.append((refs[w].at[4 * px + 2 * py + (1 - c)], refs[n + w].at[j], (x, y, 1 - c)))
        return copies
    return plan


def _reduce_plan_chips(n):
    def plan(refs):
        x, y, c = _mesh_position()
        copies = []
        for w in range(n):
            for j in range(1, 4):
                copies.append((refs[w].at[j - 1], refs[n + w].at[j - 1], (*_chip_of(j, x, y), c)))
        return copies
    return plan


def _chip_partials(grad, recv, name, tr):
    _, rows, width = grad.shape

    def body(g_ref, r_ref, own_ref, other_ref):
        x, y, c = _mesh_position()
        for j in range(4):
            px, py = _chip_of(j, x, y)
            total = g_ref[4 * px + 2 * py + c].astype(F32) + r_ref[j].astype(F32)
            if j == 0:
                own_ref[...] = total
            else:
                other_ref[j - 1] = total.astype(BF16)

    return pl.pallas_call(
        body, name=name, grid=(rows // tr,),
        in_specs=[pl.BlockSpec((N_DEV, tr, width), lambda i: (0, i, 0)),
                  pl.BlockSpec((4, tr, width), lambda i: (0, i, 0))],
        out_specs=[pl.BlockSpec((tr, width), lambda i: (i, 0)), pl.BlockSpec((3, tr, width), lambda i: (0, i, 0))],
        out_shape=[jax.ShapeDtypeStruct((rows, width), F32), jax.ShapeDtypeStruct((3, rows, width), BF16)],
        compiler_params=_params(("parallel",), 2 * 20 * tr * width * 2 + 8 * tr * width * 4),
    )(grad, recv)


def _adamw_math(w, g, m, v):
    m = ADAM_B1 * m + (1.0 - ADAM_B1) * g
    v = ADAM_B2 * v + (1.0 - ADAM_B2) * (g * g)
    m_hat = m / (1.0 - ADAM_B1 ** ADAM_STEP)
    v_hat = v / (1.0 - ADAM_B2 ** ADAM_STEP)
    delta = -ADAM_LR * (m_hat / (jnp.sqrt(v_hat) + ADAM_EPS) + ADAM_WD * w)
    return delta, m, v


def _adamw_shard(own, recv, w, m, v, name, tr):
    rows, width = own.shape

    def body(own_ref, r_ref, w_ref, m_ref, v_ref, g_ref, d_ref, nm_ref, nv_ref):
        g = own_ref[...]
        for j in range(3):
            g = g + r_ref[j].astype(F32)
        g_ref[...] = g
        d_ref[...], nm_ref[...], nv_ref[...] = _adamw_math(w_ref[...], g, m_ref[...], v_ref[...])

    spec = pl.BlockSpec((tr, width), lambda i: (i, 0))
    shape = jax.ShapeDtypeStruct((rows, width), F32)
    return pl.pallas_call(
        body, name=name, grid=(rows // tr,),
        in_specs=[spec, pl.BlockSpec((3, tr, width), lambda i: (0, i, 0)), spec, spec, spec],
        out_specs=[spec] * 4, out_shape=[shape] * 4,
        compiler_params=_params(("parallel",), 22 * tr * width * 4),
    )(own, recv, w, m, v)


def _small_all_reduce_adamw(gpart, lbpack, wpack, mpack, vpack, after):
    def body(gp_ref, lb_ref, w_ref, m_ref, v_ref, after_ref, g_ref, d_ref, nm_ref, nv_ref, gath_ref, send_sems, recv_sems):
        del after_ref
        x, y, c = _mesh_position()
        me = 4 * x + 2 * y + c
        gath_ref[me] = gp_ref[...]
        copies = []
        for j in range(1, N_DEV):
            peer = (x ^ (j >> 2), y ^ ((j >> 1) & 1), c ^ (j & 1))
            cp = pltpu.make_async_remote_copy(
                src_ref=gp_ref, dst_ref=gath_ref.at[me], send_sem=send_sems.at[j - 1], recv_sem=recv_sems.at[j - 1],
                device_id=peer, device_id_type=MESH)
            cp.start()
            copies.append(cp)
        for cp in copies:
            cp.wait()
        tot = gath_ref[0]
        for s in range(1, N_DEV):
            tot = tot + gath_ref[s]
        lb = lb_ref[...]
        dl = tot[16:32, :] * lb * (1.0 - lb)
        g = jnp.concatenate([tot[0:16, :], dl[0:8, :], -dl[0:8, :], dl[8:16, :], -dl[8:16, :],
                             tot[32:SMALL_GRAD_ROWS, :]], axis=0)
        g_ref[...] = g
        d_ref[...], nm_ref[...], nv_ref[...] = _adamw_math(w_ref[...], g, m_ref[...], v_ref[...])

    vm = pl.BlockSpec(memory_space=pltpu.VMEM)
    shape = jax.ShapeDtypeStruct((SMALL_ROWS, LANE), F32)
    return pl.pallas_call(
        body, name="small_all_reduce_adamw", in_specs=[vm] * 5 + [_any_spec()], out_specs=[vm] * 4,
        out_shape=[shape] * 4,
        scratch_shapes=[pltpu.VMEM((N_DEV, SMALL_GRAD_ROWS, LANE), F32),
                        pltpu.SemaphoreType.DMA((N_DEV - 1,)), pltpu.SemaphoreType.DMA((N_DEV - 1,))],
    )(gpart, lbpack, wpack, mpack, vpack, after)


_SMALL_NAMES = ("norm_mix_gain", "norm_mem_gain", "lb_logits_fw", "lb_logits_bw", "norm_ffn_gain",
                "hg_norm_gain", "da_q_gain", "da_k_gain", "mem_q_gain", "mem_k_gain")
_SMALL_ROW0 = {"norm_mix_gain": 0, "norm_mem_gain": 8, "lb_logits_fw": 16, "lb_logits_bw": 32, "norm_ffn_gain": 48,
               "hg_norm_gain": 56, "da_q_gain": 64, "da_k_gain": 72, "mem_q_gain": 80, "mem_k_gain": 88}
_LOSS_ROW = 96


def _pack_rows(parts, total_rows):
    rows = []
    for p in parts:
        r = p.reshape(-1, LANE)
        rows.append(jnp.pad(r, ((0, -r.shape[0] % 8), (0, 0))))
    used = sum(r.shape[0] for r in rows)
    if total_rows > used:
        rows.append(jnp.zeros((total_rows - used, LANE), F32))
    return jnp.concatenate(rows, axis=0)


def _unpack_small(pack, like):
    out = {}
    for name in _SMALL_NAMES:
        n = like[name].size // LANE
        r0 = _SMALL_ROW0[name]
        out[name] = pack[r0:r0 + n].reshape(like[name].shape)
    return out


def kernel(x, mem, norm_mix_gain, norm_mem_gain, w_in, lb_logits_fw, lb_logits_bw, hg_norm_gain, da_q_gain, da_k_gain, w_mem_kv, mem_q_gain, mem_k_gain, w_proj_hg, w_proj_da, w_proj_mem, w_out, norm_ffn_gain, w_ffn_in, w_ffn_out, loss_target, m_norm_mix_gain, m_norm_mem_gain, m_w_in, m_lb_logits_fw, m_lb_logits_bw, m_hg_norm_gain, m_da_q_gain, m_da_k_gain, m_w_mem_kv, m_mem_q_gain, m_mem_k_gain, m_w_proj_hg, m_w_proj_da, m_w_proj_mem, m_w_out, m_norm_ffn_gain, m_w_ffn_in, m_w_ffn_out, v_norm_mix_gain, v_norm_mem_gain, v_w_in, v_lb_logits_fw, v_lb_logits_bw, v_hg_norm_gain, v_da_q_gain, v_da_k_gain, v_w_mem_kv, v_mem_q_gain, v_mem_k_gain, v_w_proj_hg, v_w_proj_da, v_w_proj_mem, v_w_out, v_norm_ffn_gain, v_w_ffn_in, v_w_ffn_out):
    small_w = dict(norm_mix_gain=norm_mix_gain, norm_mem_gain=norm_mem_gain, lb_logits_fw=lb_logits_fw,
                   lb_logits_bw=lb_logits_bw, norm_ffn_gain=norm_ffn_gain, hg_norm_gain=hg_norm_gain,
                   da_q_gain=da_q_gain, da_k_gain=da_k_gain, mem_q_gain=mem_q_gain, mem_k_gain=mem_k_gain)
    small_m = dict(norm_mix_gain=m_norm_mix_gain, norm_mem_gain=m_norm_mem_gain, lb_logits_fw=m_lb_logits_fw,
                   lb_logits_bw=m_lb_logits_bw, norm_ffn_gain=m_norm_ffn_gain, hg_norm_gain=m_hg_norm_gain,
                   da_q_gain=m_da_q_gain, da_k_gain=m_da_k_gain, mem_q_gain=m_mem_q_gain, mem_k_gain=m_mem_k_gain)
    small_v = dict(norm_mix_gain=v_norm_mix_gain, norm_mem_gain=v_norm_mem_gain, lb_logits_fw=v_lb_logits_fw,
                   lb_logits_bw=v_lb_logits_bw, norm_ffn_gain=v_norm_ffn_gain, hg_norm_gain=v_hg_norm_gain,
                   da_q_gain=v_da_q_gain, da_k_gain=v_da_k_gain, mem_q_gain=v_mem_q_gain, mem_k_gain=v_mem_k_gain)
    big_w = dict(w_in=w_in[0], w_mem_kv=w_mem_kv[0], w_proj_hg=w_proj_hg[0], w_proj_da=w_proj_da[0],
                 w_proj_mem=w_proj_mem[0], w_out=w_out[0], w_ffn_in=w_ffn_in[0], w_ffn_out=w_ffn_out[0])
    big_m = dict(w_in=m_w_in[0], w_mem_kv=m_w_mem_kv[0], w_proj_hg=m_w_proj_hg[0], w_proj_da=m_w_proj_da[0],
                 w_proj_mem=m_w_proj_mem[0], w_out=m_w_out[0], w_ffn_in=m_w_ffn_in[0], w_ffn_out=m_w_ffn_out[0])
    big_v = dict(w_in=v_w_in[0], w_mem_kv=v_w_mem_kv[0], w_proj_hg=v_w_proj_hg[0], w_proj_da=v_w_proj_da[0],
                 w_proj_mem=v_w_proj_mem[0], w_out=v_w_out[0], w_ffn_in=v_w_ffn_in[0], w_ffn_out=v_w_ffn_out[0])

    row_tile = dict(w_in=128, w_mem_kv=128, w_proj_hg=128, w_proj_da=512, w_proj_mem=512, w_out=128,
                    w_ffn_in=128, w_ffn_out=352)
    rest = _BIG_NAMES[1:]
    shards = [big_w[n].astype(BF16) for n in rest]
    state = {}
    big_out = {}

    def reduce_start(group, stacked):
        names = tuple(stacked)
        arrays = [stacked[n] for n in names] + [lax.empty((4,) + stacked[n].shape[1:], BF16) for n in names]
        plan = _reduce_plan_sibling(len(names))
        send, recv, thru, token = _copies_start(f"grads_{group}_sibling_start", arrays, 4 * len(names), plan)
        state[group] = dict(names=names, plan=plan, send=send, recv=recv, arrays=thru)
        return token

    def reduce_middle(group, after):
        st = state[group]
        names, k = st["names"], len(st["names"])
        thru = _copies_wait(f"grads_{group}_sibling_wait", st["send"], st["recv"], st["arrays"], 4 * k, st["plan"], after)
        partials = [_chip_partials(thru[i], thru[k + i], f"chip_partials_{n}", row_tile[n]) for i, n in enumerate(names)]
        arrays = [p[1] for p in partials] + [lax.empty(p[1].shape, BF16) for p in partials]
        plan = _reduce_plan_chips(k)
        send, recv, thru2, token = _copies_start(f"grads_{group}_chips_start", arrays, 3 * k, plan)
        state[group] = dict(names=names, plan=plan, send=send, recv=recv, arrays=thru2, own=[p[0] for p in partials])
        return token

    def reduce_finish(group, after):
        st = state.pop(group)
        names, k = st["names"], len(st["names"])
        thru = _copies_wait(f"grads_{group}_chips_wait", st["send"], st["recv"], st["arrays"], 3 * k, st["plan"], after)
        for i, n in enumerate(names):
            big_out[n] = _adamw_shard(st["own"][i], thru[k + i], big_w[n], big_m[n], big_v[n], "adamw_" + n, row_tile[n])

    step = _local_step_stages(x[0], mem[0], loss_target[0], small_w)
    event, payload = next(step)
    local = None
    while True:
        reply = None
        if event == "gather_w_in":
            reply = _all_gather([big_w["w_in"].astype(BF16)])[0]
        elif event == "begin":
            nr = len(rest)
            me = 4 * lax.axis_index("x") + 2 * lax.axis_index("y") + lax.axis_index("c")
            arrays = shards + [lax.dynamic_update_slice(lax.empty((N_DEV,) + s.shape, BF16), s[None], (me, 0, 0))
                               for s in shards]
            plan = _gather_plan_out(nr)
            send, recv, thru, reply = _copies_start("weights_rest_out_start", arrays, 4 * nr, plan)
            state["gather"] = dict(plan=plan, send=send, recv=recv, arrays=thru)
        elif event == "after_gla_fwd":
            st = state["gather"]
            nr = len(rest)
            thru = _copies_wait("weights_rest_out_wait", st["send"], st["recv"], st["arrays"], 4 * nr, st["plan"], payload)
            plan = _gather_plan_pass(nr)
            send, recv, lands, reply = _copies_start("weights_rest_pass_start", thru[nr:], 3 * nr, plan)
            state["gather"] = dict(plan=plan, send=send, recv=recv, arrays=lands)
        elif event == "need_weights":
            st = state.pop("gather")
            nr = len(rest)
            lands = _copies_wait("weights_rest_pass_wait", st["send"], st["recv"], st["arrays"], 3 * nr, st["plan"], payload)
            reply = dict(zip(rest, lands))
        elif event == "grads_ffn":
            reply = reduce_start("ffn", payload)
        elif event == "after_gate_merge_bwd":
            reply = reduce_middle("ffn", payload)
        elif event == "grads_mix":
            reply = reduce_start("mix", payload)
        elif event == "after_da_bwd_g0":
            reply = reduce_middle("mix", payload)
            reduce_finish("ffn", payload)
        elif event == "after_gla_bwd":
            reduce_finish("mix", payload)
        elif event == "grads_in":
            reply = reduce_start("in", payload)
        elif event == "after_proj_bwd_act_top":
            reply = reduce_middle("in", payload)
        elif event == "end":
            local = payload
            reduce_finish("in", [local["grad_x"]] + [big_out[n][0] for n in rest])
            break
        event, payload = step.send(reply)
    grad_x = local["grad_x"]

    wpack = _pack_rows([small_w[n] for n in _SMALL_NAMES], SMALL_ROWS)
    mpack = _pack_rows([small_m[n] for n in _SMALL_NAMES], SMALL_ROWS)
    vpack = _pack_rows([small_v[n] for n in _SMALL_NAMES], SMALL_ROWS)
    gs, ds, ms, vs = _small_all_reduce_adamw(local["gpart"], local["lbpack"], wpack, mpack, vpack, big_out["w_in"][0])
    loss = gs[_LOSS_ROW, 0]
    small_out = [_unpack_small(t, small_w) for t in (gs, ds, ms, vs)]

    order = ("norm_mix_gain", "norm_mem_gain", "w_in", "lb_logits_fw", "lb_logits_bw", "hg_norm_gain", "da_q_gain",
             "da_k_gain", "w_mem_kv", "mem_q_gain", "mem_k_gain", "w_proj_hg", "w_proj_da", "w_proj_mem", "w_out",
             "norm_ffn_gain", "w_ffn_in", "w_ffn_out")
    outs = [loss, grad_x[None]]
    for kind in range(4):
        for n in order:
            outs.append(big_out[n][kind][None] if n in big_out else small_out[kind][n])
    return tuple(outs)


_BIG_NAMES = ("w_in", "w_mem_kv", "w_proj_hg", "w_proj_da", "w_proj_mem", "w_out", "w_ffn_in", "w_ffn_out")


def _local_step(xs, mems, target, sw, wg):
    step = _local_step_stages(xs, mems, target, sw)
    stacked = {}
    event, payload = next(step)
    while event != "end":
        reply = None
        if event.startswith("grads_"):
            stacked.update(payload)
        elif event == "need_weights":
            reply = wg
        elif event == "gather_w_in":
            reply = wg["w_in"]
        event, payload = step.send(reply)
    return dict(payload, stacked=stacked)


def _local_step_stages(xs, mems, target, sw):
    norm_mix_gain, norm_mem_gain, norm_ffn_gain = sw["norm_mix_gain"], sw["norm_mem_gain"], sw["norm_ffn_gain"]
    lb_logits_fw, lb_logits_bw, hg_norm_gain = sw["lb_logits_fw"], sw["lb_logits_bw"], sw["hg_norm_gain"]
    da_q_gain, da_k_gain, mem_q_gain, mem_k_gain = sw["da_q_gain"], sw["da_k_gain"], sw["mem_q_gain"], sw["mem_k_gain"]

    win_st = yield "gather_w_in", None
    token = yield "begin", None
    lb_fw = _lb_table(lb_logits_fw, "lb_table_fw")
    lb_bw = _lb_table(lb_logits_bw, "lb_table_bw")
    lb = jnp.concatenate([lb_fw, lb_bw], axis=0).reshape(2, HG_HEADS, 1, HEAD_DIM)
    h, h_t = _rmsnorm_fwd(xs, _after(norm_mix_gain, token), "norm_mix_fwd", 512, transposed=True)
    proj = _matmul(h, win_st, "nn", F32, 1024, IN_SHARD, D_MODEL, "proj_fwd", b_stacked=True, n_outer=True)
    o_hg, o_pre, states, hg_qt, hg_decay = _gla_fwd(proj, lb, hg_norm_gain)
    token = yield "after_gla_fwd", o_hg
    da01 = [_da_fwd(proj, _after(da_q_gain, token), da_k_gain, g) for g in (0, 1)]
    o_da, o_da32, lse_da = _da_fwd(proj, _after(da_q_gain, token), da_k_gain, 2, others=da01)
    wg = yield "need_weights", o_da
    wkv = wg["w_mem_kv"].reshape(D_MODEL, 2 * MEM_WIDTH)
    wphg = wg["w_proj_hg"].reshape(D_MODEL, D_MODEL)
    wpda = jnp.transpose(wg["w_proj_da"], (1, 0, 2)).reshape(DA_WIDTH, D_MODEL)
    wpmem = jnp.transpose(wg["w_proj_mem"], (1, 0, 2)).reshape(MEM_WIDTH, D_MODEL)
    wout = wg["w_out"].reshape(D_MODEL, D_MODEL)
    wfin = jnp.transpose(wg["w_ffn_in"], (1, 0, 2)).reshape(D_MODEL, 2 * D_FF)
    wfout = wg["w_ffn_out"].reshape(D_FF, D_MODEL)
    mem_n = _rmsnorm_fwd(mems, norm_mem_gain, "norm_mem_fwd", N_MEM)
    kv = _matmul(mem_n, wkv, "nn", F32, N_MEM, 1024, D_MODEL, "mem_kv_fwd")
    o_mem = _mem_fwd(proj, kv, mem_q_gain, mem_k_gain, 1024)
    branch_w = (wphg, wpda, wpmem)
    merged, t_hg, t_da, t_mem = _branch_merge_fwd(proj, (o_hg, o_da, o_mem), branch_w, 512)
    x1, h2, h2_t = _residual_rmsnorm_fwd(xs, merged, wout, norm_ffn_gain, "out_norm_ffn_fwd", 512)
    ffn_a, ffn_b, act, act_t = _ffn_in_swiglu(h2, wfin, 1024, 1408)
    dy, dyb, loss_part = _ffn_out_loss_head(x1, act, wfout, target, 512)

    dab = _ffn_out_bwd_swiglu(dyb, wfout, ffn_a, ffn_b, 1024, 1408)
    g_wfout = _matmul(act_t, dyb, "nn", BF16, 1408, 1024, 2048, "ffn_out_bwd_w")
    g_wfin = _matmul(h2_t, dab, "nn", BF16, 1024, 1408, 2048, "ffn_in_bwd_w", b_parts=True)
    token = yield "grads_ffn", dict(
        w_ffn_in=jnp.transpose(g_wfin.reshape(D_MODEL, N_DEV, 2 * D_FF // N_DEV), (1, 0, 2)),
        w_ffn_out=g_wfout.reshape(N_DEV, D_FF // N_DEV, D_MODEL))
    dx1, dx1b, g_norm_ffn = _matmul_rmsnorm_bwd(dab, wfin, x1, dy, _after(norm_ffn_gain, token),
                                                "ffn_in_bwd_act_norm", 512, D_FF, a_parts=True)
    g_wout = _matmul(merged, dx1b, "tn", BF16, 1024, 1024, 2048, "out_bwd_w")
    dt_hg, dt_da, dt_mem, do_pre, do_da, do_mem, dd_da, g_hg_norm, dproj = _gate_merge_bwd(
        proj, (t_hg, t_da, t_mem), branch_w, dx1b, wout, o_pre, hg_norm_gain, o_da32, _dproj_buffer(), 256)
    token = yield "after_gate_merge_bwd", dt_hg
    g_wphg = _matmul(o_hg, dt_hg, "tn", BF16, 1024, 1024, 2048, "proj_hg_bwd_w", after=token)
    g_wpda = _matmul(o_da, dt_da, "tn", BF16, DA_WIDTH, D_MODEL, 2048, "proj_da_bwd_w")
    g_wpmem = _matmul(o_mem, dt_mem, "tn", BF16, MEM_WIDTH, D_MODEL, 2048, "proj_mem_bwd_w")
    by_owner = lambda g: jnp.transpose(g.reshape(g.shape[0], N_DEV, D_MODEL // N_DEV), (1, 0, 2))
    g_wpda, g_wpmem = by_owner(g_wpda), by_owner(g_wpmem)

    dproj, dk_mem, dv_mem, g_mem_q, g_mem_k = _mem_bwd(proj, kv, mem_q_gain, mem_k_gain, do_mem, dproj, 1024)
    dkv = jnp.concatenate([dk_mem, dv_mem], axis=1).astype(BF16)
    g_wkv = _matmul(mem_n, dkv, "tn", BF16, 1024, 1024, N_MEM, "mem_kv_bwd_w")
    dmem_n = _matmul(dkv, wkv, "nt", F32, N_MEM, 1024, 1024, "mem_kv_bwd_act")
    g_norm_mem = _gain_grad(mems, dmem_n, "norm_mem_bwd")
    token = yield "grads_mix", dict(
        w_mem_kv=g_wkv.reshape(N_DEV, D_MODEL // N_DEV, 2 * MEM_WIDTH),
        w_proj_hg=g_wphg.reshape(N_DEV, D_MODEL // N_DEV, D_MODEL),
        w_proj_da=g_wpda, w_proj_mem=g_wpmem,
        w_out=g_wout.reshape(N_DEV, D_MODEL // N_DEV, D_MODEL))

    dproj, g_da_q, g_da_k = _da_bwd(proj, _after(da_q_gain, token), da_k_gain, do_da, lse_da, dd_da, dproj, 0)
    token = yield "after_da_bwd_g0", g_da_q
    for g in (1, 2):
        dproj, gq_part, gk_part = _da_bwd(proj, _after(da_q_gain, token), da_k_gain, do_da, lse_da, dd_da, dproj, g)
        g_da_q, g_da_k = g_da_q + gq_part, g_da_k + gk_part

    dproj, dlb = _gla_bwd(proj, lb, states, hg_qt, hg_decay, do_pre, dproj)
    yield "after_gla_bwd", dlb

    g_win =_matmul(h_t, dproj, "nn", BF16, 1024, IN_SHARD, 2048, "proj_bwd_w", out_stacked=True)
    token = yield "grads_in", dict(w_in=g_win)
    grad_x, g_mix_top = _matmul_rmsnorm_bwd(dproj, win_st, xs, dx1, norm_mix_gain, "proj_bwd_act_norm_top", 1024,
                                            IN_SHARD, b_stacked=True, after=token, m_blocks=(0, 1), with_bf16=False)
    token = yield "after_proj_bwd_act_top", g_mix_top
    grad_x, g_mix_bottom = _matmul_rmsnorm_bwd(dproj, win_st, xs, dx1, norm_mix_gain, "proj_bwd_act_norm_bottom", 1024,
                                               IN_SHARD, b_stacked=True, after=token, m_blocks=(1, 3),
                                               out_into=grad_x, with_bf16=False)
    g_norm_mix = g_mix_top + g_mix_bottom

    gpart = _pack_rows([g_norm_mix, g_norm_mem, dlb[0], dlb[1], g_norm_ffn, g_hg_norm, g_da_q, g_da_k,
                        g_mem_q, g_mem_k, loss_part], SMALL_GRAD_ROWS)
    lbpack = jnp.concatenate([lb_fw.reshape(8, LANE), lb_bw.reshape(8, LANE)], axis=0)
    yield "end", dict(grad_x=grad_x, gpart=gpart, lbpack=lbpack)
```

```python
import numpy as np
import jax
import jax.numpy as jnp
from jax import lax
from jax.experimental import pallas as pl
from jax.experimental.pallas import tpu as pltpu

F32 = jnp.float32
BF16 = jnp.bfloat16
MESH = pl.DeviceIdType.MESH

SEQ = 4096
D_MODEL = 1024
N_DEV = 8
N_MEM = 256
RMS_EPS = 1e-6
NEG_INF = -1e30
LANE = 128
HEAD_DIM = 128
HG_HEADS = 8
HG_CHUNK = 64
HG_SCALE = HEAD_DIM ** -0.5
DA_DILATIONS = (1, 4, 16)
DA_RADIUS = 64
DA_HEADS_PER_GROUP = 4
DA_HEADS = 12
DA_WIDTH = 512
DA_SCALE = HEAD_DIM ** -0.5
DA_QB = 128
DA_WIN = 256
DA_WAYS = 4
DA_FWD_WAYS = 4
MEM_HEADS = 4
MEM_WIDTH = 512
MEM_SCALE = HEAD_DIM ** -0.5
D_FF = 2816
IN_COLS = 13312
IN_SHARD = IN_COLS // N_DEV
CB_HG_Q, CB_F, CB_HG_I, CB_HG_G = 0, 8, 24, 32
CB_DA_Q, CB_DA_K, CB_DA_V, CB_MEM_Q = 40, 52, 64, 76
ADAM_LR, ADAM_B1, ADAM_B2, ADAM_EPS, ADAM_WD, ADAM_STEP = 0.001, 0.9, 0.999, 1e-08, 0.01, 10
VMEM_BYTES_V7X = 64 * 1024 * 1024
SMALL_ROWS = 104
SMALL_GRAD_ROWS = 88

_NN = (((1,), (0,)), ((), ()))
_NT = (((1,), (1,)), ((), ()))
_TN = (((0,), (0,)), ((), ()))


def _dot(a, b, dims):
    return lax.dot_general(a.astype(BF16), b.astype(BF16), dims, preferred_element_type=F32)


def _sigmoid(x):
    return 0.5 * jnp.tanh(0.5 * x) + 0.5


def _params(semantics, est_bytes):
    limit = int(min(VMEM_BYTES_V7X - (6 << 20), max(56 << 20, est_bytes * 3 // 2)))
    return pltpu.CompilerParams(dimension_semantics=semantics, vmem_limit_bytes=limit)


def _nbytes(shape, dtype):
    return int(np.prod(shape)) * jnp.dtype(dtype).itemsize


def _alibi_slopes(n):
    return (2.0 ** (-8.0 * np.arange(1, n + 1) / n)).astype(np.float32)


def _matmul(a, b, mode, out_dtype, tm, tn, tk, name, b_stacked=False, out_stacked=False, n_outer=False, after=None,
            m_blocks=None, out_into=None, a_parts=False, b_parts=False):
    if a_parts:
        assert mode == "nt" and tk == a.shape[2]
        m, kdim = a.shape[1], a.shape[0] * a.shape[2]
    elif mode == "tn":
        kdim, m = a.shape
    else:
        m, kdim = a.shape
    if b_parts:
        assert mode == "nn" and not b_stacked and b.shape[2] % tn == 0
        n = b.shape[0] * b.shape[2]
    elif b_stacked:
        if mode == "nn":
            n = b.shape[0] * b.shape[2]
            assert tn == b.shape[2] and tk == kdim == b.shape[1]
        else:
            assert mode == "nt" and tk == b.shape[2] and b.shape[0] * tk == kdim
            n = b.shape[1]
    else:
        n = b.shape[0] if mode == "nt" else b.shape[1]
    assert m % tm == 0 and n % tn == 0 and kdim % tk == 0
    gm, gn, gk = m // tm, n // tn, kdim // tk
    i0 = 0
    if m_blocks is not None:
        assert mode != "tn" and not out_stacked
        i0, gm = m_blocks

    def ijk(f):
        if n_outer:
            return lambda j, i, k: f(i + i0, j, k)
        return lambda i, j, k: f(i + i0, j, k)

    if a_parts:
        a_spec = pl.BlockSpec((None, tm, tk), ijk(lambda i, j, k: (k, i, 0)))
    elif mode == "tn":
        a_spec = pl.BlockSpec((tk, tm), ijk(lambda i, j, k: (k, i)))
    else:
        a_spec = pl.BlockSpec((tm, tk), ijk(lambda i, j, k: (i, k)))
    if b_parts:
        per_part = b.shape[2] // tn
        b_spec = pl.BlockSpec((None, tk, tn), ijk(lambda i, j, k: (j // per_part, k, j % per_part)))
    elif b_stacked and mode == "nn":
        b_spec = pl.BlockSpec((None, tk, tn), ijk(lambda i, j, k: (j, 0, 0)))
    elif b_stacked:
        b_spec = pl.BlockSpec((None, tn, tk), ijk(lambda i, j, k: (k, j, 0)))
    elif mode == "nt":
        b_spec = pl.BlockSpec((tn, tk), ijk(lambda i, j, k: (j, k)))
    else:
        b_spec = pl.BlockSpec((tk, tn), ijk(lambda i, j, k: (k, j)))
    if out_stacked:
        assert tm == m
        out_shape = jax.ShapeDtypeStruct((gn, m, tn), out_dtype)
        o_spec = pl.BlockSpec((None, tm, tn), ijk(lambda i, j, k: (j, i, 0)))
    else:
        out_shape = jax.ShapeDtypeStruct((m, n), out_dtype)
        o_spec = pl.BlockSpec((tm, tn), ijk(lambda i, j, k: (i, j)))
    dims = {"nn": _NN, "nt": _NT, "tn": _TN}[mode]

    n_in = 2 + (after is not None) + (out_into is not None)

    def body(*refs):
        a_ref, b_ref, o_ref = refs[0], refs[1], refs[n_in]
        part = _dot(a_ref[...], b_ref[...], dims)
        if gk == 1:
            o_ref[...] = part.astype(out_dtype)
            return
        acc_ref = refs[-1]
        k = pl.program_id(2)

        @pl.when(k == 0)
        def _():
            acc_ref[...] = part

        @pl.when(jnp.logical_and(k > 0, k < gk - 1))
        def _():
            acc_ref[...] += part

        @pl.when(k == gk - 1)
        def _():
            o_ref[...] = (acc_ref[...] + part).astype(out_dtype)

    a_tile = _nbytes((tm, tk), a.dtype)
    b_tile = _nbytes((tk, tn), b.dtype)
    o_tile = _nbytes((tm, tn), out_dtype)
    est = 2 * (a_tile + b_tile + o_tile) + 3 * tm * tn * 4 + (a_tile + b_tile)
    grid = (gn, gm, gk) if n_outer else (gm, gn, gk)
    operands, in_specs = [a, b], [a_spec, b_spec]
    if after is not None:
        operands.append(after)
        in_specs.append(pl.BlockSpec(memory_space=pl.ANY))
    aliases = {}
    if out_into is not None:
        aliases = {len(operands): 0}
        operands.append(out_into)
        in_specs.append(pl.BlockSpec(memory_space=pl.ANY))
    return pl.pallas_call(
        body, name=name, grid=grid, in_specs=in_specs, out_specs=o_spec, out_shape=out_shape,
        scratch_shapes=[] if gk == 1 else [pltpu.VMEM((tm, tn), F32)], input_output_aliases=aliases,
        compiler_params=_params(("parallel", "parallel", "arbitrary"), est),
    )(*operands)


def _row_spec(tr, width, col_block=0):
    return pl.BlockSpec((tr, width), lambda i: (i, col_block))


def _bcast_spec(width):
    return pl.BlockSpec((1, width), lambda i: (0, 0))


def _col_spec(width, tr):
    return pl.BlockSpec((width, tr), lambda i: (0, i))


def _rmsnorm_fwd(x, gain, name, tr, transposed=False):
    rows, width = x.shape

    def body(x_ref, g_ref, o_ref, *t_ref):
        xv = x_ref[...]
        r = lax.rsqrt(jnp.mean(xv * xv, axis=-1, keepdims=True) + RMS_EPS)
        h = xv * r * g_ref[...]
        o_ref[...] = h.astype(BF16)
        if transposed:
            t_ref[0][...] = h.T.astype(BF16)

    out_specs, out_shape = [_row_spec(tr, width)], [jax.ShapeDtypeStruct((rows, width), BF16)]
    if transposed:
        out_specs.append(_col_spec(width, tr))
        out_shape.append(jax.ShapeDtypeStruct((width, rows), BF16))
    out = pl.pallas_call(
        body, name=name, grid=(rows // tr,), in_specs=[_row_spec(tr, width), _bcast_spec(width)],
        out_specs=out_specs, out_shape=out_shape,
        compiler_params=_params(("parallel",), 10 * tr * width * 4),
    )(x, gain)
    return out if transposed else out[0]


def _residual_rmsnorm_fwd(x, merged, wout, gain, name, tr):
    rows, width = x.shape

    def body(x_ref, m_ref, w_ref, g_ref, x1_ref, h_ref, ht_ref):
        xv = x_ref[...] + _dot(m_ref[...], w_ref[...], _NN)
        x1_ref[...] = xv
        r = lax.rsqrt(jnp.mean(xv * xv, axis=-1, keepdims=True) + RMS_EPS)
        h = xv * r * g_ref[...]
        h_ref[...] = h.astype(BF16)
        ht_ref[...] = h.T.astype(BF16)

    return pl.pallas_call(
        body, name=name, grid=(rows // tr,),
        in_specs=[_row_spec(tr, width), _row_spec(tr, merged.shape[1]),
                  pl.BlockSpec(wout.shape, lambda i: (0, 0)), _bcast_spec(width)],
        out_specs=[_row_spec(tr, width), _row_spec(tr, width), _col_spec(width, tr)],
        out_shape=[jax.ShapeDtypeStruct((rows, width), F32), jax.ShapeDtypeStruct((rows, width), BF16),
                   jax.ShapeDtypeStruct((width, rows), BF16)],
        compiler_params=_params(("parallel",), 14 * tr * width * 4),
    )(x, merged, wout, gain)


def _matmul_rmsnorm_bwd(a, b, x, dres, gain, name, tm, tk, a_parts=False, b_stacked=False, after=None,
                        m_blocks=None, out_into=None, with_bf16=True):
    width = x.shape[1]
    m = a.shape[1] if a_parts else a.shape[0]
    kdim = a.shape[0] * a.shape[2] if a_parts else a.shape[1]
    gk = kdim // tk
    i0, gm = (0, m // tm) if m_blocks is None else m_blocks
    n_in = 5 + (after is not None) + (out_into is not None)

    def body(*refs):
        a_ref, b_ref, x_ref, dres_ref, g_ref = refs[:5]
        outs = refs[n_in:]
        dx_ref, dg_ref, acc_ref = outs[0], outs[-2], outs[-1]
        i, k = pl.program_id(0), pl.program_id(1)
        part = _dot(a_ref[...], b_ref[...], _NT)

        @pl.when(k == 0)
        def _():
            acc_ref[...] = part

        @pl.when(jnp.logical_and(k > 0, k < gk - 1))
        def _():
            acc_ref[...] += part

        @pl.when(k == gk - 1)
        def _():
            dhv = acc_ref[...] + part if gk > 1 else part
            xv = x_ref[...]
            r = lax.rsqrt(jnp.mean(xv * xv, axis=-1, keepdims=True) + RMS_EPS)
            xhat = xv * r
            dyg = dhv * g_ref[...]
            dx = dres_ref[...] + r * (dyg - xhat * jnp.mean(dyg * xhat, axis=-1, keepdims=True))
            dx_ref[...] = dx
            if with_bf16:
                outs[1][...] = dx.astype(BF16)
            gpart = jnp.sum(dhv * xhat, axis=0, keepdims=True)

            @pl.when(i == 0)
            def _():
                dg_ref[...] = gpart

            @pl.when(i > 0)
            def _():
                dg_ref[...] += gpart

    rows = lambda width_: pl.BlockSpec((tm, width_), lambda i, k: (i + i0, 0))
    if a_parts:
        a_spec = pl.BlockSpec((None, tm, tk), lambda i, k: (k, i + i0, 0))
    else:
        a_spec = pl.BlockSpec((tm, tk), lambda i, k: (i + i0, k))
    if b_stacked:
        b_spec = pl.BlockSpec((None, width, tk), lambda i, k: (k, 0, 0))
    else:
        b_spec = pl.BlockSpec((width, tk), lambda i, k: (0, k))
    operands = [a, b, x, dres, gain]
    in_specs = [a_spec, b_spec, rows(width), rows(width), pl.BlockSpec((1, width), lambda i, k: (0, 0))]
    for extra in (after, out_into):
        if extra is not None:
            operands.append(extra)
            in_specs.append(pl.BlockSpec(memory_space=pl.ANY))
    aliases = {} if out_into is None else {len(operands) - 1: 0}
    out_specs = [rows(width)] + ([rows(width)] if with_bf16 else []) + [pl.BlockSpec((1, width), lambda i, k: (0, 0))]
    out_shape = [jax.ShapeDtypeStruct((m, width), F32)] + ([jax.ShapeDtypeStruct((m, width), BF16)] if with_bf16 else [])
    out_shape.append(jax.ShapeDtypeStruct((1, width), F32))
    est = 4 * tm * tk + 4 * width * tk + 12 * tm * width * 4
    return pl.pallas_call(
        body, name=name, grid=(gm, gk), in_specs=in_specs, out_specs=out_specs, out_shape=out_shape,
        scratch_shapes=[pltpu.VMEM((tm, width), F32)], input_output_aliases=aliases,
        compiler_params=_params(("arbitrary", "arbitrary"), est),
    )(*operands)


def _gain_grad(x, dh, name):
    rows, width = x.shape

    def body(x_ref, dh_ref, dg_ref):
        xv = x_ref[...]
        r = lax.rsqrt(jnp.mean(xv * xv, axis=-1, keepdims=True) + RMS_EPS)
        dg_ref[...] = jnp.sum(dh_ref[...] * xv * r, axis=0, keepdims=True)

    return pl.pallas_call(
        body, name=name, grid=(1,), in_specs=[_row_spec(rows, width), _row_spec(rows, width)],
        out_specs=_bcast_spec(width), out_shape=jax.ShapeDtypeStruct((1, width), F32),
        compiler_params=_params(("arbitrary",), 6 * rows * width * 4),
    )(x, dh)


def _lb_table(logits, name):
    slots, width = logits.shape

    def body(l_ref, o_ref):
        lv = l_ref[...]
        mx = jnp.max(lv, axis=0, keepdims=True)
        e = jnp.exp(lv - mx)
        o_ref[...] = e[0:1, :] / jnp.sum(e, axis=0, keepdims=True)

    return pl.pallas_call(
        body, name=name, grid=(1,), in_specs=[pl.BlockSpec((slots, width), lambda i: (0, 0))],
        out_specs=_bcast_spec(width), out_shape=jax.ShapeDtypeStruct((1, width), F32),
    )(logits)


def _branch_merge_fwd(proj, outs, weights, tr):
    w = D_MODEL

    def body(ghg_ref, gda_ref, gmem_ref, ohg_ref, oda_ref, omem_ref, whg_ref, wda_ref, wmem_ref,
             m_ref, thg_ref, tda_ref, tmem_ref):
        acc = None
        for g_ref, o_ref, w_ref, t_ref in ((ghg_ref, ohg_ref, whg_ref, thg_ref), (gda_ref, oda_ref, wda_ref, tda_ref),
                                           (gmem_ref, omem_ref, wmem_ref, tmem_ref)):
            t = _dot(o_ref[...], w_ref[...], _NN)
            t_ref[...] = t.astype(BF16)
            term = _sigmoid(g_ref[...]) * t
            acc = term if acc is None else acc + term
        m_ref[...] = acc.astype(BF16)

    whole = lambda a: pl.BlockSpec(a.shape, lambda i: (0, 0))
    shape = jax.ShapeDtypeStruct((SEQ, w), BF16)
    return pl.pallas_call(
        body, name="branch_merge_fwd", grid=(SEQ // tr,),
        in_specs=[_row_spec(tr, w, 10), _row_spec(tr, w, 11), _row_spec(tr, w, 12)]
        + [_row_spec(tr, o.shape[1]) for o in outs] + [whole(wt) for wt in weights],
        out_specs=[_row_spec(tr, w)] * 4, out_shape=[shape] * 4,
        compiler_params=_params(("parallel",), 20 * tr * w * 4),
    )(proj, proj, proj, *outs, *weights)


def _dproj_buffer():
    return lax.empty((SEQ, IN_COLS), BF16)


def _gate_merge_bwd(proj, ts, weights, dx1b, wout, o_pre, hg_gain, o_da32, dproj, tr):
    w = D_MODEL
    steps = SEQ // tr
    gate_col0 = 10 * w
    hg_gate_col0 = CB_HG_G * LANE

    def body(ghg_ref, gda_ref, gmem_ref, hgg_ref, thg_ref, tda_ref, tmem_ref, whg_ref, wda_ref, wmem_ref, dx_ref,
             wout_ref, opre_ref, gain_ref, oda_ref, dproj_in,
             dthg_ref, dtda_ref, dtmem_ref, dopre_ref, doda_ref, domem_ref, dd_ref, dgain_ref, dproj_ref,
             stage, stage_hg, sems):
        del dproj_in
        i = pl.program_id(0)
        slot = i % 2
        rows = pl.ds(pl.multiple_of(i * tr, tr), tr)

        def slot_copies(s):
            return (pltpu.make_async_copy(stage.at[s], dproj_ref.at[rows, pl.ds(gate_col0, 3 * w)], sems.at[s]),
                    pltpu.make_async_copy(stage_hg.at[s], dproj_ref.at[rows, pl.ds(hg_gate_col0, w)], sems.at[2 + s]))

        @pl.when(i >= 2)
        def _():
            for cp in slot_copies(slot):
                cp.wait()

        dm = _dot(dx_ref[...], wout_ref[...], _NT)
        branches = ((ghg_ref, thg_ref, whg_ref, dthg_ref), (gda_ref, tda_ref, wda_ref, dtda_ref),
                    (gmem_ref, tmem_ref, wmem_ref, dtmem_ref))
        dos = []
        for b, (g_ref, t_ref, w_ref, dt_ref) in enumerate(branches):
            s = _sigmoid(g_ref[...])
            dt = (s * dm).astype(BF16)
            dt_ref[...] = dt
            dos.append(_dot(dt, w_ref[...], _NT))
            stage[slot, :, b * w:(b + 1) * w] = (dm * t_ref[...].astype(F32) * s * (1.0 - s)).astype(BF16)
        do_hg, do_da, do_mem = dos
        doda_ref[...] = do_da
        domem_ref[...] = do_mem

        gainv = gain_ref[...]
        part = jnp.zeros((1, HEAD_DIM), F32)
        for h in range(HG_HEADS):
            hs = slice(h * HEAD_DIM, (h + 1) * HEAD_DIM)
            o = opre_ref[:, hs]
            r = lax.rsqrt(jnp.mean(o * o, axis=-1, keepdims=True) + RMS_EPS)
            ohat = o * r
            g = hgg_ref[:, hs]
            sg = _sigmoid(g)
            silu = g * sg
            dout = do_hg[:, hs]
            stage_hg[slot, :, hs] = (dout * ohat * gainv * (sg + silu * (1.0 - sg))).astype(BF16)
            dy = dout * silu
            part = part + jnp.sum(dy * ohat, axis=0, keepdims=True)
            dn = dy * gainv
            dopre_ref[:, hs] = r * (dn - ohat * jnp.mean(dn * ohat, axis=-1, keepdims=True))
        for cp in slot_copies(slot):
            cp.start()

        prod = do_da * oda_ref[...]
        for h in range(DA_WIDTH // HEAD_DIM):
            hs = slice(h * HEAD_DIM, (h + 1) * HEAD_DIM)
            dd_ref[:, hs] = jnp.broadcast_to(jnp.sum(prod[:, hs], axis=-1, keepdims=True), (tr, HEAD_DIM))

        @pl.when(i == 0)
        def _():
            dgain_ref[...] = part

        @pl.when(i > 0)
        def _():
            dgain_ref[...] += part

        @pl.when(i == steps - 1)
        def _():
            for cp in slot_copies(1 - slot) + slot_copies(slot):
                cp.wait()

    assert steps >= 2
    whole = lambda a: pl.BlockSpec(a.shape, lambda i: (0, 0))
    widths = [wt.shape[0] for wt in weights]
    return pl.pallas_call(
        body, name="gate_merge_bwd", grid=(steps,),
        in_specs=[_row_spec(tr, w, 10), _row_spec(tr, w, 11), _row_spec(tr, w, 12), _row_spec(tr, w, hg_gate_col0 // w)]
        + [_row_spec(tr, w)] * 3 + [whole(wt) for wt in weights]
        + [_row_spec(tr, w), whole(wout), _row_spec(tr, w), _bcast_spec(HEAD_DIM), _row_spec(tr, DA_WIDTH), _any_spec()],
        out_specs=[_row_spec(tr, w)] * 4 + [_row_spec(tr, widths[1]), _row_spec(tr, widths[2]),
                                            _row_spec(tr, DA_WIDTH), _bcast_spec(HEAD_DIM), _any_spec()],
        out_shape=[jax.ShapeDtypeStruct((SEQ, w), BF16)] * 3 + [jax.ShapeDtypeStruct((SEQ, w), F32)]
        + [jax.ShapeDtypeStruct((SEQ, widths[1]), F32), jax.ShapeDtypeStruct((SEQ, widths[2]), F32),
           jax.ShapeDtypeStruct((SEQ, DA_WIDTH), F32), jax.ShapeDtypeStruct((1, HEAD_DIM), F32),
           jax.ShapeDtypeStruct((SEQ, IN_COLS), BF16)],
        scratch_shapes=[pltpu.VMEM((2, tr, 3 * w), BF16), pltpu.VMEM((2, tr, w), BF16), pltpu.SemaphoreType.DMA((4,))],
        input_output_aliases={15: 8},
        compiler_params=_params(("arbitrary",), 44 * tr * w * 4),
    )(proj, proj, proj, proj, *ts, *weights, dx1b, wout, o_pre, hg_gain, o_da32, dproj)


def _ffn_in_swiglu(h2, wfin, tm, tn):
    gm, gn = SEQ // tm, D_FF // tn

    def body(h_ref, wa_ref, wb_ref, a_ref, b_ref, act_ref, actt_ref):
        h = h_ref[...]
        a = _dot(h, wa_ref[...], _NN)
        b = _dot(h, wb_ref[...], _NN)
        act = a * _sigmoid(a) * b
        a_ref[...] = a.astype(BF16)
        b_ref[...] = b.astype(BF16)
        act_ref[...] = act.astype(BF16)
        actt_ref[...] = act.T.astype(BF16)

    tile = pl.BlockSpec((tm, tn), lambda j, i: (i, j))
    shape = jax.ShapeDtypeStruct((SEQ, D_FF), BF16)
    return pl.pallas_call(
        body, name="ffn_in_swiglu_fwd", grid=(gn, gm),
        in_specs=[pl.BlockSpec((tm, D_MODEL), lambda j, i: (i, 0)),
                  pl.BlockSpec((D_MODEL, tn), lambda j, i: (0, j)),
                  pl.BlockSpec((D_MODEL, tn), lambda j, i: (0, gn + j))],
        out_specs=[tile, tile, tile, pl.BlockSpec((tn, tm), lambda j, i: (j, i))],
        out_shape=[shape, shape, shape, jax.ShapeDtypeStruct((D_FF, SEQ), BF16)],
        compiler_params=_params(("parallel", "parallel"), 4 * tm * D_MODEL + 8 * D_MODEL * tn + 16 * tm * tn
                                + 6 * tm * tn * 4),
    )(h2, wfin, wfin)


def _ffn_out_bwd_swiglu(dyb, wfout, a, b, tm, tn):
    gm, gn = SEQ // tm, D_FF // tn

    def body(dy_ref, w_ref, a_ref, b_ref, o_ref):
        d = _dot(dy_ref[...], w_ref[...], _NT)
        av = a_ref[...].astype(F32)
        bv = b_ref[...].astype(F32)
        s = _sigmoid(av)
        silu = av * s
        o_ref[0] = (d * bv * (s + silu * (1.0 - s))).astype(BF16)
        o_ref[1] = (d * silu).astype(BF16)

    tile = pl.BlockSpec((tm, tn), lambda i, j: (i, j))
    return pl.pallas_call(
        body, name="ffn_out_bwd_swiglu", grid=(gm, gn),
        in_specs=[pl.BlockSpec((tm, D_MODEL), lambda i, j: (i, 0)), pl.BlockSpec((tn, D_MODEL), lambda i, j: (j, 0)),
                  tile, tile],
        out_specs=pl.BlockSpec((2, tm, tn), lambda i, j: (0, i, j)),
        out_shape=jax.ShapeDtypeStruct((2, SEQ, D_FF), BF16),
        compiler_params=_params(("parallel", "parallel"), 4 * tm * D_MODEL + 4 * tn * D_MODEL + 16 * tm * tn
                                + 6 * tm * tn * 4),
    )(dyb, wfout, a, b)


def _ffn_out_loss_head(x1, act, wfout, target, tr):
    w = D_MODEL

    def body(x_ref, a_ref, w_ref, t_ref, dy_ref, dyb_ref, loss_ref, acc_ref):
        err = x_ref[...] + _dot(a_ref[...], w_ref[...], _NN) - t_ref[...]
        dy = err * (1.0 / w)
        dy_ref[...] = dy
        dyb_ref[...] = dy.astype(BF16)
        part = jnp.sum(err * err, axis=0, keepdims=True)

        @pl.when(pl.program_id(0) == 0)
        def _():
            acc_ref[...] = part

        @pl.when(pl.program_id(0) > 0)
        def _():
            acc_ref[...] += part

        @pl.when(pl.program_id(0) == SEQ // tr - 1)
        def _():
            total = jnp.sum(acc_ref[...], axis=1, keepdims=True) * (0.5 / w)
            loss_ref[...] = jnp.broadcast_to(total, (1, LANE))

    return pl.pallas_call(
        body, name="ffn_out_loss_head", grid=(SEQ // tr,),
        in_specs=[_row_spec(tr, w), _row_spec(tr, D_FF), pl.BlockSpec((D_FF, w), lambda i: (0, 0)), _row_spec(tr, w)],
        out_specs=[_row_spec(tr, w), _row_spec(tr, w), _bcast_spec(LANE)],
        out_shape=[jax.ShapeDtypeStruct((SEQ, w), F32), jax.ShapeDtypeStruct((SEQ, w), BF16),
                   jax.ShapeDtypeStruct((1, LANE), F32)],
        scratch_shapes=[pltpu.VMEM((1, w), F32)],
        compiler_params=_params(("arbitrary",), 12 * tr * w * 4 + 4 * D_FF * w),
    )(x1, act, wfout, target)


GLA_ROWS = 256
GLA_CPB = GLA_ROWS // HG_CHUNK
GLA_NBLK = SEQ // GLA_ROWS
GLA_WAYS = 4
GLA_TRIPS = GLA_NBLK // GLA_WAYS
GLA_GRAD_WAYS = 4
GLA_GRAD_TRIPS = GLA_NBLK // GLA_GRAD_WAYS
GLA_NCK = SEQ // HG_CHUNK
GLA_INTER_WAYS = 8


def _dot_split(m, xv, dims, terms=3):
    dot = lambda t: lax.dot_general(m, t, dims, preferred_element_type=F32)
    hi = xv.astype(BF16)
    r1 = xv - hi.astype(F32)
    mid = r1.astype(BF16)
    if terms == 2:
        return dot(mid) + dot(hi)
    lo = (r1 - mid.astype(F32)).astype(BF16)
    return (dot(lo) + dot(mid)) + dot(hi)


def _gla_block(qc, fc, lbv, masks, b=None):
    mask, maskb, direction = masks
    sq = _sigmoid(qc)
    q = qc * sq * HG_SCALE
    sf = _sigmoid(fc)
    forget = lbv + (1.0 - lbv) * sf
    k = 1.0 - forget
    logf = jnp.log(forget)
    if b is None:
        b = _dot_split(maskb, logf, _NN)
    ends = []
    for j in range(GLA_CPB):
        lo, hi = j * HG_CHUNK, (j + 1) * HG_CHUNK
        end = jnp.where(direction == 0, b[hi - 1:hi, :], b[lo:lo + 1, :])
        ends.append(jnp.broadcast_to(end, (HG_CHUNK, HEAD_DIM)))
    bt = jnp.concatenate(ends, axis=0)
    eb = jnp.exp(b)
    qt = q * eb
    kt = k * jnp.exp(-b)
    kh = k * jnp.exp(bt - b)
    a = jnp.where(mask, _dot(qt, kt, _NT), 0.0)
    return dict(sq=sq, sf=sf, forget=forget, k=k, b=b, bt=bt, eb=eb, qt=qt, kt=kt, kh=kh, a=a)


def _gla_masks(direction):
    row = lax.broadcasted_iota(jnp.int32, (GLA_ROWS, GLA_ROWS), 0)
    col = lax.broadcasted_iota(jnp.int32, (GLA_ROWS, GLA_ROWS), 1)
    same = (row // HG_CHUNK) == (col // HG_CHUNK)
    mask = jnp.logical_and(same, jnp.where(direction == 0, row - col, col - row) >= 0)
    return mask, jnp.where(mask, 1.0, 0.0).astype(BF16), direction


def _gla_chunk_rows(j):
    return slice(j * HG_CHUNK, (j + 1) * HG_CHUNK)


def _gla_block_rows(b):
    return pl.ds(pl.multiple_of(b * GLA_ROWS, GLA_ROWS), GLA_ROWS)


def _head_spec(col_block0):
    return pl.BlockSpec((SEQ, HEAD_DIM), lambda h, d: (0, col_block0 + h))


def _gla_fwd(proj, lb, gain):
    nck = GLA_NCK

    def body(q_ref, f_ref, v_ref, g_ref, lb_ref, gain_ref, ohg_ref, opre_ref, st_ref, qt_scr, dec_scr, b_ref, cs_scr):
        d = pl.program_id(1)
        masks = _gla_masks(d)
        lbv = lb_ref[...]

        @pl.when(d == 0)
        def _():
            opre_ref[...] = jnp.zeros_like(opre_ref)

        def intra(s, carry):
            blocks = [s + w * GLA_TRIPS for w in range(GLA_WAYS)]
            rows = [_gla_block_rows(b) for b in blocks]
            loaded = [(q_ref[r, :], f_ref[r, :], v_ref[r, :], opre_ref[r, :]) for r in rows]
            results = []
            for qc, fc, v, o_prev in loaded:
                ck = _gla_block(qc, fc, lbv, masks)
                o = o_prev + _dot(ck["a"], v, _NN)
                cs = [_dot(v[_gla_chunk_rows(j), :], ck["kh"][_gla_chunk_rows(j), :], _TN) for j in range(GLA_CPB)]
                dec = [jnp.exp(ck["bt"][j * HG_CHUNK:j * HG_CHUNK + 8, :]) for j in range(GLA_CPB)]
                results.append((o, ck["qt"].astype(BF16), cs, dec, ck["b"]))
            for b, r, (o, qt, cs, dec, decay_log) in zip(blocks, rows, results):
                opre_ref[r, :] = o
                qt_scr[r, :] = qt
                b_ref[r, :] = decay_log
                for j in range(GLA_CPB):
                    cs_scr[b * GLA_CPB + j] = cs[j]
                    dec_scr[b * GLA_CPB + j] = dec[j]
            return carry

        lax.fori_loop(0, GLA_TRIPS, intra, 0)

        def scan(i, st):
            c = jnp.where(d == 0, i, nck - 1 - i)
            st_ref[c] = st
            return st * dec_scr[c][0:1, :] + cs_scr[c]

        lax.fori_loop(0, nck, scan, jnp.zeros((HEAD_DIM, HEAD_DIM), F32), unroll=4)

        def inter(t, carry):
            chunks = [t + w * (nck // GLA_INTER_WAYS) for w in range(GLA_INTER_WAYS)]
            rows = [pl.ds(pl.multiple_of(c * HG_CHUNK, HG_CHUNK), HG_CHUNK) for c in chunks]
            loaded = [(opre_ref[r, :], qt_scr[r, :], st_ref[c]) for r, c in zip(rows, chunks)]
            for r, o in zip(rows, [o_prev + _dot(qt, st, _NT) for o_prev, qt, st in loaded]):
                opre_ref[r, :] = o
            return carry

        lax.fori_loop(0, nck // GLA_INTER_WAYS, inter, 0)

        @pl.when(d == 1)
        def _():
            o = opre_ref[...]
            r = lax.rsqrt(jnp.mean(o * o, axis=-1, keepdims=True) + RMS_EPS)
            g = g_ref[...]
            ohg_ref[...] = (o * r * gain_ref[...] * (g * _sigmoid(g))).astype(BF16)

    blk = SEQ * HEAD_DIM * 4
    return pl.pallas_call(
        body, name="gla_fwd", grid=(HG_HEADS, 2),
        in_specs=[_head_spec(CB_HG_Q),
                  pl.BlockSpec((SEQ, HEAD_DIM), lambda h, d: (0, CB_F + 8 * d + h)),
                  _head_spec(CB_HG_I), _head_spec(CB_HG_G),
                  pl.BlockSpec((None, None, 1, HEAD_DIM), lambda h, d: (d, h, 0, 0)),
                  pl.BlockSpec((1, HEAD_DIM), lambda h, d: (0, 0))],
        out_specs=[_head_spec(0), _head_spec(0),
                   pl.BlockSpec((None, None, nck, HEAD_DIM, HEAD_DIM), lambda h, d: (h, d, 0, 0, 0)),
                   pl.BlockSpec((None, SEQ, HEAD_DIM), lambda h, d: (d, 0, h)),
                   pl.BlockSpec((None, None, nck, 8, HEAD_DIM), lambda h, d: (h, d, 0, 0, 0)),
                   pl.BlockSpec((None, SEQ, HEAD_DIM), lambda h, d: (d, 0, h))],
        out_shape=[jax.ShapeDtypeStruct((SEQ, D_MODEL), BF16), jax.ShapeDtypeStruct((SEQ, D_MODEL), F32),
                   jax.ShapeDtypeStruct((HG_HEADS, 2, nck, HEAD_DIM, HEAD_DIM), F32),
                   jax.ShapeDtypeStruct((2, SEQ, D_MODEL), BF16),
                   jax.ShapeDtypeStruct((HG_HEADS, 2, nck, 8, HEAD_DIM), F32),
                   jax.ShapeDtypeStruct((2, SEQ, D_MODEL), F32)],
        scratch_shapes=[pltpu.VMEM((nck, HEAD_DIM, HEAD_DIM), F32)],
        compiler_params=_params(("parallel", "arbitrary"), 8 * blk + 3 * blk + 2 * blk + 2 * blk + 3 * blk),
    )(proj, proj, proj, proj, lb, gain)


def _gla_bwd(proj, lb, states, qt_all, dec_all, b_all, do, dproj):
    nck = GLA_NCK

    def body(q_ref, f_ref, v_ref, lb_ref, do_ref, st_ref, qt_ref, dec_scr, b_ref, dproj_in, dproj_ref, dlb_ref,
             dq_acc, dv_acc, dst_scr, cs_scr, df_out, dq_out, dv_out, sems):
        del dproj_in
        h, d = pl.program_id(0), pl.program_id(1)
        masks = _gla_masks(d)
        mask, maskb, _ = masks
        lbv = lb_ref[...]

        def column_copy(k, staging, col_block):
            cols = pl.ds(pl.multiple_of(col_block * LANE, LANE), LANE)
            return pltpu.make_async_copy(staging, dproj_ref.at[:, cols], sems.at[k])

        df_copy = column_copy(0, df_out, CB_F + 8 * d + h)
        dq_copy = column_copy(1, dq_out, CB_HG_Q + h)
        dv_copy = column_copy(2, dv_out, CB_HG_I + h)
        last = jnp.logical_and(h == HG_HEADS - 1, d == 1)

        def intra(s, carry):
            blocks = [s + w * GLA_TRIPS for w in range(GLA_WAYS)]
            loaded = [(qt_ref[r, :], do_ref[r, :]) for r in map(_gla_block_rows, blocks)]
            results = [[_dot(doc[_gla_chunk_rows(j), :], qt[_gla_chunk_rows(j), :], _TN) for j in range(GLA_CPB)]
                       for qt, doc in loaded]
            for b, cs in zip(blocks, results):
                for j in range(GLA_CPB):
                    cs_scr[b * GLA_CPB + j] = cs[j]
            return carry

        lax.fori_loop(0, GLA_TRIPS, intra, 0)

        def scan(i, dst):
            c = jnp.where(d == 0, nck - 1 - i, i)
            dst_scr[c] = dst
            return dst * dec_scr[c][0:1, :] + cs_scr[c]

        lax.fori_loop(0, nck, scan, jnp.zeros((HEAD_DIM, HEAD_DIM), F32), unroll=4)

        @pl.when(d == 0)
        def _():
            dq_acc[...] = jnp.zeros_like(dq_acc)
            dv_acc[...] = jnp.zeros_like(dv_acc)

        def block_grads(qc, fc, v, doc, states_in, dstates, decays, decay_log):
            ck = _gla_block(qc, fc, lbv, masks, b=decay_log)
            qt, kt, kh, a = ck["qt"], ck["kt"], ck["kh"], ck["a"]
            da = jnp.where(mask, _dot(doc, v, _NT), 0.0)
            dqt_i = _dot(da, kt, _NN)
            dkt = _dot(da, qt, _TN)
            dv_i = _dot(a, doc, _TN)
            dqt_p, dv_p, dkh_p, dbt_p = [], [], [], []
            for j in range(GLA_CPB):
                cr = _gla_chunk_rows(j)
                st_in, dst = states_in[j], dstates[j]
                dqt_p.append(dqt_i[cr, :] + _dot(doc[cr, :], st_in, _NN))
                dv_p.append(dv_i[cr, :] + _dot(kh[cr, :], dst, _NT))
                dkh_j = _dot(v[cr, :], dst, _NN)
                dkh_p.append(dkh_j)
                dbt_j = (decays[j][0:1, :] * jnp.sum(dst * st_in, axis=0, keepdims=True)
                         + jnp.sum(dkh_j * kh[cr, :], axis=0, keepdims=True))
                dbt_p.append(jnp.broadcast_to(dbt_j, (HG_CHUNK, HEAD_DIM)))
            dqt = jnp.concatenate(dqt_p, axis=0)
            dv = jnp.concatenate(dv_p, axis=0)
            dkh = jnp.concatenate(dkh_p, axis=0)
            dbt = jnp.concatenate(dbt_p, axis=0)
            db = dqt * qt - dkt * kt - dkh * kh
            dq = dqt * ck["eb"]
            dk = dkt * jnp.exp(-ck["b"]) + dkh * jnp.exp(ck["bt"] - ck["b"])
            dlogf = _dot_split(maskb, db, _TN, terms=2) + dbt
            dforget = dlogf / ck["forget"] - dk
            sf, sq = ck["sf"], ck["sq"]
            df = (dforget * (1.0 - lbv) * sf * (1.0 - sf)).astype(BF16)
            dqc = dq * HG_SCALE * (sq + qc * sq * (1.0 - sq))
            return df, dqc, dv, jnp.sum(dforget * (1.0 - sf), axis=0, keepdims=True)

        def grads(s, dlb):
            blocks = [s + w * GLA_GRAD_TRIPS for w in range(GLA_GRAD_WAYS)]
            rows = [_gla_block_rows(b) for b in blocks]
            loaded = []
            for b, r in zip(blocks, rows):
                chunks = [b * GLA_CPB + j for j in range(GLA_CPB)]
                loaded.append((q_ref[r, :], f_ref[r, :], v_ref[r, :], do_ref[r, :], [st_ref[c] for c in chunks],
                               [dst_scr[c] for c in chunks], [dec_scr[c] for c in chunks], b_ref[r, :],
                               dq_acc[r, :], dv_acc[r, :]))
            results = [block_grads(*t[:8]) + (t[8], t[9]) for t in loaded]
            for r, (df, dqc, dv, dlb_part, dq_prev, dv_prev) in zip(rows, results):
                df_out[r, :] = df
                dq_acc[r, :] = dq_prev + dqc
                dv_acc[r, :] = dv_prev + dv
                dlb = dlb + dlb_part
            return dlb

        @pl.when(jnp.logical_or(h > 0, d > 0))
        def _():
            df_copy.wait()

        dlb_ref[...] = lax.fori_loop(0, GLA_GRAD_TRIPS, grads, jnp.zeros((1, HEAD_DIM), F32))
        df_copy.start()

        @pl.when(d == 1)
        def _():
            @pl.when(h > 0)
            def _():
                dq_copy.wait()
                dv_copy.wait()

            dq_out[...] = dq_acc[...].astype(BF16)
            dv_out[...] = dv_acc[...].astype(BF16)
            dq_copy.start()
            dv_copy.start()

        @pl.when(last)
        def _():
            df_copy.wait()
            dq_copy.wait()
            dv_copy.wait()

    blk = SEQ * HEAD_DIM * 4
    state_bytes = nck * HEAD_DIM * HEAD_DIM * 4
    dproj, dlb = pl.pallas_call(
        body, name="gla_bwd", grid=(HG_HEADS, 2),
        in_specs=[_head_spec(CB_HG_Q),
                  pl.BlockSpec((SEQ, HEAD_DIM), lambda h, d: (0, CB_F + 8 * d + h)),
                  _head_spec(CB_HG_I),
                  pl.BlockSpec((None, None, 1, HEAD_DIM), lambda h, d: (d, h, 0, 0)),
                  _head_spec(0),
                  pl.BlockSpec((None, None, nck, HEAD_DIM, HEAD_DIM), lambda h, d: (h, d, 0, 0, 0)),
                  pl.BlockSpec((None, SEQ, HEAD_DIM), lambda h, d: (d, 0, h)),
                  pl.BlockSpec((None, None, nck, 8, HEAD_DIM), lambda h, d: (h, d, 0, 0, 0)),
                  pl.BlockSpec((None, SEQ, HEAD_DIM), lambda h, d: (d, 0, h)),
                  _any_spec()],
        out_specs=[_any_spec(), pl.BlockSpec((None, None, 1, HEAD_DIM), lambda h, d: (d, h, 0, 0))],
        out_shape=[jax.ShapeDtypeStruct((SEQ, IN_COLS), BF16), jax.ShapeDtypeStruct((2, HG_HEADS, 1, HEAD_DIM), F32)],
        scratch_shapes=[pltpu.VMEM((SEQ, HEAD_DIM), F32)] * 2
        + [pltpu.VMEM((nck, HEAD_DIM, HEAD_DIM), F32)] * 2
        + [pltpu.VMEM((SEQ, HEAD_DIM), BF16)] * 3 + [pltpu.SemaphoreType.DMA((3,))],
        input_output_aliases={9: 0},
        compiler_params=_params(("arbitrary", "arbitrary"), 10 * blk + 4 * state_bytes + 3 * blk + 3 * blk),
    )(proj, proj, proj, lb, do, states, qt_all, dec_all, b_all, dproj)
    return dproj, dlb


def _da_residue_rows(r, n0, size, d):
    if d == 1:
        return pl.ds(pl.multiple_of(n0, 8), size)
    return pl.ds(r + n0 * d, size, stride=d)


def _da_residue_ways(d, nqb):
    return min(d, 4) if nqb <= 2 else 1


def _da_rmsnorm(x, gain):
    r = lax.rsqrt(jnp.mean(x * x, axis=-1, keepdims=True) + RMS_EPS)
    return x * r, r, x * r * gain


def _da_scores(qn_scr, kn_scr, slope, i, ld):
    w0 = jnp.clip(i * DA_QB - DA_RADIUS, 0, ld - DA_WIN)
    w0 = pl.multiple_of(w0, DA_RADIUS)
    qrows = pl.ds(pl.multiple_of(i * DA_QB, DA_QB), DA_QB)
    win = pl.ds(w0, DA_WIN)
    qb = qn_scr[qrows, :]
    kw = kn_scr[win, :]
    s = _dot(qb, kw, _NT) * DA_SCALE
    qpos = i * DA_QB + lax.broadcasted_iota(jnp.int32, (DA_QB, DA_WIN), 0)
    kpos = w0 + lax.broadcasted_iota(jnp.int32, (DA_QB, DA_WIN), 1)
    arel = jnp.abs(kpos - qpos)
    s = s - slope * arel.astype(F32)
    s = jnp.where(arel <= DA_RADIUS, s, NEG_INF)
    return s, qb, kw, qrows, win


def _da_slopes(group):
    d = DA_DILATIONS[group]
    sl = _alibi_slopes(DA_HEADS)[4 * group:4 * group + 4] * d
    return jnp.asarray(np.broadcast_to(sl[:, None, None], (4, 1, LANE)).copy())


def _da_fwd(proj, gq, gk, group, others=()):
    d = DA_DILATIONS[group]
    ld = SEQ // d
    nqb = ld // DA_QB
    n_other = 2 * len(others)

    ways = min(DA_FWD_WAYS, nqb)
    rways = _da_residue_ways(d, nqb)

    def body(q_ref, k_ref, v_ref, gq_ref, gk_ref, sl_ref, *refs):
        other_refs, refs = refs[:n_other], refs[n_other:]
        if others:
            ob_ref, of_ref, lsej_ref, qn_scr, kn_scr, v_scr, o_ref, lse_ref = refs
        else:
            o_ref, lse_ref, qn_scr, kn_scr, v_scr = refs
        slope = sl_ref[:, 0:1]

        def residues(t, carry):
            rs = [t * rways + u for u in range(rways)]
            for u, r in enumerate(rs):
                sel = _da_residue_rows(r, 0, ld, d)
                qn_scr[u] = _da_rmsnorm(q_ref[sel, :], gq_ref[...])[2].astype(BF16)
                kn_scr[u] = _da_rmsnorm(k_ref[sel, :], gk_ref[...])[2].astype(BF16)
                v_scr[u] = v_ref[sel, :].astype(BF16)

            def block(u, i):
                s, _, _, _, win = _da_scores(qn_scr.at[u], kn_scr.at[u], slope, i, ld)
                m = jnp.max(s, axis=-1, keepdims=True)
                p = jnp.exp(s - m)
                l = jnp.sum(p, axis=-1, keepdims=True)
                return _dot(p, v_scr[u, win, :], _NN) / l, jnp.broadcast_to(m + jnp.log(l), (DA_QB, HEAD_DIM))

            def step(i, c2):
                todo = [(u, r, i + w * (nqb // ways)) for u, r in enumerate(rs) for w in range(ways)]
                for (u, r, b), (o, lse) in zip(todo, [block(u, b) for u, _, b in todo]):
                    out = _da_residue_rows(r, b * DA_QB, DA_QB, d)
                    o_ref[out, :] = o
                    lse_ref[out, :] = lse
                return c2

            return lax.fori_loop(0, nqb // ways, step, carry)

        lax.fori_loop(0, d // rways, residues, 0)

        if others:
            pieces = 8
            rows = SEQ // pieces

            def merge(t, carry):
                r = pl.ds(pl.multiple_of(t * rows, rows), rows)
                outs = [ref[r, :] for ref in other_refs[0::2]] + [o_ref[r, :]]
                lses = [ref[r, :] for ref in other_refs[1::2]] + [lse_ref[r, :]]
                m = lses[0]
                for l in lses[1:]:
                    m = jnp.maximum(m, l)
                es = [jnp.exp(l - m) for l in lses]
                tot = sum(es[1:], es[0])
                o = sum((e * v for e, v in zip(es[1:], outs[1:])), es[0] * outs[0]) / tot
                of_ref[r, :] = o
                ob_ref[r, :] = o.astype(BF16)
                lsej_ref[r, :] = m + jnp.log(tot)
                return carry

            lax.fori_loop(0, pieces, merge, 0)

    seq_spec = lambda base: pl.BlockSpec((SEQ, HEAD_DIM), lambda h: (0, base + 4 * group + h))
    out_spec = pl.BlockSpec((SEQ, HEAD_DIM), lambda h: (0, h))
    gain_spec = pl.BlockSpec((1, HEAD_DIM), lambda h: (0, 0))
    blk = SEQ * HEAD_DIM * 4
    fshape = jax.ShapeDtypeStruct((SEQ, DA_WIDTH), F32)
    scratch = [pltpu.VMEM((rways, ld, HEAD_DIM), BF16)] * 3
    if others:
        out_shape = [jax.ShapeDtypeStruct((SEQ, DA_WIDTH), BF16), fshape, fshape]
        scratch += [pltpu.VMEM((SEQ, HEAD_DIM), F32)] * 2
    else:
        out_shape = [fshape, fshape]
    return pl.pallas_call(
        body, name=f"da_fwd_g{group}", grid=(DA_HEADS_PER_GROUP,),
        in_specs=[seq_spec(CB_DA_Q), seq_spec(CB_DA_K), seq_spec(CB_DA_V), gain_spec, gain_spec,
                  pl.BlockSpec((None, 1, LANE), lambda h: (h, 0, 0))] + [out_spec] * n_other,
        out_specs=[out_spec] * len(out_shape), out_shape=out_shape, scratch_shapes=scratch,
        compiler_params=_params(("parallel",), 10 * blk + 2 * blk + (n_other + 3) * blk),
    )(proj, proj, proj, gq, gk, _da_slopes(group), *[a for pair in others for a in pair])


def _da_bwd(proj, gq, gk, do, lse, dd, dproj, group):
    d = DA_DILATIONS[group]
    ld = SEQ // d
    nqb = ld // DA_QB
    ways = min(DA_WAYS, nqb)
    rways = _da_residue_ways(d, nqb)

    def body(q_ref, k_ref, v_ref, gq_ref, gk_ref, sl_ref, do_ref, lse_ref, dd_ref, dproj_in,
             dproj_ref, dgq_ref, dgk_ref,
             qn_scr, kn_scr, v_scr, dqn_scr, dkn_scr, dvr_scr, rq_scr, rk_scr, dq_scr, dk_scr, dv_scr,
             dq_out, dk_out, dv_out, sems):
        del dproj_in
        head = pl.program_id(0)
        gqv, gkv = gq_ref[...], gk_ref[...]
        slope = sl_ref[:, 0:1]

        copies = []
        for k, (staging, base) in enumerate(((dq_out, CB_DA_Q), (dk_out, CB_DA_K), (dv_out, CB_DA_V))):
            cols = pl.ds(pl.multiple_of((base + 4 * group + head) * LANE, LANE), LANE)
            copies.append(pltpu.make_async_copy(staging, dproj_ref.at[:, cols], sems.at[k]))

        def residues(t, carry):
            rs = [t * rways + u for u in range(rways)]
            sels = [_da_residue_rows(r, 0, ld, d) for r in rs]
            for u, sel in enumerate(sels):
                for x_ref, gv, n_scr, r_scr in ((q_ref, gqv, qn_scr, rq_scr), (k_ref, gkv, kn_scr, rk_scr)):
                    _, rstd, normed = _da_rmsnorm(x_ref[sel, :], gv)
                    n_scr[u] = normed.astype(BF16)
                    r_scr[u] = jnp.broadcast_to(rstd, (ld, HEAD_DIM))
                v_scr[u] = v_ref[sel, :].astype(BF16)
            dkn_scr[...] = jnp.zeros_like(dkn_scr)
            dvr_scr[...] = jnp.zeros_like(dvr_scr)

            def block(u, r, i):
                s, qb, kw, qrows, win = _da_scores(qn_scr.at[u], kn_scr.at[u], slope, i, ld)
                src = _da_residue_rows(r, i * DA_QB, DA_QB, d)
                p = jnp.exp(s - lse_ref[src, :][:, 0:1])
                dob = do_ref[src, :]
                dp = _dot(dob, v_scr[u, win, :], _NT)
                ds = p * (dp - dd_ref[src, :][:, 0:1]) * DA_SCALE
                return u, qrows, win, _dot(p, dob, _TN), _dot(ds, kw, _NN), _dot(ds, qb, _TN)

            def step(i, c2):
                todo = [(u, r, i + w * (nqb // ways)) for u, r in enumerate(rs) for w in range(ways)]
                for u, qrows, win, dv, dqn, dkn in [block(*t) for t in todo]:
                    dvr_scr[u, win, :] += dv
                    dqn_scr[u, qrows, :] = dqn
                    dkn_scr[u, win, :] += dkn
                return c2

            lax.fori_loop(0, nqb // ways, step, 0)

            pq, pk = carry
            for u, sel in enumerate(sels):
                parts = []
                for x_ref, gv, dn_scr, dx_scr, r_scr in ((q_ref, gqv, dqn_scr, dq_scr, rq_scr),
                                                         (k_ref, gkv, dkn_scr, dk_scr, rk_scr)):
                    rstd = r_scr[u]
                    hat = x_ref[sel, :] * rstd
                    dn = dn_scr[u]
                    dyg = dn * gv
                    dx_scr[sel, :] = rstd * (dyg - hat * jnp.mean(dyg * hat, axis=-1, keepdims=True))
                    parts.append(jnp.sum(dn * hat, axis=0, keepdims=True))
                dv_scr[sel, :] = dvr_scr[u]
                pq, pk = pq + parts[0], pk + parts[1]
            return pq, pk

        zero = jnp.zeros((1, HEAD_DIM), F32)
        pq, pk = lax.fori_loop(0, d // rways, residues, (zero, zero))

        @pl.when(pl.program_id(0) == 0)
        def _():
            dgq_ref[...] = pq
            dgk_ref[...] = pk

        @pl.when(pl.program_id(0) > 0)
        def _():
            dgq_ref[...] += pq
            dgk_ref[...] += pk

        @pl.when(head > 0)
        def _():
            for cp in copies:
                cp.wait()

        dq_out[...] = dq_scr[...].astype(BF16)
        dk_out[...] = dk_scr[...].astype(BF16)
        dv_out[...] = dv_scr[...].astype(BF16)
        for cp in copies:
            cp.start()

        @pl.when(head == DA_HEADS_PER_GROUP - 1)
        def _():
            for cp in copies:
                cp.wait()

    seq_spec = lambda base: pl.BlockSpec((SEQ, HEAD_DIM), lambda h: (0, base + 4 * group + h))
    out_spec = pl.BlockSpec((SEQ, HEAD_DIM), lambda h: (0, h))
    gain_spec = pl.BlockSpec((1, HEAD_DIM), lambda h: (0, 0))
    gshape = jax.ShapeDtypeStruct((1, HEAD_DIM), F32)
    blk = SEQ * HEAD_DIM * 4
    return pl.pallas_call(
        body, name=f"da_bwd_g{group}", grid=(DA_HEADS_PER_GROUP,),
        in_specs=[seq_spec(CB_DA_Q), seq_spec(CB_DA_K), seq_spec(CB_DA_V), gain_spec, gain_spec,
                  pl.BlockSpec((None, 1, LANE), lambda h: (h, 0, 0)), out_spec, out_spec, out_spec, _any_spec()],
        out_specs=[_any_spec(), gain_spec, gain_spec],
        out_shape=[jax.ShapeDtypeStruct((SEQ, IN_COLS), BF16), gshape, gshape],
        scratch_shapes=[pltpu.VMEM((rways, ld, HEAD_DIM), BF16)] * 3 + [pltpu.VMEM((rways, ld, HEAD_DIM), F32)] * 5
        + [pltpu.VMEM((SEQ, HEAD_DIM), F32)] * 3 + [pltpu.VMEM((SEQ, HEAD_DIM), BF16)] * 3
        + [pltpu.SemaphoreType.DMA((3,))],
        input_output_aliases={9: 0},
        compiler_params=_params(("arbitrary",), 12 * blk + 3 * blk + 3 * blk + 3 * blk + 2 * blk),
    )(proj, proj, proj, gq, gk, _da_slopes(group), do, lse, dd, dproj)


MEM_WAYS = 4


def _mem_probs(q, kn, gq):
    qhat, rq, qn = _da_rmsnorm(q, gq)
    s = _dot(qn, kn, _NT) * MEM_SCALE
    m = jnp.max(s, axis=-1, keepdims=True)
    e = jnp.exp(s - m)
    p = e / jnp.sum(e, axis=-1, keepdims=True)
    return p, qhat, rq, qn


def _mem_pieces(tq):
    rows = tq // MEM_WAYS
    return [slice(w * rows, (w + 1) * rows) for w in range(MEM_WAYS)]


def _mem_fwd(proj, kv, gq, gk, tq):
    def body(q_ref, k_ref, v_ref, gq_ref, gk_ref, o_ref):
        kn = _da_rmsnorm(k_ref[...], gk_ref[...])[2]
        v, gqv = v_ref[...], gq_ref[...]
        pieces = _mem_pieces(tq)
        outs = [_dot(_mem_probs(q_ref[rs, :], kn, gqv)[0], v, _NN) for rs in pieces]
        for rs, o in zip(pieces, outs):
            o_ref[rs, :] = o.astype(BF16)

    gain_spec = pl.BlockSpec((1, HEAD_DIM), lambda h, i: (0, 0))
    return pl.pallas_call(
        body, name="mem_fwd", grid=(MEM_HEADS, SEQ // tq),
        in_specs=[pl.BlockSpec((tq, HEAD_DIM), lambda h, i: (i, CB_MEM_Q + h)),
                  pl.BlockSpec((N_MEM, HEAD_DIM), lambda h, i: (0, h)),
                  pl.BlockSpec((N_MEM, HEAD_DIM), lambda h, i: (0, MEM_HEADS + h)), gain_spec, gain_spec],
        out_specs=pl.BlockSpec((tq, HEAD_DIM), lambda h, i: (i, h)),
        out_shape=jax.ShapeDtypeStruct((SEQ, MEM_WIDTH), BF16),
        compiler_params=_params(("parallel", "parallel"), 16 * tq * N_MEM * 4),
    )(proj, kv, kv, gq, gk)


def _mem_bwd(proj, kv, gq, gk, do, dproj, tq):
    nq = SEQ // tq

    def body(q_ref, k_ref, v_ref, gq_ref, gk_ref, do_ref, dproj_in, dq_ref, dk_ref, dv_ref, dgq_ref, dgk_ref, dkn_scr):
        del dproj_in
        h, i = pl.program_id(0), pl.program_id(1)
        gqv, gkv = gq_ref[...], gk_ref[...]
        khat, rk, kn = _da_rmsnorm(k_ref[...], gkv)
        v = v_ref[...]

        def piece(rs):
            p, qhat, rq, qn = _mem_probs(q_ref[rs, :], kn, gqv)
            dob = do_ref[rs, :]
            dp = _dot(dob, v, _NT)
            ds = p * (dp - jnp.sum(p * dp, axis=-1, keepdims=True)) * MEM_SCALE
            dqn = _dot(ds, kn, _NN)
            dyg = dqn * gqv
            dq = (rq * (dyg - qhat * jnp.mean(dyg * qhat, axis=-1, keepdims=True))).astype(BF16)
            return dq, _dot(p, dob, _TN), _dot(ds, qn, _TN), jnp.sum(dqn * qhat, axis=0, keepdims=True)

        pieces = _mem_pieces(tq)
        results = [piece(rs) for rs in pieces]
        for rs, res in zip(pieces, results):
            dq_ref[rs, :] = res[0]
        dvp = sum((res[1] for res in results[1:]), results[0][1])
        dknp = sum((res[2] for res in results[1:]), results[0][2])
        dgq_part = sum((res[3] for res in results[1:]), results[0][3])
        first = jnp.logical_and(h == 0, i == 0)

        @pl.when(first)
        def _():
            dgq_ref[...] = dgq_part

        @pl.when(jnp.logical_not(first))
        def _():
            dgq_ref[...] += dgq_part

        @pl.when(i == 0)
        def _():
            dv_ref[...] = dvp
            dkn_scr[...] = dknp

        @pl.when(i > 0)
        def _():
            dv_ref[...] += dvp
            dkn_scr[...] += dknp

        @pl.when(i == nq - 1)
        def _():
            dkn = dkn_scr[...]
            dkg = dkn * gkv
            dk_ref[...] = rk * (dkg - khat * jnp.mean(dkg * khat, axis=-1, keepdims=True))
            dgk_part = jnp.sum(dkn * khat, axis=0, keepdims=True)

            @pl.when(h == 0)
            def _():
                dgk_ref[...] = dgk_part

            @pl.when(h > 0)
            def _():
                dgk_ref[...] += dgk_part

    gain_spec = pl.BlockSpec((1, HEAD_DIM), lambda h, i: (0, 0))
    kvout = pl.BlockSpec((N_MEM, HEAD_DIM), lambda h, i: (0, h))
    return pl.pallas_call(
        body, name="mem_bwd", grid=(MEM_HEADS, nq),
        in_specs=[pl.BlockSpec((tq, HEAD_DIM), lambda h, i: (i, CB_MEM_Q + h)),
                  pl.BlockSpec((N_MEM, HEAD_DIM), lambda h, i: (0, h)),
                  pl.BlockSpec((N_MEM, HEAD_DIM), lambda h, i: (0, MEM_HEADS + h)), gain_spec, gain_spec,
                  pl.BlockSpec((tq, HEAD_DIM), lambda h, i: (i, h)), _any_spec()],
        out_specs=[pl.BlockSpec((tq, HEAD_DIM), lambda h, i: (i, CB_MEM_Q + h)), kvout, kvout, gain_spec, gain_spec],
        out_shape=[jax.ShapeDtypeStruct((SEQ, IN_COLS), BF16), jax.ShapeDtypeStruct((N_MEM, MEM_WIDTH), F32),
                   jax.ShapeDtypeStruct((N_MEM, MEM_WIDTH), F32), jax.ShapeDtypeStruct((1, HEAD_DIM), F32),
                   jax.ShapeDtypeStruct((1, HEAD_DIM), F32)],
        scratch_shapes=[pltpu.VMEM((N_MEM, HEAD_DIM), F32)], input_output_aliases={6: 0},
        compiler_params=_params(("arbitrary", "arbitrary"), 24 * tq * N_MEM * 4),
    )(proj, kv, kv, gq, gk, do, dproj)


def _mesh_position():
    return lax.axis_index("x"), lax.axis_index("y"), lax.axis_index("c")


def _any_spec():
    return pl.BlockSpec(memory_space=pl.ANY)


def _all_gather(shards):
    n = len(shards)

    def body(*refs):
        ins, outs = refs[:n], refs[n:2 * n]
        send_sems, recv_sems, local_sems = refs[2 * n:]
        x, y, c = _mesh_position()
        me, sibling = (x, y, c), (x, y, 1 - c)
        first = (jnp.where(c == 0, 1 - x, x), jnp.where(c == 0, y, 1 - y), c)
        second = (jnp.where(c == 0, x, 1 - x), jnp.where(c == 0, 1 - y, y), c)
        diagonal = (1 - x, 1 - y, c)

        def copy(w, k, block, to, src=None):
            px, py, pc = block
            rows = outs[w].at[4 * px + 2 * py + pc]
            return pltpu.make_async_remote_copy(
                src_ref=rows if src is None else src, dst_ref=rows,
                send_sem=send_sems.at[7 * w + k], recv_sem=recv_sems.at[7 * w + k],
                device_id=to, device_id_type=MESH)

        started = []
        for w in range(n):
            mine = pltpu.make_async_copy(ins[w], outs[w].at[4 * x + 2 * y + c], local_sems.at[w])
            mine.start()
            started.append(mine)
        sends = []
        for w in range(n):
            own = [copy(w, 0, me, sibling, src=ins[w]), copy(w, 1, me, first, src=ins[w]),
                   copy(w, 2, me, second, src=ins[w])]
            for cp in own:
                cp.start()
            sends += own
        for w in range(n):
            copy(w, 1, first, me).wait_recv()
            follow = [copy(w, 3, first, second), copy(w, 4, first, sibling)]
            for cp in follow:
                cp.start()
            copy(w, 2, second, me).wait_recv()
            follow.append(copy(w, 5, second, sibling))
            follow[-1].start()
            sends += follow
        for w in range(n):
            copy(w, 3, diagonal, me).wait_recv()
            passed = copy(w, 6, diagonal, sibling)
            passed.start()
            sends.append(passed)
        for w in range(n):
            for k in (0, 4, 5, 6):
                copy(w, k, sibling, me).wait_recv()
        for cp in sends:
            cp.wait_send()
        for mine in started:
            mine.wait()

    return pl.pallas_call(
        body, name="weights_all_gather",
        in_specs=[_any_spec()] * n, out_specs=[_any_spec()] * n,
        out_shape=[jax.ShapeDtypeStruct((N_DEV,) + s.shape, s.dtype) for s in shards],
        scratch_shapes=[pltpu.SemaphoreType.DMA((7 * n,)), pltpu.SemaphoreType.DMA((7 * n,)),
                        pltpu.SemaphoreType.DMA((n,))],
    )(*shards)


def _chip_of(j, x, y):
    return (1 - x if j & 1 else x, 1 - y if j & 2 else y)


_HBM_SPEC = pl.BlockSpec(memory_space=pltpu.HBM)
_SEM_SPEC = pl.BlockSpec(memory_space=pltpu.SEMAPHORE)
_DATAFLOW_EFFECT = pltpu.SideEffectType.DATAFLOW_SIDE_EFFECTING
TOKEN_SHAPE = (8, D_MODEL)


def _copies_start(name, arrays, n_copies, plan):
    n = len(arrays)

    def body(*refs):
        send_sems, recv_sems, token = refs[n], refs[n + 1], refs[2 * n + 2]
        copies = plan(refs[:n])
        assert len(copies) == n_copies
        for k, (src, dst, dev) in enumerate(copies):
            pltpu.make_async_remote_copy(src_ref=src, dst_ref=dst, send_sem=send_sems.at[k], recv_sem=recv_sems.at[k],
                                         device_id=dev, device_id_type=MESH).start()
        token[...] = jnp.zeros_like(token)

    outs = pl.pallas_call(
        body, name=name,
        out_shape=(pltpu.SemaphoreType.DMA((n_copies,)), pltpu.SemaphoreType.DMA((n_copies,)),
                   *[pltpu.HBM(a.shape, a.dtype) for a in arrays], jax.ShapeDtypeStruct(TOKEN_SHAPE, F32)),
        in_specs=[_HBM_SPEC] * n,
        out_specs=(_SEM_SPEC, _SEM_SPEC, *[_HBM_SPEC] * n, pl.BlockSpec(memory_space=pltpu.VMEM)),
        input_output_aliases={i: i + 2 for i in range(n)},
        compiler_params=pltpu.CompilerParams(has_side_effects=_DATAFLOW_EFFECT),
    )(*[pltpu.with_memory_space_constraint(a, pltpu.HBM) for a in arrays])
    return outs[0], outs[1], list(outs[2:2 + n]), outs[2 + n]


def _copies_wait(name, send_sems, recv_sems, arrays, n_copies, plan, after):
    n = len(arrays)
    after = list(after) if isinstance(after, (list, tuple)) else [after]

    def body(*refs):
        send_ref, recv_ref = refs[n], refs[n + 1]
        copies = plan(refs[:n])
        assert len(copies) == n_copies
        for k, (src, dst, dev) in enumerate(copies):
            cp = pltpu.make_async_remote_copy(src_ref=src, dst_ref=dst, send_sem=send_ref.at[k], recv_sem=recv_ref.at[k],
                                              device_id=dev, device_id_type=MESH)
            cp.wait_send()
            cp.wait_recv()

    outs = pl.pallas_call(
        body, name=name, out_shape=tuple(pltpu.HBM(a.shape, a.dtype) for a in arrays),
        in_specs=[_HBM_SPEC] * n + [_SEM_SPEC, _SEM_SPEC] + [pl.BlockSpec(memory_space=pl.ANY)] * len(after),
        out_specs=tuple([_HBM_SPEC] * n), input_output_aliases={i: i for i in range(n)},
        compiler_params=pltpu.CompilerParams(has_side_effects=_DATAFLOW_EFFECT),
    )(*arrays, send_sems, recv_sems, *after)
    return list(outs)


def _after(small, token):
    return small if token is None else small + token[0:1, :small.shape[-1]]


def _gather_plan_out(n):
    def plan(refs):
        x, y, c = _mesh_position()
        me = 4 * x + 2 * y + c
        copies = []
        for w in range(n):
            land = refs[n + w].at[me]
            copies.append((refs[w], land, (x, y, 1 - c)))
            for j in range(1, 4):
                copies.append((refs[w], land, (*_chip_of(j, x, y), c)))
        return copies
    return plan


def _gather_plan_pass(n):
    def plan(refs):
        x, y, c = _mesh_position()
        copies = []
        for w in range(n):
            for j in range(1, 4):
                px, py = _chip_of(j, x, y)
                rows = refs[w].at[4 * px + 2 * py + c]
                copies.append((rows, rows, (x, y, 1 - c)))
        return copies
    return plan


def _reduce_plan_sibling(n):
    def plan(refs):
        x, y, c = _mesh_position()
        copies = []
        for w in range(n):
            for j in range(4):
                px, py = _chip_of(j, x, y)
                copies.append((refs[w].at[4 * px + 2 * py + (1 - c)], refs[n + w].at[j], (x, y, 1 - c)))
        return copies
    return plan


def _reduce_plan_chips(n):
    def plan(refs):
        x, y, c = _mesh_position()
        copies = []
        for w in range(n):
            for j in range(1, 4):
                copies.append((refs[w].at[j - 1], refs[n + w].at[j - 1], (*_chip_of(j, x, y), c)))
        return copies
    return plan


def _chip_partials(grad, recv, name, tr):
    _, rows, width = grad.shape

    def body(g_ref, r_ref, own_ref, other_ref):
        x, y, c = _mesh_position()
        for j in range(4):
            px, py = _chip_of(j, x, y)
            total = g_ref[4 * px + 2 * py + c].astype(F32) + r_ref[j].astype(F32)
            if j == 0:
                own_ref[...] = total
            else:
                other_ref[j - 1] = total.astype(BF16)

    return pl.pallas_call(
        body, name=name, grid=(rows // tr,),
        in_specs=[pl.BlockSpec((N_DEV, tr, width), lambda i: (0, i, 0)),
                  pl.BlockSpec((4, tr, width), lambda i: (0, i, 0))],
        out_specs=[pl.BlockSpec((tr, width), lambda i: (i, 0)), pl.BlockSpec((3, tr, width), lambda i: (0, i, 0))],
        out_shape=[jax.ShapeDtypeStruct((rows, width), F32), jax.ShapeDtypeStruct((3, rows, width), BF16)],
        compiler_params=_params(("parallel",), 2 * 20 * tr * width * 2 + 8 * tr * width * 4),
    )(grad, recv)


def _adamw_math(w, g, m, v):
    m = ADAM_B1 * m + (1.0 - ADAM_B1) * g
    v = ADAM_B2 * v + (1.0 - ADAM_B2) * (g * g)
    m_hat = m / (1.0 - ADAM_B1 ** ADAM_STEP)
    v_hat = v / (1.0 - ADAM_B2 ** ADAM_STEP)
    delta = -ADAM_LR * (m_hat / (jnp.sqrt(v_hat) + ADAM_EPS) + ADAM_WD * w)
    return delta, m, v


def _adamw_shard(own, recv, w, m, v, name, tr):
    rows, width = own.shape

    def body(own_ref, r_ref, w_ref, m_ref, v_ref, g_ref, d_ref, nm_ref, nv_ref):
        g = own_ref[...]
        for j in range(3):
            g = g + r_ref[j].astype(F32)
        g_ref[...] = g
        d_ref[...], nm_ref[...], nv_ref[...] = _adamw_math(w_ref[...], g, m_ref[...], v_ref[...])

    spec = pl.BlockSpec((tr, width), lambda i: (i, 0))
    shape = jax.ShapeDtypeStruct((rows, width), F32)
    return pl.pallas_call(
        body, name=name, grid=(rows // tr,),
        in_specs=[spec, pl.BlockSpec((3, tr, width), lambda i: (0, i, 0)), spec, spec, spec],
        out_specs=[spec] * 4, out_shape=[shape] * 4,
        compiler_params=_params(("parallel",), 22 * tr * width * 4),
    )(own, recv, w, m, v)


def _small_all_reduce_adamw(gpart, lbpack, wpack, mpack, vpack, after):
    def body(gp_ref, lb_ref, w_ref, m_ref, v_ref, after_ref, g_ref, d_ref, nm_ref, nv_ref, gath_ref, send_sems, recv_sems):
        del after_ref
        x, y, c = _mesh_position()
        me = 4 * x + 2 * y + c
        gath_ref[me] = gp_ref[...]
        copies = []
        for j in range(1, N_DEV):
            peer = (x ^ (j >> 2), y ^ ((j >> 1) & 1), c ^ (j & 1))
            cp = pltpu.make_async_remote_copy(
                src_ref=gp_ref, dst_ref=gath_ref.at[me], send_sem=send_sems.at[j - 1], recv_sem=recv_sems.at[j - 1],
                device_id=peer, device_id_type=MESH)
            cp.start()
            copies.append(cp)
        for cp in copies:
            cp.wait()
        tot = gath_ref[0]
        for s in range(1, N_DEV):
            tot = tot + gath_ref[s]
        lb = lb_ref[...]
        dl = tot[16:32, :] * lb * (1.0 - lb)
        g = jnp.concatenate([tot[0:16, :], dl[0:8, :], -dl[0:8, :], dl[8:16, :], -dl[8:16, :],
                             tot[32:SMALL_GRAD_ROWS, :]], axis=0)
        g_ref[...] = g
        d_ref[...], nm_ref[...], nv_ref[...] = _adamw_math(w_ref[...], g, m_ref[...], v_ref[...])

    vm = pl.BlockSpec(memory_space=pltpu.VMEM)
    shape = jax.ShapeDtypeStruct((SMALL_ROWS, LANE), F32)
    return pl.pallas_call(
        body, name="small_all_reduce_adamw", in_specs=[vm] * 5 + [_any_spec()], out_specs=[vm] * 4,
        out_shape=[shape] * 4,
        scratch_shapes=[pltpu.VMEM((N_DEV, SMALL_GRAD_ROWS, LANE), F32),
                        pltpu.SemaphoreType.DMA((N_DEV - 1,)), pltpu.SemaphoreType.DMA((N_DEV - 1,))],
    )(gpart, lbpack, wpack, mpack, vpack, after)


_SMALL_NAMES = ("norm_mix_gain", "norm_mem_gain", "lb_logits_fw", "lb_logits_bw", "norm_ffn_gain",
                "hg_norm_gain", "da_q_gain", "da_k_gain", "mem_q_gain", "mem_k_gain")
_SMALL_ROW0 = {"norm_mix_gain": 0, "norm_mem_gain": 8, "lb_logits_fw": 16, "lb_logits_bw": 32, "norm_ffn_gain": 48,
               "hg_norm_gain": 56, "da_q_gain": 64, "da_k_gain": 72, "mem_q_gain": 80, "mem_k_gain": 88}
_LOSS_ROW = 96


def _pack_rows(parts, total_rows):
    rows = []
    for p in parts:
        r = p.reshape(-1, LANE)
        rows.append(jnp.pad(r, ((0, -r.shape[0] % 8), (0, 0))))
    used = sum(r.shape[0] for r in rows)
    if total_rows > used:
        rows.append(jnp.zeros((total_rows - used, LANE), F32))
    return jnp.concatenate(rows, axis=0)


def _unpack_small(pack, like):
    out = {}
    for name in _SMALL_NAMES:
        n = like[name].size // LANE
        r0 = _SMALL_ROW0[name]
        out[name] = pack[r0:r0 + n].reshape(like[name].shape)
    return out


def kernel(x, mem, norm_mix_gain, norm_mem_gain, w_in, lb_logits_fw, lb_logits_bw, hg_norm_gain, da_q_gain, da_k_gain, w_mem_kv, mem_q_gain, mem_k_gain, w_proj_hg, w_proj_da, w_proj_mem, w_out, norm_ffn_gain, w_ffn_in, w_ffn_out, loss_target, m_norm_mix_gain, m_norm_mem_gain, m_w_in, m_lb_logits_fw, m_lb_logits_bw, m_hg_norm_gain, m_da_q_gain, m_da_k_gain, m_w_mem_kv, m_mem_q_gain, m_mem_k_gain, m_w_proj_hg, m_w_proj_da, m_w_proj_mem, m_w_out, m_norm_ffn_gain, m_w_ffn_in, m_w_ffn_out, v_norm_mix_gain, v_norm_mem_gain, v_w_in, v_lb_logits_fw, v_lb_logits_bw, v_hg_norm_gain, v_da_q_gain, v_da_k_gain, v_w_mem_kv, v_mem_q_gain, v_mem_k_gain, v_w_proj_hg, v_w_proj_da, v_w_proj_mem, v_w_out, v_norm_ffn_gain, v_w_ffn_in, v_w_ffn_out):
    small_w = dict(norm_mix_gain=norm_mix_gain, norm_mem_gain=norm_mem_gain, lb_logits_fw=lb_logits_fw,
                   lb_logits_bw=lb_logits_bw, norm_ffn_gain=norm_ffn_gain, hg_norm_gain=hg_norm_gain,
                   da_q_gain=da_q_gain, da_k_gain=da_k_gain, mem_q_gain=mem_q_gain, mem_k_gain=mem_k_gain)
    small_m = dict(norm_mix_gain=m_norm_mix_gain, norm_mem_gain=m_norm_mem_gain, lb_logits_fw=m_lb_logits_fw,
                   lb_logits_bw=m_lb_logits_bw, norm_ffn_gain=m_norm_ffn_gain, hg_norm_gain=m_hg_norm_gain,
                   da_q_gain=m_da_q_gain, da_k_gain=m_da_k_gain, mem_q_gain=m_mem_q_gain, mem_k_gain=m_mem_k_gain)
    small_v = dict(norm_mix_gain=v_norm_mix_gain, norm_mem_gain=v_norm_mem_gain, lb_logits_fw=v_lb_logits_fw,
                   lb_logits_bw=v_lb_logits_bw, norm_ffn_gain=v_norm_ffn_gain, hg_norm_gain=v_hg_norm_gain,
                   da_q_gain=v_da_q_gain, da_k_gain=v_da_k_gain, mem_q_gain=v_mem_q_gain, mem_k_gain=v_mem_k_gain)
    big_w = dict(w_in=w_in[0], w_mem_kv=w_mem_kv[0], w_proj_hg=w_proj_hg[0], w_proj_da=w_proj_da[0],
                 w_proj_mem=w_proj_mem[0], w_out=w_out[0], w_ffn_in=w_ffn_in[0], w_ffn_out=w_ffn_out[0])
    big_m = dict(w_in=m_w_in[0], w_mem_kv=m_w_mem_kv[0], w_proj_hg=m_w_proj_hg[0], w_proj_da=m_w_proj_da[0],
                 w_proj_mem=m_w_proj_mem[0], w_out=m_w_out[0], w_ffn_in=m_w_ffn_in[0], w_ffn_out=m_w_ffn_out[0])
    big_v = dict(w_in=v_w_in[0], w_mem_kv=v_w_mem_kv[0], w_proj_hg=v_w_proj_hg[0], w_proj_da=v_w_proj_da[0],
                 w_proj_mem=v_w_proj_mem[0], w_out=v_w_out[0], w_ffn_in=v_w_ffn_in[0], w_ffn_out=v_w_ffn_out[0])

    row_tile = dict(w_in=128, w_mem_kv=128, w_proj_hg=128, w_proj_da=512, w_proj_mem=512, w_out=128,
                    w_ffn_in=128, w_ffn_out=352)
    rest = _BIG_NAMES[1:]
    shards = [big_w[n].astype(BF16) for n in rest]
    state = {}
    big_out = {}

    def reduce_start(group, stacked):
        names = tuple(stacked)
        arrays = [stacked[n] for n in names] + [lax.empty((4,) + stacked[n].shape[1:], BF16) for n in names]
        plan = _reduce_plan_sibling(len(names))
        send, recv, thru, token = _copies_start(f"grads_{group}_sibling_start", arrays, 4 * len(names), plan)
        state[group] = dict(names=names, plan=plan, send=send, recv=recv, arrays=thru)
        return token

    def reduce_middle(group, after):
        st = state[group]
        names, k = st["names"], len(st["names"])
        thru = _copies_wait(f"grads_{group}_sibling_wait", st["send"], st["recv"], st["arrays"], 4 * k, st["plan"], after)
        partials = [_chip_partials(thru[i], thru[k + i], f"chip_partials_{n}", row_tile[n]) for i, n in enumerate(names)]
        arrays = [p[1] for p in partials] + [lax.empty(p[1].shape, BF16) for p in partials]
        plan = _reduce_plan_chips(k)
        send, recv, thru2, token = _copies_start(f"grads_{group}_chips_start", arrays, 3 * k, plan)
        state[group] = dict(names=names, plan=plan, send=send, recv=recv, arrays=thru2, own=[p[0] for p in partials])
        return token

    def reduce_finish(group, after):
        st = state.pop(group)
        names, k = st["names"], len(st["names"])
        thru = _copies_wait(f"grads_{group}_chips_wait", st["send"], st["recv"], st["arrays"], 3 * k, st["plan"], after)
        for i, n in enumerate(names):
            big_out[n] = _adamw_shard(st["own"][i], thru[k + i], big_w[n], big_m[n], big_v[n], "adamw_" + n, row_tile[n])

    step = _local_step_stages(x[0], mem[0], loss_target[0], small_w)
    event, payload = next(step)
    local = None
    while True:
        reply = None
        if event == "gather_w_in":
            reply = _all_gather([big_w["w_in"].astype(BF16)])[0]
        elif event == "begin":
            nr = len(rest)
            me = 4 * lax.axis_index("x") + 2 * lax.axis_index("y") + lax.axis_index("c")
            arrays = shards + [lax.dynamic_update_slice(lax.empty((N_DEV,) + s.shape, BF16), s[None], (me, 0, 0))
                               for s in shards]
            plan = _gather_plan_out(nr)
            send, recv, thru, reply = _copies_start("weights_rest_out_start", arrays, 4 * nr, plan)
            state["gather"] = dict(plan=plan, send=send, recv=recv, arrays=thru)
        elif event == "after_gla_fwd":
            st = state["gather"]
            nr = len(rest)
            thru = _copies_wait("weights_rest_out_wait", st["send"], st["recv"], st["arrays"], 4 * nr, st["plan"], payload)
            plan = _gather_plan_pass(nr)
            send, recv, lands, reply = _copies_start("weights_rest_pass_start", thru[nr:], 3 * nr, plan)
            state["gather"] = dict(plan=plan, send=send, recv=recv, arrays=lands)
        elif event == "need_weights":
            st = state.pop("gather")
            nr = len(rest)
            lands = _copies_wait("weights_rest_pass_wait", st["send"], st["recv"], st["arrays"], 3 * nr, st["plan"], payload)
            reply = dict(zip(rest, lands))
        elif event == "grads_ffn":
            reply = reduce_start("ffn", payload)
        elif event == "after_gate_merge_bwd":
            reply = reduce_middle("ffn", payload)
        elif event == "grads_mix":
            reply = reduce_start("mix", payload)
        elif event == "after_da_bwd_g0":
            reply = reduce_middle("mix", payload)
            reduce_finish("ffn", payload)
        elif event == "after_gla_bwd":
            reduce_finish("mix", payload)
        elif event == "grads_in":
            reply = reduce_start("in", payload)
        elif event == "after_proj_bwd_act_top":
            reply = reduce_middle("in", payload)
        elif event == "end":
            local = payload
            reduce_finish("in", [local["grad_x"]] + [big_out[n][0] for n in rest])
            break
        event, payload = step.send(reply)
    grad_x = local["grad_x"]

    wpack = _pack_rows([small_w[n] for n in _SMALL_NAMES], SMALL_ROWS)
    mpack = _pack_rows([small_m[n] for n in _SMALL_NAMES], SMALL_ROWS)
    vpack = _pack_rows([small_v[n] for n in _SMALL_NAMES], SMALL_ROWS)
    gs, ds, ms, vs = _small_all_reduce_adamw(local["gpart"], local["lbpack"], wpack, mpack, vpack, big_out["w_in"][0])
    loss = gs[_LOSS_ROW, 0]
    small_out = [_unpack_small(t, small_w) for t in (gs, ds, ms, vs)]

    order = ("norm_mix_gain", "norm_mem_gain", "w_in", "lb_logits_fw", "lb_logits_bw", "hg_norm_gain", "da_q_gain",
             "da_k_gain", "w_mem_kv", "mem_q_gain", "mem_k_gain", "w_proj_hg", "w_proj_da", "w_proj_mem", "w_out",
             "norm_ffn_gain", "w_ffn_in", "w_ffn_out")
    outs = [loss, grad_x[None]]
    for kind in range(4):
        for n in order:
            outs.append(big_out[n][kind][None] if n in big_out else small_out[kind][n])
    return tuple(outs)


_BIG_NAMES = ("w_in", "w_mem_kv", "w_proj_hg", "w_proj_da", "w_proj_mem", "w_out", "w_ffn_in", "w_ffn_out")


def _local_step(xs, mems, target, sw, wg):
    step = _local_step_stages(xs, mems, target, sw)
    stacked = {}
    event, payload = next(step)
    while event != "end":
        reply = None
        if event.startswith("grads_"):
            stacked.update(payload)
        elif event == "need_weights":
            reply = wg
        elif event == "gather_w_in":
            reply = wg["w_in"]
        event, payload = step.send(reply)
    return dict(payload, stacked=stacked)


def _local_step_stages(xs, mems, target, sw):
    norm_mix_gain, norm_mem_gain, norm_ffn_gain = sw["norm_mix_gain"], sw["norm_mem_gain"], sw["norm_ffn_gain"]
    lb_logits_fw, lb_logits_bw, hg_norm_gain = sw["lb_logits_fw"], sw["lb_logits_bw"], sw["hg_norm_gain"]
    da_q_gain, da_k_gain, mem_q_gain, mem_k_gain = sw["da_q_gain"], sw["da_k_gain"], sw["mem_q_gain"], sw["mem_k_gain"]

    win_st = yield "gather_w_in", None
    token = yield "begin", None
    lb_fw = _lb_table(lb_logits_fw, "lb_table_fw")
    lb_bw = _lb_table(lb_logits_bw, "lb_table_bw")
    lb = jnp.concatenate([lb_fw, lb_bw], axis=0).reshape(2, HG_HEADS, 1, HEAD_DIM)
    h, h_t = _rmsnorm_fwd(xs, _after(norm_mix_gain, token), "norm_mix_fwd", 512, transposed=True)
    proj = _matmul(h, win_st, "nn", F32, 1024, IN_SHARD, D_MODEL, "proj_fwd", b_stacked=True, n_outer=True)
    o_hg, o_pre, states, hg_qt, hg_decay, hg_b = _gla_fwd(proj, lb, hg_norm_gain)
    token = yield "after_gla_fwd", o_hg
    da01 = [_da_fwd(proj, _after(da_q_gain, token), da_k_gain, g) for g in (0, 1)]
    o_da, o_da32, lse_da = _da_fwd(proj, _after(da_q_gain, token), da_k_gain, 2, others=da01)
    wg = yield "need_weights", o_da
    wkv = wg["w_mem_kv"].reshape(D_MODEL, 2 * MEM_WIDTH)
    wphg = wg["w_proj_hg"].reshape(D_MODEL, D_MODEL)
    wpda = jnp.transpose(wg["w_proj_da"], (1, 0, 2)).reshape(DA_WIDTH, D_MODEL)
    wpmem = jnp.transpose(wg["w_proj_mem"], (1, 0, 2)).reshape(MEM_WIDTH, D_MODEL)
    wout = wg["w_out"].reshape(D_MODEL, D_MODEL)
    wfin = jnp.transpose(wg["w_ffn_in"], (1, 0, 2)).reshape(D_MODEL, 2 * D_FF)
    wfout = wg["w_ffn_out"].reshape(D_FF, D_MODEL)
    mem_n = _rmsnorm_fwd(mems, norm_mem_gain, "norm_mem_fwd", N_MEM)
    kv = _matmul(mem_n, wkv, "nn", F32, N_MEM, 1024, D_MODEL, "mem_kv_fwd")
    o_mem = _mem_fwd(proj, kv, mem_q_gain, mem_k_gain, 1024)
    branch_w = (wphg, wpda, wpmem)
    merged, t_hg, t_da, t_mem = _branch_merge_fwd(proj, (o_hg, o_da, o_mem), branch_w, 512)
    x1, h2, h2_t = _residual_rmsnorm_fwd(xs, merged, wout, norm_ffn_gain, "out_norm_ffn_fwd", 512)
    ffn_a, ffn_b, act, act_t = _ffn_in_swiglu(h2, wfin, 1024, 1408)
    dy, dyb, loss_part = _ffn_out_loss_head(x1, act, wfout, target, 512)

    dab = _ffn_out_bwd_swiglu(dyb, wfout, ffn_a, ffn_b, 1024, 1408)
    g_wfout = _matmul(act_t, dyb, "nn", BF16, 1408, 1024, 2048, "ffn_out_bwd_w")
    g_wfin = _matmul(h2_t, dab, "nn", BF16, 1024, 1408, 2048, "ffn_in_bwd_w", b_parts=True)
    token = yield "grads_ffn", dict(
        w_ffn_in=jnp.transpose(g_wfin.reshape(D_MODEL, N_DEV, 2 * D_FF // N_DEV), (1, 0, 2)),
        w_ffn_out=g_wfout.reshape(N_DEV, D_FF // N_DEV, D_MODEL))
    dx1, dx1b, g_norm_ffn = _matmul_rmsnorm_bwd(dab, wfin, x1, dy, _after(norm_ffn_gain, token),
                                                "ffn_in_bwd_act_norm", 512, D_FF, a_parts=True)
    g_wout = _matmul(merged, dx1b, "tn", BF16, 1024, 1024, 2048, "out_bwd_w")
    dt_hg, dt_da, dt_mem, do_pre, do_da, do_mem, dd_da, g_hg_norm, dproj = _gate_merge_bwd(
        proj, (t_hg, t_da, t_mem), branch_w, dx1b, wout, o_pre, hg_norm_gain, o_da32, _dproj_buffer(), 256)
    token = yield "after_gate_merge_bwd", dt_hg
    g_wphg = _matmul(o_hg, dt_hg, "tn", BF16, 1024, 1024, 2048, "proj_hg_bwd_w", after=token)
    g_wpda = _matmul(o_da, dt_da, "tn", BF16, DA_WIDTH, D_MODEL, 2048, "proj_da_bwd_w")
    g_wpmem = _matmul(o_mem, dt_mem, "tn", BF16, MEM_WIDTH, D_MODEL, 2048, "proj_mem_bwd_w")
    by_owner = lambda g: jnp.transpose(g.reshape(g.shape[0], N_DEV, D_MODEL // N_DEV), (1, 0, 2))
    g_wpda, g_wpmem = by_owner(g_wpda), by_owner(g_wpmem)

    dproj, dk_mem, dv_mem, g_mem_q, g_mem_k = _mem_bwd(proj, kv, mem_q_gain, mem_k_gain, do_mem, dproj, 1024)
    dkv = jnp.concatenate([dk_mem, dv_mem], axis=1).astype(BF16)
    g_wkv = _matmul(mem_n, dkv, "tn", BF16, 1024, 1024, N_MEM, "mem_kv_bwd_w")
    dmem_n = _matmul(dkv, wkv, "nt", F32, N_MEM, 1024, 1024, "mem_kv_bwd_act")
    g_norm_mem = _gain_grad(mems, dmem_n, "norm_mem_bwd")
    token = yield "grads_mix", dict(
        w_mem_kv=g_wkv.reshape(N_DEV, D_MODEL // N_DEV, 2 * MEM_WIDTH),
        w_proj_hg=g_wphg.reshape(N_DEV, D_MODEL // N_DEV, D_MODEL),
        w_proj_da=g_wpda, w_proj_mem=g_wpmem,
        w_out=g_wout.reshape(N_DEV, D_MODEL // N_DEV, D_MODEL))

    dproj, g_da_q, g_da_k = _da_bwd(proj, _after(da_q_gain, token), da_k_gain, do_da, lse_da, dd_da, dproj, 0)
    token = yield "after_da_bwd_g0", g_da_q
    for g in (1, 2):
        dproj, gq_part, gk_part = _da_bwd(proj, _after(da_q_gain, token), da_k_gain, do_da, lse_da, dd_da, dproj, g)
        g_da_q, g_da_k = g_da_q + gq_part, g_da_k + gk_part

    dproj, dlb = _gla_bwd(proj, lb, states, hg_qt, hg_decay, hg_b, do_pre, dproj)
    yield "after_gla_bwd", dlb

    g_win =_matmul(h_t, dproj, "nn", BF16, 1024, IN_SHARD, 2048, "proj_bwd_w", out_stacked=True)
    token = yield "grads_in", dict(w_in=g_win)
    grad_x, g_mix_top = _matmul_rmsnorm_bwd(dproj, win_st, xs, dx1, norm_mix_gain, "proj_bwd_act_norm_top", 1024,
                                            IN_SHARD, b_stacked=True, after=token, m_blocks=(0, 1), with_bf16=False)
    token = yield "after_proj_bwd_act_top", g_mix_top
    grad_x, g_mix_bottom = _matmul_rmsnorm_bwd(dproj, win_st, xs, dx1, norm_mix_gain, "proj_bwd_act_norm_bottom", 1024,
                                               IN_SHARD, b_stacked=True, after=token, m_blocks=(1, 3),
                                               out_into=grad_x, with_bf16=False)
    g_norm_mix = g_mix_top + g_mix_bottom

    gpart = _pack_rows([g_norm_mix, g_norm_mem, dlb[0], dlb[1], g_norm_ffn, g_hg_norm, g_da_q, g_da_k,
                        g_mem_q, g_mem_k, loss_part], SMALL_GRAD_ROWS)
    lbpack = jnp.concatenate([lb_fw.reshape(8, LANE), lb_bw.reshape(8, LANE)], axis=0)
    yield "end", dict(grad_x=grad_x, gpart=gpart, lbpack=lbpack)
```

```python
import numpy as np
import jax
import jax.numpy as jnp
from jax import lax
from jax.experimental import pallas as pl
from jax.experimental.pallas import tpu as pltpu

F32 = jnp.float32
BF16 = jnp.bfloat16
MESH = pl.DeviceIdType.MESH

SEQ = 4096
D_MODEL = 1024
N_DEV = 8
N_MEM = 256
RMS_EPS = 1e-6
NEG_INF = -1e30
LANE = 128
HEAD_DIM = 128
HG_HEADS = 8
HG_CHUNK = 64
HG_SCALE = HEAD_DIM ** -0.5
DA_DILATIONS = (1, 4, 16)
DA_RADIUS = 64
DA_HEADS_PER_GROUP = 4
DA_HEADS = 12
DA_WIDTH = 512
DA_SCALE = HEAD_DIM ** -0.5
DA_QB = 128
DA_WIN = 256
DA_WAYS = 4
DA_FWD_WAYS = 4
MEM_HEADS = 4
MEM_WIDTH = 512
MEM_SCALE = HEAD_DIM ** -0.5
D_FF = 2816
IN_COLS = 13312
IN_SHARD = IN_COLS // N_DEV
CB_HG_Q, CB_F, CB_HG_I, CB_HG_G = 0, 8, 24, 32
CB_DA_Q, CB_DA_K, CB_DA_V, CB_MEM_Q = 40, 52, 64, 76
ADAM_LR, ADAM_B1, ADAM_B2, ADAM_EPS, ADAM_WD, ADAM_STEP = 0.001, 0.9, 0.999, 1e-08, 0.01, 10
VMEM_BYTES_V7X = 64 * 1024 * 1024
SMALL_ROWS = 104
SMALL_GRAD_ROWS = 88

_NN = (((1,), (0,)), ((), ()))
_NT = (((1,), (1,)), ((), ()))
_TN = (((0,), (0,)), ((), ()))


def _dot(a, b, dims):
    return lax.dot_general(a.astype(BF16), b.astype(BF16), dims, preferred_element_type=F32)


def _sigmoid(x):
    return 0.5 * jnp.tanh(0.5 * x) + 0.5


def _params(semantics, est_bytes):
    limit = int(min(VMEM_BYTES_V7X - (6 << 20), max(56 << 20, est_bytes * 3 // 2)))
    return pltpu.CompilerParams(dimension_semantics=semantics, vmem_limit_bytes=limit)


def _nbytes(shape, dtype):
    return int(np.prod(shape)) * jnp.dtype(dtype).itemsize


def _alibi_slopes(n):
    return (2.0 ** (-8.0 * np.arange(1, n + 1) / n)).astype(np.float32)


def _matmul(a, b, mode, out_dtype, tm, tn, tk, name, b_stacked=False, out_stacked=False, n_outer=False, after=None,
            m_blocks=None, out_into=None, a_parts=False, b_parts=False):
    if a_parts:
        assert mode == "nt" and tk == a.shape[2]
        m, kdim = a.shape[1], a.shape[0] * a.shape[2]
    elif mode == "tn":
        kdim, m = a.shape
    else:
        m, kdim = a.shape
    if b_parts:
        assert mode == "nn" and not b_stacked and b.shape[2] % tn == 0
        n = b.shape[0] * b.shape[2]
    elif b_stacked:
        if mode == "nn":
            n = b.shape[0] * b.shape[2]
            assert tn == b.shape[2] and tk == kdim == b.shape[1]
        else:
            assert mode == "nt" and tk == b.shape[2] and b.shape[0] * tk == kdim
            n = b.shape[1]
    else:
        n = b.shape[0] if mode == "nt" else b.shape[1]
    assert m % tm == 0 and n % tn == 0 and kdim % tk == 0
    gm, gn, gk = m // tm, n // tn, kdim // tk
    i0 = 0
    if m_blocks is not None:
        assert mode != "tn" and not out_stacked
        i0, gm = m_blocks

    def ijk(f):
        if n_outer:
            return lambda j, i, k: f(i + i0, j, k)
        return lambda i, j, k: f(i + i0, j, k)

    if a_parts:
        a_spec = pl.BlockSpec((None, tm, tk), ijk(lambda i, j, k: (k, i, 0)))
    elif mode == "tn":
        a_spec = pl.BlockSpec((tk, tm), ijk(lambda i, j, k: (k, i)))
    else:
        a_spec = pl.BlockSpec((tm, tk), ijk(lambda i, j, k: (i, k)))
    if b_parts:
        per_part = b.shape[2] // tn
        b_spec = pl.BlockSpec((None, tk, tn), ijk(lambda i, j, k: (j // per_part, k, j % per_part)))
    elif b_stacked and mode == "nn":
        b_spec = pl.BlockSpec((None, tk, tn), ijk(lambda i, j, k: (j, 0, 0)))
    elif b_stacked:
        b_spec = pl.BlockSpec((None, tn, tk), ijk(lambda i, j, k: (k, j, 0)))
    elif mode == "nt":
        b_spec = pl.BlockSpec((tn, tk), ijk(lambda i, j, k: (j, k)))
    else:
        b_spec = pl.BlockSpec((tk, tn), ijk(lambda i, j, k: (k, j)))
    if out_stacked:
        assert tm == m
        out_shape = jax.ShapeDtypeStruct((gn, m, tn), out_dtype)
        o_spec = pl.BlockSpec((None, tm, tn), ijk(lambda i, j, k: (j, i, 0)))
    else:
        out_shape = jax.ShapeDtypeStruct((m, n), out_dtype)
        o_spec = pl.BlockSpec((tm, tn), ijk(lambda i, j, k: (i, j)))
    dims = {"nn": _NN, "nt": _NT, "tn": _TN}[mode]

    n_in = 2 + (after is not None) + (out_into is not None)

    def body(*refs):
        a_ref, b_ref, o_ref = refs[0], refs[1], refs[n_in]
        part = _dot(a_ref[...], b_ref[...], dims)
        if gk == 1:
            o_ref[...] = part.astype(out_dtype)
            return
        acc_ref = refs[-1]
        k = pl.program_id(2)

        @pl.when(k == 0)
        def _():
            acc_ref[...] = part

        @pl.when(jnp.logical_and(k > 0, k < gk - 1))
        def _():
            acc_ref[...] += part

        @pl.when(k == gk - 1)
        def _():
            o_ref[...] = (acc_ref[...] + part).astype(out_dtype)

    a_tile = _nbytes((tm, tk), a.dtype)
    b_tile = _nbytes((tk, tn), b.dtype)
    o_tile = _nbytes((tm, tn), out_dtype)
    est = 2 * (a_tile + b_tile + o_tile) + 3 * tm * tn * 4 + (a_tile + b_tile)
    grid = (gn, gm, gk) if n_outer else (gm, gn, gk)
    operands, in_specs = [a, b], [a_spec, b_spec]
    if after is not None:
        operands.append(after)
        in_specs.append(pl.BlockSpec(memory_space=pl.ANY))
    aliases = {}
    if out_into is not None:
        aliases = {len(operands): 0}
        operands.append(out_into)
        in_specs.append(pl.BlockSpec(memory_space=pl.ANY))
    return pl.pallas_call(
        body, name=name, grid=grid, in_specs=in_specs, out_specs=o_spec, out_shape=out_shape,
        scratch_shapes=[] if gk == 1 else [pltpu.VMEM((tm, tn), F32)], input_output_aliases=aliases,
        compiler_params=_params(("parallel", "parallel", "arbitrary"), est),
    )(*operands)


def _row_spec(tr, width, col_block=0):
    return pl.BlockSpec((tr, width), lambda i: (i, col_block))


def _bcast_spec(width):
    return pl.BlockSpec((1, width), lambda i: (0, 0))


def _col_spec(width, tr):
    return pl.BlockSpec((width, tr), lambda i: (0, i))


def _rmsnorm_fwd(x, gain, name, tr, transposed=False):
    rows, width = x.shape

    def body(x_ref, g_ref, o_ref, *t_ref):
        xv = x_ref[...]
        r = lax.rsqrt(jnp.mean(xv * xv, axis=-1, keepdims=True) + RMS_EPS)
        h = xv * r * g_ref[...]
        o_ref[...] = h.astype(BF16)
        if transposed:
            t_ref[0][...] = h.T.astype(BF16)

    out_specs, out_shape = [_row_spec(tr, width)], [jax.ShapeDtypeStruct((rows, width), BF16)]
    if transposed:
        out_specs.append(_col_spec(width, tr))
        out_shape.append(jax.ShapeDtypeStruct((width, rows), BF16))
    out = pl.pallas_call(
        body, name=name, grid=(rows // tr,), in_specs=[_row_spec(tr, width), _bcast_spec(width)],
        out_specs=out_specs, out_shape=out_shape,
        compiler_params=_params(("parallel",), 10 * tr * width * 4),
    )(x, gain)
    return out if transposed else out[0]


def _residual_rmsnorm_fwd(x, merged, wout, gain, name, tr):
    rows, width = x.shape

    def body(x_ref, m_ref, w_ref, g_ref, x1_ref, h_ref, ht_ref):
        xv = x_ref[...] + _dot(m_ref[...], w_ref[...], _NN)
        x1_ref[...] = xv
        r = lax.rsqrt(jnp.mean(xv * xv, axis=-1, keepdims=True) + RMS_EPS)
        h = xv * r * g_ref[...]
        h_ref[...] = h.astype(BF16)
        ht_ref[...] = h.T.astype(BF16)

    return pl.pallas_call(
        body, name=name, grid=(rows // tr,),
        in_specs=[_row_spec(tr, width), _row_spec(tr, merged.shape[1]),
                  pl.BlockSpec(wout.shape, lambda i: (0, 0)), _bcast_spec(width)],
        out_specs=[_row_spec(tr, width), _row_spec(tr, width), _col_spec(width, tr)],
        out_shape=[jax.ShapeDtypeStruct((rows, width), F32), jax.ShapeDtypeStruct((rows, width), BF16),
                   jax.ShapeDtypeStruct((width, rows), BF16)],
        compiler_params=_params(("parallel",), 14 * tr * width * 4),
    )(x, merged, wout, gain)


def _matmul_rmsnorm_bwd(a, b, x, dres, gain, name, tm, tk, a_parts=False, b_stacked=False, after=None,
                        m_blocks=None, out_into=None, with_bf16=True):
    width = x.shape[1]
    m = a.shape[1] if a_parts else a.shape[0]
    kdim = a.shape[0] * a.shape[2] if a_parts else a.shape[1]
    gk = kdim // tk
    i0, gm = (0, m // tm) if m_blocks is None else m_blocks
    n_in = 5 + (after is not None) + (out_into is not None)

    def body(*refs):
        a_ref, b_ref, x_ref, dres_ref, g_ref = refs[:5]
        outs = refs[n_in:]
        dx_ref, dg_ref, acc_ref = outs[0], outs[-2], outs[-1]
        i, k = pl.program_id(0), pl.program_id(1)
        part = _dot(a_ref[...], b_ref[...], _NT)

        @pl.when(k == 0)
        def _():
            acc_ref[...] = part

        @pl.when(jnp.logical_and(k > 0, k < gk - 1))
        def _():
            acc_ref[...] += part

        @pl.when(k == gk - 1)
        def _():
            dhv = acc_ref[...] + part if gk > 1 else part
            xv = x_ref[...]
            r = lax.rsqrt(jnp.mean(xv * xv, axis=-1, keepdims=True) + RMS_EPS)
            xhat = xv * r
            dyg = dhv * g_ref[...]
            dx = dres_ref[...] + r * (dyg - xhat * jnp.mean(dyg * xhat, axis=-1, keepdims=True))
            dx_ref[...] = dx
            if with_bf16:
                outs[1][...] = dx.astype(BF16)
            gpart = jnp.sum(dhv * xhat, axis=0, keepdims=True)

            @pl.when(i == 0)
            def _():
                dg_ref[...] = gpart

            @pl.when(i > 0)
            def _():
                dg_ref[...] += gpart

    rows = lambda width_: pl.BlockSpec((tm, width_), lambda i, k: (i + i0, 0))
    if a_parts:
        a_spec = pl.BlockSpec((None, tm, tk), lambda i, k: (k, i + i0, 0))
    else:
        a_spec = pl.BlockSpec((tm, tk), lambda i, k: (i + i0, k))
    if b_stacked:
        b_spec = pl.BlockSpec((None, width, tk), lambda i, k: (k, 0, 0))
    else:
        b_spec = pl.BlockSpec((width, tk), lambda i, k: (0, k))
    operands = [a, b, x, dres, gain]
    in_specs = [a_spec, b_spec, rows(width), rows(width), pl.BlockSpec((1, width), lambda i, k: (0, 0))]
    for extra in (after, out_into):
        if extra is not None:
            operands.append(extra)
            in_specs.append(pl.BlockSpec(memory_space=pl.ANY))
    aliases = {} if out_into is None else {len(operands) - 1: 0}
    out_specs = [rows(width)] + ([rows(width)] if with_bf16 else []) + [pl.BlockSpec((1, width), lambda i, k: (0, 0))]
    out_shape = [jax.ShapeDtypeStruct((m, width), F32)] + ([jax.ShapeDtypeStruct((m, width), BF16)] if with_bf16 else [])
    out_shape.append(jax.ShapeDtypeStruct((1, width), F32))
    est = 4 * tm * tk + 4 * width * tk + 12 * tm * width * 4
    return pl.pallas_call(
        body, name=name, grid=(gm, gk), in_specs=in_specs, out_specs=out_specs, out_shape=out_shape,
        scratch_shapes=[pltpu.VMEM((tm, width), F32)], input_output_aliases=aliases,
        compiler_params=_params(("arbitrary", "arbitrary"), est),
    )(*operands)


def _gain_grad(x, dh, name):
    rows, width = x.shape

    def body(x_ref, dh_ref, dg_ref):
        xv = x_ref[...]
        r = lax.rsqrt(jnp.mean(xv * xv, axis=-1, keepdims=True) + RMS_EPS)
        dg_ref[...] = jnp.sum(dh_ref[...] * xv * r, axis=0, keepdims=True)

    return pl.pallas_call(
        body, name=name, grid=(1,), in_specs=[_row_spec(rows, width), _row_spec(rows, width)],
        out_specs=_bcast_spec(width), out_shape=jax.ShapeDtypeStruct((1, width), F32),
        compiler_params=_params(("arbitrary",), 6 * rows * width * 4),
    )(x, dh)


def _lb_table(logits, name):
    slots, width = logits.shape

    def body(l_ref, o_ref):
        lv = l_ref[...]
        mx = jnp.max(lv, axis=0, keepdims=True)
        e = jnp.exp(lv - mx)
        o_ref[...] = e[0:1, :] / jnp.sum(e, axis=0, keepdims=True)

    return pl.pallas_call(
        body, name=name, grid=(1,), in_specs=[pl.BlockSpec((slots, width), lambda i: (0, 0))],
        out_specs=_bcast_spec(width), out_shape=jax.ShapeDtypeStruct((1, width), F32),
    )(logits)


def _branch_merge_fwd(proj, outs, weights, tr):
    w = D_MODEL

    def body(ghg_ref, gda_ref, gmem_ref, ohg_ref, oda_ref, omem_ref, whg_ref, wda_ref, wmem_ref,
             m_ref, thg_ref, tda_ref, tmem_ref):
        acc = None
        for g_ref, o_ref, w_ref, t_ref in ((ghg_ref, ohg_ref, whg_ref, thg_ref), (gda_ref, oda_ref, wda_ref, tda_ref),
                                           (gmem_ref, omem_ref, wmem_ref, tmem_ref)):
            t = _dot(o_ref[...], w_ref[...], _NN)
            t_ref[...] = t.astype(BF16)
            term = _sigmoid(g_ref[...]) * t
            acc = term if acc is None else acc + term
        m_ref[...] = acc.astype(BF16)

    whole = lambda a: pl.BlockSpec(a.shape, lambda i: (0, 0))
    shape = jax.ShapeDtypeStruct((SEQ, w), BF16)
    return pl.pallas_call(
        body, name="branch_merge_fwd", grid=(SEQ // tr,),
        in_specs=[_row_spec(tr, w, 10), _row_spec(tr, w, 11), _row_spec(tr, w, 12)]
        + [_row_spec(tr, o.shape[1]) for o in outs] + [whole(wt) for wt in weights],
        out_specs=[_row_spec(tr, w)] * 4, out_shape=[shape] * 4,
        compiler_params=_params(("parallel",), 20 * tr * w * 4),
    )(proj, proj, proj, *outs, *weights)


def _dproj_buffer():
    return lax.empty((SEQ, IN_COLS), BF16)


def _gate_merge_bwd(proj, ts, weights, dx1b, wout, o_pre, hg_gain, o_da32, dproj, tr):
    w = D_MODEL
    steps = SEQ // tr
    gate_col0 = 10 * w
    hg_gate_col0 = CB_HG_G * LANE

    def body(ghg_ref, gda_ref, gmem_ref, hgg_ref, thg_ref, tda_ref, tmem_ref, whg_ref, wda_ref, wmem_ref, dx_ref,
             wout_ref, opre_ref, gain_ref, oda_ref, dproj_in,
             dthg_ref, dtda_ref, dtmem_ref, dopre_ref, doda_ref, domem_ref, dd_ref, dgain_ref, dproj_ref,
             stage, stage_hg, sems):
        del dproj_in
        i = pl.program_id(0)
        slot = i % 2
        rows = pl.ds(pl.multiple_of(i * tr, tr), tr)

        def slot_copies(s):
            return (pltpu.make_async_copy(stage.at[s], dproj_ref.at[rows, pl.ds(gate_col0, 3 * w)], sems.at[s]),
                    pltpu.make_async_copy(stage_hg.at[s], dproj_ref.at[rows, pl.ds(hg_gate_col0, w)], sems.at[2 + s]))

        @pl.when(i >= 2)
        def _():
            for cp in slot_copies(slot):
                cp.wait()

        dm = _dot(dx_ref[...], wout_ref[...], _NT)
        branches = ((ghg_ref, thg_ref, whg_ref, dthg_ref), (gda_ref, tda_ref, wda_ref, dtda_ref),
                    (gmem_ref, tmem_ref, wmem_ref, dtmem_ref))
        dos = []
        for b, (g_ref, t_ref, w_ref, dt_ref) in enumerate(branches):
            s = _sigmoid(g_ref[...])
            dt = (s * dm).astype(BF16)
            dt_ref[...] = dt
            dos.append(_dot(dt, w_ref[...], _NT))
            stage[slot, :, b * w:(b + 1) * w] = (dm * t_ref[...].astype(F32) * s * (1.0 - s)).astype(BF16)
        do_hg, do_da, do_mem = dos
        doda_ref[...] = do_da
        domem_ref[...] = do_mem

        gainv = gain_ref[...]
        part = jnp.zeros((1, HEAD_DIM), F32)
        for h in range(HG_HEADS):
            hs = slice(h * HEAD_DIM, (h + 1) * HEAD_DIM)
            o = opre_ref[:, hs]
            r = lax.rsqrt(jnp.mean(o * o, axis=-1, keepdims=True) + RMS_EPS)
            ohat = o * r
            g = hgg_ref[:, hs]
            sg = _sigmoid(g)
            silu = g * sg
            dout = do_hg[:, hs]
            stage_hg[slot, :, hs] = (dout * ohat * gainv * (sg + silu * (1.0 - sg))).astype(BF16)
            dy = dout * silu
            part = part + jnp.sum(dy * ohat, axis=0, keepdims=True)
            dn = dy * gainv
            dopre_ref[:, hs] = r * (dn - ohat * jnp.mean(dn * ohat, axis=-1, keepdims=True))
        for cp in slot_copies(slot):
            cp.start()

        prod = do_da * oda_ref[...]
        for h in range(DA_WIDTH // HEAD_DIM):
            hs = slice(h * HEAD_DIM, (h + 1) * HEAD_DIM)
            dd_ref[:, hs] = jnp.broadcast_to(jnp.sum(prod[:, hs], axis=-1, keepdims=True), (tr, HEAD_DIM))

        @pl.when(i == 0)
        def _():
            dgain_ref[...] = part

        @pl.when(i > 0)
        def _():
            dgain_ref[...] += part

        @pl.when(i == steps - 1)
        def _():
            for cp in slot_copies(1 - slot) + slot_copies(slot):
                cp.wait()

    assert steps >= 2
    whole = lambda a: pl.BlockSpec(a.shape, lambda i: (0, 0))
    widths = [wt.shape[0] for wt in weights]
    return pl.pallas_call(
        body, name="gate_merge_bwd", grid=(steps,),
        in_specs=[_row_spec(tr, w, 10), _row_spec(tr, w, 11), _row_spec(tr, w, 12), _row_spec(tr, w, hg_gate_col0 // w)]
        + [_row_spec(tr, w)] * 3 + [whole(wt) for wt in weights]
        + [_row_spec(tr, w), whole(wout), _row_spec(tr, w), _bcast_spec(HEAD_DIM), _row_spec(tr, DA_WIDTH), _any_spec()],
        out_specs=[_row_spec(tr, w)] * 4 + [_row_spec(tr, widths[1]), _row_spec(tr, widths[2]),
                                            _row_spec(tr, DA_WIDTH), _bcast_spec(HEAD_DIM), _any_spec()],
        out_shape=[jax.ShapeDtypeStruct((SEQ, w), BF16)] * 3 + [jax.ShapeDtypeStruct((SEQ, w), F32)]
        + [jax.ShapeDtypeStruct((SEQ, widths[1]), F32), jax.ShapeDtypeStruct((SEQ, widths[2]), F32),
           jax.ShapeDtypeStruct((SEQ, DA_WIDTH), F32), jax.ShapeDtypeStruct((1, HEAD_DIM), F32),
           jax.ShapeDtypeStruct((SEQ, IN_COLS), BF16)],
        scratch_shapes=[pltpu.VMEM((2, tr, 3 * w), BF16), pltpu.VMEM((2, tr, w), BF16), pltpu.SemaphoreType.DMA((4,))],
        input_output_aliases={15: 8},
        compiler_params=_params(("arbitrary",), 44 * tr * w * 4),
    )(proj, proj, proj, proj, *ts, *weights, dx1b, wout, o_pre, hg_gain, o_da32, dproj)


def _ffn_in_swiglu(h2, wfin, tm, tn):
    gm, gn = SEQ // tm, D_FF // tn

    def body(h_ref, wa_ref, wb_ref, a_ref, b_ref, act_ref, actt_ref):
        h = h_ref[...]
        a = _dot(h, wa_ref[...], _NN)
        b = _dot(h, wb_ref[...], _NN)
        act = a * _sigmoid(a) * b
        a_ref[...] = a.astype(BF16)
        b_ref[...] = b.astype(BF16)
        act_ref[...] = act.astype(BF16)
        actt_ref[...] = act.T.astype(BF16)

    tile = pl.BlockSpec((tm, tn), lambda j, i: (i, j))
    shape = jax.ShapeDtypeStruct((SEQ, D_FF), BF16)
    return pl.pallas_call(
        body, name="ffn_in_swiglu_fwd", grid=(gn, gm),
        in_specs=[pl.BlockSpec((tm, D_MODEL), lambda j, i: (i, 0)),
                  pl.BlockSpec((D_MODEL, tn), lambda j, i: (0, j)),
                  pl.BlockSpec((D_MODEL, tn), lambda j, i: (0, gn + j))],
        out_specs=[tile, tile, tile, pl.BlockSpec((tn, tm), lambda j, i: (j, i))],
        out_shape=[shape, shape, shape, jax.ShapeDtypeStruct((D_FF, SEQ), BF16)],
        compiler_params=_params(("parallel", "parallel"), 4 * tm * D_MODEL + 8 * D_MODEL * tn + 16 * tm * tn
                                + 6 * tm * tn * 4),
    )(h2, wfin, wfin)


def _ffn_out_bwd_swiglu(dyb, wfout, a, b, tm, tn):
    gm, gn = SEQ // tm, D_FF // tn

    def body(dy_ref, w_ref, a_ref, b_ref, o_ref):
        d = _dot(dy_ref[...], w_ref[...], _NT)
        av = a_ref[...].astype(F32)
        bv = b_ref[...].astype(F32)
        s = _sigmoid(av)
        silu = av * s
        o_ref[0] = (d * bv * (s + silu * (1.0 - s))).astype(BF16)
        o_ref[1] = (d * silu).astype(BF16)

    tile = pl.BlockSpec((tm, tn), lambda i, j: (i, j))
    return pl.pallas_call(
        body, name="ffn_out_bwd_swiglu", grid=(gm, gn),
        in_specs=[pl.BlockSpec((tm, D_MODEL), lambda i, j: (i, 0)), pl.BlockSpec((tn, D_MODEL), lambda i, j: (j, 0)),
                  tile, tile],
        out_specs=pl.BlockSpec((2, tm, tn), lambda i, j: (0, i, j)),
        out_shape=jax.ShapeDtypeStruct((2, SEQ, D_FF), BF16),
        compiler_params=_params(("parallel", "parallel"), 4 * tm * D_MODEL + 4 * tn * D_MODEL + 16 * tm * tn
                                + 6 * tm * tn * 4),
    )(dyb, wfout, a, b)


def _ffn_out_loss_head(x1, act, wfout, target, tr):
    w = D_MODEL

    def body(x_ref, a_ref, w_ref, t_ref, dy_ref, dyb_ref, loss_ref, acc_ref):
        err = x_ref[...] + _dot(a_ref[...], w_ref[...], _NN) - t_ref[...]
        dy = err * (1.0 / w)
        dy_ref[...] = dy
        dyb_ref[...] = dy.astype(BF16)
        part = jnp.sum(err * err, axis=0, keepdims=True)

        @pl.when(pl.program_id(0) == 0)
        def _():
            acc_ref[...] = part

        @pl.when(pl.program_id(0) > 0)
        def _():
            acc_ref[...] += part

        @pl.when(pl.program_id(0) == SEQ // tr - 1)
        def _():
            total = jnp.sum(acc_ref[...], axis=1, keepdims=True) * (0.5 / w)
            loss_ref[...] = jnp.broadcast_to(total, (1, LANE))

    return pl.pallas_call(
        body, name="ffn_out_loss_head", grid=(SEQ // tr,),
        in_specs=[_row_spec(tr, w), _row_spec(tr, D_FF), pl.BlockSpec((D_FF, w), lambda i: (0, 0)), _row_spec(tr, w)],
        out_specs=[_row_spec(tr, w), _row_spec(tr, w), _bcast_spec(LANE)],
        out_shape=[jax.ShapeDtypeStruct((SEQ, w), F32), jax.ShapeDtypeStruct((SEQ, w), BF16),
                   jax.ShapeDtypeStruct((1, LANE), F32)],
        scratch_shapes=[pltpu.VMEM((1, w), F32)],
        compiler_params=_params(("arbitrary",), 12 * tr * w * 4 + 4 * D_FF * w),
    )(x1, act, wfout, target)


GLA_ROWS = 256
GLA_CPB = GLA_ROWS // HG_CHUNK
GLA_NBLK = SEQ // GLA_ROWS
GLA_WAYS = 4
GLA_TRIPS = GLA_NBLK // GLA_WAYS
GLA_GRAD_WAYS = 4
GLA_GRAD_TRIPS = GLA_NBLK // GLA_GRAD_WAYS
GLA_NCK = SEQ // HG_CHUNK
GLA_INTER_WAYS = 8


def _dot_split(m, xv, dims, terms=3):
    dot = lambda t: lax.dot_general(m, t, dims, preferred_element_type=F32)
    hi = xv.astype(BF16)
    r1 = xv - hi.astype(F32)
    mid = r1.astype(BF16)
    if terms == 2:
        return dot(mid) + dot(hi)
    lo = (r1 - mid.astype(F32)).astype(BF16)
    return (dot(lo) + dot(mid)) + dot(hi)


def _gla_block(qc, fc, lbv, masks, b=None):
    mask, maskb, direction = masks
    sq = _sigmoid(qc)
    q = qc * sq * HG_SCALE
    sf = _sigmoid(fc)
    forget = lbv + (1.0 - lbv) * sf
    k = 1.0 - forget
    logf = jnp.log(forget)
    if b is None:
        b = _dot_split(maskb, logf, _NN)
    ends = []
    for j in range(GLA_CPB):
        lo, hi = j * HG_CHUNK, (j + 1) * HG_CHUNK
        end = jnp.where(direction == 0, b[hi - 1:hi, :], b[lo:lo + 1, :])
        ends.append(jnp.broadcast_to(end, (HG_CHUNK, HEAD_DIM)))
    bt = jnp.concatenate(ends, axis=0)
    eb = jnp.exp(b)
    qt = q * eb
    kt = k * jnp.exp(-b)
    kh = k * jnp.exp(bt - b)
    a = jnp.where(mask, _dot(qt, kt, _NT), 0.0)
    return dict(sq=sq, sf=sf, forget=forget, k=k, b=b, bt=bt, eb=eb, qt=qt, kt=kt, kh=kh, a=a)


def _gla_masks(direction):
    row = lax.broadcasted_iota(jnp.int32, (GLA_ROWS, GLA_ROWS), 0)
    col = lax.broadcasted_iota(jnp.int32, (GLA_ROWS, GLA_ROWS), 1)
    same = (row // HG_CHUNK) == (col // HG_CHUNK)
    mask = jnp.logical_and(same, jnp.where(direction == 0, row - col, col - row) >= 0)
    return mask, jnp.where(mask, 1.0, 0.0).astype(BF16), direction


def _gla_chunk_rows(j):
    return slice(j * HG_CHUNK, (j + 1) * HG_CHUNK)


def _gla_block_rows(b):
    return pl.ds(pl.multiple_of(b * GLA_ROWS, GLA_ROWS), GLA_ROWS)


def _head_spec(col_block0):
    return pl.BlockSpec((SEQ, HEAD_DIM), lambda h, d: (0, col_block0 + h))


def _gla_decay(proj, lb):
    def body(f_ref, lb_ref, b_ref):
        maskb = _gla_masks(pl.program_id(0))[1]
        lbv = lb_ref[...]

        def step(s, carry):
            rows = [_gla_block_rows(s + w * trips) for w in range(ways)]
            loaded = [f_ref[r, :] for r in rows]
            outs = [_dot_split(maskb, jnp.log(lbv + (1.0 - lbv) * _sigmoid(fc)), _NN) for fc in loaded]
            for r, o in zip(rows, outs):
                b_ref[r, :] = o
            return carry

        lax.fori_loop(0, trips, step, 0)

    ways, trips = GLA_WAYS, GLA_TRIPS
    heads = 2
    width = heads * HEAD_DIM
    return pl.pallas_call(
        body, name="gla_decay", grid=(2, HG_HEADS // heads),
        in_specs=[pl.BlockSpec((SEQ, width), lambda d, h: (0, CB_F // heads + (HG_HEADS // heads) * d + h)),
                  pl.BlockSpec((None, None, 1, width), lambda d, h: (d, h, 0, 0))],
        out_specs=pl.BlockSpec((None, SEQ, width), lambda d, h: (d, 0, h)),
        out_shape=jax.ShapeDtypeStruct((2, SEQ, D_MODEL), F32),
        compiler_params=_params(("parallel", "parallel"), 6 * SEQ * width * 4),
    )(proj, lb.reshape(2, HG_HEADS // heads, 1, width))


def _gla_fwd(proj, lb, gain, b_all):
    nck = GLA_NCK

    def body(q_ref, f_ref, v_ref, g_ref, lb_ref, gain_ref, b_ref, ohg_ref, opre_ref, st_ref, qt_scr, dec_scr, cs_scr):
        d = pl.program_id(1)
        masks = _gla_masks(d)
        lbv = lb_ref[...]

        @pl.when(d == 0)
        def _():
            opre_ref[...] = jnp.zeros_like(opre_ref)

        def intra(s, carry):
            blocks = [s + w * GLA_TRIPS for w in range(GLA_WAYS)]
            rows = [_gla_block_rows(b) for b in blocks]
            loaded = [(q_ref[r, :], f_ref[r, :], v_ref[r, :], opre_ref[r, :], b_ref[r, :]) for r in rows]
            results = []
            for qc, fc, v, o_prev, decay_log in loaded:
                ck = _gla_block(qc, fc, lbv, masks, b=decay_log)
                o = o_prev + _dot(ck["a"], v, _NN)
                cs = [_dot(v[_gla_chunk_rows(j), :], ck["kh"][_gla_chunk_rows(j), :], _TN) for j in range(GLA_CPB)]
                dec = [jnp.exp(ck["bt"][j * HG_CHUNK:j * HG_CHUNK + 8, :]) for j in range(GLA_CPB)]
                results.append((o, ck["qt"].astype(BF16), cs, dec))
            for b, r, (o, qt, cs, dec) in zip(blocks, rows, results):
                opre_ref[r, :] = o
                qt_scr[r, :] = qt
                for j in range(GLA_CPB):
                    cs_scr[b * GLA_CPB + j] = cs[j]
                    dec_scr[b * GLA_CPB + j] = dec[j]
            return carry

        lax.fori_loop(0, GLA_TRIPS, intra, 0)

        def scan(i, st):
            c = jnp.where(d == 0, i, nck - 1 - i)
            st_ref[c] = st
            return st * dec_scr[c][0:1, :] + cs_scr[c]

        lax.fori_loop(0, nck, scan, jnp.zeros((HEAD_DIM, HEAD_DIM), F32), unroll=4)

        def inter(t, carry):
            chunks = [t + w * (nck // GLA_INTER_WAYS) for w in range(GLA_INTER_WAYS)]
            rows = [pl.ds(pl.multiple_of(c * HG_CHUNK, HG_CHUNK), HG_CHUNK) for c in chunks]
            loaded = [(opre_ref[r, :], qt_scr[r, :], st_ref[c]) for r, c in zip(rows, chunks)]
            for r, o in zip(rows, [o_prev + _dot(qt, st, _NT) for o_prev, qt, st in loaded]):
                opre_ref[r, :] = o
            return carry

        lax.fori_loop(0, nck // GLA_INTER_WAYS, inter, 0)

        @pl.when(d == 1)
        def _():
            o = opre_ref[...]
            r = lax.rsqrt(jnp.mean(o * o, axis=-1, keepdims=True) + RMS_EPS)
            g = g_ref[...]
            ohg_ref[...] = (o * r * gain_ref[...] * (g * _sigmoid(g))).astype(BF16)

    blk = SEQ * HEAD_DIM * 4
    return pl.pallas_call(
        body, name="gla_fwd", grid=(HG_HEADS, 2),
        in_specs=[_head_spec(CB_HG_Q),
                  pl.BlockSpec((SEQ, HEAD_DIM), lambda h, d: (0, CB_F + 8 * d + h)),
                  _head_spec(CB_HG_I), _head_spec(CB_HG_G),
                  pl.BlockSpec((None, None, 1, HEAD_DIM), lambda h, d: (d, h, 0, 0)),
                  pl.BlockSpec((1, HEAD_DIM), lambda h, d: (0, 0)),
                  pl.BlockSpec((None, SEQ, HEAD_DIM), lambda h, d: (d, 0, h))],
        out_specs=[_head_spec(0), _head_spec(0),
                   pl.BlockSpec((None, None, nck, HEAD_DIM, HEAD_DIM), lambda h, d: (h, d, 0, 0, 0)),
                   pl.BlockSpec((None, SEQ, HEAD_DIM), lambda h, d: (d, 0, h)),
                   pl.BlockSpec((None, None, nck, 8, HEAD_DIM), lambda h, d: (h, d, 0, 0, 0))],
        out_shape=[jax.ShapeDtypeStruct((SEQ, D_MODEL), BF16), jax.ShapeDtypeStruct((SEQ, D_MODEL), F32),
                   jax.ShapeDtypeStruct((HG_HEADS, 2, nck, HEAD_DIM, HEAD_DIM), F32),
                   jax.ShapeDtypeStruct((2, SEQ, D_MODEL), BF16),
                   jax.ShapeDtypeStruct((HG_HEADS, 2, nck, 8, HEAD_DIM), F32)],
        scratch_shapes=[pltpu.VMEM((nck, HEAD_DIM, HEAD_DIM), F32)],
        compiler_params=_params(("parallel", "arbitrary"), 8 * blk + 3 * blk + 2 * blk + 2 * blk + 3 * blk),
    )(proj, proj, proj, proj, lb, gain, b_all)


def _gla_bwd(proj, lb, states, qt_all, dec_all, b_all, do, dproj):
    nck = GLA_NCK

    def body(q_ref, f_ref, v_ref, lb_ref, do_ref, st_ref, qt_ref, dec_scr, b_ref, dproj_in, dproj_ref, dlb_ref,
             dq_acc, dv_acc, dst_scr, cs_scr, df_out, dq_out, dv_out, sems):
        del dproj_in
        h, d = pl.program_id(0), pl.program_id(1)
        masks = _gla_masks(d)
        mask, maskb, _ = masks
        lbv = lb_ref[...]

        def column_copy(k, staging, col_block):
            cols = pl.ds(pl.multiple_of(col_block * LANE, LANE), LANE)
            return pltpu.make_async_copy(staging, dproj_ref.at[:, cols], sems.at[k])

        df_copy = column_copy(0, df_out, CB_F + 8 * d + h)
        dq_copy = column_copy(1, dq_out, CB_HG_Q + h)
        dv_copy = column_copy(2, dv_out, CB_HG_I + h)
        last = jnp.logical_and(h == HG_HEADS - 1, d == 1)

        def intra(s, carry):
            blocks = [s + w * GLA_TRIPS for w in range(GLA_WAYS)]
            loaded = [(qt_ref[r, :], do_ref[r, :]) for r in map(_gla_block_rows, blocks)]
            results = [[_dot(doc[_gla_chunk_rows(j), :], qt[_gla_chunk_rows(j), :], _TN) for j in range(GLA_CPB)]
                       for qt, doc in loaded]
            for b, cs in zip(blocks, results):
                for j in range(GLA_CPB):
                    cs_scr[b * GLA_CPB + j] = cs[j]
            return carry

        lax.fori_loop(0, GLA_TRIPS, intra, 0)

        def scan(i, dst):
            c = jnp.where(d == 0, nck - 1 - i, i)
            dst_scr[c] = dst
            return dst * dec_scr[c][0:1, :] + cs_scr[c]

        lax.fori_loop(0, nck, scan, jnp.zeros((HEAD_DIM, HEAD_DIM), F32), unroll=4)

        @pl.when(d == 0)
        def _():
            dq_acc[...] = jnp.zeros_like(dq_acc)
            dv_acc[...] = jnp.zeros_like(dv_acc)

        def block_grads(qc, fc, v, doc, states_in, dstates, decays, decay_log):
            ck = _gla_block(qc, fc, lbv, masks, b=decay_log)
            qt, kt, kh, a = ck["qt"], ck["kt"], ck["kh"], ck["a"]
            da = jnp.where(mask, _dot(doc, v, _NT), 0.0)
            dqt_i = _dot(da, kt, _NN)
            dkt = _dot(da, qt, _TN)
            dv_i = _dot(a, doc, _TN)
            dqt_p, dv_p, dkh_p, dbt_p = [], [], [], []
            for j in range(GLA_CPB):
                cr = _gla_chunk_rows(j)
                st_in, dst = states_in[j], dstates[j]
                dqt_p.append(dqt_i[cr, :] + _dot(doc[cr, :], st_in, _NN))
                dv_p.append(dv_i[cr, :] + _dot(kh[cr, :], dst, _NT))
                dkh_j = _dot(v[cr, :], dst, _NN)
                dkh_p.append(dkh_j)
                dbt_j = (decays[j][0:1, :] * jnp.sum(dst * st_in, axis=0, keepdims=True)
                         + jnp.sum(dkh_j * kh[cr, :], axis=0, keepdims=True))
                dbt_p.append(jnp.broadcast_to(dbt_j, (HG_CHUNK, HEAD_DIM)))
            dqt = jnp.concatenate(dqt_p, axis=0)
            dv = jnp.concatenate(dv_p, axis=0)
            dkh = jnp.concatenate(dkh_p, axis=0)
            dbt = jnp.concatenate(dbt_p, axis=0)
            db = dqt * qt - dkt * kt - dkh * kh
            dq = dqt * ck["eb"]
            dk = dkt * jnp.exp(-ck["b"]) + dkh * jnp.exp(ck["bt"] - ck["b"])
            dlogf = _dot_split(maskb, db, _TN, terms=2) + dbt
            dforget = dlogf / ck["forget"] - dk
            sf, sq = ck["sf"], ck["sq"]
            df = (dforget * (1.0 - lbv) * sf * (1.0 - sf)).astype(BF16)
            dqc = dq * HG_SCALE * (sq + qc * sq * (1.0 - sq))
            return df, dqc, dv, jnp.sum(dforget * (1.0 - sf), axis=0, keepdims=True)

        def grads(s, dlb):
            blocks = [s + w * GLA_GRAD_TRIPS for w in range(GLA_GRAD_WAYS)]
            rows = [_gla_block_rows(b) for b in blocks]
            loaded = []
            for b, r in zip(blocks, rows):
                chunks = [b * GLA_CPB + j for j in range(GLA_CPB)]
                loaded.append((q_ref[r, :], f_ref[r, :], v_ref[r, :], do_ref[r, :], [st_ref[c] for c in chunks],
                               [dst_scr[c] for c in chunks], [dec_scr[c] for c in chunks], b_ref[r, :],
                               dq_acc[r, :], dv_acc[r, :]))
            results = [block_grads(*t[:8]) + (t[8], t[9]) for t in loaded]
            for r, (df, dqc, dv, dlb_part, dq_prev, dv_prev) in zip(rows, results):
                df_out[r, :] = df
                dq_acc[r, :] = dq_prev + dqc
                dv_acc[r, :] = dv_prev + dv
                dlb = dlb + dlb_part
            return dlb

        @pl.when(jnp.logical_or(h > 0, d > 0))
        def _():
            df_copy.wait()

        dlb_ref[...] = lax.fori_loop(0, GLA_GRAD_TRIPS, grads, jnp.zeros((1, HEAD_DIM), F32))
        df_copy.start()

        @pl.when(d == 1)
        def _():
            @pl.when(h > 0)
            def _():
                dq_copy.wait()
                dv_copy.wait()

            dq_out[...] = dq_acc[...].astype(BF16)
            dv_out[...] = dv_acc[...].astype(BF16)
            dq_copy.start()
            dv_copy.start()

        @pl.when(last)
        def _():
            df_copy.wait()
            dq_copy.wait()
            dv_copy.wait()

    blk = SEQ * HEAD_DIM * 4
    state_bytes = nck * HEAD_DIM * HEAD_DIM * 4
    dproj, dlb = pl.pallas_call(
        body, name="gla_bwd", grid=(HG_HEADS, 2),
        in_specs=[_head_spec(CB_HG_Q),
                  pl.BlockSpec((SEQ, HEAD_DIM), lambda h, d: (0, CB_F + 8 * d + h)),
                  _head_spec(CB_HG_I),
                  pl.BlockSpec((None, None, 1, HEAD_DIM), lambda h, d: (d, h, 0, 0)),
                  _head_spec(0),
                  pl.BlockSpec((None, None, nck, HEAD_DIM, HEAD_DIM), lambda h, d: (h, d, 0, 0, 0)),
                  pl.BlockSpec((None, SEQ, HEAD_DIM), lambda h, d: (d, 0, h)),
                  pl.BlockSpec((None, None, nck, 8, HEAD_DIM), lambda h, d: (h, d, 0, 0, 0)),
                  pl.BlockSpec((None, SEQ, HEAD_DIM), lambda h, d: (d, 0, h)),
                  _any_spec()],
        out_specs=[_any_spec(), pl.BlockSpec((None, None, 1, HEAD_DIM), lambda h, d: (d, h, 0, 0))],
        out_shape=[jax.ShapeDtypeStruct((SEQ, IN_COLS), BF16), jax.ShapeDtypeStruct((2, HG_HEADS, 1, HEAD_DIM), F32)],
        scratch_shapes=[pltpu.VMEM((SEQ, HEAD_DIM), F32)] * 2
        + [pltpu.VMEM((nck, HEAD_DIM, HEAD_DIM), F32)] * 2
        + [pltpu.VMEM((SEQ, HEAD_DIM), BF16)] * 3 + [pltpu.SemaphoreType.DMA((3,))],
        input_output_aliases={9: 0},
        compiler_params=_params(("arbitrary", "arbitrary"), 10 * blk + 4 * state_bytes + 3 * blk + 3 * blk),
    )(proj, proj, proj, lb, do, states, qt_all, dec_all, b_all, dproj)
    return dproj, dlb


def _da_residue_rows(r, n0, size, d):
    if d == 1:
        return pl.ds(pl.multiple_of(n0, 8), size)
    return pl.ds(r + n0 * d, size, stride=d)


def _da_residue_ways(d, nqb):
    return min(d, 4) if nqb <= 2 else 1


def _da_rmsnorm(x, gain):
    r = lax.rsqrt(jnp.mean(x * x, axis=-1, keepdims=True) + RMS_EPS)
    return x * r, r, x * r * gain


def _da_scores(qn_scr, kn_scr, slope, i, ld):
    w0 = jnp.clip(i * DA_QB - DA_RADIUS, 0, ld - DA_WIN)
    w0 = pl.multiple_of(w0, DA_RADIUS)
    qrows = pl.ds(pl.multiple_of(i * DA_QB, DA_QB), DA_QB)
    win = pl.ds(w0, DA_WIN)
    qb = qn_scr[qrows, :]
    kw = kn_scr[win, :]
    s = _dot(qb, kw, _NT) * DA_SCALE
    qpos = i * DA_QB + lax.broadcasted_iota(jnp.int32, (DA_QB, DA_WIN), 0)
    kpos = w0 + lax.broadcasted_iota(jnp.int32, (DA_QB, DA_WIN), 1)
    arel = jnp.abs(kpos - qpos)
    s = s - slope * arel.astype(F32)
    s = jnp.where(arel <= DA_RADIUS, s, NEG_INF)
    return s, qb, kw, qrows, win


def _da_slopes(group):
    d = DA_DILATIONS[group]
    sl = _alibi_slopes(DA_HEADS)[4 * group:4 * group + 4] * d
    return jnp.asarray(np.broadcast_to(sl[:, None, None], (4, 1, LANE)).copy())


def _da_fwd(proj, gq, gk, group, others=()):
    d = DA_DILATIONS[group]
    ld = SEQ // d
    nqb = ld // DA_QB
    n_other = 2 * len(others)

    ways = min(DA_FWD_WAYS, nqb)
    rways = _da_residue_ways(d, nqb)

    def body(q_ref, k_ref, v_ref, gq_ref, gk_ref, sl_ref, *refs):
        other_refs, refs = refs[:n_other], refs[n_other:]
        if others:
            ob_ref, of_ref, lsej_ref, qn_scr, kn_scr, v_scr, o_ref, lse_ref = refs
        else:
            o_ref, lse_ref, qn_scr, kn_scr, v_scr = refs
        slope = sl_ref[:, 0:1]

        def residues(t, carry):
            rs = [t * rways + u for u in range(rways)]
            for u, r in enumerate(rs):
                sel = _da_residue_rows(r, 0, ld, d)
                qn_scr[u] = _da_rmsnorm(q_ref[sel, :], gq_ref[...])[2].astype(BF16)
                kn_scr[u] = _da_rmsnorm(k_ref[sel, :], gk_ref[...])[2].astype(BF16)
                v_scr[u] = v_ref[sel, :].astype(BF16)

            def block(u, i):
                s, _, _, _, win = _da_scores(qn_scr.at[u], kn_scr.at[u], slope, i, ld)
                m = jnp.max(s, axis=-1, keepdims=True)
                p = jnp.exp(s - m)
                l = jnp.sum(p, axis=-1, keepdims=True)
                return _dot(p, v_scr[u, win, :], _NN) / l, jnp.broadcast_to(m + jnp.log(l), (DA_QB, HEAD_DIM))

            def step(i, c2):
                todo = [(u, r, i + w * (nqb // ways)) for u, r in enumerate(rs) for w in range(ways)]
                for (u, r, b), (o, lse) in zip(todo, [block(u, b) for u, _, b in todo]):
                    out = _da_residue_rows(r, b * DA_QB, DA_QB, d)
                    o_ref[out, :] = o
                    lse_ref[out, :] = lse
                return c2

            return lax.fori_loop(0, nqb // ways, step, carry)

        lax.fori_loop(0, d // rways, residues, 0)

        if others:
            pieces = 8
            rows = SEQ // pieces

            def merge(t, carry):
                r = pl.ds(pl.multiple_of(t * rows, rows), rows)
                outs = [ref[r, :] for ref in other_refs[0::2]] + [o_ref[r, :]]
                lses = [ref[r, :] for ref in other_refs[1::2]] + [lse_ref[r, :]]
                m = lses[0]
                for l in lses[1:]:
                    m = jnp.maximum(m, l)
                es = [jnp.exp(l - m) for l in lses]
                tot = sum(es[1:], es[0])
                o = sum((e * v for e, v in zip(es[1:], outs[1:])), es[0] * outs[0]) / tot
                of_ref[r, :] = o
                ob_ref[r, :] = o.astype(BF16)
                lsej_ref[r, :] = m + jnp.log(tot)
                return carry

            lax.fori_loop(0, pieces, merge, 0)

    seq_spec = lambda base: pl.BlockSpec((SEQ, HEAD_DIM), lambda h: (0, base + 4 * group + h))
    out_spec = pl.BlockSpec((SEQ, HEAD_DIM), lambda h: (0, h))
    gain_spec = pl.BlockSpec((1, HEAD_DIM), lambda h: (0, 0))
    blk = SEQ * HEAD_DIM * 4
    fshape = jax.ShapeDtypeStruct((SEQ, DA_WIDTH), F32)
    scratch = [pltpu.VMEM((rways, ld, HEAD_DIM), BF16)] * 3
    if others:
        out_shape = [jax.ShapeDtypeStruct((SEQ, DA_WIDTH), BF16), fshape, fshape]
        scratch += [pltpu.VMEM((SEQ, HEAD_DIM), F32)] * 2
    else:
        out_shape = [fshape, fshape]
    return pl.pallas_call(
        body, name=f"da_fwd_g{group}", grid=(DA_HEADS_PER_GROUP,),
        in_specs=[seq_spec(CB_DA_Q), seq_spec(CB_DA_K), seq_spec(CB_DA_V), gain_spec, gain_spec,
                  pl.BlockSpec((None, 1, LANE), lambda h: (h, 0, 0))] + [out_spec] * n_other,
        out_specs=[out_spec] * len(out_shape), out_shape=out_shape, scratch_shapes=scratch,
        compiler_params=_params(("parallel",), 10 * blk + 2 * blk + (n_other + 3) * blk),
    )(proj, proj, proj, gq, gk, _da_slopes(group), *[a for pair in others for a in pair])


def _da_bwd(proj, gq, gk, do, lse, dd, dproj, group):
    d = DA_DILATIONS[group]
    ld = SEQ // d
    nqb = ld // DA_QB
    ways = min(DA_WAYS, nqb)
    rways = _da_residue_ways(d, nqb)

    def body(q_ref, k_ref, v_ref, gq_ref, gk_ref, sl_ref, do_ref, lse_ref, dd_ref, dproj_in,
             dproj_ref, dgq_ref, dgk_ref,
             qn_scr, kn_scr, v_scr, dqn_scr, dkn_scr, dvr_scr, rq_scr, rk_scr, dq_scr, dk_scr, dv_scr,
             dq_out, dk_out, dv_out, sems):
        del dproj_in
        head = pl.program_id(0)
        gqv, gkv = gq_ref[...], gk_ref[...]
        slope = sl_ref[:, 0:1]

        copies = []
        for k, (staging, base) in enumerate(((dq_out, CB_DA_Q), (dk_out, CB_DA_K), (dv_out, CB_DA_V))):
            cols = pl.ds(pl.multiple_of((base + 4 * group + head) * LANE, LANE), LANE)
            copies.append(pltpu.make_async_copy(staging, dproj_ref.at[:, cols], sems.at[k]))

        def residues(t, carry):
            rs = [t * rways + u for u in range(rways)]
            sels = [_da_residue_rows(r, 0, ld, d) for r in rs]
            for u, sel in enumerate(sels):
                for x_ref, gv, n_scr, r_scr in ((q_ref, gqv, qn_scr, rq_scr), (k_ref, gkv, kn_scr, rk_scr)):
                    _, rstd, normed = _da_rmsnorm(x_ref[sel, :], gv)
                    n_scr[u] = normed.astype(BF16)
                    r_scr[u] = jnp.broadcast_to(rstd, (ld, HEAD_DIM))
                v_scr[u] = v_ref[sel, :].astype(BF16)
            dkn_scr[...] = jnp.zeros_like(dkn_scr)
            dvr_scr[...] = jnp.zeros_like(dvr_scr)

            def block(u, r, i):
                s, qb, kw, qrows, win = _da_scores(qn_scr.at[u], kn_scr.at[u], slope, i, ld)
                src = _da_residue_rows(r, i * DA_QB, DA_QB, d)
                p = jnp.exp(s - lse_ref[src, :][:, 0:1])
                dob = do_ref[src, :]
                dp = _dot(dob, v_scr[u, win, :], _NT)
                ds = p * (dp - dd_ref[src, :][:, 0:1]) * DA_SCALE
                return u, qrows, win, _dot(p, dob, _TN), _dot(ds, kw, _NN), _dot(ds, qb, _TN)

            def step(i, c2):
                todo = [(u, r, i + w * (nqb // ways)) for u, r in enumerate(rs) for w in range(ways)]
                for u, qrows, win, dv, dqn, dkn in [block(*t) for t in todo]:
                    dvr_scr[u, win, :] += dv
                    dqn_scr[u, qrows, :] = dqn
                    dkn_scr[u, win, :] += dkn
                return c2

            lax.fori_loop(0, nqb // ways, step, 0)

            pq, pk = carry
            for u, sel in enumerate(sels):
                parts = []
                for x_ref, gv, dn_scr, dx_scr, r_scr in ((q_ref, gqv, dqn_scr, dq_scr, rq_scr),
                                                         (k_ref, gkv, dkn_scr, dk_scr, rk_scr)):
                    rstd = r_scr[u]
                    hat = x_ref[sel, :] * rstd
                    dn = dn_scr[u]
                    dyg = dn * gv
                    dx_scr[sel, :] = rstd * (dyg - hat * jnp.mean(dyg * hat, axis=-1, keepdims=True))
                    parts.append(jnp.sum(dn * hat, axis=0, keepdims=True))
                dv_scr[sel, :] = dvr_scr[u]
                pq, pk = pq + parts[0], pk + parts[1]
            return pq, pk

        zero = jnp.zeros((1, HEAD_DIM), F32)
        pq, pk = lax.fori_loop(0, d // rways, residues, (zero, zero))

        @pl.when(pl.program_id(0) == 0)
        def _():
            dgq_ref[...] = pq
            dgk_ref[...] = pk

        @pl.when(pl.program_id(0) > 0)
        def _():
            dgq_ref[...] += pq
            dgk_ref[...] += pk

        @pl.when(head > 0)
        def _():
            for cp in copies:
                cp.wait()

        dq_out[...] = dq_scr[...].astype(BF16)
        dk_out[...] = dk_scr[...].astype(BF16)
        dv_out[...] = dv_scr[...].astype(BF16)
        for cp in copies:
            cp.start()

        @pl.when(head == DA_HEADS_PER_GROUP - 1)
        def _():
            for cp in copies:
                cp.wait()

    seq_spec = lambda base: pl.BlockSpec((SEQ, HEAD_DIM), lambda h: (0, base + 4 * group + h))
    out_spec = pl.BlockSpec((SEQ, HEAD_DIM), lambda h: (0, h))
    gain_spec = pl.BlockSpec((1, HEAD_DIM), lambda h: (0, 0))
    gshape = jax.ShapeDtypeStruct((1, HEAD_DIM), F32)
    blk = SEQ * HEAD_DIM * 4
    return pl.pallas_call(
        body, name=f"da_bwd_g{group}", grid=(DA_HEADS_PER_GROUP,),
        in_specs=[seq_spec(CB_DA_Q), seq_spec(CB_DA_K), seq_spec(CB_DA_V), gain_spec, gain_spec,
                  pl.BlockSpec((None, 1, LANE), lambda h: (h, 0, 0)), out_spec, out_spec, out_spec, _any_spec()],
        out_specs=[_any_spec(), gain_spec, gain_spec],
        out_shape=[jax.ShapeDtypeStruct((SEQ, IN_COLS), BF16), gshape, gshape],
        scratch_shapes=[pltpu.VMEM((rways, ld, HEAD_DIM), BF16)] * 3 + [pltpu.VMEM((rways, ld, HEAD_DIM), F32)] * 5
        + [pltpu.VMEM((SEQ, HEAD_DIM), F32)] * 3 + [pltpu.VMEM((SEQ, HEAD_DIM), BF16)] * 3
        + [pltpu.SemaphoreType.DMA((3,))],
        input_output_aliases={9: 0},
        compiler_params=_params(("arbitrary",), 12 * blk + 3 * blk + 3 * blk + 3 * blk + 2 * blk),
    )(proj, proj, proj, gq, gk, _da_slopes(group), do, lse, dd, dproj)


MEM_WAYS = 4


def _mem_probs(q, kn, gq):
    qhat, rq, qn = _da_rmsnorm(q, gq)
    s = _dot(qn, kn, _NT) * MEM_SCALE
    m = jnp.max(s, axis=-1, keepdims=True)
    e = jnp.exp(s - m)
    p = e / jnp.sum(e, axis=-1, keepdims=True)
    return p, qhat, rq, qn


def _mem_pieces(tq):
    rows = tq // MEM_WAYS
    return [slice(w * rows, (w + 1) * rows) for w in range(MEM_WAYS)]


def _mem_fwd(proj, kv, gq, gk, tq):
    def body(q_ref, k_ref, v_ref, gq_ref, gk_ref, o_ref):
        kn = _da_rmsnorm(k_ref[...], gk_ref[...])[2]
        v, gqv = v_ref[...], gq_ref[...]
        pieces = _mem_pieces(tq)
        outs = [_dot(_mem_probs(q_ref[rs, :], kn, gqv)[0], v, _NN) for rs in pieces]
        for rs, o in zip(pieces, outs):
            o_ref[rs, :] = o.astype(BF16)

    gain_spec = pl.BlockSpec((1, HEAD_DIM), lambda h, i: (0, 0))
    return pl.pallas_call(
        body, name="mem_fwd", grid=(MEM_HEADS, SEQ // tq),
        in_specs=[pl.BlockSpec((tq, HEAD_DIM), lambda h, i: (i, CB_MEM_Q + h)),
                  pl.BlockSpec((N_MEM, HEAD_DIM), lambda h, i: (0, h)),
                  pl.BlockSpec((N_MEM, HEAD_DIM), lambda h, i: (0, MEM_HEADS + h)), gain_spec, gain_spec],
        out_specs=pl.BlockSpec((tq, HEAD_DIM), lambda h, i: (i, h)),
        out_shape=jax.ShapeDtypeStruct((SEQ, MEM_WIDTH), BF16),
        compiler_params=_params(("parallel", "parallel"), 16 * tq * N_MEM * 4),
    )(proj, kv, kv, gq, gk)


def _mem_bwd(proj, kv, gq, gk, do, dproj, tq):
    nq = SEQ // tq

    def body(q_ref, k_ref, v_ref, gq_ref, gk_ref, do_ref, dproj_in, dq_ref, dk_ref, dv_ref, dgq_ref, dgk_ref, dkn_scr):
        del dproj_in
        h, i = pl.program_id(0), pl.program_id(1)
        gqv, gkv = gq_ref[...], gk_ref[...]
        khat, rk, kn = _da_rmsnorm(k_ref[...], gkv)
        v = v_ref[...]

        def piece(rs):
            p, qhat, rq, qn = _mem_probs(q_ref[rs, :], kn, gqv)
            dob = do_ref[rs, :]
            dp = _dot(dob, v, _NT)
            ds = p * (dp - jnp.sum(p * dp, axis=-1, keepdims=True)) * MEM_SCALE
            dqn = _dot(ds, kn, _NN)
            dyg = dqn * gqv
            dq = (rq * (dyg - qhat * jnp.mean(dyg * qhat, axis=-1, keepdims=True))).astype(BF16)
            return dq, _dot(p, dob, _TN), _dot(ds, qn, _TN), jnp.sum(dqn * qhat, axis=0, keepdims=True)

        pieces = _mem_pieces(tq)
        results = [piece(rs) for rs in pieces]
        for rs, res in zip(pieces, results):
            dq_ref[rs, :] = res[0]
        dvp = sum((res[1] for res in results[1:]), results[0][1])
        dknp = sum((res[2] for res in results[1:]), results[0][2])
        dgq_part = sum((res[3] for res in results[1:]), results[0][3])
        first = jnp.logical_and(h == 0, i == 0)

        @pl.when(first)
        def _():
            dgq_ref[...] = dgq_part

        @pl.when(jnp.logical_not(first))
        def _():
            dgq_ref[...] += dgq_part

        @pl.when(i == 0)
        def _():
            dv_ref[...] = dvp
            dkn_scr[...] = dknp

        @pl.when(i > 0)
        def _():
            dv_ref[...] += dvp
            dkn_scr[...] += dknp

        @pl.when(i == nq - 1)
        def _():
            dkn = dkn_scr[...]
            dkg = dkn * gkv
            dk_ref[...] = rk * (dkg - khat * jnp.mean(dkg * khat, axis=-1, keepdims=True))
            dgk_part = jnp.sum(dkn * khat, axis=0, keepdims=True)

            @pl.when(h == 0)
            def _():
                dgk_ref[...] = dgk_part

            @pl.when(h > 0)
            def _():
                dgk_ref[...] += dgk_part

    gain_spec = pl.BlockSpec((1, HEAD_DIM), lambda h, i: (0, 0))
    kvout = pl.BlockSpec((N_MEM, HEAD_DIM), lambda h, i: (0, h))
    return pl.pallas_call(
        body, name="mem_bwd", grid=(MEM_HEADS, nq),
        in_specs=[pl.BlockSpec((tq, HEAD_DIM), lambda h, i: (i, CB_MEM_Q + h)),
                  pl.BlockSpec((N_MEM, HEAD_DIM), lambda h, i: (0, h)),
                  pl.BlockSpec((N_MEM, HEAD_DIM), lambda h, i: (0, MEM_HEADS + h)), gain_spec, gain_spec,
                  pl.BlockSpec((tq, HEAD_DIM), lambda h, i: (i, h)), _any_spec()],
        out_specs=[pl.BlockSpec((tq, HEAD_DIM), lambda h, i: (i, CB_MEM_Q + h)), kvout, kvout, gain_spec, gain_spec],
        out_shape=[jax.ShapeDtypeStruct((SEQ, IN_COLS), BF16), jax.ShapeDtypeStruct((N_MEM, MEM_WIDTH), F32),
                   jax.ShapeDtypeStruct((N_MEM, MEM_WIDTH), F32), jax.ShapeDtypeStruct((1, HEAD_DIM), F32),
                   jax.ShapeDtypeStruct((1, HEAD_DIM), F32)],
        scratch_shapes=[pltpu.VMEM((N_MEM, HEAD_DIM), F32)], input_output_aliases={6: 0},
        compiler_params=_params(("arbitrary", "arbitrary"), 24 * tq * N_MEM * 4),
    )(proj, kv, kv, gq, gk, do, dproj)


def _mesh_position():
    return lax.axis_index("x"), lax.axis_index("y"), lax.axis_index("c")


def _any_spec():
    return pl.BlockSpec(memory_space=pl.ANY)


def _all_gather(shards):
    n = len(shards)

    def body(*refs):
        ins, outs = refs[:n], refs[n:2 * n]
        send_sems, recv_sems, local_sems = refs[2 * n:]
        x, y, c = _mesh_position()
        me, sibling = (x, y, c), (x, y, 1 - c)
        first = (jnp.where(c == 0, 1 - x, x), jnp.where(c == 0, y, 1 - y), c)
        second = (jnp.where(c == 0, x, 1 - x), jnp.where(c == 0, 1 - y, y), c)
        diagonal = (1 - x, 1 - y, c)

        def copy(w, k, block, to, src=None):
            px, py, pc = block
            rows = outs[w].at[4 * px + 2 * py + pc]
            return pltpu.make_async_remote_copy(
                src_ref=rows if src is None else src, dst_ref=rows,
                send_sem=send_sems.at[7 * w + k], recv_sem=recv_sems.at[7 * w + k],
                device_id=to, device_id_type=MESH)

        started = []
        for w in range(n):
            mine = pltpu.make_async_copy(ins[w], outs[w].at[4 * x + 2 * y + c], local_sems.at[w])
            mine.start()
            started.append(mine)
        sends = []
        for w in range(n):
            own = [copy(w, 0, me, sibling, src=ins[w]), copy(w, 1, me, first, src=ins[w]),
                   copy(w, 2, me, second, src=ins[w])]
            for cp in own:
                cp.start()
            sends += own
        for w in range(n):
            copy(w, 1, first, me).wait_recv()
            follow = [copy(w, 3, first, second), copy(w, 4, first, sibling)]
            for cp in follow:
                cp.start()
            copy(w, 2, second, me).wait_recv()
            follow.append(copy(w, 5, second, sibling))
            follow[-1].start()
            sends += follow
        for w in range(n):
            copy(w, 3, diagonal, me).wait_recv()
            passed = copy(w, 6, diagonal, sibling)
            passed.start()
            sends.append(passed)
        for w in range(n):
            for k in (0, 4, 5, 6):
                copy(w, k, sibling, me).wait_recv()
        for cp in sends:
            cp.wait_send()
        for mine in started:
            mine.wait()

    return pl.pallas_call(
        body, name="weights_all_gather",
        in_specs=[_any_spec()] * n, out_specs=[_any_spec()] * n,
        out_shape=[jax.ShapeDtypeStruct((N_DEV,) + s.shape, s.dtype) for s in shards],
        scratch_shapes=[pltpu.SemaphoreType.DMA((7 * n,)), pltpu.SemaphoreType.DMA((7 * n,)),
                        pltpu.SemaphoreType.DMA((n,))],
    )(*shards)


def _chip_of(j, x, y):
    return (1 - x if j & 1 else x, 1 - y if j & 2 else y)


_HBM_SPEC = pl.BlockSpec(memory_space=pltpu.HBM)
_SEM_SPEC = pl.BlockSpec(memory_space=pltpu.SEMAPHORE)
_DATAFLOW_EFFECT = pltpu.SideEffectType.DATAFLOW_SIDE_EFFECTING
TOKEN_SHAPE = (8, D_MODEL)


def _copies_start(name, arrays, n_copies, plan):
    n = len(arrays)

    def body(*refs):
        send_sems, recv_sems, token = refs[n], refs[n + 1], refs[2 * n + 2]
        copies = plan(refs[:n])
        assert len(copies) == n_copies
        for k, (src, dst, dev) in enumerate(copies):
            pltpu.make_async_remote_copy(src_ref=src, dst_ref=dst, send_sem=send_sems.at[k], recv_sem=recv_sems.at[k],
                                         device_id=dev, device_id_type=MESH).start()
        token[...] = jnp.zeros_like(token)

    outs = pl.pallas_call(
        body, name=name,
        out_shape=(pltpu.SemaphoreType.DMA((n_copies,)), pltpu.SemaphoreType.DMA((n_copies,)),
                   *[pltpu.HBM(a.shape, a.dtype) for a in arrays], jax.ShapeDtypeStruct(TOKEN_SHAPE, F32)),
        in_specs=[_HBM_SPEC] * n,
        out_specs=(_SEM_SPEC, _SEM_SPEC, *[_HBM_SPEC] * n, pl.BlockSpec(memory_space=pltpu.VMEM)),
        input_output_aliases={i: i + 2 for i in range(n)},
        compiler_params=pltpu.CompilerParams(has_side_effects=_DATAFLOW_EFFECT),
    )(*[pltpu.with_memory_space_constraint(a, pltpu.HBM) for a in arrays])
    return outs[0], outs[1], list(outs[2:2 + n]), outs[2 + n]


def _copies_wait(name, send_sems, recv_sems, arrays, n_copies, plan, after):
    n = len(arrays)
    after = list(after) if isinstance(after, (list, tuple)) else [after]

    def body(*refs):
        send_ref, recv_ref = refs[n], refs[n + 1]
        copies = plan(refs[:n])
        assert len(copies) == n_copies
        for k, (src, dst, dev) in enumerate(copies):
            cp = pltpu.make_async_remote_copy(src_ref=src, dst_ref=dst, send_sem=send_ref.at[k], recv_sem=recv_ref.at[k],
                                              device_id=dev, device_id_type=MESH)
            cp.wait_send()
            cp.wait_recv()

    outs = pl.pallas_call(
        body, name=name, out_shape=tuple(pltpu.HBM(a.shape, a.dtype) for a in arrays),
        in_specs=[_HBM_SPEC] * n + [_SEM_SPEC, _SEM_SPEC] + [pl.BlockSpec(memory_space=pl.ANY)] * len(after),
        out_specs=tuple([_HBM_SPEC] * n), input_output_aliases={i: i for i in range(n)},
        compiler_params=pltpu.CompilerParams(has_side_effects=_DATAFLOW_EFFECT),
    )(*arrays, send_sems, recv_sems, *after)
    return list(outs)


def _after(small, token):
    return small if token is None else small + token[0:1, :small.shape[-1]]


def _gather_plan_out(n):
    def plan(refs):
        x, y, c = _mesh_position()
        me = 4 * x + 2 * y + c
        copies = []
        for w in range(n):
            land = refs[n + w].at[me]
            copies.append((refs[w], land, (x, y, 1 - c)))
            for j in range(1, 4):
                copies.append((refs[w], land, (*_chip_of(j, x, y), c)))
        return copies
    return plan


def _gather_plan_pass(n):
    def plan(refs):
        x, y, c = _mesh_position()
        copies = []
        for w in range(n):
            for j in range(1, 4):
                px, py = _chip_of(j, x, y)
                rows = refs[w].at[4 * px + 2 * py + c]
                copies.append((rows, rows, (x, y, 1 - c)))
        return copies
    return plan


def _reduce_plan_sibling(n):
    def plan(refs):
        x, y, c = _mesh_position()
        copies = []
        for w in range(n):
            for j in range(4):
                px, py = _chip_of(j, x, y)
                copies.append((refs[w].at[4 * px + 2 * py + (1 - c)], refs[n + w].at[j], (x, y, 1 - c)))
        return copies
    return plan


def _reduce_plan_chips(n):
    def plan(refs):
        x, y, c = _mesh_position()
        copies = []
        for w in range(n):
            for j in range(1, 4):
                copies.append((refs[w].at[j - 1], refs[n + w].at[j - 1], (*_chip_of(j, x, y), c)))
        return copies
    return plan


def _chip_partials(grad, recv, name, tr):
    _, rows, width = grad.shape

    def body(g_ref, r_ref, own_ref, other_ref):
        x, y, c = _mesh_position()
        for j in range(4):
            px, py = _chip_of(j, x, y)
            total = g_ref[4 * px + 2 * py + c].astype(F32) + r_ref[j].astype(F32)
            if j == 0:
                own_ref[...] = total
            else:
                other_ref[j - 1] = total.astype(BF16)

    return pl.pallas_call(
        body, name=name, grid=(rows // tr,),
        in_specs=[pl.BlockSpec((N_DEV, tr, width), lambda i: (0, i, 0)),
                  pl.BlockSpec((4, tr, width), lambda i: (0, i, 0))],
        out_specs=[pl.BlockSpec((tr, width), lambda i: (i, 0)), pl.BlockSpec((3, tr, width), lambda i: (0, i, 0))],
        out_shape=[jax.ShapeDtypeStruct((rows, width), F32), jax.ShapeDtypeStruct((3, rows, width), BF16)],
        compiler_params=_params(("parallel",), 2 * 20 * tr * width * 2 + 8 * tr * width * 4),
    )(grad, recv)


def _adamw_math(w, g, m, v):
    m = ADAM_B1 * m + (1.0 - ADAM_B1) * g
    v = ADAM_B2 * v + (1.0 - ADAM_B2) * (g * g)
    m_hat = m / (1.0 - ADAM_B1 ** ADAM_STEP)
    v_hat = v / (1.0 - ADAM_B2 ** ADAM_STEP)
    delta = -ADAM_LR * (m_hat / (jnp.sqrt(v_hat) + ADAM_EPS) + ADAM_WD * w)
    return delta, m, v


def _adamw_shard(own, recv, w, m, v, name, tr):
    rows, width = own.shape

    def body(own_ref, r_ref, w_ref, m_ref, v_ref, g_ref, d_ref, nm_ref, nv_ref):
        g = own_ref[...]
        for j in range(3):
            g = g + r_ref[j].astype(F32)
        g_ref[...] = g
        d_ref[...], nm_ref[...], nv_ref[...] = _adamw_math(w_ref[...], g, m_ref[...], v_ref[...])

    spec = pl.BlockSpec((tr, width), lambda i: (i, 0))
    shape = jax.ShapeDtypeStruct((rows, width), F32)
    return pl.pallas_call(
        body, name=name, grid=(rows // tr,),
        in_specs=[spec, pl.BlockSpec((3, tr, width), lambda i: (0, i, 0)), spec, spec, spec],
        out_specs=[spec] * 4, out_shape=[shape] * 4,
        compiler_params=_params(("parallel",), 22 * tr * width * 4),
    )(own, recv, w, m, v)


def _small_all_reduce_adamw(gpart, lbpack, wpack, mpack, vpack, after):
    def body(gp_ref, lb_ref, w_ref, m_ref, v_ref, after_ref, g_ref, d_ref, nm_ref, nv_ref, gath_ref, send_sems, recv_sems):
        del after_ref
        x, y, c = _mesh_position()
        me = 4 * x + 2 * y + c
        gath_ref[me] = gp_ref[...]
        copies = []
        for j in range(1, N_DEV):
            peer = (x ^ (j >> 2), y ^ ((j >> 1) & 1), c ^ (j & 1))
            cp = pltpu.make_async_remote_copy(
                src_ref=gp_ref, dst_ref=gath_ref.at[me], send_sem=send_sems.at[j - 1], recv_sem=recv_sems.at[j - 1],
                device_id=peer, device_id_type=MESH)
            cp.start()
            copies.append(cp)
        for cp in copies:
            cp.wait()
        tot = gath_ref[0]
        for s in range(1, N_DEV):
            tot = tot + gath_ref[s]
        lb = lb_ref[...]
        dl = tot[16:32, :] * lb * (1.0 - lb)
        g = jnp.concatenate([tot[0:16, :], dl[0:8, :], -dl[0:8, :], dl[8:16, :], -dl[8:16, :],
                             tot[32:SMALL_GRAD_ROWS, :]], axis=0)
        g_ref[...] = g
        d_ref[...], nm_ref[...], nv_ref[...] = _adamw_math(w_ref[...], g, m_ref[...], v_ref[...])

    vm = pl.BlockSpec(memory_space=pltpu.VMEM)
    shape = jax.ShapeDtypeStruct((SMALL_ROWS, LANE), F32)
    return pl.pallas_call(
        body, name="small_all_reduce_adamw", in_specs=[vm] * 5 + [_any_spec()], out_specs=[vm] * 4,
        out_shape=[shape] * 4,
        scratch_shapes=[pltpu.VMEM((N_DEV, SMALL_GRAD_ROWS, LANE), F32),
                        pltpu.SemaphoreType.DMA((N_DEV - 1,)), pltpu.SemaphoreType.DMA((N_DEV - 1,))],
    )(gpart, lbpack, wpack, mpack, vpack, after)


_SMALL_NAMES = ("norm_mix_gain", "norm_mem_gain", "lb_logits_fw", "lb_logits_bw", "norm_ffn_gain",
                "hg_norm_gain", "da_q_gain", "da_k_gain", "mem_q_gain", "mem_k_gain")
_SMALL_ROW0 = {"norm_mix_gain": 0, "norm_mem_gain": 8, "lb_logits_fw": 16, "lb_logits_bw": 32, "norm_ffn_gain": 48,
               "hg_norm_gain": 56, "da_q_gain": 64, "da_k_gain": 72, "mem_q_gain": 80, "mem_k_gain": 88}
_LOSS_ROW = 96


def _pack_rows(parts, total_rows):
    rows = []
    for p in parts:
        r = p.reshape(-1, LANE)
        rows.append(jnp.pad(r, ((0, -r.shape[0] % 8), (0, 0))))
    used = sum(r.shape[0] for r in rows)
    if total_rows > used:
        rows.append(jnp.zeros((total_rows - used, LANE), F32))
    return jnp.concatenate(rows, axis=0)


def _unpack_small(pack, like):
    out = {}
    for name in _SMALL_NAMES:
        n = like[name].size // LANE
        r0 = _SMALL_ROW0[name]
        out[name] = pack[r0:r0 + n].reshape(like[name].shape)
    return out


def kernel(x, mem, norm_mix_gain, norm_mem_gain, w_in, lb_logits_fw, lb_logits_bw, hg_norm_gain, da_q_gain, da_k_gain, w_mem_kv, mem_q_gain, mem_k_gain, w_proj_hg, w_proj_da, w_proj_mem, w_out, norm_ffn_gain, w_ffn_in, w_ffn_out, loss_target, m_norm_mix_gain, m_norm_mem_gain, m_w_in, m_lb_logits_fw, m_lb_logits_bw, m_hg_norm_gain, m_da_q_gain, m_da_k_gain, m_w_mem_kv, m_mem_q_gain, m_mem_k_gain, m_w_proj_hg, m_w_proj_da, m_w_proj_mem, m_w_out, m_norm_ffn_gain, m_w_ffn_in, m_w_ffn_out, v_norm_mix_gain, v_norm_mem_gain, v_w_in, v_lb_logits_fw, v_lb_logits_bw, v_hg_norm_gain, v_da_q_gain, v_da_k_gain, v_w_mem_kv, v_mem_q_gain, v_mem_k_gain, v_w_proj_hg, v_w_proj_da, v_w_proj_mem, v_w_out, v_norm_ffn_gain, v_w_ffn_in, v_w_ffn_out):
    small_w = dict(norm_mix_gain=norm_mix_gain, norm_mem_gain=norm_mem_gain, lb_logits_fw=lb_logits_fw,
                   lb_logits_bw=lb_logits_bw, norm_ffn_gain=norm_ffn_gain, hg_norm_gain=hg_norm_gain,
                   da_q_gain=da_q_gain, da_k_gain=da_k_gain, mem_q_gain=mem_q_gain, mem_k_gain=mem_k_gain)
    small_m = dict(norm_mix_gain=m_norm_mix_gain, norm_mem_gain=m_norm_mem_gain, lb_logits_fw=m_lb_logits_fw,
                   lb_logits_bw=m_lb_logits_bw, norm_ffn_gain=m_norm_ffn_gain, hg_norm_gain=m_hg_norm_gain,
                   da_q_gain=m_da_q_gain, da_k_gain=m_da_k_gain, mem_q_gain=m_mem_q_gain, mem_k_gain=m_mem_k_gain)
    small_v = dict(norm_mix_gain=v_norm_mix_gain, norm_mem_gain=v_norm_mem_gain, lb_logits_fw=v_lb_logits_fw,
                   lb_logits_bw=v_lb_logits_bw, norm_ffn_gain=v_norm_ffn_gain, hg_norm_gain=v_hg_norm_gain,
                   da_q_gain=v_da_q_gain, da_k_gain=v_da_k_gain, mem_q_gain=v_mem_q_gain, mem_k_gain=v_mem_k_gain)
    big_w = dict(w_in=w_in[0], w_mem_kv=w_mem_kv[0], w_proj_hg=w_proj_hg[0], w_proj_da=w_proj_da[0],
                 w_proj_mem=w_proj_mem[0], w_out=w_out[0], w_ffn_in=w_ffn_in[0], w_ffn_out=w_ffn_out[0])
    big_m = dict(w_in=m_w_in[0], w_mem_kv=m_w_mem_kv[0], w_proj_hg=m_w_proj_hg[0], w_proj_da=m_w_proj_da[0],
                 w_proj_mem=m_w_proj_mem[0], w_out=m_w_out[0], w_ffn_in=m_w_ffn_in[0], w_ffn_out=m_w_ffn_out[0])
    big_v = dict(w_in=v_w_in[0], w_mem_kv=v_w_mem_kv[0], w_proj_hg=v_w_proj_hg[0], w_proj_da=v_w_proj_da[0],
                 w_proj_mem=v_w_proj_mem[0], w_out=v_w_out[0], w_ffn_in=v_w_ffn_in[0], w_ffn_out=v_w_ffn_out[0])

    row_tile = dict(w_in=128, w_mem_kv=128, w_proj_hg=128, w_proj_da=512, w_proj_mem=512, w_out=128,
                    w_ffn_in=128, w_ffn_out=352)
    rest = _BIG_NAMES[1:]
    shards = [big_w[n].astype(BF16) for n in rest]
    state = {}
    big_out = {}

    def reduce_start(group, stacked):
        names = tuple(stacked)
        arrays = [stacked[n] for n in names] + [lax.empty((4,) + stacked[n].shape[1:], BF16) for n in names]
        plan = _reduce_plan_sibling(len(names))
        send, recv, thru, token = _copies_start(f"grads_{group}_sibling_start", arrays, 4 * len(names), plan)
        state[group] = dict(names=names, plan=plan, send=send, recv=recv, arrays=thru)
        return token

    def reduce_middle(group, after):
        st = state[group]
        names, k = st["names"], len(st["names"])
        thru = _copies_wait(f"grads_{group}_sibling_wait", st["send"], st["recv"], st["arrays"], 4 * k, st["plan"], after)
        partials = [_chip_partials(thru[i], thru[k + i], f"chip_partials_{n}", row_tile[n]) for i, n in enumerate(names)]
        arrays = [p[1] for p in partials] + [lax.empty(p[1].shape, BF16) for p in partials]
        plan = _reduce_plan_chips(k)
        send, recv, thru2, token = _copies_start(f"grads_{group}_chips_start", arrays, 3 * k, plan)
        state[group] = dict(names=names, plan=plan, send=send, recv=recv, arrays=thru2, own=[p[0] for p in partials])
        return token

    def reduce_finish(group, after):
        st = state.pop(group)
        names, k = st["names"], len(st["names"])
        thru = _copies_wait(f"grads_{group}_chips_wait", st["send"], st["recv"], st["arrays"], 3 * k, st["plan"], after)
        for i, n in enumerate(names):
            big_out[n] = _adamw_shard(st["own"][i], thru[k + i], big_w[n], big_m[n], big_v[n], "adamw_" + n, row_tile[n])

    step = _local_step_stages(x[0], mem[0], loss_target[0], small_w)
    event, payload = next(step)
    local = None
    while True:
        reply = None
        if event == "gather_w_in":
            reply = _all_gather([big_w["w_in"].astype(BF16)])[0]
        elif event == "begin":
            nr = len(rest)
            me = 4 * lax.axis_index("x") + 2 * lax.axis_index("y") + lax.axis_index("c")
            arrays = shards + [lax.dynamic_update_slice(lax.empty((N_DEV,) + s.shape, BF16), s[None], (me, 0, 0))
                               for s in shards]
            plan = _gather_plan_out(nr)
            send, recv, thru, reply = _copies_start("weights_rest_out_start", arrays, 4 * nr, plan)
            state["gather"] = dict(plan=plan, send=send, recv=recv, arrays=thru)
        elif event == "after_gla_fwd":
            st = state["gather"]
            nr = len(rest)
            thru = _copies_wait("weights_rest_out_wait", st["send"], st["recv"], st["arrays"], 4 * nr, st["plan"], payload)
            plan = _gather_plan_pass(nr)
            send, recv, lands, reply = _copies_start("weights_rest_pass_start", thru[nr:], 3 * nr, plan)
            state["gather"] = dict(plan=plan, send=send, recv=recv, arrays=lands)
        elif event == "need_weights":
            st = state.pop("gather")
            nr = len(rest)
            lands = _copies_wait("weights_rest_pass_wait", st["send"], st["recv"], st["arrays"], 3 * nr, st["plan"], payload)
            reply = dict(zip(rest, lands))
        elif event == "grads_ffn":
            reply = reduce_start("ffn", payload)
        elif event == "after_gate_merge_bwd":
            reply = reduce_middle("ffn", payload)
        elif event == "grads_mix":
            reply = reduce_start("mix", payload)
        elif event == "after_da_bwd_g0":
            reply = reduce_middle("mix", payload)
            reduce_finish("ffn", payload)
        elif event == "after_gla_bwd":
            reduce_finish("mix", payload)
        elif event == "grads_in":
            reply = reduce_start("in", payload)
        elif event == "after_proj_bwd_act_top":
            reply = reduce_middle("in", payload)
        elif event == "end":
            local = payload
            reduce_finish("in", [local["grad_x"]] + [big_out[n][0] for n in rest])
            break
        event, payload = step.send(reply)
    grad_x = local["grad_x"]

    wpack = _pack_rows([small_w[n] for n in _SMALL_NAMES], SMALL_ROWS)
    mpack = _pack_rows([small_m[n] for n in _SMALL_NAMES], SMALL_ROWS)
    vpack = _pack_rows([small_v[n] for n in _SMALL_NAMES], SMALL_ROWS)
    gs, ds, ms, vs = _small_all_reduce_adamw(local["gpart"], local["lbpack"], wpack, mpack, vpack, big_out["w_in"][0])
    loss = gs[_LOSS_ROW, 0]
    small_out = [_unpack_small(t, small_w) for t in (gs, ds, ms, vs)]

    order = ("norm_mix_gain", "norm_mem_gain", "w_in", "lb_logits_fw", "lb_logits_bw", "hg_norm_gain", "da_q_gain",
             "da_k_gain", "w_mem_kv", "mem_q_gain", "mem_k_gain", "w_proj_hg", "w_proj_da", "w_proj_mem", "w_out",
             "norm_ffn_gain", "w_ffn_in", "w_ffn_out")
    outs = [loss, grad_x[None]]
    for kind in range(4):
        for n in order:
            outs.append(big_out[n][kind][None] if n in big_out else small_out[kind][n])
    return tuple(outs)


_BIG_NAMES = ("w_in", "w_mem_kv", "w_proj_hg", "w_proj_da", "w_proj_mem", "w_out", "w_ffn_in", "w_ffn_out")


def _local_step(xs, mems, target, sw, wg):
    step = _local_step_stages(xs, mems, target, sw)
    stacked = {}
    event, payload = next(step)
    while event != "end":
        reply = None
        if event.startswith("grads_"):
            stacked.update(payload)
        elif event == "need_weights":
            reply = wg
        elif event == "gather_w_in":
            reply = wg["w_in"]
        event, payload = step.send(reply)
    return dict(payload, stacked=stacked)


def _local_step_stages(xs, mems, target, sw):
    norm_mix_gain, norm_mem_gain, norm_ffn_gain = sw["norm_mix_gain"], sw["norm_mem_gain"], sw["norm_ffn_gain"]
    lb_logits_fw, lb_logits_bw, hg_norm_gain = sw["lb_logits_fw"], sw["lb_logits_bw"], sw["hg_norm_gain"]
    da_q_gain, da_k_gain, mem_q_gain, mem_k_gain = sw["da_q_gain"], sw["da_k_gain"], sw["mem_q_gain"], sw["mem_k_gain"]

    win_st = yield "gather_w_in", None
    token = yield "begin", None
    lb_fw = _lb_table(lb_logits_fw, "lb_table_fw")
    lb_bw = _lb_table(lb_logits_bw, "lb_table_bw")
    lb = jnp.concatenate([lb_fw, lb_bw], axis=0).reshape(2, HG_HEADS, 1, HEAD_DIM)
    h, h_t = _rmsnorm_fwd(xs, _after(norm_mix_gain, token), "norm_mix_fwd", 512, transposed=True)
    proj = _matmul(h, win_st, "nn", F32, 1024, IN_SHARD, D_MODEL, "proj_fwd", b_stacked=True, n_outer=True)
    hg_b = _gla_decay(proj, lb)
    o_hg, o_pre, states, hg_qt, hg_decay = _gla_fwd(proj, lb, hg_norm_gain, hg_b)
    token = yield "after_gla_fwd", o_hg
    da01 = [_da_fwd(proj, _after(da_q_gain, token), da_k_gain, g) for g in (0, 1)]
    o_da, o_da32, lse_da = _da_fwd(proj, _after(da_q_gain, token), da_k_gain, 2, others=da01)
    wg = yield "need_weights", o_da
    wkv = wg["w_mem_kv"].reshape(D_MODEL, 2 * MEM_WIDTH)
    wphg = wg["w_proj_hg"].reshape(D_MODEL, D_MODEL)
    wpda = jnp.transpose(wg["w_proj_da"], (1, 0, 2)).reshape(DA_WIDTH, D_MODEL)
    wpmem = jnp.transpose(wg["w_proj_mem"], (1, 0, 2)).reshape(MEM_WIDTH, D_MODEL)
    wout = wg["w_out"].reshape(D_MODEL, D_MODEL)
    wfin = jnp.transpose(wg["w_ffn_in"], (1, 0, 2)).reshape(D_MODEL, 2 * D_FF)
    wfout = wg["w_ffn_out"].reshape(D_FF, D_MODEL)
    mem_n = _rmsnorm_fwd(mems, norm_mem_gain, "norm_mem_fwd", N_MEM)
    kv = _matmul(mem_n, wkv, "nn", F32, N_MEM, 1024, D_MODEL, "mem_kv_fwd")
    o_mem = _mem_fwd(proj, kv, mem_q_gain, mem_k_gain, 1024)
    branch_w = (wphg, wpda, wpmem)
    merged, t_hg, t_da, t_mem = _branch_merge_fwd(proj, (o_hg, o_da, o_mem), branch_w, 512)
    x1, h2, h2_t = _residual_rmsnorm_fwd(xs, merged, wout, norm_ffn_gain, "out_norm_ffn_fwd", 512)
    ffn_a, ffn_b, act, act_t = _ffn_in_swiglu(h2, wfin, 1024, 1408)
    dy, dyb, loss_part = _ffn_out_loss_head(x1, act, wfout, target, 512)

    dab = _ffn_out_bwd_swiglu(dyb, wfout, ffn_a, ffn_b, 1024, 1408)
    g_wfout = _matmul(act_t, dyb, "nn", BF16, 1408, 1024, 2048, "ffn_out_bwd_w")
    g_wfin = _matmul(h2_t, dab, "nn", BF16, 1024, 1408, 2048, "ffn_in_bwd_w", b_parts=True)
    token = yield "grads_ffn", dict(
        w_ffn_in=jnp.transpose(g_wfin.reshape(D_MODEL, N_DEV, 2 * D_FF // N_DEV), (1, 0, 2)),
        w_ffn_out=g_wfout.reshape(N_DEV, D_FF // N_DEV, D_MODEL))
    dx1, dx1b, g_norm_ffn = _matmul_rmsnorm_bwd(dab, wfin, x1, dy, _after(norm_ffn_gain, token),
                                                "ffn_in_bwd_act_norm", 512, D_FF, a_parts=True)
    g_wout = _matmul(merged, dx1b, "tn", BF16, 1024, 1024, 2048, "out_bwd_w")
    dt_hg, dt_da, dt_mem, do_pre, do_da, do_mem, dd_da, g_hg_norm, dproj = _gate_merge_bwd(
        proj, (t_hg, t_da, t_mem), branch_w, dx1b, wout, o_pre, hg_norm_gain, o_da32, _dproj_buffer(), 256)
    token = yield "after_gate_merge_bwd", dt_hg
    g_wphg = _matmul(o_hg, dt_hg, "tn", BF16, 1024, 1024, 2048, "proj_hg_bwd_w", after=token)
    g_wpda = _matmul(o_da, dt_da, "tn", BF16, DA_WIDTH, D_MODEL, 2048, "proj_da_bwd_w")
    g_wpmem = _matmul(o_mem, dt_mem, "tn", BF16, MEM_WIDTH, D_MODEL, 2048, "proj_mem_bwd_w")
    by_owner = lambda g: jnp.transpose(g.reshape(g.shape[0], N_DEV, D_MODEL // N_DEV), (1, 0, 2))
    g_wpda, g_wpmem = by_owner(g_wpda), by_owner(g_wpmem)

    dproj, dk_mem, dv_mem, g_mem_q, g_mem_k = _mem_bwd(proj, kv, mem_q_gain, mem_k_gain, do_mem, dproj, 1024)
    dkv = jnp.concatenate([dk_mem, dv_mem], axis=1).astype(BF16)
    g_wkv = _matmul(mem_n, dkv, "tn", BF16, 1024, 1024, N_MEM, "mem_kv_bwd_w")
    dmem_n = _matmul(dkv, wkv, "nt", F32, N_MEM, 1024, 1024, "mem_kv_bwd_act")
    g_norm_mem = _gain_grad(mems, dmem_n, "norm_mem_bwd")
    token = yield "grads_mix", dict(
        w_mem_kv=g_wkv.reshape(N_DEV, D_MODEL // N_DEV, 2 * MEM_WIDTH),
        w_proj_hg=g_wphg.reshape(N_DEV, D_MODEL // N_DEV, D_MODEL),
        w_proj_da=g_wpda, w_proj_mem=g_wpmem,
        w_out=g_wout.reshape(N_DEV, D_MODEL // N_DEV, D_MODEL))

    dproj, g_da_q, g_da_k = _da_bwd(proj, _after(da_q_gain, token), da_k_gain, do_da, lse_da, dd_da, dproj, 0)
    token = yield "after_da_bwd_g0", g_da_q
    for g in (1, 2):
        dproj, gq_part, gk_part = _da_bwd(proj, _after(da_q_gain, token), da_k_gain, do_da, lse_da, dd_da, dproj, g)
        g_da_q, g_da_k = g_da_q + gq_part, g_da_k + gk_part

    dproj, dlb = _gla_bwd(proj, lb, states, hg_qt, hg_decay, hg_b, do_pre, dproj)
    yield "after_gla_bwd", dlb

    g_win =_matmul(h_t, dproj, "nn", BF16, 1024, IN_SHARD, 2048, "proj_bwd_w", out_stacked=True)
    token = yield "grads_in", dict(w_in=g_win)
    grad_x, g_mix_top = _matmul_rmsnorm_bwd(dproj, win_st, xs, dx1, norm_mix_gain, "proj_bwd_act_norm_top", 1024,
                                            IN_SHARD, b_stacked=True, after=token, m_blocks=(0, 1), with_bf16=False)
    token = yield "after_proj_bwd_act_top", g_mix_top
    grad_x, g_mix_bottom = _matmul_rmsnorm_bwd(dproj, win_st, xs, dx1, norm_mix_gain, "proj_bwd_act_norm_bottom", 1024,
                                               IN_SHARD, b_stacked=True, after=token, m_blocks=(1, 3),
                                               out_into=grad_x, with_bf16=False)
    g_norm_mix = g_mix_top + g_mix_bottom

    gpart = _pack_rows([g_norm_mix, g_norm_mem, dlb[0], dlb[1], g_norm_ffn, g_hg_norm, g_da_q, g_da_k,
                        g_mem_q, g_mem_k, loss_part], SMALL_GRAD_ROWS)
    lbpack = jnp.concatenate([lb_fw.reshape(8, LANE), lb_bw.reshape(8, LANE)], axis=0)
    yield "end", dict(grad_x=grad_x, gpart=gpart, lbpack=lbpack)
```

```python
import numpy as np
import jax
import jax.numpy as jnp
from jax import lax
from jax.experimental import pallas as pl
from jax.experimental.pallas import tpu as pltpu

F32 = jnp.float32
BF16 = jnp.bfloat16
MESH = pl.DeviceIdType.MESH

SEQ = 4096
D_MODEL = 1024
N_DEV = 8
N_MEM = 256
RMS_EPS = 1e-6
NEG_INF = -1e30
LANE = 128
HEAD_DIM = 128
HG_HEADS = 8
HG_CHUNK = 64
HG_SCALE = HEAD_DIM ** -0.5
DA_DILATIONS = (1, 4, 16)
DA_RADIUS = 64
DA_HEADS_PER_GROUP = 4
DA_HEADS = 12
DA_WIDTH = 512
DA_SCALE = HEAD_DIM ** -0.5
DA_QB = 128
DA_WIN = 256
DA_WAYS = 4
DA_FWD_WAYS = 4
MEM_HEADS = 4
MEM_WIDTH = 512
MEM_SCALE = HEAD_DIM ** -0.5
D_FF = 2816
IN_COLS = 13312
IN_SHARD = IN_COLS // N_DEV
CB_HG_Q, CB_F, CB_HG_I, CB_HG_G = 0, 8, 24, 32
CB_DA_Q, CB_DA_K, CB_DA_V, CB_MEM_Q = 40, 52, 64, 76
ADAM_LR, ADAM_B1, ADAM_B2, ADAM_EPS, ADAM_WD, ADAM_STEP = 0.001, 0.9, 0.999, 1e-08, 0.01, 10
VMEM_BYTES_V7X = 64 * 1024 * 1024
SMALL_ROWS = 104
SMALL_GRAD_ROWS = 88

_NN = (((1,), (0,)), ((), ()))
_NT = (((1,), (1,)), ((), ()))
_TN = (((0,), (0,)), ((), ()))


def _dot(a, b, dims):
    return lax.dot_general(a.astype(BF16), b.astype(BF16), dims, preferred_element_type=F32)


def _sigmoid(x):
    return 0.5 * jnp.tanh(0.5 * x) + 0.5


def _params(semantics, est_bytes):
    limit = int(min(VMEM_BYTES_V7X - (6 << 20), max(56 << 20, est_bytes * 3 // 2)))
    return pltpu.CompilerParams(dimension_semantics=semantics, vmem_limit_bytes=limit)


def _nbytes(shape, dtype):
    return int(np.prod(shape)) * jnp.dtype(dtype).itemsize


def _alibi_slopes(n):
    return (2.0 ** (-8.0 * np.arange(1, n + 1) / n)).astype(np.float32)


def _matmul(a, b, mode, out_dtype, tm, tn, tk, name, b_stacked=False, out_stacked=False, n_outer=False, after=None,
            m_blocks=None, out_into=None, a_parts=False, b_parts=False):
    if a_parts:
        assert mode == "nt" and tk == a.shape[2]
        m, kdim = a.shape[1], a.shape[0] * a.shape[2]
    elif mode == "tn":
        kdim, m = a.shape
    else:
        m, kdim = a.shape
    if b_parts:
        assert mode == "nn" and not b_stacked and b.shape[2] % tn == 0
        n = b.shape[0] * b.shape[2]
    elif b_stacked:
        if mode == "nn":
            n = b.shape[0] * b.shape[2]
            assert tn == b.shape[2] and tk == kdim == b.shape[1]
        else:
            assert mode == "nt" and tk == b.shape[2] and b.shape[0] * tk == kdim
            n = b.shape[1]
    else:
        n = b.shape[0] if mode == "nt" else b.shape[1]
    assert m % tm == 0 and n % tn == 0 and kdim % tk == 0
    gm, gn, gk = m // tm, n // tn, kdim // tk
    i0 = 0
    if m_blocks is not None:
        assert mode != "tn" and not out_stacked
        i0, gm = m_blocks

    def ijk(f):
        if n_outer:
            return lambda j, i, k: f(i + i0, j, k)
        return lambda i, j, k: f(i + i0, j, k)

    if a_parts:
        a_spec = pl.BlockSpec((None, tm, tk), ijk(lambda i, j, k: (k, i, 0)))
    elif mode == "tn":
        a_spec = pl.BlockSpec((tk, tm), ijk(lambda i, j, k: (k, i)))
    else:
        a_spec = pl.BlockSpec((tm, tk), ijk(lambda i, j, k: (i, k)))
    if b_parts:
        per_part = b.shape[2] // tn
        b_spec = pl.BlockSpec((None, tk, tn), ijk(lambda i, j, k: (j // per_part, k, j % per_part)))
    elif b_stacked and mode == "nn":
        b_spec = pl.BlockSpec((None, tk, tn), ijk(lambda i, j, k: (j, 0, 0)))
    elif b_stacked:
        b_spec = pl.BlockSpec((None, tn, tk), ijk(lambda i, j, k: (k, j, 0)))
    elif mode == "nt":
        b_spec = pl.BlockSpec((tn, tk), ijk(lambda i, j, k: (j, k)))
    else:
        b_spec = pl.BlockSpec((tk, tn), ijk(lambda i, j, k: (k, j)))
    if out_stacked:
        assert tm == m
        out_shape = jax.ShapeDtypeStruct((gn, m, tn), out_dtype)
        o_spec = pl.BlockSpec((None, tm, tn), ijk(lambda i, j, k: (j, i, 0)))
    else:
        out_shape = jax.ShapeDtypeStruct((m, n), out_dtype)
        o_spec = pl.BlockSpec((tm, tn), ijk(lambda i, j, k: (i, j)))
    dims = {"nn": _NN, "nt": _NT, "tn": _TN}[mode]

    n_in = 2 + (after is not None) + (out_into is not None)

    def body(*refs):
        a_ref, b_ref, o_ref = refs[0], refs[1], refs[n_in]
        part = _dot(a_ref[...], b_ref[...], dims)
        if gk == 1:
            o_ref[...] = part.astype(out_dtype)
            return
        acc_ref = refs[-1]
        k = pl.program_id(2)

        @pl.when(k == 0)
        def _():
            acc_ref[...] = part

        @pl.when(jnp.logical_and(k > 0, k < gk - 1))
        def _():
            acc_ref[...] += part

        @pl.when(k == gk - 1)
        def _():
            o_ref[...] = (acc_ref[...] + part).astype(out_dtype)

    a_tile = _nbytes((tm, tk), a.dtype)
    b_tile = _nbytes((tk, tn), b.dtype)
    o_tile = _nbytes((tm, tn), out_dtype)
    est = 2 * (a_tile + b_tile + o_tile) + 3 * tm * tn * 4 + (a_tile + b_tile)
    grid = (gn, gm, gk) if n_outer else (gm, gn, gk)
    operands, in_specs = [a, b], [a_spec, b_spec]
    if after is not None:
        operands.append(after)
        in_specs.append(pl.BlockSpec(memory_space=pl.ANY))
    aliases = {}
    if out_into is not None:
        aliases = {len(operands): 0}
        operands.append(out_into)
        in_specs.append(pl.BlockSpec(memory_space=pl.ANY))
    return pl.pallas_call(
        body, name=name, grid=grid, in_specs=in_specs, out_specs=o_spec, out_shape=out_shape,
        scratch_shapes=[] if gk == 1 else [pltpu.VMEM((tm, tn), F32)], input_output_aliases=aliases,
        compiler_params=_params(("parallel", "parallel", "arbitrary"), est),
    )(*operands)


def _row_spec(tr, width, col_block=0):
    return pl.BlockSpec((tr, width), lambda i: (i, col_block))


def _bcast_spec(width):
    return pl.BlockSpec((1, width), lambda i: (0, 0))


def _col_spec(width, tr):
    return pl.BlockSpec((width, tr), lambda i: (0, i))


def _rmsnorm_fwd(x, gain, name, tr, transposed=False):
    rows, width = x.shape

    def body(x_ref, g_ref, o_ref, *t_ref):
        xv = x_ref[...]
        r = lax.rsqrt(jnp.mean(xv * xv, axis=-1, keepdims=True) + RMS_EPS)
        h = xv * r * g_ref[...]
        o_ref[...] = h.astype(BF16)
        if transposed:
            t_ref[0][...] = h.T.astype(BF16)

    out_specs, out_shape = [_row_spec(tr, width)], [jax.ShapeDtypeStruct((rows, width), BF16)]
    if transposed:
        out_specs.append(_col_spec(width, tr))
        out_shape.append(jax.ShapeDtypeStruct((width, rows), BF16))
    out = pl.pallas_call(
        body, name=name, grid=(rows // tr,), in_specs=[_row_spec(tr, width), _bcast_spec(width)],
        out_specs=out_specs, out_shape=out_shape,
        compiler_params=_params(("parallel",), 10 * tr * width * 4),
    )(x, gain)
    return out if transposed else out[0]


def _residual_rmsnorm_fwd(x, merged, wout, gain, name, tr):
    rows, width = x.shape

    def body(x_ref, m_ref, w_ref, g_ref, x1_ref, h_ref, ht_ref):
        xv = x_ref[...] + _dot(m_ref[...], w_ref[...], _NN)
        x1_ref[...] = xv
        r = lax.rsqrt(jnp.mean(xv * xv, axis=-1, keepdims=True) + RMS_EPS)
        h = xv * r * g_ref[...]
        h_ref[...] = h.astype(BF16)
        ht_ref[...] = h.T.astype(BF16)

    return pl.pallas_call(
        body, name=name, grid=(rows // tr,),
        in_specs=[_row_spec(tr, width), _row_spec(tr, merged.shape[1]),
                  pl.BlockSpec(wout.shape, lambda i: (0, 0)), _bcast_spec(width)],
        out_specs=[_row_spec(tr, width), _row_spec(tr, width), _col_spec(width, tr)],
        out_shape=[jax.ShapeDtypeStruct((rows, width), F32), jax.ShapeDtypeStruct((rows, width), BF16),
                   jax.ShapeDtypeStruct((width, rows), BF16)],
        compiler_params=_params(("parallel",), 14 * tr * width * 4),
    )(x, merged, wout, gain)


def _matmul_rmsnorm_bwd(a, b, x, dres, gain, name, tm, tk, a_parts=False, b_stacked=False, after=None,
                        m_blocks=None, out_into=None, with_bf16=True):
    width = x.shape[1]
    m = a.shape[1] if a_parts else a.shape[0]
    kdim = a.shape[0] * a.shape[2] if a_parts else a.shape[1]
    gk = kdim // tk
    i0, gm = (0, m // tm) if m_blocks is None else m_blocks
    n_in = 5 + (after is not None) + (out_into is not None)

    def body(*refs):
        a_ref, b_ref, x_ref, dres_ref, g_ref = refs[:5]
        outs = refs[n_in:]
        dx_ref, dg_ref, acc_ref = outs[0], outs[-2], outs[-1]
        i, k = pl.program_id(0), pl.program_id(1)
        part = _dot(a_ref[...], b_ref[...], _NT)

        @pl.when(k == 0)
        def _():
            acc_ref[...] = part

        @pl.when(jnp.logical_and(k > 0, k < gk - 1))
        def _():
            acc_ref[...] += part

        @pl.when(k == gk - 1)
        def _():
            dhv = acc_ref[...] + part if gk > 1 else part
            xv = x_ref[...]
            r = lax.rsqrt(jnp.mean(xv * xv, axis=-1, keepdims=True) + RMS_EPS)
            xhat = xv * r
            dyg = dhv * g_ref[...]
            dx = dres_ref[...] + r * (dyg - xhat * jnp.mean(dyg * xhat, axis=-1, keepdims=True))
            dx_ref[...] = dx
            if with_bf16:
                outs[1][...] = dx.astype(BF16)
            gpart = jnp.sum(dhv * xhat, axis=0, keepdims=True)

            @pl.when(i == 0)
            def _():
                dg_ref[...] = gpart

            @pl.when(i > 0)
            def _():
                dg_ref[...] += gpart

    rows = lambda width_: pl.BlockSpec((tm, width_), lambda i, k: (i + i0, 0))
    if a_parts:
        a_spec = pl.BlockSpec((None, tm, tk), lambda i, k: (k, i + i0, 0))
    else:
        a_spec = pl.BlockSpec((tm, tk), lambda i, k: (i + i0, k))
    if b_stacked:
        b_spec = pl.BlockSpec((None, width, tk), lambda i, k: (k, 0, 0))
    else:
        b_spec = pl.BlockSpec((width, tk), lambda i, k: (0, k))
    operands = [a, b, x, dres, gain]
    in_specs = [a_spec, b_spec, rows(width), rows(width), pl.BlockSpec((1, width), lambda i, k: (0, 0))]
    for extra in (after, out_into):
        if extra is not None:
            operands.append(extra)
            in_specs.append(pl.BlockSpec(memory_space=pl.ANY))
    aliases = {} if out_into is None else {len(operands) - 1: 0}
    out_specs = [rows(width)] + ([rows(width)] if with_bf16 else []) + [pl.BlockSpec((1, width), lambda i, k: (0, 0))]
    out_shape = [jax.ShapeDtypeStruct((m, width), F32)] + ([jax.ShapeDtypeStruct((m, width), BF16)] if with_bf16 else [])
    out_shape.append(jax.ShapeDtypeStruct((1, width), F32))
    est = 4 * tm * tk + 4 * width * tk + 12 * tm * width * 4
    return pl.pallas_call(
        body, name=name, grid=(gm, gk), in_specs=in_specs, out_specs=out_specs, out_shape=out_shape,
        scratch_shapes=[pltpu.VMEM((tm, width), F32)], input_output_aliases=aliases,
        compiler_params=_params(("arbitrary", "arbitrary"), est),
    )(*operands)


def _gain_grad(x, dh, name):
    rows, width = x.shape

    def body(x_ref, dh_ref, dg_ref):
        xv = x_ref[...]
        r = lax.rsqrt(jnp.mean(xv * xv, axis=-1, keepdims=True) + RMS_EPS)
        dg_ref[...] = jnp.sum(dh_ref[...] * xv * r, axis=0, keepdims=True)

    return pl.pallas_call(
        body, name=name, grid=(1,), in_specs=[_row_spec(rows, width), _row_spec(rows, width)],
        out_specs=_bcast_spec(width), out_shape=jax.ShapeDtypeStruct((1, width), F32),
        compiler_params=_params(("arbitrary",), 6 * rows * width * 4),
    )(x, dh)


def _lb_table(logits, name):
    slots, width = logits.shape

    def body(l_ref, o_ref):
        lv = l_ref[...]
        mx = jnp.max(lv, axis=0, keepdims=True)
        e = jnp.exp(lv - mx)
        o_ref[...] = e[0:1, :] / jnp.sum(e, axis=0, keepdims=True)

    return pl.pallas_call(
        body, name=name, grid=(1,), in_specs=[pl.BlockSpec((slots, width), lambda i: (0, 0))],
        out_specs=_bcast_spec(width), out_shape=jax.ShapeDtypeStruct((1, width), F32),
    )(logits)


def _branch_merge_fwd(proj, outs, weights, tr):
    w = D_MODEL

    def body(ghg_ref, gda_ref, gmem_ref, ohg_ref, oda_ref, omem_ref, whg_ref, wda_ref, wmem_ref,
             m_ref, thg_ref, tda_ref, tmem_ref):
        acc = None
        for g_ref, o_ref, w_ref, t_ref in ((ghg_ref, ohg_ref, whg_ref, thg_ref), (gda_ref, oda_ref, wda_ref, tda_ref),
                                           (gmem_ref, omem_ref, wmem_ref, tmem_ref)):
            t = _dot(o_ref[...], w_ref[...], _NN)
            t_ref[...] = t.astype(BF16)
            term = _sigmoid(g_ref[...]) * t
            acc = term if acc is None else acc + term
        m_ref[...] = acc.astype(BF16)

    whole = lambda a: pl.BlockSpec(a.shape, lambda i: (0, 0))
    shape = jax.ShapeDtypeStruct((SEQ, w), BF16)
    return pl.pallas_call(
        body, name="branch_merge_fwd", grid=(SEQ // tr,),
        in_specs=[_row_spec(tr, w, 10), _row_spec(tr, w, 11), _row_spec(tr, w, 12)]
        + [_row_spec(tr, o.shape[1]) for o in outs] + [whole(wt) for wt in weights],
        out_specs=[_row_spec(tr, w)] * 4, out_shape=[shape] * 4,
        compiler_params=_params(("parallel",), 20 * tr * w * 4),
    )(proj, proj, proj, *outs, *weights)


def _dproj_buffer():
    return lax.empty((SEQ, IN_COLS), BF16)


def _gate_merge_bwd(proj, ts, weights, dx1b, wout, o_pre, hg_gain, o_da32, dproj, tr):
    w = D_MODEL
    steps = SEQ // tr
    gate_col0 = 10 * w
    hg_gate_col0 = CB_HG_G * LANE

    def body(ghg_ref, gda_ref, gmem_ref, hgg_ref, thg_ref, tda_ref, tmem_ref, whg_ref, wda_ref, wmem_ref, dx_ref,
             wout_ref, opre_ref, gain_ref, oda_ref, dproj_in,
             dthg_ref, dtda_ref, dtmem_ref, dopre_ref, doda_ref, domem_ref, dd_ref, dgain_ref, dproj_ref,
             stage, stage_hg, sems):
        del dproj_in
        i = pl.program_id(0)
        slot = i % 2
        rows = pl.ds(pl.multiple_of(i * tr, tr), tr)

        def slot_copies(s):
            return (pltpu.make_async_copy(stage.at[s], dproj_ref.at[rows, pl.ds(gate_col0, 3 * w)], sems.at[s]),
                    pltpu.make_async_copy(stage_hg.at[s], dproj_ref.at[rows, pl.ds(hg_gate_col0, w)], sems.at[2 + s]))

        @pl.when(i >= 2)
        def _():
            for cp in slot_copies(slot):
                cp.wait()

        dm = _dot(dx_ref[...], wout_ref[...], _NT)
        branches = ((ghg_ref, thg_ref, whg_ref, dthg_ref), (gda_ref, tda_ref, wda_ref, dtda_ref),
                    (gmem_ref, tmem_ref, wmem_ref, dtmem_ref))
        dos = []
        for b, (g_ref, t_ref, w_ref, dt_ref) in enumerate(branches):
            s = _sigmoid(g_ref[...])
            dt = (s * dm).astype(BF16)
            dt_ref[...] = dt
            dos.append(_dot(dt, w_ref[...], _NT))
            stage[slot, :, b * w:(b + 1) * w] = (dm * t_ref[...].astype(F32) * s * (1.0 - s)).astype(BF16)
        do_hg, do_da, do_mem = dos
        doda_ref[...] = do_da
        domem_ref[...] = do_mem

        gainv = gain_ref[...]
        part = jnp.zeros((1, HEAD_DIM), F32)
        for h in range(HG_HEADS):
            hs = slice(h * HEAD_DIM, (h + 1) * HEAD_DIM)
            o = opre_ref[:, hs]
            r = lax.rsqrt(jnp.mean(o * o, axis=-1, keepdims=True) + RMS_EPS)
            ohat = o * r
            g = hgg_ref[:, hs]
            sg = _sigmoid(g)
            silu = g * sg
            dout = do_hg[:, hs]
            stage_hg[slot, :, hs] = (dout * ohat * gainv * (sg + silu * (1.0 - sg))).astype(BF16)
            dy = dout * silu
            part = part + jnp.sum(dy * ohat, axis=0, keepdims=True)
            dn = dy * gainv
            dopre_ref[:, hs] = r * (dn - ohat * jnp.mean(dn * ohat, axis=-1, keepdims=True))
        for cp in slot_copies(slot):
            cp.start()

        prod = do_da * oda_ref[...]
        for h in range(DA_WIDTH // HEAD_DIM):
            hs = slice(h * HEAD_DIM, (h + 1) * HEAD_DIM)
            dd_ref[:, hs] = jnp.broadcast_to(jnp.sum(prod[:, hs], axis=-1, keepdims=True), (tr, HEAD_DIM))

        @pl.when(i == 0)
        def _():
            dgain_ref[...] = part

        @pl.when(i > 0)
        def _():
            dgain_ref[...] += part

        @pl.when(i == steps - 1)
        def _():
            for cp in slot_copies(1 - slot) + slot_copies(slot):
                cp.wait()

    assert steps >= 2
    whole = lambda a: pl.BlockSpec(a.shape, lambda i: (0, 0))
    widths = [wt.shape[0] for wt in weights]
    return pl.pallas_call(
        body, name="gate_merge_bwd", grid=(steps,),
        in_specs=[_row_spec(tr, w, 10), _row_spec(tr, w, 11), _row_spec(tr, w, 12), _row_spec(tr, w, hg_gate_col0 // w)]
        + [_row_spec(tr, w)] * 3 + [whole(wt) for wt in weights]
        + [_row_spec(tr, w), whole(wout), _row_spec(tr, w), _bcast_spec(HEAD_DIM), _row_spec(tr, DA_WIDTH), _any_spec()],
        out_specs=[_row_spec(tr, w)] * 4 + [_row_spec(tr, widths[1]), _row_spec(tr, widths[2]),
                                            _row_spec(tr, DA_WIDTH), _bcast_spec(HEAD_DIM), _any_spec()],
        out_shape=[jax.ShapeDtypeStruct((SEQ, w), BF16)] * 3 + [jax.ShapeDtypeStruct((SEQ, w), F32)]
        + [jax.ShapeDtypeStruct((SEQ, widths[1]), F32), jax.ShapeDtypeStruct((SEQ, widths[2]), F32),
           jax.ShapeDtypeStruct((SEQ, DA_WIDTH), F32), jax.ShapeDtypeStruct((1, HEAD_DIM), F32),
           jax.ShapeDtypeStruct((SEQ, IN_COLS), BF16)],
        scratch_shapes=[pltpu.VMEM((2, tr, 3 * w), BF16), pltpu.VMEM((2, tr, w), BF16), pltpu.SemaphoreType.DMA((4,))],
        input_output_aliases={15: 8},
        compiler_params=_params(("arbitrary",), 44 * tr * w * 4),
    )(proj, proj, proj, proj, *ts, *weights, dx1b, wout, o_pre, hg_gain, o_da32, dproj)


def _ffn_in_swiglu(h2, wfin, tm, tn):
    gm, gn = SEQ // tm, D_FF // tn

    def body(h_ref, wa_ref, wb_ref, a_ref, b_ref, act_ref, actt_ref):
        h = h_ref[...]
        a = _dot(h, wa_ref[...], _NN)
        b = _dot(h, wb_ref[...], _NN)
        act = a * _sigmoid(a) * b
        a_ref[...] = a.astype(BF16)
        b_ref[...] = b.astype(BF16)
        act_ref[...] = act.astype(BF16)
        actt_ref[...] = act.T.astype(BF16)

    tile = pl.BlockSpec((tm, tn), lambda j, i: (i, j))
    shape = jax.ShapeDtypeStruct((SEQ, D_FF), BF16)
    return pl.pallas_call(
        body, name="ffn_in_swiglu_fwd", grid=(gn, gm),
        in_specs=[pl.BlockSpec((tm, D_MODEL), lambda j, i: (i, 0)),
                  pl.BlockSpec((D_MODEL, tn), lambda j, i: (0, j)),
                  pl.BlockSpec((D_MODEL, tn), lambda j, i: (0, gn + j))],
        out_specs=[tile, tile, tile, pl.BlockSpec((tn, tm), lambda j, i: (j, i))],
        out_shape=[shape, shape, shape, jax.ShapeDtypeStruct((D_FF, SEQ), BF16)],
        compiler_params=_params(("parallel", "parallel"), 4 * tm * D_MODEL + 8 * D_MODEL * tn + 16 * tm * tn
                                + 6 * tm * tn * 4),
    )(h2, wfin, wfin)


def _ffn_out_bwd_swiglu(dyb, wfout, a, b, tm, tn):
    gm, gn = SEQ // tm, D_FF // tn

    def body(dy_ref, w_ref, a_ref, b_ref, o_ref):
        d = _dot(dy_ref[...], w_ref[...], _NT)
        av = a_ref[...].astype(F32)
        bv = b_ref[...].astype(F32)
        s = _sigmoid(av)
        silu = av * s
        o_ref[0] = (d * bv * (s + silu * (1.0 - s))).astype(BF16)
        o_ref[1] = (d * silu).astype(BF16)

    tile = pl.BlockSpec((tm, tn), lambda i, j: (i, j))
    return pl.pallas_call(
        body, name="ffn_out_bwd_swiglu", grid=(gm, gn),
        in_specs=[pl.BlockSpec((tm, D_MODEL), lambda i, j: (i, 0)), pl.BlockSpec((tn, D_MODEL), lambda i, j: (j, 0)),
                  tile, tile],
        out_specs=pl.BlockSpec((2, tm, tn), lambda i, j: (0, i, j)),
        out_shape=jax.ShapeDtypeStruct((2, SEQ, D_FF), BF16),
        compiler_params=_params(("parallel", "parallel"), 4 * tm * D_MODEL + 4 * tn * D_MODEL + 16 * tm * tn
                                + 6 * tm * tn * 4),
    )(dyb, wfout, a, b)


def _ffn_out_loss_head(x1, act, wfout, target, tr):
    w = D_MODEL

    def body(x_ref, a_ref, w_ref, t_ref, dy_ref, dyb_ref, loss_ref, acc_ref):
        err = x_ref[...] + _dot(a_ref[...], w_ref[...], _NN) - t_ref[...]
        dy = err * (1.0 / w)
        dy_ref[...] = dy
        dyb_ref[...] = dy.astype(BF16)
        part = jnp.sum(err * err, axis=0, keepdims=True)

        @pl.when(pl.program_id(0) == 0)
        def _():
            acc_ref[...] = part

        @pl.when(pl.program_id(0) > 0)
        def _():
            acc_ref[...] += part

        @pl.when(pl.program_id(0) == SEQ // tr - 1)
        def _():
            total = jnp.sum(acc_ref[...], axis=1, keepdims=True) * (0.5 / w)
            loss_ref[...] = jnp.broadcast_to(total, (1, LANE))

    return pl.pallas_call(
        body, name="ffn_out_loss_head", grid=(SEQ // tr,),
        in_specs=[_row_spec(tr, w), _row_spec(tr, D_FF), pl.BlockSpec((D_FF, w), lambda i: (0, 0)), _row_spec(tr, w)],
        out_specs=[_row_spec(tr, w), _row_spec(tr, w), _bcast_spec(LANE)],
        out_shape=[jax.ShapeDtypeStruct((SEQ, w), F32), jax.ShapeDtypeStruct((SEQ, w), BF16),
                   jax.ShapeDtypeStruct((1, LANE), F32)],
        scratch_shapes=[pltpu.VMEM((1, w), F32)],
        compiler_params=_params(("arbitrary",), 12 * tr * w * 4 + 4 * D_FF * w),
    )(x1, act, wfout, target)


GLA_ROWS = 256
GLA_CPB = GLA_ROWS // HG_CHUNK
GLA_NBLK = SEQ // GLA_ROWS
GLA_WAYS = 8
GLA_TRIPS = GLA_NBLK // GLA_WAYS
GLA_GRAD_WAYS = 8
GLA_GRAD_TRIPS = GLA_NBLK // GLA_GRAD_WAYS
GLA_NCK = SEQ // HG_CHUNK
GLA_INTER_WAYS = 32


def _dot_split(m, xv, dims, terms=3):
    dot = lambda t: lax.dot_general(m, t, dims, preferred_element_type=F32)
    hi = xv.astype(BF16)
    r1 = xv - hi.astype(F32)
    mid = r1.astype(BF16)
    if terms == 2:
        return dot(mid) + dot(hi)
    lo = (r1 - mid.astype(F32)).astype(BF16)
    return (dot(lo) + dot(mid)) + dot(hi)


def _gla_block(qc, fc, lbv, masks, b=None):
    mask, maskb, direction = masks
    sq = _sigmoid(qc)
    q = qc * sq * HG_SCALE
    sf = _sigmoid(fc)
    forget = lbv + (1.0 - lbv) * sf
    k = 1.0 - forget
    logf = jnp.log(forget)
    if b is None:
        b = _dot_split(maskb, logf, _NN)
    ends = []
    for j in range(GLA_CPB):
        lo, hi = j * HG_CHUNK, (j + 1) * HG_CHUNK
        end = jnp.where(direction == 0, b[hi - 1:hi, :], b[lo:lo + 1, :])
        ends.append(jnp.broadcast_to(end, (HG_CHUNK, HEAD_DIM)))
    bt = jnp.concatenate(ends, axis=0)
    eb = jnp.exp(b)
    qt = q * eb
    kt = k * jnp.exp(-b)
    kh = k * jnp.exp(bt - b)
    a = jnp.where(mask, _dot(qt, kt, _NT), 0.0)
    return dict(sq=sq, sf=sf, forget=forget, k=k, b=b, bt=bt, eb=eb, qt=qt, kt=kt, kh=kh, a=a)


def _gla_masks(direction):
    row = lax.broadcasted_iota(jnp.int32, (GLA_ROWS, GLA_ROWS), 0)
    col = lax.broadcasted_iota(jnp.int32, (GLA_ROWS, GLA_ROWS), 1)
    same = (row // HG_CHUNK) == (col // HG_CHUNK)
    mask = jnp.logical_and(same, jnp.where(direction == 0, row - col, col - row) >= 0)
    return mask, jnp.where(mask, 1.0, 0.0).astype(BF16), direction


def _gla_chunk_rows(j):
    return slice(j * HG_CHUNK, (j + 1) * HG_CHUNK)


def _gla_block_rows(b):
    return pl.ds(pl.multiple_of(b * GLA_ROWS, GLA_ROWS), GLA_ROWS)


def _head_spec(col_block0):
    return pl.BlockSpec((SEQ, HEAD_DIM), lambda h, d: (0, col_block0 + h))


def _gla_decay(proj, lb):
    def body(f_ref, lb_ref, b_ref):
        maskb = _gla_masks(pl.program_id(0))[1]
        lbv = lb_ref[...]

        def step(s, carry):
            rows = [_gla_block_rows(s + w * trips) for w in range(ways)]
            loaded = [f_ref[r, :] for r in rows]
            outs = [_dot_split(maskb, jnp.log(lbv + (1.0 - lbv) * _sigmoid(fc)), _NN) for fc in loaded]
            for r, o in zip(rows, outs):
                b_ref[r, :] = o
            return carry

        lax.fori_loop(0, trips, step, 0)

    ways, trips = GLA_WAYS, GLA_TRIPS
    heads = 2
    width = heads * HEAD_DIM
    return pl.pallas_call(
        body, name="gla_decay", grid=(2, HG_HEADS // heads),
        in_specs=[pl.BlockSpec((SEQ, width), lambda d, h: (0, CB_F // heads + (HG_HEADS // heads) * d + h)),
                  pl.BlockSpec((None, None, 1, width), lambda d, h: (d, h, 0, 0))],
        out_specs=pl.BlockSpec((None, SEQ, width), lambda d, h: (d, 0, h)),
        out_shape=jax.ShapeDtypeStruct((2, SEQ, D_MODEL), F32),
        compiler_params=_params(("parallel", "parallel"), 6 * SEQ * width * 4),
    )(proj, lb.reshape(2, HG_HEADS // heads, 1, width))


def _gla_fwd(proj, lb, gain, b_all):
    nck = GLA_NCK

    def body(q_ref, f_ref, v_ref, g_ref, lb_ref, gain_ref, b_ref, ohg_ref, opre_ref, st_ref, qt_scr, dec_scr, cs_scr):
        d = pl.program_id(1)
        masks = _gla_masks(d)
        lbv = lb_ref[...]

        @pl.when(d == 0)
        def _():
            opre_ref[...] = jnp.zeros_like(opre_ref)

        def intra(s, carry):
            blocks = [s + w * GLA_TRIPS for w in range(GLA_WAYS)]
            rows = [_gla_block_rows(b) for b in blocks]
            loaded = [(q_ref[r, :], f_ref[r, :], v_ref[r, :], opre_ref[r, :], b_ref[r, :]) for r in rows]
            results = []
            for qc, fc, v, o_prev, decay_log in loaded:
                ck = _gla_block(qc, fc, lbv, masks, b=decay_log)
                o = o_prev + _dot(ck["a"], v, _NN)
                cs = [_dot(v[_gla_chunk_rows(j), :], ck["kh"][_gla_chunk_rows(j), :], _TN) for j in range(GLA_CPB)]
                dec = [jnp.exp(ck["bt"][j * HG_CHUNK:j * HG_CHUNK + 8, :]) for j in range(GLA_CPB)]
                results.append((o, ck["qt"].astype(BF16), cs, dec))
            for b, r, (o, qt, cs, dec) in zip(blocks, rows, results):
                opre_ref[r, :] = o
                qt_scr[r, :] = qt
                for j in range(GLA_CPB):
                    cs_scr[b * GLA_CPB + j] = cs[j]
                    dec_scr[b * GLA_CPB + j] = dec[j]
            return carry

        lax.fori_loop(0, GLA_TRIPS, intra, 0)

        def scan(i, st):
            c = jnp.where(d == 0, i, nck - 1 - i)
            st_ref[c] = st
            return st * dec_scr[c][0:1, :] + cs_scr[c]

        lax.fori_loop(0, nck, scan, jnp.zeros((HEAD_DIM, HEAD_DIM), F32), unroll=4)

        def inter(t, carry):
            chunks = [t + w * (nck // GLA_INTER_WAYS) for w in range(GLA_INTER_WAYS)]
            rows = [pl.ds(pl.multiple_of(c * HG_CHUNK, HG_CHUNK), HG_CHUNK) for c in chunks]
            loaded = [(opre_ref[r, :], qt_scr[r, :], st_ref[c]) for r, c in zip(rows, chunks)]
            for r, o in zip(rows, [o_prev + _dot(qt, st, _NT) for o_prev, qt, st in loaded]):
                opre_ref[r, :] = o
            return carry

        lax.fori_loop(0, nck // GLA_INTER_WAYS, inter, 0)

        @pl.when(d == 1)
        def _():
            o = opre_ref[...]
            r = lax.rsqrt(jnp.mean(o * o, axis=-1, keepdims=True) + RMS_EPS)
            g = g_ref[...]
            ohg_ref[...] = (o * r * gain_ref[...] * (g * _sigmoid(g))).astype(BF16)

    blk = SEQ * HEAD_DIM * 4
    return pl.pallas_call(
        body, name="gla_fwd", grid=(HG_HEADS, 2),
        in_specs=[_head_spec(CB_HG_Q),
                  pl.BlockSpec((SEQ, HEAD_DIM), lambda h, d: (0, CB_F + 8 * d + h)),
                  _head_spec(CB_HG_I), _head_spec(CB_HG_G),
                  pl.BlockSpec((None, None, 1, HEAD_DIM), lambda h, d: (d, h, 0, 0)),
                  pl.BlockSpec((1, HEAD_DIM), lambda h, d: (0, 0)),
                  pl.BlockSpec((None, SEQ, HEAD_DIM), lambda h, d: (d, 0, h))],
        out_specs=[_head_spec(0), _head_spec(0),
                   pl.BlockSpec((None, None, nck, HEAD_DIM, HEAD_DIM), lambda h, d: (h, d, 0, 0, 0)),
                   pl.BlockSpec((None, SEQ, HEAD_DIM), lambda h, d: (d, 0, h)),
                   pl.BlockSpec((None, None, nck, 8, HEAD_DIM), lambda h, d: (h, d, 0, 0, 0))],
        out_shape=[jax.ShapeDtypeStruct((SEQ, D_MODEL), BF16), jax.ShapeDtypeStruct((SEQ, D_MODEL), F32),
                   jax.ShapeDtypeStruct((HG_HEADS, 2, nck, HEAD_DIM, HEAD_DIM), F32),
                   jax.ShapeDtypeStruct((2, SEQ, D_MODEL), BF16),
                   jax.ShapeDtypeStruct((HG_HEADS, 2, nck, 8, HEAD_DIM), F32)],
        scratch_shapes=[pltpu.VMEM((nck, HEAD_DIM, HEAD_DIM), F32)],
        compiler_params=_params(("parallel", "arbitrary"), 8 * blk + 3 * blk + 2 * blk + 2 * blk + 3 * blk),
    )(proj, proj, proj, proj, lb, gain, b_all)


def _gla_bwd(proj, lb, states, qt_all, dec_all, b_all, do, dproj):
    nck = GLA_NCK

    def body(q_ref, f_ref, v_ref, lb_ref, do_ref, st_ref, qt_ref, dec_scr, b_ref, dproj_in, dproj_ref, dlb_ref,
             dq_acc, dv_acc, dst_scr, cs_scr, df_out, dq_out, dv_out, sems):
        del dproj_in
        h, d = pl.program_id(0), pl.program_id(1)
        masks = _gla_masks(d)
        mask, maskb, _ = masks
        lbv = lb_ref[...]

        def column_copy(k, staging, col_block):
            cols = pl.ds(pl.multiple_of(col_block * LANE, LANE), LANE)
            return pltpu.make_async_copy(staging, dproj_ref.at[:, cols], sems.at[k])

        df_copy = column_copy(0, df_out, CB_F + 8 * d + h)
        dq_copy = column_copy(1, dq_out, CB_HG_Q + h)
        dv_copy = column_copy(2, dv_out, CB_HG_I + h)
        last = jnp.logical_and(h == HG_HEADS - 1, d == 1)

        def intra(s, carry):
            blocks = [s + w * GLA_TRIPS for w in range(GLA_WAYS)]
            loaded = [(qt_ref[r, :], do_ref[r, :]) for r in map(_gla_block_rows, blocks)]
            results = [[_dot(doc[_gla_chunk_rows(j), :], qt[_gla_chunk_rows(j), :], _TN) for j in range(GLA_CPB)]
                       for qt, doc in loaded]
            for b, cs in zip(blocks, results):
                for j in range(GLA_CPB):
                    cs_scr[b * GLA_CPB + j] = cs[j]
            return carry

        lax.fori_loop(0, GLA_TRIPS, intra, 0)

        def scan(i, dst):
            c = jnp.where(d == 0, nck - 1 - i, i)
            dst_scr[c] = dst
            return dst * dec_scr[c][0:1, :] + cs_scr[c]

        lax.fori_loop(0, nck, scan, jnp.zeros((HEAD_DIM, HEAD_DIM), F32), unroll=4)

        @pl.when(d == 0)
        def _():
            dq_acc[...] = jnp.zeros_like(dq_acc)
            dv_acc[...] = jnp.zeros_like(dv_acc)

        def block_grads(qc, fc, v, doc, states_in, dstates, decays, decay_log):
            ck = _gla_block(qc, fc, lbv, masks, b=decay_log)
            qt, kt, kh, a = ck["qt"], ck["kt"], ck["kh"], ck["a"]
            da = jnp.where(mask, _dot(doc, v, _NT), 0.0)
            dqt_i = _dot(da, kt, _NN)
            dkt = _dot(da, qt, _TN)
            dv_i = _dot(a, doc, _TN)
            dqt_p, dv_p, dkh_p, dbt_p = [], [], [], []
            for j in range(GLA_CPB):
                cr = _gla_chunk_rows(j)
                st_in, dst = states_in[j], dstates[j]
                dqt_p.append(dqt_i[cr, :] + _dot(doc[cr, :], st_in, _NN))
                dv_p.append(dv_i[cr, :] + _dot(kh[cr, :], dst, _NT))
                dkh_j = _dot(v[cr, :], dst, _NN)
                dkh_p.append(dkh_j)
                dbt_j = (decays[j][0:1, :] * jnp.sum(dst * st_in, axis=0, keepdims=True)
                         + jnp.sum(dkh_j * kh[cr, :], axis=0, keepdims=True))
                dbt_p.append(jnp.broadcast_to(dbt_j, (HG_CHUNK, HEAD_DIM)))
            dqt = jnp.concatenate(dqt_p, axis=0)
            dv = jnp.concatenate(dv_p, axis=0)
            dkh = jnp.concatenate(dkh_p, axis=0)
            dbt = jnp.concatenate(dbt_p, axis=0)
            db = dqt * qt - dkt * kt - dkh * kh
            dq = dqt * ck["eb"]
            dk = dkt * jnp.exp(-ck["b"]) + dkh * jnp.exp(ck["bt"] - ck["b"])
            dlogf = _dot_split(maskb, db, _TN, terms=2) + dbt
            dforget = dlogf / ck["forget"] - dk
            sf, sq = ck["sf"], ck["sq"]
            df = (dforget * (1.0 - lbv) * sf * (1.0 - sf)).astype(BF16)
            dqc = dq * HG_SCALE * (sq + qc * sq * (1.0 - sq))
            return df, dqc, dv, jnp.sum(dforget * (1.0 - sf), axis=0, keepdims=True)

        def grads(s, dlb):
            blocks = [s + w * GLA_GRAD_TRIPS for w in range(GLA_GRAD_WAYS)]
            rows = [_gla_block_rows(b) for b in blocks]
            loaded = []
            for b, r in zip(blocks, rows):
                chunks = [b * GLA_CPB + j for j in range(GLA_CPB)]
                loaded.append((q_ref[r, :], f_ref[r, :], v_ref[r, :], do_ref[r, :], [st_ref[c] for c in chunks],
                               [dst_scr[c] for c in chunks], [dec_scr[c] for c in chunks], b_ref[r, :],
                               dq_acc[r, :], dv_acc[r, :]))
            results = [block_grads(*t[:8]) + (t[8], t[9]) for t in loaded]
            for r, (df, dqc, dv, dlb_part, dq_prev, dv_prev) in zip(rows, results):
                df_out[r, :] = df
                dq_acc[r, :] = dq_prev + dqc
                dv_acc[r, :] = dv_prev + dv
                dlb = dlb + dlb_part
            return dlb

        @pl.when(jnp.logical_or(h > 0, d > 0))
        def _():
            df_copy.wait()

        dlb_ref[...] = lax.fori_loop(0, GLA_GRAD_TRIPS, grads, jnp.zeros((1, HEAD_DIM), F32))
        df_copy.start()

        @pl.when(d == 1)
        def _():
            @pl.when(h > 0)
            def _():
                dq_copy.wait()
                dv_copy.wait()

            dq_out[...] = dq_acc[...].astype(BF16)
            dv_out[...] = dv_acc[...].astype(BF16)
            dq_copy.start()
            dv_copy.start()

        @pl.when(last)
        def _():
            df_copy.wait()
            dq_copy.wait()
            dv_copy.wait()

    blk = SEQ * HEAD_DIM * 4
    state_bytes = nck * HEAD_DIM * HEAD_DIM * 4
    dproj, dlb = pl.pallas_call(
        body, name="gla_bwd", grid=(HG_HEADS, 2),
        in_specs=[_head_spec(CB_HG_Q),
                  pl.BlockSpec((SEQ, HEAD_DIM), lambda h, d: (0, CB_F + 8 * d + h)),
                  _head_spec(CB_HG_I),
                  pl.BlockSpec((None, None, 1, HEAD_DIM), lambda h, d: (d, h, 0, 0)),
                  _head_spec(0),
                  pl.BlockSpec((None, None, nck, HEAD_DIM, HEAD_DIM), lambda h, d: (h, d, 0, 0, 0)),
                  pl.BlockSpec((None, SEQ, HEAD_DIM), lambda h, d: (d, 0, h)),
                  pl.BlockSpec((None, None, nck, 8, HEAD_DIM), lambda h, d: (h, d, 0, 0, 0)),
                  pl.BlockSpec((None, SEQ, HEAD_DIM), lambda h, d: (d, 0, h)),
                  _any_spec()],
        out_specs=[_any_spec(), pl.BlockSpec((None, None, 1, HEAD_DIM), lambda h, d: (d, h, 0, 0))],
        out_shape=[jax.ShapeDtypeStruct((SEQ, IN_COLS), BF16), jax.ShapeDtypeStruct((2, HG_HEADS, 1, HEAD_DIM), F32)],
        scratch_shapes=[pltpu.VMEM((SEQ, HEAD_DIM), F32)] * 2
        + [pltpu.VMEM((nck, HEAD_DIM, HEAD_DIM), F32)] * 2
        + [pltpu.VMEM((SEQ, HEAD_DIM), BF16)] * 3 + [pltpu.SemaphoreType.DMA((3,))],
        input_output_aliases={9: 0},
        compiler_params=_params(("arbitrary", "arbitrary"), 10 * blk + 4 * state_bytes + 3 * blk + 3 * blk),
    )(proj, proj, proj, lb, do, states, qt_all, dec_all, b_all, dproj)
    return dproj, dlb


def _da_residue_rows(r, n0, size, d):
    if d == 1:
        return pl.ds(pl.multiple_of(n0, 8), size)
    return pl.ds(r + n0 * d, size, stride=d)


def _da_residue_ways(d, nqb):
    return min(d, 4) if nqb <= 2 else 1


def _da_rmsnorm(x, gain):
    r = lax.rsqrt(jnp.mean(x * x, axis=-1, keepdims=True) + RMS_EPS)
    return x * r, r, x * r * gain


def _da_scores(qn_scr, kn_scr, slope, i, ld):
    w0 = jnp.clip(i * DA_QB - DA_RADIUS, 0, ld - DA_WIN)
    w0 = pl.multiple_of(w0, DA_RADIUS)
    qrows = pl.ds(pl.multiple_of(i * DA_QB, DA_QB), DA_QB)
    win = pl.ds(w0, DA_WIN)
    qb = qn_scr[qrows, :]
    kw = kn_scr[win, :]
    s = _dot(qb, kw, _NT) * DA_SCALE
    qpos = i * DA_QB + lax.broadcasted_iota(jnp.int32, (DA_QB, DA_WIN), 0)
    kpos = w0 + lax.broadcasted_iota(jnp.int32, (DA_QB, DA_WIN), 1)
    arel = jnp.abs(kpos - qpos)
    s = s - slope * arel.astype(F32)
    s = jnp.where(arel <= DA_RADIUS, s, NEG_INF)
    return s, qb, kw, qrows, win


def _da_slopes(group):
    d = DA_DILATIONS[group]
    sl = _alibi_slopes(DA_HEADS)[4 * group:4 * group + 4] * d
    return jnp.asarray(np.broadcast_to(sl[:, None, None], (4, 1, LANE)).copy())


def _da_fwd(proj, gq, gk, group, others=()):
    d = DA_DILATIONS[group]
    ld = SEQ // d
    nqb = ld // DA_QB
    n_other = 2 * len(others)

    ways = min(DA_FWD_WAYS, nqb)
    rways = _da_residue_ways(d, nqb)

    def body(q_ref, k_ref, v_ref, gq_ref, gk_ref, sl_ref, *refs):
        other_refs, refs = refs[:n_other], refs[n_other:]
        if others:
            ob_ref, of_ref, lsej_ref, qn_scr, kn_scr, v_scr, o_ref, lse_ref = refs
        else:
            o_ref, lse_ref, qn_scr, kn_scr, v_scr = refs
        slope = sl_ref[:, 0:1]

        def residues(t, carry):
            rs = [t * rways + u for u in range(rways)]
            for u, r in enumerate(rs):
                sel = _da_residue_rows(r, 0, ld, d)
                qn_scr[u] = _da_rmsnorm(q_ref[sel, :], gq_ref[...])[2].astype(BF16)
                kn_scr[u] = _da_rmsnorm(k_ref[sel, :], gk_ref[...])[2].astype(BF16)
                v_scr[u] = v_ref[sel, :].astype(BF16)

            def block(u, i):
                s, _, _, _, win = _da_scores(qn_scr.at[u], kn_scr.at[u], slope, i, ld)
                m = jnp.max(s, axis=-1, keepdims=True)
                p = jnp.exp(s - m)
                l = jnp.sum(p, axis=-1, keepdims=True)
                return _dot(p, v_scr[u, win, :], _NN) / l, jnp.broadcast_to(m + jnp.log(l), (DA_QB, HEAD_DIM))

            def step(i, c2):
                todo = [(u, r, i + w * (nqb // ways)) for u, r in enumerate(rs) for w in range(ways)]
                for (u, r, b), (o, lse) in zip(todo, [block(u, b) for u, _, b in todo]):
                    out = _da_residue_rows(r, b * DA_QB, DA_QB, d)
                    o_ref[out, :] = o
                    lse_ref[out, :] = lse
                return c2

            return lax.fori_loop(0, nqb // ways, step, carry)

        lax.fori_loop(0, d // rways, residues, 0)

        if others:
            pieces = 8
            rows = SEQ // pieces

            def merge(t, carry):
                r = pl.ds(pl.multiple_of(t * rows, rows), rows)
                outs = [ref[r, :] for ref in other_refs[0::2]] + [o_ref[r, :]]
                lses = [ref[r, :] for ref in other_refs[1::2]] + [lse_ref[r, :]]
                m = lses[0]
                for l in lses[1:]:
                    m = jnp.maximum(m, l)
                es = [jnp.exp(l - m) for l in lses]
                tot = sum(es[1:], es[0])
                o = sum((e * v for e, v in zip(es[1:], outs[1:])), es[0] * outs[0]) / tot
                of_ref[r, :] = o
                ob_ref[r, :] = o.astype(BF16)
                lsej_ref[r, :] = m + jnp.log(tot)
                return carry

            lax.fori_loop(0, pieces, merge, 0)

    seq_spec = lambda base: pl.BlockSpec((SEQ, HEAD_DIM), lambda h: (0, base + 4 * group + h))
    out_spec = pl.BlockSpec((SEQ, HEAD_DIM), lambda h: (0, h))
    gain_spec = pl.BlockSpec((1, HEAD_DIM), lambda h: (0, 0))
    blk = SEQ * HEAD_DIM * 4
    fshape = jax.ShapeDtypeStruct((SEQ, DA_WIDTH), F32)
    scratch = [pltpu.VMEM((rways, ld, HEAD_DIM), BF16)] * 3
    if others:
        out_shape = [jax.ShapeDtypeStruct((SEQ, DA_WIDTH), BF16), fshape, fshape]
        scratch += [pltpu.VMEM((SEQ, HEAD_DIM), F32)] * 2
    else:
        out_shape = [fshape, fshape]
    return pl.pallas_call(
        body, name=f"da_fwd_g{group}", grid=(DA_HEADS_PER_GROUP,),
        in_specs=[seq_spec(CB_DA_Q), seq_spec(CB_DA_K), seq_spec(CB_DA_V), gain_spec, gain_spec,
                  pl.BlockSpec((None, 1, LANE), lambda h: (h, 0, 0))] + [out_spec] * n_other,
        out_specs=[out_spec] * len(out_shape), out_shape=out_shape, scratch_shapes=scratch,
        compiler_params=_params(("parallel",), 10 * blk + 2 * blk + (n_other + 3) * blk),
    )(proj, proj, proj, gq, gk, _da_slopes(group), *[a for pair in others for a in pair])


def _da_bwd(proj, gq, gk, do, lse, dd, dproj, group):
    d = DA_DILATIONS[group]
    ld = SEQ // d
    nqb = ld // DA_QB
    ways = min(DA_WAYS, nqb)
    rways = _da_residue_ways(d, nqb)

    def body(q_ref, k_ref, v_ref, gq_ref, gk_ref, sl_ref, do_ref, lse_ref, dd_ref, dproj_in,
             dproj_ref, dgq_ref, dgk_ref,
             qn_scr, kn_scr, v_scr, dqn_scr, dkn_scr, dvr_scr, rq_scr, rk_scr, dq_scr, dk_scr, dv_scr,
             dq_out, dk_out, dv_out, sems):
        del dproj_in
        head = pl.program_id(0)
        gqv, gkv = gq_ref[...], gk_ref[...]
        slope = sl_ref[:, 0:1]

        copies = []
        for k, (staging, base) in enumerate(((dq_out, CB_DA_Q), (dk_out, CB_DA_K), (dv_out, CB_DA_V))):
            cols = pl.ds(pl.multiple_of((base + 4 * group + head) * LANE, LANE), LANE)
            copies.append(pltpu.make_async_copy(staging, dproj_ref.at[:, cols], sems.at[k]))

        def residues(t, carry):
            rs = [t * rways + u for u in range(rways)]
            sels = [_da_residue_rows(r, 0, ld, d) for r in rs]
            for u, sel in enumerate(sels):
                for x_ref, gv, n_scr, r_scr in ((q_ref, gqv, qn_scr, rq_scr), (k_ref, gkv, kn_scr, rk_scr)):
                    _, rstd, normed = _da_rmsnorm(x_ref[sel, :], gv)
                    n_scr[u] = normed.astype(BF16)
                    r_scr[u] = jnp.broadcast_to(rstd, (ld, HEAD_DIM))
                v_scr[u] = v_ref[sel, :].astype(BF16)
            dkn_scr[...] = jnp.zeros_like(dkn_scr)
            dvr_scr[...] = jnp.zeros_like(dvr_scr)

            def block(u, r, i):
                s, qb, kw, qrows, win = _da_scores(qn_scr.at[u], kn_scr.at[u], slope, i, ld)
                src = _da_residue_rows(r, i * DA_QB, DA_QB, d)
                p = jnp.exp(s - lse_ref[src, :][:, 0:1])
                dob = do_ref[src, :]
                dp = _dot(dob, v_scr[u, win, :], _NT)
                ds = p * (dp - dd_ref[src, :][:, 0:1]) * DA_SCALE
                return u, qrows, win, _dot(p, dob, _TN), _dot(ds, kw, _NN), _dot(ds, qb, _TN)

            def step(i, c2):
                todo = [(u, r, i + w * (nqb // ways)) for u, r in enumerate(rs) for w in range(ways)]
                for u, qrows, win, dv, dqn, dkn in [block(*t) for t in todo]:
                    dvr_scr[u, win, :] += dv
                    dqn_scr[u, qrows, :] = dqn
                    dkn_scr[u, win, :] += dkn
                return c2

            lax.fori_loop(0, nqb // ways, step, 0)

            pq, pk = carry
            for u, sel in enumerate(sels):
                parts = []
                for x_ref, gv, dn_scr, dx_scr, r_scr in ((q_ref, gqv, dqn_scr, dq_scr, rq_scr),
                                                         (k_ref, gkv, dkn_scr, dk_scr, rk_scr)):
                    rstd = r_scr[u]
                    hat = x_ref[sel, :] * rstd
                    dn = dn_scr[u]
                    dyg = dn * gv
                    dx_scr[sel, :] = rstd * (dyg - hat * jnp.mean(dyg * hat, axis=-1, keepdims=True))
                    parts.append(jnp.sum(dn * hat, axis=0, keepdims=True))
                dv_scr[sel, :] = dvr_scr[u]
                pq, pk = pq + parts[0], pk + parts[1]
            return pq, pk

        zero = jnp.zeros((1, HEAD_DIM), F32)
        pq, pk = lax.fori_loop(0, d // rways, residues, (zero, zero))

        @pl.when(pl.program_id(0) == 0)
        def _():
            dgq_ref[...] = pq
            dgk_ref[...] = pk

        @pl.when(pl.program_id(0) > 0)
        def _():
            dgq_ref[...] += pq
            dgk_ref[...] += pk

        @pl.when(head > 0)
        def _():
            for cp in copies:
                cp.wait()

        dq_out[...] = dq_scr[...].astype(BF16)
        dk_out[...] = dk_scr[...].astype(BF16)
        dv_out[...] = dv_scr[...].astype(BF16)
        for cp in copies:
            cp.start()

        @pl.when(head == DA_HEADS_PER_GROUP - 1)
        def _():
            for cp in copies:
                cp.wait()

    seq_spec = lambda base: pl.BlockSpec((SEQ, HEAD_DIM), lambda h: (0, base + 4 * group + h))
    out_spec = pl.BlockSpec((SEQ, HEAD_DIM), lambda h: (0, h))
    gain_spec = pl.BlockSpec((1, HEAD_DIM), lambda h: (0, 0))
    gshape = jax.ShapeDtypeStruct((1, HEAD_DIM), F32)
    blk = SEQ * HEAD_DIM * 4
    return pl.pallas_call(
        body, name=f"da_bwd_g{group}", grid=(DA_HEADS_PER_GROUP,),
        in_specs=[seq_spec(CB_DA_Q), seq_spec(CB_DA_K), seq_spec(CB_DA_V), gain_spec, gain_spec,
                  pl.BlockSpec((None, 1, LANE), lambda h: (h, 0, 0)), out_spec, out_spec, out_spec, _any_spec()],
        out_specs=[_any_spec(), gain_spec, gain_spec],
        out_shape=[jax.ShapeDtypeStruct((SEQ, IN_COLS), BF16), gshape, gshape],
        scratch_shapes=[pltpu.VMEM((rways, ld, HEAD_DIM), BF16)] * 3 + [pltpu.VMEM((rways, ld, HEAD_DIM), F32)] * 5
        + [pltpu.VMEM((SEQ, HEAD_DIM), F32)] * 3 + [pltpu.VMEM((SEQ, HEAD_DIM), BF16)] * 3
        + [pltpu.SemaphoreType.DMA((3,))],
        input_output_aliases={9: 0},
        compiler_params=_params(("arbitrary",), 12 * blk + 3 * blk + 3 * blk + 3 * blk + 2 * blk),
    )(proj, proj, proj, gq, gk, _da_slopes(group), do, lse, dd, dproj)


MEM_WAYS = 4


def _mem_probs(q, kn, gq):
    qhat, rq, qn = _da_rmsnorm(q, gq)
    s = _dot(qn, kn, _NT) * MEM_SCALE
    m = jnp.max(s, axis=-1, keepdims=True)
    e = jnp.exp(s - m)
    p = e / jnp.sum(e, axis=-1, keepdims=True)
    return p, qhat, rq, qn


def _mem_pieces(tq):
    rows = tq // MEM_WAYS
    return [slice(w * rows, (w + 1) * rows) for w in range(MEM_WAYS)]


def _mem_fwd(proj, kv, gq, gk, tq):
    def body(q_ref, k_ref, v_ref, gq_ref, gk_ref, o_ref):
        kn = _da_rmsnorm(k_ref[...], gk_ref[...])[2]
        v, gqv = v_ref[...], gq_ref[...]
        pieces = _mem_pieces(tq)
        outs = [_dot(_mem_probs(q_ref[rs, :], kn, gqv)[0], v, _NN) for rs in pieces]
        for rs, o in zip(pieces, outs):
            o_ref[rs, :] = o.astype(BF16)

    gain_spec = pl.BlockSpec((1, HEAD_DIM), lambda h, i: (0, 0))
    return pl.pallas_call(
        body, name="mem_fwd", grid=(MEM_HEADS, SEQ // tq),
        in_specs=[pl.BlockSpec((tq, HEAD_DIM), lambda h, i: (i, CB_MEM_Q + h)),
                  pl.BlockSpec((N_MEM, HEAD_DIM), lambda h, i: (0, h)),
                  pl.BlockSpec((N_MEM, HEAD_DIM), lambda h, i: (0, MEM_HEADS + h)), gain_spec, gain_spec],
        out_specs=pl.BlockSpec((tq, HEAD_DIM), lambda h, i: (i, h)),
        out_shape=jax.ShapeDtypeStruct((SEQ, MEM_WIDTH), BF16),
        compiler_params=_params(("parallel", "parallel"), 16 * tq * N_MEM * 4),
    )(proj, kv, kv, gq, gk)


def _mem_bwd(proj, kv, gq, gk, do, dproj, tq):
    nq = SEQ // tq

    def body(q_ref, k_ref, v_ref, gq_ref, gk_ref, do_ref, dproj_in, dq_ref, dk_ref, dv_ref, dgq_ref, dgk_ref, dkn_scr):
        del dproj_in
        h, i = pl.program_id(0), pl.program_id(1)
        gqv, gkv = gq_ref[...], gk_ref[...]
        khat, rk, kn = _da_rmsnorm(k_ref[...], gkv)
        v = v_ref[...]

        def piece(rs):
            p, qhat, rq, qn = _mem_probs(q_ref[rs, :], kn, gqv)
            dob = do_ref[rs, :]
            dp = _dot(dob, v, _NT)
            ds = p * (dp - jnp.sum(p * dp, axis=-1, keepdims=True)) * MEM_SCALE
            dqn = _dot(ds, kn, _NN)
            dyg = dqn * gqv
            dq = (rq * (dyg - qhat * jnp.mean(dyg * qhat, axis=-1, keepdims=True))).astype(BF16)
            return dq, _dot(p, dob, _TN), _dot(ds, qn, _TN), jnp.sum(dqn * qhat, axis=0, keepdims=True)

        pieces = _mem_pieces(tq)
        results = [piece(rs) for rs in pieces]
        for rs, res in zip(pieces, results):
            dq_ref[rs, :] = res[0]
        dvp = sum((res[1] for res in results[1:]), results[0][1])
        dknp = sum((res[2] for res in results[1:]), results[0][2])
        dgq_part = sum((res[3] for res in results[1:]), results[0][3])
        first = jnp.logical_and(h == 0, i == 0)

        @pl.when(first)
        def _():
            dgq_ref[...] = dgq_part

        @pl.when(jnp.logical_not(first))
        def _():
            dgq_ref[...] += dgq_part

        @pl.when(i == 0)
        def _():
            dv_ref[...] = dvp
            dkn_scr[...] = dknp

        @pl.when(i > 0)
        def _():
            dv_ref[...] += dvp
            dkn_scr[...] += dknp

        @pl.when(i == nq - 1)
        def _():
            dkn = dkn_scr[...]
            dkg = dkn * gkv
            dk_ref[...] = rk * (dkg - khat * jnp.mean(dkg * khat, axis=-1, keepdims=True))
            dgk_part = jnp.sum(dkn * khat, axis=0, keepdims=True)

            @pl.when(h == 0)
            def _():
                dgk_ref[...] = dgk_part

            @pl.when(h > 0)
            def _():
                dgk_ref[...] += dgk_part

    gain_spec = pl.BlockSpec((1, HEAD_DIM), lambda h, i: (0, 0))
    kvout = pl.BlockSpec((N_MEM, HEAD_DIM), lambda h, i: (0, h))
    return pl.pallas_call(
        body, name="mem_bwd", grid=(MEM_HEADS, nq),
        in_specs=[pl.BlockSpec((tq, HEAD_DIM), lambda h, i: (i, CB_MEM_Q + h)),
                  pl.BlockSpec((N_MEM, HEAD_DIM), lambda h, i: (0, h)),
                  pl.BlockSpec((N_MEM, HEAD_DIM), lambda h, i: (0, MEM_HEADS + h)), gain_spec, gain_spec,
                  pl.BlockSpec((tq, HEAD_DIM), lambda h, i: (i, h)), _any_spec()],
        out_specs=[pl.BlockSpec((tq, HEAD_DIM), lambda h, i: (i, CB_MEM_Q + h)), kvout, kvout, gain_spec, gain_spec],
        out_shape=[jax.ShapeDtypeStruct((SEQ, IN_COLS), BF16), jax.ShapeDtypeStruct((N_MEM, MEM_WIDTH), F32),
                   jax.ShapeDtypeStruct((N_MEM, MEM_WIDTH), F32), jax.ShapeDtypeStruct((1, HEAD_DIM), F32),
                   jax.ShapeDtypeStruct((1, HEAD_DIM), F32)],
        scratch_shapes=[pltpu.VMEM((N_MEM, HEAD_DIM), F32)], input_output_aliases={6: 0},
        compiler_params=_params(("arbitrary", "arbitrary"), 24 * tq * N_MEM * 4),
    )(proj, kv, kv, gq, gk, do, dproj)


def _mesh_position():
    return lax.axis_index("x"), lax.axis_index("y"), lax.axis_index("c")


def _any_spec():
    return pl.BlockSpec(memory_space=pl.ANY)


def _all_gather(shards):
    n = len(shards)

    def body(*refs):
        ins, outs = refs[:n], refs[n:2 * n]
        send_sems, recv_sems, local_sems = refs[2 * n:]
        x, y, c = _mesh_position()
        me, sibling = (x, y, c), (x, y, 1 - c)
        first = (jnp.where(c == 0, 1 - x, x), jnp.where(c == 0, y, 1 - y), c)
        second = (jnp.where(c == 0, x, 1 - x), jnp.where(c == 0, 1 - y, y), c)
        diagonal = (1 - x, 1 - y, c)

        def copy(w, k, block, to, src=None):
            px, py, pc = block
            rows = outs[w].at[4 * px + 2 * py + pc]
            return pltpu.make_async_remote_copy(
                src_ref=rows if src is None else src, dst_ref=rows,
                send_sem=send_sems.at[7 * w + k], recv_sem=recv_sems.at[7 * w + k],
                device_id=to, device_id_type=MESH)

        started = []
        for w in range(n):
            mine = pltpu.make_async_copy(ins[w], outs[w].at[4 * x + 2 * y + c], local_sems.at[w])
            mine.start()
            started.append(mine)
        sends = []
        for w in range(n):
            own = [copy(w, 0, me, sibling, src=ins[w]), copy(w, 1, me, first, src=ins[w]),
                   copy(w, 2, me, second, src=ins[w])]
            for cp in own:
                cp.start()
            sends += own
        for w in range(n):
            copy(w, 1, first, me).wait_recv()
            follow = [copy(w, 3, first, second), copy(w, 4, first, sibling)]
            for cp in follow:
                cp.start()
            copy(w, 2, second, me).wait_recv()
            follow.append(copy(w, 5, second, sibling))
            follow[-1].start()
            sends += follow
        for w in range(n):
            copy(w, 3, diagonal, me).wait_recv()
            passed = copy(w, 6, diagonal, sibling)
            passed.start()
            sends.append(passed)
        for w in range(n):
            for k in (0, 4, 5, 6):
                copy(w, k, sibling, me).wait_recv()
        for cp in sends:
            cp.wait_send()
        for mine in started:
            mine.wait()

    return pl.pallas_call(
        body, name="weights_all_gather",
        in_specs=[_any_spec()] * n, out_specs=[_any_spec()] * n,
        out_shape=[jax.ShapeDtypeStruct((N_DEV,) + s.shape, s.dtype) for s in shards],
        scratch_shapes=[pltpu.SemaphoreType.DMA((7 * n,)), pltpu.SemaphoreType.DMA((7 * n,)),
                        pltpu.SemaphoreType.DMA((n,))],
    )(*shards)


def _chip_of(j, x, y):
    return (1 - x if j & 1 else x, 1 - y if j & 2 else y)


_HBM_SPEC = pl.BlockSpec(memory_space=pltpu.HBM)
_SEM_SPEC = pl.BlockSpec(memory_space=pltpu.SEMAPHORE)
_DATAFLOW_EFFECT = pltpu.SideEffectType.DATAFLOW_SIDE_EFFECTING
TOKEN_SHAPE = (8, D_MODEL)


def _copies_start(name, arrays, n_copies, plan):
    n = len(arrays)

    def body(*refs):
        send_sems, recv_sems, token = refs[n], refs[n + 1], refs[2 * n + 2]
        copies = plan(refs[:n])
        assert len(copies) == n_copies
        for k, (src, dst, dev) in enumerate(copies):
            pltpu.make_async_remote_copy(src_ref=src, dst_ref=dst, send_sem=send_sems.at[k], recv_sem=recv_sems.at[k],
                                         device_id=dev, device_id_type=MESH).start()
        token[...] = jnp.zeros_like(token)

    outs = pl.pallas_call(
        body, name=name,
        out_shape=(pltpu.SemaphoreType.DMA((n_copies,)), pltpu.SemaphoreType.DMA((n_copies,)),
                   *[pltpu.HBM(a.shape, a.dtype) for a in arrays], jax.ShapeDtypeStruct(TOKEN_SHAPE, F32)),
        in_specs=[_HBM_SPEC] * n,
        out_specs=(_SEM_SPEC, _SEM_SPEC, *[_HBM_SPEC] * n, pl.BlockSpec(memory_space=pltpu.VMEM)),
        input_output_aliases={i: i + 2 for i in range(n)},
        compiler_params=pltpu.CompilerParams(has_side_effects=_DATAFLOW_EFFECT),
    )(*[pltpu.with_memory_space_constraint(a, pltpu.HBM) for a in arrays])
    return outs[0], outs[1], list(outs[2:2 + n]), outs[2 + n]


def _copies_wait(name, send_sems, recv_sems, arrays, n_copies, plan, after):
    n = len(arrays)
    after = list(after) if isinstance(after, (list, tuple)) else [after]

    def body(*refs):
        send_ref, recv_ref = refs[n], refs[n + 1]
        copies = plan(refs[:n])
        assert len(copies) == n_copies
        for k, (src, dst, dev) in enumerate(copies):
            cp = pltpu.make_async_remote_copy(src_ref=src, dst_ref=dst, send_sem=send_ref.at[k], recv_sem=recv_ref.at[k],
                                              device_id=dev, device_id_type=MESH)
            cp.wait_send()
            cp.wait_recv()

    outs = pl.pallas_call(
        body, name=name, out_shape=tuple(pltpu.HBM(a.shape, a.dtype) for a in arrays),
        in_specs=[_HBM_SPEC] * n + [_SEM_SPEC, _SEM_SPEC] + [pl.BlockSpec(memory_space=pl.ANY)] * len(after),
        out_specs=tuple([_HBM_SPEC] * n), input_output_aliases={i: i for i in range(n)},
        compiler_params=pltpu.CompilerParams(has_side_effects=_DATAFLOW_EFFECT),
    )(*arrays, send_sems, recv_sems, *after)
    return list(outs)


def _after(small, token):
    return small if token is None else small + token[0:1, :small.shape[-1]]


def _gather_plan_out(n):
    def plan(refs):
        x, y, c = _mesh_position()
        me = 4 * x + 2 * y + c
        copies = []
        for w in range(n):
            land = refs[n + w].at[me]
            copies.append((refs[w], land, (x, y, 1 - c)))
            for j in range(1, 4):
                copies.append((refs[w], land, (*_chip_of(j, x, y), c)))
        return copies
    return plan


def _gather_plan_pass(n):
    def plan(refs):
        x, y, c = _mesh_position()
        copies = []
        for w in range(n):
            for j in range(1, 4):
                px, py = _chip_of(j, x, y)
                rows = refs[w].at[4 * px + 2 * py + c]
                copies.append((rows, rows, (x, y, 1 - c)))
        return copies
    return plan


def _reduce_plan_sibling(n):
    def plan(refs):
        x, y, c = _mesh_position()
        copies = []
        for w in range(n):
            for j in range(4):
                px, py = _chip_of(j, x, y)
                copies.append((refs[w].at[4 * px + 2 * py + (1 - c)], refs[n + w].at[j], (x, y, 1 - c)))
        return copies
    return plan


def _reduce_plan_chips(n):
    def plan(refs):
        x, y, c = _mesh_position()
        copies = []
        for w in range(n):
            for j in range(1, 4):
                copies.append((refs[w].at[j - 1], refs[n + w].at[j - 1], (*_chip_of(j, x, y), c)))
        return copies
    return plan


def _chip_partials(grad, recv, name, tr):
    _, rows, width = grad.shape

    def body(g_ref, r_ref, own_ref, other_ref):
        x, y, c = _mesh_position()
        for j in range(4):
            px, py = _chip_of(j, x, y)
            total = g_ref[4 * px + 2 * py + c].astype(F32) + r_ref[j].astype(F32)
            if j == 0:
                own_ref[...] = total
            else:
                other_ref[j - 1] = total.astype(BF16)

    return pl.pallas_call(
        body, name=name, grid=(rows // tr,),
        in_specs=[pl.BlockSpec((N_DEV, tr, width), lambda i: (0, i, 0)),
                  pl.BlockSpec((4, tr, width), lambda i: (0, i, 0))],
        out_specs=[pl.BlockSpec((tr, width), lambda i: (i, 0)), pl.BlockSpec((3, tr, width), lambda i: (0, i, 0))],
        out_shape=[jax.ShapeDtypeStruct((rows, width), F32), jax.ShapeDtypeStruct((3, rows, width), BF16)],
        compiler_params=_params(("parallel",), 2 * 20 * tr * width * 2 + 8 * tr * width * 4),
    )(grad, recv)


def _adamw_math(w, g, m, v):
    m = ADAM_B1 * m + (1.0 - ADAM_B1) * g
    v = ADAM_B2 * v + (1.0 - ADAM_B2) * (g * g)
    m_hat = m / (1.0 - ADAM_B1 ** ADAM_STEP)
    v_hat = v / (1.0 - ADAM_B2 ** ADAM_STEP)
    delta = -ADAM_LR * (m_hat / (jnp.sqrt(v_hat) + ADAM_EPS) + ADAM_WD * w)
    return delta, m, v


def _adamw_shard(own, recv, w, m, v, name, tr):
    rows, width = own.shape

    def body(own_ref, r_ref, w_ref, m_ref, v_ref, g_ref, d_ref, nm_ref, nv_ref):
        g = own_ref[...]
        for j in range(3):
            g = g + r_ref[j].astype(F32)
        g_ref[...] = g
        d_ref[...], nm_ref[...], nv_ref[...] = _adamw_math(w_ref[...], g, m_ref[...], v_ref[...])

    spec = pl.BlockSpec((tr, width), lambda i: (i, 0))
    shape = jax.ShapeDtypeStruct((rows, width), F32)
    return pl.pallas_call(
        body, name=name, grid=(rows // tr,),
        in_specs=[spec, pl.BlockSpec((3, tr, width), lambda i: (0, i, 0)), spec, spec, spec],
        out_specs=[spec] * 4, out_shape=[shape] * 4,
        compiler_params=_params(("parallel",), 22 * tr * width * 4),
    )(own, recv, w, m, v)


def _small_all_reduce_adamw(gpart, lbpack, wpack, mpack, vpack, after):
    def body(gp_ref, lb_ref, w_ref, m_ref, v_ref, after_ref, g_ref, d_ref, nm_ref, nv_ref, gath_ref, send_sems, recv_sems):
        del after_ref
        x, y, c = _mesh_position()
        me = 4 * x + 2 * y + c
        gath_ref[me] = gp_ref[...]
        copies = []
        for j in range(1, N_DEV):
            peer = (x ^ (j >> 2), y ^ ((j >> 1) & 1), c ^ (j & 1))
            cp = pltpu.make_async_remote_copy(
                src_ref=gp_ref, dst_ref=gath_ref.at[me], send_sem=send_sems.at[j - 1], recv_sem=recv_sems.at[j - 1],
                device_id=peer, device_id_type=MESH)
            cp.start()
            copies.append(cp)
        for cp in copies:
            cp.wait()
        tot = gath_ref[0]
        for s in range(1, N_DEV):
            tot = tot + gath_ref[s]
        lb = lb_ref[...]
        dl = tot[16:32, :] * lb * (1.0 - lb)
        g = jnp.concatenate([tot[0:16, :], dl[0:8, :], -dl[0:8, :], dl[8:16, :], -dl[8:16, :],
                             tot[32:SMALL_GRAD_ROWS, :]], axis=0)
        g_ref[...] = g
        d_ref[...], nm_ref[...], nv_ref[...] = _adamw_math(w_ref[...], g, m_ref[...], v_ref[...])

    vm = pl.BlockSpec(memory_space=pltpu.VMEM)
    shape = jax.ShapeDtypeStruct((SMALL_ROWS, LANE), F32)
    return pl.pallas_call(
        body, name="small_all_reduce_adamw", in_specs=[vm] * 5 + [_any_spec()], out_specs=[vm] * 4,
        out_shape=[shape] * 4,
        scratch_shapes=[pltpu.VMEM((N_DEV, SMALL_GRAD_ROWS, LANE), F32),
                        pltpu.SemaphoreType.DMA((N_DEV - 1,)), pltpu.SemaphoreType.DMA((N_DEV - 1,))],
    )(gpart, lbpack, wpack, mpack, vpack, after)


_SMALL_NAMES = ("norm_mix_gain", "norm_mem_gain", "lb_logits_fw", "lb_logits_bw", "norm_ffn_gain",
                "hg_norm_gain", "da_q_gain", "da_k_gain", "mem_q_gain", "mem_k_gain")
_SMALL_ROW0 = {"norm_mix_gain": 0, "norm_mem_gain": 8, "lb_logits_fw": 16, "lb_logits_bw": 32, "norm_ffn_gain": 48,
               "hg_norm_gain": 56, "da_q_gain": 64, "da_k_gain": 72, "mem_q_gain": 80, "mem_k_gain": 88}
_LOSS_ROW = 96


def _pack_rows(parts, total_rows):
    rows = []
    for p in parts:
        r = p.reshape(-1, LANE)
        rows.append(jnp.pad(r, ((0, -r.shape[0] % 8), (0, 0))))
    used = sum(r.shape[0] for r in rows)
    if total_rows > used:
        rows.append(jnp.zeros((total_rows - used, LANE), F32))
    return jnp.concatenate(rows, axis=0)


def _unpack_small(pack, like):
    out = {}
    for name in _SMALL_NAMES:
        n = like[name].size // LANE
        r0 = _SMALL_ROW0[name]
        out[name] = pack[r0:r0 + n].reshape(like[name].shape)
    return out


def kernel(x, mem, norm_mix_gain, norm_mem_gain, w_in, lb_logits_fw, lb_logits_bw, hg_norm_gain, da_q_gain, da_k_gain, w_mem_kv, mem_q_gain, mem_k_gain, w_proj_hg, w_proj_da, w_proj_mem, w_out, norm_ffn_gain, w_ffn_in, w_ffn_out, loss_target, m_norm_mix_gain, m_norm_mem_gain, m_w_in, m_lb_logits_fw, m_lb_logits_bw, m_hg_norm_gain, m_da_q_gain, m_da_k_gain, m_w_mem_kv, m_mem_q_gain, m_mem_k_gain, m_w_proj_hg, m_w_proj_da, m_w_proj_mem, m_w_out, m_norm_ffn_gain, m_w_ffn_in, m_w_ffn_out, v_norm_mix_gain, v_norm_mem_gain, v_w_in, v_lb_logits_fw, v_lb_logits_bw, v_hg_norm_gain, v_da_q_gain, v_da_k_gain, v_w_mem_kv, v_mem_q_gain, v_mem_k_gain, v_w_proj_hg, v_w_proj_da, v_w_proj_mem, v_w_out, v_norm_ffn_gain, v_w_ffn_in, v_w_ffn_out):
    small_w = dict(norm_mix_gain=norm_mix_gain, norm_mem_gain=norm_mem_gain, lb_logits_fw=lb_logits_fw,
                   lb_logits_bw=lb_logits_bw, norm_ffn_gain=norm_ffn_gain, hg_norm_gain=hg_norm_gain,
                   da_q_gain=da_q_gain, da_k_gain=da_k_gain, mem_q_gain=mem_q_gain, mem_k_gain=mem_k_gain)
    small_m = dict(norm_mix_gain=m_norm_mix_gain, norm_mem_gain=m_norm_mem_gain, lb_logits_fw=m_lb_logits_fw,
                   lb_logits_bw=m_lb_logits_bw, norm_ffn_gain=m_norm_ffn_gain, hg_norm_gain=m_hg_norm_gain,
                   da_q_gain=m_da_q_gain, da_k_gain=m_da_k_gain, mem_q_gain=m_mem_q_gain, mem_k_gain=m_mem_k_gain)
    small_v = dict(norm_mix_gain=v_norm_mix_gain, norm_mem_gain=v_norm_mem_gain, lb_logits_fw=v_lb_logits_fw,
                   lb_logits_bw=v_lb_logits_bw, norm_ffn_gain=v_norm_ffn_gain, hg_norm_gain=v_hg_norm_gain,
                   da_q_gain=v_da_q_gain, da_k_gain=v_da_k_gain, mem_q_gain=v_mem_q_gain, mem_k_gain=v_mem_k_gain)
    big_w = dict(w_in=w_in[0], w_mem_kv=w_mem_kv[0], w_proj_hg=w_proj_hg[0], w_proj_da=w_proj_da[0],
                 w_proj_mem=w_proj_mem[0], w_out=w_out[0], w_ffn_in=w_ffn_in[0], w_ffn_out=w_ffn_out[0])
    big_m = dict(w_in=m_w_in[0], w_mem_kv=m_w_mem_kv[0], w_proj_hg=m_w_proj_hg[0], w_proj_da=m_w_proj_da[0],
                 w_proj_mem=m_w_proj_mem[0], w_out=m_w_out[0], w_ffn_in=m_w_ffn_in[0], w_ffn_out=m_w_ffn_out[0])
    big_v = dict(w_in=v_w_in[0], w_mem_kv=v_w_mem_kv[0], w_proj_hg=v_w_proj_hg[0], w_proj_da=v_w_proj_da[0],
                 w_proj_mem=v_w_proj_mem[0], w_out=v_w_out[0], w_ffn_in=v_w_ffn_in[0], w_ffn_out=v_w_ffn_out[0])

    row_tile = dict(w_in=128, w_mem_kv=128, w_proj_hg=128, w_proj_da=512, w_proj_mem=512, w_out=128,
                    w_ffn_in=128, w_ffn_out=352)
    rest = _BIG_NAMES[1:]
    shards = [big_w[n].astype(BF16) for n in rest]
    state = {}
    big_out = {}

    def reduce_start(group, stacked):
        names = tuple(stacked)
        arrays = [stacked[n] for n in names] + [lax.empty((4,) + stacked[n].shape[1:], BF16) for n in names]
        plan = _reduce_plan_sibling(len(names))
        send, recv, thru, token = _copies_start(f"grads_{group}_sibling_start", arrays, 4 * len(names), plan)
        state[group] = dict(names=names, plan=plan, send=send, recv=recv, arrays=thru)
        return token

    def reduce_middle(group, after):
        st = state[group]
        names, k = st["names"], len(st["names"])
        thru = _copies_wait(f"grads_{group}_sibling_wait", st["send"], st["recv"], st["arrays"], 4 * k, st["plan"], after)
        partials = [_chip_partials(thru[i], thru[k + i], f"chip_partials_{n}", row_tile[n]) for i, n in enumerate(names)]
        arrays = [p[1] for p in partials] + [lax.empty(p[1].shape, BF16) for p in partials]
        plan = _reduce_plan_chips(k)
        send, recv, thru2, token = _copies_start(f"grads_{group}_chips_start", arrays, 3 * k, plan)
        state[group] = dict(names=names, plan=plan, send=send, recv=recv, arrays=thru2, own=[p[0] for p in partials])
        return token

    def reduce_finish(group, after):
        st = state.pop(group)
        names, k = st["names"], len(st["names"])
        thru = _copies_wait(f"grads_{group}_chips_wait", st["send"], st["recv"], st["arrays"], 3 * k, st["plan"], after)
        for i, n in enumerate(names):
            big_out[n] = _adamw_shard(st["own"][i], thru[k + i], big_w[n], big_m[n], big_v[n], "adamw_" + n, row_tile[n])

    step = _local_step_stages(x[0], mem[0], loss_target[0], small_w)
    event, payload = next(step)
    local = None
    while True:
        reply = None
        if event == "gather_w_in":
            reply = _all_gather([big_w["w_in"].astype(BF16)])[0]
        elif event == "begin":
            nr = len(rest)
            me = 4 * lax.axis_index("x") + 2 * lax.axis_index("y") + lax.axis_index("c")
            arrays = shards + [lax.dynamic_update_slice(lax.empty((N_DEV,) + s.shape, BF16), s[None], (me, 0, 0))
                               for s in shards]
            plan = _gather_plan_out(nr)
            send, recv, thru, reply = _copies_start("weights_rest_out_start", arrays, 4 * nr, plan)
            state["gather"] = dict(plan=plan, send=send, recv=recv, arrays=thru)
        elif event == "after_gla_fwd":
            st = state["gather"]
            nr = len(rest)
            thru = _copies_wait("weights_rest_out_wait", st["send"], st["recv"], st["arrays"], 4 * nr, st["plan"], payload)
            plan = _gather_plan_pass(nr)
            send, recv, lands, reply = _copies_start("weights_rest_pass_start", thru[nr:], 3 * nr, plan)
            state["gather"] = dict(plan=plan, send=send, recv=recv, arrays=lands)
        elif event == "need_weights":
            st = state.pop("gather")
            nr = len(rest)
            lands = _copies_wait("weights_rest_pass_wait", st["send"], st["recv"], st["arrays"], 3 * nr, st["plan"], payload)
            reply = dict(zip(rest, lands))
        elif event == "grads_ffn":
            reply = reduce_start("ffn", payload)
        elif event == "after_gate_merge_bwd":
            reply = reduce_middle("ffn", payload)
        elif event == "grads_mix":
            reply = reduce_start("mix", payload)
        elif event == "after_da_bwd_g0":
            reply = reduce_middle("mix", payload)
            reduce_finish("ffn", payload)
        elif event == "after_gla_bwd":
            reduce_finish("mix", payload)
        elif event == "grads_in":
            reply = reduce_start("in", payload)
        elif event == "after_proj_bwd_act_top":
            reply = reduce_middle("in", payload)
        elif event == "end":
            local = payload
            reduce_finish("in", [local["grad_x"]] + [big_out[n][0] for n in rest])
            break
        event, payload = step.send(reply)
    grad_x = local["grad_x"]

    wpack = _pack_rows([small_w[n] for n in _SMALL_NAMES], SMALL_ROWS)
    mpack = _pack_rows([small_m[n] for n in _SMALL_NAMES], SMALL_ROWS)
    vpack = _pack_rows([small_v[n] for n in _SMALL_NAMES], SMALL_ROWS)
    gs, ds, ms, vs = _small_all_reduce_adamw(local["gpart"], local["lbpack"], wpack, mpack, vpack, big_out["w_in"][0])
    loss = gs[_LOSS_ROW, 0]
    small_out = [_unpack_small(t, small_w) for t in (gs, ds, ms, vs)]

    order = ("norm_mix_gain", "norm_mem_gain", "w_in", "lb_logits_fw", "lb_logits_bw", "hg_norm_gain", "da_q_gain",
             "da_k_gain", "w_mem_kv", "mem_q_gain", "mem_k_gain", "w_proj_hg", "w_proj_da", "w_proj_mem", "w_out",
             "norm_ffn_gain", "w_ffn_in", "w_ffn_out")
    outs = [loss, grad_x[None]]
    for kind in range(4):
        for n in order:
            outs.append(big_out[n][kind][None] if n in big_out else small_out[kind][n])
    return tuple(outs)


_BIG_NAMES = ("w_in", "w_mem_kv", "w_proj_hg", "w_proj_da", "w_proj_mem", "w_out", "w_ffn_in", "w_ffn_out")


def _local_step(xs, mems, target, sw, wg):
    step = _local_step_stages(xs, mems, target, sw)
    stacked = {}
    event, payload = next(step)
    while event != "end":
        reply = None
        if event.startswith("grads_"):
            stacked.update(payload)
        elif event == "need_weights":
            reply = wg
        elif event == "gather_w_in":
            reply = wg["w_in"]
        event, payload = step.send(reply)
    return dict(payload, stacked=stacked)


def _local_step_stages(xs, mems, target, sw):
    norm_mix_gain, norm_mem_gain, norm_ffn_gain = sw["norm_mix_gain"], sw["norm_mem_gain"], sw["norm_ffn_gain"]
    lb_logits_fw, lb_logits_bw, hg_norm_gain = sw["lb_logits_fw"], sw["lb_logits_bw"], sw["hg_norm_gain"]
    da_q_gain, da_k_gain, mem_q_gain, mem_k_gain = sw["da_q_gain"], sw["da_k_gain"], sw["mem_q_gain"], sw["mem_k_gain"]

    win_st = yield "gather_w_in", None
    token = yield "begin", None
    lb_fw = _lb_table(lb_logits_fw, "lb_table_fw")
    lb_bw = _lb_table(lb_logits_bw, "lb_table_bw")
    lb = jnp.concatenate([lb_fw, lb_bw], axis=0).reshape(2, HG_HEADS, 1, HEAD_DIM)
    h, h_t = _rmsnorm_fwd(xs, _after(norm_mix_gain, token), "norm_mix_fwd", 512, transposed=True)
    proj = _matmul(h, win_st, "nn", F32, 1024, IN_SHARD, D_MODEL, "proj_fwd", b_stacked=True, n_outer=True)
    hg_b = _gla_decay(proj, lb)
    o_hg, o_pre, states, hg_qt, hg_decay = _gla_fwd(proj, lb, hg_norm_gain, hg_b)
    token = yield "after_gla_fwd", o_hg
    da01 = [_da_fwd(proj, _after(da_q_gain, token), da_k_gain, g) for g in (0, 1)]
    o_da, o_da32, lse_da = _da_fwd(proj, _after(da_q_gain, token), da_k_gain, 2, others=da01)
    wg = yield "need_weights", o_da
    wkv = wg["w_mem_kv"].reshape(D_MODEL, 2 * MEM_WIDTH)
    wphg = wg["w_proj_hg"].reshape(D_MODEL, D_MODEL)
    wpda = jnp.transpose(wg["w_proj_da"], (1, 0, 2)).reshape(DA_WIDTH, D_MODEL)
    wpmem = jnp.transpose(wg["w_proj_mem"], (1, 0, 2)).reshape(MEM_WIDTH, D_MODEL)
    wout = wg["w_out"].reshape(D_MODEL, D_MODEL)
    wfin = jnp.transpose(wg["w_ffn_in"], (1, 0, 2)).reshape(D_MODEL, 2 * D_FF)
    wfout = wg["w_ffn_out"].reshape(D_FF, D_MODEL)
    mem_n = _rmsnorm_fwd(mems, norm_mem_gain, "norm_mem_fwd", N_MEM)
    kv = _matmul(mem_n, wkv, "nn", F32, N_MEM, 1024, D_MODEL, "mem_kv_fwd")
    o_mem = _mem_fwd(proj, kv, mem_q_gain, mem_k_gain, 1024)
    branch_w = (wphg, wpda, wpmem)
    merged, t_hg, t_da, t_mem = _branch_merge_fwd(proj, (o_hg, o_da, o_mem), branch_w, 512)
    x1, h2, h2_t = _residual_rmsnorm_fwd(xs, merged, wout, norm_ffn_gain, "out_norm_ffn_fwd", 512)
    ffn_a, ffn_b, act, act_t = _ffn_in_swiglu(h2, wfin, 1024, 1408)
    dy, dyb, loss_part = _ffn_out_loss_head(x1, act, wfout, target, 512)

    dab = _ffn_out_bwd_swiglu(dyb, wfout, ffn_a, ffn_b, 1024, 1408)
    g_wfout = _matmul(act_t, dyb, "nn", BF16, 1408, 1024, 2048, "ffn_out_bwd_w")
    g_wfin = _matmul(h2_t, dab, "nn", BF16, 1024, 1408, 2048, "ffn_in_bwd_w", b_parts=True)
    token = yield "grads_ffn", dict(
        w_ffn_in=jnp.transpose(g_wfin.reshape(D_MODEL, N_DEV, 2 * D_FF // N_DEV), (1, 0, 2)),
        w_ffn_out=g_wfout.reshape(N_DEV, D_FF // N_DEV, D_MODEL))
    dx1, dx1b, g_norm_ffn = _matmul_rmsnorm_bwd(dab, wfin, x1, dy, _after(norm_ffn_gain, token),
                                                "ffn_in_bwd_act_norm", 512, D_FF, a_parts=True)
    g_wout = _matmul(merged, dx1b, "tn", BF16, 1024, 1024, 2048, "out_bwd_w")
    dt_hg, dt_da, dt_mem, do_pre, do_da, do_mem, dd_da, g_hg_norm, dproj = _gate_merge_bwd(
        proj, (t_hg, t_da, t_mem), branch_w, dx1b, wout, o_pre, hg_norm_gain, o_da32, _dproj_buffer(), 256)
    token = yield "after_gate_merge_bwd", dt_hg
    g_wphg = _matmul(o_hg, dt_hg, "tn", BF16, 1024, 1024, 2048, "proj_hg_bwd_w", after=token)
    g_wpda = _matmul(o_da, dt_da, "tn", BF16, DA_WIDTH, D_MODEL, 2048, "proj_da_bwd_w")
    g_wpmem = _matmul(o_mem, dt_mem, "tn", BF16, MEM_WIDTH, D_MODEL, 2048, "proj_mem_bwd_w")
    by_owner = lambda g: jnp.transpose(g.reshape(g.shape[0], N_DEV, D_MODEL // N_DEV), (1, 0, 2))
    g_wpda, g_wpmem = by_owner(g_wpda), by_owner(g_wpmem)

    dproj, dk_mem, dv_mem, g_mem_q, g_mem_k = _mem_bwd(proj, kv, mem_q_gain, mem_k_gain, do_mem, dproj, 1024)
    dkv = jnp.concatenate([dk_mem, dv_mem], axis=1).astype(BF16)
    g_wkv = _matmul(mem_n, dkv, "tn", BF16, 1024, 1024, N_MEM, "mem_kv_bwd_w")
    dmem_n = _matmul(dkv, wkv, "nt", F32, N_MEM, 1024, 1024, "mem_kv_bwd_act")
    g_norm_mem = _gain_grad(mems, dmem_n, "norm_mem_bwd")
    token = yield "grads_mix", dict(
        w_mem_kv=g_wkv.reshape(N_DEV, D_MODEL // N_DEV, 2 * MEM_WIDTH),
        w_proj_hg=g_wphg.reshape(N_DEV, D_MODEL // N_DEV, D_MODEL),
        w_proj_da=g_wpda, w_proj_mem=g_wpmem,
        w_out=g_wout.reshape(N_DEV, D_MODEL // N_DEV, D_MODEL))

    dproj, g_da_q, g_da_k = _da_bwd(proj, _after(da_q_gain, token), da_k_gain, do_da, lse_da, dd_da, dproj, 0)
    token = yield "after_da_bwd_g0", g_da_q
    for g in (1, 2):
        dproj, gq_part, gk_part = _da_bwd(proj, _after(da_q_gain, token), da_k_gain, do_da, lse_da, dd_da, dproj, g)
        g_da_q, g_da_k = g_da_q + gq_part, g_da_k + gk_part

    dproj, dlb = _gla_bwd(proj, lb, states, hg_qt, hg_decay, hg_b, do_pre, dproj)
    yield "after_gla_bwd", dlb

    g_win =_matmul(h_t, dproj, "nn", BF16, 1024, IN_SHARD, 2048, "proj_bwd_w", out_stacked=True)
    token = yield "grads_in", dict(w_in=g_win)
    grad_x, g_mix_top = _matmul_rmsnorm_bwd(dproj, win_st, xs, dx1, norm_mix_gain, "proj_bwd_act_norm_top", 1024,
                                            IN_SHARD, b_stacked=True, after=token, m_blocks=(0, 1), with_bf16=False)
    token = yield "after_proj_bwd_act_top", g_mix_top
    grad_x, g_mix_bottom = _matmul_rmsnorm_bwd(dproj, win_st, xs, dx1, norm_mix_gain, "proj_bwd_act_norm_bottom", 1024,
                                               IN_SHARD, b_stacked=True, after=token, m_blocks=(1, 3),
                                               out_into=grad_x, with_bf16=False)
    g_norm_mix = g_mix_top + g_mix_bottom

    gpart = _pack_rows([g_norm_mix, g_norm_mem, dlb[0], dlb[1], g_norm_ffn, g_hg_norm, g_da_q, g_da_k,
                        g_mem_q, g_mem_k, loss_part], SMALL_GRAD_ROWS)
    lbpack = jnp.concatenate([lb_fw.reshape(8, LANE), lb_bw.reshape(8, LANE)], axis=0)
    yield "end", dict(grad_x=grad_x, gpart=gpart, lbpack=lbpack)
```

```python
import numpy as np
import jax
import jax.numpy as jnp
from jax import lax
from jax.experimental import pallas as pl
from jax.experimental.pallas import tpu as pltpu

F32 = jnp.float32
BF16 = jnp.bfloat16
MESH = pl.DeviceIdType.MESH

SEQ = 4096
D_MODEL = 1024
N_DEV = 8
N_MEM = 256
RMS_EPS = 1e-6
NEG_INF = -1e30
LANE = 128
HEAD_DIM = 128
HG_HEADS = 8
HG_CHUNK = 64
HG_SCALE = HEAD_DIM ** -0.5
DA_DILATIONS = (1, 4, 16)
DA_RADIUS = 64
DA_HEADS_PER_GROUP = 4
DA_HEADS = 12
DA_WIDTH = 512
DA_SCALE = HEAD_DIM ** -0.5
DA_QB = 128
DA_WIN = 256
DA_WAYS = 8
DA_FWD_WAYS = 8
MEM_HEADS = 4
MEM_WIDTH = 512
MEM_SCALE = HEAD_DIM ** -0.5
D_FF = 2816
IN_COLS = 13312
IN_SHARD = IN_COLS // N_DEV
CB_HG_Q, CB_F, CB_HG_I, CB_HG_G = 0, 8, 24, 32
CB_DA_Q, CB_DA_K, CB_DA_V, CB_MEM_Q = 40, 52, 64, 76
ADAM_LR, ADAM_B1, ADAM_B2, ADAM_EPS, ADAM_WD, ADAM_STEP = 0.001, 0.9, 0.999, 1e-08, 0.01, 10
VMEM_BYTES_V7X = 64 * 1024 * 1024
SMALL_ROWS = 104
SMALL_GRAD_ROWS = 88

_NN = (((1,), (0,)), ((), ()))
_NT = (((1,), (1,)), ((), ()))
_TN = (((0,), (0,)), ((), ()))


def _dot(a, b, dims):
    return lax.dot_general(a.astype(BF16), b.astype(BF16), dims, preferred_element_type=F32)


def _sigmoid(x):
    return 0.5 * jnp.tanh(0.5 * x) + 0.5


def _params(semantics, est_bytes):
    limit = int(min(VMEM_BYTES_V7X - (6 << 20), max(56 << 20, est_bytes * 3 // 2)))
    return pltpu.CompilerParams(dimension_semantics=semantics, vmem_limit_bytes=limit)


def _nbytes(shape, dtype):
    return int(np.prod(shape)) * jnp.dtype(dtype).itemsize


def _alibi_slopes(n):
    return (2.0 ** (-8.0 * np.arange(1, n + 1) / n)).astype(np.float32)


def _matmul(a, b, mode, out_dtype, tm, tn, tk, name, b_stacked=False, out_stacked=False, n_outer=False, after=None,
            m_blocks=None, out_into=None, a_parts=False, b_parts=False):
    if a_parts:
        assert mode == "nt" and tk == a.shape[2]
        m, kdim = a.shape[1], a.shape[0] * a.shape[2]
    elif mode == "tn":
        kdim, m = a.shape
    else:
        m, kdim = a.shape
    if b_parts:
        assert mode == "nn" and not b_stacked and b.shape[2] % tn == 0
        n = b.shape[0] * b.shape[2]
    elif b_stacked:
        if mode == "nn":
            n = b.shape[0] * b.shape[2]
            assert tn == b.shape[2] and tk == kdim == b.shape[1]
        else:
            assert mode == "nt" and tk == b.shape[2] and b.shape[0] * tk == kdim
            n = b.shape[1]
    else:
        n = b.shape[0] if mode == "nt" else b.shape[1]
    assert m % tm == 0 and n % tn == 0 and kdim % tk == 0
    gm, gn, gk = m // tm, n // tn, kdim // tk
    i0 = 0
    if m_blocks is not None:
        assert mode != "tn" and not out_stacked
        i0, gm = m_blocks

    def ijk(f):
        if n_outer:
            return lambda j, i, k: f(i + i0, j, k)
        return lambda i, j, k: f(i + i0, j, k)

    if a_parts:
        a_spec = pl.BlockSpec((None, tm, tk), ijk(lambda i, j, k: (k, i, 0)))
    elif mode == "tn":
        a_spec = pl.BlockSpec((tk, tm), ijk(lambda i, j, k: (k, i)))
    else:
        a_spec = pl.BlockSpec((tm, tk), ijk(lambda i, j, k: (i, k)))
    if b_parts:
        per_part = b.shape[2] // tn
        b_spec = pl.BlockSpec((None, tk, tn), ijk(lambda i, j, k: (j // per_part, k, j % per_part)))
    elif b_stacked and mode == "nn":
        b_spec = pl.BlockSpec((None, tk, tn), ijk(lambda i, j, k: (j, 0, 0)))
    elif b_stacked:
        b_spec = pl.BlockSpec((None, tn, tk), ijk(lambda i, j, k: (k, j, 0)))
    elif mode == "nt":
        b_spec = pl.BlockSpec((tn, tk), ijk(lambda i, j, k: (j, k)))
    else:
        b_spec = pl.BlockSpec((tk, tn), ijk(lambda i, j, k: (k, j)))
    if out_stacked:
        assert tm == m
        out_shape = jax.ShapeDtypeStruct((gn, m, tn), out_dtype)
        o_spec = pl.BlockSpec((None, tm, tn), ijk(lambda i, j, k: (j, i, 0)))
    else:
        out_shape = jax.ShapeDtypeStruct((m, n), out_dtype)
        o_spec = pl.BlockSpec((tm, tn), ijk(lambda i, j, k: (i, j)))
    dims = {"nn": _NN, "nt": _NT, "tn": _TN}[mode]

    n_in = 2 + (after is not None) + (out_into is not None)

    def body(*refs):
        a_ref, b_ref, o_ref = refs[0], refs[1], refs[n_in]
        part = _dot(a_ref[...], b_ref[...], dims)
        if gk == 1:
            o_ref[...] = part.astype(out_dtype)
            return
        acc_ref = refs[-1]
        k = pl.program_id(2)

        @pl.when(k == 0)
        def _():
            acc_ref[...] = part

        @pl.when(jnp.logical_and(k > 0, k < gk - 1))
        def _():
            acc_ref[...] += part

        @pl.when(k == gk - 1)
        def _():
            o_ref[...] = (acc_ref[...] + part).astype(out_dtype)

    a_tile = _nbytes((tm, tk), a.dtype)
    b_tile = _nbytes((tk, tn), b.dtype)
    o_tile = _nbytes((tm, tn), out_dtype)
    est = 2 * (a_tile + b_tile + o_tile) + 3 * tm * tn * 4 + (a_tile + b_tile)
    grid = (gn, gm, gk) if n_outer else (gm, gn, gk)
    operands, in_specs = [a, b], [a_spec, b_spec]
    if after is not None:
        operands.append(after)
        in_specs.append(pl.BlockSpec(memory_space=pl.ANY))
    aliases = {}
    if out_into is not None:
        aliases = {len(operands): 0}
        operands.append(out_into)
        in_specs.append(pl.BlockSpec(memory_space=pl.ANY))
    return pl.pallas_call(
        body, name=name, grid=grid, in_specs=in_specs, out_specs=o_spec, out_shape=out_shape,
        scratch_shapes=[] if gk == 1 else [pltpu.VMEM((tm, tn), F32)], input_output_aliases=aliases,
        compiler_params=_params(("parallel", "parallel", "arbitrary"), est),
    )(*operands)


def _row_spec(tr, width, col_block=0):
    return pl.BlockSpec((tr, width), lambda i: (i, col_block))


def _bcast_spec(width):
    return pl.BlockSpec((1, width), lambda i: (0, 0))


def _col_spec(width, tr):
    return pl.BlockSpec((width, tr), lambda i: (0, i))


def _rmsnorm_fwd(x, gain, name, tr, transposed=False):
    rows, width = x.shape

    def body(x_ref, g_ref, o_ref, *t_ref):
        xv = x_ref[...]
        r = lax.rsqrt(jnp.mean(xv * xv, axis=-1, keepdims=True) + RMS_EPS)
        h = xv * r * g_ref[...]
        o_ref[...] = h.astype(BF16)
        if transposed:
            t_ref[0][...] = h.T.astype(BF16)

    out_specs, out_shape = [_row_spec(tr, width)], [jax.ShapeDtypeStruct((rows, width), BF16)]
    if transposed:
        out_specs.append(_col_spec(width, tr))
        out_shape.append(jax.ShapeDtypeStruct((width, rows), BF16))
    out = pl.pallas_call(
        body, name=name, grid=(rows // tr,), in_specs=[_row_spec(tr, width), _bcast_spec(width)],
        out_specs=out_specs, out_shape=out_shape,
        compiler_params=_params(("parallel",), 10 * tr * width * 4),
    )(x, gain)
    return out if transposed else out[0]


def _residual_rmsnorm_fwd(x, merged, wout, gain, name, tr):
    rows, width = x.shape

    def body(x_ref, m_ref, w_ref, g_ref, x1_ref, h_ref, ht_ref):
        xv = x_ref[...] + _dot(m_ref[...], w_ref[...], _NN)
        x1_ref[...] = xv
        r = lax.rsqrt(jnp.mean(xv * xv, axis=-1, keepdims=True) + RMS_EPS)
        h = xv * r * g_ref[...]
        h_ref[...] = h.astype(BF16)
        ht_ref[...] = h.T.astype(BF16)

    return pl.pallas_call(
        body, name=name, grid=(rows // tr,),
        in_specs=[_row_spec(tr, width), _row_spec(tr, merged.shape[1]),
                  pl.BlockSpec(wout.shape, lambda i: (0, 0)), _bcast_spec(width)],
        out_specs=[_row_spec(tr, width), _row_spec(tr, width), _col_spec(width, tr)],
        out_shape=[jax.ShapeDtypeStruct((rows, width), F32), jax.ShapeDtypeStruct((rows, width), BF16),
                   jax.ShapeDtypeStruct((width, rows), BF16)],
        compiler_params=_params(("parallel",), 14 * tr * width * 4),
    )(x, merged, wout, gain)


def _matmul_rmsnorm_bwd(a, b, x, dres, gain, name, tm, tk, a_parts=False, b_stacked=False, after=None,
                        m_blocks=None, out_into=None, with_bf16=True):
    width = x.shape[1]
    m = a.shape[1] if a_parts else a.shape[0]
    kdim = a.shape[0] * a.shape[2] if a_parts else a.shape[1]
    gk = kdim // tk
    i0, gm = (0, m // tm) if m_blocks is None else m_blocks
    n_in = 5 + (after is not None) + (out_into is not None)

    def body(*refs):
        a_ref, b_ref, x_ref, dres_ref, g_ref = refs[:5]
        outs = refs[n_in:]
        dx_ref, dg_ref, acc_ref = outs[0], outs[-2], outs[-1]
        i, k = pl.program_id(0), pl.program_id(1)
        part = _dot(a_ref[...], b_ref[...], _NT)

        @pl.when(k == 0)
        def _():
            acc_ref[...] = part

        @pl.when(jnp.logical_and(k > 0, k < gk - 1))
        def _():
            acc_ref[...] += part

        @pl.when(k == gk - 1)
        def _():
            dhv = acc_ref[...] + part if gk > 1 else part
            xv = x_ref[...]
            r = lax.rsqrt(jnp.mean(xv * xv, axis=-1, keepdims=True) + RMS_EPS)
            xhat = xv * r
            dyg = dhv * g_ref[...]
            dx = dres_ref[...] + r * (dyg - xhat * jnp.mean(dyg * xhat, axis=-1, keepdims=True))
            dx_ref[...] = dx
            if with_bf16:
                outs[1][...] = dx.astype(BF16)
            gpart = jnp.sum(dhv * xhat, axis=0, keepdims=True)

            @pl.when(i == 0)
            def _():
                dg_ref[...] = gpart

            @pl.when(i > 0)
            def _():
                dg_ref[...] += gpart

    rows = lambda width_: pl.BlockSpec((tm, width_), lambda i, k: (i + i0, 0))
    if a_parts:
        a_spec = pl.BlockSpec((None, tm, tk), lambda i, k: (k, i + i0, 0))
    else:
        a_spec = pl.BlockSpec((tm, tk), lambda i, k: (i + i0, k))
    if b_stacked:
        b_spec = pl.BlockSpec((None, width, tk), lambda i, k: (k, 0, 0))
    else:
        b_spec = pl.BlockSpec((width, tk), lambda i, k: (0, k))
    operands = [a, b, x, dres, gain]
    in_specs = [a_spec, b_spec, rows(width), rows(width), pl.BlockSpec((1, width), lambda i, k: (0, 0))]
    for extra in (after, out_into):
        if extra is not None:
            operands.append(extra)
            in_specs.append(pl.BlockSpec(memory_space=pl.ANY))
    aliases = {} if out_into is None else {len(operands) - 1: 0}
    out_specs = [rows(width)] + ([rows(width)] if with_bf16 else []) + [pl.BlockSpec((1, width), lambda i, k: (0, 0))]
    out_shape = [jax.ShapeDtypeStruct((m, width), F32)] + ([jax.ShapeDtypeStruct((m, width), BF16)] if with_bf16 else [])
    out_shape.append(jax.ShapeDtypeStruct((1, width), F32))
    est = 4 * tm * tk + 4 * width * tk + 12 * tm * width * 4
    return pl.pallas_call(
        body, name=name, grid=(gm, gk), in_specs=in_specs, out_specs=out_specs, out_shape=out_shape,
        scratch_shapes=[pltpu.VMEM((tm, width), F32)], input_output_aliases=aliases,
        compiler_params=_params(("arbitrary", "arbitrary"), est),
    )(*operands)


def _gain_grad(x, dh, name):
    rows, width = x.shape

    def body(x_ref, dh_ref, dg_ref):
        xv = x_ref[...]
        r = lax.rsqrt(jnp.mean(xv * xv, axis=-1, keepdims=True) + RMS_EPS)
        dg_ref[...] = jnp.sum(dh_ref[...] * xv * r, axis=0, keepdims=True)

    return pl.pallas_call(
        body, name=name, grid=(1,), in_specs=[_row_spec(rows, width), _row_spec(rows, width)],
        out_specs=_bcast_spec(width), out_shape=jax.ShapeDtypeStruct((1, width), F32),
        compiler_params=_params(("arbitrary",), 6 * rows * width * 4),
    )(x, dh)


def _lb_table(logits, name):
    slots, width = logits.shape

    def body(l_ref, o_ref):
        lv = l_ref[...]
        mx = jnp.max(lv, axis=0, keepdims=True)
        e = jnp.exp(lv - mx)
        o_ref[...] = e[0:1, :] / jnp.sum(e, axis=0, keepdims=True)

    return pl.pallas_call(
        body, name=name, grid=(1,), in_specs=[pl.BlockSpec((slots, width), lambda i: (0, 0))],
        out_specs=_bcast_spec(width), out_shape=jax.ShapeDtypeStruct((1, width), F32),
    )(logits)


def _branch_merge_fwd(proj, outs, weights, tr):
    w = D_MODEL

    def body(ghg_ref, gda_ref, gmem_ref, ohg_ref, oda_ref, omem_ref, whg_ref, wda_ref, wmem_ref,
             m_ref, thg_ref, tda_ref, tmem_ref):
        acc = None
        for g_ref, o_ref, w_ref, t_ref in ((ghg_ref, ohg_ref, whg_ref, thg_ref), (gda_ref, oda_ref, wda_ref, tda_ref),
                                           (gmem_ref, omem_ref, wmem_ref, tmem_ref)):
            t = _dot(o_ref[...], w_ref[...], _NN)
            t_ref[...] = t.astype(BF16)
            term = _sigmoid(g_ref[...]) * t
            acc = term if acc is None else acc + term
        m_ref[...] = acc.astype(BF16)

    whole = lambda a: pl.BlockSpec(a.shape, lambda i: (0, 0))
    shape = jax.ShapeDtypeStruct((SEQ, w), BF16)
    return pl.pallas_call(
        body, name="branch_merge_fwd", grid=(SEQ // tr,),
        in_specs=[_row_spec(tr, w, 10), _row_spec(tr, w, 11), _row_spec(tr, w, 12)]
        + [_row_spec(tr, o.shape[1]) for o in outs] + [whole(wt) for wt in weights],
        out_specs=[_row_spec(tr, w)] * 4, out_shape=[shape] * 4,
        compiler_params=_params(("parallel",), 20 * tr * w * 4),
    )(proj, proj, proj, *outs, *weights)


def _dproj_buffer():
    return lax.empty((SEQ, IN_COLS), BF16)


def _gate_merge_bwd(proj, ts, weights, dx1b, wout, o_pre, hg_gain, o_da32, dproj, tr):
    w = D_MODEL
    steps = SEQ // tr
    gate_col0 = 10 * w
    hg_gate_col0 = CB_HG_G * LANE

    def body(ghg_ref, gda_ref, gmem_ref, hgg_ref, thg_ref, tda_ref, tmem_ref, whg_ref, wda_ref, wmem_ref, dx_ref,
             wout_ref, opre_ref, gain_ref, oda_ref, dproj_in,
             dthg_ref, dtda_ref, dtmem_ref, dopre_ref, doda_ref, domem_ref, dd_ref, dgain_ref, dproj_ref,
             stage, stage_hg, sems):
        del dproj_in
        i = pl.program_id(0)
        slot = i % 2
        rows = pl.ds(pl.multiple_of(i * tr, tr), tr)

        def slot_copies(s):
            return (pltpu.make_async_copy(stage.at[s], dproj_ref.at[rows, pl.ds(gate_col0, 3 * w)], sems.at[s]),
                    pltpu.make_async_copy(stage_hg.at[s], dproj_ref.at[rows, pl.ds(hg_gate_col0, w)], sems.at[2 + s]))

        @pl.when(i >= 2)
        def _():
            for cp in slot_copies(slot):
                cp.wait()

        dm = _dot(dx_ref[...], wout_ref[...], _NT)
        branches = ((ghg_ref, thg_ref, whg_ref, dthg_ref), (gda_ref, tda_ref, wda_ref, dtda_ref),
                    (gmem_ref, tmem_ref, wmem_ref, dtmem_ref))
        dos = []
        for b, (g_ref, t_ref, w_ref, dt_ref) in enumerate(branches):
            s = _sigmoid(g_ref[...])
            dt = (s * dm).astype(BF16)
            dt_ref[...] = dt
            dos.append(_dot(dt, w_ref[...], _NT))
            stage[slot, :, b * w:(b + 1) * w] = (dm * t_ref[...].astype(F32) * s * (1.0 - s)).astype(BF16)
        do_hg, do_da, do_mem = dos
        doda_ref[...] = do_da
        domem_ref[...] = do_mem

        gainv = gain_ref[...]
        part = jnp.zeros((1, HEAD_DIM), F32)
        for h in range(HG_HEADS):
            hs = slice(h * HEAD_DIM, (h + 1) * HEAD_DIM)
            o = opre_ref[:, hs]
            r = lax.rsqrt(jnp.mean(o * o, axis=-1, keepdims=True) + RMS_EPS)
            ohat = o * r
            g = hgg_ref[:, hs]
            sg = _sigmoid(g)
            silu = g * sg
            dout = do_hg[:, hs]
            stage_hg[slot, :, hs] = (dout * ohat * gainv * (sg + silu * (1.0 - sg))).astype(BF16)
            dy = dout * silu
            part = part + jnp.sum(dy * ohat, axis=0, keepdims=True)
            dn = dy * gainv
            dopre_ref[:, hs] = r * (dn - ohat * jnp.mean(dn * ohat, axis=-1, keepdims=True))
        for cp in slot_copies(slot):
            cp.start()

        prod = do_da * oda_ref[...]
        for h in range(DA_WIDTH // HEAD_DIM):
            hs = slice(h * HEAD_DIM, (h + 1) * HEAD_DIM)
            dd_ref[:, hs] = jnp.broadcast_to(jnp.sum(prod[:, hs], axis=-1, keepdims=True), (tr, HEAD_DIM))

        @pl.when(i == 0)
        def _():
            dgain_ref[...] = part

        @pl.when(i > 0)
        def _():
            dgain_ref[...] += part

        @pl.when(i == steps - 1)
        def _():
            for cp in slot_copies(1 - slot) + slot_copies(slot):
                cp.wait()

    assert steps >= 2
    whole = lambda a: pl.BlockSpec(a.shape, lambda i: (0, 0))
    widths = [wt.shape[0] for wt in weights]
    return pl.pallas_call(
        body, name="gate_merge_bwd", grid=(steps,),
        in_specs=[_row_spec(tr, w, 10), _row_spec(tr, w, 11), _row_spec(tr, w, 12), _row_spec(tr, w, hg_gate_col0 // w)]
        + [_row_spec(tr, w)] * 3 + [whole(wt) for wt in weights]
        + [_row_spec(tr, w), whole(wout), _row_spec(tr, w), _bcast_spec(HEAD_DIM), _row_spec(tr, DA_WIDTH), _any_spec()],
        out_specs=[_row_spec(tr, w)] * 4 + [_row_spec(tr, widths[1]), _row_spec(tr, widths[2]),
                                            _row_spec(tr, DA_WIDTH), _bcast_spec(HEAD_DIM), _any_spec()],
        out_shape=[jax.ShapeDtypeStruct((SEQ, w), BF16)] * 3 + [jax.ShapeDtypeStruct((SEQ, w), F32)]
        + [jax.ShapeDtypeStruct((SEQ, widths[1]), F32), jax.ShapeDtypeStruct((SEQ, widths[2]), F32),
           jax.ShapeDtypeStruct((SEQ, DA_WIDTH), F32), jax.ShapeDtypeStruct((1, HEAD_DIM), F32),
           jax.ShapeDtypeStruct((SEQ, IN_COLS), BF16)],
        scratch_shapes=[pltpu.VMEM((2, tr, 3 * w), BF16), pltpu.VMEM((2, tr, w), BF16), pltpu.SemaphoreType.DMA((4,))],
        input_output_aliases={15: 8},
        compiler_params=_params(("arbitrary",), 44 * tr * w * 4),
    )(proj, proj, proj, proj, *ts, *weights, dx1b, wout, o_pre, hg_gain, o_da32, dproj)


def _ffn_in_swiglu(h2, wfin, tm, tn):
    gm, gn = SEQ // tm, D_FF // tn

    def body(h_ref, wa_ref, wb_ref, a_ref, b_ref, act_ref, actt_ref):
        h = h_ref[...]
        a = _dot(h, wa_ref[...], _NN)
        b = _dot(h, wb_ref[...], _NN)
        act = a * _sigmoid(a) * b
        a_ref[...] = a.astype(BF16)
        b_ref[...] = b.astype(BF16)
        act_ref[...] = act.astype(BF16)
        actt_ref[...] = act.T.astype(BF16)

    tile = pl.BlockSpec((tm, tn), lambda j, i: (i, j))
    shape = jax.ShapeDtypeStruct((SEQ, D_FF), BF16)
    return pl.pallas_call(
        body, name="ffn_in_swiglu_fwd", grid=(gn, gm),
        in_specs=[pl.BlockSpec((tm, D_MODEL), lambda j, i: (i, 0)),
                  pl.BlockSpec((D_MODEL, tn), lambda j, i: (0, j)),
                  pl.BlockSpec((D_MODEL, tn), lambda j, i: (0, gn + j))],
        out_specs=[tile, tile, tile, pl.BlockSpec((tn, tm), lambda j, i: (j, i))],
        out_shape=[shape, shape, shape, jax.ShapeDtypeStruct((D_FF, SEQ), BF16)],
        compiler_params=_params(("parallel", "parallel"), 4 * tm * D_MODEL + 8 * D_MODEL * tn + 16 * tm * tn
                                + 6 * tm * tn * 4),
    )(h2, wfin, wfin)


def _ffn_out_bwd_swiglu(dyb, wfout, a, b, tm, tn):
    gm, gn = SEQ // tm, D_FF // tn

    def body(dy_ref, w_ref, a_ref, b_ref, o_ref):
        d = _dot(dy_ref[...], w_ref[...], _NT)
        av = a_ref[...].astype(F32)
        bv = b_ref[...].astype(F32)
        s = _sigmoid(av)
        silu = av * s
        o_ref[0] = (d * bv * (s + silu * (1.0 - s))).astype(BF16)
        o_ref[1] = (d * silu).astype(BF16)

    tile = pl.BlockSpec((tm, tn), lambda i, j: (i, j))
    return pl.pallas_call(
        body, name="ffn_out_bwd_swiglu", grid=(gm, gn),
        in_specs=[pl.BlockSpec((tm, D_MODEL), lambda i, j: (i, 0)), pl.BlockSpec((tn, D_MODEL), lambda i, j: (j, 0)),
                  tile, tile],
        out_specs=pl.BlockSpec((2, tm, tn), lambda i, j: (0, i, j)),
        out_shape=jax.ShapeDtypeStruct((2, SEQ, D_FF), BF16),
        compiler_params=_params(("parallel", "parallel"), 4 * tm * D_MODEL + 4 * tn * D_MODEL + 16 * tm * tn
                                + 6 * tm * tn * 4),
    )(dyb, wfout, a, b)


def _ffn_out_loss_head(x1, act, wfout, target, tr):
    w = D_MODEL

    def body(x_ref, a_ref, w_ref, t_ref, dy_ref, dyb_ref, loss_ref, acc_ref):
        err = x_ref[...] + _dot(a_ref[...], w_ref[...], _NN) - t_ref[...]
        dy = err * (1.0 / w)
        dy_ref[...] = dy
        dyb_ref[...] = dy.astype(BF16)
        part = jnp.sum(err * err, axis=0, keepdims=True)

        @pl.when(pl.program_id(0) == 0)
        def _():
            acc_ref[...] = part

        @pl.when(pl.program_id(0) > 0)
        def _():
            acc_ref[...] += part

        @pl.when(pl.program_id(0) == SEQ // tr - 1)
        def _():
            total = jnp.sum(acc_ref[...], axis=1, keepdims=True) * (0.5 / w)
            loss_ref[...] = jnp.broadcast_to(total, (1, LANE))

    return pl.pallas_call(
        body, name="ffn_out_loss_head", grid=(SEQ // tr,),
        in_specs=[_row_spec(tr, w), _row_spec(tr, D_FF), pl.BlockSpec((D_FF, w), lambda i: (0, 0)), _row_spec(tr, w)],
        out_specs=[_row_spec(tr, w), _row_spec(tr, w), _bcast_spec(LANE)],
        out_shape=[jax.ShapeDtypeStruct((SEQ, w), F32), jax.ShapeDtypeStruct((SEQ, w), BF16),
                   jax.ShapeDtypeStruct((1, LANE), F32)],
        scratch_shapes=[pltpu.VMEM((1, w), F32)],
        compiler_params=_params(("arbitrary",), 12 * tr * w * 4 + 4 * D_FF * w),
    )(x1, act, wfout, target)


GLA_ROWS = 256
GLA_CPB = GLA_ROWS // HG_CHUNK
GLA_NBLK = SEQ // GLA_ROWS
GLA_WAYS = 8
GLA_TRIPS = GLA_NBLK // GLA_WAYS
GLA_GRAD_WAYS = 8
GLA_GRAD_TRIPS = GLA_NBLK // GLA_GRAD_WAYS
GLA_NCK = SEQ // HG_CHUNK
GLA_INTER_WAYS = 32


def _dot_split(m, xv, dims, terms=3):
    dot = lambda t: lax.dot_general(m, t, dims, preferred_element_type=F32)
    hi = xv.astype(BF16)
    r1 = xv - hi.astype(F32)
    mid = r1.astype(BF16)
    if terms == 2:
        return dot(mid) + dot(hi)
    lo = (r1 - mid.astype(F32)).astype(BF16)
    return (dot(lo) + dot(mid)) + dot(hi)


def _gla_block(qc, fc, lbv, masks, b=None):
    mask, maskb, direction = masks
    sq = _sigmoid(qc)
    q = qc * sq * HG_SCALE
    sf = _sigmoid(fc)
    forget = lbv + (1.0 - lbv) * sf
    k = 1.0 - forget
    logf = jnp.log(forget)
    if b is None:
        b = _dot_split(maskb, logf, _NN)
    ends = []
    for j in range(GLA_CPB):
        lo, hi = j * HG_CHUNK, (j + 1) * HG_CHUNK
        end = jnp.where(direction == 0, b[hi - 1:hi, :], b[lo:lo + 1, :])
        ends.append(jnp.broadcast_to(end, (HG_CHUNK, HEAD_DIM)))
    bt = jnp.concatenate(ends, axis=0)
    eb = jnp.exp(b)
    qt = q * eb
    kt = k * jnp.exp(-b)
    kh = k * jnp.exp(bt - b)
    a = jnp.where(mask, _dot(qt, kt, _NT), 0.0)
    return dict(sq=sq, sf=sf, forget=forget, k=k, b=b, bt=bt, eb=eb, qt=qt, kt=kt, kh=kh, a=a)


def _gla_masks(direction):
    row = lax.broadcasted_iota(jnp.int32, (GLA_ROWS, GLA_ROWS), 0)
    col = lax.broadcasted_iota(jnp.int32, (GLA_ROWS, GLA_ROWS), 1)
    same = (row // HG_CHUNK) == (col // HG_CHUNK)
    mask = jnp.logical_and(same, jnp.where(direction == 0, row - col, col - row) >= 0)
    return mask, jnp.where(mask, 1.0, 0.0).astype(BF16), direction


def _gla_chunk_rows(j):
    return slice(j * HG_CHUNK, (j + 1) * HG_CHUNK)


def _gla_block_rows(b):
    return pl.ds(pl.multiple_of(b * GLA_ROWS, GLA_ROWS), GLA_ROWS)


def _head_spec(col_block0):
    return pl.BlockSpec((SEQ, HEAD_DIM), lambda h, d: (0, col_block0 + h))


def _gla_decay(proj, lb):
    def body(f_ref, lb_ref, b_ref):
        maskb = _gla_masks(pl.program_id(0))[1]
        lbv = lb_ref[...]

        def step(s, carry):
            rows = [_gla_block_rows(s + w * trips) for w in range(ways)]
            loaded = [f_ref[r, :] for r in rows]
            outs = [_dot_split(maskb, jnp.log(lbv + (1.0 - lbv) * _sigmoid(fc)), _NN) for fc in loaded]
            for r, o in zip(rows, outs):
                b_ref[r, :] = o
            return carry

        lax.fori_loop(0, trips, step, 0)

    ways, trips = GLA_WAYS, GLA_TRIPS
    heads = 2
    width = heads * HEAD_DIM
    return pl.pallas_call(
        body, name="gla_decay", grid=(2, HG_HEADS // heads),
        in_specs=[pl.BlockSpec((SEQ, width), lambda d, h: (0, CB_F // heads + (HG_HEADS // heads) * d + h)),
                  pl.BlockSpec((None, None, 1, width), lambda d, h: (d, h, 0, 0))],
        out_specs=pl.BlockSpec((None, SEQ, width), lambda d, h: (d, 0, h)),
        out_shape=jax.ShapeDtypeStruct((2, SEQ, D_MODEL), F32),
        compiler_params=_params(("parallel", "parallel"), 6 * SEQ * width * 4),
    )(proj, lb.reshape(2, HG_HEADS // heads, 1, width))


def _gla_fwd(proj, lb, gain, b_all):
    nck = GLA_NCK

    def body(q_ref, f_ref, v_ref, g_ref, lb_ref, gain_ref, b_ref, ohg_ref, opre_ref, st_ref, qt_scr, dec_scr, cs_scr):
        d = pl.program_id(1)
        masks = _gla_masks(d)
        lbv = lb_ref[...]

        @pl.when(d == 0)
        def _():
            opre_ref[...] = jnp.zeros_like(opre_ref)

        def intra(s, carry):
            blocks = [s + w * GLA_TRIPS for w in range(GLA_WAYS)]
            rows = [_gla_block_rows(b) for b in blocks]
            loaded = [(q_ref[r, :], f_ref[r, :], v_ref[r, :], opre_ref[r, :], b_ref[r, :]) for r in rows]
            results = []
            for qc, fc, v, o_prev, decay_log in loaded:
                ck = _gla_block(qc, fc, lbv, masks, b=decay_log)
                o = o_prev + _dot(ck["a"], v, _NN)
                cs = [_dot(v[_gla_chunk_rows(j), :], ck["kh"][_gla_chunk_rows(j), :], _TN) for j in range(GLA_CPB)]
                dec = [jnp.exp(ck["bt"][j * HG_CHUNK:j * HG_CHUNK + 8, :]) for j in range(GLA_CPB)]
                results.append((o, ck["qt"].astype(BF16), cs, dec))
            for b, r, (o, qt, cs, dec) in zip(blocks, rows, results):
                opre_ref[r, :] = o
                qt_scr[r, :] = qt
                for j in range(GLA_CPB):
                    cs_scr[b * GLA_CPB + j] = cs[j]
                    dec_scr[b * GLA_CPB + j] = dec[j]
            return carry

        lax.fori_loop(0, GLA_TRIPS, intra, 0)

        def scan(i, st):
            c = jnp.where(d == 0, i, nck - 1 - i)
            st_ref[c] = st
            return st * dec_scr[c][0:1, :] + cs_scr[c]

        lax.fori_loop(0, nck, scan, jnp.zeros((HEAD_DIM, HEAD_DIM), F32), unroll=4)

        def inter(t, carry):
            chunks = [t + w * (nck // GLA_INTER_WAYS) for w in range(GLA_INTER_WAYS)]
            rows = [pl.ds(pl.multiple_of(c * HG_CHUNK, HG_CHUNK), HG_CHUNK) for c in chunks]
            loaded = [(opre_ref[r, :], qt_scr[r, :], st_ref[c]) for r, c in zip(rows, chunks)]
            for r, o in zip(rows, [o_prev + _dot(qt, st, _NT) for o_prev, qt, st in loaded]):
                opre_ref[r, :] = o
            return carry

        lax.fori_loop(0, nck // GLA_INTER_WAYS, inter, 0)

        @pl.when(d == 1)
        def _():
            o = opre_ref[...]
            r = lax.rsqrt(jnp.mean(o * o, axis=-1, keepdims=True) + RMS_EPS)
            g = g_ref[...]
            ohg_ref[...] = (o * r * gain_ref[...] * (g * _sigmoid(g))).astype(BF16)

    blk = SEQ * HEAD_DIM * 4
    return pl.pallas_call(
        body, name="gla_fwd", grid=(HG_HEADS, 2),
        in_specs=[_head_spec(CB_HG_Q),
                  pl.BlockSpec((SEQ, HEAD_DIM), lambda h, d: (0, CB_F + 8 * d + h)),
                  _head_spec(CB_HG_I), _head_spec(CB_HG_G),
                  pl.BlockSpec((None, None, 1, HEAD_DIM), lambda h, d: (d, h, 0, 0)),
                  pl.BlockSpec((1, HEAD_DIM), lambda h, d: (0, 0)),
                  pl.BlockSpec((None, SEQ, HEAD_DIM), lambda h, d: (d, 0, h))],
        out_specs=[_head_spec(0), _head_spec(0),
                   pl.BlockSpec((None, None, nck, HEAD_DIM, HEAD_DIM), lambda h, d: (h, d, 0, 0, 0)),
                   pl.BlockSpec((None, SEQ, HEAD_DIM), lambda h, d: (d, 0, h)),
                   pl.BlockSpec((None, None, nck, 8, HEAD_DIM), lambda h, d: (h, d, 0, 0, 0))],
        out_shape=[jax.ShapeDtypeStruct((SEQ, D_MODEL), BF16), jax.ShapeDtypeStruct((SEQ, D_MODEL), F32),
                   jax.ShapeDtypeStruct((HG_HEADS, 2, nck, HEAD_DIM, HEAD_DIM), F32),
                   jax.ShapeDtypeStruct((2, SEQ, D_MODEL), BF16),
                   jax.ShapeDtypeStruct((HG_HEADS, 2, nck, 8, HEAD_DIM), F32)],
        scratch_shapes=[pltpu.VMEM((nck, HEAD_DIM, HEAD_DIM), F32)],
        compiler_params=_params(("parallel", "arbitrary"), 8 * blk + 3 * blk + 2 * blk + 2 * blk + 3 * blk),
    )(proj, proj, proj, proj, lb, gain, b_all)


def _gla_bwd(proj, lb, states, qt_all, dec_all, b_all, do, dproj):
    nck = GLA_NCK

    def body(q_ref, f_ref, v_ref, lb_ref, do_ref, st_ref, qt_ref, dec_scr, b_ref, dproj_in, dproj_ref, dlb_ref,
             dq_acc, dv_acc, dst_scr, cs_scr, df_out, dq_out, dv_out, sems):
        del dproj_in
        h, d = pl.program_id(0), pl.program_id(1)
        masks = _gla_masks(d)
        mask, maskb, _ = masks
        lbv = lb_ref[...]

        def column_copy(k, staging, col_block):
            cols = pl.ds(pl.multiple_of(col_block * LANE, LANE), LANE)
            return pltpu.make_async_copy(staging, dproj_ref.at[:, cols], sems.at[k])

        df_copy = column_copy(0, df_out, CB_F + 8 * d + h)
        dq_copy = column_copy(1, dq_out, CB_HG_Q + h)
        dv_copy = column_copy(2, dv_out, CB_HG_I + h)
        last = jnp.logical_and(h == HG_HEADS - 1, d == 1)

        def intra(s, carry):
            blocks = [s + w * GLA_TRIPS for w in range(GLA_WAYS)]
            loaded = [(qt_ref[r, :], do_ref[r, :]) for r in map(_gla_block_rows, blocks)]
            results = [[_dot(doc[_gla_chunk_rows(j), :], qt[_gla_chunk_rows(j), :], _TN) for j in range(GLA_CPB)]
                       for qt, doc in loaded]
            for b, cs in zip(blocks, results):
                for j in range(GLA_CPB):
                    cs_scr[b * GLA_CPB + j] = cs[j]
            return carry

        lax.fori_loop(0, GLA_TRIPS, intra, 0)

        def scan(i, dst):
            c = jnp.where(d == 0, nck - 1 - i, i)
            dst_scr[c] = dst
            return dst * dec_scr[c][0:1, :] + cs_scr[c]

        lax.fori_loop(0, nck, scan, jnp.zeros((HEAD_DIM, HEAD_DIM), F32), unroll=4)

        @pl.when(d == 0)
        def _():
            dq_acc[...] = jnp.zeros_like(dq_acc)
            dv_acc[...] = jnp.zeros_like(dv_acc)

        def block_grads(qc, fc, v, doc, states_in, dstates, decays, decay_log):
            ck = _gla_block(qc, fc, lbv, masks, b=decay_log)
            qt, kt, kh, a = ck["qt"], ck["kt"], ck["kh"], ck["a"]
            da = jnp.where(mask, _dot(doc, v, _NT), 0.0)
            dqt_i = _dot(da, kt, _NN)
            dkt = _dot(da, qt, _TN)
            dv_i = _dot(a, doc, _TN)
            dqt_p, dv_p, dkh_p, dbt_p = [], [], [], []
            for j in range(GLA_CPB):
                cr = _gla_chunk_rows(j)
                st_in, dst = states_in[j], dstates[j]
                dqt_p.append(dqt_i[cr, :] + _dot(doc[cr, :], st_in, _NN))
                dv_p.append(dv_i[cr, :] + _dot(kh[cr, :], dst, _NT))
                dkh_j = _dot(v[cr, :], dst, _NN)
                dkh_p.append(dkh_j)
                dbt_j = (decays[j][0:1, :] * jnp.sum(dst * st_in, axis=0, keepdims=True)
                         + jnp.sum(dkh_j * kh[cr, :], axis=0, keepdims=True))
                dbt_p.append(jnp.broadcast_to(dbt_j, (HG_CHUNK, HEAD_DIM)))
            dqt = jnp.concatenate(dqt_p, axis=0)
            dv = jnp.concatenate(dv_p, axis=0)
            dkh = jnp.concatenate(dkh_p, axis=0)
            dbt = jnp.concatenate(dbt_p, axis=0)
            db = dqt * qt - dkt * kt - dkh * kh
            dq = dqt * ck["eb"]
            dk = dkt * jnp.exp(-ck["b"]) + dkh * jnp.exp(ck["bt"] - ck["b"])
            dlogf = _dot_split(maskb, db, _TN, terms=2) + dbt
            dforget = dlogf / ck["forget"] - dk
            sf, sq = ck["sf"], ck["sq"]
            df = (dforget * (1.0 - lbv) * sf * (1.0 - sf)).astype(BF16)
            dqc = dq * HG_SCALE * (sq + qc * sq * (1.0 - sq))
            return df, dqc, dv, jnp.sum(dforget * (1.0 - sf), axis=0, keepdims=True)

        def grads(s, dlb):
            blocks = [s + w * GLA_GRAD_TRIPS for w in range(GLA_GRAD_WAYS)]
            rows = [_gla_block_rows(b) for b in blocks]
            loaded = []
            for b, r in zip(blocks, rows):
                chunks = [b * GLA_CPB + j for j in range(GLA_CPB)]
                loaded.append((q_ref[r, :], f_ref[r, :], v_ref[r, :], do_ref[r, :], [st_ref[c] for c in chunks],
                               [dst_scr[c] for c in chunks], [dec_scr[c] for c in chunks], b_ref[r, :],
                               dq_acc[r, :], dv_acc[r, :]))
            results = [block_grads(*t[:8]) + (t[8], t[9]) for t in loaded]
            for r, (df, dqc, dv, dlb_part, dq_prev, dv_prev) in zip(rows, results):
                df_out[r, :] = df
                dq_acc[r, :] = dq_prev + dqc
                dv_acc[r, :] = dv_prev + dv
                dlb = dlb + dlb_part
            return dlb

        @pl.when(jnp.logical_or(h > 0, d > 0))
        def _():
            df_copy.wait()

        dlb_ref[...] = lax.fori_loop(0, GLA_GRAD_TRIPS, grads, jnp.zeros((1, HEAD_DIM), F32))
        df_copy.start()

        @pl.when(d == 1)
        def _():
            @pl.when(h > 0)
            def _():
                dq_copy.wait()
                dv_copy.wait()

            dq_out[...] = dq_acc[...].astype(BF16)
            dv_out[...] = dv_acc[...].astype(BF16)
            dq_copy.start()
            dv_copy.start()

        @pl.when(last)
        def _():
            df_copy.wait()
            dq_copy.wait()
            dv_copy.wait()

    blk = SEQ * HEAD_DIM * 4
    state_bytes = nck * HEAD_DIM * HEAD_DIM * 4
    dproj, dlb = pl.pallas_call(
        body, name="gla_bwd", grid=(HG_HEADS, 2),
        in_specs=[_head_spec(CB_HG_Q),
                  pl.BlockSpec((SEQ, HEAD_DIM), lambda h, d: (0, CB_F + 8 * d + h)),
                  _head_spec(CB_HG_I),
                  pl.BlockSpec((None, None, 1, HEAD_DIM), lambda h, d: (d, h, 0, 0)),
                  _head_spec(0),
                  pl.BlockSpec((None, None, nck, HEAD_DIM, HEAD_DIM), lambda h, d: (h, d, 0, 0, 0)),
                  pl.BlockSpec((None, SEQ, HEAD_DIM), lambda h, d: (d, 0, h)),
                  pl.BlockSpec((None, None, nck, 8, HEAD_DIM), lambda h, d: (h, d, 0, 0, 0)),
                  pl.BlockSpec((None, SEQ, HEAD_DIM), lambda h, d: (d, 0, h)),
                  _any_spec()],
        out_specs=[_any_spec(), pl.BlockSpec((None, None, 1, HEAD_DIM), lambda h, d: (d, h, 0, 0))],
        out_shape=[jax.ShapeDtypeStruct((SEQ, IN_COLS), BF16), jax.ShapeDtypeStruct((2, HG_HEADS, 1, HEAD_DIM), F32)],
        scratch_shapes=[pltpu.VMEM((SEQ, HEAD_DIM), F32)] * 2
        + [pltpu.VMEM((nck, HEAD_DIM, HEAD_DIM), F32)] * 2
        + [pltpu.VMEM((SEQ, HEAD_DIM), BF16)] * 3 + [pltpu.SemaphoreType.DMA((3,))],
        input_output_aliases={9: 0},
        compiler_params=_params(("arbitrary", "arbitrary"), 10 * blk + 4 * state_bytes + 3 * blk + 3 * blk),
    )(proj, proj, proj, lb, do, states, qt_all, dec_all, b_all, dproj)
    return dproj, dlb


def _da_residue_rows(r, n0, size, d):
    if d == 1:
        return pl.ds(pl.multiple_of(n0, 8), size)
    return pl.ds(r + n0 * d, size, stride=d)


def _da_residue_ways(d, nqb):
    return min(d, 4) if nqb <= 2 else 1


def _da_rmsnorm(x, gain):
    r = lax.rsqrt(jnp.mean(x * x, axis=-1, keepdims=True) + RMS_EPS)
    return x * r, r, x * r * gain


def _da_scores(qn_scr, kn_scr, slope, i, ld):
    w0 = jnp.clip(i * DA_QB - DA_RADIUS, 0, ld - DA_WIN)
    w0 = pl.multiple_of(w0, DA_RADIUS)
    qrows = pl.ds(pl.multiple_of(i * DA_QB, DA_QB), DA_QB)
    win = pl.ds(w0, DA_WIN)
    qb = qn_scr[qrows, :]
    kw = kn_scr[win, :]
    s = _dot(qb, kw, _NT) * DA_SCALE
    qpos = i * DA_QB + lax.broadcasted_iota(jnp.int32, (DA_QB, DA_WIN), 0)
    kpos = w0 + lax.broadcasted_iota(jnp.int32, (DA_QB, DA_WIN), 1)
    arel = jnp.abs(kpos - qpos)
    s = s - slope * arel.astype(F32)
    s = jnp.where(arel <= DA_RADIUS, s, NEG_INF)
    return s, qb, kw, qrows, win


def _da_slopes(group):
    d = DA_DILATIONS[group]
    sl = _alibi_slopes(DA_HEADS)[4 * group:4 * group + 4] * d
    return jnp.asarray(np.broadcast_to(sl[:, None, None], (4, 1, LANE)).copy())


def _da_fwd(proj, gq, gk, group, others=()):
    d = DA_DILATIONS[group]
    ld = SEQ // d
    nqb = ld // DA_QB
    n_other = 2 * len(others)

    ways = min(DA_FWD_WAYS, nqb)
    rways = _da_residue_ways(d, nqb)

    def body(q_ref, k_ref, v_ref, gq_ref, gk_ref, sl_ref, *refs):
        other_refs, refs = refs[:n_other], refs[n_other:]
        if others:
            ob_ref, of_ref, lsej_ref, qn_scr, kn_scr, v_scr, o_ref, lse_ref = refs
        else:
            o_ref, lse_ref, qn_scr, kn_scr, v_scr = refs
        slope = sl_ref[:, 0:1]

        def residues(t, carry):
            rs = [t * rways + u for u in range(rways)]
            for u, r in enumerate(rs):
                sel = _da_residue_rows(r, 0, ld, d)
                qn_scr[u] = _da_rmsnorm(q_ref[sel, :], gq_ref[...])[2].astype(BF16)
                kn_scr[u] = _da_rmsnorm(k_ref[sel, :], gk_ref[...])[2].astype(BF16)
                v_scr[u] = v_ref[sel, :].astype(BF16)

            def block(u, i):
                s, _, _, _, win = _da_scores(qn_scr.at[u], kn_scr.at[u], slope, i, ld)
                m = jnp.max(s, axis=-1, keepdims=True)
                p = jnp.exp(s - m)
                l = jnp.sum(p, axis=-1, keepdims=True)
                return _dot(p, v_scr[u, win, :], _NN) / l, jnp.broadcast_to(m + jnp.log(l), (DA_QB, HEAD_DIM))

            def step(i, c2):
                todo = [(u, r, i + w * (nqb // ways)) for u, r in enumerate(rs) for w in range(ways)]
                for (u, r, b), (o, lse) in zip(todo, [block(u, b) for u, _, b in todo]):
                    out = _da_residue_rows(r, b * DA_QB, DA_QB, d)
                    o_ref[out, :] = o
                    lse_ref[out, :] = lse
                return c2

            return lax.fori_loop(0, nqb // ways, step, carry)

        lax.fori_loop(0, d // rways, residues, 0)

        if others:
            pieces = 8
            rows = SEQ // pieces

            def merge(t, carry):
                r = pl.ds(pl.multiple_of(t * rows, rows), rows)
                outs = [ref[r, :] for ref in other_refs[0::2]] + [o_ref[r, :]]
                lses = [ref[r, :] for ref in other_refs[1::2]] + [lse_ref[r, :]]
                m = lses[0]
                for l in lses[1:]:
                    m = jnp.maximum(m, l)
                es = [jnp.exp(l - m) for l in lses]
                tot = sum(es[1:], es[0])
                o = sum((e * v for e, v in zip(es[1:], outs[1:])), es[0] * outs[0]) / tot
                of_ref[r, :] = o
                ob_ref[r, :] = o.astype(BF16)
                lsej_ref[r, :] = m + jnp.log(tot)
                return carry

            lax.fori_loop(0, pieces, merge, 0)

    seq_spec = lambda base: pl.BlockSpec((SEQ, HEAD_DIM), lambda h: (0, base + 4 * group + h))
    out_spec = pl.BlockSpec((SEQ, HEAD_DIM), lambda h: (0, h))
    gain_spec = pl.BlockSpec((1, HEAD_DIM), lambda h: (0, 0))
    blk = SEQ * HEAD_DIM * 4
    fshape = jax.ShapeDtypeStruct((SEQ, DA_WIDTH), F32)
    scratch = [pltpu.VMEM((rways, ld, HEAD_DIM), BF16)] * 3
    if others:
        out_shape = [jax.ShapeDtypeStruct((SEQ, DA_WIDTH), BF16), fshape, fshape]
        scratch += [pltpu.VMEM((SEQ, HEAD_DIM), F32)] * 2
    else:
        out_shape = [fshape, fshape]
    return pl.pallas_call(
        body, name=f"da_fwd_g{group}", grid=(DA_HEADS_PER_GROUP,),
        in_specs=[seq_spec(CB_DA_Q), seq_spec(CB_DA_K), seq_spec(CB_DA_V), gain_spec, gain_spec,
                  pl.BlockSpec((None, 1, LANE), lambda h: (h, 0, 0))] + [out_spec] * n_other,
        out_specs=[out_spec] * len(out_shape), out_shape=out_shape, scratch_shapes=scratch,
        compiler_params=_params(("parallel",), 10 * blk + 2 * blk + (n_other + 3) * blk),
    )(proj, proj, proj, gq, gk, _da_slopes(group), *[a for pair in others for a in pair])


def _da_bwd(proj, gq, gk, do, lse, dd, dproj, group):
    d = DA_DILATIONS[group]
    ld = SEQ // d
    nqb = ld // DA_QB
    ways = min(DA_WAYS, nqb)
    rways = _da_residue_ways(d, nqb)

    def body(q_ref, k_ref, v_ref, gq_ref, gk_ref, sl_ref, do_ref, lse_ref, dd_ref, dproj_in,
             dproj_ref, dgq_ref, dgk_ref,
             qn_scr, kn_scr, v_scr, dqn_scr, dkn_scr, dvr_scr, rq_scr, rk_scr, dq_scr, dk_scr, dv_scr,
             dq_out, dk_out, dv_out, sems):
        del dproj_in
        head = pl.program_id(0)
        gqv, gkv = gq_ref[...], gk_ref[...]
        slope = sl_ref[:, 0:1]

        copies = []
        for k, (staging, base) in enumerate(((dq_out, CB_DA_Q), (dk_out, CB_DA_K), (dv_out, CB_DA_V))):
            cols = pl.ds(pl.multiple_of((base + 4 * group + head) * LANE, LANE), LANE)
            copies.append(pltpu.make_async_copy(staging, dproj_ref.at[:, cols], sems.at[k]))

        def residues(t, carry):
            rs = [t * rways + u for u in range(rways)]
            sels = [_da_residue_rows(r, 0, ld, d) for r in rs]
            for u, sel in enumerate(sels):
                for x_ref, gv, n_scr, r_scr in ((q_ref, gqv, qn_scr, rq_scr), (k_ref, gkv, kn_scr, rk_scr)):
                    _, rstd, normed = _da_rmsnorm(x_ref[sel, :], gv)
                    n_scr[u] = normed.astype(BF16)
                    r_scr[u] = jnp.broadcast_to(rstd, (ld, HEAD_DIM))
                v_scr[u] = v_ref[sel, :].astype(BF16)
            dkn_scr[...] = jnp.zeros_like(dkn_scr)
            dvr_scr[...] = jnp.zeros_like(dvr_scr)

            def block(u, r, i):
                s, qb, kw, qrows, win = _da_scores(qn_scr.at[u], kn_scr.at[u], slope, i, ld)
                src = _da_residue_rows(r, i * DA_QB, DA_QB, d)
                p = jnp.exp(s - lse_ref[src, :][:, 0:1])
                dob = do_ref[src, :]
                dp = _dot(dob, v_scr[u, win, :], _NT)
                ds = p * (dp - dd_ref[src, :][:, 0:1]) * DA_SCALE
                return u, qrows, win, _dot(p, dob, _TN), _dot(ds, kw, _NN), _dot(ds, qb, _TN)

            def step(i, c2):
                todo = [(u, r, i + w * (nqb // ways)) for u, r in enumerate(rs) for w in range(ways)]
                for u, qrows, win, dv, dqn, dkn in [block(*t) for t in todo]:
                    dvr_scr[u, win, :] += dv
                    dqn_scr[u, qrows, :] = dqn
                    dkn_scr[u, win, :] += dkn
                return c2

            lax.fori_loop(0, nqb // ways, step, 0)

            pq, pk = carry
            for u, sel in enumerate(sels):
                parts = []
                for x_ref, gv, dn_scr, dx_scr, r_scr in ((q_ref, gqv, dqn_scr, dq_scr, rq_scr),
                                                         (k_ref, gkv, dkn_scr, dk_scr, rk_scr)):
                    rstd = r_scr[u]
                    hat = x_ref[sel, :] * rstd
                    dn = dn_scr[u]
                    dyg = dn * gv
                    dx_scr[sel, :] = rstd * (dyg - hat * jnp.mean(dyg * hat, axis=-1, keepdims=True))
                    parts.append(jnp.sum(dn * hat, axis=0, keepdims=True))
                dv_scr[sel, :] = dvr_scr[u]
                pq, pk = pq + parts[0], pk + parts[1]
            return pq, pk

        zero = jnp.zeros((1, HEAD_DIM), F32)
        pq, pk = lax.fori_loop(0, d // rways, residues, (zero, zero))

        @pl.when(pl.program_id(0) == 0)
        def _():
            dgq_ref[...] = pq
            dgk_ref[...] = pk

        @pl.when(pl.program_id(0) > 0)
        def _():
            dgq_ref[...] += pq
            dgk_ref[...] += pk

        @pl.when(head > 0)
        def _():
            for cp in copies:
                cp.wait()

        dq_out[...] = dq_scr[...].astype(BF16)
        dk_out[...] = dk_scr[...].astype(BF16)
        dv_out[...] = dv_scr[...].astype(BF16)
        for cp in copies:
            cp.start()

        @pl.when(head == DA_HEADS_PER_GROUP - 1)
        def _():
            for cp in copies:
                cp.wait()

    seq_spec = lambda base: pl.BlockSpec((SEQ, HEAD_DIM), lambda h: (0, base + 4 * group + h))
    out_spec = pl.BlockSpec((SEQ, HEAD_DIM), lambda h: (0, h))
    gain_spec = pl.BlockSpec((1, HEAD_DIM), lambda h: (0, 0))
    gshape = jax.ShapeDtypeStruct((1, HEAD_DIM), F32)
    blk = SEQ * HEAD_DIM * 4
    return pl.pallas_call(
        body, name=f"da_bwd_g{group}", grid=(DA_HEADS_PER_GROUP,),
        in_specs=[seq_spec(CB_DA_Q), seq_spec(CB_DA_K), seq_spec(CB_DA_V), gain_spec, gain_spec,
                  pl.BlockSpec((None, 1, LANE), lambda h: (h, 0, 0)), out_spec, out_spec, out_spec, _any_spec()],
        out_specs=[_any_spec(), gain_spec, gain_spec],
        out_shape=[jax.ShapeDtypeStruct((SEQ, IN_COLS), BF16), gshape, gshape],
        scratch_shapes=[pltpu.VMEM((rways, ld, HEAD_DIM), BF16)] * 3 + [pltpu.VMEM((rways, ld, HEAD_DIM), F32)] * 5
        + [pltpu.VMEM((SEQ, HEAD_DIM), F32)] * 3 + [pltpu.VMEM((SEQ, HEAD_DIM), BF16)] * 3
        + [pltpu.SemaphoreType.DMA((3,))],
        input_output_aliases={9: 0},
        compiler_params=_params(("arbitrary",), 12 * blk + 3 * blk + 3 * blk + 3 * blk + 2 * blk),
    )(proj, proj, proj, gq, gk, _da_slopes(group), do, lse, dd, dproj)


MEM_WAYS = 4


def _mem_probs(q, kn, gq):
    qhat, rq, qn = _da_rmsnorm(q, gq)
    s = _dot(qn, kn, _NT) * MEM_SCALE
    m = jnp.max(s, axis=-1, keepdims=True)
    e = jnp.exp(s - m)
    p = e / jnp.sum(e, axis=-1, keepdims=True)
    return p, qhat, rq, qn


def _mem_pieces(tq):
    rows = tq // MEM_WAYS
    return [slice(w * rows, (w + 1) * rows) for w in range(MEM_WAYS)]


def _mem_fwd(proj, kv, gq, gk, tq):
    def body(q_ref, k_ref, v_ref, gq_ref, gk_ref, o_ref):
        kn = _da_rmsnorm(k_ref[...], gk_ref[...])[2]
        v, gqv = v_ref[...], gq_ref[...]
        pieces = _mem_pieces(tq)
        outs = [_dot(_mem_probs(q_ref[rs, :], kn, gqv)[0], v, _NN) for rs in pieces]
        for rs, o in zip(pieces, outs):
            o_ref[rs, :] = o.astype(BF16)

    gain_spec = pl.BlockSpec((1, HEAD_DIM), lambda h, i: (0, 0))
    return pl.pallas_call(
        body, name="mem_fwd", grid=(MEM_HEADS, SEQ // tq),
        in_specs=[pl.BlockSpec((tq, HEAD_DIM), lambda h, i: (i, CB_MEM_Q + h)),
                  pl.BlockSpec((N_MEM, HEAD_DIM), lambda h, i: (0, h)),
                  pl.BlockSpec((N_MEM, HEAD_DIM), lambda h, i: (0, MEM_HEADS + h)), gain_spec, gain_spec],
        out_specs=pl.BlockSpec((tq, HEAD_DIM), lambda h, i: (i, h)),
        out_shape=jax.ShapeDtypeStruct((SEQ, MEM_WIDTH), BF16),
        compiler_params=_params(("parallel", "parallel"), 16 * tq * N_MEM * 4),
    )(proj, kv, kv, gq, gk)


def _mem_bwd(proj, kv, gq, gk, do, dproj, tq):
    nq = SEQ // tq

    def body(q_ref, k_ref, v_ref, gq_ref, gk_ref, do_ref, dproj_in, dq_ref, dk_ref, dv_ref, dgq_ref, dgk_ref, dkn_scr):
        del dproj_in
        h, i = pl.program_id(0), pl.program_id(1)
        gqv, gkv = gq_ref[...], gk_ref[...]
        khat, rk, kn = _da_rmsnorm(k_ref[...], gkv)
        v = v_ref[...]

        def piece(rs):
            p, qhat, rq, qn = _mem_probs(q_ref[rs, :], kn, gqv)
            dob = do_ref[rs, :]
            dp = _dot(dob, v, _NT)
            ds = p * (dp - jnp.sum(p * dp, axis=-1, keepdims=True)) * MEM_SCALE
            dqn = _dot(ds, kn, _NN)
            dyg = dqn * gqv
            dq = (rq * (dyg - qhat * jnp.mean(dyg * qhat, axis=-1, keepdims=True))).astype(BF16)
            return dq, _dot(p, dob, _TN), _dot(ds, qn, _TN), jnp.sum(dqn * qhat, axis=0, keepdims=True)

        pieces = _mem_pieces(tq)
        results = [piece(rs) for rs in pieces]
        for rs, res in zip(pieces, results):
            dq_ref[rs, :] = res[0]
        dvp = sum((res[1] for res in results[1:]), results[0][1])
        dknp = sum((res[2] for res in results[1:]), results[0][2])
        dgq_part = sum((res[3] for res in results[1:]), results[0][3])
        first = jnp.logical_and(h == 0, i == 0)

        @pl.when(first)
        def _():
            dgq_ref[...] = dgq_part

        @pl.when(jnp.logical_not(first))
        def _():
            dgq_ref[...] += dgq_part

        @pl.when(i == 0)
        def _():
            dv_ref[...] = dvp
            dkn_scr[...] = dknp

        @pl.when(i > 0)
        def _():
            dv_ref[...] += dvp
            dkn_scr[...] += dknp

        @pl.when(i == nq - 1)
        def _():
            dkn = dkn_scr[...]
            dkg = dkn * gkv
            dk_ref[...] = rk * (dkg - khat * jnp.mean(dkg * khat, axis=-1, keepdims=True))
            dgk_part = jnp.sum(dkn * khat, axis=0, keepdims=True)

            @pl.when(h == 0)
            def _():
                dgk_ref[...] = dgk_part

            @pl.when(h > 0)
            def _():
                dgk_ref[...] += dgk_part

    gain_spec = pl.BlockSpec((1, HEAD_DIM), lambda h, i: (0, 0))
    kvout = pl.BlockSpec((N_MEM, HEAD_DIM), lambda h, i: (0, h))
    return pl.pallas_call(
        body, name="mem_bwd", grid=(MEM_HEADS, nq),
        in_specs=[pl.BlockSpec((tq, HEAD_DIM), lambda h, i: (i, CB_MEM_Q + h)),
                  pl.BlockSpec((N_MEM, HEAD_DIM), lambda h, i: (0, h)),
                  pl.BlockSpec((N_MEM, HEAD_DIM), lambda h, i: (0, MEM_HEADS + h)), gain_spec, gain_spec,
                  pl.BlockSpec((tq, HEAD_DIM), lambda h, i: (i, h)), _any_spec()],
        out_specs=[pl.BlockSpec((tq, HEAD_DIM), lambda h, i: (i, CB_MEM_Q + h)), kvout, kvout, gain_spec, gain_spec],
        out_shape=[jax.ShapeDtypeStruct((SEQ, IN_COLS), BF16), jax.ShapeDtypeStruct((N_MEM, MEM_WIDTH), F32),
                   jax.ShapeDtypeStruct((N_MEM, MEM_WIDTH), F32), jax.ShapeDtypeStruct((1, HEAD_DIM), F32),
                   jax.ShapeDtypeStruct((1, HEAD_DIM), F32)],
        scratch_shapes=[pltpu.VMEM((N_MEM, HEAD_DIM), F32)], input_output_aliases={6: 0},
        compiler_params=_params(("arbitrary", "arbitrary"), 24 * tq * N_MEM * 4),
    )(proj, kv, kv, gq, gk, do, dproj)


def _mesh_position():
    return lax.axis_index("x"), lax.axis_index("y"), lax.axis_index("c")


def _any_spec():
    return pl.BlockSpec(memory_space=pl.ANY)


def _all_gather(shards):
    n = len(shards)

    def body(*refs):
        ins, outs = refs[:n], refs[n:2 * n]
        send_sems, recv_sems, local_sems = refs[2 * n:]
        x, y, c = _mesh_position()
        me, sibling = (x, y, c), (x, y, 1 - c)
        first = (jnp.where(c == 0, 1 - x, x), jnp.where(c == 0, y, 1 - y), c)
        second = (jnp.where(c == 0, x, 1 - x), jnp.where(c == 0, 1 - y, y), c)
        diagonal = (1 - x, 1 - y, c)

        def copy(w, k, block, to, src=None):
            px, py, pc = block
            rows = outs[w].at[4 * px + 2 * py + pc]
            return pltpu.make_async_remote_copy(
                src_ref=rows if src is None else src, dst_ref=rows,
                send_sem=send_sems.at[7 * w + k], recv_sem=recv_sems.at[7 * w + k],
                device_id=to, device_id_type=MESH)

        started = []
        for w in range(n):
            mine = pltpu.make_async_copy(ins[w], outs[w].at[4 * x + 2 * y + c], local_sems.at[w])
            mine.start()
            started.append(mine)
        sends = []
        for w in range(n):
            own = [copy(w, 0, me, sibling, src=ins[w]), copy(w, 1, me, first, src=ins[w]),
                   copy(w, 2, me, second, src=ins[w])]
            for cp in own:
                cp.start()
            sends += own
        for w in range(n):
            copy(w, 1, first, me).wait_recv()
            follow = [copy(w, 3, first, second), copy(w, 4, first, sibling)]
            for cp in follow:
                cp.start()
            copy(w, 2, second, me).wait_recv()
            follow.append(copy(w, 5, second, sibling))
            follow[-1].start()
            sends += follow
        for w in range(n):
            copy(w, 3, diagonal, me).wait_recv()
            passed = copy(w, 6, diagonal, sibling)
            passed.start()
            sends.append(passed)
        for w in range(n):
            for k in (0, 4, 5, 6):
                copy(w, k, sibling, me).wait_recv()
        for cp in sends:
            cp.wait_send()
        for mine in started:
            mine.wait()

    return pl.pallas_call(
        body, name="weights_all_gather",
        in_specs=[_any_spec()] * n, out_specs=[_any_spec()] * n,
        out_shape=[jax.ShapeDtypeStruct((N_DEV,) + s.shape, s.dtype) for s in shards],
        scratch_shapes=[pltpu.SemaphoreType.DMA((7 * n,)), pltpu.SemaphoreType.DMA((7 * n,)),
                        pltpu.SemaphoreType.DMA((n,))],
    )(*shards)


def _chip_of(j, x, y):
    return (1 - x if j & 1 else x, 1 - y if j & 2 else y)


_HBM_SPEC = pl.BlockSpec(memory_space=pltpu.HBM)
_SEM_SPEC = pl.BlockSpec(memory_space=pltpu.SEMAPHORE)
_DATAFLOW_EFFECT = pltpu.SideEffectType.DATAFLOW_SIDE_EFFECTING
TOKEN_SHAPE = (8, D_MODEL)


def _copies_start(name, arrays, n_copies, plan):
    n = len(arrays)

    def body(*refs):
        send_sems, recv_sems, token = refs[n], refs[n + 1], refs[2 * n + 2]
        copies = plan(refs[:n])
        assert len(copies) == n_copies
        for k, (src, dst, dev) in enumerate(copies):
            pltpu.make_async_remote_copy(src_ref=src, dst_ref=dst, send_sem=send_sems.at[k], recv_sem=recv_sems.at[k],
                                         device_id=dev, device_id_type=MESH).start()
        token[...] = jnp.zeros_like(token)

    outs = pl.pallas_call(
        body, name=name,
        out_shape=(pltpu.SemaphoreType.DMA((n_copies,)), pltpu.SemaphoreType.DMA((n_copies,)),
                   *[pltpu.HBM(a.shape, a.dtype) for a in arrays], jax.ShapeDtypeStruct(TOKEN_SHAPE, F32)),
        in_specs=[_HBM_SPEC] * n,
        out_specs=(_SEM_SPEC, _SEM_SPEC, *[_HBM_SPEC] * n, pl.BlockSpec(memory_space=pltpu.VMEM)),
        input_output_aliases={i: i + 2 for i in range(n)},
        compiler_params=pltpu.CompilerParams(has_side_effects=_DATAFLOW_EFFECT),
    )(*[pltpu.with_memory_space_constraint(a, pltpu.HBM) for a in arrays])
    return outs[0], outs[1], list(outs[2:2 + n]), outs[2 + n]


def _copies_wait(name, send_sems, recv_sems, arrays, n_copies, plan, after):
    n = len(arrays)
    after = list(after) if isinstance(after, (list, tuple)) else [after]

    def body(*refs):
        send_ref, recv_ref = refs[n], refs[n + 1]
        copies = plan(refs[:n])
        assert len(copies) == n_copies
        for k, (src, dst, dev) in enumerate(copies):
            cp = pltpu.make_async_remote_copy(src_ref=src, dst_ref=dst, send_sem=send_ref.at[k], recv_sem=recv_ref.at[k],
                                              device_id=dev, device_id_type=MESH)
            cp.wait_send()
            cp.wait_recv()

    outs = pl.pallas_call(
        body, name=name, out_shape=tuple(pltpu.HBM(a.shape, a.dtype) for a in arrays),
        in_specs=[_HBM_SPEC] * n + [_SEM_SPEC, _SEM_SPEC] + [pl.BlockSpec(memory_space=pl.ANY)] * len(after),
        out_specs=tuple([_HBM_SPEC] * n), input_output_aliases={i: i for i in range(n)},
        compiler_params=pltpu.CompilerParams(has_side_effects=_DATAFLOW_EFFECT),
    )(*arrays, send_sems, recv_sems, *after)
    return list(outs)


def _after(small, token):
    return small if token is None else small + token[0:1, :small.shape[-1]]


def _gather_plan_out(n):
    def plan(refs):
        x, y, c = _mesh_position()
        me = 4 * x + 2 * y + c
        copies = []
        for w in range(n):
            land = refs[n + w].at[me]
            copies.append((refs[w], land, (x, y, 1 - c)))
            for j in range(1, 4):
                copies.append((refs[w], land, (*_chip_of(j, x, y), c)))
        return copies
    return plan


def _gather_plan_pass(n):
    def plan(refs):
        x, y, c = _mesh_position()
        copies = []
        for w in range(n):
            for j in range(1, 4):
                px, py = _chip_of(j, x, y)
                rows = refs[w].at[4 * px + 2 * py + c]
                copies.append((rows, rows, (x, y, 1 - c)))
        return copies
    return plan


def _reduce_plan_sibling(n):
    def plan(refs):
        x, y, c = _mesh_position()
        copies = []
        for w in range(n):
            for j in range(4):
                px, py = _chip_of(j, x, y)
                copies.append((refs[w].at[4 * px + 2 * py + (1 - c)], refs[n + w].at[j], (x, y, 1 - c)))
        return copies
    return plan


def _reduce_plan_chips(n):
    def plan(refs):
        x, y, c = _mesh_position()
        copies = []
        for w in range(n):
            for j in range(1, 4):
                copies.append((refs[w].at[j - 1], refs[n + w].at[j - 1], (*_chip_of(j, x, y), c)))
        return copies
    return plan


def _chip_partials(grad, recv, name, tr):
    _, rows, width = grad.shape

    def body(g_ref, r_ref, own_ref, other_ref):
        x, y, c = _mesh_position()
        for j in range(4):
            px, py = _chip_of(j, x, y)
            total = g_ref[4 * px + 2 * py + c].astype(F32) + r_ref[j].astype(F32)
            if j == 0:
                own_ref[...] = total
            else:
                other_ref[j - 1] = total.astype(BF16)

    return pl.pallas_call(
        body, name=name, grid=(rows // tr,),
        in_specs=[pl.BlockSpec((N_DEV, tr, width), lambda i: (0, i, 0)),
                  pl.BlockSpec((4, tr, width), lambda i: (0, i, 0))],
        out_specs=[pl.BlockSpec((tr, width), lambda i: (i, 0)), pl.BlockSpec((3, tr, width), lambda i: (0, i, 0))],
        out_shape=[jax.ShapeDtypeStruct((rows, width), F32), jax.ShapeDtypeStruct((3, rows, width), BF16)],
        compiler_params=_params(("parallel",), 2 * 20 * tr * width * 2 + 8 * tr * width * 4),
    )(grad, recv)


def _adamw_math(w, g, m, v):
    m = ADAM_B1 * m + (1.0 - ADAM_B1) * g
    v = ADAM_B2 * v + (1.0 - ADAM_B2) * (g * g)
    m_hat = m / (1.0 - ADAM_B1 ** ADAM_STEP)
    v_hat = v / (1.0 - ADAM_B2 ** ADAM_STEP)
    delta = -ADAM_LR * (m_hat / (jnp.sqrt(v_hat) + ADAM_EPS) + ADAM_WD * w)
    return delta, m, v


def _adamw_shard(own, recv, w, m, v, name, tr):
    rows, width = own.shape

    def body(own_ref, r_ref, w_ref, m_ref, v_ref, g_ref, d_ref, nm_ref, nv_ref):
        g = own_ref[...]
        for j in range(3):
            g = g + r_ref[j].astype(F32)
        g_ref[...] = g
        d_ref[...], nm_ref[...], nv_ref[...] = _adamw_math(w_ref[...], g, m_ref[...], v_ref[...])

    spec = pl.BlockSpec((tr, width), lambda i: (i, 0))
    shape = jax.ShapeDtypeStruct((rows, width), F32)
    return pl.pallas_call(
        body, name=name, grid=(rows // tr,),
        in_specs=[spec, pl.BlockSpec((3, tr, width), lambda i: (0, i, 0)), spec, spec, spec],
        out_specs=[spec] * 4, out_shape=[shape] * 4,
        compiler_params=_params(("parallel",), 22 * tr * width * 4),
    )(own, recv, w, m, v)


def _small_all_reduce_adamw(gpart, lbpack, wpack, mpack, vpack, after):
    def body(gp_ref, lb_ref, w_ref, m_ref, v_ref, after_ref, g_ref, d_ref, nm_ref, nv_ref, gath_ref, send_sems, recv_sems):
        del after_ref
        x, y, c = _mesh_position()
        me = 4 * x + 2 * y + c
        gath_ref[me] = gp_ref[...]
        copies = []
        for j in range(1, N_DEV):
            peer = (x ^ (j >> 2), y ^ ((j >> 1) & 1), c ^ (j & 1))
            cp = pltpu.make_async_remote_copy(
                src_ref=gp_ref, dst_ref=gath_ref.at[me], send_sem=send_sems.at[j - 1], recv_sem=recv_sems.at[j - 1],
                device_id=peer, device_id_type=MESH)
            cp.start()
            copies.append(cp)
        for cp in copies:
            cp.wait()
        tot = gath_ref[0]
        for s in range(1, N_DEV):
            tot = tot + gath_ref[s]
        lb = lb_ref[...]
        dl = tot[16:32, :] * lb * (1.0 - lb)
        g = jnp.concatenate([tot[0:16, :], dl[0:8, :], -dl[0:8, :], dl[8:16, :], -dl[8:16, :],
                             tot[32:SMALL_GRAD_ROWS, :]], axis=0)
        g_ref[...] = g
        d_ref[...], nm_ref[...], nv_ref[...] = _adamw_math(w_ref[...], g, m_ref[...], v_ref[...])

    vm = pl.BlockSpec(memory_space=pltpu.VMEM)
    shape = jax.ShapeDtypeStruct((SMALL_ROWS, LANE), F32)
    return pl.pallas_call(
        body, name="small_all_reduce_adamw", in_specs=[vm] * 5 + [_any_spec()], out_specs=[vm] * 4,
        out_shape=[shape] * 4,
        scratch_shapes=[pltpu.VMEM((N_DEV, SMALL_GRAD_ROWS, LANE), F32),
                        pltpu.SemaphoreType.DMA((N_DEV - 1,)), pltpu.SemaphoreType.DMA((N_DEV - 1,))],
    )(gpart, lbpack, wpack, mpack, vpack, after)


_SMALL_NAMES = ("norm_mix_gain", "norm_mem_gain", "lb_logits_fw", "lb_logits_bw", "norm_ffn_gain",
                "hg_norm_gain", "da_q_gain", "da_k_gain", "mem_q_gain", "mem_k_gain")
_SMALL_ROW0 = {"norm_mix_gain": 0, "norm_mem_gain": 8, "lb_logits_fw": 16, "lb_logits_bw": 32, "norm_ffn_gain": 48,
               "hg_norm_gain": 56, "da_q_gain": 64, "da_k_gain": 72, "mem_q_gain": 80, "mem_k_gain": 88}
_LOSS_ROW = 96


def _pack_rows(parts, total_rows):
    rows = []
    for p in parts:
        r = p.reshape(-1, LANE)
        rows.append(jnp.pad(r, ((0, -r.shape[0] % 8), (0, 0))))
    used = sum(r.shape[0] for r in rows)
    if total_rows > used:
        rows.append(jnp.zeros((total_rows - used, LANE), F32))
    return jnp.concatenate(rows, axis=0)


def _unpack_small(pack, like):
    out = {}
    for name in _SMALL_NAMES:
        n = like[name].size // LANE
        r0 = _SMALL_ROW0[name]
        out[name] = pack[r0:r0 + n].reshape(like[name].shape)
    return out


def kernel(x, mem, norm_mix_gain, norm_mem_gain, w_in, lb_logits_fw, lb_logits_bw, hg_norm_gain, da_q_gain, da_k_gain, w_mem_kv, mem_q_gain, mem_k_gain, w_proj_hg, w_proj_da, w_proj_mem, w_out, norm_ffn_gain, w_ffn_in, w_ffn_out, loss_target, m_norm_mix_gain, m_norm_mem_gain, m_w_in, m_lb_logits_fw, m_lb_logits_bw, m_hg_norm_gain, m_da_q_gain, m_da_k_gain, m_w_mem_kv, m_mem_q_gain, m_mem_k_gain, m_w_proj_hg, m_w_proj_da, m_w_proj_mem, m_w_out, m_norm_ffn_gain, m_w_ffn_in, m_w_ffn_out, v_norm_mix_gain, v_norm_mem_gain, v_w_in, v_lb_logits_fw, v_lb_logits_bw, v_hg_norm_gain, v_da_q_gain, v_da_k_gain, v_w_mem_kv, v_mem_q_gain, v_mem_k_gain, v_w_proj_hg, v_w_proj_da, v_w_proj_mem, v_w_out, v_norm_ffn_gain, v_w_ffn_in, v_w_ffn_out):
    small_w = dict(norm_mix_gain=norm_mix_gain, norm_mem_gain=norm_mem_gain, lb_logits_fw=lb_logits_fw,
                   lb_logits_bw=lb_logits_bw, norm_ffn_gain=norm_ffn_gain, hg_norm_gain=hg_norm_gain,
                   da_q_gain=da_q_gain, da_k_gain=da_k_gain, mem_q_gain=mem_q_gain, mem_k_gain=mem_k_gain)
    small_m = dict(norm_mix_gain=m_norm_mix_gain, norm_mem_gain=m_norm_mem_gain, lb_logits_fw=m_lb_logits_fw,
                   lb_logits_bw=m_lb_logits_bw, norm_ffn_gain=m_norm_ffn_gain, hg_norm_gain=m_hg_norm_gain,
                   da_q_gain=m_da_q_gain, da_k_gain=m_da_k_gain, mem_q_gain=m_mem_q_gain, mem_k_gain=m_mem_k_gain)
    small_v = dict(norm_mix_gain=v_norm_mix_gain, norm_mem_gain=v_norm_mem_gain, lb_logits_fw=v_lb_logits_fw,
                   lb_logits_bw=v_lb_logits_bw, norm_ffn_gain=v_norm_ffn_gain, hg_norm_gain=v_hg_norm_gain,
                   da_q_gain=v_da_q_gain, da_k_gain=v_da_k_gain, mem_q_gain=v_mem_q_gain, mem_k_gain=v_mem_k_gain)
    big_w = dict(w_in=w_in[0], w_mem_kv=w_mem_kv[0], w_proj_hg=w_proj_hg[0], w_proj_da=w_proj_da[0],
                 w_proj_mem=w_proj_mem[0], w_out=w_out[0], w_ffn_in=w_ffn_in[0], w_ffn_out=w_ffn_out[0])
    big_m = dict(w_in=m_w_in[0], w_mem_kv=m_w_mem_kv[0], w_proj_hg=m_w_proj_hg[0], w_proj_da=m_w_proj_da[0],
                 w_proj_mem=m_w_proj_mem[0], w_out=m_w_out[0], w_ffn_in=m_w_ffn_in[0], w_ffn_out=m_w_ffn_out[0])
    big_v = dict(w_in=v_w_in[0], w_mem_kv=v_w_mem_kv[0], w_proj_hg=v_w_proj_hg[0], w_proj_da=v_w_proj_da[0],
                 w_proj_mem=v_w_proj_mem[0], w_out=v_w_out[0], w_ffn_in=v_w_ffn_in[0], w_ffn_out=v_w_ffn_out[0])

    row_tile = dict(w_in=128, w_mem_kv=128, w_proj_hg=128, w_proj_da=512, w_proj_mem=512, w_out=128,
                    w_ffn_in=128, w_ffn_out=352)
    rest = _BIG_NAMES[1:]
    shards = [big_w[n].astype(BF16) for n in rest]
    state = {}
    big_out = {}

    def reduce_start(group, stacked):
        names = tuple(stacked)
        arrays = [stacked[n] for n in names] + [lax.empty((4,) + stacked[n].shape[1:], BF16) for n in names]
        plan = _reduce_plan_sibling(len(names))
        send, recv, thru, token = _copies_start(f"grads_{group}_sibling_start", arrays, 4 * len(names), plan)
        state[group] = dict(names=names, plan=plan, send=send, recv=recv, arrays=thru)
        return token

    def reduce_middle(group, after):
        st = state[group]
        names, k = st["names"], len(st["names"])
        thru = _copies_wait(f"grads_{group}_sibling_wait", st["send"], st["recv"], st["arrays"], 4 * k, st["plan"], after)
        partials = [_chip_partials(thru[i], thru[k + i], f"chip_partials_{n}", row_tile[n]) for i, n in enumerate(names)]
        arrays = [p[1] for p in partials] + [lax.empty(p[1].shape, BF16) for p in partials]
        plan = _reduce_plan_chips(k)
        send, recv, thru2, token = _copies_start(f"grads_{group}_chips_start", arrays, 3 * k, plan)
        state[group] = dict(names=names, plan=plan, send=send, recv=recv, arrays=thru2, own=[p[0] for p in partials])
        return token

    def reduce_finish(group, after):
        st = state.pop(group)
        names, k = st["names"], len(st["names"])
        thru = _copies_wait(f"grads_{group}_chips_wait", st["send"], st["recv"], st["arrays"], 3 * k, st["plan"], after)
        for i, n in enumerate(names):
            big_out[n] = _adamw_shard(st["own"][i], thru[k + i], big_w[n], big_m[n], big_v[n], "adamw_" + n, row_tile[n])

    step = _local_step_stages(x[0], mem[0], loss_target[0], small_w)
    event, payload = next(step)
    local = None
    while True:
        reply = None
        if event == "gather_w_in":
            reply = _all_gather([big_w["w_in"].astype(BF16)])[0]
        elif event == "begin":
            nr = len(rest)
            me = 4 * lax.axis_index("x") + 2 * lax.axis_index("y") + lax.axis_index("c")
            arrays = shards + [lax.dynamic_update_slice(lax.empty((N_DEV,) + s.shape, BF16), s[None], (me, 0, 0))
                               for s in shards]
            plan = _gather_plan_out(nr)
            send, recv, thru, reply = _copies_start("weights_rest_out_start", arrays, 4 * nr, plan)
            state["gather"] = dict(plan=plan, send=send, recv=recv, arrays=thru)
        elif event == "after_gla_fwd":
            st = state["gather"]
            nr = len(rest)
            thru = _copies_wait("weights_rest_out_wait", st["send"], st["recv"], st["arrays"], 4 * nr, st["plan"], payload)
            plan = _gather_plan_pass(nr)
            send, recv, lands, reply = _copies_start("weights_rest_pass_start", thru[nr:], 3 * nr, plan)
            state["gather"] = dict(plan=plan, send=send, recv=recv, arrays=lands)
        elif event == "need_weights":
            st = state.pop("gather")
            nr = len(rest)
            lands = _copies_wait("weights_rest_pass_wait", st["send"], st["recv"], st["arrays"], 3 * nr, st["plan"], payload)
            reply = dict(zip(rest, lands))
        elif event == "grads_ffn":
            reply = reduce_start("ffn", payload)
        elif event == "after_gate_merge_bwd":
            reply = reduce_middle("ffn", payload)
        elif event == "grads_mix":
            reply = reduce_start("mix", payload)
        elif event == "after_da_bwd_g0":
            reply = reduce_middle("mix", payload)
            reduce_finish("ffn", payload)
        elif event == "after_gla_bwd":
            reduce_finish("mix", payload)
        elif event == "grads_in":
            reply = reduce_start("in", payload)
        elif event == "after_proj_bwd_act_top":
            reply = reduce_middle("in", payload)
        elif event == "end":
            local = payload
            reduce_finish("in", [local["grad_x"]] + [big_out[n][0] for n in rest])
            break
        event, payload = step.send(reply)
    grad_x = local["grad_x"]

    wpack = _pack_rows([small_w[n] for n in _SMALL_NAMES], SMALL_ROWS)
    mpack = _pack_rows([small_m[n] for n in _SMALL_NAMES], SMALL_ROWS)
    vpack = _pack_rows([small_v[n] for n in _SMALL_NAMES], SMALL_ROWS)
    gs, ds, ms, vs = _small_all_reduce_adamw(local["gpart"], local["lbpack"], wpack, mpack, vpack, big_out["w_in"][0])
    loss = gs[_LOSS_ROW, 0]
    small_out = [_unpack_small(t, small_w) for t in (gs, ds, ms, vs)]

    order = ("norm_mix_gain", "norm_mem_gain", "w_in", "lb_logits_fw", "lb_logits_bw", "hg_norm_gain", "da_q_gain",
             "da_k_gain", "w_mem_kv", "mem_q_gain", "mem_k_gain", "w_proj_hg", "w_proj_da", "w_proj_mem", "w_out",
             "norm_ffn_gain", "w_ffn_in", "w_ffn_out")
    outs = [loss, grad_x[None]]
    for kind in range(4):
        for n in order:
            outs.append(big_out[n][kind][None] if n in big_out else small_out[kind][n])
    return tuple(outs)


_BIG_NAMES = ("w_in", "w_mem_kv", "w_proj_hg", "w_proj_da", "w_proj_mem", "w_out", "w_ffn_in", "w_ffn_out")


def _local_step(xs, mems, target, sw, wg):
    step = _local_step_stages(xs, mems, target, sw)
    stacked = {}
    event, payload = next(step)
    while event != "end":
        reply = None
        if event.startswith("grads_"):
            stacked.update(payload)
        elif event == "need_weights":
            reply = wg
        elif event == "gather_w_in":
            reply = wg["w_in"]
        event, payload = step.send(reply)
    return dict(payload, stacked=stacked)


def _local_step_stages(xs, mems, target, sw):
    norm_mix_gain, norm_mem_gain, norm_ffn_gain = sw["norm_mix_gain"], sw["norm_mem_gain"], sw["norm_ffn_gain"]
    lb_logits_fw, lb_logits_bw, hg_norm_gain = sw["lb_logits_fw"], sw["lb_logits_bw"], sw["hg_norm_gain"]
    da_q_gain, da_k_gain, mem_q_gain, mem_k_gain = sw["da_q_gain"], sw["da_k_gain"], sw["mem_q_gain"], sw["mem_k_gain"]

    win_st = yield "gather_w_in", None
    token = yield "begin", None
    lb_fw = _lb_table(lb_logits_fw, "lb_table_fw")
    lb_bw = _lb_table(lb_logits_bw, "lb_table_bw")
    lb = jnp.concatenate([lb_fw, lb_bw], axis=0).reshape(2, HG_HEADS, 1, HEAD_DIM)
    h, h_t = _rmsnorm_fwd(xs, _after(norm_mix_gain, token), "norm_mix_fwd", 512, transposed=True)
    proj = _matmul(h, win_st, "nn", F32, 1024, IN_SHARD, D_MODEL, "proj_fwd", b_stacked=True, n_outer=True)
    hg_b = _gla_decay(proj, lb)
    o_hg, o_pre, states, hg_qt, hg_decay = _gla_fwd(proj, lb, hg_norm_gain, hg_b)
    token = yield "after_gla_fwd", o_hg
    da01 = [_da_fwd(proj, _after(da_q_gain, token), da_k_gain, g) for g in (0, 1)]
    o_da, o_da32, lse_da = _da_fwd(proj, _after(da_q_gain, token), da_k_gain, 2, others=da01)
    wg = yield "need_weights", o_da
    wkv = wg["w_mem_kv"].reshape(D_MODEL, 2 * MEM_WIDTH)
    wphg = wg["w_proj_hg"].reshape(D_MODEL, D_MODEL)
    wpda = jnp.transpose(wg["w_proj_da"], (1, 0, 2)).reshape(DA_WIDTH, D_MODEL)
    wpmem = jnp.transpose(wg["w_proj_mem"], (1, 0, 2)).reshape(MEM_WIDTH, D_MODEL)
    wout = wg["w_out"].reshape(D_MODEL, D_MODEL)
    wfin = jnp.transpose(wg["w_ffn_in"], (1, 0, 2)).reshape(D_MODEL, 2 * D_FF)
    wfout = wg["w_ffn_out"].reshape(D_FF, D_MODEL)
    mem_n = _rmsnorm_fwd(mems, norm_mem_gain, "norm_mem_fwd", N_MEM)
    kv = _matmul(mem_n, wkv, "nn", F32, N_MEM, 1024, D_MODEL, "mem_kv_fwd")
    o_mem = _mem_fwd(proj, kv, mem_q_gain, mem_k_gain, 1024)
    branch_w = (wphg, wpda, wpmem)
    merged, t_hg, t_da, t_mem = _branch_merge_fwd(proj, (o_hg, o_da, o_mem), branch_w, 512)
    x1, h2, h2_t = _residual_rmsnorm_fwd(xs, merged, wout, norm_ffn_gain, "out_norm_ffn_fwd", 512)
    ffn_a, ffn_b, act, act_t = _ffn_in_swiglu(h2, wfin, 1024, 1408)
    dy, dyb, loss_part = _ffn_out_loss_head(x1, act, wfout, target, 512)

    dab = _ffn_out_bwd_swiglu(dyb, wfout, ffn_a, ffn_b, 1024, 1408)
    g_wfout = _matmul(act_t, dyb, "nn", BF16, 1408, 1024, 2048, "ffn_out_bwd_w")
    g_wfin = _matmul(h2_t, dab, "nn", BF16, 1024, 1408, 2048, "ffn_in_bwd_w", b_parts=True)
    token = yield "grads_ffn", dict(
        w_ffn_in=jnp.transpose(g_wfin.reshape(D_MODEL, N_DEV, 2 * D_FF // N_DEV), (1, 0, 2)),
        w_ffn_out=g_wfout.reshape(N_DEV, D_FF // N_DEV, D_MODEL))
    dx1, dx1b, g_norm_ffn = _matmul_rmsnorm_bwd(dab, wfin, x1, dy, _after(norm_ffn_gain, token),
                                                "ffn_in_bwd_act_norm", 512, D_FF, a_parts=True)
    g_wout = _matmul(merged, dx1b, "tn", BF16, 1024, 1024, 2048, "out_bwd_w")
    dt_hg, dt_da, dt_mem, do_pre, do_da, do_mem, dd_da, g_hg_norm, dproj = _gate_merge_bwd(
        proj, (t_hg, t_da, t_mem), branch_w, dx1b, wout, o_pre, hg_norm_gain, o_da32, _dproj_buffer(), 256)
    token = yield "after_gate_merge_bwd", dt_hg
    g_wphg = _matmul(o_hg, dt_hg, "tn", BF16, 1024, 1024, 2048, "proj_hg_bwd_w", after=token)
    g_wpda = _matmul(o_da, dt_da, "tn", BF16, DA_WIDTH, D_MODEL, 2048, "proj_da_bwd_w")
    g_wpmem = _matmul(o_mem, dt_mem, "tn", BF16, MEM_WIDTH, D_MODEL, 2048, "proj_mem_bwd_w")
    by_owner = lambda g: jnp.transpose(g.reshape(g.shape[0], N_DEV, D_MODEL // N_DEV), (1, 0, 2))
    g_wpda, g_wpmem = by_owner(g_wpda), by_owner(g_wpmem)

    dproj, dk_mem, dv_mem, g_mem_q, g_mem_k = _mem_bwd(proj, kv, mem_q_gain, mem_k_gain, do_mem, dproj, 1024)
    dkv = jnp.concatenate([dk_mem, dv_mem], axis=1).astype(BF16)
    g_wkv = _matmul(mem_n, dkv, "tn", BF16, 1024, 1024, N_MEM, "mem_kv_bwd_w")
    dmem_n = _matmul(dkv, wkv, "nt", F32, N_MEM, 1024, 1024, "mem_kv_bwd_act")
    g_norm_mem = _gain_grad(mems, dmem_n, "norm_mem_bwd")
    token = yield "grads_mix", dict(
        w_mem_kv=g_wkv.reshape(N_DEV, D_MODEL // N_DEV, 2 * MEM_WIDTH),
        w_proj_hg=g_wphg.reshape(N_DEV, D_MODEL // N_DEV, D_MODEL),
        w_proj_da=g_wpda, w_proj_mem=g_wpmem,
        w_out=g_wout.reshape(N_DEV, D_MODEL // N_DEV, D_MODEL))

    dproj, g_da_q, g_da_k = _da_bwd(proj, _after(da_q_gain, token), da_k_gain, do_da, lse_da, dd_da, dproj, 0)
    token = yield "after_da_bwd_g0", g_da_q
    for g in (1, 2):
        dproj, gq_part, gk_part = _da_bwd(proj, _after(da_q_gain, token), da_k_gain, do_da, lse_da, dd_da, dproj, g)
        g_da_q, g_da_k = g_da_q + gq_part, g_da_k + gk_part

    dproj, dlb = _gla_bwd(proj, lb, states, hg_qt, hg_decay, hg_b, do_pre, dproj)
    yield "after_gla_bwd", dlb

    g_win =_matmul(h_t, dproj, "nn", BF16, 1024, IN_SHARD, 2048, "proj_bwd_w", out_stacked=True)
    token = yield "grads_in", dict(w_in=g_win)
    grad_x, g_mix_top = _matmul_rmsnorm_bwd(dproj, win_st, xs, dx1, norm_mix_gain, "proj_bwd_act_norm_top", 1024,
                                            IN_SHARD, b_stacked=True, after=token, m_blocks=(0, 1), with_bf16=False)
    token = yield "after_proj_bwd_act_top", g_mix_top
    grad_x, g_mix_bottom = _matmul_rmsnorm_bwd(dproj, win_st, xs, dx1, norm_mix_gain, "proj_bwd_act_norm_bottom", 1024,
                                               IN_SHARD, b_stacked=True, after=token, m_blocks=(1, 3),
                                               out_into=grad_x, with_bf16=False)
    g_norm_mix = g_mix_top + g_mix_bottom

    gpart = _pack_rows([g_norm_mix, g_norm_mem, dlb[0], dlb[1], g_norm_ffn, g_hg_norm, g_da_q, g_da_k,
                        g_mem_q, g_mem_k, loss_part], SMALL_GRAD_ROWS)
    lbpack = jnp.concatenate([lb_fw.reshape(8, LANE), lb_bw.reshape(8, LANE)], axis=0)
    yield "end", dict(grad_x=grad_x, gpart=gpart, lbpack=lbpack)
```

```python
import numpy as np
import jax
import jax.numpy as jnp
from jax import lax
from jax.experimental import pallas as pl
from jax.experimental.pallas import tpu as pltpu

F32 = jnp.float32
BF16 = jnp.bfloat16
MESH = pl.DeviceIdType.MESH

SEQ = 4096
D_MODEL = 1024
N_DEV = 8
N_MEM = 256
RMS_EPS = 1e-6
NEG_INF = -1e30
LANE = 128
HEAD_DIM = 128
HG_HEADS = 8
HG_CHUNK = 64
HG_SCALE = HEAD_DIM ** -0.5
DA_DILATIONS = (1, 4, 16)
DA_RADIUS = 64
DA_HEADS_PER_GROUP = 4
DA_HEADS = 12
DA_WIDTH = 512
DA_SCALE = HEAD_DIM ** -0.5
DA_QB = 128
DA_WIN = 256
DA_WAYS = 8
DA_FWD_WAYS = 8
MEM_HEADS = 4
MEM_WIDTH = 512
MEM_SCALE = HEAD_DIM ** -0.5
D_FF = 2816
IN_COLS = 13312
IN_SHARD = IN_COLS // N_DEV
CB_HG_Q, CB_F, CB_HG_I, CB_HG_G = 0, 8, 24, 32
CB_DA_Q, CB_DA_K, CB_DA_V, CB_MEM_Q = 40, 52, 64, 76
ADAM_LR, ADAM_B1, ADAM_B2, ADAM_EPS, ADAM_WD, ADAM_STEP = 0.001, 0.9, 0.999, 1e-08, 0.01, 10
VMEM_BYTES_V7X = 64 * 1024 * 1024
SMALL_ROWS = 104
SMALL_GRAD_ROWS = 88

_NN = (((1,), (0,)), ((), ()))
_NT = (((1,), (1,)), ((), ()))
_TN = (((0,), (0,)), ((), ()))


def _dot(a, b, dims):
    return lax.dot_general(a.astype(BF16), b.astype(BF16), dims, preferred_element_type=F32)


def _sigmoid(x):
    return 0.5 * jnp.tanh(0.5 * x) + 0.5


def _params(semantics, est_bytes):
    limit = int(min(VMEM_BYTES_V7X - (6 << 20), max(56 << 20, est_bytes * 3 // 2)))
    return pltpu.CompilerParams(dimension_semantics=semantics, vmem_limit_bytes=limit)


def _nbytes(shape, dtype):
    return int(np.prod(shape)) * jnp.dtype(dtype).itemsize


def _alibi_slopes(n):
    return (2.0 ** (-8.0 * np.arange(1, n + 1) / n)).astype(np.float32)


def _matmul(a, b, mode, out_dtype, tm, tn, tk, name, b_stacked=False, out_stacked=False, n_outer=False, after=None,
            m_blocks=None, out_into=None, a_parts=False, b_parts=False):
    if a_parts:
        assert mode == "nt" and tk == a.shape[2]
        m, kdim = a.shape[1], a.shape[0] * a.shape[2]
    elif mode == "tn":
        kdim, m = a.shape
    else:
        m, kdim = a.shape
    if b_parts:
        assert mode == "nn" and not b_stacked and b.shape[2] % tn == 0
        n = b.shape[0] * b.shape[2]
    elif b_stacked:
        if mode == "nn":
            n = b.shape[0] * b.shape[2]
            assert tn == b.shape[2] and tk == kdim == b.shape[1]
        else:
            assert mode == "nt" and tk == b.shape[2] and b.shape[0] * tk == kdim
            n = b.shape[1]
    else:
        n = b.shape[0] if mode == "nt" else b.shape[1]
    assert m % tm == 0 and n % tn == 0 and kdim % tk == 0
    gm, gn, gk = m // tm, n // tn, kdim // tk
    i0 = 0
    if m_blocks is not None:
        assert mode != "tn" and not out_stacked
        i0, gm = m_blocks

    def ijk(f):
        if n_outer:
            return lambda j, i, k: f(i + i0, j, k)
        return lambda i, j, k: f(i + i0, j, k)

    if a_parts:
        a_spec = pl.BlockSpec((None, tm, tk), ijk(lambda i, j, k: (k, i, 0)))
    elif mode == "tn":
        a_spec = pl.BlockSpec((tk, tm), ijk(lambda i, j, k: (k, i)))
    else:
        a_spec = pl.BlockSpec((tm, tk), ijk(lambda i, j, k: (i, k)))
    if b_parts:
        per_part = b.shape[2] // tn
        b_spec = pl.BlockSpec((None, tk, tn), ijk(lambda i, j, k: (j // per_part, k, j % per_part)))
    elif b_stacked and mode == "nn":
        b_spec = pl.BlockSpec((None, tk, tn), ijk(lambda i, j, k: (j, 0, 0)))
    elif b_stacked:
        b_spec = pl.BlockSpec((None, tn, tk), ijk(lambda i, j, k: (k, j, 0)))
    elif mode == "nt":
        b_spec = pl.BlockSpec((tn, tk), ijk(lambda i, j, k: (j, k)))
    else:
        b_spec = pl.BlockSpec((tk, tn), ijk(lambda i, j, k: (k, j)))
    if out_stacked:
        assert tm == m
        out_shape = jax.ShapeDtypeStruct((gn, m, tn), out_dtype)
        o_spec = pl.BlockSpec((None, tm, tn), ijk(lambda i, j, k: (j, i, 0)))
    else:
        out_shape = jax.ShapeDtypeStruct((m, n), out_dtype)
        o_spec = pl.BlockSpec((tm, tn), ijk(lambda i, j, k: (i, j)))
    dims = {"nn": _NN, "nt": _NT, "tn": _TN}[mode]

    n_in = 2 + (after is not None) + (out_into is not None)

    def body(*refs):
        a_ref, b_ref, o_ref = refs[0], refs[1], refs[n_in]
        part = _dot(a_ref[...], b_ref[...], dims)
        if gk == 1:
            o_ref[...] = part.astype(out_dtype)
            return
        acc_ref = refs[-1]
        k = pl.program_id(2)

        @pl.when(k == 0)
        def _():
            acc_ref[...] = part

        @pl.when(jnp.logical_and(k > 0, k < gk - 1))
        def _():
            acc_ref[...] += part

        @pl.when(k == gk - 1)
        def _():
            o_ref[...] = (acc_ref[...] + part).astype(out_dtype)

    a_tile = _nbytes((tm, tk), a.dtype)
    b_tile = _nbytes((tk, tn), b.dtype)
    o_tile = _nbytes((tm, tn), out_dtype)
    est = 2 * (a_tile + b_tile + o_tile) + 3 * tm * tn * 4 + (a_tile + b_tile)
    grid = (gn, gm, gk) if n_outer else (gm, gn, gk)
    operands, in_specs = [a, b], [a_spec, b_spec]
    if after is not None:
        operands.append(after)
        in_specs.append(pl.BlockSpec(memory_space=pl.ANY))
    aliases = {}
    if out_into is not None:
        aliases = {len(operands): 0}
        operands.append(out_into)
        in_specs.append(pl.BlockSpec(memory_space=pl.ANY))
    return pl.pallas_call(
        body, name=name, grid=grid, in_specs=in_specs, out_specs=o_spec, out_shape=out_shape,
        scratch_shapes=[] if gk == 1 else [pltpu.VMEM((tm, tn), F32)], input_output_aliases=aliases,
        compiler_params=_params(("parallel", "parallel", "arbitrary"), est),
    )(*operands)


def _row_spec(tr, width, col_block=0):
    return pl.BlockSpec((tr, width), lambda i: (i, col_block))


def _bcast_spec(width):
    return pl.BlockSpec((1, width), lambda i: (0, 0))


def _col_spec(width, tr):
    return pl.BlockSpec((width, tr), lambda i: (0, i))


def _rmsnorm_fwd(x, gain, name, tr, transposed=False):
    rows, width = x.shape

    def body(x_ref, g_ref, o_ref, *t_ref):
        xv = x_ref[...]
        r = lax.rsqrt(jnp.mean(xv * xv, axis=-1, keepdims=True) + RMS_EPS)
        h = xv * r * g_ref[...]
        o_ref[...] = h.astype(BF16)
        if transposed:
            t_ref[0][...] = h.T.astype(BF16)

    out_specs, out_shape = [_row_spec(tr, width)], [jax.ShapeDtypeStruct((rows, width), BF16)]
    if transposed:
        out_specs.append(_col_spec(width, tr))
        out_shape.append(jax.ShapeDtypeStruct((width, rows), BF16))
    out = pl.pallas_call(
        body, name=name, grid=(rows // tr,), in_specs=[_row_spec(tr, width), _bcast_spec(width)],
        out_specs=out_specs, out_shape=out_shape,
        compiler_params=_params(("parallel",), 10 * tr * width * 4),
    )(x, gain)
    return out if transposed else out[0]


def _residual_rmsnorm_fwd(x, merged, wout, gain, name, tr):
    rows, width = x.shape

    def body(x_ref, m_ref, w_ref, g_ref, x1_ref, h_ref, ht_ref):
        xv = x_ref[...] + _dot(m_ref[...], w_ref[...], _NN)
        x1_ref[...] = xv
        r = lax.rsqrt(jnp.mean(xv * xv, axis=-1, keepdims=True) + RMS_EPS)
        h = xv * r * g_ref[...]
        h_ref[...] = h.astype(BF16)
        ht_ref[...] = h.T.astype(BF16)

    return pl.pallas_call(
        body, name=name, grid=(rows // tr,),
        in_specs=[_row_spec(tr, width), _row_spec(tr, merged.shape[1]),
                  pl.BlockSpec(wout.shape, lambda i: (0, 0)), _bcast_spec(width)],
        out_specs=[_row_spec(tr, width), _row_spec(tr, width), _col_spec(width, tr)],
        out_shape=[jax.ShapeDtypeStruct((rows, width), F32), jax.ShapeDtypeStruct((rows, width), BF16),
                   jax.ShapeDtypeStruct((width, rows), BF16)],
        compiler_params=_params(("parallel",), 14 * tr * width * 4),
    )(x, merged, wout, gain)


def _matmul_rmsnorm_bwd(a, b, x, dres, gain, name, tm, tk, a_parts=False, b_stacked=False, after=None,
                        m_blocks=None, out_into=None, with_bf16=True):
    width = x.shape[1]
    m = a.shape[1] if a_parts else a.shape[0]
    kdim = a.shape[0] * a.shape[2] if a_parts else a.shape[1]
    gk = kdim // tk
    i0, gm = (0, m // tm) if m_blocks is None else m_blocks
    n_in = 5 + (after is not None) + (out_into is not None)

    def body(*refs):
        a_ref, b_ref, x_ref, dres_ref, g_ref = refs[:5]
        outs = refs[n_in:]
        dx_ref, dg_ref, acc_ref = outs[0], outs[-2], outs[-1]
        i, k = pl.program_id(0), pl.program_id(1)
        part = _dot(a_ref[...], b_ref[...], _NT)

        @pl.when(k == 0)
        def _():
            acc_ref[...] = part

        @pl.when(jnp.logical_and(k > 0, k < gk - 1))
        def _():
            acc_ref[...] += part

        @pl.when(k == gk - 1)
        def _():
            dhv = acc_ref[...] + part if gk > 1 else part
            xv = x_ref[...]
            r = lax.rsqrt(jnp.mean(xv * xv, axis=-1, keepdims=True) + RMS_EPS)
            xhat = xv * r
            dyg = dhv * g_ref[...]
            dx = dres_ref[...] + r * (dyg - xhat * jnp.mean(dyg * xhat, axis=-1, keepdims=True))
            dx_ref[...] = dx
            if with_bf16:
                outs[1][...] = dx.astype(BF16)
            gpart = jnp.sum(dhv * xhat, axis=0, keepdims=True)

            @pl.when(i == 0)
            def _():
                dg_ref[...] = gpart

            @pl.when(i > 0)
            def _():
                dg_ref[...] += gpart

    rows = lambda width_: pl.BlockSpec((tm, width_), lambda i, k: (i + i0, 0))
    if a_parts:
        a_spec = pl.BlockSpec((None, tm, tk), lambda i, k: (k, i + i0, 0))
    else:
        a_spec = pl.BlockSpec((tm, tk), lambda i, k: (i + i0, k))
    if b_stacked:
        b_spec = pl.BlockSpec((None, width, tk), lambda i, k: (k, 0, 0))
    else:
        b_spec = pl.BlockSpec((width, tk), lambda i, k: (0, k))
    operands = [a, b, x, dres, gain]
    in_specs = [a_spec, b_spec, rows(width), rows(width), pl.BlockSpec((1, width), lambda i, k: (0, 0))]
    for extra in (after, out_into):
        if extra is not None:
            operands.append(extra)
            in_specs.append(pl.BlockSpec(memory_space=pl.ANY))
    aliases = {} if out_into is None else {len(operands) - 1: 0}
    out_specs = [rows(width)] + ([rows(width)] if with_bf16 else []) + [pl.BlockSpec((1, width), lambda i, k: (0, 0))]
    out_shape = [jax.ShapeDtypeStruct((m, width), F32)] + ([jax.ShapeDtypeStruct((m, width), BF16)] if with_bf16 else [])
    out_shape.append(jax.ShapeDtypeStruct((1, width), F32))
    est = 4 * tm * tk + 4 * width * tk + 12 * tm * width * 4
    return pl.pallas_call(
        body, name=name, grid=(gm, gk), in_specs=in_specs, out_specs=out_specs, out_shape=out_shape,
        scratch_shapes=[pltpu.VMEM((tm, width), F32)], input_output_aliases=aliases,
        compiler_params=_params(("arbitrary", "arbitrary"), est),
    )(*operands)


def _gain_grad(x, dh, name):
    rows, width = x.shape

    def body(x_ref, dh_ref, dg_ref):
        xv = x_ref[...]
        r = lax.rsqrt(jnp.mean(xv * xv, axis=-1, keepdims=True) + RMS_EPS)
        dg_ref[...] = jnp.sum(dh_ref[...] * xv * r, axis=0, keepdims=True)

    return pl.pallas_call(
        body, name=name, grid=(1,), in_specs=[_row_spec(rows, width), _row_spec(rows, width)],
        out_specs=_bcast_spec(width), out_shape=jax.ShapeDtypeStruct((1, width), F32),
        compiler_params=_params(("arbitrary",), 6 * rows * width * 4),
    )(x, dh)


def _lb_table(logits, name):
    slots, width = logits.shape

    def body(l_ref, o_ref):
        lv = l_ref[...]
        mx = jnp.max(lv, axis=0, keepdims=True)
        e = jnp.exp(lv - mx)
        o_ref[...] = e[0:1, :] / jnp.sum(e, axis=0, keepdims=True)

    return pl.pallas_call(
        body, name=name, grid=(1,), in_specs=[pl.BlockSpec((slots, width), lambda i: (0, 0))],
        out_specs=_bcast_spec(width), out_shape=jax.ShapeDtypeStruct((1, width), F32),
    )(logits)


def _branch_merge_fwd(proj, outs, weights, tr):
    w = D_MODEL

    def body(ghg_ref, gda_ref, gmem_ref, ohg_ref, oda_ref, omem_ref, whg_ref, wda_ref, wmem_ref,
             m_ref, thg_ref, tda_ref, tmem_ref):
        acc = None
        for g_ref, o_ref, w_ref, t_ref in ((ghg_ref, ohg_ref, whg_ref, thg_ref), (gda_ref, oda_ref, wda_ref, tda_ref),
                                           (gmem_ref, omem_ref, wmem_ref, tmem_ref)):
            t = _dot(o_ref[...], w_ref[...], _NN)
            t_ref[...] = t.astype(BF16)
            term = _sigmoid(g_ref[...]) * t
            acc = term if acc is None else acc + term
        m_ref[...] = acc.astype(BF16)

    whole = lambda a: pl.BlockSpec(a.shape, lambda i: (0, 0))
    shape = jax.ShapeDtypeStruct((SEQ, w), BF16)
    return pl.pallas_call(
        body, name="branch_merge_fwd", grid=(SEQ // tr,),
        in_specs=[_row_spec(tr, w, 10), _row_spec(tr, w, 11), _row_spec(tr, w, 12)]
        + [_row_spec(tr, o.shape[1]) for o in outs] + [whole(wt) for wt in weights],
        out_specs=[_row_spec(tr, w)] * 4, out_shape=[shape] * 4,
        compiler_params=_params(("parallel",), 20 * tr * w * 4),
    )(proj, proj, proj, *outs, *weights)


def _dproj_buffer():
    return lax.empty((SEQ, IN_COLS), BF16)


def _gate_merge_bwd(proj, ts, weights, dx1b, wout, o_pre, hg_gain, o_da32, dproj, tr):
    w = D_MODEL
    steps = SEQ // tr
    gate_col0 = 10 * w
    hg_gate_col0 = CB_HG_G * LANE

    def body(ghg_ref, gda_ref, gmem_ref, hgg_ref, thg_ref, tda_ref, tmem_ref, whg_ref, wda_ref, wmem_ref, dx_ref,
             wout_ref, opre_ref, gain_ref, oda_ref, dproj_in,
             dthg_ref, dtda_ref, dtmem_ref, dopre_ref, doda_ref, domem_ref, dd_ref, dgain_ref, dproj_ref,
             stage, stage_hg, sems):
        del dproj_in
        i = pl.program_id(0)
        slot = i % 2
        rows = pl.ds(pl.multiple_of(i * tr, tr), tr)

        def slot_copies(s):
            return (pltpu.make_async_copy(stage.at[s], dproj_ref.at[rows, pl.ds(gate_col0, 3 * w)], sems.at[s]),
                    pltpu.make_async_copy(stage_hg.at[s], dproj_ref.at[rows, pl.ds(hg_gate_col0, w)], sems.at[2 + s]))

        @pl.when(i >= 2)
        def _():
            for cp in slot_copies(slot):
                cp.wait()

        dm = _dot(dx_ref[...], wout_ref[...], _NT)
        branches = ((ghg_ref, thg_ref, whg_ref, dthg_ref), (gda_ref, tda_ref, wda_ref, dtda_ref),
                    (gmem_ref, tmem_ref, wmem_ref, dtmem_ref))
        dos = []
        for b, (g_ref, t_ref, w_ref, dt_ref) in enumerate(branches):
            s = _sigmoid(g_ref[...])
            dt = (s * dm).astype(BF16)
            dt_ref[...] = dt
            dos.append(_dot(dt, w_ref[...], _NT))
            stage[slot, :, b * w:(b + 1) * w] = (dm * t_ref[...].astype(F32) * s * (1.0 - s)).astype(BF16)
        do_hg, do_da, do_mem = dos
        doda_ref[...] = do_da
        domem_ref[...] = do_mem

        gainv = gain_ref[...]
        part = jnp.zeros((1, HEAD_DIM), F32)
        for h in range(HG_HEADS):
            hs = slice(h * HEAD_DIM, (h + 1) * HEAD_DIM)
            o = opre_ref[:, hs]
            r = lax.rsqrt(jnp.mean(o * o, axis=-1, keepdims=True) + RMS_EPS)
            ohat = o * r
            g = hgg_ref[:, hs]
            sg = _sigmoid(g)
            silu = g * sg
            dout = do_hg[:, hs]
            stage_hg[slot, :, hs] = (dout * ohat * gainv * (sg + silu * (1.0 - sg))).astype(BF16)
            dy = dout * silu
            part = part + jnp.sum(dy * ohat, axis=0, keepdims=True)
            dn = dy * gainv
            dopre_ref[:, hs] = r * (dn - ohat * jnp.mean(dn * ohat, axis=-1, keepdims=True))
        for cp in slot_copies(slot):
            cp.start()

        prod = do_da * oda_ref[...]
        for h in range(DA_WIDTH // HEAD_DIM):
            hs = slice(h * HEAD_DIM, (h + 1) * HEAD_DIM)
            dd_ref[:, hs] = jnp.broadcast_to(jnp.sum(prod[:, hs], axis=-1, keepdims=True), (tr, HEAD_DIM))

        @pl.when(i == 0)
        def _():
            dgain_ref[...] = part

        @pl.when(i > 0)
        def _():
            dgain_ref[...] += part

        @pl.when(i == steps - 1)
        def _():
            for cp in slot_copies(1 - slot) + slot_copies(slot):
                cp.wait()

    assert steps >= 2
    whole = lambda a: pl.BlockSpec(a.shape, lambda i: (0, 0))
    widths = [wt.shape[0] for wt in weights]
    return pl.pallas_call(
        body, name="gate_merge_bwd", grid=(steps,),
        in_specs=[_row_spec(tr, w, 10), _row_spec(tr, w, 11), _row_spec(tr, w, 12), _row_spec(tr, w, hg_gate_col0 // w)]
        + [_row_spec(tr, w)] * 3 + [whole(wt) for wt in weights]
        + [_row_spec(tr, w), whole(wout), _row_spec(tr, w), _bcast_spec(HEAD_DIM), _row_spec(tr, DA_WIDTH), _any_spec()],
        out_specs=[_row_spec(tr, w)] * 4 + [_row_spec(tr, widths[1]), _row_spec(tr, widths[2]),
                                            _row_spec(tr, DA_WIDTH), _bcast_spec(HEAD_DIM), _any_spec()],
        out_shape=[jax.ShapeDtypeStruct((SEQ, w), BF16)] * 3 + [jax.ShapeDtypeStruct((SEQ, w), F32)]
        + [jax.ShapeDtypeStruct((SEQ, widths[1]), F32), jax.ShapeDtypeStruct((SEQ, widths[2]), F32),
           jax.ShapeDtypeStruct((SEQ, DA_WIDTH), F32), jax.ShapeDtypeStruct((1, HEAD_DIM), F32),
           jax.ShapeDtypeStruct((SEQ, IN_COLS), BF16)],
        scratch_shapes=[pltpu.VMEM((2, tr, 3 * w), BF16), pltpu.VMEM((2, tr, w), BF16), pltpu.SemaphoreType.DMA((4,))],
        input_output_aliases={15: 8},
        compiler_params=_params(("arbitrary",), 44 * tr * w * 4),
    )(proj, proj, proj, proj, *ts, *weights, dx1b, wout, o_pre, hg_gain, o_da32, dproj)


def _ffn_in_swiglu(h2, wfin, tm, tn):
    gm, gn = SEQ // tm, D_FF // tn

    def body(h_ref, wa_ref, wb_ref, a_ref, b_ref, act_ref, actt_ref):
        h = h_ref[...]
        a = _dot(h, wa_ref[...], _NN)
        b = _dot(h, wb_ref[...], _NN)
        act = a * _sigmoid(a) * b
        a_ref[...] = a.astype(BF16)
        b_ref[...] = b.astype(BF16)
        act_ref[...] = act.astype(BF16)
        actt_ref[...] = act.T.astype(BF16)

    tile = pl.BlockSpec((tm, tn), lambda j, i: (i, j))
    shape = jax.ShapeDtypeStruct((SEQ, D_FF), BF16)
    return pl.pallas_call(
        body, name="ffn_in_swiglu_fwd", grid=(gn, gm),
        in_specs=[pl.BlockSpec((tm, D_MODEL), lambda j, i: (i, 0)),
                  pl.BlockSpec((D_MODEL, tn), lambda j, i: (0, j)),
                  pl.BlockSpec((D_MODEL, tn), lambda j, i: (0, gn + j))],
        out_specs=[tile, tile, tile, pl.BlockSpec((tn, tm), lambda j, i: (j, i))],
        out_shape=[shape, shape, shape, jax.ShapeDtypeStruct((D_FF, SEQ), BF16)],
        compiler_params=_params(("parallel", "parallel"), 4 * tm * D_MODEL + 8 * D_MODEL * tn + 16 * tm * tn
                                + 6 * tm * tn * 4),
    )(h2, wfin, wfin)


def _ffn_out_bwd_swiglu(dyb, wfout, a, b, tm, tn):
    gm, gn = SEQ // tm, D_FF // tn

    def body(dy_ref, w_ref, a_ref, b_ref, o_ref):
        d = _dot(dy_ref[...], w_ref[...], _NT)
        av = a_ref[...].astype(F32)
        bv = b_ref[...].astype(F32)
        s = _sigmoid(av)
        silu = av * s
        o_ref[0] = (d * bv * (s + silu * (1.0 - s))).astype(BF16)
        o_ref[1] = (d * silu).astype(BF16)

    tile = pl.BlockSpec((tm, tn), lambda i, j: (i, j))
    return pl.pallas_call(
        body, name="ffn_out_bwd_swiglu", grid=(gm, gn),
        in_specs=[pl.BlockSpec((tm, D_MODEL), lambda i, j: (i, 0)), pl.BlockSpec((tn, D_MODEL), lambda i, j: (j, 0)),
                  tile, tile],
        out_specs=pl.BlockSpec((2, tm, tn), lambda i, j: (0, i, j)),
        out_shape=jax.ShapeDtypeStruct((2, SEQ, D_FF), BF16),
        compiler_params=_params(("parallel", "parallel"), 4 * tm * D_MODEL + 4 * tn * D_MODEL + 16 * tm * tn
                                + 6 * tm * tn * 4),
    )(dyb, wfout, a, b)


def _ffn_out_loss_head(x1, act, wfout, target, tr):
    w = D_MODEL

    def body(x_ref, a_ref, w_ref, t_ref, dy_ref, dyb_ref, loss_ref, acc_ref):
        err = x_ref[...] + _dot(a_ref[...], w_ref[...], _NN) - t_ref[...]
        dy = err * (1.0 / w)
        dy_ref[...] = dy
        dyb_ref[...] = dy.astype(BF16)
        part = jnp.sum(err * err, axis=0, keepdims=True)

        @pl.when(pl.program_id(0) == 0)
        def _():
            acc_ref[...] = part

        @pl.when(pl.program_id(0) > 0)
        def _():
            acc_ref[...] += part

        @pl.when(pl.program_id(0) == SEQ // tr - 1)
        def _():
            total = jnp.sum(acc_ref[...], axis=1, keepdims=True) * (0.5 / w)
            loss_ref[...] = jnp.broadcast_to(total, (1, LANE))

    return pl.pallas_call(
        body, name="ffn_out_loss_head", grid=(SEQ // tr,),
        in_specs=[_row_spec(tr, w), _row_spec(tr, D_FF), pl.BlockSpec((D_FF, w), lambda i: (0, 0)), _row_spec(tr, w)],
        out_specs=[_row_spec(tr, w), _row_spec(tr, w), _bcast_spec(LANE)],
        out_shape=[jax.ShapeDtypeStruct((SEQ, w), F32), jax.ShapeDtypeStruct((SEQ, w), BF16),
                   jax.ShapeDtypeStruct((1, LANE), F32)],
        scratch_shapes=[pltpu.VMEM((1, w), F32)],
        compiler_params=_params(("arbitrary",), 12 * tr * w * 4 + 4 * D_FF * w),
    )(x1, act, wfout, target)


GLA_ROWS = 256
GLA_CPB = GLA_ROWS // HG_CHUNK
GLA_NBLK = SEQ // GLA_ROWS
GLA_WAYS = 8
GLA_TRIPS = GLA_NBLK // GLA_WAYS
GLA_GRAD_WAYS = 8
GLA_GRAD_TRIPS = GLA_NBLK // GLA_GRAD_WAYS
GLA_NCK = SEQ // HG_CHUNK
GLA_INTER_WAYS = 32


def _dot_split(m, xv, dims, terms=3):
    dot = lambda t: lax.dot_general(m, t, dims, preferred_element_type=F32)
    hi = xv.astype(BF16)
    r1 = xv - hi.astype(F32)
    mid = r1.astype(BF16)
    if terms == 2:
        return dot(mid) + dot(hi)
    lo = (r1 - mid.astype(F32)).astype(BF16)
    return (dot(lo) + dot(mid)) + dot(hi)


def _gla_block(qc, fc, lbv, masks, b=None):
    mask, maskb, direction = masks
    sq = _sigmoid(qc)
    q = qc * sq * HG_SCALE
    sf = _sigmoid(fc)
    forget = lbv + (1.0 - lbv) * sf
    k = 1.0 - forget
    logf = jnp.log(forget)
    if b is None:
        b = _dot_split(maskb, logf, _NN)
    ends = []
    for j in range(GLA_CPB):
        lo, hi = j * HG_CHUNK, (j + 1) * HG_CHUNK
        end = jnp.where(direction == 0, b[hi - 1:hi, :], b[lo:lo + 1, :])
        ends.append(jnp.broadcast_to(end, (HG_CHUNK, HEAD_DIM)))
    bt = jnp.concatenate(ends, axis=0)
    eb = jnp.exp(b)
    qt = q * eb
    kt = k * jnp.exp(-b)
    kh = k * jnp.exp(bt - b)
    a = jnp.where(mask, _dot(qt, kt, _NT), 0.0)
    return dict(sq=sq, sf=sf, forget=forget, k=k, b=b, bt=bt, eb=eb, qt=qt, kt=kt, kh=kh, a=a)


def _gla_masks(direction):
    row = lax.broadcasted_iota(jnp.int32, (GLA_ROWS, GLA_ROWS), 0)
    col = lax.broadcasted_iota(jnp.int32, (GLA_ROWS, GLA_ROWS), 1)
    same = (row // HG_CHUNK) == (col // HG_CHUNK)
    mask = jnp.logical_and(same, jnp.where(direction == 0, row - col, col - row) >= 0)
    return mask, jnp.where(mask, 1.0, 0.0).astype(BF16), direction


def _gla_chunk_rows(j):
    return slice(j * HG_CHUNK, (j + 1) * HG_CHUNK)


def _gla_block_rows(b):
    return pl.ds(pl.multiple_of(b * GLA_ROWS, GLA_ROWS), GLA_ROWS)


def _head_spec(col_block0):
    return pl.BlockSpec((SEQ, HEAD_DIM), lambda h, d: (0, col_block0 + h))


def _gla_decay(proj, lb):
    def body(f_ref, lb_ref, b_ref):
        maskb = _gla_masks(pl.program_id(0))[1]
        lbv = lb_ref[...]

        def step(s, carry):
            rows = [_gla_block_rows(s + w * trips) for w in range(ways)]
            loaded = [f_ref[r, :] for r in rows]
            outs = [_dot_split(maskb, jnp.log(lbv + (1.0 - lbv) * _sigmoid(fc)), _NN) for fc in loaded]
            for r, o in zip(rows, outs):
                b_ref[r, :] = o
            return carry

        lax.fori_loop(0, trips, step, 0)

    ways, trips = GLA_WAYS, GLA_TRIPS
    heads = 2
    width = heads * HEAD_DIM
    return pl.pallas_call(
        body, name="gla_decay", grid=(2, HG_HEADS // heads),
        in_specs=[pl.BlockSpec((SEQ, width), lambda d, h: (0, CB_F // heads + (HG_HEADS // heads) * d + h)),
                  pl.BlockSpec((None, None, 1, width), lambda d, h: (d, h, 0, 0))],
        out_specs=pl.BlockSpec((None, SEQ, width), lambda d, h: (d, 0, h)),
        out_shape=jax.ShapeDtypeStruct((2, SEQ, D_MODEL), F32),
        compiler_params=_params(("parallel", "parallel"), 6 * SEQ * width * 4),
    )(proj, lb.reshape(2, HG_HEADS // heads, 1, width))


def _gla_fwd(proj, lb, gain, b_all):
    nck = GLA_NCK

    def body(q_ref, f_ref, v_ref, g_ref, lb_ref, gain_ref, b_ref, ohg_ref, opre_ref, st_ref, qt_scr, dec_scr, cs_scr):
        d = pl.program_id(1)
        masks = _gla_masks(d)
        lbv = lb_ref[...]

        @pl.when(d == 0)
        def _():
            opre_ref[...] = jnp.zeros_like(opre_ref)

        def intra(s, carry):
            blocks = [s + w * GLA_TRIPS for w in range(GLA_WAYS)]
            rows = [_gla_block_rows(b) for b in blocks]
            loaded = [(q_ref[r, :], f_ref[r, :], v_ref[r, :], opre_ref[r, :], b_ref[r, :]) for r in rows]
            results = []
            for qc, fc, v, o_prev, decay_log in loaded:
                ck = _gla_block(qc, fc, lbv, masks, b=decay_log)
                o = o_prev + _dot(ck["a"], v, _NN)
                cs = [_dot(v[_gla_chunk_rows(j), :], ck["kh"][_gla_chunk_rows(j), :], _TN) for j in range(GLA_CPB)]
                dec = [jnp.exp(ck["bt"][j * HG_CHUNK:j * HG_CHUNK + 8, :]) for j in range(GLA_CPB)]
                results.append((o, ck["qt"].astype(BF16), cs, dec))
            for b, r, (o, qt, cs, dec) in zip(blocks, rows, results):
                opre_ref[r, :] = o
                qt_scr[r, :] = qt
                for j in range(GLA_CPB):
                    cs_scr[b * GLA_CPB + j] = cs[j]
                    dec_scr[b * GLA_CPB + j] = dec[j]
            return carry

        lax.fori_loop(0, GLA_TRIPS, intra, 0)

        def scan(i, st):
            c = jnp.where(d == 0, i, nck - 1 - i)
            st_ref[c] = st
            return st * dec_scr[c][0:1, :] + cs_scr[c]

        lax.fori_loop(0, nck, scan, jnp.zeros((HEAD_DIM, HEAD_DIM), F32), unroll=4)

        def inter(t, carry):
            chunks = [t + w * (nck // GLA_INTER_WAYS) for w in range(GLA_INTER_WAYS)]
            rows = [pl.ds(pl.multiple_of(c * HG_CHUNK, HG_CHUNK), HG_CHUNK) for c in chunks]
            loaded = [(opre_ref[r, :], qt_scr[r, :], st_ref[c]) for r, c in zip(rows, chunks)]
            for r, o in zip(rows, [o_prev + _dot(qt, st, _NT) for o_prev, qt, st in loaded]):
                opre_ref[r, :] = o
            return carry

        lax.fori_loop(0, nck // GLA_INTER_WAYS, inter, 0)

        @pl.when(d == 1)
        def _():
            o = opre_ref[...]
            r = lax.rsqrt(jnp.mean(o * o, axis=-1, keepdims=True) + RMS_EPS)
            g = g_ref[...]
            ohg_ref[...] = (o * r * gain_ref[...] * (g * _sigmoid(g))).astype(BF16)

    blk = SEQ * HEAD_DIM * 4
    return pl.pallas_call(
        body, name="gla_fwd", grid=(HG_HEADS, 2),
        in_specs=[_head_spec(CB_HG_Q),
                  pl.BlockSpec((SEQ, HEAD_DIM), lambda h, d: (0, CB_F + 8 * d + h)),
                  _head_spec(CB_HG_I), _head_spec(CB_HG_G),
                  pl.BlockSpec((None, None, 1, HEAD_DIM), lambda h, d: (d, h, 0, 0)),
                  pl.BlockSpec((1, HEAD_DIM), lambda h, d: (0, 0)),
                  pl.BlockSpec((None, SEQ, HEAD_DIM), lambda h, d: (d, 0, h))],
        out_specs=[_head_spec(0), _head_spec(0),
                   pl.BlockSpec((None, None, nck, HEAD_DIM, HEAD_DIM), lambda h, d: (h, d, 0, 0, 0)),
                   pl.BlockSpec((None, SEQ, HEAD_DIM), lambda h, d: (d, 0, h)),
                   pl.BlockSpec((None, None, nck, 8, HEAD_DIM), lambda h, d: (h, d, 0, 0, 0))],
        out_shape=[jax.ShapeDtypeStruct((SEQ, D_MODEL), BF16), jax.ShapeDtypeStruct((SEQ, D_MODEL), F32),
                   jax.ShapeDtypeStruct((HG_HEADS, 2, nck, HEAD_DIM, HEAD_DIM), F32),
                   jax.ShapeDtypeStruct((2, SEQ, D_MODEL), BF16),
                   jax.ShapeDtypeStruct((HG_HEADS, 2, nck, 8, HEAD_DIM), F32)],
        scratch_shapes=[pltpu.VMEM((nck, HEAD_DIM, HEAD_DIM), F32)],
        compiler_params=_params(("parallel", "arbitrary"), 8 * blk + 3 * blk + 2 * blk + 2 * blk + 3 * blk),
    )(proj, proj, proj, proj, lb, gain, b_all)


def _gla_bwd(proj, lb, states, qt_all, dec_all, b_all, do, dproj):
    nck = GLA_NCK

    def body(q_ref, f_ref, v_ref, lb_ref, do_ref, st_ref, qt_ref, dec_scr, b_ref, dproj_in, dproj_ref, dlb_ref,
             dq_acc, dv_acc, dst_scr, cs_scr, df_out, dq_out, dv_out, sems):
        del dproj_in
        h, d = pl.program_id(0), pl.program_id(1)
        masks = _gla_masks(d)
        mask, maskb, _ = masks
        lbv = lb_ref[...]

        def column_copy(k, staging, col_block):
            cols = pl.ds(pl.multiple_of(col_block * LANE, LANE), LANE)
            return pltpu.make_async_copy(staging, dproj_ref.at[:, cols], sems.at[k])

        df_copy = column_copy(0, df_out, CB_F + 8 * d + h)
        dq_copy = column_copy(1, dq_out, CB_HG_Q + h)
        dv_copy = column_copy(2, dv_out, CB_HG_I + h)
        last = jnp.logical_and(h == HG_HEADS - 1, d == 1)

        def intra(s, carry):
            blocks = [s + w * GLA_TRIPS for w in range(GLA_WAYS)]
            loaded = [(qt_ref[r, :], do_ref[r, :]) for r in map(_gla_block_rows, blocks)]
            results = [[_dot(doc[_gla_chunk_rows(j), :], qt[_gla_chunk_rows(j), :], _TN) for j in range(GLA_CPB)]
                       for qt, doc in loaded]
            for b, cs in zip(blocks, results):
                for j in range(GLA_CPB):
                    cs_scr[b * GLA_CPB + j] = cs[j]
            return carry

        lax.fori_loop(0, GLA_TRIPS, intra, 0)

        def scan(i, dst):
            c = jnp.where(d == 0, nck - 1 - i, i)
            dst_scr[c] = dst
            return dst * dec_scr[c][0:1, :] + cs_scr[c]

        lax.fori_loop(0, nck, scan, jnp.zeros((HEAD_DIM, HEAD_DIM), F32), unroll=4)

        @pl.when(d == 0)
        def _():
            dq_acc[...] = jnp.zeros_like(dq_acc)
            dv_acc[...] = jnp.zeros_like(dv_acc)

        def block_grads(qc, fc, v, doc, states_in, dstates, decays, decay_log):
            ck = _gla_block(qc, fc, lbv, masks, b=decay_log)
            qt, kt, kh, a = ck["qt"], ck["kt"], ck["kh"], ck["a"]
            da = jnp.where(mask, _dot(doc, v, _NT), 0.0)
            dqt_i = _dot(da, kt, _NN)
            dkt = _dot(da, qt, _TN)
            dv_i = _dot(a, doc, _TN)
            dqt_p, dv_p, dkh_p, dbt_p = [], [], [], []
            for j in range(GLA_CPB):
                cr = _gla_chunk_rows(j)
                st_in, dst = states_in[j], dstates[j]
                dqt_p.append(dqt_i[cr, :] + _dot(doc[cr, :], st_in, _NN))
                dv_p.append(dv_i[cr, :] + _dot(kh[cr, :], dst, _NT))
                dkh_j = _dot(v[cr, :], dst, _NN)
                dkh_p.append(dkh_j)
                dbt_j = (decays[j][0:1, :] * jnp.sum(dst * st_in, axis=0, keepdims=True)
                         + jnp.sum(dkh_j * kh[cr, :], axis=0, keepdims=True))
                dbt_p.append(jnp.broadcast_to(dbt_j, (HG_CHUNK, HEAD_DIM)))
            dqt = jnp.concatenate(dqt_p, axis=0)
            dv = jnp.concatenate(dv_p, axis=0)
            dkh = jnp.concatenate(dkh_p, axis=0)
            dbt = jnp.concatenate(dbt_p, axis=0)
            db = dqt * qt - dkt * kt - dkh * kh
            dq = dqt * ck["eb"]
            dk = dkt * jnp.exp(-ck["b"]) + dkh * jnp.exp(ck["bt"] - ck["b"])
            dlogf = _dot_split(maskb, db, _TN, terms=2) + dbt
            dforget = dlogf / ck["forget"] - dk
            sf, sq = ck["sf"], ck["sq"]
            df = (dforget * (1.0 - lbv) * sf * (1.0 - sf)).astype(BF16)
            dqc = dq * HG_SCALE * (sq + qc * sq * (1.0 - sq))
            return df, dqc, dv, jnp.sum(dforget * (1.0 - sf), axis=0, keepdims=True)

        def grads(s, dlb):
            blocks = [s + w * GLA_GRAD_TRIPS for w in range(GLA_GRAD_WAYS)]
            rows = [_gla_block_rows(b) for b in blocks]
            loaded = []
            for b, r in zip(blocks, rows):
                chunks = [b * GLA_CPB + j for j in range(GLA_CPB)]
                loaded.append((q_ref[r, :], f_ref[r, :], v_ref[r, :], do_ref[r, :], [st_ref[c] for c in chunks],
                               [dst_scr[c] for c in chunks], [dec_scr[c] for c in chunks], b_ref[r, :],
                               dq_acc[r, :], dv_acc[r, :]))
            results = [block_grads(*t[:8]) + (t[8], t[9]) for t in loaded]
            for r, (df, dqc, dv, dlb_part, dq_prev, dv_prev) in zip(rows, results):
                df_out[r, :] = df
                dq_acc[r, :] = dq_prev + dqc
                dv_acc[r, :] = dv_prev + dv
                dlb = dlb + dlb_part
            return dlb

        @pl.when(jnp.logical_or(h > 0, d > 0))
        def _():
            df_copy.wait()

        dlb_ref[...] = lax.fori_loop(0, GLA_GRAD_TRIPS, grads, jnp.zeros((1, HEAD_DIM), F32))
        df_copy.start()

        @pl.when(d == 1)
        def _():
            @pl.when(h > 0)
            def _():
                dq_copy.wait()
                dv_copy.wait()

            dq_out[...] = dq_acc[...].astype(BF16)
            dv_out[...] = dv_acc[...].astype(BF16)
            dq_copy.start()
            dv_copy.start()

        @pl.when(last)
        def _():
            df_copy.wait()
            dq_copy.wait()
            dv_copy.wait()

    blk = SEQ * HEAD_DIM * 4
    state_bytes = nck * HEAD_DIM * HEAD_DIM * 4
    dproj, dlb = pl.pallas_call(
        body, name="gla_bwd", grid=(HG_HEADS, 2),
        in_specs=[_head_spec(CB_HG_Q),
                  pl.BlockSpec((SEQ, HEAD_DIM), lambda h, d: (0, CB_F + 8 * d + h)),
                  _head_spec(CB_HG_I),
                  pl.BlockSpec((None, None, 1, HEAD_DIM), lambda h, d: (d, h, 0, 0)),
                  _head_spec(0),
                  pl.BlockSpec((None, None, nck, HEAD_DIM, HEAD_DIM), lambda h, d: (h, d, 0, 0, 0)),
                  pl.BlockSpec((None, SEQ, HEAD_DIM), lambda h, d: (d, 0, h)),
                  pl.BlockSpec((None, None, nck, 8, HEAD_DIM), lambda h, d: (h, d, 0, 0, 0)),
                  pl.BlockSpec((None, SEQ, HEAD_DIM), lambda h, d: (d, 0, h)),
                  _any_spec()],
        out_specs=[_any_spec(), pl.BlockSpec((None, None, 1, HEAD_DIM), lambda h, d: (d, h, 0, 0))],
        out_shape=[jax.ShapeDtypeStruct((SEQ, IN_COLS), BF16), jax.ShapeDtypeStruct((2, HG_HEADS, 1, HEAD_DIM), F32)],
        scratch_shapes=[pltpu.VMEM((SEQ, HEAD_DIM), F32)] * 2
        + [pltpu.VMEM((nck, HEAD_DIM, HEAD_DIM), F32)] * 2
        + [pltpu.VMEM((SEQ, HEAD_DIM), BF16)] * 3 + [pltpu.SemaphoreType.DMA((3,))],
        input_output_aliases={9: 0},
        compiler_params=_params(("arbitrary", "arbitrary"), 10 * blk + 4 * state_bytes + 3 * blk + 3 * blk),
    )(proj, proj, proj, lb, do, states, qt_all, dec_all, b_all, dproj)
    return dproj, dlb


def _da_residue_rows(r, n0, size, d):
    if d == 1:
        return pl.ds(pl.multiple_of(n0, 8), size)
    return pl.ds(r + n0 * d, size, stride=d)


def _da_residue_ways(d, nqb):
    return min(d, 4) if nqb <= 2 else 1


def _da_rmsnorm(x, gain):
    r = lax.rsqrt(jnp.mean(x * x, axis=-1, keepdims=True) + RMS_EPS)
    return x * r, r, x * r * gain


def _da_scores(qn_scr, kn_scr, slope, i, ld):
    w0 = jnp.clip(i * DA_QB - DA_RADIUS, 0, ld - DA_WIN)
    w0 = pl.multiple_of(w0, DA_RADIUS)
    qrows = pl.ds(pl.multiple_of(i * DA_QB, DA_QB), DA_QB)
    win = pl.ds(w0, DA_WIN)
    qb = qn_scr[qrows, :]
    kw = kn_scr[win, :]
    s = _dot(qb, kw, _NT) * DA_SCALE
    qpos = i * DA_QB + lax.broadcasted_iota(jnp.int32, (DA_QB, DA_WIN), 0)
    kpos = w0 + lax.broadcasted_iota(jnp.int32, (DA_QB, DA_WIN), 1)
    arel = jnp.abs(kpos - qpos)
    s = s - slope * arel.astype(F32)
    s = jnp.where(arel <= DA_RADIUS, s, NEG_INF)
    return s, qb, kw, qrows, win


def _da_slopes(group):
    d = DA_DILATIONS[group]
    sl = _alibi_slopes(DA_HEADS)[4 * group:4 * group + 4] * d
    return jnp.asarray(np.broadcast_to(sl[:, None, None], (4, 1, LANE)).copy())


def _da_fwd(proj, gq, gk, group, others=()):
    d = DA_DILATIONS[group]
    ld = SEQ // d
    nqb = ld // DA_QB
    n_other = 2 * len(others)

    ways = min(DA_FWD_WAYS, nqb)
    rways = _da_residue_ways(d, nqb)

    def body(q_ref, k_ref, v_ref, gq_ref, gk_ref, sl_ref, *refs):
        other_refs, refs = refs[:n_other], refs[n_other:]
        if others:
            ob_ref, of_ref, lsej_ref, qn_scr, kn_scr, v_scr, o_ref, lse_ref = refs
        else:
            o_ref, lse_ref, qn_scr, kn_scr, v_scr = refs
        slope = sl_ref[:, 0:1]

        def residues(t, carry):
            rs = [t * rways + u for u in range(rways)]
            for u, r in enumerate(rs):
                sel = _da_residue_rows(r, 0, ld, d)
                qn_scr[u] = _da_rmsnorm(q_ref[sel, :], gq_ref[...])[2].astype(BF16)
                kn_scr[u] = _da_rmsnorm(k_ref[sel, :], gk_ref[...])[2].astype(BF16)
                v_scr[u] = v_ref[sel, :].astype(BF16)

            def block(u, i):
                s, _, _, _, win = _da_scores(qn_scr.at[u], kn_scr.at[u], slope, i, ld)
                m = jnp.max(s, axis=-1, keepdims=True)
                p = jnp.exp(s - m)
                l = jnp.sum(p, axis=-1, keepdims=True)
                return _dot(p, v_scr[u, win, :], _NN) / l, jnp.broadcast_to(m + jnp.log(l), (DA_QB, HEAD_DIM))

            def step(i, c2):
                todo = [(u, r, i + w * (nqb // ways)) for u, r in enumerate(rs) for w in range(ways)]
                for (u, r, b), (o, lse) in zip(todo, [block(u, b) for u, _, b in todo]):
                    out = _da_residue_rows(r, b * DA_QB, DA_QB, d)
                    o_ref[out, :] = o
                    lse_ref[out, :] = lse
                return c2

            return lax.fori_loop(0, nqb // ways, step, carry)

        lax.fori_loop(0, d // rways, residues, 0)

        if others:
            pieces = 8
            rows = SEQ // pieces

            def merge(t, carry):
                r = pl.ds(pl.multiple_of(t * rows, rows), rows)
                outs = [ref[r, :] for ref in other_refs[0::2]] + [o_ref[r, :]]
                lses = [ref[r, :] for ref in other_refs[1::2]] + [lse_ref[r, :]]
                m = lses[0]
                for l in lses[1:]:
                    m = jnp.maximum(m, l)
                es = [jnp.exp(l - m) for l in lses]
                tot = sum(es[1:], es[0])
                o = sum((e * v for e, v in zip(es[1:], outs[1:])), es[0] * outs[0]) / tot
                of_ref[r, :] = o
                ob_ref[r, :] = o.astype(BF16)
                lsej_ref[r, :] = m + jnp.log(tot)
                return carry

            lax.fori_loop(0, pieces, merge, 0)

    seq_spec = lambda base: pl.BlockSpec((SEQ, HEAD_DIM), lambda h: (0, base + 4 * group + h))
    out_spec = pl.BlockSpec((SEQ, HEAD_DIM), lambda h: (0, h))
    gain_spec = pl.BlockSpec((1, HEAD_DIM), lambda h: (0, 0))
    blk = SEQ * HEAD_DIM * 4
    fshape = jax.ShapeDtypeStruct((SEQ, DA_WIDTH), F32)
    scratch = [pltpu.VMEM((rways, ld, HEAD_DIM), BF16)] * 3
    if others:
        out_shape = [jax.ShapeDtypeStruct((SEQ, DA_WIDTH), BF16), fshape, fshape]
        scratch += [pltpu.VMEM((SEQ, HEAD_DIM), F32)] * 2
    else:
        out_shape = [fshape, fshape]
    return pl.pallas_call(
        body, name=f"da_fwd_g{group}", grid=(DA_HEADS_PER_GROUP,),
        in_specs=[seq_spec(CB_DA_Q), seq_spec(CB_DA_K), seq_spec(CB_DA_V), gain_spec, gain_spec,
                  pl.BlockSpec((None, 1, LANE), lambda h: (h, 0, 0))] + [out_spec] * n_other,
        out_specs=[out_spec] * len(out_shape), out_shape=out_shape, scratch_shapes=scratch,
        compiler_params=_params(("parallel",), 10 * blk + 2 * blk + (n_other + 3) * blk),
    )(proj, proj, proj, gq, gk, _da_slopes(group), *[a for pair in others for a in pair])


def _da_bwd(proj, gq, gk, do, lse, dd, dproj, group):
    d = DA_DILATIONS[group]
    ld = SEQ // d
    nqb = ld // DA_QB
    ways = min(DA_WAYS, nqb)
    rways = _da_residue_ways(d, nqb)

    def body(q_ref, k_ref, v_ref, gq_ref, gk_ref, sl_ref, do_ref, lse_ref, dd_ref, dproj_in,
             dproj_ref, dgq_ref, dgk_ref,
             qn_scr, kn_scr, v_scr, dqn_scr, dkn_scr, dvr_scr, rq_scr, rk_scr, dq_scr, dk_scr, dv_scr,
             dq_out, dk_out, dv_out, sems):
        del dproj_in
        head = pl.program_id(0)
        gqv, gkv = gq_ref[...], gk_ref[...]
        slope = sl_ref[:, 0:1]

        copies = []
        for k, (staging, base) in enumerate(((dq_out, CB_DA_Q), (dk_out, CB_DA_K), (dv_out, CB_DA_V))):
            cols = pl.ds(pl.multiple_of((base + 4 * group + head) * LANE, LANE), LANE)
            copies.append(pltpu.make_async_copy(staging, dproj_ref.at[:, cols], sems.at[k]))

        def residues(t, carry):
            rs = [t * rways + u for u in range(rways)]
            sels = [_da_residue_rows(r, 0, ld, d) for r in rs]
            for u, sel in enumerate(sels):
                for x_ref, gv, n_scr, r_scr in ((q_ref, gqv, qn_scr, rq_scr), (k_ref, gkv, kn_scr, rk_scr)):
                    _, rstd, normed = _da_rmsnorm(x_ref[sel, :], gv)
                    n_scr[u] = normed.astype(BF16)
                    r_scr[u] = jnp.broadcast_to(rstd, (ld, HEAD_DIM))
                v_scr[u] = v_ref[sel, :].astype(BF16)
            dkn_scr[...] = jnp.zeros_like(dkn_scr)
            dvr_scr[...] = jnp.zeros_like(dvr_scr)

            def block(u, r, i):
                s, qb, kw, qrows, win = _da_scores(qn_scr.at[u], kn_scr.at[u], slope, i, ld)
                src = _da_residue_rows(r, i * DA_QB, DA_QB, d)
                p = jnp.exp(s - lse_ref[src, :][:, 0:1])
                dob = do_ref[src, :]
                dp = _dot(dob, v_scr[u, win, :], _NT)
                ds = p * (dp - dd_ref[src, :][:, 0:1]) * DA_SCALE
                return u, qrows, win, _dot(p, dob, _TN), _dot(ds, kw, _NN), _dot(ds, qb, _TN)

            def step(i, c2):
                todo = [(u, r, i + w * (nqb // ways)) for u, r in enumerate(rs) for w in range(ways)]
                for u, qrows, win, dv, dqn, dkn in [block(*t) for t in todo]:
                    dvr_scr[u, win, :] += dv
                    dqn_scr[u, qrows, :] = dqn
                    dkn_scr[u, win, :] += dkn
                return c2

            lax.fori_loop(0, nqb // ways, step, 0)

            pq, pk = carry
            for u, sel in enumerate(sels):
                parts = []
                for x_ref, gv, dn_scr, dx_scr, r_scr in ((q_ref, gqv, dqn_scr, dq_scr, rq_scr),
                                                         (k_ref, gkv, dkn_scr, dk_scr, rk_scr)):
                    rstd = r_scr[u]
                    hat = x_ref[sel, :] * rstd
                    dn = dn_scr[u]
                    dyg = dn * gv
                    dx_scr[sel, :] = rstd * (dyg - hat * jnp.mean(dyg * hat, axis=-1, keepdims=True))
                    parts.append(jnp.sum(dn * hat, axis=0, keepdims=True))
                dv_scr[sel, :] = dvr_scr[u]
                pq, pk = pq + parts[0], pk + parts[1]
            return pq, pk

        zero = jnp.zeros((1, HEAD_DIM), F32)
        pq, pk = lax.fori_loop(0, d // rways, residues, (zero, zero))

        @pl.when(pl.program_id(0) == 0)
        def _():
            dgq_ref[...] = pq
            dgk_ref[...] = pk

        @pl.when(pl.program_id(0) > 0)
        def _():
            dgq_ref[...] += pq
            dgk_ref[...] += pk

        @pl.when(head > 0)
        def _():
            for cp in copies:
                cp.wait()

        dq_out[...] = dq_scr[...].astype(BF16)
        dk_out[...] = dk_scr[...].astype(BF16)
        dv_out[...] = dv_scr[...].astype(BF16)
        for cp in copies:
            cp.start()

        @pl.when(head == DA_HEADS_PER_GROUP - 1)
        def _():
            for cp in copies:
                cp.wait()

    seq_spec = lambda base: pl.BlockSpec((SEQ, HEAD_DIM), lambda h: (0, base + 4 * group + h))
    out_spec = pl.BlockSpec((SEQ, HEAD_DIM), lambda h: (0, h))
    gain_spec = pl.BlockSpec((1, HEAD_DIM), lambda h: (0, 0))
    gshape = jax.ShapeDtypeStruct((1, HEAD_DIM), F32)
    blk = SEQ * HEAD_DIM * 4
    return pl.pallas_call(
        body, name=f"da_bwd_g{group}", grid=(DA_HEADS_PER_GROUP,),
        in_specs=[seq_spec(CB_DA_Q), seq_spec(CB_DA_K), seq_spec(CB_DA_V), gain_spec, gain_spec,
                  pl.BlockSpec((None, 1, LANE), lambda h: (h, 0, 0)), out_spec, out_spec, out_spec, _any_spec()],
        out_specs=[_any_spec(), gain_spec, gain_spec],
        out_shape=[jax.ShapeDtypeStruct((SEQ, IN_COLS), BF16), gshape, gshape],
        scratch_shapes=[pltpu.VMEM((rways, ld, HEAD_DIM), BF16)] * 3 + [pltpu.VMEM((rways, ld, HEAD_DIM), F32)] * 5
        + [pltpu.VMEM((SEQ, HEAD_DIM), F32)] * 3 + [pltpu.VMEM((SEQ, HEAD_DIM), BF16)] * 3
        + [pltpu.SemaphoreType.DMA((3,))],
        input_output_aliases={9: 0},
        compiler_params=_params(("arbitrary",), 12 * blk + 3 * blk + 3 * blk + 3 * blk + 2 * blk),
    )(proj, proj, proj, gq, gk, _da_slopes(group), do, lse, dd, dproj)


MEM_WAYS = 4


def _mem_probs(q, kn, gq):
    qhat, rq, qn = _da_rmsnorm(q, gq)
    s = _dot(qn, kn, _NT) * MEM_SCALE
    m = jnp.max(s, axis=-1, keepdims=True)
    e = jnp.exp(s - m)
    p = e / jnp.sum(e, axis=-1, keepdims=True)
    return p, qhat, rq, qn


def _mem_pieces(tq):
    rows = tq // MEM_WAYS
    return [slice(w * rows, (w + 1) * rows) for w in range(MEM_WAYS)]


def _mem_fwd(proj, kv, gq, gk, tq):
    def body(q_ref, k_ref, v_ref, gq_ref, gk_ref, o_ref):
        kn = _da_rmsnorm(k_ref[...], gk_ref[...])[2]
        v, gqv = v_ref[...], gq_ref[...]
        pieces = _mem_pieces(tq)
        outs = [_dot(_mem_probs(q_ref[rs, :], kn, gqv)[0], v, _NN) for rs in pieces]
        for rs, o in zip(pieces, outs):
            o_ref[rs, :] = o.astype(BF16)

    gain_spec = pl.BlockSpec((1, HEAD_DIM), lambda h, i: (0, 0))
    return pl.pallas_call(
        body, name="mem_fwd", grid=(MEM_HEADS, SEQ // tq),
        in_specs=[pl.BlockSpec((tq, HEAD_DIM), lambda h, i: (i, CB_MEM_Q + h)),
                  pl.BlockSpec((N_MEM, HEAD_DIM), lambda h, i: (0, h)),
                  pl.BlockSpec((N_MEM, HEAD_DIM), lambda h, i: (0, MEM_HEADS + h)), gain_spec, gain_spec],
        out_specs=pl.BlockSpec((tq, HEAD_DIM), lambda h, i: (i, h)),
        out_shape=jax.ShapeDtypeStruct((SEQ, MEM_WIDTH), BF16),
        compiler_params=_params(("parallel", "parallel"), 16 * tq * N_MEM * 4),
    )(proj, kv, kv, gq, gk)


def _mem_bwd(proj, kv, gq, gk, do, dproj, tq):
    nq = SEQ // tq

    def body(q_ref, k_ref, v_ref, gq_ref, gk_ref, do_ref, dproj_in, dq_ref, dk_ref, dv_ref, dgq_ref, dgk_ref, dkn_scr):
        del dproj_in
        h, i = pl.program_id(0), pl.program_id(1)
        gqv, gkv = gq_ref[...], gk_ref[...]
        khat, rk, kn = _da_rmsnorm(k_ref[...], gkv)
        v = v_ref[...]

        def piece(rs):
            p, qhat, rq, qn = _mem_probs(q_ref[rs, :], kn, gqv)
            dob = do_ref[rs, :]
            dp = _dot(dob, v, _NT)
            ds = p * (dp - jnp.sum(p * dp, axis=-1, keepdims=True)) * MEM_SCALE
            dqn = _dot(ds, kn, _NN)
            dyg = dqn * gqv
            dq = (rq * (dyg - qhat * jnp.mean(dyg * qhat, axis=-1, keepdims=True))).astype(BF16)
            return dq, _dot(p, dob, _TN), _dot(ds, qn, _TN), jnp.sum(dqn * qhat, axis=0, keepdims=True)

        pieces = _mem_pieces(tq)
        results = [piece(rs) for rs in pieces]
        for rs, res in zip(pieces, results):
            dq_ref[rs, :] = res[0]
        dvp = sum((res[1] for res in results[1:]), results[0][1])
        dknp = sum((res[2] for res in results[1:]), results[0][2])
        dgq_part = sum((res[3] for res in results[1:]), results[0][3])
        first = jnp.logical_and(h == 0, i == 0)

        @pl.when(first)
        def _():
            dgq_ref[...] = dgq_part

        @pl.when(jnp.logical_not(first))
        def _():
            dgq_ref[...] += dgq_part

        @pl.when(i == 0)
        def _():
            dv_ref[...] = dvp
            dkn_scr[...] = dknp

        @pl.when(i > 0)
        def _():
            dv_ref[...] += dvp
            dkn_scr[...] += dknp

        @pl.when(i == nq - 1)
        def _():
            dkn = dkn_scr[...]
            dkg = dkn * gkv
            dk_ref[...] = rk * (dkg - khat * jnp.mean(dkg * khat, axis=-1, keepdims=True))
            dgk_part = jnp.sum(dkn * khat, axis=0, keepdims=True)

            @pl.when(h == 0)
            def _():
                dgk_ref[...] = dgk_part

            @pl.when(h > 0)
            def _():
                dgk_ref[...] += dgk_part

    gain_spec = pl.BlockSpec((1, HEAD_DIM), lambda h, i: (0, 0))
    kvout = pl.BlockSpec((N_MEM, HEAD_DIM), lambda h, i: (0, h))
    return pl.pallas_call(
        body, name="mem_bwd", grid=(MEM_HEADS, nq),
        in_specs=[pl.BlockSpec((tq, HEAD_DIM), lambda h, i: (i, CB_MEM_Q + h)),
                  pl.BlockSpec((N_MEM, HEAD_DIM), lambda h, i: (0, h)),
                  pl.BlockSpec((N_MEM, HEAD_DIM), lambda h, i: (0, MEM_HEADS + h)), gain_spec, gain_spec,
                  pl.BlockSpec((tq, HEAD_DIM), lambda h, i: (i, h)), _any_spec()],
        out_specs=[pl.BlockSpec((tq, HEAD_DIM), lambda h, i: (i, CB_MEM_Q + h)), kvout, kvout, gain_spec, gain_spec],
        out_shape=[jax.ShapeDtypeStruct((SEQ, IN_COLS), BF16), jax.ShapeDtypeStruct((N_MEM, MEM_WIDTH), F32),
                   jax.ShapeDtypeStruct((N_MEM, MEM_WIDTH), F32), jax.ShapeDtypeStruct((1, HEAD_DIM), F32),
                   jax.ShapeDtypeStruct((1, HEAD_DIM), F32)],
        scratch_shapes=[pltpu.VMEM((N_MEM, HEAD_DIM), F32)], input_output_aliases={6: 0},
        compiler_params=_params(("arbitrary", "arbitrary"), 24 * tq * N_MEM * 4),
    )(proj, kv, kv, gq, gk, do, dproj)


def _mesh_position():
    return lax.axis_index("x"), lax.axis_index("y"), lax.axis_index("c")


def _any_spec():
    return pl.BlockSpec(memory_space=pl.ANY)


def _all_gather(shards):
    n = len(shards)

    def body(*refs):
        ins, outs = refs[:n], refs[n:2 * n]
        send_sems, recv_sems, local_sems = refs[2 * n:]
        x, y, c = _mesh_position()
        me, sibling = (x, y, c), (x, y, 1 - c)
        first = (jnp.where(c == 0, 1 - x, x), jnp.where(c == 0, y, 1 - y), c)
        second = (jnp.where(c == 0, x, 1 - x), jnp.where(c == 0, 1 - y, y), c)
        diagonal = (1 - x, 1 - y, c)

        def copy(w, k, block, to, src=None):
            px, py, pc = block
            rows = outs[w].at[4 * px + 2 * py + pc]
            return pltpu.make_async_remote_copy(
                src_ref=rows if src is None else src, dst_ref=rows,
                send_sem=send_sems.at[7 * w + k], recv_sem=recv_sems.at[7 * w + k],
                device_id=to, device_id_type=MESH)

        started = []
        for w in range(n):
            mine = pltpu.make_async_copy(ins[w], outs[w].at[4 * x + 2 * y + c], local_sems.at[w])
            mine.start()
            started.append(mine)
        sends = []
        for w in range(n):
            own = [copy(w, 0, me, sibling, src=ins[w]), copy(w, 1, me, first, src=ins[w]),
                   copy(w, 2, me, second, src=ins[w])]
            for cp in own:
                cp.start()
            sends += own
        for w in range(n):
            copy(w, 1, first, me).wait_recv()
            follow = [copy(w, 3, first, second), copy(w, 4, first, sibling)]
            for cp in follow:
                cp.start()
            copy(w, 2, second, me).wait_recv()
            follow.append(copy(w, 5, second, sibling))
            follow[-1].start()
            sends += follow
        for w in range(n):
            copy(w, 3, diagonal, me).wait_recv()
            passed = copy(w, 6, diagonal, sibling)
            passed.start()
            sends.append(passed)
        for w in range(n):
            for k in (0, 4, 5, 6):
                copy(w, k, sibling, me).wait_recv()
        for cp in sends:
            cp.wait_send()
        for mine in started:
            mine.wait()

    return pl.pallas_call(
        body, name="weights_all_gather",
        in_specs=[_any_spec()] * n, out_specs=[_any_spec()] * n,
        out_shape=[jax.ShapeDtypeStruct((N_DEV,) + s.shape, s.dtype) for s in shards],
        scratch_shapes=[pltpu.SemaphoreType.DMA((7 * n,)), pltpu.SemaphoreType.DMA((7 * n,)),
                        pltpu.SemaphoreType.DMA((n,))],
    )(*shards)


def _chip_of(j, x, y):
    return (1 - x if j & 1 else x, 1 - y if j & 2 else y)


_HBM_SPEC = pl.BlockSpec(memory_space=pltpu.HBM)
_SEM_SPEC = pl.BlockSpec(memory_space=pltpu.SEMAPHORE)
_DATAFLOW_EFFECT = pltpu.SideEffectType.DATAFLOW_SIDE_EFFECTING
TOKEN_SHAPE = (8, D_MODEL)


def _copies_start(name, arrays, n_copies, plan):
    n = len(arrays)

    def body(*refs):
        send_sems, recv_sems, token = refs[n], refs[n + 1], refs[2 * n + 2]
        copies = plan(refs[:n])
        assert len(copies) == n_copies
        for k, (src, dst, dev) in enumerate(copies):
            pltpu.make_async_remote_copy(src_ref=src, dst_ref=dst, send_sem=send_sems.at[k], recv_sem=recv_sems.at[k],
                                         device_id=dev, device_id_type=MESH).start()
        token[...] = jnp.zeros_like(token)

    outs = pl.pallas_call(
        body, name=name,
        out_shape=(pltpu.SemaphoreType.DMA((n_copies,)), pltpu.SemaphoreType.DMA((n_copies,)),
                   *[pltpu.HBM(a.shape, a.dtype) for a in arrays], jax.ShapeDtypeStruct(TOKEN_SHAPE, F32)),
        in_specs=[_HBM_SPEC] * n,
        out_specs=(_SEM_SPEC, _SEM_SPEC, *[_HBM_SPEC] * n, pl.BlockSpec(memory_space=pltpu.VMEM)),
        input_output_aliases={i: i + 2 for i in range(n)},
        compiler_params=pltpu.CompilerParams(has_side_effects=_DATAFLOW_EFFECT),
    )(*[pltpu.with_memory_space_constraint(a, pltpu.HBM) for a in arrays])
    return outs[0], outs[1], list(outs[2:2 + n]), outs[2 + n]


def _copies_wait(name, send_sems, recv_sems, arrays, n_copies, plan, after):
    n = len(arrays)
    after = list(after) if isinstance(after, (list, tuple)) else [after]

    def body(*refs):
        send_ref, recv_ref = refs[n], refs[n + 1]
        copies = plan(refs[:n])
        assert len(copies) == n_copies
        for k, (src, dst, dev) in enumerate(copies):
            cp = pltpu.make_async_remote_copy(src_ref=src, dst_ref=dst, send_sem=send_ref.at[k], recv_sem=recv_ref.at[k],
                                              device_id=dev, device_id_type=MESH)
            cp.wait_send()
            cp.wait_recv()

    outs = pl.pallas_call(
        body, name=name, out_shape=tuple(pltpu.HBM(a.shape, a.dtype) for a in arrays),
        in_specs=[_HBM_SPEC] * n + [_SEM_SPEC, _SEM_SPEC] + [pl.BlockSpec(memory_space=pl.ANY)] * len(after),
        out_specs=tuple([_HBM_SPEC] * n), input_output_aliases={i: i for i in range(n)},
        compiler_params=pltpu.CompilerParams(has_side_effects=_DATAFLOW_EFFECT),
    )(*arrays, send_sems, recv_sems, *after)
    return list(outs)


def _after(small, token):
    return small if token is None else small + token[0:1, :small.shape[-1]]


def _gather_plan_out(n):
    def plan(refs):
        x, y, c = _mesh_position()
        me = 4 * x + 2 * y + c
        copies = []
        for w in range(n):
            land = refs[n + w].at[me]
            copies.append((refs[w], land, (x, y, 1 - c)))
            for j in range(1, 4):
                copies.append((refs[w], land, (*_chip_of(j, x, y), c)))
        return copies
    return plan


def _gather_plan_pass(n):
    def plan(refs):
        x, y, c = _mesh_position()
        copies = []
        for w in range(n):
            for j in range(1, 4):
                px, py = _chip_of(j, x, y)
                rows = refs[w].at[4 * px + 2 * py + c]
                copies.append((rows, rows, (x, y, 1 - c)))
        return copies
    return plan


def _reduce_plan_sibling(n):
    def plan(refs):
        x, y, c = _mesh_position()
        copies = []
        for w in range(n):
            for j in range(4):
                px, py = _chip_of(j, x, y)
                copies.append((refs[w].at[4 * px + 2 * py + (1 - c)], refs[n + w].at[j], (x, y, 1 - c)))
        return copies
    return plan


def _reduce_plan_chips(n):
    def plan(refs):
        x, y, c = _mesh_position()
        copies = []
        for w in range(n):
            for j in range(1, 4):
                copies.append((refs[w].at[j - 1], refs[n + w].at[j - 1], (*_chip_of(j, x, y), c)))
        return copies
    return plan


def _chip_partials(grad, recv, name, tr):
    _, rows, width = grad.shape

    def body(g_ref, r_ref, own_ref, other_ref):
        x, y, c = _mesh_position()
        for j in range(4):
            px, py = _chip_of(j, x, y)
            total = g_ref[4 * px + 2 * py + c].astype(F32) + r_ref[j].astype(F32)
            if j == 0:
                own_ref[...] = total
            else:
                other_ref[j - 1] = total.astype(BF16)

    return pl.pallas_call(
        body, name=name, grid=(rows // tr,),
        in_specs=[pl.BlockSpec((N_DEV, tr, width), lambda i: (0, i, 0)),
                  pl.BlockSpec((4, tr, width), lambda i: (0, i, 0))],
        out_specs=[pl.BlockSpec((tr, width), lambda i: (i, 0)), pl.BlockSpec((3, tr, width), lambda i: (0, i, 0))],
        out_shape=[jax.ShapeDtypeStruct((rows, width), F32), jax.ShapeDtypeStruct((3, rows, width), BF16)],
        compiler_params=_params(("parallel",), 2 * 20 * tr * width * 2 + 8 * tr * width * 4),
    )(grad, recv)


def _adamw_math(w, g, m, v):
    m = ADAM_B1 * m + (1.0 - ADAM_B1) * g
    v = ADAM_B2 * v + (1.0 - ADAM_B2) * (g * g)
    m_hat = m / (1.0 - ADAM_B1 ** ADAM_STEP)
    v_hat = v / (1.0 - ADAM_B2 ** ADAM_STEP)
    delta = -ADAM_LR * (m_hat / (jnp.sqrt(v_hat) + ADAM_EPS) + ADAM_WD * w)
    return delta, m, v


def _adamw_shard(own, recv, w, m, v, name, tr):
    rows, width = own.shape

    def body(own_ref, r_ref, w_ref, m_ref, v_ref, g_ref, d_ref, nm_ref, nv_ref):
        g = own_ref[...]
        for j in range(3):
            g = g + r_ref[j].astype(F32)
        g_ref[...] = g
        d_ref[...], nm_ref[...], nv_ref[...] = _adamw_math(w_ref[...], g, m_ref[...], v_ref[...])

    spec = pl.BlockSpec((tr, width), lambda i: (i, 0))
    shape = jax.ShapeDtypeStruct((rows, width), F32)
    return pl.pallas_call(
        body, name=name, grid=(rows // tr,),
        in_specs=[spec, pl.BlockSpec((3, tr, width), lambda i: (0, i, 0)), spec, spec, spec],
        out_specs=[spec] * 4, out_shape=[shape] * 4,
        compiler_params=_params(("parallel",), 22 * tr * width * 4),
    )(own, recv, w, m, v)


def _small_all_reduce_adamw(gpart, lbpack, small_w, small_m, small_v, after):
    count = len(_SMALL_NAMES)
    shapes = [small_w[n].shape for n in _SMALL_NAMES]

    def row_map():
        for i, (name, (rows, cols)) in enumerate(zip(_SMALL_NAMES, shapes)):
            per = cols // LANE
            for s in range(rows):
                for j in range(per):
                    yield i, slice(s, s + 1), slice(j * LANE, (j + 1) * LANE), _SMALL_ROW0[name] + s * per + j

    def body(gp_ref, lb_ref, *refs):
        ins = [refs[k * count:(k + 1) * count] for k in range(3)]
        refs = refs[3 * count + 1:]
        outs = [refs[k * count:(k + 1) * count] for k in range(4)]
        loss_ref = refs[4 * count]
        packs = refs[4 * count + 1:4 * count + 5]
        gath_ref, send_sems, recv_sems = refs[4 * count + 5:]
        for k in range(3):
            packs[k][...] = jnp.zeros_like(packs[k])
            for i, row, lanes, r in row_map():
                packs[k][r:r + 1, :] = ins[k][i][row, lanes]
        x, y, c = _mesh_position()
        me = 4 * x + 2 * y + c
        gath_ref[me] = gp_ref[...]
        copies = []
        for j in range(1, N_DEV):
            peer = (x ^ (j >> 2), y ^ ((j >> 1) & 1), c ^ (j & 1))
            cp = pltpu.make_async_remote_copy(
                src_ref=gp_ref, dst_ref=gath_ref.at[me], send_sem=send_sems.at[j - 1], recv_sem=recv_sems.at[j - 1],
                device_id=peer, device_id_type=MESH)
            cp.start()
            copies.append(cp)
        for cp in copies:
            cp.wait()
        tot = gath_ref[0]
        for s in range(1, N_DEV):
            tot = tot + gath_ref[s]
        lb = lb_ref[...]
        dl = tot[16:32, :] * lb * (1.0 - lb)
        g = jnp.concatenate([tot[0:16, :], dl[0:8, :], -dl[0:8, :], dl[8:16, :], -dl[8:16, :],
                             tot[32:SMALL_GRAD_ROWS, :]], axis=0)
        results = (g,) + tuple(_adamw_math(packs[0][...], g, packs[1][...], packs[2][...]))
        loss_ref[...] = g[_LOSS_ROW:_LOSS_ROW + 8, :]
        for k, result in enumerate(results):
            packs[k][...] = result
            for i, row, lanes, r in row_map():
                outs[k][i][row, lanes] = packs[k][r:r + 1, :]

    vm = pl.BlockSpec(memory_space=pltpu.VMEM)
    flat = pl.pallas_call(
        body, name="small_all_reduce_adamw", in_specs=[vm] * (2 + 3 * count) + [_any_spec()],
        out_specs=[vm] * (4 * count + 1),
        out_shape=[jax.ShapeDtypeStruct(s, F32) for s in shapes] * 4 + [jax.ShapeDtypeStruct((8, LANE), F32)],
        scratch_shapes=[pltpu.VMEM((SMALL_ROWS, LANE), F32)] * 4
        + [pltpu.VMEM((N_DEV, SMALL_GRAD_ROWS, LANE), F32),
           pltpu.SemaphoreType.DMA((N_DEV - 1,)), pltpu.SemaphoreType.DMA((N_DEV - 1,))],
    )(gpart, lbpack, *[p[n] for p in (small_w, small_m, small_v) for n in _SMALL_NAMES], after)
    return flat[4 * count][0, 0], [dict(zip(_SMALL_NAMES, flat[k * count:(k + 1) * count])) for k in range(4)]


_SMALL_NAMES = ("norm_mix_gain", "norm_mem_gain", "lb_logits_fw", "lb_logits_bw", "norm_ffn_gain",
                "hg_norm_gain", "da_q_gain", "da_k_gain", "mem_q_gain", "mem_k_gain")
_SMALL_ROW0 = {"norm_mix_gain": 0, "norm_mem_gain": 8, "lb_logits_fw": 16, "lb_logits_bw": 32, "norm_ffn_gain": 48,
               "hg_norm_gain": 56, "da_q_gain": 64, "da_k_gain": 72, "mem_q_gain": 80, "mem_k_gain": 88}
_LOSS_ROW = 96


def _pack_rows(parts, total_rows):
    rows = []
    for p in parts:
        r = p.reshape(-1, LANE)
        rows.append(jnp.pad(r, ((0, -r.shape[0] % 8), (0, 0))))
    used = sum(r.shape[0] for r in rows)
    if total_rows > used:
        rows.append(jnp.zeros((total_rows - used, LANE), F32))
    return jnp.concatenate(rows, axis=0)


def kernel(x, mem, norm_mix_gain, norm_mem_gain, w_in, lb_logits_fw, lb_logits_bw, hg_norm_gain, da_q_gain, da_k_gain, w_mem_kv, mem_q_gain, mem_k_gain, w_proj_hg, w_proj_da, w_proj_mem, w_out, norm_ffn_gain, w_ffn_in, w_ffn_out, loss_target, m_norm_mix_gain, m_norm_mem_gain, m_w_in, m_lb_logits_fw, m_lb_logits_bw, m_hg_norm_gain, m_da_q_gain, m_da_k_gain, m_w_mem_kv, m_mem_q_gain, m_mem_k_gain, m_w_proj_hg, m_w_proj_da, m_w_proj_mem, m_w_out, m_norm_ffn_gain, m_w_ffn_in, m_w_ffn_out, v_norm_mix_gain, v_norm_mem_gain, v_w_in, v_lb_logits_fw, v_lb_logits_bw, v_hg_norm_gain, v_da_q_gain, v_da_k_gain, v_w_mem_kv, v_mem_q_gain, v_mem_k_gain, v_w_proj_hg, v_w_proj_da, v_w_proj_mem, v_w_out, v_norm_ffn_gain, v_w_ffn_in, v_w_ffn_out):
    small_w = dict(norm_mix_gain=norm_mix_gain, norm_mem_gain=norm_mem_gain, lb_logits_fw=lb_logits_fw,
                   lb_logits_bw=lb_logits_bw, norm_ffn_gain=norm_ffn_gain, hg_norm_gain=hg_norm_gain,
                   da_q_gain=da_q_gain, da_k_gain=da_k_gain, mem_q_gain=mem_q_gain, mem_k_gain=mem_k_gain)
    small_m = dict(norm_mix_gain=m_norm_mix_gain, norm_mem_gain=m_norm_mem_gain, lb_logits_fw=m_lb_logits_fw,
                   lb_logits_bw=m_lb_logits_bw, norm_ffn_gain=m_norm_ffn_gain, hg_norm_gain=m_hg_norm_gain,
                   da_q_gain=m_da_q_gain, da_k_gain=m_da_k_gain, mem_q_gain=m_mem_q_gain, mem_k_gain=m_mem_k_gain)
    small_v = dict(norm_mix_gain=v_norm_mix_gain, norm_mem_gain=v_norm_mem_gain, lb_logits_fw=v_lb_logits_fw,
                   lb_logits_bw=v_lb_logits_bw, norm_ffn_gain=v_norm_ffn_gain, hg_norm_gain=v_hg_norm_gain,
                   da_q_gain=v_da_q_gain, da_k_gain=v_da_k_gain, mem_q_gain=v_mem_q_gain, mem_k_gain=v_mem_k_gain)
    big_w = dict(w_in=w_in[0], w_mem_kv=w_mem_kv[0], w_proj_hg=w_proj_hg[0], w_proj_da=w_proj_da[0],
                 w_proj_mem=w_proj_mem[0], w_out=w_out[0], w_ffn_in=w_ffn_in[0], w_ffn_out=w_ffn_out[0])
    big_m = dict(w_in=m_w_in[0], w_mem_kv=m_w_mem_kv[0], w_proj_hg=m_w_proj_hg[0], w_proj_da=m_w_proj_da[0],
                 w_proj_mem=m_w_proj_mem[0], w_out=m_w_out[0], w_ffn_in=m_w_ffn_in[0], w_ffn_out=m_w_ffn_out[0])
    big_v = dict(w_in=v_w_in[0], w_mem_kv=v_w_mem_kv[0], w_proj_hg=v_w_proj_hg[0], w_proj_da=v_w_proj_da[0],
                 w_proj_mem=v_w_proj_mem[0], w_out=v_w_out[0], w_ffn_in=v_w_ffn_in[0], w_ffn_out=v_w_ffn_out[0])

    row_tile = dict(w_in=128, w_mem_kv=128, w_proj_hg=128, w_proj_da=512, w_proj_mem=512, w_out=128,
                    w_ffn_in=128, w_ffn_out=352)
    rest = _BIG_NAMES[1:]
    shards = [big_w[n].astype(BF16) for n in rest]
    state = {}
    big_out = {}

    def reduce_start(group, stacked):
        names = tuple(stacked)
        arrays = [stacked[n] for n in names] + [lax.empty((4,) + stacked[n].shape[1:], BF16) for n in names]
        plan = _reduce_plan_sibling(len(names))
        send, recv, thru, token = _copies_start(f"grads_{group}_sibling_start", arrays, 4 * len(names), plan)
        state[group] = dict(names=names, plan=plan, send=send, recv=recv, arrays=thru)
        return token

    def reduce_middle(group, after):
        st = state[group]
        names, k = st["names"], len(st["names"])
        thru = _copies_wait(f"grads_{group}_sibling_wait", st["send"], st["recv"], st["arrays"], 4 * k, st["plan"], after)
        partials = [_chip_partials(thru[i], thru[k + i], f"chip_partials_{n}", row_tile[n]) for i, n in enumerate(names)]
        arrays = [p[1] for p in partials] + [lax.empty(p[1].shape, BF16) for p in partials]
        plan = _reduce_plan_chips(k)
        send, recv, thru2, token = _copies_start(f"grads_{group}_chips_start", arrays, 3 * k, plan)
        state[group] = dict(names=names, plan=plan, send=send, recv=recv, arrays=thru2, own=[p[0] for p in partials])
        return token

    def reduce_finish(group, after):
        st = state.pop(group)
        names, k = st["names"], len(st["names"])
        thru = _copies_wait(f"grads_{group}_chips_wait", st["send"], st["recv"], st["arrays"], 3 * k, st["plan"], after)
        for i, n in enumerate(names):
            big_out[n] = _adamw_shard(st["own"][i], thru[k + i], big_w[n], big_m[n], big_v[n], "adamw_" + n, row_tile[n])

    step = _local_step_stages(x[0], mem[0], loss_target[0], small_w)
    event, payload = next(step)
    local = None
    while True:
        reply = None
        if event == "gather_w_in":
            reply = _all_gather([big_w["w_in"].astype(BF16)])[0]
        elif event == "begin":
            nr = len(rest)
            me = 4 * lax.axis_index("x") + 2 * lax.axis_index("y") + lax.axis_index("c")
            arrays = shards + [lax.dynamic_update_slice(lax.empty((N_DEV,) + s.shape, BF16), s[None], (me, 0, 0))
                               for s in shards]
            plan = _gather_plan_out(nr)
            send, recv, thru, reply = _copies_start("weights_rest_out_start", arrays, 4 * nr, plan)
            state["gather"] = dict(plan=plan, send=send, recv=recv, arrays=thru)
        elif event == "after_gla_fwd":
            st = state["gather"]
            nr = len(rest)
            thru = _copies_wait("weights_rest_out_wait", st["send"], st["recv"], st["arrays"], 4 * nr, st["plan"], payload)
            plan = _gather_plan_pass(nr)
            send, recv, lands, reply = _copies_start("weights_rest_pass_start", thru[nr:], 3 * nr, plan)
            state["gather"] = dict(plan=plan, send=send, recv=recv, arrays=lands)
        elif event == "need_weights":
            st = state.pop("gather")
            nr = len(rest)
            lands = _copies_wait("weights_rest_pass_wait", st["send"], st["recv"], st["arrays"], 3 * nr, st["plan"], payload)
            reply = dict(zip(rest, lands))
        elif event == "grads_ffn":
            reply = reduce_start("ffn", payload)
        elif event == "after_gate_merge_bwd":
            reply = reduce_middle("ffn", payload)
        elif event == "grads_mix":
            reply = reduce_start("mix", payload)
        elif event == "after_da_bwd_g0":
            reply = reduce_middle("mix", payload)
            reduce_finish("ffn", payload)
        elif event == "after_gla_bwd":
            reduce_finish("mix", payload)
        elif event == "grads_in":
            reply = reduce_start("in", payload)
        elif event == "after_proj_bwd_act_top":
            reply = reduce_middle("in", payload)
        elif event == "end":
            local = payload
            reduce_finish("in", [local["grad_x"]] + [big_out[n][0] for n in rest])
            break
        event, payload = step.send(reply)
    grad_x = local["grad_x"]

    loss, small_out = _small_all_reduce_adamw(local["gpart"], local["lbpack"], small_w, small_m, small_v,
                                              big_out["w_in"][0])

    order = ("norm_mix_gain", "norm_mem_gain", "w_in", "lb_logits_fw", "lb_logits_bw", "hg_norm_gain", "da_q_gain",
             "da_k_gain", "w_mem_kv", "mem_q_gain", "mem_k_gain", "w_proj_hg", "w_proj_da", "w_proj_mem", "w_out",
             "norm_ffn_gain", "w_ffn_in", "w_ffn_out")
    outs = [loss, grad_x[None]]
    for kind in range(4):
        for n in order:
            outs.append(big_out[n][kind][None] if n in big_out else small_out[kind][n])
    return tuple(outs)


_BIG_NAMES = ("w_in", "w_mem_kv", "w_proj_hg", "w_proj_da", "w_proj_mem", "w_out", "w_ffn_in", "w_ffn_out")


def _local_step(xs, mems, target, sw, wg):
    step = _local_step_stages(xs, mems, target, sw)
    stacked = {}
    event, payload = next(step)
    while event != "end":
        reply = None
        if event.startswith("grads_"):
            stacked.update(payload)
        elif event == "need_weights":
            reply = wg
        elif event == "gather_w_in":
            reply = wg["w_in"]
        event, payload = step.send(reply)
    return dict(payload, stacked=stacked)


def _local_step_stages(xs, mems, target, sw):
    norm_mix_gain, norm_mem_gain, norm_ffn_gain = sw["norm_mix_gain"], sw["norm_mem_gain"], sw["norm_ffn_gain"]
    lb_logits_fw, lb_logits_bw, hg_norm_gain = sw["lb_logits_fw"], sw["lb_logits_bw"], sw["hg_norm_gain"]
    da_q_gain, da_k_gain, mem_q_gain, mem_k_gain = sw["da_q_gain"], sw["da_k_gain"], sw["mem_q_gain"], sw["mem_k_gain"]

    win_st = yield "gather_w_in", None
    token = yield "begin", None
    lb_fw = _lb_table(lb_logits_fw, "lb_table_fw")
    lb_bw = _lb_table(lb_logits_bw, "lb_table_bw")
    lb = jnp.concatenate([lb_fw, lb_bw], axis=0).reshape(2, HG_HEADS, 1, HEAD_DIM)
    h, h_t = _rmsnorm_fwd(xs, _after(norm_mix_gain, token), "norm_mix_fwd", 512, transposed=True)
    proj = _matmul(h, win_st, "nn", F32, 1024, IN_SHARD, D_MODEL, "proj_fwd", b_stacked=True, n_outer=True)
    hg_b = _gla_decay(proj, lb)
    o_hg, o_pre, states, hg_qt, hg_decay = _gla_fwd(proj, lb, hg_norm_gain, hg_b)
    token = yield "after_gla_fwd", o_hg
    da01 = [_da_fwd(proj, _after(da_q_gain, token), da_k_gain, g) for g in (0, 1)]
    o_da, o_da32, lse_da = _da_fwd(proj, _after(da_q_gain, token), da_k_gain, 2, others=da01)
    wg = yield "need_weights", o_da
    wkv = wg["w_mem_kv"].reshape(D_MODEL, 2 * MEM_WIDTH)
    wphg = wg["w_proj_hg"].reshape(D_MODEL, D_MODEL)
    wpda = jnp.transpose(wg["w_proj_da"], (1, 0, 2)).reshape(DA_WIDTH, D_MODEL)
    wpmem = jnp.transpose(wg["w_proj_mem"], (1, 0, 2)).reshape(MEM_WIDTH, D_MODEL)
    wout = wg["w_out"].reshape(D_MODEL, D_MODEL)
    wfin = jnp.transpose(wg["w_ffn_in"], (1, 0, 2)).reshape(D_MODEL, 2 * D_FF)
    wfout = wg["w_ffn_out"].reshape(D_FF, D_MODEL)
    mem_n = _rmsnorm_fwd(mems, norm_mem_gain, "norm_mem_fwd", N_MEM)
    kv = _matmul(mem_n, wkv, "nn", F32, N_MEM, 1024, D_MODEL, "mem_kv_fwd")
    o_mem = _mem_fwd(proj, kv, mem_q_gain, mem_k_gain, 1024)
    branch_w = (wphg, wpda, wpmem)
    merged, t_hg, t_da, t_mem = _branch_merge_fwd(proj, (o_hg, o_da, o_mem), branch_w, 512)
    x1, h2, h2_t = _residual_rmsnorm_fwd(xs, merged, wout, norm_ffn_gain, "out_norm_ffn_fwd", 512)
    ffn_a, ffn_b, act, act_t = _ffn_in_swiglu(h2, wfin, 1024, 1408)
    dy, dyb, loss_part = _ffn_out_loss_head(x1, act, wfout, target, 512)

    dab = _ffn_out_bwd_swiglu(dyb, wfout, ffn_a, ffn_b, 1024, 1408)
    g_wfout = _matmul(act_t, dyb, "nn", BF16, 1408, 1024, 2048, "ffn_out_bwd_w")
    g_wfin = _matmul(h2_t, dab, "nn", BF16, 1024, 1408, 2048, "ffn_in_bwd_w", b_parts=True)
    token = yield "grads_ffn", dict(
        w_ffn_in=jnp.transpose(g_wfin.reshape(D_MODEL, N_DEV, 2 * D_FF // N_DEV), (1, 0, 2)),
        w_ffn_out=g_wfout.reshape(N_DEV, D_FF // N_DEV, D_MODEL))
    dx1, dx1b, g_norm_ffn = _matmul_rmsnorm_bwd(dab, wfin, x1, dy, _after(norm_ffn_gain, token),
                                                "ffn_in_bwd_act_norm", 512, D_FF, a_parts=True)
    g_wout = _matmul(merged, dx1b, "tn", BF16, 1024, 1024, 2048, "out_bwd_w")
    dt_hg, dt_da, dt_mem, do_pre, do_da, do_mem, dd_da, g_hg_norm, dproj = _gate_merge_bwd(
        proj, (t_hg, t_da, t_mem), branch_w, dx1b, wout, o_pre, hg_norm_gain, o_da32, _dproj_buffer(), 256)
    token = yield "after_gate_merge_bwd", dt_hg
    g_wphg = _matmul(o_hg, dt_hg, "tn", BF16, 1024, 1024, 2048, "proj_hg_bwd_w", after=token)
    g_wpda = _matmul(o_da, dt_da, "tn", BF16, DA_WIDTH, D_MODEL, 2048, "proj_da_bwd_w")
    g_wpmem = _matmul(o_mem, dt_mem, "tn", BF16, MEM_WIDTH, D_MODEL, 2048, "proj_mem_bwd_w")
    by_owner = lambda g: jnp.transpose(g.reshape(g.shape[0], N_DEV, D_MODEL // N_DEV), (1, 0, 2))
    g_wpda, g_wpmem = by_owner(g_wpda), by_owner(g_wpmem)

    dproj, dk_mem, dv_mem, g_mem_q, g_mem_k = _mem_bwd(proj, kv, mem_q_gain, mem_k_gain, do_mem, dproj, 1024)
    dkv = jnp.concatenate([dk_mem, dv_mem], axis=1).astype(BF16)
    g_wkv = _matmul(mem_n, dkv, "tn", BF16, 1024, 1024, N_MEM, "mem_kv_bwd_w")
    dmem_n = _matmul(dkv, wkv, "nt", F32, N_MEM, 1024, 1024, "mem_kv_bwd_act")
    g_norm_mem = _gain_grad(mems, dmem_n, "norm_mem_bwd")
    token = yield "grads_mix", dict(
        w_mem_kv=g_wkv.reshape(N_DEV, D_MODEL // N_DEV, 2 * MEM_WIDTH),
        w_proj_hg=g_wphg.reshape(N_DEV, D_MODEL // N_DEV, D_MODEL),
        w_proj_da=g_wpda, w_proj_mem=g_wpmem,
        w_out=g_wout.reshape(N_DEV, D_MODEL // N_DEV, D_MODEL))

    dproj, g_da_q, g_da_k = _da_bwd(proj, _after(da_q_gain, token), da_k_gain, do_da, lse_da, dd_da, dproj, 0)
    token = yield "after_da_bwd_g0", g_da_q
    for g in (1, 2):
        dproj, gq_part, gk_part = _da_bwd(proj, _after(da_q_gain, token), da_k_gain, do_da, lse_da, dd_da, dproj, g)
        g_da_q, g_da_k = g_da_q + gq_part, g_da_k + gk_part

    dproj, dlb = _gla_bwd(proj, lb, states, hg_qt, hg_decay, hg_b, do_pre, dproj)
    yield "after_gla_bwd", dlb

    g_win =_matmul(h_t, dproj, "nn", BF16, 1024, IN_SHARD, 2048, "proj_bwd_w", out_stacked=True)
    token = yield "grads_in", dict(w_in=g_win)
    grad_x, g_mix_top = _matmul_rmsnorm_bwd(dproj, win_st, xs, dx1, norm_mix_gain, "proj_bwd_act_norm_top", 1024,
                                            IN_SHARD, b_stacked=True, after=token, m_blocks=(0, 1), with_bf16=False)
    token = yield "after_proj_bwd_act_top", g_mix_top
    grad_x, g_mix_bottom = _matmul_rmsnorm_bwd(dproj, win_st, xs, dx1, norm_mix_gain, "proj_bwd_act_norm_bottom", 1024,
                                               IN_SHARD, b_stacked=True, after=token, m_blocks=(1, 3),
                                               out_into=grad_x, with_bf16=False)
    g_norm_mix = g_mix_top + g_mix_bottom

    gpart = _pack_rows([g_norm_mix, g_norm_mem, dlb[0], dlb[1], g_norm_ffn, g_hg_norm, g_da_q, g_da_k,
                        g_mem_q, g_mem_k, loss_part], SMALL_GRAD_ROWS)
    lbpack = jnp.concatenate([lb_fw.reshape(8, LANE), lb_bw.reshape(8, LANE)], axis=0)
    yield "end", dict(grad_x=grad_x, gpart=gpart, lbpack=lbpack)
```

```python
import numpy as np
import jax
import jax.numpy as jnp
from jax import lax
from jax.experimental import pallas as pl
from jax.experimental.pallas import tpu as pltpu

F32 = jnp.float32
BF16 = jnp.bfloat16
MESH = pl.DeviceIdType.MESH

SEQ = 4096
D_MODEL = 1024
N_DEV = 8
N_MEM = 256
RMS_EPS = 1e-6
NEG_INF = -1e30
LANE = 128
HEAD_DIM = 128
HG_HEADS = 8
HG_CHUNK = 64
HG_SCALE = HEAD_DIM ** -0.5
DA_DILATIONS = (1, 4, 16)
DA_RADIUS = 64
DA_HEADS_PER_GROUP = 4
DA_HEADS = 12
DA_WIDTH = 512
DA_SCALE = HEAD_DIM ** -0.5
DA_QB = 128
DA_WIN = 256
DA_WAYS = 8
DA_FWD_WAYS = 8
MEM_HEADS = 4
MEM_WIDTH = 512
MEM_SCALE = HEAD_DIM ** -0.5
D_FF = 2816
IN_COLS = 13312
IN_SHARD = IN_COLS // N_DEV
CB_HG_Q, CB_F, CB_HG_I, CB_HG_G = 0, 8, 24, 32
CB_DA_Q, CB_DA_K, CB_DA_V, CB_MEM_Q = 40, 52, 64, 76
ADAM_LR, ADAM_B1, ADAM_B2, ADAM_EPS, ADAM_WD, ADAM_STEP = 0.001, 0.9, 0.999, 1e-08, 0.01, 10
VMEM_BYTES_V7X = 64 * 1024 * 1024
SMALL_ROWS = 104
SMALL_GRAD_ROWS = 88

_NN = (((1,), (0,)), ((), ()))
_NT = (((1,), (1,)), ((), ()))
_TN = (((0,), (0,)), ((), ()))


def _dot(a, b, dims):
    return lax.dot_general(a.astype(BF16), b.astype(BF16), dims, preferred_element_type=F32)


def _sigmoid(x):
    return 0.5 * jnp.tanh(0.5 * x) + 0.5


def _params(semantics, est_bytes):
    limit = int(min(VMEM_BYTES_V7X - (6 << 20), max(56 << 20, est_bytes * 3 // 2)))
    return pltpu.CompilerParams(dimension_semantics=semantics, vmem_limit_bytes=limit)


def _nbytes(shape, dtype):
    return int(np.prod(shape)) * jnp.dtype(dtype).itemsize


def _alibi_slopes(n):
    return (2.0 ** (-8.0 * np.arange(1, n + 1) / n)).astype(np.float32)


def _matmul(a, b, mode, out_dtype, tm, tn, tk, name, b_stacked=False, out_stacked=False, n_outer=False, after=None,
            m_blocks=None, out_into=None, a_parts=False, b_parts=False):
    if a_parts:
        assert mode == "nt" and tk == a.shape[2]
        m, kdim = a.shape[1], a.shape[0] * a.shape[2]
    elif mode == "tn":
        kdim, m = a.shape
    else:
        m, kdim = a.shape
    if b_parts:
        assert mode == "nn" and not b_stacked and b.shape[2] % tn == 0
        n = b.shape[0] * b.shape[2]
    elif b_stacked:
        if mode == "nn":
            n = b.shape[0] * b.shape[2]
            assert tn == b.shape[2] and tk == kdim == b.shape[1]
        else:
            assert mode == "nt" and tk == b.shape[2] and b.shape[0] * tk == kdim
            n = b.shape[1]
    else:
        n = b.shape[0] if mode == "nt" else b.shape[1]
    assert m % tm == 0 and n % tn == 0 and kdim % tk == 0
    gm, gn, gk = m // tm, n // tn, kdim // tk
    i0 = 0
    if m_blocks is not None:
        assert mode != "tn" and not out_stacked
        i0, gm = m_blocks

    def ijk(f):
        if n_outer:
            return lambda j, i, k: f(i + i0, j, k)
        return lambda i, j, k: f(i + i0, j, k)

    if a_parts:
        a_spec = pl.BlockSpec((None, tm, tk), ijk(lambda i, j, k: (k, i, 0)))
    elif mode == "tn":
        a_spec = pl.BlockSpec((tk, tm), ijk(lambda i, j, k: (k, i)))
    else:
        a_spec = pl.BlockSpec((tm, tk), ijk(lambda i, j, k: (i, k)))
    if b_parts:
        per_part = b.shape[2] // tn
        b_spec = pl.BlockSpec((None, tk, tn), ijk(lambda i, j, k: (j // per_part, k, j % per_part)))
    elif b_stacked and mode == "nn":
        b_spec = pl.BlockSpec((None, tk, tn), ijk(lambda i, j, k: (j, 0, 0)))
    elif b_stacked:
        b_spec = pl.BlockSpec((None, tn, tk), ijk(lambda i, j, k: (k, j, 0)))
    elif mode == "nt":
        b_spec = pl.BlockSpec((tn, tk), ijk(lambda i, j, k: (j, k)))
    else:
        b_spec = pl.BlockSpec((tk, tn), ijk(lambda i, j, k: (k, j)))
    if out_stacked:
        assert tm == m
        out_shape = jax.ShapeDtypeStruct((gn, m, tn), out_dtype)
        o_spec = pl.BlockSpec((None, tm, tn), ijk(lambda i, j, k: (j, i, 0)))
    else:
        out_shape = jax.ShapeDtypeStruct((m, n), out_dtype)
        o_spec = pl.BlockSpec((tm, tn), ijk(lambda i, j, k: (i, j)))
    dims = {"nn": _NN, "nt": _NT, "tn": _TN}[mode]

    n_in = 2 + (after is not None) + (out_into is not None)

    def body(*refs):
        a_ref, b_ref, o_ref = refs[0], refs[1], refs[n_in]
        part = _dot(a_ref[...], b_ref[...], dims)
        if gk == 1:
            o_ref[...] = part.astype(out_dtype)
            return
        acc_ref = refs[-1]
        k = pl.program_id(2)

        @pl.when(k == 0)
        def _():
            acc_ref[...] = part

        @pl.when(jnp.logical_and(k > 0, k < gk - 1))
        def _():
            acc_ref[...] += part

        @pl.when(k == gk - 1)
        def _():
            o_ref[...] = (acc_ref[...] + part).astype(out_dtype)

    a_tile = _nbytes((tm, tk), a.dtype)
    b_tile = _nbytes((tk, tn), b.dtype)
    o_tile = _nbytes((tm, tn), out_dtype)
    est = 2 * (a_tile + b_tile + o_tile) + 3 * tm * tn * 4 + (a_tile + b_tile)
    grid = (gn, gm, gk) if n_outer else (gm, gn, gk)
    operands, in_specs = [a, b], [a_spec, b_spec]
    if after is not None:
        operands.append(after)
        in_specs.append(pl.BlockSpec(memory_space=pl.ANY))
    aliases = {}
    if out_into is not None:
        aliases = {len(operands): 0}
        operands.append(out_into)
        in_specs.append(pl.BlockSpec(memory_space=pl.ANY))
    return pl.pallas_call(
        body, name=name, grid=grid, in_specs=in_specs, out_specs=o_spec, out_shape=out_shape,
        scratch_shapes=[] if gk == 1 else [pltpu.VMEM((tm, tn), F32)], input_output_aliases=aliases,
        compiler_params=_params(("parallel", "parallel", "arbitrary"), est),
    )(*operands)


def _row_spec(tr, width, col_block=0):
    return pl.BlockSpec((tr, width), lambda i: (i, col_block))


def _bcast_spec(width):
    return pl.BlockSpec((1, width), lambda i: (0, 0))


def _col_spec(width, tr):
    return pl.BlockSpec((width, tr), lambda i: (0, i))


def _rmsnorm_fwd(x, gain, name, tr, transposed=False):
    rows, width = x.shape

    def body(x_ref, g_ref, o_ref, *t_ref):
        xv = x_ref[...]
        r = lax.rsqrt(jnp.mean(xv * xv, axis=-1, keepdims=True) + RMS_EPS)
        h = xv * r * g_ref[...]
        o_ref[...] = h.astype(BF16)
        if transposed:
            t_ref[0][...] = h.T.astype(BF16)

    out_specs, out_shape = [_row_spec(tr, width)], [jax.ShapeDtypeStruct((rows, width), BF16)]
    if transposed:
        out_specs.append(_col_spec(width, tr))
        out_shape.append(jax.ShapeDtypeStruct((width, rows), BF16))
    out = pl.pallas_call(
        body, name=name, grid=(rows // tr,), in_specs=[_row_spec(tr, width), _bcast_spec(width)],
        out_specs=out_specs, out_shape=out_shape,
        compiler_params=_params(("parallel",), 10 * tr * width * 4),
    )(x, gain)
    return out if transposed else out[0]


def _residual_rmsnorm_fwd(x, merged, wout, gain, name, tr):
    rows, width = x.shape

    def body(x_ref, m_ref, w_ref, g_ref, x1_ref, h_ref, ht_ref):
        xv = x_ref[...] + _dot(m_ref[...], w_ref[...], _NN)
        x1_ref[...] = xv
        r = lax.rsqrt(jnp.mean(xv * xv, axis=-1, keepdims=True) + RMS_EPS)
        h = xv * r * g_ref[...]
        h_ref[...] = h.astype(BF16)
        ht_ref[...] = h.T.astype(BF16)

    return pl.pallas_call(
        body, name=name, grid=(rows // tr,),
        in_specs=[_row_spec(tr, width), _row_spec(tr, merged.shape[1]),
                  pl.BlockSpec(wout.shape, lambda i: (0, 0)), _bcast_spec(width)],
        out_specs=[_row_spec(tr, width), _row_spec(tr, width), _col_spec(width, tr)],
        out_shape=[jax.ShapeDtypeStruct((rows, width), F32), jax.ShapeDtypeStruct((rows, width), BF16),
                   jax.ShapeDtypeStruct((width, rows), BF16)],
        compiler_params=_params(("parallel",), 14 * tr * width * 4),
    )(x, merged, wout, gain)


def _matmul_rmsnorm_bwd(a, b, x, dres, gain, name, tm, tk, a_parts=False, b_stacked=False, after=None,
                        m_blocks=None, out_into=None, with_bf16=True):
    width = x.shape[1]
    m = a.shape[1] if a_parts else a.shape[0]
    kdim = a.shape[0] * a.shape[2] if a_parts else a.shape[1]
    gk = kdim // tk
    i0, gm = (0, m // tm) if m_blocks is None else m_blocks
    n_in = 5 + (after is not None) + (out_into is not None)

    def body(*refs):
        a_ref, b_ref, x_ref, dres_ref, g_ref = refs[:5]
        outs = refs[n_in:]
        dx_ref, dg_ref, acc_ref = outs[0], outs[-2], outs[-1]
        i, k = pl.program_id(0), pl.program_id(1)
        part = _dot(a_ref[...], b_ref[...], _NT)

        @pl.when(k == 0)
        def _():
            acc_ref[...] = part

        @pl.when(jnp.logical_and(k > 0, k < gk - 1))
        def _():
            acc_ref[...] += part

        @pl.when(k == gk - 1)
        def _():
            dhv = acc_ref[...] + part if gk > 1 else part
            xv = x_ref[...]
            r = lax.rsqrt(jnp.mean(xv * xv, axis=-1, keepdims=True) + RMS_EPS)
            xhat = xv * r
            dyg = dhv * g_ref[...]
            dx = dres_ref[...] + r * (dyg - xhat * jnp.mean(dyg * xhat, axis=-1, keepdims=True))
            dx_ref[...] = dx
            if with_bf16:
                outs[1][...] = dx.astype(BF16)
            gpart = jnp.sum(dhv * xhat, axis=0, keepdims=True)

            @pl.when(i == 0)
            def _():
                dg_ref[...] = gpart

            @pl.when(i > 0)
            def _():
                dg_ref[...] += gpart

    rows = lambda width_: pl.BlockSpec((tm, width_), lambda i, k: (i + i0, 0))
    if a_parts:
        a_spec = pl.BlockSpec((None, tm, tk), lambda i, k: (k, i + i0, 0))
    else:
        a_spec = pl.BlockSpec((tm, tk), lambda i, k: (i + i0, k))
    if b_stacked:
        b_spec = pl.BlockSpec((None, width, tk), lambda i, k: (k, 0, 0))
    else:
        b_spec = pl.BlockSpec((width, tk), lambda i, k: (0, k))
    operands = [a, b, x, dres, gain]
    in_specs = [a_spec, b_spec, rows(width), rows(width), pl.BlockSpec((1, width), lambda i, k: (0, 0))]
    for extra in (after, out_into):
        if extra is not None:
            operands.append(extra)
            in_specs.append(pl.BlockSpec(memory_space=pl.ANY))
    aliases = {} if out_into is None else {len(operands) - 1: 0}
    out_specs = [rows(width)] + ([rows(width)] if with_bf16 else []) + [pl.BlockSpec((1, width), lambda i, k: (0, 0))]
    out_shape = [jax.ShapeDtypeStruct((m, width), F32)] + ([jax.ShapeDtypeStruct((m, width), BF16)] if with_bf16 else [])
    out_shape.append(jax.ShapeDtypeStruct((1, width), F32))
    est = 4 * tm * tk + 4 * width * tk + 12 * tm * width * 4
    return pl.pallas_call(
        body, name=name, grid=(gm, gk), in_specs=in_specs, out_specs=out_specs, out_shape=out_shape,
        scratch_shapes=[pltpu.VMEM((tm, width), F32)], input_output_aliases=aliases,
        compiler_params=_params(("arbitrary", "arbitrary"), est),
    )(*operands)


def _gain_grad(x, dh, name):
    rows, width = x.shape

    def body(x_ref, dh_ref, dg_ref):
        xv = x_ref[...]
        r = lax.rsqrt(jnp.mean(xv * xv, axis=-1, keepdims=True) + RMS_EPS)
        dg_ref[...] = jnp.sum(dh_ref[...] * xv * r, axis=0, keepdims=True)

    return pl.pallas_call(
        body, name=name, grid=(1,), in_specs=[_row_spec(rows, width), _row_spec(rows, width)],
        out_specs=_bcast_spec(width), out_shape=jax.ShapeDtypeStruct((1, width), F32),
        compiler_params=_params(("arbitrary",), 6 * rows * width * 4),
    )(x, dh)


def _lb_table(logits, name):
    slots, width = logits.shape

    def body(l_ref, o_ref):
        lv = l_ref[...]
        mx = jnp.max(lv, axis=0, keepdims=True)
        e = jnp.exp(lv - mx)
        o_ref[...] = e[0:1, :] / jnp.sum(e, axis=0, keepdims=True)

    return pl.pallas_call(
        body, name=name, grid=(1,), in_specs=[pl.BlockSpec((slots, width), lambda i: (0, 0))],
        out_specs=_bcast_spec(width), out_shape=jax.ShapeDtypeStruct((1, width), F32),
    )(logits)


def _branch_merge_fwd(proj, outs, weights, tr):
    w = D_MODEL

    def body(ghg_ref, gda_ref, gmem_ref, ohg_ref, oda_ref, omem_ref, whg_ref, wda_ref, wmem_ref,
             m_ref, thg_ref, tda_ref, tmem_ref):
        acc = None
        for g_ref, o_ref, w_ref, t_ref in ((ghg_ref, ohg_ref, whg_ref, thg_ref), (gda_ref, oda_ref, wda_ref, tda_ref),
                                           (gmem_ref, omem_ref, wmem_ref, tmem_ref)):
            t = _dot(o_ref[...], w_ref[...], _NN)
            t_ref[...] = t.astype(BF16)
            term = _sigmoid(g_ref[...]) * t
            acc = term if acc is None else acc + term
        m_ref[...] = acc.astype(BF16)

    whole = lambda a: pl.BlockSpec(a.shape, lambda i: (0, 0))
    shape = jax.ShapeDtypeStruct((SEQ, w), BF16)
    return pl.pallas_call(
        body, name="branch_merge_fwd", grid=(SEQ // tr,),
        in_specs=[_row_spec(tr, w, 10), _row_spec(tr, w, 11), _row_spec(tr, w, 12)]
        + [_row_spec(tr, o.shape[1]) for o in outs] + [whole(wt) for wt in weights],
        out_specs=[_row_spec(tr, w)] * 4, out_shape=[shape] * 4,
        compiler_params=_params(("parallel",), 20 * tr * w * 4),
    )(proj, proj, proj, *outs, *weights)


def _dproj_buffer():
    return lax.empty((SEQ, IN_COLS), BF16)


def _gate_merge_bwd(proj, ts, weights, dx1b, wout, o_pre, hg_gain, o_da32, dproj, tr):
    w = D_MODEL
    steps = SEQ // tr
    gate_col0 = 10 * w
    hg_gate_col0 = CB_HG_G * LANE

    def body(ghg_ref, gda_ref, gmem_ref, hgg_ref, thg_ref, tda_ref, tmem_ref, whg_ref, wda_ref, wmem_ref, dx_ref,
             wout_ref, opre_ref, gain_ref, oda_ref, dproj_in,
             dthg_ref, dtda_ref, dtmem_ref, dopre_ref, doda_ref, domem_ref, dd_ref, dgain_ref, dproj_ref,
             stage, stage_hg, sems):
        del dproj_in
        i = pl.program_id(0)
        slot = i % 2
        rows = pl.ds(pl.multiple_of(i * tr, tr), tr)

        def slot_copies(s):
            return (pltpu.make_async_copy(stage.at[s], dproj_ref.at[rows, pl.ds(gate_col0, 3 * w)], sems.at[s]),
                    pltpu.make_async_copy(stage_hg.at[s], dproj_ref.at[rows, pl.ds(hg_gate_col0, w)], sems.at[2 + s]))

        @pl.when(i >= 2)
        def _():
            for cp in slot_copies(slot):
                cp.wait()

        dm = _dot(dx_ref[...], wout_ref[...], _NT)
        branches = ((ghg_ref, thg_ref, whg_ref, dthg_ref), (gda_ref, tda_ref, wda_ref, dtda_ref),
                    (gmem_ref, tmem_ref, wmem_ref, dtmem_ref))
        dos = []
        for b, (g_ref, t_ref, w_ref, dt_ref) in enumerate(branches):
            s = _sigmoid(g_ref[...])
            dt = (s * dm).astype(BF16)
            dt_ref[...] = dt
            dos.append(_dot(dt, w_ref[...], _NT))
            stage[slot, :, b * w:(b + 1) * w] = (dm * t_ref[...].astype(F32) * s * (1.0 - s)).astype(BF16)
        do_hg, do_da, do_mem = dos
        doda_ref[...] = do_da
        domem_ref[...] = do_mem

        gainv = gain_ref[...]
        part = jnp.zeros((1, HEAD_DIM), F32)
        for h in range(HG_HEADS):
            hs = slice(h * HEAD_DIM, (h + 1) * HEAD_DIM)
            o = opre_ref[:, hs]
            r = lax.rsqrt(jnp.mean(o * o, axis=-1, keepdims=True) + RMS_EPS)
            ohat = o * r
            g = hgg_ref[:, hs]
            sg = _sigmoid(g)
            silu = g * sg
            dout = do_hg[:, hs]
            stage_hg[slot, :, hs] = (dout * ohat * gainv * (sg + silu * (1.0 - sg))).astype(BF16)
            dy = dout * silu
            part = part + jnp.sum(dy * ohat, axis=0, keepdims=True)
            dn = dy * gainv
            dopre_ref[:, hs] = r * (dn - ohat * jnp.mean(dn * ohat, axis=-1, keepdims=True))
        for cp in slot_copies(slot):
            cp.start()

        prod = do_da * oda_ref[...]
        for h in range(DA_WIDTH // HEAD_DIM):
            hs = slice(h * HEAD_DIM, (h + 1) * HEAD_DIM)
            dd_ref[:, hs] = jnp.broadcast_to(jnp.sum(prod[:, hs], axis=-1, keepdims=True), (tr, HEAD_DIM))

        @pl.when(i == 0)
        def _():
            dgain_ref[...] = part

        @pl.when(i > 0)
        def _():
            dgain_ref[...] += part

        @pl.when(i == steps - 1)
        def _():
            for cp in slot_copies(1 - slot) + slot_copies(slot):
                cp.wait()

    assert steps >= 2
    whole = lambda a: pl.BlockSpec(a.shape, lambda i: (0, 0))
    widths = [wt.shape[0] for wt in weights]
    return pl.pallas_call(
        body, name="gate_merge_bwd", grid=(steps,),
        in_specs=[_row_spec(tr, w, 10), _row_spec(tr, w, 11), _row_spec(tr, w, 12), _row_spec(tr, w, hg_gate_col0 // w)]
        + [_row_spec(tr, w)] * 3 + [whole(wt) for wt in weights]
        + [_row_spec(tr, w), whole(wout), _row_spec(tr, w), _bcast_spec(HEAD_DIM), _row_spec(tr, DA_WIDTH), _any_spec()],
        out_specs=[_row_spec(tr, w)] * 4 + [_row_spec(tr, widths[1]), _row_spec(tr, widths[2]),
                                            _row_spec(tr, DA_WIDTH), _bcast_spec(HEAD_DIM), _any_spec()],
        out_shape=[jax.ShapeDtypeStruct((SEQ, w), BF16)] * 3 + [jax.ShapeDtypeStruct((SEQ, w), F32)]
        + [jax.ShapeDtypeStruct((SEQ, widths[1]), F32), jax.ShapeDtypeStruct((SEQ, widths[2]), F32),
           jax.ShapeDtypeStruct((SEQ, DA_WIDTH), F32), jax.ShapeDtypeStruct((1, HEAD_DIM), F32),
           jax.ShapeDtypeStruct((SEQ, IN_COLS), BF16)],
        scratch_shapes=[pltpu.VMEM((2, tr, 3 * w), BF16), pltpu.VMEM((2, tr, w), BF16), pltpu.SemaphoreType.DMA((4,))],
        input_output_aliases={15: 8},
        compiler_params=_params(("arbitrary",), 44 * tr * w * 4),
    )(proj, proj, proj, proj, *ts, *weights, dx1b, wout, o_pre, hg_gain, o_da32, dproj)


def _ffn_in_swiglu(h2, wfin, tm, tn):
    gm, gn = SEQ // tm, D_FF // tn

    def body(h_ref, wa_ref, wb_ref, a_ref, b_ref, act_ref, actt_ref):
        h = h_ref[...]
        a = _dot(h, wa_ref[...], _NN)
        b = _dot(h, wb_ref[...], _NN)
        act = a * _sigmoid(a) * b
        a_ref[...] = a.astype(BF16)
        b_ref[...] = b.astype(BF16)
        act_ref[...] = act.astype(BF16)
        actt_ref[...] = act.T.astype(BF16)

    tile = pl.BlockSpec((tm, tn), lambda j, i: (i, j))
    shape = jax.ShapeDtypeStruct((SEQ, D_FF), BF16)
    return pl.pallas_call(
        body, name="ffn_in_swiglu_fwd", grid=(gn, gm),
        in_specs=[pl.BlockSpec((tm, D_MODEL), lambda j, i: (i, 0)),
                  pl.BlockSpec((D_MODEL, tn), lambda j, i: (0, j)),
                  pl.BlockSpec((D_MODEL, tn), lambda j, i: (0, gn + j))],
        out_specs=[tile, tile, tile, pl.BlockSpec((tn, tm), lambda j, i: (j, i))],
        out_shape=[shape, shape, shape, jax.ShapeDtypeStruct((D_FF, SEQ), BF16)],
        compiler_params=_params(("parallel", "parallel"), 4 * tm * D_MODEL + 8 * D_MODEL * tn + 16 * tm * tn
                                + 6 * tm * tn * 4),
    )(h2, wfin, wfin)


def _ffn_out_bwd_swiglu(dyb, wfout, a, b, tm, tn):
    gm, gn = SEQ // tm, D_FF // tn

    def body(dy_ref, w_ref, a_ref, b_ref, o_ref):
        d = _dot(dy_ref[...], w_ref[...], _NT)
        av = a_ref[...].astype(F32)
        bv = b_ref[...].astype(F32)
        s = _sigmoid(av)
        silu = av * s
        o_ref[0] = (d * bv * (s + silu * (1.0 - s))).astype(BF16)
        o_ref[1] = (d * silu).astype(BF16)

    tile = pl.BlockSpec((tm, tn), lambda i, j: (i, j))
    return pl.pallas_call(
        body, name="ffn_out_bwd_swiglu", grid=(gm, gn),
        in_specs=[pl.BlockSpec((tm, D_MODEL), lambda i, j: (i, 0)), pl.BlockSpec((tn, D_MODEL), lambda i, j: (j, 0)),
                  tile, tile],
        out_specs=pl.BlockSpec((2, tm, tn), lambda i, j: (0, i, j)),
        out_shape=jax.ShapeDtypeStruct((2, SEQ, D_FF), BF16),
        compiler_params=_params(("parallel", "parallel"), 4 * tm * D_MODEL + 4 * tn * D_MODEL + 16 * tm * tn
                                + 6 * tm * tn * 4),
    )(dyb, wfout, a, b)


def _ffn_out_loss_head(x1, act, wfout, target, tr):
    w = D_MODEL

    def body(x_ref, a_ref, w_ref, t_ref, dy_ref, dyb_ref, loss_ref, acc_ref):
        err = x_ref[...] + _dot(a_ref[...], w_ref[...], _NN) - t_ref[...]
        dy = err * (1.0 / w)
        dy_ref[...] = dy
        dyb_ref[...] = dy.astype(BF16)
        part = jnp.sum(err * err, axis=0, keepdims=True)

        @pl.when(pl.program_id(0) == 0)
        def _():
            acc_ref[...] = part

        @pl.when(pl.program_id(0) > 0)
        def _():
            acc_ref[...] += part

        @pl.when(pl.program_id(0) == SEQ // tr - 1)
        def _():
            total = jnp.sum(acc_ref[...], axis=1, keepdims=True) * (0.5 / w)
            loss_ref[...] = jnp.broadcast_to(total, (1, LANE))

    return pl.pallas_call(
        body, name="ffn_out_loss_head", grid=(SEQ // tr,),
        in_specs=[_row_spec(tr, w), _row_spec(tr, D_FF), pl.BlockSpec((D_FF, w), lambda i: (0, 0)), _row_spec(tr, w)],
        out_specs=[_row_spec(tr, w), _row_spec(tr, w), _bcast_spec(LANE)],
        out_shape=[jax.ShapeDtypeStruct((SEQ, w), F32), jax.ShapeDtypeStruct((SEQ, w), BF16),
                   jax.ShapeDtypeStruct((1, LANE), F32)],
        scratch_shapes=[pltpu.VMEM((1, w), F32)],
        compiler_params=_params(("arbitrary",), 12 * tr * w * 4 + 4 * D_FF * w),
    )(x1, act, wfout, target)


GLA_ROWS = 256
GLA_CPB = GLA_ROWS // HG_CHUNK
GLA_NBLK = SEQ // GLA_ROWS
GLA_WAYS = 8
GLA_TRIPS = GLA_NBLK // GLA_WAYS
GLA_GRAD_WAYS = 8
GLA_GRAD_TRIPS = GLA_NBLK // GLA_GRAD_WAYS
GLA_NCK = SEQ // HG_CHUNK
GLA_INTER_WAYS = 32


def _dot_split(m, xv, dims, terms=3):
    dot = lambda t: lax.dot_general(m, t, dims, preferred_element_type=F32)
    hi = xv.astype(BF16)
    r1 = xv - hi.astype(F32)
    mid = r1.astype(BF16)
    if terms == 2:
        return dot(mid) + dot(hi)
    lo = (r1 - mid.astype(F32)).astype(BF16)
    return (dot(lo) + dot(mid)) + dot(hi)


def _gla_block(qc, fc, lbv, masks, b=None):
    mask, maskb, direction = masks
    sq = _sigmoid(qc)
    q = qc * sq * HG_SCALE
    sf = _sigmoid(fc)
    forget = lbv + (1.0 - lbv) * sf
    k = 1.0 - forget
    logf = jnp.log(forget)
    if b is None:
        b = _dot_split(maskb, logf, _NN)
    ends = []
    for j in range(GLA_CPB):
        lo, hi = j * HG_CHUNK, (j + 1) * HG_CHUNK
        end = jnp.where(direction == 0, b[hi - 1:hi, :], b[lo:lo + 1, :])
        ends.append(jnp.broadcast_to(end, (HG_CHUNK, HEAD_DIM)))
    bt = jnp.concatenate(ends, axis=0)
    eb = jnp.exp(b)
    qt = q * eb
    kt = k * jnp.exp(-b)
    kh = k * jnp.exp(bt - b)
    a = jnp.where(mask, _dot(qt, kt, _NT), 0.0)
    return dict(sq=sq, sf=sf, forget=forget, k=k, b=b, bt=bt, eb=eb, qt=qt, kt=kt, kh=kh, a=a)


def _gla_masks(direction):
    row = lax.broadcasted_iota(jnp.int32, (GLA_ROWS, GLA_ROWS), 0)
    col = lax.broadcasted_iota(jnp.int32, (GLA_ROWS, GLA_ROWS), 1)
    same = (row // HG_CHUNK) == (col // HG_CHUNK)
    mask = jnp.logical_and(same, jnp.where(direction == 0, row - col, col - row) >= 0)
    return mask, jnp.where(mask, 1.0, 0.0).astype(BF16), direction


def _gla_chunk_rows(j):
    return slice(j * HG_CHUNK, (j + 1) * HG_CHUNK)


def _gla_block_rows(b):
    return pl.ds(pl.multiple_of(b * GLA_ROWS, GLA_ROWS), GLA_ROWS)


def _head_spec(col_block0):
    return pl.BlockSpec((SEQ, HEAD_DIM), lambda h, d: (0, col_block0 + h))


def _gla_decay(proj, lb):
    def body(f_ref, lb_ref, b_ref):
        maskb = _gla_masks(pl.program_id(0))[1]
        lbv = lb_ref[...]

        def step(s, carry):
            rows = [_gla_block_rows(s + w * trips) for w in range(ways)]
            loaded = [f_ref[r, :] for r in rows]
            outs = [_dot_split(maskb, jnp.log(lbv + (1.0 - lbv) * _sigmoid(fc)), _NN) for fc in loaded]
            for r, o in zip(rows, outs):
                b_ref[r, :] = o
            return carry

        lax.fori_loop(0, trips, step, 0)

    ways, trips = GLA_WAYS, GLA_TRIPS
    heads = 2
    width = heads * HEAD_DIM
    return pl.pallas_call(
        body, name="gla_decay", grid=(2, HG_HEADS // heads),
        in_specs=[pl.BlockSpec((SEQ, width), lambda d, h: (0, CB_F // heads + (HG_HEADS // heads) * d + h)),
                  pl.BlockSpec((None, None, 1, width), lambda d, h: (d, h, 0, 0))],
        out_specs=pl.BlockSpec((None, SEQ, width), lambda d, h: (d, 0, h)),
        out_shape=jax.ShapeDtypeStruct((2, SEQ, D_MODEL), F32),
        compiler_params=_params(("parallel", "parallel"), 6 * SEQ * width * 4),
    )(proj, lb.reshape(2, HG_HEADS // heads, 1, width))


def _gla_fwd(proj, lb, gain, b_all):
    nck = GLA_NCK

    def body(q_ref, f_ref, v_ref, g_ref, lb_ref, gain_ref, b_ref, ohg_ref, opre_ref, st_ref, qt_scr, dec_scr, cs_scr):
        d = pl.program_id(1)
        masks = _gla_masks(d)
        lbv = lb_ref[...]

        @pl.when(d == 0)
        def _():
            opre_ref[...] = jnp.zeros_like(opre_ref)

        def intra(s, carry):
            blocks = [s + w * GLA_TRIPS for w in range(GLA_WAYS)]
            rows = [_gla_block_rows(b) for b in blocks]
            loaded = [(q_ref[r, :], f_ref[r, :], v_ref[r, :], opre_ref[r, :], b_ref[r, :]) for r in rows]
            results = []
            for qc, fc, v, o_prev, decay_log in loaded:
                ck = _gla_block(qc, fc, lbv, masks, b=decay_log)
                o = o_prev + _dot(ck["a"], v, _NN)
                cs = [_dot(v[_gla_chunk_rows(j), :], ck["kh"][_gla_chunk_rows(j), :], _TN) for j in range(GLA_CPB)]
                dec = [jnp.exp(ck["bt"][j * HG_CHUNK:j * HG_CHUNK + 8, :]) for j in range(GLA_CPB)]
                results.append((o, ck["qt"].astype(BF16), cs, dec))
            for b, r, (o, qt, cs, dec) in zip(blocks, rows, results):
                opre_ref[r, :] = o
                qt_scr[r, :] = qt
                for j in range(GLA_CPB):
                    cs_scr[b * GLA_CPB + j] = cs[j]
                    dec_scr[b * GLA_CPB + j] = dec[j]
            return carry

        lax.fori_loop(0, GLA_TRIPS, intra, 0)

        def scan(i, st):
            c = jnp.where(d == 0, i, nck - 1 - i)
            st_ref[c] = st
            return st * dec_scr[c][0:1, :] + cs_scr[c]

        lax.fori_loop(0, nck, scan, jnp.zeros((HEAD_DIM, HEAD_DIM), F32), unroll=4)

        def inter(t, carry):
            chunks = [t + w * (nck // GLA_INTER_WAYS) for w in range(GLA_INTER_WAYS)]
            rows = [pl.ds(pl.multiple_of(c * HG_CHUNK, HG_CHUNK), HG_CHUNK) for c in chunks]
            loaded = [(opre_ref[r, :], qt_scr[r, :], st_ref[c]) for r, c in zip(rows, chunks)]
            for r, o in zip(rows, [o_prev + _dot(qt, st, _NT) for o_prev, qt, st in loaded]):
                opre_ref[r, :] = o
            return carry

        lax.fori_loop(0, nck // GLA_INTER_WAYS, inter, 0)

        @pl.when(d == 1)
        def _():
            o = opre_ref[...]
            r = lax.rsqrt(jnp.mean(o * o, axis=-1, keepdims=True) + RMS_EPS)
            g = g_ref[...]
            ohg_ref[...] = (o * r * gain_ref[...] * (g * _sigmoid(g))).astype(BF16)

    blk = SEQ * HEAD_DIM * 4
    return pl.pallas_call(
        body, name="gla_fwd", grid=(HG_HEADS, 2),
        in_specs=[_head_spec(CB_HG_Q),
                  pl.BlockSpec((SEQ, HEAD_DIM), lambda h, d: (0, CB_F + 8 * d + h)),
                  _head_spec(CB_HG_I), _head_spec(CB_HG_G),
                  pl.BlockSpec((None, None, 1, HEAD_DIM), lambda h, d: (d, h, 0, 0)),
                  pl.BlockSpec((1, HEAD_DIM), lambda h, d: (0, 0)),
                  pl.BlockSpec((None, SEQ, HEAD_DIM), lambda h, d: (d, 0, h))],
        out_specs=[_head_spec(0), _head_spec(0),
                   pl.BlockSpec((None, None, nck, HEAD_DIM, HEAD_DIM), lambda h, d: (h, d, 0, 0, 0)),
                   pl.BlockSpec((None, SEQ, HEAD_DIM), lambda h, d: (d, 0, h)),
                   pl.BlockSpec((None, None, nck, 8, HEAD_DIM), lambda h, d: (h, d, 0, 0, 0))],
        out_shape=[jax.ShapeDtypeStruct((SEQ, D_MODEL), BF16), jax.ShapeDtypeStruct((SEQ, D_MODEL), F32),
                   jax.ShapeDtypeStruct((HG_HEADS, 2, nck, HEAD_DIM, HEAD_DIM), F32),
                   jax.ShapeDtypeStruct((2, SEQ, D_MODEL), BF16),
                   jax.ShapeDtypeStruct((HG_HEADS, 2, nck, 8, HEAD_DIM), F32)],
        scratch_shapes=[pltpu.VMEM((nck, HEAD_DIM, HEAD_DIM), F32)],
        compiler_params=_params(("parallel", "arbitrary"), 8 * blk + 3 * blk + 2 * blk + 2 * blk + 3 * blk),
    )(proj, proj, proj, proj, lb, gain, b_all)


def _gla_bwd(proj, lb, states, qt_all, dec_all, b_all, do, dproj):
    nck = GLA_NCK

    def body(q_ref, f_ref, v_ref, lb_ref, do_ref, st_ref, qt_ref, dec_scr, b_ref, dproj_in, dproj_ref, dlb_ref,
             dq_acc, dv_acc, dst_scr, cs_scr, df_out, dq_out, dv_out, sems):
        del dproj_in
        h, d = pl.program_id(0), pl.program_id(1)
        masks = _gla_masks(d)
        mask, maskb, _ = masks
        lbv = lb_ref[...]

        def column_copy(k, staging, col_block):
            cols = pl.ds(pl.multiple_of(col_block * LANE, LANE), LANE)
            return pltpu.make_async_copy(staging, dproj_ref.at[:, cols], sems.at[k])

        df_copy = column_copy(0, df_out, CB_F + 8 * d + h)
        dq_copy = column_copy(1, dq_out, CB_HG_Q + h)
        dv_copy = column_copy(2, dv_out, CB_HG_I + h)
        last = jnp.logical_and(h == HG_HEADS - 1, d == 1)

        def intra(s, carry):
            blocks = [s + w * GLA_TRIPS for w in range(GLA_WAYS)]
            loaded = [(qt_ref[r, :], do_ref[r, :]) for r in map(_gla_block_rows, blocks)]
            results = [[_dot(doc[_gla_chunk_rows(j), :], qt[_gla_chunk_rows(j), :], _TN) for j in range(GLA_CPB)]
                       for qt, doc in loaded]
            for b, cs in zip(blocks, results):
                for j in range(GLA_CPB):
                    cs_scr[b * GLA_CPB + j] = cs[j]
            return carry

        lax.fori_loop(0, GLA_TRIPS, intra, 0)

        def scan(i, dst):
            c = jnp.where(d == 0, nck - 1 - i, i)
            dst_scr[c] = dst
            return dst * dec_scr[c][0:1, :] + cs_scr[c]

        lax.fori_loop(0, nck, scan, jnp.zeros((HEAD_DIM, HEAD_DIM), F32), unroll=4)

        @pl.when(d == 0)
        def _():
            dq_acc[...] = jnp.zeros_like(dq_acc)
            dv_acc[...] = jnp.zeros_like(dv_acc)

        def block_grads(qc, fc, v, doc, states_in, dstates, decays, decay_log):
            ck = _gla_block(qc, fc, lbv, masks, b=decay_log)
            qt, kt, kh, a = ck["qt"], ck["kt"], ck["kh"], ck["a"]
            da = jnp.where(mask, _dot(doc, v, _NT), 0.0)
            dqt_i = _dot(da, kt, _NN)
            dkt = _dot(da, qt, _TN)
            dv_i = _dot(a, doc, _TN)
            dqt_p, dv_p, dkh_p, dbt_p = [], [], [], []
            for j in range(GLA_CPB):
                cr = _gla_chunk_rows(j)
                st_in, dst = states_in[j], dstates[j]
                dqt_p.append(dqt_i[cr, :] + _dot(doc[cr, :], st_in, _NN))
                dv_p.append(dv_i[cr, :] + _dot(kh[cr, :], dst, _NT))
                dkh_j = _dot(v[cr, :], dst, _NN)
                dkh_p.append(dkh_j)
                dbt_j = (decays[j][0:1, :] * jnp.sum(dst * st_in, axis=0, keepdims=True)
                         + jnp.sum(dkh_j * kh[cr, :], axis=0, keepdims=True))
                dbt_p.append(jnp.broadcast_to(dbt_j, (HG_CHUNK, HEAD_DIM)))
            dqt = jnp.concatenate(dqt_p, axis=0)
            dv = jnp.concatenate(dv_p, axis=0)
            dkh = jnp.concatenate(dkh_p, axis=0)
            dbt = jnp.concatenate(dbt_p, axis=0)
            db = dqt * qt - dkt * kt - dkh * kh
            dq = dqt * ck["eb"]
            dk = dkt * jnp.exp(-ck["b"]) + dkh * jnp.exp(ck["bt"] - ck["b"])
            dlogf = _dot_split(maskb, db, _TN, terms=2) + dbt
            dforget = dlogf / ck["forget"] - dk
            sf, sq = ck["sf"], ck["sq"]
            df = (dforget * (1.0 - lbv) * sf * (1.0 - sf)).astype(BF16)
            dqc = dq * HG_SCALE * (sq + qc * sq * (1.0 - sq))
            return df, dqc, dv, jnp.sum(dforget * (1.0 - sf), axis=0, keepdims=True)

        def grads(s, dlb):
            blocks = [s + w * GLA_GRAD_TRIPS for w in range(GLA_GRAD_WAYS)]
            rows = [_gla_block_rows(b) for b in blocks]
            loaded = []
            for b, r in zip(blocks, rows):
                chunks = [b * GLA_CPB + j for j in range(GLA_CPB)]
                loaded.append((q_ref[r, :], f_ref[r, :], v_ref[r, :], do_ref[r, :], [st_ref[c] for c in chunks],
                               [dst_scr[c] for c in chunks], [dec_scr[c] for c in chunks], b_ref[r, :],
                               dq_acc[r, :], dv_acc[r, :]))
            results = [block_grads(*t[:8]) + (t[8], t[9]) for t in loaded]
            for r, (df, dqc, dv, dlb_part, dq_prev, dv_prev) in zip(rows, results):
                df_out[r, :] = df
                dq_acc[r, :] = dq_prev + dqc
                dv_acc[r, :] = dv_prev + dv
                dlb = dlb + dlb_part
            return dlb

        @pl.when(jnp.logical_or(h > 0, d > 0))
        def _():
            df_copy.wait()

        dlb_ref[...] = lax.fori_loop(0, GLA_GRAD_TRIPS, grads, jnp.zeros((1, HEAD_DIM), F32))
        df_copy.start()

        @pl.when(d == 1)
        def _():
            @pl.when(h > 0)
            def _():
                dq_copy.wait()
                dv_copy.wait()

            dq_out[...] = dq_acc[...].astype(BF16)
            dv_out[...] = dv_acc[...].astype(BF16)
            dq_copy.start()
            dv_copy.start()

        @pl.when(last)
        def _():
            df_copy.wait()
            dq_copy.wait()
            dv_copy.wait()

    blk = SEQ * HEAD_DIM * 4
    state_bytes = nck * HEAD_DIM * HEAD_DIM * 4
    dproj, dlb = pl.pallas_call(
        body, name="gla_bwd", grid=(HG_HEADS, 2),
        in_specs=[_head_spec(CB_HG_Q),
                  pl.BlockSpec((SEQ, HEAD_DIM), lambda h, d: (0, CB_F + 8 * d + h)),
                  _head_spec(CB_HG_I),
                  pl.BlockSpec((None, None, 1, HEAD_DIM), lambda h, d: (d, h, 0, 0)),
                  _head_spec(0),
                  pl.BlockSpec((None, None, nck, HEAD_DIM, HEAD_DIM), lambda h, d: (h, d, 0, 0, 0)),
                  pl.BlockSpec((None, SEQ, HEAD_DIM), lambda h, d: (d, 0, h)),
                  pl.BlockSpec((None, None, nck, 8, HEAD_DIM), lambda h, d: (h, d, 0, 0, 0)),
                  pl.BlockSpec((None, SEQ, HEAD_DIM), lambda h, d: (d, 0, h)),
                  _any_spec()],
        out_specs=[_any_spec(), pl.BlockSpec((None, None, 1, HEAD_DIM), lambda h, d: (d, h, 0, 0))],
        out_shape=[jax.ShapeDtypeStruct((SEQ, IN_COLS), BF16), jax.ShapeDtypeStruct((2, HG_HEADS, 1, HEAD_DIM), F32)],
        scratch_shapes=[pltpu.VMEM((SEQ, HEAD_DIM), F32)] * 2
        + [pltpu.VMEM((nck, HEAD_DIM, HEAD_DIM), F32)] * 2
        + [pltpu.VMEM((SEQ, HEAD_DIM), BF16)] * 3 + [pltpu.SemaphoreType.DMA((3,))],
        input_output_aliases={9: 0},
        compiler_params=_params(("arbitrary", "arbitrary"), 10 * blk + 4 * state_bytes + 3 * blk + 3 * blk),
    )(proj, proj, proj, lb, do, states, qt_all, dec_all, b_all, dproj)
    return dproj, dlb


def _da_residue_rows(r, n0, size, d):
    if d == 1:
        return pl.ds(pl.multiple_of(n0, 8), size)
    return pl.ds(r + n0 * d, size, stride=d)


def _da_residue_ways(d, nqb):
    return min(d, 4) if nqb <= 2 else 1


def _da_rmsnorm(x, gain):
    r = lax.rsqrt(jnp.mean(x * x, axis=-1, keepdims=True) + RMS_EPS)
    return x * r, r, x * r * gain


def _da_scores(qn_scr, kn_scr, slope, i, ld):
    w0 = jnp.clip(i * DA_QB - DA_RADIUS, 0, ld - DA_WIN)
    w0 = pl.multiple_of(w0, DA_RADIUS)
    qrows = pl.ds(pl.multiple_of(i * DA_QB, DA_QB), DA_QB)
    win = pl.ds(w0, DA_WIN)
    qb = qn_scr[qrows, :]
    kw = kn_scr[win, :]
    s = _dot(qb, kw, _NT) * DA_SCALE
    qpos = i * DA_QB + lax.broadcasted_iota(jnp.int32, (DA_QB, DA_WIN), 0)
    kpos = w0 + lax.broadcasted_iota(jnp.int32, (DA_QB, DA_WIN), 1)
    arel = jnp.abs(kpos - qpos)
    s = s - slope * arel.astype(F32)
    s = jnp.where(arel <= DA_RADIUS, s, NEG_INF)
    return s, qb, kw, qrows, win


def _da_slopes(group):
    d = DA_DILATIONS[group]
    sl = _alibi_slopes(DA_HEADS)[4 * group:4 * group + 4] * d
    return jnp.asarray(np.broadcast_to(sl[:, None, None], (4, 1, LANE)).copy())


def _da_fwd(proj, gq, gk, group, others=()):
    d = DA_DILATIONS[group]
    ld = SEQ // d
    nqb = ld // DA_QB
    n_other = 2 * len(others)

    ways = min(DA_FWD_WAYS, nqb)
    rways = _da_residue_ways(d, nqb)

    def body(q_ref, k_ref, v_ref, gq_ref, gk_ref, sl_ref, *refs):
        other_refs, refs = refs[:n_other], refs[n_other:]
        if others:
            ob_ref, of_ref, lsej_ref, qn_scr, kn_scr, v_scr, o_ref, lse_ref = refs
        else:
            o_ref, lse_ref, qn_scr, kn_scr, v_scr = refs
        slope = sl_ref[:, 0:1]

        def residues(t, carry):
            rs = [t * rways + u for u in range(rways)]
            for u, r in enumerate(rs):
                sel = _da_residue_rows(r, 0, ld, d)
                qn_scr[u] = _da_rmsnorm(q_ref[sel, :], gq_ref[...])[2].astype(BF16)
                kn_scr[u] = _da_rmsnorm(k_ref[sel, :], gk_ref[...])[2].astype(BF16)
                v_scr[u] = v_ref[sel, :].astype(BF16)

            def block(u, i):
                s, _, _, _, win = _da_scores(qn_scr.at[u], kn_scr.at[u], slope, i, ld)
                m = jnp.max(s, axis=-1, keepdims=True)
                p = jnp.exp(s - m)
                l = jnp.sum(p, axis=-1, keepdims=True)
                return _dot(p, v_scr[u, win, :], _NN) / l, jnp.broadcast_to(m + jnp.log(l), (DA_QB, HEAD_DIM))

            def step(i, c2):
                todo = [(u, r, i + w * (nqb // ways)) for u, r in enumerate(rs) for w in range(ways)]
                for (u, r, b), (o, lse) in zip(todo, [block(u, b) for u, _, b in todo]):
                    out = _da_residue_rows(r, b * DA_QB, DA_QB, d)
                    o_ref[out, :] = o
                    lse_ref[out, :] = lse
                return c2

            return lax.fori_loop(0, nqb // ways, step, carry)

        lax.fori_loop(0, d // rways, residues, 0)

        if others:
            pieces = 8
            rows = SEQ // pieces

            def merge(t, carry):
                r = pl.ds(pl.multiple_of(t * rows, rows), rows)
                outs = [ref[r, :] for ref in other_refs[0::2]] + [o_ref[r, :]]
                lses = [ref[r, :] for ref in other_refs[1::2]] + [lse_ref[r, :]]
                m = lses[0]
                for l in lses[1:]:
                    m = jnp.maximum(m, l)
                es = [jnp.exp(l - m) for l in lses]
                tot = sum(es[1:], es[0])
                o = sum((e * v for e, v in zip(es[1:], outs[1:])), es[0] * outs[0]) / tot
                of_ref[r, :] = o
                ob_ref[r, :] = o.astype(BF16)
                lsej_ref[r, :] = m + jnp.log(tot)
                return carry

            lax.fori_loop(0, pieces, merge, 0)

    seq_spec = lambda base: pl.BlockSpec((SEQ, HEAD_DIM), lambda h: (0, base + 4 * group + h))
    out_spec = pl.BlockSpec((SEQ, HEAD_DIM), lambda h: (0, h))
    gain_spec = pl.BlockSpec((1, HEAD_DIM), lambda h: (0, 0))
    blk = SEQ * HEAD_DIM * 4
    fshape = jax.ShapeDtypeStruct((SEQ, DA_WIDTH), F32)
    scratch = [pltpu.VMEM((rways, ld, HEAD_DIM), BF16)] * 3
    if others:
        out_shape = [jax.ShapeDtypeStruct((SEQ, DA_WIDTH), BF16), fshape, fshape]
        scratch += [pltpu.VMEM((SEQ, HEAD_DIM), F32)] * 2
    else:
        out_shape = [fshape, fshape]
    return pl.pallas_call(
        body, name=f"da_fwd_g{group}", grid=(DA_HEADS_PER_GROUP,),
        in_specs=[seq_spec(CB_DA_Q), seq_spec(CB_DA_K), seq_spec(CB_DA_V), gain_spec, gain_spec,
                  pl.BlockSpec((None, 1, LANE), lambda h: (h, 0, 0))] + [out_spec] * n_other,
        out_specs=[out_spec] * len(out_shape), out_shape=out_shape, scratch_shapes=scratch,
        compiler_params=_params(("parallel",), 10 * blk + 2 * blk + (n_other + 3) * blk),
    )(proj, proj, proj, gq, gk, _da_slopes(group), *[a for pair in others for a in pair])


def _da_bwd(proj, gq, gk, do, lse, dd, dproj, group):
    d = DA_DILATIONS[group]
    ld = SEQ // d
    nqb = ld // DA_QB
    ways = min(DA_WAYS, nqb)
    rways = _da_residue_ways(d, nqb)

    def body(q_ref, k_ref, v_ref, gq_ref, gk_ref, sl_ref, do_ref, lse_ref, dd_ref, dproj_in,
             dproj_ref, dgq_ref, dgk_ref,
             qn_scr, kn_scr, v_scr, dqn_scr, dkn_scr, dvr_scr, rq_scr, rk_scr, dq_scr, dk_scr, dv_scr,
             dq_out, dk_out, dv_out, sems):
        del dproj_in
        head = pl.program_id(0)
        gqv, gkv = gq_ref[...], gk_ref[...]
        slope = sl_ref[:, 0:1]

        copies = []
        for k, (staging, base) in enumerate(((dq_out, CB_DA_Q), (dk_out, CB_DA_K), (dv_out, CB_DA_V))):
            cols = pl.ds(pl.multiple_of((base + 4 * group + head) * LANE, LANE), LANE)
            copies.append(pltpu.make_async_copy(staging, dproj_ref.at[:, cols], sems.at[k]))

        def residues(t, carry):
            rs = [t * rways + u for u in range(rways)]
            sels = [_da_residue_rows(r, 0, ld, d) for r in rs]
            for u, sel in enumerate(sels):
                for x_ref, gv, n_scr, r_scr in ((q_ref, gqv, qn_scr, rq_scr), (k_ref, gkv, kn_scr, rk_scr)):
                    _, rstd, normed = _da_rmsnorm(x_ref[sel, :], gv)
                    n_scr[u] = normed.astype(BF16)
                    r_scr[u] = jnp.broadcast_to(rstd, (ld, HEAD_DIM))
                v_scr[u] = v_ref[sel, :].astype(BF16)
            dkn_scr[...] = jnp.zeros_like(dkn_scr)
            dvr_scr[...] = jnp.zeros_like(dvr_scr)

            def block(u, r, i):
                s, qb, kw, qrows, win = _da_scores(qn_scr.at[u], kn_scr.at[u], slope, i, ld)
                src = _da_residue_rows(r, i * DA_QB, DA_QB, d)
                p = jnp.exp(s - lse_ref[src, :][:, 0:1])
                dob = do_ref[src, :]
                dp = _dot(dob, v_scr[u, win, :], _NT)
                ds = p * (dp - dd_ref[src, :][:, 0:1]) * DA_SCALE
                return u, qrows, win, _dot(p, dob, _TN), _dot(ds, kw, _NN), _dot(ds, qb, _TN)

            def step(i, c2):
                todo = [(u, r, i + w * (nqb // ways)) for u, r in enumerate(rs) for w in range(ways)]
                for u, qrows, win, dv, dqn, dkn in [block(*t) for t in todo]:
                    dvr_scr[u, win, :] += dv
                    dqn_scr[u, qrows, :] = dqn
                    dkn_scr[u, win, :] += dkn
                return c2

            lax.fori_loop(0, nqb // ways, step, 0)

            pq, pk = carry
            for u, sel in enumerate(sels):
                parts = []
                for x_ref, gv, dn_scr, dx_scr, r_scr in ((q_ref, gqv, dqn_scr, dq_scr, rq_scr),
                                                         (k_ref, gkv, dkn_scr, dk_scr, rk_scr)):
                    rstd = r_scr[u]
                    hat = x_ref[sel, :] * rstd
                    dn = dn_scr[u]
                    dyg = dn * gv
                    dx_scr[sel, :] = rstd * (dyg - hat * jnp.mean(dyg * hat, axis=-1, keepdims=True))
                    parts.append(jnp.sum(dn * hat, axis=0, keepdims=True))
                dv_scr[sel, :] = dvr_scr[u]
                pq, pk = pq + parts[0], pk + parts[1]
            return pq, pk

        zero = jnp.zeros((1, HEAD_DIM), F32)
        pq, pk = lax.fori_loop(0, d // rways, residues, (zero, zero))

        @pl.when(pl.program_id(0) == 0)
        def _():
            dgq_ref[...] = pq
            dgk_ref[...] = pk

        @pl.when(pl.program_id(0) > 0)
        def _():
            dgq_ref[...] += pq
            dgk_ref[...] += pk

        @pl.when(head > 0)
        def _():
            for cp in copies:
                cp.wait()

        dq_out[...] = dq_scr[...].astype(BF16)
        dk_out[...] = dk_scr[...].astype(BF16)
        dv_out[...] = dv_scr[...].astype(BF16)
        for cp in copies:
            cp.start()

        @pl.when(head == DA_HEADS_PER_GROUP - 1)
        def _():
            for cp in copies:
                cp.wait()

    seq_spec = lambda base: pl.BlockSpec((SEQ, HEAD_DIM), lambda h: (0, base + 4 * group + h))
    out_spec = pl.BlockSpec((SEQ, HEAD_DIM), lambda h: (0, h))
    gain_spec = pl.BlockSpec((1, HEAD_DIM), lambda h: (0, 0))
    gshape = jax.ShapeDtypeStruct((1, HEAD_DIM), F32)
    blk = SEQ * HEAD_DIM * 4
    return pl.pallas_call(
        body, name=f"da_bwd_g{group}", grid=(DA_HEADS_PER_GROUP,),
        in_specs=[seq_spec(CB_DA_Q), seq_spec(CB_DA_K), seq_spec(CB_DA_V), gain_spec, gain_spec,
                  pl.BlockSpec((None, 1, LANE), lambda h: (h, 0, 0)), out_spec, out_spec, out_spec, _any_spec()],
        out_specs=[_any_spec(), gain_spec, gain_spec],
        out_shape=[jax.ShapeDtypeStruct((SEQ, IN_COLS), BF16), gshape, gshape],
        scratch_shapes=[pltpu.VMEM((rways, ld, HEAD_DIM), BF16)] * 3 + [pltpu.VMEM((rways, ld, HEAD_DIM), F32)] * 5
        + [pltpu.VMEM((SEQ, HEAD_DIM), F32)] * 3 + [pltpu.VMEM((SEQ, HEAD_DIM), BF16)] * 3
        + [pltpu.SemaphoreType.DMA((3,))],
        input_output_aliases={9: 0},
        compiler_params=_params(("arbitrary",), 12 * blk + 3 * blk + 3 * blk + 3 * blk + 2 * blk),
    )(proj, proj, proj, gq, gk, _da_slopes(group), do, lse, dd, dproj)


MEM_WAYS = 4


def _mem_probs(q, kn, gq):
    qhat, rq, qn = _da_rmsnorm(q, gq)
    s = _dot(qn, kn, _NT) * MEM_SCALE
    m = jnp.max(s, axis=-1, keepdims=True)
    e = jnp.exp(s - m)
    p = e / jnp.sum(e, axis=-1, keepdims=True)
    return p, qhat, rq, qn


def _mem_pieces(tq):
    rows = tq // MEM_WAYS
    return [slice(w * rows, (w + 1) * rows) for w in range(MEM_WAYS)]


def _mem_fwd(proj, kv, gq, gk, tq):
    def body(q_ref, k_ref, v_ref, gq_ref, gk_ref, o_ref):
        kn = _da_rmsnorm(k_ref[...], gk_ref[...])[2]
        v, gqv = v_ref[...], gq_ref[...]
        pieces = _mem_pieces(tq)
        outs = [_dot(_mem_probs(q_ref[rs, :], kn, gqv)[0], v, _NN) for rs in pieces]
        for rs, o in zip(pieces, outs):
            o_ref[rs, :] = o.astype(BF16)

    gain_spec = pl.BlockSpec((1, HEAD_DIM), lambda h, i: (0, 0))
    return pl.pallas_call(
        body, name="mem_fwd", grid=(MEM_HEADS, SEQ // tq),
        in_specs=[pl.BlockSpec((tq, HEAD_DIM), lambda h, i: (i, CB_MEM_Q + h)),
                  pl.BlockSpec((N_MEM, HEAD_DIM), lambda h, i: (0, h)),
                  pl.BlockSpec((N_MEM, HEAD_DIM), lambda h, i: (0, MEM_HEADS + h)), gain_spec, gain_spec],
        out_specs=pl.BlockSpec((tq, HEAD_DIM), lambda h, i: (i, h)),
        out_shape=jax.ShapeDtypeStruct((SEQ, MEM_WIDTH), BF16),
        compiler_params=_params(("parallel", "parallel"), 16 * tq * N_MEM * 4),
    )(proj, kv, kv, gq, gk)


def _mem_bwd(proj, kv, gq, gk, do, dproj, tq):
    nq = SEQ // tq

    def body(q_ref, k_ref, v_ref, gq_ref, gk_ref, do_ref, dproj_in, dq_ref, dk_ref, dv_ref, dgq_ref, dgk_ref, dkn_scr):
        del dproj_in
        h, i = pl.program_id(0), pl.program_id(1)
        gqv, gkv = gq_ref[...], gk_ref[...]
        khat, rk, kn = _da_rmsnorm(k_ref[...], gkv)
        v = v_ref[...]

        def piece(rs):
            p, qhat, rq, qn = _mem_probs(q_ref[rs, :], kn, gqv)
            dob = do_ref[rs, :]
            dp = _dot(dob, v, _NT)
            ds = p * (dp - jnp.sum(p * dp, axis=-1, keepdims=True)) * MEM_SCALE
            dqn = _dot(ds, kn, _NN)
            dyg = dqn * gqv
            dq = (rq * (dyg - qhat * jnp.mean(dyg * qhat, axis=-1, keepdims=True))).astype(BF16)
            return dq, _dot(p, dob, _TN), _dot(ds, qn, _TN), jnp.sum(dqn * qhat, axis=0, keepdims=True)

        pieces = _mem_pieces(tq)
        results = [piece(rs) for rs in pieces]
        for rs, res in zip(pieces, results):
            dq_ref[rs, :] = res[0]
        dvp = sum((res[1] for res in results[1:]), results[0][1])
        dknp = sum((res[2] for res in results[1:]), results[0][2])
        dgq_part = sum((res[3] for res in results[1:]), results[0][3])
        first = jnp.logical_and(h == 0, i == 0)

        @pl.when(first)
        def _():
            dgq_ref[...] = dgq_part

        @pl.when(jnp.logical_not(first))
        def _():
            dgq_ref[...] += dgq_part

        @pl.when(i == 0)
        def _():
            dv_ref[...] = dvp
            dkn_scr[...] = dknp

        @pl.when(i > 0)
        def _():
            dv_ref[...] += dvp
            dkn_scr[...] += dknp

        @pl.when(i == nq - 1)
        def _():
            dkn = dkn_scr[...]
            dkg = dkn * gkv
            dk_ref[...] = rk * (dkg - khat * jnp.mean(dkg * khat, axis=-1, keepdims=True))
            dgk_part = jnp.sum(dkn * khat, axis=0, keepdims=True)

            @pl.when(h == 0)
            def _():
                dgk_ref[...] = dgk_part

            @pl.when(h > 0)
            def _():
                dgk_ref[...] += dgk_part

    gain_spec = pl.BlockSpec((1, HEAD_DIM), lambda h, i: (0, 0))
    kvout = pl.BlockSpec((N_MEM, HEAD_DIM), lambda h, i: (0, h))
    return pl.pallas_call(
        body, name="mem_bwd", grid=(MEM_HEADS, nq),
        in_specs=[pl.BlockSpec((tq, HEAD_DIM), lambda h, i: (i, CB_MEM_Q + h)),
                  pl.BlockSpec((N_MEM, HEAD_DIM), lambda h, i: (0, h)),
                  pl.BlockSpec((N_MEM, HEAD_DIM), lambda h, i: (0, MEM_HEADS + h)), gain_spec, gain_spec,
                  pl.BlockSpec((tq, HEAD_DIM), lambda h, i: (i, h)), _any_spec()],
        out_specs=[pl.BlockSpec((tq, HEAD_DIM), lambda h, i: (i, CB_MEM_Q + h)), kvout, kvout, gain_spec, gain_spec],
        out_shape=[jax.ShapeDtypeStruct((SEQ, IN_COLS), BF16), jax.ShapeDtypeStruct((N_MEM, MEM_WIDTH), F32),
                   jax.ShapeDtypeStruct((N_MEM, MEM_WIDTH), F32), jax.ShapeDtypeStruct((1, HEAD_DIM), F32),
                   jax.ShapeDtypeStruct((1, HEAD_DIM), F32)],
        scratch_shapes=[pltpu.VMEM((N_MEM, HEAD_DIM), F32)], input_output_aliases={6: 0},
        compiler_params=_params(("arbitrary", "arbitrary"), 24 * tq * N_MEM * 4),
    )(proj, kv, kv, gq, gk, do, dproj)


def _mesh_position():
    return lax.axis_index("x"), lax.axis_index("y"), lax.axis_index("c")


def _any_spec():
    return pl.BlockSpec(memory_space=pl.ANY)


def _all_gather(shards):
    n = len(shards)

    def body(*refs):
        ins, outs = refs[:n], refs[n:2 * n]
        send_sems, recv_sems, local_sems = refs[2 * n:]
        x, y, c = _mesh_position()
        me, sibling = (x, y, c), (x, y, 1 - c)
        first = (jnp.where(c == 0, 1 - x, x), jnp.where(c == 0, y, 1 - y), c)
        second = (jnp.where(c == 0, x, 1 - x), jnp.where(c == 0, 1 - y, y), c)
        diagonal = (1 - x, 1 - y, c)

        def copy(w, k, block, to, src=None):
            px, py, pc = block
            rows = outs[w].at[4 * px + 2 * py + pc]
            return pltpu.make_async_remote_copy(
                src_ref=rows if src is None else src, dst_ref=rows,
                send_sem=send_sems.at[7 * w + k], recv_sem=recv_sems.at[7 * w + k],
                device_id=to, device_id_type=MESH)

        started = []
        for w in range(n):
            mine = pltpu.make_async_copy(ins[w], outs[w].at[4 * x + 2 * y + c], local_sems.at[w])
            mine.start()
            started.append(mine)
        sends = []
        for w in range(n):
            own = [copy(w, 0, me, sibling, src=ins[w]), copy(w, 1, me, first, src=ins[w]),
                   copy(w, 2, me, second, src=ins[w])]
            for cp in own:
                cp.start()
            sends += own
        for w in range(n):
            copy(w, 1, first, me).wait_recv()
            follow = [copy(w, 3, first, second), copy(w, 4, first, sibling)]
            for cp in follow:
                cp.start()
            copy(w, 2, second, me).wait_recv()
            follow.append(copy(w, 5, second, sibling))
            follow[-1].start()
            sends += follow
        for w in range(n):
            copy(w, 3, diagonal, me).wait_recv()
            passed = copy(w, 6, diagonal, sibling)
            passed.start()
            sends.append(passed)
        for w in range(n):
            for k in (0, 4, 5, 6):
                copy(w, k, sibling, me).wait_recv()
        for cp in sends:
            cp.wait_send()
        for mine in started:
            mine.wait()

    return pl.pallas_call(
        body, name="weights_all_gather",
        in_specs=[_any_spec()] * n, out_specs=[_any_spec()] * n,
        out_shape=[jax.ShapeDtypeStruct((N_DEV,) + s.shape, s.dtype) for s in shards],
        scratch_shapes=[pltpu.SemaphoreType.DMA((7 * n,)), pltpu.SemaphoreType.DMA((7 * n,)),
                        pltpu.SemaphoreType.DMA((n,))],
    )(*shards)


def _chip_of(j, x, y):
    return (1 - x if j & 1 else x, 1 - y if j & 2 else y)


_HBM_SPEC = pl.BlockSpec(memory_space=pltpu.HBM)
_SEM_SPEC = pl.BlockSpec(memory_space=pltpu.SEMAPHORE)
_DATAFLOW_EFFECT = pltpu.SideEffectType.DATAFLOW_SIDE_EFFECTING
TOKEN_SHAPE = (8, D_MODEL)


def _copies_start(name, arrays, n_copies, plan):
    n = len(arrays)

    def body(*refs):
        send_sems, recv_sems, token = refs[n], refs[n + 1], refs[2 * n + 2]
        copies = plan(refs[:n])
        assert len(copies) == n_copies
        for k, (src, dst, dev) in enumerate(copies):
            pltpu.make_async_remote_copy(src_ref=src, dst_ref=dst, send_sem=send_sems.at[k], recv_sem=recv_sems.at[k],
                                         device_id=dev, device_id_type=MESH).start()
        token[...] = jnp.zeros_like(token)

    outs = pl.pallas_call(
        body, name=name,
        out_shape=(pltpu.SemaphoreType.DMA((n_copies,)), pltpu.SemaphoreType.DMA((n_copies,)),
                   *[pltpu.HBM(a.shape, a.dtype) for a in arrays], jax.ShapeDtypeStruct(TOKEN_SHAPE, F32)),
        in_specs=[_HBM_SPEC] * n,
        out_specs=(_SEM_SPEC, _SEM_SPEC, *[_HBM_SPEC] * n, pl.BlockSpec(memory_space=pltpu.VMEM)),
        input_output_aliases={i: i + 2 for i in range(n)},
        compiler_params=pltpu.CompilerParams(has_side_effects=_DATAFLOW_EFFECT),
    )(*[pltpu.with_memory_space_constraint(a, pltpu.HBM) for a in arrays])
    return outs[0], outs[1], list(outs[2:2 + n]), outs[2 + n]


def _copies_wait(name, send_sems, recv_sems, arrays, n_copies, plan, after):
    n = len(arrays)
    after = list(after) if isinstance(after, (list, tuple)) else [after]

    def body(*refs):
        send_ref, recv_ref = refs[n], refs[n + 1]
        copies = plan(refs[:n])
        assert len(copies) == n_copies
        for k, (src, dst, dev) in enumerate(copies):
            cp = pltpu.make_async_remote_copy(src_ref=src, dst_ref=dst, send_sem=send_ref.at[k], recv_sem=recv_ref.at[k],
                                              device_id=dev, device_id_type=MESH)
            cp.wait_send()
            cp.wait_recv()

    outs = pl.pallas_call(
        body, name=name, out_shape=tuple(pltpu.HBM(a.shape, a.dtype) for a in arrays),
        in_specs=[_HBM_SPEC] * n + [_SEM_SPEC, _SEM_SPEC] + [pl.BlockSpec(memory_space=pl.ANY)] * len(after),
        out_specs=tuple([_HBM_SPEC] * n), input_output_aliases={i: i for i in range(n)},
        compiler_params=pltpu.CompilerParams(has_side_effects=_DATAFLOW_EFFECT),
    )(*arrays, send_sems, recv_sems, *after)
    return list(outs)


def _after(small, token):
    return small if token is None else small + token[0:1, :small.shape[-1]]


def _gather_plan_out(n):
    def plan(refs):
        x, y, c = _mesh_position()
        me = 4 * x + 2 * y + c
        copies = []
        for w in range(n):
            land = refs[n + w].at[me]
            copies.append((refs[w], land, (x, y, 1 - c)))
            for j in range(1, 4):
                copies.append((refs[w], land, (*_chip_of(j, x, y), c)))
        return copies
    return plan


def _gather_plan_pass(n):
    def plan(refs):
        x, y, c = _mesh_position()
        copies = []
        for w in range(n):
            for j in range(1, 4):
                px, py = _chip_of(j, x, y)
                rows = refs[w].at[4 * px + 2 * py + c]
                copies.append((rows, rows, (x, y, 1 - c)))
        return copies
    return plan


def _reduce_plan_sibling(n):
    def plan(refs):
        x, y, c = _mesh_position()
        copies = []
        for w in range(n):
            for j in range(4):
                px, py = _chip_of(j, x, y)
                copies.append((refs[w].at[4 * px + 2 * py + (1 - c)], refs[n + w].at[j], (x, y, 1 - c)))
        return copies
    return plan


def _reduce_plan_chips(n):
    def plan(refs):
        x, y, c = _mesh_position()
        copies = []
        for w in range(n):
            for j in range(1, 4):
                copies.append((refs[w].at[j - 1], refs[n + w].at[j - 1], (*_chip_of(j, x, y), c)))
        return copies
    return plan


def _chip_partials(grad, recv, name, tr):
    _, rows, width = grad.shape

    def body(g_ref, r_ref, own_ref, other_ref):
        x, y, c = _mesh_position()
        for j in range(4):
            px, py = _chip_of(j, x, y)
            total = g_ref[4 * px + 2 * py + c].astype(F32) + r_ref[j].astype(F32)
            if j == 0:
                own_ref[...] = total
            else:
                other_ref[j - 1] = total.astype(BF16)

    return pl.pallas_call(
        body, name=name, grid=(rows // tr,),
        in_specs=[pl.BlockSpec((N_DEV, tr, width), lambda i: (0, i, 0)),
                  pl.BlockSpec((4, tr, width), lambda i: (0, i, 0))],
        out_specs=[pl.BlockSpec((tr, width), lambda i: (i, 0)), pl.BlockSpec((3, tr, width), lambda i: (0, i, 0))],
        out_shape=[jax.ShapeDtypeStruct((rows, width), F32), jax.ShapeDtypeStruct((3, rows, width), BF16)],
        compiler_params=_params(("parallel",), 2 * 20 * tr * width * 2 + 8 * tr * width * 4),
    )(grad, recv)


def _adamw_math(w, g, m, v):
    m = ADAM_B1 * m + (1.0 - ADAM_B1) * g
    v = ADAM_B2 * v + (1.0 - ADAM_B2) * (g * g)
    m_hat = m / (1.0 - ADAM_B1 ** ADAM_STEP)
    v_hat = v / (1.0 - ADAM_B2 ** ADAM_STEP)
    delta = -ADAM_LR * (m_hat / (jnp.sqrt(v_hat) + ADAM_EPS) + ADAM_WD * w)
    return delta, m, v


def _adamw_shard(own, recv, w, m, v, name, tr):
    rows, width = own.shape

    def body(own_ref, r_ref, w_ref, m_ref, v_ref, g_ref, d_ref, nm_ref, nv_ref):
        g = own_ref[...]
        for j in range(3):
            g = g + r_ref[j].astype(F32)
        g_ref[...] = g
        d_ref[...], nm_ref[...], nv_ref[...] = _adamw_math(w_ref[...], g, m_ref[...], v_ref[...])

    spec = pl.BlockSpec((tr, width), lambda i: (i, 0))
    shape = jax.ShapeDtypeStruct((rows, width), F32)
    return pl.pallas_call(
        body, name=name, grid=(rows // tr,),
        in_specs=[spec, pl.BlockSpec((3, tr, width), lambda i: (0, i, 0)), spec, spec, spec],
        out_specs=[spec] * 4, out_shape=[shape] * 4,
        compiler_params=_params(("parallel",), 22 * tr * width * 4),
    )(own, recv, w, m, v)


def _small_all_reduce_adamw(gpart, lbpack, small_w, small_m, small_v, after):
    count = len(_SMALL_NAMES)
    shapes = [small_w[n].shape for n in _SMALL_NAMES]

    def row_map():
        for i, (name, (rows, cols)) in enumerate(zip(_SMALL_NAMES, shapes)):
            per = cols // LANE
            for s in range(rows):
                for j in range(per):
                    yield i, slice(s, s + 1), slice(j * LANE, (j + 1) * LANE), _SMALL_ROW0[name] + s * per + j

    def body(gp_ref, lb_ref, *refs):
        ins = [refs[k * count:(k + 1) * count] for k in range(3)]
        refs = refs[3 * count + 1:]
        outs = [refs[k * count:(k + 1) * count] for k in range(4)]
        loss_ref = refs[4 * count]
        packs = refs[4 * count + 1:4 * count + 5]
        gath_ref, send_sems, recv_sems = refs[4 * count + 5:]
        for k in range(3):
            packs[k][...] = jnp.zeros_like(packs[k])
            for i, row, lanes, r in row_map():
                packs[k][r:r + 1, :] = ins[k][i][row, lanes]
        x, y, c = _mesh_position()
        me = 4 * x + 2 * y + c
        gath_ref[me] = gp_ref[...]
        copies = []
        for j in range(1, N_DEV):
            peer = (x ^ (j >> 2), y ^ ((j >> 1) & 1), c ^ (j & 1))
            cp = pltpu.make_async_remote_copy(
                src_ref=gp_ref, dst_ref=gath_ref.at[me], send_sem=send_sems.at[j - 1], recv_sem=recv_sems.at[j - 1],
                device_id=peer, device_id_type=MESH)
            cp.start()
            copies.append(cp)
        for cp in copies:
            cp.wait()
        tot = gath_ref[0]
        for s in range(1, N_DEV):
            tot = tot + gath_ref[s]
        lb = lb_ref[...]
        dl = tot[16:32, :] * lb * (1.0 - lb)
        g = jnp.concatenate([tot[0:16, :], dl[0:8, :], -dl[0:8, :], dl[8:16, :], -dl[8:16, :],
                             tot[32:SMALL_GRAD_ROWS, :]], axis=0)
        results = (g,) + tuple(_adamw_math(packs[0][...], g, packs[1][...], packs[2][...]))
        loss_ref[...] = g[_LOSS_ROW:_LOSS_ROW + 8, :]
        for k, result in enumerate(results):
            packs[k][...] = result
            for i, row, lanes, r in row_map():
                outs[k][i][row, lanes] = packs[k][r:r + 1, :]

    vm = pl.BlockSpec(memory_space=pltpu.VMEM)
    flat = pl.pallas_call(
        body, name="small_all_reduce_adamw", grid=(1,), in_specs=[vm] * (2 + 3 * count) + [_any_spec()],
        out_specs=[pl.BlockSpec(s, lambda i: (0, 0)) for s in shapes * 4 + [(8, LANE)]],
        out_shape=[jax.ShapeDtypeStruct(s, F32) for s in shapes * 4 + [(8, LANE)]],
        scratch_shapes=[pltpu.VMEM((SMALL_ROWS, LANE), F32)] * 4
        + [pltpu.VMEM((N_DEV, SMALL_GRAD_ROWS, LANE), F32),
           pltpu.SemaphoreType.DMA((N_DEV - 1,)), pltpu.SemaphoreType.DMA((N_DEV - 1,))],
    )(gpart, lbpack, *[p[n] for p in (small_w, small_m, small_v) for n in _SMALL_NAMES], after)
    return flat[4 * count][0, 0], [dict(zip(_SMALL_NAMES, flat[k * count:(k + 1) * count])) for k in range(4)]


_SMALL_NAMES = ("norm_mix_gain", "norm_mem_gain", "lb_logits_fw", "lb_logits_bw", "norm_ffn_gain",
                "hg_norm_gain", "da_q_gain", "da_k_gain", "mem_q_gain", "mem_k_gain")
_SMALL_ROW0 = {"norm_mix_gain": 0, "norm_mem_gain": 8, "lb_logits_fw": 16, "lb_logits_bw": 32, "norm_ffn_gain": 48,
               "hg_norm_gain": 56, "da_q_gain": 64, "da_k_gain": 72, "mem_q_gain": 80, "mem_k_gain": 88}
_LOSS_ROW = 96


def _pack_rows(parts, total_rows):
    rows = []
    for p in parts:
        r = p.reshape(-1, LANE)
        rows.append(jnp.pad(r, ((0, -r.shape[0] % 8), (0, 0))))
    used = sum(r.shape[0] for r in rows)
    if total_rows > used:
        rows.append(jnp.zeros((total_rows - used, LANE), F32))
    return jnp.concatenate(rows, axis=0)


def kernel(x, mem, norm_mix_gain, norm_mem_gain, w_in, lb_logits_fw, lb_logits_bw, hg_norm_gain, da_q_gain, da_k_gain, w_mem_kv, mem_q_gain, mem_k_gain, w_proj_hg, w_proj_da, w_proj_mem, w_out, norm_ffn_gain, w_ffn_in, w_ffn_out, loss_target, m_norm_mix_gain, m_norm_mem_gain, m_w_in, m_lb_logits_fw, m_lb_logits_bw, m_hg_norm_gain, m_da_q_gain, m_da_k_gain, m_w_mem_kv, m_mem_q_gain, m_mem_k_gain, m_w_proj_hg, m_w_proj_da, m_w_proj_mem, m_w_out, m_norm_ffn_gain, m_w_ffn_in, m_w_ffn_out, v_norm_mix_gain, v_norm_mem_gain, v_w_in, v_lb_logits_fw, v_lb_logits_bw, v_hg_norm_gain, v_da_q_gain, v_da_k_gain, v_w_mem_kv, v_mem_q_gain, v_mem_k_gain, v_w_proj_hg, v_w_proj_da, v_w_proj_mem, v_w_out, v_norm_ffn_gain, v_w_ffn_in, v_w_ffn_out):
    small_w = dict(norm_mix_gain=norm_mix_gain, norm_mem_gain=norm_mem_gain, lb_logits_fw=lb_logits_fw,
                   lb_logits_bw=lb_logits_bw, norm_ffn_gain=norm_ffn_gain, hg_norm_gain=hg_norm_gain,
                   da_q_gain=da_q_gain, da_k_gain=da_k_gain, mem_q_gain=mem_q_gain, mem_k_gain=mem_k_gain)
    small_m = dict(norm_mix_gain=m_norm_mix_gain, norm_mem_gain=m_norm_mem_gain, lb_logits_fw=m_lb_logits_fw,
                   lb_logits_bw=m_lb_logits_bw, norm_ffn_gain=m_norm_ffn_gain, hg_norm_gain=m_hg_norm_gain,
                   da_q_gain=m_da_q_gain, da_k_gain=m_da_k_gain, mem_q_gain=m_mem_q_gain, mem_k_gain=m_mem_k_gain)
    small_v = dict(norm_mix_gain=v_norm_mix_gain, norm_mem_gain=v_norm_mem_gain, lb_logits_fw=v_lb_logits_fw,
                   lb_logits_bw=v_lb_logits_bw, norm_ffn_gain=v_norm_ffn_gain, hg_norm_gain=v_hg_norm_gain,
                   da_q_gain=v_da_q_gain, da_k_gain=v_da_k_gain, mem_q_gain=v_mem_q_gain, mem_k_gain=v_mem_k_gain)
    big_w = dict(w_in=w_in[0], w_mem_kv=w_mem_kv[0], w_proj_hg=w_proj_hg[0], w_proj_da=w_proj_da[0],
                 w_proj_mem=w_proj_mem[0], w_out=w_out[0], w_ffn_in=w_ffn_in[0], w_ffn_out=w_ffn_out[0])
    big_m = dict(w_in=m_w_in[0], w_mem_kv=m_w_mem_kv[0], w_proj_hg=m_w_proj_hg[0], w_proj_da=m_w_proj_da[0],
                 w_proj_mem=m_w_proj_mem[0], w_out=m_w_out[0], w_ffn_in=m_w_ffn_in[0], w_ffn_out=m_w_ffn_out[0])
    big_v = dict(w_in=v_w_in[0], w_mem_kv=v_w_mem_kv[0], w_proj_hg=v_w_proj_hg[0], w_proj_da=v_w_proj_da[0],
                 w_proj_mem=v_w_proj_mem[0], w_out=v_w_out[0], w_ffn_in=v_w_ffn_in[0], w_ffn_out=v_w_ffn_out[0])

    row_tile = dict(w_in=128, w_mem_kv=128, w_proj_hg=128, w_proj_da=512, w_proj_mem=512, w_out=128,
                    w_ffn_in=128, w_ffn_out=352)
    rest = _BIG_NAMES[1:]
    shards = [big_w[n].astype(BF16) for n in rest]
    state = {}
    big_out = {}

    def reduce_start(group, stacked):
        names = tuple(stacked)
        arrays = [stacked[n] for n in names] + [lax.empty((4,) + stacked[n].shape[1:], BF16) for n in names]
        plan = _reduce_plan_sibling(len(names))
        send, recv, thru, token = _copies_start(f"grads_{group}_sibling_start", arrays, 4 * len(names), plan)
        state[group] = dict(names=names, plan=plan, send=send, recv=recv, arrays=thru)
        return token

    def reduce_middle(group, after):
        st = state[group]
        names, k = st["names"], len(st["names"])
        thru = _copies_wait(f"grads_{group}_sibling_wait", st["send"], st["recv"], st["arrays"], 4 * k, st["plan"], after)
        partials = [_chip_partials(thru[i], thru[k + i], f"chip_partials_{n}", row_tile[n]) for i, n in enumerate(names)]
        arrays = [p[1] for p in partials] + [lax.empty(p[1].shape, BF16) for p in partials]
        plan = _reduce_plan_chips(k)
        send, recv, thru2, token = _copies_start(f"grads_{group}_chips_start", arrays, 3 * k, plan)
        state[group] = dict(names=names, plan=plan, send=send, recv=recv, arrays=thru2, own=[p[0] for p in partials])
        return token

    def reduce_finish(group, after):
        st = state.pop(group)
        names, k = st["names"], len(st["names"])
        thru = _copies_wait(f"grads_{group}_chips_wait", st["send"], st["recv"], st["arrays"], 3 * k, st["plan"], after)
        for i, n in enumerate(names):
            big_out[n] = _adamw_shard(st["own"][i], thru[k + i], big_w[n], big_m[n], big_v[n], "adamw_" + n, row_tile[n])

    step = _local_step_stages(x[0], mem[0], loss_target[0], small_w)
    event, payload = next(step)
    local = None
    while True:
        reply = None
        if event == "gather_w_in":
            reply = _all_gather([big_w["w_in"].astype(BF16)])[0]
        elif event == "begin":
            nr = len(rest)
            me = 4 * lax.axis_index("x") + 2 * lax.axis_index("y") + lax.axis_index("c")
            arrays = shards + [lax.dynamic_update_slice(lax.empty((N_DEV,) + s.shape, BF16), s[None], (me, 0, 0))
                               for s in shards]
            plan = _gather_plan_out(nr)
            send, recv, thru, reply = _copies_start("weights_rest_out_start", arrays, 4 * nr, plan)
            state["gather"] = dict(plan=plan, send=send, recv=recv, arrays=thru)
        elif event == "after_gla_fwd":
            st = state["gather"]
            nr = len(rest)
            thru = _copies_wait("weights_rest_out_wait", st["send"], st["recv"], st["arrays"], 4 * nr, st["plan"], payload)
            plan = _gather_plan_pass(nr)
            send, recv, lands, reply = _copies_start("weights_rest_pass_start", thru[nr:], 3 * nr, plan)
            state["gather"] = dict(plan=plan, send=send, recv=recv, arrays=lands)
        elif event == "need_weights":
            st = state.pop("gather")
            nr = len(rest)
            lands = _copies_wait("weights_rest_pass_wait", st["send"], st["recv"], st["arrays"], 3 * nr, st["plan"], payload)
            reply = dict(zip(rest, lands))
        elif event == "grads_ffn":
            reply = reduce_start("ffn", payload)
        elif event == "after_gate_merge_bwd":
            reply = reduce_middle("ffn", payload)
        elif event == "grads_mix":
            reply = reduce_start("mix", payload)
        elif event == "after_da_bwd_g0":
            reply = reduce_middle("mix", payload)
            reduce_finish("ffn", payload)
        elif event == "after_gla_bwd":
            reduce_finish("mix", payload)
        elif event == "grads_in":
            reply = reduce_start("in", payload)
        elif event == "after_proj_bwd_act_top":
            reply = reduce_middle("in", payload)
        elif event == "end":
            local = payload
            reduce_finish("in", [local["grad_x"]] + [big_out[n][0] for n in rest])
            break
        event, payload = step.send(reply)
    grad_x = local["grad_x"]

    loss, small_out = _small_all_reduce_adamw(local["gpart"], local["lbpack"], small_w, small_m, small_v,
                                              big_out["w_in"][0])

    order = ("norm_mix_gain", "norm_mem_gain", "w_in", "lb_logits_fw", "lb_logits_bw", "hg_norm_gain", "da_q_gain",
             "da_k_gain", "w_mem_kv", "mem_q_gain", "mem_k_gain", "w_proj_hg", "w_proj_da", "w_proj_mem", "w_out",
             "norm_ffn_gain", "w_ffn_in", "w_ffn_out")
    outs = [loss, grad_x[None]]
    for kind in range(4):
        for n in order:
            outs.append(big_out[n][kind][None] if n in big_out else small_out[kind][n])
    return tuple(outs)


_BIG_NAMES = ("w_in", "w_mem_kv", "w_proj_hg", "w_proj_da", "w_proj_mem", "w_out", "w_ffn_in", "w_ffn_out")


def _local_step(xs, mems, target, sw, wg):
    step = _local_step_stages(xs, mems, target, sw)
    stacked = {}
    event, payload = next(step)
    while event != "end":
        reply = None
        if event.startswith("grads_"):
            stacked.update(payload)
        elif event == "need_weights":
            reply = wg
        elif event == "gather_w_in":
            reply = wg["w_in"]
        event, payload = step.send(reply)
    return dict(payload, stacked=stacked)


def _local_step_stages(xs, mems, target, sw):
    norm_mix_gain, norm_mem_gain, norm_ffn_gain = sw["norm_mix_gain"], sw["norm_mem_gain"], sw["norm_ffn_gain"]
    lb_logits_fw, lb_logits_bw, hg_norm_gain = sw["lb_logits_fw"], sw["lb_logits_bw"], sw["hg_norm_gain"]
    da_q_gain, da_k_gain, mem_q_gain, mem_k_gain = sw["da_q_gain"], sw["da_k_gain"], sw["mem_q_gain"], sw["mem_k_gain"]

    win_st = yield "gather_w_in", None
    token = yield "begin", None
    lb_fw = _lb_table(lb_logits_fw, "lb_table_fw")
    lb_bw = _lb_table(lb_logits_bw, "lb_table_bw")
    lb = jnp.concatenate([lb_fw, lb_bw], axis=0).reshape(2, HG_HEADS, 1, HEAD_DIM)
    h, h_t = _rmsnorm_fwd(xs, _after(norm_mix_gain, token), "norm_mix_fwd", 512, transposed=True)
    proj = _matmul(h, win_st, "nn", F32, 2048, IN_SHARD, D_MODEL, "proj_fwd", b_stacked=True, n_outer=True)
    hg_b = _gla_decay(proj, lb)
    o_hg, o_pre, states, hg_qt, hg_decay = _gla_fwd(proj, lb, hg_norm_gain, hg_b)
    token = yield "after_gla_fwd", o_hg
    da01 = [_da_fwd(proj, _after(da_q_gain, token), da_k_gain, g) for g in (0, 1)]
    o_da, o_da32, lse_da = _da_fwd(proj, _after(da_q_gain, token), da_k_gain, 2, others=da01)
    wg = yield "need_weights", o_da
    wkv = wg["w_mem_kv"].reshape(D_MODEL, 2 * MEM_WIDTH)
    wphg = wg["w_proj_hg"].reshape(D_MODEL, D_MODEL)
    wpda = jnp.transpose(wg["w_proj_da"], (1, 0, 2)).reshape(DA_WIDTH, D_MODEL)
    wpmem = jnp.transpose(wg["w_proj_mem"], (1, 0, 2)).reshape(MEM_WIDTH, D_MODEL)
    wout = wg["w_out"].reshape(D_MODEL, D_MODEL)
    wfin = jnp.transpose(wg["w_ffn_in"], (1, 0, 2)).reshape(D_MODEL, 2 * D_FF)
    wfout = wg["w_ffn_out"].reshape(D_FF, D_MODEL)
    mem_n = _rmsnorm_fwd(mems, norm_mem_gain, "norm_mem_fwd", N_MEM)
    kv = _matmul(mem_n, wkv, "nn", F32, N_MEM, 1024, D_MODEL, "mem_kv_fwd")
    o_mem = _mem_fwd(proj, kv, mem_q_gain, mem_k_gain, 1024)
    branch_w = (wphg, wpda, wpmem)
    merged, t_hg, t_da, t_mem = _branch_merge_fwd(proj, (o_hg, o_da, o_mem), branch_w, 512)
    x1, h2, h2_t = _residual_rmsnorm_fwd(xs, merged, wout, norm_ffn_gain, "out_norm_ffn_fwd", 512)
    ffn_a, ffn_b, act, act_t = _ffn_in_swiglu(h2, wfin, 1024, 1408)
    dy, dyb, loss_part = _ffn_out_loss_head(x1, act, wfout, target, 512)

    dab = _ffn_out_bwd_swiglu(dyb, wfout, ffn_a, ffn_b, 1024, 1408)
    g_wfout = _matmul(act_t, dyb, "nn", BF16, 1408, 1024, 2048, "ffn_out_bwd_w")
    g_wfin = _matmul(h2_t, dab, "nn", BF16, 1024, 1408, 2048, "ffn_in_bwd_w", b_parts=True)
    token = yield "grads_ffn", dict(
        w_ffn_in=jnp.transpose(g_wfin.reshape(D_MODEL, N_DEV, 2 * D_FF // N_DEV), (1, 0, 2)),
        w_ffn_out=g_wfout.reshape(N_DEV, D_FF // N_DEV, D_MODEL))
    dx1, dx1b, g_norm_ffn = _matmul_rmsnorm_bwd(dab, wfin, x1, dy, _after(norm_ffn_gain, token),
                                                "ffn_in_bwd_act_norm", 512, D_FF, a_parts=True)
    g_wout = _matmul(merged, dx1b, "tn", BF16, 1024, 1024, 2048, "out_bwd_w")
    dt_hg, dt_da, dt_mem, do_pre, do_da, do_mem, dd_da, g_hg_norm, dproj = _gate_merge_bwd(
        proj, (t_hg, t_da, t_mem), branch_w, dx1b, wout, o_pre, hg_norm_gain, o_da32, _dproj_buffer(), 256)
    token = yield "after_gate_merge_bwd", dt_hg
    g_wphg = _matmul(o_hg, dt_hg, "tn", BF16, 1024, 1024, 2048, "proj_hg_bwd_w", after=token)
    g_wpda = _matmul(o_da, dt_da, "tn", BF16, DA_WIDTH, D_MODEL, 2048, "proj_da_bwd_w")
    g_wpmem = _matmul(o_mem, dt_mem, "tn", BF16, MEM_WIDTH, D_MODEL, 2048, "proj_mem_bwd_w")
    by_owner = lambda g: jnp.transpose(g.reshape(g.shape[0], N_DEV, D_MODEL // N_DEV), (1, 0, 2))
    g_wpda, g_wpmem = by_owner(g_wpda), by_owner(g_wpmem)

    dproj, dk_mem, dv_mem, g_mem_q, g_mem_k = _mem_bwd(proj, kv, mem_q_gain, mem_k_gain, do_mem, dproj, 1024)
    dkv = jnp.concatenate([dk_mem, dv_mem], axis=1).astype(BF16)
    g_wkv = _matmul(mem_n, dkv, "tn", BF16, 1024, 1024, N_MEM, "mem_kv_bwd_w")
    dmem_n = _matmul(dkv, wkv, "nt", F32, N_MEM, 1024, 1024, "mem_kv_bwd_act")
    g_norm_mem = _gain_grad(mems, dmem_n, "norm_mem_bwd")
    token = yield "grads_mix", dict(
        w_mem_kv=g_wkv.reshape(N_DEV, D_MODEL // N_DEV, 2 * MEM_WIDTH),
        w_proj_hg=g_wphg.reshape(N_DEV, D_MODEL // N_DEV, D_MODEL),
        w_proj_da=g_wpda, w_proj_mem=g_wpmem,
        w_out=g_wout.reshape(N_DEV, D_MODEL // N_DEV, D_MODEL))

    dproj, g_da_q, g_da_k = _da_bwd(proj, _after(da_q_gain, token), da_k_gain, do_da, lse_da, dd_da, dproj, 0)
    token = yield "after_da_bwd_g0", g_da_q
    for g in (1, 2):
        dproj, gq_part, gk_part = _da_bwd(proj, _after(da_q_gain, token), da_k_gain, do_da, lse_da, dd_da, dproj, g)
        g_da_q, g_da_k = g_da_q + gq_part, g_da_k + gk_part

    dproj, dlb = _gla_bwd(proj, lb, states, hg_qt, hg_decay, hg_b, do_pre, dproj)
    yield "after_gla_bwd", dlb

    g_win =_matmul(h_t, dproj, "nn", BF16, 1024, IN_SHARD, 2048, "proj_bwd_w", out_stacked=True)
    token = yield "grads_in", dict(w_in=g_win)
    grad_x, g_mix_top = _matmul_rmsnorm_bwd(dproj, win_st, xs, dx1, norm_mix_gain, "proj_bwd_act_norm_top", 1024,
                                            IN_SHARD, b_stacked=True, after=token, m_blocks=(0, 1), with_bf16=False)
    token = yield "after_proj_bwd_act_top", g_mix_top
    grad_x, g_mix_bottom = _matmul_rmsnorm_bwd(dproj, win_st, xs, dx1, norm_mix_gain, "proj_bwd_act_norm_bottom", 1024,
                                               IN_SHARD, b_stacked=True, after=token, m_blocks=(1, 3),
                                               out_into=grad_x, with_bf16=False)
    g_norm_mix = g_mix_top + g_mix_bottom

    gpart = _pack_rows([g_norm_mix, g_norm_mem, dlb[0], dlb[1], g_norm_ffn, g_hg_norm, g_da_q, g_da_k,
                        g_mem_q, g_mem_k, loss_part], SMALL_GRAD_ROWS)
    lbpack = jnp.concatenate([lb_fw.reshape(8, LANE), lb_bw.reshape(8, LANE)], axis=0)
    yield "end", dict(grad_x=grad_x, gpart=gpart, lbpack=lbpack)
```

```python
import numpy as np
import jax
import jax.numpy as jnp
from jax import lax
from jax.experimental import pallas as pl
from jax.experimental.pallas import tpu as pltpu

F32 = jnp.float32
BF16 = jnp.bfloat16
MESH = pl.DeviceIdType.MESH

SEQ = 4096
D_MODEL = 1024
N_DEV = 8
N_MEM = 256
RMS_EPS = 1e-6
NEG_INF = -1e30
LANE = 128
HEAD_DIM = 128
HG_HEADS = 8
HG_CHUNK = 64
HG_SCALE = HEAD_DIM ** -0.5
DA_DILATIONS = (1, 4, 16)
DA_RADIUS = 64
DA_HEADS_PER_GROUP = 4
DA_HEADS = 12
DA_WIDTH = 512
DA_SCALE = HEAD_DIM ** -0.5
DA_QB = 128
DA_WIN = 256
DA_WAYS = 8
DA_FWD_WAYS = 8
DA_NORM_ROWS = 2048
MEM_HEADS = 4
MEM_WIDTH = 512
MEM_SCALE = HEAD_DIM ** -0.5
D_FF = 2816
IN_COLS = 13312
IN_SHARD = IN_COLS // N_DEV
CB_HG_Q, CB_F, CB_HG_I, CB_HG_G = 0, 8, 24, 32
CB_DA_Q, CB_DA_K, CB_DA_V, CB_MEM_Q = 40, 52, 64, 76
ADAM_LR, ADAM_B1, ADAM_B2, ADAM_EPS, ADAM_WD, ADAM_STEP = 0.001, 0.9, 0.999, 1e-08, 0.01, 10
VMEM_BYTES_V7X = 64 * 1024 * 1024
SMALL_ROWS = 104
SMALL_GRAD_ROWS = 88

_NN = (((1,), (0,)), ((), ()))
_NT = (((1,), (1,)), ((), ()))
_TN = (((0,), (0,)), ((), ()))


def _dot(a, b, dims):
    return lax.dot_general(a.astype(BF16), b.astype(BF16), dims, preferred_element_type=F32)


def _sigmoid(x):
    return 0.5 * jnp.tanh(0.5 * x) + 0.5


def _params(semantics, est_bytes):
    limit = int(min(VMEM_BYTES_V7X - (6 << 20), max(56 << 20, est_bytes * 3 // 2)))
    return pltpu.CompilerParams(dimension_semantics=semantics, vmem_limit_bytes=limit)


def _nbytes(shape, dtype):
    return int(np.prod(shape)) * jnp.dtype(dtype).itemsize


def _alibi_slopes(n):
    return (2.0 ** (-8.0 * np.arange(1, n + 1) / n)).astype(np.float32)


def _matmul(a, b, mode, out_dtype, tm, tn, tk, name, b_stacked=False, out_stacked=False, n_outer=False, after=None,
            m_blocks=None, out_into=None, a_parts=False, b_parts=False):
    if a_parts:
        assert mode == "nt" and tk == a.shape[2]
        m, kdim = a.shape[1], a.shape[0] * a.shape[2]
    elif mode == "tn":
        kdim, m = a.shape
    else:
        m, kdim = a.shape
    if b_parts:
        assert mode == "nn" and not b_stacked and b.shape[2] % tn == 0
        n = b.shape[0] * b.shape[2]
    elif b_stacked:
        if mode == "nn":
            n = b.shape[0] * b.shape[2]
            assert tn == b.shape[2] and tk == kdim == b.shape[1]
        else:
            assert mode == "nt" and tk == b.shape[2] and b.shape[0] * tk == kdim
            n = b.shape[1]
    else:
        n = b.shape[0] if mode == "nt" else b.shape[1]
    assert m % tm == 0 and n % tn == 0 and kdim % tk == 0
    gm, gn, gk = m // tm, n // tn, kdim // tk
    i0 = 0
    if m_blocks is not None:
        assert mode != "tn" and not out_stacked
        i0, gm = m_blocks

    def ijk(f):
        if n_outer:
            return lambda j, i, k: f(i + i0, j, k)
        return lambda i, j, k: f(i + i0, j, k)

    if a_parts:
        a_spec = pl.BlockSpec((None, tm, tk), ijk(lambda i, j, k: (k, i, 0)))
    elif mode == "tn":
        a_spec = pl.BlockSpec((tk, tm), ijk(lambda i, j, k: (k, i)))
    else:
        a_spec = pl.BlockSpec((tm, tk), ijk(lambda i, j, k: (i, k)))
    if b_parts:
        per_part = b.shape[2] // tn
        b_spec = pl.BlockSpec((None, tk, tn), ijk(lambda i, j, k: (j // per_part, k, j % per_part)))
    elif b_stacked and mode == "nn":
        b_spec = pl.BlockSpec((None, tk, tn), ijk(lambda i, j, k: (j, 0, 0)))
    elif b_stacked:
        b_spec = pl.BlockSpec((None, tn, tk), ijk(lambda i, j, k: (k, j, 0)))
    elif mode == "nt":
        b_spec = pl.BlockSpec((tn, tk), ijk(lambda i, j, k: (j, k)))
    else:
        b_spec = pl.BlockSpec((tk, tn), ijk(lambda i, j, k: (k, j)))
    if out_stacked:
        assert tm == m
        out_shape = jax.ShapeDtypeStruct((gn, m, tn), out_dtype)
        o_spec = pl.BlockSpec((None, tm, tn), ijk(lambda i, j, k: (j, i, 0)))
    else:
        out_shape = jax.ShapeDtypeStruct((m, n), out_dtype)
        o_spec = pl.BlockSpec((tm, tn), ijk(lambda i, j, k: (i, j)))
    dims = {"nn": _NN, "nt": _NT, "tn": _TN}[mode]

    n_in = 2 + (after is not None) + (out_into is not None)

    def body(*refs):
        a_ref, b_ref, o_ref = refs[0], refs[1], refs[n_in]
        part = _dot(a_ref[...], b_ref[...], dims)
        if gk == 1:
            o_ref[...] = part.astype(out_dtype)
            return
        acc_ref = refs[-1]
        k = pl.program_id(2)

        @pl.when(k == 0)
        def _():
            acc_ref[...] = part

        @pl.when(jnp.logical_and(k > 0, k < gk - 1))
        def _():
            acc_ref[...] += part

        @pl.when(k == gk - 1)
        def _():
            o_ref[...] = (acc_ref[...] + part).astype(out_dtype)

    a_tile = _nbytes((tm, tk), a.dtype)
    b_tile = _nbytes((tk, tn), b.dtype)
    o_tile = _nbytes((tm, tn), out_dtype)
    est = 2 * (a_tile + b_tile + o_tile) + 3 * tm * tn * 4 + (a_tile + b_tile)
    grid = (gn, gm, gk) if n_outer else (gm, gn, gk)
    operands, in_specs = [a, b], [a_spec, b_spec]
    if after is not None:
        operands.append(after)
        in_specs.append(pl.BlockSpec(memory_space=pl.ANY))
    aliases = {}
    if out_into is not None:
        aliases = {len(operands): 0}
        operands.append(out_into)
        in_specs.append(pl.BlockSpec(memory_space=pl.ANY))
    return pl.pallas_call(
        body, name=name, grid=grid, in_specs=in_specs, out_specs=o_spec, out_shape=out_shape,
        scratch_shapes=[] if gk == 1 else [pltpu.VMEM((tm, tn), F32)], input_output_aliases=aliases,
        compiler_params=_params(("parallel", "parallel", "arbitrary"), est),
    )(*operands)


def _row_spec(tr, width, col_block=0):
    return pl.BlockSpec((tr, width), lambda i: (i, col_block))


def _bcast_spec(width):
    return pl.BlockSpec((1, width), lambda i: (0, 0))


def _col_spec(width, tr):
    return pl.BlockSpec((width, tr), lambda i: (0, i))


def _rmsnorm_fwd(x, gain, name, tr, transposed=False):
    rows, width = x.shape

    def body(x_ref, g_ref, o_ref, *t_ref):
        xv = x_ref[...]
        r = lax.rsqrt(jnp.mean(xv * xv, axis=-1, keepdims=True) + RMS_EPS)
        h = xv * r * g_ref[...]
        o_ref[...] = h.astype(BF16)
        if transposed:
            t_ref[0][...] = h.T.astype(BF16)

    out_specs, out_shape = [_row_spec(tr, width)], [jax.ShapeDtypeStruct((rows, width), BF16)]
    if transposed:
        out_specs.append(_col_spec(width, tr))
        out_shape.append(jax.ShapeDtypeStruct((width, rows), BF16))
    out = pl.pallas_call(
        body, name=name, grid=(rows // tr,), in_specs=[_row_spec(tr, width), _bcast_spec(width)],
        out_specs=out_specs, out_shape=out_shape,
        compiler_params=_params(("parallel",), 10 * tr * width * 4),
    )(x, gain)
    return out if transposed else out[0]


def _residual_rmsnorm_fwd(x, merged, wout, gain, name, tr):
    rows, width = x.shape

    def body(x_ref, m_ref, w_ref, g_ref, x1_ref, h_ref, ht_ref):
        xv = x_ref[...] + _dot(m_ref[...], w_ref[...], _NN)
        x1_ref[...] = xv
        r = lax.rsqrt(jnp.mean(xv * xv, axis=-1, keepdims=True) + RMS_EPS)
        h = xv * r * g_ref[...]
        h_ref[...] = h.astype(BF16)
        ht_ref[...] = h.T.astype(BF16)

    return pl.pallas_call(
        body, name=name, grid=(rows // tr,),
        in_specs=[_row_spec(tr, width), _row_spec(tr, merged.shape[1]),
                  pl.BlockSpec(wout.shape, lambda i: (0, 0)), _bcast_spec(width)],
        out_specs=[_row_spec(tr, width), _row_spec(tr, width), _col_spec(width, tr)],
        out_shape=[jax.ShapeDtypeStruct((rows, width), F32), jax.ShapeDtypeStruct((rows, width), BF16),
                   jax.ShapeDtypeStruct((width, rows), BF16)],
        compiler_params=_params(("parallel",), 14 * tr * width * 4),
    )(x, merged, wout, gain)


def _matmul_rmsnorm_bwd(a, b, x, dres, gain, name, tm, tk, a_parts=False, b_stacked=False, after=None,
                        m_blocks=None, out_into=None, with_bf16=True):
    width = x.shape[1]
    m = a.shape[1] if a_parts else a.shape[0]
    kdim = a.shape[0] * a.shape[2] if a_parts else a.shape[1]
    gk = kdim // tk
    i0, gm = (0, m // tm) if m_blocks is None else m_blocks
    n_in = 5 + (after is not None) + (out_into is not None)

    def body(*refs):
        a_ref, b_ref, x_ref, dres_ref, g_ref = refs[:5]
        outs = refs[n_in:]
        dx_ref, dg_ref, acc_ref = outs[0], outs[-2], outs[-1]
        i, k = pl.program_id(0), pl.program_id(1)
        part = _dot(a_ref[...], b_ref[...], _NT)

        @pl.when(k == 0)
        def _():
            acc_ref[...] = part

        @pl.when(jnp.logical_and(k > 0, k < gk - 1))
        def _():
            acc_ref[...] += part

        @pl.when(k == gk - 1)
        def _():
            dhv = acc_ref[...] + part if gk > 1 else part
            xv = x_ref[...]
            r = lax.rsqrt(jnp.mean(xv * xv, axis=-1, keepdims=True) + RMS_EPS)
            xhat = xv * r
            dyg = dhv * g_ref[...]
            dx = dres_ref[...] + r * (dyg - xhat * jnp.mean(dyg * xhat, axis=-1, keepdims=True))
            dx_ref[...] = dx
            if with_bf16:
                outs[1][...] = dx.astype(BF16)
            gpart = jnp.sum(dhv * xhat, axis=0, keepdims=True)

            @pl.when(i == 0)
            def _():
                dg_ref[...] = gpart

            @pl.when(i > 0)
            def _():
                dg_ref[...] += gpart

    rows = lambda width_: pl.BlockSpec((tm, width_), lambda i, k: (i + i0, 0))
    if a_parts:
        a_spec = pl.BlockSpec((None, tm, tk), lambda i, k: (k, i + i0, 0))
    else:
        a_spec = pl.BlockSpec((tm, tk), lambda i, k: (i + i0, k))
    if b_stacked:
        b_spec = pl.BlockSpec((None, width, tk), lambda i, k: (k, 0, 0))
    else:
        b_spec = pl.BlockSpec((width, tk), lambda i, k: (0, k))
    operands = [a, b, x, dres, gain]
    in_specs = [a_spec, b_spec, rows(width), rows(width), pl.BlockSpec((1, width), lambda i, k: (0, 0))]
    for extra in (after, out_into):
        if extra is not None:
            operands.append(extra)
            in_specs.append(pl.BlockSpec(memory_space=pl.ANY))
    aliases = {} if out_into is None else {len(operands) - 1: 0}
    out_specs = [rows(width)] + ([rows(width)] if with_bf16 else []) + [pl.BlockSpec((1, width), lambda i, k: (0, 0))]
    out_shape = [jax.ShapeDtypeStruct((m, width), F32)] + ([jax.ShapeDtypeStruct((m, width), BF16)] if with_bf16 else [])
    out_shape.append(jax.ShapeDtypeStruct((1, width), F32))
    est = 4 * tm * tk + 4 * width * tk + 12 * tm * width * 4
    return pl.pallas_call(
        body, name=name, grid=(gm, gk), in_specs=in_specs, out_specs=out_specs, out_shape=out_shape,
        scratch_shapes=[pltpu.VMEM((tm, width), F32)], input_output_aliases=aliases,
        compiler_params=_params(("arbitrary", "arbitrary"), est),
    )(*operands)


def _gain_grad(x, dh, name):
    rows, width = x.shape

    def body(x_ref, dh_ref, dg_ref):
        xv = x_ref[...]
        r = lax.rsqrt(jnp.mean(xv * xv, axis=-1, keepdims=True) + RMS_EPS)
        dg_ref[...] = jnp.sum(dh_ref[...] * xv * r, axis=0, keepdims=True)

    return pl.pallas_call(
        body, name=name, grid=(1,), in_specs=[_row_spec(rows, width), _row_spec(rows, width)],
        out_specs=_bcast_spec(width), out_shape=jax.ShapeDtypeStruct((1, width), F32),
        compiler_params=_params(("arbitrary",), 6 * rows * width * 4),
    )(x, dh)


def _lb_table(logits, name):
    slots, width = logits.shape

    def body(l_ref, o_ref):
        lv = l_ref[...]
        mx = jnp.max(lv, axis=0, keepdims=True)
        e = jnp.exp(lv - mx)
        o_ref[...] = e[0:1, :] / jnp.sum(e, axis=0, keepdims=True)

    return pl.pallas_call(
        body, name=name, grid=(1,), in_specs=[pl.BlockSpec((slots, width), lambda i: (0, 0))],
        out_specs=_bcast_spec(width), out_shape=jax.ShapeDtypeStruct((1, width), F32),
    )(logits)


def _branch_merge_fwd(proj, outs, weights, tr):
    w = D_MODEL

    def body(ghg_ref, gda_ref, gmem_ref, ohg_ref, oda_ref, omem_ref, whg_ref, wda_ref, wmem_ref,
             m_ref, thg_ref, tda_ref, tmem_ref):
        acc = None
        for g_ref, o_ref, w_ref, t_ref in ((ghg_ref, ohg_ref, whg_ref, thg_ref), (gda_ref, oda_ref, wda_ref, tda_ref),
                                           (gmem_ref, omem_ref, wmem_ref, tmem_ref)):
            t = _dot(o_ref[...], w_ref[...], _NN)
            t_ref[...] = t.astype(BF16)
            term = _sigmoid(g_ref[...]) * t
            acc = term if acc is None else acc + term
        m_ref[...] = acc.astype(BF16)

    whole = lambda a: pl.BlockSpec(a.shape, lambda i: (0, 0))
    shape = jax.ShapeDtypeStruct((SEQ, w), BF16)
    return pl.pallas_call(
        body, name="branch_merge_fwd", grid=(SEQ // tr,),
        in_specs=[_row_spec(tr, w, 10), _row_spec(tr, w, 11), _row_spec(tr, w, 12)]
        + [_row_spec(tr, o.shape[1]) for o in outs] + [whole(wt) for wt in weights],
        out_specs=[_row_spec(tr, w)] * 4, out_shape=[shape] * 4,
        compiler_params=_params(("parallel",), 20 * tr * w * 4),
    )(proj, proj, proj, *outs, *weights)


def _dproj_buffer():
    return lax.empty((SEQ, IN_COLS), BF16)


def _gate_merge_bwd(proj, ts, weights, dx1b, wout, o_pre, hg_gain, o_da32, dproj, tr):
    w = D_MODEL
    steps = SEQ // tr
    gate_col0 = 10 * w
    hg_gate_col0 = CB_HG_G * LANE

    def body(ghg_ref, gda_ref, gmem_ref, hgg_ref, thg_ref, tda_ref, tmem_ref, whg_ref, wda_ref, wmem_ref, dx_ref,
             wout_ref, opre_ref, gain_ref, oda_ref, dproj_in,
             dthg_ref, dtda_ref, dtmem_ref, dopre_ref, doda_ref, domem_ref, dd_ref, dgain_ref, dproj_ref,
             stage, stage_hg, sems):
        del dproj_in
        i = pl.program_id(0)
        slot = i % 2
        rows = pl.ds(pl.multiple_of(i * tr, tr), tr)

        def slot_copies(s):
            return (pltpu.make_async_copy(stage.at[s], dproj_ref.at[rows, pl.ds(gate_col0, 3 * w)], sems.at[s]),
                    pltpu.make_async_copy(stage_hg.at[s], dproj_ref.at[rows, pl.ds(hg_gate_col0, w)], sems.at[2 + s]))

        @pl.when(i >= 2)
        def _():
            for cp in slot_copies(slot):
                cp.wait()

        dm = _dot(dx_ref[...], wout_ref[...], _NT)
        branches = ((ghg_ref, thg_ref, whg_ref, dthg_ref), (gda_ref, tda_ref, wda_ref, dtda_ref),
                    (gmem_ref, tmem_ref, wmem_ref, dtmem_ref))
        dos = []
        for b, (g_ref, t_ref, w_ref, dt_ref) in enumerate(branches):
            s = _sigmoid(g_ref[...])
            dt = (s * dm).astype(BF16)
            dt_ref[...] = dt
            dos.append(_dot(dt, w_ref[...], _NT))
            stage[slot, :, b * w:(b + 1) * w] = (dm * t_ref[...].astype(F32) * s * (1.0 - s)).astype(BF16)
        do_hg, do_da, do_mem = dos
        doda_ref[...] = do_da
        domem_ref[...] = do_mem

        gainv = gain_ref[...]
        part = jnp.zeros((1, HEAD_DIM), F32)
        for h in range(HG_HEADS):
            hs = slice(h * HEAD_DIM, (h + 1) * HEAD_DIM)
            o = opre_ref[:, hs]
            r = lax.rsqrt(jnp.mean(o * o, axis=-1, keepdims=True) + RMS_EPS)
            ohat = o * r
            g = hgg_ref[:, hs]
            sg = _sigmoid(g)
            silu = g * sg
            dout = do_hg[:, hs]
            stage_hg[slot, :, hs] = (dout * ohat * gainv * (sg + silu * (1.0 - sg))).astype(BF16)
            dy = dout * silu
            part = part + jnp.sum(dy * ohat, axis=0, keepdims=True)
            dn = dy * gainv
            dopre_ref[:, hs] = r * (dn - ohat * jnp.mean(dn * ohat, axis=-1, keepdims=True))
        for cp in slot_copies(slot):
            cp.start()

        prod = do_da * oda_ref[...]
        for h in range(DA_WIDTH // HEAD_DIM):
            hs = slice(h * HEAD_DIM, (h + 1) * HEAD_DIM)
            dd_ref[:, hs] = jnp.broadcast_to(jnp.sum(prod[:, hs], axis=-1, keepdims=True), (tr, HEAD_DIM))

        @pl.when(i == 0)
        def _():
            dgain_ref[...] = part

        @pl.when(i > 0)
        def _():
            dgain_ref[...] += part

        @pl.when(i == steps - 1)
        def _():
            for cp in slot_copies(1 - slot) + slot_copies(slot):
                cp.wait()

    assert steps >= 2
    whole = lambda a: pl.BlockSpec(a.shape, lambda i: (0, 0))
    widths = [wt.shape[0] for wt in weights]
    return pl.pallas_call(
        body, name="gate_merge_bwd", grid=(steps,),
        in_specs=[_row_spec(tr, w, 10), _row_spec(tr, w, 11), _row_spec(tr, w, 12), _row_spec(tr, w, hg_gate_col0 // w)]
        + [_row_spec(tr, w)] * 3 + [whole(wt) for wt in weights]
        + [_row_spec(tr, w), whole(wout), _row_spec(tr, w), _bcast_spec(HEAD_DIM), _row_spec(tr, DA_WIDTH), _any_spec()],
        out_specs=[_row_spec(tr, w)] * 4 + [_row_spec(tr, widths[1]), _row_spec(tr, widths[2]),
                                            _row_spec(tr, DA_WIDTH), _bcast_spec(HEAD_DIM), _any_spec()],
        out_shape=[jax.ShapeDtypeStruct((SEQ, w), BF16)] * 3 + [jax.ShapeDtypeStruct((SEQ, w), F32)]
        + [jax.ShapeDtypeStruct((SEQ, widths[1]), F32), jax.ShapeDtypeStruct((SEQ, widths[2]), F32),
           jax.ShapeDtypeStruct((SEQ, DA_WIDTH), F32), jax.ShapeDtypeStruct((1, HEAD_DIM), F32),
           jax.ShapeDtypeStruct((SEQ, IN_COLS), BF16)],
        scratch_shapes=[pltpu.VMEM((2, tr, 3 * w), BF16), pltpu.VMEM((2, tr, w), BF16), pltpu.SemaphoreType.DMA((4,))],
        input_output_aliases={15: 8},
        compiler_params=_params(("arbitrary",), 44 * tr * w * 4),
    )(proj, proj, proj, proj, *ts, *weights, dx1b, wout, o_pre, hg_gain, o_da32, dproj)


def _ffn_in_swiglu(h2, wfin, tm, tn):
    gm, gn = SEQ // tm, D_FF // tn

    def body(h_ref, wa_ref, wb_ref, a_ref, b_ref, act_ref, actt_ref):
        h = h_ref[...]
        a = _dot(h, wa_ref[...], _NN)
        b = _dot(h, wb_ref[...], _NN)
        act = a * _sigmoid(a) * b
        a_ref[...] = a.astype(BF16)
        b_ref[...] = b.astype(BF16)
        act_ref[...] = act.astype(BF16)
        actt_ref[...] = act.T.astype(BF16)

    tile = pl.BlockSpec((tm, tn), lambda j, i: (i, j))
    shape = jax.ShapeDtypeStruct((SEQ, D_FF), BF16)
    return pl.pallas_call(
        body, name="ffn_in_swiglu_fwd", grid=(gn, gm),
        in_specs=[pl.BlockSpec((tm, D_MODEL), lambda j, i: (i, 0)),
                  pl.BlockSpec((D_MODEL, tn), lambda j, i: (0, j)),
                  pl.BlockSpec((D_MODEL, tn), lambda j, i: (0, gn + j))],
        out_specs=[tile, tile, tile, pl.BlockSpec((tn, tm), lambda j, i: (j, i))],
        out_shape=[shape, shape, shape, jax.ShapeDtypeStruct((D_FF, SEQ), BF16)],
        compiler_params=_params(("parallel", "parallel"), 4 * tm * D_MODEL + 8 * D_MODEL * tn + 16 * tm * tn
                                + 6 * tm * tn * 4),
    )(h2, wfin, wfin)


def _ffn_out_bwd_swiglu(dyb, wfout, a, b, tm, tn):
    gm, gn = SEQ // tm, D_FF // tn

    def body(dy_ref, w_ref, a_ref, b_ref, o_ref):
        d = _dot(dy_ref[...], w_ref[...], _NT)
        av = a_ref[...].astype(F32)
        bv = b_ref[...].astype(F32)
        s = _sigmoid(av)
        silu = av * s
        o_ref[0] = (d * bv * (s + silu * (1.0 - s))).astype(BF16)
        o_ref[1] = (d * silu).astype(BF16)

    tile = pl.BlockSpec((tm, tn), lambda i, j: (i, j))
    return pl.pallas_call(
        body, name="ffn_out_bwd_swiglu", grid=(gm, gn),
        in_specs=[pl.BlockSpec((tm, D_MODEL), lambda i, j: (i, 0)), pl.BlockSpec((tn, D_MODEL), lambda i, j: (j, 0)),
                  tile, tile],
        out_specs=pl.BlockSpec((2, tm, tn), lambda i, j: (0, i, j)),
        out_shape=jax.ShapeDtypeStruct((2, SEQ, D_FF), BF16),
        compiler_params=_params(("parallel", "parallel"), 4 * tm * D_MODEL + 4 * tn * D_MODEL + 16 * tm * tn
                                + 6 * tm * tn * 4),
    )(dyb, wfout, a, b)


def _ffn_out_loss_head(x1, act, wfout, target, tr):
    w = D_MODEL

    def body(x_ref, a_ref, w_ref, t_ref, dy_ref, dyb_ref, loss_ref, acc_ref):
        err = x_ref[...] + _dot(a_ref[...], w_ref[...], _NN) - t_ref[...]
        dy = err * (1.0 / w)
        dy_ref[...] = dy
        dyb_ref[...] = dy.astype(BF16)
        part = jnp.sum(err * err, axis=0, keepdims=True)

        @pl.when(pl.program_id(0) == 0)
        def _():
            acc_ref[...] = part

        @pl.when(pl.program_id(0) > 0)
        def _():
            acc_ref[...] += part

        @pl.when(pl.program_id(0) == SEQ // tr - 1)
        def _():
            total = jnp.sum(acc_ref[...], axis=1, keepdims=True) * (0.5 / w)
            loss_ref[...] = jnp.broadcast_to(total, (1, LANE))

    return pl.pallas_call(
        body, name="ffn_out_loss_head", grid=(SEQ // tr,),
        in_specs=[_row_spec(tr, w), _row_spec(tr, D_FF), pl.BlockSpec((D_FF, w), lambda i: (0, 0)), _row_spec(tr, w)],
        out_specs=[_row_spec(tr, w), _row_spec(tr, w), _bcast_spec(LANE)],
        out_shape=[jax.ShapeDtypeStruct((SEQ, w), F32), jax.ShapeDtypeStruct((SEQ, w), BF16),
                   jax.ShapeDtypeStruct((1, LANE), F32)],
        scratch_shapes=[pltpu.VMEM((1, w), F32)],
        compiler_params=_params(("arbitrary",), 12 * tr * w * 4 + 4 * D_FF * w),
    )(x1, act, wfout, target)


GLA_ROWS = 256
GLA_CPB = GLA_ROWS // HG_CHUNK
GLA_NBLK = SEQ // GLA_ROWS
GLA_WAYS = 8
GLA_TRIPS = GLA_NBLK // GLA_WAYS
GLA_GRAD_WAYS = 8
GLA_GRAD_TRIPS = GLA_NBLK // GLA_GRAD_WAYS
GLA_NCK = SEQ // HG_CHUNK
GLA_INTER_WAYS = 32


def _dot_split(m, xv, dims, terms=3):
    dot = lambda t: lax.dot_general(m, t, dims, preferred_element_type=F32)
    hi = xv.astype(BF16)
    r1 = xv - hi.astype(F32)
    mid = r1.astype(BF16)
    if terms == 2:
        return dot(mid) + dot(hi)
    lo = (r1 - mid.astype(F32)).astype(BF16)
    return (dot(lo) + dot(mid)) + dot(hi)


def _gla_block(qc, fc, lbv, masks, b=None):
    mask, maskb, direction = masks
    sq = _sigmoid(qc)
    q = qc * sq * HG_SCALE
    sf = _sigmoid(fc)
    forget = lbv + (1.0 - lbv) * sf
    k = 1.0 - forget
    logf = jnp.log(forget)
    if b is None:
        b = _dot_split(maskb, logf, _NN)
    ends = []
    for j in range(GLA_CPB):
        lo, hi = j * HG_CHUNK, (j + 1) * HG_CHUNK
        end = jnp.where(direction == 0, b[hi - 1:hi, :], b[lo:lo + 1, :])
        ends.append(jnp.broadcast_to(end, (HG_CHUNK, HEAD_DIM)))
    bt = jnp.concatenate(ends, axis=0)
    eb = jnp.exp(b)
    qt = q * eb
    kt = k * jnp.exp(-b)
    kh = k * jnp.exp(bt - b)
    a = jnp.where(mask, _dot(qt, kt, _NT), 0.0)
    return dict(sq=sq, sf=sf, forget=forget, k=k, b=b, bt=bt, eb=eb, qt=qt, kt=kt, kh=kh, a=a)


def _gla_masks(direction):
    row = lax.broadcasted_iota(jnp.int32, (GLA_ROWS, GLA_ROWS), 0)
    col = lax.broadcasted_iota(jnp.int32, (GLA_ROWS, GLA_ROWS), 1)
    same = (row // HG_CHUNK) == (col // HG_CHUNK)
    mask = jnp.logical_and(same, jnp.where(direction == 0, row - col, col - row) >= 0)
    return mask, jnp.where(mask, 1.0, 0.0).astype(BF16), direction


def _gla_chunk_rows(j):
    return slice(j * HG_CHUNK, (j + 1) * HG_CHUNK)


def _gla_block_rows(b):
    return pl.ds(pl.multiple_of(b * GLA_ROWS, GLA_ROWS), GLA_ROWS)


def _head_spec(col_block0):
    return pl.BlockSpec((SEQ, HEAD_DIM), lambda h, d: (0, col_block0 + h))


def _gla_decay(proj, lb):
    def body(f_ref, lb_ref, b_ref):
        maskb = _gla_masks(pl.program_id(0))[1]
        lbv = lb_ref[...]

        def step(s, carry):
            rows = [_gla_block_rows(s + w * trips) for w in range(ways)]
            loaded = [f_ref[r, :] for r in rows]
            outs = [_dot_split(maskb, jnp.log(lbv + (1.0 - lbv) * _sigmoid(fc)), _NN) for fc in loaded]
            for r, o in zip(rows, outs):
                b_ref[r, :] = o
            return carry

        lax.fori_loop(0, trips, step, 0)

    ways, trips = GLA_WAYS, GLA_TRIPS
    heads = 2
    width = heads * HEAD_DIM
    return pl.pallas_call(
        body, name="gla_decay", grid=(2, HG_HEADS // heads),
        in_specs=[pl.BlockSpec((SEQ, width), lambda d, h: (0, CB_F // heads + (HG_HEADS // heads) * d + h)),
                  pl.BlockSpec((None, None, 1, width), lambda d, h: (d, h, 0, 0))],
        out_specs=pl.BlockSpec((None, SEQ, width), lambda d, h: (d, 0, h)),
        out_shape=jax.ShapeDtypeStruct((2, SEQ, D_MODEL), F32),
        compiler_params=_params(("parallel", "parallel"), 6 * SEQ * width * 4),
    )(proj, lb.reshape(2, HG_HEADS // heads, 1, width))


def _gla_fwd(proj, lb, gain, b_all):
    nck = GLA_NCK

    def body(q_ref, f_ref, v_ref, g_ref, lb_ref, gain_ref, b_ref, ohg_ref, opre_ref, st_ref, qt_scr, dec_scr, cs_scr):
        d = pl.program_id(1)
        masks = _gla_masks(d)
        lbv = lb_ref[...]

        @pl.when(d == 0)
        def _():
            opre_ref[...] = jnp.zeros_like(opre_ref)

        def intra(s, carry):
            blocks = [s + w * GLA_TRIPS for w in range(GLA_WAYS)]
            rows = [_gla_block_rows(b) for b in blocks]
            loaded = [(q_ref[r, :], f_ref[r, :], v_ref[r, :], opre_ref[r, :], b_ref[r, :]) for r in rows]
            results = []
            for qc, fc, v, o_prev, decay_log in loaded:
                ck = _gla_block(qc, fc, lbv, masks, b=decay_log)
                o = o_prev + _dot(ck["a"], v, _NN)
                cs = [_dot(v[_gla_chunk_rows(j), :], ck["kh"][_gla_chunk_rows(j), :], _TN) for j in range(GLA_CPB)]
                dec = [jnp.exp(ck["bt"][j * HG_CHUNK:j * HG_CHUNK + 8, :]) for j in range(GLA_CPB)]
                results.append((o, ck["qt"].astype(BF16), cs, dec))
            for b, r, (o, qt, cs, dec) in zip(blocks, rows, results):
                opre_ref[r, :] = o
                qt_scr[r, :] = qt
                for j in range(GLA_CPB):
                    cs_scr[b * GLA_CPB + j] = cs[j]
                    dec_scr[b * GLA_CPB + j] = dec[j]
            return carry

        lax.fori_loop(0, GLA_TRIPS, intra, 0)

        def scan(i, st):
            c = jnp.where(d == 0, i, nck - 1 - i)
            st_ref[c] = st
            return st * dec_scr[c][0:1, :] + cs_scr[c]

        lax.fori_loop(0, nck, scan, jnp.zeros((HEAD_DIM, HEAD_DIM), F32), unroll=4)

        def inter(t, carry):
            chunks = [t + w * (nck // GLA_INTER_WAYS) for w in range(GLA_INTER_WAYS)]
            rows = [pl.ds(pl.multiple_of(c * HG_CHUNK, HG_CHUNK), HG_CHUNK) for c in chunks]
            loaded = [(opre_ref[r, :], qt_scr[r, :], st_ref[c]) for r, c in zip(rows, chunks)]
            for r, o in zip(rows, [o_prev + _dot(qt, st, _NT) for o_prev, qt, st in loaded]):
                opre_ref[r, :] = o
            return carry

        lax.fori_loop(0, nck // GLA_INTER_WAYS, inter, 0)

        @pl.when(d == 1)
        def _():
            o = opre_ref[...]
            r = lax.rsqrt(jnp.mean(o * o, axis=-1, keepdims=True) + RMS_EPS)
            g = g_ref[...]
            ohg_ref[...] = (o * r * gain_ref[...] * (g * _sigmoid(g))).astype(BF16)

    blk = SEQ * HEAD_DIM * 4
    return pl.pallas_call(
        body, name="gla_fwd", grid=(HG_HEADS, 2),
        in_specs=[_head_spec(CB_HG_Q),
                  pl.BlockSpec((SEQ, HEAD_DIM), lambda h, d: (0, CB_F + 8 * d + h)),
                  _head_spec(CB_HG_I), _head_spec(CB_HG_G),
                  pl.BlockSpec((None, None, 1, HEAD_DIM), lambda h, d: (d, h, 0, 0)),
                  pl.BlockSpec((1, HEAD_DIM), lambda h, d: (0, 0)),
                  pl.BlockSpec((None, SEQ, HEAD_DIM), lambda h, d: (d, 0, h))],
        out_specs=[_head_spec(0), _head_spec(0),
                   pl.BlockSpec((None, None, nck, HEAD_DIM, HEAD_DIM), lambda h, d: (h, d, 0, 0, 0)),
                   pl.BlockSpec((None, SEQ, HEAD_DIM), lambda h, d: (d, 0, h)),
                   pl.BlockSpec((None, None, nck, 8, HEAD_DIM), lambda h, d: (h, d, 0, 0, 0))],
        out_shape=[jax.ShapeDtypeStruct((SEQ, D_MODEL), BF16), jax.ShapeDtypeStruct((SEQ, D_MODEL), F32),
                   jax.ShapeDtypeStruct((HG_HEADS, 2, nck, HEAD_DIM, HEAD_DIM), F32),
                   jax.ShapeDtypeStruct((2, SEQ, D_MODEL), BF16),
                   jax.ShapeDtypeStruct((HG_HEADS, 2, nck, 8, HEAD_DIM), F32)],
        scratch_shapes=[pltpu.VMEM((nck, HEAD_DIM, HEAD_DIM), F32)],
        compiler_params=_params(("parallel", "arbitrary"), 8 * blk + 3 * blk + 2 * blk + 2 * blk + 3 * blk),
    )(proj, proj, proj, proj, lb, gain, b_all)


def _gla_bwd(proj, lb, states, qt_all, dec_all, b_all, do, dproj):
    nck = GLA_NCK

    def body(q_ref, f_ref, v_ref, lb_ref, do_ref, st_ref, qt_ref, dec_scr, b_ref, dproj_in, dproj_ref, dlb_ref,
             dq_acc, dv_acc, dst_scr, cs_scr, df_out, dq_out, dv_out, sems):
        del dproj_in
        h, d = pl.program_id(0), pl.program_id(1)
        masks = _gla_masks(d)
        mask, maskb, _ = masks
        lbv = lb_ref[...]

        def column_copy(k, staging, col_block):
            cols = pl.ds(pl.multiple_of(col_block * LANE, LANE), LANE)
            return pltpu.make_async_copy(staging, dproj_ref.at[:, cols], sems.at[k])

        df_copy = column_copy(0, df_out, CB_F + 8 * d + h)
        dq_copy = column_copy(1, dq_out, CB_HG_Q + h)
        dv_copy = column_copy(2, dv_out, CB_HG_I + h)
        last = jnp.logical_and(h == HG_HEADS - 1, d == 1)

        def intra(s, carry):
            blocks = [s + w * GLA_TRIPS for w in range(GLA_WAYS)]
            loaded = [(qt_ref[r, :], do_ref[r, :]) for r in map(_gla_block_rows, blocks)]
            results = [[_dot(doc[_gla_chunk_rows(j), :], qt[_gla_chunk_rows(j), :], _TN) for j in range(GLA_CPB)]
                       for qt, doc in loaded]
            for b, cs in zip(blocks, results):
                for j in range(GLA_CPB):
                    cs_scr[b * GLA_CPB + j] = cs[j]
            return carry

        lax.fori_loop(0, GLA_TRIPS, intra, 0)

        def scan(i, dst):
            c = jnp.where(d == 0, nck - 1 - i, i)
            dst_scr[c] = dst
            return dst * dec_scr[c][0:1, :] + cs_scr[c]

        lax.fori_loop(0, nck, scan, jnp.zeros((HEAD_DIM, HEAD_DIM), F32), unroll=4)

        @pl.when(d == 0)
        def _():
            dq_acc[...] = jnp.zeros_like(dq_acc)
            dv_acc[...] = jnp.zeros_like(dv_acc)

        def block_grads(qc, fc, v, doc, states_in, dstates, decays, decay_log):
            ck = _gla_block(qc, fc, lbv, masks, b=decay_log)
            qt, kt, kh, a = ck["qt"], ck["kt"], ck["kh"], ck["a"]
            da = jnp.where(mask, _dot(doc, v, _NT), 0.0)
            dqt_i = _dot(da, kt, _NN)
            dkt = _dot(da, qt, _TN)
            dv_i = _dot(a, doc, _TN)
            dqt_p, dv_p, dkh_p, dbt_p = [], [], [], []
            for j in range(GLA_CPB):
                cr = _gla_chunk_rows(j)
                st_in, dst = states_in[j], dstates[j]
                dqt_p.append(dqt_i[cr, :] + _dot(doc[cr, :], st_in, _NN))
                dv_p.append(dv_i[cr, :] + _dot(kh[cr, :], dst, _NT))
                dkh_j = _dot(v[cr, :], dst, _NN)
                dkh_p.append(dkh_j)
                dbt_j = (decays[j][0:1, :] * jnp.sum(dst * st_in, axis=0, keepdims=True)
                         + jnp.sum(dkh_j * kh[cr, :], axis=0, keepdims=True))
                dbt_p.append(jnp.broadcast_to(dbt_j, (HG_CHUNK, HEAD_DIM)))
            dqt = jnp.concatenate(dqt_p, axis=0)
            dv = jnp.concatenate(dv_p, axis=0)
            dkh = jnp.concatenate(dkh_p, axis=0)
            dbt = jnp.concatenate(dbt_p, axis=0)
            db = dqt * qt - dkt * kt - dkh * kh
            dq = dqt * ck["eb"]
            dk = dkt * jnp.exp(-ck["b"]) + dkh * jnp.exp(ck["bt"] - ck["b"])
            dlogf = _dot_split(maskb, db, _TN, terms=2) + dbt
            dforget = dlogf / ck["forget"] - dk
            sf, sq = ck["sf"], ck["sq"]
            df = (dforget * (1.0 - lbv) * sf * (1.0 - sf)).astype(BF16)
            dqc = dq * HG_SCALE * (sq + qc * sq * (1.0 - sq))
            return df, dqc, dv, jnp.sum(dforget * (1.0 - sf), axis=0, keepdims=True)

        def grads(s, dlb):
            blocks = [s + w * GLA_GRAD_TRIPS for w in range(GLA_GRAD_WAYS)]
            rows = [_gla_block_rows(b) for b in blocks]
            loaded = []
            for b, r in zip(blocks, rows):
                chunks = [b * GLA_CPB + j for j in range(GLA_CPB)]
                loaded.append((q_ref[r, :], f_ref[r, :], v_ref[r, :], do_ref[r, :], [st_ref[c] for c in chunks],
                               [dst_scr[c] for c in chunks], [dec_scr[c] for c in chunks], b_ref[r, :],
                               dq_acc[r, :], dv_acc[r, :]))
            results = [block_grads(*t[:8]) + (t[8], t[9]) for t in loaded]
            for r, (df, dqc, dv, dlb_part, dq_prev, dv_prev) in zip(rows, results):
                df_out[r, :] = df
                dq_acc[r, :] = dq_prev + dqc
                dv_acc[r, :] = dv_prev + dv
                dlb = dlb + dlb_part
            return dlb

        @pl.when(jnp.logical_or(h > 0, d > 0))
        def _():
            df_copy.wait()

        dlb_ref[...] = lax.fori_loop(0, GLA_GRAD_TRIPS, grads, jnp.zeros((1, HEAD_DIM), F32))
        df_copy.start()

        @pl.when(d == 1)
        def _():
            @pl.when(h > 0)
            def _():
                dq_copy.wait()
                dv_copy.wait()

            dq_out[...] = dq_acc[...].astype(BF16)
            dv_out[...] = dv_acc[...].astype(BF16)
            dq_copy.start()
            dv_copy.start()

        @pl.when(last)
        def _():
            df_copy.wait()
            dq_copy.wait()
            dv_copy.wait()

    blk = SEQ * HEAD_DIM * 4
    state_bytes = nck * HEAD_DIM * HEAD_DIM * 4
    dproj, dlb = pl.pallas_call(
        body, name="gla_bwd", grid=(HG_HEADS, 2),
        in_specs=[_head_spec(CB_HG_Q),
                  pl.BlockSpec((SEQ, HEAD_DIM), lambda h, d: (0, CB_F + 8 * d + h)),
                  _head_spec(CB_HG_I),
                  pl.BlockSpec((None, None, 1, HEAD_DIM), lambda h, d: (d, h, 0, 0)),
                  _head_spec(0),
                  pl.BlockSpec((None, None, nck, HEAD_DIM, HEAD_DIM), lambda h, d: (h, d, 0, 0, 0)),
                  pl.BlockSpec((None, SEQ, HEAD_DIM), lambda h, d: (d, 0, h)),
                  pl.BlockSpec((None, None, nck, 8, HEAD_DIM), lambda h, d: (h, d, 0, 0, 0)),
                  pl.BlockSpec((None, SEQ, HEAD_DIM), lambda h, d: (d, 0, h)),
                  _any_spec()],
        out_specs=[_any_spec(), pl.BlockSpec((None, None, 1, HEAD_DIM), lambda h, d: (d, h, 0, 0))],
        out_shape=[jax.ShapeDtypeStruct((SEQ, IN_COLS), BF16), jax.ShapeDtypeStruct((2, HG_HEADS, 1, HEAD_DIM), F32)],
        scratch_shapes=[pltpu.VMEM((SEQ, HEAD_DIM), F32)] * 2
        + [pltpu.VMEM((nck, HEAD_DIM, HEAD_DIM), F32)] * 2
        + [pltpu.VMEM((SEQ, HEAD_DIM), BF16)] * 3 + [pltpu.SemaphoreType.DMA((3,))],
        input_output_aliases={9: 0},
        compiler_params=_params(("arbitrary", "arbitrary"), 10 * blk + 4 * state_bytes + 3 * blk + 3 * blk),
    )(proj, proj, proj, lb, do, states, qt_all, dec_all, b_all, dproj)
    return dproj, dlb


def _da_residue_rows(r, n0, size, d):
    if d == 1:
        return pl.ds(pl.multiple_of(n0, 8), size)
    return pl.ds(r + n0 * d, size, stride=d)


def _da_residue_ways(d, nqb):
    return min(d, 4) if nqb <= 2 else 1


def _da_rmsnorm(x, gain):
    r = lax.rsqrt(jnp.mean(x * x, axis=-1, keepdims=True) + RMS_EPS)
    return x * r, r, x * r * gain


def _da_scores(qn_scr, kn_scr, slope, i, ld):
    w0 = jnp.clip(i * DA_QB - DA_RADIUS, 0, ld - DA_WIN)
    w0 = pl.multiple_of(w0, DA_RADIUS)
    qrows = pl.ds(pl.multiple_of(i * DA_QB, DA_QB), DA_QB)
    win = pl.ds(w0, DA_WIN)
    qb = qn_scr[qrows, :]
    kw = kn_scr[win, :]
    s = _dot(qb, kw, _NT) * DA_SCALE
    qpos = i * DA_QB + lax.broadcasted_iota(jnp.int32, (DA_QB, DA_WIN), 0)
    kpos = w0 + lax.broadcasted_iota(jnp.int32, (DA_QB, DA_WIN), 1)
    arel = jnp.abs(kpos - qpos)
    s = s - slope * arel.astype(F32)
    s = jnp.where(arel <= DA_RADIUS, s, NEG_INF)
    return s, qb, kw, qrows, win


def _da_slopes(group):
    d = DA_DILATIONS[group]
    sl = _alibi_slopes(DA_HEADS)[4 * group:4 * group + 4] * d
    return jnp.asarray(np.broadcast_to(sl[:, None, None], (4, 1, LANE)).copy())


def _da_fwd(proj, gq, gk, group, others=()):
    d = DA_DILATIONS[group]
    ld = SEQ // d
    nqb = ld // DA_QB
    n_other = 2 * len(others)

    ways = min(DA_FWD_WAYS, nqb)
    rways = _da_residue_ways(d, nqb)
    piece = min(ld, DA_NORM_ROWS)

    def body(q_ref, k_ref, v_ref, gq_ref, gk_ref, sl_ref, *refs):
        other_refs, refs = refs[:n_other], refs[n_other:]
        if others:
            ob_ref, of_ref, lsej_ref, qn_scr, kn_scr, v_scr, o_ref, lse_ref = refs
        else:
            o_ref, lse_ref, qn_scr, kn_scr, v_scr = refs
        slope = sl_ref[:, 0:1]

        def residues(t, carry):
            rs = [t * rways + u for u in range(rways)]

            def normalise(p, c2):
                dst = pl.ds(pl.multiple_of(p * piece, piece), piece)
                for u, r in enumerate(rs):
                    src = _da_residue_rows(r, p * piece, piece, d)
                    qn_scr[u, dst, :] = _da_rmsnorm(q_ref[src, :], gq_ref[...])[2].astype(BF16)
                    kn_scr[u, dst, :] = _da_rmsnorm(k_ref[src, :], gk_ref[...])[2].astype(BF16)
                    v_scr[u, dst, :] = v_ref[src, :].astype(BF16)
                return c2

            lax.fori_loop(0, ld // piece, normalise, 0)

            def block(u, i):
                s, _, _, _, win = _da_scores(qn_scr.at[u], kn_scr.at[u], slope, i, ld)
                m = jnp.max(s, axis=-1, keepdims=True)
                p = jnp.exp(s - m)
                l = jnp.sum(p, axis=-1, keepdims=True)
                return _dot(p, v_scr[u, win, :], _NN) / l, jnp.broadcast_to(m + jnp.log(l), (DA_QB, HEAD_DIM))

            def step(i, c2):
                todo = [(u, r, i + w * (nqb // ways)) for u, r in enumerate(rs) for w in range(ways)]
                for (u, r, b), (o, lse) in zip(todo, [block(u, b) for u, _, b in todo]):
                    out = _da_residue_rows(r, b * DA_QB, DA_QB, d)
                    o_ref[out, :] = o
                    lse_ref[out, :] = lse
                return c2

            return lax.fori_loop(0, nqb // ways, step, carry)

        lax.fori_loop(0, d // rways, residues, 0)

        if others:
            pieces = 8
            rows = SEQ // pieces

            def merge(t, carry):
                r = pl.ds(pl.multiple_of(t * rows, rows), rows)
                outs = [ref[r, :] for ref in other_refs[0::2]] + [o_ref[r, :]]
                lses = [ref[r, :] for ref in other_refs[1::2]] + [lse_ref[r, :]]
                m = lses[0]
                for l in lses[1:]:
                    m = jnp.maximum(m, l)
                es = [jnp.exp(l - m) for l in lses]
                tot = sum(es[1:], es[0])
                o = sum((e * v for e, v in zip(es[1:], outs[1:])), es[0] * outs[0]) / tot
                of_ref[r, :] = o
                ob_ref[r, :] = o.astype(BF16)
                lsej_ref[r, :] = m + jnp.log(tot)
                return carry

            lax.fori_loop(0, pieces, merge, 0)

    seq_spec = lambda base: pl.BlockSpec((SEQ, HEAD_DIM), lambda h: (0, base + 4 * group + h))
    out_spec = pl.BlockSpec((SEQ, HEAD_DIM), lambda h: (0, h))
    gain_spec = pl.BlockSpec((1, HEAD_DIM), lambda h: (0, 0))
    blk = SEQ * HEAD_DIM * 4
    fshape = jax.ShapeDtypeStruct((SEQ, DA_WIDTH), F32)
    scratch = [pltpu.VMEM((rways, ld, HEAD_DIM), BF16)] * 3
    if others:
        out_shape = [jax.ShapeDtypeStruct((SEQ, DA_WIDTH), BF16), fshape, fshape]
        scratch += [pltpu.VMEM((SEQ, HEAD_DIM), F32)] * 2
    else:
        out_shape = [fshape, fshape]
    return pl.pallas_call(
        body, name=f"da_fwd_g{group}", grid=(DA_HEADS_PER_GROUP,),
        in_specs=[seq_spec(CB_DA_Q), seq_spec(CB_DA_K), seq_spec(CB_DA_V), gain_spec, gain_spec,
                  pl.BlockSpec((None, 1, LANE), lambda h: (h, 0, 0))] + [out_spec] * n_other,
        out_specs=[out_spec] * len(out_shape), out_shape=out_shape, scratch_shapes=scratch,
        compiler_params=_params(("parallel",), 10 * blk + 2 * blk + (n_other + 3) * blk),
    )(proj, proj, proj, gq, gk, _da_slopes(group), *[a for pair in others for a in pair])


def _da_bwd(proj, gq, gk, do, lse, dd, dproj, group):
    d = DA_DILATIONS[group]
    ld = SEQ // d
    nqb = ld // DA_QB
    ways = min(DA_WAYS, nqb)
    rways = _da_residue_ways(d, nqb)
    piece = min(ld, DA_NORM_ROWS)

    def body(q_ref, k_ref, v_ref, gq_ref, gk_ref, sl_ref, do_ref, lse_ref, dd_ref, dproj_in,
             dproj_ref, dgq_ref, dgk_ref,
             qn_scr, kn_scr, v_scr, dqn_scr, dkn_scr, dvr_scr, rq_scr, rk_scr, dq_scr, dk_scr, dv_scr,
             dq_out, dk_out, dv_out, sems):
        del dproj_in
        head = pl.program_id(0)
        gqv, gkv = gq_ref[...], gk_ref[...]
        slope = sl_ref[:, 0:1]

        copies = []
        for k, (staging, base) in enumerate(((dq_out, CB_DA_Q), (dk_out, CB_DA_K), (dv_out, CB_DA_V))):
            cols = pl.ds(pl.multiple_of((base + 4 * group + head) * LANE, LANE), LANE)
            copies.append(pltpu.make_async_copy(staging, dproj_ref.at[:, cols], sems.at[k]))

        def residues(t, carry):
            rs = [t * rways + u for u in range(rways)]

            def pieces_of(p):
                return [(u, _da_residue_rows(r, p * piece, piece, d), pl.ds(pl.multiple_of(p * piece, piece), piece))
                        for u, r in enumerate(rs)]

            def normalise(p, c2):
                for u, src, dst in pieces_of(p):
                    for x_ref, gv, n_scr, r_scr in ((q_ref, gqv, qn_scr, rq_scr), (k_ref, gkv, kn_scr, rk_scr)):
                        _, rstd, normed = _da_rmsnorm(x_ref[src, :], gv)
                        n_scr[u, dst, :] = normed.astype(BF16)
                        r_scr[u, dst, :] = jnp.broadcast_to(rstd, (piece, HEAD_DIM))
                    v_scr[u, dst, :] = v_ref[src, :].astype(BF16)
                return c2

            lax.fori_loop(0, ld // piece, normalise, 0)
            dkn_scr[...] = jnp.zeros_like(dkn_scr)
            dvr_scr[...] = jnp.zeros_like(dvr_scr)

            def block(u, r, i):
                s, qb, kw, qrows, win = _da_scores(qn_scr.at[u], kn_scr.at[u], slope, i, ld)
                src = _da_residue_rows(r, i * DA_QB, DA_QB, d)
                p = jnp.exp(s - lse_ref[src, :][:, 0:1])
                dob = do_ref[src, :]
                dp = _dot(dob, v_scr[u, win, :], _NT)
                ds = p * (dp - dd_ref[src, :][:, 0:1]) * DA_SCALE
                return u, qrows, win, _dot(p, dob, _TN), _dot(ds, kw, _NN), _dot(ds, qb, _TN)

            def step(i, c2):
                todo = [(u, r, i + w * (nqb // ways)) for u, r in enumerate(rs) for w in range(ways)]
                for u, qrows, win, dv, dqn, dkn in [block(*t) for t in todo]:
                    dvr_scr[u, win, :] += dv
                    dqn_scr[u, qrows, :] = dqn
                    dkn_scr[u, win, :] += dkn
                return c2

            lax.fori_loop(0, nqb // ways, step, 0)

            def norm_backward(p, gains):
                pq, pk = gains
                for u, src, dst in pieces_of(p):
                    parts = []
                    for x_ref, gv, dn_scr, dx_scr, r_scr in ((q_ref, gqv, dqn_scr, dq_scr, rq_scr),
                                                             (k_ref, gkv, dkn_scr, dk_scr, rk_scr)):
                        rstd = r_scr[u, dst, :]
                        hat = x_ref[src, :] * rstd
                        dn = dn_scr[u, dst, :]
                        dyg = dn * gv
                        dx_scr[src, :] = rstd * (dyg - hat * jnp.mean(dyg * hat, axis=-1, keepdims=True))
                        parts.append(jnp.sum(dn * hat, axis=0, keepdims=True))
                    dv_scr[src, :] = dvr_scr[u, dst, :]
                    pq, pk = pq + parts[0], pk + parts[1]
                return pq, pk

            return lax.fori_loop(0, ld // piece, norm_backward, carry)

        zero = jnp.zeros((1, HEAD_DIM), F32)
        pq, pk = lax.fori_loop(0, d // rways, residues, (zero, zero))

        @pl.when(pl.program_id(0) == 0)
        def _():
            dgq_ref[...] = pq
            dgk_ref[...] = pk

        @pl.when(pl.program_id(0) > 0)
        def _():
            dgq_ref[...] += pq
            dgk_ref[...] += pk

        @pl.when(head > 0)
        def _():
            for cp in copies:
                cp.wait()

        dq_out[...] = dq_scr[...].astype(BF16)
        dk_out[...] = dk_scr[...].astype(BF16)
        dv_out[...] = dv_scr[...].astype(BF16)
        for cp in copies:
            cp.start()

        @pl.when(head == DA_HEADS_PER_GROUP - 1)
        def _():
            for cp in copies:
                cp.wait()

    seq_spec = lambda base: pl.BlockSpec((SEQ, HEAD_DIM), lambda h: (0, base + 4 * group + h))
    out_spec = pl.BlockSpec((SEQ, HEAD_DIM), lambda h: (0, h))
    gain_spec = pl.BlockSpec((1, HEAD_DIM), lambda h: (0, 0))
    gshape = jax.ShapeDtypeStruct((1, HEAD_DIM), F32)
    blk = SEQ * HEAD_DIM * 4
    return pl.pallas_call(
        body, name=f"da_bwd_g{group}", grid=(DA_HEADS_PER_GROUP,),
        in_specs=[seq_spec(CB_DA_Q), seq_spec(CB_DA_K), seq_spec(CB_DA_V), gain_spec, gain_spec,
                  pl.BlockSpec((None, 1, LANE), lambda h: (h, 0, 0)), out_spec, out_spec, out_spec, _any_spec()],
        out_specs=[_any_spec(), gain_spec, gain_spec],
        out_shape=[jax.ShapeDtypeStruct((SEQ, IN_COLS), BF16), gshape, gshape],
        scratch_shapes=[pltpu.VMEM((rways, ld, HEAD_DIM), BF16)] * 3 + [pltpu.VMEM((rways, ld, HEAD_DIM), F32)] * 5
        + [pltpu.VMEM((SEQ, HEAD_DIM), F32)] * 3 + [pltpu.VMEM((SEQ, HEAD_DIM), BF16)] * 3
        + [pltpu.SemaphoreType.DMA((3,))],
        input_output_aliases={9: 0},
        compiler_params=_params(("arbitrary",), 12 * blk + 3 * blk + 3 * blk + 3 * blk + 2 * blk),
    )(proj, proj, proj, gq, gk, _da_slopes(group), do, lse, dd, dproj)


MEM_WAYS = 4


def _mem_probs(q, kn, gq):
    qhat, rq, qn = _da_rmsnorm(q, gq)
    s = _dot(qn, kn, _NT) * MEM_SCALE
    m = jnp.max(s, axis=-1, keepdims=True)
    e = jnp.exp(s - m)
    p = e / jnp.sum(e, axis=-1, keepdims=True)
    return p, qhat, rq, qn


def _mem_pieces(tq):
    rows = tq // MEM_WAYS
    return [slice(w * rows, (w + 1) * rows) for w in range(MEM_WAYS)]


def _mem_fwd(proj, kv, gq, gk, tq):
    def body(q_ref, k_ref, v_ref, gq_ref, gk_ref, o_ref):
        kn = _da_rmsnorm(k_ref[...], gk_ref[...])[2]
        v, gqv = v_ref[...], gq_ref[...]
        pieces = _mem_pieces(tq)
        outs = [_dot(_mem_probs(q_ref[rs, :], kn, gqv)[0], v, _NN) for rs in pieces]
        for rs, o in zip(pieces, outs):
            o_ref[rs, :] = o.astype(BF16)

    gain_spec = pl.BlockSpec((1, HEAD_DIM), lambda h, i: (0, 0))
    return pl.pallas_call(
        body, name="mem_fwd", grid=(MEM_HEADS, SEQ // tq),
        in_specs=[pl.BlockSpec((tq, HEAD_DIM), lambda h, i: (i, CB_MEM_Q + h)),
                  pl.BlockSpec((N_MEM, HEAD_DIM), lambda h, i: (0, h)),
                  pl.BlockSpec((N_MEM, HEAD_DIM), lambda h, i: (0, MEM_HEADS + h)), gain_spec, gain_spec],
        out_specs=pl.BlockSpec((tq, HEAD_DIM), lambda h, i: (i, h)),
        out_shape=jax.ShapeDtypeStruct((SEQ, MEM_WIDTH), BF16),
        compiler_params=_params(("parallel", "parallel"), 16 * tq * N_MEM * 4),
    )(proj, kv, kv, gq, gk)


def _mem_bwd(proj, kv, gq, gk, do, dproj, tq):
    nq = SEQ // tq

    def body(q_ref, k_ref, v_ref, gq_ref, gk_ref, do_ref, dproj_in, dq_ref, dk_ref, dv_ref, dgq_ref, dgk_ref, dkn_scr):
        del dproj_in
        h, i = pl.program_id(0), pl.program_id(1)
        gqv, gkv = gq_ref[...], gk_ref[...]
        khat, rk, kn = _da_rmsnorm(k_ref[...], gkv)
        v = v_ref[...]

        def piece(rs):
            p, qhat, rq, qn = _mem_probs(q_ref[rs, :], kn, gqv)
            dob = do_ref[rs, :]
            dp = _dot(dob, v, _NT)
            ds = p * (dp - jnp.sum(p * dp, axis=-1, keepdims=True)) * MEM_SCALE
            dqn = _dot(ds, kn, _NN)
            dyg = dqn * gqv
            dq = (rq * (dyg - qhat * jnp.mean(dyg * qhat, axis=-1, keepdims=True))).astype(BF16)
            return dq, _dot(p, dob, _TN), _dot(ds, qn, _TN), jnp.sum(dqn * qhat, axis=0, keepdims=True)

        pieces = _mem_pieces(tq)
        results = [piece(rs) for rs in pieces]
        for rs, res in zip(pieces, results):
            dq_ref[rs, :] = res[0]
        dvp = sum((res[1] for res in results[1:]), results[0][1])
        dknp = sum((res[2] for res in results[1:]), results[0][2])
        dgq_part = sum((res[3] for res in results[1:]), results[0][3])
        first = jnp.logical_and(h == 0, i == 0)

        @pl.when(first)
        def _():
            dgq_ref[...] = dgq_part

        @pl.when(jnp.logical_not(first))
        def _():
            dgq_ref[...] += dgq_part

        @pl.when(i == 0)
        def _():
            dv_ref[...] = dvp
            dkn_scr[...] = dknp

        @pl.when(i > 0)
        def _():
            dv_ref[...] += dvp
            dkn_scr[...] += dknp

        @pl.when(i == nq - 1)
        def _():
            dkn = dkn_scr[...]
            dkg = dkn * gkv
            dk_ref[...] = rk * (dkg - khat * jnp.mean(dkg * khat, axis=-1, keepdims=True))
            dgk_part = jnp.sum(dkn * khat, axis=0, keepdims=True)

            @pl.when(h == 0)
            def _():
                dgk_ref[...] = dgk_part

            @pl.when(h > 0)
            def _():
                dgk_ref[...] += dgk_part

    gain_spec = pl.BlockSpec((1, HEAD_DIM), lambda h, i: (0, 0))
    kvout = pl.BlockSpec((N_MEM, HEAD_DIM), lambda h, i: (0, h))
    return pl.pallas_call(
        body, name="mem_bwd", grid=(MEM_HEADS, nq),
        in_specs=[pl.BlockSpec((tq, HEAD_DIM), lambda h, i: (i, CB_MEM_Q + h)),
                  pl.BlockSpec((N_MEM, HEAD_DIM), lambda h, i: (0, h)),
                  pl.BlockSpec((N_MEM, HEAD_DIM), lambda h, i: (0, MEM_HEADS + h)), gain_spec, gain_spec,
                  pl.BlockSpec((tq, HEAD_DIM), lambda h, i: (i, h)), _any_spec()],
        out_specs=[pl.BlockSpec((tq, HEAD_DIM), lambda h, i: (i, CB_MEM_Q + h)), kvout, kvout, gain_spec, gain_spec],
        out_shape=[jax.ShapeDtypeStruct((SEQ, IN_COLS), BF16), jax.ShapeDtypeStruct((N_MEM, MEM_WIDTH), F32),
                   jax.ShapeDtypeStruct((N_MEM, MEM_WIDTH), F32), jax.ShapeDtypeStruct((1, HEAD_DIM), F32),
                   jax.ShapeDtypeStruct((1, HEAD_DIM), F32)],
        scratch_shapes=[pltpu.VMEM((N_MEM, HEAD_DIM), F32)], input_output_aliases={6: 0},
        compiler_params=_params(("arbitrary", "arbitrary"), 24 * tq * N_MEM * 4),
    )(proj, kv, kv, gq, gk, do, dproj)


def _mesh_position():
    return lax.axis_index("x"), lax.axis_index("y"), lax.axis_index("c")


def _any_spec():
    return pl.BlockSpec(memory_space=pl.ANY)


def _all_gather(shards):
    n = len(shards)

    def body(*refs):
        ins, outs = refs[:n], refs[n:2 * n]
        send_sems, recv_sems, local_sems = refs[2 * n:]
        x, y, c = _mesh_position()
        me, sibling = (x, y, c), (x, y, 1 - c)
        first = (jnp.where(c == 0, 1 - x, x), jnp.where(c == 0, y, 1 - y), c)
        second = (jnp.where(c == 0, x, 1 - x), jnp.where(c == 0, 1 - y, y), c)
        diagonal = (1 - x, 1 - y, c)

        def copy(w, k, block, to, src=None):
            px, py, pc = block
            rows = outs[w].at[4 * px + 2 * py + pc]
            return pltpu.make_async_remote_copy(
                src_ref=rows if src is None else src, dst_ref=rows,
                send_sem=send_sems.at[7 * w + k], recv_sem=recv_sems.at[7 * w + k],
                device_id=to, device_id_type=MESH)

        started = []
        for w in range(n):
            mine = pltpu.make_async_copy(ins[w], outs[w].at[4 * x + 2 * y + c], local_sems.at[w])
            mine.start()
            started.append(mine)
        sends = []
        for w in range(n):
            own = [copy(w, 0, me, sibling, src=ins[w]), copy(w, 1, me, first, src=ins[w]),
                   copy(w, 2, me, second, src=ins[w])]
            for cp in own:
                cp.start()
            sends += own
        for w in range(n):
            copy(w, 1, first, me).wait_recv()
            follow = [copy(w, 3, first, second), copy(w, 4, first, sibling)]
            for cp in follow:
                cp.start()
            copy(w, 2, second, me).wait_recv()
            follow.append(copy(w, 5, second, sibling))
            follow[-1].start()
            sends += follow
        for w in range(n):
            copy(w, 3, diagonal, me).wait_recv()
            passed = copy(w, 6, diagonal, sibling)
            passed.start()
            sends.append(passed)
        for w in range(n):
            for k in (0, 4, 5, 6):
                copy(w, k, sibling, me).wait_recv()
        for cp in sends:
            cp.wait_send()
        for mine in started:
            mine.wait()

    return pl.pallas_call(
        body, name="weights_all_gather",
        in_specs=[_any_spec()] * n, out_specs=[_any_spec()] * n,
        out_shape=[jax.ShapeDtypeStruct((N_DEV,) + s.shape, s.dtype) for s in shards],
        scratch_shapes=[pltpu.SemaphoreType.DMA((7 * n,)), pltpu.SemaphoreType.DMA((7 * n,)),
                        pltpu.SemaphoreType.DMA((n,))],
    )(*shards)


def _chip_of(j, x, y):
    return (1 - x if j & 1 else x, 1 - y if j & 2 else y)


_HBM_SPEC = pl.BlockSpec(memory_space=pltpu.HBM)
_SEM_SPEC = pl.BlockSpec(memory_space=pltpu.SEMAPHORE)
_DATAFLOW_EFFECT = pltpu.SideEffectType.DATAFLOW_SIDE_EFFECTING
TOKEN_SHAPE = (8, D_MODEL)


def _copies_start(name, arrays, n_copies, plan):
    n = len(arrays)

    def body(*refs):
        send_sems, recv_sems, token = refs[n], refs[n + 1], refs[2 * n + 2]
        copies = plan(refs[:n])
        assert len(copies) == n_copies
        for k, (src, dst, dev) in enumerate(copies):
            pltpu.make_async_remote_copy(src_ref=src, dst_ref=dst, send_sem=send_sems.at[k], recv_sem=recv_sems.at[k],
                                         device_id=dev, device_id_type=MESH).start()
        token[...] = jnp.zeros_like(token)

    outs = pl.pallas_call(
        body, name=name,
        out_shape=(pltpu.SemaphoreType.DMA((n_copies,)), pltpu.SemaphoreType.DMA((n_copies,)),
                   *[pltpu.HBM(a.shape, a.dtype) for a in arrays], jax.ShapeDtypeStruct(TOKEN_SHAPE, F32)),
        in_specs=[_HBM_SPEC] * n,
        out_specs=(_SEM_SPEC, _SEM_SPEC, *[_HBM_SPEC] * n, pl.BlockSpec(memory_space=pltpu.VMEM)),
        input_output_aliases={i: i + 2 for i in range(n)},
        compiler_params=pltpu.CompilerParams(has_side_effects=_DATAFLOW_EFFECT),
    )(*[pltpu.with_memory_space_constraint(a, pltpu.HBM) for a in arrays])
    return outs[0], outs[1], list(outs[2:2 + n]), outs[2 + n]


def _copies_wait(name, send_sems, recv_sems, arrays, n_copies, plan, after):
    n = len(arrays)
    after = list(after) if isinstance(after, (list, tuple)) else [after]

    def body(*refs):
        send_ref, recv_ref = refs[n], refs[n + 1]
        copies = plan(refs[:n])
        assert len(copies) == n_copies
        for k, (src, dst, dev) in enumerate(copies):
            cp = pltpu.make_async_remote_copy(src_ref=src, dst_ref=dst, send_sem=send_ref.at[k], recv_sem=recv_ref.at[k],
                                              device_id=dev, device_id_type=MESH)
            cp.wait_send()
            cp.wait_recv()

    outs = pl.pallas_call(
        body, name=name, out_shape=tuple(pltpu.HBM(a.shape, a.dtype) for a in arrays),
        in_specs=[_HBM_SPEC] * n + [_SEM_SPEC, _SEM_SPEC] + [pl.BlockSpec(memory_space=pl.ANY)] * len(after),
        out_specs=tuple([_HBM_SPEC] * n), input_output_aliases={i: i for i in range(n)},
        compiler_params=pltpu.CompilerParams(has_side_effects=_DATAFLOW_EFFECT),
    )(*arrays, send_sems, recv_sems, *after)
    return list(outs)


def _after(small, token):
    return small if token is None else small + token[0:1, :small.shape[-1]]


def _gather_plan_out(n):
    def plan(refs):
        x, y, c = _mesh_position()
        me = 4 * x + 2 * y + c
        copies = []
        for w in range(n):
            land = refs[n + w].at[me]
            copies.append((refs[w], land, (x, y, 1 - c)))
            for j in range(1, 4):
                copies.append((refs[w], land, (*_chip_of(j, x, y), c)))
        return copies
    return plan


def _gather_plan_pass(n):
    def plan(refs):
        x, y, c = _mesh_position()
        copies = []
        for w in range(n):
            for j in range(1, 4):
                px, py = _chip_of(j, x, y)
                rows = refs[w].at[4 * px + 2 * py + c]
                copies.append((rows, rows, (x, y, 1 - c)))
        return copies
    return plan


def _reduce_plan_sibling(n):
    def plan(refs):
        x, y, c = _mesh_position()
        copies = []
        for w in range(n):
            for j in range(4):
                px, py = _chip_of(j, x, y)
                copies.append((refs[w].at[4 * px + 2 * py + (1 - c)], refs[n + w].at[j], (x, y, 1 - c)))
        return copies
    return plan


def _reduce_plan_chips(n):
    def plan(refs):
        x, y, c = _mesh_position()
        copies = []
        for w in range(n):
            for j in range(1, 4):
                copies.append((refs[w].at[j - 1], refs[n + w].at[j - 1], (*_chip_of(j, x, y), c)))
        return copies
    return plan


def _chip_partials(grad, recv, name, tr):
    _, rows, width = grad.shape

    def body(g_ref, r_ref, own_ref, other_ref):
        x, y, c = _mesh_position()
        for j in range(4):
            px, py = _chip_of(j, x, y)
            total = g_ref[4 * px + 2 * py + c].astype(F32) + r_ref[j].astype(F32)
            if j == 0:
                own_ref[...] = total
            else:
                other_ref[j - 1] = total.astype(BF16)

    return pl.pallas_call(
        body, name=name, grid=(rows // tr,),
        in_specs=[pl.BlockSpec((N_DEV, tr, width), lambda i: (0, i, 0)),
                  pl.BlockSpec((4, tr, width), lambda i: (0, i, 0))],
        out_specs=[pl.BlockSpec((tr, width), lambda i: (i, 0)), pl.BlockSpec((3, tr, width), lambda i: (0, i, 0))],
        out_shape=[jax.ShapeDtypeStruct((rows, width), F32), jax.ShapeDtypeStruct((3, rows, width), BF16)],
        compiler_params=_params(("parallel",), 2 * 20 * tr * width * 2 + 8 * tr * width * 4),
    )(grad, recv)


def _adamw_math(w, g, m, v):
    m = ADAM_B1 * m + (1.0 - ADAM_B1) * g
    v = ADAM_B2 * v + (1.0 - ADAM_B2) * (g * g)
    m_hat = m / (1.0 - ADAM_B1 ** ADAM_STEP)
    v_hat = v / (1.0 - ADAM_B2 ** ADAM_STEP)
    delta = -ADAM_LR * (m_hat / (jnp.sqrt(v_hat) + ADAM_EPS) + ADAM_WD * w)
    return delta, m, v


def _adamw_shard(own, recv, w, m, v, name, tr):
    rows, width = own.shape

    def body(own_ref, r_ref, w_ref, m_ref, v_ref, g_ref, d_ref, nm_ref, nv_ref):
        g = own_ref[...]
        for j in range(3):
            g = g + r_ref[j].astype(F32)
        g_ref[...] = g
        d_ref[...], nm_ref[...], nv_ref[...] = _adamw_math(w_ref[...], g, m_ref[...], v_ref[...])

    spec = pl.BlockSpec((tr, width), lambda i: (i, 0))
    shape = jax.ShapeDtypeStruct((rows, width), F32)
    return pl.pallas_call(
        body, name=name, grid=(rows // tr,),
        in_specs=[spec, pl.BlockSpec((3, tr, width), lambda i: (0, i, 0)), spec, spec, spec],
        out_specs=[spec] * 4, out_shape=[shape] * 4,
        compiler_params=_params(("parallel",), 22 * tr * width * 4),
    )(own, recv, w, m, v)


def _small_all_reduce_adamw(gpart, lbpack, small_w, small_m, small_v, after):
    count = len(_SMALL_NAMES)
    shapes = [small_w[n].shape for n in _SMALL_NAMES]

    def row_map():
        for i, (name, (rows, cols)) in enumerate(zip(_SMALL_NAMES, shapes)):
            per = cols // LANE
            for s in range(rows):
                for j in range(per):
                    yield i, slice(s, s + 1), slice(j * LANE, (j + 1) * LANE), _SMALL_ROW0[name] + s * per + j

    def body(gp_ref, lb_ref, *refs):
        ins = [refs[k * count:(k + 1) * count] for k in range(3)]
        refs = refs[3 * count + 1:]
        outs = [refs[k * count:(k + 1) * count] for k in range(4)]
        loss_ref = refs[4 * count]
        packs = refs[4 * count + 1:4 * count + 5]
        gath_ref, send_sems, recv_sems = refs[4 * count + 5:]
        for k in range(3):
            packs[k][...] = jnp.zeros_like(packs[k])
            for i, row, lanes, r in row_map():
                packs[k][r:r + 1, :] = ins[k][i][row, lanes]
        x, y, c = _mesh_position()
        me = 4 * x + 2 * y + c
        gath_ref[me] = gp_ref[...]
        copies = []
        for j in range(1, N_DEV):
            peer = (x ^ (j >> 2), y ^ ((j >> 1) & 1), c ^ (j & 1))
            cp = pltpu.make_async_remote_copy(
                src_ref=gp_ref, dst_ref=gath_ref.at[me], send_sem=send_sems.at[j - 1], recv_sem=recv_sems.at[j - 1],
                device_id=peer, device_id_type=MESH)
            cp.start()
            copies.append(cp)
        for cp in copies:
            cp.wait()
        tot = gath_ref[0]
        for s in range(1, N_DEV):
            tot = tot + gath_ref[s]
        lb = lb_ref[...]
        dl = tot[16:32, :] * lb * (1.0 - lb)
        g = jnp.concatenate([tot[0:16, :], dl[0:8, :], -dl[0:8, :], dl[8:16, :], -dl[8:16, :],
                             tot[32:SMALL_GRAD_ROWS, :]], axis=0)
        results = (g,) + tuple(_adamw_math(packs[0][...], g, packs[1][...], packs[2][...]))
        loss_ref[...] = g[_LOSS_ROW:_LOSS_ROW + 8, :]
        for k, result in enumerate(results):
            packs[k][...] = result
            for i, row, lanes, r in row_map():
                outs[k][i][row, lanes] = packs[k][r:r + 1, :]

    vm = pl.BlockSpec(memory_space=pltpu.VMEM)
    flat = pl.pallas_call(
        body, name="small_all_reduce_adamw", grid=(1,), in_specs=[vm] * (2 + 3 * count) + [_any_spec()],
        out_specs=[pl.BlockSpec(s, lambda i: (0, 0)) for s in shapes * 4 + [(8, LANE)]],
        out_shape=[jax.ShapeDtypeStruct(s, F32) for s in shapes * 4 + [(8, LANE)]],
        scratch_shapes=[pltpu.VMEM((SMALL_ROWS, LANE), F32)] * 4
        + [pltpu.VMEM((N_DEV, SMALL_GRAD_ROWS, LANE), F32),
           pltpu.SemaphoreType.DMA((N_DEV - 1,)), pltpu.SemaphoreType.DMA((N_DEV - 1,))],
    )(gpart, lbpack, *[p[n] for p in (small_w, small_m, small_v) for n in _SMALL_NAMES], after)
    return flat[4 * count][0, 0], [dict(zip(_SMALL_NAMES, flat[k * count:(k + 1) * count])) for k in range(4)]


_SMALL_NAMES = ("norm_mix_gain", "norm_mem_gain", "lb_logits_fw", "lb_logits_bw", "norm_ffn_gain",
                "hg_norm_gain", "da_q_gain", "da_k_gain", "mem_q_gain", "mem_k_gain")
_SMALL_ROW0 = {"norm_mix_gain": 0, "norm_mem_gain": 8, "lb_logits_fw": 16, "lb_logits_bw": 32, "norm_ffn_gain": 48,
               "hg_norm_gain": 56, "da_q_gain": 64, "da_k_gain": 72, "mem_q_gain": 80, "mem_k_gain": 88}
_LOSS_ROW = 96


def _pack_rows(parts, total_rows):
    rows = []
    for p in parts:
        r = p.reshape(-1, LANE)
        rows.append(jnp.pad(r, ((0, -r.shape[0] % 8), (0, 0))))
    used = sum(r.shape[0] for r in rows)
    if total_rows > used:
        rows.append(jnp.zeros((total_rows - used, LANE), F32))
    return jnp.concatenate(rows, axis=0)


def kernel(x, mem, norm_mix_gain, norm_mem_gain, w_in, lb_logits_fw, lb_logits_bw, hg_norm_gain, da_q_gain, da_k_gain, w_mem_kv, mem_q_gain, mem_k_gain, w_proj_hg, w_proj_da, w_proj_mem, w_out, norm_ffn_gain, w_ffn_in, w_ffn_out, loss_target, m_norm_mix_gain, m_norm_mem_gain, m_w_in, m_lb_logits_fw, m_lb_logits_bw, m_hg_norm_gain, m_da_q_gain, m_da_k_gain, m_w_mem_kv, m_mem_q_gain, m_mem_k_gain, m_w_proj_hg, m_w_proj_da, m_w_proj_mem, m_w_out, m_norm_ffn_gain, m_w_ffn_in, m_w_ffn_out, v_norm_mix_gain, v_norm_mem_gain, v_w_in, v_lb_logits_fw, v_lb_logits_bw, v_hg_norm_gain, v_da_q_gain, v_da_k_gain, v_w_mem_kv, v_mem_q_gain, v_mem_k_gain, v_w_proj_hg, v_w_proj_da, v_w_proj_mem, v_w_out, v_norm_ffn_gain, v_w_ffn_in, v_w_ffn_out):
    small_w = dict(norm_mix_gain=norm_mix_gain, norm_mem_gain=norm_mem_gain, lb_logits_fw=lb_logits_fw,
                   lb_logits_bw=lb_logits_bw, norm_ffn_gain=norm_ffn_gain, hg_norm_gain=hg_norm_gain,
                   da_q_gain=da_q_gain, da_k_gain=da_k_gain, mem_q_gain=mem_q_gain, mem_k_gain=mem_k_gain)
    small_m = dict(norm_mix_gain=m_norm_mix_gain, norm_mem_gain=m_norm_mem_gain, lb_logits_fw=m_lb_logits_fw,
                   lb_logits_bw=m_lb_logits_bw, norm_ffn_gain=m_norm_ffn_gain, hg_norm_gain=m_hg_norm_gain,
                   da_q_gain=m_da_q_gain, da_k_gain=m_da_k_gain, mem_q_gain=m_mem_q_gain, mem_k_gain=m_mem_k_gain)
    small_v = dict(norm_mix_gain=v_norm_mix_gain, norm_mem_gain=v_norm_mem_gain, lb_logits_fw=v_lb_logits_fw,
                   lb_logits_bw=v_lb_logits_bw, norm_ffn_gain=v_norm_ffn_gain, hg_norm_gain=v_hg_norm_gain,
                   da_q_gain=v_da_q_gain, da_k_gain=v_da_k_gain, mem_q_gain=v_mem_q_gain, mem_k_gain=v_mem_k_gain)
    big_w = dict(w_in=w_in[0], w_mem_kv=w_mem_kv[0], w_proj_hg=w_proj_hg[0], w_proj_da=w_proj_da[0],
                 w_proj_mem=w_proj_mem[0], w_out=w_out[0], w_ffn_in=w_ffn_in[0], w_ffn_out=w_ffn_out[0])
    big_m = dict(w_in=m_w_in[0], w_mem_kv=m_w_mem_kv[0], w_proj_hg=m_w_proj_hg[0], w_proj_da=m_w_proj_da[0],
                 w_proj_mem=m_w_proj_mem[0], w_out=m_w_out[0], w_ffn_in=m_w_ffn_in[0], w_ffn_out=m_w_ffn_out[0])
    big_v = dict(w_in=v_w_in[0], w_mem_kv=v_w_mem_kv[0], w_proj_hg=v_w_proj_hg[0], w_proj_da=v_w_proj_da[0],
                 w_proj_mem=v_w_proj_mem[0], w_out=v_w_out[0], w_ffn_in=v_w_ffn_in[0], w_ffn_out=v_w_ffn_out[0])

    row_tile = dict(w_in=128, w_mem_kv=128, w_proj_hg=128, w_proj_da=512, w_proj_mem=512, w_out=128,
                    w_ffn_in=128, w_ffn_out=352)
    rest = _BIG_NAMES[1:]
    shards = [big_w[n].astype(BF16) for n in rest]
    state = {}
    big_out = {}

    def reduce_start(group, stacked):
        names = tuple(stacked)
        arrays = [stacked[n] for n in names] + [lax.empty((4,) + stacked[n].shape[1:], BF16) for n in names]
        plan = _reduce_plan_sibling(len(names))
        send, recv, thru, token = _copies_start(f"grads_{group}_sibling_start", arrays, 4 * len(names), plan)
        state[group] = dict(names=names, plan=plan, send=send, recv=recv, arrays=thru)
        return token

    def reduce_middle(group, after):
        st = state[group]
        names, k = st["names"], len(st["names"])
        thru = _copies_wait(f"grads_{group}_sibling_wait", st["send"], st["recv"], st["arrays"], 4 * k, st["plan"], after)
        partials = [_chip_partials(thru[i], thru[k + i], f"chip_partials_{n}", row_tile[n]) for i, n in enumerate(names)]
        arrays = [p[1] for p in partials] + [lax.empty(p[1].shape, BF16) for p in partials]
        plan = _reduce_plan_chips(k)
        send, recv, thru2, token = _copies_start(f"grads_{group}_chips_start", arrays, 3 * k, plan)
        state[group] = dict(names=names, plan=plan, send=send, recv=recv, arrays=thru2, own=[p[0] for p in partials])
        return token

    def reduce_finish(group, after):
        st = state.pop(group)
        names, k = st["names"], len(st["names"])
        thru = _copies_wait(f"grads_{group}_chips_wait", st["send"], st["recv"], st["arrays"], 3 * k, st["plan"], after)
        for i, n in enumerate(names):
            big_out[n] = _adamw_shard(st["own"][i], thru[k + i], big_w[n], big_m[n], big_v[n], "adamw_" + n, row_tile[n])

    step = _local_step_stages(x[0], mem[0], loss_target[0], small_w)
    event, payload = next(step)
    local = None
    while True:
        reply = None
        if event == "gather_w_in":
            reply = _all_gather([big_w["w_in"].astype(BF16)])[0]
        elif event == "begin":
            nr = len(rest)
            me = 4 * lax.axis_index("x") + 2 * lax.axis_index("y") + lax.axis_index("c")
            arrays = shards + [lax.dynamic_update_slice(lax.empty((N_DEV,) + s.shape, BF16), s[None], (me, 0, 0))
                               for s in shards]
            plan = _gather_plan_out(nr)
            send, recv, thru, reply = _copies_start("weights_rest_out_start", arrays, 4 * nr, plan)
            state["gather"] = dict(plan=plan, send=send, recv=recv, arrays=thru)
        elif event == "after_gla_fwd":
            st = state["gather"]
            nr = len(rest)
            thru = _copies_wait("weights_rest_out_wait", st["send"], st["recv"], st["arrays"], 4 * nr, st["plan"], payload)
            plan = _gather_plan_pass(nr)
            send, recv, lands, reply = _copies_start("weights_rest_pass_start", thru[nr:], 3 * nr, plan)
            state["gather"] = dict(plan=plan, send=send, recv=recv, arrays=lands)
        elif event == "need_weights":
            st = state.pop("gather")
            nr = len(rest)
            lands = _copies_wait("weights_rest_pass_wait", st["send"], st["recv"], st["arrays"], 3 * nr, st["plan"], payload)
            reply = dict(zip(rest, lands))
        elif event == "grads_ffn":
            reply = reduce_start("ffn", payload)
        elif event == "after_gate_merge_bwd":
            reply = reduce_middle("ffn", payload)
        elif event == "grads_mix":
            reply = reduce_start("mix", payload)
        elif event == "after_da_bwd_g0":
            reply = reduce_middle("mix", payload)
            reduce_finish("ffn", payload)
        elif event == "after_gla_bwd":
            reduce_finish("mix", payload)
        elif event == "grads_in":
            reply = reduce_start("in", payload)
        elif event == "after_proj_bwd_act_top":
            reply = reduce_middle("in", payload)
        elif event == "end":
            local = payload
            reduce_finish("in", [local["grad_x"]] + [big_out[n][0] for n in rest])
            break
        event, payload = step.send(reply)
    grad_x = local["grad_x"]

    loss, small_out = _small_all_reduce_adamw(local["gpart"], local["lbpack"], small_w, small_m, small_v,
                                              big_out["w_in"][0])

    order = ("norm_mix_gain", "norm_mem_gain", "w_in", "lb_logits_fw", "lb_logits_bw", "hg_norm_gain", "da_q_gain",
             "da_k_gain", "w_mem_kv", "mem_q_gain", "mem_k_gain", "w_proj_hg", "w_proj_da", "w_proj_mem", "w_out",
             "norm_ffn_gain", "w_ffn_in", "w_ffn_out")
    outs = [loss, grad_x[None]]
    for kind in range(4):
        for n in order:
            outs.append(big_out[n][kind][None] if n in big_out else small_out[kind][n])
    return tuple(outs)


_BIG_NAMES = ("w_in", "w_mem_kv", "w_proj_hg", "w_proj_da", "w_proj_mem", "w_out", "w_ffn_in", "w_ffn_out")


def _local_step(xs, mems, target, sw, wg):
    step = _local_step_stages(xs, mems, target, sw)
    stacked = {}
    event, payload = next(step)
    while event != "end":
        reply = None
        if event.startswith("grads_"):
            stacked.update(payload)
        elif event == "need_weights":
            reply = wg
        elif event == "gather_w_in":
            reply = wg["w_in"]
        event, payload = step.send(reply)
    return dict(payload, stacked=stacked)


def _local_step_stages(xs, mems, target, sw):
    norm_mix_gain, norm_mem_gain, norm_ffn_gain = sw["norm_mix_gain"], sw["norm_mem_gain"], sw["norm_ffn_gain"]
    lb_logits_fw, lb_logits_bw, hg_norm_gain = sw["lb_logits_fw"], sw["lb_logits_bw"], sw["hg_norm_gain"]
    da_q_gain, da_k_gain, mem_q_gain, mem_k_gain = sw["da_q_gain"], sw["da_k_gain"], sw["mem_q_gain"], sw["mem_k_gain"]

    win_st = yield "gather_w_in", None
    token = yield "begin", None
    lb_fw = _lb_table(lb_logits_fw, "lb_table_fw")
    lb_bw = _lb_table(lb_logits_bw, "lb_table_bw")
    lb = jnp.concatenate([lb_fw, lb_bw], axis=0).reshape(2, HG_HEADS, 1, HEAD_DIM)
    h, h_t = _rmsnorm_fwd(xs, _after(norm_mix_gain, token), "norm_mix_fwd", 512, transposed=True)
    proj = _matmul(h, win_st, "nn", F32, 2048, IN_SHARD, D_MODEL, "proj_fwd", b_stacked=True, n_outer=True)
    hg_b = _gla_decay(proj, lb)
    o_hg, o_pre, states, hg_qt, hg_decay = _gla_fwd(proj, lb, hg_norm_gain, hg_b)
    token = yield "after_gla_fwd", o_hg
    da01 = [_da_fwd(proj, _after(da_q_gain, token), da_k_gain, g) for g in (0, 1)]
    o_da, o_da32, lse_da = _da_fwd(proj, _after(da_q_gain, token), da_k_gain, 2, others=da01)
    wg = yield "need_weights", o_da
    wkv = wg["w_mem_kv"].reshape(D_MODEL, 2 * MEM_WIDTH)
    wphg = wg["w_proj_hg"].reshape(D_MODEL, D_MODEL)
    wpda = jnp.transpose(wg["w_proj_da"], (1, 0, 2)).reshape(DA_WIDTH, D_MODEL)
    wpmem = jnp.transpose(wg["w_proj_mem"], (1, 0, 2)).reshape(MEM_WIDTH, D_MODEL)
    wout = wg["w_out"].reshape(D_MODEL, D_MODEL)
    wfin = jnp.transpose(wg["w_ffn_in"], (1, 0, 2)).reshape(D_MODEL, 2 * D_FF)
    wfout = wg["w_ffn_out"].reshape(D_FF, D_MODEL)
    mem_n = _rmsnorm_fwd(mems, norm_mem_gain, "norm_mem_fwd", N_MEM)
    kv = _matmul(mem_n, wkv, "nn", F32, N_MEM, 1024, D_MODEL, "mem_kv_fwd")
    o_mem = _mem_fwd(proj, kv, mem_q_gain, mem_k_gain, 1024)
    branch_w = (wphg, wpda, wpmem)
    merged, t_hg, t_da, t_mem = _branch_merge_fwd(proj, (o_hg, o_da, o_mem), branch_w, 512)
    x1, h2, h2_t = _residual_rmsnorm_fwd(xs, merged, wout, norm_ffn_gain, "out_norm_ffn_fwd", 512)
    ffn_a, ffn_b, act, act_t = _ffn_in_swiglu(h2, wfin, 1024, 1408)
    dy, dyb, loss_part = _ffn_out_loss_head(x1, act, wfout, target, 512)

    dab = _ffn_out_bwd_swiglu(dyb, wfout, ffn_a, ffn_b, 1024, 1408)
    g_wfout = _matmul(act_t, dyb, "nn", BF16, 1408, 1024, 2048, "ffn_out_bwd_w")
    g_wfin = _matmul(h2_t, dab, "nn", BF16, 1024, 1408, 2048, "ffn_in_bwd_w", b_parts=True)
    token = yield "grads_ffn", dict(
        w_ffn_in=jnp.transpose(g_wfin.reshape(D_MODEL, N_DEV, 2 * D_FF // N_DEV), (1, 0, 2)),
        w_ffn_out=g_wfout.reshape(N_DEV, D_FF // N_DEV, D_MODEL))
    dx1, dx1b, g_norm_ffn = _matmul_rmsnorm_bwd(dab, wfin, x1, dy, _after(norm_ffn_gain, token),
                                                "ffn_in_bwd_act_norm", 512, D_FF, a_parts=True)
    g_wout = _matmul(merged, dx1b, "tn", BF16, 1024, 1024, 2048, "out_bwd_w")
    dt_hg, dt_da, dt_mem, do_pre, do_da, do_mem, dd_da, g_hg_norm, dproj = _gate_merge_bwd(
        proj, (t_hg, t_da, t_mem), branch_w, dx1b, wout, o_pre, hg_norm_gain, o_da32, _dproj_buffer(), 256)
    token = yield "after_gate_merge_bwd", dt_hg
    g_wphg = _matmul(o_hg, dt_hg, "tn", BF16, 1024, 1024, 2048, "proj_hg_bwd_w", after=token)
    g_wpda = _matmul(o_da, dt_da, "tn", BF16, DA_WIDTH, D_MODEL, 2048, "proj_da_bwd_w")
    g_wpmem = _matmul(o_mem, dt_mem, "tn", BF16, MEM_WIDTH, D_MODEL, 2048, "proj_mem_bwd_w")
    by_owner = lambda g: jnp.transpose(g.reshape(g.shape[0], N_DEV, D_MODEL // N_DEV), (1, 0, 2))
    g_wpda, g_wpmem = by_owner(g_wpda), by_owner(g_wpmem)

    dproj, dk_mem, dv_mem, g_mem_q, g_mem_k = _mem_bwd(proj, kv, mem_q_gain, mem_k_gain, do_mem, dproj, 1024)
    dkv = jnp.concatenate([dk_mem, dv_mem], axis=1).astype(BF16)
    g_wkv = _matmul(mem_n, dkv, "tn", BF16, 1024, 1024, N_MEM, "mem_kv_bwd_w")
    dmem_n = _matmul(dkv, wkv, "nt", F32, N_MEM, 1024, 1024, "mem_kv_bwd_act")
    g_norm_mem = _gain_grad(mems, dmem_n, "norm_mem_bwd")
    token = yield "grads_mix", dict(
        w_mem_kv=g_wkv.reshape(N_DEV, D_MODEL // N_DEV, 2 * MEM_WIDTH),
        w_proj_hg=g_wphg.reshape(N_DEV, D_MODEL // N_DEV, D_MODEL),
        w_proj_da=g_wpda, w_proj_mem=g_wpmem,
        w_out=g_wout.reshape(N_DEV, D_MODEL // N_DEV, D_MODEL))

    dproj, g_da_q, g_da_k = _da_bwd(proj, _after(da_q_gain, token), da_k_gain, do_da, lse_da, dd_da, dproj, 0)
    token = yield "after_da_bwd_g0", g_da_q
    for g in (1, 2):
        dproj, gq_part, gk_part = _da_bwd(proj, _after(da_q_gain, token), da_k_gain, do_da, lse_da, dd_da, dproj, g)
        g_da_q, g_da_k = g_da_q + gq_part, g_da_k + gk_part

    dproj, dlb = _gla_bwd(proj, lb, states, hg_qt, hg_decay, hg_b, do_pre, dproj)
    yield "after_gla_bwd", dlb

    g_win =_matmul(h_t, dproj, "nn", BF16, 1024, IN_SHARD, 2048, "proj_bwd_w", out_stacked=True)
    token = yield "grads_in", dict(w_in=g_win)
    grad_x, g_mix_top = _matmul_rmsnorm_bwd(dproj, win_st, xs, dx1, norm_mix_gain, "proj_bwd_act_norm_top", 1024,
                                            IN_SHARD, b_stacked=True, after=token, m_blocks=(0, 1), with_bf16=False)
    token = yield "after_proj_bwd_act_top", g_mix_top
    grad_x, g_mix_bottom = _matmul_rmsnorm_bwd(dproj, win_st, xs, dx1, norm_mix_gain, "proj_bwd_act_norm_bottom", 1024,
                                               IN_SHARD, b_stacked=True, after=token, m_blocks=(1, 3),
                                               out_into=grad_x, with_bf16=False)
    g_norm_mix = g_mix_top + g_mix_bottom

    gpart = _pack_rows([g_norm_mix, g_norm_mem, dlb[0], dlb[1], g_norm_ffn, g_hg_norm, g_da_q, g_da_k,
                        g_mem_q, g_mem_k, loss_part], SMALL_GRAD_ROWS)
    lbpack = jnp.concatenate([lb_fw.reshape(8, LANE), lb_bw.reshape(8, LANE)], axis=0)
    yield "end", dict(grad_x=grad_x, gpart=gpart, lbpack=lbpack)
```
